```python
import jax, jax.numpy as jnp
from jax import lax
import numpy as np

D_MODEL = 1024
BATCH = 8
SEQ = 4096
DEPTH = 1

HEAD_DIM = 64
N_HEADS = 8
N_KV_HEADS = 2
GROUP = N_HEADS // N_KV_HEADS
WINDOW = 128
BLOCK = 128
ATTN_SCALE = HEAD_DIM ** -0.5
ATTN_WIDTH = N_HEADS * HEAD_DIM
KV_WIDTH = N_KV_HEADS * HEAD_DIM
CONV_GROUPS = 8
CONV_WIDTH = CONV_GROUPS * 64
CONV_K = 3
IN_WIDTH = ATTN_WIDTH + 2 * KV_WIDTH + 3 * CONV_WIDTH + 2 * D_MODEL
D_FF = 2816
FFN_CONV_K = 3
NORM_EPS = 1e-5

kernel_name = "hybrid_swa_sink_shortconv_gated_convffn"


def rms_norm(x, g):
    xf = x.astype(jnp.float32)
    y = xf * lax.rsqrt(jnp.mean(xf * xf, axis=-1, keepdims=True) + NORM_EPS)
    return (y * g.astype(jnp.float32)).astype(x.dtype)


def causal_depthwise_conv(x, w):
    k_width, ch = w.shape
    return lax.conv_general_dilated(
        x, w[:, None, :].astype(x.dtype), window_strides=(1,), padding=((k_width - 1, 0),),
        dimension_numbers=("NWC", "WIO", "NWC"), feature_group_count=ch)


def sliding_window_attention(q, k, v, sinks):
    b, s, _ = q.shape
    nb = s // BLOCK
    q = q.reshape(b, nb, BLOCK, N_KV_HEADS, GROUP, HEAD_DIM)
    k = k.reshape(b, nb, BLOCK, N_KV_HEADS, HEAD_DIM)
    v = v.reshape(b, nb, BLOCK, N_KV_HEADS, HEAD_DIM)

    def with_prev(t):
        prev = jnp.pad(t, ((0, 0), (1, 0), (0, 0), (0, 0), (0, 0)))[:, :-1]
        return jnp.concatenate([prev, t], axis=2)

    kw, vw = with_prev(k), with_prev(v)
    scores = jnp.einsum("bnqhgd,bnkhd->bnhgqk", q, kw).astype(jnp.float32) * ATTN_SCALE
    qi = jnp.arange(BLOCK)[:, None]
    kj = jnp.arange(2 * BLOCK)[None, :]
    dist = qi + BLOCK - kj
    band = (dist >= 0) & (dist < WINDOW)
    real = (jnp.arange(nb)[:, None, None] > 0) | (kj[None] >= BLOCK)
    mask = band[None] & real
    scores = jnp.where(mask[None, :, None, None], scores, -jnp.inf)
    sink = jnp.broadcast_to(sinks.astype(jnp.float32).reshape(1, 1, N_KV_HEADS, GROUP, 1, 1),
                            scores.shape[:-1] + (1,))
    probs = jax.nn.softmax(jnp.concatenate([scores, sink], axis=-1), axis=-1)[..., :-1]
    out = jnp.einsum("bnhgqk,bnkhd->bnqhgd", probs.astype(v.dtype), vw)
    return out.reshape(b, s, ATTN_WIDTH)


def _fwd_setup_inputs(seed: int = 0) -> dict:
    key = jax.random.key(seed)
    ks = jax.random.split(key, 16)
    f32 = jnp.float32

    def nrm(k, shape, scale):
        return jax.random.normal(k, shape, f32) * scale

    return {
        "x": nrm(ks[0], (BATCH, SEQ, D_MODEL), 1.0),
        "mix_norm": 1.0 + nrm(ks[1], (DEPTH, D_MODEL), 0.02),
        "w_in": nrm(ks[2], (DEPTH, D_MODEL, IN_WIDTH), D_MODEL ** -0.5),
        "b_in": nrm(ks[3], (DEPTH, IN_WIDTH), 0.02),
        "sinks": nrm(ks[4], (DEPTH, N_HEADS), 0.5),
        "conv_w": nrm(ks[5], (DEPTH, CONV_K, CONV_WIDTH), CONV_K ** -0.5),
        "w_attn_branch": nrm(ks[6], (DEPTH, ATTN_WIDTH, D_MODEL), ATTN_WIDTH ** -0.5),
        "w_conv_branch": nrm(ks[7], (DEPTH, CONV_WIDTH, D_MODEL), CONV_WIDTH ** -0.5),
        "w_out": nrm(ks[8], (DEPTH, D_MODEL, D_MODEL), D_MODEL ** -0.5),
        "ffn_norm": 1.0 + nrm(ks[9], (DEPTH, D_MODEL), 0.02),
        "w_up": nrm(ks[10], (DEPTH, D_MODEL, 2 * D_FF), D_MODEL ** -0.5),
        "ffn_conv_w": nrm(ks[11], (DEPTH, FFN_CONV_K, 2 * D_FF), FFN_CONV_K ** -0.5),
        "w_down": nrm(ks[12], (DEPTH, D_FF, D_MODEL), D_FF ** -0.5),
        "final_norm": 1.0 + nrm(ks[13], (D_MODEL,), 0.02),
    }


def _fwd_reference(x, mix_norm, w_in, b_in, sinks, conv_w, w_attn_branch, w_conv_branch, w_out,
              ffn_norm, w_up, ffn_conv_w, w_down, final_norm):
    h = x
    splits = np.cumsum([ATTN_WIDTH, KV_WIDTH, KV_WIDTH, CONV_WIDTH, CONV_WIDTH, CONV_WIDTH, D_MODEL])
    for l in range(DEPTH):
        xn = rms_norm(h, mix_norm[l])
        proj = jnp.einsum("bsd,dp->bsp", xn, w_in[l]) + b_in[l]
        q, k, v, cb, cc, cx, ga, gc = jnp.split(proj, splits, axis=-1)
        attn = sliding_window_attention(q, k, v, sinks[l])
        conv = cb * causal_depthwise_conv(cc * cx, conv_w[l])
        merged = (jax.nn.sigmoid(ga) * jnp.einsum("bsc,cd->bsd", attn, w_attn_branch[l])
                  + jax.nn.sigmoid(gc) * jnp.einsum("bsc,cd->bsd", conv, w_conv_branch[l]))
        h = h + jnp.einsum("bsd,de->bse", merged, w_out[l])
        hn = rms_norm(h, ffn_norm[l])
        up = causal_depthwise_conv(jnp.einsum("bsd,df->bsf", hn, w_up[l]), ffn_conv_w[l])
        gate, val = jnp.split(up, 2, axis=-1)
        h = h + jnp.einsum("bsf,fd->bsd", jax.nn.silu(gate) * val, w_down[l])
    return rms_norm(h, final_norm)


import jax as _jax
import jax.numpy as _jnp

TWIN_FORMAT = 'train_step'
FWD_PARAMS = ['x', 'mix_norm', 'w_in', 'b_in', 'sinks', 'conv_w', 'w_attn_branch', 'w_conv_branch', 'w_out', 'ffn_norm', 'w_up', 'ffn_conv_w', 'w_down', 'final_norm']
TWIN_WEIGHTS = ['mix_norm', 'w_in', 'b_in', 'sinks', 'conv_w', 'w_attn_branch', 'w_conv_branch', 'w_out', 'ffn_norm', 'w_up', 'ffn_conv_w', 'w_down', 'final_norm']
TWIN_DIFF_INPUT = 'x'
TWIN_INPUTS = ['x', 'mix_norm', 'w_in', 'b_in', 'sinks', 'conv_w', 'w_attn_branch', 'w_conv_branch', 'w_out', 'ffn_norm', 'w_up', 'ffn_conv_w', 'w_down', 'final_norm', 'loss_target', 'm_mix_norm', 'm_w_in', 'm_b_in', 'm_sinks', 'm_conv_w', 'm_w_attn_branch', 'm_w_conv_branch', 'm_w_out', 'm_ffn_norm', 'm_w_up', 'm_ffn_conv_w', 'm_w_down', 'm_final_norm', 'v_mix_norm', 'v_w_in', 'v_b_in', 'v_sinks', 'v_conv_w', 'v_w_attn_branch', 'v_w_conv_branch', 'v_w_out', 'v_ffn_norm', 'v_w_up', 'v_ffn_conv_w', 'v_w_down', 'v_final_norm']
TWIN_OUTPUTS = ['loss', 'grad_x', 'grad_mix_norm', 'grad_w_in', 'grad_b_in', 'grad_sinks', 'grad_conv_w', 'grad_w_attn_branch', 'grad_w_conv_branch', 'grad_w_out', 'grad_ffn_norm', 'grad_w_up', 'grad_ffn_conv_w', 'grad_w_down', 'grad_final_norm', 'delta_mix_norm', 'delta_w_in', 'delta_b_in', 'delta_sinks', 'delta_conv_w', 'delta_w_attn_branch', 'delta_w_conv_branch', 'delta_w_out', 'delta_ffn_norm', 'delta_w_up', 'delta_ffn_conv_w', 'delta_w_down', 'delta_final_norm', 'new_m_mix_norm', 'new_m_w_in', 'new_m_b_in', 'new_m_sinks', 'new_m_conv_w', 'new_m_w_attn_branch', 'new_m_w_conv_branch', 'new_m_w_out', 'new_m_ffn_norm', 'new_m_w_up', 'new_m_ffn_conv_w', 'new_m_w_down', 'new_m_final_norm', 'new_v_mix_norm', 'new_v_w_in', 'new_v_b_in', 'new_v_sinks', 'new_v_conv_w', 'new_v_w_attn_branch', 'new_v_w_conv_branch', 'new_v_w_out', 'new_v_ffn_norm', 'new_v_w_up', 'new_v_ffn_conv_w', 'new_v_w_down', 'new_v_final_norm']
TWIN_LEAF_KINDS = {'loss': 'loss', 'grad_x': 'grad_x', 'grad_mix_norm': 'grad_w', 'grad_w_in': 'grad_w', 'grad_b_in': 'grad_w', 'grad_sinks': 'grad_w', 'grad_conv_w': 'grad_w', 'grad_w_attn_branch': 'grad_w', 'grad_w_conv_branch': 'grad_w', 'grad_w_out': 'grad_w', 'grad_ffn_norm': 'grad_w', 'grad_w_up': 'grad_w', 'grad_ffn_conv_w': 'grad_w', 'grad_w_down': 'grad_w', 'grad_final_norm': 'grad_w', 'delta_mix_norm': 'delta_w', 'delta_w_in': 'delta_w', 'delta_b_in': 'delta_w', 'delta_sinks': 'delta_w', 'delta_conv_w': 'delta_w', 'delta_w_attn_branch': 'delta_w', 'delta_w_conv_branch': 'delta_w', 'delta_w_out': 'delta_w', 'delta_ffn_norm': 'delta_w', 'delta_w_up': 'delta_w', 'delta_ffn_conv_w': 'delta_w', 'delta_w_down': 'delta_w', 'delta_final_norm': 'delta_w', 'new_m_mix_norm': 'new_m', 'new_m_w_in': 'new_m', 'new_m_b_in': 'new_m', 'new_m_sinks': 'new_m', 'new_m_conv_w': 'new_m', 'new_m_w_attn_branch': 'new_m', 'new_m_w_conv_branch': 'new_m', 'new_m_w_out': 'new_m', 'new_m_ffn_norm': 'new_m', 'new_m_w_up': 'new_m', 'new_m_ffn_conv_w': 'new_m', 'new_m_w_down': 'new_m', 'new_m_final_norm': 'new_m', 'new_v_mix_norm': 'new_v', 'new_v_w_in': 'new_v', 'new_v_b_in': 'new_v', 'new_v_sinks': 'new_v', 'new_v_conv_w': 'new_v', 'new_v_w_attn_branch': 'new_v', 'new_v_w_conv_branch': 'new_v', 'new_v_w_out': 'new_v', 'new_v_ffn_norm': 'new_v', 'new_v_w_up': 'new_v', 'new_v_ffn_conv_w': 'new_v', 'new_v_w_down': 'new_v', 'new_v_final_norm': 'new_v'}


def _forward(args):
    return _fwd_reference(*[args[k] for k in FWD_PARAMS])


def _output_shape():
    out = _jax.eval_shape(lambda: _forward(_fwd_setup_inputs(0)))
    return out.shape, out.dtype

N_MICROBATCH = 1
ADAM_LR = 0.001
ADAM_B1 = 0.9
ADAM_B2 = 0.999
ADAM_EPS = 1e-08
ADAM_WD = 0.01
ADAM_STEP = 10
PER_EXAMPLE_BATCH_AXIS = {'x': 0, 'loss_target': 0}
SHARED_INPUTS = []
_WEIGHT_DTYPES = {'mix_norm': _jnp.float32, 'w_in': _jnp.float32, 'b_in': _jnp.float32, 'sinks': _jnp.float32, 'conv_w': _jnp.float32, 'w_attn_branch': _jnp.float32, 'w_conv_branch': _jnp.float32, 'w_out': _jnp.float32, 'ffn_norm': _jnp.float32, 'w_up': _jnp.float32, 'ffn_conv_w': _jnp.float32, 'w_down': _jnp.float32, 'final_norm': _jnp.float32}
MOMENT_SCALE = {'mix_norm': 1.686218e-01, 'w_in': 8.266083e-02, 'b_in': 9.553989e-02, 'sinks': 2.191988e-02, 'conv_w': 1.473576e-01, 'w_attn_branch': 1.930702e-02, 'w_conv_branch': 9.473703e-02, 'w_out': 9.656401e-02, 'ffn_norm': 1.231490e-01, 'w_up': 5.226678e-02, 'ffn_conv_w': 5.168728e-02, 'w_down': 8.542675e-02, 'final_norm': 3.198621e+01}


def _to_microbatches(a, axis):
    t = _jnp.moveaxis(a, axis, 0)
    t = t.reshape((N_MICROBATCH, t.shape[0] // N_MICROBATCH) + t.shape[1:])
    return _jnp.moveaxis(t, 1, axis + 1)


def setup_inputs(seed: int = 0) -> dict:
    inp = _fwd_setup_inputs(seed)
    key = _jax.random.fold_in(_jax.random.key(seed), 7919)
    shape, _ = _output_shape()
    out = dict(inp)
    out["loss_target"] = _jax.random.normal(_jax.random.fold_in(key, 0), shape, _jnp.float32)
    for i, name in enumerate(TWIN_WEIGHTS):
        w = inp[name].astype(_jnp.float32)
        if MOMENT_SCALE is None:
            s = _jnp.sqrt(_jnp.mean(_jnp.square(w)) + 1e-30)
        else:
            s = MOMENT_SCALE[name]
        km, kv = _jax.random.split(_jax.random.fold_in(key, i + 1))
        out[name] = w
        out["m_" + name] = s * _jax.random.normal(km, w.shape, _jnp.float32)
        out["v_" + name] = (s * s) * _jax.random.uniform(kv, w.shape, _jnp.float32, 0.5, 1.5)
    if N_MICROBATCH > 1:
        for name, axis in PER_EXAMPLE_BATCH_AXIS.items():
            out[name] = _to_microbatches(out[name], axis)
    return {'x': out['x'], 'mix_norm': out['mix_norm'], 'w_in': out['w_in'], 'b_in': out['b_in'], 'sinks': out['sinks'], 'conv_w': out['conv_w'], 'w_attn_branch': out['w_attn_branch'], 'w_conv_branch': out['w_conv_branch'], 'w_out': out['w_out'], 'ffn_norm': out['ffn_norm'], 'w_up': out['w_up'], 'ffn_conv_w': out['ffn_conv_w'], 'w_down': out['w_down'], 'final_norm': out['final_norm'], 'loss_target': out['loss_target'], 'm_mix_norm': out['m_mix_norm'], 'm_w_in': out['m_w_in'], 'm_b_in': out['m_b_in'], 'm_sinks': out['m_sinks'], 'm_conv_w': out['m_conv_w'], 'm_w_attn_branch': out['m_w_attn_branch'], 'm_w_conv_branch': out['m_w_conv_branch'], 'm_w_out': out['m_w_out'], 'm_ffn_norm': out['m_ffn_norm'], 'm_w_up': out['m_w_up'], 'm_ffn_conv_w': out['m_ffn_conv_w'], 'm_w_down': out['m_w_down'], 'm_final_norm': out['m_final_norm'], 'v_mix_norm': out['v_mix_norm'], 'v_w_in': out['v_w_in'], 'v_b_in': out['v_b_in'], 'v_sinks': out['v_sinks'], 'v_conv_w': out['v_conv_w'], 'v_w_attn_branch': out['v_w_attn_branch'], 'v_w_conv_branch': out['v_w_conv_branch'], 'v_w_out': out['v_w_out'], 'v_ffn_norm': out['v_ffn_norm'], 'v_w_up': out['v_w_up'], 'v_ffn_conv_w': out['v_ffn_conv_w'], 'v_w_down': out['v_w_down'], 'v_final_norm': out['v_final_norm']}


def _loss(weights, diff, rest, loss_target):
    with _jax.named_scope("forward"):
        args = {**rest, TWIN_DIFF_INPUT: diff, **{k: w.astype(_WEIGHT_DTYPES[k]) for k, w in weights.items()}}
        y = _forward(args)
    with _jax.named_scope("loss_head"):
        err = _jnp.square(y.astype(_jnp.float32) - loss_target)
        return 0.5 * _jnp.sum(_jnp.mean(err, axis=-1)) if err.ndim else 0.5 * err


def _adamw(w, g, m, v):
    m = ADAM_B1 * m + (1.0 - ADAM_B1) * g
    v = ADAM_B2 * v + (1.0 - ADAM_B2) * _jnp.square(g)
    m_hat = m / (1.0 - ADAM_B1 ** ADAM_STEP)
    v_hat = v / (1.0 - ADAM_B2 ** ADAM_STEP)
    delta = -ADAM_LR * (m_hat / (_jnp.sqrt(v_hat) + ADAM_EPS) + ADAM_WD * w)
    return delta, m, v


def reference(x, mix_norm, w_in, b_in, sinks, conv_w, w_attn_branch, w_conv_branch, w_out, ffn_norm, w_up, ffn_conv_w, w_down, final_norm, loss_target, m_mix_norm, m_w_in, m_b_in, m_sinks, m_conv_w, m_w_attn_branch, m_w_conv_branch, m_w_out, m_ffn_norm, m_w_up, m_ffn_conv_w, m_w_down, m_final_norm, v_mix_norm, v_w_in, v_b_in, v_sinks, v_conv_w, v_w_attn_branch, v_w_conv_branch, v_w_out, v_ffn_norm, v_w_up, v_ffn_conv_w, v_w_down, v_final_norm):
    given = dict(x=x, mix_norm=mix_norm, w_in=w_in, b_in=b_in, sinks=sinks, conv_w=conv_w, w_attn_branch=w_attn_branch, w_conv_branch=w_conv_branch, w_out=w_out, ffn_norm=ffn_norm, w_up=w_up, ffn_conv_w=ffn_conv_w, w_down=w_down, final_norm=final_norm, loss_target=loss_target, m_mix_norm=m_mix_norm, m_w_in=m_w_in, m_b_in=m_b_in, m_sinks=m_sinks, m_conv_w=m_conv_w, m_w_attn_branch=m_w_attn_branch, m_w_conv_branch=m_w_conv_branch, m_w_out=m_w_out, m_ffn_norm=m_ffn_norm, m_w_up=m_w_up, m_ffn_conv_w=m_ffn_conv_w, m_w_down=m_w_down, m_final_norm=m_final_norm, v_mix_norm=v_mix_norm, v_w_in=v_w_in, v_b_in=v_b_in, v_sinks=v_sinks, v_conv_w=v_conv_w, v_w_attn_branch=v_w_attn_branch, v_w_conv_branch=v_w_conv_branch, v_w_out=v_w_out, v_ffn_norm=v_ffn_norm, v_w_up=v_w_up, v_ffn_conv_w=v_ffn_conv_w, v_w_down=v_w_down, v_final_norm=v_final_norm)
    weights = {n: given[n] for n in TWIN_WEIGHTS}
    shared = {n: given[n] for n in SHARED_INPUTS}
    per_example = {n: given[n] for n in ['x']}
    grad_fn = _jax.value_and_grad(_loss, argnums=(0, 1))

    def one_microbatch(ex, loss_target):
        ex = dict(ex)
        diff = ex.pop(TWIN_DIFF_INPUT)
        return grad_fn(weights, diff, {**shared, **ex}, loss_target)

    if N_MICROBATCH == 1:
        loss, (grad_w, grad_x) = one_microbatch(per_example, given["loss_target"])
    else:
        def body(carry, xs):
            loss_sum, grad_sum = carry
            l_k, (gw_k, gx_k) = one_microbatch(xs[0], xs[1])
            with _jax.named_scope("update"):
                return (loss_sum + l_k, _jax.tree.map(_jnp.add, grad_sum, gw_k)), gx_k

        init = (_jnp.zeros((), _jnp.float32), _jax.tree.map(_jnp.zeros_like, weights))
        (loss, grad_w), grad_x = _jax.lax.scan(body, init, (per_example, given["loss_target"]))
    with _jax.named_scope("update"):
        delta_w, new_m, new_v = {}, {}, {}
        for n in TWIN_WEIGHTS:
            delta_w[n], new_m[n], new_v[n] = _adamw(weights[n], grad_w[n], given["m_" + n], given["v_" + n])
    return (loss, grad_x, *[grad_w[n] for n in TWIN_WEIGHTS], *[delta_w[n] for n in TWIN_WEIGHTS],
            *[new_m[n] for n in TWIN_WEIGHTS], *[new_v[n] for n in TWIN_WEIGHTS])
```

```python
import functools

import jax
import jax.numpy as jnp
from jax import lax
from jax.experimental import pallas as pl
from jax.experimental.pallas import tpu as pltpu

F32 = jnp.float32
BF16 = jnp.bfloat16

D_MODEL = 1024
HEAD_DIM = 64
N_HEADS = 8
N_KV_HEADS = 2
GROUP = N_HEADS // N_KV_HEADS
BLOCK = 128
ATTN_SCALE = HEAD_DIM ** -0.5
ATTN_W = N_HEADS * HEAD_DIM
KV_W = N_KV_HEADS * HEAD_DIM
CONV_W = 512
QKV_W = ATTN_W + 2 * KV_W
C3_W = 3 * CONV_W
GATES_W = 2 * D_MODEL
IN_W = QKV_W + C3_W + GATES_W
D_FF = 2816
FF2 = 2 * D_FF
NORM_EPS = 1e-5
N_CHIPS = 4
IN_SHARD = IN_W // N_CHIPS
NEG = -1e30

ADAM_LR = 0.001
ADAM_B1 = 0.9
ADAM_B2 = 0.999
ADAM_EPS = 1e-08
ADAM_WD = 0.01
ADAM_STEP = 10

VMEM_LIMIT = 56 * 1024 * 1024
MESH = pl.DeviceIdType.MESH

NT = (((1,), (1,)), ((), ()))
TN = (((0,), (0,)), ((), ()))


def _params(*sem):
    return pltpu.CompilerParams(dimension_semantics=sem, vmem_limit_bytes=VMEM_LIMIT)


def _resident(shape):
    return pl.BlockSpec(shape, lambda *_: (0,) * len(shape), pipeline_mode=pl.Buffered(1))


def _sigmoid(v):
    return 1.0 / (1.0 + jnp.exp(-v))


def _rstd(v):
    return lax.rsqrt(jnp.mean(v * v, axis=-1, keepdims=True) + NORM_EPS)


def _rms_bwd(dy, v, rstd, g):
    vhat = v * rstd
    t = dy * g
    return rstd * (t - vhat * jnp.mean(t * vhat, axis=-1, keepdims=True)), dy * vhat


def _shift_down(z, k, prev):
    r = pltpu.roll(z, k, 0)
    rows = lax.broadcasted_iota(jnp.int32, z.shape, 0)
    for j in range(k):
        r = jnp.where(rows == j, prev[j:j + 1, :], r)
    return r


def _shift_up(z, k, nxt):
    n = z.shape[0]
    r = pltpu.roll(z, n - k, 0)
    rows = lax.broadcasted_iota(jnp.int32, z.shape, 0)
    for j in range(k):
        r = jnp.where(rows == n - k + j, nxt[j:j + 1, :], r)
    return r


def _inproj_fwd(x, g1, w_in, b_in, tm):
    s = x.shape[0]

    def body(x_ref, g_ref, w_ref, b_ref, xn_ref, qkv_ref, c3_ref, gt_ref):
        xf = x_ref[...]
        xn = (xf * _rstd(xf) * g_ref[...]).astype(BF16)
        xn_ref[...] = xn

        def seg(a, b):
            return jnp.dot(xn, w_ref[:, a:b], preferred_element_type=F32) + b_ref[:, a:b]

        qkv_ref[...] = seg(0, QKV_W).astype(BF16)
        c3_ref[...] = seg(QKV_W, QKV_W + C3_W)
        gt_ref[...] = seg(QKV_W + C3_W, IN_W)

    row = lambda w: pl.BlockSpec((tm, w), lambda i: (i, 0))
    return pl.pallas_call(
        body, name="inproj_fwd", grid=(s // tm,),
        in_specs=[row(D_MODEL), _resident((1, D_MODEL)), _resident((D_MODEL, IN_W)), _resident((1, IN_W))],
        out_specs=[row(D_MODEL), row(QKV_W), row(C3_W), row(GATES_W)],
        out_shape=[jax.ShapeDtypeStruct((s, D_MODEL), BF16), jax.ShapeDtypeStruct((s, QKV_W), BF16),
                   jax.ShapeDtypeStruct((s, C3_W), F32), jax.ShapeDtypeStruct((s, GATES_W), F32)],
        compiler_params=_params("parallel"),
    )(x, g1, w_in, b_in)


def _attn_mask(first_block):
    qi = lax.broadcasted_iota(jnp.int32, (GROUP * BLOCK, 2 * BLOCK), 0) & (BLOCK - 1)
    kj = lax.broadcasted_iota(jnp.int32, (GROUP * BLOCK, 2 * BLOCK), 1)
    band = (kj > qi) & (kj <= qi + BLOCK)
    return band & ((kj >= BLOCK) | jnp.logical_not(first_block))


def _sink_column(sk_ref, h):
    rows = lax.broadcasted_iota(jnp.int32, (GROUP * BLOCK, 1), 0)
    col = jnp.full((GROUP * BLOCK, 1), sk_ref[h * GROUP], F32)
    for g in range(1, GROUP):
        col = jnp.where(rows >= g * BLOCK, sk_ref[h * GROUP + g], col)
    return col


def _stack_heads(t, h):
    return jnp.concatenate(
        [t[:, (h * GROUP + g) * HEAD_DIM:(h * GROUP + g + 1) * HEAD_DIM] for g in range(GROUP)], axis=0)


def _unstack_heads(per_kv):
    return jnp.concatenate(
        [t[g * BLOCK:(g + 1) * BLOCK] for t in per_kv for g in range(GROUP)], axis=1)


def _attn_specs(nb):
    cur = lambda i: jnp.minimum(i, nb - 1)
    prev = lambda i: jnp.maximum(jnp.minimum(i, nb - 1) - 1, 0)
    q = pl.BlockSpec((BLOCK, ATTN_W), lambda i: (cur(i), 0))
    kp = pl.BlockSpec((BLOCK, KV_W), lambda i: (prev(i), ATTN_W // KV_W))
    kc = pl.BlockSpec((BLOCK, KV_W), lambda i: (cur(i), ATTN_W // KV_W))
    vp = pl.BlockSpec((BLOCK, KV_W), lambda i: (prev(i), ATTN_W // KV_W + 1))
    vc = pl.BlockSpec((BLOCK, KV_W), lambda i: (cur(i), ATTN_W // KV_W + 1))
    return q, kp, kc, vp, vc


def _attn_fwd(qkv, sinks):
    s = qkv.shape[0]
    nb = s // BLOCK

    def body(sk_ref, q_ref, kp_ref, kc_ref, vp_ref, vc_ref, o_ref):
        mask = _attn_mask(pl.program_id(0) == 0)
        q, kp, kc, vp, vc = q_ref[...], kp_ref[...], kc_ref[...], vp_ref[...], vc_ref[...]
        outs = []
        for h in range(N_KV_HEADS):
            hs = slice(h * HEAD_DIM, (h + 1) * HEAD_DIM)
            k2 = jnp.concatenate([kp[:, hs], kc[:, hs]], axis=0)
            v2 = jnp.concatenate([vp[:, hs], vc[:, hs]], axis=0)
            sc = lax.dot_general(_stack_heads(q, h), k2, NT, preferred_element_type=F32) * ATTN_SCALE
            sc = jnp.where(mask, sc, NEG)
            sink = _sink_column(sk_ref, h)
            m = jnp.maximum(jnp.max(sc, axis=1, keepdims=True), sink)
            p = jnp.exp(sc - m)
            den = jnp.sum(p, axis=1, keepdims=True) + jnp.exp(sink - m)
            outs.append(jnp.dot(p.astype(BF16), v2, preferred_element_type=F32) / den)
        o_ref[...] = _unstack_heads(outs).astype(BF16)

    return pl.pallas_call(
        body, name="attn_fwd", grid=(nb,),
        in_specs=[pl.BlockSpec(memory_space=pltpu.SMEM), *_attn_specs(nb)],
        out_specs=pl.BlockSpec((BLOCK, ATTN_W), lambda i: (i, 0)),
        out_shape=jax.ShapeDtypeStruct((s, ATTN_W), BF16),
        compiler_params=_params("parallel"),
    )(sinks, qkv, qkv, qkv, qkv, qkv)


def _mix_fwd(x, attn, c3, gates, conv_w, w_ab, w_cb, w_out, g2, tm):
    s = x.shape[0]

    def body(x_ref, at_ref, c3_ref, gt_ref, cw_ref, wab_ref, wcb_ref, wo_ref, g_ref,
             conv_ref, a_ref, cv_ref, mg_ref, h1_ref, hn_ref, carry_ref):
        @pl.when(pl.program_id(0) == 0)
        def _():
            carry_ref[...] = jnp.zeros_like(carry_ref)

        c3v = c3_ref[...]
        cb, cc, cx = c3v[:, :CONV_W], c3v[:, CONV_W:2 * CONV_W], c3v[:, 2 * CONV_W:]
        z = cc * cx
        prev = carry_ref[...]
        cw = cw_ref[...]
        cz = cw[2:3] * z + cw[1:2] * _shift_down(z, 1, prev[7:8]) + cw[0:1] * _shift_down(z, 2, prev[6:8])
        carry_ref[...] = z[tm - 8:tm]
        conv = (cb * cz).astype(BF16)
        conv_ref[...] = conv
        a = jnp.dot(at_ref[...], wab_ref[...], preferred_element_type=F32)
        cv = jnp.dot(conv, wcb_ref[...], preferred_element_type=F32)
        a_ref[...] = a.astype(BF16)
        cv_ref[...] = cv.astype(BF16)
        gt = gt_ref[...]
        merged = (_sigmoid(gt[:, :D_MODEL]) * a + _sigmoid(gt[:, D_MODEL:]) * cv).astype(BF16)
        mg_ref[...] = merged
        h1 = x_ref[...] + jnp.dot(merged, wo_ref[...], preferred_element_type=F32)
        h1_ref[...] = h1
        hn_ref[...] = (h1 * _rstd(h1) * g_ref[...]).astype(BF16)

    row = lambda w: pl.BlockSpec((tm, w), lambda i: (i, 0))
    return pl.pallas_call(
        body, name="mix_fwd", grid=(s // tm,),
        in_specs=[row(D_MODEL), row(ATTN_W), row(C3_W), row(GATES_W), _resident((3, CONV_W)),
                  _resident((ATTN_W, D_MODEL)), _resident((CONV_W, D_MODEL)), _resident((D_MODEL, D_MODEL)),
                  _resident((1, D_MODEL))],
        out_specs=[row(CONV_W), row(D_MODEL), row(D_MODEL), row(D_MODEL), row(D_MODEL), row(D_MODEL)],
        out_shape=[jax.ShapeDtypeStruct((s, CONV_W), BF16), jax.ShapeDtypeStruct((s, D_MODEL), BF16),
                   jax.ShapeDtypeStruct((s, D_MODEL), BF16), jax.ShapeDtypeStruct((s, D_MODEL), BF16),
                   jax.ShapeDtypeStruct((s, D_MODEL), F32), jax.ShapeDtypeStruct((s, D_MODEL), BF16)],
        scratch_shapes=[pltpu.VMEM((8, CONV_W), F32)],
        compiler_params=_params("arbitrary"),
    )(x, attn, c3, gates, conv_w, w_ab, w_cb, w_out, g2)


def _ffn_fwd_loss(hn, h1, w_up, ffn_cw, w_down, g3, target, tm):
    s = hn.shape[0]

    def body(hn_ref, h1_ref, wu_ref, cw_ref, wd_ref, g_ref, t_ref,
             u_ref, act_ref, dh2_ref, loss_ref, gfn_ref, carry_ref):
        @pl.when(pl.program_id(0) == 0)
        def _():
            carry_ref[...] = jnp.zeros_like(carry_ref)
            loss_ref[...] = jnp.zeros_like(loss_ref)
            gfn_ref[...] = jnp.zeros_like(gfn_ref)

        u = jnp.dot(hn_ref[...], wu_ref[...], preferred_element_type=F32)
        u_ref[...] = u.astype(BF16)
        prev = carry_ref[...]
        cw = cw_ref[...]
        up = cw[2:3] * u + cw[1:2] * _shift_down(u, 1, prev[7:8]) + cw[0:1] * _shift_down(u, 2, prev[6:8])
        carry_ref[...] = u[tm - 8:tm]
        gate, val = up[:, :D_FF], up[:, D_FF:]
        act = (gate * _sigmoid(gate) * val).astype(BF16)
        act_ref[...] = act
        h2 = h1_ref[...] + jnp.dot(act, wd_ref[...], preferred_element_type=F32)
        rstd = _rstd(h2)
        g = g_ref[...]
        err = h2 * rstd * g - t_ref[...]
        loss_ref[...] += jnp.sum(err * err) * (0.5 / D_MODEL)
        dh2, dg = _rms_bwd(err * (1.0 / D_MODEL), h2, rstd, g)
        dh2_ref[...] = dh2
        gfn_ref[...] += jnp.sum(dg, axis=0, keepdims=True)

    row = lambda w: pl.BlockSpec((tm, w), lambda i: (i, 0))
    acc = lambda w: pl.BlockSpec((1, w), lambda i: (0, 0))
    return pl.pallas_call(
        body, name="ffn_fwd_loss", grid=(s // tm,),
        in_specs=[row(D_MODEL), row(D_MODEL), _resident((D_MODEL, FF2)), _resident((3, FF2)),
                  _resident((D_FF, D_MODEL)), _resident((1, D_MODEL)), row(D_MODEL)],
        out_specs=[row(FF2), row(D_FF), row(D_MODEL), acc(128), acc(D_MODEL)],
        out_shape=[jax.ShapeDtypeStruct((s, FF2), BF16), jax.ShapeDtypeStruct((s, D_FF), BF16),
                   jax.ShapeDtypeStruct((s, D_MODEL), F32), jax.ShapeDtypeStruct((1, 128), F32),
                   jax.ShapeDtypeStruct((1, D_MODEL), F32)],
        scratch_shapes=[pltpu.VMEM((8, FF2), F32)],
        compiler_params=_params("arbitrary"),
    )(hn, h1, w_up, ffn_cw, w_down, g3, target)


def _ffn_bwd(dh2, u, h1, w_up, ffn_cw, w_down, g2, tm):
    s = dh2.shape[0]
    nt = s // tm
    halo = 16

    def body(dh2_ref, u_ref, uh_ref, h1_ref, wu_ref, cw_ref, wd_ref, g_ref,
             du_ref, dh1_ref, gcw_ref, gg_ref, carry_ref):
        i = pl.program_id(0)

        @pl.when(i == 0)
        def _():
            carry_ref[...] = jnp.zeros_like(carry_ref)
            gcw_ref[...] = jnp.zeros_like(gcw_ref)
            gg_ref[...] = jnp.zeros_like(gg_ref)

        dh2v = dh2_ref[...]
        dact = lax.dot_general(dh2v.astype(BF16), wd_ref[...], NT, preferred_element_type=F32)
        u = u_ref[...].astype(F32)
        uh = uh_ref[...].astype(F32) * (i < nt - 1).astype(F32)
        u1 = _shift_down(u, 1, uh[halo - 1:halo])
        u2 = _shift_down(u, 2, uh[halo - 2:halo])
        cw = cw_ref[...]
        up = cw[2:3] * u + cw[1:2] * u1 + cw[0:1] * u2
        gate, val = up[:, :D_FF], up[:, D_FF:]
        sg = _sigmoid(gate)
        dval = dact * (gate * sg)
        dgate = dact * val * (sg * (1.0 + gate * (1.0 - sg)))
        dup = jnp.concatenate([dgate, dval], axis=1)
        gcw_ref[2:3, :] += jnp.sum(dup * u, axis=0, keepdims=True)
        gcw_ref[1:2, :] += jnp.sum(dup * u1, axis=0, keepdims=True)
        gcw_ref[0:1, :] += jnp.sum(dup * u2, axis=0, keepdims=True)
        nxt = carry_ref[...]
        du = (cw[2:3] * dup + cw[1:2] * _shift_up(dup, 1, nxt[0:1]) + cw[0:1] * _shift_up(dup, 2, nxt[0:2]))
        carry_ref[...] = dup[0:8]
        du = du.astype(BF16)
        du_ref[...] = du
        dhn = lax.dot_general(du, wu_ref[...], NT, preferred_element_type=F32)
        h1v = h1_ref[...]
        dh1, dg = _rms_bwd(dhn, h1v, _rstd(h1v), g_ref[...])
        dh1_ref[...] = dh2v + dh1
        gg_ref[...] += jnp.sum(dg, axis=0, keepdims=True)

    row = lambda w: pl.BlockSpec((tm, w), lambda i: (nt - 1 - i, 0))
    return pl.pallas_call(
        body, name="ffn_bwd", grid=(nt,),
        in_specs=[row(D_MODEL), row(FF2),
                  pl.BlockSpec((halo, FF2), lambda i: (jnp.maximum((nt - 1 - i) * (tm // halo) - 1, 0), 0)),
                  row(D_MODEL), _resident((D_MODEL, FF2)), _resident((3, FF2)), _resident((D_FF, D_MODEL)),
                  _resident((1, D_MODEL))],
        out_specs=[row(FF2), row(D_MODEL), pl.BlockSpec((3, FF2), lambda i: (0, 0)),
                   pl.BlockSpec((1, D_MODEL), lambda i: (0, 0))],
        out_shape=[jax.ShapeDtypeStruct((s, FF2), BF16), jax.ShapeDtypeStruct((s, D_MODEL), F32),
                   jax.ShapeDtypeStruct((3, FF2), F32), jax.ShapeDtypeStruct((1, D_MODEL), F32)],
        scratch_shapes=[pltpu.VMEM((8, FF2), F32)],
        compiler_params=_params("arbitrary"),
    )(dh2, u, u, h1, w_up, ffn_cw, w_down, g2)


def _mix_bwd(dh1, gates, a, cv, c3, conv_w, w_ab, w_cb, w_out, tm):
    s = dh1.shape[0]
    nt = s // tm
    halo = 8

    def body(dh1_ref, gt_ref, a_ref, cv_ref, c3_ref, ch_ref, cw_ref, wab_ref, wcb_ref, wo_ref,
             dat_ref, da_ref, dcv_ref, dc3_ref, dgt_ref, gcw_ref, carry_ref):
        i = pl.program_id(0)

        @pl.when(i == 0)
        def _():
            carry_ref[...] = jnp.zeros_like(carry_ref)
            gcw_ref[...] = jnp.zeros_like(gcw_ref)

        dm = lax.dot_general(dh1_ref[...].astype(BF16), wo_ref[...], NT, preferred_element_type=F32)
        gt = gt_ref[...]
        sa, sc = _sigmoid(gt[:, :D_MODEL]), _sigmoid(gt[:, D_MODEL:])
        da = (dm * sa).astype(BF16)
        dcv = (dm * sc).astype(BF16)
        da_ref[...] = da
        dcv_ref[...] = dcv
        dgt_ref[...] = jnp.concatenate(
            [dm * a_ref[...].astype(F32) * (sa * (1.0 - sa)), dm * cv_ref[...].astype(F32) * (sc * (1.0 - sc))],
            axis=1).astype(BF16)
        dat_ref[...] = lax.dot_general(da, wab_ref[...], NT, preferred_element_type=F32).astype(BF16)
        dconv = lax.dot_general(dcv, wcb_ref[...], NT, preferred_element_type=F32)
        c3v = c3_ref[...]
        cb, cc, cx = c3v[:, :CONV_W], c3v[:, CONV_W:2 * CONV_W], c3v[:, 2 * CONV_W:]
        z = cc * cx
        chv = ch_ref[...] * (i < nt - 1).astype(F32)
        zh = chv[:, CONV_W:2 * CONV_W] * chv[:, 2 * CONV_W:]
        z1 = _shift_down(z, 1, zh[halo - 1:halo])
        z2 = _shift_down(z, 2, zh[halo - 2:halo])
        cw = cw_ref[...]
        cz = cw[2:3] * z + cw[1:2] * z1 + cw[0:1] * z2
        dcz = dconv * cb
        gcw_ref[2:3, :] += jnp.sum(dcz * z, axis=0, keepdims=True)
        gcw_ref[1:2, :] += jnp.sum(dcz * z1, axis=0, keepdims=True)
        gcw_ref[0:1, :] += jnp.sum(dcz * z2, axis=0, keepdims=True)
        nxt = carry_ref[...]
        dz = cw[2:3] * dcz + cw[1:2] * _shift_up(dcz, 1, nxt[0:1]) + cw[0:1] * _shift_up(dcz, 2, nxt[0:2])
        carry_ref[...] = dcz[0:8]
        dc3_ref[...] = jnp.concatenate([dconv * cz, dz * cx, dz * cc], axis=1).astype(BF16)

    row = lambda w: pl.BlockSpec((tm, w), lambda i: (nt - 1 - i, 0))
    return pl.pallas_call(
        body, name="mix_bwd", grid=(nt,),
        in_specs=[row(D_MODEL), row(GATES_W), row(D_MODEL), row(D_MODEL), row(C3_W),
                  pl.BlockSpec((halo, C3_W), lambda i: (jnp.maximum((nt - 1 - i) * (tm // halo) - 1, 0), 0)),
                  _resident((3, CONV_W)), _resident((ATTN_W, D_MODEL)), _resident((CONV_W, D_MODEL)),
                  _resident((D_MODEL, D_MODEL))],
        out_specs=[row(ATTN_W), row(D_MODEL), row(D_MODEL), row(C3_W), row(GATES_W),
                   pl.BlockSpec((3, CONV_W), lambda i: (0, 0))],
        out_shape=[jax.ShapeDtypeStruct((s, ATTN_W), BF16), jax.ShapeDtypeStruct((s, D_MODEL), BF16),
                   jax.ShapeDtypeStruct((s, D_MODEL), BF16), jax.ShapeDtypeStruct((s, C3_W), BF16),
                   jax.ShapeDtypeStruct((s, GATES_W), BF16), jax.ShapeDtypeStruct((3, CONV_W), F32)],
        scratch_shapes=[pltpu.VMEM((8, CONV_W), F32)],
        compiler_params=_params("arbitrary"),
    )(dh1, gates, a, cv, c3, c3, conv_w, w_ab, w_cb, w_out)


def _attn_bwd(qkv, sinks, o, do):
    s = qkv.shape[0]
    nb = s // BLOCK

    def body(sk_ref, q_ref, kp_ref, kc_ref, vp_ref, vc_ref, o_ref, do_ref,
             dq_ref, dk_ref, dv_ref, dsk_ref, ck_ref, cvv_ref):
        i = pl.program_id(0)

        @pl.when(i == 0)
        def _():
            ck_ref[...] = jnp.zeros_like(ck_ref)
            cvv_ref[...] = jnp.zeros_like(cvv_ref)
            dsk_ref[...] = jnp.zeros_like(dsk_ref)

        @pl.when(i < nb)
        def _():
            mask = _attn_mask(i == 0)
            q, kp, kc, vp, vc = q_ref[...], kp_ref[...], kc_ref[...], vp_ref[...], vc_ref[...]
            ov, dov = o_ref[...], do_ref[...]
            dqs, dks, dvs = [], [], []
            for h in range(N_KV_HEADS):
                hs = slice(h * HEAD_DIM, (h + 1) * HEAD_DIM)
                k2 = jnp.concatenate([kp[:, hs], kc[:, hs]], axis=0)
                v2 = jnp.concatenate([vp[:, hs], vc[:, hs]], axis=0)
                qg, og, dog = _stack_heads(q, h), _stack_heads(ov, h), _stack_heads(dov, h)
                sc = lax.dot_general(qg, k2, NT, preferred_element_type=F32) * ATTN_SCALE
                sc = jnp.where(mask, sc, NEG)
                sink = _sink_column(sk_ref, h)
                m = jnp.maximum(jnp.max(sc, axis=1, keepdims=True), sink)
                p = jnp.exp(sc - m)
                psink = jnp.exp(sink - m)
                inv = 1.0 / (jnp.sum(p, axis=1, keepdims=True) + psink)
                p = p * inv
                delta = jnp.sum(dog.astype(F32) * og.astype(F32), axis=1, keepdims=True)
                dp = lax.dot_general(dog, v2, NT, preferred_element_type=F32)
                ds = (p * (dp - delta)).astype(BF16)
                dqs.append(jnp.dot(ds, k2, preferred_element_type=F32) * ATTN_SCALE)
                dks.append(lax.dot_general(ds, qg, TN, preferred_element_type=F32) * ATTN_SCALE)
                dvs.append(lax.dot_general(p.astype(BF16), dog, TN, preferred_element_type=F32))
                dsink = -(psink * inv * delta)
                for g in range(GROUP):
                    r = h * GROUP + g
                    dsk_ref[r:r + 1, :] += jnp.sum(dsink[g * BLOCK:(g + 1) * BLOCK])
            dq_ref[...] = _unstack_heads(dqs).astype(BF16)
            dk2 = jnp.concatenate(dks, axis=1)
            dv2 = jnp.concatenate(dvs, axis=1)
            dk_ref[...] = (ck_ref[...] + dk2[:BLOCK]).astype(BF16)
            dv_ref[...] = (cvv_ref[...] + dv2[:BLOCK]).astype(BF16)
            ck_ref[...] = dk2[BLOCK:]
            cvv_ref[...] = dv2[BLOCK:]

        @pl.when(i == nb)
        def _():
            dk_ref[...] = ck_ref[...].astype(BF16)
            dv_ref[...] = cvv_ref[...].astype(BF16)

    cur = lambda i: jnp.minimum(i, nb - 1)
    done = lambda i: jnp.maximum(i - 1, 0)
    return pl.pallas_call(
        body, name="attn_bwd", grid=(nb + 1,),
        in_specs=[pl.BlockSpec(memory_space=pltpu.SMEM), *_attn_specs(nb),
                  pl.BlockSpec((BLOCK, ATTN_W), lambda i: (cur(i), 0)),
                  pl.BlockSpec((BLOCK, ATTN_W), lambda i: (cur(i), 0))],
        out_specs=[pl.BlockSpec((BLOCK, ATTN_W), lambda i: (cur(i), 0)),
                   pl.BlockSpec((BLOCK, KV_W), lambda i: (done(i), 0)),
                   pl.BlockSpec((BLOCK, KV_W), lambda i: (done(i), 0)),
                   pl.BlockSpec((N_HEADS, 128), lambda i: (0, 0))],
        out_shape=[jax.ShapeDtypeStruct((s, ATTN_W), BF16), jax.ShapeDtypeStruct((s, KV_W), BF16),
                   jax.ShapeDtypeStruct((s, KV_W), BF16), jax.ShapeDtypeStruct((N_HEADS, 128), F32)],
        scratch_shapes=[pltpu.VMEM((BLOCK, KV_W), F32), pltpu.VMEM((BLOCK, KV_W), F32)],
        compiler_params=_params("arbitrary"),
    )(sinks, qkv, qkv, qkv, qkv, qkv, o, do)


def _inproj_bwd(dq, dk, dv, dc3, dgt, w_in, x, dh1, g1, tm):
    s = x.shape[0]

    def body(dq_ref, dk_ref, dv_ref, dc3_ref, dgt_ref, w_ref, x_ref, dh1_ref, g_ref,
             dx_ref, dp_ref, gb_ref, gg_ref):
        @pl.when(pl.program_id(0) == 0)
        def _():
            gb_ref[...] = jnp.zeros_like(gb_ref)
            gg_ref[...] = jnp.zeros_like(gg_ref)

        dp = jnp.concatenate([dq_ref[...], dk_ref[...], dv_ref[...], dc3_ref[...], dgt_ref[...]], axis=1)
        dp_ref[...] = dp
        gb_ref[...] += jnp.sum(dp.astype(F32), axis=0, keepdims=True)
        dxn = lax.dot_general(dp, w_ref[...], NT, preferred_element_type=F32)
        xf = x_ref[...]
        dx, dg = _rms_bwd(dxn, xf, _rstd(xf), g_ref[...])
        dx_ref[...] = dh1_ref[...] + dx
        gg_ref[...] += jnp.sum(dg, axis=0, keepdims=True)

    row = lambda w: pl.BlockSpec((tm, w), lambda i: (i, 0))
    acc = lambda w: pl.BlockSpec((1, w), lambda i: (0, 0))
    return pl.pallas_call(
        body, name="inproj_bwd", grid=(s // tm,),
        in_specs=[row(ATTN_W), row(KV_W), row(KV_W), row(C3_W), row(GATES_W), _resident((D_MODEL, IN_W)),
                  row(D_MODEL), row(D_MODEL), _resident((1, D_MODEL))],
        out_specs=[row(D_MODEL), row(IN_W), acc(IN_W), acc(D_MODEL)],
        out_shape=[jax.ShapeDtypeStruct((s, D_MODEL), F32), jax.ShapeDtypeStruct((s, IN_W), BF16),
                   jax.ShapeDtypeStruct((1, IN_W), F32), jax.ShapeDtypeStruct((1, D_MODEL), F32)],
        compiler_params=_params("arbitrary"),
    )(dq, dk, dv, dc3, dgt, w_in, x, dh1, g1)


def _wgrad(a, b, bm, bn, bk, name):
    s, m = a.shape
    n = b.shape[1]
    nk = s // bk

    def body(a_ref, b_ref, o_ref, acc_ref):
        k = pl.program_id(2)

        @pl.when(k == 0)
        def _():
            acc_ref[...] = jnp.zeros_like(acc_ref)

        acc_ref[...] += lax.dot_general(a_ref[...].astype(BF16), b_ref[...].astype(BF16), TN,
                                        preferred_element_type=F32)

        @pl.when(k == nk - 1)
        def _():
            o_ref[...] = acc_ref[...].astype(BF16)

    return pl.pallas_call(
        body, name=name, grid=(m // bm, n // bn, nk),
        in_specs=[pl.BlockSpec((bk, bm), lambda i, j, k: (k, i)), pl.BlockSpec((bk, bn), lambda i, j, k: (k, j))],
        out_specs=pl.BlockSpec((bm, bn), lambda i, j, k: (i, j)),
        out_shape=jax.ShapeDtypeStruct((m, n), BF16),
        scratch_shapes=[pltpu.VMEM((bm, bn), F32)],
        compiler_params=_params("parallel", "parallel", "arbitrary"),
    )(a, b)


def _wgrad_in(xn, dproj, bk):
    s = xn.shape[0]
    nk = s // bk

    def body(a_ref, b_ref, o_ref, acc_ref):
        k = pl.program_id(0)

        @pl.when(k == 0)
        def _():
            acc_ref[...] = jnp.zeros_like(acc_ref)

        av, bv = a_ref[...], b_ref[...]
        for j in range(N_CHIPS):
            acc_ref[j] += lax.dot_general(av, bv[:, j * IN_SHARD:(j + 1) * IN_SHARD], TN,
                                          preferred_element_type=F32)

        @pl.when(k == nk - 1)
        def _():
            o_ref[...] = acc_ref[...].astype(BF16)

    return pl.pallas_call(
        body, name="wgrad_in", grid=(nk,),
        in_specs=[pl.BlockSpec((bk, D_MODEL), lambda k: (k, 0)), pl.BlockSpec((bk, IN_W), lambda k: (k, 0))],
        out_specs=_resident((N_CHIPS, D_MODEL, IN_SHARD)),
        out_shape=jax.ShapeDtypeStruct((N_CHIPS, D_MODEL, IN_SHARD), BF16),
        scratch_shapes=[pltpu.VMEM((N_CHIPS, D_MODEL, IN_SHARD), F32)],
        compiler_params=_params("arbitrary"),
    )(xn, dproj)


def _local_step(x, target, g1, b_in, sinks, conv_w, g2, ffn_cw, g3, w_in, w_ab, w_cb, w_out, w_up, w_down):
    s = x.shape[0]
    tm = min(256, s)
    tm2 = min(512, s)
    bk = min(1024, s)
    xn, qkv, c3, gates = _inproj_fwd(x, g1, w_in, b_in, tm2)
    attn = _attn_fwd(qkv, sinks)
    conv, a, cv, merged, h1, hn = _mix_fwd(x, attn, c3, gates, conv_w, w_ab, w_cb, w_out, g2, tm)
    u, act, dh2, loss, g_fn = _ffn_fwd_loss(hn, h1, w_up, ffn_cw, w_down, g3, target, tm)
    du, dh1, g_fcw, g_g2 = _ffn_bwd(dh2, u, h1, w_up, ffn_cw, w_down, g2, tm)
    gw_down = _wgrad(act, dh2, D_FF // 2, D_MODEL, bk, "wgrad_down")
    gw_up = _wgrad(hn, du, D_MODEL, FF2 // 4, bk, "wgrad_up")
    dattn, da, dcv, dc3, dgt, g_cw = _mix_bwd(dh1, gates, a, cv, c3, conv_w, w_ab, w_cb, w_out, tm)
    gw_out = _wgrad(merged, dh1, D_MODEL, D_MODEL, bk, "wgrad_out")
    gw_ab = _wgrad(attn, da, ATTN_W, D_MODEL, bk, "wgrad_ab")
    gw_cb = _wgrad(conv, dcv, CONV_W, D_MODEL, bk, "wgrad_cb")
    dq, dk, dv, g_sk = _attn_bwd(qkv, sinks, attn, dattn)
    grad_x, dproj, g_b, g_g1 = _inproj_bwd(dq, dk, dv, dc3, dgt, w_in, x, dh1, g1, tm2)
    gw_in = _wgrad_in(xn, dproj, min(512, s))
    small = dict(loss=loss, g_g1=g_g1, g_b=g_b, g_sk=g_sk, g_cw=g_cw, g_g2=g_g2, g_fcw=g_fcw, g_fn=g_fn)
    big = [gw_in, gw_ab, gw_cb, gw_out, gw_up, gw_down]
    return grad_x, small, big


def _row_tile(rows, bytes_per_row):
    best = 16
    for t in range(16, rows + 1, 16):
        if rows % t == 0 and t * bytes_per_row <= 6 * 1024 * 1024:
            best = t
    return best


def _rowwise(fn, ins, out_dtypes, name):
    rows, cols = ins[0].shape[-2:]
    per_row = sum(a.size // rows * a.dtype.itemsize for a in ins) + sum(cols * jnp.dtype(d).itemsize for d in out_dtypes)
    tr = _row_tile(rows, per_row)
    n_in = len(ins)

    def body(*refs):
        outs = fn(*[r[...] for r in refs[:n_in]])
        for o_ref, o in zip(refs[n_in:], outs):
            o_ref[...] = o.astype(o_ref.dtype)

    def spec(a):
        if a.ndim == 3:
            return pl.BlockSpec((a.shape[0], tr, cols), lambda i: (0, i, 0))
        return pl.BlockSpec((tr, cols), lambda i: (i, 0))

    return pl.pallas_call(
        body, name=name, grid=(rows // tr,),
        in_specs=[spec(a) for a in ins],
        out_specs=[pl.BlockSpec((tr, cols), lambda i: (i, 0)) for _ in out_dtypes],
        out_shape=[jax.ShapeDtypeStruct((rows, cols), d) for d in out_dtypes],
        compiler_params=_params("parallel"),
    )(*ins)


def _adamw(w, g, m, v):
    m = ADAM_B1 * m + (1.0 - ADAM_B1) * g
    v = ADAM_B2 * v + (1.0 - ADAM_B2) * (g * g)
    m_hat = m / (1.0 - ADAM_B1 ** ADAM_STEP)
    v_hat = v / (1.0 - ADAM_B2 ** ADAM_STEP)
    return -ADAM_LR * (m_hat / (jnp.sqrt(v_hat) + ADAM_EPS) + ADAM_WD * w), m, v


def _adamw_small(params):
    n = len(params)

    def body(*refs):
        for k in range(n):
            w, g, m, v = (r[...] for r in refs[4 * k:4 * k + 4])
            for o_ref, o in zip(refs[4 * n + 3 * k:4 * n + 3 * k + 3], _adamw(w, g, m, v)):
                o_ref[...] = o

    flat = [a for p in params for a in p]
    return pl.pallas_call(
        body, name="adamw_small",
        out_shape=[jax.ShapeDtypeStruct(p[0].shape, F32) for p in params for _ in range(3)],
    )(*flat)


def _repack_in(wg):
    tr = 256

    def body(w_ref, o_ref):
        for j in range(N_CHIPS):
            o_ref[:, j * IN_SHARD:(j + 1) * IN_SHARD] = w_ref[j]

    return pl.pallas_call(
        body, name="repack_in", grid=(D_MODEL // tr,),
        in_specs=[pl.BlockSpec((N_CHIPS, tr, IN_SHARD), lambda i: (0, i, 0))],
        out_specs=pl.BlockSpec((tr, IN_W), lambda i: (i, 0)),
        out_shape=jax.ShapeDtypeStruct((D_MODEL, IN_W), BF16),
        compiler_params=_params("parallel"),
    )(wg)


class _Layout:
    def __init__(self, rows, cols, stacked):
        self.rows, self.cols, self.stacked = rows, cols, stacked

    def whole(self, rows=None):
        r = self.rows if rows is None else rows
        return (N_CHIPS, r, self.cols) if self.stacked else (r, N_CHIPS * self.cols)

    def half_rows(self, h):
        nr = self.rows // 2
        return pl.ds(pl.multiple_of(h * nr, 16), nr)

    def block(self, ref, p, rows=slice(None)):
        if self.stacked:
            return ref.at[p, rows, :]
        return ref.at[rows, pl.ds(pl.multiple_of(p * self.cols, 128), self.cols)]

    def all_chips(self, ref, rows):
        return ref.at[:, rows, :] if self.stacked else ref.at[rows, :]


BIG = (
    _Layout(D_MODEL, IN_SHARD, True),
    _Layout(ATTN_W, D_MODEL // N_CHIPS, False),
    _Layout(CONV_W, D_MODEL // N_CHIPS, False),
    _Layout(D_MODEL // N_CHIPS, D_MODEL, True),
    _Layout(D_MODEL, FF2 // N_CHIPS, False),
    _Layout(D_FF // N_CHIPS, D_MODEL, True),
)
N_BIG = len(BIG)
_ANY = pl.BlockSpec(memory_space=pl.ANY)


def _position():
    x, y, c = lax.axis_index("x"), lax.axis_index("y"), lax.axis_index("c")
    return x, y, c, 2 * x + y


def _core_of_chip(p, c):
    return (p >> 1, p & 1, c)


def _gather_weights(shards):
    def body(*refs):
        src, dst = refs[:N_BIG], refs[N_BIG:2 * N_BIG]
        send1, recv1, send2, recv2, lsem = refs[2 * N_BIG:]
        x, y, c, me = _position()
        sibling = (x, y, 1 - c)

        local = [pltpu.make_async_copy(src[w], BIG[w].block(dst[w], me), lsem.at[w]) for w in range(N_BIG)]
        for cp in local:
            cp.start()

        def first(w, d):
            rows = BIG[w].half_rows(c)
            return pltpu.make_async_remote_copy(
                src_ref=src[w].at[rows, :], dst_ref=BIG[w].block(dst[w], me, rows),
                send_sem=send1.at[w, d - 1], recv_sem=recv1.at[w, d - 1],
                device_id=_core_of_chip(me ^ d, c), device_id_type=MESH)

        def passed(w, d, h):
            blk = BIG[w].block(dst[w], me ^ d, BIG[w].half_rows(h))
            return pltpu.make_async_remote_copy(
                src_ref=blk, dst_ref=blk, send_sem=send2.at[w, d - 1], recv_sem=recv2.at[w, d - 1],
                device_id=sibling, device_id_type=MESH)

        def arrived(w, d):
            blk = BIG[w].block(dst[w], me ^ d, BIG[w].half_rows(c))
            return pltpu.make_async_remote_copy(
                src_ref=blk, dst_ref=blk, send_sem=send1.at[w, d - 1], recv_sem=recv1.at[w, d - 1],
                device_id=sibling, device_id_type=MESH)

        pairs = [(w, d) for w in range(N_BIG) for d in (1, 2, 3)]
        for w, d in pairs:
            first(w, d).start()
        for w, d in pairs:
            arrived(w, d).wait_recv()
            passed(w, d, c).start()
        for w, d in pairs:
            passed(w, d, 1 - c).wait_recv()
        for w, d in pairs:
            first(w, d).wait_send()
            passed(w, d, c).wait_send()
        for cp in local:
            cp.wait()

    sems = pltpu.SemaphoreType.DMA((N_BIG, 3))
    return pl.pallas_call(
        body, name="gather_weights",
        in_specs=[_ANY] * N_BIG, out_specs=[_ANY] * N_BIG,
        out_shape=[jax.ShapeDtypeStruct(lay.whole(), BF16) for lay in BIG],
        scratch_shapes=[sems, sems, sems, sems, pltpu.SemaphoreType.DMA((N_BIG,))],
    )(*shards)


def _rs_pair(grads):
    def body(*refs):
        src, mine, theirs = refs[:N_BIG], refs[N_BIG:2 * N_BIG], refs[2 * N_BIG:3 * N_BIG]
        send, recv, lsem = refs[3 * N_BIG:]
        x, y, c, _ = _position()
        local, remote = [], []
        for w in range(N_BIG):
            lay = BIG[w]
            local.append(pltpu.make_async_copy(lay.all_chips(src[w], lay.half_rows(c)), mine[w], lsem.at[w]))
            remote.append(pltpu.make_async_remote_copy(
                src_ref=lay.all_chips(src[w], lay.half_rows(1 - c)), dst_ref=theirs[w],
                send_sem=send.at[w], recv_sem=recv.at[w], device_id=(x, y, 1 - c), device_id_type=MESH))
        for cp in local + remote:
            cp.start()
        for cp in remote:
            cp.wait()
        for cp in local:
            cp.wait()

    halves = [jax.ShapeDtypeStruct(lay.whole(lay.rows // 2), BF16) for lay in BIG]
    sems = pltpu.SemaphoreType.DMA((N_BIG,))
    out = pl.pallas_call(
        body, name="rs_pair", in_specs=[_ANY] * N_BIG, out_specs=[_ANY] * (2 * N_BIG),
        out_shape=halves + halves, scratch_shapes=[sems, sems, sems],
    )(*grads)
    return out[:N_BIG], out[N_BIG:]


def _rs_chips(sums):
    def body(*refs):
        src, dst = refs[:N_BIG], refs[N_BIG:2 * N_BIG]
        send, recv, lsem = refs[2 * N_BIG:]
        x, y, c, me = _position()

        def to_chip(w, d):
            return pltpu.make_async_remote_copy(
                src_ref=BIG[w].block(src[w], me ^ d), dst_ref=dst[w].at[me],
                send_sem=send.at[w, d - 1], recv_sem=recv.at[w, d - 1],
                device_id=_core_of_chip(me ^ d, c), device_id_type=MESH)

        def from_chip(w, d):
            slot = dst[w].at[me ^ d]
            return pltpu.make_async_remote_copy(
                src_ref=slot, dst_ref=slot, send_sem=send.at[w, d - 1], recv_sem=recv.at[w, d - 1],
                device_id=(x, y, 1 - c), device_id_type=MESH)

        local = [pltpu.make_async_copy(BIG[w].block(src[w], me), dst[w].at[me], lsem.at[w]) for w in range(N_BIG)]
        pairs = [(w, d) for w in range(N_BIG) for d in (1, 2, 3)]
        for cp in local:
            cp.start()
        for w, d in pairs:
            to_chip(w, d).start()
        for w, d in pairs:
            from_chip(w, d).wait_recv()
        for w, d in pairs:
            to_chip(w, d).wait_send()
        for cp in local:
            cp.wait()

    sems = pltpu.SemaphoreType.DMA((N_BIG, 3))
    return pl.pallas_call(
        body, name="rs_chips", in_specs=[_ANY] * N_BIG, out_specs=[_ANY] * N_BIG,
        out_shape=[jax.ShapeDtypeStruct((N_CHIPS, lay.rows // 2, lay.cols), BF16) for lay in BIG],
        scratch_shapes=[sems, sems, pltpu.SemaphoreType.DMA((N_BIG,))],
    )(*sums)


def _share_halves(halves):
    def body(*refs):
        src, dst = refs[:N_BIG], refs[N_BIG:2 * N_BIG]
        send, recv, lsem = refs[2 * N_BIG:]
        x, y, c, _ = _position()
        local, remote, landed = [], [], []
        for w in range(N_BIG):
            lay = BIG[w]
            local.append(pltpu.make_async_copy(src[w], dst[w].at[lay.half_rows(c), :], lsem.at[w]))
            remote.append(pltpu.make_async_remote_copy(
                src_ref=src[w], dst_ref=dst[w].at[lay.half_rows(c), :],
                send_sem=send.at[w], recv_sem=recv.at[w], device_id=(x, y, 1 - c), device_id_type=MESH))
            other = dst[w].at[lay.half_rows(1 - c), :]
            landed.append(pltpu.make_async_remote_copy(
                src_ref=other, dst_ref=other, send_sem=send.at[w], recv_sem=recv.at[w],
                device_id=(x, y, 1 - c), device_id_type=MESH))
        for cp in local + remote:
            cp.start()
        for cp in landed:
            cp.wait_recv()
        for cp in remote:
            cp.wait_send()
        for cp in local:
            cp.wait()

    sems = pltpu.SemaphoreType.DMA((N_BIG,))
    return pl.pallas_call(
        body, name="share_halves", in_specs=[_ANY] * N_BIG, out_specs=[_ANY] * N_BIG,
        out_shape=[jax.ShapeDtypeStruct((lay.rows, lay.cols), F32) for lay in BIG],
        scratch_shapes=[sems, sems, sems],
    )(*halves)


N_DEV = 8


def _exchange_small(v, reduce):
    rows = v.shape[0]

    def body(v_ref, o_ref, *scratch):
        if reduce:
            slots, send, recv = scratch
        else:
            slots, (send, recv) = o_ref, scratch
        x, y, c = lax.axis_index("x"), lax.axis_index("y"), lax.axis_index("c")
        idx = 4 * x + 2 * y + c
        slots[idx] = v_ref[...]

        def to_peer(k):
            return pltpu.make_async_remote_copy(
                src_ref=v_ref, dst_ref=slots.at[idx], send_sem=send.at[k - 1], recv_sem=recv.at[k - 1],
                device_id=(x ^ (k >> 2), y ^ ((k >> 1) & 1), c ^ (k & 1)), device_id_type=MESH)

        def from_peer(k):
            return pltpu.make_async_remote_copy(
                src_ref=v_ref, dst_ref=slots.at[idx ^ k], send_sem=send.at[k - 1], recv_sem=recv.at[k - 1],
                device_id=(x, y, c), device_id_type=MESH)

        for k in range(1, N_DEV):
            to_peer(k).start()
        for k in range(1, N_DEV):
            from_peer(k).wait_recv()
        for k in range(1, N_DEV):
            to_peer(k).wait_send()
        if reduce:
            acc = slots[0]
            for q in range(1, N_DEV):
                acc = acc + slots[q]
            o_ref[...] = acc

    sems = pltpu.SemaphoreType.DMA((N_DEV - 1,))
    stacked = jax.ShapeDtypeStruct((N_DEV, rows, 128), F32)
    return pl.pallas_call(
        body, name="allreduce_small" if reduce else "allgather_small",
        out_shape=jax.ShapeDtypeStruct((rows, 128), F32) if reduce else stacked,
        scratch_shapes=([pltpu.VMEM((N_DEV, rows, 128), F32)] if reduce else []) + [sems, sems],
    )(v)


def _pack_rows(parts):
    padded = [jnp.pad(a, ((0, -a.shape[0] % 8), (0, 0))) for a in parts]
    starts = [sum(p.shape[0] for p in padded[:k]) for k in range(len(padded))]
    return jnp.concatenate(padded, axis=0), starts


def kernel(x, mix_norm, w_in, b_in, sinks, conv_w, w_attn_branch, w_conv_branch, w_out, ffn_norm, w_up, ffn_conv_w, w_down, final_norm, loss_target, m_mix_norm, m_w_in, m_b_in, m_sinks, m_conv_w, m_w_attn_branch, m_w_conv_branch, m_w_out, m_ffn_norm, m_w_up, m_ffn_conv_w, m_w_down, m_final_norm, v_mix_norm, v_w_in, v_b_in, v_sinks, v_conv_w, v_w_attn_branch, v_w_conv_branch, v_w_out, v_ffn_norm, v_w_up, v_ffn_conv_w, v_w_down, v_final_norm):
    me = 2 * lax.axis_index("x") + lax.axis_index("y")
    big_w = [w_in[0], w_attn_branch[0], w_conv_branch[0], w_out[0], w_up[0], w_down[0]]
    big_m = [m_w_in[0], m_w_attn_branch[0], m_w_conv_branch[0], m_w_out[0], m_w_up[0], m_w_down[0]]
    big_v = [v_w_in[0], v_w_attn_branch[0], v_w_conv_branch[0], v_w_out[0], v_w_up[0], v_w_down[0]]
    names = ("w_in", "w_ab", "w_cb", "w_out", "w_up", "w_down")

    shards16 = [_rowwise(lambda a: (a,), [w], [BF16], "cast_" + n)[0] for w, n in zip(big_w, names)]
    g_in, g_ab, g_cb, g_out, g_up, g_down = _gather_weights(shards16)
    taps, (_, t0) = _pack_rows([conv_w[0], ffn_conv_w[0].reshape(3 * (FF2 // N_CHIPS // 128), 128)])
    taps = _exchange_small(taps, reduce=False)[0::2]
    conv_full = taps[:, 0:3].transpose(1, 0, 2).reshape(3, CONV_W)
    ffn_cw_full = taps[:, t0:t0 + 33].reshape(N_CHIPS, 3, FF2 // N_CHIPS).transpose(1, 0, 2).reshape(3, FF2)

    grad_x, small, big = _local_step(
        x[0], loss_target[0], mix_norm, b_in, sinks[0], conv_full, ffn_norm, ffn_cw_full, final_norm[None, :],
        _repack_in(g_in), g_ab, g_cb, g_out.reshape(D_MODEL, D_MODEL), g_up, g_down.reshape(D_FF, D_MODEL))

    big[3] = big[3].reshape(BIG[3].whole())
    big[5] = big[5].reshape(BIG[5].whole())
    mine, theirs = _rs_pair(big)
    flat = lambda a: a.reshape(-1, a.shape[-1])
    sums = [_rowwise(lambda a, b: (a.astype(F32) + b.astype(F32),), [flat(a), flat(b)], [BF16], "pair_sum_" + n)[0]
            .reshape(a.shape) for a, b, n in zip(mine, theirs, names)]
    slots = _rs_chips(sums)

    def chip_sum(r):
        r = r.astype(F32)
        return (((r[0] + r[1]) + r[2]) + r[3],)

    halves = [_rowwise(chip_sum, [r], [F32], "chip_sum_" + n)[0] for r, n in zip(slots, names)]
    big_g = _share_halves(halves)
    big_new = [_rowwise(_adamw, [w, g, m, v], [F32, F32, F32], "adamw_" + n)
               for w, g, m, v, n in zip(big_w, big_g, big_m, big_v, names)]

    parts = [small["loss"], small["g_g1"], small["g_b"], jnp.pad(small["g_sk"][:, 0], (0, 120))[None, :],
             small["g_cw"], small["g_g2"], small["g_fcw"], small["g_fn"]]
    packed, at = _pack_rows([a.reshape(-1, 128) for a in parts])
    total = _exchange_small(packed, reduce=True)
    part = lambda k: total[at[k]:at[k] + parts[k].size // 128].reshape(parts[k].shape)
    loss = total[0, 0]
    g_mix, g_b, g_g2, g_fn = part(1), part(2), part(5), part(7)
    g_sk = part(3)[:, 0:N_HEADS]
    g_cw = lax.dynamic_slice(part(4), (0, me * 128), (3, 128))
    g_fcw = lax.dynamic_slice(part(6), (0, me * (FF2 // N_CHIPS)), (3, FF2 // N_CHIPS))
    small_p = [
        (mix_norm, g_mix, m_mix_norm, v_mix_norm), (b_in, g_b, m_b_in, v_b_in), (sinks, g_sk, m_sinks, v_sinks),
        (conv_w[0], g_cw, m_conv_w[0], v_conv_w[0]), (ffn_norm, g_g2, m_ffn_norm, v_ffn_norm),
        (ffn_conv_w[0], g_fcw, m_ffn_conv_w[0], v_ffn_conv_w[0]),
        (final_norm[None, :], g_fn, m_final_norm[None, :], v_final_norm[None, :])]
    small_new = _adamw_small(small_p)
    small_new = [small_new[3 * k:3 * k + 3] for k in range(len(small_p))]

    order = [("s", 0), ("b", 0), ("s", 1), ("s", 2), ("s", 3), ("b", 1), ("b", 2), ("b", 3), ("s", 4), ("b", 4),
             ("s", 5), ("b", 5), ("s", 6)]
    shapes = [mix_norm.shape, w_in.shape, b_in.shape, sinks.shape, conv_w.shape, w_attn_branch.shape,
              w_conv_branch.shape, w_out.shape, ffn_norm.shape, w_up.shape, ffn_conv_w.shape, w_down.shape,
              final_norm.shape]
    small_g = [p[1] for p in small_p]
    grads = [(small_g[k] if kind == "s" else big_g[k]).reshape(shp) for (kind, k), shp in zip(order, shapes)]
    news = [[(small_new[k][j] if kind == "s" else big_new[k][j]).reshape(shp) for (kind, k), shp in zip(order, shapes)]
            for j in range(3)]
    return (loss, grad_x[None], *grads, *news[0], *news[1], *news[2])
```

```python
import functools

import jax
import jax.numpy as jnp
from jax import lax
from jax.experimental import pallas as pl
from jax.experimental.pallas import tpu as pltpu

F32 = jnp.float32
BF16 = jnp.bfloat16

D_MODEL = 1024
HEAD_DIM = 64
N_HEADS = 8
N_KV_HEADS = 2
GROUP = N_HEADS // N_KV_HEADS
BLOCK = 128
ATTN_SCALE = HEAD_DIM ** -0.5
ATTN_W = N_HEADS * HEAD_DIM
KV_W = N_KV_HEADS * HEAD_DIM
CONV_W = 512
QKV_W = ATTN_W + 2 * KV_W
C3_W = 3 * CONV_W
GATES_W = 2 * D_MODEL
IN_W = QKV_W + C3_W + GATES_W
D_FF = 2816
FF2 = 2 * D_FF
NORM_EPS = 1e-5
N_CHIPS = 4
IN_SHARD = IN_W // N_CHIPS
NEG = -1e30

ADAM_LR = 0.001
ADAM_B1 = 0.9
ADAM_B2 = 0.999
ADAM_EPS = 1e-08
ADAM_WD = 0.01
ADAM_STEP = 10

VMEM_LIMIT = 56 * 1024 * 1024
MESH = pl.DeviceIdType.MESH

NT = (((1,), (1,)), ((), ()))
TN = (((0,), (0,)), ((), ()))


def _params(*sem):
    return pltpu.CompilerParams(dimension_semantics=sem, vmem_limit_bytes=VMEM_LIMIT)


def _resident(shape):
    return pl.BlockSpec(shape, lambda *_: (0,) * len(shape), pipeline_mode=pl.Buffered(1))


def _sigmoid(v):
    return 1.0 / (1.0 + jnp.exp(-v))


def _rstd(v):
    return lax.rsqrt(jnp.mean(v * v, axis=-1, keepdims=True) + NORM_EPS)


def _rms_bwd(dy, v, rstd, g):
    vhat = v * rstd
    t = dy * g
    return rstd * (t - vhat * jnp.mean(t * vhat, axis=-1, keepdims=True)), dy * vhat


def _shift_down(z, k, prev):
    r = pltpu.roll(z, k, 0)
    rows = lax.broadcasted_iota(jnp.int32, z.shape, 0)
    for j in range(k):
        r = jnp.where(rows == j, prev[j:j + 1, :], r)
    return r


def _shift_up(z, k, nxt):
    n = z.shape[0]
    r = pltpu.roll(z, n - k, 0)
    rows = lax.broadcasted_iota(jnp.int32, z.shape, 0)
    for j in range(k):
        r = jnp.where(rows == n - k + j, nxt[j:j + 1, :], r)
    return r


def _inproj_fwd(x, g1, w_in, b_in, tm):
    s = x.shape[0]

    def body(x_ref, g_ref, w_ref, b_ref, xn_ref, qkv_ref, c3_ref, gt_ref):
        xf = x_ref[...]
        xn = (xf * _rstd(xf) * g_ref[...]).astype(BF16)
        xn_ref[...] = xn

        def seg(a, b):
            return jnp.dot(xn, w_ref[:, a:b], preferred_element_type=F32) + b_ref[:, a:b]

        qkv_ref[...] = seg(0, QKV_W).astype(BF16)
        c3_ref[...] = seg(QKV_W, QKV_W + C3_W)
        gt_ref[...] = seg(QKV_W + C3_W, IN_W)

    row = lambda w: pl.BlockSpec((tm, w), lambda i: (i, 0))
    return pl.pallas_call(
        body, name="inproj_fwd", grid=(s // tm,),
        in_specs=[row(D_MODEL), _resident((1, D_MODEL)), _resident((D_MODEL, IN_W)), _resident((1, IN_W))],
        out_specs=[row(D_MODEL), row(QKV_W), row(C3_W), row(GATES_W)],
        out_shape=[jax.ShapeDtypeStruct((s, D_MODEL), BF16), jax.ShapeDtypeStruct((s, QKV_W), BF16),
                   jax.ShapeDtypeStruct((s, C3_W), F32), jax.ShapeDtypeStruct((s, GATES_W), F32)],
        compiler_params=_params("parallel"),
    )(x, g1, w_in, b_in)


def _attn_mask(first_block):
    qi = lax.broadcasted_iota(jnp.int32, (GROUP * BLOCK, 2 * BLOCK), 0) & (BLOCK - 1)
    kj = lax.broadcasted_iota(jnp.int32, (GROUP * BLOCK, 2 * BLOCK), 1)
    band = (kj > qi) & (kj <= qi + BLOCK)
    return band & ((kj >= BLOCK) | jnp.logical_not(first_block))


def _sink_column(sk_ref, h):
    rows = lax.broadcasted_iota(jnp.int32, (GROUP * BLOCK, 1), 0)
    col = jnp.full((GROUP * BLOCK, 1), sk_ref[h * GROUP], F32)
    for g in range(1, GROUP):
        col = jnp.where(rows >= g * BLOCK, sk_ref[h * GROUP + g], col)
    return col


def _stack_heads(t, h):
    return jnp.concatenate(
        [t[:, (h * GROUP + g) * HEAD_DIM:(h * GROUP + g + 1) * HEAD_DIM] for g in range(GROUP)], axis=0)


def _unstack_heads(per_kv):
    return jnp.concatenate(
        [t[g * BLOCK:(g + 1) * BLOCK] for t in per_kv for g in range(GROUP)], axis=1)


def _attn_specs(nb):
    cur = lambda i: jnp.minimum(i, nb - 1)
    prev = lambda i: jnp.maximum(jnp.minimum(i, nb - 1) - 1, 0)
    q = pl.BlockSpec((BLOCK, ATTN_W), lambda i: (cur(i), 0))
    kp = pl.BlockSpec((BLOCK, KV_W), lambda i: (prev(i), ATTN_W // KV_W))
    kc = pl.BlockSpec((BLOCK, KV_W), lambda i: (cur(i), ATTN_W // KV_W))
    vp = pl.BlockSpec((BLOCK, KV_W), lambda i: (prev(i), ATTN_W // KV_W + 1))
    vc = pl.BlockSpec((BLOCK, KV_W), lambda i: (cur(i), ATTN_W // KV_W + 1))
    return q, kp, kc, vp, vc


def _attn_fwd(qkv, sinks):
    s = qkv.shape[0]
    nb = s // BLOCK

    def body(sk_ref, q_ref, kp_ref, kc_ref, vp_ref, vc_ref, o_ref):
        mask = _attn_mask(pl.program_id(0) == 0)
        q, kp, kc, vp, vc = q_ref[...], kp_ref[...], kc_ref[...], vp_ref[...], vc_ref[...]
        outs = []
        for h in range(N_KV_HEADS):
            hs = slice(h * HEAD_DIM, (h + 1) * HEAD_DIM)
            k2 = jnp.concatenate([kp[:, hs], kc[:, hs]], axis=0)
            v2 = jnp.concatenate([vp[:, hs], vc[:, hs]], axis=0)
            sc = lax.dot_general(_stack_heads(q, h), k2, NT, preferred_element_type=F32) * ATTN_SCALE
            sc = jnp.where(mask, sc, NEG)
            sink = _sink_column(sk_ref, h)
            m = jnp.maximum(jnp.max(sc, axis=1, keepdims=True), sink)
            p = jnp.exp(sc - m)
            den = jnp.sum(p, axis=1, keepdims=True) + jnp.exp(sink - m)
            outs.append(jnp.dot(p.astype(BF16), v2, preferred_element_type=F32) / den)
        o_ref[...] = _unstack_heads(outs).astype(BF16)

    return pl.pallas_call(
        body, name="attn_fwd", grid=(nb,),
        in_specs=[pl.BlockSpec(memory_space=pltpu.SMEM), *_attn_specs(nb)],
        out_specs=pl.BlockSpec((BLOCK, ATTN_W), lambda i: (i, 0)),
        out_shape=jax.ShapeDtypeStruct((s, ATTN_W), BF16),
        compiler_params=_params("parallel"),
    )(sinks, qkv, qkv, qkv, qkv, qkv)


def _mix_fwd(x, attn, c3, gates, conv_w, w_ab, w_cb, w_out, g2, tm):
    s = x.shape[0]

    def body(x_ref, at_ref, c3_ref, gt_ref, cw_ref, wab_ref, wcb_ref, wo_ref, g_ref,
             conv_ref, a_ref, cv_ref, mg_ref, h1_ref, hn_ref, carry_ref):
        @pl.when(pl.program_id(0) == 0)
        def _():
            carry_ref[...] = jnp.zeros_like(carry_ref)

        c3v = c3_ref[...]
        cb, cc, cx = c3v[:, :CONV_W], c3v[:, CONV_W:2 * CONV_W], c3v[:, 2 * CONV_W:]
        z = cc * cx
        prev = carry_ref[...]
        cw = cw_ref[...]
        cz = cw[2:3] * z + cw[1:2] * _shift_down(z, 1, prev[7:8]) + cw[0:1] * _shift_down(z, 2, prev[6:8])
        carry_ref[...] = z[tm - 8:tm]
        conv = (cb * cz).astype(BF16)
        conv_ref[...] = conv
        a = jnp.dot(at_ref[...], wab_ref[...], preferred_element_type=F32)
        cv = jnp.dot(conv, wcb_ref[...], preferred_element_type=F32)
        a_ref[...] = a.astype(BF16)
        cv_ref[...] = cv.astype(BF16)
        gt = gt_ref[...]
        merged = (_sigmoid(gt[:, :D_MODEL]) * a + _sigmoid(gt[:, D_MODEL:]) * cv).astype(BF16)
        mg_ref[...] = merged
        h1 = x_ref[...] + jnp.dot(merged, wo_ref[...], preferred_element_type=F32)
        h1_ref[...] = h1
        hn_ref[...] = (h1 * _rstd(h1) * g_ref[...]).astype(BF16)

    row = lambda w: pl.BlockSpec((tm, w), lambda i: (i, 0))
    return pl.pallas_call(
        body, name="mix_fwd", grid=(s // tm,),
        in_specs=[row(D_MODEL), row(ATTN_W), row(C3_W), row(GATES_W), _resident((3, CONV_W)),
                  _resident((ATTN_W, D_MODEL)), _resident((CONV_W, D_MODEL)), _resident((D_MODEL, D_MODEL)),
                  _resident((1, D_MODEL))],
        out_specs=[row(CONV_W), row(D_MODEL), row(D_MODEL), row(D_MODEL), row(D_MODEL), row(D_MODEL)],
        out_shape=[jax.ShapeDtypeStruct((s, CONV_W), BF16), jax.ShapeDtypeStruct((s, D_MODEL), BF16),
                   jax.ShapeDtypeStruct((s, D_MODEL), BF16), jax.ShapeDtypeStruct((s, D_MODEL), BF16),
                   jax.ShapeDtypeStruct((s, D_MODEL), F32), jax.ShapeDtypeStruct((s, D_MODEL), BF16)],
        scratch_shapes=[pltpu.VMEM((8, CONV_W), F32)],
        compiler_params=_params("arbitrary"),
    )(x, attn, c3, gates, conv_w, w_ab, w_cb, w_out, g2)


def _ffn_fwd_loss(hn, h1, w_up, ffn_cw, w_down, g3, target, tm):
    s = hn.shape[0]

    def body(hn_ref, h1_ref, wu_ref, cw_ref, wd_ref, g_ref, t_ref,
             u_ref, act_ref, dh2_ref, loss_ref, gfn_ref, carry_ref):
        @pl.when(pl.program_id(0) == 0)
        def _():
            carry_ref[...] = jnp.zeros_like(carry_ref)
            loss_ref[...] = jnp.zeros_like(loss_ref)
            gfn_ref[...] = jnp.zeros_like(gfn_ref)

        u = jnp.dot(hn_ref[...], wu_ref[...], preferred_element_type=F32)
        u_ref[...] = u.astype(BF16)
        prev = carry_ref[...]
        cw = cw_ref[...]
        up = cw[2:3] * u + cw[1:2] * _shift_down(u, 1, prev[7:8]) + cw[0:1] * _shift_down(u, 2, prev[6:8])
        carry_ref[...] = u[tm - 8:tm]
        gate, val = up[:, :D_FF], up[:, D_FF:]
        act = (gate * _sigmoid(gate) * val).astype(BF16)
        act_ref[...] = act
        h2 = h1_ref[...] + jnp.dot(act, wd_ref[...], preferred_element_type=F32)
        rstd = _rstd(h2)
        g = g_ref[...]
        err = h2 * rstd * g - t_ref[...]
        loss_ref[...] += jnp.sum(err * err) * (0.5 / D_MODEL)
        dh2, dg = _rms_bwd(err * (1.0 / D_MODEL), h2, rstd, g)
        dh2_ref[...] = dh2
        gfn_ref[...] += jnp.sum(dg, axis=0, keepdims=True)

    row = lambda w: pl.BlockSpec((tm, w), lambda i: (i, 0))
    acc = lambda w: pl.BlockSpec((1, w), lambda i: (0, 0))
    return pl.pallas_call(
        body, name="ffn_fwd_loss", grid=(s // tm,),
        in_specs=[row(D_MODEL), row(D_MODEL), _resident((D_MODEL, FF2)), _resident((3, FF2)),
                  _resident((D_FF, D_MODEL)), _resident((1, D_MODEL)), row(D_MODEL)],
        out_specs=[row(FF2), row(D_FF), row(D_MODEL), acc(128), acc(D_MODEL)],
        out_shape=[jax.ShapeDtypeStruct((s, FF2), BF16), jax.ShapeDtypeStruct((s, D_FF), BF16),
                   jax.ShapeDtypeStruct((s, D_MODEL), F32), jax.ShapeDtypeStruct((1, 128), F32),
                   jax.ShapeDtypeStruct((1, D_MODEL), F32)],
        scratch_shapes=[pltpu.VMEM((8, FF2), F32)],
        compiler_params=_params("arbitrary"),
    )(hn, h1, w_up, ffn_cw, w_down, g3, target)


def _ffn_bwd(dh2, u, h1, w_up, ffn_cw, w_down, g2, tm):
    s = dh2.shape[0]
    nt = s // tm
    halo = 16

    def body(dh2_ref, u_ref, uh_ref, h1_ref, wu_ref, cw_ref, wd_ref, g_ref,
             du_ref, dh1_ref, gcw_ref, gg_ref, carry_ref):
        i = pl.program_id(0)

        @pl.when(i == 0)
        def _():
            carry_ref[...] = jnp.zeros_like(carry_ref)
            gcw_ref[...] = jnp.zeros_like(gcw_ref)
            gg_ref[...] = jnp.zeros_like(gg_ref)

        dh2v = dh2_ref[...]
        dact = lax.dot_general(dh2v.astype(BF16), wd_ref[...], NT, preferred_element_type=F32)
        u = u_ref[...].astype(F32)
        uh = uh_ref[...].astype(F32) * (i < nt - 1).astype(F32)
        u1 = _shift_down(u, 1, uh[halo - 1:halo])
        u2 = _shift_down(u, 2, uh[halo - 2:halo])
        cw = cw_ref[...]
        up = cw[2:3] * u + cw[1:2] * u1 + cw[0:1] * u2
        gate, val = up[:, :D_FF], up[:, D_FF:]
        sg = _sigmoid(gate)
        dval = dact * (gate * sg)
        dgate = dact * val * (sg * (1.0 + gate * (1.0 - sg)))
        dup = jnp.concatenate([dgate, dval], axis=1)
        gcw_ref[2:3, :] += jnp.sum(dup * u, axis=0, keepdims=True)
        gcw_ref[1:2, :] += jnp.sum(dup * u1, axis=0, keepdims=True)
        gcw_ref[0:1, :] += jnp.sum(dup * u2, axis=0, keepdims=True)
        nxt = carry_ref[...]
        du = (cw[2:3] * dup + cw[1:2] * _shift_up(dup, 1, nxt[0:1]) + cw[0:1] * _shift_up(dup, 2, nxt[0:2]))
        carry_ref[...] = dup[0:8]
        du = du.astype(BF16)
        du_ref[...] = du
        dhn = lax.dot_general(du, wu_ref[...], NT, preferred_element_type=F32)
        h1v = h1_ref[...]
        dh1, dg = _rms_bwd(dhn, h1v, _rstd(h1v), g_ref[...])
        dh1_ref[...] = dh2v + dh1
        gg_ref[...] += jnp.sum(dg, axis=0, keepdims=True)

    row = lambda w: pl.BlockSpec((tm, w), lambda i: (nt - 1 - i, 0))
    return pl.pallas_call(
        body, name="ffn_bwd", grid=(nt,),
        in_specs=[row(D_MODEL), row(FF2),
                  pl.BlockSpec((halo, FF2), lambda i: (jnp.maximum((nt - 1 - i) * (tm // halo) - 1, 0), 0)),
                  row(D_MODEL), _resident((D_MODEL, FF2)), _resident((3, FF2)), _resident((D_FF, D_MODEL)),
                  _resident((1, D_MODEL))],
        out_specs=[row(FF2), row(D_MODEL), pl.BlockSpec((3, FF2), lambda i: (0, 0)),
                   pl.BlockSpec((1, D_MODEL), lambda i: (0, 0))],
        out_shape=[jax.ShapeDtypeStruct((s, FF2), BF16), jax.ShapeDtypeStruct((s, D_MODEL), F32),
                   jax.ShapeDtypeStruct((3, FF2), F32), jax.ShapeDtypeStruct((1, D_MODEL), F32)],
        scratch_shapes=[pltpu.VMEM((8, FF2), F32)],
        compiler_params=_params("arbitrary"),
    )(dh2, u, u, h1, w_up, ffn_cw, w_down, g2)


def _mix_bwd(dh1, gates, a, cv, c3, conv_w, w_ab, w_cb, w_out, tm):
    s = dh1.shape[0]
    nt = s // tm
    halo = 8

    def body(dh1_ref, gt_ref, a_ref, cv_ref, c3_ref, ch_ref, cw_ref, wab_ref, wcb_ref, wo_ref,
             dat_ref, da_ref, dcv_ref, dc3_ref, dgt_ref, gcw_ref, carry_ref):
        i = pl.program_id(0)

        @pl.when(i == 0)
        def _():
            carry_ref[...] = jnp.zeros_like(carry_ref)
            gcw_ref[...] = jnp.zeros_like(gcw_ref)

        dm = lax.dot_general(dh1_ref[...].astype(BF16), wo_ref[...], NT, preferred_element_type=F32)
        gt = gt_ref[...]
        sa, sc = _sigmoid(gt[:, :D_MODEL]), _sigmoid(gt[:, D_MODEL:])
        da = (dm * sa).astype(BF16)
        dcv = (dm * sc).astype(BF16)
        da_ref[...] = da
        dcv_ref[...] = dcv
        dgt_ref[...] = jnp.concatenate(
            [dm * a_ref[...].astype(F32) * (sa * (1.0 - sa)), dm * cv_ref[...].astype(F32) * (sc * (1.0 - sc))],
            axis=1).astype(BF16)
        dat_ref[...] = lax.dot_general(da, wab_ref[...], NT, preferred_element_type=F32).astype(BF16)
        dconv = lax.dot_general(dcv, wcb_ref[...], NT, preferred_element_type=F32)
        c3v = c3_ref[...]
        cb, cc, cx = c3v[:, :CONV_W], c3v[:, CONV_W:2 * CONV_W], c3v[:, 2 * CONV_W:]
        z = cc * cx
        chv = ch_ref[...] * (i < nt - 1).astype(F32)
        zh = chv[:, CONV_W:2 * CONV_W] * chv[:, 2 * CONV_W:]
        z1 = _shift_down(z, 1, zh[halo - 1:halo])
        z2 = _shift_down(z, 2, zh[halo - 2:halo])
        cw = cw_ref[...]
        cz = cw[2:3] * z + cw[1:2] * z1 + cw[0:1] * z2
        dcz = dconv * cb
        gcw_ref[2:3, :] += jnp.sum(dcz * z, axis=0, keepdims=True)
        gcw_ref[1:2, :] += jnp.sum(dcz * z1, axis=0, keepdims=True)
        gcw_ref[0:1, :] += jnp.sum(dcz * z2, axis=0, keepdims=True)
        nxt = carry_ref[...]
        dz = cw[2:3] * dcz + cw[1:2] * _shift_up(dcz, 1, nxt[0:1]) + cw[0:1] * _shift_up(dcz, 2, nxt[0:2])
        carry_ref[...] = dcz[0:8]
        dc3_ref[...] = jnp.concatenate([dconv * cz, dz * cx, dz * cc], axis=1).astype(BF16)

    row = lambda w: pl.BlockSpec((tm, w), lambda i: (nt - 1 - i, 0))
    return pl.pallas_call(
        body, name="mix_bwd", grid=(nt,),
        in_specs=[row(D_MODEL), row(GATES_W), row(D_MODEL), row(D_MODEL), row(C3_W),
                  pl.BlockSpec((halo, C3_W), lambda i: (jnp.maximum((nt - 1 - i) * (tm // halo) - 1, 0), 0)),
                  _resident((3, CONV_W)), _resident((ATTN_W, D_MODEL)), _resident((CONV_W, D_MODEL)),
                  _resident((D_MODEL, D_MODEL))],
        out_specs=[row(ATTN_W), row(D_MODEL), row(D_MODEL), row(C3_W), row(GATES_W),
                   pl.BlockSpec((3, CONV_W), lambda i: (0, 0))],
        out_shape=[jax.ShapeDtypeStruct((s, ATTN_W), BF16), jax.ShapeDtypeStruct((s, D_MODEL), BF16),
                   jax.ShapeDtypeStruct((s, D_MODEL), BF16), jax.ShapeDtypeStruct((s, C3_W), BF16),
                   jax.ShapeDtypeStruct((s, GATES_W), BF16), jax.ShapeDtypeStruct((3, CONV_W), F32)],
        scratch_shapes=[pltpu.VMEM((8, CONV_W), F32)],
        compiler_params=_params("arbitrary"),
    )(dh1, gates, a, cv, c3, c3, conv_w, w_ab, w_cb, w_out)


def _attn_bwd(qkv, sinks, o, do):
    s = qkv.shape[0]
    nb = s // BLOCK

    def body(sk_ref, q_ref, kp_ref, kc_ref, vp_ref, vc_ref, o_ref, do_ref,
             dq_ref, dk_ref, dv_ref, dsk_ref, ck_ref, cvv_ref):
        i = pl.program_id(0)

        @pl.when(i == 0)
        def _():
            ck_ref[...] = jnp.zeros_like(ck_ref)
            cvv_ref[...] = jnp.zeros_like(cvv_ref)
            dsk_ref[...] = jnp.zeros_like(dsk_ref)

        @pl.when(i < nb)
        def _():
            mask = _attn_mask(i == 0)
            q, kp, kc, vp, vc = q_ref[...], kp_ref[...], kc_ref[...], vp_ref[...], vc_ref[...]
            ov, dov = o_ref[...], do_ref[...]
            dqs, dks, dvs = [], [], []
            for h in range(N_KV_HEADS):
                hs = slice(h * HEAD_DIM, (h + 1) * HEAD_DIM)
                k2 = jnp.concatenate([kp[:, hs], kc[:, hs]], axis=0)
                v2 = jnp.concatenate([vp[:, hs], vc[:, hs]], axis=0)
                qg, og, dog = _stack_heads(q, h), _stack_heads(ov, h), _stack_heads(dov, h)
                sc = lax.dot_general(qg, k2, NT, preferred_element_type=F32) * ATTN_SCALE
                sc = jnp.where(mask, sc, NEG)
                sink = _sink_column(sk_ref, h)
                m = jnp.maximum(jnp.max(sc, axis=1, keepdims=True), sink)
                p = jnp.exp(sc - m)
                psink = jnp.exp(sink - m)
                inv = 1.0 / (jnp.sum(p, axis=1, keepdims=True) + psink)
                p = p * inv
                delta = jnp.sum(dog.astype(F32) * og.astype(F32), axis=1, keepdims=True)
                dp = lax.dot_general(dog, v2, NT, preferred_element_type=F32)
                ds = (p * (dp - delta)).astype(BF16)
                dqs.append(jnp.dot(ds, k2, preferred_element_type=F32) * ATTN_SCALE)
                dks.append(lax.dot_general(ds, qg, TN, preferred_element_type=F32) * ATTN_SCALE)
                dvs.append(lax.dot_general(p.astype(BF16), dog, TN, preferred_element_type=F32))
                dsink = -(psink * inv * delta)
                for g in range(GROUP):
                    r = h * GROUP + g
                    dsk_ref[r:r + 1, :] += jnp.sum(dsink[g * BLOCK:(g + 1) * BLOCK])
            dq_ref[...] = _unstack_heads(dqs).astype(BF16)
            dk2 = jnp.concatenate(dks, axis=1)
            dv2 = jnp.concatenate(dvs, axis=1)
            dk_ref[...] = (ck_ref[...] + dk2[:BLOCK]).astype(BF16)
            dv_ref[...] = (cvv_ref[...] + dv2[:BLOCK]).astype(BF16)
            ck_ref[...] = dk2[BLOCK:]
            cvv_ref[...] = dv2[BLOCK:]

        @pl.when(i == nb)
        def _():
            dk_ref[...] = ck_ref[...].astype(BF16)
            dv_ref[...] = cvv_ref[...].astype(BF16)

    cur = lambda i: jnp.minimum(i, nb - 1)
    done = lambda i: jnp.maximum(i - 1, 0)
    return pl.pallas_call(
        body, name="attn_bwd", grid=(nb + 1,),
        in_specs=[pl.BlockSpec(memory_space=pltpu.SMEM), *_attn_specs(nb),
                  pl.BlockSpec((BLOCK, ATTN_W), lambda i: (cur(i), 0)),
                  pl.BlockSpec((BLOCK, ATTN_W), lambda i: (cur(i), 0))],
        out_specs=[pl.BlockSpec((BLOCK, ATTN_W), lambda i: (cur(i), 0)),
                   pl.BlockSpec((BLOCK, KV_W), lambda i: (done(i), 0)),
                   pl.BlockSpec((BLOCK, KV_W), lambda i: (done(i), 0)),
                   pl.BlockSpec((N_HEADS, 128), lambda i: (0, 0))],
        out_shape=[jax.ShapeDtypeStruct((s, ATTN_W), BF16), jax.ShapeDtypeStruct((s, KV_W), BF16),
                   jax.ShapeDtypeStruct((s, KV_W), BF16), jax.ShapeDtypeStruct((N_HEADS, 128), F32)],
        scratch_shapes=[pltpu.VMEM((BLOCK, KV_W), F32), pltpu.VMEM((BLOCK, KV_W), F32)],
        compiler_params=_params("arbitrary"),
    )(sinks, qkv, qkv, qkv, qkv, qkv, o, do)


def _inproj_bwd(dq, dk, dv, dc3, dgt, w_in, x, dh1, g1, tm):
    s = x.shape[0]

    def body(dq_ref, dk_ref, dv_ref, dc3_ref, dgt_ref, w_ref, x_ref, dh1_ref, g_ref,
             dx_ref, dp_ref, gb_ref, gg_ref):
        @pl.when(pl.program_id(0) == 0)
        def _():
            gb_ref[...] = jnp.zeros_like(gb_ref)
            gg_ref[...] = jnp.zeros_like(gg_ref)

        dp = jnp.concatenate([dq_ref[...], dk_ref[...], dv_ref[...], dc3_ref[...], dgt_ref[...]], axis=1)
        dp_ref[...] = dp
        gb_ref[...] += jnp.sum(dp.astype(F32), axis=0, keepdims=True)
        dxn = lax.dot_general(dp, w_ref[...], NT, preferred_element_type=F32)
        xf = x_ref[...]
        dx, dg = _rms_bwd(dxn, xf, _rstd(xf), g_ref[...])
        dx_ref[...] = dh1_ref[...] + dx
        gg_ref[...] += jnp.sum(dg, axis=0, keepdims=True)

    row = lambda w: pl.BlockSpec((tm, w), lambda i: (i, 0))
    acc = lambda w: pl.BlockSpec((1, w), lambda i: (0, 0))
    return pl.pallas_call(
        body, name="inproj_bwd", grid=(s // tm,),
        in_specs=[row(ATTN_W), row(KV_W), row(KV_W), row(C3_W), row(GATES_W), _resident((D_MODEL, IN_W)),
                  row(D_MODEL), row(D_MODEL), _resident((1, D_MODEL))],
        out_specs=[row(D_MODEL), row(IN_W), acc(IN_W), acc(D_MODEL)],
        out_shape=[jax.ShapeDtypeStruct((s, D_MODEL), F32), jax.ShapeDtypeStruct((s, IN_W), BF16),
                   jax.ShapeDtypeStruct((1, IN_W), F32), jax.ShapeDtypeStruct((1, D_MODEL), F32)],
        compiler_params=_params("arbitrary"),
    )(dq, dk, dv, dc3, dgt, w_in, x, dh1, g1)


def _wgrad(a, b, bm, bn, bk, name):
    s, m = a.shape
    n = b.shape[1]
    nk = s // bk

    def body(a_ref, b_ref, o_ref, acc_ref):
        k = pl.program_id(2)

        @pl.when(k == 0)
        def _():
            acc_ref[...] = jnp.zeros_like(acc_ref)

        acc_ref[...] += lax.dot_general(a_ref[...].astype(BF16), b_ref[...].astype(BF16), TN,
                                        preferred_element_type=F32)

        @pl.when(k == nk - 1)
        def _():
            o_ref[...] = acc_ref[...].astype(BF16)

    return pl.pallas_call(
        body, name=name, grid=(m // bm, n // bn, nk),
        in_specs=[pl.BlockSpec((bk, bm), lambda i, j, k: (k, i)), pl.BlockSpec((bk, bn), lambda i, j, k: (k, j))],
        out_specs=pl.BlockSpec((bm, bn), lambda i, j, k: (i, j)),
        out_shape=jax.ShapeDtypeStruct((m, n), BF16),
        scratch_shapes=[pltpu.VMEM((bm, bn), F32)],
        compiler_params=_params("parallel", "parallel", "arbitrary"),
    )(a, b)


def _wgrad_in(xn, dproj, bk):
    s = xn.shape[0]
    nk = s // bk

    def body(a_ref, b_ref, o_ref, acc_ref):
        k = pl.program_id(0)

        @pl.when(k == 0)
        def _():
            acc_ref[...] = jnp.zeros_like(acc_ref)

        av, bv = a_ref[...], b_ref[...]
        for j in range(N_CHIPS):
            acc_ref[j] += lax.dot_general(av, bv[:, j * IN_SHARD:(j + 1) * IN_SHARD], TN,
                                          preferred_element_type=F32)

        @pl.when(k == nk - 1)
        def _():
            o_ref[...] = acc_ref[...].astype(BF16)

    return pl.pallas_call(
        body, name="wgrad_in", grid=(nk,),
        in_specs=[pl.BlockSpec((bk, D_MODEL), lambda k: (k, 0)), pl.BlockSpec((bk, IN_W), lambda k: (k, 0))],
        out_specs=_resident((N_CHIPS, D_MODEL, IN_SHARD)),
        out_shape=jax.ShapeDtypeStruct((N_CHIPS, D_MODEL, IN_SHARD), BF16),
        scratch_shapes=[pltpu.VMEM((N_CHIPS, D_MODEL, IN_SHARD), F32)],
        compiler_params=_params("arbitrary"),
    )(xn, dproj)


def _local_step(x, target, g1, b_in, sinks, conv_w, g2, ffn_cw, g3, w_in, w_ab, w_cb, w_out, w_up, w_down):
    s = x.shape[0]
    tm = min(256, s)
    tm2 = min(512, s)
    bk = min(1024, s)
    xn, qkv, c3, gates = _inproj_fwd(x, g1, w_in, b_in, tm2)
    attn = _attn_fwd(qkv, sinks)
    conv, a, cv, merged, h1, hn = _mix_fwd(x, attn, c3, gates, conv_w, w_ab, w_cb, w_out, g2, tm)
    u, act, dh2, loss, g_fn = _ffn_fwd_loss(hn, h1, w_up, ffn_cw, w_down, g3, target, tm)
    du, dh1, g_fcw, g_g2 = _ffn_bwd(dh2, u, h1, w_up, ffn_cw, w_down, g2, tm)
    gw_down = _wgrad(act, dh2, D_FF // 2, D_MODEL, bk, "wgrad_down")
    gw_up = _wgrad(hn, du, D_MODEL, FF2 // 4, bk, "wgrad_up")
    dattn, da, dcv, dc3, dgt, g_cw = _mix_bwd(dh1, gates, a, cv, c3, conv_w, w_ab, w_cb, w_out, tm)
    gw_out = _wgrad(merged, dh1, D_MODEL, D_MODEL, bk, "wgrad_out")
    gw_ab = _wgrad(attn, da, ATTN_W, D_MODEL, bk, "wgrad_ab")
    gw_cb = _wgrad(conv, dcv, CONV_W, D_MODEL, bk, "wgrad_cb")
    dq, dk, dv, g_sk = _attn_bwd(qkv, sinks, attn, dattn)
    grad_x, dproj, g_b, g_g1 = _inproj_bwd(dq, dk, dv, dc3, dgt, w_in, x, dh1, g1, tm2)
    gw_in = _wgrad_in(xn, dproj, min(512, s))
    small = dict(loss=loss, g_g1=g_g1, g_b=g_b, g_sk=g_sk, g_cw=g_cw, g_g2=g_g2, g_fcw=g_fcw, g_fn=g_fn)
    big = [gw_in, gw_ab, gw_cb, gw_out, gw_up, gw_down]
    return grad_x, small, big


def _row_tile(rows, bytes_per_row):
    best = 16
    for t in range(16, rows + 1, 16):
        if rows % t == 0 and t * bytes_per_row <= 6 * 1024 * 1024:
            best = t
    return best


def _rowwise(fn, ins, out_dtypes, name):
    rows, cols = ins[0].shape[-2:]
    per_row = sum(a.size // rows * a.dtype.itemsize for a in ins) + sum(cols * jnp.dtype(d).itemsize for d in out_dtypes)
    tr = _row_tile(rows, per_row)
    n_in = len(ins)

    def body(*refs):
        outs = fn(*[r[...] for r in refs[:n_in]])
        for o_ref, o in zip(refs[n_in:], outs):
            o_ref[...] = o.astype(o_ref.dtype)

    def spec(a):
        if a.ndim == 3:
            return pl.BlockSpec((a.shape[0], tr, cols), lambda i: (0, i, 0))
        return pl.BlockSpec((tr, cols), lambda i: (i, 0))

    return pl.pallas_call(
        body, name=name, grid=(rows // tr,),
        in_specs=[spec(a) for a in ins],
        out_specs=[pl.BlockSpec((tr, cols), lambda i: (i, 0)) for _ in out_dtypes],
        out_shape=[jax.ShapeDtypeStruct((rows, cols), d) for d in out_dtypes],
        compiler_params=_params("parallel"),
    )(*ins)


def _tiled(fn, name, grid, pos, ins, outs):
    n_in = len(ins)

    def body(pos_ref, *refs):
        res = fn(*[r[...] for r in refs[:n_in]])
        for o_ref, o in zip(refs[n_in:], res):
            o_ref[...] = o.astype(o_ref.dtype)

    return pl.pallas_call(
        body, name=name,
        grid_spec=pltpu.PrefetchScalarGridSpec(
            num_scalar_prefetch=1, grid=grid,
            in_specs=[pl.BlockSpec(bs, im) for _, bs, im in ins],
            out_specs=[pl.BlockSpec(bs, im) for _, _, bs, im in outs]),
        out_shape=[jax.ShapeDtypeStruct(s, d) for s, d, _, _ in outs],
        compiler_params=_params("parallel"),
    )(pos, *[a for a, _, _ in ins])


def _adamw(w, g, m, v):
    m = ADAM_B1 * m + (1.0 - ADAM_B1) * g
    v = ADAM_B2 * v + (1.0 - ADAM_B2) * (g * g)
    m_hat = m / (1.0 - ADAM_B1 ** ADAM_STEP)
    v_hat = v / (1.0 - ADAM_B2 ** ADAM_STEP)
    return -ADAM_LR * (m_hat / (jnp.sqrt(v_hat) + ADAM_EPS) + ADAM_WD * w), m, v


def _adamw_small(params):
    n = len(params)

    def body(*refs):
        for k in range(n):
            w, g, m, v = (r[...] for r in refs[4 * k:4 * k + 4])
            for o_ref, o in zip(refs[4 * n + 3 * k:4 * n + 3 * k + 3], _adamw(w, g, m, v)):
                o_ref[...] = o

    flat = [a for p in params for a in p]
    return pl.pallas_call(
        body, name="adamw_small",
        out_shape=[jax.ShapeDtypeStruct(p[0].shape, F32) for p in params for _ in range(3)],
    )(*flat)


def _repack_in(wg):
    tr = 256

    def body(w_ref, o_ref):
        for j in range(N_CHIPS):
            o_ref[:, j * IN_SHARD:(j + 1) * IN_SHARD] = w_ref[j]

    return pl.pallas_call(
        body, name="repack_in", grid=(D_MODEL // tr,),
        in_specs=[pl.BlockSpec((N_CHIPS, tr, IN_SHARD), lambda i: (0, i, 0))],
        out_specs=pl.BlockSpec((tr, IN_W), lambda i: (i, 0)),
        out_shape=jax.ShapeDtypeStruct((D_MODEL, IN_W), BF16),
        compiler_params=_params("parallel"),
    )(wg)


class _Layout:
    def __init__(self, rows, cols, stacked):
        self.rows, self.cols, self.stacked = rows, cols, stacked

    def whole(self, rows=None):
        r = self.rows if rows is None else rows
        return (N_CHIPS, r, self.cols) if self.stacked else (r, N_CHIPS * self.cols)

    def half_rows(self, h):
        nr = self.rows // 2
        return pl.ds(pl.multiple_of(h * nr, 16), nr)

    def block(self, ref, p, rows=slice(None)):
        if self.stacked:
            return ref.at[p, rows, :]
        return ref.at[rows, pl.ds(pl.multiple_of(p * self.cols, 128), self.cols)]

    def all_chips(self, ref, rows):
        return ref.at[:, rows, :] if self.stacked else ref.at[rows, :]


BIG = (
    _Layout(D_MODEL, IN_SHARD, True),
    _Layout(ATTN_W, D_MODEL // N_CHIPS, False),
    _Layout(CONV_W, D_MODEL // N_CHIPS, False),
    _Layout(D_MODEL // N_CHIPS, D_MODEL, True),
    _Layout(D_MODEL, FF2 // N_CHIPS, False),
    _Layout(D_FF // N_CHIPS, D_MODEL, True),
)
N_BIG = len(BIG)
_ANY = pl.BlockSpec(memory_space=pl.ANY)


def _position():
    x, y, c = lax.axis_index("x"), lax.axis_index("y"), lax.axis_index("c")
    return x, y, c, 2 * x + y


def _core_of_chip(p, c):
    return (p >> 1, p & 1, c)


def _place_cast(shard, lay, pos, name):
    rows, cols = shard.shape
    tr = _row_tile(rows, cols * 6)
    if lay.stacked:
        out = (lay.whole(), BF16, (None, tr, cols), lambda i, pos: (pos[0], i, 0))
    else:
        out = (lay.whole(), BF16, (tr, cols), lambda i, pos: (i, pos[0]))
    return _tiled(lambda a: (a,), name, (rows // tr,), pos, [(shard, (tr, cols), lambda i, pos: (i, 0))], [out])[0]


def _gather_weights(wholes):
    def body(*refs):
        dst = refs[N_BIG:2 * N_BIG]
        send1, recv1, send2, recv2 = refs[2 * N_BIG:]
        x, y, c, me = _position()
        sibling = (x, y, 1 - c)

        def first(w, d):
            blk = BIG[w].block(dst[w], me, BIG[w].half_rows(c))
            return pltpu.make_async_remote_copy(
                src_ref=blk, dst_ref=blk,
                send_sem=send1.at[w, d - 1], recv_sem=recv1.at[w, d - 1],
                device_id=_core_of_chip(me ^ d, c), device_id_type=MESH)

        def passed(w, d, h):
            blk = BIG[w].block(dst[w], me ^ d, BIG[w].half_rows(h))
            return pltpu.make_async_remote_copy(
                src_ref=blk, dst_ref=blk, send_sem=send2.at[w, d - 1], recv_sem=recv2.at[w, d - 1],
                device_id=sibling, device_id_type=MESH)

        def arrived(w, d):
            blk = BIG[w].block(dst[w], me ^ d, BIG[w].half_rows(c))
            return pltpu.make_async_remote_copy(
                src_ref=blk, dst_ref=blk, send_sem=send1.at[w, d - 1], recv_sem=recv1.at[w, d - 1],
                device_id=sibling, device_id_type=MESH)

        pairs = [(w, d) for w in range(N_BIG) for d in (1, 2, 3)]
        for w, d in pairs:
            first(w, d).start()
        for w, d in pairs:
            arrived(w, d).wait_recv()
            passed(w, d, c).start()
        for w, d in pairs:
            passed(w, d, 1 - c).wait_recv()
        for w, d in pairs:
            first(w, d).wait_send()
            passed(w, d, c).wait_send()

    sems = pltpu.SemaphoreType.DMA((N_BIG, 3))
    return pl.pallas_call(
        body, name="gather_weights",
        in_specs=[_ANY] * N_BIG, out_specs=[_ANY] * N_BIG,
        out_shape=[jax.ShapeDtypeStruct(lay.whole(), BF16) for lay in BIG],
        input_output_aliases={w: w for w in range(N_BIG)},
        scratch_shapes=[sems, sems, sems, sems],
    )(*wholes)


def _rs_pair(grads):
    def body(*refs):
        src, theirs = refs[:N_BIG], refs[N_BIG:2 * N_BIG]
        send, recv = refs[2 * N_BIG:]
        x, y, c, _ = _position()
        remote = [pltpu.make_async_remote_copy(
            src_ref=BIG[w].all_chips(src[w], BIG[w].half_rows(1 - c)), dst_ref=theirs[w],
            send_sem=send.at[w], recv_sem=recv.at[w], device_id=(x, y, 1 - c), device_id_type=MESH)
            for w in range(N_BIG)]
        for cp in remote:
            cp.start()
        for cp in remote:
            cp.wait()

    sems = pltpu.SemaphoreType.DMA((N_BIG,))
    return pl.pallas_call(
        body, name="rs_pair", in_specs=[_ANY] * N_BIG, out_specs=[_ANY] * N_BIG,
        out_shape=[jax.ShapeDtypeStruct(lay.whole(lay.rows // 2), BF16) for lay in BIG],
        scratch_shapes=[sems, sems],
    )(*grads)


def _pair_sum(grad, theirs, lay, pos, name):
    half = lay.rows // 2
    add = lambda a, b: (a.astype(F32) + b.astype(F32),)
    if lay.stacked:
        tr = _row_tile(half, lay.cols * 6)
        nt = half // tr
        flat = lambda a: a.reshape(-1, lay.cols)
        mine = lambda t, pos: ((t // nt) * (2 * nt) + pos[1] * nt + t % nt, 0)
        grid, blk = (N_CHIPS * nt,), (tr, lay.cols)
        grad, theirs = flat(grad), flat(theirs)
    else:
        tr = _row_tile(half, N_CHIPS * lay.cols * 6)
        nt = half // tr
        mine = lambda t, pos: (pos[1] * nt + t, 0)
        grid, blk = (nt,), (tr, N_CHIPS * lay.cols)
    same = lambda t, pos: (t, 0)
    out = _tiled(add, name, grid, pos, [(grad, blk, mine), (theirs, blk, same)], [(theirs.shape, BF16, blk, same)])[0]
    return out.reshape(lay.whole(half))


def _chip_sum(sums, slots, lay, pos, name):
    half = lay.rows // 2
    tr = _row_tile(half, lay.cols * 12)
    nt = half // tr
    blk3 = (None, tr, lay.cols)
    if lay.stacked:
        own = (sums, blk3, lambda i, pos: (pos[0], i, 0))
    else:
        own = (sums, (tr, lay.cols), lambda i, pos: (i, pos[0]))
    others = [(slots, blk3, functools.partial(lambda d, i, pos: (pos[0] ^ d, i, 0), d)) for d in (1, 2, 3)]

    def add(a, b1, b2, b3):
        return (((a.astype(F32) + b1.astype(F32)) + b2.astype(F32)) + b3.astype(F32),)

    return _tiled(add, name, (nt,), pos, [own] + others,
                  [((lay.rows, lay.cols), F32, (tr, lay.cols), lambda i, pos: (pos[1] * nt + i, 0))])[0]


def _rs_chips(sums):
    def body(*refs):
        src, dst = refs[:N_BIG], refs[N_BIG:2 * N_BIG]
        send, recv = refs[2 * N_BIG:]
        x, y, c, me = _position()

        def to_chip(w, d):
            return pltpu.make_async_remote_copy(
                src_ref=BIG[w].block(src[w], me ^ d), dst_ref=dst[w].at[me],
                send_sem=send.at[w, d - 1], recv_sem=recv.at[w, d - 1],
                device_id=_core_of_chip(me ^ d, c), device_id_type=MESH)

        def from_chip(w, d):
            slot = dst[w].at[me ^ d]
            return pltpu.make_async_remote_copy(
                src_ref=slot, dst_ref=slot, send_sem=send.at[w, d - 1], recv_sem=recv.at[w, d - 1],
                device_id=(x, y, 1 - c), device_id_type=MESH)

        pairs = [(w, d) for w in range(N_BIG) for d in (1, 2, 3)]
        for w, d in pairs:
            to_chip(w, d).start()
        for w, d in pairs:
            from_chip(w, d).wait_recv()
        for w, d in pairs:
            to_chip(w, d).wait_send()

    sems = pltpu.SemaphoreType.DMA((N_BIG, 3))
    return pl.pallas_call(
        body, name="rs_chips", in_specs=[_ANY] * N_BIG, out_specs=[_ANY] * N_BIG,
        out_shape=[jax.ShapeDtypeStruct((N_CHIPS, lay.rows // 2, lay.cols), BF16) for lay in BIG],
        scratch_shapes=[sems, sems],
    )(*sums)


def _share_halves(shards):
    def body(*refs):
        dst = refs[N_BIG:2 * N_BIG]
        send, recv = refs[2 * N_BIG:]
        x, y, c, _ = _position()
        remote, landed = [], []
        for w in range(N_BIG):
            mine = dst[w].at[BIG[w].half_rows(c), :]
            other = dst[w].at[BIG[w].half_rows(1 - c), :]
            remote.append(pltpu.make_async_remote_copy(
                src_ref=mine, dst_ref=mine, send_sem=send.at[w], recv_sem=recv.at[w],
                device_id=(x, y, 1 - c), device_id_type=MESH))
            landed.append(pltpu.make_async_remote_copy(
                src_ref=other, dst_ref=other, send_sem=send.at[w], recv_sem=recv.at[w],
                device_id=(x, y, 1 - c), device_id_type=MESH))
        for cp in remote:
            cp.start()
        for cp in landed:
            cp.wait_recv()
        for cp in remote:
            cp.wait_send()

    sems = pltpu.SemaphoreType.DMA((N_BIG,))
    return pl.pallas_call(
        body, name="share_halves", in_specs=[_ANY] * N_BIG, out_specs=[_ANY] * N_BIG,
        out_shape=[jax.ShapeDtypeStruct((lay.rows, lay.cols), F32) for lay in BIG],
        input_output_aliases={w: w for w in range(N_BIG)},
        scratch_shapes=[sems, sems],
    )(*shards)


N_DEV = 8


def _exchange_small(v, reduce):
    rows = v.shape[0]

    def body(v_ref, o_ref, *scratch):
        if reduce:
            slots, send, recv = scratch
        else:
            slots, (send, recv) = o_ref, scratch
        x, y, c = lax.axis_index("x"), lax.axis_index("y"), lax.axis_index("c")
        idx = 4 * x + 2 * y + c
        slots[idx] = v_ref[...]

        def to_peer(k):
            return pltpu.make_async_remote_copy(
                src_ref=v_ref, dst_ref=slots.at[idx], send_sem=send.at[k - 1], recv_sem=recv.at[k - 1],
                device_id=(x ^ (k >> 2), y ^ ((k >> 1) & 1), c ^ (k & 1)), device_id_type=MESH)

        def from_peer(k):
            return pltpu.make_async_remote_copy(
                src_ref=v_ref, dst_ref=slots.at[idx ^ k], send_sem=send.at[k - 1], recv_sem=recv.at[k - 1],
                device_id=(x, y, c), device_id_type=MESH)

        for k in range(1, N_DEV):
            to_peer(k).start()
        for k in range(1, N_DEV):
            from_peer(k).wait_recv()
        for k in range(1, N_DEV):
            to_peer(k).wait_send()
        if reduce:
            acc = slots[0]
            for q in range(1, N_DEV):
                acc = acc + slots[q]
            o_ref[...] = acc

    sems = pltpu.SemaphoreType.DMA((N_DEV - 1,))
    stacked = jax.ShapeDtypeStruct((N_DEV, rows, 128), F32)
    return pl.pallas_call(
        body, name="allreduce_small" if reduce else "allgather_small",
        out_shape=jax.ShapeDtypeStruct((rows, 128), F32) if reduce else stacked,
        scratch_shapes=([pltpu.VMEM((N_DEV, rows, 128), F32)] if reduce else []) + [sems, sems],
    )(v)


def _pack_rows(parts):
    padded = [jnp.pad(a, ((0, -a.shape[0] % 8), (0, 0))) for a in parts]
    starts = [sum(p.shape[0] for p in padded[:k]) for k in range(len(padded))]
    return jnp.concatenate(padded, axis=0), starts


def kernel(x, mix_norm, w_in, b_in, sinks, conv_w, w_attn_branch, w_conv_branch, w_out, ffn_norm, w_up, ffn_conv_w, w_down, final_norm, loss_target, m_mix_norm, m_w_in, m_b_in, m_sinks, m_conv_w, m_w_attn_branch, m_w_conv_branch, m_w_out, m_ffn_norm, m_w_up, m_ffn_conv_w, m_w_down, m_final_norm, v_mix_norm, v_w_in, v_b_in, v_sinks, v_conv_w, v_w_attn_branch, v_w_conv_branch, v_w_out, v_ffn_norm, v_w_up, v_ffn_conv_w, v_w_down, v_final_norm):
    me = 2 * lax.axis_index("x") + lax.axis_index("y")
    big_w = [w_in[0], w_attn_branch[0], w_conv_branch[0], w_out[0], w_up[0], w_down[0]]
    big_m = [m_w_in[0], m_w_attn_branch[0], m_w_conv_branch[0], m_w_out[0], m_w_up[0], m_w_down[0]]
    big_v = [v_w_in[0], v_w_attn_branch[0], v_w_conv_branch[0], v_w_out[0], v_w_up[0], v_w_down[0]]
    names = ("w_in", "w_ab", "w_cb", "w_out", "w_up", "w_down")

    pos = jnp.stack([me, lax.axis_index("c")]).astype(jnp.int32)

    placed = [_place_cast(w, lay, pos, "cast_" + n) for w, lay, n in zip(big_w, BIG, names)]
    g_in, g_ab, g_cb, g_out, g_up, g_down = _gather_weights(placed)
    taps, (_, t0) = _pack_rows([conv_w[0], ffn_conv_w[0].reshape(3 * (FF2 // N_CHIPS // 128), 128)])
    taps = _exchange_small(taps, reduce=False)[0::2]
    conv_full = taps[:, 0:3].transpose(1, 0, 2).reshape(3, CONV_W)
    ffn_cw_full = taps[:, t0:t0 + 33].reshape(N_CHIPS, 3, FF2 // N_CHIPS).transpose(1, 0, 2).reshape(3, FF2)

    grad_x, small, big = _local_step(
        x[0], loss_target[0], mix_norm, b_in, sinks[0], conv_full, ffn_norm, ffn_cw_full, final_norm[None, :],
        _repack_in(g_in), g_ab, g_cb, g_out.reshape(D_MODEL, D_MODEL), g_up, g_down.reshape(D_FF, D_MODEL))

    big[3] = big[3].reshape(BIG[3].whole())
    big[5] = big[5].reshape(BIG[5].whole())
    theirs = _rs_pair(big)
    sums = [_pair_sum(g, t, lay, pos, "pair_sum_" + n) for g, t, lay, n in zip(big, theirs, BIG, names)]
    slots = _rs_chips(sums)
    halves = [_chip_sum(s, r, lay, pos, "chip_sum_" + n) for s, r, lay, n in zip(sums, slots, BIG, names)]
    big_g = _share_halves(halves)
    big_new = [_rowwise(_adamw, [w, g, m, v], [F32, F32, F32], "adamw_" + n)
               for w, g, m, v, n in zip(big_w, big_g, big_m, big_v, names)]

    parts = [small["loss"], small["g_g1"], small["g_b"], jnp.pad(small["g_sk"][:, 0], (0, 120))[None, :],
             small["g_cw"], small["g_g2"], small["g_fcw"], small["g_fn"]]
    packed, at = _pack_rows([a.reshape(-1, 128) for a in parts])
    total = _exchange_small(packed, reduce=True)
    part = lambda k: total[at[k]:at[k] + parts[k].size // 128].reshape(parts[k].shape)
    loss = total[0, 0]
    g_mix, g_b, g_g2, g_fn = part(1), part(2), part(5), part(7)
    g_sk = part(3)[:, 0:N_HEADS]
    g_cw = lax.dynamic_slice(part(4), (0, me * 128), (3, 128))
    g_fcw = lax.dynamic_slice(part(6), (0, me * (FF2 // N_CHIPS)), (3, FF2 // N_CHIPS))
    small_p = [
        (mix_norm, g_mix, m_mix_norm, v_mix_norm), (b_in, g_b, m_b_in, v_b_in), (sinks, g_sk, m_sinks, v_sinks),
        (conv_w[0], g_cw, m_conv_w[0], v_conv_w[0]), (ffn_norm, g_g2, m_ffn_norm, v_ffn_norm),
        (ffn_conv_w[0], g_fcw, m_ffn_conv_w[0], v_ffn_conv_w[0]),
        (final_norm[None, :], g_fn, m_final_norm[None, :], v_final_norm[None, :])]
    small_new = _adamw_small(small_p)
    small_new = [small_new[3 * k:3 * k + 3] for k in range(len(small_p))]

    order = [("s", 0), ("b", 0), ("s", 1), ("s", 2), ("s", 3), ("b", 1), ("b", 2), ("b", 3), ("s", 4), ("b", 4),
             ("s", 5), ("b", 5), ("s", 6)]
    shapes = [mix_norm.shape, w_in.shape, b_in.shape, sinks.shape, conv_w.shape, w_attn_branch.shape,
              w_conv_branch.shape, w_out.shape, ffn_norm.shape, w_up.shape, ffn_conv_w.shape, w_down.shape,
              final_norm.shape]
    small_g = [p[1] for p in small_p]
    grads = [(small_g[k] if kind == "s" else big_g[k]).reshape(shp) for (kind, k), shp in zip(order, shapes)]
    news = [[(small_new[k][j] if kind == "s" else big_new[k][j]).reshape(shp) for (kind, k), shp in zip(order, shapes)]
            for j in range(3)]
    return (loss, grad_x[None], *grads, *news[0], *news[1], *news[2])
```

```python
import functools

import jax
import jax.numpy as jnp
from jax import lax
from jax.experimental import pallas as pl
from jax.experimental.pallas import tpu as pltpu

F32 = jnp.float32
BF16 = jnp.bfloat16

D_MODEL = 1024
HEAD_DIM = 64
N_HEADS = 8
N_KV_HEADS = 2
GROUP = N_HEADS // N_KV_HEADS
BLOCK = 128
ATTN_SCALE = HEAD_DIM ** -0.5
ATTN_W = N_HEADS * HEAD_DIM
KV_W = N_KV_HEADS * HEAD_DIM
CONV_W = 512
QKV_W = ATTN_W + 2 * KV_W
C3_W = 3 * CONV_W
GATES_W = 2 * D_MODEL
IN_W = QKV_W + C3_W + GATES_W
D_FF = 2816
FF2 = 2 * D_FF
NORM_EPS = 1e-5
N_CHIPS = 4
IN_SHARD = IN_W // N_CHIPS
NEG = -1e30

ADAM_LR = 0.001
ADAM_B1 = 0.9
ADAM_B2 = 0.999
ADAM_EPS = 1e-08
ADAM_WD = 0.01
ADAM_STEP = 10

VMEM_LIMIT = 56 * 1024 * 1024
MESH = pl.DeviceIdType.MESH

NT = (((1,), (1,)), ((), ()))
TN = (((0,), (0,)), ((), ()))


def _params(*sem):
    return pltpu.CompilerParams(dimension_semantics=sem, vmem_limit_bytes=VMEM_LIMIT)


def _resident(shape):
    return pl.BlockSpec(shape, lambda *_: (0,) * len(shape), pipeline_mode=pl.Buffered(1))


def _sigmoid(v):
    return 1.0 / (1.0 + jnp.exp(-v))


def _rstd(v):
    return lax.rsqrt(jnp.mean(v * v, axis=-1, keepdims=True) + NORM_EPS)


def _rms_bwd(dy, v, rstd, g):
    vhat = v * rstd
    t = dy * g
    return rstd * (t - vhat * jnp.mean(t * vhat, axis=-1, keepdims=True)), dy * vhat


def _shift_down(z, k, prev):
    r = pltpu.roll(z, k, 0)
    rows = lax.broadcasted_iota(jnp.int32, z.shape, 0)
    for j in range(k):
        r = jnp.where(rows == j, prev[j:j + 1, :], r)
    return r


def _shift_up(z, k, nxt):
    n = z.shape[0]
    r = pltpu.roll(z, n - k, 0)
    rows = lax.broadcasted_iota(jnp.int32, z.shape, 0)
    for j in range(k):
        r = jnp.where(rows == n - k + j, nxt[j:j + 1, :], r)
    return r


def _inproj_fwd(x, g1, w_in, b_in, tm, comm=None):
    s = x.shape[0]

    def body(x_ref, g_ref, w_ref, b_ref, xn_ref, qkv_ref, c3_ref, gt_ref):
        xf = x_ref[...]
        xn = (xf * _rstd(xf) * g_ref[...]).astype(BF16)
        xn_ref[...] = xn

        def seg(a, b):
            return jnp.dot(xn, w_ref[:, a:b], preferred_element_type=F32) + b_ref[:, a:b]

        qkv_ref[...] = seg(0, QKV_W).astype(BF16)
        c3_ref[...] = seg(QKV_W, QKV_W + C3_W)
        gt_ref[...] = seg(QKV_W + C3_W, IN_W)

    row = lambda w: pl.BlockSpec((tm, w), lambda i: (i, 0))
    return _call(
        comm, body, name="inproj_fwd", grid=(s // tm,),
        in_specs=[row(D_MODEL), _resident((1, D_MODEL)), _resident((D_MODEL, IN_W)), _resident((1, IN_W))],
        out_specs=[row(D_MODEL), row(QKV_W), row(C3_W), row(GATES_W)],
        out_shape=[jax.ShapeDtypeStruct((s, D_MODEL), BF16), jax.ShapeDtypeStruct((s, QKV_W), BF16),
                   jax.ShapeDtypeStruct((s, C3_W), F32), jax.ShapeDtypeStruct((s, GATES_W), F32)],
        compiler_params=_params("parallel"),
    )(x, g1, w_in, b_in)


def _attn_mask(first_block):
    qi = lax.broadcasted_iota(jnp.int32, (GROUP * BLOCK, 2 * BLOCK), 0) & (BLOCK - 1)
    kj = lax.broadcasted_iota(jnp.int32, (GROUP * BLOCK, 2 * BLOCK), 1)
    band = (kj > qi) & (kj <= qi + BLOCK)
    return band & ((kj >= BLOCK) | jnp.logical_not(first_block))


def _sink_column(sk_ref, h):
    rows = lax.broadcasted_iota(jnp.int32, (GROUP * BLOCK, 1), 0)
    col = jnp.full((GROUP * BLOCK, 1), sk_ref[h * GROUP], F32)
    for g in range(1, GROUP):
        col = jnp.where(rows >= g * BLOCK, sk_ref[h * GROUP + g], col)
    return col


def _stack_heads(t, h):
    return jnp.concatenate(
        [t[:, (h * GROUP + g) * HEAD_DIM:(h * GROUP + g + 1) * HEAD_DIM] for g in range(GROUP)], axis=0)


def _unstack_heads(per_kv):
    return jnp.concatenate(
        [t[g * BLOCK:(g + 1) * BLOCK] for t in per_kv for g in range(GROUP)], axis=1)


def _attn_specs(nb):
    cur = lambda i: jnp.minimum(i, nb - 1)
    prev = lambda i: jnp.maximum(jnp.minimum(i, nb - 1) - 1, 0)
    q = pl.BlockSpec((BLOCK, ATTN_W), lambda i: (cur(i), 0))
    kp = pl.BlockSpec((BLOCK, KV_W), lambda i: (prev(i), ATTN_W // KV_W))
    kc = pl.BlockSpec((BLOCK, KV_W), lambda i: (cur(i), ATTN_W // KV_W))
    vp = pl.BlockSpec((BLOCK, KV_W), lambda i: (prev(i), ATTN_W // KV_W + 1))
    vc = pl.BlockSpec((BLOCK, KV_W), lambda i: (cur(i), ATTN_W // KV_W + 1))
    return q, kp, kc, vp, vc


def _attn_fwd(qkv, sinks, comm=None):
    s = qkv.shape[0]
    nb = s // BLOCK

    def body(sk_ref, q_ref, kp_ref, kc_ref, vp_ref, vc_ref, o_ref):
        mask = _attn_mask(pl.program_id(0) == 0)
        q, kp, kc, vp, vc = q_ref[...], kp_ref[...], kc_ref[...], vp_ref[...], vc_ref[...]
        outs = []
        for h in range(N_KV_HEADS):
            hs = slice(h * HEAD_DIM, (h + 1) * HEAD_DIM)
            k2 = jnp.concatenate([kp[:, hs], kc[:, hs]], axis=0)
            v2 = jnp.concatenate([vp[:, hs], vc[:, hs]], axis=0)
            sc = lax.dot_general(_stack_heads(q, h), k2, NT, preferred_element_type=F32) * ATTN_SCALE
            sc = jnp.where(mask, sc, NEG)
            sink = _sink_column(sk_ref, h)
            m = jnp.maximum(jnp.max(sc, axis=1, keepdims=True), sink)
            p = jnp.exp(sc - m)
            den = jnp.sum(p, axis=1, keepdims=True) + jnp.exp(sink - m)
            outs.append(jnp.dot(p.astype(BF16), v2, preferred_element_type=F32) / den)
        o_ref[...] = _unstack_heads(outs).astype(BF16)

    return _call(
        comm, body, name="attn_fwd", grid=(nb,),
        in_specs=[pl.BlockSpec(memory_space=pltpu.SMEM), *_attn_specs(nb)],
        out_specs=pl.BlockSpec((BLOCK, ATTN_W), lambda i: (i, 0)),
        out_shape=jax.ShapeDtypeStruct((s, ATTN_W), BF16),
        compiler_params=_params("parallel"),
    )(sinks, qkv, qkv, qkv, qkv, qkv)


def _mix_fwd(x, attn, c3, gates, conv_w, w_ab, w_cb, w_out, g2, tm, comm=None):
    s = x.shape[0]

    def body(x_ref, at_ref, c3_ref, gt_ref, cw_ref, wab_ref, wcb_ref, wo_ref, g_ref,
             conv_ref, a_ref, cv_ref, mg_ref, h1_ref, hn_ref, carry_ref):
        @pl.when(pl.program_id(0) == 0)
        def _():
            carry_ref[...] = jnp.zeros_like(carry_ref)

        c3v = c3_ref[...]
        cb, cc, cx = c3v[:, :CONV_W], c3v[:, CONV_W:2 * CONV_W], c3v[:, 2 * CONV_W:]
        z = cc * cx
        prev = carry_ref[...]
        cw = cw_ref[...]
        cz = cw[2:3] * z + cw[1:2] * _shift_down(z, 1, prev[7:8]) + cw[0:1] * _shift_down(z, 2, prev[6:8])
        carry_ref[...] = z[tm - 8:tm]
        conv = (cb * cz).astype(BF16)
        conv_ref[...] = conv
        a = jnp.dot(at_ref[...], wab_ref[...], preferred_element_type=F32)
        cv = jnp.dot(conv, wcb_ref[...], preferred_element_type=F32)
        a_ref[...] = a.astype(BF16)
        cv_ref[...] = cv.astype(BF16)
        gt = gt_ref[...]
        merged = (_sigmoid(gt[:, :D_MODEL]) * a + _sigmoid(gt[:, D_MODEL:]) * cv).astype(BF16)
        mg_ref[...] = merged
        h1 = x_ref[...] + jnp.dot(merged, wo_ref[...], preferred_element_type=F32)
        h1_ref[...] = h1
        hn_ref[...] = (h1 * _rstd(h1) * g_ref[...]).astype(BF16)

    row = lambda w: pl.BlockSpec((tm, w), lambda i: (i, 0))
    return _call(
        comm, body, name="mix_fwd", grid=(s // tm,),
        in_specs=[row(D_MODEL), row(ATTN_W), row(C3_W), row(GATES_W), _resident((3, CONV_W)),
                  _resident((ATTN_W, D_MODEL)), _resident((CONV_W, D_MODEL)), _resident((D_MODEL, D_MODEL)),
                  _resident((1, D_MODEL))],
        out_specs=[row(CONV_W), row(D_MODEL), row(D_MODEL), row(D_MODEL), row(D_MODEL), row(D_MODEL)],
        out_shape=[jax.ShapeDtypeStruct((s, CONV_W), BF16), jax.ShapeDtypeStruct((s, D_MODEL), BF16),
                   jax.ShapeDtypeStruct((s, D_MODEL), BF16), jax.ShapeDtypeStruct((s, D_MODEL), BF16),
                   jax.ShapeDtypeStruct((s, D_MODEL), F32), jax.ShapeDtypeStruct((s, D_MODEL), BF16)],
        scratch_shapes=[pltpu.VMEM((8, CONV_W), F32)],
        compiler_params=_params("arbitrary"),
    )(x, attn, c3, gates, conv_w, w_ab, w_cb, w_out, g2)


def _ffn_fwd_loss(hn, h1, w_up, ffn_cw, w_down, g3, target, tm):
    s = hn.shape[0]

    def body(hn_ref, h1_ref, wu_ref, cw_ref, wd_ref, g_ref, t_ref,
             u_ref, act_ref, dh2_ref, loss_ref, gfn_ref, carry_ref):
        @pl.when(pl.program_id(0) == 0)
        def _():
            carry_ref[...] = jnp.zeros_like(carry_ref)
            loss_ref[...] = jnp.zeros_like(loss_ref)
            gfn_ref[...] = jnp.zeros_like(gfn_ref)

        u = jnp.dot(hn_ref[...], wu_ref[...], preferred_element_type=F32)
        u_ref[...] = u.astype(BF16)
        prev = carry_ref[...]
        cw = cw_ref[...]
        up = cw[2:3] * u + cw[1:2] * _shift_down(u, 1, prev[7:8]) + cw[0:1] * _shift_down(u, 2, prev[6:8])
        carry_ref[...] = u[tm - 8:tm]
        gate, val = up[:, :D_FF], up[:, D_FF:]
        act = (gate * _sigmoid(gate) * val).astype(BF16)
        act_ref[...] = act
        h2 = h1_ref[...] + jnp.dot(act, wd_ref[...], preferred_element_type=F32)
        rstd = _rstd(h2)
        g = g_ref[...]
        err = h2 * rstd * g - t_ref[...]
        loss_ref[...] += jnp.sum(err * err) * (0.5 / D_MODEL)
        dh2, dg = _rms_bwd(err * (1.0 / D_MODEL), h2, rstd, g)
        dh2_ref[...] = dh2
        gfn_ref[...] += jnp.sum(dg, axis=0, keepdims=True)

    row = lambda w: pl.BlockSpec((tm, w), lambda i: (i, 0))
    acc = lambda w: pl.BlockSpec((1, w), lambda i: (0, 0))
    return pl.pallas_call(
        body, name="ffn_fwd_loss", grid=(s // tm,),
        in_specs=[row(D_MODEL), row(D_MODEL), _resident((D_MODEL, FF2)), _resident((3, FF2)),
                  _resident((D_FF, D_MODEL)), _resident((1, D_MODEL)), row(D_MODEL)],
        out_specs=[row(FF2), row(D_FF), row(D_MODEL), acc(128), acc(D_MODEL)],
        out_shape=[jax.ShapeDtypeStruct((s, FF2), BF16), jax.ShapeDtypeStruct((s, D_FF), BF16),
                   jax.ShapeDtypeStruct((s, D_MODEL), F32), jax.ShapeDtypeStruct((1, 128), F32),
                   jax.ShapeDtypeStruct((1, D_MODEL), F32)],
        scratch_shapes=[pltpu.VMEM((8, FF2), F32)],
        compiler_params=_params("arbitrary"),
    )(hn, h1, w_up, ffn_cw, w_down, g3, target)


def _ffn_bwd(dh2, u, h1, w_up, ffn_cw, w_down, g2, tm):
    s = dh2.shape[0]
    nt = s // tm
    halo = 16

    def body(dh2_ref, u_ref, uh_ref, h1_ref, wu_ref, cw_ref, wd_ref, g_ref,
             du_ref, dh1_ref, gcw_ref, gg_ref, carry_ref):
        i = pl.program_id(0)

        @pl.when(i == 0)
        def _():
            carry_ref[...] = jnp.zeros_like(carry_ref)
            gcw_ref[...] = jnp.zeros_like(gcw_ref)
            gg_ref[...] = jnp.zeros_like(gg_ref)

        dh2v = dh2_ref[...]
        dact = lax.dot_general(dh2v.astype(BF16), wd_ref[...], NT, preferred_element_type=F32)
        u = u_ref[...].astype(F32)
        uh = uh_ref[...].astype(F32) * (i < nt - 1).astype(F32)
        u1 = _shift_down(u, 1, uh[halo - 1:halo])
        u2 = _shift_down(u, 2, uh[halo - 2:halo])
        cw = cw_ref[...]
        up = cw[2:3] * u + cw[1:2] * u1 + cw[0:1] * u2
        gate, val = up[:, :D_FF], up[:, D_FF:]
        sg = _sigmoid(gate)
        dval = dact * (gate * sg)
        dgate = dact * val * (sg * (1.0 + gate * (1.0 - sg)))
        dup = jnp.concatenate([dgate, dval], axis=1)
        gcw_ref[2:3, :] += jnp.sum(dup * u, axis=0, keepdims=True)
        gcw_ref[1:2, :] += jnp.sum(dup * u1, axis=0, keepdims=True)
        gcw_ref[0:1, :] += jnp.sum(dup * u2, axis=0, keepdims=True)
        nxt = carry_ref[...]
        du = (cw[2:3] * dup + cw[1:2] * _shift_up(dup, 1, nxt[0:1]) + cw[0:1] * _shift_up(dup, 2, nxt[0:2]))
        carry_ref[...] = dup[0:8]
        du = du.astype(BF16)
        du_ref[...] = du
        dhn = lax.dot_general(du, wu_ref[...], NT, preferred_element_type=F32)
        h1v = h1_ref[...]
        dh1, dg = _rms_bwd(dhn, h1v, _rstd(h1v), g_ref[...])
        dh1_ref[...] = dh2v + dh1
        gg_ref[...] += jnp.sum(dg, axis=0, keepdims=True)

    row = lambda w: pl.BlockSpec((tm, w), lambda i: (nt - 1 - i, 0))
    return pl.pallas_call(
        body, name="ffn_bwd", grid=(nt,),
        in_specs=[row(D_MODEL), row(FF2),
                  pl.BlockSpec((halo, FF2), lambda i: (jnp.maximum((nt - 1 - i) * (tm // halo) - 1, 0), 0)),
                  row(D_MODEL), _resident((D_MODEL, FF2)), _resident((3, FF2)), _resident((D_FF, D_MODEL)),
                  _resident((1, D_MODEL))],
        out_specs=[row(FF2), row(D_MODEL), pl.BlockSpec((3, FF2), lambda i: (0, 0)),
                   pl.BlockSpec((1, D_MODEL), lambda i: (0, 0))],
        out_shape=[jax.ShapeDtypeStruct((s, FF2), BF16), jax.ShapeDtypeStruct((s, D_MODEL), F32),
                   jax.ShapeDtypeStruct((3, FF2), F32), jax.ShapeDtypeStruct((1, D_MODEL), F32)],
        scratch_shapes=[pltpu.VMEM((8, FF2), F32)],
        compiler_params=_params("arbitrary"),
    )(dh2, u, u, h1, w_up, ffn_cw, w_down, g2)


def _mix_bwd(dh1, gates, a, cv, c3, conv_w, w_ab, w_cb, w_out, tm, comm=None):
    s = dh1.shape[0]
    nt = s // tm
    halo = 8

    def body(dh1_ref, gt_ref, a_ref, cv_ref, c3_ref, ch_ref, cw_ref, wab_ref, wcb_ref, wo_ref,
             dat_ref, da_ref, dcv_ref, dc3_ref, dgt_ref, gcw_ref, carry_ref):
        i = pl.program_id(0)

        @pl.when(i == 0)
        def _():
            carry_ref[...] = jnp.zeros_like(carry_ref)
            gcw_ref[...] = jnp.zeros_like(gcw_ref)

        dm = lax.dot_general(dh1_ref[...].astype(BF16), wo_ref[...], NT, preferred_element_type=F32)
        gt = gt_ref[...]
        sa, sc = _sigmoid(gt[:, :D_MODEL]), _sigmoid(gt[:, D_MODEL:])
        da = (dm * sa).astype(BF16)
        dcv = (dm * sc).astype(BF16)
        da_ref[...] = da
        dcv_ref[...] = dcv
        dgt_ref[...] = jnp.concatenate(
            [dm * a_ref[...].astype(F32) * (sa * (1.0 - sa)), dm * cv_ref[...].astype(F32) * (sc * (1.0 - sc))],
            axis=1).astype(BF16)
        dat_ref[...] = lax.dot_general(da, wab_ref[...], NT, preferred_element_type=F32).astype(BF16)
        dconv = lax.dot_general(dcv, wcb_ref[...], NT, preferred_element_type=F32)
        c3v = c3_ref[...]
        cb, cc, cx = c3v[:, :CONV_W], c3v[:, CONV_W:2 * CONV_W], c3v[:, 2 * CONV_W:]
        z = cc * cx
        chv = ch_ref[...] * (i < nt - 1).astype(F32)
        zh = chv[:, CONV_W:2 * CONV_W] * chv[:, 2 * CONV_W:]
        z1 = _shift_down(z, 1, zh[halo - 1:halo])
        z2 = _shift_down(z, 2, zh[halo - 2:halo])
        cw = cw_ref[...]
        cz = cw[2:3] * z + cw[1:2] * z1 + cw[0:1] * z2
        dcz = dconv * cb
        gcw_ref[2:3, :] += jnp.sum(dcz * z, axis=0, keepdims=True)
        gcw_ref[1:2, :] += jnp.sum(dcz * z1, axis=0, keepdims=True)
        gcw_ref[0:1, :] += jnp.sum(dcz * z2, axis=0, keepdims=True)
        nxt = carry_ref[...]
        dz = cw[2:3] * dcz + cw[1:2] * _shift_up(dcz, 1, nxt[0:1]) + cw[0:1] * _shift_up(dcz, 2, nxt[0:2])
        carry_ref[...] = dcz[0:8]
        dc3_ref[...] = jnp.concatenate([dconv * cz, dz * cx, dz * cc], axis=1).astype(BF16)

    row = lambda w: pl.BlockSpec((tm, w), lambda i: (nt - 1 - i, 0))
    return _call(
        comm, body, name="mix_bwd", grid=(nt,),
        in_specs=[row(D_MODEL), row(GATES_W), row(D_MODEL), row(D_MODEL), row(C3_W),
                  pl.BlockSpec((halo, C3_W), lambda i: (jnp.maximum((nt - 1 - i) * (tm // halo) - 1, 0), 0)),
                  _resident((3, CONV_W)), _resident((ATTN_W, D_MODEL)), _resident((CONV_W, D_MODEL)),
                  _resident((D_MODEL, D_MODEL))],
        out_specs=[row(ATTN_W), row(D_MODEL), row(D_MODEL), row(C3_W), row(GATES_W),
                   pl.BlockSpec((3, CONV_W), lambda i: (0, 0))],
        out_shape=[jax.ShapeDtypeStruct((s, ATTN_W), BF16), jax.ShapeDtypeStruct((s, D_MODEL), BF16),
                   jax.ShapeDtypeStruct((s, D_MODEL), BF16), jax.ShapeDtypeStruct((s, C3_W), BF16),
                   jax.ShapeDtypeStruct((s, GATES_W), BF16), jax.ShapeDtypeStruct((3, CONV_W), F32)],
        scratch_shapes=[pltpu.VMEM((8, CONV_W), F32)],
        compiler_params=_params("arbitrary"),
    )(dh1, gates, a, cv, c3, c3, conv_w, w_ab, w_cb, w_out)


def _attn_bwd(qkv, sinks, o, do, comm=None):
    s = qkv.shape[0]
    nb = s // BLOCK

    def body(sk_ref, q_ref, kp_ref, kc_ref, vp_ref, vc_ref, o_ref, do_ref,
             dq_ref, dk_ref, dv_ref, dsk_ref, ck_ref, cvv_ref):
        i = pl.program_id(0)

        @pl.when(i == 0)
        def _():
            ck_ref[...] = jnp.zeros_like(ck_ref)
            cvv_ref[...] = jnp.zeros_like(cvv_ref)
            dsk_ref[...] = jnp.zeros_like(dsk_ref)

        @pl.when(i < nb)
        def _():
            mask = _attn_mask(i == 0)
            q, kp, kc, vp, vc = q_ref[...], kp_ref[...], kc_ref[...], vp_ref[...], vc_ref[...]
            ov, dov = o_ref[...], do_ref[...]
            dqs, dks, dvs = [], [], []
            for h in range(N_KV_HEADS):
                hs = slice(h * HEAD_DIM, (h + 1) * HEAD_DIM)
                k2 = jnp.concatenate([kp[:, hs], kc[:, hs]], axis=0)
                v2 = jnp.concatenate([vp[:, hs], vc[:, hs]], axis=0)
                qg, og, dog = _stack_heads(q, h), _stack_heads(ov, h), _stack_heads(dov, h)
                sc = lax.dot_general(qg, k2, NT, preferred_element_type=F32) * ATTN_SCALE
                sc = jnp.where(mask, sc, NEG)
                sink = _sink_column(sk_ref, h)
                m = jnp.maximum(jnp.max(sc, axis=1, keepdims=True), sink)
                p = jnp.exp(sc - m)
                psink = jnp.exp(sink - m)
                inv = 1.0 / (jnp.sum(p, axis=1, keepdims=True) + psink)
                p = p * inv
                delta = jnp.sum(dog.astype(F32) * og.astype(F32), axis=1, keepdims=True)
                dp = lax.dot_general(dog, v2, NT, preferred_element_type=F32)
                ds = (p * (dp - delta)).astype(BF16)
                dqs.append(jnp.dot(ds, k2, preferred_element_type=F32) * ATTN_SCALE)
                dks.append(lax.dot_general(ds, qg, TN, preferred_element_type=F32) * ATTN_SCALE)
                dvs.append(lax.dot_general(p.astype(BF16), dog, TN, preferred_element_type=F32))
                dsink = -(psink * inv * delta)
                for g in range(GROUP):
                    r = h * GROUP + g
                    dsk_ref[r:r + 1, :] += jnp.sum(dsink[g * BLOCK:(g + 1) * BLOCK])
            dq_ref[...] = _unstack_heads(dqs).astype(BF16)
            dk2 = jnp.concatenate(dks, axis=1)
            dv2 = jnp.concatenate(dvs, axis=1)
            dk_ref[...] = (ck_ref[...] + dk2[:BLOCK]).astype(BF16)
            dv_ref[...] = (cvv_ref[...] + dv2[:BLOCK]).astype(BF16)
            ck_ref[...] = dk2[BLOCK:]
            cvv_ref[...] = dv2[BLOCK:]

        @pl.when(i == nb)
        def _():
            dk_ref[...] = ck_ref[...].astype(BF16)
            dv_ref[...] = cvv_ref[...].astype(BF16)

    cur = lambda i: jnp.minimum(i, nb - 1)
    done = lambda i: jnp.maximum(i - 1, 0)
    return _call(
        comm, body, name="attn_bwd", grid=(nb + 1,),
        in_specs=[pl.BlockSpec(memory_space=pltpu.SMEM), *_attn_specs(nb),
                  pl.BlockSpec((BLOCK, ATTN_W), lambda i: (cur(i), 0)),
                  pl.BlockSpec((BLOCK, ATTN_W), lambda i: (cur(i), 0))],
        out_specs=[pl.BlockSpec((BLOCK, ATTN_W), lambda i: (cur(i), 0)),
                   pl.BlockSpec((BLOCK, KV_W), lambda i: (done(i), 0)),
                   pl.BlockSpec((BLOCK, KV_W), lambda i: (done(i), 0)),
                   pl.BlockSpec((N_HEADS, 128), lambda i: (0, 0))],
        out_shape=[jax.ShapeDtypeStruct((s, ATTN_W), BF16), jax.ShapeDtypeStruct((s, KV_W), BF16),
                   jax.ShapeDtypeStruct((s, KV_W), BF16), jax.ShapeDtypeStruct((N_HEADS, 128), F32)],
        scratch_shapes=[pltpu.VMEM((BLOCK, KV_W), F32), pltpu.VMEM((BLOCK, KV_W), F32)],
        compiler_params=_params("arbitrary"),
    )(sinks, qkv, qkv, qkv, qkv, qkv, o, do)


def _inproj_bwd(dq, dk, dv, dc3, dgt, w_in, x, dh1, g1, tm, comm=None):
    s = x.shape[0]

    def body(dq_ref, dk_ref, dv_ref, dc3_ref, dgt_ref, w_ref, x_ref, dh1_ref, g_ref,
             dx_ref, dp_ref, gb_ref, gg_ref):
        @pl.when(pl.program_id(0) == 0)
        def _():
            gb_ref[...] = jnp.zeros_like(gb_ref)
            gg_ref[...] = jnp.zeros_like(gg_ref)

        dp = jnp.concatenate([dq_ref[...], dk_ref[...], dv_ref[...], dc3_ref[...], dgt_ref[...]], axis=1)
        dp_ref[...] = dp
        gb_ref[...] += jnp.sum(dp.astype(F32), axis=0, keepdims=True)
        dxn = lax.dot_general(dp, w_ref[...], NT, preferred_element_type=F32)
        xf = x_ref[...]
        dx, dg = _rms_bwd(dxn, xf, _rstd(xf), g_ref[...])
        dx_ref[...] = dh1_ref[...] + dx
        gg_ref[...] += jnp.sum(dg, axis=0, keepdims=True)

    row = lambda w: pl.BlockSpec((tm, w), lambda i: (i, 0))
    acc = lambda w: pl.BlockSpec((1, w), lambda i: (0, 0))
    return _call(
        comm, body, name="inproj_bwd", grid=(s // tm,),
        in_specs=[row(ATTN_W), row(KV_W), row(KV_W), row(C3_W), row(GATES_W), _resident((D_MODEL, IN_W)),
                  row(D_MODEL), row(D_MODEL), _resident((1, D_MODEL))],
        out_specs=[row(D_MODEL), row(IN_W), acc(IN_W), acc(D_MODEL)],
        out_shape=[jax.ShapeDtypeStruct((s, D_MODEL), F32), jax.ShapeDtypeStruct((s, IN_W), BF16),
                   jax.ShapeDtypeStruct((1, IN_W), F32), jax.ShapeDtypeStruct((1, D_MODEL), F32)],
        compiler_params=_params("arbitrary"),
    )(dq, dk, dv, dc3, dgt, w_in, x, dh1, g1)


def _wgrad(a, b, bm, bn, bk, name, comm=None):
    s, m = a.shape
    n = b.shape[1]
    nk = s // bk

    def body(a_ref, b_ref, o_ref, acc_ref):
        k = pl.program_id(2)

        @pl.when(k == 0)
        def _():
            acc_ref[...] = jnp.zeros_like(acc_ref)

        acc_ref[...] += lax.dot_general(a_ref[...].astype(BF16), b_ref[...].astype(BF16), TN,
                                        preferred_element_type=F32)

        @pl.when(k == nk - 1)
        def _():
            o_ref[...] = acc_ref[...].astype(BF16)

    return _call(
        comm, body, name=name, grid=(m // bm, n // bn, nk),
        in_specs=[pl.BlockSpec((bk, bm), lambda i, j, k: (k, i)), pl.BlockSpec((bk, bn), lambda i, j, k: (k, j))],
        out_specs=pl.BlockSpec((bm, bn), lambda i, j, k: (i, j)),
        out_shape=jax.ShapeDtypeStruct((m, n), BF16),
        scratch_shapes=[pltpu.VMEM((bm, bn), F32)],
        compiler_params=_params("parallel", "parallel", "arbitrary"),
    )(a, b)


def _wgrad_in(xn, dproj, bk, comm=None):
    s = xn.shape[0]
    nk = s // bk

    def body(a_ref, b_ref, o_ref, acc_ref):
        k = pl.program_id(0)

        @pl.when(k == 0)
        def _():
            acc_ref[...] = jnp.zeros_like(acc_ref)

        av, bv = a_ref[...], b_ref[...]
        for j in range(N_CHIPS):
            acc_ref[j] += lax.dot_general(av, bv[:, j * IN_SHARD:(j + 1) * IN_SHARD], TN,
                                          preferred_element_type=F32)

        @pl.when(k == nk - 1)
        def _():
            o_ref[...] = acc_ref[...].astype(BF16)

    return _call(
        comm, body, name="wgrad_in", grid=(nk,),
        in_specs=[pl.BlockSpec((bk, D_MODEL), lambda k: (k, 0)), pl.BlockSpec((bk, IN_W), lambda k: (k, 0))],
        out_specs=_resident((N_CHIPS, D_MODEL, IN_SHARD)),
        out_shape=jax.ShapeDtypeStruct((N_CHIPS, D_MODEL, IN_SHARD), BF16),
        scratch_shapes=[pltpu.VMEM((N_CHIPS, D_MODEL, IN_SHARD), F32)],
        compiler_params=_params("arbitrary"),
    )(xn, dproj)


class _Carry:
    def __init__(self, jobs, reads=None, bufs=None, fresh=None):
        self.jobs, self.reads, self.bufs, self.fresh = jobs, reads or {}, bufs or {}, fresh or {}
        self.out = {}


class _Job:
    def __init__(self, n_sems, plan):
        self.n_sems, self.plan = n_sems, plan


def _plan_all(jobs, hbm, send, recv):
    pos = _position()
    starts, waits, base = [], [], 0
    for job in jobs:
        s, w = job.plan(hbm, pos, send, recv, base)
        starts, waits, base = starts + s, waits + w, base + job.n_sems
    return starts, waits


def _call(comm, body, **kw):
    if comm is None:
        return pl.pallas_call(body, **kw)
    grid = kw["grid"]
    single = not isinstance(kw["out_shape"], (list, tuple))
    out_shape = [kw["out_shape"]] if single else list(kw["out_shape"])
    out_specs = [kw["out_specs"]] if single else list(kw["out_specs"])
    in_specs = list(kw["in_specs"])
    scratch = list(kw.get("scratch_shapes", ()))
    r_names, b_names, f_names = list(comm.reads), list(comm.bufs), list(comm.fresh)
    n_args, n_out, n_scr = len(in_specs), len(out_shape), len(scratch)
    n_sems = sum(j.n_sems for j in comm.jobs)

    def wrapped(*refs):
        k = n_args
        hbm = dict(zip(r_names, refs[k:k + len(r_names)]))
        k += len(r_names) + len(b_names)
        outs = refs[k:k + n_out]
        k += n_out
        hbm.update(zip(b_names + f_names, refs[k:k + len(b_names) + len(f_names)]))
        k += len(b_names) + len(f_names)
        send, recv = refs[k + n_scr:]
        starts, waits = _plan_all(comm.jobs, hbm, send, recv)
        ids = [pl.program_id(a) for a in range(len(grid))]
        first = functools.reduce(jnp.logical_and, [i == 0 for i in ids])
        last = functools.reduce(jnp.logical_and, [i == g - 1 for i, g in zip(ids, grid)])

        @pl.when(first)
        def _():
            for cp in starts:
                cp.start()

        body(*refs[:n_args], *outs, *refs[k:k + n_scr])

        @pl.when(last)
        def _():
            for cp in waits:
                cp.wait_recv()
            for cp in starts:
                cp.wait_send()

    sems = pltpu.SemaphoreType.DMA((n_sems,))
    held = [jax.ShapeDtypeStruct(a.shape, a.dtype) for a in comm.bufs.values()] + list(comm.fresh.values())
    call = pl.pallas_call(
        wrapped, name=kw["name"], grid=grid,
        in_specs=in_specs + [_ANY] * (len(r_names) + len(b_names)),
        out_specs=out_specs + [_ANY] * len(held),
        out_shape=out_shape + held,
        input_output_aliases={n_args + len(r_names) + i: n_out + i for i in range(len(b_names))},
        scratch_shapes=scratch + [sems, sems],
        compiler_params=_params(*["arbitrary"] * len(grid)),
    )

    def run(*args):
        res = call(*args, *comm.reads.values(), *comm.bufs.values())
        comm.out = dict(zip(b_names + f_names, res[n_out:]))
        return res[0] if single else res[:n_out]

    return run


def _exchange(name, phases, reads=None, bufs=None, fresh=None):
    comm = _Carry([j for ph in phases for j in ph], reads, bufs, fresh)
    r_names, b_names, f_names = list(comm.reads), list(comm.bufs), list(comm.fresh)
    n_sems = sum(j.n_sems for j in comm.jobs)

    def body(*refs):
        hbm = dict(zip(r_names, refs[:len(r_names)]))
        k = len(r_names) + len(b_names)
        hbm.update(zip(b_names + f_names, refs[k:k + len(b_names) + len(f_names)]))
        send, recv = refs[-2:]
        pos = _position()
        started, base = [], 0
        for ph in phases:
            waits = []
            for job in ph:
                s, w = job.plan(hbm, pos, send, recv, base)
                base += job.n_sems
                for cp in s:
                    cp.start()
                started, waits = started + s, waits + w
            for cp in waits:
                cp.wait_recv()
        for cp in started:
            cp.wait_send()

    sems = pltpu.SemaphoreType.DMA((n_sems,))
    held = [jax.ShapeDtypeStruct(a.shape, a.dtype) for a in comm.bufs.values()] + list(comm.fresh.values())
    res = pl.pallas_call(
        body, name=name, in_specs=[_ANY] * (len(r_names) + len(b_names)), out_specs=[_ANY] * len(held),
        out_shape=held, input_output_aliases={len(r_names) + i: i for i in range(len(b_names))},
        scratch_shapes=[sems, sems],
    )(*comm.reads.values(), *comm.bufs.values())
    return dict(zip(b_names + f_names, res))


def _row_tile(rows, bytes_per_row):
    best = 16
    for t in range(16, rows + 1, 16):
        if rows % t == 0 and t * bytes_per_row <= 6 * 1024 * 1024:
            best = t
    return best


def _rowwise(fn, ins, out_dtypes, name):
    rows, cols = ins[0].shape[-2:]
    per_row = sum(a.size // rows * a.dtype.itemsize for a in ins) + sum(cols * jnp.dtype(d).itemsize for d in out_dtypes)
    tr = _row_tile(rows, per_row)
    n_in = len(ins)

    def body(*refs):
        outs = fn(*[r[...] for r in refs[:n_in]])
        for o_ref, o in zip(refs[n_in:], outs):
            o_ref[...] = o.astype(o_ref.dtype)

    def spec(a):
        if a.ndim == 3:
            return pl.BlockSpec((a.shape[0], tr, cols), lambda i: (0, i, 0))
        return pl.BlockSpec((tr, cols), lambda i: (i, 0))

    return pl.pallas_call(
        body, name=name, grid=(rows // tr,),
        in_specs=[spec(a) for a in ins],
        out_specs=[pl.BlockSpec((tr, cols), lambda i: (i, 0)) for _ in out_dtypes],
        out_shape=[jax.ShapeDtypeStruct((rows, cols), d) for d in out_dtypes],
        compiler_params=_params("parallel"),
    )(*ins)


def _tiled(fn, name, grid, pos, ins, outs):
    n_in = len(ins)

    def body(pos_ref, *refs):
        res = fn(*[r[...] for r in refs[:n_in]])
        for o_ref, o in zip(refs[n_in:], res):
            o_ref[...] = o.astype(o_ref.dtype)

    return pl.pallas_call(
        body, name=name,
        grid_spec=pltpu.PrefetchScalarGridSpec(
            num_scalar_prefetch=1, grid=grid,
            in_specs=[pl.BlockSpec(bs, im) for _, bs, im in ins],
            out_specs=[pl.BlockSpec(bs, im) for _, _, bs, im in outs]),
        out_shape=[jax.ShapeDtypeStruct(s, d) for s, d, _, _ in outs],
        compiler_params=_params("parallel"),
    )(pos, *[a for a, _, _ in ins])


def _adamw(w, g, m, v):
    m = ADAM_B1 * m + (1.0 - ADAM_B1) * g
    v = ADAM_B2 * v + (1.0 - ADAM_B2) * (g * g)
    m_hat = m / (1.0 - ADAM_B1 ** ADAM_STEP)
    v_hat = v / (1.0 - ADAM_B2 ** ADAM_STEP)
    return -ADAM_LR * (m_hat / (jnp.sqrt(v_hat) + ADAM_EPS) + ADAM_WD * w), m, v


def _adamw_small(params):
    n = len(params)

    def body(*refs):
        for k in range(n):
            w, g, m, v = (r[...] for r in refs[4 * k:4 * k + 4])
            for o_ref, o in zip(refs[4 * n + 3 * k:4 * n + 3 * k + 3], _adamw(w, g, m, v)):
                o_ref[...] = o

    flat = [a for p in params for a in p]
    return pl.pallas_call(
        body, name="adamw_small",
        out_shape=[jax.ShapeDtypeStruct(p[0].shape, F32) for p in params for _ in range(3)],
    )(*flat)


def _repack_in(wg):
    tr = 256

    def body(w_ref, o_ref):
        for j in range(N_CHIPS):
            o_ref[:, j * IN_SHARD:(j + 1) * IN_SHARD] = w_ref[j]

    return pl.pallas_call(
        body, name="repack_in", grid=(D_MODEL // tr,),
        in_specs=[pl.BlockSpec((N_CHIPS, tr, IN_SHARD), lambda i: (0, i, 0))],
        out_specs=pl.BlockSpec((tr, IN_W), lambda i: (i, 0)),
        out_shape=jax.ShapeDtypeStruct((D_MODEL, IN_W), BF16),
        compiler_params=_params("parallel"),
    )(wg)


class _Layout:
    def __init__(self, rows, cols, stacked):
        self.rows, self.cols, self.stacked = rows, cols, stacked

    def whole(self, rows=None):
        r = self.rows if rows is None else rows
        return (N_CHIPS, r, self.cols) if self.stacked else (r, N_CHIPS * self.cols)

    def part_rows(self, h, q=0, nq=1):
        n = self.rows // 2 // nq
        return pl.ds(pl.multiple_of(h * (self.rows // 2) + q * n, 16), n)

    def half_rows(self, h):
        return self.part_rows(h)

    def block(self, ref, p, rows=slice(None)):
        if self.stacked:
            return ref.at[p, rows, :]
        return ref.at[rows, pl.ds(pl.multiple_of(p * self.cols, 128), self.cols)]

    def all_chips(self, ref, rows):
        return ref.at[:, rows, :] if self.stacked else ref.at[rows, :]


BIG = (
    _Layout(D_MODEL, IN_SHARD, True),
    _Layout(ATTN_W, D_MODEL // N_CHIPS, False),
    _Layout(CONV_W, D_MODEL // N_CHIPS, False),
    _Layout(D_MODEL // N_CHIPS, D_MODEL, True),
    _Layout(D_MODEL, FF2 // N_CHIPS, False),
    _Layout(D_FF // N_CHIPS, D_MODEL, True),
)
N_BIG = len(BIG)
_ANY = pl.BlockSpec(memory_space=pl.ANY)


def _position():
    x, y, c = lax.axis_index("x"), lax.axis_index("y"), lax.axis_index("c")
    return x, y, c, 2 * x + y


def _core_of_chip(p, c):
    return (p >> 1, p & 1, c)


def _place_cast(shard, lay, pos, name):
    rows, cols = shard.shape
    tr = _row_tile(rows, cols * 6)
    if lay.stacked:
        out = (lay.whole(), BF16, (None, tr, cols), lambda i, pos: (pos[0], i, 0))
    else:
        out = (lay.whole(), BF16, (tr, cols), lambda i, pos: (i, pos[0]))
    return _tiled(lambda a: (a,), name, (rows // tr,), pos, [(shard, (tr, cols), lambda i, pos: (i, 0))], [out])[0]


def _remote(src, dst, send, recv, k, device):
    return pltpu.make_async_remote_copy(src_ref=src, dst_ref=dst, send_sem=send.at[k], recv_sem=recv.at[k],
                                        device_id=device, device_id_type=MESH)


def _arrival(dst, send, recv, k, me):
    return _remote(dst, dst, send, recv, k, me)


def _gather_ici(lay, name, q=0, nq=1):
    def plan(hbm, pos, send, recv, base):
        x, y, c, me = pos
        rows = lay.part_rows(c, q, nq)
        mine = lay.block(hbm[name], me, rows)
        starts = [_remote(mine, mine, send, recv, base + d - 1, _core_of_chip(me ^ d, c)) for d in (1, 2, 3)]
        waits = [_arrival(lay.block(hbm[name], me ^ d, rows), send, recv, base + d - 1, (x, y, c)) for d in (1, 2, 3)]
        return starts, waits
    return _Job(3, plan)


def _gather_d2d(lay, name, q=0, nq=1):
    def plan(hbm, pos, send, recv, base):
        x, y, c, me = pos
        starts, waits = [], []
        for d in (1, 2, 3):
            got = lay.block(hbm[name], me ^ d, lay.part_rows(c, q, nq))
            starts.append(_remote(got, got, send, recv, base + d - 1, (x, y, 1 - c)))
            waits.append(_arrival(lay.block(hbm[name], me ^ d, lay.part_rows(1 - c, q, nq)), send, recv, base + d - 1,
                                  (x, y, c)))
        return starts, waits
    return _Job(3, plan)


def _rs_pair(lay, grad, theirs):
    def plan(hbm, pos, send, recv, base):
        x, y, c, _ = pos
        out = _remote(lay.all_chips(hbm[grad], lay.half_rows(1 - c)), hbm[theirs], send, recv, base, (x, y, 1 - c))
        return [out], [_arrival(hbm[theirs], send, recv, base, (x, y, c))]
    return _Job(1, plan)


def _rs_chips(lay, sums, slots):
    def plan(hbm, pos, send, recv, base):
        x, y, c, me = pos
        starts = [_remote(lay.block(hbm[sums], me ^ d), hbm[slots].at[me], send, recv, base + d - 1,
                          _core_of_chip(me ^ d, c)) for d in (1, 2, 3)]
        waits = [_arrival(hbm[slots].at[me ^ d], send, recv, base + d - 1, (x, y, c)) for d in (1, 2, 3)]
        return starts, waits
    return _Job(3, plan)


def _rs_share(lay, shard):
    def plan(hbm, pos, send, recv, base):
        x, y, c, _ = pos
        mine = hbm[shard].at[lay.half_rows(c), :]
        other = hbm[shard].at[lay.half_rows(1 - c), :]
        return [_remote(mine, mine, send, recv, base, (x, y, 1 - c))], [_arrival(other, send, recv, base, (x, y, c))]
    return _Job(1, plan)


def _slots_shape(lay):
    return jax.ShapeDtypeStruct((N_CHIPS, lay.rows // 2, lay.cols), BF16)


def _theirs_shape(lay):
    return jax.ShapeDtypeStruct(lay.whole(lay.rows // 2), BF16)


def _pair_sum(grad, theirs, lay, pos, name):
    half = lay.rows // 2
    add = lambda a, b: (a.astype(F32) + b.astype(F32),)
    if lay.stacked:
        tr = _row_tile(half, lay.cols * 6)
        nt = half // tr
        flat = lambda a: a.reshape(-1, lay.cols)
        mine = lambda t, pos: ((t // nt) * (2 * nt) + pos[1] * nt + t % nt, 0)
        grid, blk = (N_CHIPS * nt,), (tr, lay.cols)
        grad, theirs = flat(grad), flat(theirs)
    else:
        tr = _row_tile(half, N_CHIPS * lay.cols * 6)
        nt = half // tr
        mine = lambda t, pos: (pos[1] * nt + t, 0)
        grid, blk = (nt,), (tr, N_CHIPS * lay.cols)
    same = lambda t, pos: (t, 0)
    out = _tiled(add, name, grid, pos, [(grad, blk, mine), (theirs, blk, same)], [(theirs.shape, BF16, blk, same)])[0]
    return out.reshape(lay.whole(half))


def _chip_sum(sums, slots, lay, pos, name):
    half = lay.rows // 2
    tr = _row_tile(half, lay.cols * 12)
    nt = half // tr
    blk3 = (None, tr, lay.cols)
    if lay.stacked:
        own = (sums, blk3, lambda i, pos: (pos[0], i, 0))
    else:
        own = (sums, (tr, lay.cols), lambda i, pos: (i, pos[0]))
    others = [(slots, blk3, functools.partial(lambda d, i, pos: (pos[0] ^ d, i, 0), d)) for d in (1, 2, 3)]

    def add(a, b1, b2, b3):
        return (((a.astype(F32) + b1.astype(F32)) + b2.astype(F32)) + b3.astype(F32),)

    return _tiled(add, name, (nt,), pos, [own] + others,
                  [((lay.rows, lay.cols), F32, (tr, lay.cols), lambda i, pos: (pos[1] * nt + i, 0))])[0]


N_DEV = 8


def _exchange_small(v, reduce):
    rows = v.shape[0]

    def body(v_ref, o_ref, *scratch):
        if reduce:
            slots, send, recv = scratch
        else:
            slots, (send, recv) = o_ref, scratch
        x, y, c = lax.axis_index("x"), lax.axis_index("y"), lax.axis_index("c")
        idx = 4 * x + 2 * y + c
        slots[idx] = v_ref[...]

        def to_peer(k):
            return pltpu.make_async_remote_copy(
                src_ref=v_ref, dst_ref=slots.at[idx], send_sem=send.at[k - 1], recv_sem=recv.at[k - 1],
                device_id=(x ^ (k >> 2), y ^ ((k >> 1) & 1), c ^ (k & 1)), device_id_type=MESH)

        def from_peer(k):
            return pltpu.make_async_remote_copy(
                src_ref=v_ref, dst_ref=slots.at[idx ^ k], send_sem=send.at[k - 1], recv_sem=recv.at[k - 1],
                device_id=(x, y, c), device_id_type=MESH)

        for k in range(1, N_DEV):
            to_peer(k).start()
        for k in range(1, N_DEV):
            from_peer(k).wait_recv()
        for k in range(1, N_DEV):
            to_peer(k).wait_send()
        if reduce:
            acc = slots[0]
            for q in range(1, N_DEV):
                acc = acc + slots[q]
            o_ref[...] = acc

    sems = pltpu.SemaphoreType.DMA((N_DEV - 1,))
    stacked = jax.ShapeDtypeStruct((N_DEV, rows, 128), F32)
    return pl.pallas_call(
        body, name="allreduce_small" if reduce else "allgather_small",
        out_shape=jax.ShapeDtypeStruct((rows, 128), F32) if reduce else stacked,
        scratch_shapes=([pltpu.VMEM((N_DEV, rows, 128), F32)] if reduce else []) + [sems, sems],
    )(v)


def _pack_rows(parts):
    padded = [jnp.pad(a, ((0, -a.shape[0] % 8), (0, 0))) for a in parts]
    starts = [sum(p.shape[0] for p in padded[:k]) for k in range(len(padded))]
    return jnp.concatenate(padded, axis=0), starts


def kernel(x, mix_norm, w_in, b_in, sinks, conv_w, w_attn_branch, w_conv_branch, w_out, ffn_norm, w_up, ffn_conv_w, w_down, final_norm, loss_target, m_mix_norm, m_w_in, m_b_in, m_sinks, m_conv_w, m_w_attn_branch, m_w_conv_branch, m_w_out, m_ffn_norm, m_w_up, m_ffn_conv_w, m_w_down, m_final_norm, v_mix_norm, v_w_in, v_b_in, v_sinks, v_conv_w, v_w_attn_branch, v_w_conv_branch, v_w_out, v_ffn_norm, v_w_up, v_ffn_conv_w, v_w_down, v_final_norm):
    me = 2 * lax.axis_index("x") + lax.axis_index("y")
    big_w = [w_in[0], w_attn_branch[0], w_conv_branch[0], w_out[0], w_up[0], w_down[0]]
    big_m = [m_w_in[0], m_w_attn_branch[0], m_w_conv_branch[0], m_w_out[0], m_w_up[0], m_w_down[0]]
    big_v = [v_w_in[0], v_w_attn_branch[0], v_w_conv_branch[0], v_w_out[0], v_w_up[0], v_w_down[0]]
    names = ("w_in", "w_ab", "w_cb", "w_out", "w_up", "w_down")

    pos = jnp.stack([me, lax.axis_index("c")]).astype(jnp.int32)

    lay = dict(zip(names, BIG))
    xs, target, sk = x[0], loss_target[0], sinks[0]
    s = xs.shape[0]
    tm, tm2, bk = min(256, s), min(512, s), min(1024, s)

    placed = {n: _place_cast(w, lay[n], pos, "cast_" + n) for w, n in zip(big_w, names)}
    taps, (_, t0) = _pack_rows([conv_w[0], ffn_conv_w[0].reshape(3 * (FF2 // N_CHIPS // 128), 128)])
    taps = _exchange_small(taps, reduce=False)[0::2]
    conv_full = taps[:, 0:3].transpose(1, 0, 2).reshape(3, CONV_W)
    ffn_cw_full = taps[:, t0:t0 + 33].reshape(N_CHIPS, 3, FF2 // N_CHIPS).transpose(1, 0, 2).reshape(3, FF2)

    w_in_full = _repack_in(_exchange(
        "gather_in", [[_gather_ici(lay["w_in"], "w_in")], [_gather_d2d(lay["w_in"], "w_in")]],
        bufs={"w_in": placed["w_in"]})["w_in"])
    early = ("w_ab", "w_cb", "w_out", "w_down")
    k1 = _Carry([_gather_ici(lay[n], n) for n in early], bufs={n: placed[n] for n in early})
    xn, qkv, c3, gates = _inproj_fwd(xs, mix_norm, w_in_full, b_in, tm2, comm=k1)
    k2 = _Carry([_gather_d2d(lay[n], n) for n in early] + [_gather_ici(lay["w_up"], "w_up", 0, 2)],
                bufs={**k1.out, "w_up": placed["w_up"]})
    attn = _attn_fwd(qkv, sk, comm=k2)
    w_ab, w_cb = k2.out["w_ab"], k2.out["w_cb"]
    w_out_full = k2.out["w_out"].reshape(D_MODEL, D_MODEL)
    w_down_full = k2.out["w_down"].reshape(D_FF, D_MODEL)
    k3 = _Carry([_gather_d2d(lay["w_up"], "w_up", 0, 2), _gather_ici(lay["w_up"], "w_up", 1, 2)],
                bufs={"w_up": k2.out["w_up"]})
    conv, a, cv, merged, h1, hn = _mix_fwd(xs, attn, c3, gates, conv_full, w_ab, w_cb, w_out_full, ffn_norm, tm, comm=k3)
    w_up_full = _exchange("gather_up_tail", [[_gather_d2d(lay["w_up"], "w_up", 1, 2)]],
                          bufs={"w_up": k3.out["w_up"]})["w_up"]
    u, act, dh2, loss_part, g_fn = _ffn_fwd_loss(hn, h1, w_up_full, ffn_cw_full, w_down_full, final_norm[None, :],
                                                 target, tm)

    grads, sums, slots = {}, {}, {}

    def pair(*ws):
        return _Carry([_rs_pair(lay[n], "g_" + n, "t_" + n) for n in ws], reads={"g_" + n: grads[n] for n in ws},
                      fresh={"t_" + n: _theirs_shape(lay[n]) for n in ws})

    def chips(*ws, also=None):
        k = _Carry([_rs_chips(lay[n], "s_" + n, "r_" + n) for n in ws], reads={"s_" + n: sums[n] for n in ws},
                   fresh={"r_" + n: _slots_shape(lay[n]) for n in ws})
        if also is not None:
            k = _Carry(k.jobs + also.jobs, {**k.reads, **also.reads}, None, {**k.fresh, **also.fresh})
        return k

    def pair_sums(k, *ws):
        for n in ws:
            sums[n] = _pair_sum(grads[n], k.out["t_" + n], lay[n], pos, "pair_sum_" + n)

    def take_slots(k, *ws):
        for n in ws:
            slots[n] = k.out["r_" + n]

    du, dh1, g_fcw, g_g2 = _ffn_bwd(dh2, u, h1, w_up_full, ffn_cw_full, w_down_full, ffn_norm, tm)
    grads["w_down"] = _wgrad(act, dh2, D_FF // 2, D_MODEL, bk, "wgrad_down").reshape(lay["w_down"].whole())
    k4 = pair("w_down")
    grads["w_up"] = _wgrad(hn, du, D_MODEL, FF2 // 4, bk, "wgrad_up", comm=k4)
    pair_sums(k4, "w_down")
    k5 = chips("w_down", also=pair("w_up"))
    dattn, da, dcv, dc3, dgt, g_cw = _mix_bwd(dh1, gates, a, cv, c3, conv_full, w_ab, w_cb, w_out_full, tm, comm=k5)
    take_slots(k5, "w_down")
    pair_sums(k5, "w_up")
    k6 = chips("w_up")
    dq, dk, dv, g_sk = _attn_bwd(qkv, sk, attn, dattn, comm=k6)
    take_slots(k6, "w_up")
    grads["w_out"] = _wgrad(merged, dh1, D_MODEL, D_MODEL, bk, "wgrad_out").reshape(lay["w_out"].whole())
    grads["w_ab"] = _wgrad(attn, da, ATTN_W, D_MODEL, bk, "wgrad_ab")
    grads["w_cb"] = _wgrad(conv, dcv, CONV_W, D_MODEL, bk, "wgrad_cb")
    k7 = pair("w_out", "w_ab", "w_cb")
    grad_x, dproj, g_b, g_g1 = _inproj_bwd(dq, dk, dv, dc3, dgt, w_in_full, xs, dh1, mix_norm, tm2, comm=k7)
    pair_sums(k7, "w_out", "w_ab", "w_cb")
    k8 = chips("w_out", "w_ab", "w_cb")
    grads["w_in"] = _wgrad_in(xn, dproj, min(512, s), comm=k8)
    take_slots(k8, "w_out", "w_ab", "w_cb")
    sums["w_in"] = _pair_sum(
        grads["w_in"],
        _exchange("rs_pair_in", [[_rs_pair(lay["w_in"], "g", "t")]], reads={"g": grads["w_in"]},
                  fresh={"t": _theirs_shape(lay["w_in"])})["t"],
        lay["w_in"], pos, "pair_sum_w_in")
    slots["w_in"] = _exchange("rs_chips_in", [[_rs_chips(lay["w_in"], "s", "r")]], reads={"s": sums["w_in"]},
                              fresh={"r": _slots_shape(lay["w_in"])})["r"]
    halves = {n: _chip_sum(sums[n], slots[n], lay[n], pos, "chip_sum_" + n) for n in names}
    shared = _exchange("share_halves", [[_rs_share(lay[n], n) for n in names]], bufs=halves)
    big_g = [shared[n] for n in names]
    small = dict(loss=loss_part, g_g1=g_g1, g_b=g_b, g_sk=g_sk, g_cw=g_cw, g_g2=g_g2, g_fcw=g_fcw, g_fn=g_fn)
    big_new = [_rowwise(_adamw, [w, g, m, v], [F32, F32, F32], "adamw_" + n)
               for w, g, m, v, n in zip(big_w, big_g, big_m, big_v, names)]

    parts = [small["loss"], small["g_g1"], small["g_b"], jnp.pad(small["g_sk"][:, 0], (0, 120))[None, :],
             small["g_cw"], small["g_g2"], small["g_fcw"], small["g_fn"]]
    packed, at = _pack_rows([a.reshape(-1, 128) for a in parts])
    total = _exchange_small(packed, reduce=True)
    part = lambda k: total[at[k]:at[k] + parts[k].size // 128].reshape(parts[k].shape)
    loss = total[0, 0]
    g_mix, g_b, g_g2, g_fn = part(1), part(2), part(5), part(7)
    g_sk = part(3)[:, 0:N_HEADS]
    g_cw = lax.dynamic_slice(part(4), (0, me * 128), (3, 128))
    g_fcw = lax.dynamic_slice(part(6), (0, me * (FF2 // N_CHIPS)), (3, FF2 // N_CHIPS))
    small_p = [
        (mix_norm, g_mix, m_mix_norm, v_mix_norm), (b_in, g_b, m_b_in, v_b_in), (sinks, g_sk, m_sinks, v_sinks),
        (conv_w[0], g_cw, m_conv_w[0], v_conv_w[0]), (ffn_norm, g_g2, m_ffn_norm, v_ffn_norm),
        (ffn_conv_w[0], g_fcw, m_ffn_conv_w[0], v_ffn_conv_w[0]),
        (final_norm[None, :], g_fn, m_final_norm[None, :], v_final_norm[None, :])]
    small_new = _adamw_small(small_p)
    small_new = [small_new[3 * k:3 * k + 3] for k in range(len(small_p))]

    order = [("s", 0), ("b", 0), ("s", 1), ("s", 2), ("s", 3), ("b", 1), ("b", 2), ("b", 3), ("s", 4), ("b", 4),
             ("s", 5), ("b", 5), ("s", 6)]
    shapes = [mix_norm.shape, w_in.shape, b_in.shape, sinks.shape, conv_w.shape, w_attn_branch.shape,
              w_conv_branch.shape, w_out.shape, ffn_norm.shape, w_up.shape, ffn_conv_w.shape, w_down.shape,
              final_norm.shape]
    small_g = [p[1] for p in small_p]
    grads = [(small_g[k] if kind == "s" else big_g[k]).reshape(shp) for (kind, k), shp in zip(order, shapes)]
    news = [[(small_new[k][j] if kind == "s" else big_new[k][j]).reshape(shp) for (kind, k), shp in zip(order, shapes)]
            for j in range(3)]
    return (loss, grad_x[None], *grads, *news[0], *news[1], *news[2])
```

```python
import functools

import jax
import jax.numpy as jnp
from jax import lax
from jax.experimental import pallas as pl
from jax.experimental.pallas import tpu as pltpu

F32 = jnp.float32
BF16 = jnp.bfloat16

D_MODEL = 1024
HEAD_DIM = 64
N_HEADS = 8
N_KV_HEADS = 2
GROUP = N_HEADS // N_KV_HEADS
BLOCK = 128
ATTN_SCALE = HEAD_DIM ** -0.5
ATTN_W = N_HEADS * HEAD_DIM
KV_W = N_KV_HEADS * HEAD_DIM
CONV_W = 512
QKV_W = ATTN_W + 2 * KV_W
C3_W = 3 * CONV_W
GATES_W = 2 * D_MODEL
IN_W = QKV_W + C3_W + GATES_W
D_FF = 2816
FF2 = 2 * D_FF
NORM_EPS = 1e-5
N_CHIPS = 4
IN_SHARD = IN_W // N_CHIPS
NEG = -1e30

ADAM_LR = 0.001
ADAM_B1 = 0.9
ADAM_B2 = 0.999
ADAM_EPS = 1e-08
ADAM_WD = 0.01
ADAM_STEP = 10

VMEM_LIMIT = 56 * 1024 * 1024
MESH = pl.DeviceIdType.MESH

NT = (((1,), (1,)), ((), ()))
TN = (((0,), (0,)), ((), ()))


def _params(*sem):
    return pltpu.CompilerParams(dimension_semantics=sem, vmem_limit_bytes=VMEM_LIMIT)


def _resident(shape):
    return pl.BlockSpec(shape, lambda *_: (0,) * len(shape), pipeline_mode=pl.Buffered(1))


def _sigmoid(v):
    return 0.5 * jnp.tanh(0.5 * v) + 0.5


def _rstd(v):
    return lax.rsqrt(jnp.mean(v * v, axis=-1, keepdims=True) + NORM_EPS)


def _rms_bwd(dy, v, rstd, g):
    vhat = v * rstd
    t = dy * g
    return rstd * (t - vhat * jnp.mean(t * vhat, axis=-1, keepdims=True)), dy * vhat


def _taps(z, cw):
    return cw[2:3] * z + cw[1:2] * pltpu.roll(z, 1, 0) + cw[0:1] * pltpu.roll(z, 2, 0)


def _causal_conv(z, prev, cw):
    edge = _taps(jnp.concatenate([prev, z[0:8]], axis=0), cw)
    return jnp.concatenate([edge[8:16], _taps(z, cw)[8:]], axis=0)


def _rows_after(z, nxt):
    n = z.shape[0]
    edge = jnp.concatenate([z[n - 8:n], nxt], axis=0)
    return tuple(jnp.concatenate([pltpu.roll(z, n - k, 0)[:n - 8], pltpu.roll(edge, 16 - k, 0)[0:8]], axis=0)
                 for k in (1, 2))


def _inproj_fwd(x, g1, w_in, b_in, tm, comm=None):
    s = x.shape[0]

    def body(x_ref, g_ref, w_ref, b_ref, xn_ref, qkv_ref, c3_ref, gt_ref):
        xf = x_ref[...]
        xn = (xf * _rstd(xf) * g_ref[...]).astype(BF16)
        xn_ref[...] = xn

        def seg(a, b):
            return lax.dot_general(xn, w_ref[a:b, :], NT, preferred_element_type=F32) + b_ref[:, a:b]

        qkv_ref[...] = seg(0, QKV_W).astype(BF16)
        c3_ref[...] = seg(QKV_W, QKV_W + C3_W)
        gt_ref[...] = seg(QKV_W + C3_W, IN_W)

    row = lambda w: pl.BlockSpec((tm, w), lambda i: (i, 0))
    return _call(
        comm, body, name="inproj_fwd", grid=(s // tm,),
        in_specs=[row(D_MODEL), _resident((1, D_MODEL)), _resident((IN_W, D_MODEL)), _resident((1, IN_W))],
        out_specs=[row(D_MODEL), row(QKV_W), row(C3_W), row(GATES_W)],
        out_shape=[jax.ShapeDtypeStruct((s, D_MODEL), BF16), jax.ShapeDtypeStruct((s, QKV_W), BF16),
                   jax.ShapeDtypeStruct((s, C3_W), F32), jax.ShapeDtypeStruct((s, GATES_W), F32)],
        compiler_params=_params("parallel"),
    )(x, g1, w_in, b_in)


def _attn_mask(first_block):
    qi = lax.broadcasted_iota(jnp.int32, (GROUP * BLOCK, 2 * BLOCK), 0) & (BLOCK - 1)
    kj = lax.broadcasted_iota(jnp.int32, (GROUP * BLOCK, 2 * BLOCK), 1)
    band = (kj > qi) & (kj <= qi + BLOCK)
    return band & ((kj >= BLOCK) | jnp.logical_not(first_block))


def _sink_column(sk_ref, h):
    rows = lax.broadcasted_iota(jnp.int32, (GROUP * BLOCK, 1), 0)
    col = jnp.full((GROUP * BLOCK, 1), sk_ref[h * GROUP], F32)
    for g in range(1, GROUP):
        col = jnp.where(rows >= g * BLOCK, sk_ref[h * GROUP + g], col)
    return col


def _stack_heads(t, h):
    return jnp.concatenate(
        [t[:, (h * GROUP + g) * HEAD_DIM:(h * GROUP + g + 1) * HEAD_DIM] for g in range(GROUP)], axis=0)


def _unstack_heads(per_kv):
    return jnp.concatenate(
        [t[g * BLOCK:(g + 1) * BLOCK] for t in per_kv for g in range(GROUP)], axis=1)


def _attn_specs(nb):
    cur = lambda i: jnp.minimum(i, nb - 1)
    prev = lambda i: jnp.maximum(jnp.minimum(i, nb - 1) - 1, 0)
    q = pl.BlockSpec((BLOCK, ATTN_W), lambda i: (cur(i), 0))
    kp = pl.BlockSpec((BLOCK, KV_W), lambda i: (prev(i), ATTN_W // KV_W))
    kc = pl.BlockSpec((BLOCK, KV_W), lambda i: (cur(i), ATTN_W // KV_W))
    vp = pl.BlockSpec((BLOCK, KV_W), lambda i: (prev(i), ATTN_W // KV_W + 1))
    vc = pl.BlockSpec((BLOCK, KV_W), lambda i: (cur(i), ATTN_W // KV_W + 1))
    return q, kp, kc, vp, vc


def _attn_fwd(qkv, sinks, comm=None):
    s = qkv.shape[0]
    nb = s // BLOCK

    def body(sk_ref, q_ref, kp_ref, kc_ref, vp_ref, vc_ref, o_ref):
        mask = _attn_mask(pl.program_id(0) == 0)
        q, kp, kc, vp, vc = q_ref[...], kp_ref[...], kc_ref[...], vp_ref[...], vc_ref[...]
        outs = []
        for h in range(N_KV_HEADS):
            hs = slice(h * HEAD_DIM, (h + 1) * HEAD_DIM)
            k2 = jnp.concatenate([kp[:, hs], kc[:, hs]], axis=0)
            v2 = jnp.concatenate([vp[:, hs], vc[:, hs]], axis=0)
            sc = lax.dot_general(_stack_heads(q, h), k2, NT, preferred_element_type=F32) * ATTN_SCALE
            sc = jnp.where(mask, sc, NEG)
            sink = _sink_column(sk_ref, h)
            m = jnp.maximum(jnp.max(sc, axis=1, keepdims=True), sink)
            p = jnp.exp(sc - m)
            den = jnp.sum(p, axis=1, keepdims=True) + jnp.exp(sink - m)
            outs.append(jnp.dot(p.astype(BF16), v2, preferred_element_type=F32) / den)
        o_ref[...] = _unstack_heads(outs).astype(BF16)

    return _call(
        comm, body, name="attn_fwd", grid=(nb,),
        in_specs=[pl.BlockSpec(memory_space=pltpu.SMEM), *_attn_specs(nb)],
        out_specs=pl.BlockSpec((BLOCK, ATTN_W), lambda i: (i, 0)),
        out_shape=jax.ShapeDtypeStruct((s, ATTN_W), BF16),
        compiler_params=_params("parallel"),
    )(sinks, qkv, qkv, qkv, qkv, qkv)


def _mix_fwd(x, attn, c3, gates, conv_w, w_ab, w_cb, w_out, g2, tm, comm=None):
    s = x.shape[0]

    def body(x_ref, at_ref, c3_ref, gt_ref, cw_ref, wab_ref, wcb_ref, wo_ref, g_ref,
             conv_ref, a_ref, cv_ref, mg_ref, h1_ref, hn_ref, carry_ref):
        @pl.when(pl.program_id(0) == 0)
        def _():
            carry_ref[...] = jnp.zeros_like(carry_ref)

        c3v = c3_ref[...]
        cb, cc, cx = c3v[:, :CONV_W], c3v[:, CONV_W:2 * CONV_W], c3v[:, 2 * CONV_W:]
        z = cc * cx
        cz = _causal_conv(z, carry_ref[...], cw_ref[...])
        carry_ref[...] = z[tm - 8:tm]
        conv = (cb * cz).astype(BF16)
        conv_ref[...] = conv
        a = jnp.dot(at_ref[...], wab_ref[...], preferred_element_type=F32)
        cv = jnp.dot(conv, wcb_ref[...], preferred_element_type=F32)
        a_ref[...] = a.astype(BF16)
        cv_ref[...] = cv.astype(BF16)
        gt = gt_ref[...]
        merged = (_sigmoid(gt[:, :D_MODEL]) * a + _sigmoid(gt[:, D_MODEL:]) * cv).astype(BF16)
        mg_ref[...] = merged
        h1 = x_ref[...] + jnp.dot(merged, wo_ref[...], preferred_element_type=F32)
        h1_ref[...] = h1
        hn_ref[...] = (h1 * _rstd(h1) * g_ref[...]).astype(BF16)

    row = lambda w: pl.BlockSpec((tm, w), lambda i: (i, 0))
    return _call(
        comm, body, name="mix_fwd", grid=(s // tm,),
        in_specs=[row(D_MODEL), row(ATTN_W), row(C3_W), row(GATES_W), _resident((3, CONV_W)),
                  _resident((ATTN_W, D_MODEL)), _resident((CONV_W, D_MODEL)), _resident((D_MODEL, D_MODEL)),
                  _resident((1, D_MODEL))],
        out_specs=[row(CONV_W), row(D_MODEL), row(D_MODEL), row(D_MODEL), row(D_MODEL), row(D_MODEL)],
        out_shape=[jax.ShapeDtypeStruct((s, CONV_W), BF16), jax.ShapeDtypeStruct((s, D_MODEL), BF16),
                   jax.ShapeDtypeStruct((s, D_MODEL), BF16), jax.ShapeDtypeStruct((s, D_MODEL), BF16),
                   jax.ShapeDtypeStruct((s, D_MODEL), F32), jax.ShapeDtypeStruct((s, D_MODEL), BF16)],
        scratch_shapes=[pltpu.VMEM((8, CONV_W), F32)],
        compiler_params=_params("arbitrary"),
    )(x, attn, c3, gates, conv_w, w_ab, w_cb, w_out, g2)


def _ffn_fwd_loss(hn, h1, w_up, ffn_cw, w_down, g3, target, tm):
    s = hn.shape[0]

    def body(hn_ref, h1_ref, wu_ref, cw_ref, wd_ref, g_ref, t_ref,
             u_ref, up_ref, act_ref, dh2_ref, loss_ref, gfn_ref, carry_ref):
        @pl.when(pl.program_id(0) == 0)
        def _():
            carry_ref[...] = jnp.zeros_like(carry_ref)
            loss_ref[...] = jnp.zeros_like(loss_ref)
            gfn_ref[...] = jnp.zeros_like(gfn_ref)

        u = jnp.dot(hn_ref[...], wu_ref[...], preferred_element_type=F32)
        u_ref[...] = u.astype(BF16)
        up = _causal_conv(u, carry_ref[...], cw_ref[...])
        up_ref[...] = up.astype(BF16)
        carry_ref[...] = u[tm - 8:tm]
        gate, val = up[:, :D_FF], up[:, D_FF:]
        act = (gate * _sigmoid(gate) * val).astype(BF16)
        act_ref[...] = act
        h2 = h1_ref[...] + jnp.dot(act, wd_ref[...], preferred_element_type=F32)
        rstd = _rstd(h2)
        g = g_ref[...]
        err = h2 * rstd * g - t_ref[...]
        loss_ref[...] += jnp.sum(err * err) * (0.5 / D_MODEL)
        dh2, dg = _rms_bwd(err * (1.0 / D_MODEL), h2, rstd, g)
        dh2_ref[...] = dh2
        gfn_ref[...] += jnp.sum(dg, axis=0, keepdims=True)

    row = lambda w: pl.BlockSpec((tm, w), lambda i: (i, 0))
    acc = lambda w: pl.BlockSpec((1, w), lambda i: (0, 0))
    return pl.pallas_call(
        body, name="ffn_fwd_loss", grid=(s // tm,),
        in_specs=[row(D_MODEL), row(D_MODEL), _resident((D_MODEL, FF2)), _resident((3, FF2)),
                  _resident((D_FF, D_MODEL)), _resident((1, D_MODEL)), row(D_MODEL)],
        out_specs=[row(FF2), row(FF2), row(D_FF), row(D_MODEL), acc(128), acc(D_MODEL)],
        out_shape=[jax.ShapeDtypeStruct((s, FF2), BF16), jax.ShapeDtypeStruct((s, FF2), BF16),
                   jax.ShapeDtypeStruct((s, D_FF), BF16),
                   jax.ShapeDtypeStruct((s, D_MODEL), F32), jax.ShapeDtypeStruct((1, 128), F32),
                   jax.ShapeDtypeStruct((1, D_MODEL), F32)],
        scratch_shapes=[pltpu.VMEM((8, FF2), F32)],
        compiler_params=_params("arbitrary"),
    )(hn, h1, w_up, ffn_cw, w_down, g3, target)


def _ffn_bwd(dh2, u, up, h1, w_up, ffn_cw, w_down, g2, tm):
    s = dh2.shape[0]
    nt = s // tm

    def body(dh2_ref, u_ref, up_ref, h1_ref, wu_ref, cw_ref, wd_ref, g_ref,
             du_ref, dh1_ref, gcw_ref, gg_ref, carry_ref):
        @pl.when(pl.program_id(0) == 0)
        def _():
            carry_ref[...] = jnp.zeros_like(carry_ref)
            gcw_ref[...] = jnp.zeros_like(gcw_ref)
            gg_ref[...] = jnp.zeros_like(gg_ref)

        dh2v = dh2_ref[...]
        dact = lax.dot_general(dh2v.astype(BF16), wd_ref[...], NT, preferred_element_type=F32)
        upv = up_ref[...].astype(F32)
        gate, val = upv[:, :D_FF], upv[:, D_FF:]
        sg = _sigmoid(gate)
        dval = dact * (gate * sg)
        dgate = dact * val * (sg * (1.0 + gate * (1.0 - sg)))
        dup = jnp.concatenate([dgate, dval], axis=1)
        dup1, dup2 = _rows_after(dup, carry_ref[...])
        carry_ref[...] = dup[0:8]
        u = u_ref[...].astype(F32)
        gcw_ref[2:3, :] += jnp.sum(dup * u, axis=0, keepdims=True)
        gcw_ref[1:2, :] += jnp.sum(dup1 * u, axis=0, keepdims=True)
        gcw_ref[0:1, :] += jnp.sum(dup2 * u, axis=0, keepdims=True)
        cw = cw_ref[...]
        du = (cw[2:3] * dup + cw[1:2] * dup1 + cw[0:1] * dup2).astype(BF16)
        du_ref[...] = du
        dhn = lax.dot_general(du, wu_ref[...], NT, preferred_element_type=F32)
        h1v = h1_ref[...]
        dh1, dg = _rms_bwd(dhn, h1v, _rstd(h1v), g_ref[...])
        dh1_ref[...] = dh2v + dh1
        gg_ref[...] += jnp.sum(dg, axis=0, keepdims=True)

    row = lambda w: pl.BlockSpec((tm, w), lambda i: (nt - 1 - i, 0))
    return pl.pallas_call(
        body, name="ffn_bwd", grid=(nt,),
        in_specs=[row(D_MODEL), row(FF2), row(FF2),
                  row(D_MODEL), _resident((D_MODEL, FF2)), _resident((3, FF2)), _resident((D_FF, D_MODEL)),
                  _resident((1, D_MODEL))],
        out_specs=[row(FF2), row(D_MODEL), pl.BlockSpec((3, FF2), lambda i: (0, 0)),
                   pl.BlockSpec((1, D_MODEL), lambda i: (0, 0))],
        out_shape=[jax.ShapeDtypeStruct((s, FF2), BF16), jax.ShapeDtypeStruct((s, D_MODEL), F32),
                   jax.ShapeDtypeStruct((3, FF2), F32), jax.ShapeDtypeStruct((1, D_MODEL), F32)],
        scratch_shapes=[pltpu.VMEM((8, FF2), F32)],
        compiler_params=_params("arbitrary"),
    )(dh2, u, up, h1, w_up, ffn_cw, w_down, g2)


def _mix_bwd(dh1, gates, a, cv, c3, conv_w, w_ab, w_cb, w_out, tm, comm=None):
    s = dh1.shape[0]
    nt = s // tm
    halo = 8

    def body(dh1_ref, gt_ref, a_ref, cv_ref, c3_ref, ch_ref, cw_ref, wab_ref, wcb_ref, wo_ref,
             dat_ref, da_ref, dcv_ref, dc3_ref, dgt_ref, gcw_ref, carry_ref):
        i = pl.program_id(0)

        @pl.when(i == 0)
        def _():
            carry_ref[...] = jnp.zeros_like(carry_ref)
            gcw_ref[...] = jnp.zeros_like(gcw_ref)

        dm = lax.dot_general(dh1_ref[...].astype(BF16), wo_ref[...], NT, preferred_element_type=F32)
        gt = gt_ref[...]
        sa, sc = _sigmoid(gt[:, :D_MODEL]), _sigmoid(gt[:, D_MODEL:])
        da = (dm * sa).astype(BF16)
        dcv = (dm * sc).astype(BF16)
        da_ref[...] = da
        dcv_ref[...] = dcv
        dgt_ref[...] = jnp.concatenate(
            [dm * a_ref[...].astype(F32) * (sa * (1.0 - sa)), dm * cv_ref[...].astype(F32) * (sc * (1.0 - sc))],
            axis=1).astype(BF16)
        dat_ref[...] = lax.dot_general(da, wab_ref[...], NT, preferred_element_type=F32).astype(BF16)
        dconv = lax.dot_general(dcv, wcb_ref[...], NT, preferred_element_type=F32)
        c3v = c3_ref[...]
        cb, cc, cx = c3v[:, :CONV_W], c3v[:, CONV_W:2 * CONV_W], c3v[:, 2 * CONV_W:]
        z = cc * cx
        chv = ch_ref[...] * (i < nt - 1).astype(F32)
        zh = chv[:, CONV_W:2 * CONV_W] * chv[:, 2 * CONV_W:]
        cw = cw_ref[...]
        cz = _causal_conv(z, zh, cw)
        dcz = dconv * cb
        dcz1, dcz2 = _rows_after(dcz, carry_ref[...])
        carry_ref[...] = dcz[0:8]
        gcw_ref[2:3, :] += jnp.sum(dcz * z, axis=0, keepdims=True)
        gcw_ref[1:2, :] += jnp.sum(dcz1 * z, axis=0, keepdims=True)
        gcw_ref[0:1, :] += jnp.sum(dcz2 * z, axis=0, keepdims=True)
        dz = cw[2:3] * dcz + cw[1:2] * dcz1 + cw[0:1] * dcz2
        dc3_ref[...] = jnp.concatenate([dconv * cz, dz * cx, dz * cc], axis=1).astype(BF16)

    row = lambda w: pl.BlockSpec((tm, w), lambda i: (nt - 1 - i, 0))
    return _call(
        comm, body, name="mix_bwd", grid=(nt,),
        in_specs=[row(D_MODEL), row(GATES_W), row(D_MODEL), row(D_MODEL), row(C3_W),
                  pl.BlockSpec((halo, C3_W), lambda i: (jnp.maximum((nt - 1 - i) * (tm // halo) - 1, 0), 0)),
                  _resident((3, CONV_W)), _resident((ATTN_W, D_MODEL)), _resident((CONV_W, D_MODEL)),
                  _resident((D_MODEL, D_MODEL))],
        out_specs=[row(ATTN_W), row(D_MODEL), row(D_MODEL), row(C3_W), row(GATES_W),
                   pl.BlockSpec((3, CONV_W), lambda i: (0, 0))],
        out_shape=[jax.ShapeDtypeStruct((s, ATTN_W), BF16), jax.ShapeDtypeStruct((s, D_MODEL), BF16),
                   jax.ShapeDtypeStruct((s, D_MODEL), BF16), jax.ShapeDtypeStruct((s, C3_W), BF16),
                   jax.ShapeDtypeStruct((s, GATES_W), BF16), jax.ShapeDtypeStruct((3, CONV_W), F32)],
        scratch_shapes=[pltpu.VMEM((8, CONV_W), F32)],
        compiler_params=_params("arbitrary"),
    )(dh1, gates, a, cv, c3, c3, conv_w, w_ab, w_cb, w_out)


def _attn_bwd(qkv, sinks, o, do, comm=None):
    s = qkv.shape[0]
    nb = s // BLOCK

    def body(sk_ref, q_ref, kp_ref, kc_ref, vp_ref, vc_ref, o_ref, do_ref,
             dq_ref, dk_ref, dv_ref, dsk_ref, ck_ref, cvv_ref):
        i = pl.program_id(0)

        @pl.when(i == 0)
        def _():
            ck_ref[...] = jnp.zeros_like(ck_ref)
            cvv_ref[...] = jnp.zeros_like(cvv_ref)
            dsk_ref[...] = jnp.zeros_like(dsk_ref)

        @pl.when(i < nb)
        def _():
            mask = _attn_mask(i == 0)
            q, kp, kc, vp, vc = q_ref[...], kp_ref[...], kc_ref[...], vp_ref[...], vc_ref[...]
            ov, dov = o_ref[...], do_ref[...]
            dqs, dks, dvs = [], [], []
            for h in range(N_KV_HEADS):
                hs = slice(h * HEAD_DIM, (h + 1) * HEAD_DIM)
                k2 = jnp.concatenate([kp[:, hs], kc[:, hs]], axis=0)
                v2 = jnp.concatenate([vp[:, hs], vc[:, hs]], axis=0)
                qg, og, dog = _stack_heads(q, h), _stack_heads(ov, h), _stack_heads(dov, h)
                sc = lax.dot_general(qg, k2, NT, preferred_element_type=F32) * ATTN_SCALE
                sc = jnp.where(mask, sc, NEG)
                sink = _sink_column(sk_ref, h)
                m = jnp.maximum(jnp.max(sc, axis=1, keepdims=True), sink)
                p = jnp.exp(sc - m)
                psink = jnp.exp(sink - m)
                inv = 1.0 / (jnp.sum(p, axis=1, keepdims=True) + psink)
                p = p * inv
                delta = jnp.sum(dog.astype(F32) * og.astype(F32), axis=1, keepdims=True)
                dp = lax.dot_general(dog, v2, NT, preferred_element_type=F32)
                ds = (p * (dp - delta)).astype(BF16)
                dqs.append(jnp.dot(ds, k2, preferred_element_type=F32) * ATTN_SCALE)
                dks.append(lax.dot_general(ds, qg, TN, preferred_element_type=F32) * ATTN_SCALE)
                dvs.append(lax.dot_general(p.astype(BF16), dog, TN, preferred_element_type=F32))
                dsink = -(psink * inv * delta)
                for g in range(GROUP):
                    r = h * GROUP + g
                    dsk_ref[r:r + 1, :] += jnp.sum(dsink[g * BLOCK:(g + 1) * BLOCK])
            dq_ref[...] = _unstack_heads(dqs).astype(BF16)
            dk2 = jnp.concatenate(dks, axis=1)
            dv2 = jnp.concatenate(dvs, axis=1)
            dk_ref[...] = (ck_ref[...] + dk2[:BLOCK]).astype(BF16)
            dv_ref[...] = (cvv_ref[...] + dv2[:BLOCK]).astype(BF16)
            ck_ref[...] = dk2[BLOCK:]
            cvv_ref[...] = dv2[BLOCK:]

        @pl.when(i == nb)
        def _():
            dk_ref[...] = ck_ref[...].astype(BF16)
            dv_ref[...] = cvv_ref[...].astype(BF16)

    cur = lambda i: jnp.minimum(i, nb - 1)
    done = lambda i: jnp.maximum(i - 1, 0)
    return _call(
        comm, body, name="attn_bwd", grid=(nb + 1,),
        in_specs=[pl.BlockSpec(memory_space=pltpu.SMEM), *_attn_specs(nb),
                  pl.BlockSpec((BLOCK, ATTN_W), lambda i: (cur(i), 0)),
                  pl.BlockSpec((BLOCK, ATTN_W), lambda i: (cur(i), 0))],
        out_specs=[pl.BlockSpec((BLOCK, ATTN_W), lambda i: (cur(i), 0)),
                   pl.BlockSpec((BLOCK, KV_W), lambda i: (done(i), 0)),
                   pl.BlockSpec((BLOCK, KV_W), lambda i: (done(i), 0)),
                   pl.BlockSpec((N_HEADS, 128), lambda i: (0, 0))],
        out_shape=[jax.ShapeDtypeStruct((s, ATTN_W), BF16), jax.ShapeDtypeStruct((s, KV_W), BF16),
                   jax.ShapeDtypeStruct((s, KV_W), BF16), jax.ShapeDtypeStruct((N_HEADS, 128), F32)],
        scratch_shapes=[pltpu.VMEM((BLOCK, KV_W), F32), pltpu.VMEM((BLOCK, KV_W), F32)],
        compiler_params=_params("arbitrary"),
    )(sinks, qkv, qkv, qkv, qkv, qkv, o, do)


def _inproj_bwd(dq, dk, dv, dc3, dgt, w_in, x, dh1, g1, tm, comm=None):
    s = x.shape[0]

    def body(dq_ref, dk_ref, dv_ref, dc3_ref, dgt_ref, w_ref, x_ref, dh1_ref, g_ref,
             dx_ref, dp_ref, gb_ref, gg_ref):
        @pl.when(pl.program_id(0) == 0)
        def _():
            gb_ref[...] = jnp.zeros_like(gb_ref)
            gg_ref[...] = jnp.zeros_like(gg_ref)

        dp = jnp.concatenate([dq_ref[...], dk_ref[...], dv_ref[...], dc3_ref[...], dgt_ref[...]], axis=1)
        dp_ref[...] = dp
        gb_ref[...] += jnp.sum(dp.astype(F32), axis=0, keepdims=True)
        dxn = jnp.dot(dp, w_ref[...], preferred_element_type=F32)
        xf = x_ref[...]
        dx, dg = _rms_bwd(dxn, xf, _rstd(xf), g_ref[...])
        dx_ref[...] = dh1_ref[...] + dx
        gg_ref[...] += jnp.sum(dg, axis=0, keepdims=True)

    row = lambda w: pl.BlockSpec((tm, w), lambda i: (i, 0))
    acc = lambda w: pl.BlockSpec((1, w), lambda i: (0, 0))
    return _call(
        comm, body, name="inproj_bwd", grid=(s // tm,),
        in_specs=[row(ATTN_W), row(KV_W), row(KV_W), row(C3_W), row(GATES_W), _resident((IN_W, D_MODEL)),
                  row(D_MODEL), row(D_MODEL), _resident((1, D_MODEL))],
        out_specs=[row(D_MODEL), row(IN_W), acc(IN_W), acc(D_MODEL)],
        out_shape=[jax.ShapeDtypeStruct((s, D_MODEL), F32), jax.ShapeDtypeStruct((s, IN_W), BF16),
                   jax.ShapeDtypeStruct((1, IN_W), F32), jax.ShapeDtypeStruct((1, D_MODEL), F32)],
        compiler_params=_params("arbitrary"),
    )(dq, dk, dv, dc3, dgt, w_in, x, dh1, g1)


def _wgrad(a, b, bm, bn, bk, name, comm=None):
    s, m = a.shape
    n = b.shape[1]
    nk = s // bk

    def body(a_ref, b_ref, o_ref, acc_ref):
        k = pl.program_id(2)

        @pl.when(k == 0)
        def _():
            acc_ref[...] = jnp.zeros_like(acc_ref)

        acc_ref[...] += lax.dot_general(a_ref[...].astype(BF16), b_ref[...].astype(BF16), TN,
                                        preferred_element_type=F32)

        @pl.when(k == nk - 1)
        def _():
            o_ref[...] = acc_ref[...].astype(BF16)

    return _call(
        comm, body, name=name, grid=(m // bm, n // bn, nk),
        in_specs=[pl.BlockSpec((bk, bm), lambda i, j, k: (k, i)), pl.BlockSpec((bk, bn), lambda i, j, k: (k, j))],
        out_specs=pl.BlockSpec((bm, bn), lambda i, j, k: (i, j)),
        out_shape=jax.ShapeDtypeStruct((m, n), BF16),
        scratch_shapes=[pltpu.VMEM((bm, bn), F32)],
        compiler_params=_params("parallel", "parallel", "arbitrary"),
    )(a, b)


def _wgrad_in(xn, dproj, bk, comm=None):
    s = xn.shape[0]
    nk = s // bk

    def body(a_ref, b_ref, o_ref, acc_ref):
        k = pl.program_id(0)

        @pl.when(k == 0)
        def _():
            acc_ref[...] = jnp.zeros_like(acc_ref)

        acc_ref[...] += lax.dot_general(b_ref[...], a_ref[...], TN, preferred_element_type=F32)

        @pl.when(k == nk - 1)
        def _():
            o_ref[...] = acc_ref[...].astype(BF16)

    return _call(
        comm, body, name="wgrad_in", grid=(nk,),
        in_specs=[pl.BlockSpec((bk, D_MODEL), lambda k: (k, 0)), pl.BlockSpec((bk, IN_W), lambda k: (k, 0))],
        out_specs=_resident((IN_W, D_MODEL)),
        out_shape=jax.ShapeDtypeStruct((IN_W, D_MODEL), BF16),
        scratch_shapes=[pltpu.VMEM((IN_W, D_MODEL), F32)],
        compiler_params=_params("arbitrary"),
    )(xn, dproj)


class _Carry:
    def __init__(self, jobs, reads=None, bufs=None, fresh=None):
        self.jobs, self.reads, self.bufs, self.fresh = jobs, reads or {}, bufs or {}, fresh or {}
        self.out = {}


class _Job:
    def __init__(self, n_sems, plan):
        self.n_sems, self.plan = n_sems, plan


def _plan_all(jobs, hbm, send, recv):
    pos = _position()
    starts, waits, base = [], [], 0
    for job in jobs:
        s, w = job.plan(hbm, pos, send, recv, base)
        starts, waits, base = starts + s, waits + w, base + job.n_sems
    return starts, waits


def _call(comm, body, **kw):
    if comm is None:
        return pl.pallas_call(body, **kw)
    grid = kw["grid"]
    single = not isinstance(kw["out_shape"], (list, tuple))
    out_shape = [kw["out_shape"]] if single else list(kw["out_shape"])
    out_specs = [kw["out_specs"]] if single else list(kw["out_specs"])
    in_specs = list(kw["in_specs"])
    scratch = list(kw.get("scratch_shapes", ()))
    r_names, b_names, f_names = list(comm.reads), list(comm.bufs), list(comm.fresh)
    n_args, n_out, n_scr = len(in_specs), len(out_shape), len(scratch)
    n_sems = sum(j.n_sems for j in comm.jobs)

    def wrapped(*refs):
        k = n_args
        hbm = dict(zip(r_names, refs[k:k + len(r_names)]))
        k += len(r_names) + len(b_names)
        outs = refs[k:k + n_out]
        k += n_out
        hbm.update(zip(b_names + f_names, refs[k:k + len(b_names) + len(f_names)]))
        k += len(b_names) + len(f_names)
        send, recv = refs[k + n_scr:]
        starts, waits = _plan_all(comm.jobs, hbm, send, recv)
        ids = [pl.program_id(a) for a in range(len(grid))]
        first = functools.reduce(jnp.logical_and, [i == 0 for i in ids])
        last = functools.reduce(jnp.logical_and, [i == g - 1 for i, g in zip(ids, grid)])

        @pl.when(first)
        def _():
            for cp in starts:
                cp.start()

        body(*refs[:n_args], *outs, *refs[k:k + n_scr])

        @pl.when(last)
        def _():
            for cp in waits:
                cp.wait_recv()
            for cp in starts:
                cp.wait_send()

    sems = pltpu.SemaphoreType.DMA((n_sems,))
    held = [jax.ShapeDtypeStruct(a.shape, a.dtype) for a in comm.bufs.values()] + list(comm.fresh.values())
    call = pl.pallas_call(
        wrapped, name=kw["name"], grid=grid,
        in_specs=in_specs + [_ANY] * (len(r_names) + len(b_names)),
        out_specs=out_specs + [_ANY] * len(held),
        out_shape=out_shape + held,
        input_output_aliases={n_args + len(r_names) + i: n_out + i for i in range(len(b_names))},
        scratch_shapes=scratch + [sems, sems],
        compiler_params=_params(*["arbitrary"] * len(grid)),
    )

    def run(*args):
        res = call(*args, *comm.reads.values(), *comm.bufs.values())
        comm.out = dict(zip(b_names + f_names, res[n_out:]))
        return res[0] if single else res[:n_out]

    return run


def _exchange(name, phases, reads=None, bufs=None, fresh=None):
    comm = _Carry([j for ph in phases for j in ph], reads, bufs, fresh)
    r_names, b_names, f_names = list(comm.reads), list(comm.bufs), list(comm.fresh)
    n_sems = sum(j.n_sems for j in comm.jobs)

    def body(*refs):
        hbm = dict(zip(r_names, refs[:len(r_names)]))
        k = len(r_names) + len(b_names)
        hbm.update(zip(b_names + f_names, refs[k:k + len(b_names) + len(f_names)]))
        send, recv = refs[-2:]
        pos = _position()
        started, base = [], 0
        for ph in phases:
            waits = []
            for job in ph:
                s, w = job.plan(hbm, pos, send, recv, base)
                base += job.n_sems
                for cp in s:
                    cp.start()
                started, waits = started + s, waits + w
            for cp in waits:
                cp.wait_recv()
        for cp in started:
            cp.wait_send()

    sems = pltpu.SemaphoreType.DMA((n_sems,))
    held = [jax.ShapeDtypeStruct(a.shape, a.dtype) for a in comm.bufs.values()] + list(comm.fresh.values())
    res = pl.pallas_call(
        body, name=name, in_specs=[_ANY] * (len(r_names) + len(b_names)), out_specs=[_ANY] * len(held),
        out_shape=held, input_output_aliases={len(r_names) + i: i for i in range(len(b_names))},
        scratch_shapes=[sems, sems],
    )(*comm.reads.values(), *comm.bufs.values())
    return dict(zip(b_names + f_names, res))


def _row_tile(rows, bytes_per_row):
    best = 16
    for t in range(16, rows + 1, 16):
        if rows % t == 0 and t * bytes_per_row <= 6 * 1024 * 1024:
            best = t
    return best


def _rowwise(fn, ins, out_dtypes, name):
    rows, cols = ins[0].shape[-2:]
    per_row = sum(a.size // rows * a.dtype.itemsize for a in ins) + sum(cols * jnp.dtype(d).itemsize for d in out_dtypes)
    tr = _row_tile(rows, per_row)
    n_in = len(ins)

    def body(*refs):
        outs = fn(*[r[...] for r in refs[:n_in]])
        for o_ref, o in zip(refs[n_in:], outs):
            o_ref[...] = o.astype(o_ref.dtype)

    def spec(a):
        if a.ndim == 3:
            return pl.BlockSpec((a.shape[0], tr, cols), lambda i: (0, i, 0))
        return pl.BlockSpec((tr, cols), lambda i: (i, 0))

    return pl.pallas_call(
        body, name=name, grid=(rows // tr,),
        in_specs=[spec(a) for a in ins],
        out_specs=[pl.BlockSpec((tr, cols), lambda i: (i, 0)) for _ in out_dtypes],
        out_shape=[jax.ShapeDtypeStruct((rows, cols), d) for d in out_dtypes],
        compiler_params=_params("parallel"),
    )(*ins)


def _tiled(fn, name, grid, pos, ins, outs):
    n_in = len(ins)

    def body(pos_ref, *refs):
        res = fn(*[r[...] for r in refs[:n_in]])
        for o_ref, o in zip(refs[n_in:], res):
            o_ref[...] = o.astype(o_ref.dtype)

    return pl.pallas_call(
        body, name=name,
        grid_spec=pltpu.PrefetchScalarGridSpec(
            num_scalar_prefetch=1, grid=grid,
            in_specs=[pl.BlockSpec(bs, im) for _, bs, im in ins],
            out_specs=[pl.BlockSpec(bs, im) for _, _, bs, im in outs]),
        out_shape=[jax.ShapeDtypeStruct(s, d) for s, d, _, _ in outs],
        compiler_params=_params("parallel"),
    )(pos, *[a for a, _, _ in ins])


def _adamw(w, g, m, v):
    m = ADAM_B1 * m + (1.0 - ADAM_B1) * g
    v = ADAM_B2 * v + (1.0 - ADAM_B2) * (g * g)
    m_hat = m / (1.0 - ADAM_B1 ** ADAM_STEP)
    v_hat = v / (1.0 - ADAM_B2 ** ADAM_STEP)
    return -ADAM_LR * (m_hat / (jnp.sqrt(v_hat) + ADAM_EPS) + ADAM_WD * w), m, v


def _adamw_small(params):
    n = len(params)

    def body(*refs):
        for k in range(n):
            w, g, m, v = (r[...] for r in refs[4 * k:4 * k + 4])
            for o_ref, o in zip(refs[4 * n + 3 * k:4 * n + 3 * k + 3], _adamw(w, g, m, v)):
                o_ref[...] = o

    flat = [a for p in params for a in p]
    return pl.pallas_call(
        body, name="adamw_small",
        out_shape=[jax.ShapeDtypeStruct(p[0].shape, F32) for p in params for _ in range(3)],
    )(*flat)


class _Layout:
    def __init__(self, rows, cols, stacked):
        self.rows, self.cols, self.stacked = rows, cols, stacked

    def whole(self, rows=None):
        r = self.rows if rows is None else rows
        return (N_CHIPS, r, self.cols) if self.stacked else (r, N_CHIPS * self.cols)

    def part_rows(self, h, q=0, nq=1):
        n = self.rows // 2 // nq
        return pl.ds(pl.multiple_of(h * (self.rows // 2) + q * n, 16), n)

    def half_rows(self, h):
        return self.part_rows(h)

    def block(self, ref, p, rows=slice(None)):
        if self.stacked:
            return ref.at[p, rows, :]
        return ref.at[rows, pl.ds(pl.multiple_of(p * self.cols, 128), self.cols)]

    def all_chips(self, ref, rows):
        return ref.at[:, rows, :] if self.stacked else ref.at[rows, :]


BIG = (
    _Layout(IN_SHARD, D_MODEL, True),
    _Layout(ATTN_W, D_MODEL // N_CHIPS, False),
    _Layout(CONV_W, D_MODEL // N_CHIPS, False),
    _Layout(D_MODEL // N_CHIPS, D_MODEL, True),
    _Layout(D_MODEL, FF2 // N_CHIPS, False),
    _Layout(D_FF // N_CHIPS, D_MODEL, True),
)
N_BIG = len(BIG)
_ANY = pl.BlockSpec(memory_space=pl.ANY)


def _position():
    x, y, c = lax.axis_index("x"), lax.axis_index("y"), lax.axis_index("c")
    return x, y, c, 2 * x + y


def _core_of_chip(p, c):
    return (p >> 1, p & 1, c)


def _place_cast(shard, lay, pos, name):
    rows, cols = shard.shape
    tr = _row_tile(rows, cols * 6)
    if lay.stacked:
        out = (lay.whole(), BF16, (None, tr, cols), lambda i, pos: (pos[0], i, 0))
    else:
        out = (lay.whole(), BF16, (tr, cols), lambda i, pos: (i, pos[0]))
    return _tiled(lambda a: (a,), name, (rows // tr,), pos, [(shard, (tr, cols), lambda i, pos: (i, 0))], [out])[0]


def _remote(src, dst, send, recv, k, device):
    return pltpu.make_async_remote_copy(src_ref=src, dst_ref=dst, send_sem=send.at[k], recv_sem=recv.at[k],
                                        device_id=device, device_id_type=MESH)


def _arrival(dst, send, recv, k, me):
    return _remote(dst, dst, send, recv, k, me)


def _gather_ici(lay, name, q=0, nq=1):
    def plan(hbm, pos, send, recv, base):
        x, y, c, me = pos
        rows = lay.part_rows(c, q, nq)
        mine = lay.block(hbm[name], me, rows)
        starts = [_remote(mine, mine, send, recv, base + d - 1, _core_of_chip(me ^ d, c)) for d in (1, 2, 3)]
        waits = [_arrival(lay.block(hbm[name], me ^ d, rows), send, recv, base + d - 1, (x, y, c)) for d in (1, 2, 3)]
        return starts, waits
    return _Job(3, plan)


def _gather_d2d(lay, name, q=0, nq=1):
    def plan(hbm, pos, send, recv, base):
        x, y, c, me = pos
        starts, waits = [], []
        for d in (1, 2, 3):
            got = lay.block(hbm[name], me ^ d, lay.part_rows(c, q, nq))
            starts.append(_remote(got, got, send, recv, base + d - 1, (x, y, 1 - c)))
            waits.append(_arrival(lay.block(hbm[name], me ^ d, lay.part_rows(1 - c, q, nq)), send, recv, base + d - 1,
                                  (x, y, c)))
        return starts, waits
    return _Job(3, plan)


def _rs_pair(lay, grad, theirs):
    def plan(hbm, pos, send, recv, base):
        x, y, c, _ = pos
        out = _remote(lay.all_chips(hbm[grad], lay.half_rows(1 - c)), hbm[theirs], send, recv, base, (x, y, 1 - c))
        return [out], [_arrival(hbm[theirs], send, recv, base, (x, y, c))]
    return _Job(1, plan)


def _rs_chips(lay, sums, slots):
    def plan(hbm, pos, send, recv, base):
        x, y, c, me = pos
        starts = [_remote(lay.block(hbm[sums], me ^ d), hbm[slots].at[me], send, recv, base + d - 1,
                          _core_of_chip(me ^ d, c)) for d in (1, 2, 3)]
        waits = [_arrival(hbm[slots].at[me ^ d], send, recv, base + d - 1, (x, y, c)) for d in (1, 2, 3)]
        return starts, waits
    return _Job(3, plan)


def _rs_share(lay, shard):
    def plan(hbm, pos, send, recv, base):
        x, y, c, _ = pos
        mine = hbm[shard].at[lay.half_rows(c), :]
        other = hbm[shard].at[lay.half_rows(1 - c), :]
        return [_remote(mine, mine, send, recv, base, (x, y, 1 - c))], [_arrival(other, send, recv, base, (x, y, c))]
    return _Job(1, plan)


def _slots_shape(lay):
    return jax.ShapeDtypeStruct((N_CHIPS, lay.rows // 2, lay.cols), BF16)


def _theirs_shape(lay):
    return jax.ShapeDtypeStruct(lay.whole(lay.rows // 2), BF16)


def _pair_sum(grad, theirs, lay, pos, name):
    half = lay.rows // 2
    add = lambda a, b: (a.astype(F32) + b.astype(F32),)
    if lay.stacked:
        tr = _row_tile(half, lay.cols * 6)
        nt = half // tr
        flat = lambda a: a.reshape(-1, lay.cols)
        mine = lambda t, pos: ((t // nt) * (2 * nt) + pos[1] * nt + t % nt, 0)
        grid, blk = (N_CHIPS * nt,), (tr, lay.cols)
        grad, theirs = flat(grad), flat(theirs)
    else:
        tr = _row_tile(half, N_CHIPS * lay.cols * 6)
        nt = half // tr
        mine = lambda t, pos: (pos[1] * nt + t, 0)
        grid, blk = (nt,), (tr, N_CHIPS * lay.cols)
    same = lambda t, pos: (t, 0)
    out = _tiled(add, name, grid, pos, [(grad, blk, mine), (theirs, blk, same)], [(theirs.shape, BF16, blk, same)])[0]
    return out.reshape(lay.whole(half))


def _chip_sum(sums, slots, lay, pos, name):
    half = lay.rows // 2
    tr = _row_tile(half, lay.cols * 12)
    nt = half // tr
    blk3 = (None, tr, lay.cols)
    if lay.stacked:
        own = (sums, blk3, lambda i, pos: (pos[0], i, 0))
    else:
        own = (sums, (tr, lay.cols), lambda i, pos: (i, pos[0]))
    others = [(slots, blk3, functools.partial(lambda d, i, pos: (pos[0] ^ d, i, 0), d)) for d in (1, 2, 3)]

    def add(a, b1, b2, b3):
        return (((a.astype(F32) + b1.astype(F32)) + b2.astype(F32)) + b3.astype(F32),)

    return _tiled(add, name, (nt,), pos, [own] + others,
                  [((lay.rows, lay.cols), F32, (tr, lay.cols), lambda i, pos: (pos[1] * nt + i, 0))])[0]


N_DEV = 8


def _exchange_small(v, reduce):
    rows = v.shape[0]

    def body(v_ref, o_ref, *scratch):
        if reduce:
            slots, send, recv = scratch
        else:
            slots, (send, recv) = o_ref, scratch
        x, y, c = lax.axis_index("x"), lax.axis_index("y"), lax.axis_index("c")
        idx = 4 * x + 2 * y + c
        slots[idx] = v_ref[...]

        def to_peer(k):
            return pltpu.make_async_remote_copy(
                src_ref=v_ref, dst_ref=slots.at[idx], send_sem=send.at[k - 1], recv_sem=recv.at[k - 1],
                device_id=(x ^ (k >> 2), y ^ ((k >> 1) & 1), c ^ (k & 1)), device_id_type=MESH)

        def from_peer(k):
            return pltpu.make_async_remote_copy(
                src_ref=v_ref, dst_ref=slots.at[idx ^ k], send_sem=send.at[k - 1], recv_sem=recv.at[k - 1],
                device_id=(x, y, c), device_id_type=MESH)

        for k in range(1, N_DEV):
            to_peer(k).start()
        for k in range(1, N_DEV):
            from_peer(k).wait_recv()
        for k in range(1, N_DEV):
            to_peer(k).wait_send()
        if reduce:
            acc = slots[0]
            for q in range(1, N_DEV):
                acc = acc + slots[q]
            o_ref[...] = acc

    sems = pltpu.SemaphoreType.DMA((N_DEV - 1,))
    stacked = jax.ShapeDtypeStruct((N_DEV, rows, 128), F32)
    return pl.pallas_call(
        body, name="allreduce_small" if reduce else "allgather_small",
        out_shape=jax.ShapeDtypeStruct((rows, 128), F32) if reduce else stacked,
        scratch_shapes=([pltpu.VMEM((N_DEV, rows, 128), F32)] if reduce else []) + [sems, sems],
    )(v)


def _pack_rows(parts):
    padded = [jnp.pad(a, ((0, -a.shape[0] % 8), (0, 0))) for a in parts]
    starts = [sum(p.shape[0] for p in padded[:k]) for k in range(len(padded))]
    return jnp.concatenate(padded, axis=0), starts


def kernel(x, mix_norm, w_in, b_in, sinks, conv_w, w_attn_branch, w_conv_branch, w_out, ffn_norm, w_up, ffn_conv_w, w_down, final_norm, loss_target, m_mix_norm, m_w_in, m_b_in, m_sinks, m_conv_w, m_w_attn_branch, m_w_conv_branch, m_w_out, m_ffn_norm, m_w_up, m_ffn_conv_w, m_w_down, m_final_norm, v_mix_norm, v_w_in, v_b_in, v_sinks, v_conv_w, v_w_attn_branch, v_w_conv_branch, v_w_out, v_ffn_norm, v_w_up, v_ffn_conv_w, v_w_down, v_final_norm):
    me = 2 * lax.axis_index("x") + lax.axis_index("y")
    big_w = [w_in[0].T, w_attn_branch[0], w_conv_branch[0], w_out[0], w_up[0], w_down[0]]
    big_m = [m_w_in[0].T, m_w_attn_branch[0], m_w_conv_branch[0], m_w_out[0], m_w_up[0], m_w_down[0]]
    big_v = [v_w_in[0].T, v_w_attn_branch[0], v_w_conv_branch[0], v_w_out[0], v_w_up[0], v_w_down[0]]
    names = ("w_in", "w_ab", "w_cb", "w_out", "w_up", "w_down")

    pos = jnp.stack([me, lax.axis_index("c")]).astype(jnp.int32)

    lay = dict(zip(names, BIG))
    xs, target, sk = x[0], loss_target[0], sinks[0]
    s = xs.shape[0]
    tm, tm2, bk = min(256, s), min(512, s), min(1024, s)

    placed = {n: _place_cast(w, lay[n], pos, "cast_" + n) for w, n in zip(big_w, names)}
    taps, (_, t0) = _pack_rows([conv_w[0], ffn_conv_w[0].reshape(3 * (FF2 // N_CHIPS // 128), 128)])
    taps = _exchange_small(taps, reduce=False)[0::2]
    conv_full = taps[:, 0:3].transpose(1, 0, 2).reshape(3, CONV_W)
    ffn_cw_full = taps[:, t0:t0 + 33].reshape(N_CHIPS, 3, FF2 // N_CHIPS).transpose(1, 0, 2).reshape(3, FF2)

    w_in_full = _exchange(
        "gather_in", [[_gather_ici(lay["w_in"], "w_in")], [_gather_d2d(lay["w_in"], "w_in")]],
        bufs={"w_in": placed["w_in"]})["w_in"].reshape(IN_W, D_MODEL)
    early = ("w_ab", "w_cb", "w_out", "w_down")
    k1 = _Carry([_gather_ici(lay[n], n) for n in early], bufs={n: placed[n] for n in early})
    xn, qkv, c3, gates = _inproj_fwd(xs, mix_norm, w_in_full, b_in, tm2, comm=k1)
    k2 = _Carry([_gather_d2d(lay[n], n) for n in early] + [_gather_ici(lay["w_up"], "w_up", 0, 2)],
                bufs={**k1.out, "w_up": placed["w_up"]})
    attn = _attn_fwd(qkv, sk, comm=k2)
    w_ab, w_cb = k2.out["w_ab"], k2.out["w_cb"]
    w_out_full = k2.out["w_out"].reshape(D_MODEL, D_MODEL)
    w_down_full = k2.out["w_down"].reshape(D_FF, D_MODEL)
    k3 = _Carry([_gather_d2d(lay["w_up"], "w_up", 0, 2), _gather_ici(lay["w_up"], "w_up", 1, 2)],
                bufs={"w_up": k2.out["w_up"]})
    conv, a, cv, merged, h1, hn = _mix_fwd(xs, attn, c3, gates, conv_full, w_ab, w_cb, w_out_full, ffn_norm, tm, comm=k3)
    w_up_full = _exchange("gather_up_tail", [[_gather_d2d(lay["w_up"], "w_up", 1, 2)]],
                          bufs={"w_up": k3.out["w_up"]})["w_up"]
    u, up, act, dh2, loss_part, g_fn = _ffn_fwd_loss(hn, h1, w_up_full, ffn_cw_full, w_down_full,
                                                     final_norm[None, :], target, tm)

    grads, sums, slots = {}, {}, {}

    def pair(*ws):
        return _Carry([_rs_pair(lay[n], "g_" + n, "t_" + n) for n in ws], reads={"g_" + n: grads[n] for n in ws},
                      fresh={"t_" + n: _theirs_shape(lay[n]) for n in ws})

    def chips(*ws, also=None):
        k = _Carry([_rs_chips(lay[n], "s_" + n, "r_" + n) for n in ws], reads={"s_" + n: sums[n] for n in ws},
                   fresh={"r_" + n: _slots_shape(lay[n]) for n in ws})
        if also is not None:
            k = _Carry(k.jobs + also.jobs, {**k.reads, **also.reads}, None, {**k.fresh, **also.fresh})
        return k

    def pair_sums(k, *ws):
        for n in ws:
            sums[n] = _pair_sum(grads[n], k.out["t_" + n], lay[n], pos, "pair_sum_" + n)

    def take_slots(k, *ws):
        for n in ws:
            slots[n] = k.out["r_" + n]

    du, dh1, g_fcw, g_g2 = _ffn_bwd(dh2, u, up, h1, w_up_full, ffn_cw_full, w_down_full, ffn_norm, tm)
    grads["w_down"] = _wgrad(act, dh2, D_FF // 2, D_MODEL, bk, "wgrad_down").reshape(lay["w_down"].whole())
    k4 = pair("w_down")
    grads["w_up"] = _wgrad(hn, du, D_MODEL, FF2 // 4, bk, "wgrad_up", comm=k4)
    pair_sums(k4, "w_down")
    k5 = chips("w_down", also=pair("w_up"))
    dattn, da, dcv, dc3, dgt, g_cw = _mix_bwd(dh1, gates, a, cv, c3, conv_full, w_ab, w_cb, w_out_full, tm, comm=k5)
    take_slots(k5, "w_down")
    pair_sums(k5, "w_up")
    grads["w_out"] = _wgrad(merged, dh1, D_MODEL, D_MODEL, bk, "wgrad_out").reshape(lay["w_out"].whole())
    grads["w_ab"] = _wgrad(attn, da, ATTN_W, D_MODEL, bk, "wgrad_ab")
    grads["w_cb"] = _wgrad(conv, dcv, CONV_W, D_MODEL, bk, "wgrad_cb")
    k6 = chips("w_up", also=pair("w_out", "w_ab", "w_cb"))
    dq, dk, dv, g_sk = _attn_bwd(qkv, sk, attn, dattn, comm=k6)
    take_slots(k6, "w_up")
    pair_sums(k6, "w_out", "w_ab", "w_cb")
    grad_x, dproj, g_b, g_g1 = _inproj_bwd(dq, dk, dv, dc3, dgt, w_in_full, xs, dh1, mix_norm, tm2)
    k8 = chips("w_out", "w_ab", "w_cb")
    grads["w_in"] = _wgrad_in(xn, dproj, min(512, s), comm=k8).reshape(lay["w_in"].whole())
    take_slots(k8, "w_out", "w_ab", "w_cb")
    sums["w_in"] = _pair_sum(
        grads["w_in"],
        _exchange("rs_pair_in", [[_rs_pair(lay["w_in"], "g", "t")]], reads={"g": grads["w_in"]},
                  fresh={"t": _theirs_shape(lay["w_in"])})["t"],
        lay["w_in"], pos, "pair_sum_w_in")
    slots["w_in"] = _exchange("rs_chips_in", [[_rs_chips(lay["w_in"], "s", "r")]], reads={"s": sums["w_in"]},
                              fresh={"r": _slots_shape(lay["w_in"])})["r"]
    halves = {n: _chip_sum(sums[n], slots[n], lay[n], pos, "chip_sum_" + n) for n in names}
    shared = _exchange("share_halves", [[_rs_share(lay[n], n) for n in names]], bufs=halves)
    big_g = [shared[n] for n in names]
    small = dict(loss=loss_part, g_g1=g_g1, g_b=g_b, g_sk=g_sk, g_cw=g_cw, g_g2=g_g2, g_fcw=g_fcw, g_fn=g_fn)
    big_new = [_rowwise(_adamw, [w, g, m, v], [F32, F32, F32], "adamw_" + n)
               for w, g, m, v, n in zip(big_w, big_g, big_m, big_v, names)]

    parts = [small["loss"], small["g_g1"], small["g_b"], jnp.pad(small["g_sk"][:, 0], (0, 120))[None, :],
             small["g_cw"], small["g_g2"], small["g_fcw"], small["g_fn"]]
    packed, at = _pack_rows([a.reshape(-1, 128) for a in parts])
    total = _exchange_small(packed, reduce=True)
    part = lambda k: total[at[k]:at[k] + parts[k].size // 128].reshape(parts[k].shape)
    loss = total[0, 0]
    g_mix, g_b, g_g2, g_fn = part(1), part(2), part(5), part(7)
    g_sk = part(3)[:, 0:N_HEADS]
    g_cw = lax.dynamic_slice(part(4), (0, me * 128), (3, 128))
    g_fcw = lax.dynamic_slice(part(6), (0, me * (FF2 // N_CHIPS)), (3, FF2 // N_CHIPS))
    small_p = [
        (mix_norm, g_mix, m_mix_norm, v_mix_norm), (b_in, g_b, m_b_in, v_b_in), (sinks, g_sk, m_sinks, v_sinks),
        (conv_w[0], g_cw, m_conv_w[0], v_conv_w[0]), (ffn_norm, g_g2, m_ffn_norm, v_ffn_norm),
        (ffn_conv_w[0], g_fcw, m_ffn_conv_w[0], v_ffn_conv_w[0]),
        (final_norm[None, :], g_fn, m_final_norm[None, :], v_final_norm[None, :])]
    small_new = _adamw_small(small_p)
    small_new = [small_new[3 * k:3 * k + 3] for k in range(len(small_p))]

    order = [("s", 0), ("b", 0), ("s", 1), ("s", 2), ("s", 3), ("b", 1), ("b", 2), ("b", 3), ("s", 4), ("b", 4),
             ("s", 5), ("b", 5), ("s", 6)]
    shapes = [mix_norm.shape, w_in.shape, b_in.shape, sinks.shape, conv_w.shape, w_attn_branch.shape,
              w_conv_branch.shape, w_out.shape, ffn_norm.shape, w_up.shape, ffn_conv_w.shape, w_down.shape,
              final_norm.shape]
    small_g = [p[1] for p in small_p]
    big_g[0] = big_g[0].T
    big_new[0] = [a.T for a in big_new[0]]
    out_g = [(small_g[k] if kind == "s" else big_g[k]).reshape(shp) for (kind, k), shp in zip(order, shapes)]
    news = [[(small_new[k][j] if kind == "s" else big_new[k][j]).reshape(shp) for (kind, k), shp in zip(order, shapes)]
            for j in range(3)]
    return (loss, grad_x[None], *out_g, *news[0], *news[1], *news[2])
```

```python
import functools

import jax
import jax.numpy as jnp
from jax import lax
from jax.experimental import pallas as pl
from jax.experimental.pallas import tpu as pltpu

F32 = jnp.float32
BF16 = jnp.bfloat16

D_MODEL = 1024
HEAD_DIM = 64
N_HEADS = 8
N_KV_HEADS = 2
GROUP = N_HEADS // N_KV_HEADS
BLOCK = 128
ATTN_SCALE = HEAD_DIM ** -0.5
ATTN_W = N_HEADS * HEAD_DIM
KV_W = N_KV_HEADS * HEAD_DIM
CONV_W = 512
QKV_W = ATTN_W + 2 * KV_W
C3_W = 3 * CONV_W
GATES_W = 2 * D_MODEL
IN_W = QKV_W + C3_W + GATES_W
D_FF = 2816
FF2 = 2 * D_FF
NORM_EPS = 1e-5
N_CHIPS = 4
IN_SHARD = IN_W // N_CHIPS
NEG = -1e30

ADAM_LR = 0.001
ADAM_B1 = 0.9
ADAM_B2 = 0.999
ADAM_EPS = 1e-08
ADAM_WD = 0.01
ADAM_STEP = 10

VMEM_LIMIT = 56 * 1024 * 1024
MESH = pl.DeviceIdType.MESH

NT = (((1,), (1,)), ((), ()))
TN = (((0,), (0,)), ((), ()))


def _params(*sem):
    return pltpu.CompilerParams(dimension_semantics=sem, vmem_limit_bytes=VMEM_LIMIT)


def _resident(shape):
    return pl.BlockSpec(shape, lambda *_: (0,) * len(shape), pipeline_mode=pl.Buffered(1))


def _sigmoid(v):
    return 0.5 * jnp.tanh(0.5 * v) + 0.5


def _rstd(v):
    return lax.rsqrt(jnp.mean(v * v, axis=-1, keepdims=True) + NORM_EPS)


def _rms_bwd(dy, v, rstd, g):
    vhat = v * rstd
    t = dy * g
    return rstd * (t - vhat * jnp.mean(t * vhat, axis=-1, keepdims=True)), dy * vhat


def _taps(z, cw):
    return cw[2:3] * z + cw[1:2] * pltpu.roll(z, 1, 0) + cw[0:1] * pltpu.roll(z, 2, 0)


def _causal_conv(z, prev, cw):
    edge = _taps(jnp.concatenate([prev, z[0:8]], axis=0), cw)
    return jnp.concatenate([edge[8:16], _taps(z, cw)[8:]], axis=0)


def _rows_after(z, nxt):
    n = z.shape[0]
    edge = jnp.concatenate([z[n - 8:n], nxt], axis=0)
    return tuple(jnp.concatenate([pltpu.roll(z, n - k, 0)[:n - 8], pltpu.roll(edge, 16 - k, 0)[0:8]], axis=0)
                 for k in (1, 2))


def _inproj_fwd(x, g1, w_in, b_in, tm, comm=None):
    s = x.shape[0]

    def body(x_ref, g_ref, w_ref, b_ref, xn_ref, qkv_ref, c3_ref, gt_ref):
        xf = x_ref[...]
        xn = (xf * _rstd(xf) * g_ref[...]).astype(BF16)
        xn_ref[...] = xn

        def seg(a, b):
            return lax.dot_general(xn, w_ref[a:b, :], NT, preferred_element_type=F32) + b_ref[:, a:b]

        qkv_ref[...] = seg(0, QKV_W).astype(BF16)
        c3_ref[...] = seg(QKV_W, QKV_W + C3_W)
        gt_ref[...] = seg(QKV_W + C3_W, IN_W)

    row = lambda w: pl.BlockSpec((tm, w), lambda i: (i, 0))
    return _call(
        comm, body, name="inproj_fwd", grid=(s // tm,),
        in_specs=[row(D_MODEL), _resident((1, D_MODEL)), _resident((IN_W, D_MODEL)), _resident((1, IN_W))],
        out_specs=[row(D_MODEL), row(QKV_W), row(C3_W), row(GATES_W)],
        out_shape=[jax.ShapeDtypeStruct((s, D_MODEL), BF16), jax.ShapeDtypeStruct((s, QKV_W), BF16),
                   jax.ShapeDtypeStruct((s, C3_W), F32), jax.ShapeDtypeStruct((s, GATES_W), F32)],
        compiler_params=_params("parallel"),
    )(x, g1, w_in, b_in)


def _attn_mask(first_block):
    qi = lax.broadcasted_iota(jnp.int32, (GROUP * BLOCK, 2 * BLOCK), 0) & (BLOCK - 1)
    kj = lax.broadcasted_iota(jnp.int32, (GROUP * BLOCK, 2 * BLOCK), 1)
    band = (kj > qi) & (kj <= qi + BLOCK)
    return band & ((kj >= BLOCK) | jnp.logical_not(first_block))


def _sink_column(sk_ref, h):
    rows = lax.broadcasted_iota(jnp.int32, (GROUP * BLOCK, 1), 0)
    col = jnp.full((GROUP * BLOCK, 1), sk_ref[h * GROUP], F32)
    for g in range(1, GROUP):
        col = jnp.where(rows >= g * BLOCK, sk_ref[h * GROUP + g], col)
    return col


def _stack_heads(t, h):
    return jnp.concatenate(
        [t[:, (h * GROUP + g) * HEAD_DIM:(h * GROUP + g + 1) * HEAD_DIM] for g in range(GROUP)], axis=0)


def _unstack_heads(per_kv):
    return jnp.concatenate(
        [t[g * BLOCK:(g + 1) * BLOCK] for t in per_kv for g in range(GROUP)], axis=1)


def _attn_specs(nb):
    cur = lambda i: jnp.minimum(i, nb - 1)
    prev = lambda i: jnp.maximum(jnp.minimum(i, nb - 1) - 1, 0)
    q = pl.BlockSpec((BLOCK, ATTN_W), lambda i: (cur(i), 0))
    kp = pl.BlockSpec((BLOCK, KV_W), lambda i: (prev(i), ATTN_W // KV_W))
    kc = pl.BlockSpec((BLOCK, KV_W), lambda i: (cur(i), ATTN_W // KV_W))
    vp = pl.BlockSpec((BLOCK, KV_W), lambda i: (prev(i), ATTN_W // KV_W + 1))
    vc = pl.BlockSpec((BLOCK, KV_W), lambda i: (cur(i), ATTN_W // KV_W + 1))
    return q, kp, kc, vp, vc


def _attn_fwd(qkv, sinks, comm=None):
    s = qkv.shape[0]
    nb = s // BLOCK

    def body(sk_ref, q_ref, kp_ref, kc_ref, vp_ref, vc_ref, o_ref):
        mask = _attn_mask(pl.program_id(0) == 0)
        q, kp, kc, vp, vc = q_ref[...], kp_ref[...], kc_ref[...], vp_ref[...], vc_ref[...]
        outs = []
        for h in range(N_KV_HEADS):
            hs = slice(h * HEAD_DIM, (h + 1) * HEAD_DIM)
            k2 = jnp.concatenate([kp[:, hs], kc[:, hs]], axis=0)
            v2 = jnp.concatenate([vp[:, hs], vc[:, hs]], axis=0)
            sc = lax.dot_general(_stack_heads(q, h), k2, NT, preferred_element_type=F32) * ATTN_SCALE
            sc = jnp.where(mask, sc, NEG)
            sink = _sink_column(sk_ref, h)
            m = jnp.maximum(jnp.max(sc, axis=1, keepdims=True), sink)
            p = jnp.exp(sc - m)
            den = jnp.sum(p, axis=1, keepdims=True) + jnp.exp(sink - m)
            outs.append(jnp.dot(p.astype(BF16), v2, preferred_element_type=F32) / den)
        o_ref[...] = _unstack_heads(outs).astype(BF16)

    return _call(
        comm, body, name="attn_fwd", grid=(nb,),
        in_specs=[pl.BlockSpec(memory_space=pltpu.SMEM), *_attn_specs(nb)],
        out_specs=pl.BlockSpec((BLOCK, ATTN_W), lambda i: (i, 0)),
        out_shape=jax.ShapeDtypeStruct((s, ATTN_W), BF16),
        compiler_params=_params("parallel"),
    )(sinks, qkv, qkv, qkv, qkv, qkv)


def _mix_fwd(x, attn, c3, gates, conv_w, w_ab, w_cb, w_out, g2, tm, comm=None):
    s = x.shape[0]

    def body(x_ref, at_ref, c3_ref, gt_ref, cw_ref, wab_ref, wcb_ref, wo_ref, g_ref,
             conv_ref, a_ref, cv_ref, mg_ref, h1_ref, hn_ref, carry_ref):
        @pl.when(pl.program_id(0) == 0)
        def _():
            carry_ref[...] = jnp.zeros_like(carry_ref)

        c3v = c3_ref[...]
        cb, cc, cx = c3v[:, :CONV_W], c3v[:, CONV_W:2 * CONV_W], c3v[:, 2 * CONV_W:]
        z = cc * cx
        cz = _causal_conv(z, carry_ref[...], cw_ref[...])
        carry_ref[...] = z[tm - 8:tm]
        conv = (cb * cz).astype(BF16)
        conv_ref[...] = conv
        a = jnp.dot(at_ref[...], wab_ref[...], preferred_element_type=F32)
        cv = jnp.dot(conv, wcb_ref[...], preferred_element_type=F32)
        a_ref[...] = a.astype(BF16)
        cv_ref[...] = cv.astype(BF16)
        gt = gt_ref[...]
        merged = (_sigmoid(gt[:, :D_MODEL]) * a + _sigmoid(gt[:, D_MODEL:]) * cv).astype(BF16)
        mg_ref[...] = merged
        h1 = x_ref[...] + jnp.dot(merged, wo_ref[...], preferred_element_type=F32)
        h1_ref[...] = h1
        hn_ref[...] = (h1 * _rstd(h1) * g_ref[...]).astype(BF16)

    row = lambda w: pl.BlockSpec((tm, w), lambda i: (i, 0))
    return _call(
        comm, body, name="mix_fwd", grid=(s // tm,),
        in_specs=[row(D_MODEL), row(ATTN_W), row(C3_W), row(GATES_W), _resident((3, CONV_W)),
                  _resident((ATTN_W, D_MODEL)), _resident((CONV_W, D_MODEL)), _resident((D_MODEL, D_MODEL)),
                  _resident((1, D_MODEL))],
        out_specs=[row(CONV_W), row(D_MODEL), row(D_MODEL), row(D_MODEL), row(D_MODEL), row(D_MODEL)],
        out_shape=[jax.ShapeDtypeStruct((s, CONV_W), BF16), jax.ShapeDtypeStruct((s, D_MODEL), BF16),
                   jax.ShapeDtypeStruct((s, D_MODEL), BF16), jax.ShapeDtypeStruct((s, D_MODEL), BF16),
                   jax.ShapeDtypeStruct((s, D_MODEL), F32), jax.ShapeDtypeStruct((s, D_MODEL), BF16)],
        scratch_shapes=[pltpu.VMEM((8, CONV_W), F32)],
        compiler_params=_params("arbitrary"),
    )(x, attn, c3, gates, conv_w, w_ab, w_cb, w_out, g2)


def _ffn_fwd_loss(hn, h1, w_up, ffn_cw, w_down, g3, target, tm):
    s = hn.shape[0]

    def body(hn_ref, h1_ref, wu_ref, cw_ref, wd_ref, g_ref, t_ref,
             u_ref, up_ref, act_ref, dh2_ref, loss_ref, gfn_ref, carry_ref):
        @pl.when(pl.program_id(0) == 0)
        def _():
            carry_ref[...] = jnp.zeros_like(carry_ref)
            loss_ref[...] = jnp.zeros_like(loss_ref)
            gfn_ref[...] = jnp.zeros_like(gfn_ref)

        u = jnp.dot(hn_ref[...], wu_ref[...], preferred_element_type=F32)
        u_ref[...] = u.astype(BF16)
        up = _causal_conv(u, carry_ref[...], cw_ref[...])
        up_ref[...] = up.astype(BF16)
        carry_ref[...] = u[tm - 8:tm]
        gate, val = up[:, :D_FF], up[:, D_FF:]
        act = (gate * _sigmoid(gate) * val).astype(BF16)
        act_ref[...] = act
        h2 = h1_ref[...] + jnp.dot(act, wd_ref[...], preferred_element_type=F32)
        rstd = _rstd(h2)
        g = g_ref[...]
        err = h2 * rstd * g - t_ref[...]
        loss_ref[...] += jnp.sum(err * err) * (0.5 / D_MODEL)
        dh2, dg = _rms_bwd(err * (1.0 / D_MODEL), h2, rstd, g)
        dh2_ref[...] = dh2
        gfn_ref[...] += jnp.sum(dg, axis=0, keepdims=True)

    row = lambda w: pl.BlockSpec((tm, w), lambda i: (i, 0))
    acc = lambda w: pl.BlockSpec((1, w), lambda i: (0, 0))
    return pl.pallas_call(
        body, name="ffn_fwd_loss", grid=(s // tm,),
        in_specs=[row(D_MODEL), row(D_MODEL), _resident((D_MODEL, FF2)), _resident((3, FF2)),
                  _resident((D_FF, D_MODEL)), _resident((1, D_MODEL)), row(D_MODEL)],
        out_specs=[row(FF2), row(FF2), row(D_FF), row(D_MODEL), acc(128), acc(D_MODEL)],
        out_shape=[jax.ShapeDtypeStruct((s, FF2), BF16), jax.ShapeDtypeStruct((s, FF2), BF16),
                   jax.ShapeDtypeStruct((s, D_FF), BF16),
                   jax.ShapeDtypeStruct((s, D_MODEL), F32), jax.ShapeDtypeStruct((1, 128), F32),
                   jax.ShapeDtypeStruct((1, D_MODEL), F32)],
        scratch_shapes=[pltpu.VMEM((8, FF2), F32)],
        compiler_params=_params("arbitrary"),
    )(hn, h1, w_up, ffn_cw, w_down, g3, target)


def _ffn_bwd(dh2, u, up, h1, w_up, ffn_cw, w_down, g2, tm):
    s = dh2.shape[0]
    nt = s // tm

    def body(dh2_ref, u_ref, up_ref, h1_ref, wu_ref, cw_ref, wd_ref, g_ref,
             du_ref, dh1_ref, gcw_ref, gg_ref, carry_ref):
        @pl.when(pl.program_id(0) == 0)
        def _():
            carry_ref[...] = jnp.zeros_like(carry_ref)
            gcw_ref[...] = jnp.zeros_like(gcw_ref)
            gg_ref[...] = jnp.zeros_like(gg_ref)

        dh2v = dh2_ref[...]
        dact = lax.dot_general(dh2v.astype(BF16), wd_ref[...], NT, preferred_element_type=F32)
        upv = up_ref[...].astype(F32)
        gate, val = upv[:, :D_FF], upv[:, D_FF:]
        sg = _sigmoid(gate)
        dval = dact * (gate * sg)
        dgate = dact * val * (sg * (1.0 + gate * (1.0 - sg)))
        dup = jnp.concatenate([dgate, dval], axis=1)
        dup1, dup2 = _rows_after(dup, carry_ref[...])
        carry_ref[...] = dup[0:8]
        u = u_ref[...].astype(F32)
        gcw_ref[2:3, :] += jnp.sum(dup * u, axis=0, keepdims=True)
        gcw_ref[1:2, :] += jnp.sum(dup1 * u, axis=0, keepdims=True)
        gcw_ref[0:1, :] += jnp.sum(dup2 * u, axis=0, keepdims=True)
        cw = cw_ref[...]
        du = (cw[2:3] * dup + cw[1:2] * dup1 + cw[0:1] * dup2).astype(BF16)
        du_ref[...] = du
        dhn = lax.dot_general(du, wu_ref[...], NT, preferred_element_type=F32)
        h1v = h1_ref[...]
        dh1, dg = _rms_bwd(dhn, h1v, _rstd(h1v), g_ref[...])
        dh1_ref[...] = dh2v + dh1
        gg_ref[...] += jnp.sum(dg, axis=0, keepdims=True)

    row = lambda w: pl.BlockSpec((tm, w), lambda i: (nt - 1 - i, 0))
    return pl.pallas_call(
        body, name="ffn_bwd", grid=(nt,),
        in_specs=[row(D_MODEL), row(FF2), row(FF2),
                  row(D_MODEL), _resident((D_MODEL, FF2)), _resident((3, FF2)), _resident((D_FF, D_MODEL)),
                  _resident((1, D_MODEL))],
        out_specs=[row(FF2), row(D_MODEL), pl.BlockSpec((3, FF2), lambda i: (0, 0)),
                   pl.BlockSpec((1, D_MODEL), lambda i: (0, 0))],
        out_shape=[jax.ShapeDtypeStruct((s, FF2), BF16), jax.ShapeDtypeStruct((s, D_MODEL), F32),
                   jax.ShapeDtypeStruct((3, FF2), F32), jax.ShapeDtypeStruct((1, D_MODEL), F32)],
        scratch_shapes=[pltpu.VMEM((8, FF2), F32)],
        compiler_params=_params("arbitrary"),
    )(dh2, u, up, h1, w_up, ffn_cw, w_down, g2)


def _mix_bwd(dh1, gates, a, cv, c3, conv_w, w_ab, w_cb, w_out, tm, comm=None):
    s = dh1.shape[0]
    nt = s // tm
    halo = 8

    def body(dh1_ref, gt_ref, a_ref, cv_ref, c3_ref, ch_ref, cw_ref, wab_ref, wcb_ref, wo_ref,
             dat_ref, da_ref, dcv_ref, dc3_ref, dgt_ref, gcw_ref, carry_ref):
        i = pl.program_id(0)

        @pl.when(i == 0)
        def _():
            carry_ref[...] = jnp.zeros_like(carry_ref)
            gcw_ref[...] = jnp.zeros_like(gcw_ref)

        dm = lax.dot_general(dh1_ref[...].astype(BF16), wo_ref[...], NT, preferred_element_type=F32)
        gt = gt_ref[...]
        sa, sc = _sigmoid(gt[:, :D_MODEL]), _sigmoid(gt[:, D_MODEL:])
        da = (dm * sa).astype(BF16)
        dcv = (dm * sc).astype(BF16)
        da_ref[...] = da
        dcv_ref[...] = dcv
        dgt_ref[...] = jnp.concatenate(
            [dm * a_ref[...].astype(F32) * (sa * (1.0 - sa)), dm * cv_ref[...].astype(F32) * (sc * (1.0 - sc))],
            axis=1).astype(BF16)
        dat_ref[...] = lax.dot_general(da, wab_ref[...], NT, preferred_element_type=F32).astype(BF16)
        dconv = lax.dot_general(dcv, wcb_ref[...], NT, preferred_element_type=F32)
        c3v = c3_ref[...]
        cb, cc, cx = c3v[:, :CONV_W], c3v[:, CONV_W:2 * CONV_W], c3v[:, 2 * CONV_W:]
        z = cc * cx
        chv = ch_ref[...] * (i < nt - 1).astype(F32)
        zh = chv[:, CONV_W:2 * CONV_W] * chv[:, 2 * CONV_W:]
        cw = cw_ref[...]
        cz = _causal_conv(z, zh, cw)
        dcz = dconv * cb
        dcz1, dcz2 = _rows_after(dcz, carry_ref[...])
        carry_ref[...] = dcz[0:8]
        gcw_ref[2:3, :] += jnp.sum(dcz * z, axis=0, keepdims=True)
        gcw_ref[1:2, :] += jnp.sum(dcz1 * z, axis=0, keepdims=True)
        gcw_ref[0:1, :] += jnp.sum(dcz2 * z, axis=0, keepdims=True)
        dz = cw[2:3] * dcz + cw[1:2] * dcz1 + cw[0:1] * dcz2
        dc3_ref[...] = jnp.concatenate([dconv * cz, dz * cx, dz * cc], axis=1).astype(BF16)

    row = lambda w: pl.BlockSpec((tm, w), lambda i: (nt - 1 - i, 0))
    return _call(
        comm, body, name="mix_bwd", grid=(nt,),
        in_specs=[row(D_MODEL), row(GATES_W), row(D_MODEL), row(D_MODEL), row(C3_W),
                  pl.BlockSpec((halo, C3_W), lambda i: (jnp.maximum((nt - 1 - i) * (tm // halo) - 1, 0), 0)),
                  _resident((3, CONV_W)), _resident((ATTN_W, D_MODEL)), _resident((CONV_W, D_MODEL)),
                  _resident((D_MODEL, D_MODEL))],
        out_specs=[row(ATTN_W), row(D_MODEL), row(D_MODEL), row(C3_W), row(GATES_W),
                   pl.BlockSpec((3, CONV_W), lambda i: (0, 0))],
        out_shape=[jax.ShapeDtypeStruct((s, ATTN_W), BF16), jax.ShapeDtypeStruct((s, D_MODEL), BF16),
                   jax.ShapeDtypeStruct((s, D_MODEL), BF16), jax.ShapeDtypeStruct((s, C3_W), BF16),
                   jax.ShapeDtypeStruct((s, GATES_W), BF16), jax.ShapeDtypeStruct((3, CONV_W), F32)],
        scratch_shapes=[pltpu.VMEM((8, CONV_W), F32)],
        compiler_params=_params("arbitrary"),
    )(dh1, gates, a, cv, c3, c3, conv_w, w_ab, w_cb, w_out)


def _attn_bwd(qkv, sinks, o, do, comm=None):
    s = qkv.shape[0]
    nb = s // BLOCK

    def body(sk_ref, q_ref, kp_ref, kc_ref, vp_ref, vc_ref, o_ref, do_ref,
             dq_ref, dk_ref, dv_ref, dsk_ref, ck_ref, cvv_ref):
        i = pl.program_id(0)

        @pl.when(i == 0)
        def _():
            ck_ref[...] = jnp.zeros_like(ck_ref)
            cvv_ref[...] = jnp.zeros_like(cvv_ref)
            dsk_ref[...] = jnp.zeros_like(dsk_ref)

        @pl.when(i < nb)
        def _():
            mask = _attn_mask(i == 0)
            q, kp, kc, vp, vc = q_ref[...], kp_ref[...], kc_ref[...], vp_ref[...], vc_ref[...]
            ov, dov = o_ref[...], do_ref[...]
            dqs, dks, dvs = [], [], []
            for h in range(N_KV_HEADS):
                hs = slice(h * HEAD_DIM, (h + 1) * HEAD_DIM)
                k2 = jnp.concatenate([kp[:, hs], kc[:, hs]], axis=0)
                v2 = jnp.concatenate([vp[:, hs], vc[:, hs]], axis=0)
                qg, og, dog = _stack_heads(q, h), _stack_heads(ov, h), _stack_heads(dov, h)
                sc = lax.dot_general(qg, k2, NT, preferred_element_type=F32) * ATTN_SCALE
                sc = jnp.where(mask, sc, NEG)
                sink = _sink_column(sk_ref, h)
                m = jnp.maximum(jnp.max(sc, axis=1, keepdims=True), sink)
                p = jnp.exp(sc - m)
                psink = jnp.exp(sink - m)
                inv = 1.0 / (jnp.sum(p, axis=1, keepdims=True) + psink)
                p = p * inv
                delta = jnp.sum(dog.astype(F32) * og.astype(F32), axis=1, keepdims=True)
                dp = lax.dot_general(dog, v2, NT, preferred_element_type=F32)
                ds = (p * (dp - delta)).astype(BF16)
                dqs.append(jnp.dot(ds, k2, preferred_element_type=F32) * ATTN_SCALE)
                dks.append(lax.dot_general(ds, qg, TN, preferred_element_type=F32) * ATTN_SCALE)
                dvs.append(lax.dot_general(p.astype(BF16), dog, TN, preferred_element_type=F32))
                dsink = -(psink * inv * delta)
                for g in range(GROUP):
                    r = h * GROUP + g
                    dsk_ref[r:r + 1, :] += jnp.sum(dsink[g * BLOCK:(g + 1) * BLOCK])
            dq_ref[...] = _unstack_heads(dqs).astype(BF16)
            dk2 = jnp.concatenate(dks, axis=1)
            dv2 = jnp.concatenate(dvs, axis=1)
            dk_ref[...] = (ck_ref[...] + dk2[:BLOCK]).astype(BF16)
            dv_ref[...] = (cvv_ref[...] + dv2[:BLOCK]).astype(BF16)
            ck_ref[...] = dk2[BLOCK:]
            cvv_ref[...] = dv2[BLOCK:]

        @pl.when(i == nb)
        def _():
            dk_ref[...] = ck_ref[...].astype(BF16)
            dv_ref[...] = cvv_ref[...].astype(BF16)

    cur = lambda i: jnp.minimum(i, nb - 1)
    done = lambda i: jnp.maximum(i - 1, 0)
    return _call(
        comm, body, name="attn_bwd", grid=(nb + 1,),
        in_specs=[pl.BlockSpec(memory_space=pltpu.SMEM), *_attn_specs(nb),
                  pl.BlockSpec((BLOCK, ATTN_W), lambda i: (cur(i), 0)),
                  pl.BlockSpec((BLOCK, ATTN_W), lambda i: (cur(i), 0))],
        out_specs=[pl.BlockSpec((BLOCK, ATTN_W), lambda i: (cur(i), 0)),
                   pl.BlockSpec((BLOCK, KV_W), lambda i: (done(i), 0)),
                   pl.BlockSpec((BLOCK, KV_W), lambda i: (done(i), 0)),
                   pl.BlockSpec((N_HEADS, 128), lambda i: (0, 0))],
        out_shape=[jax.ShapeDtypeStruct((s, ATTN_W), BF16), jax.ShapeDtypeStruct((s, KV_W), BF16),
                   jax.ShapeDtypeStruct((s, KV_W), BF16), jax.ShapeDtypeStruct((N_HEADS, 128), F32)],
        scratch_shapes=[pltpu.VMEM((BLOCK, KV_W), F32), pltpu.VMEM((BLOCK, KV_W), F32)],
        compiler_params=_params("arbitrary"),
    )(sinks, qkv, qkv, qkv, qkv, qkv, o, do)


def _inproj_bwd(dq, dk, dv, dc3, dgt, w_in, x, dh1, g1, tm, comm=None):
    s = x.shape[0]

    def body(dq_ref, dk_ref, dv_ref, dc3_ref, dgt_ref, w_ref, x_ref, dh1_ref, g_ref,
             dx_ref, dp_ref, gb_ref, gg_ref):
        @pl.when(pl.program_id(0) == 0)
        def _():
            gb_ref[...] = jnp.zeros_like(gb_ref)
            gg_ref[...] = jnp.zeros_like(gg_ref)

        dp = jnp.concatenate([dq_ref[...], dk_ref[...], dv_ref[...], dc3_ref[...], dgt_ref[...]], axis=1)
        dp_ref[...] = dp
        gb_ref[...] += jnp.sum(dp.astype(F32), axis=0, keepdims=True)
        dxn = jnp.dot(dp, w_ref[...], preferred_element_type=F32)
        xf = x_ref[...]
        dx, dg = _rms_bwd(dxn, xf, _rstd(xf), g_ref[...])
        dx_ref[...] = dh1_ref[...] + dx
        gg_ref[...] += jnp.sum(dg, axis=0, keepdims=True)

    row = lambda w: pl.BlockSpec((tm, w), lambda i: (i, 0))
    acc = lambda w: pl.BlockSpec((1, w), lambda i: (0, 0))
    return _call(
        comm, body, name="inproj_bwd", grid=(s // tm,),
        in_specs=[row(ATTN_W), row(KV_W), row(KV_W), row(C3_W), row(GATES_W), _resident((IN_W, D_MODEL)),
                  row(D_MODEL), row(D_MODEL), _resident((1, D_MODEL))],
        out_specs=[row(D_MODEL), row(IN_W), acc(IN_W), acc(D_MODEL)],
        out_shape=[jax.ShapeDtypeStruct((s, D_MODEL), F32), jax.ShapeDtypeStruct((s, IN_W), BF16),
                   jax.ShapeDtypeStruct((1, IN_W), F32), jax.ShapeDtypeStruct((1, D_MODEL), F32)],
        compiler_params=_params("arbitrary"),
    )(dq, dk, dv, dc3, dgt, w_in, x, dh1, g1)


def _wgrad(a, b, bm, bn, bk, name, comm=None):
    s, m = a.shape
    n = b.shape[1]
    nk = s // bk

    def body(a_ref, b_ref, o_ref, acc_ref):
        k = pl.program_id(2)

        @pl.when(k == 0)
        def _():
            acc_ref[...] = jnp.zeros_like(acc_ref)

        acc_ref[...] += lax.dot_general(a_ref[...].astype(BF16), b_ref[...].astype(BF16), TN,
                                        preferred_element_type=F32)

        @pl.when(k == nk - 1)
        def _():
            o_ref[...] = acc_ref[...].astype(BF16)

    return _call(
        comm, body, name=name, grid=(m // bm, n // bn, nk),
        in_specs=[pl.BlockSpec((bk, bm), lambda i, j, k: (k, i)), pl.BlockSpec((bk, bn), lambda i, j, k: (k, j))],
        out_specs=pl.BlockSpec((bm, bn), lambda i, j, k: (i, j)),
        out_shape=jax.ShapeDtypeStruct((m, n), BF16),
        scratch_shapes=[pltpu.VMEM((bm, bn), F32)],
        compiler_params=_params("parallel", "parallel", "arbitrary"),
    )(a, b)


def _wgrad_in(xn, dproj, bk, comm=None):
    s = xn.shape[0]
    nk = s // bk

    def body(a_ref, b_ref, o_ref, acc_ref):
        k = pl.program_id(0)

        @pl.when(k == 0)
        def _():
            acc_ref[...] = jnp.zeros_like(acc_ref)

        acc_ref[...] += lax.dot_general(b_ref[...], a_ref[...], TN, preferred_element_type=F32)

        @pl.when(k == nk - 1)
        def _():
            o_ref[...] = acc_ref[...].astype(BF16)

    return _call(
        comm, body, name="wgrad_in", grid=(nk,),
        in_specs=[pl.BlockSpec((bk, D_MODEL), lambda k: (k, 0)), pl.BlockSpec((bk, IN_W), lambda k: (k, 0))],
        out_specs=_resident((IN_W, D_MODEL)),
        out_shape=jax.ShapeDtypeStruct((IN_W, D_MODEL), BF16),
        scratch_shapes=[pltpu.VMEM((IN_W, D_MODEL), F32)],
        compiler_params=_params("arbitrary"),
    )(xn, dproj)


class _Carry:
    def __init__(self, jobs, reads=None, bufs=None, fresh=None):
        self.jobs, self.reads, self.bufs, self.fresh = jobs, reads or {}, bufs or {}, fresh or {}
        self.out = {}


class _Job:
    def __init__(self, n_sems, plan):
        self.n_sems, self.plan = n_sems, plan


def _plan_all(jobs, hbm, send, recv):
    pos = _position()
    starts, waits, base = [], [], 0
    for job in jobs:
        s, w = job.plan(hbm, pos, send, recv, base)
        starts, waits, base = starts + s, waits + w, base + job.n_sems
    return starts, waits


def _call(comm, body, **kw):
    if comm is None:
        return pl.pallas_call(body, **kw)
    grid = kw["grid"]
    single = not isinstance(kw["out_shape"], (list, tuple))
    out_shape = [kw["out_shape"]] if single else list(kw["out_shape"])
    out_specs = [kw["out_specs"]] if single else list(kw["out_specs"])
    in_specs = list(kw["in_specs"])
    scratch = list(kw.get("scratch_shapes", ()))
    r_names, b_names, f_names = list(comm.reads), list(comm.bufs), list(comm.fresh)
    n_args, n_out, n_scr = len(in_specs), len(out_shape), len(scratch)
    n_sems = sum(j.n_sems for j in comm.jobs)

    def wrapped(*refs):
        k = n_args
        hbm = dict(zip(r_names, refs[k:k + len(r_names)]))
        k += len(r_names) + len(b_names)
        outs = refs[k:k + n_out]
        k += n_out
        hbm.update(zip(b_names + f_names, refs[k:k + len(b_names) + len(f_names)]))
        k += len(b_names) + len(f_names)
        send, recv = refs[k + n_scr:]
        starts, waits = _plan_all(comm.jobs, hbm, send, recv)
        ids = [pl.program_id(a) for a in range(len(grid))]
        first = functools.reduce(jnp.logical_and, [i == 0 for i in ids])
        last = functools.reduce(jnp.logical_and, [i == g - 1 for i, g in zip(ids, grid)])

        @pl.when(first)
        def _():
            for cp in starts:
                cp.start()

        body(*refs[:n_args], *outs, *refs[k:k + n_scr])

        @pl.when(last)
        def _():
            for cp in waits:
                cp.wait_recv()
            for cp in starts:
                cp.wait_send()

    sems = pltpu.SemaphoreType.DMA((n_sems,))
    held = [jax.ShapeDtypeStruct(a.shape, a.dtype) for a in comm.bufs.values()] + list(comm.fresh.values())
    call = pl.pallas_call(
        wrapped, name=kw["name"], grid=grid,
        in_specs=in_specs + [_ANY] * (len(r_names) + len(b_names)),
        out_specs=out_specs + [_ANY] * len(held),
        out_shape=out_shape + held,
        input_output_aliases={n_args + len(r_names) + i: n_out + i for i in range(len(b_names))},
        scratch_shapes=scratch + [sems, sems],
        compiler_params=_params(*["arbitrary"] * len(grid)),
    )

    def run(*args):
        res = call(*args, *comm.reads.values(), *comm.bufs.values())
        comm.out = dict(zip(b_names + f_names, res[n_out:]))
        return res[0] if single else res[:n_out]

    return run


def _exchange(name, phases, reads=None, bufs=None, fresh=None):
    comm = _Carry([j for ph in phases for j in ph], reads, bufs, fresh)
    r_names, b_names, f_names = list(comm.reads), list(comm.bufs), list(comm.fresh)
    n_sems = sum(j.n_sems for j in comm.jobs)

    def body(*refs):
        hbm = dict(zip(r_names, refs[:len(r_names)]))
        k = len(r_names) + len(b_names)
        hbm.update(zip(b_names + f_names, refs[k:k + len(b_names) + len(f_names)]))
        send, recv = refs[-2:]
        pos = _position()
        started, base = [], 0
        for ph in phases:
            waits = []
            for job in ph:
                s, w = job.plan(hbm, pos, send, recv, base)
                base += job.n_sems
                for cp in s:
                    cp.start()
                started, waits = started + s, waits + w
            for cp in waits:
                cp.wait_recv()
        for cp in started:
            cp.wait_send()

    sems = pltpu.SemaphoreType.DMA((n_sems,))
    held = [jax.ShapeDtypeStruct(a.shape, a.dtype) for a in comm.bufs.values()] + list(comm.fresh.values())
    res = pl.pallas_call(
        body, name=name, in_specs=[_ANY] * (len(r_names) + len(b_names)), out_specs=[_ANY] * len(held),
        out_shape=held, input_output_aliases={len(r_names) + i: i for i in range(len(b_names))},
        scratch_shapes=[sems, sems],
    )(*comm.reads.values(), *comm.bufs.values())
    return dict(zip(b_names + f_names, res))


_HBM = pl.BlockSpec(memory_space=pltpu.HBM)
_SEM = pl.BlockSpec(memory_space=pltpu.SEMAPHORE)
_EFFECT = pltpu.SideEffectType.DATAFLOW_SIDE_EFFECTING


def _start_exchange(name, jobs, arrays):
    names = list(arrays)
    n = len(names)
    n_sems = sum(j.n_sems for j in jobs)

    def body(*refs):
        hbm = dict(zip(names, refs[:n]))
        send, recv = refs[n:n + 2]
        token = refs[-1]
        for cp in _plan_all(jobs, hbm, send, recv)[0]:
            cp.start()
        token[...] = jnp.zeros_like(token)

    given = [pltpu.with_memory_space_constraint(
        a if isinstance(a, jax.Array) else lax.empty(a.shape, a.dtype), pltpu.HBM) for a in arrays.values()]
    sems = pltpu.SemaphoreType.DMA((n_sems,))
    res = pl.pallas_call(
        body, name=name,
        out_shape=(sems, sems, *[pltpu.HBM(a.shape, a.dtype) for a in given], jax.ShapeDtypeStruct((8, 128), F32)),
        in_specs=[_HBM] * n, out_specs=(_SEM, _SEM, *[_HBM] * n, pl.BlockSpec(memory_space=pltpu.VMEM)),
        input_output_aliases={i: 2 + i for i in range(n)},
        compiler_params=pltpu.CompilerParams(has_side_effects=_EFFECT),
    )(*given)
    return (names, jobs, res[0], res[1], res[2:2 + n]), res[-1]


def _finish_exchange(name, state, after):
    names, jobs, send_sem, recv_sem, held = state
    n = len(names)

    def body(*refs):
        hbm = dict(zip(names, refs[:n]))
        send, recv = refs[n:n + 2]
        starts, waits = _plan_all(jobs, hbm, send, recv)
        for cp in waits:
            cp.wait_recv()
        for cp in starts:
            cp.wait_send()

    res = pl.pallas_call(
        body, name=name, out_shape=tuple(pltpu.HBM(a.shape, a.dtype) for a in held),
        in_specs=[_HBM] * n + [_SEM, _SEM, _ANY], out_specs=tuple([_HBM] * n),
        input_output_aliases={i: i for i in range(n)},
        compiler_params=pltpu.CompilerParams(has_side_effects=_EFFECT),
    )(*held, send_sem, recv_sem, after)
    return dict(zip(names, res))


def _row_tile(rows, bytes_per_row):
    best = 16
    for t in range(16, rows + 1, 16):
        if rows % t == 0 and t * bytes_per_row <= 6 * 1024 * 1024:
            best = t
    return best


def _rowwise(fn, ins, out_dtypes, name):
    rows, cols = ins[0].shape[-2:]
    per_row = sum(a.size // rows * a.dtype.itemsize for a in ins) + sum(cols * jnp.dtype(d).itemsize for d in out_dtypes)
    tr = _row_tile(rows, per_row)
    n_in = len(ins)

    def body(*refs):
        outs = fn(*[r[...] for r in refs[:n_in]])
        for o_ref, o in zip(refs[n_in:], outs):
            o_ref[...] = o.astype(o_ref.dtype)

    def spec(a):
        if a.ndim == 3:
            return pl.BlockSpec((a.shape[0], tr, cols), lambda i: (0, i, 0))
        return pl.BlockSpec((tr, cols), lambda i: (i, 0))

    return pl.pallas_call(
        body, name=name, grid=(rows // tr,),
        in_specs=[spec(a) for a in ins],
        out_specs=[pl.BlockSpec((tr, cols), lambda i: (i, 0)) for _ in out_dtypes],
        out_shape=[jax.ShapeDtypeStruct((rows, cols), d) for d in out_dtypes],
        compiler_params=_params("parallel"),
    )(*ins)


def _tiled(fn, name, grid, pos, ins, outs):
    n_in = len(ins)

    def body(pos_ref, *refs):
        res = fn(*[r[...] for r in refs[:n_in]])
        for o_ref, o in zip(refs[n_in:], res):
            o_ref[...] = o.astype(o_ref.dtype)

    return pl.pallas_call(
        body, name=name,
        grid_spec=pltpu.PrefetchScalarGridSpec(
            num_scalar_prefetch=1, grid=grid,
            in_specs=[pl.BlockSpec(bs, im) for _, bs, im in ins],
            out_specs=[pl.BlockSpec(bs, im) for _, _, bs, im in outs]),
        out_shape=[jax.ShapeDtypeStruct(s, d) for s, d, _, _ in outs],
        compiler_params=_params("parallel"),
    )(pos, *[a for a, _, _ in ins])


def _adamw(w, g, m, v):
    m = ADAM_B1 * m + (1.0 - ADAM_B1) * g
    v = ADAM_B2 * v + (1.0 - ADAM_B2) * (g * g)
    m_hat = m / (1.0 - ADAM_B1 ** ADAM_STEP)
    v_hat = v / (1.0 - ADAM_B2 ** ADAM_STEP)
    return -ADAM_LR * (m_hat / (jnp.sqrt(v_hat) + ADAM_EPS) + ADAM_WD * w), m, v


def _adamw_small(params):
    n = len(params)

    def body(*refs):
        for k in range(n):
            w, g, m, v = (r[...] for r in refs[4 * k:4 * k + 4])
            for o_ref, o in zip(refs[4 * n + 3 * k:4 * n + 3 * k + 3], _adamw(w, g, m, v)):
                o_ref[...] = o

    flat = [a for p in params for a in p]
    return pl.pallas_call(
        body, name="adamw_small",
        out_shape=[jax.ShapeDtypeStruct(p[0].shape, F32) for p in params for _ in range(3)],
    )(*flat)


class _Layout:
    def __init__(self, rows, cols, stacked):
        self.rows, self.cols, self.stacked = rows, cols, stacked

    def whole(self, rows=None):
        r = self.rows if rows is None else rows
        return (N_CHIPS, r, self.cols) if self.stacked else (r, N_CHIPS * self.cols)

    def part_rows(self, h, q=0, nq=1):
        n = self.rows // 2 // nq
        return pl.ds(pl.multiple_of(h * (self.rows // 2) + q * n, 16), n)

    def half_rows(self, h):
        return self.part_rows(h)

    def block(self, ref, p, rows=slice(None)):
        if self.stacked:
            return ref.at[p, rows, :]
        return ref.at[rows, pl.ds(pl.multiple_of(p * self.cols, 128), self.cols)]

    def all_chips(self, ref, rows):
        return ref.at[:, rows, :] if self.stacked else ref.at[rows, :]


BIG = (
    _Layout(IN_SHARD, D_MODEL, True),
    _Layout(ATTN_W, D_MODEL // N_CHIPS, False),
    _Layout(CONV_W, D_MODEL // N_CHIPS, False),
    _Layout(D_MODEL // N_CHIPS, D_MODEL, True),
    _Layout(D_MODEL, FF2 // N_CHIPS, False),
    _Layout(D_FF // N_CHIPS, D_MODEL, True),
)
N_BIG = len(BIG)
_ANY = pl.BlockSpec(memory_space=pl.ANY)


def _position():
    x, y, c = lax.axis_index("x"), lax.axis_index("y"), lax.axis_index("c")
    return x, y, c, 2 * x + y


def _core_of_chip(p, c):
    return (p >> 1, p & 1, c)


def _place_cast(shard, lay, pos, name):
    rows, cols = shard.shape
    tr = _row_tile(rows, cols * 6)
    if lay.stacked:
        out = (lay.whole(), BF16, (None, tr, cols), lambda i, pos: (pos[0], i, 0))
    else:
        out = (lay.whole(), BF16, (tr, cols), lambda i, pos: (i, pos[0]))
    return _tiled(lambda a: (a,), name, (rows // tr,), pos, [(shard, (tr, cols), lambda i, pos: (i, 0))], [out])[0]


def _remote(src, dst, send, recv, k, device):
    return pltpu.make_async_remote_copy(src_ref=src, dst_ref=dst, send_sem=send.at[k], recv_sem=recv.at[k],
                                        device_id=device, device_id_type=MESH)


def _arrival(dst, send, recv, k, me):
    return _remote(dst, dst, send, recv, k, me)


def _gather_ici(lay, name, q=0, nq=1):
    def plan(hbm, pos, send, recv, base):
        x, y, c, me = pos
        rows = lay.part_rows(c, q, nq)
        mine = lay.block(hbm[name], me, rows)
        starts = [_remote(mine, mine, send, recv, base + d - 1, _core_of_chip(me ^ d, c)) for d in (1, 2, 3)]
        waits = [_arrival(lay.block(hbm[name], me ^ d, rows), send, recv, base + d - 1, (x, y, c)) for d in (1, 2, 3)]
        return starts, waits
    return _Job(3, plan)


def _gather_d2d(lay, name, q=0, nq=1):
    def plan(hbm, pos, send, recv, base):
        x, y, c, me = pos
        starts, waits = [], []
        for d in (1, 2, 3):
            got = lay.block(hbm[name], me ^ d, lay.part_rows(c, q, nq))
            starts.append(_remote(got, got, send, recv, base + d - 1, (x, y, 1 - c)))
            waits.append(_arrival(lay.block(hbm[name], me ^ d, lay.part_rows(1 - c, q, nq)), send, recv, base + d - 1,
                                  (x, y, c)))
        return starts, waits
    return _Job(3, plan)


def _rs_pair(lay, grad, theirs):
    def plan(hbm, pos, send, recv, base):
        x, y, c, _ = pos
        out = _remote(lay.all_chips(hbm[grad], lay.half_rows(1 - c)), hbm[theirs], send, recv, base, (x, y, 1 - c))
        return [out], [_arrival(hbm[theirs], send, recv, base, (x, y, c))]
    return _Job(1, plan)


def _rs_chips(lay, sums, slots):
    def plan(hbm, pos, send, recv, base):
        x, y, c, me = pos
        starts = [_remote(lay.block(hbm[sums], me ^ d), hbm[slots].at[me], send, recv, base + d - 1,
                          _core_of_chip(me ^ d, c)) for d in (1, 2, 3)]
        waits = [_arrival(hbm[slots].at[me ^ d], send, recv, base + d - 1, (x, y, c)) for d in (1, 2, 3)]
        return starts, waits
    return _Job(3, plan)


def _rs_share(lay, shard):
    def plan(hbm, pos, send, recv, base):
        x, y, c, _ = pos
        mine = hbm[shard].at[lay.half_rows(c), :]
        other = hbm[shard].at[lay.half_rows(1 - c), :]
        return [_remote(mine, mine, send, recv, base, (x, y, 1 - c))], [_arrival(other, send, recv, base, (x, y, c))]
    return _Job(1, plan)


def _slots_shape(lay):
    return jax.ShapeDtypeStruct((N_CHIPS, lay.rows // 2, lay.cols), BF16)


def _theirs_shape(lay):
    return jax.ShapeDtypeStruct(lay.whole(lay.rows // 2), BF16)


def _pair_sum(grad, theirs, lay, pos, name):
    half = lay.rows // 2
    add = lambda a, b: (a.astype(F32) + b.astype(F32),)
    if lay.stacked:
        tr = _row_tile(half, lay.cols * 6)
        nt = half // tr
        flat = lambda a: a.reshape(-1, lay.cols)
        mine = lambda t, pos: ((t // nt) * (2 * nt) + pos[1] * nt + t % nt, 0)
        grid, blk = (N_CHIPS * nt,), (tr, lay.cols)
        grad, theirs = flat(grad), flat(theirs)
    else:
        tr = _row_tile(half, N_CHIPS * lay.cols * 6)
        nt = half // tr
        mine = lambda t, pos: (pos[1] * nt + t, 0)
        grid, blk = (nt,), (tr, N_CHIPS * lay.cols)
    same = lambda t, pos: (t, 0)
    out = _tiled(add, name, grid, pos, [(grad, blk, mine), (theirs, blk, same)], [(theirs.shape, BF16, blk, same)])[0]
    return out.reshape(lay.whole(half))


def _chip_sum(sums, slots, lay, pos, name, after=None):
    half = lay.rows // 2
    tr = _row_tile(half, lay.cols * 12)
    nt = half // tr
    blk3 = (None, tr, lay.cols)
    if lay.stacked:
        own = (sums, blk3, lambda i, pos: (pos[0], i, 0))
    else:
        own = (sums, (tr, lay.cols), lambda i, pos: (i, pos[0]))
    others = [(slots, blk3, functools.partial(lambda d, i, pos: (pos[0] ^ d, i, 0), d)) for d in (1, 2, 3)]

    def add(a, b1, b2, b3, *_):
        return (((a.astype(F32) + b1.astype(F32)) + b2.astype(F32)) + b3.astype(F32),)

    if after is not None:
        others.append((after, (8, 128), lambda i, pos: (0, 0)))
    return _tiled(add, name, (nt,), pos, [own] + others,
                  [((lay.rows, lay.cols), F32, (tr, lay.cols), lambda i, pos: (pos[1] * nt + i, 0))])[0]


N_DEV = 8


def _exchange_small(v, reduce):
    rows = v.shape[0]

    def body(v_ref, o_ref, *scratch):
        if reduce:
            slots, send, recv = scratch
        else:
            slots, (send, recv) = o_ref, scratch
        x, y, c = lax.axis_index("x"), lax.axis_index("y"), lax.axis_index("c")
        idx = 4 * x + 2 * y + c
        slots[idx] = v_ref[...]

        def to_peer(k):
            return pltpu.make_async_remote_copy(
                src_ref=v_ref, dst_ref=slots.at[idx], send_sem=send.at[k - 1], recv_sem=recv.at[k - 1],
                device_id=(x ^ (k >> 2), y ^ ((k >> 1) & 1), c ^ (k & 1)), device_id_type=MESH)

        def from_peer(k):
            return pltpu.make_async_remote_copy(
                src_ref=v_ref, dst_ref=slots.at[idx ^ k], send_sem=send.at[k - 1], recv_sem=recv.at[k - 1],
                device_id=(x, y, c), device_id_type=MESH)

        for k in range(1, N_DEV):
            to_peer(k).start()
        for k in range(1, N_DEV):
            from_peer(k).wait_recv()
        for k in range(1, N_DEV):
            to_peer(k).wait_send()
        if reduce:
            acc = slots[0]
            for q in range(1, N_DEV):
                acc = acc + slots[q]
            o_ref[...] = acc

    sems = pltpu.SemaphoreType.DMA((N_DEV - 1,))
    stacked = jax.ShapeDtypeStruct((N_DEV, rows, 128), F32)
    return pl.pallas_call(
        body, name="allreduce_small" if reduce else "allgather_small",
        out_shape=jax.ShapeDtypeStruct((rows, 128), F32) if reduce else stacked,
        scratch_shapes=([pltpu.VMEM((N_DEV, rows, 128), F32)] if reduce else []) + [sems, sems],
    )(v)


def _pack_rows(parts):
    padded = [jnp.pad(a, ((0, -a.shape[0] % 8), (0, 0))) for a in parts]
    starts = [sum(p.shape[0] for p in padded[:k]) for k in range(len(padded))]
    return jnp.concatenate(padded, axis=0), starts


def kernel(x, mix_norm, w_in, b_in, sinks, conv_w, w_attn_branch, w_conv_branch, w_out, ffn_norm, w_up, ffn_conv_w, w_down, final_norm, loss_target, m_mix_norm, m_w_in, m_b_in, m_sinks, m_conv_w, m_w_attn_branch, m_w_conv_branch, m_w_out, m_ffn_norm, m_w_up, m_ffn_conv_w, m_w_down, m_final_norm, v_mix_norm, v_w_in, v_b_in, v_sinks, v_conv_w, v_w_attn_branch, v_w_conv_branch, v_w_out, v_ffn_norm, v_w_up, v_ffn_conv_w, v_w_down, v_final_norm):
    me = 2 * lax.axis_index("x") + lax.axis_index("y")
    big_w = [w_in[0].T, w_attn_branch[0], w_conv_branch[0], w_out[0], w_up[0], w_down[0]]
    big_m = [m_w_in[0].T, m_w_attn_branch[0], m_w_conv_branch[0], m_w_out[0], m_w_up[0], m_w_down[0]]
    big_v = [v_w_in[0].T, v_w_attn_branch[0], v_w_conv_branch[0], v_w_out[0], v_w_up[0], v_w_down[0]]
    names = ("w_in", "w_ab", "w_cb", "w_out", "w_up", "w_down")

    pos = jnp.stack([me, lax.axis_index("c")]).astype(jnp.int32)

    lay = dict(zip(names, BIG))
    xs, target, sk = x[0], loss_target[0], sinks[0]
    s = xs.shape[0]
    tm, tm2, bk = min(256, s), min(512, s), min(1024, s)

    placed = {n: _place_cast(w, lay[n], pos, "cast_" + n) for w, n in zip(big_w, names)}
    taps, (_, t0) = _pack_rows([conv_w[0], ffn_conv_w[0].reshape(3 * (FF2 // N_CHIPS // 128), 128)])
    taps = _exchange_small(taps, reduce=False)[0::2]
    conv_full = taps[:, 0:3].transpose(1, 0, 2).reshape(3, CONV_W)
    ffn_cw_full = taps[:, t0:t0 + 33].reshape(N_CHIPS, 3, FF2 // N_CHIPS).transpose(1, 0, 2).reshape(3, FF2)

    w_in_full = _exchange(
        "gather_in", [[_gather_ici(lay["w_in"], "w_in")], [_gather_d2d(lay["w_in"], "w_in")]],
        bufs={"w_in": placed["w_in"]})["w_in"].reshape(IN_W, D_MODEL)
    early = ("w_ab", "w_cb", "w_out", "w_down")
    k1 = _Carry([_gather_ici(lay[n], n) for n in early], bufs={n: placed[n] for n in early})
    xn, qkv, c3, gates = _inproj_fwd(xs, mix_norm, w_in_full, b_in, tm2, comm=k1)
    k2 = _Carry([_gather_d2d(lay[n], n) for n in early] + [_gather_ici(lay["w_up"], "w_up", 0, 2)],
                bufs={**k1.out, "w_up": placed["w_up"]})
    attn = _attn_fwd(qkv, sk, comm=k2)
    w_ab, w_cb = k2.out["w_ab"], k2.out["w_cb"]
    w_out_full = k2.out["w_out"].reshape(D_MODEL, D_MODEL)
    w_down_full = k2.out["w_down"].reshape(D_FF, D_MODEL)
    k3 = _Carry([_gather_d2d(lay["w_up"], "w_up", 0, 2), _gather_ici(lay["w_up"], "w_up", 1, 2)],
                bufs={"w_up": k2.out["w_up"]})
    conv, a, cv, merged, h1, hn = _mix_fwd(xs, attn, c3, gates, conv_full, w_ab, w_cb, w_out_full, ffn_norm, tm, comm=k3)
    w_up_full = _exchange("gather_up_tail", [[_gather_d2d(lay["w_up"], "w_up", 1, 2)]],
                          bufs={"w_up": k3.out["w_up"]})["w_up"]
    u, up, act, dh2, loss_part, g_fn = _ffn_fwd_loss(hn, h1, w_up_full, ffn_cw_full, w_down_full,
                                                     final_norm[None, :], target, tm)

    grads, sums, slots = {}, {}, {}

    def pair(*ws):
        return _Carry([_rs_pair(lay[n], "g_" + n, "t_" + n) for n in ws], reads={"g_" + n: grads[n] for n in ws},
                      fresh={"t_" + n: _theirs_shape(lay[n]) for n in ws})

    def chips(*ws, also=None):
        k = _Carry([_rs_chips(lay[n], "s_" + n, "r_" + n) for n in ws], reads={"s_" + n: sums[n] for n in ws},
                   fresh={"r_" + n: _slots_shape(lay[n]) for n in ws})
        if also is not None:
            k = _Carry(k.jobs + also.jobs, {**k.reads, **also.reads}, None, {**k.fresh, **also.fresh})
        return k

    def pair_sums(k, *ws):
        for n in ws:
            sums[n] = _pair_sum(grads[n], k.out["t_" + n], lay[n], pos, "pair_sum_" + n)

    def take_slots(k, *ws):
        for n in ws:
            slots[n] = k.out["r_" + n]

    du, dh1, g_fcw, g_g2 = _ffn_bwd(dh2, u, up, h1, w_up_full, ffn_cw_full, w_down_full, ffn_norm, tm)
    grads["w_down"] = _wgrad(act, dh2, D_FF // 2, D_MODEL, bk, "wgrad_down").reshape(lay["w_down"].whole())
    k4 = pair("w_down")
    grads["w_up"] = _wgrad(hn, du, D_MODEL, FF2 // 4, bk, "wgrad_up", comm=k4)
    pair_sums(k4, "w_down")
    k5 = chips("w_down", also=pair("w_up"))
    dattn, da, dcv, dc3, dgt, g_cw = _mix_bwd(dh1, gates, a, cv, c3, conv_full, w_ab, w_cb, w_out_full, tm, comm=k5)
    take_slots(k5, "w_down")
    pair_sums(k5, "w_up")
    grads["w_out"] = _wgrad(merged, dh1, D_MODEL, D_MODEL, bk, "wgrad_out").reshape(lay["w_out"].whole())
    grads["w_ab"] = _wgrad(attn, da, ATTN_W, D_MODEL, bk, "wgrad_ab")
    grads["w_cb"] = _wgrad(conv, dcv, CONV_W, D_MODEL, bk, "wgrad_cb")
    k6 = chips("w_up", also=pair("w_out", "w_ab", "w_cb"))
    dq, dk, dv, g_sk = _attn_bwd(qkv, sk, attn, dattn, comm=k6)
    take_slots(k6, "w_up")
    pair_sums(k6, "w_out", "w_ab", "w_cb")
    grad_x, dproj, g_b, g_g1 = _inproj_bwd(dq, dk, dv, dc3, dgt, w_in_full, xs, dh1, mix_norm, tm2)
    k8 = chips("w_out", "w_ab", "w_cb")
    grads["w_in"] = _wgrad_in(xn, dproj, min(512, s), comm=k8).reshape(lay["w_in"].whole())
    take_slots(k8, "w_out", "w_ab", "w_cb")
    sums["w_in"] = _pair_sum(
        grads["w_in"],
        _exchange("rs_pair_in", [[_rs_pair(lay["w_in"], "g", "t")]], reads={"g": grads["w_in"]},
                  fresh={"t": _theirs_shape(lay["w_in"])})["t"],
        lay["w_in"], pos, "pair_sum_w_in")
    in_flight, token = _start_exchange("rs_chips_in_start", [_rs_chips(lay["w_in"], "s", "r")],
                                       {"s": sums["w_in"], "r": _slots_shape(lay["w_in"])})
    others = names[1:]
    halves = {n: _chip_sum(sums[n], slots[n], lay[n], pos, "chip_sum_" + n, after=token) for n in others}
    shared = _exchange("share_halves", [[_rs_share(lay[n], n) for n in others]], bufs=halves)
    w_of, m_of, v_of = dict(zip(names, big_w)), dict(zip(names, big_m)), dict(zip(names, big_v))
    adam = lambda n, g: _rowwise(_adamw, [w_of[n], g, m_of[n], v_of[n]], [F32, F32, F32], "adamw_" + n)
    new_of = {n: adam(n, shared[n]) for n in others}
    small = dict(loss=loss_part, g_g1=g_g1, g_b=g_b, g_sk=g_sk, g_cw=g_cw, g_g2=g_g2, g_fcw=g_fcw, g_fn=g_fn)

    parts = [small["loss"], small["g_g1"], small["g_b"], jnp.pad(small["g_sk"][:, 0], (0, 120))[None, :],
             small["g_cw"], small["g_g2"], small["g_fcw"], small["g_fn"]]
    packed, at = _pack_rows([a.reshape(-1, 128) for a in parts])
    total = _exchange_small(packed, reduce=True)
    part = lambda k: total[at[k]:at[k] + parts[k].size // 128].reshape(parts[k].shape)
    loss = total[0, 0]
    g_mix, g_b, g_g2, g_fn = part(1), part(2), part(5), part(7)
    g_sk = part(3)[:, 0:N_HEADS]
    g_cw = lax.dynamic_slice(part(4), (0, me * 128), (3, 128))
    g_fcw = lax.dynamic_slice(part(6), (0, me * (FF2 // N_CHIPS)), (3, FF2 // N_CHIPS))
    small_p = [
        (mix_norm, g_mix, m_mix_norm, v_mix_norm), (b_in, g_b, m_b_in, v_b_in), (sinks, g_sk, m_sinks, v_sinks),
        (conv_w[0], g_cw, m_conv_w[0], v_conv_w[0]), (ffn_norm, g_g2, m_ffn_norm, v_ffn_norm),
        (ffn_conv_w[0], g_fcw, m_ffn_conv_w[0], v_ffn_conv_w[0]),
        (final_norm[None, :], g_fn, m_final_norm[None, :], v_final_norm[None, :])]
    small_new = _adamw_small(small_p)
    small_new = [small_new[3 * k:3 * k + 3] for k in range(len(small_p))]

    landed = _finish_exchange("rs_chips_in_wait", in_flight, after=small_new[0][0])
    half_in = _chip_sum(landed["s"], landed["r"], lay["w_in"], pos, "chip_sum_w_in")
    shared["w_in"] = _exchange("share_in", [[_rs_share(lay["w_in"], "w_in")]], bufs={"w_in": half_in})["w_in"]
    new_of["w_in"] = adam("w_in", shared["w_in"])
    big_g = [shared[n] for n in names]
    big_new = [new_of[n] for n in names]

    order = [("s", 0), ("b", 0), ("s", 1), ("s", 2), ("s", 3), ("b", 1), ("b", 2), ("b", 3), ("s", 4), ("b", 4),
             ("s", 5), ("b", 5), ("s", 6)]
    shapes = [mix_norm.shape, w_in.shape, b_in.shape, sinks.shape, conv_w.shape, w_attn_branch.shape,
              w_conv_branch.shape, w_out.shape, ffn_norm.shape, w_up.shape, ffn_conv_w.shape, w_down.shape,
              final_norm.shape]
    small_g = [p[1] for p in small_p]
    big_g[0] = big_g[0].T
    big_new[0] = [a.T for a in big_new[0]]
    out_g = [(small_g[k] if kind == "s" else big_g[k]).reshape(shp) for (kind, k), shp in zip(order, shapes)]
    news = [[(small_new[k][j] if kind == "s" else big_new[k][j]).reshape(shp) for (kind, k), shp in zip(order, shapes)]
            for j in range(3)]
    return (loss, grad_x[None], *out_g, *news[0], *news[1], *news[2])
```

```python
import functools

import jax
import jax.numpy as jnp
from jax import lax
from jax.experimental import pallas as pl
from jax.experimental.pallas import tpu as pltpu

F32 = jnp.float32
BF16 = jnp.bfloat16

D_MODEL = 1024
HEAD_DIM = 64
N_HEADS = 8
N_KV_HEADS = 2
GROUP = N_HEADS // N_KV_HEADS
BLOCK = 128
ATTN_SCALE = HEAD_DIM ** -0.5
ATTN_W = N_HEADS * HEAD_DIM
KV_W = N_KV_HEADS * HEAD_DIM
CONV_W = 512
QKV_W = ATTN_W + 2 * KV_W
C3_W = 3 * CONV_W
GATES_W = 2 * D_MODEL
IN_W = QKV_W + C3_W + GATES_W
D_FF = 2816
FF2 = 2 * D_FF
NORM_EPS = 1e-5
N_CHIPS = 4
IN_SHARD = IN_W // N_CHIPS
NEG = -1e30

ADAM_LR = 0.001
ADAM_B1 = 0.9
ADAM_B2 = 0.999
ADAM_EPS = 1e-08
ADAM_WD = 0.01
ADAM_STEP = 10

VMEM_LIMIT = 56 * 1024 * 1024
MESH = pl.DeviceIdType.MESH

NT = (((1,), (1,)), ((), ()))
TN = (((0,), (0,)), ((), ()))


def _params(*sem):
    return pltpu.CompilerParams(dimension_semantics=sem, vmem_limit_bytes=VMEM_LIMIT)


def _resident(shape):
    return pl.BlockSpec(shape, lambda *_: (0,) * len(shape), pipeline_mode=pl.Buffered(1))


def _sigmoid(v):
    return 0.5 * jnp.tanh(0.5 * v) + 0.5


def _rstd(v):
    return lax.rsqrt(jnp.mean(v * v, axis=-1, keepdims=True) + NORM_EPS)


def _rms_bwd(dy, v, rstd, g):
    vhat = v * rstd
    t = dy * g
    return rstd * (t - vhat * jnp.mean(t * vhat, axis=-1, keepdims=True)), dy * vhat


def _taps(z, cw):
    return cw[2:3] * z + cw[1:2] * pltpu.roll(z, 1, 0) + cw[0:1] * pltpu.roll(z, 2, 0)


def _causal_conv(z, prev, cw):
    edge = _taps(jnp.concatenate([prev, z[0:8]], axis=0), cw)
    return jnp.concatenate([edge[8:16], _taps(z, cw)[8:]], axis=0)


def _rows_after(z, nxt):
    n = z.shape[0]
    edge = jnp.concatenate([z[n - 8:n], nxt], axis=0)
    return tuple(jnp.concatenate([pltpu.roll(z, n - k, 0)[:n - 8], pltpu.roll(edge, 16 - k, 0)[0:8]], axis=0)
                 for k in (1, 2))


def _inproj_fwd(x, g1, w_in, b_in, tm, comm=None):
    s = x.shape[0]

    def body(x_ref, g_ref, w_ref, b_ref, xn_ref, qkv_ref, c3_ref, gt_ref):
        xf = x_ref[...]
        xn = (xf * _rstd(xf) * g_ref[...]).astype(BF16)
        xn_ref[...] = xn

        def seg(a, b):
            return lax.dot_general(xn, w_ref[a:b, :], NT, preferred_element_type=F32) + b_ref[:, a:b]

        qkv_ref[...] = seg(0, QKV_W).astype(BF16)
        c3_ref[...] = seg(QKV_W, QKV_W + C3_W)
        gt_ref[...] = seg(QKV_W + C3_W, IN_W)

    row = lambda w: pl.BlockSpec((tm, w), lambda i: (i, 0))
    return _call(
        comm, body, name="inproj_fwd", grid=(s // tm,),
        in_specs=[row(D_MODEL), _resident((1, D_MODEL)), _resident((IN_W, D_MODEL)), _resident((1, IN_W))],
        out_specs=[row(D_MODEL), row(QKV_W), row(C3_W), row(GATES_W)],
        out_shape=[jax.ShapeDtypeStruct((s, D_MODEL), BF16), jax.ShapeDtypeStruct((s, QKV_W), BF16),
                   jax.ShapeDtypeStruct((s, C3_W), F32), jax.ShapeDtypeStruct((s, GATES_W), F32)],
        compiler_params=_params("parallel"),
    )(x, g1, w_in, b_in)


def _attn_mask(first_block):
    qi = lax.broadcasted_iota(jnp.int32, (GROUP * BLOCK, 2 * BLOCK), 0) & (BLOCK - 1)
    kj = lax.broadcasted_iota(jnp.int32, (GROUP * BLOCK, 2 * BLOCK), 1)
    band = (kj > qi) & (kj <= qi + BLOCK)
    return band & ((kj >= BLOCK) | jnp.logical_not(first_block))


def _sink_column(sk_ref, h):
    rows = lax.broadcasted_iota(jnp.int32, (GROUP * BLOCK, 1), 0)
    col = jnp.full((GROUP * BLOCK, 1), sk_ref[h * GROUP], F32)
    for g in range(1, GROUP):
        col = jnp.where(rows >= g * BLOCK, sk_ref[h * GROUP + g], col)
    return col


def _stack_heads(t, h):
    return jnp.concatenate(
        [t[:, (h * GROUP + g) * HEAD_DIM:(h * GROUP + g + 1) * HEAD_DIM] for g in range(GROUP)], axis=0)


def _unstack_heads(per_kv):
    return jnp.concatenate(
        [t[g * BLOCK:(g + 1) * BLOCK] for t in per_kv for g in range(GROUP)], axis=1)


def _attn_specs(nb):
    cur = lambda i: jnp.minimum(i, nb - 1)
    prev = lambda i: jnp.maximum(jnp.minimum(i, nb - 1) - 1, 0)
    q = pl.BlockSpec((BLOCK, ATTN_W), lambda i: (cur(i), 0))
    kp = pl.BlockSpec((BLOCK, KV_W), lambda i: (prev(i), ATTN_W // KV_W))
    kc = pl.BlockSpec((BLOCK, KV_W), lambda i: (cur(i), ATTN_W // KV_W))
    vp = pl.BlockSpec((BLOCK, KV_W), lambda i: (prev(i), ATTN_W // KV_W + 1))
    vc = pl.BlockSpec((BLOCK, KV_W), lambda i: (cur(i), ATTN_W // KV_W + 1))
    return q, kp, kc, vp, vc


def _attn_fwd(qkv, sinks, comm=None):
    s = qkv.shape[0]
    nb = s // BLOCK

    def body(sk_ref, q_ref, kp_ref, kc_ref, vp_ref, vc_ref, o_ref):
        mask = _attn_mask(pl.program_id(0) == 0)
        q, kp, kc, vp, vc = q_ref[...], kp_ref[...], kc_ref[...], vp_ref[...], vc_ref[...]
        outs = []
        for h in range(N_KV_HEADS):
            hs = slice(h * HEAD_DIM, (h + 1) * HEAD_DIM)
            k2 = jnp.concatenate([kp[:, hs], kc[:, hs]], axis=0)
            v2 = jnp.concatenate([vp[:, hs], vc[:, hs]], axis=0)
            sc = lax.dot_general(_stack_heads(q, h), k2, NT, preferred_element_type=F32) * ATTN_SCALE
            sc = jnp.where(mask, sc, NEG)
            sink = _sink_column(sk_ref, h)
            m = jnp.maximum(jnp.max(sc, axis=1, keepdims=True), sink)
            p = jnp.exp(sc - m)
            den = jnp.sum(p, axis=1, keepdims=True) + jnp.exp(sink - m)
            outs.append(jnp.dot(p.astype(BF16), v2, preferred_element_type=F32) / den)
        o_ref[...] = _unstack_heads(outs).astype(BF16)

    return _call(
        comm, body, name="attn_fwd", grid=(nb,),
        in_specs=[pl.BlockSpec(memory_space=pltpu.SMEM), *_attn_specs(nb)],
        out_specs=pl.BlockSpec((BLOCK, ATTN_W), lambda i: (i, 0)),
        out_shape=jax.ShapeDtypeStruct((s, ATTN_W), BF16),
        compiler_params=_params("parallel"),
    )(sinks, qkv, qkv, qkv, qkv, qkv)


def _mix_fwd(x, attn, c3, gates, conv_w, w_ab, w_cb, w_out, g2, tm, comm=None):
    s = x.shape[0]

    def body(x_ref, at_ref, c3_ref, gt_ref, cw_ref, wab_ref, wcb_ref, wo_ref, g_ref,
             conv_ref, a_ref, cv_ref, mg_ref, h1_ref, hn_ref, carry_ref):
        @pl.when(pl.program_id(0) == 0)
        def _():
            carry_ref[...] = jnp.zeros_like(carry_ref)

        c3v = c3_ref[...]
        cb, cc, cx = c3v[:, :CONV_W], c3v[:, CONV_W:2 * CONV_W], c3v[:, 2 * CONV_W:]
        z = cc * cx
        cz = _causal_conv(z, carry_ref[...], cw_ref[...])
        carry_ref[...] = z[tm - 8:tm]
        conv = (cb * cz).astype(BF16)
        conv_ref[...] = conv
        a = jnp.dot(at_ref[...], wab_ref[...], preferred_element_type=F32)
        cv = jnp.dot(conv, wcb_ref[...], preferred_element_type=F32)
        a_ref[...] = a.astype(BF16)
        cv_ref[...] = cv.astype(BF16)
        gt = gt_ref[...]
        merged = (_sigmoid(gt[:, :D_MODEL]) * a + _sigmoid(gt[:, D_MODEL:]) * cv).astype(BF16)
        mg_ref[...] = merged
        h1 = x_ref[...] + jnp.dot(merged, wo_ref[...], preferred_element_type=F32)
        h1_ref[...] = h1
        hn_ref[...] = (h1 * _rstd(h1) * g_ref[...]).astype(BF16)

    row = lambda w: pl.BlockSpec((tm, w), lambda i: (i, 0))
    return _call(
        comm, body, name="mix_fwd", grid=(s // tm,),
        in_specs=[row(D_MODEL), row(ATTN_W), row(C3_W), row(GATES_W), _resident((3, CONV_W)),
                  _resident((ATTN_W, D_MODEL)), _resident((CONV_W, D_MODEL)), _resident((D_MODEL, D_MODEL)),
                  _resident((1, D_MODEL))],
        out_specs=[row(CONV_W), row(D_MODEL), row(D_MODEL), row(D_MODEL), row(D_MODEL), row(D_MODEL)],
        out_shape=[jax.ShapeDtypeStruct((s, CONV_W), BF16), jax.ShapeDtypeStruct((s, D_MODEL), BF16),
                   jax.ShapeDtypeStruct((s, D_MODEL), BF16), jax.ShapeDtypeStruct((s, D_MODEL), BF16),
                   jax.ShapeDtypeStruct((s, D_MODEL), F32), jax.ShapeDtypeStruct((s, D_MODEL), BF16)],
        scratch_shapes=[pltpu.VMEM((8, CONV_W), F32)],
        compiler_params=_params("arbitrary"),
    )(x, attn, c3, gates, conv_w, w_ab, w_cb, w_out, g2)


def _ffn_fwd_loss(hn, h1, w_up, ffn_cw, w_down, g3, target, tm):
    s = hn.shape[0]

    def body(hn_ref, h1_ref, wu_ref, cw_ref, wd_ref, g_ref, t_ref,
             u_ref, up_ref, act_ref, dh2_ref, loss_ref, gfn_ref, carry_ref):
        @pl.when(pl.program_id(0) == 0)
        def _():
            carry_ref[...] = jnp.zeros_like(carry_ref)
            loss_ref[...] = jnp.zeros_like(loss_ref)
            gfn_ref[...] = jnp.zeros_like(gfn_ref)

        u = jnp.dot(hn_ref[...], wu_ref[...], preferred_element_type=F32)
        u_ref[...] = u.astype(BF16)
        up = _causal_conv(u, carry_ref[...], cw_ref[...])
        up_ref[...] = up.astype(BF16)
        carry_ref[...] = u[tm - 8:tm]
        gate, val = up[:, :D_FF], up[:, D_FF:]
        act = (gate * _sigmoid(gate) * val).astype(BF16)
        act_ref[...] = act
        h2 = h1_ref[...] + jnp.dot(act, wd_ref[...], preferred_element_type=F32)
        rstd = _rstd(h2)
        g = g_ref[...]
        err = h2 * rstd * g - t_ref[...]
        loss_ref[...] += jnp.sum(err * err) * (0.5 / D_MODEL)
        dh2, dg = _rms_bwd(err * (1.0 / D_MODEL), h2, rstd, g)
        dh2_ref[...] = dh2
        gfn_ref[...] += jnp.sum(dg, axis=0, keepdims=True)

    row = lambda w: pl.BlockSpec((tm, w), lambda i: (i, 0))
    acc = lambda w: pl.BlockSpec((1, w), lambda i: (0, 0))
    return pl.pallas_call(
        body, name="ffn_fwd_loss", grid=(s // tm,),
        in_specs=[row(D_MODEL), row(D_MODEL), _resident((D_MODEL, FF2)), _resident((3, FF2)),
                  _resident((D_FF, D_MODEL)), _resident((1, D_MODEL)), row(D_MODEL)],
        out_specs=[row(FF2), row(FF2), row(D_FF), row(D_MODEL), acc(128), acc(D_MODEL)],
        out_shape=[jax.ShapeDtypeStruct((s, FF2), BF16), jax.ShapeDtypeStruct((s, FF2), BF16),
                   jax.ShapeDtypeStruct((s, D_FF), BF16),
                   jax.ShapeDtypeStruct((s, D_MODEL), F32), jax.ShapeDtypeStruct((1, 128), F32),
                   jax.ShapeDtypeStruct((1, D_MODEL), F32)],
        scratch_shapes=[pltpu.VMEM((8, FF2), F32)],
        compiler_params=_params("arbitrary"),
    )(hn, h1, w_up, ffn_cw, w_down, g3, target)


def _ffn_bwd(dh2, u, up, h1, w_up, ffn_cw, w_down, g2, tm):
    s = dh2.shape[0]
    nt = s // tm

    def body(dh2_ref, u_ref, up_ref, h1_ref, wu_ref, cw_ref, wd_ref, g_ref,
             du_ref, dh1_ref, gcw_ref, gg_ref, carry_ref):
        @pl.when(pl.program_id(0) == 0)
        def _():
            carry_ref[...] = jnp.zeros_like(carry_ref)
            gcw_ref[...] = jnp.zeros_like(gcw_ref)
            gg_ref[...] = jnp.zeros_like(gg_ref)

        dh2v = dh2_ref[...]
        dact = lax.dot_general(dh2v.astype(BF16), wd_ref[...], NT, preferred_element_type=F32)
        upv = up_ref[...].astype(F32)
        gate, val = upv[:, :D_FF], upv[:, D_FF:]
        sg = _sigmoid(gate)
        dval = dact * (gate * sg)
        dgate = dact * val * (sg * (1.0 + gate * (1.0 - sg)))
        dup = jnp.concatenate([dgate, dval], axis=1)
        dup1, dup2 = _rows_after(dup, carry_ref[...])
        carry_ref[...] = dup[0:8]
        u = u_ref[...].astype(F32)
        gcw_ref[2:3, :] += jnp.sum(dup * u, axis=0, keepdims=True)
        gcw_ref[1:2, :] += jnp.sum(dup1 * u, axis=0, keepdims=True)
        gcw_ref[0:1, :] += jnp.sum(dup2 * u, axis=0, keepdims=True)
        cw = cw_ref[...]
        du = (cw[2:3] * dup + cw[1:2] * dup1 + cw[0:1] * dup2).astype(BF16)
        du_ref[...] = du
        dhn = lax.dot_general(du, wu_ref[...], NT, preferred_element_type=F32)
        h1v = h1_ref[...]
        dh1, dg = _rms_bwd(dhn, h1v, _rstd(h1v), g_ref[...])
        dh1_ref[...] = dh2v + dh1
        gg_ref[...] += jnp.sum(dg, axis=0, keepdims=True)

    row = lambda w: pl.BlockSpec((tm, w), lambda i: (nt - 1 - i, 0))
    return pl.pallas_call(
        body, name="ffn_bwd", grid=(nt,),
        in_specs=[row(D_MODEL), row(FF2), row(FF2),
                  row(D_MODEL), _resident((D_MODEL, FF2)), _resident((3, FF2)), _resident((D_FF, D_MODEL)),
                  _resident((1, D_MODEL))],
        out_specs=[row(FF2), row(D_MODEL), pl.BlockSpec((3, FF2), lambda i: (0, 0)),
                   pl.BlockSpec((1, D_MODEL), lambda i: (0, 0))],
        out_shape=[jax.ShapeDtypeStruct((s, FF2), BF16), jax.ShapeDtypeStruct((s, D_MODEL), F32),
                   jax.ShapeDtypeStruct((3, FF2), F32), jax.ShapeDtypeStruct((1, D_MODEL), F32)],
        scratch_shapes=[pltpu.VMEM((8, FF2), F32)],
        compiler_params=_params("arbitrary"),
    )(dh2, u, up, h1, w_up, ffn_cw, w_down, g2)


def _mix_bwd(dh1, gates, a, cv, c3, conv_w, w_ab, w_cb, w_out, tm, comm=None):
    s = dh1.shape[0]
    nt = s // tm
    halo = 8

    def body(dh1_ref, gt_ref, a_ref, cv_ref, c3_ref, ch_ref, cw_ref, wab_ref, wcb_ref, wo_ref,
             dat_ref, da_ref, dcv_ref, dc3_ref, dgt_ref, gcw_ref, carry_ref):
        i = pl.program_id(0)

        @pl.when(i == 0)
        def _():
            carry_ref[...] = jnp.zeros_like(carry_ref)
            gcw_ref[...] = jnp.zeros_like(gcw_ref)

        dm = lax.dot_general(dh1_ref[...].astype(BF16), wo_ref[...], NT, preferred_element_type=F32)
        gt = gt_ref[...]
        sa, sc = _sigmoid(gt[:, :D_MODEL]), _sigmoid(gt[:, D_MODEL:])
        da = (dm * sa).astype(BF16)
        dcv = (dm * sc).astype(BF16)
        da_ref[...] = da
        dcv_ref[...] = dcv
        dgt_ref[...] = jnp.concatenate(
            [dm * a_ref[...].astype(F32) * (sa * (1.0 - sa)), dm * cv_ref[...].astype(F32) * (sc * (1.0 - sc))],
            axis=1).astype(BF16)
        dat_ref[...] = lax.dot_general(da, wab_ref[...], NT, preferred_element_type=F32).astype(BF16)
        dconv = lax.dot_general(dcv, wcb_ref[...], NT, preferred_element_type=F32)
        c3v = c3_ref[...]
        cb, cc, cx = c3v[:, :CONV_W], c3v[:, CONV_W:2 * CONV_W], c3v[:, 2 * CONV_W:]
        z = cc * cx
        chv = ch_ref[...] * (i < nt - 1).astype(F32)
        zh = chv[:, CONV_W:2 * CONV_W] * chv[:, 2 * CONV_W:]
        cw = cw_ref[...]
        cz = _causal_conv(z, zh, cw)
        dcz = dconv * cb
        dcz1, dcz2 = _rows_after(dcz, carry_ref[...])
        carry_ref[...] = dcz[0:8]
        gcw_ref[2:3, :] += jnp.sum(dcz * z, axis=0, keepdims=True)
        gcw_ref[1:2, :] += jnp.sum(dcz1 * z, axis=0, keepdims=True)
        gcw_ref[0:1, :] += jnp.sum(dcz2 * z, axis=0, keepdims=True)
        dz = cw[2:3] * dcz + cw[1:2] * dcz1 + cw[0:1] * dcz2
        dc3_ref[...] = jnp.concatenate([dconv * cz, dz * cx, dz * cc], axis=1).astype(BF16)

    row = lambda w: pl.BlockSpec((tm, w), lambda i: (nt - 1 - i, 0))
    return _call(
        comm, body, name="mix_bwd", grid=(nt,),
        in_specs=[row(D_MODEL), row(GATES_W), row(D_MODEL), row(D_MODEL), row(C3_W),
                  pl.BlockSpec((halo, C3_W), lambda i: (jnp.maximum((nt - 1 - i) * (tm // halo) - 1, 0), 0)),
                  _resident((3, CONV_W)), _resident((ATTN_W, D_MODEL)), _resident((CONV_W, D_MODEL)),
                  _resident((D_MODEL, D_MODEL))],
        out_specs=[row(ATTN_W), row(D_MODEL), row(D_MODEL), row(C3_W), row(GATES_W),
                   pl.BlockSpec((3, CONV_W), lambda i: (0, 0))],
        out_shape=[jax.ShapeDtypeStruct((s, ATTN_W), BF16), jax.ShapeDtypeStruct((s, D_MODEL), BF16),
                   jax.ShapeDtypeStruct((s, D_MODEL), BF16), jax.ShapeDtypeStruct((s, C3_W), BF16),
                   jax.ShapeDtypeStruct((s, GATES_W), BF16), jax.ShapeDtypeStruct((3, CONV_W), F32)],
        scratch_shapes=[pltpu.VMEM((8, CONV_W), F32)],
        compiler_params=_params("arbitrary"),
    )(dh1, gates, a, cv, c3, c3, conv_w, w_ab, w_cb, w_out)


def _attn_bwd(qkv, sinks, o, do, comm=None):
    s = qkv.shape[0]
    nb = s // BLOCK

    def body(sk_ref, q_ref, kp_ref, kc_ref, vp_ref, vc_ref, o_ref, do_ref,
             dq_ref, dk_ref, dv_ref, dsk_ref, ck_ref, cvv_ref):
        i = pl.program_id(0)

        @pl.when(i == 0)
        def _():
            ck_ref[...] = jnp.zeros_like(ck_ref)
            cvv_ref[...] = jnp.zeros_like(cvv_ref)
            dsk_ref[...] = jnp.zeros_like(dsk_ref)

        @pl.when(i < nb)
        def _():
            mask = _attn_mask(i == 0)
            q, kp, kc, vp, vc = q_ref[...], kp_ref[...], kc_ref[...], vp_ref[...], vc_ref[...]
            ov, dov = o_ref[...], do_ref[...]
            dqs, dks, dvs = [], [], []
            for h in range(N_KV_HEADS):
                hs = slice(h * HEAD_DIM, (h + 1) * HEAD_DIM)
                k2 = jnp.concatenate([kp[:, hs], kc[:, hs]], axis=0)
                v2 = jnp.concatenate([vp[:, hs], vc[:, hs]], axis=0)
                qg, og, dog = _stack_heads(q, h), _stack_heads(ov, h), _stack_heads(dov, h)
                sc = lax.dot_general(qg, k2, NT, preferred_element_type=F32) * ATTN_SCALE
                sc = jnp.where(mask, sc, NEG)
                sink = _sink_column(sk_ref, h)
                m = jnp.maximum(jnp.max(sc, axis=1, keepdims=True), sink)
                p = jnp.exp(sc - m)
                psink = jnp.exp(sink - m)
                inv = 1.0 / (jnp.sum(p, axis=1, keepdims=True) + psink)
                p = p * inv
                delta = jnp.sum(dog.astype(F32) * og.astype(F32), axis=1, keepdims=True)
                dp = lax.dot_general(dog, v2, NT, preferred_element_type=F32)
                ds = (p * (dp - delta)).astype(BF16)
                dqs.append(jnp.dot(ds, k2, preferred_element_type=F32) * ATTN_SCALE)
                dks.append(lax.dot_general(ds, qg, TN, preferred_element_type=F32) * ATTN_SCALE)
                dvs.append(lax.dot_general(p.astype(BF16), dog, TN, preferred_element_type=F32))
                dsink = -(psink * inv * delta)
                for g in range(GROUP):
                    r = h * GROUP + g
                    dsk_ref[r:r + 1, :] += jnp.sum(dsink[g * BLOCK:(g + 1) * BLOCK])
            dq_ref[...] = _unstack_heads(dqs).astype(BF16)
            dk2 = jnp.concatenate(dks, axis=1)
            dv2 = jnp.concatenate(dvs, axis=1)
            dk_ref[...] = (ck_ref[...] + dk2[:BLOCK]).astype(BF16)
            dv_ref[...] = (cvv_ref[...] + dv2[:BLOCK]).astype(BF16)
            ck_ref[...] = dk2[BLOCK:]
            cvv_ref[...] = dv2[BLOCK:]

        @pl.when(i == nb)
        def _():
            dk_ref[...] = ck_ref[...].astype(BF16)
            dv_ref[...] = cvv_ref[...].astype(BF16)

    cur = lambda i: jnp.minimum(i, nb - 1)
    done = lambda i: jnp.maximum(i - 1, 0)
    return _call(
        comm, body, name="attn_bwd", grid=(nb + 1,),
        in_specs=[pl.BlockSpec(memory_space=pltpu.SMEM), *_attn_specs(nb),
                  pl.BlockSpec((BLOCK, ATTN_W), lambda i: (cur(i), 0)),
                  pl.BlockSpec((BLOCK, ATTN_W), lambda i: (cur(i), 0))],
        out_specs=[pl.BlockSpec((BLOCK, ATTN_W), lambda i: (cur(i), 0)),
                   pl.BlockSpec((BLOCK, KV_W), lambda i: (done(i), 0)),
                   pl.BlockSpec((BLOCK, KV_W), lambda i: (done(i), 0)),
                   pl.BlockSpec((N_HEADS, 128), lambda i: (0, 0))],
        out_shape=[jax.ShapeDtypeStruct((s, ATTN_W), BF16), jax.ShapeDtypeStruct((s, KV_W), BF16),
                   jax.ShapeDtypeStruct((s, KV_W), BF16), jax.ShapeDtypeStruct((N_HEADS, 128), F32)],
        scratch_shapes=[pltpu.VMEM((BLOCK, KV_W), F32), pltpu.VMEM((BLOCK, KV_W), F32)],
        compiler_params=_params("arbitrary"),
    )(sinks, qkv, qkv, qkv, qkv, qkv, o, do)


def _inproj_bwd(dq, dk, dv, dc3, dgt, w_in, x, dh1, g1, tm, comm=None):
    s = x.shape[0]

    def body(dq_ref, dk_ref, dv_ref, dc3_ref, dgt_ref, w_ref, x_ref, dh1_ref, g_ref,
             dx_ref, dp_ref, gb_ref, gg_ref):
        @pl.when(pl.program_id(0) == 0)
        def _():
            gb_ref[...] = jnp.zeros_like(gb_ref)
            gg_ref[...] = jnp.zeros_like(gg_ref)

        dp = jnp.concatenate([dq_ref[...], dk_ref[...], dv_ref[...], dc3_ref[...], dgt_ref[...]], axis=1)
        dp_ref[...] = dp
        gb_ref[...] += jnp.sum(dp.astype(F32), axis=0, keepdims=True)
        dxn = jnp.dot(dp, w_ref[...], preferred_element_type=F32)
        xf = x_ref[...]
        dx, dg = _rms_bwd(dxn, xf, _rstd(xf), g_ref[...])
        dx_ref[...] = dh1_ref[...] + dx
        gg_ref[...] += jnp.sum(dg, axis=0, keepdims=True)

    row = lambda w: pl.BlockSpec((tm, w), lambda i: (i, 0))
    acc = lambda w: pl.BlockSpec((1, w), lambda i: (0, 0))
    return _call(
        comm, body, name="inproj_bwd", grid=(s // tm,),
        in_specs=[row(ATTN_W), row(KV_W), row(KV_W), row(C3_W), row(GATES_W), _resident((IN_W, D_MODEL)),
                  row(D_MODEL), row(D_MODEL), _resident((1, D_MODEL))],
        out_specs=[row(D_MODEL), row(IN_W), acc(IN_W), acc(D_MODEL)],
        out_shape=[jax.ShapeDtypeStruct((s, D_MODEL), F32), jax.ShapeDtypeStruct((s, IN_W), BF16),
                   jax.ShapeDtypeStruct((1, IN_W), F32), jax.ShapeDtypeStruct((1, D_MODEL), F32)],
        compiler_params=_params("arbitrary"),
    )(dq, dk, dv, dc3, dgt, w_in, x, dh1, g1)


def _wgrad(a, b, bm, bn, bk, name, comm=None):
    s, m = a.shape
    n = b.shape[1]
    nk = s // bk

    def body(a_ref, b_ref, o_ref, acc_ref):
        k = pl.program_id(2)

        @pl.when(k == 0)
        def _():
            acc_ref[...] = jnp.zeros_like(acc_ref)

        acc_ref[...] += lax.dot_general(a_ref[...].astype(BF16), b_ref[...].astype(BF16), TN,
                                        preferred_element_type=F32)

        @pl.when(k == nk - 1)
        def _():
            o_ref[...] = acc_ref[...].astype(BF16)

    return _call(
        comm, body, name=name, grid=(m // bm, n // bn, nk),
        in_specs=[pl.BlockSpec((bk, bm), lambda i, j, k: (k, i)), pl.BlockSpec((bk, bn), lambda i, j, k: (k, j))],
        out_specs=pl.BlockSpec((bm, bn), lambda i, j, k: (i, j)),
        out_shape=jax.ShapeDtypeStruct((m, n), BF16),
        scratch_shapes=[pltpu.VMEM((bm, bn), F32)],
        compiler_params=_params("parallel", "parallel", "arbitrary"),
    )(a, b)


def _wgrad_in(xn, dproj, bk, comm=None):
    s = xn.shape[0]
    nk = s // bk

    def body(a_ref, b_ref, o_ref, acc_ref):
        k = pl.program_id(0)

        @pl.when(k == 0)
        def _():
            acc_ref[...] = jnp.zeros_like(acc_ref)

        acc_ref[...] += lax.dot_general(b_ref[...], a_ref[...], TN, preferred_element_type=F32)

        @pl.when(k == nk - 1)
        def _():
            o_ref[...] = acc_ref[...].astype(BF16)

    return _call(
        comm, body, name="wgrad_in", grid=(nk,),
        in_specs=[pl.BlockSpec((bk, D_MODEL), lambda k: (k, 0)), pl.BlockSpec((bk, IN_W), lambda k: (k, 0))],
        out_specs=_resident((IN_W, D_MODEL)),
        out_shape=jax.ShapeDtypeStruct((IN_W, D_MODEL), BF16),
        scratch_shapes=[pltpu.VMEM((IN_W, D_MODEL), F32)],
        compiler_params=_params("arbitrary"),
    )(xn, dproj)


class _Carry:
    def __init__(self, jobs, reads=None, bufs=None, fresh=None):
        self.jobs, self.reads, self.bufs, self.fresh = jobs, reads or {}, bufs or {}, fresh or {}
        self.out = {}


class _Job:
    def __init__(self, n_sems, plan):
        self.n_sems, self.plan = n_sems, plan


def _plan_all(jobs, hbm, send, recv):
    pos = _position()
    starts, waits, base = [], [], 0
    for job in jobs:
        s, w = job.plan(hbm, pos, send, recv, base)
        starts, waits, base = starts + s, waits + w, base + job.n_sems
    return starts, waits


def _call(comm, body, **kw):
    if comm is None:
        return pl.pallas_call(body, **kw)
    grid = kw["grid"]
    single = not isinstance(kw["out_shape"], (list, tuple))
    out_shape = [kw["out_shape"]] if single else list(kw["out_shape"])
    out_specs = [kw["out_specs"]] if single else list(kw["out_specs"])
    in_specs = list(kw["in_specs"])
    scratch = list(kw.get("scratch_shapes", ()))
    r_names, b_names, f_names = list(comm.reads), list(comm.bufs), list(comm.fresh)
    n_args, n_out, n_scr = len(in_specs), len(out_shape), len(scratch)
    n_sems = sum(j.n_sems for j in comm.jobs)

    def wrapped(*refs):
        k = n_args
        hbm = dict(zip(r_names, refs[k:k + len(r_names)]))
        k += len(r_names) + len(b_names)
        outs = refs[k:k + n_out]
        k += n_out
        hbm.update(zip(b_names + f_names, refs[k:k + len(b_names) + len(f_names)]))
        k += len(b_names) + len(f_names)
        send, recv = refs[k + n_scr:]
        starts, waits = _plan_all(comm.jobs, hbm, send, recv)
        ids = [pl.program_id(a) for a in range(len(grid))]
        first = functools.reduce(jnp.logical_and, [i == 0 for i in ids])
        last = functools.reduce(jnp.logical_and, [i == g - 1 for i, g in zip(ids, grid)])

        @pl.when(first)
        def _():
            for cp in starts:
                cp.start()

        body(*refs[:n_args], *outs, *refs[k:k + n_scr])

        @pl.when(last)
        def _():
            for cp in waits:
                cp.wait_recv()
            for cp in starts:
                cp.wait_send()

    sems = pltpu.SemaphoreType.DMA((n_sems,))
    held = [jax.ShapeDtypeStruct(a.shape, a.dtype) for a in comm.bufs.values()] + list(comm.fresh.values())
    call = pl.pallas_call(
        wrapped, name=kw["name"], grid=grid,
        in_specs=in_specs + [_ANY] * (len(r_names) + len(b_names)),
        out_specs=out_specs + [_ANY] * len(held),
        out_shape=out_shape + held,
        input_output_aliases={n_args + len(r_names) + i: n_out + i for i in range(len(b_names))},
        scratch_shapes=scratch + [sems, sems],
        compiler_params=_params(*["arbitrary"] * len(grid)),
    )

    def run(*args):
        res = call(*args, *comm.reads.values(), *comm.bufs.values())
        comm.out = dict(zip(b_names + f_names, res[n_out:]))
        return res[0] if single else res[:n_out]

    return run


def _exchange(name, phases, reads=None, bufs=None, fresh=None):
    comm = _Carry([j for ph in phases for j in ph], reads, bufs, fresh)
    r_names, b_names, f_names = list(comm.reads), list(comm.bufs), list(comm.fresh)
    n_sems = sum(j.n_sems for j in comm.jobs)

    def body(*refs):
        hbm = dict(zip(r_names, refs[:len(r_names)]))
        k = len(r_names) + len(b_names)
        hbm.update(zip(b_names + f_names, refs[k:k + len(b_names) + len(f_names)]))
        send, recv = refs[-2:]
        pos = _position()
        started, base = [], 0
        for ph in phases:
            waits = []
            for job in ph:
                s, w = job.plan(hbm, pos, send, recv, base)
                base += job.n_sems
                for cp in s:
                    cp.start()
                started, waits = started + s, waits + w
            for cp in waits:
                cp.wait_recv()
        for cp in started:
            cp.wait_send()

    sems = pltpu.SemaphoreType.DMA((n_sems,))
    held = [jax.ShapeDtypeStruct(a.shape, a.dtype) for a in comm.bufs.values()] + list(comm.fresh.values())
    res = pl.pallas_call(
        body, name=name, in_specs=[_ANY] * (len(r_names) + len(b_names)), out_specs=[_ANY] * len(held),
        out_shape=held, input_output_aliases={len(r_names) + i: i for i in range(len(b_names))},
        scratch_shapes=[sems, sems],
    )(*comm.reads.values(), *comm.bufs.values())
    return dict(zip(b_names + f_names, res))


_HBM = pl.BlockSpec(memory_space=pltpu.HBM)
_SEM = pl.BlockSpec(memory_space=pltpu.SEMAPHORE)
_EFFECT = pltpu.SideEffectType.DATAFLOW_SIDE_EFFECTING


def _start_exchanges(name, groups):
    names = [list(arrays) for _, arrays in groups]
    first = [sum(len(ns) for ns in names[:g]) for g in range(len(groups))]
    n, ng = sum(len(ns) for ns in names), len(groups)

    def body(*refs):
        for g, (jobs, _) in enumerate(groups):
            hbm = dict(zip(names[g], refs[first[g]:first[g] + len(names[g])]))
            for cp in _plan_all(jobs, hbm, refs[n + 2 * g], refs[n + 2 * g + 1])[0]:
                cp.start()
        refs[-1][...] = jnp.zeros_like(refs[-1])

    given = [pltpu.with_memory_space_constraint(
        a if isinstance(a, jax.Array) else lax.empty(a.shape, a.dtype), pltpu.HBM)
        for _, arrays in groups for a in arrays.values()]
    sems = [pltpu.SemaphoreType.DMA((sum(j.n_sems for j in jobs),)) for jobs, _ in groups for _ in range(2)]
    res = pl.pallas_call(
        body, name=name,
        out_shape=(*sems, *[pltpu.HBM(a.shape, a.dtype) for a in given], jax.ShapeDtypeStruct((8, 128), F32)),
        in_specs=[_HBM] * n, out_specs=(*[_SEM] * (2 * ng), *[_HBM] * n, pl.BlockSpec(memory_space=pltpu.VMEM)),
        input_output_aliases={i: 2 * ng + i for i in range(n)},
        compiler_params=pltpu.CompilerParams(has_side_effects=_EFFECT),
    )(*given)
    held = res[2 * ng:2 * ng + n]
    states = [(names[g], groups[g][0], res[2 * g], res[2 * g + 1], held[first[g]:first[g] + len(names[g])])
              for g in range(ng)]
    return states, res[-1]


def _start_exchange(name, jobs, arrays):
    states, token = _start_exchanges(name, [(jobs, arrays)])
    return states[0], token


def _finish_exchange(name, state, after):
    names, jobs, send_sem, recv_sem, held = state
    n = len(names)

    def body(*refs):
        hbm = dict(zip(names, refs[:n]))
        send, recv = refs[n:n + 2]
        starts, waits = _plan_all(jobs, hbm, send, recv)
        for cp in waits:
            cp.wait_recv()
        for cp in starts:
            cp.wait_send()

    res = pl.pallas_call(
        body, name=name, out_shape=tuple(pltpu.HBM(a.shape, a.dtype) for a in held),
        in_specs=[_HBM] * n + [_SEM, _SEM, _ANY], out_specs=tuple([_HBM] * n),
        input_output_aliases={i: i for i in range(n)},
        compiler_params=pltpu.CompilerParams(has_side_effects=_EFFECT),
    )(*held, send_sem, recv_sem, after)
    return dict(zip(names, res))


def _row_tile(rows, bytes_per_row):
    best = 16
    for t in range(16, rows + 1, 16):
        if rows % t == 0 and t * bytes_per_row <= 6 * 1024 * 1024:
            best = t
    return best


def _rowwise(fn, ins, out_dtypes, name):
    rows, cols = ins[0].shape[-2:]
    per_row = sum(a.size // rows * a.dtype.itemsize for a in ins) + sum(cols * jnp.dtype(d).itemsize for d in out_dtypes)
    tr = _row_tile(rows, per_row)
    n_in = len(ins)

    def body(*refs):
        outs = fn(*[r[...] for r in refs[:n_in]])
        for o_ref, o in zip(refs[n_in:], outs):
            o_ref[...] = o.astype(o_ref.dtype)

    def spec(a):
        if a.ndim == 3:
            return pl.BlockSpec((a.shape[0], tr, cols), lambda i: (0, i, 0))
        return pl.BlockSpec((tr, cols), lambda i: (i, 0))

    return pl.pallas_call(
        body, name=name, grid=(rows // tr,),
        in_specs=[spec(a) for a in ins],
        out_specs=[pl.BlockSpec((tr, cols), lambda i: (i, 0)) for _ in out_dtypes],
        out_shape=[jax.ShapeDtypeStruct((rows, cols), d) for d in out_dtypes],
        compiler_params=_params("parallel"),
    )(*ins)


def _tiled(fn, name, grid, pos, ins, outs):
    n_in = len(ins)

    def body(pos_ref, *refs):
        res = fn(*[r[...] for r in refs[:n_in]])
        for o_ref, o in zip(refs[n_in:], res):
            o_ref[...] = o.astype(o_ref.dtype)

    return pl.pallas_call(
        body, name=name,
        grid_spec=pltpu.PrefetchScalarGridSpec(
            num_scalar_prefetch=1, grid=grid,
            in_specs=[pl.BlockSpec(bs, im) for _, bs, im in ins],
            out_specs=[pl.BlockSpec(bs, im) for _, _, bs, im in outs]),
        out_shape=[jax.ShapeDtypeStruct(s, d) for s, d, _, _ in outs],
        compiler_params=_params("parallel"),
    )(pos, *[a for a, _, _ in ins])


def _adamw(w, g, m, v):
    m = ADAM_B1 * m + (1.0 - ADAM_B1) * g
    v = ADAM_B2 * v + (1.0 - ADAM_B2) * (g * g)
    m_hat = m / (1.0 - ADAM_B1 ** ADAM_STEP)
    v_hat = v / (1.0 - ADAM_B2 ** ADAM_STEP)
    return -ADAM_LR * (m_hat / (jnp.sqrt(v_hat) + ADAM_EPS) + ADAM_WD * w), m, v


def _adamw_small(params):
    n = len(params)

    def body(*refs):
        for k in range(n):
            w, g, m, v = (r[...] for r in refs[4 * k:4 * k + 4])
            for o_ref, o in zip(refs[4 * n + 3 * k:4 * n + 3 * k + 3], _adamw(w, g, m, v)):
                o_ref[...] = o

    flat = [a for p in params for a in p]
    return pl.pallas_call(
        body, name="adamw_small",
        out_shape=[jax.ShapeDtypeStruct(p[0].shape, F32) for p in params for _ in range(3)],
    )(*flat)


class _Layout:
    def __init__(self, rows, cols, stacked):
        self.rows, self.cols, self.stacked = rows, cols, stacked

    def whole(self, rows=None):
        r = self.rows if rows is None else rows
        return (N_CHIPS, r, self.cols) if self.stacked else (r, N_CHIPS * self.cols)

    def part_rows(self, h, q=0, nq=1):
        n = self.rows // 2 // nq
        return pl.ds(pl.multiple_of(h * (self.rows // 2) + q * n, 16), n)

    def half_rows(self, h):
        return self.part_rows(h)

    def block(self, ref, p, rows=slice(None)):
        if self.stacked:
            return ref.at[p, rows, :]
        return ref.at[rows, pl.ds(pl.multiple_of(p * self.cols, 128), self.cols)]

    def all_chips(self, ref, rows):
        return ref.at[:, rows, :] if self.stacked else ref.at[rows, :]


BIG = (
    _Layout(IN_SHARD, D_MODEL, True),
    _Layout(ATTN_W, D_MODEL // N_CHIPS, False),
    _Layout(CONV_W, D_MODEL // N_CHIPS, False),
    _Layout(D_MODEL // N_CHIPS, D_MODEL, True),
    _Layout(D_MODEL, FF2 // N_CHIPS, False),
    _Layout(D_FF // N_CHIPS, D_MODEL, True),
)
N_BIG = len(BIG)
_ANY = pl.BlockSpec(memory_space=pl.ANY)


def _position():
    x, y, c = lax.axis_index("x"), lax.axis_index("y"), lax.axis_index("c")
    return x, y, c, 2 * x + y


def _core_of_chip(p, c):
    return (p >> 1, p & 1, c)


def _place_cast(shard, lay, pos, name, after=None):
    rows, cols = shard.shape
    tr = _row_tile(rows, cols * 6)
    if lay.stacked:
        out = (lay.whole(), BF16, (None, tr, cols), lambda i, pos: (pos[0], i, 0))
    else:
        out = (lay.whole(), BF16, (tr, cols), lambda i, pos: (i, pos[0]))
    ins = [(shard, (tr, cols), lambda i, pos: (i, 0))]
    if after is not None:
        ins.append((after, (8, 128), lambda i, pos: (0, 0)))
    return _tiled(lambda a, *_: (a,), name, (rows // tr,), pos, ins, [out])[0]


def _remote(src, dst, send, recv, k, device):
    return pltpu.make_async_remote_copy(src_ref=src, dst_ref=dst, send_sem=send.at[k], recv_sem=recv.at[k],
                                        device_id=device, device_id_type=MESH)


def _arrival(dst, send, recv, k, me):
    return _remote(dst, dst, send, recv, k, me)


def _gather_ici(lay, name, q=0, nq=1):
    def plan(hbm, pos, send, recv, base):
        x, y, c, me = pos
        rows = lay.part_rows(c, q, nq)
        mine = lay.block(hbm[name], me, rows)
        starts = [_remote(mine, mine, send, recv, base + d - 1, _core_of_chip(me ^ d, c)) for d in (1, 2, 3)]
        waits = [_arrival(lay.block(hbm[name], me ^ d, rows), send, recv, base + d - 1, (x, y, c)) for d in (1, 2, 3)]
        return starts, waits
    return _Job(3, plan)


def _gather_d2d(lay, name, q=0, nq=1):
    def plan(hbm, pos, send, recv, base):
        x, y, c, me = pos
        starts, waits = [], []
        for d in (1, 2, 3):
            got = lay.block(hbm[name], me ^ d, lay.part_rows(c, q, nq))
            starts.append(_remote(got, got, send, recv, base + d - 1, (x, y, 1 - c)))
            waits.append(_arrival(lay.block(hbm[name], me ^ d, lay.part_rows(1 - c, q, nq)), send, recv, base + d - 1,
                                  (x, y, c)))
        return starts, waits
    return _Job(3, plan)


def _rs_pair(lay, grad, theirs):
    def plan(hbm, pos, send, recv, base):
        x, y, c, _ = pos
        out = _remote(lay.all_chips(hbm[grad], lay.half_rows(1 - c)), hbm[theirs], send, recv, base, (x, y, 1 - c))
        return [out], [_arrival(hbm[theirs], send, recv, base, (x, y, c))]
    return _Job(1, plan)


def _rs_chips(lay, sums, slots):
    def plan(hbm, pos, send, recv, base):
        x, y, c, me = pos
        starts = [_remote(lay.block(hbm[sums], me ^ d), hbm[slots].at[me], send, recv, base + d - 1,
                          _core_of_chip(me ^ d, c)) for d in (1, 2, 3)]
        waits = [_arrival(hbm[slots].at[me ^ d], send, recv, base + d - 1, (x, y, c)) for d in (1, 2, 3)]
        return starts, waits
    return _Job(3, plan)


def _rs_share(lay, shard):
    def plan(hbm, pos, send, recv, base):
        x, y, c, _ = pos
        mine = hbm[shard].at[lay.half_rows(c), :]
        other = hbm[shard].at[lay.half_rows(1 - c), :]
        return [_remote(mine, mine, send, recv, base, (x, y, 1 - c))], [_arrival(other, send, recv, base, (x, y, c))]
    return _Job(1, plan)


def _slots_shape(lay):
    return jax.ShapeDtypeStruct((N_CHIPS, lay.rows // 2, lay.cols), BF16)


def _theirs_shape(lay):
    return jax.ShapeDtypeStruct(lay.whole(lay.rows // 2), BF16)


def _pair_sum(grad, theirs, lay, pos, name):
    half = lay.rows // 2
    add = lambda a, b: (a.astype(F32) + b.astype(F32),)
    if lay.stacked:
        tr = _row_tile(half, lay.cols * 6)
        nt = half // tr
        flat = lambda a: a.reshape(-1, lay.cols)
        mine = lambda t, pos: ((t // nt) * (2 * nt) + pos[1] * nt + t % nt, 0)
        grid, blk = (N_CHIPS * nt,), (tr, lay.cols)
        grad, theirs = flat(grad), flat(theirs)
    else:
        tr = _row_tile(half, N_CHIPS * lay.cols * 6)
        nt = half // tr
        mine = lambda t, pos: (pos[1] * nt + t, 0)
        grid, blk = (nt,), (tr, N_CHIPS * lay.cols)
    same = lambda t, pos: (t, 0)
    out = _tiled(add, name, grid, pos, [(grad, blk, mine), (theirs, blk, same)], [(theirs.shape, BF16, blk, same)])[0]
    return out.reshape(lay.whole(half))


def _chip_sum(sums, slots, lay, pos, name, after=None):
    half = lay.rows // 2
    tr = _row_tile(half, lay.cols * 12)
    nt = half // tr
    blk3 = (None, tr, lay.cols)
    if lay.stacked:
        own = (sums, blk3, lambda i, pos: (pos[0], i, 0))
    else:
        own = (sums, (tr, lay.cols), lambda i, pos: (i, pos[0]))
    others = [(slots, blk3, functools.partial(lambda d, i, pos: (pos[0] ^ d, i, 0), d)) for d in (1, 2, 3)]

    def add(a, b1, b2, b3, *_):
        return (((a.astype(F32) + b1.astype(F32)) + b2.astype(F32)) + b3.astype(F32),)

    if after is not None:
        others.append((after, (8, 128), lambda i, pos: (0, 0)))
    return _tiled(add, name, (nt,), pos, [own] + others,
                  [((lay.rows, lay.cols), F32, (tr, lay.cols), lambda i, pos: (pos[1] * nt + i, 0))])[0]


N_DEV = 8


def _exchange_small(v, reduce):
    rows = v.shape[0]

    def body(v_ref, o_ref, *scratch):
        if reduce:
            slots, send, recv = scratch
        else:
            slots, (send, recv) = o_ref, scratch
        x, y, c = lax.axis_index("x"), lax.axis_index("y"), lax.axis_index("c")
        idx = 4 * x + 2 * y + c
        slots[idx] = v_ref[...]

        def to_peer(k):
            return pltpu.make_async_remote_copy(
                src_ref=v_ref, dst_ref=slots.at[idx], send_sem=send.at[k - 1], recv_sem=recv.at[k - 1],
                device_id=(x ^ (k >> 2), y ^ ((k >> 1) & 1), c ^ (k & 1)), device_id_type=MESH)

        def from_peer(k):
            return pltpu.make_async_remote_copy(
                src_ref=v_ref, dst_ref=slots.at[idx ^ k], send_sem=send.at[k - 1], recv_sem=recv.at[k - 1],
                device_id=(x, y, c), device_id_type=MESH)

        for k in range(1, N_DEV):
            to_peer(k).start()
        for k in range(1, N_DEV):
            from_peer(k).wait_recv()
        for k in range(1, N_DEV):
            to_peer(k).wait_send()
        if reduce:
            acc = slots[0]
            for q in range(1, N_DEV):
                acc = acc + slots[q]
            o_ref[...] = acc

    sems = pltpu.SemaphoreType.DMA((N_DEV - 1,))
    stacked = jax.ShapeDtypeStruct((N_DEV, rows, 128), F32)
    return pl.pallas_call(
        body, name="allreduce_small" if reduce else "allgather_small",
        out_shape=jax.ShapeDtypeStruct((rows, 128), F32) if reduce else stacked,
        scratch_shapes=([pltpu.VMEM((N_DEV, rows, 128), F32)] if reduce else []) + [sems, sems],
    )(v)


def _pack_rows(parts):
    padded = [jnp.pad(a, ((0, -a.shape[0] % 8), (0, 0))) for a in parts]
    starts = [sum(p.shape[0] for p in padded[:k]) for k in range(len(padded))]
    return jnp.concatenate(padded, axis=0), starts


def kernel(x, mix_norm, w_in, b_in, sinks, conv_w, w_attn_branch, w_conv_branch, w_out, ffn_norm, w_up, ffn_conv_w, w_down, final_norm, loss_target, m_mix_norm, m_w_in, m_b_in, m_sinks, m_conv_w, m_w_attn_branch, m_w_conv_branch, m_w_out, m_ffn_norm, m_w_up, m_ffn_conv_w, m_w_down, m_final_norm, v_mix_norm, v_w_in, v_b_in, v_sinks, v_conv_w, v_w_attn_branch, v_w_conv_branch, v_w_out, v_ffn_norm, v_w_up, v_ffn_conv_w, v_w_down, v_final_norm):
    me = 2 * lax.axis_index("x") + lax.axis_index("y")
    big_w = [w_in[0].T, w_attn_branch[0], w_conv_branch[0], w_out[0], w_up[0], w_down[0]]
    big_m = [m_w_in[0].T, m_w_attn_branch[0], m_w_conv_branch[0], m_w_out[0], m_w_up[0], m_w_down[0]]
    big_v = [v_w_in[0].T, v_w_attn_branch[0], v_w_conv_branch[0], v_w_out[0], v_w_up[0], v_w_down[0]]
    names = ("w_in", "w_ab", "w_cb", "w_out", "w_up", "w_down")

    pos = jnp.stack([me, lax.axis_index("c")]).astype(jnp.int32)

    lay = dict(zip(names, BIG))
    xs, target, sk = x[0], loss_target[0], sinks[0]
    s = xs.shape[0]
    tm, tm2, bk = min(256, s), min(512, s), min(1024, s)

    placed = {"w_in": _place_cast(big_w[0], lay["w_in"], pos, "cast_w_in")}
    fly_in, started = _start_exchange("gather_in_start", [_gather_ici(lay["w_in"], "w_in")], {"w_in": placed["w_in"]})
    for w, n in zip(big_w[1:], names[1:]):
        placed[n] = _place_cast(w, lay[n], pos, "cast_" + n, after=started)
    trio = ("w_ab", "w_cb", "w_out")
    (fly_trio, fly_down, fly_up), started = _start_exchanges("gather_rest_start", [
        ([_gather_ici(lay[n], n) for n in ws], {n: placed[n] for n in ws}) for ws in (trio, ("w_down",), ("w_up",))])

    taps, (_, t0) = _pack_rows([conv_w[0] + started[0:3], ffn_conv_w[0].reshape(3 * (FF2 // N_CHIPS // 128), 128)])
    taps = _exchange_small(taps, reduce=False)
    conv_full = taps[0::2, 0:3].transpose(1, 0, 2).reshape(3, CONV_W)
    ffn_cw_full = taps[0::2, t0:t0 + 33].reshape(N_CHIPS, 3, FF2 // N_CHIPS).transpose(1, 0, 2).reshape(3, FF2)

    got = _finish_exchange("gather_in_wait", fly_in, after=taps)
    w_in_full = _exchange("gather_in_d2d", [[_gather_d2d(lay["w_in"], "w_in")]], bufs=got)["w_in"].reshape(IN_W, D_MODEL)
    xn, qkv, c3, gates = _inproj_fwd(xs, mix_norm, w_in_full, b_in, tm2)
    k2 = _Carry([_gather_d2d(lay[n], n) for n in trio], bufs=_finish_exchange("gather_trio_wait", fly_trio, after=qkv))
    attn = _attn_fwd(qkv, sk, comm=k2)
    w_ab, w_cb = k2.out["w_ab"], k2.out["w_cb"]
    w_out_full = k2.out["w_out"].reshape(D_MODEL, D_MODEL)
    k3 = _Carry([_gather_d2d(lay["w_down"], "w_down")], bufs=_finish_exchange("gather_down_wait", fly_down, after=attn))
    conv, a, cv, merged, h1, hn = _mix_fwd(xs, attn, c3, gates, conv_full, w_ab, w_cb, w_out_full, ffn_norm, tm, comm=k3)
    w_down_full = k3.out["w_down"].reshape(D_FF, D_MODEL)
    w_up_full = _exchange("gather_up_d2d", [[_gather_d2d(lay["w_up"], "w_up")]],
                          bufs=_finish_exchange("gather_up_wait", fly_up, after=hn))["w_up"]
    u, up, act, dh2, loss_part, g_fn = _ffn_fwd_loss(hn, h1, w_up_full, ffn_cw_full, w_down_full,
                                                     final_norm[None, :], target, tm)

    grads, sums, slots = {}, {}, {}

    def pair(*ws):
        return _Carry([_rs_pair(lay[n], "g_" + n, "t_" + n) for n in ws], reads={"g_" + n: grads[n] for n in ws},
                      fresh={"t_" + n: _theirs_shape(lay[n]) for n in ws})

    def chips(*ws, also=None):
        k = _Carry([_rs_chips(lay[n], "s_" + n, "r_" + n) for n in ws], reads={"s_" + n: sums[n] for n in ws},
                   fresh={"r_" + n: _slots_shape(lay[n]) for n in ws})
        if also is not None:
            k = _Carry(k.jobs + also.jobs, {**k.reads, **also.reads}, None, {**k.fresh, **also.fresh})
        return k

    def pair_sums(k, *ws):
        for n in ws:
            sums[n] = _pair_sum(grads[n], k.out["t_" + n], lay[n], pos, "pair_sum_" + n)

    def take_slots(k, *ws):
        for n in ws:
            slots[n] = k.out["r_" + n]

    du, dh1, g_fcw, g_g2 = _ffn_bwd(dh2, u, up, h1, w_up_full, ffn_cw_full, w_down_full, ffn_norm, tm)
    grads["w_down"] = _wgrad(act, dh2, D_FF // 2, D_MODEL, bk, "wgrad_down").reshape(lay["w_down"].whole())
    k4 = pair("w_down")
    grads["w_up"] = _wgrad(hn, du, D_MODEL, FF2 // 4, bk, "wgrad_up", comm=k4)
    pair_sums(k4, "w_down")
    k5 = chips("w_down", also=pair("w_up"))
    dattn, da, dcv, dc3, dgt, g_cw = _mix_bwd(dh1, gates, a, cv, c3, conv_full, w_ab, w_cb, w_out_full, tm, comm=k5)
    take_slots(k5, "w_down")
    pair_sums(k5, "w_up")
    grads["w_out"] = _wgrad(merged, dh1, D_MODEL, D_MODEL, bk, "wgrad_out").reshape(lay["w_out"].whole())
    grads["w_ab"] = _wgrad(attn, da, ATTN_W, D_MODEL, bk, "wgrad_ab")
    grads["w_cb"] = _wgrad(conv, dcv, CONV_W, D_MODEL, bk, "wgrad_cb")
    k6 = chips("w_up", also=pair("w_out", "w_ab", "w_cb"))
    dq, dk, dv, g_sk = _attn_bwd(qkv, sk, attn, dattn, comm=k6)
    take_slots(k6, "w_up")
    pair_sums(k6, "w_out", "w_ab", "w_cb")
    grad_x, dproj, g_b, g_g1 = _inproj_bwd(dq, dk, dv, dc3, dgt, w_in_full, xs, dh1, mix_norm, tm2)
    k8 = chips("w_out", "w_ab", "w_cb")
    grads["w_in"] = _wgrad_in(xn, dproj, min(512, s), comm=k8).reshape(lay["w_in"].whole())
    take_slots(k8, "w_out", "w_ab", "w_cb")
    others = names[1:]
    in_flight, started = _start_exchange("rs_pair_in_start", [_rs_pair(lay["w_in"], "g", "t")],
                                         {"g": grads["w_in"], "t": _theirs_shape(lay["w_in"])})
    halves = {n: _chip_sum(sums[n], slots[n], lay[n], pos, "chip_sum_" + n, after=started) for n in others}
    landed = _finish_exchange("rs_pair_in_wait", in_flight, after=halves["w_down"])
    sums["w_in"] = _pair_sum(landed["g"], landed["t"], lay["w_in"], pos, "pair_sum_w_in")
    in_flight, started = _start_exchange("rs_chips_in_start", [_rs_chips(lay["w_in"], "s", "r")],
                                         {"s": sums["w_in"], "r": _slots_shape(lay["w_in"])})
    shared = _exchange("share_halves", [[_rs_share(lay[n], n) for n in others]], reads={"after": started}, bufs=halves)
    w_of, m_of, v_of = dict(zip(names, big_w)), dict(zip(names, big_m)), dict(zip(names, big_v))
    adam = lambda n, g: _rowwise(_adamw, [w_of[n], g, m_of[n], v_of[n]], [F32, F32, F32], "adamw_" + n)
    new_of = {n: adam(n, shared[n]) for n in others}
    small = dict(loss=loss_part, g_g1=g_g1, g_b=g_b, g_sk=g_sk, g_cw=g_cw, g_g2=g_g2, g_fcw=g_fcw, g_fn=g_fn)

    parts = [small["loss"], small["g_g1"], small["g_b"], jnp.pad(small["g_sk"][:, 0], (0, 120))[None, :],
             small["g_cw"], small["g_g2"], small["g_fcw"], small["g_fn"]]
    packed, at = _pack_rows([a.reshape(-1, 128) for a in parts])
    total = _exchange_small(packed, reduce=True)
    part = lambda k: total[at[k]:at[k] + parts[k].size // 128].reshape(parts[k].shape)
    loss = total[0, 0]
    g_mix, g_b, g_g2, g_fn = part(1), part(2), part(5), part(7)
    g_sk = part(3)[:, 0:N_HEADS]
    g_cw = lax.dynamic_slice(part(4), (0, me * 128), (3, 128))
    g_fcw = lax.dynamic_slice(part(6), (0, me * (FF2 // N_CHIPS)), (3, FF2 // N_CHIPS))
    small_p = [
        (mix_norm, g_mix, m_mix_norm, v_mix_norm), (b_in, g_b, m_b_in, v_b_in), (sinks, g_sk, m_sinks, v_sinks),
        (conv_w[0], g_cw, m_conv_w[0], v_conv_w[0]), (ffn_norm, g_g2, m_ffn_norm, v_ffn_norm),
        (ffn_conv_w[0], g_fcw, m_ffn_conv_w[0], v_ffn_conv_w[0]),
        (final_norm[None, :], g_fn, m_final_norm[None, :], v_final_norm[None, :])]
    small_new = _adamw_small(small_p)
    small_new = [small_new[3 * k:3 * k + 3] for k in range(len(small_p))]

    landed = _finish_exchange("rs_chips_in_wait", in_flight, after=small_new[0][0])
    half_in = _chip_sum(landed["s"], landed["r"], lay["w_in"], pos, "chip_sum_w_in")
    shared["w_in"] = _exchange("share_in", [[_rs_share(lay["w_in"], "w_in")]], bufs={"w_in": half_in})["w_in"]
    new_of["w_in"] = adam("w_in", shared["w_in"])
    big_g = [shared[n] for n in names]
    big_new = [new_of[n] for n in names]

    order = [("s", 0), ("b", 0), ("s", 1), ("s", 2), ("s", 3), ("b", 1), ("b", 2), ("b", 3), ("s", 4), ("b", 4),
             ("s", 5), ("b", 5), ("s", 6)]
    shapes = [mix_norm.shape, w_in.shape, b_in.shape, sinks.shape, conv_w.shape, w_attn_branch.shape,
              w_conv_branch.shape, w_out.shape, ffn_norm.shape, w_up.shape, ffn_conv_w.shape, w_down.shape,
              final_norm.shape]
    small_g = [p[1] for p in small_p]
    big_g[0] = big_g[0].T
    big_new[0] = [a.T for a in big_new[0]]
    out_g = [(small_g[k] if kind == "s" else big_g[k]).reshape(shp) for (kind, k), shp in zip(order, shapes)]
    news = [[(small_new[k][j] if kind == "s" else big_new[k][j]).reshape(shp) for (kind, k), shp in zip(order, shapes)]
            for j in range(3)]
    return (loss, grad_x[None], *out_g, *news[0], *news[1], *news[2])
```

```python
import functools

import jax
import jax.numpy as jnp
from jax import lax
from jax.experimental import pallas as pl
from jax.experimental.pallas import tpu as pltpu

F32 = jnp.float32
BF16 = jnp.bfloat16

D_MODEL = 1024
HEAD_DIM = 64
N_HEADS = 8
N_KV_HEADS = 2
GROUP = N_HEADS // N_KV_HEADS
BLOCK = 128
ATTN_SCALE = HEAD_DIM ** -0.5
ATTN_W = N_HEADS * HEAD_DIM
KV_W = N_KV_HEADS * HEAD_DIM
CONV_W = 512
QKV_W = ATTN_W + 2 * KV_W
C3_W = 3 * CONV_W
GATES_W = 2 * D_MODEL
IN_W = QKV_W + C3_W + GATES_W
D_FF = 2816
FF2 = 2 * D_FF
NORM_EPS = 1e-5
N_CHIPS = 4
IN_SHARD = IN_W // N_CHIPS
NEG = -1e30

ADAM_LR = 0.001
ADAM_B1 = 0.9
ADAM_B2 = 0.999
ADAM_EPS = 1e-08
ADAM_WD = 0.01
ADAM_STEP = 10

VMEM_LIMIT = 56 * 1024 * 1024
MESH = pl.DeviceIdType.MESH

NT = (((1,), (1,)), ((), ()))
TN = (((0,), (0,)), ((), ()))


def _params(*sem):
    return pltpu.CompilerParams(dimension_semantics=sem, vmem_limit_bytes=VMEM_LIMIT)


def _resident(shape):
    return pl.BlockSpec(shape, lambda *_: (0,) * len(shape), pipeline_mode=pl.Buffered(1))


def _sigmoid(v):
    return 0.5 * jnp.tanh(0.5 * v) + 0.5


def _rstd(v):
    return lax.rsqrt(jnp.mean(v * v, axis=-1, keepdims=True) + NORM_EPS)


def _rms_bwd(dy, v, rstd, g):
    vhat = v * rstd
    t = dy * g
    return rstd * (t - vhat * jnp.mean(t * vhat, axis=-1, keepdims=True)), dy * vhat


def _taps(z, cw):
    return cw[2:3] * z + cw[1:2] * pltpu.roll(z, 1, 0) + cw[0:1] * pltpu.roll(z, 2, 0)


def _causal_conv(z, prev, cw):
    edge = _taps(jnp.concatenate([prev, z[0:8]], axis=0), cw)
    return jnp.concatenate([edge[8:16], _taps(z, cw)[8:]], axis=0)


def _rows_after(z, nxt):
    n = z.shape[0]
    edge = jnp.concatenate([z[n - 8:n], nxt], axis=0)
    return tuple(jnp.concatenate([pltpu.roll(z, n - k, 0)[:n - 8], pltpu.roll(edge, 16 - k, 0)[0:8]], axis=0)
                 for k in (1, 2))


def _inproj_fwd(x, g1, w_in, b_in, tm, comm=None):
    s = x.shape[0]

    def body(x_ref, g_ref, w_ref, b_ref, xn_ref, qkv_ref, c3_ref, gt_ref):
        xf = x_ref[...]
        xn = (xf * _rstd(xf) * g_ref[...]).astype(BF16)
        xn_ref[...] = xn

        def seg(a, b):
            return lax.dot_general(xn, w_ref[a:b, :], NT, preferred_element_type=F32) + b_ref[:, a:b]

        qkv_ref[...] = seg(0, QKV_W).astype(BF16)
        c3_ref[...] = seg(QKV_W, QKV_W + C3_W)
        gt_ref[...] = seg(QKV_W + C3_W, IN_W)

    row = lambda w: pl.BlockSpec((tm, w), lambda i: (i, 0))
    return _call(
        comm, body, name="inproj_fwd", grid=(s // tm,),
        in_specs=[row(D_MODEL), _resident((1, D_MODEL)), _resident((IN_W, D_MODEL)), _resident((1, IN_W))],
        out_specs=[row(D_MODEL), row(QKV_W), row(C3_W), row(GATES_W)],
        out_shape=[jax.ShapeDtypeStruct((s, D_MODEL), BF16), jax.ShapeDtypeStruct((s, QKV_W), BF16),
                   jax.ShapeDtypeStruct((s, C3_W), F32), jax.ShapeDtypeStruct((s, GATES_W), F32)],
        compiler_params=_params("parallel"),
    )(x, g1, w_in, b_in)


def _attn_mask(first_block):
    qi = lax.broadcasted_iota(jnp.int32, (GROUP * BLOCK, 2 * BLOCK), 0) & (BLOCK - 1)
    kj = lax.broadcasted_iota(jnp.int32, (GROUP * BLOCK, 2 * BLOCK), 1)
    band = (kj > qi) & (kj <= qi + BLOCK)
    return band & ((kj >= BLOCK) | jnp.logical_not(first_block))


def _sink_column(sk_ref, h):
    rows = lax.broadcasted_iota(jnp.int32, (GROUP * BLOCK, 1), 0)
    col = jnp.full((GROUP * BLOCK, 1), sk_ref[h * GROUP], F32)
    for g in range(1, GROUP):
        col = jnp.where(rows >= g * BLOCK, sk_ref[h * GROUP + g], col)
    return col


def _stack_heads(t, h):
    return jnp.concatenate(
        [t[:, (h * GROUP + g) * HEAD_DIM:(h * GROUP + g + 1) * HEAD_DIM] for g in range(GROUP)], axis=0)


def _unstack_heads(per_kv):
    return jnp.concatenate(
        [t[g * BLOCK:(g + 1) * BLOCK] for t in per_kv for g in range(GROUP)], axis=1)


def _attn_specs(nb):
    cur = lambda i: jnp.minimum(i, nb - 1)
    prev = lambda i: jnp.maximum(jnp.minimum(i, nb - 1) - 1, 0)
    q = pl.BlockSpec((BLOCK, ATTN_W), lambda i: (cur(i), 0))
    kp = pl.BlockSpec((BLOCK, KV_W), lambda i: (prev(i), ATTN_W // KV_W))
    kc = pl.BlockSpec((BLOCK, KV_W), lambda i: (cur(i), ATTN_W // KV_W))
    vp = pl.BlockSpec((BLOCK, KV_W), lambda i: (prev(i), ATTN_W // KV_W + 1))
    vc = pl.BlockSpec((BLOCK, KV_W), lambda i: (cur(i), ATTN_W // KV_W + 1))
    return q, kp, kc, vp, vc


def _attn_fwd(qkv, sinks, comm=None):
    s = qkv.shape[0]
    nb = s // BLOCK

    def body(sk_ref, q_ref, kp_ref, kc_ref, vp_ref, vc_ref, o_ref):
        mask = _attn_mask(pl.program_id(0) == 0)
        q, kp, kc, vp, vc = q_ref[...], kp_ref[...], kc_ref[...], vp_ref[...], vc_ref[...]
        outs = []
        for h in range(N_KV_HEADS):
            hs = slice(h * HEAD_DIM, (h + 1) * HEAD_DIM)
            k2 = jnp.concatenate([kp[:, hs], kc[:, hs]], axis=0)
            v2 = jnp.concatenate([vp[:, hs], vc[:, hs]], axis=0)
            sc = lax.dot_general(_stack_heads(q, h), k2, NT, preferred_element_type=F32) * ATTN_SCALE
            sc = jnp.where(mask, sc, NEG)
            sink = _sink_column(sk_ref, h)
            m = jnp.maximum(jnp.max(sc, axis=1, keepdims=True), sink)
            p = jnp.exp(sc - m)
            den = jnp.sum(p, axis=1, keepdims=True) + jnp.exp(sink - m)
            outs.append(jnp.dot(p.astype(BF16), v2, preferred_element_type=F32) / den)
        o_ref[...] = _unstack_heads(outs).astype(BF16)

    return _call(
        comm, body, name="attn_fwd", grid=(nb,),
        in_specs=[pl.BlockSpec(memory_space=pltpu.SMEM), *_attn_specs(nb)],
        out_specs=pl.BlockSpec((BLOCK, ATTN_W), lambda i: (i, 0)),
        out_shape=jax.ShapeDtypeStruct((s, ATTN_W), BF16),
        compiler_params=_params("parallel"),
    )(sinks, qkv, qkv, qkv, qkv, qkv)


def _mix_fwd(x, attn, c3, gates, conv_w, w_ab, w_cb, w_out, g2, tm, comm=None):
    s = x.shape[0]

    def body(x_ref, at_ref, c3_ref, gt_ref, cw_ref, wab_ref, wcb_ref, wo_ref, g_ref,
             conv_ref, a_ref, cv_ref, mg_ref, h1_ref, hn_ref, carry_ref):
        @pl.when(pl.program_id(0) == 0)
        def _():
            carry_ref[...] = jnp.zeros_like(carry_ref)

        c3v = c3_ref[...]
        cb, cc, cx = c3v[:, :CONV_W], c3v[:, CONV_W:2 * CONV_W], c3v[:, 2 * CONV_W:]
        z = cc * cx
        cz = _causal_conv(z, carry_ref[...], cw_ref[...])
        carry_ref[...] = z[tm - 8:tm]
        conv = (cb * cz).astype(BF16)
        conv_ref[...] = conv
        a = jnp.dot(at_ref[...], wab_ref[...], preferred_element_type=F32)
        cv = jnp.dot(conv, wcb_ref[...], preferred_element_type=F32)
        a_ref[...] = a.astype(BF16)
        cv_ref[...] = cv.astype(BF16)
        gt = gt_ref[...]
        merged = (_sigmoid(gt[:, :D_MODEL]) * a + _sigmoid(gt[:, D_MODEL:]) * cv).astype(BF16)
        mg_ref[...] = merged
        h1 = x_ref[...] + jnp.dot(merged, wo_ref[...], preferred_element_type=F32)
        h1_ref[...] = h1
        hn_ref[...] = (h1 * _rstd(h1) * g_ref[...]).astype(BF16)

    row = lambda w: pl.BlockSpec((tm, w), lambda i: (i, 0))
    return _call(
        comm, body, name="mix_fwd", grid=(s // tm,),
        in_specs=[row(D_MODEL), row(ATTN_W), row(C3_W), row(GATES_W), _resident((3, CONV_W)),
                  _resident((ATTN_W, D_MODEL)), _resident((CONV_W, D_MODEL)), _resident((D_MODEL, D_MODEL)),
                  _resident((1, D_MODEL))],
        out_specs=[row(CONV_W), row(D_MODEL), row(D_MODEL), row(D_MODEL), row(D_MODEL), row(D_MODEL)],
        out_shape=[jax.ShapeDtypeStruct((s, CONV_W), BF16), jax.ShapeDtypeStruct((s, D_MODEL), BF16),
                   jax.ShapeDtypeStruct((s, D_MODEL), BF16), jax.ShapeDtypeStruct((s, D_MODEL), BF16),
                   jax.ShapeDtypeStruct((s, D_MODEL), F32), jax.ShapeDtypeStruct((s, D_MODEL), BF16)],
        scratch_shapes=[pltpu.VMEM((8, CONV_W), F32)],
        compiler_params=_params("arbitrary"),
    )(x, attn, c3, gates, conv_w, w_ab, w_cb, w_out, g2)


def _ffn_fwd_loss(hn, h1, w_up, ffn_cw, w_down, g3, target, tm):
    s = hn.shape[0]

    def body(hn_ref, h1_ref, wu_ref, cw_ref, wd_ref, g_ref, t_ref,
             u_ref, up_ref, act_ref, dh2_ref, loss_ref, gfn_ref, carry_ref):
        @pl.when(pl.program_id(0) == 0)
        def _():
            carry_ref[...] = jnp.zeros_like(carry_ref)
            loss_ref[...] = jnp.zeros_like(loss_ref)
            gfn_ref[...] = jnp.zeros_like(gfn_ref)

        u = jnp.dot(hn_ref[...], wu_ref[...], preferred_element_type=F32)
        u_ref[...] = u.astype(BF16)
        up = _causal_conv(u, carry_ref[...], cw_ref[...])
        up_ref[...] = up.astype(BF16)
        carry_ref[...] = u[tm - 8:tm]
        gate, val = up[:, :D_FF], up[:, D_FF:]
        act = (gate * _sigmoid(gate) * val).astype(BF16)
        act_ref[...] = act
        h2 = h1_ref[...] + jnp.dot(act, wd_ref[...], preferred_element_type=F32)
        rstd = _rstd(h2)
        g = g_ref[...]
        err = h2 * rstd * g - t_ref[...]
        loss_ref[...] += jnp.sum(err * err) * (0.5 / D_MODEL)
        dh2, dg = _rms_bwd(err * (1.0 / D_MODEL), h2, rstd, g)
        dh2_ref[...] = dh2
        gfn_ref[...] += jnp.sum(dg, axis=0, keepdims=True)

    row = lambda w: pl.BlockSpec((tm, w), lambda i: (i, 0))
    acc = lambda w: pl.BlockSpec((1, w), lambda i: (0, 0))
    return pl.pallas_call(
        body, name="ffn_fwd_loss", grid=(s // tm,),
        in_specs=[row(D_MODEL), row(D_MODEL), _resident((D_MODEL, FF2)), _resident((3, FF2)),
                  _resident((D_FF, D_MODEL)), _resident((1, D_MODEL)), row(D_MODEL)],
        out_specs=[row(FF2), row(FF2), row(D_FF), row(D_MODEL), acc(128), acc(D_MODEL)],
        out_shape=[jax.ShapeDtypeStruct((s, FF2), BF16), jax.ShapeDtypeStruct((s, FF2), BF16),
                   jax.ShapeDtypeStruct((s, D_FF), BF16),
                   jax.ShapeDtypeStruct((s, D_MODEL), F32), jax.ShapeDtypeStruct((1, 128), F32),
                   jax.ShapeDtypeStruct((1, D_MODEL), F32)],
        scratch_shapes=[pltpu.VMEM((8, FF2), F32)],
        compiler_params=_params("arbitrary"),
    )(hn, h1, w_up, ffn_cw, w_down, g3, target)


def _ffn_bwd(dh2, u, up, h1, w_up, ffn_cw, w_down, g2, tm):
    s = dh2.shape[0]
    nt = s // tm

    def body(dh2_ref, u_ref, up_ref, h1_ref, wu_ref, cw_ref, wd_ref, g_ref,
             du_ref, dh1_ref, gcw_ref, gg_ref, carry_ref):
        @pl.when(pl.program_id(0) == 0)
        def _():
            carry_ref[...] = jnp.zeros_like(carry_ref)
            gcw_ref[...] = jnp.zeros_like(gcw_ref)
            gg_ref[...] = jnp.zeros_like(gg_ref)

        dh2v = dh2_ref[...]
        dact = lax.dot_general(dh2v.astype(BF16), wd_ref[...], NT, preferred_element_type=F32)
        upv = up_ref[...].astype(F32)
        gate, val = upv[:, :D_FF], upv[:, D_FF:]
        sg = _sigmoid(gate)
        dval = dact * (gate * sg)
        dgate = dact * val * (sg * (1.0 + gate * (1.0 - sg)))
        dup = jnp.concatenate([dgate, dval], axis=1)
        dup1, dup2 = _rows_after(dup, carry_ref[...])
        carry_ref[...] = dup[0:8]
        u = u_ref[...].astype(F32)
        gcw_ref[2:3, :] += jnp.sum(dup * u, axis=0, keepdims=True)
        gcw_ref[1:2, :] += jnp.sum(dup1 * u, axis=0, keepdims=True)
        gcw_ref[0:1, :] += jnp.sum(dup2 * u, axis=0, keepdims=True)
        cw = cw_ref[...]
        du = (cw[2:3] * dup + cw[1:2] * dup1 + cw[0:1] * dup2).astype(BF16)
        du_ref[...] = du
        dhn = lax.dot_general(du, wu_ref[...], NT, preferred_element_type=F32)
        h1v = h1_ref[...]
        dh1, dg = _rms_bwd(dhn, h1v, _rstd(h1v), g_ref[...])
        dh1_ref[...] = dh2v + dh1
        gg_ref[...] += jnp.sum(dg, axis=0, keepdims=True)

    row = lambda w: pl.BlockSpec((tm, w), lambda i: (nt - 1 - i, 0))
    return pl.pallas_call(
        body, name="ffn_bwd", grid=(nt,),
        in_specs=[row(D_MODEL), row(FF2), row(FF2),
                  row(D_MODEL), _resident((D_MODEL, FF2)), _resident((3, FF2)), _resident((D_FF, D_MODEL)),
                  _resident((1, D_MODEL))],
        out_specs=[row(FF2), row(D_MODEL), pl.BlockSpec((3, FF2), lambda i: (0, 0)),
                   pl.BlockSpec((1, D_MODEL), lambda i: (0, 0))],
        out_shape=[jax.ShapeDtypeStruct((s, FF2), BF16), jax.ShapeDtypeStruct((s, D_MODEL), F32),
                   jax.ShapeDtypeStruct((3, FF2), F32), jax.ShapeDtypeStruct((1, D_MODEL), F32)],
        scratch_shapes=[pltpu.VMEM((8, FF2), F32)],
        compiler_params=_params("arbitrary"),
    )(dh2, u, up, h1, w_up, ffn_cw, w_down, g2)


def _mix_bwd(dh1, gates, a, cv, c3, conv_w, w_ab, w_cb, w_out, tm, comm=None):
    s = dh1.shape[0]
    nt = s // tm
    halo = 8

    def body(dh1_ref, gt_ref, a_ref, cv_ref, c3_ref, ch_ref, cw_ref, wab_ref, wcb_ref, wo_ref,
             dat_ref, da_ref, dcv_ref, dc3_ref, dgt_ref, gcw_ref, carry_ref):
        i = pl.program_id(0)

        @pl.when(i == 0)
        def _():
            carry_ref[...] = jnp.zeros_like(carry_ref)
            gcw_ref[...] = jnp.zeros_like(gcw_ref)

        dm = lax.dot_general(dh1_ref[...].astype(BF16), wo_ref[...], NT, preferred_element_type=F32)
        gt = gt_ref[...]
        sa, sc = _sigmoid(gt[:, :D_MODEL]), _sigmoid(gt[:, D_MODEL:])
        da = (dm * sa).astype(BF16)
        dcv = (dm * sc).astype(BF16)
        da_ref[...] = da
        dcv_ref[...] = dcv
        dgt_ref[...] = jnp.concatenate(
            [dm * a_ref[...].astype(F32) * (sa * (1.0 - sa)), dm * cv_ref[...].astype(F32) * (sc * (1.0 - sc))],
            axis=1).astype(BF16)
        dat_ref[...] = lax.dot_general(da, wab_ref[...], NT, preferred_element_type=F32).astype(BF16)
        dconv = lax.dot_general(dcv, wcb_ref[...], NT, preferred_element_type=F32)
        c3v = c3_ref[...]
        cb, cc, cx = c3v[:, :CONV_W], c3v[:, CONV_W:2 * CONV_W], c3v[:, 2 * CONV_W:]
        z = cc * cx
        chv = ch_ref[...] * (i < nt - 1).astype(F32)
        zh = chv[:, CONV_W:2 * CONV_W] * chv[:, 2 * CONV_W:]
        cw = cw_ref[...]
        cz = _causal_conv(z, zh, cw)
        dcz = dconv * cb
        dcz1, dcz2 = _rows_after(dcz, carry_ref[...])
        carry_ref[...] = dcz[0:8]
        gcw_ref[2:3, :] += jnp.sum(dcz * z, axis=0, keepdims=True)
        gcw_ref[1:2, :] += jnp.sum(dcz1 * z, axis=0, keepdims=True)
        gcw_ref[0:1, :] += jnp.sum(dcz2 * z, axis=0, keepdims=True)
        dz = cw[2:3] * dcz + cw[1:2] * dcz1 + cw[0:1] * dcz2
        dc3_ref[...] = jnp.concatenate([dconv * cz, dz * cx, dz * cc], axis=1).astype(BF16)

    row = lambda w: pl.BlockSpec((tm, w), lambda i: (nt - 1 - i, 0))
    return _call(
        comm, body, name="mix_bwd", grid=(nt,),
        in_specs=[row(D_MODEL), row(GATES_W), row(D_MODEL), row(D_MODEL), row(C3_W),
                  pl.BlockSpec((halo, C3_W), lambda i: (jnp.maximum((nt - 1 - i) * (tm // halo) - 1, 0), 0)),
                  _resident((3, CONV_W)), _resident((ATTN_W, D_MODEL)), _resident((CONV_W, D_MODEL)),
                  _resident((D_MODEL, D_MODEL))],
        out_specs=[row(ATTN_W), row(D_MODEL), row(D_MODEL), row(C3_W), row(GATES_W),
                   pl.BlockSpec((3, CONV_W), lambda i: (0, 0))],
        out_shape=[jax.ShapeDtypeStruct((s, ATTN_W), BF16), jax.ShapeDtypeStruct((s, D_MODEL), BF16),
                   jax.ShapeDtypeStruct((s, D_MODEL), BF16), jax.ShapeDtypeStruct((s, C3_W), BF16),
                   jax.ShapeDtypeStruct((s, GATES_W), BF16), jax.ShapeDtypeStruct((3, CONV_W), F32)],
        scratch_shapes=[pltpu.VMEM((8, CONV_W), F32)],
        compiler_params=_params("arbitrary"),
    )(dh1, gates, a, cv, c3, c3, conv_w, w_ab, w_cb, w_out)


def _attn_bwd(qkv, sinks, o, do, comm=None):
    s = qkv.shape[0]
    nb = s // BLOCK

    def body(sk_ref, q_ref, kp_ref, kc_ref, vp_ref, vc_ref, o_ref, do_ref,
             dq_ref, dk_ref, dv_ref, dsk_ref, ck_ref, cvv_ref):
        i = pl.program_id(0)

        @pl.when(i == 0)
        def _():
            ck_ref[...] = jnp.zeros_like(ck_ref)
            cvv_ref[...] = jnp.zeros_like(cvv_ref)
            dsk_ref[...] = jnp.zeros_like(dsk_ref)

        @pl.when(i < nb)
        def _():
            mask = _attn_mask(i == 0)
            q, kp, kc, vp, vc = q_ref[...], kp_ref[...], kc_ref[...], vp_ref[...], vc_ref[...]
            ov, dov = o_ref[...], do_ref[...]
            dqs, dks, dvs = [], [], []
            for h in range(N_KV_HEADS):
                hs = slice(h * HEAD_DIM, (h + 1) * HEAD_DIM)
                k2 = jnp.concatenate([kp[:, hs], kc[:, hs]], axis=0)
                v2 = jnp.concatenate([vp[:, hs], vc[:, hs]], axis=0)
                qg, og, dog = _stack_heads(q, h), _stack_heads(ov, h), _stack_heads(dov, h)
                sc = lax.dot_general(qg, k2, NT, preferred_element_type=F32) * ATTN_SCALE
                sc = jnp.where(mask, sc, NEG)
                sink = _sink_column(sk_ref, h)
                m = jnp.maximum(jnp.max(sc, axis=1, keepdims=True), sink)
                p = jnp.exp(sc - m)
                psink = jnp.exp(sink - m)
                inv = 1.0 / (jnp.sum(p, axis=1, keepdims=True) + psink)
                p = p * inv
                delta = jnp.sum(dog.astype(F32) * og.astype(F32), axis=1, keepdims=True)
                dp = lax.dot_general(dog, v2, NT, preferred_element_type=F32)
                ds = (p * (dp - delta)).astype(BF16)
                dqs.append(jnp.dot(ds, k2, preferred_element_type=F32) * ATTN_SCALE)
                dks.append(lax.dot_general(ds, qg, TN, preferred_element_type=F32) * ATTN_SCALE)
                dvs.append(lax.dot_general(p.astype(BF16), dog, TN, preferred_element_type=F32))
                dsink = -(psink * inv * delta)
                for g in range(GROUP):
                    r = h * GROUP + g
                    dsk_ref[r:r + 1, :] += jnp.sum(dsink[g * BLOCK:(g + 1) * BLOCK])
            dq_ref[...] = _unstack_heads(dqs).astype(BF16)
            dk2 = jnp.concatenate(dks, axis=1)
            dv2 = jnp.concatenate(dvs, axis=1)
            dk_ref[...] = (ck_ref[...] + dk2[:BLOCK]).astype(BF16)
            dv_ref[...] = (cvv_ref[...] + dv2[:BLOCK]).astype(BF16)
            ck_ref[...] = dk2[BLOCK:]
            cvv_ref[...] = dv2[BLOCK:]

        @pl.when(i == nb)
        def _():
            dk_ref[...] = ck_ref[...].astype(BF16)
            dv_ref[...] = cvv_ref[...].astype(BF16)

    cur = lambda i: jnp.minimum(i, nb - 1)
    done = lambda i: jnp.maximum(i - 1, 0)
    return _call(
        comm, body, name="attn_bwd", grid=(nb + 1,),
        in_specs=[pl.BlockSpec(memory_space=pltpu.SMEM), *_attn_specs(nb),
                  pl.BlockSpec((BLOCK, ATTN_W), lambda i: (cur(i), 0)),
                  pl.BlockSpec((BLOCK, ATTN_W), lambda i: (cur(i), 0))],
        out_specs=[pl.BlockSpec((BLOCK, ATTN_W), lambda i: (cur(i), 0)),
                   pl.BlockSpec((BLOCK, KV_W), lambda i: (done(i), 0)),
                   pl.BlockSpec((BLOCK, KV_W), lambda i: (done(i), 0)),
                   pl.BlockSpec((N_HEADS, 128), lambda i: (0, 0))],
        out_shape=[jax.ShapeDtypeStruct((s, ATTN_W), BF16), jax.ShapeDtypeStruct((s, KV_W), BF16),
                   jax.ShapeDtypeStruct((s, KV_W), BF16), jax.ShapeDtypeStruct((N_HEADS, 128), F32)],
        scratch_shapes=[pltpu.VMEM((BLOCK, KV_W), F32), pltpu.VMEM((BLOCK, KV_W), F32)],
        compiler_params=_params("arbitrary"),
    )(sinks, qkv, qkv, qkv, qkv, qkv, o, do)


def _inproj_bwd(dq, dk, dv, dc3, dgt, w_in, x, dh1, g1, tm, comm=None):
    s = x.shape[0]

    def body(dq_ref, dk_ref, dv_ref, dc3_ref, dgt_ref, w_ref, x_ref, dh1_ref, g_ref,
             dx_ref, dp_ref, gb_ref, gg_ref):
        @pl.when(pl.program_id(0) == 0)
        def _():
            gb_ref[...] = jnp.zeros_like(gb_ref)
            gg_ref[...] = jnp.zeros_like(gg_ref)

        dp = jnp.concatenate([dq_ref[...], dk_ref[...], dv_ref[...], dc3_ref[...], dgt_ref[...]], axis=1)
        dp_ref[...] = dp
        gb_ref[...] += jnp.sum(dp.astype(F32), axis=0, keepdims=True)
        dxn = jnp.dot(dp, w_ref[...], preferred_element_type=F32)
        xf = x_ref[...]
        dx, dg = _rms_bwd(dxn, xf, _rstd(xf), g_ref[...])
        dx_ref[...] = dh1_ref[...] + dx
        gg_ref[...] += jnp.sum(dg, axis=0, keepdims=True)

    row = lambda w: pl.BlockSpec((tm, w), lambda i: (i, 0))
    acc = lambda w: pl.BlockSpec((1, w), lambda i: (0, 0))
    return _call(
        comm, body, name="inproj_bwd", grid=(s // tm,),
        in_specs=[row(ATTN_W), row(KV_W), row(KV_W), row(C3_W), row(GATES_W), _resident((IN_W, D_MODEL)),
                  row(D_MODEL), row(D_MODEL), _resident((1, D_MODEL))],
        out_specs=[row(D_MODEL), row(IN_W), acc(IN_W), acc(D_MODEL)],
        out_shape=[jax.ShapeDtypeStruct((s, D_MODEL), F32), jax.ShapeDtypeStruct((s, IN_W), BF16),
                   jax.ShapeDtypeStruct((1, IN_W), F32), jax.ShapeDtypeStruct((1, D_MODEL), F32)],
        compiler_params=_params("arbitrary"),
    )(dq, dk, dv, dc3, dgt, w_in, x, dh1, g1)


def _wgrad(a, b, bm, bn, bk, name, comm=None):
    s, m = a.shape
    n = b.shape[1]
    nk = s // bk

    def body(a_ref, b_ref, o_ref, acc_ref):
        k = pl.program_id(2)

        @pl.when(k == 0)
        def _():
            acc_ref[...] = jnp.zeros_like(acc_ref)

        acc_ref[...] += lax.dot_general(a_ref[...].astype(BF16), b_ref[...].astype(BF16), TN,
                                        preferred_element_type=F32)

        @pl.when(k == nk - 1)
        def _():
            o_ref[...] = acc_ref[...].astype(BF16)

    return _call(
        comm, body, name=name, grid=(m // bm, n // bn, nk),
        in_specs=[pl.BlockSpec((bk, bm), lambda i, j, k: (k, i)), pl.BlockSpec((bk, bn), lambda i, j, k: (k, j))],
        out_specs=pl.BlockSpec((bm, bn), lambda i, j, k: (i, j)),
        out_shape=jax.ShapeDtypeStruct((m, n), BF16),
        scratch_shapes=[pltpu.VMEM((bm, bn), F32)],
        compiler_params=_params("parallel", "parallel", "arbitrary"),
    )(a, b)


def _wgrad_in(xn, dproj, bk, comm=None):
    s = xn.shape[0]
    nk = s // bk

    def body(a_ref, b_ref, o_ref, acc_ref):
        k = pl.program_id(0)

        @pl.when(k == 0)
        def _():
            acc_ref[...] = jnp.zeros_like(acc_ref)

        acc_ref[...] += lax.dot_general(b_ref[...], a_ref[...], TN, preferred_element_type=F32)

        @pl.when(k == nk - 1)
        def _():
            o_ref[...] = acc_ref[...].astype(BF16)

    return _call(
        comm, body, name="wgrad_in", grid=(nk,),
        in_specs=[pl.BlockSpec((bk, D_MODEL), lambda k: (k, 0)), pl.BlockSpec((bk, IN_W), lambda k: (k, 0))],
        out_specs=_resident((IN_W, D_MODEL)),
        out_shape=jax.ShapeDtypeStruct((IN_W, D_MODEL), BF16),
        scratch_shapes=[pltpu.VMEM((IN_W, D_MODEL), F32)],
        compiler_params=_params("arbitrary"),
    )(xn, dproj)


class _Carry:
    def __init__(self, jobs, reads=None, bufs=None, fresh=None):
        self.jobs, self.reads, self.bufs, self.fresh = jobs, reads or {}, bufs or {}, fresh or {}
        self.out = {}


class _Job:
    def __init__(self, n_sems, plan):
        self.n_sems, self.plan = n_sems, plan


def _plan_all(jobs, hbm, send, recv):
    pos = _position()
    starts, waits, base = [], [], 0
    for job in jobs:
        s, w = job.plan(hbm, pos, send, recv, base)
        starts, waits, base = starts + s, waits + w, base + job.n_sems
    return starts, waits


def _call(comm, body, **kw):
    if comm is None:
        return pl.pallas_call(body, **kw)
    grid = kw["grid"]
    single = not isinstance(kw["out_shape"], (list, tuple))
    out_shape = [kw["out_shape"]] if single else list(kw["out_shape"])
    out_specs = [kw["out_specs"]] if single else list(kw["out_specs"])
    in_specs = list(kw["in_specs"])
    scratch = list(kw.get("scratch_shapes", ()))
    r_names, b_names, f_names = list(comm.reads), list(comm.bufs), list(comm.fresh)
    n_args, n_out, n_scr = len(in_specs), len(out_shape), len(scratch)
    n_sems = sum(j.n_sems for j in comm.jobs)

    def wrapped(*refs):
        k = n_args
        hbm = dict(zip(r_names, refs[k:k + len(r_names)]))
        k += len(r_names) + len(b_names)
        outs = refs[k:k + n_out]
        k += n_out
        hbm.update(zip(b_names + f_names, refs[k:k + len(b_names) + len(f_names)]))
        k += len(b_names) + len(f_names)
        send, recv = refs[k + n_scr:]
        starts, waits = _plan_all(comm.jobs, hbm, send, recv)
        ids = [pl.program_id(a) for a in range(len(grid))]
        first = functools.reduce(jnp.logical_and, [i == 0 for i in ids])
        last = functools.reduce(jnp.logical_and, [i == g - 1 for i, g in zip(ids, grid)])

        @pl.when(first)
        def _():
            for cp in starts:
                cp.start()

        body(*refs[:n_args], *outs, *refs[k:k + n_scr])

        @pl.when(last)
        def _():
            for cp in waits:
                cp.wait_recv()
            for cp in starts:
                cp.wait_send()

    sems = pltpu.SemaphoreType.DMA((n_sems,))
    held = [jax.ShapeDtypeStruct(a.shape, a.dtype) for a in comm.bufs.values()] + list(comm.fresh.values())
    call = pl.pallas_call(
        wrapped, name=kw["name"], grid=grid,
        in_specs=in_specs + [_ANY] * (len(r_names) + len(b_names)),
        out_specs=out_specs + [_ANY] * len(held),
        out_shape=out_shape + held,
        input_output_aliases={n_args + len(r_names) + i: n_out + i for i in range(len(b_names))},
        scratch_shapes=scratch + [sems, sems],
        compiler_params=_params(*["arbitrary"] * len(grid)),
    )

    def run(*args):
        res = call(*args, *comm.reads.values(), *comm.bufs.values())
        comm.out = dict(zip(b_names + f_names, res[n_out:]))
        return res[0] if single else res[:n_out]

    return run


def _exchange(name, phases, reads=None, bufs=None, fresh=None):
    comm = _Carry([j for ph in phases for j in ph], reads, bufs, fresh)
    r_names, b_names, f_names = list(comm.reads), list(comm.bufs), list(comm.fresh)
    n_sems = sum(j.n_sems for j in comm.jobs)

    def body(*refs):
        hbm = dict(zip(r_names, refs[:len(r_names)]))
        k = len(r_names) + len(b_names)
        hbm.update(zip(b_names + f_names, refs[k:k + len(b_names) + len(f_names)]))
        send, recv = refs[-2:]
        pos = _position()
        started, base = [], 0
        for ph in phases:
            waits = []
            for job in ph:
                s, w = job.plan(hbm, pos, send, recv, base)
                base += job.n_sems
                for cp in s:
                    cp.start()
                started, waits = started + s, waits + w
            for cp in waits:
                cp.wait_recv()
        for cp in started:
            cp.wait_send()

    sems = pltpu.SemaphoreType.DMA((n_sems,))
    held = [jax.ShapeDtypeStruct(a.shape, a.dtype) for a in comm.bufs.values()] + list(comm.fresh.values())
    res = pl.pallas_call(
        body, name=name, in_specs=[_ANY] * (len(r_names) + len(b_names)), out_specs=[_ANY] * len(held),
        out_shape=held, input_output_aliases={len(r_names) + i: i for i in range(len(b_names))},
        scratch_shapes=[sems, sems],
    )(*comm.reads.values(), *comm.bufs.values())
    return dict(zip(b_names + f_names, res))


_HBM = pl.BlockSpec(memory_space=pltpu.HBM)
_SEM = pl.BlockSpec(memory_space=pltpu.SEMAPHORE)
_EFFECT = pltpu.SideEffectType.DATAFLOW_SIDE_EFFECTING


def _start_exchanges(name, groups):
    names = [list(arrays) for _, arrays in groups]
    first = [sum(len(ns) for ns in names[:g]) for g in range(len(groups))]
    n, ng = sum(len(ns) for ns in names), len(groups)

    def body(*refs):
        for g, (jobs, _) in enumerate(groups):
            hbm = dict(zip(names[g], refs[first[g]:first[g] + len(names[g])]))
            for cp in _plan_all(jobs, hbm, refs[n + 2 * g], refs[n + 2 * g + 1])[0]:
                cp.start()
        refs[-1][...] = jnp.zeros_like(refs[-1])

    given = [pltpu.with_memory_space_constraint(
        a if isinstance(a, jax.Array) else lax.empty(a.shape, a.dtype), pltpu.HBM)
        for _, arrays in groups for a in arrays.values()]
    sems = [pltpu.SemaphoreType.DMA((sum(j.n_sems for j in jobs),)) for jobs, _ in groups for _ in range(2)]
    res = pl.pallas_call(
        body, name=name,
        out_shape=(*sems, *[pltpu.HBM(a.shape, a.dtype) for a in given], jax.ShapeDtypeStruct((8, 128), F32)),
        in_specs=[_HBM] * n, out_specs=(*[_SEM] * (2 * ng), *[_HBM] * n, pl.BlockSpec(memory_space=pltpu.VMEM)),
        input_output_aliases={i: 2 * ng + i for i in range(n)},
        compiler_params=pltpu.CompilerParams(has_side_effects=_EFFECT),
    )(*given)
    held = res[2 * ng:2 * ng + n]
    states = [(names[g], groups[g][0], res[2 * g], res[2 * g + 1], held[first[g]:first[g] + len(names[g])])
              for g in range(ng)]
    return states, res[-1]


def _start_exchange(name, jobs, arrays):
    states, token = _start_exchanges(name, [(jobs, arrays)])
    return states[0], token


def _finish_exchange(name, state, after):
    names, jobs, send_sem, recv_sem, held = state
    n = len(names)

    def body(*refs):
        hbm = dict(zip(names, refs[:n]))
        send, recv = refs[n:n + 2]
        starts, waits = _plan_all(jobs, hbm, send, recv)
        for cp in waits:
            cp.wait_recv()
        for cp in starts:
            cp.wait_send()

    res = pl.pallas_call(
        body, name=name, out_shape=tuple(pltpu.HBM(a.shape, a.dtype) for a in held),
        in_specs=[_HBM] * n + [_SEM, _SEM, _ANY], out_specs=tuple([_HBM] * n),
        input_output_aliases={i: i for i in range(n)},
        compiler_params=pltpu.CompilerParams(has_side_effects=_EFFECT),
    )(*held, send_sem, recv_sem, after)
    return dict(zip(names, res))


def _row_tile(rows, bytes_per_row):
    best = 16
    for t in range(16, rows + 1, 16):
        if rows % t == 0 and t * bytes_per_row <= 6 * 1024 * 1024:
            best = t
    return best


def _rowwise(fn, ins, out_dtypes, name):
    rows, cols = ins[0].shape[-2:]
    per_row = sum(a.size // rows * a.dtype.itemsize for a in ins) + sum(cols * jnp.dtype(d).itemsize for d in out_dtypes)
    tr = _row_tile(rows, per_row)
    n_in = len(ins)

    def body(*refs):
        outs = fn(*[r[...] for r in refs[:n_in]])
        for o_ref, o in zip(refs[n_in:], outs):
            o_ref[...] = o.astype(o_ref.dtype)

    def spec(a):
        if a.ndim == 3:
            return pl.BlockSpec((a.shape[0], tr, cols), lambda i: (0, i, 0))
        return pl.BlockSpec((tr, cols), lambda i: (i, 0))

    return pl.pallas_call(
        body, name=name, grid=(rows // tr,),
        in_specs=[spec(a) for a in ins],
        out_specs=[pl.BlockSpec((tr, cols), lambda i: (i, 0)) for _ in out_dtypes],
        out_shape=[jax.ShapeDtypeStruct((rows, cols), d) for d in out_dtypes],
        compiler_params=_params("parallel"),
    )(*ins)


def _tiled(fn, name, grid, pos, ins, outs):
    n_in = len(ins)

    def body(pos_ref, *refs):
        res = fn(*[r[...] for r in refs[:n_in]])
        for o_ref, o in zip(refs[n_in:], res):
            o_ref[...] = o.astype(o_ref.dtype)

    return pl.pallas_call(
        body, name=name,
        grid_spec=pltpu.PrefetchScalarGridSpec(
            num_scalar_prefetch=1, grid=grid,
            in_specs=[pl.BlockSpec(bs, im) for _, bs, im in ins],
            out_specs=[pl.BlockSpec(bs, im) for _, _, bs, im in outs]),
        out_shape=[jax.ShapeDtypeStruct(s, d) for s, d, _, _ in outs],
        compiler_params=_params("parallel"),
    )(pos, *[a for a, _, _ in ins])


def _adamw(w, g, m, v):
    m = ADAM_B1 * m + (1.0 - ADAM_B1) * g
    v = ADAM_B2 * v + (1.0 - ADAM_B2) * (g * g)
    m_hat = m / (1.0 - ADAM_B1 ** ADAM_STEP)
    v_hat = v / (1.0 - ADAM_B2 ** ADAM_STEP)
    return -ADAM_LR * (m_hat / (jnp.sqrt(v_hat) + ADAM_EPS) + ADAM_WD * w), m, v


def _adamw_small(params):
    n = len(params)

    def body(*refs):
        for k in range(n):
            w, g, m, v = (r[...] for r in refs[4 * k:4 * k + 4])
            for o_ref, o in zip(refs[4 * n + 3 * k:4 * n + 3 * k + 3], _adamw(w, g, m, v)):
                o_ref[...] = o

    flat = [a for p in params for a in p]
    return pl.pallas_call(
        body, name="adamw_small",
        out_shape=[jax.ShapeDtypeStruct(p[0].shape, F32) for p in params for _ in range(3)],
    )(*flat)


class _Layout:
    def __init__(self, rows, cols, stacked):
        self.rows, self.cols, self.stacked = rows, cols, stacked

    def whole(self, rows=None):
        r = self.rows if rows is None else rows
        return (N_CHIPS, r, self.cols) if self.stacked else (r, N_CHIPS * self.cols)

    def part_rows(self, h, q=0, nq=1):
        n = self.rows // 2 // nq
        return pl.ds(pl.multiple_of(h * (self.rows // 2) + q * n, 16), n)

    def half_rows(self, h):
        return self.part_rows(h)

    def block(self, ref, p, rows=slice(None)):
        if self.stacked:
            return ref.at[p, rows, :]
        return ref.at[rows, pl.ds(pl.multiple_of(p * self.cols, 128), self.cols)]

    def all_chips(self, ref, rows):
        return ref.at[:, rows, :] if self.stacked else ref.at[rows, :]


BIG = (
    _Layout(IN_SHARD, D_MODEL, True),
    _Layout(ATTN_W, D_MODEL // N_CHIPS, False),
    _Layout(CONV_W, D_MODEL // N_CHIPS, False),
    _Layout(D_MODEL // N_CHIPS, D_MODEL, True),
    _Layout(D_MODEL, FF2 // N_CHIPS, False),
    _Layout(D_FF // N_CHIPS, D_MODEL, True),
)
N_BIG = len(BIG)
_ANY = pl.BlockSpec(memory_space=pl.ANY)


def _position():
    x, y, c = lax.axis_index("x"), lax.axis_index("y"), lax.axis_index("c")
    return x, y, c, 2 * x + y


def _core_of_chip(p, c):
    return (p >> 1, p & 1, c)


def _place_cast(shard, lay, pos, name, after=None):
    rows, cols = shard.shape
    tr = _row_tile(rows, cols * 6)
    if lay.stacked:
        out = (lay.whole(), BF16, (None, tr, cols), lambda i, pos: (pos[0], i, 0))
    else:
        out = (lay.whole(), BF16, (tr, cols), lambda i, pos: (i, pos[0]))
    ins = [(shard, (tr, cols), lambda i, pos: (i, 0))]
    if after is not None:
        ins.append((after, (8, 128), lambda i, pos: (0, 0)))
    return _tiled(lambda a, *_: (a,), name, (rows // tr,), pos, ins, [out])[0]


def _remote(src, dst, send, recv, k, device):
    return pltpu.make_async_remote_copy(src_ref=src, dst_ref=dst, send_sem=send.at[k], recv_sem=recv.at[k],
                                        device_id=device, device_id_type=MESH)


def _arrival(dst, send, recv, k, me):
    return _remote(dst, dst, send, recv, k, me)


def _gather_ici(lay, name, q=0, nq=1):
    def plan(hbm, pos, send, recv, base):
        x, y, c, me = pos
        rows = lay.part_rows(c, q, nq)
        mine = lay.block(hbm[name], me, rows)
        starts = [_remote(mine, mine, send, recv, base + d - 1, _core_of_chip(me ^ d, c)) for d in (1, 2, 3)]
        waits = [_arrival(lay.block(hbm[name], me ^ d, rows), send, recv, base + d - 1, (x, y, c)) for d in (1, 2, 3)]
        return starts, waits
    return _Job(3, plan)


def _gather_d2d(lay, name, q=0, nq=1):
    def plan(hbm, pos, send, recv, base):
        x, y, c, me = pos
        starts, waits = [], []
        for d in (1, 2, 3):
            got = lay.block(hbm[name], me ^ d, lay.part_rows(c, q, nq))
            starts.append(_remote(got, got, send, recv, base + d - 1, (x, y, 1 - c)))
            waits.append(_arrival(lay.block(hbm[name], me ^ d, lay.part_rows(1 - c, q, nq)), send, recv, base + d - 1,
                                  (x, y, c)))
        return starts, waits
    return _Job(3, plan)


def _rs_pair(lay, grad, theirs):
    def plan(hbm, pos, send, recv, base):
        x, y, c, _ = pos
        out = _remote(lay.all_chips(hbm[grad], lay.half_rows(1 - c)), hbm[theirs], send, recv, base, (x, y, 1 - c))
        return [out], [_arrival(hbm[theirs], send, recv, base, (x, y, c))]
    return _Job(1, plan)


def _rs_chips(lay, sums, slots):
    def plan(hbm, pos, send, recv, base):
        x, y, c, me = pos
        starts = [_remote(lay.block(hbm[sums], me ^ d), hbm[slots].at[me], send, recv, base + d - 1,
                          _core_of_chip(me ^ d, c)) for d in (1, 2, 3)]
        waits = [_arrival(hbm[slots].at[me ^ d], send, recv, base + d - 1, (x, y, c)) for d in (1, 2, 3)]
        return starts, waits
    return _Job(3, plan)


def _rs_share(lay, shard):
    def plan(hbm, pos, send, recv, base):
        x, y, c, _ = pos
        mine = hbm[shard].at[lay.half_rows(c), :]
        other = hbm[shard].at[lay.half_rows(1 - c), :]
        return [_remote(mine, mine, send, recv, base, (x, y, 1 - c))], [_arrival(other, send, recv, base, (x, y, c))]
    return _Job(1, plan)


def _slots_shape(lay):
    return jax.ShapeDtypeStruct((N_CHIPS, lay.rows // 2, lay.cols), BF16)


def _theirs_shape(lay):
    return jax.ShapeDtypeStruct(lay.whole(lay.rows // 2), BF16)


def _pair_sum(grad, theirs, lay, pos, name):
    half = lay.rows // 2
    add = lambda a, b: (a.astype(F32) + b.astype(F32),)
    if lay.stacked:
        tr = _row_tile(half, lay.cols * 6)
        nt = half // tr
        flat = lambda a: a.reshape(-1, lay.cols)
        mine = lambda t, pos: ((t // nt) * (2 * nt) + pos[1] * nt + t % nt, 0)
        grid, blk = (N_CHIPS * nt,), (tr, lay.cols)
        grad, theirs = flat(grad), flat(theirs)
    else:
        tr = _row_tile(half, N_CHIPS * lay.cols * 6)
        nt = half // tr
        mine = lambda t, pos: (pos[1] * nt + t, 0)
        grid, blk = (nt,), (tr, N_CHIPS * lay.cols)
    same = lambda t, pos: (t, 0)
    out = _tiled(add, name, grid, pos, [(grad, blk, mine), (theirs, blk, same)], [(theirs.shape, BF16, blk, same)])[0]
    return out.reshape(lay.whole(half))


def _chip_sum(sums, slots, lay, pos, name, after=None):
    half = lay.rows // 2
    tr = _row_tile(half, lay.cols * 12)
    nt = half // tr
    blk3 = (None, tr, lay.cols)
    if lay.stacked:
        own = (sums, blk3, lambda i, pos: (pos[0], i, 0))
    else:
        own = (sums, (tr, lay.cols), lambda i, pos: (i, pos[0]))
    others = [(slots, blk3, functools.partial(lambda d, i, pos: (pos[0] ^ d, i, 0), d)) for d in (1, 2, 3)]

    def add(a, b1, b2, b3, *_):
        return (((a.astype(F32) + b1.astype(F32)) + b2.astype(F32)) + b3.astype(F32),)

    if after is not None:
        others.append((after, (8, 128), lambda i, pos: (0, 0)))
    return _tiled(add, name, (nt,), pos, [own] + others,
                  [((lay.rows, lay.cols), F32, (tr, lay.cols), lambda i, pos: (pos[1] * nt + i, 0))])[0]


N_DEV = 8


def _to_all(src, slots):
    def plan(hbm, pos, send, recv, base):
        x, y, c, _ = pos
        idx = 4 * x + 2 * y + c
        starts = [_remote(hbm[src], hbm[slots].at[idx], send, recv, base + k - 1,
                          (x ^ (k >> 2), y ^ ((k >> 1) & 1), c ^ (k & 1))) for k in range(1, N_DEV)]
        waits = [_arrival(hbm[slots].at[idx ^ k], send, recv, base + k - 1, (x, y, c)) for k in range(1, N_DEV)]
        return starts, waits
    return _Job(N_DEV - 1, plan)


def _sum_slots(own, slots, pos):
    def body(pos_ref, own_ref, slots_ref, o_ref):
        idx = 2 * pos_ref[0] + pos_ref[1]
        term = lambda q: jnp.where(idx == q, own_ref[...], slots_ref[q])
        acc = term(0)
        for q in range(1, N_DEV):
            acc = acc + term(q)
        o_ref[...] = acc

    return pl.pallas_call(
        body, name="sum_small", out_shape=jax.ShapeDtypeStruct(own.shape, F32),
        in_specs=[pl.BlockSpec(memory_space=pltpu.SMEM), pl.BlockSpec(memory_space=pltpu.VMEM),
                  pl.BlockSpec(memory_space=pltpu.VMEM)],
    )(pos, own, slots)


def _exchange_small(v):
    rows = v.shape[0]

    def body(v_ref, slots, send, recv):
        x, y, c = lax.axis_index("x"), lax.axis_index("y"), lax.axis_index("c")
        idx = 4 * x + 2 * y + c
        slots[idx] = v_ref[...]

        def to_peer(k):
            return pltpu.make_async_remote_copy(
                src_ref=v_ref, dst_ref=slots.at[idx], send_sem=send.at[k - 1], recv_sem=recv.at[k - 1],
                device_id=(x ^ (k >> 2), y ^ ((k >> 1) & 1), c ^ (k & 1)), device_id_type=MESH)

        def from_peer(k):
            return pltpu.make_async_remote_copy(
                src_ref=v_ref, dst_ref=slots.at[idx ^ k], send_sem=send.at[k - 1], recv_sem=recv.at[k - 1],
                device_id=(x, y, c), device_id_type=MESH)

        for k in range(1, N_DEV):
            to_peer(k).start()
        for k in range(1, N_DEV):
            from_peer(k).wait_recv()
        for k in range(1, N_DEV):
            to_peer(k).wait_send()

    sems = pltpu.SemaphoreType.DMA((N_DEV - 1,))
    return pl.pallas_call(
        body, name="allgather_small", out_shape=jax.ShapeDtypeStruct((N_DEV, rows, 128), F32),
        scratch_shapes=[sems, sems],
    )(v)


def _pack_rows(parts):
    padded = [jnp.pad(a, ((0, -a.shape[0] % 8), (0, 0))) for a in parts]
    starts = [sum(p.shape[0] for p in padded[:k]) for k in range(len(padded))]
    return jnp.concatenate(padded, axis=0), starts


def kernel(x, mix_norm, w_in, b_in, sinks, conv_w, w_attn_branch, w_conv_branch, w_out, ffn_norm, w_up, ffn_conv_w, w_down, final_norm, loss_target, m_mix_norm, m_w_in, m_b_in, m_sinks, m_conv_w, m_w_attn_branch, m_w_conv_branch, m_w_out, m_ffn_norm, m_w_up, m_ffn_conv_w, m_w_down, m_final_norm, v_mix_norm, v_w_in, v_b_in, v_sinks, v_conv_w, v_w_attn_branch, v_w_conv_branch, v_w_out, v_ffn_norm, v_w_up, v_ffn_conv_w, v_w_down, v_final_norm):
    me = 2 * lax.axis_index("x") + lax.axis_index("y")
    big_w = [w_in[0].T, w_attn_branch[0], w_conv_branch[0], w_out[0], w_up[0], w_down[0]]
    big_m = [m_w_in[0].T, m_w_attn_branch[0], m_w_conv_branch[0], m_w_out[0], m_w_up[0], m_w_down[0]]
    big_v = [v_w_in[0].T, v_w_attn_branch[0], v_w_conv_branch[0], v_w_out[0], v_w_up[0], v_w_down[0]]
    names = ("w_in", "w_ab", "w_cb", "w_out", "w_up", "w_down")

    pos = jnp.stack([me, lax.axis_index("c")]).astype(jnp.int32)

    lay = dict(zip(names, BIG))
    xs, target, sk = x[0], loss_target[0], sinks[0]
    s = xs.shape[0]
    tm, tm2, bk = min(256, s), min(512, s), min(1024, s)

    taps, (_, t0) = _pack_rows([conv_w[0], ffn_conv_w[0].reshape(3 * (FF2 // N_CHIPS // 128), 128)])
    taps = _exchange_small(taps)
    conv_full = taps[0::2, 0:3].transpose(1, 0, 2).reshape(3, CONV_W)
    ffn_cw_full = taps[0::2, t0:t0 + 33].reshape(N_CHIPS, 3, FF2 // N_CHIPS).transpose(1, 0, 2).reshape(3, FF2)
    placed = {"w_in": _place_cast(big_w[0], lay["w_in"], pos, "cast_w_in", after=taps[0, 0:8])}
    fly_in, started = _start_exchange("gather_in_start", [_gather_ici(lay["w_in"], "w_in")], {"w_in": placed["w_in"]})
    for w, n in zip(big_w[1:], names[1:]):
        placed[n] = _place_cast(w, lay[n], pos, "cast_" + n, after=started)
    trio = ("w_ab", "w_cb", "w_out")
    (fly_trio, fly_down, fly_up), started = _start_exchanges("gather_rest_start", [
        ([_gather_ici(lay[n], n) for n in ws], {n: placed[n] for n in ws}) for ws in (trio, ("w_down",), ("w_up",))])

    got = _finish_exchange("gather_in_wait", fly_in, after=started)
    w_in_full = _exchange("gather_in_d2d", [[_gather_d2d(lay["w_in"], "w_in")]], bufs=got)["w_in"].reshape(IN_W, D_MODEL)
    xn, qkv, c3, gates = _inproj_fwd(xs, mix_norm, w_in_full, b_in, tm2)
    k2 = _Carry([_gather_d2d(lay[n], n) for n in trio], bufs=_finish_exchange("gather_trio_wait", fly_trio, after=qkv))
    attn = _attn_fwd(qkv, sk, comm=k2)
    w_ab, w_cb = k2.out["w_ab"], k2.out["w_cb"]
    w_out_full = k2.out["w_out"].reshape(D_MODEL, D_MODEL)
    k3 = _Carry([_gather_d2d(lay["w_down"], "w_down")], bufs=_finish_exchange("gather_down_wait", fly_down, after=attn))
    conv, a, cv, merged, h1, hn = _mix_fwd(xs, attn, c3, gates, conv_full, w_ab, w_cb, w_out_full, ffn_norm, tm, comm=k3)
    w_down_full = k3.out["w_down"].reshape(D_FF, D_MODEL)
    w_up_full = _exchange("gather_up_d2d", [[_gather_d2d(lay["w_up"], "w_up")]],
                          bufs=_finish_exchange("gather_up_wait", fly_up, after=hn))["w_up"]
    u, up, act, dh2, loss_part, g_fn = _ffn_fwd_loss(hn, h1, w_up_full, ffn_cw_full, w_down_full,
                                                     final_norm[None, :], target, tm)

    grads, sums, slots = {}, {}, {}

    def pair(*ws):
        return _Carry([_rs_pair(lay[n], "g_" + n, "t_" + n) for n in ws], reads={"g_" + n: grads[n] for n in ws},
                      fresh={"t_" + n: _theirs_shape(lay[n]) for n in ws})

    def chips(*ws, also=None):
        k = _Carry([_rs_chips(lay[n], "s_" + n, "r_" + n) for n in ws], reads={"s_" + n: sums[n] for n in ws},
                   fresh={"r_" + n: _slots_shape(lay[n]) for n in ws})
        if also is not None:
            k = _Carry(k.jobs + also.jobs, {**k.reads, **also.reads}, None, {**k.fresh, **also.fresh})
        return k

    def pair_sums(k, *ws):
        for n in ws:
            sums[n] = _pair_sum(grads[n], k.out["t_" + n], lay[n], pos, "pair_sum_" + n)

    def take_slots(k, *ws):
        for n in ws:
            slots[n] = k.out["r_" + n]

    du, dh1, g_fcw, g_g2 = _ffn_bwd(dh2, u, up, h1, w_up_full, ffn_cw_full, w_down_full, ffn_norm, tm)
    grads["w_down"] = _wgrad(act, dh2, D_FF // 2, D_MODEL, bk, "wgrad_down").reshape(lay["w_down"].whole())
    k4 = pair("w_down")
    grads["w_up"] = _wgrad(hn, du, D_MODEL, FF2 // 4, bk, "wgrad_up", comm=k4)
    pair_sums(k4, "w_down")
    k5 = chips("w_down", also=pair("w_up"))
    dattn, da, dcv, dc3, dgt, g_cw = _mix_bwd(dh1, gates, a, cv, c3, conv_full, w_ab, w_cb, w_out_full, tm, comm=k5)
    take_slots(k5, "w_down")
    pair_sums(k5, "w_up")
    grads["w_out"] = _wgrad(merged, dh1, D_MODEL, D_MODEL, bk, "wgrad_out").reshape(lay["w_out"].whole())
    grads["w_ab"] = _wgrad(attn, da, ATTN_W, D_MODEL, bk, "wgrad_ab")
    grads["w_cb"] = _wgrad(conv, dcv, CONV_W, D_MODEL, bk, "wgrad_cb")
    k6 = chips("w_up", also=pair("w_out", "w_ab", "w_cb"))
    dq, dk, dv, g_sk = _attn_bwd(qkv, sk, attn, dattn, comm=k6)
    take_slots(k6, "w_up")
    pair_sums(k6, "w_out", "w_ab", "w_cb")
    grad_x, dproj, g_b, g_g1 = _inproj_bwd(dq, dk, dv, dc3, dgt, w_in_full, xs, dh1, mix_norm, tm2)

    parts = [loss_part, g_g1, g_b, jnp.pad(g_sk[:, 0], (0, 120))[None, :], g_cw, g_g2, g_fcw, g_fn]
    packed, at = _pack_rows([p.reshape(-1, 128) for p in parts])
    small_flight, _ = _start_exchange("small_start", [_to_all("v", "slots")],
                                      {"v": packed, "slots": jnp.zeros((N_DEV, *packed.shape), F32)})
    k8 = chips("w_out", "w_ab", "w_cb")
    grads["w_in"] = _wgrad_in(xn, dproj, min(512, s), comm=k8).reshape(lay["w_in"].whole())
    take_slots(k8, "w_out", "w_ab", "w_cb")
    others = names[1:]
    in_flight, started = _start_exchange("rs_pair_in_start", [_rs_pair(lay["w_in"], "g", "t")],
                                         {"g": grads["w_in"], "t": _theirs_shape(lay["w_in"])})
    halves = {n: _chip_sum(sums[n], slots[n], lay[n], pos, "chip_sum_" + n, after=started) for n in others}
    landed = _finish_exchange("rs_pair_in_wait", in_flight, after=halves["w_down"])
    sums["w_in"] = _pair_sum(landed["g"], landed["t"], lay["w_in"], pos, "pair_sum_w_in")
    in_flight, started = _start_exchange("rs_chips_in_start", [_rs_chips(lay["w_in"], "s", "r")],
                                         {"s": sums["w_in"], "r": _slots_shape(lay["w_in"])})
    shared = _exchange("share_halves", [[_rs_share(lay[n], n) for n in others]], reads={"after": started}, bufs=halves)
    w_of, m_of, v_of = dict(zip(names, big_w)), dict(zip(names, big_m)), dict(zip(names, big_v))
    adam = lambda n, g: _rowwise(_adamw, [w_of[n], g, m_of[n], v_of[n]], [F32, F32, F32], "adamw_" + n)
    new_of = {n: adam(n, shared[n]) for n in others}

    arrived = _finish_exchange("small_wait", small_flight, after=new_of["w_down"][0])
    total = _sum_slots(arrived["v"], arrived["slots"], pos)
    part = lambda k: total[at[k]:at[k] + parts[k].size // 128].reshape(parts[k].shape)
    loss = total[0, 0]
    g_mix, g_b, g_g2, g_fn = part(1), part(2), part(5), part(7)
    g_sk = part(3)[:, 0:N_HEADS]
    g_cw = lax.dynamic_slice(part(4), (0, me * 128), (3, 128))
    g_fcw = lax.dynamic_slice(part(6), (0, me * (FF2 // N_CHIPS)), (3, FF2 // N_CHIPS))
    small_p = [
        (mix_norm, g_mix, m_mix_norm, v_mix_norm), (b_in, g_b, m_b_in, v_b_in), (sinks, g_sk, m_sinks, v_sinks),
        (conv_w[0], g_cw, m_conv_w[0], v_conv_w[0]), (ffn_norm, g_g2, m_ffn_norm, v_ffn_norm),
        (ffn_conv_w[0], g_fcw, m_ffn_conv_w[0], v_ffn_conv_w[0]),
        (final_norm[None, :], g_fn, m_final_norm[None, :], v_final_norm[None, :])]
    small_new = _adamw_small(small_p)
    small_new = [small_new[3 * k:3 * k + 3] for k in range(len(small_p))]

    landed = _finish_exchange("rs_chips_in_wait", in_flight, after=small_new[0][0])
    half_in = _chip_sum(landed["s"], landed["r"], lay["w_in"], pos, "chip_sum_w_in")
    shared["w_in"] = _exchange("share_in", [[_rs_share(lay["w_in"], "w_in")]], bufs={"w_in": half_in})["w_in"]
    new_of["w_in"] = adam("w_in", shared["w_in"])
    big_g = [shared[n] for n in names]
    big_new = [new_of[n] for n in names]

    order = [("s", 0), ("b", 0), ("s", 1), ("s", 2), ("s", 3), ("b", 1), ("b", 2), ("b", 3), ("s", 4), ("b", 4),
             ("s", 5), ("b", 5), ("s", 6)]
    shapes = [mix_norm.shape, w_in.shape, b_in.shape, sinks.shape, conv_w.shape, w_attn_branch.shape,
              w_conv_branch.shape, w_out.shape, ffn_norm.shape, w_up.shape, ffn_conv_w.shape, w_down.shape,
              final_norm.shape]
    small_g = [p[1] for p in small_p]
    big_g[0] = big_g[0].T
    big_new[0] = [a.T for a in big_new[0]]
    out_g = [(small_g[k] if kind == "s" else big_g[k]).reshape(shp) for (kind, k), shp in zip(order, shapes)]
    news = [[(small_new[k][j] if kind == "s" else big_new[k][j]).reshape(shp) for (kind, k), shp in zip(order, shapes)]
            for j in range(3)]
    return (loss, grad_x[None], *out_g, *news[0], *news[1], *news[2])
```

```python
import functools

import jax
import jax.numpy as jnp
from jax import lax
from jax.experimental import pallas as pl
from jax.experimental.pallas import tpu as pltpu

F32 = jnp.float32
BF16 = jnp.bfloat16

D_MODEL = 1024
HEAD_DIM = 64
N_HEADS = 8
N_KV_HEADS = 2
GROUP = N_HEADS // N_KV_HEADS
BLOCK = 128
ATTN_SCALE = HEAD_DIM ** -0.5
ATTN_W = N_HEADS * HEAD_DIM
KV_W = N_KV_HEADS * HEAD_DIM
CONV_W = 512
QKV_W = ATTN_W + 2 * KV_W
C3_W = 3 * CONV_W
GATES_W = 2 * D_MODEL
IN_W = QKV_W + C3_W + GATES_W
D_FF = 2816
FF2 = 2 * D_FF
NORM_EPS = 1e-5
N_CHIPS = 4
IN_SHARD = IN_W // N_CHIPS
NEG = -1e30

ADAM_LR = 0.001
ADAM_B1 = 0.9
ADAM_B2 = 0.999
ADAM_EPS = 1e-08
ADAM_WD = 0.01
ADAM_STEP = 10

VMEM_LIMIT = 56 * 1024 * 1024
MESH = pl.DeviceIdType.MESH

NT = (((1,), (1,)), ((), ()))
TN = (((0,), (0,)), ((), ()))


def _params(*sem):
    return pltpu.CompilerParams(dimension_semantics=sem, vmem_limit_bytes=VMEM_LIMIT)


def _resident(shape):
    return pl.BlockSpec(shape, lambda *_: (0,) * len(shape), pipeline_mode=pl.Buffered(1))


def _sigmoid(v):
    return 0.5 * jnp.tanh(0.5 * v) + 0.5


def _rstd(v):
    return lax.rsqrt(jnp.mean(v * v, axis=-1, keepdims=True) + NORM_EPS)


def _rms_bwd(dy, v, rstd, g):
    vhat = v * rstd
    t = dy * g
    return rstd * (t - vhat * jnp.mean(t * vhat, axis=-1, keepdims=True)), dy * vhat


def _taps(z, cw):
    return cw[2:3] * z + cw[1:2] * pltpu.roll(z, 1, 0) + cw[0:1] * pltpu.roll(z, 2, 0)


def _causal_conv(z, prev, cw):
    edge = _taps(jnp.concatenate([prev, z[0:8]], axis=0), cw)
    return jnp.concatenate([edge[8:16], _taps(z, cw)[8:]], axis=0)


def _rows_after(z, nxt):
    n = z.shape[0]
    edge = jnp.concatenate([z[n - 8:n], nxt], axis=0)
    return tuple(jnp.concatenate([pltpu.roll(z, n - k, 0)[:n - 8], pltpu.roll(edge, 16 - k, 0)[0:8]], axis=0)
                 for k in (1, 2))


def _inproj_fwd(x, g1, w_in, b_in, tm, comm=None):
    s = x.shape[0]

    def body(x_ref, g_ref, w_ref, b_ref, xn_ref, qkv_ref, c3_ref, gt_ref):
        xf = x_ref[...]
        xn = (xf * _rstd(xf) * g_ref[...]).astype(BF16)
        xn_ref[...] = xn

        def seg(a, b):
            return lax.dot_general(xn, w_ref[a:b, :], NT, preferred_element_type=F32) + b_ref[:, a:b]

        qkv_ref[...] = seg(0, QKV_W).astype(BF16)
        c3_ref[...] = seg(QKV_W, QKV_W + C3_W)
        gt_ref[...] = seg(QKV_W + C3_W, IN_W)

    row = lambda w: pl.BlockSpec((tm, w), lambda i: (i, 0))
    return _call(
        comm, body, name="inproj_fwd", grid=(s // tm,),
        in_specs=[row(D_MODEL), _resident((1, D_MODEL)), _resident((IN_W, D_MODEL)), _resident((1, IN_W))],
        out_specs=[row(D_MODEL), row(QKV_W), row(C3_W), row(GATES_W)],
        out_shape=[jax.ShapeDtypeStruct((s, D_MODEL), BF16), jax.ShapeDtypeStruct((s, QKV_W), BF16),
                   jax.ShapeDtypeStruct((s, C3_W), F32), jax.ShapeDtypeStruct((s, GATES_W), F32)],
        compiler_params=_params("parallel"),
    )(x, g1, w_in, b_in)


def _attn_mask(first_block):
    qi = lax.broadcasted_iota(jnp.int32, (GROUP * BLOCK, 2 * BLOCK), 0) & (BLOCK - 1)
    kj = lax.broadcasted_iota(jnp.int32, (GROUP * BLOCK, 2 * BLOCK), 1)
    band = (kj > qi) & (kj <= qi + BLOCK)
    return band & ((kj >= BLOCK) | jnp.logical_not(first_block))


def _sink_column(sk_ref, h):
    rows = lax.broadcasted_iota(jnp.int32, (GROUP * BLOCK, 1), 0)
    col = jnp.full((GROUP * BLOCK, 1), sk_ref[h * GROUP], F32)
    for g in range(1, GROUP):
        col = jnp.where(rows >= g * BLOCK, sk_ref[h * GROUP + g], col)
    return col


def _stack_heads(t, h):
    return jnp.concatenate(
        [t[:, (h * GROUP + g) * HEAD_DIM:(h * GROUP + g + 1) * HEAD_DIM] for g in range(GROUP)], axis=0)


def _unstack_heads(per_kv):
    return jnp.concatenate(
        [t[g * BLOCK:(g + 1) * BLOCK] for t in per_kv for g in range(GROUP)], axis=1)


def _attn_specs(nb):
    cur = lambda i: jnp.minimum(i, nb - 1)
    prev = lambda i: jnp.maximum(jnp.minimum(i, nb - 1) - 1, 0)
    q = pl.BlockSpec((BLOCK, ATTN_W), lambda i: (cur(i), 0))
    kp = pl.BlockSpec((BLOCK, KV_W), lambda i: (prev(i), ATTN_W // KV_W))
    kc = pl.BlockSpec((BLOCK, KV_W), lambda i: (cur(i), ATTN_W // KV_W))
    vp = pl.BlockSpec((BLOCK, KV_W), lambda i: (prev(i), ATTN_W // KV_W + 1))
    vc = pl.BlockSpec((BLOCK, KV_W), lambda i: (cur(i), ATTN_W // KV_W + 1))
    return q, kp, kc, vp, vc


def _attn_fwd(qkv, sinks, comm=None):
    s = qkv.shape[0]
    nb = s // BLOCK

    def body(sk_ref, q_ref, kp_ref, kc_ref, vp_ref, vc_ref, o_ref):
        mask = _attn_mask(pl.program_id(0) == 0)
        q, kp, kc, vp, vc = q_ref[...], kp_ref[...], kc_ref[...], vp_ref[...], vc_ref[...]
        outs = []
        for h in range(N_KV_HEADS):
            hs = slice(h * HEAD_DIM, (h + 1) * HEAD_DIM)
            k2 = jnp.concatenate([kp[:, hs], kc[:, hs]], axis=0)
            v2 = jnp.concatenate([vp[:, hs], vc[:, hs]], axis=0)
            sc = lax.dot_general(_stack_heads(q, h), k2, NT, preferred_element_type=F32) * ATTN_SCALE
            sc = jnp.where(mask, sc, NEG)
            sink = _sink_column(sk_ref, h)
            m = jnp.maximum(jnp.max(sc, axis=1, keepdims=True), sink)
            p = jnp.exp(sc - m)
            den = jnp.sum(p, axis=1, keepdims=True) + jnp.exp(sink - m)
            outs.append(jnp.dot(p.astype(BF16), v2, preferred_element_type=F32) / den)
        o_ref[...] = _unstack_heads(outs).astype(BF16)

    return _call(
        comm, body, name="attn_fwd", grid=(nb,),
        in_specs=[pl.BlockSpec(memory_space=pltpu.SMEM), *_attn_specs(nb)],
        out_specs=pl.BlockSpec((BLOCK, ATTN_W), lambda i: (i, 0)),
        out_shape=jax.ShapeDtypeStruct((s, ATTN_W), BF16),
        compiler_params=_params("parallel"),
    )(sinks, qkv, qkv, qkv, qkv, qkv)


def _mix_fwd(x, attn, c3, gates, conv_w, w_ab, w_cb, w_out, g2, tm, comm=None):
    s = x.shape[0]

    def body(x_ref, at_ref, c3_ref, gt_ref, cw_ref, wab_ref, wcb_ref, wo_ref, g_ref,
             conv_ref, a_ref, cv_ref, mg_ref, h1_ref, hn_ref, carry_ref):
        @pl.when(pl.program_id(0) == 0)
        def _():
            carry_ref[...] = jnp.zeros_like(carry_ref)

        c3v = c3_ref[...]
        cb, cc, cx = c3v[:, :CONV_W], c3v[:, CONV_W:2 * CONV_W], c3v[:, 2 * CONV_W:]
        z = cc * cx
        cz = _causal_conv(z, carry_ref[...], cw_ref[...])
        carry_ref[...] = z[tm - 8:tm]
        conv = (cb * cz).astype(BF16)
        conv_ref[...] = conv
        a = jnp.dot(at_ref[...], wab_ref[...], preferred_element_type=F32)
        cv = jnp.dot(conv, wcb_ref[...], preferred_element_type=F32)
        a_ref[...] = a.astype(BF16)
        cv_ref[...] = cv.astype(BF16)
        gt = gt_ref[...]
        merged = (_sigmoid(gt[:, :D_MODEL]) * a + _sigmoid(gt[:, D_MODEL:]) * cv).astype(BF16)
        mg_ref[...] = merged
        h1 = x_ref[...] + jnp.dot(merged, wo_ref[...], preferred_element_type=F32)
        h1_ref[...] = h1
        hn_ref[...] = (h1 * _rstd(h1) * g_ref[...]).astype(BF16)

    row = lambda w: pl.BlockSpec((tm, w), lambda i: (i, 0))
    return _call(
        comm, body, name="mix_fwd", grid=(s // tm,),
        in_specs=[row(D_MODEL), row(ATTN_W), row(C3_W), row(GATES_W), _resident((3, CONV_W)),
                  _resident((ATTN_W, D_MODEL)), _resident((CONV_W, D_MODEL)), _resident((D_MODEL, D_MODEL)),
                  _resident((1, D_MODEL))],
        out_specs=[row(CONV_W), row(D_MODEL), row(D_MODEL), row(D_MODEL), row(D_MODEL), row(D_MODEL)],
        out_shape=[jax.ShapeDtypeStruct((s, CONV_W), BF16), jax.ShapeDtypeStruct((s, D_MODEL), BF16),
                   jax.ShapeDtypeStruct((s, D_MODEL), BF16), jax.ShapeDtypeStruct((s, D_MODEL), BF16),
                   jax.ShapeDtypeStruct((s, D_MODEL), F32), jax.ShapeDtypeStruct((s, D_MODEL), BF16)],
        scratch_shapes=[pltpu.VMEM((8, CONV_W), F32)],
        compiler_params=_params("arbitrary"),
    )(x, attn, c3, gates, conv_w, w_ab, w_cb, w_out, g2)


def _ffn_fwd_loss(hn, h1, w_up, ffn_cw, w_down, g3, target, tm):
    s = hn.shape[0]

    def body(hn_ref, h1_ref, wu_ref, cw_ref, wd_ref, g_ref, t_ref,
             u_ref, up_ref, act_ref, dh2_ref, loss_ref, gfn_ref, carry_ref):
        @pl.when(pl.program_id(0) == 0)
        def _():
            carry_ref[...] = jnp.zeros_like(carry_ref)
            loss_ref[...] = jnp.zeros_like(loss_ref)
            gfn_ref[...] = jnp.zeros_like(gfn_ref)

        u = jnp.dot(hn_ref[...], wu_ref[...], preferred_element_type=F32)
        u_ref[...] = u.astype(BF16)
        up = _causal_conv(u, carry_ref[...], cw_ref[...])
        up_ref[...] = up.astype(BF16)
        carry_ref[...] = u[tm - 8:tm]
        gate, val = up[:, :D_FF], up[:, D_FF:]
        act = (gate * _sigmoid(gate) * val).astype(BF16)
        act_ref[...] = act
        h2 = h1_ref[...] + jnp.dot(act, wd_ref[...], preferred_element_type=F32)
        rstd = _rstd(h2)
        g = g_ref[...]
        err = h2 * rstd * g - t_ref[...]
        loss_ref[...] += jnp.sum(err * err) * (0.5 / D_MODEL)
        dh2, dg = _rms_bwd(err * (1.0 / D_MODEL), h2, rstd, g)
        dh2_ref[...] = dh2
        gfn_ref[...] += jnp.sum(dg, axis=0, keepdims=True)

    row = lambda w: pl.BlockSpec((tm, w), lambda i: (i, 0))
    acc = lambda w: pl.BlockSpec((1, w), lambda i: (0, 0))
    return pl.pallas_call(
        body, name="ffn_fwd_loss", grid=(s // tm,),
        in_specs=[row(D_MODEL), row(D_MODEL), _resident((D_MODEL, FF2)), _resident((3, FF2)),
                  _resident((D_FF, D_MODEL)), _resident((1, D_MODEL)), row(D_MODEL)],
        out_specs=[row(FF2), row(FF2), row(D_FF), row(D_MODEL), acc(128), acc(D_MODEL)],
        out_shape=[jax.ShapeDtypeStruct((s, FF2), BF16), jax.ShapeDtypeStruct((s, FF2), BF16),
                   jax.ShapeDtypeStruct((s, D_FF), BF16),
                   jax.ShapeDtypeStruct((s, D_MODEL), F32), jax.ShapeDtypeStruct((1, 128), F32),
                   jax.ShapeDtypeStruct((1, D_MODEL), F32)],
        scratch_shapes=[pltpu.VMEM((8, FF2), F32)],
        compiler_params=_params("arbitrary"),
    )(hn, h1, w_up, ffn_cw, w_down, g3, target)


def _ffn_bwd(dh2, u, up, h1, w_up, ffn_cw, w_down, g2, tm):
    s = dh2.shape[0]
    nt = s // tm

    def body(dh2_ref, u_ref, up_ref, h1_ref, wu_ref, cw_ref, wd_ref, g_ref,
             du_ref, dh1_ref, gcw_ref, gg_ref, carry_ref):
        @pl.when(pl.program_id(0) == 0)
        def _():
            carry_ref[...] = jnp.zeros_like(carry_ref)
            gcw_ref[...] = jnp.zeros_like(gcw_ref)
            gg_ref[...] = jnp.zeros_like(gg_ref)

        dh2v = dh2_ref[...]
        dact = lax.dot_general(dh2v.astype(BF16), wd_ref[...], NT, preferred_element_type=F32)
        upv = up_ref[...].astype(F32)
        gate, val = upv[:, :D_FF], upv[:, D_FF:]
        sg = _sigmoid(gate)
        dval = dact * (gate * sg)
        dgate = dact * val * (sg * (1.0 + gate * (1.0 - sg)))
        dup = jnp.concatenate([dgate, dval], axis=1)
        dup1, dup2 = _rows_after(dup, carry_ref[...])
        carry_ref[...] = dup[0:8]
        u = u_ref[...].astype(F32)
        gcw_ref[2:3, :] += jnp.sum(dup * u, axis=0, keepdims=True)
        gcw_ref[1:2, :] += jnp.sum(dup1 * u, axis=0, keepdims=True)
        gcw_ref[0:1, :] += jnp.sum(dup2 * u, axis=0, keepdims=True)
        cw = cw_ref[...]
        du = (cw[2:3] * dup + cw[1:2] * dup1 + cw[0:1] * dup2).astype(BF16)
        du_ref[...] = du
        dhn = lax.dot_general(du, wu_ref[...], NT, preferred_element_type=F32)
        h1v = h1_ref[...]
        dh1, dg = _rms_bwd(dhn, h1v, _rstd(h1v), g_ref[...])
        dh1_ref[...] = dh2v + dh1
        gg_ref[...] += jnp.sum(dg, axis=0, keepdims=True)

    row = lambda w: pl.BlockSpec((tm, w), lambda i: (nt - 1 - i, 0))
    return pl.pallas_call(
        body, name="ffn_bwd", grid=(nt,),
        in_specs=[row(D_MODEL), row(FF2), row(FF2),
                  row(D_MODEL), _resident((D_MODEL, FF2)), _resident((3, FF2)), _resident((D_FF, D_MODEL)),
                  _resident((1, D_MODEL))],
        out_specs=[row(FF2), row(D_MODEL), pl.BlockSpec((3, FF2), lambda i: (0, 0)),
                   pl.BlockSpec((1, D_MODEL), lambda i: (0, 0))],
        out_shape=[jax.ShapeDtypeStruct((s, FF2), BF16), jax.ShapeDtypeStruct((s, D_MODEL), F32),
                   jax.ShapeDtypeStruct((3, FF2), F32), jax.ShapeDtypeStruct((1, D_MODEL), F32)],
        scratch_shapes=[pltpu.VMEM((8, FF2), F32)],
        compiler_params=_params("arbitrary"),
    )(dh2, u, up, h1, w_up, ffn_cw, w_down, g2)


def _mix_bwd(dh1, gates, a, cv, c3, conv_w, w_ab, w_cb, w_out, tm, comm=None):
    s = dh1.shape[0]
    nt = s // tm
    halo = 8

    def body(dh1_ref, gt_ref, a_ref, cv_ref, c3_ref, ch_ref, cw_ref, wab_ref, wcb_ref, wo_ref,
             dat_ref, da_ref, dcv_ref, dc3_ref, dgt_ref, gcw_ref, carry_ref):
        i = pl.program_id(0)

        @pl.when(i == 0)
        def _():
            carry_ref[...] = jnp.zeros_like(carry_ref)
            gcw_ref[...] = jnp.zeros_like(gcw_ref)

        dm = lax.dot_general(dh1_ref[...].astype(BF16), wo_ref[...], NT, preferred_element_type=F32)
        gt = gt_ref[...]
        sa, sc = _sigmoid(gt[:, :D_MODEL]), _sigmoid(gt[:, D_MODEL:])
        da = (dm * sa).astype(BF16)
        dcv = (dm * sc).astype(BF16)
        da_ref[...] = da
        dcv_ref[...] = dcv
        dgt_ref[...] = jnp.concatenate(
            [dm * a_ref[...].astype(F32) * (sa * (1.0 - sa)), dm * cv_ref[...].astype(F32) * (sc * (1.0 - sc))],
            axis=1).astype(BF16)
        dat_ref[...] = lax.dot_general(da, wab_ref[...], NT, preferred_element_type=F32).astype(BF16)
        dconv = lax.dot_general(dcv, wcb_ref[...], NT, preferred_element_type=F32)
        c3v = c3_ref[...]
        cb, cc, cx = c3v[:, :CONV_W], c3v[:, CONV_W:2 * CONV_W], c3v[:, 2 * CONV_W:]
        z = cc * cx
        chv = ch_ref[...] * (i < nt - 1).astype(F32)
        zh = chv[:, CONV_W:2 * CONV_W] * chv[:, 2 * CONV_W:]
        cw = cw_ref[...]
        cz = _causal_conv(z, zh, cw)
        dcz = dconv * cb
        dcz1, dcz2 = _rows_after(dcz, carry_ref[...])
        carry_ref[...] = dcz[0:8]
        gcw_ref[2:3, :] += jnp.sum(dcz * z, axis=0, keepdims=True)
        gcw_ref[1:2, :] += jnp.sum(dcz1 * z, axis=0, keepdims=True)
        gcw_ref[0:1, :] += jnp.sum(dcz2 * z, axis=0, keepdims=True)
        dz = cw[2:3] * dcz + cw[1:2] * dcz1 + cw[0:1] * dcz2
        dc3_ref[...] = jnp.concatenate([dconv * cz, dz * cx, dz * cc], axis=1).astype(BF16)

    row = lambda w: pl.BlockSpec((tm, w), lambda i: (nt - 1 - i, 0))
    return _call(
        comm, body, name="mix_bwd", grid=(nt,),
        in_specs=[row(D_MODEL), row(GATES_W), row(D_MODEL), row(D_MODEL), row(C3_W),
                  pl.BlockSpec((halo, C3_W), lambda i: (jnp.maximum((nt - 1 - i) * (tm // halo) - 1, 0), 0)),
                  _resident((3, CONV_W)), _resident((ATTN_W, D_MODEL)), _resident((CONV_W, D_MODEL)),
                  _resident((D_MODEL, D_MODEL))],
        out_specs=[row(ATTN_W), row(D_MODEL), row(D_MODEL), row(C3_W), row(GATES_W),
                   pl.BlockSpec((3, CONV_W), lambda i: (0, 0))],
        out_shape=[jax.ShapeDtypeStruct((s, ATTN_W), BF16), jax.ShapeDtypeStruct((s, D_MODEL), BF16),
                   jax.ShapeDtypeStruct((s, D_MODEL), BF16), jax.ShapeDtypeStruct((s, C3_W), BF16),
                   jax.ShapeDtypeStruct((s, GATES_W), BF16), jax.ShapeDtypeStruct((3, CONV_W), F32)],
        scratch_shapes=[pltpu.VMEM((8, CONV_W), F32)],
        compiler_params=_params("arbitrary"),
    )(dh1, gates, a, cv, c3, c3, conv_w, w_ab, w_cb, w_out)


def _attn_bwd(qkv, sinks, o, do, comm=None):
    s = qkv.shape[0]
    nb = s // BLOCK

    def body(sk_ref, q_ref, kp_ref, kc_ref, vp_ref, vc_ref, o_ref, do_ref,
             dq_ref, dk_ref, dv_ref, dsk_ref, ck_ref, cvv_ref):
        i = pl.program_id(0)

        @pl.when(i == 0)
        def _():
            ck_ref[...] = jnp.zeros_like(ck_ref)
            cvv_ref[...] = jnp.zeros_like(cvv_ref)
            dsk_ref[...] = jnp.zeros_like(dsk_ref)

        @pl.when(i < nb)
        def _():
            mask = _attn_mask(i == 0)
            q, kp, kc, vp, vc = q_ref[...], kp_ref[...], kc_ref[...], vp_ref[...], vc_ref[...]
            ov, dov = o_ref[...], do_ref[...]
            dqs, dks, dvs = [], [], []
            for h in range(N_KV_HEADS):
                hs = slice(h * HEAD_DIM, (h + 1) * HEAD_DIM)
                k2 = jnp.concatenate([kp[:, hs], kc[:, hs]], axis=0)
                v2 = jnp.concatenate([vp[:, hs], vc[:, hs]], axis=0)
                qg, og, dog = _stack_heads(q, h), _stack_heads(ov, h), _stack_heads(dov, h)
                sc = lax.dot_general(qg, k2, NT, preferred_element_type=F32) * ATTN_SCALE
                sc = jnp.where(mask, sc, NEG)
                sink = _sink_column(sk_ref, h)
                m = jnp.maximum(jnp.max(sc, axis=1, keepdims=True), sink)
                p = jnp.exp(sc - m)
                psink = jnp.exp(sink - m)
                inv = 1.0 / (jnp.sum(p, axis=1, keepdims=True) + psink)
                p = p * inv
                delta = jnp.sum(dog.astype(F32) * og.astype(F32), axis=1, keepdims=True)
                dp = lax.dot_general(dog, v2, NT, preferred_element_type=F32)
                ds = (p * (dp - delta)).astype(BF16)
                dqs.append(jnp.dot(ds, k2, preferred_element_type=F32) * ATTN_SCALE)
                dks.append(lax.dot_general(ds, qg, TN, preferred_element_type=F32) * ATTN_SCALE)
                dvs.append(lax.dot_general(p.astype(BF16), dog, TN, preferred_element_type=F32))
                dsink = -(psink * inv * delta)
                for g in range(GROUP):
                    r = h * GROUP + g
                    dsk_ref[r:r + 1, :] += jnp.sum(dsink[g * BLOCK:(g + 1) * BLOCK])
            dq_ref[...] = _unstack_heads(dqs).astype(BF16)
            dk2 = jnp.concatenate(dks, axis=1)
            dv2 = jnp.concatenate(dvs, axis=1)
            dk_ref[...] = (ck_ref[...] + dk2[:BLOCK]).astype(BF16)
            dv_ref[...] = (cvv_ref[...] + dv2[:BLOCK]).astype(BF16)
            ck_ref[...] = dk2[BLOCK:]
            cvv_ref[...] = dv2[BLOCK:]

        @pl.when(i == nb)
        def _():
            dk_ref[...] = ck_ref[...].astype(BF16)
            dv_ref[...] = cvv_ref[...].astype(BF16)

    cur = lambda i: jnp.minimum(i, nb - 1)
    done = lambda i: jnp.maximum(i - 1, 0)
    return _call(
        comm, body, name="attn_bwd", grid=(nb + 1,),
        in_specs=[pl.BlockSpec(memory_space=pltpu.SMEM), *_attn_specs(nb),
                  pl.BlockSpec((BLOCK, ATTN_W), lambda i: (cur(i), 0)),
                  pl.BlockSpec((BLOCK, ATTN_W), lambda i: (cur(i), 0))],
        out_specs=[pl.BlockSpec((BLOCK, ATTN_W), lambda i: (cur(i), 0)),
                   pl.BlockSpec((BLOCK, KV_W), lambda i: (done(i), 0)),
                   pl.BlockSpec((BLOCK, KV_W), lambda i: (done(i), 0)),
                   pl.BlockSpec((N_HEADS, 128), lambda i: (0, 0))],
        out_shape=[jax.ShapeDtypeStruct((s, ATTN_W), BF16), jax.ShapeDtypeStruct((s, KV_W), BF16),
                   jax.ShapeDtypeStruct((s, KV_W), BF16), jax.ShapeDtypeStruct((N_HEADS, 128), F32)],
        scratch_shapes=[pltpu.VMEM((BLOCK, KV_W), F32), pltpu.VMEM((BLOCK, KV_W), F32)],
        compiler_params=_params("arbitrary"),
    )(sinks, qkv, qkv, qkv, qkv, qkv, o, do)


def _inproj_bwd(dq, dk, dv, dc3, dgt, w_in, x, dh1, g1, tm, comm=None):
    s = x.shape[0]

    def body(dq_ref, dk_ref, dv_ref, dc3_ref, dgt_ref, w_ref, x_ref, dh1_ref, g_ref,
             dx_ref, dp_ref, gb_ref, gg_ref):
        @pl.when(pl.program_id(0) == 0)
        def _():
            gb_ref[...] = jnp.zeros_like(gb_ref)
            gg_ref[...] = jnp.zeros_like(gg_ref)

        dp = jnp.concatenate([dq_ref[...], dk_ref[...], dv_ref[...], dc3_ref[...], dgt_ref[...]], axis=1)
        dp_ref[...] = dp
        gb_ref[...] += jnp.sum(dp.astype(F32), axis=0, keepdims=True)
        dxn = jnp.dot(dp, w_ref[...], preferred_element_type=F32)
        xf = x_ref[...]
        dx, dg = _rms_bwd(dxn, xf, _rstd(xf), g_ref[...])
        dx_ref[...] = dh1_ref[...] + dx
        gg_ref[...] += jnp.sum(dg, axis=0, keepdims=True)

    row = lambda w: pl.BlockSpec((tm, w), lambda i: (i, 0))
    acc = lambda w: pl.BlockSpec((1, w), lambda i: (0, 0))
    return _call(
        comm, body, name="inproj_bwd", grid=(s // tm,),
        in_specs=[row(ATTN_W), row(KV_W), row(KV_W), row(C3_W), row(GATES_W), _resident((IN_W, D_MODEL)),
                  row(D_MODEL), row(D_MODEL), _resident((1, D_MODEL))],
        out_specs=[row(D_MODEL), row(IN_W), acc(IN_W), acc(D_MODEL)],
        out_shape=[jax.ShapeDtypeStruct((s, D_MODEL), F32), jax.ShapeDtypeStruct((s, IN_W), BF16),
                   jax.ShapeDtypeStruct((1, IN_W), F32), jax.ShapeDtypeStruct((1, D_MODEL), F32)],
        compiler_params=_params("arbitrary"),
    )(dq, dk, dv, dc3, dgt, w_in, x, dh1, g1)


def _wgrad(a, b, bm, bn, bk, name, comm=None):
    s, m = a.shape
    n = b.shape[1]
    nk = s // bk

    def body(a_ref, b_ref, o_ref, acc_ref):
        k = pl.program_id(2)

        @pl.when(k == 0)
        def _():
            acc_ref[...] = jnp.zeros_like(acc_ref)

        acc_ref[...] += lax.dot_general(a_ref[...].astype(BF16), b_ref[...].astype(BF16), TN,
                                        preferred_element_type=F32)

        @pl.when(k == nk - 1)
        def _():
            o_ref[...] = acc_ref[...].astype(BF16)

    return _call(
        comm, body, name=name, grid=(m // bm, n // bn, nk),
        in_specs=[pl.BlockSpec((bk, bm), lambda i, j, k: (k, i)), pl.BlockSpec((bk, bn), lambda i, j, k: (k, j))],
        out_specs=pl.BlockSpec((bm, bn), lambda i, j, k: (i, j)),
        out_shape=jax.ShapeDtypeStruct((m, n), BF16),
        scratch_shapes=[pltpu.VMEM((bm, bn), F32)],
        compiler_params=_params("parallel", "parallel", "arbitrary"),
    )(a, b)


def _wgrad_in(xn, dproj, bk, comm=None):
    s = xn.shape[0]
    nk = s // bk

    def body(a_ref, b_ref, o_ref, acc_ref):
        k = pl.program_id(0)

        @pl.when(k == 0)
        def _():
            acc_ref[...] = jnp.zeros_like(acc_ref)

        acc_ref[...] += lax.dot_general(b_ref[...], a_ref[...], TN, preferred_element_type=F32)

        @pl.when(k == nk - 1)
        def _():
            o_ref[...] = acc_ref[...].astype(BF16)

    return _call(
        comm, body, name="wgrad_in", grid=(nk,),
        in_specs=[pl.BlockSpec((bk, D_MODEL), lambda k: (k, 0)), pl.BlockSpec((bk, IN_W), lambda k: (k, 0))],
        out_specs=_resident((IN_W, D_MODEL)),
        out_shape=jax.ShapeDtypeStruct((IN_W, D_MODEL), BF16),
        scratch_shapes=[pltpu.VMEM((IN_W, D_MODEL), F32)],
        compiler_params=_params("arbitrary"),
    )(xn, dproj)


class _Carry:
    def __init__(self, jobs, reads=None, bufs=None, fresh=None):
        self.jobs, self.reads, self.bufs, self.fresh = jobs, reads or {}, bufs or {}, fresh or {}
        self.out = {}


class _Job:
    def __init__(self, n_sems, plan):
        self.n_sems, self.plan = n_sems, plan


def _plan_all(jobs, hbm, send, recv):
    pos = _position()
    starts, waits, base = [], [], 0
    for job in jobs:
        s, w = job.plan(hbm, pos, send, recv, base)
        starts, waits, base = starts + s, waits + w, base + job.n_sems
    return starts, waits


def _call(comm, body, **kw):
    if comm is None:
        return pl.pallas_call(body, **kw)
    grid = kw["grid"]
    single = not isinstance(kw["out_shape"], (list, tuple))
    out_shape = [kw["out_shape"]] if single else list(kw["out_shape"])
    out_specs = [kw["out_specs"]] if single else list(kw["out_specs"])
    in_specs = list(kw["in_specs"])
    scratch = list(kw.get("scratch_shapes", ()))
    r_names, b_names, f_names = list(comm.reads), list(comm.bufs), list(comm.fresh)
    n_args, n_out, n_scr = len(in_specs), len(out_shape), len(scratch)
    n_sems = sum(j.n_sems for j in comm.jobs)

    def wrapped(*refs):
        k = n_args
        hbm = dict(zip(r_names, refs[k:k + len(r_names)]))
        k += len(r_names) + len(b_names)
        outs = refs[k:k + n_out]
        k += n_out
        hbm.update(zip(b_names + f_names, refs[k:k + len(b_names) + len(f_names)]))
        k += len(b_names) + len(f_names)
        send, recv = refs[k + n_scr:]
        starts, waits = _plan_all(comm.jobs, hbm, send, recv)
        ids = [pl.program_id(a) for a in range(len(grid))]
        first = functools.reduce(jnp.logical_and, [i == 0 for i in ids])
        last = functools.reduce(jnp.logical_and, [i == g - 1 for i, g in zip(ids, grid)])

        @pl.when(first)
        def _():
            for cp in starts:
                cp.start()

        body(*refs[:n_args], *outs, *refs[k:k + n_scr])

        @pl.when(last)
        def _():
            for cp in waits:
                cp.wait_recv()
            for cp in starts:
                cp.wait_send()

    sems = pltpu.SemaphoreType.DMA((n_sems,))
    held = [jax.ShapeDtypeStruct(a.shape, a.dtype) for a in comm.bufs.values()] + list(comm.fresh.values())
    call = pl.pallas_call(
        wrapped, name=kw["name"], grid=grid,
        in_specs=in_specs + [_ANY] * (len(r_names) + len(b_names)),
        out_specs=out_specs + [_ANY] * len(held),
        out_shape=out_shape + held,
        input_output_aliases={n_args + len(r_names) + i: n_out + i for i in range(len(b_names))},
        scratch_shapes=scratch + [sems, sems],
        compiler_params=_params(*["arbitrary"] * len(grid)),
    )

    def run(*args):
        res = call(*args, *comm.reads.values(), *comm.bufs.values())
        comm.out = dict(zip(b_names + f_names, res[n_out:]))
        return res[0] if single else res[:n_out]

    return run


def _exchange(name, phases, reads=None, bufs=None, fresh=None):
    comm = _Carry([j for ph in phases for j in ph], reads, bufs, fresh)
    r_names, b_names, f_names = list(comm.reads), list(comm.bufs), list(comm.fresh)
    n_sems = sum(j.n_sems for j in comm.jobs)

    def body(*refs):
        hbm = dict(zip(r_names, refs[:len(r_names)]))
        k = len(r_names) + len(b_names)
        hbm.update(zip(b_names + f_names, refs[k:k + len(b_names) + len(f_names)]))
        send, recv = refs[-2:]
        pos = _position()
        started, base = [], 0
        for ph in phases:
            waits = []
            for job in ph:
                s, w = job.plan(hbm, pos, send, recv, base)
                base += job.n_sems
                for cp in s:
                    cp.start()
                started, waits = started + s, waits + w
            for cp in waits:
                cp.wait_recv()
        for cp in started:
            cp.wait_send()

    sems = pltpu.SemaphoreType.DMA((n_sems,))
    held = [jax.ShapeDtypeStruct(a.shape, a.dtype) for a in comm.bufs.values()] + list(comm.fresh.values())
    res = pl.pallas_call(
        body, name=name, in_specs=[_ANY] * (len(r_names) + len(b_names)), out_specs=[_ANY] * len(held),
        out_shape=held, input_output_aliases={len(r_names) + i: i for i in range(len(b_names))},
        scratch_shapes=[sems, sems],
    )(*comm.reads.values(), *comm.bufs.values())
    return dict(zip(b_names + f_names, res))


_HBM = pl.BlockSpec(memory_space=pltpu.HBM)
_SEM = pl.BlockSpec(memory_space=pltpu.SEMAPHORE)
_EFFECT = pltpu.SideEffectType.DATAFLOW_SIDE_EFFECTING


def _start_exchanges(name, groups):
    names = [list(arrays) for _, arrays in groups]
    first = [sum(len(ns) for ns in names[:g]) for g in range(len(groups))]
    n, ng = sum(len(ns) for ns in names), len(groups)

    def body(*refs):
        for g, (jobs, _) in enumerate(groups):
            hbm = dict(zip(names[g], refs[first[g]:first[g] + len(names[g])]))
            for cp in _plan_all(jobs, hbm, refs[n + 2 * g], refs[n + 2 * g + 1])[0]:
                cp.start()
        refs[-1][...] = jnp.zeros_like(refs[-1])

    given = [pltpu.with_memory_space_constraint(
        a if isinstance(a, jax.Array) else lax.empty(a.shape, a.dtype), pltpu.HBM)
        for _, arrays in groups for a in arrays.values()]
    sems = [pltpu.SemaphoreType.DMA((sum(j.n_sems for j in jobs),)) for jobs, _ in groups for _ in range(2)]
    res = pl.pallas_call(
        body, name=name,
        out_shape=(*sems, *[pltpu.HBM(a.shape, a.dtype) for a in given], jax.ShapeDtypeStruct((8, 128), F32)),
        in_specs=[_HBM] * n, out_specs=(*[_SEM] * (2 * ng), *[_HBM] * n, pl.BlockSpec(memory_space=pltpu.VMEM)),
        input_output_aliases={i: 2 * ng + i for i in range(n)},
        compiler_params=pltpu.CompilerParams(has_side_effects=_EFFECT),
    )(*given)
    held = res[2 * ng:2 * ng + n]
    states = [(names[g], groups[g][0], res[2 * g], res[2 * g + 1], held[first[g]:first[g] + len(names[g])])
              for g in range(ng)]
    return states, res[-1]


def _start_exchange(name, jobs, arrays):
    states, token = _start_exchanges(name, [(jobs, arrays)])
    return states[0], token


def _finish_exchange(name, state, after):
    names, jobs, send_sem, recv_sem, held = state
    n = len(names)

    def body(*refs):
        hbm = dict(zip(names, refs[:n]))
        send, recv = refs[n:n + 2]
        starts, waits = _plan_all(jobs, hbm, send, recv)
        for cp in waits:
            cp.wait_recv()
        for cp in starts:
            cp.wait_send()

    res = pl.pallas_call(
        body, name=name, out_shape=tuple(pltpu.HBM(a.shape, a.dtype) for a in held),
        in_specs=[_HBM] * n + [_SEM, _SEM, _ANY], out_specs=tuple([_HBM] * n),
        input_output_aliases={i: i for i in range(n)},
        compiler_params=pltpu.CompilerParams(has_side_effects=_EFFECT),
    )(*held, send_sem, recv_sem, after)
    return dict(zip(names, res))


def _row_tile(rows, bytes_per_row):
    best = 16
    for t in range(16, rows + 1, 16):
        if rows % t == 0 and t * bytes_per_row <= 6 * 1024 * 1024:
            best = t
    return best


def _rowwise(fn, ins, out_dtypes, name, after=None):
    rows, cols = ins[0].shape
    per_row = sum(cols * a.dtype.itemsize for a in ins) + sum(cols * jnp.dtype(d).itemsize for d in out_dtypes)
    tr = _row_tile(rows, per_row)
    n_in = len(ins)

    def body(*refs):
        outs = fn(*[r[...] for r in refs[:n_in]])
        for o_ref, o in zip(refs[-len(out_dtypes):], outs):
            o_ref[...] = o.astype(o_ref.dtype)

    tile = pl.BlockSpec((tr, cols), lambda i: (i, 0))
    behind = [] if after is None else [after]
    return pl.pallas_call(
        body, name=name, grid=(rows // tr,),
        in_specs=[tile] * n_in + [pl.BlockSpec((8, 128), lambda i: (0, 0))] * len(behind),
        out_specs=[tile] * len(out_dtypes),
        out_shape=[jax.ShapeDtypeStruct((rows, cols), d) for d in out_dtypes],
        compiler_params=_params("parallel"),
    )(*ins, *behind)


def _tiled(fn, name, grid, pos, ins, outs):
    n_in = len(ins)

    def body(pos_ref, *refs):
        res = fn(*[r[...] for r in refs[:n_in]])
        for o_ref, o in zip(refs[n_in:], res):
            o_ref[...] = o.astype(o_ref.dtype)

    return pl.pallas_call(
        body, name=name,
        grid_spec=pltpu.PrefetchScalarGridSpec(
            num_scalar_prefetch=1, grid=grid,
            in_specs=[pl.BlockSpec(bs, im) for _, bs, im in ins],
            out_specs=[pl.BlockSpec(bs, im) for _, _, bs, im in outs]),
        out_shape=[jax.ShapeDtypeStruct(s, d) for s, d, _, _ in outs],
        compiler_params=_params("parallel"),
    )(pos, *[a for a, _, _ in ins])


def _adamw(w, g, m, v):
    m = ADAM_B1 * m + (1.0 - ADAM_B1) * g
    v = ADAM_B2 * v + (1.0 - ADAM_B2) * (g * g)
    m_hat = m / (1.0 - ADAM_B1 ** ADAM_STEP)
    v_hat = v / (1.0 - ADAM_B2 ** ADAM_STEP)
    return -ADAM_LR * (m_hat / (jnp.sqrt(v_hat) + ADAM_EPS) + ADAM_WD * w), m, v


def _adamw_small(params):
    n = len(params)

    def body(*refs):
        for k in range(n):
            w, g, m, v = (r[...] for r in refs[4 * k:4 * k + 4])
            for o_ref, o in zip(refs[4 * n + 3 * k:4 * n + 3 * k + 3], _adamw(w, g, m, v)):
                o_ref[...] = o

    flat = [a for p in params for a in p]
    return pl.pallas_call(
        body, name="adamw_small",
        out_shape=[jax.ShapeDtypeStruct(p[0].shape, F32) for p in params for _ in range(3)],
    )(*flat)


class _Layout:
    def __init__(self, rows, cols, stacked):
        self.rows, self.cols, self.stacked = rows, cols, stacked

    def whole(self, rows=None):
        r = self.rows if rows is None else rows
        return (N_CHIPS, r, self.cols) if self.stacked else (r, N_CHIPS * self.cols)

    def part_rows(self, h, q=0, nq=1):
        n = self.rows // 2 // nq
        return pl.ds(pl.multiple_of(h * (self.rows // 2) + q * n, 16), n)

    def half_rows(self, h):
        return self.part_rows(h)

    def block(self, ref, p, rows=slice(None)):
        if self.stacked:
            return ref.at[p, rows, :]
        return ref.at[rows, pl.ds(pl.multiple_of(p * self.cols, 128), self.cols)]

    def all_chips(self, ref, rows):
        return ref.at[:, rows, :] if self.stacked else ref.at[rows, :]


BIG = (
    _Layout(IN_SHARD, D_MODEL, True),
    _Layout(ATTN_W, D_MODEL // N_CHIPS, False),
    _Layout(CONV_W, D_MODEL // N_CHIPS, False),
    _Layout(D_MODEL // N_CHIPS, D_MODEL, True),
    _Layout(D_MODEL, FF2 // N_CHIPS, False),
    _Layout(D_FF // N_CHIPS, D_MODEL, True),
)
N_BIG = len(BIG)
_ANY = pl.BlockSpec(memory_space=pl.ANY)


def _position():
    x, y, c = lax.axis_index("x"), lax.axis_index("y"), lax.axis_index("c")
    return x, y, c, 2 * x + y


def _core_of_chip(p, c):
    return (p >> 1, p & 1, c)


def _place_cast(shard, lay, pos, name, after=None):
    rows, cols = shard.shape
    tr = _row_tile(rows, cols * 6)
    if lay.stacked:
        out = (lay.whole(), BF16, (None, tr, cols), lambda i, pos: (pos[0], i, 0))
    else:
        out = (lay.whole(), BF16, (tr, cols), lambda i, pos: (i, pos[0]))
    ins = [(shard, (tr, cols), lambda i, pos: (i, 0))]
    if after is not None:
        ins.append((after, (8, 128), lambda i, pos: (0, 0)))
    return _tiled(lambda a, *_: (a,), name, (rows // tr,), pos, ins, [out])[0]


def _remote(src, dst, send, recv, k, device):
    return pltpu.make_async_remote_copy(src_ref=src, dst_ref=dst, send_sem=send.at[k], recv_sem=recv.at[k],
                                        device_id=device, device_id_type=MESH)


def _arrival(dst, send, recv, k, me):
    return _remote(dst, dst, send, recv, k, me)


def _gather_ici(lay, name, q=0, nq=1):
    def plan(hbm, pos, send, recv, base):
        x, y, c, me = pos
        rows = lay.part_rows(c, q, nq)
        mine = lay.block(hbm[name], me, rows)
        starts = [_remote(mine, mine, send, recv, base + d - 1, _core_of_chip(me ^ d, c)) for d in (1, 2, 3)]
        waits = [_arrival(lay.block(hbm[name], me ^ d, rows), send, recv, base + d - 1, (x, y, c)) for d in (1, 2, 3)]
        return starts, waits
    return _Job(3, plan)


def _gather_d2d(lay, name, q=0, nq=1):
    def plan(hbm, pos, send, recv, base):
        x, y, c, me = pos
        starts, waits = [], []
        for d in (1, 2, 3):
            got = lay.block(hbm[name], me ^ d, lay.part_rows(c, q, nq))
            starts.append(_remote(got, got, send, recv, base + d - 1, (x, y, 1 - c)))
            waits.append(_arrival(lay.block(hbm[name], me ^ d, lay.part_rows(1 - c, q, nq)), send, recv, base + d - 1,
                                  (x, y, c)))
        return starts, waits
    return _Job(3, plan)


def _rs_pair(lay, grad, theirs):
    def plan(hbm, pos, send, recv, base):
        x, y, c, _ = pos
        out = _remote(lay.all_chips(hbm[grad], lay.half_rows(1 - c)), hbm[theirs], send, recv, base, (x, y, 1 - c))
        return [out], [_arrival(hbm[theirs], send, recv, base, (x, y, c))]
    return _Job(1, plan)


def _rs_chips(lay, sums, slots):
    def plan(hbm, pos, send, recv, base):
        x, y, c, me = pos
        starts = [_remote(lay.block(hbm[sums], me ^ d), hbm[slots].at[me], send, recv, base + d - 1,
                          _core_of_chip(me ^ d, c)) for d in (1, 2, 3)]
        waits = [_arrival(hbm[slots].at[me ^ d], send, recv, base + d - 1, (x, y, c)) for d in (1, 2, 3)]
        return starts, waits
    return _Job(3, plan)


def _rs_share(lay, shard):
    def plan(hbm, pos, send, recv, base):
        x, y, c, _ = pos
        mine = hbm[shard].at[lay.half_rows(c), :]
        other = hbm[shard].at[lay.half_rows(1 - c), :]
        return [_remote(mine, mine, send, recv, base, (x, y, 1 - c))], [_arrival(other, send, recv, base, (x, y, c))]
    return _Job(1, plan)


def _slots_shape(lay):
    return jax.ShapeDtypeStruct((N_CHIPS, lay.rows // 2, lay.cols), BF16)


def _theirs_shape(lay):
    return jax.ShapeDtypeStruct(lay.whole(lay.rows // 2), BF16)


def _pair_sum(grad, theirs, lay, pos, name):
    half = lay.rows // 2
    add = lambda a, b: (a.astype(F32) + b.astype(F32),)
    if lay.stacked:
        tr = _row_tile(half, lay.cols * 6)
        nt = half // tr
        flat = lambda a: a.reshape(-1, lay.cols)
        mine = lambda t, pos: ((t // nt) * (2 * nt) + pos[1] * nt + t % nt, 0)
        grid, blk = (N_CHIPS * nt,), (tr, lay.cols)
        grad, theirs = flat(grad), flat(theirs)
    else:
        tr = _row_tile(half, N_CHIPS * lay.cols * 6)
        nt = half // tr
        mine = lambda t, pos: (pos[1] * nt + t, 0)
        grid, blk = (nt,), (tr, N_CHIPS * lay.cols)
    same = lambda t, pos: (t, 0)
    out = _tiled(add, name, grid, pos, [(grad, blk, mine), (theirs, blk, same)], [(theirs.shape, BF16, blk, same)])[0]
    return out.reshape(lay.whole(half))


def _chip_sum(sums, slots, lay, pos, name, after=None):
    half = lay.rows // 2
    tr = _row_tile(half, lay.cols * 12)
    nt = half // tr
    blk3 = (None, tr, lay.cols)
    if lay.stacked:
        own = (sums, blk3, lambda i, pos: (pos[0], i, 0))
    else:
        own = (sums, (tr, lay.cols), lambda i, pos: (i, pos[0]))
    others = [(slots, blk3, functools.partial(lambda d, i, pos: (pos[0] ^ d, i, 0), d)) for d in (1, 2, 3)]

    def add(a, b1, b2, b3, *_):
        return (((a.astype(F32) + b1.astype(F32)) + b2.astype(F32)) + b3.astype(F32),)

    if after is not None:
        others.append((after, (8, 128), lambda i, pos: (0, 0)))
    return _tiled(add, name, (nt,), pos, [own] + others,
                  [((lay.rows, lay.cols), F32, (tr, lay.cols), lambda i, pos: (pos[1] * nt + i, 0))])[0]


N_DEV = 8


def _to_all(src, slots):
    def plan(hbm, pos, send, recv, base):
        x, y, c, _ = pos
        idx = 4 * x + 2 * y + c
        starts = [_remote(hbm[src], hbm[slots].at[idx], send, recv, base + k - 1,
                          (x ^ (k >> 2), y ^ ((k >> 1) & 1), c ^ (k & 1))) for k in range(1, N_DEV)]
        waits = [_arrival(hbm[slots].at[idx ^ k], send, recv, base + k - 1, (x, y, c)) for k in range(1, N_DEV)]
        return starts, waits
    return _Job(N_DEV - 1, plan)


def _sum_slots(own, slots, pos):
    def body(pos_ref, own_ref, slots_ref, o_ref):
        idx = 2 * pos_ref[0] + pos_ref[1]
        term = lambda q: jnp.where(idx == q, own_ref[...], slots_ref[q])
        acc = term(0)
        for q in range(1, N_DEV):
            acc = acc + term(q)
        o_ref[...] = acc

    return pl.pallas_call(
        body, name="sum_small", out_shape=jax.ShapeDtypeStruct(own.shape, F32),
        in_specs=[pl.BlockSpec(memory_space=pltpu.SMEM), pl.BlockSpec(memory_space=pltpu.VMEM),
                  pl.BlockSpec(memory_space=pltpu.VMEM)],
    )(pos, own, slots)


def _exchange_small(v):
    rows = v.shape[0]

    def body(v_ref, slots, send, recv):
        x, y, c = lax.axis_index("x"), lax.axis_index("y"), lax.axis_index("c")
        idx = 4 * x + 2 * y + c
        slots[idx] = v_ref[...]

        def to_peer(k):
            return pltpu.make_async_remote_copy(
                src_ref=v_ref, dst_ref=slots.at[idx], send_sem=send.at[k - 1], recv_sem=recv.at[k - 1],
                device_id=(x ^ (k >> 2), y ^ ((k >> 1) & 1), c ^ (k & 1)), device_id_type=MESH)

        def from_peer(k):
            return pltpu.make_async_remote_copy(
                src_ref=v_ref, dst_ref=slots.at[idx ^ k], send_sem=send.at[k - 1], recv_sem=recv.at[k - 1],
                device_id=(x, y, c), device_id_type=MESH)

        for k in range(1, N_DEV):
            to_peer(k).start()
        for k in range(1, N_DEV):
            from_peer(k).wait_recv()
        for k in range(1, N_DEV):
            to_peer(k).wait_send()

    sems = pltpu.SemaphoreType.DMA((N_DEV - 1,))
    return pl.pallas_call(
        body, name="allgather_small", out_shape=jax.ShapeDtypeStruct((N_DEV, rows, 128), F32),
        scratch_shapes=[sems, sems],
    )(v)


def _pack_rows(parts):
    padded = [jnp.pad(a, ((0, -a.shape[0] % 8), (0, 0))) for a in parts]
    starts = [sum(p.shape[0] for p in padded[:k]) for k in range(len(padded))]
    return jnp.concatenate(padded, axis=0), starts


def kernel(x, mix_norm, w_in, b_in, sinks, conv_w, w_attn_branch, w_conv_branch, w_out, ffn_norm, w_up, ffn_conv_w, w_down, final_norm, loss_target, m_mix_norm, m_w_in, m_b_in, m_sinks, m_conv_w, m_w_attn_branch, m_w_conv_branch, m_w_out, m_ffn_norm, m_w_up, m_ffn_conv_w, m_w_down, m_final_norm, v_mix_norm, v_w_in, v_b_in, v_sinks, v_conv_w, v_w_attn_branch, v_w_conv_branch, v_w_out, v_ffn_norm, v_w_up, v_ffn_conv_w, v_w_down, v_final_norm):
    me = 2 * lax.axis_index("x") + lax.axis_index("y")
    big_w = [w_in[0].T, w_attn_branch[0], w_conv_branch[0], w_out[0], w_up[0], w_down[0]]
    big_m = [m_w_in[0].T, m_w_attn_branch[0], m_w_conv_branch[0], m_w_out[0], m_w_up[0], m_w_down[0]]
    big_v = [v_w_in[0].T, v_w_attn_branch[0], v_w_conv_branch[0], v_w_out[0], v_w_up[0], v_w_down[0]]
    names = ("w_in", "w_ab", "w_cb", "w_out", "w_up", "w_down")

    pos = jnp.stack([me, lax.axis_index("c")]).astype(jnp.int32)

    lay = dict(zip(names, BIG))
    xs, target, sk = x[0], loss_target[0], sinks[0]
    s = xs.shape[0]
    tm, tm2, bk = min(256, s), min(512, s), min(1024, s)

    taps, (_, t0) = _pack_rows([conv_w[0], ffn_conv_w[0].reshape(3 * (FF2 // N_CHIPS // 128), 128)])
    taps = _exchange_small(taps)
    conv_full = taps[0::2, 0:3].transpose(1, 0, 2).reshape(3, CONV_W)
    ffn_cw_full = taps[0::2, t0:t0 + 33].reshape(N_CHIPS, 3, FF2 // N_CHIPS).transpose(1, 0, 2).reshape(3, FF2)
    placed = {"w_in": _place_cast(big_w[0], lay["w_in"], pos, "cast_w_in", after=taps[0, 0:8])}
    fly_in, started = _start_exchange("gather_in_start", [_gather_ici(lay["w_in"], "w_in")], {"w_in": placed["w_in"]})
    for w, n in zip(big_w[1:], names[1:]):
        placed[n] = _place_cast(w, lay[n], pos, "cast_" + n, after=started)
    trio = ("w_ab", "w_cb", "w_out")
    (fly_trio, fly_down, fly_up), started = _start_exchanges("gather_rest_start", [
        ([_gather_ici(lay[n], n) for n in ws], {n: placed[n] for n in ws}) for ws in (trio, ("w_down",), ("w_up",))])

    got = _finish_exchange("gather_in_wait", fly_in, after=started)
    w_in_full = _exchange("gather_in_d2d", [[_gather_d2d(lay["w_in"], "w_in")]], bufs=got)["w_in"].reshape(IN_W, D_MODEL)
    xn, qkv, c3, gates = _inproj_fwd(xs, mix_norm, w_in_full, b_in, tm2)
    k2 = _Carry([_gather_d2d(lay[n], n) for n in trio], bufs=_finish_exchange("gather_trio_wait", fly_trio, after=qkv))
    attn = _attn_fwd(qkv, sk, comm=k2)
    w_ab, w_cb = k2.out["w_ab"], k2.out["w_cb"]
    w_out_full = k2.out["w_out"].reshape(D_MODEL, D_MODEL)
    k3 = _Carry([_gather_d2d(lay["w_down"], "w_down")], bufs=_finish_exchange("gather_down_wait", fly_down, after=attn))
    conv, a, cv, merged, h1, hn = _mix_fwd(xs, attn, c3, gates, conv_full, w_ab, w_cb, w_out_full, ffn_norm, tm, comm=k3)
    w_down_full = k3.out["w_down"].reshape(D_FF, D_MODEL)
    w_up_full = _exchange("gather_up_d2d", [[_gather_d2d(lay["w_up"], "w_up")]],
                          bufs=_finish_exchange("gather_up_wait", fly_up, after=hn))["w_up"]
    u, up, act, dh2, loss_part, g_fn = _ffn_fwd_loss(hn, h1, w_up_full, ffn_cw_full, w_down_full,
                                                     final_norm[None, :], target, tm)

    grads, sums, slots = {}, {}, {}

    def pair(*ws):
        return _Carry([_rs_pair(lay[n], "g_" + n, "t_" + n) for n in ws], reads={"g_" + n: grads[n] for n in ws},
                      fresh={"t_" + n: _theirs_shape(lay[n]) for n in ws})

    def chips(*ws, also=None):
        k = _Carry([_rs_chips(lay[n], "s_" + n, "r_" + n) for n in ws], reads={"s_" + n: sums[n] for n in ws},
                   fresh={"r_" + n: _slots_shape(lay[n]) for n in ws})
        if also is not None:
            k = _Carry(k.jobs + also.jobs, {**k.reads, **also.reads}, None, {**k.fresh, **also.fresh})
        return k

    def pair_sums(k, *ws):
        for n in ws:
            sums[n] = _pair_sum(grads[n], k.out["t_" + n], lay[n], pos, "pair_sum_" + n)

    def take_slots(k, *ws):
        for n in ws:
            slots[n] = k.out["r_" + n]

    du, dh1, g_fcw, g_g2 = _ffn_bwd(dh2, u, up, h1, w_up_full, ffn_cw_full, w_down_full, ffn_norm, tm)
    grads["w_down"] = _wgrad(act, dh2, D_FF // 2, D_MODEL, bk, "wgrad_down").reshape(lay["w_down"].whole())
    k4 = pair("w_down")
    grads["w_up"] = _wgrad(hn, du, D_MODEL, FF2 // 4, bk, "wgrad_up", comm=k4)
    pair_sums(k4, "w_down")
    k5 = chips("w_down", also=pair("w_up"))
    dattn, da, dcv, dc3, dgt, g_cw = _mix_bwd(dh1, gates, a, cv, c3, conv_full, w_ab, w_cb, w_out_full, tm, comm=k5)
    take_slots(k5, "w_down")
    pair_sums(k5, "w_up")
    grads["w_out"] = _wgrad(merged, dh1, D_MODEL, D_MODEL, bk, "wgrad_out").reshape(lay["w_out"].whole())
    grads["w_ab"] = _wgrad(attn, da, ATTN_W, D_MODEL, bk, "wgrad_ab")
    grads["w_cb"] = _wgrad(conv, dcv, CONV_W, D_MODEL, bk, "wgrad_cb")
    k6 = chips("w_up", also=pair("w_out", "w_ab", "w_cb"))
    dq, dk, dv, g_sk = _attn_bwd(qkv, sk, attn, dattn, comm=k6)
    take_slots(k6, "w_up")
    pair_sums(k6, "w_out", "w_ab", "w_cb")
    grad_x, dproj, g_b, g_g1 = _inproj_bwd(dq, dk, dv, dc3, dgt, w_in_full, xs, dh1, mix_norm, tm2)

    parts = [loss_part, g_g1, g_b, jnp.pad(g_sk[:, 0], (0, 120))[None, :], g_cw, g_g2, g_fcw, g_fn]
    packed, at = _pack_rows([p.reshape(-1, 128) for p in parts])
    small_flight, started = _start_exchange("small_start", [_to_all("v", "slots")],
                                            {"v": packed, "slots": jnp.zeros((N_DEV, *packed.shape), F32)})
    k8 = chips("w_out", "w_ab", "w_cb")
    k8.reads["after"] = started
    grads["w_in"] = _wgrad_in(xn, dproj, min(512, s), comm=k8).reshape(lay["w_in"].whole())
    take_slots(k8, "w_out", "w_ab", "w_cb")
    others = names[1:]
    in_flight, started = _start_exchange("rs_pair_in_start", [_rs_pair(lay["w_in"], "g", "t")],
                                         {"g": grads["w_in"], "t": _theirs_shape(lay["w_in"])})
    halves = {n: _chip_sum(sums[n], slots[n], lay[n], pos, "chip_sum_" + n, after=started) for n in others}
    landed = _finish_exchange("rs_pair_in_wait", in_flight, after=halves["w_down"])
    sums["w_in"] = _pair_sum(landed["g"], landed["t"], lay["w_in"], pos, "pair_sum_w_in")
    in_flight, started = _start_exchange("rs_chips_in_start", [_rs_chips(lay["w_in"], "s", "r")],
                                         {"s": sums["w_in"], "r": _slots_shape(lay["w_in"])})
    shared = _exchange("share_halves", [[_rs_share(lay[n], n) for n in others]], reads={"after": started}, bufs=halves)
    w_of, m_of, v_of = dict(zip(names, big_w)), dict(zip(names, big_m)), dict(zip(names, big_v))
    adam = lambda n, g, after=None: _rowwise(_adamw, [w_of[n], g, m_of[n], v_of[n]], [F32, F32, F32], "adamw_" + n,
                                             after=after)
    new_of, last = {}, None
    for n in ("w_up", "w_down", "w_out", "w_ab", "w_cb"):
        new_of[n] = adam(n, shared[n], last)
        last = new_of[n][0]

    arrived = _finish_exchange("small_wait", small_flight, after=last)
    total = _sum_slots(arrived["v"], arrived["slots"], pos)
    part = lambda k: total[at[k]:at[k] + parts[k].size // 128].reshape(parts[k].shape)
    loss = total[0, 0]
    g_mix, g_b, g_g2, g_fn = part(1), part(2), part(5), part(7)
    g_sk = part(3)[:, 0:N_HEADS]
    g_cw = lax.dynamic_slice(part(4), (0, me * 128), (3, 128))
    g_fcw = lax.dynamic_slice(part(6), (0, me * (FF2 // N_CHIPS)), (3, FF2 // N_CHIPS))
    small_p = [
        (mix_norm, g_mix, m_mix_norm, v_mix_norm), (b_in, g_b, m_b_in, v_b_in), (sinks, g_sk, m_sinks, v_sinks),
        (conv_w[0], g_cw, m_conv_w[0], v_conv_w[0]), (ffn_norm, g_g2, m_ffn_norm, v_ffn_norm),
        (ffn_conv_w[0], g_fcw, m_ffn_conv_w[0], v_ffn_conv_w[0]),
        (final_norm[None, :], g_fn, m_final_norm[None, :], v_final_norm[None, :])]
    small_new = _adamw_small(small_p)
    small_new = [small_new[3 * k:3 * k + 3] for k in range(len(small_p))]

    landed = _finish_exchange("rs_chips_in_wait", in_flight, after=small_new[0][0])
    half_in = _chip_sum(landed["s"], landed["r"], lay["w_in"], pos, "chip_sum_w_in")
    shared["w_in"] = _exchange("share_in", [[_rs_share(lay["w_in"], "w_in")]], bufs={"w_in": half_in})["w_in"]
    new_of["w_in"] = adam("w_in", shared["w_in"])
    big_g = [shared[n] for n in names]
    big_new = [new_of[n] for n in names]

    order = [("s", 0), ("b", 0), ("s", 1), ("s", 2), ("s", 3), ("b", 1), ("b", 2), ("b", 3), ("s", 4), ("b", 4),
             ("s", 5), ("b", 5), ("s", 6)]
    shapes = [mix_norm.shape, w_in.shape, b_in.shape, sinks.shape, conv_w.shape, w_attn_branch.shape,
              w_conv_branch.shape, w_out.shape, ffn_norm.shape, w_up.shape, ffn_conv_w.shape, w_down.shape,
              final_norm.shape]
    small_g = [p[1] for p in small_p]
    big_g[0] = big_g[0].T
    big_new[0] = [a.T for a in big_new[0]]
    out_g = [(small_g[k] if kind == "s" else big_g[k]).reshape(shp) for (kind, k), shp in zip(order, shapes)]
    news = [[(small_new[k][j] if kind == "s" else big_new[k][j]).reshape(shp) for (kind, k), shp in zip(order, shapes)]
            for j in range(3)]
    return (loss, grad_x[None], *out_g, *news[0], *news[1], *news[2])
```

```python
import functools

import jax
import jax.numpy as jnp
from jax import lax
from jax.experimental import pallas as pl
from jax.experimental.pallas import tpu as pltpu

F32 = jnp.float32
BF16 = jnp.bfloat16

D_MODEL = 1024
HEAD_DIM = 64
N_HEADS = 8
N_KV_HEADS = 2
GROUP = N_HEADS // N_KV_HEADS
BLOCK = 128
ATTN_SCALE = HEAD_DIM ** -0.5
ATTN_W = N_HEADS * HEAD_DIM
KV_W = N_KV_HEADS * HEAD_DIM
CONV_W = 512
QKV_W = ATTN_W + 2 * KV_W
C3_W = 3 * CONV_W
GATES_W = 2 * D_MODEL
IN_W = QKV_W + C3_W + GATES_W
D_FF = 2816
FF2 = 2 * D_FF
NORM_EPS = 1e-5
N_CHIPS = 4
IN_SHARD = IN_W // N_CHIPS
NEG = -1e30

ADAM_LR = 0.001
ADAM_B1 = 0.9
ADAM_B2 = 0.999
ADAM_EPS = 1e-08
ADAM_WD = 0.01
ADAM_STEP = 10

VMEM_LIMIT = 56 * 1024 * 1024
MESH = pl.DeviceIdType.MESH

NT = (((1,), (1,)), ((), ()))
TN = (((0,), (0,)), ((), ()))


def _params(*sem):
    return pltpu.CompilerParams(dimension_semantics=sem, vmem_limit_bytes=VMEM_LIMIT)


def _resident(shape):
    return pl.BlockSpec(shape, lambda *_: (0,) * len(shape), pipeline_mode=pl.Buffered(1))


def _sigmoid(v):
    return 0.5 * jnp.tanh(0.5 * v) + 0.5


def _rstd(v):
    return lax.rsqrt(jnp.mean(v * v, axis=-1, keepdims=True) + NORM_EPS)


def _rms_bwd(dy, v, rstd, g):
    vhat = v * rstd
    t = dy * g
    return rstd * (t - vhat * jnp.mean(t * vhat, axis=-1, keepdims=True)), dy * vhat


def _taps(z, cw):
    return cw[2:3] * z + cw[1:2] * pltpu.roll(z, 1, 0) + cw[0:1] * pltpu.roll(z, 2, 0)


def _causal_conv(z, prev, cw):
    edge = _taps(jnp.concatenate([prev, z[0:8]], axis=0), cw)
    return jnp.concatenate([edge[8:16], _taps(z, cw)[8:]], axis=0)


def _rows_after(z, nxt):
    n = z.shape[0]
    edge = jnp.concatenate([z[n - 8:n], nxt], axis=0)
    return tuple(jnp.concatenate([pltpu.roll(z, n - k, 0)[:n - 8], pltpu.roll(edge, 16 - k, 0)[0:8]], axis=0)
                 for k in (1, 2))


def _inproj_fwd(x, g1, w_in, b_in, tm, comm=None):
    s = x.shape[0]

    def body(x_ref, g_ref, w_ref, b_ref, xn_ref, qkv_ref, c3_ref, gt_ref):
        xf = x_ref[...]
        xn = (xf * _rstd(xf) * g_ref[...]).astype(BF16)
        xn_ref[...] = xn

        def seg(a, b):
            return lax.dot_general(xn, w_ref[a:b, :], NT, preferred_element_type=F32) + b_ref[:, a:b]

        qkv_ref[...] = seg(0, QKV_W).astype(BF16)
        c3_ref[...] = seg(QKV_W, QKV_W + C3_W)
        gt_ref[...] = seg(QKV_W + C3_W, IN_W)

    row = lambda w: pl.BlockSpec((tm, w), lambda i: (i, 0))
    return _call(
        comm, body, name="inproj_fwd", grid=(s // tm,),
        in_specs=[row(D_MODEL), _resident((1, D_MODEL)), _resident((IN_W, D_MODEL)), _resident((1, IN_W))],
        out_specs=[row(D_MODEL), row(QKV_W), row(C3_W), row(GATES_W)],
        out_shape=[jax.ShapeDtypeStruct((s, D_MODEL), BF16), jax.ShapeDtypeStruct((s, QKV_W), BF16),
                   jax.ShapeDtypeStruct((s, C3_W), F32), jax.ShapeDtypeStruct((s, GATES_W), F32)],
        compiler_params=_params("parallel"),
    )(x, g1, w_in, b_in)


def _attn_mask(first_block):
    qi = lax.broadcasted_iota(jnp.int32, (GROUP * BLOCK, 2 * BLOCK), 0) & (BLOCK - 1)
    kj = lax.broadcasted_iota(jnp.int32, (GROUP * BLOCK, 2 * BLOCK), 1)
    band = (kj > qi) & (kj <= qi + BLOCK)
    return band & ((kj >= BLOCK) | jnp.logical_not(first_block))


def _sink_column(sk_ref, h):
    rows = lax.broadcasted_iota(jnp.int32, (GROUP * BLOCK, 1), 0)
    col = jnp.full((GROUP * BLOCK, 1), sk_ref[h * GROUP], F32)
    for g in range(1, GROUP):
        col = jnp.where(rows >= g * BLOCK, sk_ref[h * GROUP + g], col)
    return col


def _stack_heads(t, h):
    return jnp.concatenate(
        [t[:, (h * GROUP + g) * HEAD_DIM:(h * GROUP + g + 1) * HEAD_DIM] for g in range(GROUP)], axis=0)


def _unstack_heads(per_kv):
    return jnp.concatenate(
        [t[g * BLOCK:(g + 1) * BLOCK] for t in per_kv for g in range(GROUP)], axis=1)


def _attn_specs(nb):
    cur = lambda i: jnp.minimum(i, nb - 1)
    prev = lambda i: jnp.maximum(jnp.minimum(i, nb - 1) - 1, 0)
    q = pl.BlockSpec((BLOCK, ATTN_W), lambda i: (cur(i), 0))
    kp = pl.BlockSpec((BLOCK, KV_W), lambda i: (prev(i), ATTN_W // KV_W))
    kc = pl.BlockSpec((BLOCK, KV_W), lambda i: (cur(i), ATTN_W // KV_W))
    vp = pl.BlockSpec((BLOCK, KV_W), lambda i: (prev(i), ATTN_W // KV_W + 1))
    vc = pl.BlockSpec((BLOCK, KV_W), lambda i: (cur(i), ATTN_W // KV_W + 1))
    return q, kp, kc, vp, vc


def _attn_fwd(qkv, sinks, comm=None):
    s = qkv.shape[0]
    nb = s // BLOCK

    def body(sk_ref, q_ref, kp_ref, kc_ref, vp_ref, vc_ref, o_ref):
        mask = _attn_mask(pl.program_id(0) == 0)
        q, kp, kc, vp, vc = q_ref[...], kp_ref[...], kc_ref[...], vp_ref[...], vc_ref[...]
        outs = []
        for h in range(N_KV_HEADS):
            hs = slice(h * HEAD_DIM, (h + 1) * HEAD_DIM)
            k2 = jnp.concatenate([kp[:, hs], kc[:, hs]], axis=0)
            v2 = jnp.concatenate([vp[:, hs], vc[:, hs]], axis=0)
            sc = lax.dot_general(_stack_heads(q, h), k2, NT, preferred_element_type=F32) * ATTN_SCALE
            sc = jnp.where(mask, sc, NEG)
            sink = _sink_column(sk_ref, h)
            m = jnp.maximum(jnp.max(sc, axis=1, keepdims=True), sink)
            p = jnp.exp(sc - m)
            den = jnp.sum(p, axis=1, keepdims=True) + jnp.exp(sink - m)
            outs.append(jnp.dot(p.astype(BF16), v2, preferred_element_type=F32) / den)
        o_ref[...] = _unstack_heads(outs).astype(BF16)

    return _call(
        comm, body, name="attn_fwd", grid=(nb,),
        in_specs=[pl.BlockSpec(memory_space=pltpu.SMEM), *_attn_specs(nb)],
        out_specs=pl.BlockSpec((BLOCK, ATTN_W), lambda i: (i, 0)),
        out_shape=jax.ShapeDtypeStruct((s, ATTN_W), BF16),
        compiler_params=_params("parallel"),
    )(sinks, qkv, qkv, qkv, qkv, qkv)


def _mix_fwd(x, attn, c3, gates, conv_w, w_ab, w_cb, w_out, g2, tm, comm=None):
    s = x.shape[0]

    def body(x_ref, at_ref, c3_ref, gt_ref, cw_ref, wab_ref, wcb_ref, wo_ref, g_ref,
             conv_ref, a_ref, cv_ref, mg_ref, h1_ref, hn_ref, carry_ref):
        @pl.when(pl.program_id(0) == 0)
        def _():
            carry_ref[...] = jnp.zeros_like(carry_ref)

        c3v = c3_ref[...]
        cb, cc, cx = c3v[:, :CONV_W], c3v[:, CONV_W:2 * CONV_W], c3v[:, 2 * CONV_W:]
        z = cc * cx
        cz = _causal_conv(z, carry_ref[...], cw_ref[...])
        carry_ref[...] = z[tm - 8:tm]
        conv = (cb * cz).astype(BF16)
        conv_ref[...] = conv
        a = jnp.dot(at_ref[...], wab_ref[...], preferred_element_type=F32)
        cv = jnp.dot(conv, wcb_ref[...], preferred_element_type=F32)
        a_ref[...] = a.astype(BF16)
        cv_ref[...] = cv.astype(BF16)
        gt = gt_ref[...]
        merged = (_sigmoid(gt[:, :D_MODEL]) * a + _sigmoid(gt[:, D_MODEL:]) * cv).astype(BF16)
        mg_ref[...] = merged
        h1 = x_ref[...] + jnp.dot(merged, wo_ref[...], preferred_element_type=F32)
        h1_ref[...] = h1
        hn_ref[...] = (h1 * _rstd(h1) * g_ref[...]).astype(BF16)

    row = lambda w: pl.BlockSpec((tm, w), lambda i: (i, 0))
    return _call(
        comm, body, name="mix_fwd", grid=(s // tm,),
        in_specs=[row(D_MODEL), row(ATTN_W), row(C3_W), row(GATES_W), _resident((3, CONV_W)),
                  _resident((ATTN_W, D_MODEL)), _resident((CONV_W, D_MODEL)), _resident((D_MODEL, D_MODEL)),
                  _resident((1, D_MODEL))],
        out_specs=[row(CONV_W), row(D_MODEL), row(D_MODEL), row(D_MODEL), row(D_MODEL), row(D_MODEL)],
        out_shape=[jax.ShapeDtypeStruct((s, CONV_W), BF16), jax.ShapeDtypeStruct((s, D_MODEL), BF16),
                   jax.ShapeDtypeStruct((s, D_MODEL), BF16), jax.ShapeDtypeStruct((s, D_MODEL), BF16),
                   jax.ShapeDtypeStruct((s, D_MODEL), F32), jax.ShapeDtypeStruct((s, D_MODEL), BF16)],
        scratch_shapes=[pltpu.VMEM((8, CONV_W), F32)],
        compiler_params=_params("arbitrary"),
    )(x, attn, c3, gates, conv_w, w_ab, w_cb, w_out, g2)


def _ffn_fwd_loss(hn, h1, w_up, ffn_cw, w_down, g3, target, tm):
    s = hn.shape[0]

    def body(hn_ref, h1_ref, wu_ref, cw_ref, wd_ref, g_ref, t_ref,
             u_ref, up_ref, act_ref, dh2_ref, loss_ref, gfn_ref, carry_ref):
        @pl.when(pl.program_id(0) == 0)
        def _():
            carry_ref[...] = jnp.zeros_like(carry_ref)
            loss_ref[...] = jnp.zeros_like(loss_ref)
            gfn_ref[...] = jnp.zeros_like(gfn_ref)

        u = jnp.dot(hn_ref[...], wu_ref[...], preferred_element_type=F32)
        u_ref[...] = u.astype(BF16)
        up = _causal_conv(u, carry_ref[...], cw_ref[...])
        up_ref[...] = up.astype(BF16)
        carry_ref[...] = u[tm - 8:tm]
        gate, val = up[:, :D_FF], up[:, D_FF:]
        act = (gate * _sigmoid(gate) * val).astype(BF16)
        act_ref[...] = act
        h2 = h1_ref[...] + jnp.dot(act, wd_ref[...], preferred_element_type=F32)
        rstd = _rstd(h2)
        g = g_ref[...]
        err = h2 * rstd * g - t_ref[...]
        loss_ref[...] += jnp.sum(err * err) * (0.5 / D_MODEL)
        dh2, dg = _rms_bwd(err * (1.0 / D_MODEL), h2, rstd, g)
        dh2_ref[...] = dh2
        gfn_ref[...] += jnp.sum(dg, axis=0, keepdims=True)

    row = lambda w: pl.BlockSpec((tm, w), lambda i: (i, 0))
    acc = lambda w: pl.BlockSpec((1, w), lambda i: (0, 0))
    return pl.pallas_call(
        body, name="ffn_fwd_loss", grid=(s // tm,),
        in_specs=[row(D_MODEL), row(D_MODEL), _resident((D_MODEL, FF2)), _resident((3, FF2)),
                  _resident((D_FF, D_MODEL)), _resident((1, D_MODEL)), row(D_MODEL)],
        out_specs=[row(FF2), row(FF2), row(D_FF), row(D_MODEL), acc(128), acc(D_MODEL)],
        out_shape=[jax.ShapeDtypeStruct((s, FF2), BF16), jax.ShapeDtypeStruct((s, FF2), BF16),
                   jax.ShapeDtypeStruct((s, D_FF), BF16),
                   jax.ShapeDtypeStruct((s, D_MODEL), F32), jax.ShapeDtypeStruct((1, 128), F32),
                   jax.ShapeDtypeStruct((1, D_MODEL), F32)],
        scratch_shapes=[pltpu.VMEM((8, FF2), F32)],
        compiler_params=_params("arbitrary"),
    )(hn, h1, w_up, ffn_cw, w_down, g3, target)


def _ffn_bwd(dh2, u, up, h1, w_up, ffn_cw, w_down, g2, tm):
    s = dh2.shape[0]
    nt = s // tm

    def body(dh2_ref, u_ref, up_ref, h1_ref, wu_ref, cw_ref, wd_ref, g_ref,
             du_ref, dh1_ref, gcw_ref, gg_ref, carry_ref):
        @pl.when(pl.program_id(0) == 0)
        def _():
            carry_ref[...] = jnp.zeros_like(carry_ref)
            gcw_ref[...] = jnp.zeros_like(gcw_ref)
            gg_ref[...] = jnp.zeros_like(gg_ref)

        dh2v = dh2_ref[...]
        dact = lax.dot_general(dh2v.astype(BF16), wd_ref[...], NT, preferred_element_type=F32)
        upv = up_ref[...].astype(F32)
        gate, val = upv[:, :D_FF], upv[:, D_FF:]
        sg = _sigmoid(gate)
        dval = dact * (gate * sg)
        dgate = dact * val * (sg * (1.0 + gate * (1.0 - sg)))
        dup = jnp.concatenate([dgate, dval], axis=1)
        dup1, dup2 = _rows_after(dup, carry_ref[...])
        carry_ref[...] = dup[0:8]
        u = u_ref[...].astype(F32)
        gcw_ref[2:3, :] += jnp.sum(dup * u, axis=0, keepdims=True)
        gcw_ref[1:2, :] += jnp.sum(dup1 * u, axis=0, keepdims=True)
        gcw_ref[0:1, :] += jnp.sum(dup2 * u, axis=0, keepdims=True)
        cw = cw_ref[...]
        du = (cw[2:3] * dup + cw[1:2] * dup1 + cw[0:1] * dup2).astype(BF16)
        du_ref[...] = du
        dhn = lax.dot_general(du, wu_ref[...], NT, preferred_element_type=F32)
        h1v = h1_ref[...]
        dh1, dg = _rms_bwd(dhn, h1v, _rstd(h1v), g_ref[...])
        dh1_ref[...] = dh2v + dh1
        gg_ref[...] += jnp.sum(dg, axis=0, keepdims=True)

    row = lambda w: pl.BlockSpec((tm, w), lambda i: (nt - 1 - i, 0))
    return pl.pallas_call(
        body, name="ffn_bwd", grid=(nt,),
        in_specs=[row(D_MODEL), row(FF2), row(FF2),
                  row(D_MODEL), _resident((D_MODEL, FF2)), _resident((3, FF2)), _resident((D_FF, D_MODEL)),
                  _resident((1, D_MODEL))],
        out_specs=[row(FF2), row(D_MODEL), pl.BlockSpec((3, FF2), lambda i: (0, 0)),
                   pl.BlockSpec((1, D_MODEL), lambda i: (0, 0))],
        out_shape=[jax.ShapeDtypeStruct((s, FF2), BF16), jax.ShapeDtypeStruct((s, D_MODEL), F32),
                   jax.ShapeDtypeStruct((3, FF2), F32), jax.ShapeDtypeStruct((1, D_MODEL), F32)],
        scratch_shapes=[pltpu.VMEM((8, FF2), F32)],
        compiler_params=_params("arbitrary"),
    )(dh2, u, up, h1, w_up, ffn_cw, w_down, g2)


def _mix_bwd(dh1, gates, a, cv, c3, conv_w, w_ab, w_cb, w_out, tm, comm=None):
    s = dh1.shape[0]
    nt = s // tm
    halo = 8

    def body(dh1_ref, gt_ref, a_ref, cv_ref, c3_ref, ch_ref, cw_ref, wab_ref, wcb_ref, wo_ref,
             dat_ref, da_ref, dcv_ref, dc3_ref, dgt_ref, gcw_ref, carry_ref):
        i = pl.program_id(0)

        @pl.when(i == 0)
        def _():
            carry_ref[...] = jnp.zeros_like(carry_ref)
            gcw_ref[...] = jnp.zeros_like(gcw_ref)

        dm = lax.dot_general(dh1_ref[...].astype(BF16), wo_ref[...], NT, preferred_element_type=F32)
        gt = gt_ref[...]
        sa, sc = _sigmoid(gt[:, :D_MODEL]), _sigmoid(gt[:, D_MODEL:])
        da = (dm * sa).astype(BF16)
        dcv = (dm * sc).astype(BF16)
        da_ref[...] = da
        dcv_ref[...] = dcv
        dgt_ref[...] = jnp.concatenate(
            [dm * a_ref[...].astype(F32) * (sa * (1.0 - sa)), dm * cv_ref[...].astype(F32) * (sc * (1.0 - sc))],
            axis=1).astype(BF16)
        dat_ref[...] = lax.dot_general(da, wab_ref[...], NT, preferred_element_type=F32).astype(BF16)
        dconv = lax.dot_general(dcv, wcb_ref[...], NT, preferred_element_type=F32)
        c3v = c3_ref[...]
        cb, cc, cx = c3v[:, :CONV_W], c3v[:, CONV_W:2 * CONV_W], c3v[:, 2 * CONV_W:]
        z = cc * cx
        chv = ch_ref[...] * (i < nt - 1).astype(F32)
        zh = chv[:, CONV_W:2 * CONV_W] * chv[:, 2 * CONV_W:]
        cw = cw_ref[...]
        cz = _causal_conv(z, zh, cw)
        dcz = dconv * cb
        dcz1, dcz2 = _rows_after(dcz, carry_ref[...])
        carry_ref[...] = dcz[0:8]
        gcw_ref[2:3, :] += jnp.sum(dcz * z, axis=0, keepdims=True)
        gcw_ref[1:2, :] += jnp.sum(dcz1 * z, axis=0, keepdims=True)
        gcw_ref[0:1, :] += jnp.sum(dcz2 * z, axis=0, keepdims=True)
        dz = cw[2:3] * dcz + cw[1:2] * dcz1 + cw[0:1] * dcz2
        dc3_ref[...] = jnp.concatenate([dconv * cz, dz * cx, dz * cc], axis=1).astype(BF16)

    row = lambda w: pl.BlockSpec((tm, w), lambda i: (nt - 1 - i, 0))
    return _call(
        comm, body, name="mix_bwd", grid=(nt,),
        in_specs=[row(D_MODEL), row(GATES_W), row(D_MODEL), row(D_MODEL), row(C3_W),
                  pl.BlockSpec((halo, C3_W), lambda i: (jnp.maximum((nt - 1 - i) * (tm // halo) - 1, 0), 0)),
                  _resident((3, CONV_W)), _resident((ATTN_W, D_MODEL)), _resident((CONV_W, D_MODEL)),
                  _resident((D_MODEL, D_MODEL))],
        out_specs=[row(ATTN_W), row(D_MODEL), row(D_MODEL), row(C3_W), row(GATES_W),
                   pl.BlockSpec((3, CONV_W), lambda i: (0, 0))],
        out_shape=[jax.ShapeDtypeStruct((s, ATTN_W), BF16), jax.ShapeDtypeStruct((s, D_MODEL), BF16),
                   jax.ShapeDtypeStruct((s, D_MODEL), BF16), jax.ShapeDtypeStruct((s, C3_W), BF16),
                   jax.ShapeDtypeStruct((s, GATES_W), BF16), jax.ShapeDtypeStruct((3, CONV_W), F32)],
        scratch_shapes=[pltpu.VMEM((8, CONV_W), F32)],
        compiler_params=_params("arbitrary"),
    )(dh1, gates, a, cv, c3, c3, conv_w, w_ab, w_cb, w_out)


def _attn_bwd(qkv, sinks, o, do, comm=None):
    s = qkv.shape[0]
    nb = s // BLOCK

    def body(sk_ref, q_ref, kp_ref, kc_ref, vp_ref, vc_ref, o_ref, do_ref,
             dq_ref, dk_ref, dv_ref, dsk_ref, ck_ref, cvv_ref):
        i = pl.program_id(0)

        @pl.when(i == 0)
        def _():
            ck_ref[...] = jnp.zeros_like(ck_ref)
            cvv_ref[...] = jnp.zeros_like(cvv_ref)
            dsk_ref[...] = jnp.zeros_like(dsk_ref)

        @pl.when(i < nb)
        def _():
            mask = _attn_mask(i == 0)
            q, kp, kc, vp, vc = q_ref[...], kp_ref[...], kc_ref[...], vp_ref[...], vc_ref[...]
            ov, dov = o_ref[...], do_ref[...]
            dqs, dks, dvs = [], [], []
            for h in range(N_KV_HEADS):
                hs = slice(h * HEAD_DIM, (h + 1) * HEAD_DIM)
                k2 = jnp.concatenate([kp[:, hs], kc[:, hs]], axis=0)
                v2 = jnp.concatenate([vp[:, hs], vc[:, hs]], axis=0)
                qg, og, dog = _stack_heads(q, h), _stack_heads(ov, h), _stack_heads(dov, h)
                sc = lax.dot_general(qg, k2, NT, preferred_element_type=F32) * ATTN_SCALE
                sc = jnp.where(mask, sc, NEG)
                sink = _sink_column(sk_ref, h)
                m = jnp.maximum(jnp.max(sc, axis=1, keepdims=True), sink)
                p = jnp.exp(sc - m)
                psink = jnp.exp(sink - m)
                inv = 1.0 / (jnp.sum(p, axis=1, keepdims=True) + psink)
                p = p * inv
                delta = jnp.sum(dog.astype(F32) * og.astype(F32), axis=1, keepdims=True)
                dp = lax.dot_general(dog, v2, NT, preferred_element_type=F32)
                ds = (p * (dp - delta)).astype(BF16)
                dqs.append(jnp.dot(ds, k2, preferred_element_type=F32) * ATTN_SCALE)
                dks.append(lax.dot_general(ds, qg, TN, preferred_element_type=F32) * ATTN_SCALE)
                dvs.append(lax.dot_general(p.astype(BF16), dog, TN, preferred_element_type=F32))
                dsink = -(psink * inv * delta)
                for g in range(GROUP):
                    r = h * GROUP + g
                    dsk_ref[r:r + 1, :] += jnp.sum(dsink[g * BLOCK:(g + 1) * BLOCK])
            dq_ref[...] = _unstack_heads(dqs).astype(BF16)
            dk2 = jnp.concatenate(dks, axis=1)
            dv2 = jnp.concatenate(dvs, axis=1)
            dk_ref[...] = (ck_ref[...] + dk2[:BLOCK]).astype(BF16)
            dv_ref[...] = (cvv_ref[...] + dv2[:BLOCK]).astype(BF16)
            ck_ref[...] = dk2[BLOCK:]
            cvv_ref[...] = dv2[BLOCK:]

        @pl.when(i == nb)
        def _():
            dk_ref[...] = ck_ref[...].astype(BF16)
            dv_ref[...] = cvv_ref[...].astype(BF16)

    cur = lambda i: jnp.minimum(i, nb - 1)
    done = lambda i: jnp.maximum(i - 1, 0)
    return _call(
        comm, body, name="attn_bwd", grid=(nb + 1,),
        in_specs=[pl.BlockSpec(memory_space=pltpu.SMEM), *_attn_specs(nb),
                  pl.BlockSpec((BLOCK, ATTN_W), lambda i: (cur(i), 0)),
                  pl.BlockSpec((BLOCK, ATTN_W), lambda i: (cur(i), 0))],
        out_specs=[pl.BlockSpec((BLOCK, ATTN_W), lambda i: (cur(i), 0)),
                   pl.BlockSpec((BLOCK, KV_W), lambda i: (done(i), 0)),
                   pl.BlockSpec((BLOCK, KV_W), lambda i: (done(i), 0)),
                   pl.BlockSpec((N_HEADS, 128), lambda i: (0, 0))],
        out_shape=[jax.ShapeDtypeStruct((s, ATTN_W), BF16), jax.ShapeDtypeStruct((s, KV_W), BF16),
                   jax.ShapeDtypeStruct((s, KV_W), BF16), jax.ShapeDtypeStruct((N_HEADS, 128), F32)],
        scratch_shapes=[pltpu.VMEM((BLOCK, KV_W), F32), pltpu.VMEM((BLOCK, KV_W), F32)],
        compiler_params=_params("arbitrary"),
    )(sinks, qkv, qkv, qkv, qkv, qkv, o, do)


def _inproj_bwd(dq, dk, dv, dc3, dgt, w_in, x, dh1, g1, tm, comm=None):
    s = x.shape[0]

    def body(dq_ref, dk_ref, dv_ref, dc3_ref, dgt_ref, w_ref, x_ref, dh1_ref, g_ref,
             dx_ref, dp_ref, gb_ref, gg_ref):
        @pl.when(pl.program_id(0) == 0)
        def _():
            gb_ref[...] = jnp.zeros_like(gb_ref)
            gg_ref[...] = jnp.zeros_like(gg_ref)

        dp = jnp.concatenate([dq_ref[...], dk_ref[...], dv_ref[...], dc3_ref[...], dgt_ref[...]], axis=1)
        dp_ref[...] = dp
        gb_ref[...] += jnp.sum(dp.astype(F32), axis=0, keepdims=True)
        dxn = jnp.dot(dp, w_ref[...], preferred_element_type=F32)
        xf = x_ref[...]
        dx, dg = _rms_bwd(dxn, xf, _rstd(xf), g_ref[...])
        dx_ref[...] = dh1_ref[...] + dx
        gg_ref[...] += jnp.sum(dg, axis=0, keepdims=True)

    row = lambda w: pl.BlockSpec((tm, w), lambda i: (i, 0))
    acc = lambda w: pl.BlockSpec((1, w), lambda i: (0, 0))
    return _call(
        comm, body, name="inproj_bwd", grid=(s // tm,),
        in_specs=[row(ATTN_W), row(KV_W), row(KV_W), row(C3_W), row(GATES_W), _resident((IN_W, D_MODEL)),
                  row(D_MODEL), row(D_MODEL), _resident((1, D_MODEL))],
        out_specs=[row(D_MODEL), row(IN_W), acc(IN_W), acc(D_MODEL)],
        out_shape=[jax.ShapeDtypeStruct((s, D_MODEL), F32), jax.ShapeDtypeStruct((s, IN_W), BF16),
                   jax.ShapeDtypeStruct((1, IN_W), F32), jax.ShapeDtypeStruct((1, D_MODEL), F32)],
        compiler_params=_params("arbitrary"),
    )(dq, dk, dv, dc3, dgt, w_in, x, dh1, g1)


def _wgrad(a, b, bm, bn, bk, name, comm=None):
    s, m = a.shape
    n = b.shape[1]
    nk = s // bk

    def body(a_ref, b_ref, o_ref, acc_ref):
        k = pl.program_id(2)

        @pl.when(k == 0)
        def _():
            acc_ref[...] = jnp.zeros_like(acc_ref)

        acc_ref[...] += lax.dot_general(a_ref[...].astype(BF16), b_ref[...].astype(BF16), TN,
                                        preferred_element_type=F32)

        @pl.when(k == nk - 1)
        def _():
            o_ref[...] = acc_ref[...].astype(BF16)

    return _call(
        comm, body, name=name, grid=(m // bm, n // bn, nk),
        in_specs=[pl.BlockSpec((bk, bm), lambda i, j, k: (k, i)), pl.BlockSpec((bk, bn), lambda i, j, k: (k, j))],
        out_specs=pl.BlockSpec((bm, bn), lambda i, j, k: (i, j)),
        out_shape=jax.ShapeDtypeStruct((m, n), BF16),
        scratch_shapes=[pltpu.VMEM((bm, bn), F32)],
        compiler_params=_params("parallel", "parallel", "arbitrary"),
    )(a, b)


def _wgrad_in(xn, dproj, bk, comm=None):
    s = xn.shape[0]
    nk = s // bk

    def body(a_ref, b_ref, o_ref, acc_ref):
        k = pl.program_id(0)

        @pl.when(k == 0)
        def _():
            acc_ref[...] = jnp.zeros_like(acc_ref)

        acc_ref[...] += lax.dot_general(b_ref[...], a_ref[...], TN, preferred_element_type=F32)

        @pl.when(k == nk - 1)
        def _():
            o_ref[...] = acc_ref[...].astype(BF16)

    return _call(
        comm, body, name="wgrad_in", grid=(nk,),
        in_specs=[pl.BlockSpec((bk, D_MODEL), lambda k: (k, 0)), pl.BlockSpec((bk, IN_W), lambda k: (k, 0))],
        out_specs=_resident((IN_W, D_MODEL)),
        out_shape=jax.ShapeDtypeStruct((IN_W, D_MODEL), BF16),
        scratch_shapes=[pltpu.VMEM((IN_W, D_MODEL), F32)],
        compiler_params=_params("arbitrary"),
    )(xn, dproj)


class _Carry:
    def __init__(self, jobs, reads=None, bufs=None, fresh=None):
        self.jobs, self.reads, self.bufs, self.fresh = jobs, reads or {}, bufs or {}, fresh or {}
        self.out = {}


class _Job:
    def __init__(self, n_sems, plan):
        self.n_sems, self.plan = n_sems, plan


def _plan_all(jobs, hbm, send, recv):
    pos = _position()
    starts, waits, base = [], [], 0
    for job in jobs:
        s, w = job.plan(hbm, pos, send, recv, base)
        starts, waits, base = starts + s, waits + w, base + job.n_sems
    return starts, waits


def _call(comm, body, **kw):
    if comm is None:
        return pl.pallas_call(body, **kw)
    grid = kw["grid"]
    single = not isinstance(kw["out_shape"], (list, tuple))
    out_shape = [kw["out_shape"]] if single else list(kw["out_shape"])
    out_specs = [kw["out_specs"]] if single else list(kw["out_specs"])
    in_specs = list(kw["in_specs"])
    scratch = list(kw.get("scratch_shapes", ()))
    r_names, b_names, f_names = list(comm.reads), list(comm.bufs), list(comm.fresh)
    n_args, n_out, n_scr = len(in_specs), len(out_shape), len(scratch)
    n_sems = sum(j.n_sems for j in comm.jobs)

    def wrapped(*refs):
        k = n_args
        hbm = dict(zip(r_names, refs[k:k + len(r_names)]))
        k += len(r_names) + len(b_names)
        outs = refs[k:k + n_out]
        k += n_out
        hbm.update(zip(b_names + f_names, refs[k:k + len(b_names) + len(f_names)]))
        k += len(b_names) + len(f_names)
        send, recv = refs[k + n_scr:]
        starts, waits = _plan_all(comm.jobs, hbm, send, recv)
        ids = [pl.program_id(a) for a in range(len(grid))]
        first = functools.reduce(jnp.logical_and, [i == 0 for i in ids])
        last = functools.reduce(jnp.logical_and, [i == g - 1 for i, g in zip(ids, grid)])

        @pl.when(first)
        def _():
            for cp in starts:
                cp.start()

        body(*refs[:n_args], *outs, *refs[k:k + n_scr])

        @pl.when(last)
        def _():
            for cp in waits:
                cp.wait_recv()
            for cp in starts:
                cp.wait_send()

    sems = pltpu.SemaphoreType.DMA((n_sems,))
    held = [jax.ShapeDtypeStruct(a.shape, a.dtype) for a in comm.bufs.values()] + list(comm.fresh.values())
    call = pl.pallas_call(
        wrapped, name=kw["name"], grid=grid,
        in_specs=in_specs + [_ANY] * (len(r_names) + len(b_names)),
        out_specs=out_specs + [_ANY] * len(held),
        out_shape=out_shape + held,
        input_output_aliases={n_args + len(r_names) + i: n_out + i for i in range(len(b_names))},
        scratch_shapes=scratch + [sems, sems],
        compiler_params=_params(*["arbitrary"] * len(grid)),
    )

    def run(*args):
        res = call(*args, *comm.reads.values(), *comm.bufs.values())
        comm.out = dict(zip(b_names + f_names, res[n_out:]))
        return res[0] if single else res[:n_out]

    return run


def _exchange(name, phases, reads=None, bufs=None, fresh=None):
    comm = _Carry([j for ph in phases for j in ph], reads, bufs, fresh)
    r_names, b_names, f_names = list(comm.reads), list(comm.bufs), list(comm.fresh)
    n_sems = sum(j.n_sems for j in comm.jobs)

    def body(*refs):
        hbm = dict(zip(r_names, refs[:len(r_names)]))
        k = len(r_names) + len(b_names)
        hbm.update(zip(b_names + f_names, refs[k:k + len(b_names) + len(f_names)]))
        send, recv = refs[-2:]
        pos = _position()
        started, base = [], 0
        for ph in phases:
            waits = []
            for job in ph:
                s, w = job.plan(hbm, pos, send, recv, base)
                base += job.n_sems
                for cp in s:
                    cp.start()
                started, waits = started + s, waits + w
            for cp in waits:
                cp.wait_recv()
        for cp in started:
            cp.wait_send()

    sems = pltpu.SemaphoreType.DMA((n_sems,))
    held = [jax.ShapeDtypeStruct(a.shape, a.dtype) for a in comm.bufs.values()] + list(comm.fresh.values())
    res = pl.pallas_call(
        body, name=name, in_specs=[_ANY] * (len(r_names) + len(b_names)), out_specs=[_ANY] * len(held),
        out_shape=held, input_output_aliases={len(r_names) + i: i for i in range(len(b_names))},
        scratch_shapes=[sems, sems],
    )(*comm.reads.values(), *comm.bufs.values())
    return dict(zip(b_names + f_names, res))


_HBM = pl.BlockSpec(memory_space=pltpu.HBM)
_SEM = pl.BlockSpec(memory_space=pltpu.SEMAPHORE)
_EFFECT = pltpu.SideEffectType.DATAFLOW_SIDE_EFFECTING


def _start_exchanges(name, groups):
    names = [list(arrays) for _, arrays in groups]
    first = [sum(len(ns) for ns in names[:g]) for g in range(len(groups))]
    n, ng = sum(len(ns) for ns in names), len(groups)

    def body(*refs):
        for g, (jobs, _) in enumerate(groups):
            hbm = dict(zip(names[g], refs[first[g]:first[g] + len(names[g])]))
            for cp in _plan_all(jobs, hbm, refs[n + 2 * g], refs[n + 2 * g + 1])[0]:
                cp.start()
        refs[-1][...] = jnp.zeros_like(refs[-1])

    given = [pltpu.with_memory_space_constraint(
        a if isinstance(a, jax.Array) else lax.empty(a.shape, a.dtype), pltpu.HBM)
        for _, arrays in groups for a in arrays.values()]
    sems = [pltpu.SemaphoreType.DMA((sum(j.n_sems for j in jobs),)) for jobs, _ in groups for _ in range(2)]
    res = pl.pallas_call(
        body, name=name,
        out_shape=(*sems, *[pltpu.HBM(a.shape, a.dtype) for a in given], jax.ShapeDtypeStruct((8, 128), F32)),
        in_specs=[_HBM] * n, out_specs=(*[_SEM] * (2 * ng), *[_HBM] * n, pl.BlockSpec(memory_space=pltpu.VMEM)),
        input_output_aliases={i: 2 * ng + i for i in range(n)},
        compiler_params=pltpu.CompilerParams(has_side_effects=_EFFECT),
    )(*given)
    held = res[2 * ng:2 * ng + n]
    states = [(names[g], groups[g][0], res[2 * g], res[2 * g + 1], held[first[g]:first[g] + len(names[g])])
              for g in range(ng)]
    return states, res[-1]


def _start_exchange(name, jobs, arrays):
    states, token = _start_exchanges(name, [(jobs, arrays)])
    return states[0], token


def _finish_exchange(name, state, after):
    names, jobs, send_sem, recv_sem, held = state
    n = len(names)

    def body(*refs):
        hbm = dict(zip(names, refs[:n]))
        send, recv = refs[n:n + 2]
        starts, waits = _plan_all(jobs, hbm, send, recv)
        for cp in waits:
            cp.wait_recv()
        for cp in starts:
            cp.wait_send()

    res = pl.pallas_call(
        body, name=name, out_shape=tuple(pltpu.HBM(a.shape, a.dtype) for a in held),
        in_specs=[_HBM] * n + [_SEM, _SEM, _ANY], out_specs=tuple([_HBM] * n),
        input_output_aliases={i: i for i in range(n)},
        compiler_params=pltpu.CompilerParams(has_side_effects=_EFFECT),
    )(*held, send_sem, recv_sem, after)
    return dict(zip(names, res))


def _row_tile(rows, bytes_per_row):
    best = 16
    for t in range(16, rows + 1, 16):
        if rows % t == 0 and t * bytes_per_row <= 6 * 1024 * 1024:
            best = t
    return best


def _rowwise(fn, ins, out_dtypes, name, after=None):
    rows, cols = ins[0].shape
    per_row = sum(cols * a.dtype.itemsize for a in ins) + sum(cols * jnp.dtype(d).itemsize for d in out_dtypes)
    tr = _row_tile(rows, per_row)
    n_in = len(ins)

    def body(*refs):
        outs = fn(*[r[...] for r in refs[:n_in]])
        for o_ref, o in zip(refs[-len(out_dtypes):], outs):
            o_ref[...] = o.astype(o_ref.dtype)

    tile = pl.BlockSpec((tr, cols), lambda i: (i, 0))
    behind = [] if after is None else [after]
    return pl.pallas_call(
        body, name=name, grid=(rows // tr,),
        in_specs=[tile] * n_in + [pl.BlockSpec((8, 128), lambda i: (0, 0))] * len(behind),
        out_specs=[tile] * len(out_dtypes),
        out_shape=[jax.ShapeDtypeStruct((rows, cols), d) for d in out_dtypes],
        compiler_params=_params("parallel"),
    )(*ins, *behind)


def _tiled(fn, name, grid, pos, ins, outs):
    n_in = len(ins)

    def body(pos_ref, *refs):
        res = fn(*[r[...] for r in refs[:n_in]])
        for o_ref, o in zip(refs[n_in:], res):
            o_ref[...] = o.astype(o_ref.dtype)

    return pl.pallas_call(
        body, name=name,
        grid_spec=pltpu.PrefetchScalarGridSpec(
            num_scalar_prefetch=1, grid=grid,
            in_specs=[pl.BlockSpec(bs, im) for _, bs, im in ins],
            out_specs=[pl.BlockSpec(bs, im) for _, _, bs, im in outs]),
        out_shape=[jax.ShapeDtypeStruct(s, d) for s, d, _, _ in outs],
        compiler_params=_params("parallel"),
    )(pos, *[a for a, _, _ in ins])


def _adamw(w, g, m, v):
    m = ADAM_B1 * m + (1.0 - ADAM_B1) * g
    v = ADAM_B2 * v + (1.0 - ADAM_B2) * (g * g)
    m_hat = m / (1.0 - ADAM_B1 ** ADAM_STEP)
    v_hat = v / (1.0 - ADAM_B2 ** ADAM_STEP)
    return -ADAM_LR * (m_hat / (jnp.sqrt(v_hat) + ADAM_EPS) + ADAM_WD * w), m, v


def _adamw_small(params):
    n = len(params)

    def body(*refs):
        for k in range(n):
            w, g, m, v = (r[...] for r in refs[4 * k:4 * k + 4])
            for o_ref, o in zip(refs[4 * n + 3 * k:4 * n + 3 * k + 3], _adamw(w, g, m, v)):
                o_ref[...] = o

    flat = [a for p in params for a in p]
    return pl.pallas_call(
        body, name="adamw_small",
        out_shape=[jax.ShapeDtypeStruct(p[0].shape, F32) for p in params for _ in range(3)],
    )(*flat)


class _Layout:
    def __init__(self, rows, cols, stacked):
        self.rows, self.cols, self.stacked = rows, cols, stacked

    def whole(self, rows=None):
        r = self.rows if rows is None else rows
        return (N_CHIPS, r, self.cols) if self.stacked else (r, N_CHIPS * self.cols)

    def part_rows(self, h, q=0, nq=1):
        n = self.rows // 2 // nq
        return pl.ds(pl.multiple_of(h * (self.rows // 2) + q * n, 16), n)

    def half_rows(self, h):
        return self.part_rows(h)

    def block(self, ref, p, rows=slice(None)):
        if self.stacked:
            return ref.at[p, rows, :]
        return ref.at[rows, pl.ds(pl.multiple_of(p * self.cols, 128), self.cols)]

    def all_chips(self, ref, rows):
        return ref.at[:, rows, :] if self.stacked else ref.at[rows, :]


BIG = (
    _Layout(IN_SHARD, D_MODEL, True),
    _Layout(ATTN_W, D_MODEL // N_CHIPS, False),
    _Layout(CONV_W, D_MODEL // N_CHIPS, False),
    _Layout(D_MODEL // N_CHIPS, D_MODEL, True),
    _Layout(D_MODEL, FF2 // N_CHIPS, False),
    _Layout(D_FF // N_CHIPS, D_MODEL, True),
)
N_BIG = len(BIG)
_ANY = pl.BlockSpec(memory_space=pl.ANY)


def _position():
    x, y, c = lax.axis_index("x"), lax.axis_index("y"), lax.axis_index("c")
    return x, y, c, 2 * x + y


def _core_of_chip(p, c):
    return (p >> 1, p & 1, c)


def _place_cast(shard, lay, pos, name, after=None):
    rows, cols = shard.shape
    tr = _row_tile(rows, cols * 6)
    if lay.stacked:
        out = (lay.whole(), BF16, (None, tr, cols), lambda i, pos: (pos[0], i, 0))
    else:
        out = (lay.whole(), BF16, (tr, cols), lambda i, pos: (i, pos[0]))
    ins = [(shard, (tr, cols), lambda i, pos: (i, 0))]
    if after is not None:
        ins.append((after, (8, 128), lambda i, pos: (0, 0)))
    return _tiled(lambda a, *_: (a,), name, (rows // tr,), pos, ins, [out])[0]


def _remote(src, dst, send, recv, k, device):
    return pltpu.make_async_remote_copy(src_ref=src, dst_ref=dst, send_sem=send.at[k], recv_sem=recv.at[k],
                                        device_id=device, device_id_type=MESH)


def _arrival(dst, send, recv, k, me):
    return _remote(dst, dst, send, recv, k, me)


def _gather_ici(lay, name, q=0, nq=1):
    def plan(hbm, pos, send, recv, base):
        x, y, c, me = pos
        rows = lay.part_rows(c, q, nq)
        mine = lay.block(hbm[name], me, rows)
        starts = [_remote(mine, mine, send, recv, base + d - 1, _core_of_chip(me ^ d, c)) for d in (1, 2, 3)]
        waits = [_arrival(lay.block(hbm[name], me ^ d, rows), send, recv, base + d - 1, (x, y, c)) for d in (1, 2, 3)]
        return starts, waits
    return _Job(3, plan)


def _gather_d2d(lay, name, q=0, nq=1):
    def plan(hbm, pos, send, recv, base):
        x, y, c, me = pos
        starts, waits = [], []
        for d in (1, 2, 3):
            got = lay.block(hbm[name], me ^ d, lay.part_rows(c, q, nq))
            starts.append(_remote(got, got, send, recv, base + d - 1, (x, y, 1 - c)))
            waits.append(_arrival(lay.block(hbm[name], me ^ d, lay.part_rows(1 - c, q, nq)), send, recv, base + d - 1,
                                  (x, y, c)))
        return starts, waits
    return _Job(3, plan)


def _rs_pair(lay, grad, theirs):
    def plan(hbm, pos, send, recv, base):
        x, y, c, _ = pos
        out = _remote(lay.all_chips(hbm[grad], lay.half_rows(1 - c)), hbm[theirs], send, recv, base, (x, y, 1 - c))
        return [out], [_arrival(hbm[theirs], send, recv, base, (x, y, c))]
    return _Job(1, plan)


def _rs_chips(lay, sums, slots):
    def plan(hbm, pos, send, recv, base):
        x, y, c, me = pos
        starts = [_remote(lay.block(hbm[sums], me ^ d), hbm[slots].at[me], send, recv, base + d - 1,
                          _core_of_chip(me ^ d, c)) for d in (1, 2, 3)]
        waits = [_arrival(hbm[slots].at[me ^ d], send, recv, base + d - 1, (x, y, c)) for d in (1, 2, 3)]
        return starts, waits
    return _Job(3, plan)


def _rs_share(lay, shard):
    def plan(hbm, pos, send, recv, base):
        x, y, c, _ = pos
        mine = hbm[shard].at[lay.half_rows(c), :]
        other = hbm[shard].at[lay.half_rows(1 - c), :]
        return [_remote(mine, mine, send, recv, base, (x, y, 1 - c))], [_arrival(other, send, recv, base, (x, y, c))]
    return _Job(1, plan)


def _slots_shape(lay):
    return jax.ShapeDtypeStruct((N_CHIPS, lay.rows // 2, lay.cols), BF16)


def _theirs_shape(lay):
    return jax.ShapeDtypeStruct(lay.whole(lay.rows // 2), BF16)


def _pair_sum(grad, theirs, lay, pos, name):
    half = lay.rows // 2
    add = lambda a, b: (a.astype(F32) + b.astype(F32),)
    if lay.stacked:
        tr = _row_tile(half, lay.cols * 6)
        nt = half // tr
        flat = lambda a: a.reshape(-1, lay.cols)
        mine = lambda t, pos: ((t // nt) * (2 * nt) + pos[1] * nt + t % nt, 0)
        grid, blk = (N_CHIPS * nt,), (tr, lay.cols)
        grad, theirs = flat(grad), flat(theirs)
    else:
        tr = _row_tile(half, N_CHIPS * lay.cols * 6)
        nt = half // tr
        mine = lambda t, pos: (pos[1] * nt + t, 0)
        grid, blk = (nt,), (tr, N_CHIPS * lay.cols)
    same = lambda t, pos: (t, 0)
    out = _tiled(add, name, grid, pos, [(grad, blk, mine), (theirs, blk, same)], [(theirs.shape, BF16, blk, same)])[0]
    return out.reshape(lay.whole(half))


def _chip_sum(sums, slots, lay, pos, name, after=None):
    half = lay.rows // 2
    tr = _row_tile(half, lay.cols * 12)
    nt = half // tr
    blk3 = (None, tr, lay.cols)
    if lay.stacked:
        own = (sums, blk3, lambda i, pos: (pos[0], i, 0))
    else:
        own = (sums, (tr, lay.cols), lambda i, pos: (i, pos[0]))
    others = [(slots, blk3, functools.partial(lambda d, i, pos: (pos[0] ^ d, i, 0), d)) for d in (1, 2, 3)]

    def add(a, b1, b2, b3, *_):
        return (((a.astype(F32) + b1.astype(F32)) + b2.astype(F32)) + b3.astype(F32),)

    if after is not None:
        others.append((after, (8, 128), lambda i, pos: (0, 0)))
    return _tiled(add, name, (nt,), pos, [own] + others,
                  [((lay.rows, lay.cols), F32, (tr, lay.cols), lambda i, pos: (pos[1] * nt + i, 0))])[0]


N_DEV = 8


def _to_all(src, slots):
    def plan(hbm, pos, send, recv, base):
        x, y, c, _ = pos
        idx = 4 * x + 2 * y + c
        starts = [_remote(hbm[src], hbm[slots].at[idx], send, recv, base + k - 1,
                          (x ^ (k >> 2), y ^ ((k >> 1) & 1), c ^ (k & 1))) for k in range(1, N_DEV)]
        waits = [_arrival(hbm[slots].at[idx ^ k], send, recv, base + k - 1, (x, y, c)) for k in range(1, N_DEV)]
        return starts, waits
    return _Job(N_DEV - 1, plan)


def _sum_slots(own, slots, pos):
    def body(pos_ref, own_ref, slots_ref, o_ref):
        idx = 2 * pos_ref[0] + pos_ref[1]
        term = lambda q: jnp.where(idx == q, own_ref[...], slots_ref[q])
        acc = term(0)
        for q in range(1, N_DEV):
            acc = acc + term(q)
        o_ref[...] = acc

    return pl.pallas_call(
        body, name="sum_small", out_shape=jax.ShapeDtypeStruct(own.shape, F32),
        in_specs=[pl.BlockSpec(memory_space=pltpu.SMEM), pl.BlockSpec(memory_space=pltpu.VMEM),
                  pl.BlockSpec(memory_space=pltpu.VMEM)],
    )(pos, own, slots)


def _pack_rows(parts):
    padded = [jnp.pad(a, ((0, -a.shape[0] % 8), (0, 0))) for a in parts]
    starts = [sum(p.shape[0] for p in padded[:k]) for k in range(len(padded))]
    return jnp.concatenate(padded, axis=0), starts


def kernel(x, mix_norm, w_in, b_in, sinks, conv_w, w_attn_branch, w_conv_branch, w_out, ffn_norm, w_up, ffn_conv_w, w_down, final_norm, loss_target, m_mix_norm, m_w_in, m_b_in, m_sinks, m_conv_w, m_w_attn_branch, m_w_conv_branch, m_w_out, m_ffn_norm, m_w_up, m_ffn_conv_w, m_w_down, m_final_norm, v_mix_norm, v_w_in, v_b_in, v_sinks, v_conv_w, v_w_attn_branch, v_w_conv_branch, v_w_out, v_ffn_norm, v_w_up, v_ffn_conv_w, v_w_down, v_final_norm):
    me = 2 * lax.axis_index("x") + lax.axis_index("y")
    big_w = [w_in[0].T, w_attn_branch[0], w_conv_branch[0], w_out[0], w_up[0], w_down[0]]
    big_m = [m_w_in[0].T, m_w_attn_branch[0], m_w_conv_branch[0], m_w_out[0], m_w_up[0], m_w_down[0]]
    big_v = [v_w_in[0].T, v_w_attn_branch[0], v_w_conv_branch[0], v_w_out[0], v_w_up[0], v_w_down[0]]
    names = ("w_in", "w_ab", "w_cb", "w_out", "w_up", "w_down")

    pos = jnp.stack([me, lax.axis_index("c")]).astype(jnp.int32)

    lay = dict(zip(names, BIG))
    xs, target, sk = x[0], loss_target[0], sinks[0]
    s = xs.shape[0]
    tm, tm2, bk = min(256, s), min(512, s), min(1024, s)

    taps, (_, t0) = _pack_rows([conv_w[0], ffn_conv_w[0].reshape(3 * (FF2 // N_CHIPS // 128), 128)])
    taps_flight, started = _start_exchange("taps_start", [_to_all("v", "slots")],
                                           {"v": taps, "slots": jnp.zeros((N_DEV, *taps.shape), F32)})
    placed = {"w_in": _place_cast(big_w[0], lay["w_in"], pos, "cast_w_in", after=started)}
    fly_in, started = _start_exchange("gather_in_start", [_gather_ici(lay["w_in"], "w_in")], {"w_in": placed["w_in"]})
    for w, n in zip(big_w[1:], names[1:]):
        placed[n] = _place_cast(w, lay[n], pos, "cast_" + n, after=started)
    trio = ("w_ab", "w_cb", "w_out")
    (fly_trio, fly_down, fly_up), started = _start_exchanges("gather_rest_start", [
        ([_gather_ici(lay[n], n) for n in ws], {n: placed[n] for n in ws}) for ws in (trio, ("w_down",), ("w_up",))])

    got = _finish_exchange("gather_in_wait", fly_in, after=started)
    w_in_full = _exchange("gather_in_d2d", [[_gather_d2d(lay["w_in"], "w_in")]], bufs=got)["w_in"].reshape(IN_W, D_MODEL)
    xn, qkv, c3, gates = _inproj_fwd(xs, mix_norm, w_in_full, b_in, tm2)
    k2 = _Carry([_gather_d2d(lay[n], n) for n in trio], bufs=_finish_exchange("gather_trio_wait", fly_trio, after=qkv))
    attn = _attn_fwd(qkv, sk, comm=k2)
    w_ab, w_cb = k2.out["w_ab"], k2.out["w_cb"]
    w_out_full = k2.out["w_out"].reshape(D_MODEL, D_MODEL)
    k3 = _Carry([_gather_d2d(lay["w_down"], "w_down")], bufs=_finish_exchange("gather_down_wait", fly_down, after=attn))
    taps = _finish_exchange("taps_wait", taps_flight, after=attn)
    taps = lax.dynamic_update_slice(taps["slots"], taps["v"][None], (2 * me + lax.axis_index("c"), 0, 0))
    conv_full = taps[0::2, 0:3].transpose(1, 0, 2).reshape(3, CONV_W)
    ffn_cw_full = taps[0::2, t0:t0 + 33].reshape(N_CHIPS, 3, FF2 // N_CHIPS).transpose(1, 0, 2).reshape(3, FF2)
    conv, a, cv, merged, h1, hn = _mix_fwd(xs, attn, c3, gates, conv_full, w_ab, w_cb, w_out_full, ffn_norm, tm2, comm=k3)
    w_down_full = k3.out["w_down"].reshape(D_FF, D_MODEL)
    w_up_full = _exchange("gather_up_d2d", [[_gather_d2d(lay["w_up"], "w_up")]],
                          bufs=_finish_exchange("gather_up_wait", fly_up, after=hn))["w_up"]
    u, up, act, dh2, loss_part, g_fn = _ffn_fwd_loss(hn, h1, w_up_full, ffn_cw_full, w_down_full,
                                                     final_norm[None, :], target, tm)

    grads, sums, slots = {}, {}, {}

    def pair(*ws):
        return _Carry([_rs_pair(lay[n], "g_" + n, "t_" + n) for n in ws], reads={"g_" + n: grads[n] for n in ws},
                      fresh={"t_" + n: _theirs_shape(lay[n]) for n in ws})

    def chips(*ws, also=None):
        k = _Carry([_rs_chips(lay[n], "s_" + n, "r_" + n) for n in ws], reads={"s_" + n: sums[n] for n in ws},
                   fresh={"r_" + n: _slots_shape(lay[n]) for n in ws})
        if also is not None:
            k = _Carry(k.jobs + also.jobs, {**k.reads, **also.reads}, None, {**k.fresh, **also.fresh})
        return k

    def pair_sums(k, *ws):
        for n in ws:
            sums[n] = _pair_sum(grads[n], k.out["t_" + n], lay[n], pos, "pair_sum_" + n)

    def take_slots(k, *ws):
        for n in ws:
            slots[n] = k.out["r_" + n]

    du, dh1, g_fcw, g_g2 = _ffn_bwd(dh2, u, up, h1, w_up_full, ffn_cw_full, w_down_full, ffn_norm, tm)
    grads["w_down"] = _wgrad(act, dh2, D_FF // 2, D_MODEL, bk, "wgrad_down").reshape(lay["w_down"].whole())
    k4 = pair("w_down")
    grads["w_up"] = _wgrad(hn, du, D_MODEL, FF2 // 4, bk, "wgrad_up", comm=k4)
    pair_sums(k4, "w_down")
    k5 = chips("w_down", also=pair("w_up"))
    dattn, da, dcv, dc3, dgt, g_cw = _mix_bwd(dh1, gates, a, cv, c3, conv_full, w_ab, w_cb, w_out_full, tm2, comm=k5)
    take_slots(k5, "w_down")
    pair_sums(k5, "w_up")
    grads["w_out"] = _wgrad(merged, dh1, D_MODEL, D_MODEL, bk, "wgrad_out").reshape(lay["w_out"].whole())
    grads["w_ab"] = _wgrad(attn, da, ATTN_W, D_MODEL, bk, "wgrad_ab")
    grads["w_cb"] = _wgrad(conv, dcv, CONV_W, D_MODEL, bk, "wgrad_cb")
    k6 = chips("w_up", also=pair("w_out", "w_ab", "w_cb"))
    dq, dk, dv, g_sk = _attn_bwd(qkv, sk, attn, dattn, comm=k6)
    take_slots(k6, "w_up")
    pair_sums(k6, "w_out", "w_ab", "w_cb")
    grad_x, dproj, g_b, g_g1 = _inproj_bwd(dq, dk, dv, dc3, dgt, w_in_full, xs, dh1, mix_norm, tm2)

    parts = [loss_part, g_g1, g_b, jnp.pad(g_sk[:, 0], (0, 120))[None, :], g_cw, g_g2, g_fcw, g_fn]
    packed, at = _pack_rows([p.reshape(-1, 128) for p in parts])
    small_flight, started = _start_exchange("small_start", [_to_all("v", "slots")],
                                            {"v": packed, "slots": jnp.zeros((N_DEV, *packed.shape), F32)})
    k8 = chips("w_out", "w_ab", "w_cb")
    k8.reads["after"] = started
    grads["w_in"] = _wgrad_in(xn, dproj, min(512, s), comm=k8).reshape(lay["w_in"].whole())
    take_slots(k8, "w_out", "w_ab", "w_cb")
    others = names[1:]
    in_flight, started = _start_exchange("rs_pair_in_start", [_rs_pair(lay["w_in"], "g", "t")],
                                         {"g": grads["w_in"], "t": _theirs_shape(lay["w_in"])})
    halves = {n: _chip_sum(sums[n], slots[n], lay[n], pos, "chip_sum_" + n, after=started) for n in ("w_up", "w_down")}
    landed = _finish_exchange("rs_pair_in_wait", in_flight, after=halves["w_down"])
    sums["w_in"] = _pair_sum(landed["g"], landed["t"], lay["w_in"], pos, "pair_sum_w_in")
    in_flight, started = _start_exchange("rs_chips_in_start", [_rs_chips(lay["w_in"], "s", "r")],
                                         {"s": sums["w_in"], "r": _slots_shape(lay["w_in"])})
    for n in trio:
        halves[n] = _chip_sum(sums[n], slots[n], lay[n], pos, "chip_sum_" + n, after=started)
    shared = _exchange("share_halves", [[_rs_share(lay[n], n) for n in others]], bufs=halves)
    w_of, m_of, v_of = dict(zip(names, big_w)), dict(zip(names, big_m)), dict(zip(names, big_v))

    def adam(n, g, after=None):
        return _rowwise(lambda w, g, m, v: (g, *_adamw(w, g, m, v)), [w_of[n], g, m_of[n], v_of[n]], [F32] * 4,
                        "adamw_" + n, after=after)

    new_of, last = {}, None
    for n in ("w_up", "w_down", "w_out", "w_ab", "w_cb"):
        new_of[n] = adam(n, shared[n], last)
        last = new_of[n][1]

    arrived = _finish_exchange("small_wait", small_flight, after=last)
    total = _sum_slots(arrived["v"], arrived["slots"], pos)
    part = lambda k: total[at[k]:at[k] + parts[k].size // 128].reshape(parts[k].shape)
    loss = total[0, 0]
    g_mix, g_b, g_g2, g_fn = part(1), part(2), part(5), part(7)
    g_sk = part(3)[:, 0:N_HEADS]
    g_cw = lax.dynamic_slice(part(4), (0, me * 128), (3, 128))
    g_fcw = lax.dynamic_slice(part(6), (0, me * (FF2 // N_CHIPS)), (3, FF2 // N_CHIPS))
    small_p = [
        (mix_norm, g_mix, m_mix_norm, v_mix_norm), (b_in, g_b, m_b_in, v_b_in), (sinks, g_sk, m_sinks, v_sinks),
        (conv_w[0], g_cw, m_conv_w[0], v_conv_w[0]), (ffn_norm, g_g2, m_ffn_norm, v_ffn_norm),
        (ffn_conv_w[0], g_fcw, m_ffn_conv_w[0], v_ffn_conv_w[0]),
        (final_norm[None, :], g_fn, m_final_norm[None, :], v_final_norm[None, :])]
    small_new = _adamw_small(small_p)
    small_new = [small_new[3 * k:3 * k + 3] for k in range(len(small_p))]

    landed = _finish_exchange("rs_chips_in_wait", in_flight, after=small_new[0][0])
    half_in = _chip_sum(landed["s"], landed["r"], lay["w_in"], pos, "chip_sum_w_in")
    shared["w_in"] = _exchange("share_in", [[_rs_share(lay["w_in"], "w_in")]], bufs={"w_in": half_in})["w_in"]
    new_of["w_in"] = adam("w_in", shared["w_in"])
    big_g = [new_of[n][0] for n in names]
    big_new = [new_of[n][1:] for n in names]

    order = [("s", 0), ("b", 0), ("s", 1), ("s", 2), ("s", 3), ("b", 1), ("b", 2), ("b", 3), ("s", 4), ("b", 4),
             ("s", 5), ("b", 5), ("s", 6)]
    shapes = [mix_norm.shape, w_in.shape, b_in.shape, sinks.shape, conv_w.shape, w_attn_branch.shape,
              w_conv_branch.shape, w_out.shape, ffn_norm.shape, w_up.shape, ffn_conv_w.shape, w_down.shape,
              final_norm.shape]
    small_g = [p[1] for p in small_p]
    big_g[0] = big_g[0].T
    big_new[0] = [a.T for a in big_new[0]]
    out_g = [(small_g[k] if kind == "s" else big_g[k]).reshape(shp) for (kind, k), shp in zip(order, shapes)]
    news = [[(small_new[k][j] if kind == "s" else big_new[k][j]).reshape(shp) for (kind, k), shp in zip(order, shapes)]
            for j in range(3)]
    return (loss, grad_x[None], *out_g, *news[0], *news[1], *news[2])
```

```python
import functools

import jax
import jax.numpy as jnp
from jax import lax
from jax.experimental import pallas as pl
from jax.experimental.pallas import tpu as pltpu

F32 = jnp.float32
BF16 = jnp.bfloat16

D_MODEL = 1024
HEAD_DIM = 64
N_HEADS = 8
N_KV_HEADS = 2
GROUP = N_HEADS // N_KV_HEADS
BLOCK = 128
ATTN_SCALE = HEAD_DIM ** -0.5
ATTN_W = N_HEADS * HEAD_DIM
KV_W = N_KV_HEADS * HEAD_DIM
CONV_W = 512
QKV_W = ATTN_W + 2 * KV_W
C3_W = 3 * CONV_W
GATES_W = 2 * D_MODEL
IN_W = QKV_W + C3_W + GATES_W
D_FF = 2816
FF2 = 2 * D_FF
NORM_EPS = 1e-5
N_CHIPS = 4
IN_SHARD = IN_W // N_CHIPS
NEG = -1e30

ADAM_LR = 0.001
ADAM_B1 = 0.9
ADAM_B2 = 0.999
ADAM_EPS = 1e-08
ADAM_WD = 0.01
ADAM_STEP = 10

VMEM_LIMIT = 56 * 1024 * 1024
MESH = pl.DeviceIdType.MESH

NT = (((1,), (1,)), ((), ()))
TN = (((0,), (0,)), ((), ()))


def _params(*sem):
    return pltpu.CompilerParams(dimension_semantics=sem, vmem_limit_bytes=VMEM_LIMIT)


def _resident(shape):
    return pl.BlockSpec(shape, lambda *_: (0,) * len(shape), pipeline_mode=pl.Buffered(1))


def _sigmoid(v):
    return 0.5 * jnp.tanh(0.5 * v) + 0.5


def _rstd(v):
    return lax.rsqrt(jnp.mean(v * v, axis=-1, keepdims=True) + NORM_EPS)


def _rms_bwd(dy, v, rstd, g):
    vhat = v * rstd
    t = dy * g
    return rstd * (t - vhat * jnp.mean(t * vhat, axis=-1, keepdims=True)), dy * vhat


def _taps(z, cw):
    return cw[2:3] * z + cw[1:2] * pltpu.roll(z, 1, 0) + cw[0:1] * pltpu.roll(z, 2, 0)


def _causal_conv(z, prev, cw):
    edge = _taps(jnp.concatenate([prev, z[0:8]], axis=0), cw)
    return jnp.concatenate([edge[8:16], _taps(z, cw)[8:]], axis=0)


def _rows_after(z, nxt):
    n = z.shape[0]
    edge = jnp.concatenate([z[n - 8:n], nxt], axis=0)
    return tuple(jnp.concatenate([pltpu.roll(z, n - k, 0)[:n - 8], pltpu.roll(edge, 16 - k, 0)[0:8]], axis=0)
                 for k in (1, 2))


def _inproj_fwd(x, g1, w_in, b_in, tm, comm=None):
    s = x.shape[0]

    def body(x_ref, g_ref, w_ref, b_ref, xn_ref, qkv_ref, c3_ref, gt_ref):
        xf = x_ref[...]
        xn = (xf * _rstd(xf) * g_ref[...]).astype(BF16)
        xn_ref[...] = xn

        def seg(a, b):
            return lax.dot_general(xn, w_ref[a:b, :], NT, preferred_element_type=F32) + b_ref[:, a:b]

        qkv_ref[...] = seg(0, QKV_W).astype(BF16)
        c3_ref[...] = seg(QKV_W, QKV_W + C3_W)
        gt_ref[...] = seg(QKV_W + C3_W, IN_W)

    row = lambda w: pl.BlockSpec((tm, w), lambda i: (i, 0))
    return _call(
        comm, body, name="inproj_fwd", grid=(s // tm,),
        in_specs=[row(D_MODEL), _resident((1, D_MODEL)), _resident((IN_W, D_MODEL)), _resident((1, IN_W))],
        out_specs=[row(D_MODEL), row(QKV_W), row(C3_W), row(GATES_W)],
        out_shape=[jax.ShapeDtypeStruct((s, D_MODEL), BF16), jax.ShapeDtypeStruct((s, QKV_W), BF16),
                   jax.ShapeDtypeStruct((s, C3_W), F32), jax.ShapeDtypeStruct((s, GATES_W), F32)],
        compiler_params=_params("parallel"),
    )(x, g1, w_in, b_in)


def _attn_mask(first_block):
    qi = lax.broadcasted_iota(jnp.int32, (GROUP * BLOCK, 2 * BLOCK), 0) & (BLOCK - 1)
    kj = lax.broadcasted_iota(jnp.int32, (GROUP * BLOCK, 2 * BLOCK), 1)
    band = (kj > qi) & (kj <= qi + BLOCK)
    return band & ((kj >= BLOCK) | jnp.logical_not(first_block))


def _sink_column(sk_ref, h):
    rows = lax.broadcasted_iota(jnp.int32, (GROUP * BLOCK, 1), 0)
    col = jnp.full((GROUP * BLOCK, 1), sk_ref[h * GROUP], F32)
    for g in range(1, GROUP):
        col = jnp.where(rows >= g * BLOCK, sk_ref[h * GROUP + g], col)
    return col


def _stack_heads(t, h):
    return jnp.concatenate(
        [t[:, (h * GROUP + g) * HEAD_DIM:(h * GROUP + g + 1) * HEAD_DIM] for g in range(GROUP)], axis=0)


def _unstack_heads(per_kv):
    return jnp.concatenate(
        [t[g * BLOCK:(g + 1) * BLOCK] for t in per_kv for g in range(GROUP)], axis=1)


def _attn_specs(nb):
    cur = lambda i: jnp.minimum(i, nb - 1)
    prev = lambda i: jnp.maximum(jnp.minimum(i, nb - 1) - 1, 0)
    q = pl.BlockSpec((BLOCK, ATTN_W), lambda i: (cur(i), 0))
    kp = pl.BlockSpec((BLOCK, KV_W), lambda i: (prev(i), ATTN_W // KV_W))
    kc = pl.BlockSpec((BLOCK, KV_W), lambda i: (cur(i), ATTN_W // KV_W))
    vp = pl.BlockSpec((BLOCK, KV_W), lambda i: (prev(i), ATTN_W // KV_W + 1))
    vc = pl.BlockSpec((BLOCK, KV_W), lambda i: (cur(i), ATTN_W // KV_W + 1))
    return q, kp, kc, vp, vc


def _attn_fwd(qkv, sinks, comm=None):
    s = qkv.shape[0]
    nb = s // BLOCK

    def body(sk_ref, q_ref, kp_ref, kc_ref, vp_ref, vc_ref, o_ref):
        mask = _attn_mask(pl.program_id(0) == 0)
        q, kp, kc, vp, vc = q_ref[...], kp_ref[...], kc_ref[...], vp_ref[...], vc_ref[...]
        outs = []
        for h in range(N_KV_HEADS):
            hs = slice(h * HEAD_DIM, (h + 1) * HEAD_DIM)
            k2 = jnp.concatenate([kp[:, hs], kc[:, hs]], axis=0)
            v2 = jnp.concatenate([vp[:, hs], vc[:, hs]], axis=0)
            sc = lax.dot_general(_stack_heads(q, h), k2, NT, preferred_element_type=F32) * ATTN_SCALE
            sc = jnp.where(mask, sc, NEG)
            sink = _sink_column(sk_ref, h)
            m = jnp.maximum(jnp.max(sc, axis=1, keepdims=True), sink)
            p = jnp.exp(sc - m)
            den = jnp.sum(p, axis=1, keepdims=True) + jnp.exp(sink - m)
            outs.append(jnp.dot(p.astype(BF16), v2, preferred_element_type=F32) / den)
        o_ref[...] = _unstack_heads(outs).astype(BF16)

    return _call(
        comm, body, name="attn_fwd", grid=(nb,),
        in_specs=[pl.BlockSpec(memory_space=pltpu.SMEM), *_attn_specs(nb)],
        out_specs=pl.BlockSpec((BLOCK, ATTN_W), lambda i: (i, 0)),
        out_shape=jax.ShapeDtypeStruct((s, ATTN_W), BF16),
        compiler_params=_params("parallel"),
    )(sinks, qkv, qkv, qkv, qkv, qkv)


def _mix_fwd(x, attn, c3, gates, conv_w, w_ab, w_cb, w_out, g2, tm, comm=None):
    s = x.shape[0]

    def body(x_ref, at_ref, c3_ref, gt_ref, cw_ref, wab_ref, wcb_ref, wo_ref, g_ref,
             conv_ref, a_ref, cv_ref, mg_ref, h1_ref, hn_ref, carry_ref):
        @pl.when(pl.program_id(0) == 0)
        def _():
            carry_ref[...] = jnp.zeros_like(carry_ref)

        c3v = c3_ref[...]
        cb, cc, cx = c3v[:, :CONV_W], c3v[:, CONV_W:2 * CONV_W], c3v[:, 2 * CONV_W:]
        z = cc * cx
        cz = _causal_conv(z, carry_ref[...], cw_ref[...])
        carry_ref[...] = z[tm - 8:tm]
        conv = (cb * cz).astype(BF16)
        conv_ref[...] = conv
        a = jnp.dot(at_ref[...], wab_ref[...], preferred_element_type=F32)
        cv = jnp.dot(conv, wcb_ref[...], preferred_element_type=F32)
        a_ref[...] = a.astype(BF16)
        cv_ref[...] = cv.astype(BF16)
        gt = gt_ref[...]
        merged = (_sigmoid(gt[:, :D_MODEL]) * a + _sigmoid(gt[:, D_MODEL:]) * cv).astype(BF16)
        mg_ref[...] = merged
        h1 = x_ref[...] + jnp.dot(merged, wo_ref[...], preferred_element_type=F32)
        h1_ref[...] = h1
        hn_ref[...] = (h1 * _rstd(h1) * g_ref[...]).astype(BF16)

    row = lambda w: pl.BlockSpec((tm, w), lambda i: (i, 0))
    return _call(
        comm, body, name="mix_fwd", grid=(s // tm,),
        in_specs=[row(D_MODEL), row(ATTN_W), row(C3_W), row(GATES_W), _resident((3, CONV_W)),
                  _resident((ATTN_W, D_MODEL)), _resident((CONV_W, D_MODEL)), _resident((D_MODEL, D_MODEL)),
                  _resident((1, D_MODEL))],
        out_specs=[row(CONV_W), row(D_MODEL), row(D_MODEL), row(D_MODEL), row(D_MODEL), row(D_MODEL)],
        out_shape=[jax.ShapeDtypeStruct((s, CONV_W), BF16), jax.ShapeDtypeStruct((s, D_MODEL), BF16),
                   jax.ShapeDtypeStruct((s, D_MODEL), BF16), jax.ShapeDtypeStruct((s, D_MODEL), BF16),
                   jax.ShapeDtypeStruct((s, D_MODEL), F32), jax.ShapeDtypeStruct((s, D_MODEL), BF16)],
        scratch_shapes=[pltpu.VMEM((8, CONV_W), F32)],
        compiler_params=_params("arbitrary"),
    )(x, attn, c3, gates, conv_w, w_ab, w_cb, w_out, g2)


def _ffn_fwd_loss(hn, h1, w_up, ffn_cw, w_down, g3, target, tm):
    s = hn.shape[0]

    def body(hn_ref, h1_ref, wu_ref, cw_ref, wd_ref, g_ref, t_ref,
             u_ref, up_ref, act_ref, dh2_ref, loss_ref, gfn_ref, carry_ref):
        @pl.when(pl.program_id(0) == 0)
        def _():
            carry_ref[...] = jnp.zeros_like(carry_ref)
            loss_ref[...] = jnp.zeros_like(loss_ref)
            gfn_ref[...] = jnp.zeros_like(gfn_ref)

        u = jnp.dot(hn_ref[...], wu_ref[...], preferred_element_type=F32)
        u_ref[...] = u.astype(BF16)
        up = _causal_conv(u, carry_ref[...], cw_ref[...])
        up_ref[...] = up
        carry_ref[...] = u[tm - 8:tm]
        gate, val = up[:, :D_FF], up[:, D_FF:]
        act = (gate * _sigmoid(gate) * val).astype(BF16)
        act_ref[...] = act
        h2 = h1_ref[...] + jnp.dot(act, wd_ref[...], preferred_element_type=F32)
        rstd = _rstd(h2)
        g = g_ref[...]
        err = h2 * rstd * g - t_ref[...]
        loss_ref[...] += jnp.sum(err * err) * (0.5 / D_MODEL)
        dh2, dg = _rms_bwd(err * (1.0 / D_MODEL), h2, rstd, g)
        dh2_ref[...] = dh2
        gfn_ref[...] += jnp.sum(dg, axis=0, keepdims=True)

    row = lambda w: pl.BlockSpec((tm, w), lambda i: (i, 0))
    acc = lambda w: pl.BlockSpec((1, w), lambda i: (0, 0))
    return pl.pallas_call(
        body, name="ffn_fwd_loss", grid=(s // tm,),
        in_specs=[row(D_MODEL), row(D_MODEL), _resident((D_MODEL, FF2)), _resident((3, FF2)),
                  _resident((D_FF, D_MODEL)), _resident((1, D_MODEL)), row(D_MODEL)],
        out_specs=[row(FF2), row(FF2), row(D_FF), row(D_MODEL), acc(128), acc(D_MODEL)],
        out_shape=[jax.ShapeDtypeStruct((s, FF2), BF16), jax.ShapeDtypeStruct((s, FF2), F32),
                   jax.ShapeDtypeStruct((s, D_FF), BF16),
                   jax.ShapeDtypeStruct((s, D_MODEL), F32), jax.ShapeDtypeStruct((1, 128), F32),
                   jax.ShapeDtypeStruct((1, D_MODEL), F32)],
        scratch_shapes=[pltpu.VMEM((8, FF2), F32)],
        compiler_params=_params("arbitrary"),
    )(hn, h1, w_up, ffn_cw, w_down, g3, target)


def _ffn_bwd(dh2, u, up, h1, w_up, ffn_cw, w_down, g2, tm):
    s = dh2.shape[0]
    nt = s // tm

    def body(dh2_ref, u_ref, up_ref, h1_ref, wu_ref, cw_ref, wd_ref, g_ref,
             du_ref, dh1_ref, gcw_ref, gg_ref, carry_ref):
        @pl.when(pl.program_id(0) == 0)
        def _():
            carry_ref[...] = jnp.zeros_like(carry_ref)
            gcw_ref[...] = jnp.zeros_like(gcw_ref)
            gg_ref[...] = jnp.zeros_like(gg_ref)

        dh2v = dh2_ref[...]
        dact = lax.dot_general(dh2v.astype(BF16), wd_ref[...], NT, preferred_element_type=F32)
        upv = up_ref[...]
        gate, val = upv[:, :D_FF], upv[:, D_FF:]
        sg = _sigmoid(gate)
        dval = dact * (gate * sg)
        dgate = dact * val * (sg * (1.0 + gate * (1.0 - sg)))
        dup = jnp.concatenate([dgate, dval], axis=1)
        dup1, dup2 = _rows_after(dup, carry_ref[...])
        carry_ref[...] = dup[0:8]
        u = u_ref[...].astype(F32)
        gcw_ref[2:3, :] += jnp.sum(dup * u, axis=0, keepdims=True)
        gcw_ref[1:2, :] += jnp.sum(dup1 * u, axis=0, keepdims=True)
        gcw_ref[0:1, :] += jnp.sum(dup2 * u, axis=0, keepdims=True)
        cw = cw_ref[...]
        du = (cw[2:3] * dup + cw[1:2] * dup1 + cw[0:1] * dup2).astype(BF16)
        du_ref[...] = du
        dhn = lax.dot_general(du, wu_ref[...], NT, preferred_element_type=F32)
        h1v = h1_ref[...]
        dh1, dg = _rms_bwd(dhn, h1v, _rstd(h1v), g_ref[...])
        dh1_ref[...] = dh2v + dh1
        gg_ref[...] += jnp.sum(dg, axis=0, keepdims=True)

    row = lambda w: pl.BlockSpec((tm, w), lambda i: (nt - 1 - i, 0))
    return pl.pallas_call(
        body, name="ffn_bwd", grid=(nt,),
        in_specs=[row(D_MODEL), row(FF2), row(FF2),
                  row(D_MODEL), _resident((D_MODEL, FF2)), _resident((3, FF2)), _resident((D_FF, D_MODEL)),
                  _resident((1, D_MODEL))],
        out_specs=[row(FF2), row(D_MODEL), pl.BlockSpec((3, FF2), lambda i: (0, 0)),
                   pl.BlockSpec((1, D_MODEL), lambda i: (0, 0))],
        out_shape=[jax.ShapeDtypeStruct((s, FF2), BF16), jax.ShapeDtypeStruct((s, D_MODEL), F32),
                   jax.ShapeDtypeStruct((3, FF2), F32), jax.ShapeDtypeStruct((1, D_MODEL), F32)],
        scratch_shapes=[pltpu.VMEM((8, FF2), F32)],
        compiler_params=_params("arbitrary"),
    )(dh2, u, up, h1, w_up, ffn_cw, w_down, g2)


def _mix_bwd(dh1, gates, a, cv, c3, conv_w, w_ab, w_cb, w_out, tm, comm=None):
    s = dh1.shape[0]
    nt = s // tm
    halo = 8

    def body(dh1_ref, gt_ref, a_ref, cv_ref, c3_ref, ch_ref, cw_ref, wab_ref, wcb_ref, wo_ref,
             dat_ref, da_ref, dcv_ref, dc3_ref, dgt_ref, gcw_ref, carry_ref):
        i = pl.program_id(0)

        @pl.when(i == 0)
        def _():
            carry_ref[...] = jnp.zeros_like(carry_ref)
            gcw_ref[...] = jnp.zeros_like(gcw_ref)

        dm = lax.dot_general(dh1_ref[...].astype(BF16), wo_ref[...], NT, preferred_element_type=F32)
        gt = gt_ref[...]
        sa, sc = _sigmoid(gt[:, :D_MODEL]), _sigmoid(gt[:, D_MODEL:])
        da = (dm * sa).astype(BF16)
        dcv = (dm * sc).astype(BF16)
        da_ref[...] = da
        dcv_ref[...] = dcv
        dgt_ref[...] = jnp.concatenate(
            [dm * a_ref[...].astype(F32) * (sa * (1.0 - sa)), dm * cv_ref[...].astype(F32) * (sc * (1.0 - sc))],
            axis=1).astype(BF16)
        dat_ref[...] = lax.dot_general(da, wab_ref[...], NT, preferred_element_type=F32).astype(BF16)
        dconv = lax.dot_general(dcv, wcb_ref[...], NT, preferred_element_type=F32)
        c3v = c3_ref[...]
        cb, cc, cx = c3v[:, :CONV_W], c3v[:, CONV_W:2 * CONV_W], c3v[:, 2 * CONV_W:]
        z = cc * cx
        chv = ch_ref[...] * (i < nt - 1).astype(F32)
        zh = chv[:, CONV_W:2 * CONV_W] * chv[:, 2 * CONV_W:]
        cw = cw_ref[...]
        cz = _causal_conv(z, zh, cw)
        dcz = dconv * cb
        dcz1, dcz2 = _rows_after(dcz, carry_ref[...])
        carry_ref[...] = dcz[0:8]
        gcw_ref[2:3, :] += jnp.sum(dcz * z, axis=0, keepdims=True)
        gcw_ref[1:2, :] += jnp.sum(dcz1 * z, axis=0, keepdims=True)
        gcw_ref[0:1, :] += jnp.sum(dcz2 * z, axis=0, keepdims=True)
        dz = cw[2:3] * dcz + cw[1:2] * dcz1 + cw[0:1] * dcz2
        dc3_ref[...] = jnp.concatenate([dconv * cz, dz * cx, dz * cc], axis=1).astype(BF16)

    row = lambda w: pl.BlockSpec((tm, w), lambda i: (nt - 1 - i, 0))
    return _call(
        comm, body, name="mix_bwd", grid=(nt,),
        in_specs=[row(D_MODEL), row(GATES_W), row(D_MODEL), row(D_MODEL), row(C3_W),
                  pl.BlockSpec((halo, C3_W), lambda i: (jnp.maximum((nt - 1 - i) * (tm // halo) - 1, 0), 0)),
                  _resident((3, CONV_W)), _resident((ATTN_W, D_MODEL)), _resident((CONV_W, D_MODEL)),
                  _resident((D_MODEL, D_MODEL))],
        out_specs=[row(ATTN_W), row(D_MODEL), row(D_MODEL), row(C3_W), row(GATES_W),
                   pl.BlockSpec((3, CONV_W), lambda i: (0, 0))],
        out_shape=[jax.ShapeDtypeStruct((s, ATTN_W), BF16), jax.ShapeDtypeStruct((s, D_MODEL), BF16),
                   jax.ShapeDtypeStruct((s, D_MODEL), BF16), jax.ShapeDtypeStruct((s, C3_W), BF16),
                   jax.ShapeDtypeStruct((s, GATES_W), BF16), jax.ShapeDtypeStruct((3, CONV_W), F32)],
        scratch_shapes=[pltpu.VMEM((8, CONV_W), F32)],
        compiler_params=_params("arbitrary"),
    )(dh1, gates, a, cv, c3, c3, conv_w, w_ab, w_cb, w_out)


def _attn_bwd(qkv, sinks, o, do, comm=None):
    s = qkv.shape[0]
    nb = s // BLOCK

    def body(sk_ref, q_ref, kp_ref, kc_ref, vp_ref, vc_ref, o_ref, do_ref,
             dq_ref, dk_ref, dv_ref, dsk_ref, ck_ref, cvv_ref):
        i = pl.program_id(0)

        @pl.when(i == 0)
        def _():
            ck_ref[...] = jnp.zeros_like(ck_ref)
            cvv_ref[...] = jnp.zeros_like(cvv_ref)
            dsk_ref[...] = jnp.zeros_like(dsk_ref)

        @pl.when(i < nb)
        def _():
            mask = _attn_mask(i == 0)
            q, kp, kc, vp, vc = q_ref[...], kp_ref[...], kc_ref[...], vp_ref[...], vc_ref[...]
            ov, dov = o_ref[...], do_ref[...]
            dqs, dks, dvs = [], [], []
            for h in range(N_KV_HEADS):
                hs = slice(h * HEAD_DIM, (h + 1) * HEAD_DIM)
                k2 = jnp.concatenate([kp[:, hs], kc[:, hs]], axis=0)
                v2 = jnp.concatenate([vp[:, hs], vc[:, hs]], axis=0)
                qg, og, dog = _stack_heads(q, h), _stack_heads(ov, h), _stack_heads(dov, h)
                sc = lax.dot_general(qg, k2, NT, preferred_element_type=F32) * ATTN_SCALE
                sc = jnp.where(mask, sc, NEG)
                sink = _sink_column(sk_ref, h)
                m = jnp.maximum(jnp.max(sc, axis=1, keepdims=True), sink)
                p = jnp.exp(sc - m)
                psink = jnp.exp(sink - m)
                inv = 1.0 / (jnp.sum(p, axis=1, keepdims=True) + psink)
                p = p * inv
                delta = jnp.sum(dog.astype(F32) * og.astype(F32), axis=1, keepdims=True)
                dp = lax.dot_general(dog, v2, NT, preferred_element_type=F32)
                ds = (p * (dp - delta)).astype(BF16)
                dqs.append(jnp.dot(ds, k2, preferred_element_type=F32) * ATTN_SCALE)
                dks.append(lax.dot_general(ds, qg, TN, preferred_element_type=F32) * ATTN_SCALE)
                dvs.append(lax.dot_general(p.astype(BF16), dog, TN, preferred_element_type=F32))
                dsink = -(psink * inv * delta)
                for g in range(GROUP):
                    r = h * GROUP + g
                    dsk_ref[r:r + 1, :] += jnp.sum(dsink[g * BLOCK:(g + 1) * BLOCK])
            dq_ref[...] = _unstack_heads(dqs).astype(BF16)
            dk2 = jnp.concatenate(dks, axis=1)
            dv2 = jnp.concatenate(dvs, axis=1)
            dk_ref[...] = (ck_ref[...] + dk2[:BLOCK]).astype(BF16)
            dv_ref[...] = (cvv_ref[...] + dv2[:BLOCK]).astype(BF16)
            ck_ref[...] = dk2[BLOCK:]
            cvv_ref[...] = dv2[BLOCK:]

        @pl.when(i == nb)
        def _():
            dk_ref[...] = ck_ref[...].astype(BF16)
            dv_ref[...] = cvv_ref[...].astype(BF16)

    cur = lambda i: jnp.minimum(i, nb - 1)
    done = lambda i: jnp.maximum(i - 1, 0)
    return _call(
        comm, body, name="attn_bwd", grid=(nb + 1,),
        in_specs=[pl.BlockSpec(memory_space=pltpu.SMEM), *_attn_specs(nb),
                  pl.BlockSpec((BLOCK, ATTN_W), lambda i: (cur(i), 0)),
                  pl.BlockSpec((BLOCK, ATTN_W), lambda i: (cur(i), 0))],
        out_specs=[pl.BlockSpec((BLOCK, ATTN_W), lambda i: (cur(i), 0)),
                   pl.BlockSpec((BLOCK, KV_W), lambda i: (done(i), 0)),
                   pl.BlockSpec((BLOCK, KV_W), lambda i: (done(i), 0)),
                   pl.BlockSpec((N_HEADS, 128), lambda i: (0, 0))],
        out_shape=[jax.ShapeDtypeStruct((s, ATTN_W), BF16), jax.ShapeDtypeStruct((s, KV_W), BF16),
                   jax.ShapeDtypeStruct((s, KV_W), BF16), jax.ShapeDtypeStruct((N_HEADS, 128), F32)],
        scratch_shapes=[pltpu.VMEM((BLOCK, KV_W), F32), pltpu.VMEM((BLOCK, KV_W), F32)],
        compiler_params=_params("arbitrary"),
    )(sinks, qkv, qkv, qkv, qkv, qkv, o, do)


def _inproj_bwd(dq, dk, dv, dc3, dgt, w_in, x, dh1, g1, tm, comm=None):
    s = x.shape[0]

    def body(dq_ref, dk_ref, dv_ref, dc3_ref, dgt_ref, w_ref, x_ref, dh1_ref, g_ref,
             dx_ref, dp_ref, gb_ref, gg_ref):
        @pl.when(pl.program_id(0) == 0)
        def _():
            gb_ref[...] = jnp.zeros_like(gb_ref)
            gg_ref[...] = jnp.zeros_like(gg_ref)

        dp = jnp.concatenate([dq_ref[...], dk_ref[...], dv_ref[...], dc3_ref[...], dgt_ref[...]], axis=1)
        dp_ref[...] = dp
        gb_ref[...] += jnp.sum(dp.astype(F32), axis=0, keepdims=True)
        dxn = jnp.dot(dp, w_ref[...], preferred_element_type=F32)
        xf = x_ref[...]
        dx, dg = _rms_bwd(dxn, xf, _rstd(xf), g_ref[...])
        dx_ref[...] = dh1_ref[...] + dx
        gg_ref[...] += jnp.sum(dg, axis=0, keepdims=True)

    row = lambda w: pl.BlockSpec((tm, w), lambda i: (i, 0))
    acc = lambda w: pl.BlockSpec((1, w), lambda i: (0, 0))
    return _call(
        comm, body, name="inproj_bwd", grid=(s // tm,),
        in_specs=[row(ATTN_W), row(KV_W), row(KV_W), row(C3_W), row(GATES_W), _resident((IN_W, D_MODEL)),
                  row(D_MODEL), row(D_MODEL), _resident((1, D_MODEL))],
        out_specs=[row(D_MODEL), row(IN_W), acc(IN_W), acc(D_MODEL)],
        out_shape=[jax.ShapeDtypeStruct((s, D_MODEL), F32), jax.ShapeDtypeStruct((s, IN_W), BF16),
                   jax.ShapeDtypeStruct((1, IN_W), F32), jax.ShapeDtypeStruct((1, D_MODEL), F32)],
        compiler_params=_params("arbitrary"),
    )(dq, dk, dv, dc3, dgt, w_in, x, dh1, g1)


def _wgrad(a, b, bm, bn, bk, name, comm=None):
    s, m = a.shape
    n = b.shape[1]
    nk = s // bk

    def body(a_ref, b_ref, o_ref, acc_ref):
        k = pl.program_id(2)

        @pl.when(k == 0)
        def _():
            acc_ref[...] = jnp.zeros_like(acc_ref)

        acc_ref[...] += lax.dot_general(a_ref[...].astype(BF16), b_ref[...].astype(BF16), TN,
                                        preferred_element_type=F32)

        @pl.when(k == nk - 1)
        def _():
            o_ref[...] = acc_ref[...].astype(BF16)

    return _call(
        comm, body, name=name, grid=(m // bm, n // bn, nk),
        in_specs=[pl.BlockSpec((bk, bm), lambda i, j, k: (k, i)), pl.BlockSpec((bk, bn), lambda i, j, k: (k, j))],
        out_specs=pl.BlockSpec((bm, bn), lambda i, j, k: (i, j)),
        out_shape=jax.ShapeDtypeStruct((m, n), BF16),
        scratch_shapes=[pltpu.VMEM((bm, bn), F32)],
        compiler_params=_params("parallel", "parallel", "arbitrary"),
    )(a, b)


def _wgrad_in(xn, dproj, bk, comm=None):
    s = xn.shape[0]
    nk = s // bk

    def body(a_ref, b_ref, o_ref, acc_ref):
        k = pl.program_id(0)

        @pl.when(k == 0)
        def _():
            acc_ref[...] = jnp.zeros_like(acc_ref)

        acc_ref[...] += lax.dot_general(b_ref[...], a_ref[...], TN, preferred_element_type=F32)

        @pl.when(k == nk - 1)
        def _():
            o_ref[...] = acc_ref[...].astype(BF16)

    return _call(
        comm, body, name="wgrad_in", grid=(nk,),
        in_specs=[pl.BlockSpec((bk, D_MODEL), lambda k: (k, 0)), pl.BlockSpec((bk, IN_W), lambda k: (k, 0))],
        out_specs=_resident((IN_W, D_MODEL)),
        out_shape=jax.ShapeDtypeStruct((IN_W, D_MODEL), BF16),
        scratch_shapes=[pltpu.VMEM((IN_W, D_MODEL), F32)],
        compiler_params=_params("arbitrary"),
    )(xn, dproj)


class _Carry:
    def __init__(self, jobs, reads=None, bufs=None, fresh=None):
        self.jobs, self.reads, self.bufs, self.fresh = jobs, reads or {}, bufs or {}, fresh or {}
        self.out = {}


class _Job:
    def __init__(self, n_sems, plan):
        self.n_sems, self.plan = n_sems, plan


def _plan_all(jobs, hbm, send, recv):
    pos = _position()
    starts, waits, base = [], [], 0
    for job in jobs:
        s, w = job.plan(hbm, pos, send, recv, base)
        starts, waits, base = starts + s, waits + w, base + job.n_sems
    return starts, waits


def _call(comm, body, **kw):
    if comm is None:
        return pl.pallas_call(body, **kw)
    grid = kw["grid"]
    single = not isinstance(kw["out_shape"], (list, tuple))
    out_shape = [kw["out_shape"]] if single else list(kw["out_shape"])
    out_specs = [kw["out_specs"]] if single else list(kw["out_specs"])
    in_specs = list(kw["in_specs"])
    scratch = list(kw.get("scratch_shapes", ()))
    r_names, b_names, f_names = list(comm.reads), list(comm.bufs), list(comm.fresh)
    n_args, n_out, n_scr = len(in_specs), len(out_shape), len(scratch)
    n_sems = sum(j.n_sems for j in comm.jobs)

    def wrapped(*refs):
        k = n_args
        hbm = dict(zip(r_names, refs[k:k + len(r_names)]))
        k += len(r_names) + len(b_names)
        outs = refs[k:k + n_out]
        k += n_out
        hbm.update(zip(b_names + f_names, refs[k:k + len(b_names) + len(f_names)]))
        k += len(b_names) + len(f_names)
        send, recv = refs[k + n_scr:]
        starts, waits = _plan_all(comm.jobs, hbm, send, recv)
        ids = [pl.program_id(a) for a in range(len(grid))]
        first = functools.reduce(jnp.logical_and, [i == 0 for i in ids])
        last = functools.reduce(jnp.logical_and, [i == g - 1 for i, g in zip(ids, grid)])

        @pl.when(first)
        def _():
            for cp in starts:
                cp.start()

        body(*refs[:n_args], *outs, *refs[k:k + n_scr])

        @pl.when(last)
        def _():
            for cp in waits:
                cp.wait_recv()
            for cp in starts:
                cp.wait_send()

    sems = pltpu.SemaphoreType.DMA((n_sems,))
    held = [jax.ShapeDtypeStruct(a.shape, a.dtype) for a in comm.bufs.values()] + list(comm.fresh.values())
    call = pl.pallas_call(
        wrapped, name=kw["name"], grid=grid,
        in_specs=in_specs + [_ANY] * (len(r_names) + len(b_names)),
        out_specs=out_specs + [_ANY] * len(held),
        out_shape=out_shape + held,
        input_output_aliases={n_args + len(r_names) + i: n_out + i for i in range(len(b_names))},
        scratch_shapes=scratch + [sems, sems],
        compiler_params=_params(*["arbitrary"] * len(grid)),
    )

    def run(*args):
        res = call(*args, *comm.reads.values(), *comm.bufs.values())
        comm.out = dict(zip(b_names + f_names, res[n_out:]))
        return res[0] if single else res[:n_out]

    return run


def _exchange(name, phases, reads=None, bufs=None, fresh=None):
    comm = _Carry([j for ph in phases for j in ph], reads, bufs, fresh)
    r_names, b_names, f_names = list(comm.reads), list(comm.bufs), list(comm.fresh)
    n_sems = sum(j.n_sems for j in comm.jobs)

    def body(*refs):
        hbm = dict(zip(r_names, refs[:len(r_names)]))
        k = len(r_names) + len(b_names)
        hbm.update(zip(b_names + f_names, refs[k:k + len(b_names) + len(f_names)]))
        send, recv = refs[-2:]
        pos = _position()
        started, base = [], 0
        for ph in phases:
            waits = []
            for job in ph:
                s, w = job.plan(hbm, pos, send, recv, base)
                base += job.n_sems
                for cp in s:
                    cp.start()
                started, waits = started + s, waits + w
            for cp in waits:
                cp.wait_recv()
        for cp in started:
            cp.wait_send()

    sems = pltpu.SemaphoreType.DMA((n_sems,))
    held = [jax.ShapeDtypeStruct(a.shape, a.dtype) for a in comm.bufs.values()] + list(comm.fresh.values())
    res = pl.pallas_call(
        body, name=name, in_specs=[_ANY] * (len(r_names) + len(b_names)), out_specs=[_ANY] * len(held),
        out_shape=held, input_output_aliases={len(r_names) + i: i for i in range(len(b_names))},
        scratch_shapes=[sems, sems],
    )(*comm.reads.values(), *comm.bufs.values())
    return dict(zip(b_names + f_names, res))


_HBM = pl.BlockSpec(memory_space=pltpu.HBM)
_SEM = pl.BlockSpec(memory_space=pltpu.SEMAPHORE)
_EFFECT = pltpu.SideEffectType.DATAFLOW_SIDE_EFFECTING


def _start_exchanges(name, groups):
    names = [list(arrays) for _, arrays in groups]
    first = [sum(len(ns) for ns in names[:g]) for g in range(len(groups))]
    n, ng = sum(len(ns) for ns in names), len(groups)

    def body(*refs):
        for g, (jobs, _) in enumerate(groups):
            hbm = dict(zip(names[g], refs[first[g]:first[g] + len(names[g])]))
            for cp in _plan_all(jobs, hbm, refs[n + 2 * g], refs[n + 2 * g + 1])[0]:
                cp.start()
        refs[-1][...] = jnp.zeros_like(refs[-1])

    given = [pltpu.with_memory_space_constraint(
        a if isinstance(a, jax.Array) else lax.empty(a.shape, a.dtype), pltpu.HBM)
        for _, arrays in groups for a in arrays.values()]
    sems = [pltpu.SemaphoreType.DMA((sum(j.n_sems for j in jobs),)) for jobs, _ in groups for _ in range(2)]
    res = pl.pallas_call(
        body, name=name,
        out_shape=(*sems, *[pltpu.HBM(a.shape, a.dtype) for a in given], jax.ShapeDtypeStruct((8, 128), F32)),
        in_specs=[_HBM] * n, out_specs=(*[_SEM] * (2 * ng), *[_HBM] * n, pl.BlockSpec(memory_space=pltpu.VMEM)),
        input_output_aliases={i: 2 * ng + i for i in range(n)},
        compiler_params=pltpu.CompilerParams(has_side_effects=_EFFECT),
    )(*given)
    held = res[2 * ng:2 * ng + n]
    states = [(names[g], groups[g][0], res[2 * g], res[2 * g + 1], held[first[g]:first[g] + len(names[g])])
              for g in range(ng)]
    return states, res[-1]


def _start_exchange(name, jobs, arrays):
    states, token = _start_exchanges(name, [(jobs, arrays)])
    return states[0], token


def _finish_exchange(name, state, after):
    names, jobs, send_sem, recv_sem, held = state
    n = len(names)

    def body(*refs):
        hbm = dict(zip(names, refs[:n]))
        send, recv = refs[n:n + 2]
        starts, waits = _plan_all(jobs, hbm, send, recv)
        for cp in waits:
            cp.wait_recv()
        for cp in starts:
            cp.wait_send()

    res = pl.pallas_call(
        body, name=name, out_shape=tuple(pltpu.HBM(a.shape, a.dtype) for a in held),
        in_specs=[_HBM] * n + [_SEM, _SEM, _ANY], out_specs=tuple([_HBM] * n),
        input_output_aliases={i: i for i in range(n)},
        compiler_params=pltpu.CompilerParams(has_side_effects=_EFFECT),
    )(*held, send_sem, recv_sem, after)
    return dict(zip(names, res))


def _row_tile(rows, bytes_per_row):
    best = 16
    for t in range(16, rows + 1, 16):
        if rows % t == 0 and t * bytes_per_row <= 6 * 1024 * 1024:
            best = t
    return best


def _rowwise(fn, ins, out_dtypes, name, after=None):
    rows, cols = ins[0].shape
    per_row = sum(cols * a.dtype.itemsize for a in ins) + sum(cols * jnp.dtype(d).itemsize for d in out_dtypes)
    tr = _row_tile(rows, per_row)
    n_in = len(ins)

    def body(*refs):
        outs = fn(*[r[...] for r in refs[:n_in]])
        for o_ref, o in zip(refs[-len(out_dtypes):], outs):
            o_ref[...] = o.astype(o_ref.dtype)

    tile = pl.BlockSpec((tr, cols), lambda i: (i, 0))
    behind = [] if after is None else [after]
    return pl.pallas_call(
        body, name=name, grid=(rows // tr,),
        in_specs=[tile] * n_in + [pl.BlockSpec((8, 128), lambda i: (0, 0))] * len(behind),
        out_specs=[tile] * len(out_dtypes),
        out_shape=[jax.ShapeDtypeStruct((rows, cols), d) for d in out_dtypes],
        compiler_params=_params("parallel"),
    )(*ins, *behind)


def _tiled(fn, name, grid, pos, ins, outs):
    n_in = len(ins)

    def body(pos_ref, *refs):
        res = fn(*[r[...] for r in refs[:n_in]])
        for o_ref, o in zip(refs[n_in:], res):
            o_ref[...] = o.astype(o_ref.dtype)

    return pl.pallas_call(
        body, name=name,
        grid_spec=pltpu.PrefetchScalarGridSpec(
            num_scalar_prefetch=1, grid=grid,
            in_specs=[pl.BlockSpec(bs, im) for _, bs, im in ins],
            out_specs=[pl.BlockSpec(bs, im) for _, _, bs, im in outs]),
        out_shape=[jax.ShapeDtypeStruct(s, d) for s, d, _, _ in outs],
        compiler_params=_params("parallel"),
    )(pos, *[a for a, _, _ in ins])


def _adamw(w, g, m, v):
    m = ADAM_B1 * m + (1.0 - ADAM_B1) * g
    v = ADAM_B2 * v + (1.0 - ADAM_B2) * (g * g)
    m_hat = m / (1.0 - ADAM_B1 ** ADAM_STEP)
    v_hat = v / (1.0 - ADAM_B2 ** ADAM_STEP)
    return -ADAM_LR * (m_hat / (jnp.sqrt(v_hat) + ADAM_EPS) + ADAM_WD * w), m, v


def _adamw_small(params):
    n = len(params)

    def body(*refs):
        for k in range(n):
            w, g, m, v = (r[...] for r in refs[4 * k:4 * k + 4])
            for o_ref, o in zip(refs[4 * n + 3 * k:4 * n + 3 * k + 3], _adamw(w, g, m, v)):
                o_ref[...] = o

    flat = [a for p in params for a in p]
    return pl.pallas_call(
        body, name="adamw_small",
        out_shape=[jax.ShapeDtypeStruct(p[0].shape, F32) for p in params for _ in range(3)],
    )(*flat)


class _Layout:
    def __init__(self, rows, cols, stacked):
        self.rows, self.cols, self.stacked = rows, cols, stacked

    def whole(self, rows=None):
        r = self.rows if rows is None else rows
        return (N_CHIPS, r, self.cols) if self.stacked else (r, N_CHIPS * self.cols)

    def part_rows(self, h, q=0, nq=1):
        n = self.rows // 2 // nq
        return pl.ds(pl.multiple_of(h * (self.rows // 2) + q * n, 16), n)

    def half_rows(self, h):
        return self.part_rows(h)

    def block(self, ref, p, rows=slice(None)):
        if self.stacked:
            return ref.at[p, rows, :]
        return ref.at[rows, pl.ds(pl.multiple_of(p * self.cols, 128), self.cols)]

    def all_chips(self, ref, rows):
        return ref.at[:, rows, :] if self.stacked else ref.at[rows, :]


BIG = (
    _Layout(IN_SHARD, D_MODEL, True),
    _Layout(ATTN_W, D_MODEL // N_CHIPS, False),
    _Layout(CONV_W, D_MODEL // N_CHIPS, False),
    _Layout(D_MODEL // N_CHIPS, D_MODEL, True),
    _Layout(D_MODEL, FF2 // N_CHIPS, False),
    _Layout(D_FF // N_CHIPS, D_MODEL, True),
)
N_BIG = len(BIG)
_ANY = pl.BlockSpec(memory_space=pl.ANY)


def _position():
    x, y, c = lax.axis_index("x"), lax.axis_index("y"), lax.axis_index("c")
    return x, y, c, 2 * x + y


def _core_of_chip(p, c):
    return (p >> 1, p & 1, c)


def _place_cast(shard, lay, pos, name, after=None):
    rows, cols = shard.shape
    tr = _row_tile(rows, cols * 6)
    if lay.stacked:
        out = (lay.whole(), BF16, (None, tr, cols), lambda i, pos: (pos[0], i, 0))
    else:
        out = (lay.whole(), BF16, (tr, cols), lambda i, pos: (i, pos[0]))
    ins = [(shard, (tr, cols), lambda i, pos: (i, 0))]
    if after is not None:
        ins.append((after, (8, 128), lambda i, pos: (0, 0)))
    return _tiled(lambda a, *_: (a,), name, (rows // tr,), pos, ins, [out])[0]


def _remote(src, dst, send, recv, k, device):
    return pltpu.make_async_remote_copy(src_ref=src, dst_ref=dst, send_sem=send.at[k], recv_sem=recv.at[k],
                                        device_id=device, device_id_type=MESH)


def _arrival(dst, send, recv, k, me):
    return _remote(dst, dst, send, recv, k, me)


def _gather_ici(lay, name, q=0, nq=1):
    def plan(hbm, pos, send, recv, base):
        x, y, c, me = pos
        rows = lay.part_rows(c, q, nq)
        mine = lay.block(hbm[name], me, rows)
        starts = [_remote(mine, mine, send, recv, base + d - 1, _core_of_chip(me ^ d, c)) for d in (1, 2, 3)]
        waits = [_arrival(lay.block(hbm[name], me ^ d, rows), send, recv, base + d - 1, (x, y, c)) for d in (1, 2, 3)]
        return starts, waits
    return _Job(3, plan)


def _gather_d2d(lay, name, q=0, nq=1):
    def plan(hbm, pos, send, recv, base):
        x, y, c, me = pos
        starts, waits = [], []
        for d in (1, 2, 3):
            got = lay.block(hbm[name], me ^ d, lay.part_rows(c, q, nq))
            starts.append(_remote(got, got, send, recv, base + d - 1, (x, y, 1 - c)))
            waits.append(_arrival(lay.block(hbm[name], me ^ d, lay.part_rows(1 - c, q, nq)), send, recv, base + d - 1,
                                  (x, y, c)))
        return starts, waits
    return _Job(3, plan)


def _rs_pair(lay, grad, theirs):
    def plan(hbm, pos, send, recv, base):
        x, y, c, _ = pos
        out = _remote(lay.all_chips(hbm[grad], lay.half_rows(1 - c)), hbm[theirs], send, recv, base, (x, y, 1 - c))
        return [out], [_arrival(hbm[theirs], send, recv, base, (x, y, c))]
    return _Job(1, plan)


def _rs_chips(lay, sums, slots):
    def plan(hbm, pos, send, recv, base):
        x, y, c, me = pos
        starts = [_remote(lay.block(hbm[sums], me ^ d), hbm[slots].at[me], send, recv, base + d - 1,
                          _core_of_chip(me ^ d, c)) for d in (1, 2, 3)]
        waits = [_arrival(hbm[slots].at[me ^ d], send, recv, base + d - 1, (x, y, c)) for d in (1, 2, 3)]
        return starts, waits
    return _Job(3, plan)


def _rs_share(lay, shard):
    def plan(hbm, pos, send, recv, base):
        x, y, c, _ = pos
        mine = hbm[shard].at[lay.half_rows(c), :]
        other = hbm[shard].at[lay.half_rows(1 - c), :]
        return [_remote(mine, mine, send, recv, base, (x, y, 1 - c))], [_arrival(other, send, recv, base, (x, y, c))]
    return _Job(1, plan)


def _slots_shape(lay):
    return jax.ShapeDtypeStruct((N_CHIPS, lay.rows // 2, lay.cols), BF16)


def _theirs_shape(lay):
    return jax.ShapeDtypeStruct(lay.whole(lay.rows // 2), BF16)


def _pair_sum(grad, theirs, lay, pos, name):
    half = lay.rows // 2
    add = lambda a, b: (a.astype(F32) + b.astype(F32),)
    if lay.stacked:
        tr = _row_tile(half, lay.cols * 6)
        nt = half // tr
        flat = lambda a: a.reshape(-1, lay.cols)
        mine = lambda t, pos: ((t // nt) * (2 * nt) + pos[1] * nt + t % nt, 0)
        grid, blk = (N_CHIPS * nt,), (tr, lay.cols)
        grad, theirs = flat(grad), flat(theirs)
    else:
        tr = _row_tile(half, N_CHIPS * lay.cols * 6)
        nt = half // tr
        mine = lambda t, pos: (pos[1] * nt + t, 0)
        grid, blk = (nt,), (tr, N_CHIPS * lay.cols)
    same = lambda t, pos: (t, 0)
    out = _tiled(add, name, grid, pos, [(grad, blk, mine), (theirs, blk, same)], [(theirs.shape, BF16, blk, same)])[0]
    return out.reshape(lay.whole(half))


def _chip_sum(sums, slots, lay, pos, name, after=None):
    half = lay.rows // 2
    tr = _row_tile(half, lay.cols * 12)
    nt = half // tr
    blk3 = (None, tr, lay.cols)
    if lay.stacked:
        own = (sums, blk3, lambda i, pos: (pos[0], i, 0))
    else:
        own = (sums, (tr, lay.cols), lambda i, pos: (i, pos[0]))
    others = [(slots, blk3, functools.partial(lambda d, i, pos: (pos[0] ^ d, i, 0), d)) for d in (1, 2, 3)]

    def add(a, b1, b2, b3, *_):
        return (((a.astype(F32) + b1.astype(F32)) + b2.astype(F32)) + b3.astype(F32),)

    if after is not None:
        others.append((after, (8, 128), lambda i, pos: (0, 0)))
    return _tiled(add, name, (nt,), pos, [own] + others,
                  [((lay.rows, lay.cols), F32, (tr, lay.cols), lambda i, pos: (pos[1] * nt + i, 0))])[0]


N_DEV = 8


def _to_all(src, slots):
    def plan(hbm, pos, send, recv, base):
        x, y, c, _ = pos
        idx = 4 * x + 2 * y + c
        starts = [_remote(hbm[src], hbm[slots].at[idx], send, recv, base + k - 1,
                          (x ^ (k >> 2), y ^ ((k >> 1) & 1), c ^ (k & 1))) for k in range(1, N_DEV)]
        waits = [_arrival(hbm[slots].at[idx ^ k], send, recv, base + k - 1, (x, y, c)) for k in range(1, N_DEV)]
        return starts, waits
    return _Job(N_DEV - 1, plan)


def _sum_slots(own, slots, pos):
    def body(pos_ref, own_ref, slots_ref, o_ref):
        idx = 2 * pos_ref[0] + pos_ref[1]
        term = lambda q: jnp.where(idx == q, own_ref[...], slots_ref[q])
        acc = term(0)
        for q in range(1, N_DEV):
            acc = acc + term(q)
        o_ref[...] = acc

    return pl.pallas_call(
        body, name="sum_small", out_shape=jax.ShapeDtypeStruct(own.shape, F32),
        in_specs=[pl.BlockSpec(memory_space=pltpu.SMEM), pl.BlockSpec(memory_space=pltpu.VMEM),
                  pl.BlockSpec(memory_space=pltpu.VMEM)],
    )(pos, own, slots)


def _pack_rows(parts):
    padded = [jnp.pad(a, ((0, -a.shape[0] % 8), (0, 0))) for a in parts]
    starts = [sum(p.shape[0] for p in padded[:k]) for k in range(len(padded))]
    return jnp.concatenate(padded, axis=0), starts


def kernel(x, mix_norm, w_in, b_in, sinks, conv_w, w_attn_branch, w_conv_branch, w_out, ffn_norm, w_up, ffn_conv_w, w_down, final_norm, loss_target, m_mix_norm, m_w_in, m_b_in, m_sinks, m_conv_w, m_w_attn_branch, m_w_conv_branch, m_w_out, m_ffn_norm, m_w_up, m_ffn_conv_w, m_w_down, m_final_norm, v_mix_norm, v_w_in, v_b_in, v_sinks, v_conv_w, v_w_attn_branch, v_w_conv_branch, v_w_out, v_ffn_norm, v_w_up, v_ffn_conv_w, v_w_down, v_final_norm):
    me = 2 * lax.axis_index("x") + lax.axis_index("y")
    big_w = [w_in[0].T, w_attn_branch[0], w_conv_branch[0], w_out[0], w_up[0], w_down[0]]
    big_m = [m_w_in[0].T, m_w_attn_branch[0], m_w_conv_branch[0], m_w_out[0], m_w_up[0], m_w_down[0]]
    big_v = [v_w_in[0].T, v_w_attn_branch[0], v_w_conv_branch[0], v_w_out[0], v_w_up[0], v_w_down[0]]
    names = ("w_in", "w_ab", "w_cb", "w_out", "w_up", "w_down")

    pos = jnp.stack([me, lax.axis_index("c")]).astype(jnp.int32)

    lay = dict(zip(names, BIG))
    xs, target, sk = x[0], loss_target[0], sinks[0]
    s = xs.shape[0]
    tm, tm2, bk = min(256, s), min(512, s), min(1024, s)

    taps, (_, t0) = _pack_rows([conv_w[0], ffn_conv_w[0].reshape(3 * (FF2 // N_CHIPS // 128), 128)])
    placed = {"w_in": _place_cast(big_w[0], lay["w_in"], pos, "cast_w_in")}
    fly_in, started = _start_exchange("gather_in_start", [_gather_ici(lay["w_in"], "w_in")], {"w_in": placed["w_in"]})
    taps_flight, started = _start_exchange("taps_start", [_to_all("v", "slots")],
                                           {"v": taps + started[0:1], "slots": jnp.zeros((N_DEV, *taps.shape), F32)})
    for w, n in zip(big_w[1:], names[1:]):
        placed[n] = _place_cast(w, lay[n], pos, "cast_" + n, after=started)
    trio = ("w_ab", "w_cb", "w_out")
    (fly_trio, fly_up, fly_down), started = _start_exchanges("gather_rest_start", [
        ([_gather_ici(lay[n], n) for n in ws], {n: placed[n] for n in ws}) for ws in (trio, ("w_up",), ("w_down",))])

    got = _finish_exchange("gather_in_wait", fly_in, after=started)
    w_in_full = _exchange("gather_in_d2d", [[_gather_d2d(lay["w_in"], "w_in")]], bufs=got)["w_in"].reshape(IN_W, D_MODEL)
    xn, qkv, c3, gates = _inproj_fwd(xs, mix_norm, w_in_full, b_in, tm2)
    k2 = _Carry([_gather_d2d(lay[n], n) for n in trio], bufs=_finish_exchange("gather_trio_wait", fly_trio, after=qkv))
    attn = _attn_fwd(qkv, sk, comm=k2)
    w_ab, w_cb = k2.out["w_ab"], k2.out["w_cb"]
    w_out_full = k2.out["w_out"].reshape(D_MODEL, D_MODEL)
    k3 = _Carry([_gather_d2d(lay["w_up"], "w_up")], bufs=_finish_exchange("gather_up_wait", fly_up, after=attn))
    taps = _finish_exchange("taps_wait", taps_flight, after=attn)
    taps = lax.dynamic_update_slice(taps["slots"], taps["v"][None], (2 * me + lax.axis_index("c"), 0, 0))
    conv_full = taps[0::2, 0:3].transpose(1, 0, 2).reshape(3, CONV_W)
    ffn_cw_full = taps[0::2, t0:t0 + 33].reshape(N_CHIPS, 3, FF2 // N_CHIPS).transpose(1, 0, 2).reshape(3, FF2)
    conv, a, cv, merged, h1, hn = _mix_fwd(xs, attn, c3, gates, conv_full, w_ab, w_cb, w_out_full, ffn_norm, tm2, comm=k3)
    w_up_full = k3.out["w_up"]
    w_down_full = _exchange("gather_down_d2d", [[_gather_d2d(lay["w_down"], "w_down")]],
                            bufs=_finish_exchange("gather_down_wait", fly_down, after=hn))["w_down"].reshape(D_FF, D_MODEL)
    u, up, act, dh2, loss_part, g_fn = _ffn_fwd_loss(hn, h1, w_up_full, ffn_cw_full, w_down_full,
                                                     final_norm[None, :], target, tm)

    grads, sums, slots = {}, {}, {}

    def pair(*ws):
        return _Carry([_rs_pair(lay[n], "g_" + n, "t_" + n) for n in ws], reads={"g_" + n: grads[n] for n in ws},
                      fresh={"t_" + n: _theirs_shape(lay[n]) for n in ws})

    def chips(*ws, also=None):
        k = _Carry([_rs_chips(lay[n], "s_" + n, "r_" + n) for n in ws], reads={"s_" + n: sums[n] for n in ws},
                   fresh={"r_" + n: _slots_shape(lay[n]) for n in ws})
        if also is not None:
            k = _Carry(k.jobs + also.jobs, {**k.reads, **also.reads}, None, {**k.fresh, **also.fresh})
        return k

    def pair_sums(k, *ws):
        for n in ws:
            sums[n] = _pair_sum(grads[n], k.out["t_" + n], lay[n], pos, "pair_sum_" + n)

    def take_slots(k, *ws):
        for n in ws:
            slots[n] = k.out["r_" + n]

    du, dh1, g_fcw, g_g2 = _ffn_bwd(dh2, u, up, h1, w_up_full, ffn_cw_full, w_down_full, ffn_norm, tm)
    grads["w_down"] = _wgrad(act, dh2, D_FF // 2, D_MODEL, bk, "wgrad_down").reshape(lay["w_down"].whole())
    k4 = pair("w_down")
    grads["w_up"] = _wgrad(hn, du, D_MODEL, FF2 // 4, bk, "wgrad_up", comm=k4)
    pair_sums(k4, "w_down")
    k5 = chips("w_down", also=pair("w_up"))
    dattn, da, dcv, dc3, dgt, g_cw = _mix_bwd(dh1, gates, a, cv, c3, conv_full, w_ab, w_cb, w_out_full, tm2, comm=k5)
    take_slots(k5, "w_down")
    pair_sums(k5, "w_up")
    grads["w_out"] = _wgrad(merged, dh1, D_MODEL, D_MODEL, bk, "wgrad_out").reshape(lay["w_out"].whole())
    grads["w_ab"] = _wgrad(attn, da, ATTN_W, D_MODEL, bk, "wgrad_ab")
    grads["w_cb"] = _wgrad(conv, dcv, CONV_W, D_MODEL, bk, "wgrad_cb")
    k6 = chips("w_up", also=pair("w_out", "w_ab", "w_cb"))
    dq, dk, dv, g_sk = _attn_bwd(qkv, sk, attn, dattn, comm=k6)
    take_slots(k6, "w_up")
    pair_sums(k6, "w_out", "w_ab", "w_cb")
    grad_x, dproj, g_b, g_g1 = _inproj_bwd(dq, dk, dv, dc3, dgt, w_in_full, xs, dh1, mix_norm, tm2)

    parts = [loss_part, g_g1, g_b, jnp.pad(g_sk[:, 0], (0, 120))[None, :], g_cw, g_g2, g_fcw, g_fn]
    packed, at = _pack_rows([p.reshape(-1, 128) for p in parts])
    small_flight, started = _start_exchange("small_start", [_to_all("v", "slots")],
                                            {"v": packed, "slots": jnp.zeros((N_DEV, *packed.shape), F32)})
    k8 = chips("w_out", "w_ab", "w_cb")
    k8.reads["after"] = started
    grads["w_in"] = _wgrad_in(xn, dproj, min(512, s), comm=k8).reshape(lay["w_in"].whole())
    take_slots(k8, "w_out", "w_ab", "w_cb")
    others = names[1:]
    in_flight, started = _start_exchange("rs_pair_in_start", [_rs_pair(lay["w_in"], "g", "t")],
                                         {"g": grads["w_in"], "t": _theirs_shape(lay["w_in"])})
    halves = {n: _chip_sum(sums[n], slots[n], lay[n], pos, "chip_sum_" + n, after=started) for n in ("w_up", "w_down")}
    landed = _finish_exchange("rs_pair_in_wait", in_flight, after=halves["w_down"])
    sums["w_in"] = _pair_sum(landed["g"], landed["t"], lay["w_in"], pos, "pair_sum_w_in")
    in_flight, started = _start_exchange("rs_chips_in_start", [_rs_chips(lay["w_in"], "s", "r")],
                                         {"s": sums["w_in"], "r": _slots_shape(lay["w_in"])})
    for n in trio:
        halves[n] = _chip_sum(sums[n], slots[n], lay[n], pos, "chip_sum_" + n, after=started)
    shared = _exchange("share_halves", [[_rs_share(lay[n], n) for n in others]], bufs=halves)
    w_of, m_of, v_of = dict(zip(names, big_w)), dict(zip(names, big_m)), dict(zip(names, big_v))

    def adam(n, g, after=None):
        return _rowwise(lambda w, g, m, v: (g, *_adamw(w, g, m, v)), [w_of[n], g, m_of[n], v_of[n]], [F32] * 4,
                        "adamw_" + n, after=after)

    new_of, last = {}, None
    for n in ("w_up", "w_down", "w_out", "w_ab", "w_cb"):
        new_of[n] = adam(n, shared[n], last)
        last = new_of[n][1]

    arrived = _finish_exchange("small_wait", small_flight, after=last)
    total = _sum_slots(arrived["v"], arrived["slots"], pos)
    part = lambda k: total[at[k]:at[k] + parts[k].size // 128].reshape(parts[k].shape)
    loss = total[0, 0]
    g_mix, g_b, g_g2, g_fn = part(1), part(2), part(5), part(7)
    g_sk = part(3)[:, 0:N_HEADS]
    g_cw = lax.dynamic_slice(part(4), (0, me * 128), (3, 128))
    g_fcw = lax.dynamic_slice(part(6), (0, me * (FF2 // N_CHIPS)), (3, FF2 // N_CHIPS))
    small_p = [
        (mix_norm, g_mix, m_mix_norm, v_mix_norm), (b_in, g_b, m_b_in, v_b_in), (sinks, g_sk, m_sinks, v_sinks),
        (conv_w[0], g_cw, m_conv_w[0], v_conv_w[0]), (ffn_norm, g_g2, m_ffn_norm, v_ffn_norm),
        (ffn_conv_w[0], g_fcw, m_ffn_conv_w[0], v_ffn_conv_w[0]),
        (final_norm[None, :], g_fn, m_final_norm[None, :], v_final_norm[None, :])]
    small_new = _adamw_small(small_p)
    small_new = [small_new[3 * k:3 * k + 3] for k in range(len(small_p))]

    landed = _finish_exchange("rs_chips_in_wait", in_flight, after=small_new[0][0])
    half_in = _chip_sum(landed["s"], landed["r"], lay["w_in"], pos, "chip_sum_w_in")
    shared["w_in"] = _exchange("share_in", [[_rs_share(lay["w_in"], "w_in")]], bufs={"w_in": half_in})["w_in"]
    new_of["w_in"] = adam("w_in", shared["w_in"])
    big_g = [new_of[n][0] for n in names]
    big_new = [new_of[n][1:] for n in names]

    order = [("s", 0), ("b", 0), ("s", 1), ("s", 2), ("s", 3), ("b", 1), ("b", 2), ("b", 3), ("s", 4), ("b", 4),
             ("s", 5), ("b", 5), ("s", 6)]
    shapes = [mix_norm.shape, w_in.shape, b_in.shape, sinks.shape, conv_w.shape, w_attn_branch.shape,
              w_conv_branch.shape, w_out.shape, ffn_norm.shape, w_up.shape, ffn_conv_w.shape, w_down.shape,
              final_norm.shape]
    small_g = [p[1] for p in small_p]
    big_g[0] = big_g[0].T
    big_new[0] = [a.T for a in big_new[0]]
    out_g = [(small_g[k] if kind == "s" else big_g[k]).reshape(shp) for (kind, k), shp in zip(order, shapes)]
    news = [[(small_new[k][j] if kind == "s" else big_new[k][j]).reshape(shp) for (kind, k), shp in zip(order, shapes)]
            for j in range(3)]
    return (loss, grad_x[None], *out_g, *news[0], *news[1], *news[2])
```

```python
import functools

import jax
import jax.numpy as jnp
from jax import lax
from jax.experimental import pallas as pl
from jax.experimental.pallas import tpu as pltpu

F32 = jnp.float32
BF16 = jnp.bfloat16

D_MODEL = 1024
HEAD_DIM = 64
N_HEADS = 8
N_KV_HEADS = 2
GROUP = N_HEADS // N_KV_HEADS
BLOCK = 128
ATTN_SCALE = HEAD_DIM ** -0.5
ATTN_W = N_HEADS * HEAD_DIM
KV_W = N_KV_HEADS * HEAD_DIM
CONV_W = 512
QKV_W = ATTN_W + 2 * KV_W
C3_W = 3 * CONV_W
GATES_W = 2 * D_MODEL
IN_W = QKV_W + C3_W + GATES_W
D_FF = 2816
FF2 = 2 * D_FF
NORM_EPS = 1e-5
N_CHIPS = 4
IN_SHARD = IN_W // N_CHIPS
NEG = -1e30

ADAM_LR = 0.001
ADAM_B1 = 0.9
ADAM_B2 = 0.999
ADAM_EPS = 1e-08
ADAM_WD = 0.01
ADAM_STEP = 10

VMEM_LIMIT = 56 * 1024 * 1024
MESH = pl.DeviceIdType.MESH

NT = (((1,), (1,)), ((), ()))
TN = (((0,), (0,)), ((), ()))


def _params(*sem):
    return pltpu.CompilerParams(dimension_semantics=sem, vmem_limit_bytes=VMEM_LIMIT)


def _resident(shape):
    return pl.BlockSpec(shape, lambda *_: (0,) * len(shape), pipeline_mode=pl.Buffered(1))


def _sigmoid(v):
    return 0.5 * jnp.tanh(0.5 * v) + 0.5


def _rstd(v):
    return lax.rsqrt(jnp.mean(v * v, axis=-1, keepdims=True) + NORM_EPS)


def _rms_bwd(dy, v, rstd, g):
    vhat = v * rstd
    t = dy * g
    return rstd * (t - vhat * jnp.mean(t * vhat, axis=-1, keepdims=True)), dy * vhat


def _taps(z, cw):
    return cw[2:3] * z + cw[1:2] * pltpu.roll(z, 1, 0) + cw[0:1] * pltpu.roll(z, 2, 0)


def _causal_conv(z, prev, cw):
    edge = _taps(jnp.concatenate([prev, z[0:8]], axis=0), cw)
    return jnp.concatenate([edge[8:16], _taps(z, cw)[8:]], axis=0)


def _rows_after(z, nxt):
    n = z.shape[0]
    edge = jnp.concatenate([z[n - 8:n], nxt], axis=0)
    return tuple(jnp.concatenate([pltpu.roll(z, n - k, 0)[:n - 8], pltpu.roll(edge, 16 - k, 0)[0:8]], axis=0)
                 for k in (1, 2))


def _rows_before(z, prev):
    edge = jnp.concatenate([prev, z[0:8]], axis=0)
    return tuple(jnp.concatenate([pltpu.roll(edge, k, 0)[8:16], pltpu.roll(z, k, 0)[8:]], axis=0) for k in (1, 2))


def _inproj_fwd(x, g1, w_in, b_in, tm, comm=None):
    s = x.shape[0]

    def body(x_ref, g_ref, w_ref, b_ref, xn_ref, qkv_ref, c3_ref, gt_ref):
        xf = x_ref[...]
        xn = (xf * _rstd(xf) * g_ref[...]).astype(BF16)
        xn_ref[...] = xn

        def seg(a, b):
            return lax.dot_general(xn, w_ref[a:b, :], NT, preferred_element_type=F32) + b_ref[:, a:b]

        qkv_ref[...] = seg(0, QKV_W).astype(BF16)
        c3_ref[...] = seg(QKV_W, QKV_W + C3_W)
        gt_ref[...] = seg(QKV_W + C3_W, IN_W)

    row = lambda w: pl.BlockSpec((tm, w), lambda i: (i, 0))
    return _call(
        comm, body, name="inproj_fwd", grid=(s // tm,),
        in_specs=[row(D_MODEL), _resident((1, D_MODEL)), _resident((IN_W, D_MODEL)), _resident((1, IN_W))],
        out_specs=[row(D_MODEL), row(QKV_W), row(C3_W), row(GATES_W)],
        out_shape=[jax.ShapeDtypeStruct((s, D_MODEL), BF16), jax.ShapeDtypeStruct((s, QKV_W), BF16),
                   jax.ShapeDtypeStruct((s, C3_W), F32), jax.ShapeDtypeStruct((s, GATES_W), F32)],
        compiler_params=_params("parallel"),
    )(x, g1, w_in, b_in)


def _attn_bias():
    qi = (jnp.arange(GROUP * BLOCK) % BLOCK)[:, None]
    kj = jnp.arange(2 * BLOCK)[None, :]
    band = (kj > qi) & (kj <= qi + BLOCK)
    return jnp.stack([jnp.where(band & (kj >= BLOCK), 0.0, NEG), jnp.where(band, 0.0, NEG)]).astype(F32)


def _attn_bias_spec():
    return pl.BlockSpec((None, GROUP * BLOCK, 2 * BLOCK), lambda i: (jnp.minimum(i, 1), 0, 0))


def _sink_column(sk_ref, h):
    rows = lax.broadcasted_iota(jnp.int32, (GROUP * BLOCK, 1), 0)
    col = jnp.full((GROUP * BLOCK, 1), sk_ref[h * GROUP], F32)
    for g in range(1, GROUP):
        col = jnp.where(rows >= g * BLOCK, sk_ref[h * GROUP + g], col)
    return col


def _stack_heads(t, h):
    return jnp.concatenate(
        [t[:, (h * GROUP + g) * HEAD_DIM:(h * GROUP + g + 1) * HEAD_DIM] for g in range(GROUP)], axis=0)


def _unstack_heads(per_kv):
    return jnp.concatenate(
        [t[g * BLOCK:(g + 1) * BLOCK] for t in per_kv for g in range(GROUP)], axis=1)


def _attn_specs(nb):
    cur = lambda i: jnp.minimum(i, nb - 1)
    prev = lambda i: jnp.maximum(jnp.minimum(i, nb - 1) - 1, 0)
    q = pl.BlockSpec((BLOCK, ATTN_W), lambda i: (cur(i), 0))
    kp = pl.BlockSpec((BLOCK, KV_W), lambda i: (prev(i), ATTN_W // KV_W))
    kc = pl.BlockSpec((BLOCK, KV_W), lambda i: (cur(i), ATTN_W // KV_W))
    vp = pl.BlockSpec((BLOCK, KV_W), lambda i: (prev(i), ATTN_W // KV_W + 1))
    vc = pl.BlockSpec((BLOCK, KV_W), lambda i: (cur(i), ATTN_W // KV_W + 1))
    return q, kp, kc, vp, vc


def _attn_fwd(qkv, sinks, comm=None):
    s = qkv.shape[0]
    nb = s // BLOCK

    def body(sk_ref, bias_ref, q_ref, kp_ref, kc_ref, vp_ref, vc_ref, o_ref):
        bias = bias_ref[...]
        q, kp, kc, vp, vc = q_ref[...], kp_ref[...], kc_ref[...], vp_ref[...], vc_ref[...]
        outs = []
        for h in range(N_KV_HEADS):
            hs = slice(h * HEAD_DIM, (h + 1) * HEAD_DIM)
            k2 = jnp.concatenate([kp[:, hs], kc[:, hs]], axis=0)
            v2 = jnp.concatenate([vp[:, hs], vc[:, hs]], axis=0)
            sc = lax.dot_general(_stack_heads(q, h), k2, NT, preferred_element_type=F32) * ATTN_SCALE + bias
            sink = _sink_column(sk_ref, h)
            m = jnp.maximum(jnp.max(sc, axis=1, keepdims=True), sink)
            p = jnp.exp(sc - m)
            den = jnp.sum(p, axis=1, keepdims=True) + jnp.exp(sink - m)
            outs.append(jnp.dot(p.astype(BF16), v2, preferred_element_type=F32) / den)
        o_ref[...] = _unstack_heads(outs).astype(BF16)

    return _call(
        comm, body, name="attn_fwd", grid=(nb,),
        in_specs=[pl.BlockSpec(memory_space=pltpu.SMEM), _attn_bias_spec(), *_attn_specs(nb)],
        out_specs=pl.BlockSpec((BLOCK, ATTN_W), lambda i: (i, 0)),
        out_shape=jax.ShapeDtypeStruct((s, ATTN_W), BF16),
        compiler_params=_params("parallel"),
    )(sinks, _attn_bias(), qkv, qkv, qkv, qkv, qkv)


def _mix_fwd(x, attn, c3, gates, conv_w, w_ab, w_cb, w_out, g2, tm, comm=None):
    s = x.shape[0]

    def body(x_ref, at_ref, c3_ref, gt_ref, cw_ref, wab_ref, wcb_ref, wo_ref, g_ref,
             conv_ref, a_ref, cv_ref, mg_ref, h1_ref, hn_ref, carry_ref):
        @pl.when(pl.program_id(0) == 0)
        def _():
            carry_ref[...] = jnp.zeros_like(carry_ref)

        c3v = c3_ref[...]
        cb, cc, cx = c3v[:, :CONV_W], c3v[:, CONV_W:2 * CONV_W], c3v[:, 2 * CONV_W:]
        z = cc * cx
        cz = _causal_conv(z, carry_ref[...], cw_ref[...])
        carry_ref[...] = z[tm - 8:tm]
        conv = (cb * cz).astype(BF16)
        conv_ref[...] = conv
        a = jnp.dot(at_ref[...], wab_ref[...], preferred_element_type=F32)
        cv = jnp.dot(conv, wcb_ref[...], preferred_element_type=F32)
        a_ref[...] = a.astype(BF16)
        cv_ref[...] = cv.astype(BF16)
        gt = gt_ref[...]
        merged = (_sigmoid(gt[:, :D_MODEL]) * a + _sigmoid(gt[:, D_MODEL:]) * cv).astype(BF16)
        mg_ref[...] = merged
        h1 = x_ref[...] + jnp.dot(merged, wo_ref[...], preferred_element_type=F32)
        h1_ref[...] = h1
        hn_ref[...] = (h1 * _rstd(h1) * g_ref[...]).astype(BF16)

    row = lambda w: pl.BlockSpec((tm, w), lambda i: (i, 0))
    return _call(
        comm, body, name="mix_fwd", grid=(s // tm,),
        in_specs=[row(D_MODEL), row(ATTN_W), row(C3_W), row(GATES_W), _resident((3, CONV_W)),
                  _resident((ATTN_W, D_MODEL)), _resident((CONV_W, D_MODEL)), _resident((D_MODEL, D_MODEL)),
                  _resident((1, D_MODEL))],
        out_specs=[row(CONV_W), row(D_MODEL), row(D_MODEL), row(D_MODEL), row(D_MODEL), row(D_MODEL)],
        out_shape=[jax.ShapeDtypeStruct((s, CONV_W), BF16), jax.ShapeDtypeStruct((s, D_MODEL), BF16),
                   jax.ShapeDtypeStruct((s, D_MODEL), BF16), jax.ShapeDtypeStruct((s, D_MODEL), BF16),
                   jax.ShapeDtypeStruct((s, D_MODEL), F32), jax.ShapeDtypeStruct((s, D_MODEL), BF16)],
        scratch_shapes=[pltpu.VMEM((8, CONV_W), F32)],
        compiler_params=_params("arbitrary"),
    )(x, attn, c3, gates, conv_w, w_ab, w_cb, w_out, g2)


def _ffn_fwd_loss(hn, h1, w_up, ffn_cw, w_down, g3, target, tm):
    s = hn.shape[0]

    def body(hn_ref, h1_ref, wu_ref, cw_ref, wd_ref, g_ref, t_ref,
             u_ref, up_ref, act_ref, dh2_ref, loss_ref, gfn_ref, carry_ref):
        @pl.when(pl.program_id(0) == 0)
        def _():
            carry_ref[...] = jnp.zeros_like(carry_ref)
            loss_ref[...] = jnp.zeros_like(loss_ref)
            gfn_ref[...] = jnp.zeros_like(gfn_ref)

        u = jnp.dot(hn_ref[...], wu_ref[...], preferred_element_type=F32)
        u_ref[...] = u.astype(BF16)
        up = _causal_conv(u, carry_ref[...], cw_ref[...])
        up_ref[...] = up
        carry_ref[...] = u[tm - 8:tm]
        gate, val = up[:, :D_FF], up[:, D_FF:]
        act = (gate * _sigmoid(gate) * val).astype(BF16)
        act_ref[...] = act
        h2 = h1_ref[...] + jnp.dot(act, wd_ref[...], preferred_element_type=F32)
        rstd = _rstd(h2)
        g = g_ref[...]
        err = h2 * rstd * g - t_ref[...]
        loss_ref[...] += jnp.sum(err * err) * (0.5 / D_MODEL)
        dh2, dg = _rms_bwd(err * (1.0 / D_MODEL), h2, rstd, g)
        dh2_ref[...] = dh2
        gfn_ref[...] += jnp.sum(dg, axis=0, keepdims=True)

    row = lambda w: pl.BlockSpec((tm, w), lambda i: (i, 0))
    acc = lambda w: pl.BlockSpec((1, w), lambda i: (0, 0))
    return pl.pallas_call(
        body, name="ffn_fwd_loss", grid=(s // tm,),
        in_specs=[row(D_MODEL), row(D_MODEL), _resident((D_MODEL, FF2)), _resident((3, FF2)),
                  _resident((D_FF, D_MODEL)), _resident((1, D_MODEL)), row(D_MODEL)],
        out_specs=[row(FF2), row(FF2), row(D_FF), row(D_MODEL), acc(128), acc(D_MODEL)],
        out_shape=[jax.ShapeDtypeStruct((s, FF2), BF16), jax.ShapeDtypeStruct((s, FF2), F32),
                   jax.ShapeDtypeStruct((s, D_FF), BF16),
                   jax.ShapeDtypeStruct((s, D_MODEL), F32), jax.ShapeDtypeStruct((1, 128), F32),
                   jax.ShapeDtypeStruct((1, D_MODEL), F32)],
        scratch_shapes=[pltpu.VMEM((8, FF2), F32)],
        compiler_params=_params("arbitrary"),
    )(hn, h1, w_up, ffn_cw, w_down, g3, target)


def _ffn_bwd(dh2, up, h1, w_up, ffn_cw, w_down, g2, tm):
    s = dh2.shape[0]
    nt = s // tm

    def body(dh2_ref, up_ref, h1_ref, wu_ref, cw_ref, wd_ref, g_ref,
             dup_ref, du_ref, dh1_ref, gg_ref, carry_ref):
        @pl.when(pl.program_id(0) == 0)
        def _():
            carry_ref[...] = jnp.zeros_like(carry_ref)
            gg_ref[...] = jnp.zeros_like(gg_ref)

        dh2v = dh2_ref[...]
        dact = lax.dot_general(dh2v.astype(BF16), wd_ref[...], NT, preferred_element_type=F32)
        upv = up_ref[...]
        gate, val = upv[:, :D_FF], upv[:, D_FF:]
        sg = _sigmoid(gate)
        dval = dact * (gate * sg)
        dgate = dact * val * (sg * (1.0 + gate * (1.0 - sg)))
        dup = jnp.concatenate([dgate, dval], axis=1)
        dup_ref[...] = dup.astype(BF16)
        dup1, dup2 = _rows_after(dup, carry_ref[...])
        carry_ref[...] = dup[0:8]
        cw = cw_ref[...]
        du = (cw[2:3] * dup + cw[1:2] * dup1 + cw[0:1] * dup2).astype(BF16)
        du_ref[...] = du
        dhn = lax.dot_general(du, wu_ref[...], NT, preferred_element_type=F32)
        h1v = h1_ref[...]
        dh1, dg = _rms_bwd(dhn, h1v, _rstd(h1v), g_ref[...])
        dh1_ref[...] = dh2v + dh1
        gg_ref[...] += jnp.sum(dg, axis=0, keepdims=True)

    row = lambda w: pl.BlockSpec((tm, w), lambda i: (nt - 1 - i, 0))
    return pl.pallas_call(
        body, name="ffn_bwd", grid=(nt,),
        in_specs=[row(D_MODEL), row(FF2),
                  row(D_MODEL), _resident((D_MODEL, FF2)), _resident((3, FF2)), _resident((D_FF, D_MODEL)),
                  _resident((1, D_MODEL))],
        out_specs=[row(FF2), row(FF2), row(D_MODEL), pl.BlockSpec((1, D_MODEL), lambda i: (0, 0))],
        out_shape=[jax.ShapeDtypeStruct((s, FF2), BF16), jax.ShapeDtypeStruct((s, FF2), BF16),
                   jax.ShapeDtypeStruct((s, D_MODEL), F32), jax.ShapeDtypeStruct((1, D_MODEL), F32)],
        scratch_shapes=[pltpu.VMEM((8, FF2), F32)],
        compiler_params=_params("arbitrary"),
    )(dh2, up, h1, w_up, ffn_cw, w_down, g2)


def _mix_bwd(dh1, gates, a, cv, c3, conv_w, w_ab, w_cb, w_out, tm, comm=None):
    s = dh1.shape[0]
    nt = s // tm
    halo = 8

    def body(dh1_ref, gt_ref, a_ref, cv_ref, c3_ref, ch_ref, cw_ref, wab_ref, wcb_ref, wo_ref,
             dat_ref, da_ref, dcv_ref, dc3_ref, dgt_ref, gcw_ref, carry_ref):
        i = pl.program_id(0)

        @pl.when(i == 0)
        def _():
            carry_ref[...] = jnp.zeros_like(carry_ref)
            gcw_ref[...] = jnp.zeros_like(gcw_ref)

        dm = lax.dot_general(dh1_ref[...].astype(BF16), wo_ref[...], NT, preferred_element_type=F32)
        gt = gt_ref[...]
        sa, sc = _sigmoid(gt[:, :D_MODEL]), _sigmoid(gt[:, D_MODEL:])
        da = (dm * sa).astype(BF16)
        dcv = (dm * sc).astype(BF16)
        da_ref[...] = da
        dcv_ref[...] = dcv
        dgt_ref[...] = jnp.concatenate(
            [dm * a_ref[...].astype(F32) * (sa * (1.0 - sa)), dm * cv_ref[...].astype(F32) * (sc * (1.0 - sc))],
            axis=1).astype(BF16)
        dat_ref[...] = lax.dot_general(da, wab_ref[...], NT, preferred_element_type=F32).astype(BF16)
        dconv = lax.dot_general(dcv, wcb_ref[...], NT, preferred_element_type=F32)
        c3v = c3_ref[...]
        cb, cc, cx = c3v[:, :CONV_W], c3v[:, CONV_W:2 * CONV_W], c3v[:, 2 * CONV_W:]
        z = cc * cx
        chv = ch_ref[...] * (i < nt - 1).astype(F32)
        zh = chv[:, CONV_W:2 * CONV_W] * chv[:, 2 * CONV_W:]
        cw = cw_ref[...]
        cz = _causal_conv(z, zh, cw)
        dcz = dconv * cb
        dcz1, dcz2 = _rows_after(dcz, carry_ref[...])
        carry_ref[...] = dcz[0:8]
        gcw_ref[2:3, :] += jnp.sum(dcz * z, axis=0, keepdims=True)
        gcw_ref[1:2, :] += jnp.sum(dcz1 * z, axis=0, keepdims=True)
        gcw_ref[0:1, :] += jnp.sum(dcz2 * z, axis=0, keepdims=True)
        dz = cw[2:3] * dcz + cw[1:2] * dcz1 + cw[0:1] * dcz2
        dc3_ref[...] = jnp.concatenate([dconv * cz, dz * cx, dz * cc], axis=1).astype(BF16)

    row = lambda w: pl.BlockSpec((tm, w), lambda i: (nt - 1 - i, 0))
    return _call(
        comm, body, name="mix_bwd", grid=(nt,),
        in_specs=[row(D_MODEL), row(GATES_W), row(D_MODEL), row(D_MODEL), row(C3_W),
                  pl.BlockSpec((halo, C3_W), lambda i: (jnp.maximum((nt - 1 - i) * (tm // halo) - 1, 0), 0)),
                  _resident((3, CONV_W)), _resident((ATTN_W, D_MODEL)), _resident((CONV_W, D_MODEL)),
                  _resident((D_MODEL, D_MODEL))],
        out_specs=[row(ATTN_W), row(D_MODEL), row(D_MODEL), row(C3_W), row(GATES_W),
                   pl.BlockSpec((3, CONV_W), lambda i: (0, 0))],
        out_shape=[jax.ShapeDtypeStruct((s, ATTN_W), BF16), jax.ShapeDtypeStruct((s, D_MODEL), BF16),
                   jax.ShapeDtypeStruct((s, D_MODEL), BF16), jax.ShapeDtypeStruct((s, C3_W), BF16),
                   jax.ShapeDtypeStruct((s, GATES_W), BF16), jax.ShapeDtypeStruct((3, CONV_W), F32)],
        scratch_shapes=[pltpu.VMEM((8, CONV_W), F32)],
        compiler_params=_params("arbitrary"),
    )(dh1, gates, a, cv, c3, c3, conv_w, w_ab, w_cb, w_out)


def _attn_bwd(qkv, sinks, o, do, comm=None):
    s = qkv.shape[0]
    nb = s // BLOCK

    def body(sk_ref, bias_ref, q_ref, kp_ref, kc_ref, vp_ref, vc_ref, o_ref, do_ref,
             dq_ref, dk_ref, dv_ref, dsk_ref, ck_ref, cvv_ref):
        i = pl.program_id(0)

        @pl.when(i == 0)
        def _():
            ck_ref[...] = jnp.zeros_like(ck_ref)
            cvv_ref[...] = jnp.zeros_like(cvv_ref)
            dsk_ref[...] = jnp.zeros_like(dsk_ref)

        @pl.when(i < nb)
        def _():
            bias = bias_ref[...]
            q, kp, kc, vp, vc = q_ref[...], kp_ref[...], kc_ref[...], vp_ref[...], vc_ref[...]
            ov, dov = o_ref[...], do_ref[...]
            dqs, dks, dvs = [], [], []
            for h in range(N_KV_HEADS):
                hs = slice(h * HEAD_DIM, (h + 1) * HEAD_DIM)
                k2 = jnp.concatenate([kp[:, hs], kc[:, hs]], axis=0)
                v2 = jnp.concatenate([vp[:, hs], vc[:, hs]], axis=0)
                qg, og, dog = _stack_heads(q, h), _stack_heads(ov, h), _stack_heads(dov, h)
                sc = lax.dot_general(qg, k2, NT, preferred_element_type=F32) * ATTN_SCALE + bias
                sink = _sink_column(sk_ref, h)
                m = jnp.maximum(jnp.max(sc, axis=1, keepdims=True), sink)
                p = jnp.exp(sc - m)
                psink = jnp.exp(sink - m)
                inv = 1.0 / (jnp.sum(p, axis=1, keepdims=True) + psink)
                p = p * inv
                delta = jnp.sum(dog.astype(F32) * og.astype(F32), axis=1, keepdims=True)
                dp = lax.dot_general(dog, v2, NT, preferred_element_type=F32)
                ds = (p * (dp - delta)).astype(BF16)
                dqs.append(jnp.dot(ds, k2, preferred_element_type=F32) * ATTN_SCALE)
                dks.append(lax.dot_general(ds, qg, TN, preferred_element_type=F32) * ATTN_SCALE)
                dvs.append(lax.dot_general(p.astype(BF16), dog, TN, preferred_element_type=F32))
                dsink = -(psink * inv * delta)
                for g in range(GROUP):
                    r = h * GROUP + g
                    dsk_ref[r:r + 1, :] += jnp.sum(dsink[g * BLOCK:(g + 1) * BLOCK])
            dq_ref[...] = _unstack_heads(dqs).astype(BF16)
            dk2 = jnp.concatenate(dks, axis=1)
            dv2 = jnp.concatenate(dvs, axis=1)
            dk_ref[...] = (ck_ref[...] + dk2[:BLOCK]).astype(BF16)
            dv_ref[...] = (cvv_ref[...] + dv2[:BLOCK]).astype(BF16)
            ck_ref[...] = dk2[BLOCK:]
            cvv_ref[...] = dv2[BLOCK:]

        @pl.when(i == nb)
        def _():
            dk_ref[...] = ck_ref[...].astype(BF16)
            dv_ref[...] = cvv_ref[...].astype(BF16)

    cur = lambda i: jnp.minimum(i, nb - 1)
    done = lambda i: jnp.maximum(i - 1, 0)
    return _call(
        comm, body, name="attn_bwd", grid=(nb + 1,),
        in_specs=[pl.BlockSpec(memory_space=pltpu.SMEM), _attn_bias_spec(), *_attn_specs(nb),
                  pl.BlockSpec((BLOCK, ATTN_W), lambda i: (cur(i), 0)),
                  pl.BlockSpec((BLOCK, ATTN_W), lambda i: (cur(i), 0))],
        out_specs=[pl.BlockSpec((BLOCK, ATTN_W), lambda i: (cur(i), 0)),
                   pl.BlockSpec((BLOCK, KV_W), lambda i: (done(i), 0)),
                   pl.BlockSpec((BLOCK, KV_W), lambda i: (done(i), 0)),
                   pl.BlockSpec((N_HEADS, 128), lambda i: (0, 0))],
        out_shape=[jax.ShapeDtypeStruct((s, ATTN_W), BF16), jax.ShapeDtypeStruct((s, KV_W), BF16),
                   jax.ShapeDtypeStruct((s, KV_W), BF16), jax.ShapeDtypeStruct((N_HEADS, 128), F32)],
        scratch_shapes=[pltpu.VMEM((BLOCK, KV_W), F32), pltpu.VMEM((BLOCK, KV_W), F32)],
        compiler_params=_params("arbitrary"),
    )(sinks, _attn_bias(), qkv, qkv, qkv, qkv, qkv, o, do)


def _inproj_bwd(dq, dk, dv, dc3, dgt, w_in, x, dh1, g1, tm, comm=None):
    s = x.shape[0]

    def body(dq_ref, dk_ref, dv_ref, dc3_ref, dgt_ref, w_ref, x_ref, dh1_ref, g_ref,
             dx_ref, dp_ref, gb_ref, gg_ref):
        @pl.when(pl.program_id(0) == 0)
        def _():
            gb_ref[...] = jnp.zeros_like(gb_ref)
            gg_ref[...] = jnp.zeros_like(gg_ref)

        dp = jnp.concatenate([dq_ref[...], dk_ref[...], dv_ref[...], dc3_ref[...], dgt_ref[...]], axis=1)
        dp_ref[...] = dp
        gb_ref[...] += jnp.sum(dp.astype(F32), axis=0, keepdims=True)
        dxn = jnp.dot(dp, w_ref[...], preferred_element_type=F32)
        xf = x_ref[...]
        dx, dg = _rms_bwd(dxn, xf, _rstd(xf), g_ref[...])
        dx_ref[...] = dh1_ref[...] + dx
        gg_ref[...] += jnp.sum(dg, axis=0, keepdims=True)

    row = lambda w: pl.BlockSpec((tm, w), lambda i: (i, 0))
    acc = lambda w: pl.BlockSpec((1, w), lambda i: (0, 0))
    return _call(
        comm, body, name="inproj_bwd", grid=(s // tm,),
        in_specs=[row(ATTN_W), row(KV_W), row(KV_W), row(C3_W), row(GATES_W), _resident((IN_W, D_MODEL)),
                  row(D_MODEL), row(D_MODEL), _resident((1, D_MODEL))],
        out_specs=[row(D_MODEL), row(IN_W), acc(IN_W), acc(D_MODEL)],
        out_shape=[jax.ShapeDtypeStruct((s, D_MODEL), F32), jax.ShapeDtypeStruct((s, IN_W), BF16),
                   jax.ShapeDtypeStruct((1, IN_W), F32), jax.ShapeDtypeStruct((1, D_MODEL), F32)],
        compiler_params=_params("arbitrary"),
    )(dq, dk, dv, dc3, dgt, w_in, x, dh1, g1)


def _wgrad(a, b, bm, bn, bk, name, comm=None):
    s, m = a.shape
    n = b.shape[1]
    nk = s // bk

    def body(a_ref, b_ref, o_ref, acc_ref):
        k = pl.program_id(2)

        @pl.when(k == 0)
        def _():
            acc_ref[...] = jnp.zeros_like(acc_ref)

        acc_ref[...] += lax.dot_general(a_ref[...].astype(BF16), b_ref[...].astype(BF16), TN,
                                        preferred_element_type=F32)

        @pl.when(k == nk - 1)
        def _():
            o_ref[...] = acc_ref[...].astype(BF16)

    return _call(
        comm, body, name=name, grid=(m // bm, n // bn, nk),
        in_specs=[pl.BlockSpec((bk, bm), lambda i, j, k: (k, i)), pl.BlockSpec((bk, bn), lambda i, j, k: (k, j))],
        out_specs=pl.BlockSpec((bm, bn), lambda i, j, k: (i, j)),
        out_shape=jax.ShapeDtypeStruct((m, n), BF16),
        scratch_shapes=[pltpu.VMEM((bm, bn), F32)],
        compiler_params=_params("parallel", "parallel", "arbitrary"),
    )(a, b)


def _wgrad_up_taps(hn, du, dup, u, bn, bk, comm=None):
    s, m = hn.shape
    n = du.shape[1]
    nk = s // bk
    chunk = min(256, bk)
    halo = 16

    def body(a_ref, b_ref, d_ref, u_ref, uh_ref, o_ref, gcw_ref, acc_ref):
        k = pl.program_id(1)

        @pl.when(k == 0)
        def _():
            acc_ref[...] = jnp.zeros_like(acc_ref)
            gcw_ref[...] = jnp.zeros_like(gcw_ref)

        acc_ref[...] += lax.dot_general(a_ref[...], b_ref[...], TN, preferred_element_type=F32)
        for r in range(0, bk, chunk):
            d = d_ref[r:r + chunk, :].astype(F32)
            uu = u_ref[r:r + chunk, :].astype(F32)
            if r == 0:
                before = uh_ref[...].astype(F32) * (k > 0).astype(F32)
            else:
                before = u_ref[r - halo:r, :].astype(F32)
            u1, u2 = _rows_before(uu, before[halo - 8:halo])
            gcw_ref[2:3, :] += jnp.sum(d * uu, axis=0, keepdims=True)
            gcw_ref[1:2, :] += jnp.sum(d * u1, axis=0, keepdims=True)
            gcw_ref[0:1, :] += jnp.sum(d * u2, axis=0, keepdims=True)

        @pl.when(k == nk - 1)
        def _():
            o_ref[...] = acc_ref[...].astype(BF16)

    tile = pl.BlockSpec((bk, bn), lambda j, k: (k, j))
    return _call(
        comm, body, name="wgrad_up", grid=(n // bn, nk),
        in_specs=[pl.BlockSpec((bk, m), lambda j, k: (k, 0)), tile, tile, tile,
                  pl.BlockSpec((halo, bn), lambda j, k: (jnp.maximum(k * (bk // halo) - 1, 0), j))],
        out_specs=[pl.BlockSpec((m, bn), lambda j, k: (0, j)), pl.BlockSpec((3, bn), lambda j, k: (0, j))],
        out_shape=[jax.ShapeDtypeStruct((m, n), BF16), jax.ShapeDtypeStruct((3, n), F32)],
        scratch_shapes=[pltpu.VMEM((m, bn), F32)],
        compiler_params=_params("parallel", "arbitrary"),
    )(hn, du, dup, u, u)


def _wgrad_in(xn, dproj, bk, comm=None):
    s = xn.shape[0]
    nk = s // bk

    def body(a_ref, b_ref, o_ref, acc_ref):
        k = pl.program_id(0)

        @pl.when(k == 0)
        def _():
            acc_ref[...] = jnp.zeros_like(acc_ref)

        acc_ref[...] += lax.dot_general(b_ref[...], a_ref[...], TN, preferred_element_type=F32)

        @pl.when(k == nk - 1)
        def _():
            o_ref[...] = acc_ref[...].astype(BF16)

    return _call(
        comm, body, name="wgrad_in", grid=(nk,),
        in_specs=[pl.BlockSpec((bk, D_MODEL), lambda k: (k, 0)), pl.BlockSpec((bk, IN_W), lambda k: (k, 0))],
        out_specs=_resident((IN_W, D_MODEL)),
        out_shape=jax.ShapeDtypeStruct((IN_W, D_MODEL), BF16),
        scratch_shapes=[pltpu.VMEM((IN_W, D_MODEL), F32)],
        compiler_params=_params("arbitrary"),
    )(xn, dproj)


class _Carry:
    def __init__(self, jobs, reads=None, bufs=None, fresh=None):
        self.jobs, self.reads, self.bufs, self.fresh = jobs, reads or {}, bufs or {}, fresh or {}
        self.out = {}


class _Job:
    def __init__(self, n_sems, plan):
        self.n_sems, self.plan = n_sems, plan


def _plan_all(jobs, hbm, send, recv):
    pos = _position()
    starts, waits, base = [], [], 0
    for job in jobs:
        s, w = job.plan(hbm, pos, send, recv, base)
        starts, waits, base = starts + s, waits + w, base + job.n_sems
    return starts, waits


def _call(comm, body, **kw):
    if comm is None:
        return pl.pallas_call(body, **kw)
    grid = kw["grid"]
    single = not isinstance(kw["out_shape"], (list, tuple))
    out_shape = [kw["out_shape"]] if single else list(kw["out_shape"])
    out_specs = [kw["out_specs"]] if single else list(kw["out_specs"])
    in_specs = list(kw["in_specs"])
    scratch = list(kw.get("scratch_shapes", ()))
    r_names, b_names, f_names = list(comm.reads), list(comm.bufs), list(comm.fresh)
    n_args, n_out, n_scr = len(in_specs), len(out_shape), len(scratch)
    n_sems = sum(j.n_sems for j in comm.jobs)

    def wrapped(*refs):
        k = n_args
        hbm = dict(zip(r_names, refs[k:k + len(r_names)]))
        k += len(r_names) + len(b_names)
        outs = refs[k:k + n_out]
        k += n_out
        hbm.update(zip(b_names + f_names, refs[k:k + len(b_names) + len(f_names)]))
        k += len(b_names) + len(f_names)
        send, recv = refs[k + n_scr:]
        starts, waits = _plan_all(comm.jobs, hbm, send, recv)
        ids = [pl.program_id(a) for a in range(len(grid))]
        first = functools.reduce(jnp.logical_and, [i == 0 for i in ids])
        last = functools.reduce(jnp.logical_and, [i == g - 1 for i, g in zip(ids, grid)])

        @pl.when(first)
        def _():
            for cp in starts:
                cp.start()

        body(*refs[:n_args], *outs, *refs[k:k + n_scr])

        @pl.when(last)
        def _():
            for cp in waits:
                cp.wait_recv()
            for cp in starts:
                cp.wait_send()

    sems = pltpu.SemaphoreType.DMA((n_sems,))
    held = [jax.ShapeDtypeStruct(a.shape, a.dtype) for a in comm.bufs.values()] + list(comm.fresh.values())
    call = pl.pallas_call(
        wrapped, name=kw["name"], grid=grid,
        in_specs=in_specs + [_ANY] * (len(r_names) + len(b_names)),
        out_specs=out_specs + [_ANY] * len(held),
        out_shape=out_shape + held,
        input_output_aliases={n_args + len(r_names) + i: n_out + i for i in range(len(b_names))},
        scratch_shapes=scratch + [sems, sems],
        compiler_params=_params(*["arbitrary"] * len(grid)),
    )

    def run(*args):
        res = call(*args, *comm.reads.values(), *comm.bufs.values())
        comm.out = dict(zip(b_names + f_names, res[n_out:]))
        return res[0] if single else res[:n_out]

    return run


def _exchange(name, phases, reads=None, bufs=None, fresh=None):
    comm = _Carry([j for ph in phases for j in ph], reads, bufs, fresh)
    r_names, b_names, f_names = list(comm.reads), list(comm.bufs), list(comm.fresh)
    n_sems = sum(j.n_sems for j in comm.jobs)

    def body(*refs):
        hbm = dict(zip(r_names, refs[:len(r_names)]))
        k = len(r_names) + len(b_names)
        hbm.update(zip(b_names + f_names, refs[k:k + len(b_names) + len(f_names)]))
        send, recv = refs[-2:]
        pos = _position()
        started, base = [], 0
        for ph in phases:
            waits = []
            for job in ph:
                s, w = job.plan(hbm, pos, send, recv, base)
                base += job.n_sems
                for cp in s:
                    cp.start()
                started, waits = started + s, waits + w
            for cp in waits:
                cp.wait_recv()
        for cp in started:
            cp.wait_send()

    sems = pltpu.SemaphoreType.DMA((n_sems,))
    held = [jax.ShapeDtypeStruct(a.shape, a.dtype) for a in comm.bufs.values()] + list(comm.fresh.values())
    res = pl.pallas_call(
        body, name=name, in_specs=[_ANY] * (len(r_names) + len(b_names)), out_specs=[_ANY] * len(held),
        out_shape=held, input_output_aliases={len(r_names) + i: i for i in range(len(b_names))},
        scratch_shapes=[sems, sems],
    )(*comm.reads.values(), *comm.bufs.values())
    return dict(zip(b_names + f_names, res))


_HBM = pl.BlockSpec(memory_space=pltpu.HBM)
_SEM = pl.BlockSpec(memory_space=pltpu.SEMAPHORE)
_EFFECT = pltpu.SideEffectType.DATAFLOW_SIDE_EFFECTING


def _start_exchanges(name, groups):
    names = [list(arrays) for _, arrays in groups]
    first = [sum(len(ns) for ns in names[:g]) for g in range(len(groups))]
    n, ng = sum(len(ns) for ns in names), len(groups)

    def body(*refs):
        for g, (jobs, _) in enumerate(groups):
            hbm = dict(zip(names[g], refs[first[g]:first[g] + len(names[g])]))
            for cp in _plan_all(jobs, hbm, refs[n + 2 * g], refs[n + 2 * g + 1])[0]:
                cp.start()
        refs[-1][...] = jnp.zeros_like(refs[-1])

    given = [pltpu.with_memory_space_constraint(
        a if isinstance(a, jax.Array) else lax.empty(a.shape, a.dtype), pltpu.HBM)
        for _, arrays in groups for a in arrays.values()]
    sems = [pltpu.SemaphoreType.DMA((sum(j.n_sems for j in jobs),)) for jobs, _ in groups for _ in range(2)]
    res = pl.pallas_call(
        body, name=name,
        out_shape=(*sems, *[pltpu.HBM(a.shape, a.dtype) for a in given], jax.ShapeDtypeStruct((8, 128), F32)),
        in_specs=[_HBM] * n, out_specs=(*[_SEM] * (2 * ng), *[_HBM] * n, pl.BlockSpec(memory_space=pltpu.VMEM)),
        input_output_aliases={i: 2 * ng + i for i in range(n)},
        compiler_params=pltpu.CompilerParams(has_side_effects=_EFFECT),
    )(*given)
    held = res[2 * ng:2 * ng + n]
    states = [(names[g], groups[g][0], res[2 * g], res[2 * g + 1], held[first[g]:first[g] + len(names[g])])
              for g in range(ng)]
    return states, res[-1]


def _start_exchange(name, jobs, arrays):
    states, token = _start_exchanges(name, [(jobs, arrays)])
    return states[0], token


def _finish_exchange(name, state, after):
    names, jobs, send_sem, recv_sem, held = state
    n = len(names)

    def body(*refs):
        hbm = dict(zip(names, refs[:n]))
        send, recv = refs[n:n + 2]
        starts, waits = _plan_all(jobs, hbm, send, recv)
        for cp in waits:
            cp.wait_recv()
        for cp in starts:
            cp.wait_send()

    res = pl.pallas_call(
        body, name=name, out_shape=tuple(pltpu.HBM(a.shape, a.dtype) for a in held),
        in_specs=[_HBM] * n + [_SEM, _SEM, _ANY], out_specs=tuple([_HBM] * n),
        input_output_aliases={i: i for i in range(n)},
        compiler_params=pltpu.CompilerParams(has_side_effects=_EFFECT),
    )(*held, send_sem, recv_sem, after)
    return dict(zip(names, res))


def _row_tile(rows, bytes_per_row):
    best = 16
    for t in range(16, rows + 1, 16):
        if rows % t == 0 and t * bytes_per_row <= 6 * 1024 * 1024:
            best = t
    return best


def _rowwise(fn, ins, out_dtypes, name, after=None):
    rows, cols = ins[0].shape
    per_row = sum(cols * a.dtype.itemsize for a in ins) + sum(cols * jnp.dtype(d).itemsize for d in out_dtypes)
    tr = _row_tile(rows, per_row)
    n_in = len(ins)

    def body(*refs):
        outs = fn(*[r[...] for r in refs[:n_in]])
        for o_ref, o in zip(refs[-len(out_dtypes):], outs):
            o_ref[...] = o.astype(o_ref.dtype)

    tile = pl.BlockSpec((tr, cols), lambda i: (i, 0))
    behind = [] if after is None else [after]
    return pl.pallas_call(
        body, name=name, grid=(rows // tr,),
        in_specs=[tile] * n_in + [pl.BlockSpec((8, 128), lambda i: (0, 0))] * len(behind),
        out_specs=[tile] * len(out_dtypes),
        out_shape=[jax.ShapeDtypeStruct((rows, cols), d) for d in out_dtypes],
        compiler_params=_params("parallel"),
    )(*ins, *behind)


def _tiled(fn, name, grid, pos, ins, outs):
    n_in = len(ins)

    def body(pos_ref, *refs):
        res = fn(*[r[...] for r in refs[:n_in]])
        for o_ref, o in zip(refs[n_in:], res):
            o_ref[...] = o.astype(o_ref.dtype)

    return pl.pallas_call(
        body, name=name,
        grid_spec=pltpu.PrefetchScalarGridSpec(
            num_scalar_prefetch=1, grid=grid,
            in_specs=[pl.BlockSpec(bs, im) for _, bs, im in ins],
            out_specs=[pl.BlockSpec(bs, im) for _, _, bs, im in outs]),
        out_shape=[jax.ShapeDtypeStruct(s, d) for s, d, _, _ in outs],
        compiler_params=_params("parallel"),
    )(pos, *[a for a, _, _ in ins])


def _adamw(w, g, m, v):
    m = ADAM_B1 * m + (1.0 - ADAM_B1) * g
    v = ADAM_B2 * v + (1.0 - ADAM_B2) * (g * g)
    m_hat = m / (1.0 - ADAM_B1 ** ADAM_STEP)
    v_hat = v / (1.0 - ADAM_B2 ** ADAM_STEP)
    return -ADAM_LR * (m_hat / (jnp.sqrt(v_hat) + ADAM_EPS) + ADAM_WD * w), m, v


def _adamw_small(params):
    n = len(params)

    def body(*refs):
        for k in range(n):
            w, g, m, v = (r[...] for r in refs[4 * k:4 * k + 4])
            for o_ref, o in zip(refs[4 * n + 3 * k:4 * n + 3 * k + 3], _adamw(w, g, m, v)):
                o_ref[...] = o

    flat = [a for p in params for a in p]
    return pl.pallas_call(
        body, name="adamw_small",
        out_shape=[jax.ShapeDtypeStruct(p[0].shape, F32) for p in params for _ in range(3)],
    )(*flat)


class _Layout:
    def __init__(self, rows, cols, stacked):
        self.rows, self.cols, self.stacked = rows, cols, stacked

    def whole(self, rows=None):
        r = self.rows if rows is None else rows
        return (N_CHIPS, r, self.cols) if self.stacked else (r, N_CHIPS * self.cols)

    def part_rows(self, h, q=0, nq=1):
        n = self.rows // 2 // nq
        return pl.ds(pl.multiple_of(h * (self.rows // 2) + q * n, 16), n)

    def half_rows(self, h):
        return self.part_rows(h)

    def block(self, ref, p, rows=slice(None)):
        if self.stacked:
            return ref.at[p, rows, :]
        return ref.at[rows, pl.ds(pl.multiple_of(p * self.cols, 128), self.cols)]

    def all_chips(self, ref, rows):
        return ref.at[:, rows, :] if self.stacked else ref.at[rows, :]


BIG = (
    _Layout(IN_SHARD, D_MODEL, True),
    _Layout(ATTN_W, D_MODEL // N_CHIPS, False),
    _Layout(CONV_W, D_MODEL // N_CHIPS, False),
    _Layout(D_MODEL // N_CHIPS, D_MODEL, True),
    _Layout(D_MODEL, FF2 // N_CHIPS, False),
    _Layout(D_FF // N_CHIPS, D_MODEL, True),
)
N_BIG = len(BIG)
_ANY = pl.BlockSpec(memory_space=pl.ANY)


def _position():
    x, y, c = lax.axis_index("x"), lax.axis_index("y"), lax.axis_index("c")
    return x, y, c, 2 * x + y


def _core_of_chip(p, c):
    return (p >> 1, p & 1, c)


def _place_cast(shard, lay, pos, name, after=None):
    rows, cols = shard.shape
    tr = _row_tile(rows, cols * 6)
    if lay.stacked:
        out = (lay.whole(), BF16, (None, tr, cols), lambda i, pos: (pos[0], i, 0))
    else:
        out = (lay.whole(), BF16, (tr, cols), lambda i, pos: (i, pos[0]))
    ins = [(shard, (tr, cols), lambda i, pos: (i, 0))]
    if after is not None:
        ins.append((after, (8, 128), lambda i, pos: (0, 0)))
    return _tiled(lambda a, *_: (a,), name, (rows // tr,), pos, ins, [out])[0]


def _remote(src, dst, send, recv, k, device):
    return pltpu.make_async_remote_copy(src_ref=src, dst_ref=dst, send_sem=send.at[k], recv_sem=recv.at[k],
                                        device_id=device, device_id_type=MESH)


def _arrival(dst, send, recv, k, me):
    return _remote(dst, dst, send, recv, k, me)


def _gather_ici(lay, name, q=0, nq=1):
    def plan(hbm, pos, send, recv, base):
        x, y, c, me = pos
        rows = lay.part_rows(c, q, nq)
        mine = lay.block(hbm[name], me, rows)
        starts = [_remote(mine, mine, send, recv, base + d - 1, _core_of_chip(me ^ d, c)) for d in (1, 2, 3)]
        waits = [_arrival(lay.block(hbm[name], me ^ d, rows), send, recv, base + d - 1, (x, y, c)) for d in (1, 2, 3)]
        return starts, waits
    return _Job(3, plan)


def _gather_d2d(lay, name, q=0, nq=1):
    def plan(hbm, pos, send, recv, base):
        x, y, c, me = pos
        starts, waits = [], []
        for d in (1, 2, 3):
            got = lay.block(hbm[name], me ^ d, lay.part_rows(c, q, nq))
            starts.append(_remote(got, got, send, recv, base + d - 1, (x, y, 1 - c)))
            waits.append(_arrival(lay.block(hbm[name], me ^ d, lay.part_rows(1 - c, q, nq)), send, recv, base + d - 1,
                                  (x, y, c)))
        return starts, waits
    return _Job(3, plan)


def _rs_pair(lay, grad, theirs):
    def plan(hbm, pos, send, recv, base):
        x, y, c, _ = pos
        out = _remote(lay.all_chips(hbm[grad], lay.half_rows(1 - c)), hbm[theirs], send, recv, base, (x, y, 1 - c))
        return [out], [_arrival(hbm[theirs], send, recv, base, (x, y, c))]
    return _Job(1, plan)


def _rs_chips(lay, sums, slots):
    def plan(hbm, pos, send, recv, base):
        x, y, c, me = pos
        starts = [_remote(lay.block(hbm[sums], me ^ d), hbm[slots].at[me], send, recv, base + d - 1,
                          _core_of_chip(me ^ d, c)) for d in (1, 2, 3)]
        waits = [_arrival(hbm[slots].at[me ^ d], send, recv, base + d - 1, (x, y, c)) for d in (1, 2, 3)]
        return starts, waits
    return _Job(3, plan)


def _rs_share(lay, shard):
    def plan(hbm, pos, send, recv, base):
        x, y, c, _ = pos
        mine = hbm[shard].at[lay.half_rows(c), :]
        other = hbm[shard].at[lay.half_rows(1 - c), :]
        return [_remote(mine, mine, send, recv, base, (x, y, 1 - c))], [_arrival(other, send, recv, base, (x, y, c))]
    return _Job(1, plan)


def _slots_shape(lay):
    return jax.ShapeDtypeStruct((N_CHIPS, lay.rows // 2, lay.cols), BF16)


def _theirs_shape(lay):
    return jax.ShapeDtypeStruct(lay.whole(lay.rows // 2), BF16)


def _pair_sum(grad, theirs, lay, pos, name):
    half = lay.rows // 2
    add = lambda a, b: (a.astype(F32) + b.astype(F32),)
    if lay.stacked:
        tr = _row_tile(half, lay.cols * 6)
        nt = half // tr
        flat = lambda a: a.reshape(-1, lay.cols)
        mine = lambda t, pos: ((t // nt) * (2 * nt) + pos[1] * nt + t % nt, 0)
        grid, blk = (N_CHIPS * nt,), (tr, lay.cols)
        grad, theirs = flat(grad), flat(theirs)
    else:
        tr = _row_tile(half, N_CHIPS * lay.cols * 6)
        nt = half // tr
        mine = lambda t, pos: (pos[1] * nt + t, 0)
        grid, blk = (nt,), (tr, N_CHIPS * lay.cols)
    same = lambda t, pos: (t, 0)
    out = _tiled(add, name, grid, pos, [(grad, blk, mine), (theirs, blk, same)], [(theirs.shape, BF16, blk, same)])[0]
    return out.reshape(lay.whole(half))


def _chip_sum(sums, slots, lay, pos, name, after=None):
    half = lay.rows // 2
    tr = _row_tile(half, lay.cols * 12)
    nt = half // tr
    blk3 = (None, tr, lay.cols)
    if lay.stacked:
        own = (sums, blk3, lambda i, pos: (pos[0], i, 0))
    else:
        own = (sums, (tr, lay.cols), lambda i, pos: (i, pos[0]))
    others = [(slots, blk3, functools.partial(lambda d, i, pos: (pos[0] ^ d, i, 0), d)) for d in (1, 2, 3)]

    def add(a, b1, b2, b3, *_):
        return (((a.astype(F32) + b1.astype(F32)) + b2.astype(F32)) + b3.astype(F32),)

    if after is not None:
        others.append((after, (8, 128), lambda i, pos: (0, 0)))
    return _tiled(add, name, (nt,), pos, [own] + others,
                  [((lay.rows, lay.cols), F32, (tr, lay.cols), lambda i, pos: (pos[1] * nt + i, 0))])[0]


N_DEV = 8


def _to_all(src, slots):
    def plan(hbm, pos, send, recv, base):
        x, y, c, _ = pos
        idx = 4 * x + 2 * y + c
        starts = [_remote(hbm[src], hbm[slots].at[idx], send, recv, base + k - 1,
                          (x ^ (k >> 2), y ^ ((k >> 1) & 1), c ^ (k & 1))) for k in range(1, N_DEV)]
        waits = [_arrival(hbm[slots].at[idx ^ k], send, recv, base + k - 1, (x, y, c)) for k in range(1, N_DEV)]
        return starts, waits
    return _Job(N_DEV - 1, plan)


def _sum_slots(own, slots, pos):
    def body(pos_ref, own_ref, slots_ref, o_ref):
        idx = 2 * pos_ref[0] + pos_ref[1]
        term = lambda q: jnp.where(idx == q, own_ref[...], slots_ref[q])
        acc = term(0)
        for q in range(1, N_DEV):
            acc = acc + term(q)
        o_ref[...] = acc

    return pl.pallas_call(
        body, name="sum_small", out_shape=jax.ShapeDtypeStruct(own.shape, F32),
        in_specs=[pl.BlockSpec(memory_space=pltpu.SMEM), pl.BlockSpec(memory_space=pltpu.VMEM),
                  pl.BlockSpec(memory_space=pltpu.VMEM)],
    )(pos, own, slots)


def _pack_rows(parts):
    padded = [jnp.pad(a, ((0, -a.shape[0] % 8), (0, 0))) for a in parts]
    starts = [sum(p.shape[0] for p in padded[:k]) for k in range(len(padded))]
    return jnp.concatenate(padded, axis=0), starts


def kernel(x, mix_norm, w_in, b_in, sinks, conv_w, w_attn_branch, w_conv_branch, w_out, ffn_norm, w_up, ffn_conv_w, w_down, final_norm, loss_target, m_mix_norm, m_w_in, m_b_in, m_sinks, m_conv_w, m_w_attn_branch, m_w_conv_branch, m_w_out, m_ffn_norm, m_w_up, m_ffn_conv_w, m_w_down, m_final_norm, v_mix_norm, v_w_in, v_b_in, v_sinks, v_conv_w, v_w_attn_branch, v_w_conv_branch, v_w_out, v_ffn_norm, v_w_up, v_ffn_conv_w, v_w_down, v_final_norm):
    me = 2 * lax.axis_index("x") + lax.axis_index("y")
    big_w = [w_in[0].T, w_attn_branch[0], w_conv_branch[0], w_out[0], w_up[0], w_down[0]]
    big_m = [m_w_in[0].T, m_w_attn_branch[0], m_w_conv_branch[0], m_w_out[0], m_w_up[0], m_w_down[0]]
    big_v = [v_w_in[0].T, v_w_attn_branch[0], v_w_conv_branch[0], v_w_out[0], v_w_up[0], v_w_down[0]]
    names = ("w_in", "w_ab", "w_cb", "w_out", "w_up", "w_down")

    pos = jnp.stack([me, lax.axis_index("c")]).astype(jnp.int32)

    lay = dict(zip(names, BIG))
    xs, target, sk = x[0], loss_target[0], sinks[0]
    s = xs.shape[0]
    tm, tm2, bk = min(256, s), min(512, s), min(1024, s)

    taps, (_, t0) = _pack_rows([conv_w[0], ffn_conv_w[0].reshape(3 * (FF2 // N_CHIPS // 128), 128)])
    placed = {"w_in": _place_cast(big_w[0], lay["w_in"], pos, "cast_w_in")}
    fly_in, started = _start_exchange("gather_in_start", [_gather_ici(lay["w_in"], "w_in")], {"w_in": placed["w_in"]})
    taps_flight, started = _start_exchange("taps_start", [_to_all("v", "slots")],
                                           {"v": taps + started[0:1], "slots": jnp.zeros((N_DEV, *taps.shape), F32)})
    for w, n in zip(big_w[1:], names[1:]):
        placed[n] = _place_cast(w, lay[n], pos, "cast_" + n, after=started)
    trio = ("w_ab", "w_cb", "w_out")
    (fly_trio, fly_up, fly_down), started = _start_exchanges("gather_rest_start", [
        ([_gather_ici(lay[n], n) for n in ws], {n: placed[n] for n in ws}) for ws in (trio, ("w_up",), ("w_down",))])

    got = _finish_exchange("gather_in_wait", fly_in, after=started)
    w_in_full = _exchange("gather_in_d2d", [[_gather_d2d(lay["w_in"], "w_in")]], bufs=got)["w_in"].reshape(IN_W, D_MODEL)
    xn, qkv, c3, gates = _inproj_fwd(xs, mix_norm, w_in_full, b_in, tm2)
    k2 = _Carry([_gather_d2d(lay[n], n) for n in trio], bufs=_finish_exchange("gather_trio_wait", fly_trio, after=qkv))
    attn = _attn_fwd(qkv, sk, comm=k2)
    w_ab, w_cb = k2.out["w_ab"], k2.out["w_cb"]
    w_out_full = k2.out["w_out"].reshape(D_MODEL, D_MODEL)
    k3 = _Carry([_gather_d2d(lay["w_up"], "w_up")], bufs=_finish_exchange("gather_up_wait", fly_up, after=attn))
    taps = _finish_exchange("taps_wait", taps_flight, after=attn)
    taps = lax.dynamic_update_slice(taps["slots"], taps["v"][None], (2 * me + lax.axis_index("c"), 0, 0))
    conv_full = taps[0::2, 0:3].transpose(1, 0, 2).reshape(3, CONV_W)
    ffn_cw_full = taps[0::2, t0:t0 + 33].reshape(N_CHIPS, 3, FF2 // N_CHIPS).transpose(1, 0, 2).reshape(3, FF2)
    conv, a, cv, merged, h1, hn = _mix_fwd(xs, attn, c3, gates, conv_full, w_ab, w_cb, w_out_full, ffn_norm, tm2, comm=k3)
    w_up_full = k3.out["w_up"]
    w_down_full = _exchange("gather_down_d2d", [[_gather_d2d(lay["w_down"], "w_down")]],
                            bufs=_finish_exchange("gather_down_wait", fly_down, after=hn))["w_down"].reshape(D_FF, D_MODEL)
    u, up, act, dh2, loss_part, g_fn = _ffn_fwd_loss(hn, h1, w_up_full, ffn_cw_full, w_down_full,
                                                     final_norm[None, :], target, tm)

    grads, sums, slots = {}, {}, {}

    def pair(*ws):
        return _Carry([_rs_pair(lay[n], "g_" + n, "t_" + n) for n in ws], reads={"g_" + n: grads[n] for n in ws},
                      fresh={"t_" + n: _theirs_shape(lay[n]) for n in ws})

    def chips(*ws, also=None):
        k = _Carry([_rs_chips(lay[n], "s_" + n, "r_" + n) for n in ws], reads={"s_" + n: sums[n] for n in ws},
                   fresh={"r_" + n: _slots_shape(lay[n]) for n in ws})
        if also is not None:
            k = _Carry(k.jobs + also.jobs, {**k.reads, **also.reads}, None, {**k.fresh, **also.fresh})
        return k

    def pair_sums(k, *ws):
        for n in ws:
            sums[n] = _pair_sum(grads[n], k.out["t_" + n], lay[n], pos, "pair_sum_" + n)

    def take_slots(k, *ws):
        for n in ws:
            slots[n] = k.out["r_" + n]

    dup, du, dh1, g_g2 = _ffn_bwd(dh2, up, h1, w_up_full, ffn_cw_full, w_down_full, ffn_norm, tm)
    grads["w_down"] = _wgrad(act, dh2, D_FF // 2, D_MODEL, bk, "wgrad_down").reshape(lay["w_down"].whole())
    k4 = pair("w_down")
    grads["w_up"], g_fcw = _wgrad_up_taps(hn, du, dup, u, FF2 // 4, bk, comm=k4)
    pair_sums(k4, "w_down")
    k5 = chips("w_down", also=pair("w_up"))
    dattn, da, dcv, dc3, dgt, g_cw = _mix_bwd(dh1, gates, a, cv, c3, conv_full, w_ab, w_cb, w_out_full, tm2, comm=k5)
    take_slots(k5, "w_down")
    pair_sums(k5, "w_up")
    grads["w_out"] = _wgrad(merged, dh1, D_MODEL, D_MODEL, bk, "wgrad_out").reshape(lay["w_out"].whole())
    grads["w_ab"] = _wgrad(attn, da, ATTN_W, D_MODEL, bk, "wgrad_ab")
    grads["w_cb"] = _wgrad(conv, dcv, CONV_W, D_MODEL, bk, "wgrad_cb")
    k6 = chips("w_up", also=pair("w_out", "w_ab", "w_cb"))
    dq, dk, dv, g_sk = _attn_bwd(qkv, sk, attn, dattn, comm=k6)
    take_slots(k6, "w_up")
    pair_sums(k6, "w_out", "w_ab", "w_cb")
    grad_x, dproj, g_b, g_g1 = _inproj_bwd(dq, dk, dv, dc3, dgt, w_in_full, xs, dh1, mix_norm, tm2)

    parts = [loss_part, g_g1, g_b, jnp.pad(g_sk[:, 0], (0, 120))[None, :], g_cw, g_g2, g_fcw, g_fn]
    packed, at = _pack_rows([p.reshape(-1, 128) for p in parts])
    small_flight, started = _start_exchange("small_start", [_to_all("v", "slots")],
                                            {"v": packed, "slots": jnp.zeros((N_DEV, *packed.shape), F32)})
    k8 = chips("w_out", "w_ab", "w_cb")
    k8.reads["after"] = started
    grads["w_in"] = _wgrad_in(xn, dproj, min(512, s), comm=k8).reshape(lay["w_in"].whole())
    take_slots(k8, "w_out", "w_ab", "w_cb")
    others = names[1:]
    in_flight, started = _start_exchange("rs_pair_in_start", [_rs_pair(lay["w_in"], "g", "t")],
                                         {"g": grads["w_in"], "t": _theirs_shape(lay["w_in"])})
    halves = {n: _chip_sum(sums[n], slots[n], lay[n], pos, "chip_sum_" + n, after=started) for n in ("w_up", "w_down")}
    landed = _finish_exchange("rs_pair_in_wait", in_flight, after=halves["w_down"])
    sums["w_in"] = _pair_sum(landed["g"], landed["t"], lay["w_in"], pos, "pair_sum_w_in")
    in_flight, started = _start_exchange("rs_chips_in_start", [_rs_chips(lay["w_in"], "s", "r")],
                                         {"s": sums["w_in"], "r": _slots_shape(lay["w_in"])})
    for n in trio:
        halves[n] = _chip_sum(sums[n], slots[n], lay[n], pos, "chip_sum_" + n, after=started)
    shared = _exchange("share_halves", [[_rs_share(lay[n], n) for n in others]], bufs=halves)
    w_of, m_of, v_of = dict(zip(names, big_w)), dict(zip(names, big_m)), dict(zip(names, big_v))

    def adam(n, g, after=None):
        return _rowwise(lambda w, g, m, v: (g, *_adamw(w, g, m, v)), [w_of[n], g, m_of[n], v_of[n]], [F32] * 4,
                        "adamw_" + n, after=after)

    new_of, last = {}, None
    for n in ("w_up", "w_down", "w_out", "w_ab", "w_cb"):
        new_of[n] = adam(n, shared[n], last)
        last = new_of[n][1]

    arrived = _finish_exchange("small_wait", small_flight, after=last)
    total = _sum_slots(arrived["v"], arrived["slots"], pos)
    part = lambda k: total[at[k]:at[k] + parts[k].size // 128].reshape(parts[k].shape)
    loss = total[0, 0]
    g_mix, g_b, g_g2, g_fn = part(1), part(2), part(5), part(7)
    g_sk = part(3)[:, 0:N_HEADS]
    g_cw = lax.dynamic_slice(part(4), (0, me * 128), (3, 128))
    g_fcw = lax.dynamic_slice(part(6), (0, me * (FF2 // N_CHIPS)), (3, FF2 // N_CHIPS))
    small_p = [
        (mix_norm, g_mix, m_mix_norm, v_mix_norm), (b_in, g_b, m_b_in, v_b_in), (sinks, g_sk, m_sinks, v_sinks),
        (conv_w[0], g_cw, m_conv_w[0], v_conv_w[0]), (ffn_norm, g_g2, m_ffn_norm, v_ffn_norm),
        (ffn_conv_w[0], g_fcw, m_ffn_conv_w[0], v_ffn_conv_w[0]),
        (final_norm[None, :], g_fn, m_final_norm[None, :], v_final_norm[None, :])]
    small_new = _adamw_small(small_p)
    small_new = [small_new[3 * k:3 * k + 3] for k in range(len(small_p))]

    landed = _finish_exchange("rs_chips_in_wait", in_flight, after=small_new[0][0])
    half_in = _chip_sum(landed["s"], landed["r"], lay["w_in"], pos, "chip_sum_w_in")
    shared["w_in"] = _exchange("share_in", [[_rs_share(lay["w_in"], "w_in")]], bufs={"w_in": half_in})["w_in"]
    new_of["w_in"] = adam("w_in", shared["w_in"])
    big_g = [new_of[n][0] for n in names]
    big_new = [new_of[n][1:] for n in names]

    order = [("s", 0), ("b", 0), ("s", 1), ("s", 2), ("s", 3), ("b", 1), ("b", 2), ("b", 3), ("s", 4), ("b", 4),
             ("s", 5), ("b", 5), ("s", 6)]
    shapes = [mix_norm.shape, w_in.shape, b_in.shape, sinks.shape, conv_w.shape, w_attn_branch.shape,
              w_conv_branch.shape, w_out.shape, ffn_norm.shape, w_up.shape, ffn_conv_w.shape, w_down.shape,
              final_norm.shape]
    small_g = [p[1] for p in small_p]
    big_g[0] = big_g[0].T
    big_new[0] = [a.T for a in big_new[0]]
    out_g = [(small_g[k] if kind == "s" else big_g[k]).reshape(shp) for (kind, k), shp in zip(order, shapes)]
    news = [[(small_new[k][j] if kind == "s" else big_new[k][j]).reshape(shp) for (kind, k), shp in zip(order, shapes)]
            for j in range(3)]
    return (loss, grad_x[None], *out_g, *news[0], *news[1], *news[2])
```

```python
import functools

import jax
import jax.numpy as jnp
from jax import lax
from jax.experimental import pallas as pl
from jax.experimental.pallas import tpu as pltpu

F32 = jnp.float32
BF16 = jnp.bfloat16

D_MODEL = 1024
HEAD_DIM = 64
N_HEADS = 8
N_KV_HEADS = 2
GROUP = N_HEADS // N_KV_HEADS
BLOCK = 128
ATTN_SCALE = HEAD_DIM ** -0.5
ATTN_W = N_HEADS * HEAD_DIM
KV_W = N_KV_HEADS * HEAD_DIM
CONV_W = 512
QKV_W = ATTN_W + 2 * KV_W
C3_W = 3 * CONV_W
GATES_W = 2 * D_MODEL
IN_W = QKV_W + C3_W + GATES_W
D_FF = 2816
FF2 = 2 * D_FF
NORM_EPS = 1e-5
N_CHIPS = 4
IN_SHARD = IN_W // N_CHIPS
NEG = -1e30

ADAM_LR = 0.001
ADAM_B1 = 0.9
ADAM_B2 = 0.999
ADAM_EPS = 1e-08
ADAM_WD = 0.01
ADAM_STEP = 10

VMEM_LIMIT = 56 * 1024 * 1024
MESH = pl.DeviceIdType.MESH

NT = (((1,), (1,)), ((), ()))
TN = (((0,), (0,)), ((), ()))


def _params(*sem):
    return pltpu.CompilerParams(dimension_semantics=sem, vmem_limit_bytes=VMEM_LIMIT)


def _resident(shape):
    return pl.BlockSpec(shape, lambda *_: (0,) * len(shape), pipeline_mode=pl.Buffered(1))


def _sigmoid(v):
    return 0.5 * jnp.tanh(0.5 * v) + 0.5


def _rstd(v):
    return lax.rsqrt(jnp.mean(v * v, axis=-1, keepdims=True) + NORM_EPS)


def _rms_bwd(dy, v, rstd, g):
    vhat = v * rstd
    t = dy * g
    return rstd * (t - vhat * jnp.mean(t * vhat, axis=-1, keepdims=True)), dy * vhat


def _taps(z, cw):
    return cw[2:3] * z + cw[1:2] * pltpu.roll(z, 1, 0) + cw[0:1] * pltpu.roll(z, 2, 0)


def _causal_conv(z, prev, cw):
    edge = _taps(jnp.concatenate([prev, z[0:8]], axis=0), cw)
    return jnp.concatenate([edge[8:16], _taps(z, cw)[8:]], axis=0)


def _rows_after(z, nxt):
    n = z.shape[0]
    edge = jnp.concatenate([z[n - 8:n], nxt], axis=0)
    return tuple(jnp.concatenate([pltpu.roll(z, n - k, 0)[:n - 8], pltpu.roll(edge, 16 - k, 0)[0:8]], axis=0)
                 for k in (1, 2))


def _inproj_fwd(x, g1, w_in, b_in, tm, comm=None):
    s = x.shape[0]

    def body(x_ref, g_ref, w_ref, b_ref, xn_ref, qkv_ref, c3_ref, gt_ref):
        xf = x_ref[...]
        xn = (xf * _rstd(xf) * g_ref[...]).astype(BF16)
        xn_ref[...] = xn

        def seg(a, b):
            return lax.dot_general(xn, w_ref[a:b, :], NT, preferred_element_type=F32) + b_ref[:, a:b]

        qkv_ref[...] = seg(0, QKV_W).astype(BF16)
        c3_ref[...] = seg(QKV_W, QKV_W + C3_W)
        gt_ref[...] = seg(QKV_W + C3_W, IN_W)

    row = lambda w: pl.BlockSpec((tm, w), lambda i: (i, 0))
    return _call(
        comm, body, name="inproj_fwd", grid=(s // tm,),
        in_specs=[row(D_MODEL), _resident((1, D_MODEL)), _resident((IN_W, D_MODEL)), _resident((1, IN_W))],
        out_specs=[row(D_MODEL), row(QKV_W), row(C3_W), row(GATES_W)],
        out_shape=[jax.ShapeDtypeStruct((s, D_MODEL), BF16), jax.ShapeDtypeStruct((s, QKV_W), BF16),
                   jax.ShapeDtypeStruct((s, C3_W), F32), jax.ShapeDtypeStruct((s, GATES_W), F32)],
        compiler_params=_params("parallel"),
    )(x, g1, w_in, b_in)


def _attn_bias():
    qi = (jnp.arange(GROUP * BLOCK) % BLOCK)[:, None]
    kj = jnp.arange(2 * BLOCK)[None, :]
    band = (kj > qi) & (kj <= qi + BLOCK)
    return jnp.stack([jnp.where(band & (kj >= BLOCK), 0.0, NEG), jnp.where(band, 0.0, NEG)]).astype(F32)


def _attn_bias_spec():
    return pl.BlockSpec((None, GROUP * BLOCK, 2 * BLOCK), lambda i: (jnp.minimum(i, 1), 0, 0))


def _sink_column(sk_ref, h):
    rows = lax.broadcasted_iota(jnp.int32, (GROUP * BLOCK, 1), 0)
    col = jnp.full((GROUP * BLOCK, 1), sk_ref[h * GROUP], F32)
    for g in range(1, GROUP):
        col = jnp.where(rows >= g * BLOCK, sk_ref[h * GROUP + g], col)
    return col


def _stack_heads(t, h):
    return jnp.concatenate(
        [t[:, (h * GROUP + g) * HEAD_DIM:(h * GROUP + g + 1) * HEAD_DIM] for g in range(GROUP)], axis=0)


def _unstack_heads(per_kv):
    return jnp.concatenate(
        [t[g * BLOCK:(g + 1) * BLOCK] for t in per_kv for g in range(GROUP)], axis=1)


def _attn_specs(nb):
    cur = lambda i: jnp.minimum(i, nb - 1)
    prev = lambda i: jnp.maximum(jnp.minimum(i, nb - 1) - 1, 0)
    q = pl.BlockSpec((BLOCK, ATTN_W), lambda i: (cur(i), 0))
    kp = pl.BlockSpec((BLOCK, KV_W), lambda i: (prev(i), ATTN_W // KV_W))
    kc = pl.BlockSpec((BLOCK, KV_W), lambda i: (cur(i), ATTN_W // KV_W))
    vp = pl.BlockSpec((BLOCK, KV_W), lambda i: (prev(i), ATTN_W // KV_W + 1))
    vc = pl.BlockSpec((BLOCK, KV_W), lambda i: (cur(i), ATTN_W // KV_W + 1))
    return q, kp, kc, vp, vc


def _attn_fwd(qkv, sinks, comm=None):
    s = qkv.shape[0]
    nb = s // BLOCK

    def body(sk_ref, bias_ref, q_ref, kp_ref, kc_ref, vp_ref, vc_ref, o_ref):
        bias = bias_ref[...]
        q, kp, kc, vp, vc = q_ref[...], kp_ref[...], kc_ref[...], vp_ref[...], vc_ref[...]
        outs = []
        for h in range(N_KV_HEADS):
            hs = slice(h * HEAD_DIM, (h + 1) * HEAD_DIM)
            k2 = jnp.concatenate([kp[:, hs], kc[:, hs]], axis=0)
            v2 = jnp.concatenate([vp[:, hs], vc[:, hs]], axis=0)
            sc = lax.dot_general(_stack_heads(q, h), k2, NT, preferred_element_type=F32) * ATTN_SCALE + bias
            sink = _sink_column(sk_ref, h)
            m = jnp.maximum(jnp.max(sc, axis=1, keepdims=True), sink)
            p = jnp.exp(sc - m)
            den = jnp.sum(p, axis=1, keepdims=True) + jnp.exp(sink - m)
            outs.append(jnp.dot(p.astype(BF16), v2, preferred_element_type=F32) / den)
        o_ref[...] = _unstack_heads(outs).astype(BF16)

    return _call(
        comm, body, name="attn_fwd", grid=(nb,),
        in_specs=[pl.BlockSpec(memory_space=pltpu.SMEM), _attn_bias_spec(), *_attn_specs(nb)],
        out_specs=pl.BlockSpec((BLOCK, ATTN_W), lambda i: (i, 0)),
        out_shape=jax.ShapeDtypeStruct((s, ATTN_W), BF16),
        compiler_params=_params("parallel"),
    )(sinks, _attn_bias(), qkv, qkv, qkv, qkv, qkv)


def _mix_fwd(x, attn, c3, gates, conv_w, w_ab, w_cb, w_out, g2, tm, comm=None):
    s = x.shape[0]

    def body(x_ref, at_ref, c3_ref, gt_ref, cw_ref, wab_ref, wcb_ref, wo_ref, g_ref,
             conv_ref, a_ref, cv_ref, mg_ref, h1_ref, hn_ref, carry_ref):
        @pl.when(pl.program_id(0) == 0)
        def _():
            carry_ref[...] = jnp.zeros_like(carry_ref)

        c3v = c3_ref[...]
        cb, cc, cx = c3v[:, :CONV_W], c3v[:, CONV_W:2 * CONV_W], c3v[:, 2 * CONV_W:]
        z = cc * cx
        cz = _causal_conv(z, carry_ref[...], cw_ref[...])
        carry_ref[...] = z[tm - 8:tm]
        conv = (cb * cz).astype(BF16)
        conv_ref[...] = conv
        a = jnp.dot(at_ref[...], wab_ref[...], preferred_element_type=F32)
        cv = jnp.dot(conv, wcb_ref[...], preferred_element_type=F32)
        a_ref[...] = a.astype(BF16)
        cv_ref[...] = cv.astype(BF16)
        gt = gt_ref[...]
        merged = (_sigmoid(gt[:, :D_MODEL]) * a + _sigmoid(gt[:, D_MODEL:]) * cv).astype(BF16)
        mg_ref[...] = merged
        h1 = x_ref[...] + jnp.dot(merged, wo_ref[...], preferred_element_type=F32)
        h1_ref[...] = h1
        hn_ref[...] = (h1 * _rstd(h1) * g_ref[...]).astype(BF16)

    row = lambda w: pl.BlockSpec((tm, w), lambda i: (i, 0))
    return _call(
        comm, body, name="mix_fwd", grid=(s // tm,),
        in_specs=[row(D_MODEL), row(ATTN_W), row(C3_W), row(GATES_W), _resident((3, CONV_W)),
                  _resident((ATTN_W, D_MODEL)), _resident((CONV_W, D_MODEL)), _resident((D_MODEL, D_MODEL)),
                  _resident((1, D_MODEL))],
        out_specs=[row(CONV_W), row(D_MODEL), row(D_MODEL), row(D_MODEL), row(D_MODEL), row(D_MODEL)],
        out_shape=[jax.ShapeDtypeStruct((s, CONV_W), BF16), jax.ShapeDtypeStruct((s, D_MODEL), BF16),
                   jax.ShapeDtypeStruct((s, D_MODEL), BF16), jax.ShapeDtypeStruct((s, D_MODEL), BF16),
                   jax.ShapeDtypeStruct((s, D_MODEL), F32), jax.ShapeDtypeStruct((s, D_MODEL), BF16)],
        scratch_shapes=[pltpu.VMEM((8, CONV_W), F32)],
        compiler_params=_params("arbitrary"),
    )(x, attn, c3, gates, conv_w, w_ab, w_cb, w_out, g2)


def _ffn_fwd_loss(hn, h1, w_up, ffn_cw, w_down, g3, target, tm):
    s = hn.shape[0]

    def body(hn_ref, h1_ref, wu_ref, cw_ref, wd_ref, g_ref, t_ref,
             u_ref, up_ref, act_ref, dh2_ref, loss_ref, gfn_ref, carry_ref):
        @pl.when(pl.program_id(0) == 0)
        def _():
            carry_ref[...] = jnp.zeros_like(carry_ref)
            loss_ref[...] = jnp.zeros_like(loss_ref)
            gfn_ref[...] = jnp.zeros_like(gfn_ref)

        u = jnp.dot(hn_ref[...], wu_ref[...], preferred_element_type=F32)
        u_ref[...] = u.astype(BF16)
        up = _causal_conv(u, carry_ref[...], cw_ref[...])
        up_ref[...] = up
        carry_ref[...] = u[tm - 8:tm]
        gate, val = up[:, :D_FF], up[:, D_FF:]
        act = (gate * _sigmoid(gate) * val).astype(BF16)
        act_ref[...] = act
        h2 = h1_ref[...] + jnp.dot(act, wd_ref[...], preferred_element_type=F32)
        rstd = _rstd(h2)
        g = g_ref[...]
        err = h2 * rstd * g - t_ref[...]
        loss_ref[...] += jnp.sum(err * err) * (0.5 / D_MODEL)
        dh2, dg = _rms_bwd(err * (1.0 / D_MODEL), h2, rstd, g)
        dh2_ref[...] = dh2
        gfn_ref[...] += jnp.sum(dg, axis=0, keepdims=True)

    row = lambda w: pl.BlockSpec((tm, w), lambda i: (i, 0))
    acc = lambda w: pl.BlockSpec((1, w), lambda i: (0, 0))
    return pl.pallas_call(
        body, name="ffn_fwd_loss", grid=(s // tm,),
        in_specs=[row(D_MODEL), row(D_MODEL), _resident((D_MODEL, FF2)), _resident((3, FF2)),
                  _resident((D_FF, D_MODEL)), _resident((1, D_MODEL)), row(D_MODEL)],
        out_specs=[row(FF2), row(FF2), row(D_FF), row(D_MODEL), acc(128), acc(D_MODEL)],
        out_shape=[jax.ShapeDtypeStruct((s, FF2), BF16), jax.ShapeDtypeStruct((s, FF2), F32),
                   jax.ShapeDtypeStruct((s, D_FF), BF16),
                   jax.ShapeDtypeStruct((s, D_MODEL), F32), jax.ShapeDtypeStruct((1, 128), F32),
                   jax.ShapeDtypeStruct((1, D_MODEL), F32)],
        scratch_shapes=[pltpu.VMEM((8, FF2), F32)],
        compiler_params=_params("arbitrary"),
    )(hn, h1, w_up, ffn_cw, w_down, g3, target)


def _ffn_bwd(dh2, u, up, h1, w_up, ffn_cw, w_down, g2, tm):
    s = dh2.shape[0]
    nt = s // tm

    def body(dh2_ref, u_ref, up_ref, h1_ref, wu_ref, cw_ref, wd_ref, g_ref,
             du_ref, dh1_ref, gcw_ref, gg_ref, carry_ref):
        @pl.when(pl.program_id(0) == 0)
        def _():
            carry_ref[...] = jnp.zeros_like(carry_ref)
            gcw_ref[...] = jnp.zeros_like(gcw_ref)
            gg_ref[...] = jnp.zeros_like(gg_ref)

        dh2v = dh2_ref[...]
        dact = lax.dot_general(dh2v.astype(BF16), wd_ref[...], NT, preferred_element_type=F32)
        upv = up_ref[...]
        gate, val = upv[:, :D_FF], upv[:, D_FF:]
        sg = _sigmoid(gate)
        dval = dact * (gate * sg)
        dgate = dact * val * (sg * (1.0 + gate * (1.0 - sg)))
        dup = jnp.concatenate([dgate, dval], axis=1)
        dup1, dup2 = _rows_after(dup, carry_ref[...])
        carry_ref[...] = dup[0:8]
        u = u_ref[...].astype(F32)
        gcw_ref[2:3, :] += jnp.sum(dup * u, axis=0, keepdims=True)
        gcw_ref[1:2, :] += jnp.sum(dup1 * u, axis=0, keepdims=True)
        gcw_ref[0:1, :] += jnp.sum(dup2 * u, axis=0, keepdims=True)
        cw = cw_ref[...]
        du = (cw[2:3] * dup + cw[1:2] * dup1 + cw[0:1] * dup2).astype(BF16)
        du_ref[...] = du
        dhn = lax.dot_general(du, wu_ref[...], NT, preferred_element_type=F32)
        h1v = h1_ref[...]
        dh1, dg = _rms_bwd(dhn, h1v, _rstd(h1v), g_ref[...])
        dh1_ref[...] = dh2v + dh1
        gg_ref[...] += jnp.sum(dg, axis=0, keepdims=True)

    row = lambda w: pl.BlockSpec((tm, w), lambda i: (nt - 1 - i, 0))
    return pl.pallas_call(
        body, name="ffn_bwd", grid=(nt,),
        in_specs=[row(D_MODEL), row(FF2), row(FF2),
                  row(D_MODEL), _resident((D_MODEL, FF2)), _resident((3, FF2)), _resident((D_FF, D_MODEL)),
                  _resident((1, D_MODEL))],
        out_specs=[row(FF2), row(D_MODEL), pl.BlockSpec((3, FF2), lambda i: (0, 0)),
                   pl.BlockSpec((1, D_MODEL), lambda i: (0, 0))],
        out_shape=[jax.ShapeDtypeStruct((s, FF2), BF16), jax.ShapeDtypeStruct((s, D_MODEL), F32),
                   jax.ShapeDtypeStruct((3, FF2), F32), jax.ShapeDtypeStruct((1, D_MODEL), F32)],
        scratch_shapes=[pltpu.VMEM((8, FF2), F32)],
        compiler_params=_params("arbitrary"),
    )(dh2, u, up, h1, w_up, ffn_cw, w_down, g2)


def _mix_bwd(dh1, gates, a, cv, c3, conv_w, w_ab, w_cb, w_out, tm, comm=None):
    s = dh1.shape[0]
    nt = s // tm
    halo = 8

    def body(dh1_ref, gt_ref, a_ref, cv_ref, c3_ref, ch_ref, cw_ref, wab_ref, wcb_ref, wo_ref,
             dat_ref, da_ref, dcv_ref, dc3_ref, dgt_ref, gcw_ref, carry_ref):
        i = pl.program_id(0)

        @pl.when(i == 0)
        def _():
            carry_ref[...] = jnp.zeros_like(carry_ref)
            gcw_ref[...] = jnp.zeros_like(gcw_ref)

        dm = lax.dot_general(dh1_ref[...].astype(BF16), wo_ref[...], NT, preferred_element_type=F32)
        gt = gt_ref[...]
        sa, sc = _sigmoid(gt[:, :D_MODEL]), _sigmoid(gt[:, D_MODEL:])
        da = (dm * sa).astype(BF16)
        dcv = (dm * sc).astype(BF16)
        da_ref[...] = da
        dcv_ref[...] = dcv
        dgt_ref[...] = jnp.concatenate(
            [dm * a_ref[...].astype(F32) * (sa * (1.0 - sa)), dm * cv_ref[...].astype(F32) * (sc * (1.0 - sc))],
            axis=1).astype(BF16)
        dat_ref[...] = lax.dot_general(da, wab_ref[...], NT, preferred_element_type=F32).astype(BF16)
        dconv = lax.dot_general(dcv, wcb_ref[...], NT, preferred_element_type=F32)
        c3v = c3_ref[...]
        cb, cc, cx = c3v[:, :CONV_W], c3v[:, CONV_W:2 * CONV_W], c3v[:, 2 * CONV_W:]
        z = cc * cx
        chv = ch_ref[...] * (i < nt - 1).astype(F32)
        zh = chv[:, CONV_W:2 * CONV_W] * chv[:, 2 * CONV_W:]
        cw = cw_ref[...]
        cz = _causal_conv(z, zh, cw)
        dcz = dconv * cb
        dcz1, dcz2 = _rows_after(dcz, carry_ref[...])
        carry_ref[...] = dcz[0:8]
        gcw_ref[2:3, :] += jnp.sum(dcz * z, axis=0, keepdims=True)
        gcw_ref[1:2, :] += jnp.sum(dcz1 * z, axis=0, keepdims=True)
        gcw_ref[0:1, :] += jnp.sum(dcz2 * z, axis=0, keepdims=True)
        dz = cw[2:3] * dcz + cw[1:2] * dcz1 + cw[0:1] * dcz2
        dc3_ref[...] = jnp.concatenate([dconv * cz, dz * cx, dz * cc], axis=1).astype(BF16)

    row = lambda w: pl.BlockSpec((tm, w), lambda i: (nt - 1 - i, 0))
    return _call(
        comm, body, name="mix_bwd", grid=(nt,),
        in_specs=[row(D_MODEL), row(GATES_W), row(D_MODEL), row(D_MODEL), row(C3_W),
                  pl.BlockSpec((halo, C3_W), lambda i: (jnp.maximum((nt - 1 - i) * (tm // halo) - 1, 0), 0)),
                  _resident((3, CONV_W)), _resident((ATTN_W, D_MODEL)), _resident((CONV_W, D_MODEL)),
                  _resident((D_MODEL, D_MODEL))],
        out_specs=[row(ATTN_W), row(D_MODEL), row(D_MODEL), row(C3_W), row(GATES_W),
                   pl.BlockSpec((3, CONV_W), lambda i: (0, 0))],
        out_shape=[jax.ShapeDtypeStruct((s, ATTN_W), BF16), jax.ShapeDtypeStruct((s, D_MODEL), BF16),
                   jax.ShapeDtypeStruct((s, D_MODEL), BF16), jax.ShapeDtypeStruct((s, C3_W), BF16),
                   jax.ShapeDtypeStruct((s, GATES_W), BF16), jax.ShapeDtypeStruct((3, CONV_W), F32)],
        scratch_shapes=[pltpu.VMEM((8, CONV_W), F32)],
        compiler_params=_params("arbitrary"),
    )(dh1, gates, a, cv, c3, c3, conv_w, w_ab, w_cb, w_out)


def _attn_bwd(qkv, sinks, o, do, comm=None):
    s = qkv.shape[0]
    nb = s // BLOCK

    def body(sk_ref, bias_ref, q_ref, kp_ref, kc_ref, vp_ref, vc_ref, o_ref, do_ref,
             dq_ref, dk_ref, dv_ref, dsk_ref, ck_ref, cvv_ref):
        i = pl.program_id(0)

        @pl.when(i == 0)
        def _():
            ck_ref[...] = jnp.zeros_like(ck_ref)
            cvv_ref[...] = jnp.zeros_like(cvv_ref)
            dsk_ref[...] = jnp.zeros_like(dsk_ref)

        @pl.when(i < nb)
        def _():
            bias = bias_ref[...]
            q, kp, kc, vp, vc = q_ref[...], kp_ref[...], kc_ref[...], vp_ref[...], vc_ref[...]
            ov, dov = o_ref[...], do_ref[...]
            dqs, dks, dvs = [], [], []
            for h in range(N_KV_HEADS):
                hs = slice(h * HEAD_DIM, (h + 1) * HEAD_DIM)
                k2 = jnp.concatenate([kp[:, hs], kc[:, hs]], axis=0)
                v2 = jnp.concatenate([vp[:, hs], vc[:, hs]], axis=0)
                qg, og, dog = _stack_heads(q, h), _stack_heads(ov, h), _stack_heads(dov, h)
                sc = lax.dot_general(qg, k2, NT, preferred_element_type=F32) * ATTN_SCALE + bias
                sink = _sink_column(sk_ref, h)
                m = jnp.maximum(jnp.max(sc, axis=1, keepdims=True), sink)
                p = jnp.exp(sc - m)
                psink = jnp.exp(sink - m)
                inv = 1.0 / (jnp.sum(p, axis=1, keepdims=True) + psink)
                p = p * inv
                delta = jnp.sum(dog.astype(F32) * og.astype(F32), axis=1, keepdims=True)
                dp = lax.dot_general(dog, v2, NT, preferred_element_type=F32)
                ds = (p * (dp - delta)).astype(BF16)
                dqs.append(jnp.dot(ds, k2, preferred_element_type=F32) * ATTN_SCALE)
                dks.append(lax.dot_general(ds, qg, TN, preferred_element_type=F32) * ATTN_SCALE)
                dvs.append(lax.dot_general(p.astype(BF16), dog, TN, preferred_element_type=F32))
                dsink = -(psink * inv * delta)
                for g in range(GROUP):
                    r = h * GROUP + g
                    dsk_ref[r:r + 1, :] += jnp.sum(dsink[g * BLOCK:(g + 1) * BLOCK])
            dq_ref[...] = _unstack_heads(dqs).astype(BF16)
            dk2 = jnp.concatenate(dks, axis=1)
            dv2 = jnp.concatenate(dvs, axis=1)
            dk_ref[...] = (ck_ref[...] + dk2[:BLOCK]).astype(BF16)
            dv_ref[...] = (cvv_ref[...] + dv2[:BLOCK]).astype(BF16)
            ck_ref[...] = dk2[BLOCK:]
            cvv_ref[...] = dv2[BLOCK:]

        @pl.when(i == nb)
        def _():
            dk_ref[...] = ck_ref[...].astype(BF16)
            dv_ref[...] = cvv_ref[...].astype(BF16)

    cur = lambda i: jnp.minimum(i, nb - 1)
    done = lambda i: jnp.maximum(i - 1, 0)
    return _call(
        comm, body, name="attn_bwd", grid=(nb + 1,),
        in_specs=[pl.BlockSpec(memory_space=pltpu.SMEM), _attn_bias_spec(), *_attn_specs(nb),
                  pl.BlockSpec((BLOCK, ATTN_W), lambda i: (cur(i), 0)),
                  pl.BlockSpec((BLOCK, ATTN_W), lambda i: (cur(i), 0))],
        out_specs=[pl.BlockSpec((BLOCK, ATTN_W), lambda i: (cur(i), 0)),
                   pl.BlockSpec((BLOCK, KV_W), lambda i: (done(i), 0)),
                   pl.BlockSpec((BLOCK, KV_W), lambda i: (done(i), 0)),
                   pl.BlockSpec((N_HEADS, 128), lambda i: (0, 0))],
        out_shape=[jax.ShapeDtypeStruct((s, ATTN_W), BF16), jax.ShapeDtypeStruct((s, KV_W), BF16),
                   jax.ShapeDtypeStruct((s, KV_W), BF16), jax.ShapeDtypeStruct((N_HEADS, 128), F32)],
        scratch_shapes=[pltpu.VMEM((BLOCK, KV_W), F32), pltpu.VMEM((BLOCK, KV_W), F32)],
        compiler_params=_params("arbitrary"),
    )(sinks, _attn_bias(), qkv, qkv, qkv, qkv, qkv, o, do)


def _inproj_bwd(dq, dk, dv, dc3, dgt, w_in, x, dh1, g1, tm, comm=None):
    s = x.shape[0]

    def body(dq_ref, dk_ref, dv_ref, dc3_ref, dgt_ref, w_ref, x_ref, dh1_ref, g_ref,
             dx_ref, dp_ref, gb_ref, gg_ref):
        @pl.when(pl.program_id(0) == 0)
        def _():
            gb_ref[...] = jnp.zeros_like(gb_ref)
            gg_ref[...] = jnp.zeros_like(gg_ref)

        dp = jnp.concatenate([dq_ref[...], dk_ref[...], dv_ref[...], dc3_ref[...], dgt_ref[...]], axis=1)
        dp_ref[...] = dp
        gb_ref[...] += jnp.sum(dp.astype(F32), axis=0, keepdims=True)
        dxn = jnp.dot(dp, w_ref[...], preferred_element_type=F32)
        xf = x_ref[...]
        dx, dg = _rms_bwd(dxn, xf, _rstd(xf), g_ref[...])
        dx_ref[...] = dh1_ref[...] + dx
        gg_ref[...] += jnp.sum(dg, axis=0, keepdims=True)

    row = lambda w: pl.BlockSpec((tm, w), lambda i: (i, 0))
    acc = lambda w: pl.BlockSpec((1, w), lambda i: (0, 0))
    return _call(
        comm, body, name="inproj_bwd", grid=(s // tm,),
        in_specs=[row(ATTN_W), row(KV_W), row(KV_W), row(C3_W), row(GATES_W), _resident((IN_W, D_MODEL)),
                  row(D_MODEL), row(D_MODEL), _resident((1, D_MODEL))],
        out_specs=[row(D_MODEL), row(IN_W), acc(IN_W), acc(D_MODEL)],
        out_shape=[jax.ShapeDtypeStruct((s, D_MODEL), F32), jax.ShapeDtypeStruct((s, IN_W), BF16),
                   jax.ShapeDtypeStruct((1, IN_W), F32), jax.ShapeDtypeStruct((1, D_MODEL), F32)],
        compiler_params=_params("arbitrary"),
    )(dq, dk, dv, dc3, dgt, w_in, x, dh1, g1)


def _wgrad(a, b, bm, bn, bk, name, comm=None):
    s, m = a.shape
    n = b.shape[1]
    nk = s // bk

    def body(a_ref, b_ref, o_ref, acc_ref):
        k = pl.program_id(2)

        @pl.when(k == 0)
        def _():
            acc_ref[...] = jnp.zeros_like(acc_ref)

        acc_ref[...] += lax.dot_general(a_ref[...].astype(BF16), b_ref[...].astype(BF16), TN,
                                        preferred_element_type=F32)

        @pl.when(k == nk - 1)
        def _():
            o_ref[...] = acc_ref[...].astype(BF16)

    return _call(
        comm, body, name=name, grid=(m // bm, n // bn, nk),
        in_specs=[pl.BlockSpec((bk, bm), lambda i, j, k: (k, i)), pl.BlockSpec((bk, bn), lambda i, j, k: (k, j))],
        out_specs=pl.BlockSpec((bm, bn), lambda i, j, k: (i, j)),
        out_shape=jax.ShapeDtypeStruct((m, n), BF16),
        scratch_shapes=[pltpu.VMEM((bm, bn), F32)],
        compiler_params=_params("parallel", "parallel", "arbitrary"),
    )(a, b)


class _Carry:
    def __init__(self, jobs, reads=None, bufs=None, fresh=None):
        self.jobs, self.reads, self.bufs, self.fresh = jobs, reads or {}, bufs or {}, fresh or {}
        self.out = {}


class _Job:
    def __init__(self, n_sems, plan):
        self.n_sems, self.plan = n_sems, plan


def _plan_all(jobs, hbm, send, recv):
    pos = _position()
    starts, waits, base = [], [], 0
    for job in jobs:
        s, w = job.plan(hbm, pos, send, recv, base)
        starts, waits, base = starts + s, waits + w, base + job.n_sems
    return starts, waits


def _call(comm, body, **kw):
    if comm is None:
        return pl.pallas_call(body, **kw)
    grid = kw["grid"]
    single = not isinstance(kw["out_shape"], (list, tuple))
    out_shape = [kw["out_shape"]] if single else list(kw["out_shape"])
    out_specs = [kw["out_specs"]] if single else list(kw["out_specs"])
    in_specs = list(kw["in_specs"])
    scratch = list(kw.get("scratch_shapes", ()))
    r_names, b_names, f_names = list(comm.reads), list(comm.bufs), list(comm.fresh)
    n_args, n_out, n_scr = len(in_specs), len(out_shape), len(scratch)
    n_sems = sum(j.n_sems for j in comm.jobs)

    def wrapped(*refs):
        k = n_args
        hbm = dict(zip(r_names, refs[k:k + len(r_names)]))
        k += len(r_names) + len(b_names)
        outs = refs[k:k + n_out]
        k += n_out
        hbm.update(zip(b_names + f_names, refs[k:k + len(b_names) + len(f_names)]))
        k += len(b_names) + len(f_names)
        send, recv = refs[k + n_scr:]
        starts, waits = _plan_all(comm.jobs, hbm, send, recv)
        ids = [pl.program_id(a) for a in range(len(grid))]
        first = functools.reduce(jnp.logical_and, [i == 0 for i in ids])
        last = functools.reduce(jnp.logical_and, [i == g - 1 for i, g in zip(ids, grid)])

        @pl.when(first)
        def _():
            for cp in starts:
                cp.start()

        body(*refs[:n_args], *outs, *refs[k:k + n_scr])

        @pl.when(last)
        def _():
            for cp in waits:
                cp.wait_recv()
            for cp in starts:
                cp.wait_send()

    sems = pltpu.SemaphoreType.DMA((n_sems,))
    held = [jax.ShapeDtypeStruct(a.shape, a.dtype) for a in comm.bufs.values()] + list(comm.fresh.values())
    call = pl.pallas_call(
        wrapped, name=kw["name"], grid=grid,
        in_specs=in_specs + [_ANY] * (len(r_names) + len(b_names)),
        out_specs=out_specs + [_ANY] * len(held),
        out_shape=out_shape + held,
        input_output_aliases={n_args + len(r_names) + i: n_out + i for i in range(len(b_names))},
        scratch_shapes=scratch + [sems, sems],
        compiler_params=_params(*["arbitrary"] * len(grid)),
    )

    def run(*args):
        res = call(*args, *comm.reads.values(), *comm.bufs.values())
        comm.out = dict(zip(b_names + f_names, res[n_out:]))
        return res[0] if single else res[:n_out]

    return run


def _exchange(name, phases, reads=None, bufs=None, fresh=None):
    comm = _Carry([j for ph in phases for j in ph], reads, bufs, fresh)
    r_names, b_names, f_names = list(comm.reads), list(comm.bufs), list(comm.fresh)
    n_sems = sum(j.n_sems for j in comm.jobs)

    def body(*refs):
        hbm = dict(zip(r_names, refs[:len(r_names)]))
        k = len(r_names) + len(b_names)
        hbm.update(zip(b_names + f_names, refs[k:k + len(b_names) + len(f_names)]))
        send, recv = refs[-2:]
        pos = _position()
        started, base = [], 0
        for ph in phases:
            waits = []
            for job in ph:
                s, w = job.plan(hbm, pos, send, recv, base)
                base += job.n_sems
                for cp in s:
                    cp.start()
                started, waits = started + s, waits + w
            for cp in waits:
                cp.wait_recv()
        for cp in started:
            cp.wait_send()

    sems = pltpu.SemaphoreType.DMA((n_sems,))
    held = [jax.ShapeDtypeStruct(a.shape, a.dtype) for a in comm.bufs.values()] + list(comm.fresh.values())
    res = pl.pallas_call(
        body, name=name, in_specs=[_ANY] * (len(r_names) + len(b_names)), out_specs=[_ANY] * len(held),
        out_shape=held, input_output_aliases={len(r_names) + i: i for i in range(len(b_names))},
        scratch_shapes=[sems, sems],
    )(*comm.reads.values(), *comm.bufs.values())
    return dict(zip(b_names + f_names, res))


_HBM = pl.BlockSpec(memory_space=pltpu.HBM)
_SEM = pl.BlockSpec(memory_space=pltpu.SEMAPHORE)
_EFFECT = pltpu.SideEffectType.DATAFLOW_SIDE_EFFECTING


def _start_exchanges(name, groups):
    names = [list(arrays) for _, arrays in groups]
    first = [sum(len(ns) for ns in names[:g]) for g in range(len(groups))]
    n, ng = sum(len(ns) for ns in names), len(groups)

    def body(*refs):
        for g, (jobs, _) in enumerate(groups):
            hbm = dict(zip(names[g], refs[first[g]:first[g] + len(names[g])]))
            for cp in _plan_all(jobs, hbm, refs[n + 2 * g], refs[n + 2 * g + 1])[0]:
                cp.start()
        refs[-1][...] = jnp.zeros_like(refs[-1])

    given = [pltpu.with_memory_space_constraint(
        a if isinstance(a, jax.Array) else lax.empty(a.shape, a.dtype), pltpu.HBM)
        for _, arrays in groups for a in arrays.values()]
    sems = [pltpu.SemaphoreType.DMA((sum(j.n_sems for j in jobs),)) for jobs, _ in groups for _ in range(2)]
    res = pl.pallas_call(
        body, name=name,
        out_shape=(*sems, *[pltpu.HBM(a.shape, a.dtype) for a in given], jax.ShapeDtypeStruct((8, 128), F32)),
        in_specs=[_HBM] * n, out_specs=(*[_SEM] * (2 * ng), *[_HBM] * n, pl.BlockSpec(memory_space=pltpu.VMEM)),
        input_output_aliases={i: 2 * ng + i for i in range(n)},
        compiler_params=pltpu.CompilerParams(has_side_effects=_EFFECT),
    )(*given)
    held = res[2 * ng:2 * ng + n]
    states = [(names[g], groups[g][0], res[2 * g], res[2 * g + 1], held[first[g]:first[g] + len(names[g])])
              for g in range(ng)]
    return states, res[-1]


def _start_exchange(name, jobs, arrays):
    states, token = _start_exchanges(name, [(jobs, arrays)])
    return states[0], token


def _finish_exchange(name, state, after):
    names, jobs, send_sem, recv_sem, held = state
    n = len(names)

    def body(*refs):
        hbm = dict(zip(names, refs[:n]))
        send, recv = refs[n:n + 2]
        starts, waits = _plan_all(jobs, hbm, send, recv)
        for cp in waits:
            cp.wait_recv()
        for cp in starts:
            cp.wait_send()

    res = pl.pallas_call(
        body, name=name, out_shape=tuple(pltpu.HBM(a.shape, a.dtype) for a in held),
        in_specs=[_HBM] * n + [_SEM, _SEM, _ANY], out_specs=tuple([_HBM] * n),
        input_output_aliases={i: i for i in range(n)},
        compiler_params=pltpu.CompilerParams(has_side_effects=_EFFECT),
    )(*held, send_sem, recv_sem, after)
    return dict(zip(names, res))


def _row_tile(rows, bytes_per_row):
    best = 16
    for t in range(16, rows + 1, 16):
        if rows % t == 0 and t * bytes_per_row <= 9 * 1024 * 1024:
            best = t
    return best


def _rowwise(fn, ins, out_dtypes, name, after=None):
    rows, cols = ins[0].shape
    per_row = sum(cols * a.dtype.itemsize for a in ins) + sum(cols * jnp.dtype(d).itemsize for d in out_dtypes)
    tr = _row_tile(rows, per_row)
    n_in = len(ins)

    def body(*refs):
        outs = fn(*[r[...] for r in refs[:n_in]])
        for o_ref, o in zip(refs[-len(out_dtypes):], outs):
            o_ref[...] = o.astype(o_ref.dtype)

    tile = pl.BlockSpec((tr, cols), lambda i: (i, 0))
    behind = [] if after is None else [after]
    return pl.pallas_call(
        body, name=name, grid=(rows // tr,),
        in_specs=[tile] * n_in + [pl.BlockSpec((8, 128), lambda i: (0, 0))] * len(behind),
        out_specs=[tile] * len(out_dtypes),
        out_shape=[jax.ShapeDtypeStruct((rows, cols), d) for d in out_dtypes],
        compiler_params=_params("parallel"),
    )(*ins, *behind)


def _tiled(fn, name, grid, pos, ins, outs):
    n_in = len(ins)

    def body(pos_ref, *refs):
        res = fn(*[r[...] for r in refs[:n_in]])
        for o_ref, o in zip(refs[n_in:], res):
            o_ref[...] = o.astype(o_ref.dtype)

    return pl.pallas_call(
        body, name=name,
        grid_spec=pltpu.PrefetchScalarGridSpec(
            num_scalar_prefetch=1, grid=grid,
            in_specs=[pl.BlockSpec(bs, im) for _, bs, im in ins],
            out_specs=[pl.BlockSpec(bs, im) for _, _, bs, im in outs]),
        out_shape=[jax.ShapeDtypeStruct(s, d) for s, d, _, _ in outs],
        compiler_params=_params("parallel"),
    )(pos, *[a for a, _, _ in ins])


def _adamw(w, g, m, v):
    m = ADAM_B1 * m + (1.0 - ADAM_B1) * g
    v = ADAM_B2 * v + (1.0 - ADAM_B2) * (g * g)
    m_hat = m / (1.0 - ADAM_B1 ** ADAM_STEP)
    v_hat = v / (1.0 - ADAM_B2 ** ADAM_STEP)
    return -ADAM_LR * (m_hat / (jnp.sqrt(v_hat) + ADAM_EPS) + ADAM_WD * w), m, v


def _adamw_small(params):
    n = len(params)

    def body(*refs):
        for k in range(n):
            w, g, m, v = (r[...] for r in refs[4 * k:4 * k + 4])
            for o_ref, o in zip(refs[4 * n + 3 * k:4 * n + 3 * k + 3], _adamw(w, g, m, v)):
                o_ref[...] = o

    flat = [a for p in params for a in p]
    return pl.pallas_call(
        body, name="adamw_small",
        out_shape=[jax.ShapeDtypeStruct(p[0].shape, F32) for p in params for _ in range(3)],
    )(*flat)


class _Layout:
    def __init__(self, rows, cols, stacked):
        self.rows, self.cols, self.stacked = rows, cols, stacked

    def whole(self, rows=None):
        r = self.rows if rows is None else rows
        return (N_CHIPS, r, self.cols) if self.stacked else (r, N_CHIPS * self.cols)

    def part_rows(self, h, q=0, nq=1):
        n = self.rows // 2 // nq
        return pl.ds(pl.multiple_of(h * (self.rows // 2) + q * n, 16), n)

    def half_rows(self, h):
        return self.part_rows(h)

    def block(self, ref, p, rows=slice(None)):
        if self.stacked:
            return ref.at[p, rows, :]
        return ref.at[rows, pl.ds(pl.multiple_of(p * self.cols, 128), self.cols)]

    def all_chips(self, ref, rows):
        return ref.at[:, rows, :] if self.stacked else ref.at[rows, :]


BIG = (
    _Layout(IN_SHARD, D_MODEL, True),
    _Layout(ATTN_W, D_MODEL // N_CHIPS, False),
    _Layout(CONV_W, D_MODEL // N_CHIPS, False),
    _Layout(D_MODEL // N_CHIPS, D_MODEL, True),
    _Layout(D_MODEL, FF2 // N_CHIPS, False),
    _Layout(D_FF // N_CHIPS, D_MODEL, True),
)
N_BIG = len(BIG)
_ANY = pl.BlockSpec(memory_space=pl.ANY)


def _position():
    x, y, c = lax.axis_index("x"), lax.axis_index("y"), lax.axis_index("c")
    return x, y, c, 2 * x + y


def _core_of_chip(p, c):
    return (p >> 1, p & 1, c)


def _place_cast(shard, lay, pos, name, after=None):
    rows, cols = shard.shape
    tr = _row_tile(rows, cols * 6)
    if lay.stacked:
        out = (lay.whole(), BF16, (None, tr, cols), lambda i, pos: (pos[0], i, 0))
    else:
        out = (lay.whole(), BF16, (tr, cols), lambda i, pos: (i, pos[0]))
    ins = [(shard, (tr, cols), lambda i, pos: (i, 0))]
    if after is not None:
        ins.append((after, (8, 128), lambda i, pos: (0, 0)))
    return _tiled(lambda a, *_: (a,), name, (rows // tr,), pos, ins, [out])[0]


def _remote(src, dst, send, recv, k, device):
    return pltpu.make_async_remote_copy(src_ref=src, dst_ref=dst, send_sem=send.at[k], recv_sem=recv.at[k],
                                        device_id=device, device_id_type=MESH)


def _arrival(dst, send, recv, k, me):
    return _remote(dst, dst, send, recv, k, me)


def _gather_ici(lay, name, q=0, nq=1):
    def plan(hbm, pos, send, recv, base):
        x, y, c, me = pos
        rows = lay.part_rows(c, q, nq)
        mine = lay.block(hbm[name], me, rows)
        starts = [_remote(mine, mine, send, recv, base + d - 1, _core_of_chip(me ^ d, c)) for d in (1, 2, 3)]
        waits = [_arrival(lay.block(hbm[name], me ^ d, rows), send, recv, base + d - 1, (x, y, c)) for d in (1, 2, 3)]
        return starts, waits
    return _Job(3, plan)


def _gather_d2d(lay, name, q=0, nq=1):
    def plan(hbm, pos, send, recv, base):
        x, y, c, me = pos
        starts, waits = [], []
        for d in (1, 2, 3):
            got = lay.block(hbm[name], me ^ d, lay.part_rows(c, q, nq))
            starts.append(_remote(got, got, send, recv, base + d - 1, (x, y, 1 - c)))
            waits.append(_arrival(lay.block(hbm[name], me ^ d, lay.part_rows(1 - c, q, nq)), send, recv, base + d - 1,
                                  (x, y, c)))
        return starts, waits
    return _Job(3, plan)


def _rs_pair(lay, grad, theirs):
    def plan(hbm, pos, send, recv, base):
        x, y, c, _ = pos
        out = _remote(lay.all_chips(hbm[grad], lay.half_rows(1 - c)), hbm[theirs], send, recv, base, (x, y, 1 - c))
        return [out], [_arrival(hbm[theirs], send, recv, base, (x, y, c))]
    return _Job(1, plan)


def _rs_chips(lay, sums, slots):
    def plan(hbm, pos, send, recv, base):
        x, y, c, me = pos
        starts = [_remote(lay.block(hbm[sums], me ^ d), hbm[slots].at[me], send, recv, base + d - 1,
                          _core_of_chip(me ^ d, c)) for d in (1, 2, 3)]
        waits = [_arrival(hbm[slots].at[me ^ d], send, recv, base + d - 1, (x, y, c)) for d in (1, 2, 3)]
        return starts, waits
    return _Job(3, plan)


def _rs_share(lay, shard):
    def plan(hbm, pos, send, recv, base):
        x, y, c, _ = pos
        mine = hbm[shard].at[lay.half_rows(c), :]
        other = hbm[shard].at[lay.half_rows(1 - c), :]
        return [_remote(mine, mine, send, recv, base, (x, y, 1 - c))], [_arrival(other, send, recv, base, (x, y, c))]
    return _Job(1, plan)


def _slots_shape(lay):
    return jax.ShapeDtypeStruct((N_CHIPS, lay.rows // 2, lay.cols), BF16)


def _theirs_shape(lay):
    return jax.ShapeDtypeStruct(lay.whole(lay.rows // 2), BF16)


def _pair_sum(grad, theirs, lay, pos, name):
    half = lay.rows // 2
    add = lambda a, b: (a.astype(F32) + b.astype(F32),)
    if lay.stacked:
        tr = _row_tile(half, lay.cols * 6)
        nt = half // tr
        flat = lambda a: a.reshape(-1, lay.cols)
        mine = lambda t, pos: ((t // nt) * (2 * nt) + pos[1] * nt + t % nt, 0)
        grid, blk = (N_CHIPS * nt,), (tr, lay.cols)
        grad, theirs = flat(grad), flat(theirs)
    else:
        tr = _row_tile(half, N_CHIPS * lay.cols * 6)
        nt = half // tr
        mine = lambda t, pos: (pos[1] * nt + t, 0)
        grid, blk = (nt,), (tr, N_CHIPS * lay.cols)
    same = lambda t, pos: (t, 0)
    out = _tiled(add, name, grid, pos, [(grad, blk, mine), (theirs, blk, same)], [(theirs.shape, BF16, blk, same)])[0]
    return out.reshape(lay.whole(half))


def _chip_sum(sums, slots, lay, pos, name, after=None):
    half = lay.rows // 2
    tr = _row_tile(half, lay.cols * 12)
    nt = half // tr
    blk3 = (None, tr, lay.cols)
    if lay.stacked:
        own = (sums, blk3, lambda i, pos: (pos[0], i, 0))
    else:
        own = (sums, (tr, lay.cols), lambda i, pos: (i, pos[0]))
    others = [(slots, blk3, functools.partial(lambda d, i, pos: (pos[0] ^ d, i, 0), d)) for d in (1, 2, 3)]

    def add(a, b1, b2, b3, *_):
        return (((a.astype(F32) + b1.astype(F32)) + b2.astype(F32)) + b3.astype(F32),)

    if after is not None:
        others.append((after, (8, 128), lambda i, pos: (0, 0)))
    return _tiled(add, name, (nt,), pos, [own] + others,
                  [((lay.rows, lay.cols), F32, (tr, lay.cols), lambda i, pos: (pos[1] * nt + i, 0))])[0]


N_DEV = 8


def _to_all(src, slots):
    def plan(hbm, pos, send, recv, base):
        x, y, c, _ = pos
        idx = 4 * x + 2 * y + c
        starts = [_remote(hbm[src], hbm[slots].at[idx], send, recv, base + k - 1,
                          (x ^ (k >> 2), y ^ ((k >> 1) & 1), c ^ (k & 1))) for k in range(1, N_DEV)]
        waits = [_arrival(hbm[slots].at[idx ^ k], send, recv, base + k - 1, (x, y, c)) for k in range(1, N_DEV)]
        return starts, waits
    return _Job(N_DEV - 1, plan)


def _sum_slots(own, slots, pos):
    def body(pos_ref, own_ref, slots_ref, o_ref):
        idx = 2 * pos_ref[0] + pos_ref[1]
        term = lambda q: jnp.where(idx == q, own_ref[...], slots_ref[q])
        acc = term(0)
        for q in range(1, N_DEV):
            acc = acc + term(q)
        o_ref[...] = acc

    return pl.pallas_call(
        body, name="sum_small", out_shape=jax.ShapeDtypeStruct(own.shape, F32),
        in_specs=[pl.BlockSpec(memory_space=pltpu.SMEM), pl.BlockSpec(memory_space=pltpu.VMEM),
                  pl.BlockSpec(memory_space=pltpu.VMEM)],
    )(pos, own, slots)


def _pack_rows(parts):
    padded = [jnp.pad(a, ((0, -a.shape[0] % 8), (0, 0))) for a in parts]
    starts = [sum(p.shape[0] for p in padded[:k]) for k in range(len(padded))]
    return jnp.concatenate(padded, axis=0), starts


def kernel(x, mix_norm, w_in, b_in, sinks, conv_w, w_attn_branch, w_conv_branch, w_out, ffn_norm, w_up, ffn_conv_w, w_down, final_norm, loss_target, m_mix_norm, m_w_in, m_b_in, m_sinks, m_conv_w, m_w_attn_branch, m_w_conv_branch, m_w_out, m_ffn_norm, m_w_up, m_ffn_conv_w, m_w_down, m_final_norm, v_mix_norm, v_w_in, v_b_in, v_sinks, v_conv_w, v_w_attn_branch, v_w_conv_branch, v_w_out, v_ffn_norm, v_w_up, v_ffn_conv_w, v_w_down, v_final_norm):
    me = 2 * lax.axis_index("x") + lax.axis_index("y")
    big_w = [w_in[0].T, w_attn_branch[0], w_conv_branch[0], w_out[0], w_up[0], w_down[0]]
    big_m = [m_w_in[0].T, m_w_attn_branch[0], m_w_conv_branch[0], m_w_out[0], m_w_up[0], m_w_down[0]]
    big_v = [v_w_in[0].T, v_w_attn_branch[0], v_w_conv_branch[0], v_w_out[0], v_w_up[0], v_w_down[0]]
    names = ("w_in", "w_ab", "w_cb", "w_out", "w_up", "w_down")

    pos = jnp.stack([me, lax.axis_index("c")]).astype(jnp.int32)

    lay = dict(zip(names, BIG))
    xs, target, sk = x[0], loss_target[0], sinks[0]
    s = xs.shape[0]
    tm, tm2, bk, bk2 = min(256, s), min(512, s), min(1024, s), min(2048, s)

    taps, (_, t0) = _pack_rows([conv_w[0], ffn_conv_w[0].reshape(3 * (FF2 // N_CHIPS // 128), 128)])
    placed = {"w_in": _place_cast(big_w[0], lay["w_in"], pos, "cast_w_in")}
    fly_in, started = _start_exchange("gather_in_start", [_gather_ici(lay["w_in"], "w_in")], {"w_in": placed["w_in"]})
    taps_flight, started = _start_exchange("taps_start", [_to_all("v", "slots")],
                                           {"v": taps + started[0:1], "slots": jnp.zeros((N_DEV, *taps.shape), F32)})
    for w, n in zip(big_w[1:], names[1:]):
        placed[n] = _place_cast(w, lay[n], pos, "cast_" + n, after=started)
    trio = ("w_ab", "w_cb", "w_out")
    (fly_trio, fly_up, fly_down), started = _start_exchanges("gather_rest_start", [
        ([_gather_ici(lay[n], n) for n in ws], {n: placed[n] for n in ws}) for ws in (trio, ("w_up",), ("w_down",))])

    got = _finish_exchange("gather_in_wait", fly_in, after=started)
    w_in_full = _exchange("gather_in_d2d", [[_gather_d2d(lay["w_in"], "w_in")]], bufs=got)["w_in"].reshape(IN_W, D_MODEL)
    xn, qkv, c3, gates = _inproj_fwd(xs, mix_norm, w_in_full, b_in, tm2)
    k2 = _Carry([_gather_d2d(lay[n], n) for n in trio], bufs=_finish_exchange("gather_trio_wait", fly_trio, after=qkv))
    attn = _attn_fwd(qkv, sk, comm=k2)
    w_ab, w_cb = k2.out["w_ab"], k2.out["w_cb"]
    w_out_full = k2.out["w_out"].reshape(D_MODEL, D_MODEL)
    k3 = _Carry([_gather_d2d(lay["w_up"], "w_up")], bufs=_finish_exchange("gather_up_wait", fly_up, after=attn))
    taps = _finish_exchange("taps_wait", taps_flight, after=attn)
    taps = lax.dynamic_update_slice(taps["slots"], taps["v"][None], (2 * me + lax.axis_index("c"), 0, 0))
    conv_full = taps[0::2, 0:3].transpose(1, 0, 2).reshape(3, CONV_W)
    ffn_cw_full = taps[0::2, t0:t0 + 33].reshape(N_CHIPS, 3, FF2 // N_CHIPS).transpose(1, 0, 2).reshape(3, FF2)
    conv, a, cv, merged, h1, hn = _mix_fwd(xs, attn, c3, gates, conv_full, w_ab, w_cb, w_out_full, ffn_norm, tm2, comm=k3)
    w_up_full = k3.out["w_up"]
    w_down_full = _exchange("gather_down_d2d", [[_gather_d2d(lay["w_down"], "w_down")]],
                            bufs=_finish_exchange("gather_down_wait", fly_down, after=hn))["w_down"].reshape(D_FF, D_MODEL)
    u, up, act, dh2, loss_part, g_fn = _ffn_fwd_loss(hn, h1, w_up_full, ffn_cw_full, w_down_full,
                                                     final_norm[None, :], target, tm)

    grads, sums, slots = {}, {}, {}

    def pair(*ws):
        return _Carry([_rs_pair(lay[n], "g_" + n, "t_" + n) for n in ws], reads={"g_" + n: grads[n] for n in ws},
                      fresh={"t_" + n: _theirs_shape(lay[n]) for n in ws})

    def chips(*ws, also=None):
        k = _Carry([_rs_chips(lay[n], "s_" + n, "r_" + n) for n in ws], reads={"s_" + n: sums[n] for n in ws},
                   fresh={"r_" + n: _slots_shape(lay[n]) for n in ws})
        if also is not None:
            k = _Carry(k.jobs + also.jobs, {**k.reads, **also.reads}, None, {**k.fresh, **also.fresh})
        return k

    def pair_sums(k, *ws):
        for n in ws:
            sums[n] = _pair_sum(grads[n], k.out["t_" + n], lay[n], pos, "pair_sum_" + n)

    def take_slots(k, *ws):
        for n in ws:
            slots[n] = k.out["r_" + n]

    du, dh1, g_fcw, g_g2 = _ffn_bwd(dh2, u, up, h1, w_up_full, ffn_cw_full, w_down_full, ffn_norm, tm)
    grads["w_down"] = _wgrad(act, dh2, D_FF // 2, D_MODEL, bk2, "wgrad_down").reshape(lay["w_down"].whole())
    k4 = pair("w_down")
    grads["w_up"] = _wgrad(hn, du, D_MODEL, FF2 // 4, bk2, "wgrad_up", comm=k4)
    pair_sums(k4, "w_down")
    k5 = chips("w_down", also=pair("w_up"))
    dattn, da, dcv, dc3, dgt, g_cw = _mix_bwd(dh1, gates, a, cv, c3, conv_full, w_ab, w_cb, w_out_full, tm2, comm=k5)
    take_slots(k5, "w_down")
    pair_sums(k5, "w_up")
    grads["w_out"] = _wgrad(merged, dh1, D_MODEL, D_MODEL, bk2, "wgrad_out").reshape(lay["w_out"].whole())
    grads["w_ab"] = _wgrad(attn, da, ATTN_W, D_MODEL, bk2, "wgrad_ab")
    grads["w_cb"] = _wgrad(conv, dcv, CONV_W, D_MODEL, bk2, "wgrad_cb")
    k6 = chips("w_up", also=pair("w_out", "w_ab", "w_cb"))
    dq, dk, dv, g_sk = _attn_bwd(qkv, sk, attn, dattn, comm=k6)
    take_slots(k6, "w_up")
    pair_sums(k6, "w_out", "w_ab", "w_cb")
    grad_x, dproj, g_b, g_g1 = _inproj_bwd(dq, dk, dv, dc3, dgt, w_in_full, xs, dh1, mix_norm, tm2)

    parts = [loss_part, g_g1, g_b, jnp.pad(g_sk[:, 0], (0, 120))[None, :], g_cw, g_g2, g_fcw, g_fn]
    packed, at = _pack_rows([p.reshape(-1, 128) for p in parts])
    small_flight, started = _start_exchange("small_start", [_to_all("v", "slots")],
                                            {"v": packed, "slots": jnp.zeros((N_DEV, *packed.shape), F32)})
    k8 = chips("w_out", "w_ab", "w_cb")
    k8.reads["after"] = started
    grads["w_in"] = _wgrad(dproj, xn, IN_W // 2, D_MODEL, bk, "wgrad_in", comm=k8).reshape(lay["w_in"].whole())
    take_slots(k8, "w_out", "w_ab", "w_cb")
    others = names[1:]
    in_flight, started = _start_exchange("rs_pair_in_start", [_rs_pair(lay["w_in"], "g", "t")],
                                         {"g": grads["w_in"], "t": _theirs_shape(lay["w_in"])})
    halves = {n: _chip_sum(sums[n], slots[n], lay[n], pos, "chip_sum_" + n, after=started) for n in ("w_up", "w_down")}
    landed = _finish_exchange("rs_pair_in_wait", in_flight, after=halves["w_down"])
    sums["w_in"] = _pair_sum(landed["g"], landed["t"], lay["w_in"], pos, "pair_sum_w_in")
    in_flight, started = _start_exchange("rs_chips_in_start", [_rs_chips(lay["w_in"], "s", "r")],
                                         {"s": sums["w_in"], "r": _slots_shape(lay["w_in"])})
    for n in trio:
        halves[n] = _chip_sum(sums[n], slots[n], lay[n], pos, "chip_sum_" + n, after=started)
    shared = _exchange("share_halves", [[_rs_share(lay[n], n) for n in others]], bufs=halves)
    w_of, m_of, v_of = dict(zip(names, big_w)), dict(zip(names, big_m)), dict(zip(names, big_v))

    def adam(n, g, after=None):
        return _rowwise(lambda w, g, m, v: (g, *_adamw(w, g, m, v)), [w_of[n], g, m_of[n], v_of[n]], [F32] * 4,
                        "adamw_" + n, after=after)

    new_of, last = {}, None
    for n in ("w_up", "w_down", "w_out", "w_ab", "w_cb"):
        new_of[n] = adam(n, shared[n], last)
        last = new_of[n][1]

    arrived = _finish_exchange("small_wait", small_flight, after=last)
    total = _sum_slots(arrived["v"], arrived["slots"], pos)
    part = lambda k: total[at[k]:at[k] + parts[k].size // 128].reshape(parts[k].shape)
    loss = total[0, 0]
    g_mix, g_b, g_g2, g_fn = part(1), part(2), part(5), part(7)
    g_sk = part(3)[:, 0:N_HEADS]
    g_cw = lax.dynamic_slice(part(4), (0, me * 128), (3, 128))
    g_fcw = lax.dynamic_slice(part(6), (0, me * (FF2 // N_CHIPS)), (3, FF2 // N_CHIPS))
    small_p = [
        (mix_norm, g_mix, m_mix_norm, v_mix_norm), (b_in, g_b, m_b_in, v_b_in), (sinks, g_sk, m_sinks, v_sinks),
        (conv_w[0], g_cw, m_conv_w[0], v_conv_w[0]), (ffn_norm, g_g2, m_ffn_norm, v_ffn_norm),
        (ffn_conv_w[0], g_fcw, m_ffn_conv_w[0], v_ffn_conv_w[0]),
        (final_norm[None, :], g_fn, m_final_norm[None, :], v_final_norm[None, :])]
    small_new = _adamw_small(small_p)
    small_new = [small_new[3 * k:3 * k + 3] for k in range(len(small_p))]

    landed = _finish_exchange("rs_chips_in_wait", in_flight, after=small_new[0][0])
    half_in = _chip_sum(landed["s"], landed["r"], lay["w_in"], pos, "chip_sum_w_in")
    shared["w_in"] = _exchange("share_in", [[_rs_share(lay["w_in"], "w_in")]], bufs={"w_in": half_in})["w_in"]
    new_of["w_in"] = adam("w_in", shared["w_in"])
    big_g = [new_of[n][0] for n in names]
    big_new = [new_of[n][1:] for n in names]

    order = [("s", 0), ("b", 0), ("s", 1), ("s", 2), ("s", 3), ("b", 1), ("b", 2), ("b", 3), ("s", 4), ("b", 4),
             ("s", 5), ("b", 5), ("s", 6)]
    shapes = [mix_norm.shape, w_in.shape, b_in.shape, sinks.shape, conv_w.shape, w_attn_branch.shape,
              w_conv_branch.shape, w_out.shape, ffn_norm.shape, w_up.shape, ffn_conv_w.shape, w_down.shape,
              final_norm.shape]
    small_g = [p[1] for p in small_p]
    big_g[0] = big_g[0].T
    big_new[0] = [a.T for a in big_new[0]]
    out_g = [(small_g[k] if kind == "s" else big_g[k]).reshape(shp) for (kind, k), shp in zip(order, shapes)]
    news = [[(small_new[k][j] if kind == "s" else big_new[k][j]).reshape(shp) for (kind, k), shp in zip(order, shapes)]
            for j in range(3)]
    return (loss, grad_x[None], *out_g, *news[0], *news[1], *news[2])
```

```python
import functools

import jax
import jax.numpy as jnp
from jax import lax
from jax.experimental import pallas as pl
from jax.experimental.pallas import tpu as pltpu

F32 = jnp.float32
BF16 = jnp.bfloat16

D_MODEL = 1024
HEAD_DIM = 64
N_HEADS = 8
N_KV_HEADS = 2
GROUP = N_HEADS // N_KV_HEADS
BLOCK = 128
ATTN_SCALE = HEAD_DIM ** -0.5
ATTN_W = N_HEADS * HEAD_DIM
KV_W = N_KV_HEADS * HEAD_DIM
CONV_W = 512
QKV_W = ATTN_W + 2 * KV_W
C3_W = 3 * CONV_W
GATES_W = 2 * D_MODEL
IN_W = QKV_W + C3_W + GATES_W
D_FF = 2816
FF2 = 2 * D_FF
NORM_EPS = 1e-5
N_CHIPS = 4
IN_SHARD = IN_W // N_CHIPS
NEG = -1e30

ADAM_LR = 0.001
ADAM_B1 = 0.9
ADAM_B2 = 0.999
ADAM_EPS = 1e-08
ADAM_WD = 0.01
ADAM_STEP = 10

VMEM_LIMIT = 56 * 1024 * 1024
MESH = pl.DeviceIdType.MESH

NT = (((1,), (1,)), ((), ()))
TN = (((0,), (0,)), ((), ()))


def _params(*sem):
    return pltpu.CompilerParams(dimension_semantics=sem, vmem_limit_bytes=VMEM_LIMIT)


def _resident(shape):
    return pl.BlockSpec(shape, lambda *_: (0,) * len(shape), pipeline_mode=pl.Buffered(1))


def _sigmoid(v):
    return 0.5 * jnp.tanh(0.5 * v) + 0.5


def _rstd(v):
    return lax.rsqrt(jnp.mean(v * v, axis=-1, keepdims=True) + NORM_EPS)


def _rms_bwd(dy, v, rstd, g):
    vhat = v * rstd
    t = dy * g
    return rstd * (t - vhat * jnp.mean(t * vhat, axis=-1, keepdims=True)), dy * vhat


def _taps(z, cw):
    return cw[2:3] * z + cw[1:2] * pltpu.roll(z, 1, 0) + cw[0:1] * pltpu.roll(z, 2, 0)


def _causal_conv(z, prev, cw):
    edge = _taps(jnp.concatenate([prev, z[0:8]], axis=0), cw)
    return jnp.concatenate([edge[8:16], _taps(z, cw)[8:]], axis=0)


def _rows_after(z, nxt):
    n = z.shape[0]
    edge = jnp.concatenate([z[n - 8:n], nxt], axis=0)
    return tuple(jnp.concatenate([pltpu.roll(z, n - k, 0)[:n - 8], pltpu.roll(edge, 16 - k, 0)[0:8]], axis=0)
                 for k in (1, 2))


def _inproj_fwd(x, g1, w_in, b_in, tm, comm=None):
    s = x.shape[0]

    def body(x_ref, g_ref, w_ref, b_ref, xn_ref, qkv_ref, c3_ref, gt_ref):
        xf = x_ref[...]
        xn = (xf * _rstd(xf) * g_ref[...]).astype(BF16)
        xn_ref[...] = xn

        def seg(a, b):
            return lax.dot_general(xn, w_ref[a:b, :], NT, preferred_element_type=F32) + b_ref[:, a:b]

        qkv_ref[...] = seg(0, QKV_W).astype(BF16)
        c3_ref[...] = seg(QKV_W, QKV_W + C3_W).astype(BF16)
        gt_ref[...] = seg(QKV_W + C3_W, IN_W).astype(BF16)

    row = lambda w: pl.BlockSpec((tm, w), lambda i: (i, 0))
    return _call(
        comm, body, name="inproj_fwd", grid=(s // tm,),
        in_specs=[row(D_MODEL), _resident((1, D_MODEL)), _resident((IN_W, D_MODEL)), _resident((1, IN_W))],
        out_specs=[row(D_MODEL), row(QKV_W), row(C3_W), row(GATES_W)],
        out_shape=[jax.ShapeDtypeStruct((s, D_MODEL), BF16), jax.ShapeDtypeStruct((s, QKV_W), BF16),
                   jax.ShapeDtypeStruct((s, C3_W), BF16), jax.ShapeDtypeStruct((s, GATES_W), BF16)],
        compiler_params=_params("parallel"),
    )(x, g1, w_in, b_in)


def _attn_bias():
    qi = (jnp.arange(GROUP * BLOCK) % BLOCK)[:, None]
    kj = jnp.arange(2 * BLOCK)[None, :]
    band = (kj > qi) & (kj <= qi + BLOCK)
    return jnp.stack([jnp.where(band & (kj >= BLOCK), 0.0, NEG), jnp.where(band, 0.0, NEG)]).astype(F32)


def _attn_bias_spec():
    return pl.BlockSpec((None, GROUP * BLOCK, 2 * BLOCK), lambda i: (jnp.minimum(i, 1), 0, 0))


def _sink_column(sk_ref, h):
    rows = lax.broadcasted_iota(jnp.int32, (GROUP * BLOCK, 1), 0)
    col = jnp.full((GROUP * BLOCK, 1), sk_ref[h * GROUP], F32)
    for g in range(1, GROUP):
        col = jnp.where(rows >= g * BLOCK, sk_ref[h * GROUP + g], col)
    return col


def _stack_heads(t, h):
    return jnp.concatenate(
        [t[:, (h * GROUP + g) * HEAD_DIM:(h * GROUP + g + 1) * HEAD_DIM] for g in range(GROUP)], axis=0)


def _unstack_heads(per_kv):
    return jnp.concatenate(
        [t[g * BLOCK:(g + 1) * BLOCK] for t in per_kv for g in range(GROUP)], axis=1)


def _attn_specs(nb):
    cur = lambda i: jnp.minimum(i, nb - 1)
    prev = lambda i: jnp.maximum(jnp.minimum(i, nb - 1) - 1, 0)
    q = pl.BlockSpec((BLOCK, ATTN_W), lambda i: (cur(i), 0))
    kp = pl.BlockSpec((BLOCK, KV_W), lambda i: (prev(i), ATTN_W // KV_W))
    kc = pl.BlockSpec((BLOCK, KV_W), lambda i: (cur(i), ATTN_W // KV_W))
    vp = pl.BlockSpec((BLOCK, KV_W), lambda i: (prev(i), ATTN_W // KV_W + 1))
    vc = pl.BlockSpec((BLOCK, KV_W), lambda i: (cur(i), ATTN_W // KV_W + 1))
    return q, kp, kc, vp, vc


def _attn_fwd(qkv, sinks, comm=None):
    s = qkv.shape[0]
    nb = s // BLOCK

    def body(sk_ref, bias_ref, q_ref, kp_ref, kc_ref, vp_ref, vc_ref, o_ref):
        bias = bias_ref[...]
        q, kp, kc, vp, vc = q_ref[...], kp_ref[...], kc_ref[...], vp_ref[...], vc_ref[...]
        outs = []
        for h in range(N_KV_HEADS):
            hs = slice(h * HEAD_DIM, (h + 1) * HEAD_DIM)
            k2 = jnp.concatenate([kp[:, hs], kc[:, hs]], axis=0)
            v2 = jnp.concatenate([vp[:, hs], vc[:, hs]], axis=0)
            sc = lax.dot_general(_stack_heads(q, h), k2, NT, preferred_element_type=F32) * ATTN_SCALE + bias
            sink = _sink_column(sk_ref, h)
            m = jnp.maximum(jnp.max(sc, axis=1, keepdims=True), sink)
            p = jnp.exp(sc - m)
            den = jnp.sum(p, axis=1, keepdims=True) + jnp.exp(sink - m)
            outs.append(jnp.dot(p.astype(BF16), v2, preferred_element_type=F32) / den)
        o_ref[...] = _unstack_heads(outs).astype(BF16)

    return _call(
        comm, body, name="attn_fwd", grid=(nb,),
        in_specs=[pl.BlockSpec(memory_space=pltpu.SMEM), _attn_bias_spec(), *_attn_specs(nb)],
        out_specs=pl.BlockSpec((BLOCK, ATTN_W), lambda i: (i, 0)),
        out_shape=jax.ShapeDtypeStruct((s, ATTN_W), BF16),
        compiler_params=_params("parallel"),
    )(sinks, _attn_bias(), qkv, qkv, qkv, qkv, qkv)


def _mix_fwd(x, attn, c3, gates, conv_w, w_ab, w_cb, w_out, g2, tm, comm=None):
    s = x.shape[0]

    def body(x_ref, at_ref, c3_ref, gt_ref, cw_ref, wab_ref, wcb_ref, wo_ref, g_ref,
             conv_ref, a_ref, cv_ref, mg_ref, h1_ref, hn_ref, carry_ref):
        @pl.when(pl.program_id(0) == 0)
        def _():
            carry_ref[...] = jnp.zeros_like(carry_ref)

        c3v = c3_ref[...].astype(F32)
        cb, cc, cx = c3v[:, :CONV_W], c3v[:, CONV_W:2 * CONV_W], c3v[:, 2 * CONV_W:]
        z = cc * cx
        cz = _causal_conv(z, carry_ref[...], cw_ref[...])
        carry_ref[...] = z[tm - 8:tm]
        conv = (cb * cz).astype(BF16)
        conv_ref[...] = conv
        a = jnp.dot(at_ref[...], wab_ref[...], preferred_element_type=F32)
        cv = jnp.dot(conv, wcb_ref[...], preferred_element_type=F32)
        a_ref[...] = a.astype(BF16)
        cv_ref[...] = cv.astype(BF16)
        gt = gt_ref[...].astype(F32)
        merged = (_sigmoid(gt[:, :D_MODEL]) * a + _sigmoid(gt[:, D_MODEL:]) * cv).astype(BF16)
        mg_ref[...] = merged
        h1 = x_ref[...] + jnp.dot(merged, wo_ref[...], preferred_element_type=F32)
        h1_ref[...] = h1
        hn_ref[...] = (h1 * _rstd(h1) * g_ref[...]).astype(BF16)

    row = lambda w: pl.BlockSpec((tm, w), lambda i: (i, 0))
    return _call(
        comm, body, name="mix_fwd", grid=(s // tm,),
        in_specs=[row(D_MODEL), row(ATTN_W), row(C3_W), row(GATES_W), _resident((3, CONV_W)),
                  _resident((ATTN_W, D_MODEL)), _resident((CONV_W, D_MODEL)), _resident((D_MODEL, D_MODEL)),
                  _resident((1, D_MODEL))],
        out_specs=[row(CONV_W), row(D_MODEL), row(D_MODEL), row(D_MODEL), row(D_MODEL), row(D_MODEL)],
        out_shape=[jax.ShapeDtypeStruct((s, CONV_W), BF16), jax.ShapeDtypeStruct((s, D_MODEL), BF16),
                   jax.ShapeDtypeStruct((s, D_MODEL), BF16), jax.ShapeDtypeStruct((s, D_MODEL), BF16),
                   jax.ShapeDtypeStruct((s, D_MODEL), F32), jax.ShapeDtypeStruct((s, D_MODEL), BF16)],
        scratch_shapes=[pltpu.VMEM((8, CONV_W), F32)],
        compiler_params=_params("arbitrary"),
    )(x, attn, c3, gates, conv_w, w_ab, w_cb, w_out, g2)


def _ffn_fwd_loss(hn, h1, w_up, ffn_cw, w_down, g3, target, tm):
    s = hn.shape[0]

    def body(hn_ref, h1_ref, wu_ref, cw_ref, wd_ref, g_ref, t_ref,
             u_ref, up_ref, act_ref, dh2_ref, loss_ref, gfn_ref, carry_ref):
        @pl.when(pl.program_id(0) == 0)
        def _():
            carry_ref[...] = jnp.zeros_like(carry_ref)
            loss_ref[...] = jnp.zeros_like(loss_ref)
            gfn_ref[...] = jnp.zeros_like(gfn_ref)

        u = jnp.dot(hn_ref[...], wu_ref[...], preferred_element_type=F32)
        u_ref[...] = u.astype(BF16)
        up = _causal_conv(u, carry_ref[...], cw_ref[...])
        up_ref[...] = up
        carry_ref[...] = u[tm - 8:tm]
        gate, val = up[:, :D_FF], up[:, D_FF:]
        act = (gate * _sigmoid(gate) * val).astype(BF16)
        act_ref[...] = act
        h2 = h1_ref[...] + jnp.dot(act, wd_ref[...], preferred_element_type=F32)
        rstd = _rstd(h2)
        g = g_ref[...]
        err = h2 * rstd * g - t_ref[...]
        loss_ref[...] += jnp.sum(err * err) * (0.5 / D_MODEL)
        dh2, dg = _rms_bwd(err * (1.0 / D_MODEL), h2, rstd, g)
        dh2_ref[...] = dh2
        gfn_ref[...] += jnp.sum(dg, axis=0, keepdims=True)

    row = lambda w: pl.BlockSpec((tm, w), lambda i: (i, 0))
    acc = lambda w: pl.BlockSpec((1, w), lambda i: (0, 0))
    return pl.pallas_call(
        body, name="ffn_fwd_loss", grid=(s // tm,),
        in_specs=[row(D_MODEL), row(D_MODEL), _resident((D_MODEL, FF2)), _resident((3, FF2)),
                  _resident((D_FF, D_MODEL)), _resident((1, D_MODEL)), row(D_MODEL)],
        out_specs=[row(FF2), row(FF2), row(D_FF), row(D_MODEL), acc(128), acc(D_MODEL)],
        out_shape=[jax.ShapeDtypeStruct((s, FF2), BF16), jax.ShapeDtypeStruct((s, FF2), F32),
                   jax.ShapeDtypeStruct((s, D_FF), BF16),
                   jax.ShapeDtypeStruct((s, D_MODEL), F32), jax.ShapeDtypeStruct((1, 128), F32),
                   jax.ShapeDtypeStruct((1, D_MODEL), F32)],
        scratch_shapes=[pltpu.VMEM((8, FF2), F32)],
        compiler_params=_params("arbitrary"),
    )(hn, h1, w_up, ffn_cw, w_down, g3, target)


def _ffn_bwd(dh2, u, up, h1, w_up, ffn_cw, w_down, g2, tm):
    s = dh2.shape[0]
    nt = s // tm

    def body(dh2_ref, u_ref, up_ref, h1_ref, wu_ref, cw_ref, wd_ref, g_ref,
             du_ref, dh1_ref, gcw_ref, gg_ref, carry_ref):
        @pl.when(pl.program_id(0) == 0)
        def _():
            carry_ref[...] = jnp.zeros_like(carry_ref)
            gcw_ref[...] = jnp.zeros_like(gcw_ref)
            gg_ref[...] = jnp.zeros_like(gg_ref)

        dh2v = dh2_ref[...]
        dact = lax.dot_general(dh2v.astype(BF16), wd_ref[...], NT, preferred_element_type=F32)
        upv = up_ref[...]
        gate, val = upv[:, :D_FF], upv[:, D_FF:]
        sg = _sigmoid(gate)
        dval = dact * (gate * sg)
        dgate = dact * val * (sg * (1.0 + gate * (1.0 - sg)))
        dup = jnp.concatenate([dgate, dval], axis=1)
        dup1, dup2 = _rows_after(dup, carry_ref[...])
        carry_ref[...] = dup[0:8]
        u = u_ref[...].astype(F32)
        gcw_ref[2:3, :] += jnp.sum(dup * u, axis=0, keepdims=True)
        gcw_ref[1:2, :] += jnp.sum(dup1 * u, axis=0, keepdims=True)
        gcw_ref[0:1, :] += jnp.sum(dup2 * u, axis=0, keepdims=True)
        cw = cw_ref[...]
        du = (cw[2:3] * dup + cw[1:2] * dup1 + cw[0:1] * dup2).astype(BF16)
        du_ref[...] = du
        dhn = lax.dot_general(du, wu_ref[...], NT, preferred_element_type=F32)
        h1v = h1_ref[...]
        dh1, dg = _rms_bwd(dhn, h1v, _rstd(h1v), g_ref[...])
        dh1_ref[...] = dh2v + dh1
        gg_ref[...] += jnp.sum(dg, axis=0, keepdims=True)

    row = lambda w: pl.BlockSpec((tm, w), lambda i: (nt - 1 - i, 0))
    return pl.pallas_call(
        body, name="ffn_bwd", grid=(nt,),
        in_specs=[row(D_MODEL), row(FF2), row(FF2),
                  row(D_MODEL), _resident((D_MODEL, FF2)), _resident((3, FF2)), _resident((D_FF, D_MODEL)),
                  _resident((1, D_MODEL))],
        out_specs=[row(FF2), row(D_MODEL), pl.BlockSpec((3, FF2), lambda i: (0, 0)),
                   pl.BlockSpec((1, D_MODEL), lambda i: (0, 0))],
        out_shape=[jax.ShapeDtypeStruct((s, FF2), BF16), jax.ShapeDtypeStruct((s, D_MODEL), F32),
                   jax.ShapeDtypeStruct((3, FF2), F32), jax.ShapeDtypeStruct((1, D_MODEL), F32)],
        scratch_shapes=[pltpu.VMEM((8, FF2), F32)],
        compiler_params=_params("arbitrary"),
    )(dh2, u, up, h1, w_up, ffn_cw, w_down, g2)


def _mix_bwd(dh1, gates, a, cv, c3, conv_w, w_ab, w_cb, w_out, tm, comm=None):
    s = dh1.shape[0]
    nt = s // tm
    halo = 16

    def body(dh1_ref, gt_ref, a_ref, cv_ref, c3_ref, ch_ref, cw_ref, wab_ref, wcb_ref, wo_ref,
             dat_ref, da_ref, dcv_ref, dc3_ref, dgt_ref, gcw_ref, carry_ref):
        i = pl.program_id(0)

        @pl.when(i == 0)
        def _():
            carry_ref[...] = jnp.zeros_like(carry_ref)
            gcw_ref[...] = jnp.zeros_like(gcw_ref)

        dm = lax.dot_general(dh1_ref[...].astype(BF16), wo_ref[...], NT, preferred_element_type=F32)
        gt = gt_ref[...].astype(F32)
        sa, sc = _sigmoid(gt[:, :D_MODEL]), _sigmoid(gt[:, D_MODEL:])
        da = (dm * sa).astype(BF16)
        dcv = (dm * sc).astype(BF16)
        da_ref[...] = da
        dcv_ref[...] = dcv
        dgt_ref[...] = jnp.concatenate(
            [dm * a_ref[...].astype(F32) * (sa * (1.0 - sa)), dm * cv_ref[...].astype(F32) * (sc * (1.0 - sc))],
            axis=1).astype(BF16)
        dat_ref[...] = lax.dot_general(da, wab_ref[...], NT, preferred_element_type=F32).astype(BF16)
        dconv = lax.dot_general(dcv, wcb_ref[...], NT, preferred_element_type=F32)
        c3v = c3_ref[...].astype(F32)
        cb, cc, cx = c3v[:, :CONV_W], c3v[:, CONV_W:2 * CONV_W], c3v[:, 2 * CONV_W:]
        z = cc * cx
        chv = ch_ref[...].astype(F32)[halo - 8:halo] * (i < nt - 1).astype(F32)
        zh = chv[:, CONV_W:2 * CONV_W] * chv[:, 2 * CONV_W:]
        cw = cw_ref[...]
        cz = _causal_conv(z, zh, cw)
        dcz = dconv * cb
        dcz1, dcz2 = _rows_after(dcz, carry_ref[...])
        carry_ref[...] = dcz[0:8]
        gcw_ref[2:3, :] += jnp.sum(dcz * z, axis=0, keepdims=True)
        gcw_ref[1:2, :] += jnp.sum(dcz1 * z, axis=0, keepdims=True)
        gcw_ref[0:1, :] += jnp.sum(dcz2 * z, axis=0, keepdims=True)
        dz = cw[2:3] * dcz + cw[1:2] * dcz1 + cw[0:1] * dcz2
        dc3_ref[...] = jnp.concatenate([dconv * cz, dz * cx, dz * cc], axis=1).astype(BF16)

    row = lambda w: pl.BlockSpec((tm, w), lambda i: (nt - 1 - i, 0))
    return _call(
        comm, body, name="mix_bwd", grid=(nt,),
        in_specs=[row(D_MODEL), row(GATES_W), row(D_MODEL), row(D_MODEL), row(C3_W),
                  pl.BlockSpec((halo, C3_W), lambda i: (jnp.maximum((nt - 1 - i) * (tm // halo) - 1, 0), 0)),
                  _resident((3, CONV_W)), _resident((ATTN_W, D_MODEL)), _resident((CONV_W, D_MODEL)),
                  _resident((D_MODEL, D_MODEL))],
        out_specs=[row(ATTN_W), row(D_MODEL), row(D_MODEL), row(C3_W), row(GATES_W),
                   pl.BlockSpec((3, CONV_W), lambda i: (0, 0))],
        out_shape=[jax.ShapeDtypeStruct((s, ATTN_W), BF16), jax.ShapeDtypeStruct((s, D_MODEL), BF16),
                   jax.ShapeDtypeStruct((s, D_MODEL), BF16), jax.ShapeDtypeStruct((s, C3_W), BF16),
                   jax.ShapeDtypeStruct((s, GATES_W), BF16), jax.ShapeDtypeStruct((3, CONV_W), F32)],
        scratch_shapes=[pltpu.VMEM((8, CONV_W), F32)],
        compiler_params=_params("arbitrary"),
    )(dh1, gates, a, cv, c3, c3, conv_w, w_ab, w_cb, w_out)


def _attn_bwd(qkv, sinks, o, do, comm=None):
    s = qkv.shape[0]
    nb = s // BLOCK

    def body(sk_ref, bias_ref, q_ref, kp_ref, kc_ref, vp_ref, vc_ref, o_ref, do_ref,
             dq_ref, dk_ref, dv_ref, dsk_ref, ck_ref, cvv_ref):
        i = pl.program_id(0)

        @pl.when(i == 0)
        def _():
            ck_ref[...] = jnp.zeros_like(ck_ref)
            cvv_ref[...] = jnp.zeros_like(cvv_ref)
            dsk_ref[...] = jnp.zeros_like(dsk_ref)

        @pl.when(i < nb)
        def _():
            bias = bias_ref[...]
            q, kp, kc, vp, vc = q_ref[...], kp_ref[...], kc_ref[...], vp_ref[...], vc_ref[...]
            ov, dov = o_ref[...], do_ref[...]
            dqs, dks, dvs = [], [], []
            for h in range(N_KV_HEADS):
                hs = slice(h * HEAD_DIM, (h + 1) * HEAD_DIM)
                k2 = jnp.concatenate([kp[:, hs], kc[:, hs]], axis=0)
                v2 = jnp.concatenate([vp[:, hs], vc[:, hs]], axis=0)
                qg, og, dog = _stack_heads(q, h), _stack_heads(ov, h), _stack_heads(dov, h)
                sc = lax.dot_general(qg, k2, NT, preferred_element_type=F32) * ATTN_SCALE + bias
                sink = _sink_column(sk_ref, h)
                m = jnp.maximum(jnp.max(sc, axis=1, keepdims=True), sink)
                p = jnp.exp(sc - m)
                psink = jnp.exp(sink - m)
                inv = 1.0 / (jnp.sum(p, axis=1, keepdims=True) + psink)
                p = p * inv
                delta = jnp.sum(dog.astype(F32) * og.astype(F32), axis=1, keepdims=True)
                dp = lax.dot_general(dog, v2, NT, preferred_element_type=F32)
                ds = (p * (dp - delta)).astype(BF16)
                dqs.append(jnp.dot(ds, k2, preferred_element_type=F32) * ATTN_SCALE)
                dks.append(lax.dot_general(ds, qg, TN, preferred_element_type=F32) * ATTN_SCALE)
                dvs.append(lax.dot_general(p.astype(BF16), dog, TN, preferred_element_type=F32))
                dsink = -(psink * inv * delta)
                for g in range(GROUP):
                    r = h * GROUP + g
                    dsk_ref[r:r + 1, :] += jnp.sum(dsink[g * BLOCK:(g + 1) * BLOCK])
            dq_ref[...] = _unstack_heads(dqs).astype(BF16)
            dk2 = jnp.concatenate(dks, axis=1)
            dv2 = jnp.concatenate(dvs, axis=1)
            dk_ref[...] = (ck_ref[...] + dk2[:BLOCK]).astype(BF16)
            dv_ref[...] = (cvv_ref[...] + dv2[:BLOCK]).astype(BF16)
            ck_ref[...] = dk2[BLOCK:]
            cvv_ref[...] = dv2[BLOCK:]

        @pl.when(i == nb)
        def _():
            dk_ref[...] = ck_ref[...].astype(BF16)
            dv_ref[...] = cvv_ref[...].astype(BF16)

    cur = lambda i: jnp.minimum(i, nb - 1)
    done = lambda i: jnp.maximum(i - 1, 0)
    return _call(
        comm, body, name="attn_bwd", grid=(nb + 1,),
        in_specs=[pl.BlockSpec(memory_space=pltpu.SMEM), _attn_bias_spec(), *_attn_specs(nb),
                  pl.BlockSpec((BLOCK, ATTN_W), lambda i: (cur(i), 0)),
                  pl.BlockSpec((BLOCK, ATTN_W), lambda i: (cur(i), 0))],
        out_specs=[pl.BlockSpec((BLOCK, ATTN_W), lambda i: (cur(i), 0)),
                   pl.BlockSpec((BLOCK, KV_W), lambda i: (done(i), 0)),
                   pl.BlockSpec((BLOCK, KV_W), lambda i: (done(i), 0)),
                   pl.BlockSpec((N_HEADS, 128), lambda i: (0, 0))],
        out_shape=[jax.ShapeDtypeStruct((s, ATTN_W), BF16), jax.ShapeDtypeStruct((s, KV_W), BF16),
                   jax.ShapeDtypeStruct((s, KV_W), BF16), jax.ShapeDtypeStruct((N_HEADS, 128), F32)],
        scratch_shapes=[pltpu.VMEM((BLOCK, KV_W), F32), pltpu.VMEM((BLOCK, KV_W), F32)],
        compiler_params=_params("arbitrary"),
    )(sinks, _attn_bias(), qkv, qkv, qkv, qkv, qkv, o, do)


def _inproj_bwd(dq, dk, dv, dc3, dgt, w_in, x, dh1, g1, tm, comm=None):
    s = x.shape[0]

    def body(dq_ref, dk_ref, dv_ref, dc3_ref, dgt_ref, w_ref, x_ref, dh1_ref, g_ref,
             dx_ref, dp_ref, gb_ref, gg_ref):
        @pl.when(pl.program_id(0) == 0)
        def _():
            gb_ref[...] = jnp.zeros_like(gb_ref)
            gg_ref[...] = jnp.zeros_like(gg_ref)

        dp = jnp.concatenate([dq_ref[...], dk_ref[...], dv_ref[...], dc3_ref[...], dgt_ref[...]], axis=1)
        dp_ref[...] = dp
        gb_ref[...] += jnp.sum(dp.astype(F32), axis=0, keepdims=True)
        dxn = jnp.dot(dp, w_ref[...], preferred_element_type=F32)
        xf = x_ref[...]
        dx, dg = _rms_bwd(dxn, xf, _rstd(xf), g_ref[...])
        dx_ref[...] = dh1_ref[...] + dx
        gg_ref[...] += jnp.sum(dg, axis=0, keepdims=True)

    row = lambda w: pl.BlockSpec((tm, w), lambda i: (i, 0))
    acc = lambda w: pl.BlockSpec((1, w), lambda i: (0, 0))
    return _call(
        comm, body, name="inproj_bwd", grid=(s // tm,),
        in_specs=[row(ATTN_W), row(KV_W), row(KV_W), row(C3_W), row(GATES_W), _resident((IN_W, D_MODEL)),
                  row(D_MODEL), row(D_MODEL), _resident((1, D_MODEL))],
        out_specs=[row(D_MODEL), row(IN_W), acc(IN_W), acc(D_MODEL)],
        out_shape=[jax.ShapeDtypeStruct((s, D_MODEL), F32), jax.ShapeDtypeStruct((s, IN_W), BF16),
                   jax.ShapeDtypeStruct((1, IN_W), F32), jax.ShapeDtypeStruct((1, D_MODEL), F32)],
        compiler_params=_params("arbitrary"),
    )(dq, dk, dv, dc3, dgt, w_in, x, dh1, g1)


def _wgrad(a, b, bm, bn, bk, name, comm=None):
    s, m = a.shape
    n = b.shape[1]
    nk = s // bk

    def body(a_ref, b_ref, o_ref, acc_ref):
        k = pl.program_id(2)

        @pl.when(k == 0)
        def _():
            acc_ref[...] = jnp.zeros_like(acc_ref)

        acc_ref[...] += lax.dot_general(a_ref[...].astype(BF16), b_ref[...].astype(BF16), TN,
                                        preferred_element_type=F32)

        @pl.when(k == nk - 1)
        def _():
            o_ref[...] = acc_ref[...].astype(BF16)

    return _call(
        comm, body, name=name, grid=(m // bm, n // bn, nk),
        in_specs=[pl.BlockSpec((bk, bm), lambda i, j, k: (k, i)), pl.BlockSpec((bk, bn), lambda i, j, k: (k, j))],
        out_specs=pl.BlockSpec((bm, bn), lambda i, j, k: (i, j)),
        out_shape=jax.ShapeDtypeStruct((m, n), BF16),
        scratch_shapes=[pltpu.VMEM((bm, bn), F32)],
        compiler_params=_params("parallel", "parallel", "arbitrary"),
    )(a, b)


class _Carry:
    def __init__(self, jobs, reads=None, bufs=None, fresh=None):
        self.jobs, self.reads, self.bufs, self.fresh = jobs, reads or {}, bufs or {}, fresh or {}
        self.out = {}


class _Job:
    def __init__(self, n_sems, plan):
        self.n_sems, self.plan = n_sems, plan


def _plan_all(jobs, hbm, send, recv):
    pos = _position()
    starts, waits, base = [], [], 0
    for job in jobs:
        s, w = job.plan(hbm, pos, send, recv, base)
        starts, waits, base = starts + s, waits + w, base + job.n_sems
    return starts, waits


def _call(comm, body, **kw):
    if comm is None:
        return pl.pallas_call(body, **kw)
    grid = kw["grid"]
    single = not isinstance(kw["out_shape"], (list, tuple))
    out_shape = [kw["out_shape"]] if single else list(kw["out_shape"])
    out_specs = [kw["out_specs"]] if single else list(kw["out_specs"])
    in_specs = list(kw["in_specs"])
    scratch = list(kw.get("scratch_shapes", ()))
    r_names, b_names, f_names = list(comm.reads), list(comm.bufs), list(comm.fresh)
    n_args, n_out, n_scr = len(in_specs), len(out_shape), len(scratch)
    n_sems = sum(j.n_sems for j in comm.jobs)

    def wrapped(*refs):
        k = n_args
        hbm = dict(zip(r_names, refs[k:k + len(r_names)]))
        k += len(r_names) + len(b_names)
        outs = refs[k:k + n_out]
        k += n_out
        hbm.update(zip(b_names + f_names, refs[k:k + len(b_names) + len(f_names)]))
        k += len(b_names) + len(f_names)
        send, recv = refs[k + n_scr:]
        starts, waits = _plan_all(comm.jobs, hbm, send, recv)
        ids = [pl.program_id(a) for a in range(len(grid))]
        first = functools.reduce(jnp.logical_and, [i == 0 for i in ids])
        last = functools.reduce(jnp.logical_and, [i == g - 1 for i, g in zip(ids, grid)])

        @pl.when(first)
        def _():
            for cp in starts:
                cp.start()

        body(*refs[:n_args], *outs, *refs[k:k + n_scr])

        @pl.when(last)
        def _():
            for cp in waits:
                cp.wait_recv()
            for cp in starts:
                cp.wait_send()

    sems = pltpu.SemaphoreType.DMA((n_sems,))
    held = [jax.ShapeDtypeStruct(a.shape, a.dtype) for a in comm.bufs.values()] + list(comm.fresh.values())
    call = pl.pallas_call(
        wrapped, name=kw["name"], grid=grid,
        in_specs=in_specs + [_ANY] * (len(r_names) + len(b_names)),
        out_specs=out_specs + [_ANY] * len(held),
        out_shape=out_shape + held,
        input_output_aliases={n_args + len(r_names) + i: n_out + i for i in range(len(b_names))},
        scratch_shapes=scratch + [sems, sems],
        compiler_params=_params(*["arbitrary"] * len(grid)),
    )

    def run(*args):
        res = call(*args, *comm.reads.values(), *comm.bufs.values())
        comm.out = dict(zip(b_names + f_names, res[n_out:]))
        return res[0] if single else res[:n_out]

    return run


def _exchange(name, phases, reads=None, bufs=None, fresh=None):
    comm = _Carry([j for ph in phases for j in ph], reads, bufs, fresh)
    r_names, b_names, f_names = list(comm.reads), list(comm.bufs), list(comm.fresh)
    n_sems = sum(j.n_sems for j in comm.jobs)

    def body(*refs):
        hbm = dict(zip(r_names, refs[:len(r_names)]))
        k = len(r_names) + len(b_names)
        hbm.update(zip(b_names + f_names, refs[k:k + len(b_names) + len(f_names)]))
        send, recv = refs[-2:]
        pos = _position()
        started, base = [], 0
        for ph in phases:
            waits = []
            for job in ph:
                s, w = job.plan(hbm, pos, send, recv, base)
                base += job.n_sems
                for cp in s:
                    cp.start()
                started, waits = started + s, waits + w
            for cp in waits:
                cp.wait_recv()
        for cp in started:
            cp.wait_send()

    sems = pltpu.SemaphoreType.DMA((n_sems,))
    held = [jax.ShapeDtypeStruct(a.shape, a.dtype) for a in comm.bufs.values()] + list(comm.fresh.values())
    res = pl.pallas_call(
        body, name=name, in_specs=[_ANY] * (len(r_names) + len(b_names)), out_specs=[_ANY] * len(held),
        out_shape=held, input_output_aliases={len(r_names) + i: i for i in range(len(b_names))},
        scratch_shapes=[sems, sems],
    )(*comm.reads.values(), *comm.bufs.values())
    return dict(zip(b_names + f_names, res))


_HBM = pl.BlockSpec(memory_space=pltpu.HBM)
_SEM = pl.BlockSpec(memory_space=pltpu.SEMAPHORE)
_EFFECT = pltpu.SideEffectType.DATAFLOW_SIDE_EFFECTING


def _start_exchanges(name, groups):
    names = [list(arrays) for _, arrays in groups]
    first = [sum(len(ns) for ns in names[:g]) for g in range(len(groups))]
    n, ng = sum(len(ns) for ns in names), len(groups)

    def body(*refs):
        for g, (jobs, _) in enumerate(groups):
            hbm = dict(zip(names[g], refs[first[g]:first[g] + len(names[g])]))
            for cp in _plan_all(jobs, hbm, refs[n + 2 * g], refs[n + 2 * g + 1])[0]:
                cp.start()
        refs[-1][...] = jnp.zeros_like(refs[-1])

    given = [pltpu.with_memory_space_constraint(
        a if isinstance(a, jax.Array) else lax.empty(a.shape, a.dtype), pltpu.HBM)
        for _, arrays in groups for a in arrays.values()]
    sems = [pltpu.SemaphoreType.DMA((sum(j.n_sems for j in jobs),)) for jobs, _ in groups for _ in range(2)]
    res = pl.pallas_call(
        body, name=name,
        out_shape=(*sems, *[pltpu.HBM(a.shape, a.dtype) for a in given], jax.ShapeDtypeStruct((8, 128), F32)),
        in_specs=[_HBM] * n, out_specs=(*[_SEM] * (2 * ng), *[_HBM] * n, pl.BlockSpec(memory_space=pltpu.VMEM)),
        input_output_aliases={i: 2 * ng + i for i in range(n)},
        compiler_params=pltpu.CompilerParams(has_side_effects=_EFFECT),
    )(*given)
    held = res[2 * ng:2 * ng + n]
    states = [(names[g], groups[g][0], res[2 * g], res[2 * g + 1], held[first[g]:first[g] + len(names[g])])
              for g in range(ng)]
    return states, res[-1]


def _start_exchange(name, jobs, arrays):
    states, token = _start_exchanges(name, [(jobs, arrays)])
    return states[0], token


def _finish_exchange(name, state, after):
    names, jobs, send_sem, recv_sem, held = state
    n = len(names)

    def body(*refs):
        hbm = dict(zip(names, refs[:n]))
        send, recv = refs[n:n + 2]
        starts, waits = _plan_all(jobs, hbm, send, recv)
        for cp in waits:
            cp.wait_recv()
        for cp in starts:
            cp.wait_send()

    res = pl.pallas_call(
        body, name=name, out_shape=tuple(pltpu.HBM(a.shape, a.dtype) for a in held),
        in_specs=[_HBM] * n + [_SEM, _SEM, _ANY], out_specs=tuple([_HBM] * n),
        input_output_aliases={i: i for i in range(n)},
        compiler_params=pltpu.CompilerParams(has_side_effects=_EFFECT),
    )(*held, send_sem, recv_sem, after)
    return dict(zip(names, res))


def _row_tile(rows, bytes_per_row):
    best = 16
    for t in range(16, rows + 1, 16):
        if rows % t == 0 and t * bytes_per_row <= 9 * 1024 * 1024:
            best = t
    return best


def _rowwise(fn, ins, out_dtypes, name, after=None):
    rows, cols = ins[0].shape
    per_row = sum(cols * a.dtype.itemsize for a in ins) + sum(cols * jnp.dtype(d).itemsize for d in out_dtypes)
    tr = _row_tile(rows, per_row)
    n_in = len(ins)

    def body(*refs):
        outs = fn(*[r[...] for r in refs[:n_in]])
        for o_ref, o in zip(refs[-len(out_dtypes):], outs):
            o_ref[...] = o.astype(o_ref.dtype)

    tile = pl.BlockSpec((tr, cols), lambda i: (i, 0))
    behind = [] if after is None else [after]
    return pl.pallas_call(
        body, name=name, grid=(rows // tr,),
        in_specs=[tile] * n_in + [pl.BlockSpec((8, 128), lambda i: (0, 0))] * len(behind),
        out_specs=[tile] * len(out_dtypes),
        out_shape=[jax.ShapeDtypeStruct((rows, cols), d) for d in out_dtypes],
        compiler_params=_params("parallel"),
    )(*ins, *behind)


def _tiled(fn, name, grid, pos, ins, outs):
    n_in = len(ins)

    def body(pos_ref, *refs):
        res = fn(*[r[...] for r in refs[:n_in]])
        for o_ref, o in zip(refs[n_in:], res):
            o_ref[...] = o.astype(o_ref.dtype)

    return pl.pallas_call(
        body, name=name,
        grid_spec=pltpu.PrefetchScalarGridSpec(
            num_scalar_prefetch=1, grid=grid,
            in_specs=[pl.BlockSpec(bs, im) for _, bs, im in ins],
            out_specs=[pl.BlockSpec(bs, im) for _, _, bs, im in outs]),
        out_shape=[jax.ShapeDtypeStruct(s, d) for s, d, _, _ in outs],
        compiler_params=_params("parallel"),
    )(pos, *[a for a, _, _ in ins])


def _adamw(w, g, m, v):
    m = ADAM_B1 * m + (1.0 - ADAM_B1) * g
    v = ADAM_B2 * v + (1.0 - ADAM_B2) * (g * g)
    m_hat = m / (1.0 - ADAM_B1 ** ADAM_STEP)
    v_hat = v / (1.0 - ADAM_B2 ** ADAM_STEP)
    return -ADAM_LR * (m_hat / (jnp.sqrt(v_hat) + ADAM_EPS) + ADAM_WD * w), m, v


def _adamw_small(params):
    n = len(params)

    def body(*refs):
        for k in range(n):
            w, g, m, v = (r[...] for r in refs[4 * k:4 * k + 4])
            for o_ref, o in zip(refs[4 * n + 3 * k:4 * n + 3 * k + 3], _adamw(w, g, m, v)):
                o_ref[...] = o

    flat = [a for p in params for a in p]
    return pl.pallas_call(
        body, name="adamw_small",
        out_shape=[jax.ShapeDtypeStruct(p[0].shape, F32) for p in params for _ in range(3)],
    )(*flat)


class _Layout:
    def __init__(self, rows, cols, stacked):
        self.rows, self.cols, self.stacked = rows, cols, stacked

    def whole(self, rows=None):
        r = self.rows if rows is None else rows
        return (N_CHIPS, r, self.cols) if self.stacked else (r, N_CHIPS * self.cols)

    def part_rows(self, h, q=0, nq=1):
        n = self.rows // 2 // nq
        return pl.ds(pl.multiple_of(h * (self.rows // 2) + q * n, 16), n)

    def half_rows(self, h):
        return self.part_rows(h)

    def block(self, ref, p, rows=slice(None)):
        if self.stacked:
            return ref.at[p, rows, :]
        return ref.at[rows, pl.ds(pl.multiple_of(p * self.cols, 128), self.cols)]

    def all_chips(self, ref, rows):
        return ref.at[:, rows, :] if self.stacked else ref.at[rows, :]


BIG = (
    _Layout(IN_SHARD, D_MODEL, True),
    _Layout(ATTN_W, D_MODEL // N_CHIPS, False),
    _Layout(CONV_W, D_MODEL // N_CHIPS, False),
    _Layout(D_MODEL // N_CHIPS, D_MODEL, True),
    _Layout(D_MODEL, FF2 // N_CHIPS, False),
    _Layout(D_FF // N_CHIPS, D_MODEL, True),
)
N_BIG = len(BIG)
_ANY = pl.BlockSpec(memory_space=pl.ANY)


def _position():
    x, y, c = lax.axis_index("x"), lax.axis_index("y"), lax.axis_index("c")
    return x, y, c, 2 * x + y


def _core_of_chip(p, c):
    return (p >> 1, p & 1, c)


def _place_cast(shard, lay, pos, name, after=None):
    rows, cols = shard.shape
    tr = _row_tile(rows, cols * 6)
    if lay.stacked:
        out = (lay.whole(), BF16, (None, tr, cols), lambda i, pos: (pos[0], i, 0))
    else:
        out = (lay.whole(), BF16, (tr, cols), lambda i, pos: (i, pos[0]))
    ins = [(shard, (tr, cols), lambda i, pos: (i, 0))]
    if after is not None:
        ins.append((after, (8, 128), lambda i, pos: (0, 0)))
    return _tiled(lambda a, *_: (a,), name, (rows // tr,), pos, ins, [out])[0]


def _remote(src, dst, send, recv, k, device):
    return pltpu.make_async_remote_copy(src_ref=src, dst_ref=dst, send_sem=send.at[k], recv_sem=recv.at[k],
                                        device_id=device, device_id_type=MESH)


def _arrival(dst, send, recv, k, me):
    return _remote(dst, dst, send, recv, k, me)


def _gather_ici(lay, name, q=0, nq=1):
    def plan(hbm, pos, send, recv, base):
        x, y, c, me = pos
        rows = lay.part_rows(c, q, nq)
        mine = lay.block(hbm[name], me, rows)
        starts = [_remote(mine, mine, send, recv, base + d - 1, _core_of_chip(me ^ d, c)) for d in (1, 2, 3)]
        waits = [_arrival(lay.block(hbm[name], me ^ d, rows), send, recv, base + d - 1, (x, y, c)) for d in (1, 2, 3)]
        return starts, waits
    return _Job(3, plan)


def _gather_d2d(lay, name, q=0, nq=1):
    def plan(hbm, pos, send, recv, base):
        x, y, c, me = pos
        starts, waits = [], []
        for d in (1, 2, 3):
            got = lay.block(hbm[name], me ^ d, lay.part_rows(c, q, nq))
            starts.append(_remote(got, got, send, recv, base + d - 1, (x, y, 1 - c)))
            waits.append(_arrival(lay.block(hbm[name], me ^ d, lay.part_rows(1 - c, q, nq)), send, recv, base + d - 1,
                                  (x, y, c)))
        return starts, waits
    return _Job(3, plan)


def _rs_pair(lay, grad, theirs):
    def plan(hbm, pos, send, recv, base):
        x, y, c, _ = pos
        out = _remote(lay.all_chips(hbm[grad], lay.half_rows(1 - c)), hbm[theirs], send, recv, base, (x, y, 1 - c))
        return [out], [_arrival(hbm[theirs], send, recv, base, (x, y, c))]
    return _Job(1, plan)


def _rs_chips(lay, sums, slots):
    def plan(hbm, pos, send, recv, base):
        x, y, c, me = pos
        starts = [_remote(lay.block(hbm[sums], me ^ d), hbm[slots].at[me], send, recv, base + d - 1,
                          _core_of_chip(me ^ d, c)) for d in (1, 2, 3)]
        waits = [_arrival(hbm[slots].at[me ^ d], send, recv, base + d - 1, (x, y, c)) for d in (1, 2, 3)]
        return starts, waits
    return _Job(3, plan)


def _rs_share(lay, shard):
    def plan(hbm, pos, send, recv, base):
        x, y, c, _ = pos
        mine = hbm[shard].at[lay.half_rows(c), :]
        other = hbm[shard].at[lay.half_rows(1 - c), :]
        return [_remote(mine, mine, send, recv, base, (x, y, 1 - c))], [_arrival(other, send, recv, base, (x, y, c))]
    return _Job(1, plan)


def _slots_shape(lay):
    return jax.ShapeDtypeStruct((N_CHIPS, lay.rows // 2, lay.cols), BF16)


def _theirs_shape(lay):
    return jax.ShapeDtypeStruct(lay.whole(lay.rows // 2), BF16)


def _pair_sum(grad, theirs, lay, pos, name):
    half = lay.rows // 2
    add = lambda a, b: (a.astype(F32) + b.astype(F32),)
    if lay.stacked:
        tr = _row_tile(half, lay.cols * 6)
        nt = half // tr
        flat = lambda a: a.reshape(-1, lay.cols)
        mine = lambda t, pos: ((t // nt) * (2 * nt) + pos[1] * nt + t % nt, 0)
        grid, blk = (N_CHIPS * nt,), (tr, lay.cols)
        grad, theirs = flat(grad), flat(theirs)
    else:
        tr = _row_tile(half, N_CHIPS * lay.cols * 6)
        nt = half // tr
        mine = lambda t, pos: (pos[1] * nt + t, 0)
        grid, blk = (nt,), (tr, N_CHIPS * lay.cols)
    same = lambda t, pos: (t, 0)
    out = _tiled(add, name, grid, pos, [(grad, blk, mine), (theirs, blk, same)], [(theirs.shape, BF16, blk, same)])[0]
    return out.reshape(lay.whole(half))


def _chip_sum(sums, slots, lay, pos, name, after=None):
    half = lay.rows // 2
    tr = _row_tile(half, lay.cols * 12)
    nt = half // tr
    blk3 = (None, tr, lay.cols)
    if lay.stacked:
        own = (sums, blk3, lambda i, pos: (pos[0], i, 0))
    else:
        own = (sums, (tr, lay.cols), lambda i, pos: (i, pos[0]))
    others = [(slots, blk3, functools.partial(lambda d, i, pos: (pos[0] ^ d, i, 0), d)) for d in (1, 2, 3)]

    def add(a, b1, b2, b3, *_):
        return (((a.astype(F32) + b1.astype(F32)) + b2.astype(F32)) + b3.astype(F32),)

    if after is not None:
        others.append((after, (8, 128), lambda i, pos: (0, 0)))
    return _tiled(add, name, (nt,), pos, [own] + others,
                  [((lay.rows, lay.cols), F32, (tr, lay.cols), lambda i, pos: (pos[1] * nt + i, 0))])[0]


N_DEV = 8


def _to_all(src, slots):
    def plan(hbm, pos, send, recv, base):
        x, y, c, _ = pos
        idx = 4 * x + 2 * y + c
        starts = [_remote(hbm[src], hbm[slots].at[idx], send, recv, base + k - 1,
                          (x ^ (k >> 2), y ^ ((k >> 1) & 1), c ^ (k & 1))) for k in range(1, N_DEV)]
        waits = [_arrival(hbm[slots].at[idx ^ k], send, recv, base + k - 1, (x, y, c)) for k in range(1, N_DEV)]
        return starts, waits
    return _Job(N_DEV - 1, plan)


def _sum_slots(own, slots, pos):
    def body(pos_ref, own_ref, slots_ref, o_ref):
        idx = 2 * pos_ref[0] + pos_ref[1]
        term = lambda q: jnp.where(idx == q, own_ref[...], slots_ref[q])
        acc = term(0)
        for q in range(1, N_DEV):
            acc = acc + term(q)
        o_ref[...] = acc

    return pl.pallas_call(
        body, name="sum_small", out_shape=jax.ShapeDtypeStruct(own.shape, F32),
        in_specs=[pl.BlockSpec(memory_space=pltpu.SMEM), pl.BlockSpec(memory_space=pltpu.VMEM),
                  pl.BlockSpec(memory_space=pltpu.VMEM)],
    )(pos, own, slots)


def _pack_rows(parts):
    padded = [jnp.pad(a, ((0, -a.shape[0] % 8), (0, 0))) for a in parts]
    starts = [sum(p.shape[0] for p in padded[:k]) for k in range(len(padded))]
    return jnp.concatenate(padded, axis=0), starts


def kernel(x, mix_norm, w_in, b_in, sinks, conv_w, w_attn_branch, w_conv_branch, w_out, ffn_norm, w_up, ffn_conv_w, w_down, final_norm, loss_target, m_mix_norm, m_w_in, m_b_in, m_sinks, m_conv_w, m_w_attn_branch, m_w_conv_branch, m_w_out, m_ffn_norm, m_w_up, m_ffn_conv_w, m_w_down, m_final_norm, v_mix_norm, v_w_in, v_b_in, v_sinks, v_conv_w, v_w_attn_branch, v_w_conv_branch, v_w_out, v_ffn_norm, v_w_up, v_ffn_conv_w, v_w_down, v_final_norm):
    me = 2 * lax.axis_index("x") + lax.axis_index("y")
    big_w = [w_in[0].T, w_attn_branch[0], w_conv_branch[0], w_out[0], w_up[0], w_down[0]]
    big_m = [m_w_in[0].T, m_w_attn_branch[0], m_w_conv_branch[0], m_w_out[0], m_w_up[0], m_w_down[0]]
    big_v = [v_w_in[0].T, v_w_attn_branch[0], v_w_conv_branch[0], v_w_out[0], v_w_up[0], v_w_down[0]]
    names = ("w_in", "w_ab", "w_cb", "w_out", "w_up", "w_down")

    pos = jnp.stack([me, lax.axis_index("c")]).astype(jnp.int32)

    lay = dict(zip(names, BIG))
    xs, target, sk = x[0], loss_target[0], sinks[0]
    s = xs.shape[0]
    tm, tm2, bk, bk2 = min(256, s), min(512, s), min(1024, s), min(2048, s)

    taps, (_, t0) = _pack_rows([conv_w[0], ffn_conv_w[0].reshape(3 * (FF2 // N_CHIPS // 128), 128)])
    placed = {"w_in": _place_cast(big_w[0], lay["w_in"], pos, "cast_w_in")}
    fly_in, started = _start_exchange("gather_in_start", [_gather_ici(lay["w_in"], "w_in")], {"w_in": placed["w_in"]})
    taps_flight, started = _start_exchange("taps_start", [_to_all("v", "slots")],
                                           {"v": taps + started[0:1], "slots": jnp.zeros((N_DEV, *taps.shape), F32)})
    for w, n in zip(big_w[1:], names[1:]):
        placed[n] = _place_cast(w, lay[n], pos, "cast_" + n, after=started)
    trio = ("w_ab", "w_cb", "w_out")
    (fly_trio, fly_up, fly_down), started = _start_exchanges("gather_rest_start", [
        ([_gather_ici(lay[n], n) for n in ws], {n: placed[n] for n in ws}) for ws in (trio, ("w_up",), ("w_down",))])

    got = _finish_exchange("gather_in_wait", fly_in, after=started)
    w_in_full = _exchange("gather_in_d2d", [[_gather_d2d(lay["w_in"], "w_in")]], bufs=got)["w_in"].reshape(IN_W, D_MODEL)
    xn, qkv, c3, gates = _inproj_fwd(xs, mix_norm, w_in_full, b_in, tm2)
    k2 = _Carry([_gather_d2d(lay[n], n) for n in trio], bufs=_finish_exchange("gather_trio_wait", fly_trio, after=qkv))
    attn = _attn_fwd(qkv, sk, comm=k2)
    w_ab, w_cb = k2.out["w_ab"], k2.out["w_cb"]
    w_out_full = k2.out["w_out"].reshape(D_MODEL, D_MODEL)
    k3 = _Carry([_gather_d2d(lay["w_up"], "w_up")], bufs=_finish_exchange("gather_up_wait", fly_up, after=attn))
    taps = _finish_exchange("taps_wait", taps_flight, after=attn)
    taps = lax.dynamic_update_slice(taps["slots"], taps["v"][None], (2 * me + lax.axis_index("c"), 0, 0))
    conv_full = taps[0::2, 0:3].transpose(1, 0, 2).reshape(3, CONV_W)
    ffn_cw_full = taps[0::2, t0:t0 + 33].reshape(N_CHIPS, 3, FF2 // N_CHIPS).transpose(1, 0, 2).reshape(3, FF2)
    conv, a, cv, merged, h1, hn = _mix_fwd(xs, attn, c3, gates, conv_full, w_ab, w_cb, w_out_full, ffn_norm, tm2, comm=k3)
    w_up_full = k3.out["w_up"]
    w_down_full = _exchange("gather_down_d2d", [[_gather_d2d(lay["w_down"], "w_down")]],
                            bufs=_finish_exchange("gather_down_wait", fly_down, after=hn))["w_down"].reshape(D_FF, D_MODEL)
    u, up, act, dh2, loss_part, g_fn = _ffn_fwd_loss(hn, h1, w_up_full, ffn_cw_full, w_down_full,
                                                     final_norm[None, :], target, tm)

    grads, sums, slots = {}, {}, {}

    def pair(*ws):
        return _Carry([_rs_pair(lay[n], "g_" + n, "t_" + n) for n in ws], reads={"g_" + n: grads[n] for n in ws},
                      fresh={"t_" + n: _theirs_shape(lay[n]) for n in ws})

    def chips(*ws, also=None):
        k = _Carry([_rs_chips(lay[n], "s_" + n, "r_" + n) for n in ws], reads={"s_" + n: sums[n] for n in ws},
                   fresh={"r_" + n: _slots_shape(lay[n]) for n in ws})
        if also is not None:
            k = _Carry(k.jobs + also.jobs, {**k.reads, **also.reads}, None, {**k.fresh, **also.fresh})
        return k

    def pair_sums(k, *ws):
        for n in ws:
            sums[n] = _pair_sum(grads[n], k.out["t_" + n], lay[n], pos, "pair_sum_" + n)

    def take_slots(k, *ws):
        for n in ws:
            slots[n] = k.out["r_" + n]

    du, dh1, g_fcw, g_g2 = _ffn_bwd(dh2, u, up, h1, w_up_full, ffn_cw_full, w_down_full, ffn_norm, tm)
    grads["w_down"] = _wgrad(act, dh2, D_FF // 2, D_MODEL, bk2, "wgrad_down").reshape(lay["w_down"].whole())
    k4 = pair("w_down")
    grads["w_up"] = _wgrad(hn, du, D_MODEL, FF2 // 4, bk2, "wgrad_up", comm=k4)
    pair_sums(k4, "w_down")
    k5 = chips("w_down", also=pair("w_up"))
    dattn, da, dcv, dc3, dgt, g_cw = _mix_bwd(dh1, gates, a, cv, c3, conv_full, w_ab, w_cb, w_out_full, tm2, comm=k5)
    take_slots(k5, "w_down")
    pair_sums(k5, "w_up")
    grads["w_out"] = _wgrad(merged, dh1, D_MODEL, D_MODEL, bk2, "wgrad_out").reshape(lay["w_out"].whole())
    grads["w_ab"] = _wgrad(attn, da, ATTN_W, D_MODEL, bk2, "wgrad_ab")
    grads["w_cb"] = _wgrad(conv, dcv, CONV_W, D_MODEL, bk2, "wgrad_cb")
    k6 = chips("w_up", also=pair("w_out", "w_ab", "w_cb"))
    dq, dk, dv, g_sk = _attn_bwd(qkv, sk, attn, dattn, comm=k6)
    take_slots(k6, "w_up")
    pair_sums(k6, "w_out", "w_ab", "w_cb")
    grad_x, dproj, g_b, g_g1 = _inproj_bwd(dq, dk, dv, dc3, dgt, w_in_full, xs, dh1, mix_norm, tm2)

    parts = [loss_part, g_g1, g_b, jnp.pad(g_sk[:, 0], (0, 120))[None, :], g_cw, g_g2, g_fcw, g_fn]
    packed, at = _pack_rows([p.reshape(-1, 128) for p in parts])
    small_flight, started = _start_exchange("small_start", [_to_all("v", "slots")],
                                            {"v": packed, "slots": jnp.zeros((N_DEV, *packed.shape), F32)})
    k8 = chips("w_out", "w_ab", "w_cb")
    k8.reads["after"] = started
    grads["w_in"] = _wgrad(dproj, xn, IN_W // 2, D_MODEL, bk, "wgrad_in", comm=k8).reshape(lay["w_in"].whole())
    take_slots(k8, "w_out", "w_ab", "w_cb")
    others = names[1:]
    in_flight, started = _start_exchange("rs_pair_in_start", [_rs_pair(lay["w_in"], "g", "t")],
                                         {"g": grads["w_in"], "t": _theirs_shape(lay["w_in"])})
    halves = {n: _chip_sum(sums[n], slots[n], lay[n], pos, "chip_sum_" + n, after=started) for n in ("w_up", "w_down")}
    landed = _finish_exchange("rs_pair_in_wait", in_flight, after=halves["w_down"])
    sums["w_in"] = _pair_sum(landed["g"], landed["t"], lay["w_in"], pos, "pair_sum_w_in")
    in_flight, started = _start_exchange("rs_chips_in_start", [_rs_chips(lay["w_in"], "s", "r")],
                                         {"s": sums["w_in"], "r": _slots_shape(lay["w_in"])})
    for n in trio:
        halves[n] = _chip_sum(sums[n], slots[n], lay[n], pos, "chip_sum_" + n, after=started)
    shared = _exchange("share_halves", [[_rs_share(lay[n], n) for n in others]], bufs=halves)
    w_of, m_of, v_of = dict(zip(names, big_w)), dict(zip(names, big_m)), dict(zip(names, big_v))

    def adam(n, g, after=None):
        return _rowwise(lambda w, g, m, v: (g, *_adamw(w, g, m, v)), [w_of[n], g, m_of[n], v_of[n]], [F32] * 4,
                        "adamw_" + n, after=after)

    new_of, last = {}, None
    for n in ("w_up", "w_down", "w_out", "w_ab", "w_cb"):
        new_of[n] = adam(n, shared[n], last)
        last = new_of[n][1]

    arrived = _finish_exchange("small_wait", small_flight, after=last)
    total = _sum_slots(arrived["v"], arrived["slots"], pos)
    part = lambda k: total[at[k]:at[k] + parts[k].size // 128].reshape(parts[k].shape)
    loss = total[0, 0]
    g_mix, g_b, g_g2, g_fn = part(1), part(2), part(5), part(7)
    g_sk = part(3)[:, 0:N_HEADS]
    g_cw = lax.dynamic_slice(part(4), (0, me * 128), (3, 128))
    g_fcw = lax.dynamic_slice(part(6), (0, me * (FF2 // N_CHIPS)), (3, FF2 // N_CHIPS))
    small_p = [
        (mix_norm, g_mix, m_mix_norm, v_mix_norm), (b_in, g_b, m_b_in, v_b_in), (sinks, g_sk, m_sinks, v_sinks),
        (conv_w[0], g_cw, m_conv_w[0], v_conv_w[0]), (ffn_norm, g_g2, m_ffn_norm, v_ffn_norm),
        (ffn_conv_w[0], g_fcw, m_ffn_conv_w[0], v_ffn_conv_w[0]),
        (final_norm[None, :], g_fn, m_final_norm[None, :], v_final_norm[None, :])]
    small_new = _adamw_small(small_p)
    small_new = [small_new[3 * k:3 * k + 3] for k in range(len(small_p))]

    landed = _finish_exchange("rs_chips_in_wait", in_flight, after=small_new[0][0])
    half_in = _chip_sum(landed["s"], landed["r"], lay["w_in"], pos, "chip_sum_w_in")
    shared["w_in"] = _exchange("share_in", [[_rs_share(lay["w_in"], "w_in")]], bufs={"w_in": half_in})["w_in"]
    new_of["w_in"] = adam("w_in", shared["w_in"])
    big_g = [new_of[n][0] for n in names]
    big_new = [new_of[n][1:] for n in names]

    order = [("s", 0), ("b", 0), ("s", 1), ("s", 2), ("s", 3), ("b", 1), ("b", 2), ("b", 3), ("s", 4), ("b", 4),
             ("s", 5), ("b", 5), ("s", 6)]
    shapes = [mix_norm.shape, w_in.shape, b_in.shape, sinks.shape, conv_w.shape, w_attn_branch.shape,
              w_conv_branch.shape, w_out.shape, ffn_norm.shape, w_up.shape, ffn_conv_w.shape, w_down.shape,
              final_norm.shape]
    small_g = [p[1] for p in small_p]
    big_g[0] = big_g[0].T
    big_new[0] = [a.T for a in big_new[0]]
    out_g = [(small_g[k] if kind == "s" else big_g[k]).reshape(shp) for (kind, k), shp in zip(order, shapes)]
    news = [[(small_new[k][j] if kind == "s" else big_new[k][j]).reshape(shp) for (kind, k), shp in zip(order, shapes)]
            for j in range(3)]
    return (loss, grad_x[None], *out_g, *news[0], *news[1], *news[2])
```

```python
import functools

import jax
import jax.numpy as jnp
from jax import lax
from jax.experimental import pallas as pl
from jax.experimental.pallas import tpu as pltpu

F32 = jnp.float32
BF16 = jnp.bfloat16

D_MODEL = 1024
HEAD_DIM = 64
N_HEADS = 8
N_KV_HEADS = 2
GROUP = N_HEADS // N_KV_HEADS
BLOCK = 128
ATTN_SCALE = HEAD_DIM ** -0.5
ATTN_W = N_HEADS * HEAD_DIM
KV_W = N_KV_HEADS * HEAD_DIM
CONV_W = 512
QKV_W = ATTN_W + 2 * KV_W
C3_W = 3 * CONV_W
GATES_W = 2 * D_MODEL
IN_W = QKV_W + C3_W + GATES_W
D_FF = 2816
FF2 = 2 * D_FF
NORM_EPS = 1e-5
N_CHIPS = 4
IN_SHARD = IN_W // N_CHIPS
NEG = -1e30

ADAM_LR = 0.001
ADAM_B1 = 0.9
ADAM_B2 = 0.999
ADAM_EPS = 1e-08
ADAM_WD = 0.01
ADAM_STEP = 10

VMEM_LIMIT = 56 * 1024 * 1024
MESH = pl.DeviceIdType.MESH

NT = (((1,), (1,)), ((), ()))
TN = (((0,), (0,)), ((), ()))


def _params(*sem):
    return pltpu.CompilerParams(dimension_semantics=sem, vmem_limit_bytes=VMEM_LIMIT)


def _resident(shape):
    return pl.BlockSpec(shape, lambda *_: (0,) * len(shape), pipeline_mode=pl.Buffered(1))


def _sigmoid(v):
    return 0.5 * jnp.tanh(0.5 * v) + 0.5


def _rstd(v):
    return lax.rsqrt(jnp.mean(v * v, axis=-1, keepdims=True) + NORM_EPS)


def _rms_bwd(dy, v, rstd, g):
    vhat = v * rstd
    t = dy * g
    return rstd * (t - vhat * jnp.mean(t * vhat, axis=-1, keepdims=True)), dy * vhat


def _taps(z, cw):
    return cw[2:3] * z + cw[1:2] * pltpu.roll(z, 1, 0) + cw[0:1] * pltpu.roll(z, 2, 0)


def _causal_conv(z, prev, cw):
    edge = _taps(jnp.concatenate([prev, z[0:8]], axis=0), cw)
    return jnp.concatenate([edge[8:16], _taps(z, cw)[8:]], axis=0)


def _rows_after(z, nxt):
    n = z.shape[0]
    edge = jnp.concatenate([z[n - 8:n], nxt], axis=0)
    return tuple(jnp.concatenate([pltpu.roll(z, n - k, 0)[:n - 8], pltpu.roll(edge, 16 - k, 0)[0:8]], axis=0)
                 for k in (1, 2))


def _inproj_fwd(x, g1, w_in, b_in, tm, comm=None):
    s = x.shape[0]

    def body(x_ref, g_ref, w_ref, b_ref, xn_ref, qkv_ref, c3_ref, gt_ref):
        xf = x_ref[...]
        xn = (xf * _rstd(xf) * g_ref[...]).astype(BF16)
        xn_ref[...] = xn

        def seg(a, b):
            return lax.dot_general(xn, w_ref[a:b, :], NT, preferred_element_type=F32) + b_ref[:, a:b]

        qkv_ref[...] = seg(0, QKV_W).astype(BF16)
        c3_ref[...] = seg(QKV_W, QKV_W + C3_W).astype(BF16)
        gt_ref[...] = seg(QKV_W + C3_W, IN_W).astype(BF16)

    row = lambda w: pl.BlockSpec((tm, w), lambda i: (i, 0))
    return _call(
        comm, body, name="inproj_fwd", grid=(s // tm,),
        in_specs=[row(D_MODEL), _resident((1, D_MODEL)), _resident((IN_W, D_MODEL)), _resident((1, IN_W))],
        out_specs=[row(D_MODEL), row(QKV_W), row(C3_W), row(GATES_W)],
        out_shape=[jax.ShapeDtypeStruct((s, D_MODEL), BF16), jax.ShapeDtypeStruct((s, QKV_W), BF16),
                   jax.ShapeDtypeStruct((s, C3_W), BF16), jax.ShapeDtypeStruct((s, GATES_W), BF16)],
        compiler_params=_params("parallel"),
    )(x, g1, w_in, b_in)


def _attn_bias():
    qi = (jnp.arange(GROUP * BLOCK) % BLOCK)[:, None]
    kj = jnp.arange(2 * BLOCK)[None, :]
    band = (kj > qi) & (kj <= qi + BLOCK)
    return jnp.stack([jnp.where(band & (kj >= BLOCK), 0.0, NEG), jnp.where(band, 0.0, NEG)]).astype(F32)


def _attn_bias_spec():
    return pl.BlockSpec((None, GROUP * BLOCK, 2 * BLOCK), lambda i: (jnp.minimum(i, 1), 0, 0))


def _sink_column(sk_ref, h):
    rows = lax.broadcasted_iota(jnp.int32, (GROUP * BLOCK, 1), 0)
    col = jnp.full((GROUP * BLOCK, 1), sk_ref[h * GROUP], F32)
    for g in range(1, GROUP):
        col = jnp.where(rows >= g * BLOCK, sk_ref[h * GROUP + g], col)
    return col


def _stack_heads(t, h):
    return jnp.concatenate(
        [t[:, (h * GROUP + g) * HEAD_DIM:(h * GROUP + g + 1) * HEAD_DIM] for g in range(GROUP)], axis=0)


def _unstack_heads(per_kv):
    return jnp.concatenate(
        [t[g * BLOCK:(g + 1) * BLOCK] for t in per_kv for g in range(GROUP)], axis=1)


def _attn_specs(nb):
    cur = lambda i: jnp.minimum(i, nb - 1)
    prev = lambda i: jnp.maximum(jnp.minimum(i, nb - 1) - 1, 0)
    q = pl.BlockSpec((BLOCK, ATTN_W), lambda i: (cur(i), 0))
    kp = pl.BlockSpec((BLOCK, KV_W), lambda i: (prev(i), ATTN_W // KV_W))
    kc = pl.BlockSpec((BLOCK, KV_W), lambda i: (cur(i), ATTN_W // KV_W))
    vp = pl.BlockSpec((BLOCK, KV_W), lambda i: (prev(i), ATTN_W // KV_W + 1))
    vc = pl.BlockSpec((BLOCK, KV_W), lambda i: (cur(i), ATTN_W // KV_W + 1))
    return q, kp, kc, vp, vc


def _attn_fwd(qkv, sinks, comm=None):
    s = qkv.shape[0]
    nb = s // BLOCK

    def body(sk_ref, bias_ref, q_ref, kp_ref, kc_ref, vp_ref, vc_ref, o_ref):
        bias = bias_ref[...]
        q, kp, kc, vp, vc = q_ref[...], kp_ref[...], kc_ref[...], vp_ref[...], vc_ref[...]
        outs = []
        for h in range(N_KV_HEADS):
            hs = slice(h * HEAD_DIM, (h + 1) * HEAD_DIM)
            k2 = jnp.concatenate([kp[:, hs], kc[:, hs]], axis=0)
            v2 = jnp.concatenate([vp[:, hs], vc[:, hs]], axis=0)
            sc = lax.dot_general(_stack_heads(q, h), k2, NT, preferred_element_type=F32) * ATTN_SCALE + bias
            sink = _sink_column(sk_ref, h)
            m = jnp.maximum(jnp.max(sc, axis=1, keepdims=True), sink)
            p = jnp.exp(sc - m)
            den = jnp.sum(p, axis=1, keepdims=True) + jnp.exp(sink - m)
            outs.append(jnp.dot(p.astype(BF16), v2, preferred_element_type=F32) / den)
        o_ref[...] = _unstack_heads(outs).astype(BF16)

    return _call(
        comm, body, name="attn_fwd", grid=(nb,),
        in_specs=[pl.BlockSpec(memory_space=pltpu.SMEM), _attn_bias_spec(), *_attn_specs(nb)],
        out_specs=pl.BlockSpec((BLOCK, ATTN_W), lambda i: (i, 0)),
        out_shape=jax.ShapeDtypeStruct((s, ATTN_W), BF16),
        compiler_params=_params("parallel"),
    )(sinks, _attn_bias(), qkv, qkv, qkv, qkv, qkv)


def _mix_fwd(x, attn, c3, gates, conv_w, w_ab, w_cb, w_out, g2, tm, comm=None):
    s = x.shape[0]

    def body(x_ref, at_ref, c3_ref, gt_ref, cw_ref, wab_ref, wcb_ref, wo_ref, g_ref,
             conv_ref, a_ref, cv_ref, mg_ref, h1_ref, hn_ref, carry_ref):
        @pl.when(pl.program_id(0) == 0)
        def _():
            carry_ref[...] = jnp.zeros_like(carry_ref)

        c3v = c3_ref[...].astype(F32)
        cb, cc, cx = c3v[:, :CONV_W], c3v[:, CONV_W:2 * CONV_W], c3v[:, 2 * CONV_W:]
        z = cc * cx
        cz = _causal_conv(z, carry_ref[...], cw_ref[...])
        carry_ref[...] = z[tm - 8:tm]
        conv = (cb * cz).astype(BF16)
        conv_ref[...] = conv
        a = jnp.dot(at_ref[...], wab_ref[...], preferred_element_type=F32)
        cv = jnp.dot(conv, wcb_ref[...], preferred_element_type=F32)
        a_ref[...] = a.astype(BF16)
        cv_ref[...] = cv.astype(BF16)
        gt = gt_ref[...].astype(F32)
        merged = (_sigmoid(gt[:, :D_MODEL]) * a + _sigmoid(gt[:, D_MODEL:]) * cv).astype(BF16)
        mg_ref[...] = merged
        h1 = x_ref[...] + jnp.dot(merged, wo_ref[...], preferred_element_type=F32)
        h1_ref[...] = h1
        hn_ref[...] = (h1 * _rstd(h1) * g_ref[...]).astype(BF16)

    row = lambda w: pl.BlockSpec((tm, w), lambda i: (i, 0))
    return _call(
        comm, body, name="mix_fwd", grid=(s // tm,),
        in_specs=[row(D_MODEL), row(ATTN_W), row(C3_W), row(GATES_W), _resident((3, CONV_W)),
                  _resident((ATTN_W, D_MODEL)), _resident((CONV_W, D_MODEL)), _resident((D_MODEL, D_MODEL)),
                  _resident((1, D_MODEL))],
        out_specs=[row(CONV_W), row(D_MODEL), row(D_MODEL), row(D_MODEL), row(D_MODEL), row(D_MODEL)],
        out_shape=[jax.ShapeDtypeStruct((s, CONV_W), BF16), jax.ShapeDtypeStruct((s, D_MODEL), BF16),
                   jax.ShapeDtypeStruct((s, D_MODEL), BF16), jax.ShapeDtypeStruct((s, D_MODEL), BF16),
                   jax.ShapeDtypeStruct((s, D_MODEL), F32), jax.ShapeDtypeStruct((s, D_MODEL), BF16)],
        scratch_shapes=[pltpu.VMEM((8, CONV_W), F32)],
        compiler_params=_params("arbitrary"),
    )(x, attn, c3, gates, conv_w, w_ab, w_cb, w_out, g2)


def _ffn_fwd_loss(hn, h1, w_up, ffn_cw, w_down, g3, target, tm):
    s = hn.shape[0]

    def body(hn_ref, h1_ref, wu_ref, cw_ref, wd_ref, g_ref, t_ref,
             u_ref, up_ref, act_ref, dh2_ref, loss_ref, gfn_ref, carry_ref):
        @pl.when(pl.program_id(0) == 0)
        def _():
            carry_ref[...] = jnp.zeros_like(carry_ref)
            loss_ref[...] = jnp.zeros_like(loss_ref)
            gfn_ref[...] = jnp.zeros_like(gfn_ref)

        u = jnp.dot(hn_ref[...], wu_ref[...], preferred_element_type=F32)
        u_ref[...] = u.astype(BF16)
        up = _causal_conv(u, carry_ref[...], cw_ref[...])
        up_ref[...] = up
        carry_ref[...] = u[tm - 8:tm]
        gate, val = up[:, :D_FF], up[:, D_FF:]
        act = (gate * _sigmoid(gate) * val).astype(BF16)
        act_ref[...] = act
        h2 = h1_ref[...] + jnp.dot(act, wd_ref[...], preferred_element_type=F32)
        rstd = _rstd(h2)
        g = g_ref[...]
        err = h2 * rstd * g - t_ref[...]
        loss_ref[...] += jnp.sum(err * err) * (0.5 / D_MODEL)
        dh2, dg = _rms_bwd(err * (1.0 / D_MODEL), h2, rstd, g)
        dh2_ref[...] = dh2
        gfn_ref[...] += jnp.sum(dg, axis=0, keepdims=True)

    row = lambda w: pl.BlockSpec((tm, w), lambda i: (i, 0))
    acc = lambda w: pl.BlockSpec((1, w), lambda i: (0, 0))
    return pl.pallas_call(
        body, name="ffn_fwd_loss", grid=(s // tm,),
        in_specs=[row(D_MODEL), row(D_MODEL), _resident((D_MODEL, FF2)), _resident((3, FF2)),
                  _resident((D_FF, D_MODEL)), _resident((1, D_MODEL)), row(D_MODEL)],
        out_specs=[row(FF2), row(FF2), row(D_FF), row(D_MODEL), acc(128), acc(D_MODEL)],
        out_shape=[jax.ShapeDtypeStruct((s, FF2), BF16), jax.ShapeDtypeStruct((s, FF2), F32),
                   jax.ShapeDtypeStruct((s, D_FF), BF16),
                   jax.ShapeDtypeStruct((s, D_MODEL), F32), jax.ShapeDtypeStruct((1, 128), F32),
                   jax.ShapeDtypeStruct((1, D_MODEL), F32)],
        scratch_shapes=[pltpu.VMEM((8, FF2), F32)],
        compiler_params=_params("arbitrary"),
    )(hn, h1, w_up, ffn_cw, w_down, g3, target)


def _ffn_bwd(dh2, u, up, h1, w_up, ffn_cw, w_down, g2, tm):
    s = dh2.shape[0]
    nt = s // tm

    def body(dh2_ref, u_ref, up_ref, h1_ref, wu_ref, cw_ref, wd_ref, g_ref,
             du_ref, dh1_ref, gcw_ref, gg_ref, carry_ref):
        @pl.when(pl.program_id(0) == 0)
        def _():
            carry_ref[...] = jnp.zeros_like(carry_ref)
            gcw_ref[...] = jnp.zeros_like(gcw_ref)
            gg_ref[...] = jnp.zeros_like(gg_ref)

        dh2v = dh2_ref[...]
        dact = lax.dot_general(dh2v.astype(BF16), wd_ref[...], NT, preferred_element_type=F32)
        upv = up_ref[...]
        gate, val = upv[:, :D_FF], upv[:, D_FF:]
        sg = _sigmoid(gate)
        dval = dact * (gate * sg)
        dgate = dact * val * (sg * (1.0 + gate * (1.0 - sg)))
        dup = jnp.concatenate([dgate, dval], axis=1)
        dup1, dup2 = _rows_after(dup, carry_ref[...])
        carry_ref[...] = dup[0:8]
        u = u_ref[...].astype(F32)
        gcw_ref[2:3, :] += jnp.sum(dup * u, axis=0, keepdims=True)
        gcw_ref[1:2, :] += jnp.sum(dup1 * u, axis=0, keepdims=True)
        gcw_ref[0:1, :] += jnp.sum(dup2 * u, axis=0, keepdims=True)
        cw = cw_ref[...]
        du = (cw[2:3] * dup + cw[1:2] * dup1 + cw[0:1] * dup2).astype(BF16)
        du_ref[...] = du
        dhn = lax.dot_general(du, wu_ref[...], NT, preferred_element_type=F32)
        h1v = h1_ref[...]
        dh1, dg = _rms_bwd(dhn, h1v, _rstd(h1v), g_ref[...])
        dh1_ref[...] = dh2v + dh1
        gg_ref[...] += jnp.sum(dg, axis=0, keepdims=True)

    row = lambda w: pl.BlockSpec((tm, w), lambda i: (nt - 1 - i, 0))
    return pl.pallas_call(
        body, name="ffn_bwd", grid=(nt,),
        in_specs=[row(D_MODEL), row(FF2), row(FF2),
                  row(D_MODEL), _resident((D_MODEL, FF2)), _resident((3, FF2)), _resident((D_FF, D_MODEL)),
                  _resident((1, D_MODEL))],
        out_specs=[row(FF2), row(D_MODEL), pl.BlockSpec((3, FF2), lambda i: (0, 0)),
                   pl.BlockSpec((1, D_MODEL), lambda i: (0, 0))],
        out_shape=[jax.ShapeDtypeStruct((s, FF2), BF16), jax.ShapeDtypeStruct((s, D_MODEL), F32),
                   jax.ShapeDtypeStruct((3, FF2), F32), jax.ShapeDtypeStruct((1, D_MODEL), F32)],
        scratch_shapes=[pltpu.VMEM((8, FF2), F32)],
        compiler_params=_params("arbitrary"),
    )(dh2, u, up, h1, w_up, ffn_cw, w_down, g2)


def _mix_bwd(dh1, gates, a, cv, c3, attn, conv, merged, conv_w, w_ab, w_cb, w_out, tm, comm=None):
    s = dh1.shape[0]
    nt = s // tm
    halo = 16

    def body(dh1_ref, gt_ref, a_ref, cv_ref, c3_ref, ch_ref, at_ref, cn_ref, mg_ref, cw_ref, wab_ref, wcb_ref,
             wo_ref, dat_ref, dc3_ref, dgt_ref, gcw_ref, gab_ref, gcb_ref, gout_ref,
             carry_ref, ab_acc, cb_acc, out_acc):
        i = pl.program_id(0)

        @pl.when(i == 0)
        def _():
            for ref in (carry_ref, gcw_ref, ab_acc, cb_acc, out_acc):
                ref[...] = jnp.zeros_like(ref)

        dh1v = dh1_ref[...].astype(BF16)
        out_acc[...] += lax.dot_general(mg_ref[...], dh1v, TN, preferred_element_type=F32)
        dm = lax.dot_general(dh1v, wo_ref[...], NT, preferred_element_type=F32)
        gt = gt_ref[...].astype(F32)
        sa, sc = _sigmoid(gt[:, :D_MODEL]), _sigmoid(gt[:, D_MODEL:])
        da = (dm * sa).astype(BF16)
        dcv = (dm * sc).astype(BF16)
        ab_acc[...] += lax.dot_general(at_ref[...], da, TN, preferred_element_type=F32)
        cb_acc[...] += lax.dot_general(cn_ref[...], dcv, TN, preferred_element_type=F32)
        dgt_ref[...] = jnp.concatenate(
            [dm * a_ref[...].astype(F32) * (sa * (1.0 - sa)), dm * cv_ref[...].astype(F32) * (sc * (1.0 - sc))],
            axis=1).astype(BF16)
        dat_ref[...] = lax.dot_general(da, wab_ref[...], NT, preferred_element_type=F32).astype(BF16)
        dconv = lax.dot_general(dcv, wcb_ref[...], NT, preferred_element_type=F32)
        c3v = c3_ref[...].astype(F32)
        cb, cc, cx = c3v[:, :CONV_W], c3v[:, CONV_W:2 * CONV_W], c3v[:, 2 * CONV_W:]
        z = cc * cx
        chv = ch_ref[...].astype(F32)[halo - 8:halo] * (i < nt - 1).astype(F32)
        zh = chv[:, CONV_W:2 * CONV_W] * chv[:, 2 * CONV_W:]
        cw = cw_ref[...]
        cz = _causal_conv(z, zh, cw)
        dcz = dconv * cb
        dcz1, dcz2 = _rows_after(dcz, carry_ref[...])
        carry_ref[...] = dcz[0:8]
        gcw_ref[2:3, :] += jnp.sum(dcz * z, axis=0, keepdims=True)
        gcw_ref[1:2, :] += jnp.sum(dcz1 * z, axis=0, keepdims=True)
        gcw_ref[0:1, :] += jnp.sum(dcz2 * z, axis=0, keepdims=True)
        dz = cw[2:3] * dcz + cw[1:2] * dcz1 + cw[0:1] * dcz2
        dc3_ref[...] = jnp.concatenate([dconv * cz, dz * cx, dz * cc], axis=1).astype(BF16)

        @pl.when(i == nt - 1)
        def _():
            gab_ref[...] = ab_acc[...].astype(BF16)
            gcb_ref[...] = cb_acc[...].astype(BF16)
            gout_ref[...] = out_acc[...].astype(BF16)

    row = lambda w: pl.BlockSpec((tm, w), lambda i: (nt - 1 - i, 0))
    return _call(
        comm, body, name="mix_bwd", grid=(nt,),
        in_specs=[row(D_MODEL), row(GATES_W), row(D_MODEL), row(D_MODEL), row(C3_W),
                  pl.BlockSpec((halo, C3_W), lambda i: (jnp.maximum((nt - 1 - i) * (tm // halo) - 1, 0), 0)),
                  row(ATTN_W), row(CONV_W), row(D_MODEL), _resident((3, CONV_W)), _resident((ATTN_W, D_MODEL)),
                  _resident((CONV_W, D_MODEL)), _resident((D_MODEL, D_MODEL))],
        out_specs=[row(ATTN_W), row(C3_W), row(GATES_W), pl.BlockSpec((3, CONV_W), lambda i: (0, 0)),
                   _resident((ATTN_W, D_MODEL)), _resident((CONV_W, D_MODEL)), _resident((D_MODEL, D_MODEL))],
        out_shape=[jax.ShapeDtypeStruct((s, ATTN_W), BF16), jax.ShapeDtypeStruct((s, C3_W), BF16),
                   jax.ShapeDtypeStruct((s, GATES_W), BF16), jax.ShapeDtypeStruct((3, CONV_W), F32),
                   jax.ShapeDtypeStruct((ATTN_W, D_MODEL), BF16), jax.ShapeDtypeStruct((CONV_W, D_MODEL), BF16),
                   jax.ShapeDtypeStruct((D_MODEL, D_MODEL), BF16)],
        scratch_shapes=[pltpu.VMEM((8, CONV_W), F32), pltpu.VMEM((ATTN_W, D_MODEL), F32),
                        pltpu.VMEM((CONV_W, D_MODEL), F32), pltpu.VMEM((D_MODEL, D_MODEL), F32)],
        compiler_params=_params("arbitrary"),
    )(dh1, gates, a, cv, c3, c3, attn, conv, merged, conv_w, w_ab, w_cb, w_out)


def _attn_bwd(qkv, sinks, o, do, comm=None):
    s = qkv.shape[0]
    nb = s // BLOCK

    def body(sk_ref, bias_ref, q_ref, kp_ref, kc_ref, vp_ref, vc_ref, o_ref, do_ref,
             dq_ref, dk_ref, dv_ref, dsk_ref, ck_ref, cvv_ref):
        i = pl.program_id(0)

        @pl.when(i == 0)
        def _():
            ck_ref[...] = jnp.zeros_like(ck_ref)
            cvv_ref[...] = jnp.zeros_like(cvv_ref)
            dsk_ref[...] = jnp.zeros_like(dsk_ref)

        @pl.when(i < nb)
        def _():
            bias = bias_ref[...]
            q, kp, kc, vp, vc = q_ref[...], kp_ref[...], kc_ref[...], vp_ref[...], vc_ref[...]
            ov, dov = o_ref[...], do_ref[...]
            dqs, dks, dvs = [], [], []
            for h in range(N_KV_HEADS):
                hs = slice(h * HEAD_DIM, (h + 1) * HEAD_DIM)
                k2 = jnp.concatenate([kp[:, hs], kc[:, hs]], axis=0)
                v2 = jnp.concatenate([vp[:, hs], vc[:, hs]], axis=0)
                qg, og, dog = _stack_heads(q, h), _stack_heads(ov, h), _stack_heads(dov, h)
                sc = lax.dot_general(qg, k2, NT, preferred_element_type=F32) * ATTN_SCALE + bias
                sink = _sink_column(sk_ref, h)
                m = jnp.maximum(jnp.max(sc, axis=1, keepdims=True), sink)
                p = jnp.exp(sc - m)
                psink = jnp.exp(sink - m)
                inv = 1.0 / (jnp.sum(p, axis=1, keepdims=True) + psink)
                p = p * inv
                delta = jnp.sum(dog.astype(F32) * og.astype(F32), axis=1, keepdims=True)
                dp = lax.dot_general(dog, v2, NT, preferred_element_type=F32)
                ds = (p * (dp - delta)).astype(BF16)
                dqs.append(jnp.dot(ds, k2, preferred_element_type=F32) * ATTN_SCALE)
                dks.append(lax.dot_general(ds, qg, TN, preferred_element_type=F32) * ATTN_SCALE)
                dvs.append(lax.dot_general(p.astype(BF16), dog, TN, preferred_element_type=F32))
                dsink = -(psink * inv * delta)
                for g in range(GROUP):
                    r = h * GROUP + g
                    dsk_ref[r:r + 1, :] += jnp.sum(dsink[g * BLOCK:(g + 1) * BLOCK])
            dq_ref[...] = _unstack_heads(dqs).astype(BF16)
            dk2 = jnp.concatenate(dks, axis=1)
            dv2 = jnp.concatenate(dvs, axis=1)
            dk_ref[...] = (ck_ref[...] + dk2[:BLOCK]).astype(BF16)
            dv_ref[...] = (cvv_ref[...] + dv2[:BLOCK]).astype(BF16)
            ck_ref[...] = dk2[BLOCK:]
            cvv_ref[...] = dv2[BLOCK:]

        @pl.when(i == nb)
        def _():
            dk_ref[...] = ck_ref[...].astype(BF16)
            dv_ref[...] = cvv_ref[...].astype(BF16)

    cur = lambda i: jnp.minimum(i, nb - 1)
    done = lambda i: jnp.maximum(i - 1, 0)
    return _call(
        comm, body, name="attn_bwd", grid=(nb + 1,),
        in_specs=[pl.BlockSpec(memory_space=pltpu.SMEM), _attn_bias_spec(), *_attn_specs(nb),
                  pl.BlockSpec((BLOCK, ATTN_W), lambda i: (cur(i), 0)),
                  pl.BlockSpec((BLOCK, ATTN_W), lambda i: (cur(i), 0))],
        out_specs=[pl.BlockSpec((BLOCK, ATTN_W), lambda i: (cur(i), 0)),
                   pl.BlockSpec((BLOCK, KV_W), lambda i: (done(i), 0)),
                   pl.BlockSpec((BLOCK, KV_W), lambda i: (done(i), 0)),
                   pl.BlockSpec((N_HEADS, 128), lambda i: (0, 0))],
        out_shape=[jax.ShapeDtypeStruct((s, ATTN_W), BF16), jax.ShapeDtypeStruct((s, KV_W), BF16),
                   jax.ShapeDtypeStruct((s, KV_W), BF16), jax.ShapeDtypeStruct((N_HEADS, 128), F32)],
        scratch_shapes=[pltpu.VMEM((BLOCK, KV_W), F32), pltpu.VMEM((BLOCK, KV_W), F32)],
        compiler_params=_params("arbitrary"),
    )(sinks, _attn_bias(), qkv, qkv, qkv, qkv, qkv, o, do)


def _inproj_bwd(dq, dk, dv, dc3, dgt, w_in, x, dh1, g1, tm, comm=None):
    s = x.shape[0]

    def body(dq_ref, dk_ref, dv_ref, dc3_ref, dgt_ref, w_ref, x_ref, dh1_ref, g_ref,
             dx_ref, dp_ref, gb_ref, gg_ref):
        @pl.when(pl.program_id(0) == 0)
        def _():
            gb_ref[...] = jnp.zeros_like(gb_ref)
            gg_ref[...] = jnp.zeros_like(gg_ref)

        dp = jnp.concatenate([dq_ref[...], dk_ref[...], dv_ref[...], dc3_ref[...], dgt_ref[...]], axis=1)
        dp_ref[...] = dp
        gb_ref[...] += jnp.sum(dp.astype(F32), axis=0, keepdims=True)
        dxn = jnp.dot(dp, w_ref[...], preferred_element_type=F32)
        xf = x_ref[...]
        dx, dg = _rms_bwd(dxn, xf, _rstd(xf), g_ref[...])
        dx_ref[...] = dh1_ref[...] + dx
        gg_ref[...] += jnp.sum(dg, axis=0, keepdims=True)

    row = lambda w: pl.BlockSpec((tm, w), lambda i: (i, 0))
    acc = lambda w: pl.BlockSpec((1, w), lambda i: (0, 0))
    return _call(
        comm, body, name="inproj_bwd", grid=(s // tm,),
        in_specs=[row(ATTN_W), row(KV_W), row(KV_W), row(C3_W), row(GATES_W), _resident((IN_W, D_MODEL)),
                  row(D_MODEL), row(D_MODEL), _resident((1, D_MODEL))],
        out_specs=[row(D_MODEL), row(IN_W), acc(IN_W), acc(D_MODEL)],
        out_shape=[jax.ShapeDtypeStruct((s, D_MODEL), F32), jax.ShapeDtypeStruct((s, IN_W), BF16),
                   jax.ShapeDtypeStruct((1, IN_W), F32), jax.ShapeDtypeStruct((1, D_MODEL), F32)],
        compiler_params=_params("arbitrary"),
    )(dq, dk, dv, dc3, dgt, w_in, x, dh1, g1)


def _wgrad(a, b, bm, bn, bk, name, comm=None):
    s, m = a.shape
    n = b.shape[1]
    nk = s // bk

    def body(a_ref, b_ref, o_ref, acc_ref):
        k = pl.program_id(2)

        @pl.when(k == 0)
        def _():
            acc_ref[...] = jnp.zeros_like(acc_ref)

        acc_ref[...] += lax.dot_general(a_ref[...].astype(BF16), b_ref[...].astype(BF16), TN,
                                        preferred_element_type=F32)

        @pl.when(k == nk - 1)
        def _():
            o_ref[...] = acc_ref[...].astype(BF16)

    return _call(
        comm, body, name=name, grid=(m // bm, n // bn, nk),
        in_specs=[pl.BlockSpec((bk, bm), lambda i, j, k: (k, i)), pl.BlockSpec((bk, bn), lambda i, j, k: (k, j))],
        out_specs=pl.BlockSpec((bm, bn), lambda i, j, k: (i, j)),
        out_shape=jax.ShapeDtypeStruct((m, n), BF16),
        scratch_shapes=[pltpu.VMEM((bm, bn), F32)],
        compiler_params=_params("parallel", "parallel", "arbitrary"),
    )(a, b)


class _Carry:
    def __init__(self, jobs, reads=None, bufs=None, fresh=None):
        self.jobs, self.reads, self.bufs, self.fresh = jobs, reads or {}, bufs or {}, fresh or {}
        self.out = {}


class _Job:
    def __init__(self, n_sems, plan):
        self.n_sems, self.plan = n_sems, plan


def _plan_all(jobs, hbm, send, recv):
    pos = _position()
    starts, waits, base = [], [], 0
    for job in jobs:
        s, w = job.plan(hbm, pos, send, recv, base)
        starts, waits, base = starts + s, waits + w, base + job.n_sems
    return starts, waits


def _call(comm, body, **kw):
    if comm is None:
        return pl.pallas_call(body, **kw)
    grid = kw["grid"]
    single = not isinstance(kw["out_shape"], (list, tuple))
    out_shape = [kw["out_shape"]] if single else list(kw["out_shape"])
    out_specs = [kw["out_specs"]] if single else list(kw["out_specs"])
    in_specs = list(kw["in_specs"])
    scratch = list(kw.get("scratch_shapes", ()))
    r_names, b_names, f_names = list(comm.reads), list(comm.bufs), list(comm.fresh)
    n_args, n_out, n_scr = len(in_specs), len(out_shape), len(scratch)
    n_sems = sum(j.n_sems for j in comm.jobs)

    def wrapped(*refs):
        k = n_args
        hbm = dict(zip(r_names, refs[k:k + len(r_names)]))
        k += len(r_names) + len(b_names)
        outs = refs[k:k + n_out]
        k += n_out
        hbm.update(zip(b_names + f_names, refs[k:k + len(b_names) + len(f_names)]))
        k += len(b_names) + len(f_names)
        send, recv = refs[k + n_scr:]
        starts, waits = _plan_all(comm.jobs, hbm, send, recv)
        ids = [pl.program_id(a) for a in range(len(grid))]
        first = functools.reduce(jnp.logical_and, [i == 0 for i in ids])
        last = functools.reduce(jnp.logical_and, [i == g - 1 for i, g in zip(ids, grid)])

        @pl.when(first)
        def _():
            for cp in starts:
                cp.start()

        body(*refs[:n_args], *outs, *refs[k:k + n_scr])

        @pl.when(last)
        def _():
            for cp in waits:
                cp.wait_recv()
            for cp in starts:
                cp.wait_send()

    sems = pltpu.SemaphoreType.DMA((n_sems,))
    held = [jax.ShapeDtypeStruct(a.shape, a.dtype) for a in comm.bufs.values()] + list(comm.fresh.values())
    call = pl.pallas_call(
        wrapped, name=kw["name"], grid=grid,
        in_specs=in_specs + [_ANY] * (len(r_names) + len(b_names)),
        out_specs=out_specs + [_ANY] * len(held),
        out_shape=out_shape + held,
        input_output_aliases={n_args + len(r_names) + i: n_out + i for i in range(len(b_names))},
        scratch_shapes=scratch + [sems, sems],
        compiler_params=_params(*["arbitrary"] * len(grid)),
    )

    def run(*args):
        res = call(*args, *comm.reads.values(), *comm.bufs.values())
        comm.out = dict(zip(b_names + f_names, res[n_out:]))
        return res[0] if single else res[:n_out]

    return run


def _exchange(name, phases, reads=None, bufs=None, fresh=None):
    comm = _Carry([j for ph in phases for j in ph], reads, bufs, fresh)
    r_names, b_names, f_names = list(comm.reads), list(comm.bufs), list(comm.fresh)
    n_sems = sum(j.n_sems for j in comm.jobs)

    def body(*refs):
        hbm = dict(zip(r_names, refs[:len(r_names)]))
        k = len(r_names) + len(b_names)
        hbm.update(zip(b_names + f_names, refs[k:k + len(b_names) + len(f_names)]))
        send, recv = refs[-2:]
        pos = _position()
        started, base = [], 0
        for ph in phases:
            waits = []
            for job in ph:
                s, w = job.plan(hbm, pos, send, recv, base)
                base += job.n_sems
                for cp in s:
                    cp.start()
                started, waits = started + s, waits + w
            for cp in waits:
                cp.wait_recv()
        for cp in started:
            cp.wait_send()

    sems = pltpu.SemaphoreType.DMA((n_sems,))
    held = [jax.ShapeDtypeStruct(a.shape, a.dtype) for a in comm.bufs.values()] + list(comm.fresh.values())
    res = pl.pallas_call(
        body, name=name, in_specs=[_ANY] * (len(r_names) + len(b_names)), out_specs=[_ANY] * len(held),
        out_shape=held, input_output_aliases={len(r_names) + i: i for i in range(len(b_names))},
        scratch_shapes=[sems, sems],
    )(*comm.reads.values(), *comm.bufs.values())
    return dict(zip(b_names + f_names, res))


_HBM = pl.BlockSpec(memory_space=pltpu.HBM)
_SEM = pl.BlockSpec(memory_space=pltpu.SEMAPHORE)
_EFFECT = pltpu.SideEffectType.DATAFLOW_SIDE_EFFECTING


def _start_exchanges(name, groups):
    names = [list(arrays) for _, arrays in groups]
    first = [sum(len(ns) for ns in names[:g]) for g in range(len(groups))]
    n, ng = sum(len(ns) for ns in names), len(groups)

    def body(*refs):
        for g, (jobs, _) in enumerate(groups):
            hbm = dict(zip(names[g], refs[first[g]:first[g] + len(names[g])]))
            for cp in _plan_all(jobs, hbm, refs[n + 2 * g], refs[n + 2 * g + 1])[0]:
                cp.start()
        refs[-1][...] = jnp.zeros_like(refs[-1])

    given = [pltpu.with_memory_space_constraint(
        a if isinstance(a, jax.Array) else lax.empty(a.shape, a.dtype), pltpu.HBM)
        for _, arrays in groups for a in arrays.values()]
    sems = [pltpu.SemaphoreType.DMA((sum(j.n_sems for j in jobs),)) for jobs, _ in groups for _ in range(2)]
    res = pl.pallas_call(
        body, name=name,
        out_shape=(*sems, *[pltpu.HBM(a.shape, a.dtype) for a in given], jax.ShapeDtypeStruct((8, 128), F32)),
        in_specs=[_HBM] * n, out_specs=(*[_SEM] * (2 * ng), *[_HBM] * n, pl.BlockSpec(memory_space=pltpu.VMEM)),
        input_output_aliases={i: 2 * ng + i for i in range(n)},
        compiler_params=pltpu.CompilerParams(has_side_effects=_EFFECT),
    )(*given)
    held = res[2 * ng:2 * ng + n]
    states = [(names[g], groups[g][0], res[2 * g], res[2 * g + 1], held[first[g]:first[g] + len(names[g])])
              for g in range(ng)]
    return states, res[-1]


def _start_exchange(name, jobs, arrays):
    states, token = _start_exchanges(name, [(jobs, arrays)])
    return states[0], token


def _finish_exchange(name, state, after):
    names, jobs, send_sem, recv_sem, held = state
    n = len(names)

    def body(*refs):
        hbm = dict(zip(names, refs[:n]))
        send, recv = refs[n:n + 2]
        starts, waits = _plan_all(jobs, hbm, send, recv)
        for cp in waits:
            cp.wait_recv()
        for cp in starts:
            cp.wait_send()

    res = pl.pallas_call(
        body, name=name, out_shape=tuple(pltpu.HBM(a.shape, a.dtype) for a in held),
        in_specs=[_HBM] * n + [_SEM, _SEM, _ANY], out_specs=tuple([_HBM] * n),
        input_output_aliases={i: i for i in range(n)},
        compiler_params=pltpu.CompilerParams(has_side_effects=_EFFECT),
    )(*held, send_sem, recv_sem, after)
    return dict(zip(names, res))


def _row_tile(rows, bytes_per_row):
    best = 16
    for t in range(16, rows + 1, 16):
        if rows % t == 0 and t * bytes_per_row <= 9 * 1024 * 1024:
            best = t
    return best


def _rowwise(fn, ins, out_dtypes, name, after=None):
    rows, cols = ins[0].shape
    per_row = sum(cols * a.dtype.itemsize for a in ins) + sum(cols * jnp.dtype(d).itemsize for d in out_dtypes)
    tr = _row_tile(rows, per_row)
    n_in = len(ins)

    def body(*refs):
        outs = fn(*[r[...] for r in refs[:n_in]])
        for o_ref, o in zip(refs[-len(out_dtypes):], outs):
            o_ref[...] = o.astype(o_ref.dtype)

    tile = pl.BlockSpec((tr, cols), lambda i: (i, 0))
    behind = [] if after is None else [after]
    return pl.pallas_call(
        body, name=name, grid=(rows // tr,),
        in_specs=[tile] * n_in + [pl.BlockSpec((8, 128), lambda i: (0, 0))] * len(behind),
        out_specs=[tile] * len(out_dtypes),
        out_shape=[jax.ShapeDtypeStruct((rows, cols), d) for d in out_dtypes],
        compiler_params=_params("parallel"),
    )(*ins, *behind)


def _tiled(fn, name, grid, pos, ins, outs):
    n_in = len(ins)

    def body(pos_ref, *refs):
        res = fn(*[r[...] for r in refs[:n_in]])
        for o_ref, o in zip(refs[n_in:], res):
            o_ref[...] = o.astype(o_ref.dtype)

    return pl.pallas_call(
        body, name=name,
        grid_spec=pltpu.PrefetchScalarGridSpec(
            num_scalar_prefetch=1, grid=grid,
            in_specs=[pl.BlockSpec(bs, im) for _, bs, im in ins],
            out_specs=[pl.BlockSpec(bs, im) for _, _, bs, im in outs]),
        out_shape=[jax.ShapeDtypeStruct(s, d) for s, d, _, _ in outs],
        compiler_params=_params("parallel"),
    )(pos, *[a for a, _, _ in ins])


def _adamw(w, g, m, v):
    m = ADAM_B1 * m + (1.0 - ADAM_B1) * g
    v = ADAM_B2 * v + (1.0 - ADAM_B2) * (g * g)
    m_hat = m / (1.0 - ADAM_B1 ** ADAM_STEP)
    v_hat = v / (1.0 - ADAM_B2 ** ADAM_STEP)
    return -ADAM_LR * (m_hat / (jnp.sqrt(v_hat) + ADAM_EPS) + ADAM_WD * w), m, v


def _adamw_small(params):
    n = len(params)

    def body(*refs):
        for k in range(n):
            w, g, m, v = (r[...] for r in refs[4 * k:4 * k + 4])
            for o_ref, o in zip(refs[4 * n + 3 * k:4 * n + 3 * k + 3], _adamw(w, g, m, v)):
                o_ref[...] = o

    flat = [a for p in params for a in p]
    return pl.pallas_call(
        body, name="adamw_small",
        out_shape=[jax.ShapeDtypeStruct(p[0].shape, F32) for p in params for _ in range(3)],
    )(*flat)


class _Layout:
    def __init__(self, rows, cols, stacked):
        self.rows, self.cols, self.stacked = rows, cols, stacked

    def whole(self, rows=None):
        r = self.rows if rows is None else rows
        return (N_CHIPS, r, self.cols) if self.stacked else (r, N_CHIPS * self.cols)

    def part_rows(self, h, q=0, nq=1):
        n = self.rows // 2 // nq
        return pl.ds(pl.multiple_of(h * (self.rows // 2) + q * n, 16), n)

    def half_rows(self, h):
        return self.part_rows(h)

    def block(self, ref, p, rows=slice(None)):
        if self.stacked:
            return ref.at[p, rows, :]
        return ref.at[rows, pl.ds(pl.multiple_of(p * self.cols, 128), self.cols)]

    def all_chips(self, ref, rows):
        return ref.at[:, rows, :] if self.stacked else ref.at[rows, :]


BIG = (
    _Layout(IN_SHARD, D_MODEL, True),
    _Layout(ATTN_W, D_MODEL // N_CHIPS, False),
    _Layout(CONV_W, D_MODEL // N_CHIPS, False),
    _Layout(D_MODEL // N_CHIPS, D_MODEL, True),
    _Layout(D_MODEL, FF2 // N_CHIPS, False),
    _Layout(D_FF // N_CHIPS, D_MODEL, True),
)
N_BIG = len(BIG)
_ANY = pl.BlockSpec(memory_space=pl.ANY)


def _position():
    x, y, c = lax.axis_index("x"), lax.axis_index("y"), lax.axis_index("c")
    return x, y, c, 2 * x + y


def _core_of_chip(p, c):
    return (p >> 1, p & 1, c)


def _place_cast(shard, lay, pos, name, after=None):
    rows, cols = shard.shape
    tr = _row_tile(rows, cols * 6)
    if lay.stacked:
        out = (lay.whole(), BF16, (None, tr, cols), lambda i, pos: (pos[0], i, 0))
    else:
        out = (lay.whole(), BF16, (tr, cols), lambda i, pos: (i, pos[0]))
    ins = [(shard, (tr, cols), lambda i, pos: (i, 0))]
    if after is not None:
        ins.append((after, (8, 128), lambda i, pos: (0, 0)))
    return _tiled(lambda a, *_: (a,), name, (rows // tr,), pos, ins, [out])[0]


def _remote(src, dst, send, recv, k, device):
    return pltpu.make_async_remote_copy(src_ref=src, dst_ref=dst, send_sem=send.at[k], recv_sem=recv.at[k],
                                        device_id=device, device_id_type=MESH)


def _arrival(dst, send, recv, k, me):
    return _remote(dst, dst, send, recv, k, me)


def _gather_ici(lay, name, q=0, nq=1):
    def plan(hbm, pos, send, recv, base):
        x, y, c, me = pos
        rows = lay.part_rows(c, q, nq)
        mine = lay.block(hbm[name], me, rows)
        starts = [_remote(mine, mine, send, recv, base + d - 1, _core_of_chip(me ^ d, c)) for d in (1, 2, 3)]
        waits = [_arrival(lay.block(hbm[name], me ^ d, rows), send, recv, base + d - 1, (x, y, c)) for d in (1, 2, 3)]
        return starts, waits
    return _Job(3, plan)


def _gather_d2d(lay, name, q=0, nq=1):
    def plan(hbm, pos, send, recv, base):
        x, y, c, me = pos
        starts, waits = [], []
        for d in (1, 2, 3):
            got = lay.block(hbm[name], me ^ d, lay.part_rows(c, q, nq))
            starts.append(_remote(got, got, send, recv, base + d - 1, (x, y, 1 - c)))
            waits.append(_arrival(lay.block(hbm[name], me ^ d, lay.part_rows(1 - c, q, nq)), send, recv, base + d - 1,
                                  (x, y, c)))
        return starts, waits
    return _Job(3, plan)


def _rs_pair(lay, grad, theirs):
    def plan(hbm, pos, send, recv, base):
        x, y, c, _ = pos
        out = _remote(lay.all_chips(hbm[grad], lay.half_rows(1 - c)), hbm[theirs], send, recv, base, (x, y, 1 - c))
        return [out], [_arrival(hbm[theirs], send, recv, base, (x, y, c))]
    return _Job(1, plan)


def _rs_chips(lay, sums, slots):
    def plan(hbm, pos, send, recv, base):
        x, y, c, me = pos
        starts = [_remote(lay.block(hbm[sums], me ^ d), hbm[slots].at[me], send, recv, base + d - 1,
                          _core_of_chip(me ^ d, c)) for d in (1, 2, 3)]
        waits = [_arrival(hbm[slots].at[me ^ d], send, recv, base + d - 1, (x, y, c)) for d in (1, 2, 3)]
        return starts, waits
    return _Job(3, plan)


def _rs_share(lay, shard):
    def plan(hbm, pos, send, recv, base):
        x, y, c, _ = pos
        mine = hbm[shard].at[lay.half_rows(c), :]
        other = hbm[shard].at[lay.half_rows(1 - c), :]
        return [_remote(mine, mine, send, recv, base, (x, y, 1 - c))], [_arrival(other, send, recv, base, (x, y, c))]
    return _Job(1, plan)


def _slots_shape(lay):
    return jax.ShapeDtypeStruct((N_CHIPS, lay.rows // 2, lay.cols), BF16)


def _theirs_shape(lay):
    return jax.ShapeDtypeStruct(lay.whole(lay.rows // 2), BF16)


def _pair_sum(grad, theirs, lay, pos, name):
    half = lay.rows // 2
    add = lambda a, b: (a.astype(F32) + b.astype(F32),)
    if lay.stacked:
        tr = _row_tile(half, lay.cols * 6)
        nt = half // tr
        flat = lambda a: a.reshape(-1, lay.cols)
        mine = lambda t, pos: ((t // nt) * (2 * nt) + pos[1] * nt + t % nt, 0)
        grid, blk = (N_CHIPS * nt,), (tr, lay.cols)
        grad, theirs = flat(grad), flat(theirs)
    else:
        tr = _row_tile(half, N_CHIPS * lay.cols * 6)
        nt = half // tr
        mine = lambda t, pos: (pos[1] * nt + t, 0)
        grid, blk = (nt,), (tr, N_CHIPS * lay.cols)
    same = lambda t, pos: (t, 0)
    out = _tiled(add, name, grid, pos, [(grad, blk, mine), (theirs, blk, same)], [(theirs.shape, BF16, blk, same)])[0]
    return out.reshape(lay.whole(half))


def _chip_sum(sums, slots, lay, pos, name, after=None):
    half = lay.rows // 2
    tr = _row_tile(half, lay.cols * 12)
    nt = half // tr
    blk3 = (None, tr, lay.cols)
    if lay.stacked:
        own = (sums, blk3, lambda i, pos: (pos[0], i, 0))
    else:
        own = (sums, (tr, lay.cols), lambda i, pos: (i, pos[0]))
    others = [(slots, blk3, functools.partial(lambda d, i, pos: (pos[0] ^ d, i, 0), d)) for d in (1, 2, 3)]

    def add(a, b1, b2, b3, *_):
        return (((a.astype(F32) + b1.astype(F32)) + b2.astype(F32)) + b3.astype(F32),)

    if after is not None:
        others.append((after, (8, 128), lambda i, pos: (0, 0)))
    return _tiled(add, name, (nt,), pos, [own] + others,
                  [((lay.rows, lay.cols), F32, (tr, lay.cols), lambda i, pos: (pos[1] * nt + i, 0))])[0]


N_DEV = 8


def _to_all(src, slots):
    def plan(hbm, pos, send, recv, base):
        x, y, c, _ = pos
        idx = 4 * x + 2 * y + c
        starts = [_remote(hbm[src], hbm[slots].at[idx], send, recv, base + k - 1,
                          (x ^ (k >> 2), y ^ ((k >> 1) & 1), c ^ (k & 1))) for k in range(1, N_DEV)]
        waits = [_arrival(hbm[slots].at[idx ^ k], send, recv, base + k - 1, (x, y, c)) for k in range(1, N_DEV)]
        return starts, waits
    return _Job(N_DEV - 1, plan)


def _sum_slots(own, slots, pos):
    def body(pos_ref, own_ref, slots_ref, o_ref):
        idx = 2 * pos_ref[0] + pos_ref[1]
        term = lambda q: jnp.where(idx == q, own_ref[...], slots_ref[q])
        acc = term(0)
        for q in range(1, N_DEV):
            acc = acc + term(q)
        o_ref[...] = acc

    return pl.pallas_call(
        body, name="sum_small", out_shape=jax.ShapeDtypeStruct(own.shape, F32),
        in_specs=[pl.BlockSpec(memory_space=pltpu.SMEM), pl.BlockSpec(memory_space=pltpu.VMEM),
                  pl.BlockSpec(memory_space=pltpu.VMEM)],
    )(pos, own, slots)


def _pack_rows(parts):
    padded = [jnp.pad(a, ((0, -a.shape[0] % 8), (0, 0))) for a in parts]
    starts = [sum(p.shape[0] for p in padded[:k]) for k in range(len(padded))]
    return jnp.concatenate(padded, axis=0), starts


def kernel(x, mix_norm, w_in, b_in, sinks, conv_w, w_attn_branch, w_conv_branch, w_out, ffn_norm, w_up, ffn_conv_w, w_down, final_norm, loss_target, m_mix_norm, m_w_in, m_b_in, m_sinks, m_conv_w, m_w_attn_branch, m_w_conv_branch, m_w_out, m_ffn_norm, m_w_up, m_ffn_conv_w, m_w_down, m_final_norm, v_mix_norm, v_w_in, v_b_in, v_sinks, v_conv_w, v_w_attn_branch, v_w_conv_branch, v_w_out, v_ffn_norm, v_w_up, v_ffn_conv_w, v_w_down, v_final_norm):
    me = 2 * lax.axis_index("x") + lax.axis_index("y")
    big_w = [w_in[0].T, w_attn_branch[0], w_conv_branch[0], w_out[0], w_up[0], w_down[0]]
    big_m = [m_w_in[0].T, m_w_attn_branch[0], m_w_conv_branch[0], m_w_out[0], m_w_up[0], m_w_down[0]]
    big_v = [v_w_in[0].T, v_w_attn_branch[0], v_w_conv_branch[0], v_w_out[0], v_w_up[0], v_w_down[0]]
    names = ("w_in", "w_ab", "w_cb", "w_out", "w_up", "w_down")

    pos = jnp.stack([me, lax.axis_index("c")]).astype(jnp.int32)

    lay = dict(zip(names, BIG))
    xs, target, sk = x[0], loss_target[0], sinks[0]
    s = xs.shape[0]
    tm, tm2, bk, bk2 = min(256, s), min(512, s), min(1024, s), min(2048, s)

    taps, (_, t0) = _pack_rows([conv_w[0], ffn_conv_w[0].reshape(3 * (FF2 // N_CHIPS // 128), 128)])
    placed = {"w_in": _place_cast(big_w[0], lay["w_in"], pos, "cast_w_in")}
    fly_in, started = _start_exchange("gather_in_start", [_gather_ici(lay["w_in"], "w_in")], {"w_in": placed["w_in"]})
    taps_flight, started = _start_exchange("taps_start", [_to_all("v", "slots")],
                                           {"v": taps + started[0:1], "slots": jnp.zeros((N_DEV, *taps.shape), F32)})
    for w, n in zip(big_w[1:], names[1:]):
        placed[n] = _place_cast(w, lay[n], pos, "cast_" + n, after=started)
    trio = ("w_ab", "w_cb", "w_out")
    (fly_trio, fly_up, fly_down), started = _start_exchanges("gather_rest_start", [
        ([_gather_ici(lay[n], n) for n in ws], {n: placed[n] for n in ws}) for ws in (trio, ("w_up",), ("w_down",))])

    got = _finish_exchange("gather_in_wait", fly_in, after=started)
    w_in_full = _exchange("gather_in_d2d", [[_gather_d2d(lay["w_in"], "w_in")]], bufs=got)["w_in"].reshape(IN_W, D_MODEL)
    xn, qkv, c3, gates = _inproj_fwd(xs, mix_norm, w_in_full, b_in, tm2)
    k2 = _Carry([_gather_d2d(lay[n], n) for n in trio], bufs=_finish_exchange("gather_trio_wait", fly_trio, after=qkv))
    attn = _attn_fwd(qkv, sk, comm=k2)
    w_ab, w_cb = k2.out["w_ab"], k2.out["w_cb"]
    w_out_full = k2.out["w_out"].reshape(D_MODEL, D_MODEL)
    k3 = _Carry([_gather_d2d(lay["w_up"], "w_up")], bufs=_finish_exchange("gather_up_wait", fly_up, after=attn))
    taps = _finish_exchange("taps_wait", taps_flight, after=attn)
    taps = lax.dynamic_update_slice(taps["slots"], taps["v"][None], (2 * me + lax.axis_index("c"), 0, 0))
    conv_full = taps[0::2, 0:3].transpose(1, 0, 2).reshape(3, CONV_W)
    ffn_cw_full = taps[0::2, t0:t0 + 33].reshape(N_CHIPS, 3, FF2 // N_CHIPS).transpose(1, 0, 2).reshape(3, FF2)
    conv, a, cv, merged, h1, hn = _mix_fwd(xs, attn, c3, gates, conv_full, w_ab, w_cb, w_out_full, ffn_norm, tm2, comm=k3)
    w_up_full = k3.out["w_up"]
    w_down_full = _exchange("gather_down_d2d", [[_gather_d2d(lay["w_down"], "w_down")]],
                            bufs=_finish_exchange("gather_down_wait", fly_down, after=hn))["w_down"].reshape(D_FF, D_MODEL)
    u, up, act, dh2, loss_part, g_fn = _ffn_fwd_loss(hn, h1, w_up_full, ffn_cw_full, w_down_full,
                                                     final_norm[None, :], target, tm)

    grads, sums, slots = {}, {}, {}

    def pair(*ws):
        return _Carry([_rs_pair(lay[n], "g_" + n, "t_" + n) for n in ws], reads={"g_" + n: grads[n] for n in ws},
                      fresh={"t_" + n: _theirs_shape(lay[n]) for n in ws})

    def chips(*ws, also=None):
        k = _Carry([_rs_chips(lay[n], "s_" + n, "r_" + n) for n in ws], reads={"s_" + n: sums[n] for n in ws},
                   fresh={"r_" + n: _slots_shape(lay[n]) for n in ws})
        if also is not None:
            k = _Carry(k.jobs + also.jobs, {**k.reads, **also.reads}, None, {**k.fresh, **also.fresh})
        return k

    def pair_sums(k, *ws):
        for n in ws:
            sums[n] = _pair_sum(grads[n], k.out["t_" + n], lay[n], pos, "pair_sum_" + n)

    def take_slots(k, *ws):
        for n in ws:
            slots[n] = k.out["r_" + n]

    du, dh1, g_fcw, g_g2 = _ffn_bwd(dh2, u, up, h1, w_up_full, ffn_cw_full, w_down_full, ffn_norm, tm)
    grads["w_down"] = _wgrad(act, dh2, D_FF // 2, D_MODEL, bk2, "wgrad_down").reshape(lay["w_down"].whole())
    k4 = pair("w_down")
    grads["w_up"] = _wgrad(hn, du, D_MODEL, FF2 // 4, bk2, "wgrad_up", comm=k4)
    pair_sums(k4, "w_down")
    k5 = chips("w_down", also=pair("w_up"))
    dattn, dc3, dgt, g_cw, grads["w_ab"], grads["w_cb"], gw_out = _mix_bwd(
        dh1, gates, a, cv, c3, attn, conv, merged, conv_full, w_ab, w_cb, w_out_full, tm, comm=k5)
    grads["w_out"] = gw_out.reshape(lay["w_out"].whole())
    take_slots(k5, "w_down")
    pair_sums(k5, "w_up")
    k6 = chips("w_up", also=pair("w_out", "w_ab", "w_cb"))
    dq, dk, dv, g_sk = _attn_bwd(qkv, sk, attn, dattn, comm=k6)
    take_slots(k6, "w_up")
    pair_sums(k6, "w_out", "w_ab", "w_cb")
    grad_x, dproj, g_b, g_g1 = _inproj_bwd(dq, dk, dv, dc3, dgt, w_in_full, xs, dh1, mix_norm, tm2)

    parts = [loss_part, g_g1, g_b, jnp.pad(g_sk[:, 0], (0, 120))[None, :], g_cw, g_g2, g_fcw, g_fn]
    packed, at = _pack_rows([p.reshape(-1, 128) for p in parts])
    small_flight, started = _start_exchange("small_start", [_to_all("v", "slots")],
                                            {"v": packed, "slots": jnp.zeros((N_DEV, *packed.shape), F32)})
    k8 = chips("w_out", "w_ab", "w_cb")
    k8.reads["after"] = started
    grads["w_in"] = _wgrad(dproj, xn, IN_W // 2, D_MODEL, bk, "wgrad_in", comm=k8).reshape(lay["w_in"].whole())
    take_slots(k8, "w_out", "w_ab", "w_cb")
    others = names[1:]
    in_flight, started = _start_exchange("rs_pair_in_start", [_rs_pair(lay["w_in"], "g", "t")],
                                         {"g": grads["w_in"], "t": _theirs_shape(lay["w_in"])})
    halves = {n: _chip_sum(sums[n], slots[n], lay[n], pos, "chip_sum_" + n, after=started) for n in ("w_up", "w_down")}
    landed = _finish_exchange("rs_pair_in_wait", in_flight, after=halves["w_down"])
    sums["w_in"] = _pair_sum(landed["g"], landed["t"], lay["w_in"], pos, "pair_sum_w_in")
    in_flight, started = _start_exchange("rs_chips_in_start", [_rs_chips(lay["w_in"], "s", "r")],
                                         {"s": sums["w_in"], "r": _slots_shape(lay["w_in"])})
    for n in trio:
        halves[n] = _chip_sum(sums[n], slots[n], lay[n], pos, "chip_sum_" + n, after=started)
    shared = _exchange("share_halves", [[_rs_share(lay[n], n) for n in others]], bufs=halves)
    w_of, m_of, v_of = dict(zip(names, big_w)), dict(zip(names, big_m)), dict(zip(names, big_v))

    def adam(n, g, after=None):
        return _rowwise(lambda w, g, m, v: (g, *_adamw(w, g, m, v)), [w_of[n], g, m_of[n], v_of[n]], [F32] * 4,
                        "adamw_" + n, after=after)

    new_of, last = {}, None
    for n in ("w_up", "w_down", "w_out", "w_ab", "w_cb"):
        new_of[n] = adam(n, shared[n], last)
        last = new_of[n][1]

    arrived = _finish_exchange("small_wait", small_flight, after=last)
    total = _sum_slots(arrived["v"], arrived["slots"], pos)
    part = lambda k: total[at[k]:at[k] + parts[k].size // 128].reshape(parts[k].shape)
    loss = total[0, 0]
    g_mix, g_b, g_g2, g_fn = part(1), part(2), part(5), part(7)
    g_sk = part(3)[:, 0:N_HEADS]
    g_cw = lax.dynamic_slice(part(4), (0, me * 128), (3, 128))
    g_fcw = lax.dynamic_slice(part(6), (0, me * (FF2 // N_CHIPS)), (3, FF2 // N_CHIPS))
    small_p = [
        (mix_norm, g_mix, m_mix_norm, v_mix_norm), (b_in, g_b, m_b_in, v_b_in), (sinks, g_sk, m_sinks, v_sinks),
        (conv_w[0], g_cw, m_conv_w[0], v_conv_w[0]), (ffn_norm, g_g2, m_ffn_norm, v_ffn_norm),
        (ffn_conv_w[0], g_fcw, m_ffn_conv_w[0], v_ffn_conv_w[0]),
        (final_norm[None, :], g_fn, m_final_norm[None, :], v_final_norm[None, :])]
    small_new = _adamw_small(small_p)
    small_new = [small_new[3 * k:3 * k + 3] for k in range(len(small_p))]

    landed = _finish_exchange("rs_chips_in_wait", in_flight, after=small_new[0][0])
    half_in = _chip_sum(landed["s"], landed["r"], lay["w_in"], pos, "chip_sum_w_in")
    shared["w_in"] = _exchange("share_in", [[_rs_share(lay["w_in"], "w_in")]], bufs={"w_in": half_in})["w_in"]
    new_of["w_in"] = adam("w_in", shared["w_in"])
    big_g = [new_of[n][0] for n in names]
    big_new = [new_of[n][1:] for n in names]

    order = [("s", 0), ("b", 0), ("s", 1), ("s", 2), ("s", 3), ("b", 1), ("b", 2), ("b", 3), ("s", 4), ("b", 4),
             ("s", 5), ("b", 5), ("s", 6)]
    shapes = [mix_norm.shape, w_in.shape, b_in.shape, sinks.shape, conv_w.shape, w_attn_branch.shape,
              w_conv_branch.shape, w_out.shape, ffn_norm.shape, w_up.shape, ffn_conv_w.shape, w_down.shape,
              final_norm.shape]
    small_g = [p[1] for p in small_p]
    big_g[0] = big_g[0].T
    big_new[0] = [a.T for a in big_new[0]]
    out_g = [(small_g[k] if kind == "s" else big_g[k]).reshape(shp) for (kind, k), shp in zip(order, shapes)]
    news = [[(small_new[k][j] if kind == "s" else big_new[k][j]).reshape(shp) for (kind, k), shp in zip(order, shapes)]
            for j in range(3)]
    return (loss, grad_x[None], *out_g, *news[0], *news[1], *news[2])
```

```python
import functools

import jax
import jax.numpy as jnp
from jax import lax
from jax.experimental import pallas as pl
from jax.experimental.pallas import tpu as pltpu

F32 = jnp.float32
BF16 = jnp.bfloat16

D_MODEL = 1024
HEAD_DIM = 64
N_HEADS = 8
N_KV_HEADS = 2
GROUP = N_HEADS // N_KV_HEADS
BLOCK = 128
ATTN_SCALE = HEAD_DIM ** -0.5
ATTN_W = N_HEADS * HEAD_DIM
KV_W = N_KV_HEADS * HEAD_DIM
CONV_W = 512
QKV_W = ATTN_W + 2 * KV_W
C3_W = 3 * CONV_W
GATES_W = 2 * D_MODEL
IN_W = QKV_W + C3_W + GATES_W
D_FF = 2816
FF2 = 2 * D_FF
NORM_EPS = 1e-5
N_CHIPS = 4
IN_SHARD = IN_W // N_CHIPS
NEG = -1e30

ADAM_LR = 0.001
ADAM_B1 = 0.9
ADAM_B2 = 0.999
ADAM_EPS = 1e-08
ADAM_WD = 0.01
ADAM_STEP = 10

VMEM_LIMIT = 56 * 1024 * 1024
MESH = pl.DeviceIdType.MESH

NT = (((1,), (1,)), ((), ()))
TN = (((0,), (0,)), ((), ()))


def _params(*sem):
    return pltpu.CompilerParams(dimension_semantics=sem, vmem_limit_bytes=VMEM_LIMIT)


def _resident(shape):
    return pl.BlockSpec(shape, lambda *_: (0,) * len(shape), pipeline_mode=pl.Buffered(1))


def _sigmoid(v):
    return 0.5 * jnp.tanh(0.5 * v) + 0.5


def _rstd(v):
    return lax.rsqrt(jnp.mean(v * v, axis=-1, keepdims=True) + NORM_EPS)


def _rms_bwd(dy, v, rstd, g):
    vhat = v * rstd
    t = dy * g
    return rstd * (t - vhat * jnp.mean(t * vhat, axis=-1, keepdims=True)), dy * vhat


def _taps(z, cw):
    return cw[2:3] * z + cw[1:2] * pltpu.roll(z, 1, 0) + cw[0:1] * pltpu.roll(z, 2, 0)


def _causal_conv(z, prev, cw):
    edge = _taps(jnp.concatenate([prev, z[0:8]], axis=0), cw)
    return jnp.concatenate([edge[8:16], _taps(z, cw)[8:]], axis=0)


def _rows_after(z, nxt):
    n = z.shape[0]
    edge = jnp.concatenate([z[n - 8:n], nxt], axis=0)
    return tuple(jnp.concatenate([pltpu.roll(z, n - k, 0)[:n - 8], pltpu.roll(edge, 16 - k, 0)[0:8]], axis=0)
                 for k in (1, 2))


def _inproj_fwd(x, g1, w_in, b_in, tm, comm=None):
    s = x.shape[0]

    def body(x_ref, g_ref, w_ref, b_ref, xn_ref, qkv_ref, c3_ref, gt_ref):
        xf = x_ref[...]
        xn = (xf * _rstd(xf) * g_ref[...]).astype(BF16)
        xn_ref[...] = xn

        def seg(a, b):
            return lax.dot_general(xn, w_ref[a:b, :], NT, preferred_element_type=F32) + b_ref[:, a:b]

        qkv_ref[...] = seg(0, QKV_W).astype(BF16)
        c3_ref[...] = seg(QKV_W, QKV_W + C3_W).astype(BF16)
        gt_ref[...] = seg(QKV_W + C3_W, IN_W).astype(BF16)

    row = lambda w: pl.BlockSpec((tm, w), lambda i: (i, 0))
    return _call(
        comm, body, name="inproj_fwd", grid=(s // tm,),
        in_specs=[row(D_MODEL), _resident((1, D_MODEL)), _resident((IN_W, D_MODEL)), _resident((1, IN_W))],
        out_specs=[row(D_MODEL), row(QKV_W), row(C3_W), row(GATES_W)],
        out_shape=[jax.ShapeDtypeStruct((s, D_MODEL), BF16), jax.ShapeDtypeStruct((s, QKV_W), BF16),
                   jax.ShapeDtypeStruct((s, C3_W), BF16), jax.ShapeDtypeStruct((s, GATES_W), BF16)],
        compiler_params=_params("parallel"),
    )(x, g1, w_in, b_in)


def _attn_bias():
    qi = (jnp.arange(GROUP * BLOCK) % BLOCK)[:, None]
    kj = jnp.arange(2 * BLOCK)[None, :]
    band = (kj > qi) & (kj <= qi + BLOCK)
    return jnp.stack([jnp.where(band & (kj >= BLOCK), 0.0, NEG), jnp.where(band, 0.0, NEG)]).astype(F32)


def _attn_bias_spec():
    return pl.BlockSpec((None, GROUP * BLOCK, 2 * BLOCK), lambda i: (jnp.minimum(i, 1), 0, 0))


def _sink_column(sk_ref, h):
    rows = lax.broadcasted_iota(jnp.int32, (GROUP * BLOCK, 1), 0)
    col = jnp.full((GROUP * BLOCK, 1), sk_ref[h * GROUP], F32)
    for g in range(1, GROUP):
        col = jnp.where(rows >= g * BLOCK, sk_ref[h * GROUP + g], col)
    return col


def _stack_heads(t, h):
    return jnp.concatenate(
        [t[:, (h * GROUP + g) * HEAD_DIM:(h * GROUP + g + 1) * HEAD_DIM] for g in range(GROUP)], axis=0)


def _unstack_heads(per_kv):
    return jnp.concatenate(
        [t[g * BLOCK:(g + 1) * BLOCK] for t in per_kv for g in range(GROUP)], axis=1)


def _attn_specs(nb):
    cur = lambda i: jnp.minimum(i, nb - 1)
    prev = lambda i: jnp.maximum(jnp.minimum(i, nb - 1) - 1, 0)
    q = pl.BlockSpec((BLOCK, ATTN_W), lambda i: (cur(i), 0))
    kp = pl.BlockSpec((BLOCK, KV_W), lambda i: (prev(i), ATTN_W // KV_W))
    kc = pl.BlockSpec((BLOCK, KV_W), lambda i: (cur(i), ATTN_W // KV_W))
    vp = pl.BlockSpec((BLOCK, KV_W), lambda i: (prev(i), ATTN_W // KV_W + 1))
    vc = pl.BlockSpec((BLOCK, KV_W), lambda i: (cur(i), ATTN_W // KV_W + 1))
    return q, kp, kc, vp, vc


def _attn_fwd(qkv, sinks, comm=None):
    s = qkv.shape[0]
    nb = s // BLOCK

    def body(sk_ref, bias_ref, q_ref, kp_ref, kc_ref, vp_ref, vc_ref, o_ref):
        bias = bias_ref[...]
        q, kp, kc, vp, vc = q_ref[...], kp_ref[...], kc_ref[...], vp_ref[...], vc_ref[...]
        outs = []
        for h in range(N_KV_HEADS):
            hs = slice(h * HEAD_DIM, (h + 1) * HEAD_DIM)
            k2 = jnp.concatenate([kp[:, hs], kc[:, hs]], axis=0)
            v2 = jnp.concatenate([vp[:, hs], vc[:, hs]], axis=0)
            sc = lax.dot_general(_stack_heads(q, h), k2, NT, preferred_element_type=F32) * ATTN_SCALE + bias
            sink = _sink_column(sk_ref, h)
            m = jnp.maximum(jnp.max(sc, axis=1, keepdims=True), sink)
            p = jnp.exp(sc - m)
            den = jnp.sum(p, axis=1, keepdims=True) + jnp.exp(sink - m)
            outs.append(jnp.dot(p.astype(BF16), v2, preferred_element_type=F32) / den)
        o_ref[...] = _unstack_heads(outs).astype(BF16)

    return _call(
        comm, body, name="attn_fwd", grid=(nb,),
        in_specs=[pl.BlockSpec(memory_space=pltpu.SMEM), _attn_bias_spec(), *_attn_specs(nb)],
        out_specs=pl.BlockSpec((BLOCK, ATTN_W), lambda i: (i, 0)),
        out_shape=jax.ShapeDtypeStruct((s, ATTN_W), BF16),
        compiler_params=_params("parallel"),
    )(sinks, _attn_bias(), qkv, qkv, qkv, qkv, qkv)


def _mix_fwd(x, attn, c3, gates, conv_w, w_ab, w_cb, w_out, g2, tm, comm=None):
    s = x.shape[0]

    def body(x_ref, at_ref, c3_ref, gt_ref, cw_ref, wab_ref, wcb_ref, wo_ref, g_ref,
             conv_ref, a_ref, cv_ref, mg_ref, h1_ref, hn_ref, carry_ref):
        @pl.when(pl.program_id(0) == 0)
        def _():
            carry_ref[...] = jnp.zeros_like(carry_ref)

        c3v = c3_ref[...].astype(F32)
        cb, cc, cx = c3v[:, :CONV_W], c3v[:, CONV_W:2 * CONV_W], c3v[:, 2 * CONV_W:]
        z = cc * cx
        cz = _causal_conv(z, carry_ref[...], cw_ref[...])
        carry_ref[...] = z[tm - 8:tm]
        conv = (cb * cz).astype(BF16)
        conv_ref[...] = conv
        a = jnp.dot(at_ref[...], wab_ref[...], preferred_element_type=F32)
        cv = jnp.dot(conv, wcb_ref[...], preferred_element_type=F32)
        a_ref[...] = a.astype(BF16)
        cv_ref[...] = cv.astype(BF16)
        gt = gt_ref[...].astype(F32)
        merged = (_sigmoid(gt[:, :D_MODEL]) * a + _sigmoid(gt[:, D_MODEL:]) * cv).astype(BF16)
        mg_ref[...] = merged
        h1 = x_ref[...] + jnp.dot(merged, wo_ref[...], preferred_element_type=F32)
        h1_ref[...] = h1
        hn_ref[...] = (h1 * _rstd(h1) * g_ref[...]).astype(BF16)

    row = lambda w: pl.BlockSpec((tm, w), lambda i: (i, 0))
    return _call(
        comm, body, name="mix_fwd", grid=(s // tm,),
        in_specs=[row(D_MODEL), row(ATTN_W), row(C3_W), row(GATES_W), _resident((3, CONV_W)),
                  _resident((ATTN_W, D_MODEL)), _resident((CONV_W, D_MODEL)), _resident((D_MODEL, D_MODEL)),
                  _resident((1, D_MODEL))],
        out_specs=[row(CONV_W), row(D_MODEL), row(D_MODEL), row(D_MODEL), row(D_MODEL), row(D_MODEL)],
        out_shape=[jax.ShapeDtypeStruct((s, CONV_W), BF16), jax.ShapeDtypeStruct((s, D_MODEL), BF16),
                   jax.ShapeDtypeStruct((s, D_MODEL), BF16), jax.ShapeDtypeStruct((s, D_MODEL), BF16),
                   jax.ShapeDtypeStruct((s, D_MODEL), F32), jax.ShapeDtypeStruct((s, D_MODEL), BF16)],
        scratch_shapes=[pltpu.VMEM((8, CONV_W), F32)],
        compiler_params=_params("arbitrary"),
    )(x, attn, c3, gates, conv_w, w_ab, w_cb, w_out, g2)


def _ffn_fwd_loss(hn, h1, w_up, ffn_cw, w_down, g3, target, tm):
    s = hn.shape[0]

    def body(hn_ref, h1_ref, wu_ref, cw_ref, wd_ref, g_ref, t_ref,
             u_ref, up_ref, act_ref, dh2_ref, loss_ref, gfn_ref, carry_ref):
        @pl.when(pl.program_id(0) == 0)
        def _():
            carry_ref[...] = jnp.zeros_like(carry_ref)
            loss_ref[...] = jnp.zeros_like(loss_ref)
            gfn_ref[...] = jnp.zeros_like(gfn_ref)

        u = jnp.dot(hn_ref[...], wu_ref[...], preferred_element_type=F32)
        u_ref[...] = u.astype(BF16)
        up = _causal_conv(u, carry_ref[...], cw_ref[...])
        up_ref[...] = up
        carry_ref[...] = u[tm - 8:tm]
        gate, val = up[:, :D_FF], up[:, D_FF:]
        act = (gate * _sigmoid(gate) * val).astype(BF16)
        act_ref[...] = act
        h2 = h1_ref[...] + jnp.dot(act, wd_ref[...], preferred_element_type=F32)
        rstd = _rstd(h2)
        g = g_ref[...]
        err = h2 * rstd * g - t_ref[...]
        loss_ref[...] += jnp.sum(err * err) * (0.5 / D_MODEL)
        dh2, dg = _rms_bwd(err * (1.0 / D_MODEL), h2, rstd, g)
        dh2_ref[...] = dh2
        gfn_ref[...] += jnp.sum(dg, axis=0, keepdims=True)

    row = lambda w: pl.BlockSpec((tm, w), lambda i: (i, 0))
    acc = lambda w: pl.BlockSpec((1, w), lambda i: (0, 0))
    return pl.pallas_call(
        body, name="ffn_fwd_loss", grid=(s // tm,),
        in_specs=[row(D_MODEL), row(D_MODEL), _resident((D_MODEL, FF2)), _resident((3, FF2)),
                  _resident((D_FF, D_MODEL)), _resident((1, D_MODEL)), row(D_MODEL)],
        out_specs=[row(FF2), row(FF2), row(D_FF), row(D_MODEL), acc(128), acc(D_MODEL)],
        out_shape=[jax.ShapeDtypeStruct((s, FF2), BF16), jax.ShapeDtypeStruct((s, FF2), F32),
                   jax.ShapeDtypeStruct((s, D_FF), BF16),
                   jax.ShapeDtypeStruct((s, D_MODEL), F32), jax.ShapeDtypeStruct((1, 128), F32),
                   jax.ShapeDtypeStruct((1, D_MODEL), F32)],
        scratch_shapes=[pltpu.VMEM((8, FF2), F32)],
        compiler_params=_params("arbitrary"),
    )(hn, h1, w_up, ffn_cw, w_down, g3, target)


def _ffn_bwd(dh2, u, up, h1, w_up, ffn_cw, w_down, g2, tm):
    s = dh2.shape[0]
    nt = s // tm

    def body(dh2_ref, u_ref, up_ref, h1_ref, wu_ref, cw_ref, wd_ref, g_ref,
             du_ref, dh1_ref, gcw_ref, gg_ref, carry_ref):
        @pl.when(pl.program_id(0) == 0)
        def _():
            carry_ref[...] = jnp.zeros_like(carry_ref)
            gcw_ref[...] = jnp.zeros_like(gcw_ref)
            gg_ref[...] = jnp.zeros_like(gg_ref)

        dh2v = dh2_ref[...]
        dact = lax.dot_general(dh2v.astype(BF16), wd_ref[...], NT, preferred_element_type=F32)
        upv = up_ref[...]
        gate, val = upv[:, :D_FF], upv[:, D_FF:]
        sg = _sigmoid(gate)
        dval = dact * (gate * sg)
        dgate = dact * val * (sg * (1.0 + gate * (1.0 - sg)))
        dup = jnp.concatenate([dgate, dval], axis=1)
        dup1, dup2 = _rows_after(dup, carry_ref[...])
        carry_ref[...] = dup[0:8]
        u = u_ref[...].astype(F32)
        gcw_ref[2:3, :] += jnp.sum(dup * u, axis=0, keepdims=True)
        gcw_ref[1:2, :] += jnp.sum(dup1 * u, axis=0, keepdims=True)
        gcw_ref[0:1, :] += jnp.sum(dup2 * u, axis=0, keepdims=True)
        cw = cw_ref[...]
        du = (cw[2:3] * dup + cw[1:2] * dup1 + cw[0:1] * dup2).astype(BF16)
        du_ref[...] = du
        dhn = lax.dot_general(du, wu_ref[...], NT, preferred_element_type=F32)
        h1v = h1_ref[...]
        dh1, dg = _rms_bwd(dhn, h1v, _rstd(h1v), g_ref[...])
        dh1_ref[...] = dh2v + dh1
        gg_ref[...] += jnp.sum(dg, axis=0, keepdims=True)

    row = lambda w: pl.BlockSpec((tm, w), lambda i: (nt - 1 - i, 0))
    return pl.pallas_call(
        body, name="ffn_bwd", grid=(nt,),
        in_specs=[row(D_MODEL), row(FF2), row(FF2),
                  row(D_MODEL), _resident((D_MODEL, FF2)), _resident((3, FF2)), _resident((D_FF, D_MODEL)),
                  _resident((1, D_MODEL))],
        out_specs=[row(FF2), row(D_MODEL), pl.BlockSpec((3, FF2), lambda i: (0, 0)),
                   pl.BlockSpec((1, D_MODEL), lambda i: (0, 0))],
        out_shape=[jax.ShapeDtypeStruct((s, FF2), BF16), jax.ShapeDtypeStruct((s, D_MODEL), F32),
                   jax.ShapeDtypeStruct((3, FF2), F32), jax.ShapeDtypeStruct((1, D_MODEL), F32)],
        scratch_shapes=[pltpu.VMEM((8, FF2), F32)],
        compiler_params=_params("arbitrary"),
    )(dh2, u, up, h1, w_up, ffn_cw, w_down, g2)


def _mix_bwd(dh1, gates, a, cv, c3, attn, conv, merged, conv_w, w_ab, w_cb, w_out, tm, comm=None):
    s = dh1.shape[0]
    nt = s // tm
    halo = 16

    def body(dh1_ref, gt_ref, a_ref, cv_ref, c3_ref, ch_ref, at_ref, cn_ref, mg_ref, cw_ref, wab_ref, wcb_ref,
             wo_ref, dat_ref, dc3_ref, dgt_ref, gcw_ref, gab_ref, gcb_ref, gout_ref,
             carry_ref, ab_acc, cb_acc, out_acc):
        i = pl.program_id(0)

        @pl.when(i == 0)
        def _():
            for ref in (carry_ref, gcw_ref, ab_acc, cb_acc, out_acc):
                ref[...] = jnp.zeros_like(ref)

        dh1v = dh1_ref[...].astype(BF16)
        out_acc[...] += lax.dot_general(mg_ref[...], dh1v, TN, preferred_element_type=F32)
        dm = lax.dot_general(dh1v, wo_ref[...], NT, preferred_element_type=F32)
        gt = gt_ref[...].astype(F32)
        sa, sc = _sigmoid(gt[:, :D_MODEL]), _sigmoid(gt[:, D_MODEL:])
        da = (dm * sa).astype(BF16)
        dcv = (dm * sc).astype(BF16)
        ab_acc[...] += lax.dot_general(at_ref[...], da, TN, preferred_element_type=F32)
        cb_acc[...] += lax.dot_general(cn_ref[...], dcv, TN, preferred_element_type=F32)
        dgt_ref[...] = jnp.concatenate(
            [dm * a_ref[...].astype(F32) * (sa * (1.0 - sa)), dm * cv_ref[...].astype(F32) * (sc * (1.0 - sc))],
            axis=1).astype(BF16)
        dat_ref[...] = lax.dot_general(da, wab_ref[...], NT, preferred_element_type=F32).astype(BF16)
        dconv = lax.dot_general(dcv, wcb_ref[...], NT, preferred_element_type=F32)
        c3v = c3_ref[...].astype(F32)
        cb, cc, cx = c3v[:, :CONV_W], c3v[:, CONV_W:2 * CONV_W], c3v[:, 2 * CONV_W:]
        z = cc * cx
        chv = ch_ref[...].astype(F32)[halo - 8:halo] * (i < nt - 1).astype(F32)
        zh = chv[:, CONV_W:2 * CONV_W] * chv[:, 2 * CONV_W:]
        cw = cw_ref[...]
        cz = _causal_conv(z, zh, cw)
        dcz = dconv * cb
        dcz1, dcz2 = _rows_after(dcz, carry_ref[...])
        carry_ref[...] = dcz[0:8]
        gcw_ref[2:3, :] += jnp.sum(dcz * z, axis=0, keepdims=True)
        gcw_ref[1:2, :] += jnp.sum(dcz1 * z, axis=0, keepdims=True)
        gcw_ref[0:1, :] += jnp.sum(dcz2 * z, axis=0, keepdims=True)
        dz = cw[2:3] * dcz + cw[1:2] * dcz1 + cw[0:1] * dcz2
        dc3_ref[...] = jnp.concatenate([dconv * cz, dz * cx, dz * cc], axis=1).astype(BF16)

        @pl.when(i == nt - 1)
        def _():
            gab_ref[...] = ab_acc[...].astype(BF16)
            gcb_ref[...] = cb_acc[...].astype(BF16)
            gout_ref[...] = out_acc[...].astype(BF16)

    row = lambda w: pl.BlockSpec((tm, w), lambda i: (nt - 1 - i, 0))
    return _call(
        comm, body, name="mix_bwd", grid=(nt,),
        in_specs=[row(D_MODEL), row(GATES_W), row(D_MODEL), row(D_MODEL), row(C3_W),
                  pl.BlockSpec((halo, C3_W), lambda i: (jnp.maximum((nt - 1 - i) * (tm // halo) - 1, 0), 0)),
                  row(ATTN_W), row(CONV_W), row(D_MODEL), _resident((3, CONV_W)), _resident((ATTN_W, D_MODEL)),
                  _resident((CONV_W, D_MODEL)), _resident((D_MODEL, D_MODEL))],
        out_specs=[row(ATTN_W), row(C3_W), row(GATES_W), pl.BlockSpec((3, CONV_W), lambda i: (0, 0)),
                   _resident((ATTN_W, D_MODEL)), _resident((CONV_W, D_MODEL)), _resident((D_MODEL, D_MODEL))],
        out_shape=[jax.ShapeDtypeStruct((s, ATTN_W), BF16), jax.ShapeDtypeStruct((s, C3_W), BF16),
                   jax.ShapeDtypeStruct((s, GATES_W), BF16), jax.ShapeDtypeStruct((3, CONV_W), F32),
                   jax.ShapeDtypeStruct((ATTN_W, D_MODEL), BF16), jax.ShapeDtypeStruct((CONV_W, D_MODEL), BF16),
                   jax.ShapeDtypeStruct((D_MODEL, D_MODEL), BF16)],
        scratch_shapes=[pltpu.VMEM((8, CONV_W), F32), pltpu.VMEM((ATTN_W, D_MODEL), F32),
                        pltpu.VMEM((CONV_W, D_MODEL), F32), pltpu.VMEM((D_MODEL, D_MODEL), F32)],
        compiler_params=_params("arbitrary"),
    )(dh1, gates, a, cv, c3, c3, attn, conv, merged, conv_w, w_ab, w_cb, w_out)


def _attn_bwd(qkv, sinks, o, do, comm=None):
    s = qkv.shape[0]
    nb = s // BLOCK

    def body(sk_ref, bias_ref, q_ref, kp_ref, kc_ref, vp_ref, vc_ref, o_ref, do_ref,
             dq_ref, dk_ref, dv_ref, dsk_ref, ck_ref, cvv_ref):
        i = pl.program_id(0)

        @pl.when(i == 0)
        def _():
            ck_ref[...] = jnp.zeros_like(ck_ref)
            cvv_ref[...] = jnp.zeros_like(cvv_ref)
            dsk_ref[...] = jnp.zeros_like(dsk_ref)

        @pl.when(i < nb)
        def _():
            bias = bias_ref[...]
            q, kp, kc, vp, vc = q_ref[...], kp_ref[...], kc_ref[...], vp_ref[...], vc_ref[...]
            ov, dov = o_ref[...], do_ref[...]
            dqs, dks, dvs = [], [], []
            for h in range(N_KV_HEADS):
                hs = slice(h * HEAD_DIM, (h + 1) * HEAD_DIM)
                k2 = jnp.concatenate([kp[:, hs], kc[:, hs]], axis=0)
                v2 = jnp.concatenate([vp[:, hs], vc[:, hs]], axis=0)
                qg, og, dog = _stack_heads(q, h), _stack_heads(ov, h), _stack_heads(dov, h)
                sc = lax.dot_general(qg, k2, NT, preferred_element_type=F32) * ATTN_SCALE + bias
                sink = _sink_column(sk_ref, h)
                m = jnp.maximum(jnp.max(sc, axis=1, keepdims=True), sink)
                p = jnp.exp(sc - m)
                psink = jnp.exp(sink - m)
                inv = 1.0 / (jnp.sum(p, axis=1, keepdims=True) + psink)
                p = p * inv
                delta = jnp.sum(dog.astype(F32) * og.astype(F32), axis=1, keepdims=True)
                dp = lax.dot_general(dog, v2, NT, preferred_element_type=F32)
                ds = (p * (dp - delta)).astype(BF16)
                dqs.append(jnp.dot(ds, k2, preferred_element_type=F32) * ATTN_SCALE)
                dks.append(lax.dot_general(ds, qg, TN, preferred_element_type=F32) * ATTN_SCALE)
                dvs.append(lax.dot_general(p.astype(BF16), dog, TN, preferred_element_type=F32))
                dsink = -(psink * inv * delta)
                for g in range(GROUP):
                    r = h * GROUP + g
                    dsk_ref[r:r + 1, :] += jnp.sum(dsink[g * BLOCK:(g + 1) * BLOCK])
            dq_ref[...] = _unstack_heads(dqs).astype(BF16)
            dk2 = jnp.concatenate(dks, axis=1)
            dv2 = jnp.concatenate(dvs, axis=1)
            dk_ref[...] = (ck_ref[...] + dk2[:BLOCK]).astype(BF16)
            dv_ref[...] = (cvv_ref[...] + dv2[:BLOCK]).astype(BF16)
            ck_ref[...] = dk2[BLOCK:]
            cvv_ref[...] = dv2[BLOCK:]

        @pl.when(i == nb)
        def _():
            dk_ref[...] = ck_ref[...].astype(BF16)
            dv_ref[...] = cvv_ref[...].astype(BF16)

    cur = lambda i: jnp.minimum(i, nb - 1)
    done = lambda i: jnp.maximum(i - 1, 0)
    return _call(
        comm, body, name="attn_bwd", grid=(nb + 1,),
        in_specs=[pl.BlockSpec(memory_space=pltpu.SMEM), _attn_bias_spec(), *_attn_specs(nb),
                  pl.BlockSpec((BLOCK, ATTN_W), lambda i: (cur(i), 0)),
                  pl.BlockSpec((BLOCK, ATTN_W), lambda i: (cur(i), 0))],
        out_specs=[pl.BlockSpec((BLOCK, ATTN_W), lambda i: (cur(i), 0)),
                   pl.BlockSpec((BLOCK, KV_W), lambda i: (done(i), 0)),
                   pl.BlockSpec((BLOCK, KV_W), lambda i: (done(i), 0)),
                   pl.BlockSpec((N_HEADS, 128), lambda i: (0, 0))],
        out_shape=[jax.ShapeDtypeStruct((s, ATTN_W), BF16), jax.ShapeDtypeStruct((s, KV_W), BF16),
                   jax.ShapeDtypeStruct((s, KV_W), BF16), jax.ShapeDtypeStruct((N_HEADS, 128), F32)],
        scratch_shapes=[pltpu.VMEM((BLOCK, KV_W), F32), pltpu.VMEM((BLOCK, KV_W), F32)],
        compiler_params=_params("arbitrary"),
    )(sinks, _attn_bias(), qkv, qkv, qkv, qkv, qkv, o, do)


def _inproj_bwd(dq, dk, dv, dc3, dgt, w_in, x, xn, dh1, g1, tm):
    s = x.shape[0]
    nt = s // tm

    def body(dq_ref, dk_ref, dv_ref, dc3_ref, dgt_ref, w_ref, x_ref, xn_ref, dh1_ref, g_ref,
             dx_ref, gw_ref, gb_ref, gg_ref, acc_ref):
        i = pl.program_id(0)

        @pl.when(i == 0)
        def _():
            for ref in (gb_ref, gg_ref, acc_ref):
                ref[...] = jnp.zeros_like(ref)

        dp = jnp.concatenate([dq_ref[...], dk_ref[...], dv_ref[...], dc3_ref[...], dgt_ref[...]], axis=1)
        acc_ref[...] += lax.dot_general(dp, xn_ref[...], TN, preferred_element_type=F32)
        gb_ref[...] += jnp.sum(dp.astype(F32), axis=0, keepdims=True)
        dxn = jnp.dot(dp, w_ref[...], preferred_element_type=F32)
        xf = x_ref[...]
        dx, dg = _rms_bwd(dxn, xf, _rstd(xf), g_ref[...])
        dx_ref[...] = dh1_ref[...] + dx
        gg_ref[...] += jnp.sum(dg, axis=0, keepdims=True)

        @pl.when(i == nt - 1)
        def _():
            gw_ref[...] = acc_ref[...].astype(BF16)

    row = lambda w: pl.BlockSpec((tm, w), lambda i: (i, 0))
    acc = lambda w: pl.BlockSpec((1, w), lambda i: (0, 0))
    return pl.pallas_call(
        body, name="inproj_bwd", grid=(nt,),
        in_specs=[row(ATTN_W), row(KV_W), row(KV_W), row(C3_W), row(GATES_W), _resident((IN_W, D_MODEL)),
                  row(D_MODEL), row(D_MODEL), row(D_MODEL), _resident((1, D_MODEL))],
        out_specs=[row(D_MODEL), _resident((IN_W, D_MODEL)), acc(IN_W), acc(D_MODEL)],
        out_shape=[jax.ShapeDtypeStruct((s, D_MODEL), F32), jax.ShapeDtypeStruct((IN_W, D_MODEL), BF16),
                   jax.ShapeDtypeStruct((1, IN_W), F32), jax.ShapeDtypeStruct((1, D_MODEL), F32)],
        scratch_shapes=[pltpu.VMEM((IN_W, D_MODEL), F32)],
        compiler_params=_params("arbitrary"),
    )(dq, dk, dv, dc3, dgt, w_in, x, xn, dh1, g1)


def _wgrad(a, b, bm, bn, bk, name, comm=None):
    s, m = a.shape
    n = b.shape[1]
    nk = s // bk

    def body(a_ref, b_ref, o_ref, acc_ref):
        k = pl.program_id(2)

        @pl.when(k == 0)
        def _():
            acc_ref[...] = jnp.zeros_like(acc_ref)

        acc_ref[...] += lax.dot_general(a_ref[...].astype(BF16), b_ref[...].astype(BF16), TN,
                                        preferred_element_type=F32)

        @pl.when(k == nk - 1)
        def _():
            o_ref[...] = acc_ref[...].astype(BF16)

    return _call(
        comm, body, name=name, grid=(m // bm, n // bn, nk),
        in_specs=[pl.BlockSpec((bk, bm), lambda i, j, k: (k, i)), pl.BlockSpec((bk, bn), lambda i, j, k: (k, j))],
        out_specs=pl.BlockSpec((bm, bn), lambda i, j, k: (i, j)),
        out_shape=jax.ShapeDtypeStruct((m, n), BF16),
        scratch_shapes=[pltpu.VMEM((bm, bn), F32)],
        compiler_params=_params("parallel", "parallel", "arbitrary"),
    )(a, b)


class _Carry:
    def __init__(self, jobs, reads=None, bufs=None, fresh=None):
        self.jobs, self.reads, self.bufs, self.fresh = jobs, reads or {}, bufs or {}, fresh or {}
        self.out = {}


class _Job:
    def __init__(self, n_sems, plan):
        self.n_sems, self.plan = n_sems, plan


def _plan_all(jobs, hbm, send, recv):
    pos = _position()
    starts, waits, base = [], [], 0
    for job in jobs:
        s, w = job.plan(hbm, pos, send, recv, base)
        starts, waits, base = starts + s, waits + w, base + job.n_sems
    return starts, waits


def _call(comm, body, **kw):
    if comm is None:
        return pl.pallas_call(body, **kw)
    grid = kw["grid"]
    single = not isinstance(kw["out_shape"], (list, tuple))
    out_shape = [kw["out_shape"]] if single else list(kw["out_shape"])
    out_specs = [kw["out_specs"]] if single else list(kw["out_specs"])
    in_specs = list(kw["in_specs"])
    scratch = list(kw.get("scratch_shapes", ()))
    r_names, b_names, f_names = list(comm.reads), list(comm.bufs), list(comm.fresh)
    n_args, n_out, n_scr = len(in_specs), len(out_shape), len(scratch)
    n_sems = sum(j.n_sems for j in comm.jobs)

    def wrapped(*refs):
        k = n_args
        hbm = dict(zip(r_names, refs[k:k + len(r_names)]))
        k += len(r_names) + len(b_names)
        outs = refs[k:k + n_out]
        k += n_out
        hbm.update(zip(b_names + f_names, refs[k:k + len(b_names) + len(f_names)]))
        k += len(b_names) + len(f_names)
        send, recv = refs[k + n_scr:]
        starts, waits = _plan_all(comm.jobs, hbm, send, recv)
        ids = [pl.program_id(a) for a in range(len(grid))]
        first = functools.reduce(jnp.logical_and, [i == 0 for i in ids])
        last = functools.reduce(jnp.logical_and, [i == g - 1 for i, g in zip(ids, grid)])

        @pl.when(first)
        def _():
            for cp in starts:
                cp.start()

        body(*refs[:n_args], *outs, *refs[k:k + n_scr])

        @pl.when(last)
        def _():
            for cp in waits:
                cp.wait_recv()
            for cp in starts:
                cp.wait_send()

    sems = pltpu.SemaphoreType.DMA((n_sems,))
    held = [jax.ShapeDtypeStruct(a.shape, a.dtype) for a in comm.bufs.values()] + list(comm.fresh.values())
    call = pl.pallas_call(
        wrapped, name=kw["name"], grid=grid,
        in_specs=in_specs + [_ANY] * (len(r_names) + len(b_names)),
        out_specs=out_specs + [_ANY] * len(held),
        out_shape=out_shape + held,
        input_output_aliases={n_args + len(r_names) + i: n_out + i for i in range(len(b_names))},
        scratch_shapes=scratch + [sems, sems],
        compiler_params=_params(*["arbitrary"] * len(grid)),
    )

    def run(*args):
        res = call(*args, *comm.reads.values(), *comm.bufs.values())
        comm.out = dict(zip(b_names + f_names, res[n_out:]))
        return res[0] if single else res[:n_out]

    return run


def _exchange(name, phases, reads=None, bufs=None, fresh=None):
    comm = _Carry([j for ph in phases for j in ph], reads, bufs, fresh)
    r_names, b_names, f_names = list(comm.reads), list(comm.bufs), list(comm.fresh)
    n_sems = sum(j.n_sems for j in comm.jobs)

    def body(*refs):
        hbm = dict(zip(r_names, refs[:len(r_names)]))
        k = len(r_names) + len(b_names)
        hbm.update(zip(b_names + f_names, refs[k:k + len(b_names) + len(f_names)]))
        send, recv = refs[-2:]
        pos = _position()
        started, base = [], 0
        for ph in phases:
            waits = []
            for job in ph:
                s, w = job.plan(hbm, pos, send, recv, base)
                base += job.n_sems
                for cp in s:
                    cp.start()
                started, waits = started + s, waits + w
            for cp in waits:
                cp.wait_recv()
        for cp in started:
            cp.wait_send()

    sems = pltpu.SemaphoreType.DMA((n_sems,))
    held = [jax.ShapeDtypeStruct(a.shape, a.dtype) for a in comm.bufs.values()] + list(comm.fresh.values())
    res = pl.pallas_call(
        body, name=name, in_specs=[_ANY] * (len(r_names) + len(b_names)), out_specs=[_ANY] * len(held),
        out_shape=held, input_output_aliases={len(r_names) + i: i for i in range(len(b_names))},
        scratch_shapes=[sems, sems],
    )(*comm.reads.values(), *comm.bufs.values())
    return dict(zip(b_names + f_names, res))


_HBM = pl.BlockSpec(memory_space=pltpu.HBM)
_SEM = pl.BlockSpec(memory_space=pltpu.SEMAPHORE)
_EFFECT = pltpu.SideEffectType.DATAFLOW_SIDE_EFFECTING


def _start_exchanges(name, groups):
    names = [list(arrays) for _, arrays in groups]
    first = [sum(len(ns) for ns in names[:g]) for g in range(len(groups))]
    n, ng = sum(len(ns) for ns in names), len(groups)

    def body(*refs):
        for g, (jobs, _) in enumerate(groups):
            hbm = dict(zip(names[g], refs[first[g]:first[g] + len(names[g])]))
            for cp in _plan_all(jobs, hbm, refs[n + 2 * g], refs[n + 2 * g + 1])[0]:
                cp.start()
        refs[-1][...] = jnp.zeros_like(refs[-1])

    given = [pltpu.with_memory_space_constraint(
        a if isinstance(a, jax.Array) else lax.empty(a.shape, a.dtype), pltpu.HBM)
        for _, arrays in groups for a in arrays.values()]
    sems = [pltpu.SemaphoreType.DMA((sum(j.n_sems for j in jobs),)) for jobs, _ in groups for _ in range(2)]
    res = pl.pallas_call(
        body, name=name,
        out_shape=(*sems, *[pltpu.HBM(a.shape, a.dtype) for a in given], jax.ShapeDtypeStruct((8, 128), F32)),
        in_specs=[_HBM] * n, out_specs=(*[_SEM] * (2 * ng), *[_HBM] * n, pl.BlockSpec(memory_space=pltpu.VMEM)),
        input_output_aliases={i: 2 * ng + i for i in range(n)},
        compiler_params=pltpu.CompilerParams(has_side_effects=_EFFECT),
    )(*given)
    held = res[2 * ng:2 * ng + n]
    states = [(names[g], groups[g][0], res[2 * g], res[2 * g + 1], held[first[g]:first[g] + len(names[g])])
              for g in range(ng)]
    return states, res[-1]


def _start_exchange(name, jobs, arrays):
    states, token = _start_exchanges(name, [(jobs, arrays)])
    return states[0], token


def _finish_exchange(name, state, after):
    names, jobs, send_sem, recv_sem, held = state
    n = len(names)

    def body(*refs):
        hbm = dict(zip(names, refs[:n]))
        send, recv = refs[n:n + 2]
        starts, waits = _plan_all(jobs, hbm, send, recv)
        for cp in waits:
            cp.wait_recv()
        for cp in starts:
            cp.wait_send()

    res = pl.pallas_call(
        body, name=name, out_shape=tuple(pltpu.HBM(a.shape, a.dtype) for a in held),
        in_specs=[_HBM] * n + [_SEM, _SEM, _ANY], out_specs=tuple([_HBM] * n),
        input_output_aliases={i: i for i in range(n)},
        compiler_params=pltpu.CompilerParams(has_side_effects=_EFFECT),
    )(*held, send_sem, recv_sem, after)
    return dict(zip(names, res))


def _row_tile(rows, bytes_per_row):
    best = 16
    for t in range(16, rows + 1, 16):
        if rows % t == 0 and t * bytes_per_row <= 9 * 1024 * 1024:
            best = t
    return best


def _rowwise(fn, ins, out_dtypes, name, after=None):
    rows, cols = ins[0].shape
    per_row = sum(cols * a.dtype.itemsize for a in ins) + sum(cols * jnp.dtype(d).itemsize for d in out_dtypes)
    tr = _row_tile(rows, per_row)
    n_in = len(ins)

    def body(*refs):
        outs = fn(*[r[...] for r in refs[:n_in]])
        for o_ref, o in zip(refs[-len(out_dtypes):], outs):
            o_ref[...] = o.astype(o_ref.dtype)

    tile = pl.BlockSpec((tr, cols), lambda i: (i, 0))
    behind = [] if after is None else [after]
    return pl.pallas_call(
        body, name=name, grid=(rows // tr,),
        in_specs=[tile] * n_in + [pl.BlockSpec((8, 128), lambda i: (0, 0))] * len(behind),
        out_specs=[tile] * len(out_dtypes),
        out_shape=[jax.ShapeDtypeStruct((rows, cols), d) for d in out_dtypes],
        compiler_params=_params("parallel"),
    )(*ins, *behind)


def _tiled(fn, name, grid, pos, ins, outs):
    n_in = len(ins)

    def body(pos_ref, *refs):
        res = fn(*[r[...] for r in refs[:n_in]])
        for o_ref, o in zip(refs[n_in:], res):
            o_ref[...] = o.astype(o_ref.dtype)

    return pl.pallas_call(
        body, name=name,
        grid_spec=pltpu.PrefetchScalarGridSpec(
            num_scalar_prefetch=1, grid=grid,
            in_specs=[pl.BlockSpec(bs, im) for _, bs, im in ins],
            out_specs=[pl.BlockSpec(bs, im) for _, _, bs, im in outs]),
        out_shape=[jax.ShapeDtypeStruct(s, d) for s, d, _, _ in outs],
        compiler_params=_params("parallel"),
    )(pos, *[a for a, _, _ in ins])


def _adamw(w, g, m, v):
    m = ADAM_B1 * m + (1.0 - ADAM_B1) * g
    v = ADAM_B2 * v + (1.0 - ADAM_B2) * (g * g)
    m_hat = m / (1.0 - ADAM_B1 ** ADAM_STEP)
    v_hat = v / (1.0 - ADAM_B2 ** ADAM_STEP)
    return -ADAM_LR * (m_hat / (jnp.sqrt(v_hat) + ADAM_EPS) + ADAM_WD * w), m, v


def _adamw_small(params):
    n = len(params)

    def body(*refs):
        for k in range(n):
            w, g, m, v = (r[...] for r in refs[4 * k:4 * k + 4])
            for o_ref, o in zip(refs[4 * n + 3 * k:4 * n + 3 * k + 3], _adamw(w, g, m, v)):
                o_ref[...] = o

    flat = [a for p in params for a in p]
    return pl.pallas_call(
        body, name="adamw_small",
        out_shape=[jax.ShapeDtypeStruct(p[0].shape, F32) for p in params for _ in range(3)],
    )(*flat)


class _Layout:
    def __init__(self, rows, cols, stacked):
        self.rows, self.cols, self.stacked = rows, cols, stacked

    def whole(self, rows=None):
        r = self.rows if rows is None else rows
        return (N_CHIPS, r, self.cols) if self.stacked else (r, N_CHIPS * self.cols)

    def part_rows(self, h, q=0, nq=1):
        n = self.rows // 2 // nq
        return pl.ds(pl.multiple_of(h * (self.rows // 2) + q * n, 16), n)

    def half_rows(self, h):
        return self.part_rows(h)

    def block(self, ref, p, rows=slice(None)):
        if self.stacked:
            return ref.at[p, rows, :]
        return ref.at[rows, pl.ds(pl.multiple_of(p * self.cols, 128), self.cols)]

    def all_chips(self, ref, rows):
        return ref.at[:, rows, :] if self.stacked else ref.at[rows, :]


BIG = (
    _Layout(IN_SHARD, D_MODEL, True),
    _Layout(ATTN_W, D_MODEL // N_CHIPS, False),
    _Layout(CONV_W, D_MODEL // N_CHIPS, False),
    _Layout(D_MODEL // N_CHIPS, D_MODEL, True),
    _Layout(D_MODEL, FF2 // N_CHIPS, False),
    _Layout(D_FF // N_CHIPS, D_MODEL, True),
)
N_BIG = len(BIG)
_ANY = pl.BlockSpec(memory_space=pl.ANY)


def _position():
    x, y, c = lax.axis_index("x"), lax.axis_index("y"), lax.axis_index("c")
    return x, y, c, 2 * x + y


def _core_of_chip(p, c):
    return (p >> 1, p & 1, c)


def _place_cast(shard, lay, pos, name, after=None):
    rows, cols = shard.shape
    tr = _row_tile(rows, cols * 6)
    if lay.stacked:
        out = (lay.whole(), BF16, (None, tr, cols), lambda i, pos: (pos[0], i, 0))
    else:
        out = (lay.whole(), BF16, (tr, cols), lambda i, pos: (i, pos[0]))
    ins = [(shard, (tr, cols), lambda i, pos: (i, 0))]
    if after is not None:
        ins.append((after, (8, 128), lambda i, pos: (0, 0)))
    return _tiled(lambda a, *_: (a,), name, (rows // tr,), pos, ins, [out])[0]


def _remote(src, dst, send, recv, k, device):
    return pltpu.make_async_remote_copy(src_ref=src, dst_ref=dst, send_sem=send.at[k], recv_sem=recv.at[k],
                                        device_id=device, device_id_type=MESH)


def _arrival(dst, send, recv, k, me):
    return _remote(dst, dst, send, recv, k, me)


def _gather_ici(lay, name, q=0, nq=1):
    def plan(hbm, pos, send, recv, base):
        x, y, c, me = pos
        rows = lay.part_rows(c, q, nq)
        mine = lay.block(hbm[name], me, rows)
        starts = [_remote(mine, mine, send, recv, base + d - 1, _core_of_chip(me ^ d, c)) for d in (1, 2, 3)]
        waits = [_arrival(lay.block(hbm[name], me ^ d, rows), send, recv, base + d - 1, (x, y, c)) for d in (1, 2, 3)]
        return starts, waits
    return _Job(3, plan)


def _gather_d2d(lay, name, q=0, nq=1):
    def plan(hbm, pos, send, recv, base):
        x, y, c, me = pos
        starts, waits = [], []
        for d in (1, 2, 3):
            got = lay.block(hbm[name], me ^ d, lay.part_rows(c, q, nq))
            starts.append(_remote(got, got, send, recv, base + d - 1, (x, y, 1 - c)))
            waits.append(_arrival(lay.block(hbm[name], me ^ d, lay.part_rows(1 - c, q, nq)), send, recv, base + d - 1,
                                  (x, y, c)))
        return starts, waits
    return _Job(3, plan)


def _rs_pair(lay, grad, theirs):
    def plan(hbm, pos, send, recv, base):
        x, y, c, _ = pos
        out = _remote(lay.all_chips(hbm[grad], lay.half_rows(1 - c)), hbm[theirs], send, recv, base, (x, y, 1 - c))
        return [out], [_arrival(hbm[theirs], send, recv, base, (x, y, c))]
    return _Job(1, plan)


def _rs_chips(lay, sums, slots):
    def plan(hbm, pos, send, recv, base):
        x, y, c, me = pos
        starts = [_remote(lay.block(hbm[sums], me ^ d), hbm[slots].at[me], send, recv, base + d - 1,
                          _core_of_chip(me ^ d, c)) for d in (1, 2, 3)]
        waits = [_arrival(hbm[slots].at[me ^ d], send, recv, base + d - 1, (x, y, c)) for d in (1, 2, 3)]
        return starts, waits
    return _Job(3, plan)


def _rs_share(lay, shard):
    def plan(hbm, pos, send, recv, base):
        x, y, c, _ = pos
        mine = hbm[shard].at[lay.half_rows(c), :]
        other = hbm[shard].at[lay.half_rows(1 - c), :]
        return [_remote(mine, mine, send, recv, base, (x, y, 1 - c))], [_arrival(other, send, recv, base, (x, y, c))]
    return _Job(1, plan)


def _slots_shape(lay):
    return jax.ShapeDtypeStruct((N_CHIPS, lay.rows // 2, lay.cols), BF16)


def _theirs_shape(lay):
    return jax.ShapeDtypeStruct(lay.whole(lay.rows // 2), BF16)


def _pair_sum(grad, theirs, lay, pos, name):
    half = lay.rows // 2
    add = lambda a, b: (a.astype(F32) + b.astype(F32),)
    if lay.stacked:
        tr = _row_tile(half, lay.cols * 6)
        nt = half // tr
        flat = lambda a: a.reshape(-1, lay.cols)
        mine = lambda t, pos: ((t // nt) * (2 * nt) + pos[1] * nt + t % nt, 0)
        grid, blk = (N_CHIPS * nt,), (tr, lay.cols)
        grad, theirs = flat(grad), flat(theirs)
    else:
        tr = _row_tile(half, N_CHIPS * lay.cols * 6)
        nt = half // tr
        mine = lambda t, pos: (pos[1] * nt + t, 0)
        grid, blk = (nt,), (tr, N_CHIPS * lay.cols)
    same = lambda t, pos: (t, 0)
    out = _tiled(add, name, grid, pos, [(grad, blk, mine), (theirs, blk, same)], [(theirs.shape, BF16, blk, same)])[0]
    return out.reshape(lay.whole(half))


def _chip_sum(sums, slots, lay, pos, name, after=None):
    half = lay.rows // 2
    tr = _row_tile(half, lay.cols * 12)
    nt = half // tr
    blk3 = (None, tr, lay.cols)
    if lay.stacked:
        own = (sums, blk3, lambda i, pos: (pos[0], i, 0))
    else:
        own = (sums, (tr, lay.cols), lambda i, pos: (i, pos[0]))
    others = [(slots, blk3, functools.partial(lambda d, i, pos: (pos[0] ^ d, i, 0), d)) for d in (1, 2, 3)]

    def add(a, b1, b2, b3, *_):
        return (((a.astype(F32) + b1.astype(F32)) + b2.astype(F32)) + b3.astype(F32),)

    if after is not None:
        others.append((after, (8, 128), lambda i, pos: (0, 0)))
    return _tiled(add, name, (nt,), pos, [own] + others,
                  [((lay.rows, lay.cols), F32, (tr, lay.cols), lambda i, pos: (pos[1] * nt + i, 0))])[0]


N_DEV = 8


def _to_all(src, slots):
    def plan(hbm, pos, send, recv, base):
        x, y, c, _ = pos
        idx = 4 * x + 2 * y + c
        starts = [_remote(hbm[src], hbm[slots].at[idx], send, recv, base + k - 1,
                          (x ^ (k >> 2), y ^ ((k >> 1) & 1), c ^ (k & 1))) for k in range(1, N_DEV)]
        waits = [_arrival(hbm[slots].at[idx ^ k], send, recv, base + k - 1, (x, y, c)) for k in range(1, N_DEV)]
        return starts, waits
    return _Job(N_DEV - 1, plan)


def _sum_slots(own, slots, pos):
    def body(pos_ref, own_ref, slots_ref, o_ref):
        idx = 2 * pos_ref[0] + pos_ref[1]
        term = lambda q: jnp.where(idx == q, own_ref[...], slots_ref[q])
        acc = term(0)
        for q in range(1, N_DEV):
            acc = acc + term(q)
        o_ref[...] = acc

    return pl.pallas_call(
        body, name="sum_small", out_shape=jax.ShapeDtypeStruct(own.shape, F32),
        in_specs=[pl.BlockSpec(memory_space=pltpu.SMEM), pl.BlockSpec(memory_space=pltpu.VMEM),
                  pl.BlockSpec(memory_space=pltpu.VMEM)],
    )(pos, own, slots)


def _pack_rows(parts):
    padded = [jnp.pad(a, ((0, -a.shape[0] % 8), (0, 0))) for a in parts]
    starts = [sum(p.shape[0] for p in padded[:k]) for k in range(len(padded))]
    return jnp.concatenate(padded, axis=0), starts


def kernel(x, mix_norm, w_in, b_in, sinks, conv_w, w_attn_branch, w_conv_branch, w_out, ffn_norm, w_up, ffn_conv_w, w_down, final_norm, loss_target, m_mix_norm, m_w_in, m_b_in, m_sinks, m_conv_w, m_w_attn_branch, m_w_conv_branch, m_w_out, m_ffn_norm, m_w_up, m_ffn_conv_w, m_w_down, m_final_norm, v_mix_norm, v_w_in, v_b_in, v_sinks, v_conv_w, v_w_attn_branch, v_w_conv_branch, v_w_out, v_ffn_norm, v_w_up, v_ffn_conv_w, v_w_down, v_final_norm):
    me = 2 * lax.axis_index("x") + lax.axis_index("y")
    big_w = [w_in[0].T, w_attn_branch[0], w_conv_branch[0], w_out[0], w_up[0], w_down[0]]
    big_m = [m_w_in[0].T, m_w_attn_branch[0], m_w_conv_branch[0], m_w_out[0], m_w_up[0], m_w_down[0]]
    big_v = [v_w_in[0].T, v_w_attn_branch[0], v_w_conv_branch[0], v_w_out[0], v_w_up[0], v_w_down[0]]
    names = ("w_in", "w_ab", "w_cb", "w_out", "w_up", "w_down")

    pos = jnp.stack([me, lax.axis_index("c")]).astype(jnp.int32)

    lay = dict(zip(names, BIG))
    xs, target, sk = x[0], loss_target[0], sinks[0]
    s = xs.shape[0]
    tm, tm2, bk, bk2 = min(256, s), min(512, s), min(1024, s), min(2048, s)

    taps, (_, t0) = _pack_rows([conv_w[0], ffn_conv_w[0].reshape(3 * (FF2 // N_CHIPS // 128), 128)])
    placed = {"w_in": _place_cast(big_w[0], lay["w_in"], pos, "cast_w_in")}
    fly_in, started = _start_exchange("gather_in_start", [_gather_ici(lay["w_in"], "w_in")], {"w_in": placed["w_in"]})
    taps_flight, started = _start_exchange("taps_start", [_to_all("v", "slots")],
                                           {"v": taps + started[0:1], "slots": jnp.zeros((N_DEV, *taps.shape), F32)})
    for w, n in zip(big_w[1:], names[1:]):
        placed[n] = _place_cast(w, lay[n], pos, "cast_" + n, after=started)
    trio = ("w_ab", "w_cb", "w_out")
    (fly_trio, fly_up, fly_down), started = _start_exchanges("gather_rest_start", [
        ([_gather_ici(lay[n], n) for n in ws], {n: placed[n] for n in ws}) for ws in (trio, ("w_up",), ("w_down",))])

    got = _finish_exchange("gather_in_wait", fly_in, after=started)
    w_in_full = _exchange("gather_in_d2d", [[_gather_d2d(lay["w_in"], "w_in")]], bufs=got)["w_in"].reshape(IN_W, D_MODEL)
    xn, qkv, c3, gates = _inproj_fwd(xs, mix_norm, w_in_full, b_in, tm2)
    k2 = _Carry([_gather_d2d(lay[n], n) for n in trio], bufs=_finish_exchange("gather_trio_wait", fly_trio, after=qkv))
    attn = _attn_fwd(qkv, sk, comm=k2)
    w_ab, w_cb = k2.out["w_ab"], k2.out["w_cb"]
    w_out_full = k2.out["w_out"].reshape(D_MODEL, D_MODEL)
    k3 = _Carry([_gather_d2d(lay["w_up"], "w_up")], bufs=_finish_exchange("gather_up_wait", fly_up, after=attn))
    taps = _finish_exchange("taps_wait", taps_flight, after=attn)
    taps = lax.dynamic_update_slice(taps["slots"], taps["v"][None], (2 * me + lax.axis_index("c"), 0, 0))
    conv_full = taps[0::2, 0:3].transpose(1, 0, 2).reshape(3, CONV_W)
    ffn_cw_full = taps[0::2, t0:t0 + 33].reshape(N_CHIPS, 3, FF2 // N_CHIPS).transpose(1, 0, 2).reshape(3, FF2)
    conv, a, cv, merged, h1, hn = _mix_fwd(xs, attn, c3, gates, conv_full, w_ab, w_cb, w_out_full, ffn_norm, tm2, comm=k3)
    w_up_full = k3.out["w_up"]
    w_down_full = _exchange("gather_down_d2d", [[_gather_d2d(lay["w_down"], "w_down")]],
                            bufs=_finish_exchange("gather_down_wait", fly_down, after=hn))["w_down"].reshape(D_FF, D_MODEL)
    u, up, act, dh2, loss_part, g_fn = _ffn_fwd_loss(hn, h1, w_up_full, ffn_cw_full, w_down_full,
                                                     final_norm[None, :], target, tm)

    grads, sums, slots = {}, {}, {}

    def pair(*ws):
        return _Carry([_rs_pair(lay[n], "g_" + n, "t_" + n) for n in ws], reads={"g_" + n: grads[n] for n in ws},
                      fresh={"t_" + n: _theirs_shape(lay[n]) for n in ws})

    def chips(*ws, also=None):
        k = _Carry([_rs_chips(lay[n], "s_" + n, "r_" + n) for n in ws], reads={"s_" + n: sums[n] for n in ws},
                   fresh={"r_" + n: _slots_shape(lay[n]) for n in ws})
        if also is not None:
            k = _Carry(k.jobs + also.jobs, {**k.reads, **also.reads}, None, {**k.fresh, **also.fresh})
        return k

    def pair_sums(k, *ws):
        for n in ws:
            sums[n] = _pair_sum(grads[n], k.out["t_" + n], lay[n], pos, "pair_sum_" + n)

    def take_slots(k, *ws):
        for n in ws:
            slots[n] = k.out["r_" + n]

    du, dh1, g_fcw, g_g2 = _ffn_bwd(dh2, u, up, h1, w_up_full, ffn_cw_full, w_down_full, ffn_norm, tm)
    grads["w_down"] = _wgrad(act, dh2, D_FF // 2, D_MODEL, bk2, "wgrad_down").reshape(lay["w_down"].whole())
    k4 = pair("w_down")
    grads["w_up"] = _wgrad(hn, du, D_MODEL, FF2 // 4, bk2, "wgrad_up", comm=k4)
    pair_sums(k4, "w_down")
    k5 = chips("w_down", also=pair("w_up"))
    dattn, dc3, dgt, g_cw, grads["w_ab"], grads["w_cb"], gw_out = _mix_bwd(
        dh1, gates, a, cv, c3, attn, conv, merged, conv_full, w_ab, w_cb, w_out_full, tm, comm=k5)
    grads["w_out"] = gw_out.reshape(lay["w_out"].whole())
    take_slots(k5, "w_down")
    pair_sums(k5, "w_up")
    k6 = chips("w_up", also=pair("w_out", "w_ab", "w_cb"))
    dq, dk, dv, g_sk = _attn_bwd(qkv, sk, attn, dattn, comm=k6)
    take_slots(k6, "w_up")
    pair_sums(k6, "w_out", "w_ab", "w_cb")
    trio_flight, started = _start_exchange(
        "rs_chips_trio_start", [_rs_chips(lay[n], "s_" + n, "r_" + n) for n in trio],
        {**{"s_" + n: sums[n] for n in trio}, **{"r_" + n: _slots_shape(lay[n]) for n in trio}})
    behind = mix_norm + jnp.tile(started[0:1], (1, D_MODEL // 128))
    grad_x, gw_in, g_b, g_g1 = _inproj_bwd(dq, dk, dv, dc3, dgt, w_in_full, xs, xn, dh1, behind, tm)
    grads["w_in"] = gw_in.reshape(lay["w_in"].whole())

    parts = [loss_part, g_g1, g_b, jnp.pad(g_sk[:, 0], (0, 120))[None, :], g_cw, g_g2, g_fcw, g_fn]
    packed, at = _pack_rows([p.reshape(-1, 128) for p in parts])
    small_flight, started = _start_exchange("small_start", [_to_all("v", "slots")],
                                            {"v": packed, "slots": jnp.zeros((N_DEV, *packed.shape), F32)})
    others = names[1:]
    in_flight, started = _start_exchange("rs_pair_in_start", [_rs_pair(lay["w_in"], "g", "t")],
                                         {"g": grads["w_in"], "t": _theirs_shape(lay["w_in"]), "behind": started})
    halves = {n: _chip_sum(sums[n], slots[n], lay[n], pos, "chip_sum_" + n, after=started) for n in ("w_up", "w_down")}
    landed = _finish_exchange("rs_pair_in_wait", in_flight, after=halves["w_down"])
    sums["w_in"] = _pair_sum(landed["g"], landed["t"], lay["w_in"], pos, "pair_sum_w_in")
    in_flight, started = _start_exchange("rs_chips_in_start", [_rs_chips(lay["w_in"], "s", "r")],
                                         {"s": sums["w_in"], "r": _slots_shape(lay["w_in"])})
    landed = _finish_exchange("rs_chips_trio_wait", trio_flight, after=started)
    for n in trio:
        halves[n] = _chip_sum(landed["s_" + n], landed["r_" + n], lay[n], pos, "chip_sum_" + n)
    shared = _exchange("share_halves", [[_rs_share(lay[n], n) for n in others]], bufs=halves)
    w_of, m_of, v_of = dict(zip(names, big_w)), dict(zip(names, big_m)), dict(zip(names, big_v))

    def adam(n, g, after=None):
        return _rowwise(lambda w, g, m, v: (g, *_adamw(w, g, m, v)), [w_of[n], g, m_of[n], v_of[n]], [F32] * 4,
                        "adamw_" + n, after=after)

    new_of, last = {}, None
    for n in ("w_up", "w_down", "w_out", "w_ab", "w_cb"):
        new_of[n] = adam(n, shared[n], last)
        last = new_of[n][1]

    arrived = _finish_exchange("small_wait", small_flight, after=last)
    total = _sum_slots(arrived["v"], arrived["slots"], pos)
    part = lambda k: total[at[k]:at[k] + parts[k].size // 128].reshape(parts[k].shape)
    loss = total[0, 0]
    g_mix, g_b, g_g2, g_fn = part(1), part(2), part(5), part(7)
    g_sk = part(3)[:, 0:N_HEADS]
    g_cw = lax.dynamic_slice(part(4), (0, me * 128), (3, 128))
    g_fcw = lax.dynamic_slice(part(6), (0, me * (FF2 // N_CHIPS)), (3, FF2 // N_CHIPS))
    small_p = [
        (mix_norm, g_mix, m_mix_norm, v_mix_norm), (b_in, g_b, m_b_in, v_b_in), (sinks, g_sk, m_sinks, v_sinks),
        (conv_w[0], g_cw, m_conv_w[0], v_conv_w[0]), (ffn_norm, g_g2, m_ffn_norm, v_ffn_norm),
        (ffn_conv_w[0], g_fcw, m_ffn_conv_w[0], v_ffn_conv_w[0]),
        (final_norm[None, :], g_fn, m_final_norm[None, :], v_final_norm[None, :])]
    small_new = _adamw_small(small_p)
    small_new = [small_new[3 * k:3 * k + 3] for k in range(len(small_p))]

    landed = _finish_exchange("rs_chips_in_wait", in_flight, after=small_new[0][0])
    half_in = _chip_sum(landed["s"], landed["r"], lay["w_in"], pos, "chip_sum_w_in")
    shared["w_in"] = _exchange("share_in", [[_rs_share(lay["w_in"], "w_in")]], bufs={"w_in": half_in})["w_in"]
    new_of["w_in"] = adam("w_in", shared["w_in"])
    big_g = [new_of[n][0] for n in names]
    big_new = [new_of[n][1:] for n in names]

    order = [("s", 0), ("b", 0), ("s", 1), ("s", 2), ("s", 3), ("b", 1), ("b", 2), ("b", 3), ("s", 4), ("b", 4),
             ("s", 5), ("b", 5), ("s", 6)]
    shapes = [mix_norm.shape, w_in.shape, b_in.shape, sinks.shape, conv_w.shape, w_attn_branch.shape,
              w_conv_branch.shape, w_out.shape, ffn_norm.shape, w_up.shape, ffn_conv_w.shape, w_down.shape,
              final_norm.shape]
    small_g = [p[1] for p in small_p]
    big_g[0] = big_g[0].T
    big_new[0] = [a.T for a in big_new[0]]
    out_g = [(small_g[k] if kind == "s" else big_g[k]).reshape(shp) for (kind, k), shp in zip(order, shapes)]
    news = [[(small_new[k][j] if kind == "s" else big_new[k][j]).reshape(shp) for (kind, k), shp in zip(order, shapes)]
            for j in range(3)]
    return (loss, grad_x[None], *out_g, *news[0], *news[1], *news[2])
```

```python
import functools

import jax
import jax.numpy as jnp
from jax import lax
from jax.experimental import pallas as pl
from jax.experimental.pallas import tpu as pltpu

F32 = jnp.float32
BF16 = jnp.bfloat16

D_MODEL = 1024
HEAD_DIM = 64
N_HEADS = 8
N_KV_HEADS = 2
GROUP = N_HEADS // N_KV_HEADS
BLOCK = 128
ATTN_SCALE = HEAD_DIM ** -0.5
ATTN_W = N_HEADS * HEAD_DIM
KV_W = N_KV_HEADS * HEAD_DIM
CONV_W = 512
QKV_W = ATTN_W + 2 * KV_W
C3_W = 3 * CONV_W
GATES_W = 2 * D_MODEL
IN_W = QKV_W + C3_W + GATES_W
D_FF = 2816
FF2 = 2 * D_FF
NORM_EPS = 1e-5
N_CHIPS = 4
IN_SHARD = IN_W // N_CHIPS
NEG = -1e30

ADAM_LR = 0.001
ADAM_B1 = 0.9
ADAM_B2 = 0.999
ADAM_EPS = 1e-08
ADAM_WD = 0.01
ADAM_STEP = 10

VMEM_LIMIT = 56 * 1024 * 1024
MESH = pl.DeviceIdType.MESH

NT = (((1,), (1,)), ((), ()))
TN = (((0,), (0,)), ((), ()))


def _params(*sem):
    return pltpu.CompilerParams(dimension_semantics=sem, vmem_limit_bytes=VMEM_LIMIT)


def _resident(shape):
    return pl.BlockSpec(shape, lambda *_: (0,) * len(shape), pipeline_mode=pl.Buffered(1))


def _sigmoid(v):
    return 0.5 * jnp.tanh(0.5 * v) + 0.5


def _rstd(v):
    return lax.rsqrt(jnp.mean(v * v, axis=-1, keepdims=True) + NORM_EPS)


def _rms_bwd(dy, v, rstd, g):
    vhat = v * rstd
    t = dy * g
    return rstd * (t - vhat * jnp.mean(t * vhat, axis=-1, keepdims=True)), dy * vhat


def _taps(z, cw):
    return cw[2:3] * z + cw[1:2] * pltpu.roll(z, 1, 0) + cw[0:1] * pltpu.roll(z, 2, 0)


def _causal_conv(z, prev, cw):
    edge = _taps(jnp.concatenate([prev, z[0:8]], axis=0), cw)
    return jnp.concatenate([edge[8:16], _taps(z, cw)[8:]], axis=0)


def _rows_after(z, nxt):
    n = z.shape[0]
    edge = jnp.concatenate([z[n - 8:n], nxt], axis=0)
    return tuple(jnp.concatenate([pltpu.roll(z, n - k, 0)[:n - 8], pltpu.roll(edge, 16 - k, 0)[0:8]], axis=0)
                 for k in (1, 2))


def _inproj_fwd(x, g1, w_in, b_in, tm, comm=None):
    s = x.shape[0]

    def body(x_ref, g_ref, w_ref, b_ref, xn_ref, qkv_ref, c3_ref, gt_ref):
        xf = x_ref[...]
        xn = (xf * _rstd(xf) * g_ref[...]).astype(BF16)
        xn_ref[...] = xn

        def seg(a, b):
            return lax.dot_general(xn, w_ref[a:b, :], NT, preferred_element_type=F32) + b_ref[:, a:b]

        qkv_ref[...] = seg(0, QKV_W).astype(BF16)
        c3_ref[...] = seg(QKV_W, QKV_W + C3_W).astype(BF16)
        gt_ref[...] = seg(QKV_W + C3_W, IN_W).astype(BF16)

    row = lambda w: pl.BlockSpec((tm, w), lambda i: (i, 0))
    return _call(
        comm, body, name="inproj_fwd", grid=(s // tm,),
        in_specs=[row(D_MODEL), _resident((1, D_MODEL)), _resident((IN_W, D_MODEL)), _resident((1, IN_W))],
        out_specs=[row(D_MODEL), row(QKV_W), row(C3_W), row(GATES_W)],
        out_shape=[jax.ShapeDtypeStruct((s, D_MODEL), BF16), jax.ShapeDtypeStruct((s, QKV_W), BF16),
                   jax.ShapeDtypeStruct((s, C3_W), BF16), jax.ShapeDtypeStruct((s, GATES_W), BF16)],
        compiler_params=_params("parallel"),
    )(x, g1, w_in, b_in)


def _attn_bias():
    qi = (jnp.arange(GROUP * BLOCK) % BLOCK)[:, None]
    kj = jnp.arange(2 * BLOCK)[None, :]
    band = (kj > qi) & (kj <= qi + BLOCK)
    return jnp.stack([jnp.where(band & (kj >= BLOCK), 0.0, NEG), jnp.where(band, 0.0, NEG)]).astype(F32)


def _attn_bias_spec():
    return pl.BlockSpec((None, GROUP * BLOCK, 2 * BLOCK), lambda i: (jnp.minimum(i, 1), 0, 0))


def _sink_column(sk_ref, h):
    rows = lax.broadcasted_iota(jnp.int32, (GROUP * BLOCK, 1), 0)
    col = jnp.full((GROUP * BLOCK, 1), sk_ref[h * GROUP], F32)
    for g in range(1, GROUP):
        col = jnp.where(rows >= g * BLOCK, sk_ref[h * GROUP + g], col)
    return col


def _stack_heads(t, h):
    return jnp.concatenate(
        [t[:, (h * GROUP + g) * HEAD_DIM:(h * GROUP + g + 1) * HEAD_DIM] for g in range(GROUP)], axis=0)


def _unstack_heads(per_kv):
    return jnp.concatenate(
        [t[g * BLOCK:(g + 1) * BLOCK] for t in per_kv for g in range(GROUP)], axis=1)


def _attn_specs(nb):
    cur = lambda i: jnp.minimum(i, nb - 1)
    prev = lambda i: jnp.maximum(jnp.minimum(i, nb - 1) - 1, 0)
    q = pl.BlockSpec((BLOCK, ATTN_W), lambda i: (cur(i), 0))
    kp = pl.BlockSpec((BLOCK, KV_W), lambda i: (prev(i), ATTN_W // KV_W))
    kc = pl.BlockSpec((BLOCK, KV_W), lambda i: (cur(i), ATTN_W // KV_W))
    vp = pl.BlockSpec((BLOCK, KV_W), lambda i: (prev(i), ATTN_W // KV_W + 1))
    vc = pl.BlockSpec((BLOCK, KV_W), lambda i: (cur(i), ATTN_W // KV_W + 1))
    return q, kp, kc, vp, vc


def _attn_fwd(qkv, sinks, comm=None):
    s = qkv.shape[0]
    nb = s // BLOCK

    def body(sk_ref, bias_ref, q_ref, kp_ref, kc_ref, vp_ref, vc_ref, o_ref):
        bias = bias_ref[...]
        q, kp, kc, vp, vc = q_ref[...], kp_ref[...], kc_ref[...], vp_ref[...], vc_ref[...]
        outs = []
        for h in range(N_KV_HEADS):
            hs = slice(h * HEAD_DIM, (h + 1) * HEAD_DIM)
            k2 = jnp.concatenate([kp[:, hs], kc[:, hs]], axis=0)
            v2 = jnp.concatenate([vp[:, hs], vc[:, hs]], axis=0)
            sc = lax.dot_general(_stack_heads(q, h), k2, NT, preferred_element_type=F32) * ATTN_SCALE + bias
            sink = _sink_column(sk_ref, h)
            m = jnp.maximum(jnp.max(sc, axis=1, keepdims=True), sink)
            p = jnp.exp(sc - m)
            den = jnp.sum(p, axis=1, keepdims=True) + jnp.exp(sink - m)
            outs.append(jnp.dot(p.astype(BF16), v2, preferred_element_type=F32) / den)
        o_ref[...] = _unstack_heads(outs).astype(BF16)

    return _call(
        comm, body, name="attn_fwd", grid=(nb,),
        in_specs=[pl.BlockSpec(memory_space=pltpu.SMEM), _attn_bias_spec(), *_attn_specs(nb)],
        out_specs=pl.BlockSpec((BLOCK, ATTN_W), lambda i: (i, 0)),
        out_shape=jax.ShapeDtypeStruct((s, ATTN_W), BF16),
        compiler_params=_params("parallel"),
    )(sinks, _attn_bias(), qkv, qkv, qkv, qkv, qkv)


def _mix_fwd(x, attn, c3, gates, conv_w, w_ab, w_cb, w_out, g2, tm, comm=None):
    s = x.shape[0]

    def body(x_ref, at_ref, c3_ref, gt_ref, cw_ref, wab_ref, wcb_ref, wo_ref, g_ref,
             conv_ref, a_ref, cv_ref, mg_ref, h1_ref, hn_ref, carry_ref):
        @pl.when(pl.program_id(0) == 0)
        def _():
            carry_ref[...] = jnp.zeros_like(carry_ref)

        c3v = c3_ref[...].astype(F32)
        cb, cc, cx = c3v[:, :CONV_W], c3v[:, CONV_W:2 * CONV_W], c3v[:, 2 * CONV_W:]
        z = cc * cx
        cz = _causal_conv(z, carry_ref[...], cw_ref[...])
        carry_ref[...] = z[tm - 8:tm]
        conv = (cb * cz).astype(BF16)
        conv_ref[...] = conv
        a = jnp.dot(at_ref[...], wab_ref[...], preferred_element_type=F32)
        cv = jnp.dot(conv, wcb_ref[...], preferred_element_type=F32)
        a_ref[...] = a.astype(BF16)
        cv_ref[...] = cv.astype(BF16)
        gt = gt_ref[...].astype(F32)
        merged = (_sigmoid(gt[:, :D_MODEL]) * a + _sigmoid(gt[:, D_MODEL:]) * cv).astype(BF16)
        mg_ref[...] = merged
        h1 = x_ref[...] + jnp.dot(merged, wo_ref[...], preferred_element_type=F32)
        h1_ref[...] = h1
        hn_ref[...] = (h1 * _rstd(h1) * g_ref[...]).astype(BF16)

    row = lambda w: pl.BlockSpec((tm, w), lambda i: (i, 0))
    return _call(
        comm, body, name="mix_fwd", grid=(s // tm,),
        in_specs=[row(D_MODEL), row(ATTN_W), row(C3_W), row(GATES_W), _resident((3, CONV_W)),
                  _resident((ATTN_W, D_MODEL)), _resident((CONV_W, D_MODEL)), _resident((D_MODEL, D_MODEL)),
                  _resident((1, D_MODEL))],
        out_specs=[row(CONV_W), row(D_MODEL), row(D_MODEL), row(D_MODEL), row(D_MODEL), row(D_MODEL)],
        out_shape=[jax.ShapeDtypeStruct((s, CONV_W), BF16), jax.ShapeDtypeStruct((s, D_MODEL), BF16),
                   jax.ShapeDtypeStruct((s, D_MODEL), BF16), jax.ShapeDtypeStruct((s, D_MODEL), BF16),
                   jax.ShapeDtypeStruct((s, D_MODEL), F32), jax.ShapeDtypeStruct((s, D_MODEL), BF16)],
        scratch_shapes=[pltpu.VMEM((8, CONV_W), F32)],
        compiler_params=_params("arbitrary"),
    )(x, attn, c3, gates, conv_w, w_ab, w_cb, w_out, g2)


def _ffn_fwd_loss(hn, h1, w_up, ffn_cw, w_down, g3, target, tm):
    s = hn.shape[0]

    def body(hn_ref, h1_ref, wu_ref, cw_ref, wd_ref, g_ref, t_ref,
             u_ref, up_ref, act_ref, dh2_ref, loss_ref, gfn_ref, carry_ref):
        @pl.when(pl.program_id(0) == 0)
        def _():
            carry_ref[...] = jnp.zeros_like(carry_ref)
            loss_ref[...] = jnp.zeros_like(loss_ref)
            gfn_ref[...] = jnp.zeros_like(gfn_ref)

        u = jnp.dot(hn_ref[...], wu_ref[...], preferred_element_type=F32)
        u_ref[...] = u.astype(BF16)
        up = _causal_conv(u, carry_ref[...], cw_ref[...])
        up_ref[...] = up.astype(BF16)
        carry_ref[...] = u[tm - 8:tm]
        gate, val = up[:, :D_FF], up[:, D_FF:]
        act = (gate * _sigmoid(gate) * val).astype(BF16)
        act_ref[...] = act
        h2 = h1_ref[...] + jnp.dot(act, wd_ref[...], preferred_element_type=F32)
        rstd = _rstd(h2)
        g = g_ref[...]
        err = h2 * rstd * g - t_ref[...]
        loss_ref[...] += jnp.sum(err * err) * (0.5 / D_MODEL)
        dh2, dg = _rms_bwd(err * (1.0 / D_MODEL), h2, rstd, g)
        dh2_ref[...] = dh2
        gfn_ref[...] += jnp.sum(dg, axis=0, keepdims=True)

    row = lambda w: pl.BlockSpec((tm, w), lambda i: (i, 0))
    acc = lambda w: pl.BlockSpec((1, w), lambda i: (0, 0))
    return pl.pallas_call(
        body, name="ffn_fwd_loss", grid=(s // tm,),
        in_specs=[row(D_MODEL), row(D_MODEL), _resident((D_MODEL, FF2)), _resident((3, FF2)),
                  _resident((D_FF, D_MODEL)), _resident((1, D_MODEL)), row(D_MODEL)],
        out_specs=[row(FF2), row(FF2), row(D_FF), row(D_MODEL), acc(128), acc(D_MODEL)],
        out_shape=[jax.ShapeDtypeStruct((s, FF2), BF16), jax.ShapeDtypeStruct((s, FF2), BF16),
                   jax.ShapeDtypeStruct((s, D_FF), BF16),
                   jax.ShapeDtypeStruct((s, D_MODEL), F32), jax.ShapeDtypeStruct((1, 128), F32),
                   jax.ShapeDtypeStruct((1, D_MODEL), F32)],
        scratch_shapes=[pltpu.VMEM((8, FF2), F32)],
        compiler_params=_params("arbitrary"),
    )(hn, h1, w_up, ffn_cw, w_down, g3, target)


def _ffn_bwd(dh2, u, up, act, h1, w_up, ffn_cw, w_down, g2, tm):
    s = dh2.shape[0]
    nt = s // tm

    def body(dh2_ref, u_ref, up_ref, act_ref, h1_ref, wu_ref, cw_ref, wd_ref, g_ref,
             du_ref, dh1_ref, gcw_ref, gg_ref, gwd_ref, carry_ref):
        @pl.when(pl.program_id(0) == 0)
        def _():
            for ref in (carry_ref, gcw_ref, gg_ref, gwd_ref):
                ref[...] = jnp.zeros_like(ref)

        dh2v = dh2_ref[...]
        dh2b = dh2v.astype(BF16)
        gwd_ref[...] += lax.dot_general(act_ref[...], dh2b, TN, preferred_element_type=F32)
        dact = lax.dot_general(dh2b, wd_ref[...], NT, preferred_element_type=F32)
        upv = up_ref[...].astype(F32)
        gate, val = upv[:, :D_FF], upv[:, D_FF:]
        sg = _sigmoid(gate)
        dval = dact * (gate * sg)
        dgate = dact * val * (sg * (1.0 + gate * (1.0 - sg)))
        dup = jnp.concatenate([dgate, dval], axis=1)
        dup1, dup2 = _rows_after(dup, carry_ref[...])
        carry_ref[...] = dup[0:8]
        u = u_ref[...].astype(F32)
        gcw_ref[2:3, :] += jnp.sum(dup * u, axis=0, keepdims=True)
        gcw_ref[1:2, :] += jnp.sum(dup1 * u, axis=0, keepdims=True)
        gcw_ref[0:1, :] += jnp.sum(dup2 * u, axis=0, keepdims=True)
        cw = cw_ref[...]
        du = (cw[2:3] * dup + cw[1:2] * dup1 + cw[0:1] * dup2).astype(BF16)
        du_ref[...] = du
        dhn = lax.dot_general(du, wu_ref[...], NT, preferred_element_type=F32)
        h1v = h1_ref[...]
        dh1, dg = _rms_bwd(dhn, h1v, _rstd(h1v), g_ref[...])
        dh1_ref[...] = dh2v + dh1
        gg_ref[...] += jnp.sum(dg, axis=0, keepdims=True)

    row = lambda w: pl.BlockSpec((tm, w), lambda i: (nt - 1 - i, 0))
    return pl.pallas_call(
        body, name="ffn_bwd", grid=(nt,),
        in_specs=[row(D_MODEL), row(FF2), row(FF2), row(D_FF),
                  row(D_MODEL), _resident((D_MODEL, FF2)), _resident((3, FF2)), _resident((D_FF, D_MODEL)),
                  _resident((1, D_MODEL))],
        out_specs=[row(FF2), row(D_MODEL), pl.BlockSpec((3, FF2), lambda i: (0, 0)),
                   pl.BlockSpec((1, D_MODEL), lambda i: (0, 0)), _resident((D_FF, D_MODEL))],
        out_shape=[jax.ShapeDtypeStruct((s, FF2), BF16), jax.ShapeDtypeStruct((s, D_MODEL), F32),
                   jax.ShapeDtypeStruct((3, FF2), F32), jax.ShapeDtypeStruct((1, D_MODEL), F32),
                   jax.ShapeDtypeStruct((D_FF, D_MODEL), F32)],
        scratch_shapes=[pltpu.VMEM((8, FF2), F32)],
        compiler_params=_params("arbitrary"),
    )(dh2, u, up, act, h1, w_up, ffn_cw, w_down, g2)


def _mix_bwd(dh1, gates, a, cv, c3, attn, conv, merged, conv_w, w_ab, w_cb, w_out, tm, comm=None):
    s = dh1.shape[0]
    nt = s // tm
    halo = 16

    def body(dh1_ref, gt_ref, a_ref, cv_ref, c3_ref, ch_ref, at_ref, cn_ref, mg_ref, cw_ref, wab_ref, wcb_ref,
             wo_ref, dat_ref, dc3_ref, dgt_ref, gcw_ref, gab_ref, gcb_ref, gout_ref,
             carry_ref, ab_acc, cb_acc, out_acc):
        i = pl.program_id(0)

        @pl.when(i == 0)
        def _():
            for ref in (carry_ref, gcw_ref, ab_acc, cb_acc, out_acc):
                ref[...] = jnp.zeros_like(ref)

        dh1v = dh1_ref[...].astype(BF16)
        out_acc[...] += lax.dot_general(mg_ref[...], dh1v, TN, preferred_element_type=F32)
        dm = lax.dot_general(dh1v, wo_ref[...], NT, preferred_element_type=F32)
        gt = gt_ref[...].astype(F32)
        sa, sc = _sigmoid(gt[:, :D_MODEL]), _sigmoid(gt[:, D_MODEL:])
        da = (dm * sa).astype(BF16)
        dcv = (dm * sc).astype(BF16)
        ab_acc[...] += lax.dot_general(at_ref[...], da, TN, preferred_element_type=F32)
        cb_acc[...] += lax.dot_general(cn_ref[...], dcv, TN, preferred_element_type=F32)
        dgt_ref[...] = jnp.concatenate(
            [dm * a_ref[...].astype(F32) * (sa * (1.0 - sa)), dm * cv_ref[...].astype(F32) * (sc * (1.0 - sc))],
            axis=1).astype(BF16)
        dat_ref[...] = lax.dot_general(da, wab_ref[...], NT, preferred_element_type=F32).astype(BF16)
        dconv = lax.dot_general(dcv, wcb_ref[...], NT, preferred_element_type=F32)
        c3v = c3_ref[...].astype(F32)
        cb, cc, cx = c3v[:, :CONV_W], c3v[:, CONV_W:2 * CONV_W], c3v[:, 2 * CONV_W:]
        z = cc * cx
        chv = ch_ref[...].astype(F32)[halo - 8:halo] * (i < nt - 1).astype(F32)
        zh = chv[:, CONV_W:2 * CONV_W] * chv[:, 2 * CONV_W:]
        cw = cw_ref[...]
        cz = _causal_conv(z, zh, cw)
        dcz = dconv * cb
        dcz1, dcz2 = _rows_after(dcz, carry_ref[...])
        carry_ref[...] = dcz[0:8]
        gcw_ref[2:3, :] += jnp.sum(dcz * z, axis=0, keepdims=True)
        gcw_ref[1:2, :] += jnp.sum(dcz1 * z, axis=0, keepdims=True)
        gcw_ref[0:1, :] += jnp.sum(dcz2 * z, axis=0, keepdims=True)
        dz = cw[2:3] * dcz + cw[1:2] * dcz1 + cw[0:1] * dcz2
        dc3_ref[...] = jnp.concatenate([dconv * cz, dz * cx, dz * cc], axis=1).astype(BF16)

        @pl.when(i == nt - 1)
        def _():
            gab_ref[...] = ab_acc[...].astype(BF16)
            gcb_ref[...] = cb_acc[...].astype(BF16)
            gout_ref[...] = out_acc[...].astype(BF16)

    row = lambda w: pl.BlockSpec((tm, w), lambda i: (nt - 1 - i, 0))
    return _call(
        comm, body, name="mix_bwd", grid=(nt,),
        in_specs=[row(D_MODEL), row(GATES_W), row(D_MODEL), row(D_MODEL), row(C3_W),
                  pl.BlockSpec((halo, C3_W), lambda i: (jnp.maximum((nt - 1 - i) * (tm // halo) - 1, 0), 0)),
                  row(ATTN_W), row(CONV_W), row(D_MODEL), _resident((3, CONV_W)), _resident((ATTN_W, D_MODEL)),
                  _resident((CONV_W, D_MODEL)), _resident((D_MODEL, D_MODEL))],
        out_specs=[row(ATTN_W), row(C3_W), row(GATES_W), pl.BlockSpec((3, CONV_W), lambda i: (0, 0)),
                   _resident((ATTN_W, D_MODEL)), _resident((CONV_W, D_MODEL)), _resident((D_MODEL, D_MODEL))],
        out_shape=[jax.ShapeDtypeStruct((s, ATTN_W), BF16), jax.ShapeDtypeStruct((s, C3_W), BF16),
                   jax.ShapeDtypeStruct((s, GATES_W), BF16), jax.ShapeDtypeStruct((3, CONV_W), F32),
                   jax.ShapeDtypeStruct((ATTN_W, D_MODEL), BF16), jax.ShapeDtypeStruct((CONV_W, D_MODEL), BF16),
                   jax.ShapeDtypeStruct((D_MODEL, D_MODEL), BF16)],
        scratch_shapes=[pltpu.VMEM((8, CONV_W), F32), pltpu.VMEM((ATTN_W, D_MODEL), F32),
                        pltpu.VMEM((CONV_W, D_MODEL), F32), pltpu.VMEM((D_MODEL, D_MODEL), F32)],
        compiler_params=_params("arbitrary"),
    )(dh1, gates, a, cv, c3, c3, attn, conv, merged, conv_w, w_ab, w_cb, w_out)


def _attn_bwd(qkv, sinks, o, do, comm=None):
    s = qkv.shape[0]
    nb = s // BLOCK

    def body(sk_ref, bias_ref, q_ref, kp_ref, kc_ref, vp_ref, vc_ref, o_ref, do_ref,
             dq_ref, dk_ref, dv_ref, dsk_ref, ck_ref, cvv_ref):
        i = pl.program_id(0)

        @pl.when(i == 0)
        def _():
            ck_ref[...] = jnp.zeros_like(ck_ref)
            cvv_ref[...] = jnp.zeros_like(cvv_ref)
            dsk_ref[...] = jnp.zeros_like(dsk_ref)

        @pl.when(i < nb)
        def _():
            bias = bias_ref[...]
            q, kp, kc, vp, vc = q_ref[...], kp_ref[...], kc_ref[...], vp_ref[...], vc_ref[...]
            ov, dov = o_ref[...], do_ref[...]
            dqs, dks, dvs = [], [], []
            for h in range(N_KV_HEADS):
                hs = slice(h * HEAD_DIM, (h + 1) * HEAD_DIM)
                k2 = jnp.concatenate([kp[:, hs], kc[:, hs]], axis=0)
                v2 = jnp.concatenate([vp[:, hs], vc[:, hs]], axis=0)
                qg, og, dog = _stack_heads(q, h), _stack_heads(ov, h), _stack_heads(dov, h)
                sc = lax.dot_general(qg, k2, NT, preferred_element_type=F32) * ATTN_SCALE + bias
                sink = _sink_column(sk_ref, h)
                m = jnp.maximum(jnp.max(sc, axis=1, keepdims=True), sink)
                p = jnp.exp(sc - m)
                psink = jnp.exp(sink - m)
                inv = 1.0 / (jnp.sum(p, axis=1, keepdims=True) + psink)
                p = p * inv
                delta = jnp.sum(dog.astype(F32) * og.astype(F32), axis=1, keepdims=True)
                dp = lax.dot_general(dog, v2, NT, preferred_element_type=F32)
                ds = (p * (dp - delta)).astype(BF16)
                dqs.append(jnp.dot(ds, k2, preferred_element_type=F32) * ATTN_SCALE)
                dks.append(lax.dot_general(ds, qg, TN, preferred_element_type=F32) * ATTN_SCALE)
                dvs.append(lax.dot_general(p.astype(BF16), dog, TN, preferred_element_type=F32))
                dsink = -(psink * inv * delta)
                for g in range(GROUP):
                    r = h * GROUP + g
                    dsk_ref[r:r + 1, :] += jnp.sum(dsink[g * BLOCK:(g + 1) * BLOCK])
            dq_ref[...] = _unstack_heads(dqs).astype(BF16)
            dk2 = jnp.concatenate(dks, axis=1)
            dv2 = jnp.concatenate(dvs, axis=1)
            dk_ref[...] = (ck_ref[...] + dk2[:BLOCK]).astype(BF16)
            dv_ref[...] = (cvv_ref[...] + dv2[:BLOCK]).astype(BF16)
            ck_ref[...] = dk2[BLOCK:]
            cvv_ref[...] = dv2[BLOCK:]

        @pl.when(i == nb)
        def _():
            dk_ref[...] = ck_ref[...].astype(BF16)
            dv_ref[...] = cvv_ref[...].astype(BF16)

    cur = lambda i: jnp.minimum(i, nb - 1)
    done = lambda i: jnp.maximum(i - 1, 0)
    return _call(
        comm, body, name="attn_bwd", grid=(nb + 1,),
        in_specs=[pl.BlockSpec(memory_space=pltpu.SMEM), _attn_bias_spec(), *_attn_specs(nb),
                  pl.BlockSpec((BLOCK, ATTN_W), lambda i: (cur(i), 0)),
                  pl.BlockSpec((BLOCK, ATTN_W), lambda i: (cur(i), 0))],
        out_specs=[pl.BlockSpec((BLOCK, ATTN_W), lambda i: (cur(i), 0)),
                   pl.BlockSpec((BLOCK, KV_W), lambda i: (done(i), 0)),
                   pl.BlockSpec((BLOCK, KV_W), lambda i: (done(i), 0)),
                   pl.BlockSpec((N_HEADS, 128), lambda i: (0, 0))],
        out_shape=[jax.ShapeDtypeStruct((s, ATTN_W), BF16), jax.ShapeDtypeStruct((s, KV_W), BF16),
                   jax.ShapeDtypeStruct((s, KV_W), BF16), jax.ShapeDtypeStruct((N_HEADS, 128), F32)],
        scratch_shapes=[pltpu.VMEM((BLOCK, KV_W), F32), pltpu.VMEM((BLOCK, KV_W), F32)],
        compiler_params=_params("arbitrary"),
    )(sinks, _attn_bias(), qkv, qkv, qkv, qkv, qkv, o, do)


def _inproj_bwd(dq, dk, dv, dc3, dgt, w_in, x, xn, dh1, g1, tm):
    s = x.shape[0]
    nt = s // tm

    def body(dq_ref, dk_ref, dv_ref, dc3_ref, dgt_ref, w_ref, x_ref, xn_ref, dh1_ref, g_ref,
             dx_ref, gw_ref, gb_ref, gg_ref, acc_ref):
        i = pl.program_id(0)

        @pl.when(i == 0)
        def _():
            for ref in (gb_ref, gg_ref, acc_ref):
                ref[...] = jnp.zeros_like(ref)

        dp = jnp.concatenate([dq_ref[...], dk_ref[...], dv_ref[...], dc3_ref[...], dgt_ref[...]], axis=1)
        acc_ref[...] += lax.dot_general(dp, xn_ref[...], TN, preferred_element_type=F32)
        gb_ref[...] += jnp.sum(dp.astype(F32), axis=0, keepdims=True)
        dxn = jnp.dot(dp, w_ref[...], preferred_element_type=F32)
        xf = x_ref[...]
        dx, dg = _rms_bwd(dxn, xf, _rstd(xf), g_ref[...])
        dx_ref[...] = dh1_ref[...] + dx
        gg_ref[...] += jnp.sum(dg, axis=0, keepdims=True)

        @pl.when(i == nt - 1)
        def _():
            gw_ref[...] = acc_ref[...].astype(BF16)

    row = lambda w: pl.BlockSpec((tm, w), lambda i: (i, 0))
    acc = lambda w: pl.BlockSpec((1, w), lambda i: (0, 0))
    return pl.pallas_call(
        body, name="inproj_bwd", grid=(nt,),
        in_specs=[row(ATTN_W), row(KV_W), row(KV_W), row(C3_W), row(GATES_W), _resident((IN_W, D_MODEL)),
                  row(D_MODEL), row(D_MODEL), row(D_MODEL), _resident((1, D_MODEL))],
        out_specs=[row(D_MODEL), _resident((IN_W, D_MODEL)), acc(IN_W), acc(D_MODEL)],
        out_shape=[jax.ShapeDtypeStruct((s, D_MODEL), F32), jax.ShapeDtypeStruct((IN_W, D_MODEL), BF16),
                   jax.ShapeDtypeStruct((1, IN_W), F32), jax.ShapeDtypeStruct((1, D_MODEL), F32)],
        scratch_shapes=[pltpu.VMEM((IN_W, D_MODEL), F32)],
        compiler_params=_params("arbitrary"),
    )(dq, dk, dv, dc3, dgt, w_in, x, xn, dh1, g1)


def _wgrad(a, b, bm, bn, bk, name, comm=None):
    s, m = a.shape
    n = b.shape[1]
    nk = s // bk

    def body(a_ref, b_ref, o_ref, acc_ref):
        k = pl.program_id(2)

        @pl.when(k == 0)
        def _():
            acc_ref[...] = jnp.zeros_like(acc_ref)

        acc_ref[...] += lax.dot_general(a_ref[...].astype(BF16), b_ref[...].astype(BF16), TN,
                                        preferred_element_type=F32)

        @pl.when(k == nk - 1)
        def _():
            o_ref[...] = acc_ref[...].astype(BF16)

    return _call(
        comm, body, name=name, grid=(m // bm, n // bn, nk),
        in_specs=[pl.BlockSpec((bk, bm), lambda i, j, k: (k, i)), pl.BlockSpec((bk, bn), lambda i, j, k: (k, j))],
        out_specs=pl.BlockSpec((bm, bn), lambda i, j, k: (i, j)),
        out_shape=jax.ShapeDtypeStruct((m, n), BF16),
        scratch_shapes=[pltpu.VMEM((bm, bn), F32)],
        compiler_params=_params("parallel", "parallel", "arbitrary"),
    )(a, b)


class _Carry:
    def __init__(self, jobs, reads=None, bufs=None, fresh=None):
        self.jobs, self.reads, self.bufs, self.fresh = jobs, reads or {}, bufs or {}, fresh or {}
        self.out = {}


class _Job:
    def __init__(self, n_sems, plan):
        self.n_sems, self.plan = n_sems, plan


def _plan_all(jobs, hbm, send, recv):
    pos = _position()
    starts, waits, base = [], [], 0
    for job in jobs:
        s, w = job.plan(hbm, pos, send, recv, base)
        starts, waits, base = starts + s, waits + w, base + job.n_sems
    return starts, waits


def _call(comm, body, **kw):
    if comm is None:
        return pl.pallas_call(body, **kw)
    grid = kw["grid"]
    single = not isinstance(kw["out_shape"], (list, tuple))
    out_shape = [kw["out_shape"]] if single else list(kw["out_shape"])
    out_specs = [kw["out_specs"]] if single else list(kw["out_specs"])
    in_specs = list(kw["in_specs"])
    scratch = list(kw.get("scratch_shapes", ()))
    r_names, b_names, f_names = list(comm.reads), list(comm.bufs), list(comm.fresh)
    n_args, n_out, n_scr = len(in_specs), len(out_shape), len(scratch)
    n_sems = sum(j.n_sems for j in comm.jobs)

    def wrapped(*refs):
        k = n_args
        hbm = dict(zip(r_names, refs[k:k + len(r_names)]))
        k += len(r_names) + len(b_names)
        outs = refs[k:k + n_out]
        k += n_out
        hbm.update(zip(b_names + f_names, refs[k:k + len(b_names) + len(f_names)]))
        k += len(b_names) + len(f_names)
        send, recv = refs[k + n_scr:]
        starts, waits = _plan_all(comm.jobs, hbm, send, recv)
        ids = [pl.program_id(a) for a in range(len(grid))]
        first = functools.reduce(jnp.logical_and, [i == 0 for i in ids])
        last = functools.reduce(jnp.logical_and, [i == g - 1 for i, g in zip(ids, grid)])

        @pl.when(first)
        def _():
            for cp in starts:
                cp.start()

        body(*refs[:n_args], *outs, *refs[k:k + n_scr])

        @pl.when(last)
        def _():
            for cp in waits:
                cp.wait_recv()
            for cp in starts:
                cp.wait_send()

    sems = pltpu.SemaphoreType.DMA((n_sems,))
    held = [jax.ShapeDtypeStruct(a.shape, a.dtype) for a in comm.bufs.values()] + list(comm.fresh.values())
    call = pl.pallas_call(
        wrapped, name=kw["name"], grid=grid,
        in_specs=in_specs + [_ANY] * (len(r_names) + len(b_names)),
        out_specs=out_specs + [_ANY] * len(held),
        out_shape=out_shape + held,
        input_output_aliases={n_args + len(r_names) + i: n_out + i for i in range(len(b_names))},
        scratch_shapes=scratch + [sems, sems],
        compiler_params=_params(*["arbitrary"] * len(grid)),
    )

    def run(*args):
        res = call(*args, *comm.reads.values(), *comm.bufs.values())
        comm.out = dict(zip(b_names + f_names, res[n_out:]))
        return res[0] if single else res[:n_out]

    return run


def _exchange(name, phases, reads=None, bufs=None, fresh=None):
    comm = _Carry([j for ph in phases for j in ph], reads, bufs, fresh)
    r_names, b_names, f_names = list(comm.reads), list(comm.bufs), list(comm.fresh)
    n_sems = sum(j.n_sems for j in comm.jobs)

    def body(*refs):
        hbm = dict(zip(r_names, refs[:len(r_names)]))
        k = len(r_names) + len(b_names)
        hbm.update(zip(b_names + f_names, refs[k:k + len(b_names) + len(f_names)]))
        send, recv = refs[-2:]
        pos = _position()
        started, base = [], 0
        for ph in phases:
            waits = []
            for job in ph:
                s, w = job.plan(hbm, pos, send, recv, base)
                base += job.n_sems
                for cp in s:
                    cp.start()
                started, waits = started + s, waits + w
            for cp in waits:
                cp.wait_recv()
        for cp in started:
            cp.wait_send()

    sems = pltpu.SemaphoreType.DMA((n_sems,))
    held = [jax.ShapeDtypeStruct(a.shape, a.dtype) for a in comm.bufs.values()] + list(comm.fresh.values())
    res = pl.pallas_call(
        body, name=name, in_specs=[_ANY] * (len(r_names) + len(b_names)), out_specs=[_ANY] * len(held),
        out_shape=held, input_output_aliases={len(r_names) + i: i for i in range(len(b_names))},
        scratch_shapes=[sems, sems],
    )(*comm.reads.values(), *comm.bufs.values())
    return dict(zip(b_names + f_names, res))


_HBM = pl.BlockSpec(memory_space=pltpu.HBM)
_SEM = pl.BlockSpec(memory_space=pltpu.SEMAPHORE)
_EFFECT = pltpu.SideEffectType.DATAFLOW_SIDE_EFFECTING


def _start_exchanges(name, groups):
    names = [list(arrays) for _, arrays in groups]
    first = [sum(len(ns) for ns in names[:g]) for g in range(len(groups))]
    n, ng = sum(len(ns) for ns in names), len(groups)

    def body(*refs):
        for g, (jobs, _) in enumerate(groups):
            hbm = dict(zip(names[g], refs[first[g]:first[g] + len(names[g])]))
            for cp in _plan_all(jobs, hbm, refs[n + 2 * g], refs[n + 2 * g + 1])[0]:
                cp.start()
        refs[-1][...] = jnp.zeros_like(refs[-1])

    given = [pltpu.with_memory_space_constraint(
        a if isinstance(a, jax.Array) else lax.empty(a.shape, a.dtype), pltpu.HBM)
        for _, arrays in groups for a in arrays.values()]
    sems = [pltpu.SemaphoreType.DMA((sum(j.n_sems for j in jobs),)) for jobs, _ in groups for _ in range(2)]
    res = pl.pallas_call(
        body, name=name,
        out_shape=(*sems, *[pltpu.HBM(a.shape, a.dtype) for a in given], jax.ShapeDtypeStruct((8, 128), F32)),
        in_specs=[_HBM] * n, out_specs=(*[_SEM] * (2 * ng), *[_HBM] * n, pl.BlockSpec(memory_space=pltpu.VMEM)),
        input_output_aliases={i: 2 * ng + i for i in range(n)},
        compiler_params=pltpu.CompilerParams(has_side_effects=_EFFECT),
    )(*given)
    held = res[2 * ng:2 * ng + n]
    states = [(names[g], groups[g][0], res[2 * g], res[2 * g + 1], held[first[g]:first[g] + len(names[g])])
              for g in range(ng)]
    return states, res[-1]


def _start_exchange(name, jobs, arrays):
    states, token = _start_exchanges(name, [(jobs, arrays)])
    return states[0], token


def _finish_exchange(name, state, after):
    names, jobs, send_sem, recv_sem, held = state
    n = len(names)

    def body(*refs):
        hbm = dict(zip(names, refs[:n]))
        send, recv = refs[n:n + 2]
        starts, waits = _plan_all(jobs, hbm, send, recv)
        for cp in waits:
            cp.wait_recv()
        for cp in starts:
            cp.wait_send()

    res = pl.pallas_call(
        body, name=name, out_shape=tuple(pltpu.HBM(a.shape, a.dtype) for a in held),
        in_specs=[_HBM] * n + [_SEM, _SEM, _ANY], out_specs=tuple([_HBM] * n),
        input_output_aliases={i: i for i in range(n)},
        compiler_params=pltpu.CompilerParams(has_side_effects=_EFFECT),
    )(*held, send_sem, recv_sem, after)
    return dict(zip(names, res))


def _row_tile(rows, bytes_per_row):
    best = 16
    for t in range(16, rows + 1, 16):
        if rows % t == 0 and t * bytes_per_row <= 9 * 1024 * 1024:
            best = t
    return best


def _rowwise(fn, ins, out_dtypes, name, after=None):
    rows, cols = ins[0].shape
    per_row = sum(cols * a.dtype.itemsize for a in ins) + sum(cols * jnp.dtype(d).itemsize for d in out_dtypes)
    tr = _row_tile(rows, per_row)
    n_in = len(ins)

    def body(*refs):
        outs = fn(*[r[...] for r in refs[:n_in]])
        for o_ref, o in zip(refs[-len(out_dtypes):], outs):
            o_ref[...] = o.astype(o_ref.dtype)

    tile = pl.BlockSpec((tr, cols), lambda i: (i, 0))
    behind = [] if after is None else [after]
    return pl.pallas_call(
        body, name=name, grid=(rows // tr,),
        in_specs=[tile] * n_in + [pl.BlockSpec((8, 128), lambda i: (0, 0))] * len(behind),
        out_specs=[tile] * len(out_dtypes),
        out_shape=[jax.ShapeDtypeStruct((rows, cols), d) for d in out_dtypes],
        compiler_params=_params("parallel"),
    )(*ins, *behind)


def _tiled(fn, name, grid, pos, ins, outs):
    n_in = len(ins)

    def body(pos_ref, *refs):
        res = fn(*[r[...] for r in refs[:n_in]])
        for o_ref, o in zip(refs[n_in:], res):
            o_ref[...] = o.astype(o_ref.dtype)

    return pl.pallas_call(
        body, name=name,
        grid_spec=pltpu.PrefetchScalarGridSpec(
            num_scalar_prefetch=1, grid=grid,
            in_specs=[pl.BlockSpec(bs, im) for _, bs, im in ins],
            out_specs=[pl.BlockSpec(bs, im) for _, _, bs, im in outs]),
        out_shape=[jax.ShapeDtypeStruct(s, d) for s, d, _, _ in outs],
        compiler_params=_params("parallel"),
    )(pos, *[a for a, _, _ in ins])


def _adamw(w, g, m, v):
    m = ADAM_B1 * m + (1.0 - ADAM_B1) * g
    v = ADAM_B2 * v + (1.0 - ADAM_B2) * (g * g)
    m_hat = m / (1.0 - ADAM_B1 ** ADAM_STEP)
    v_hat = v / (1.0 - ADAM_B2 ** ADAM_STEP)
    return -ADAM_LR * (m_hat / (jnp.sqrt(v_hat) + ADAM_EPS) + ADAM_WD * w), m, v


def _adamw_small(params):
    n = len(params)

    def body(*refs):
        for k in range(n):
            w, g, m, v = (r[...] for r in refs[4 * k:4 * k + 4])
            for o_ref, o in zip(refs[4 * n + 3 * k:4 * n + 3 * k + 3], _adamw(w, g, m, v)):
                o_ref[...] = o

    flat = [a for p in params for a in p]
    return pl.pallas_call(
        body, name="adamw_small",
        out_shape=[jax.ShapeDtypeStruct(p[0].shape, F32) for p in params for _ in range(3)],
    )(*flat)


class _Layout:
    def __init__(self, rows, cols, stacked):
        self.rows, self.cols, self.stacked = rows, cols, stacked

    def whole(self, rows=None):
        r = self.rows if rows is None else rows
        return (N_CHIPS, r, self.cols) if self.stacked else (r, N_CHIPS * self.cols)

    def part_rows(self, h, q=0, nq=1):
        n = self.rows // 2 // nq
        return pl.ds(pl.multiple_of(h * (self.rows // 2) + q * n, 16), n)

    def half_rows(self, h):
        return self.part_rows(h)

    def block(self, ref, p, rows=slice(None)):
        if self.stacked:
            return ref.at[p, rows, :]
        return ref.at[rows, pl.ds(pl.multiple_of(p * self.cols, 128), self.cols)]

    def all_chips(self, ref, rows):
        return ref.at[:, rows, :] if self.stacked else ref.at[rows, :]


BIG = (
    _Layout(IN_SHARD, D_MODEL, True),
    _Layout(ATTN_W, D_MODEL // N_CHIPS, False),
    _Layout(CONV_W, D_MODEL // N_CHIPS, False),
    _Layout(D_MODEL // N_CHIPS, D_MODEL, True),
    _Layout(D_MODEL, FF2 // N_CHIPS, False),
    _Layout(D_FF // N_CHIPS, D_MODEL, True),
)
N_BIG = len(BIG)
_ANY = pl.BlockSpec(memory_space=pl.ANY)


def _position():
    x, y, c = lax.axis_index("x"), lax.axis_index("y"), lax.axis_index("c")
    return x, y, c, 2 * x + y


def _core_of_chip(p, c):
    return (p >> 1, p & 1, c)


def _place_cast(shard, lay, pos, name, after=None):
    rows, cols = shard.shape
    tr = _row_tile(rows, cols * 6)
    if lay.stacked:
        out = (lay.whole(), BF16, (None, tr, cols), lambda i, pos: (pos[0], i, 0))
    else:
        out = (lay.whole(), BF16, (tr, cols), lambda i, pos: (i, pos[0]))
    ins = [(shard, (tr, cols), lambda i, pos: (i, 0))]
    if after is not None:
        ins.append((after, (8, 128), lambda i, pos: (0, 0)))
    return _tiled(lambda a, *_: (a,), name, (rows // tr,), pos, ins, [out])[0]


def _remote(src, dst, send, recv, k, device):
    return pltpu.make_async_remote_copy(src_ref=src, dst_ref=dst, send_sem=send.at[k], recv_sem=recv.at[k],
                                        device_id=device, device_id_type=MESH)


def _arrival(dst, send, recv, k, me):
    return _remote(dst, dst, send, recv, k, me)


def _gather_ici(lay, name, q=0, nq=1):
    def plan(hbm, pos, send, recv, base):
        x, y, c, me = pos
        rows = lay.part_rows(c, q, nq)
        mine = lay.block(hbm[name], me, rows)
        starts = [_remote(mine, mine, send, recv, base + d - 1, _core_of_chip(me ^ d, c)) for d in (1, 2, 3)]
        waits = [_arrival(lay.block(hbm[name], me ^ d, rows), send, recv, base + d - 1, (x, y, c)) for d in (1, 2, 3)]
        return starts, waits
    return _Job(3, plan)


def _gather_d2d(lay, name, q=0, nq=1):
    def plan(hbm, pos, send, recv, base):
        x, y, c, me = pos
        starts, waits = [], []
        for d in (1, 2, 3):
            got = lay.block(hbm[name], me ^ d, lay.part_rows(c, q, nq))
            starts.append(_remote(got, got, send, recv, base + d - 1, (x, y, 1 - c)))
            waits.append(_arrival(lay.block(hbm[name], me ^ d, lay.part_rows(1 - c, q, nq)), send, recv, base + d - 1,
                                  (x, y, c)))
        return starts, waits
    return _Job(3, plan)


def _rs_pair(lay, grad, theirs):
    def plan(hbm, pos, send, recv, base):
        x, y, c, _ = pos
        out = _remote(lay.all_chips(hbm[grad], lay.half_rows(1 - c)), hbm[theirs], send, recv, base, (x, y, 1 - c))
        return [out], [_arrival(hbm[theirs], send, recv, base, (x, y, c))]
    return _Job(1, plan)


def _rs_chips(lay, sums, slots):
    def plan(hbm, pos, send, recv, base):
        x, y, c, me = pos
        starts = [_remote(lay.block(hbm[sums], me ^ d), hbm[slots].at[me], send, recv, base + d - 1,
                          _core_of_chip(me ^ d, c)) for d in (1, 2, 3)]
        waits = [_arrival(hbm[slots].at[me ^ d], send, recv, base + d - 1, (x, y, c)) for d in (1, 2, 3)]
        return starts, waits
    return _Job(3, plan)


def _rs_share(lay, shard):
    def plan(hbm, pos, send, recv, base):
        x, y, c, _ = pos
        mine = hbm[shard].at[lay.half_rows(c), :]
        other = hbm[shard].at[lay.half_rows(1 - c), :]
        return [_remote(mine, mine, send, recv, base, (x, y, 1 - c))], [_arrival(other, send, recv, base, (x, y, c))]
    return _Job(1, plan)


def _slots_shape(lay):
    return jax.ShapeDtypeStruct((N_CHIPS, lay.rows // 2, lay.cols), BF16)


def _theirs_shape(lay, dtype=BF16):
    return jax.ShapeDtypeStruct(lay.whole(lay.rows // 2), dtype)


def _pair_sum(grad, theirs, lay, pos, name):
    half = lay.rows // 2
    add = lambda a, b: (a.astype(F32) + b.astype(F32),)
    if lay.stacked:
        tr = _row_tile(half, lay.cols * 6)
        nt = half // tr
        flat = lambda a: a.reshape(-1, lay.cols)
        mine = lambda t, pos: ((t // nt) * (2 * nt) + pos[1] * nt + t % nt, 0)
        grid, blk = (N_CHIPS * nt,), (tr, lay.cols)
        grad, theirs = flat(grad), flat(theirs)
    else:
        tr = _row_tile(half, N_CHIPS * lay.cols * 6)
        nt = half // tr
        mine = lambda t, pos: (pos[1] * nt + t, 0)
        grid, blk = (nt,), (tr, N_CHIPS * lay.cols)
    same = lambda t, pos: (t, 0)
    out = _tiled(add, name, grid, pos, [(grad, blk, mine), (theirs, blk, same)], [(theirs.shape, BF16, blk, same)])[0]
    return out.reshape(lay.whole(half))


def _chip_sum(sums, slots, lay, pos, name, after=None):
    half = lay.rows // 2
    tr = _row_tile(half, lay.cols * 12)
    nt = half // tr
    blk3 = (None, tr, lay.cols)
    if lay.stacked:
        own = (sums, blk3, lambda i, pos: (pos[0], i, 0))
    else:
        own = (sums, (tr, lay.cols), lambda i, pos: (i, pos[0]))
    others = [(slots, blk3, functools.partial(lambda d, i, pos: (pos[0] ^ d, i, 0), d)) for d in (1, 2, 3)]

    def add(a, b1, b2, b3, *_):
        return (((a.astype(F32) + b1.astype(F32)) + b2.astype(F32)) + b3.astype(F32),)

    if after is not None:
        others.append((after, (8, 128), lambda i, pos: (0, 0)))
    return _tiled(add, name, (nt,), pos, [own] + others,
                  [((lay.rows, lay.cols), F32, (tr, lay.cols), lambda i, pos: (pos[1] * nt + i, 0))])[0]


N_DEV = 8


def _to_all(src, slots):
    def plan(hbm, pos, send, recv, base):
        x, y, c, _ = pos
        idx = 4 * x + 2 * y + c
        starts = [_remote(hbm[src], hbm[slots].at[idx], send, recv, base + k - 1,
                          (x ^ (k >> 2), y ^ ((k >> 1) & 1), c ^ (k & 1))) for k in range(1, N_DEV)]
        waits = [_arrival(hbm[slots].at[idx ^ k], send, recv, base + k - 1, (x, y, c)) for k in range(1, N_DEV)]
        return starts, waits
    return _Job(N_DEV - 1, plan)


def _sum_slots(own, slots, pos):
    def body(pos_ref, own_ref, slots_ref, o_ref):
        idx = 2 * pos_ref[0] + pos_ref[1]
        term = lambda q: jnp.where(idx == q, own_ref[...], slots_ref[q])
        acc = term(0)
        for q in range(1, N_DEV):
            acc = acc + term(q)
        o_ref[...] = acc

    return pl.pallas_call(
        body, name="sum_small", out_shape=jax.ShapeDtypeStruct(own.shape, F32),
        in_specs=[pl.BlockSpec(memory_space=pltpu.SMEM), pl.BlockSpec(memory_space=pltpu.VMEM),
                  pl.BlockSpec(memory_space=pltpu.VMEM)],
    )(pos, own, slots)


def _pack_rows(parts):
    padded = [jnp.pad(a, ((0, -a.shape[0] % 8), (0, 0))) for a in parts]
    starts = [sum(p.shape[0] for p in padded[:k]) for k in range(len(padded))]
    return jnp.concatenate(padded, axis=0), starts


def kernel(x, mix_norm, w_in, b_in, sinks, conv_w, w_attn_branch, w_conv_branch, w_out, ffn_norm, w_up, ffn_conv_w, w_down, final_norm, loss_target, m_mix_norm, m_w_in, m_b_in, m_sinks, m_conv_w, m_w_attn_branch, m_w_conv_branch, m_w_out, m_ffn_norm, m_w_up, m_ffn_conv_w, m_w_down, m_final_norm, v_mix_norm, v_w_in, v_b_in, v_sinks, v_conv_w, v_w_attn_branch, v_w_conv_branch, v_w_out, v_ffn_norm, v_w_up, v_ffn_conv_w, v_w_down, v_final_norm):
    me = 2 * lax.axis_index("x") + lax.axis_index("y")
    big_w = [w_in[0].T, w_attn_branch[0], w_conv_branch[0], w_out[0], w_up[0], w_down[0]]
    big_m = [m_w_in[0].T, m_w_attn_branch[0], m_w_conv_branch[0], m_w_out[0], m_w_up[0], m_w_down[0]]
    big_v = [v_w_in[0].T, v_w_attn_branch[0], v_w_conv_branch[0], v_w_out[0], v_w_up[0], v_w_down[0]]
    names = ("w_in", "w_ab", "w_cb", "w_out", "w_up", "w_down")

    pos = jnp.stack([me, lax.axis_index("c")]).astype(jnp.int32)

    lay = dict(zip(names, BIG))
    xs, target, sk = x[0], loss_target[0], sinks[0]
    s = xs.shape[0]
    tm, tm2, bk, bk2 = min(256, s), min(512, s), min(1024, s), min(2048, s)

    taps, (_, t0) = _pack_rows([conv_w[0], ffn_conv_w[0].reshape(3 * (FF2 // N_CHIPS // 128), 128)])
    placed = {"w_in": _place_cast(big_w[0], lay["w_in"], pos, "cast_w_in")}
    fly_in, started = _start_exchange("gather_in_start", [_gather_ici(lay["w_in"], "w_in")], {"w_in": placed["w_in"]})
    taps_flight, started = _start_exchange("taps_start", [_to_all("v", "slots")],
                                           {"v": taps + started[0:1], "slots": jnp.zeros((N_DEV, *taps.shape), F32)})
    for w, n in zip(big_w[1:], names[1:]):
        placed[n] = _place_cast(w, lay[n], pos, "cast_" + n, after=started)
    trio = ("w_ab", "w_cb", "w_out")
    (fly_trio, fly_up, fly_down), started = _start_exchanges("gather_rest_start", [
        ([_gather_ici(lay[n], n) for n in ws], {n: placed[n] for n in ws}) for ws in (trio, ("w_up",), ("w_down",))])

    got = _finish_exchange("gather_in_wait", fly_in, after=started)
    w_in_full = _exchange("gather_in_d2d", [[_gather_d2d(lay["w_in"], "w_in")]], bufs=got)["w_in"].reshape(IN_W, D_MODEL)
    xn, qkv, c3, gates = _inproj_fwd(xs, mix_norm, w_in_full, b_in, tm2)
    k2 = _Carry([_gather_d2d(lay[n], n) for n in trio], bufs=_finish_exchange("gather_trio_wait", fly_trio, after=qkv))
    attn = _attn_fwd(qkv, sk, comm=k2)
    w_ab, w_cb = k2.out["w_ab"], k2.out["w_cb"]
    w_out_full = k2.out["w_out"].reshape(D_MODEL, D_MODEL)
    k3 = _Carry([_gather_d2d(lay["w_up"], "w_up")], bufs=_finish_exchange("gather_up_wait", fly_up, after=attn))
    taps = _finish_exchange("taps_wait", taps_flight, after=attn)
    taps = lax.dynamic_update_slice(taps["slots"], taps["v"][None], (2 * me + lax.axis_index("c"), 0, 0))
    conv_full = taps[0::2, 0:3].transpose(1, 0, 2).reshape(3, CONV_W)
    ffn_cw_full = taps[0::2, t0:t0 + 33].reshape(N_CHIPS, 3, FF2 // N_CHIPS).transpose(1, 0, 2).reshape(3, FF2)
    conv, a, cv, merged, h1, hn = _mix_fwd(xs, attn, c3, gates, conv_full, w_ab, w_cb, w_out_full, ffn_norm, tm2, comm=k3)
    w_up_full = k3.out["w_up"]
    w_down_full = _exchange("gather_down_d2d", [[_gather_d2d(lay["w_down"], "w_down")]],
                            bufs=_finish_exchange("gather_down_wait", fly_down, after=hn))["w_down"].reshape(D_FF, D_MODEL)
    u, up, act, dh2, loss_part, g_fn = _ffn_fwd_loss(hn, h1, w_up_full, ffn_cw_full, w_down_full,
                                                     final_norm[None, :], target, tm)

    grads, sums, slots = {}, {}, {}

    def pair(*ws):
        return _Carry([_rs_pair(lay[n], "g_" + n, "t_" + n) for n in ws], reads={"g_" + n: grads[n] for n in ws},
                      fresh={"t_" + n: _theirs_shape(lay[n], grads[n].dtype) for n in ws})

    def chips(*ws, also=None):
        k = _Carry([_rs_chips(lay[n], "s_" + n, "r_" + n) for n in ws], reads={"s_" + n: sums[n] for n in ws},
                   fresh={"r_" + n: _slots_shape(lay[n]) for n in ws})
        if also is not None:
            k = _Carry(k.jobs + also.jobs, {**k.reads, **also.reads}, None, {**k.fresh, **also.fresh})
        return k

    def pair_sums(k, *ws):
        for n in ws:
            sums[n] = _pair_sum(grads[n], k.out["t_" + n], lay[n], pos, "pair_sum_" + n)

    def take_slots(k, *ws):
        for n in ws:
            slots[n] = k.out["r_" + n]

    du, dh1, g_fcw, g_g2, gw_down = _ffn_bwd(dh2, u, up, act, h1, w_up_full, ffn_cw_full, w_down_full, ffn_norm,
                                             min(128, s))
    grads["w_down"] = gw_down.reshape(lay["w_down"].whole())
    k4 = pair("w_down")
    grads["w_up"] = _wgrad(hn, du, D_MODEL, FF2 // 4, bk2, "wgrad_up", comm=k4)
    pair_sums(k4, "w_down")
    k5 = chips("w_down", also=pair("w_up"))
    dattn, dc3, dgt, g_cw, grads["w_ab"], grads["w_cb"], gw_out = _mix_bwd(
        dh1, gates, a, cv, c3, attn, conv, merged, conv_full, w_ab, w_cb, w_out_full, tm, comm=k5)
    grads["w_out"] = gw_out.reshape(lay["w_out"].whole())
    take_slots(k5, "w_down")
    pair_sums(k5, "w_up")
    k6 = chips("w_up", also=pair("w_out", "w_ab", "w_cb"))
    dq, dk, dv, g_sk = _attn_bwd(qkv, sk, attn, dattn, comm=k6)
    take_slots(k6, "w_up")
    pair_sums(k6, "w_out", "w_ab", "w_cb")
    trio_flight, started = _start_exchange(
        "rs_chips_trio_start", [_rs_chips(lay[n], "s_" + n, "r_" + n) for n in trio],
        {**{"s_" + n: sums[n] for n in trio}, **{"r_" + n: _slots_shape(lay[n]) for n in trio}})
    behind = mix_norm + jnp.tile(started[0:1], (1, D_MODEL // 128))
    grad_x, gw_in, g_b, g_g1 = _inproj_bwd(dq, dk, dv, dc3, dgt, w_in_full, xs, xn, dh1, behind, tm)
    grads["w_in"] = gw_in.reshape(lay["w_in"].whole())

    parts = [loss_part, g_g1, g_b, jnp.pad(g_sk[:, 0], (0, 120))[None, :], g_cw, g_g2, g_fcw, g_fn]
    packed, at = _pack_rows([p.reshape(-1, 128) for p in parts])
    small_flight, started = _start_exchange("small_start", [_to_all("v", "slots")],
                                            {"v": packed, "slots": jnp.zeros((N_DEV, *packed.shape), F32)})
    others = names[1:]
    in_flight, started = _start_exchange("rs_pair_in_start", [_rs_pair(lay["w_in"], "g", "t")],
                                         {"g": grads["w_in"], "t": _theirs_shape(lay["w_in"]), "behind": started})
    halves = {n: _chip_sum(sums[n], slots[n], lay[n], pos, "chip_sum_" + n, after=started) for n in ("w_up", "w_down")}
    landed = _finish_exchange("rs_pair_in_wait", in_flight, after=halves["w_down"])
    sums["w_in"] = _pair_sum(landed["g"], landed["t"], lay["w_in"], pos, "pair_sum_w_in")
    in_flight, started = _start_exchange("rs_chips_in_start", [_rs_chips(lay["w_in"], "s", "r")],
                                         {"s": sums["w_in"], "r": _slots_shape(lay["w_in"])})
    landed = _finish_exchange("rs_chips_trio_wait", trio_flight, after=started)
    for n in trio:
        halves[n] = _chip_sum(landed["s_" + n], landed["r_" + n], lay[n], pos, "chip_sum_" + n)
    shared = _exchange("share_halves", [[_rs_share(lay[n], n) for n in others]], bufs=halves)
    w_of, m_of, v_of = dict(zip(names, big_w)), dict(zip(names, big_m)), dict(zip(names, big_v))

    def adam(n, g, after=None):
        return _rowwise(lambda w, g, m, v: (g, *_adamw(w, g, m, v)), [w_of[n], g, m_of[n], v_of[n]], [F32] * 4,
                        "adamw_" + n, after=after)

    new_of, last = {}, None
    for n in ("w_up", "w_down", "w_out", "w_ab", "w_cb"):
        new_of[n] = adam(n, shared[n], last)
        last = new_of[n][1]

    arrived = _finish_exchange("small_wait", small_flight, after=last)
    total = _sum_slots(arrived["v"], arrived["slots"], pos)
    part = lambda k: total[at[k]:at[k] + parts[k].size // 128].reshape(parts[k].shape)
    loss = total[0, 0]
    g_mix, g_b, g_g2, g_fn = part(1), part(2), part(5), part(7)
    g_sk = part(3)[:, 0:N_HEADS]
    g_cw = lax.dynamic_slice(part(4), (0, me * 128), (3, 128))
    g_fcw = lax.dynamic_slice(part(6), (0, me * (FF2 // N_CHIPS)), (3, FF2 // N_CHIPS))
    small_p = [
        (mix_norm, g_mix, m_mix_norm, v_mix_norm), (b_in, g_b, m_b_in, v_b_in), (sinks, g_sk, m_sinks, v_sinks),
        (conv_w[0], g_cw, m_conv_w[0], v_conv_w[0]), (ffn_norm, g_g2, m_ffn_norm, v_ffn_norm),
        (ffn_conv_w[0], g_fcw, m_ffn_conv_w[0], v_ffn_conv_w[0]),
        (final_norm[None, :], g_fn, m_final_norm[None, :], v_final_norm[None, :])]
    small_new = _adamw_small(small_p)
    small_new = [small_new[3 * k:3 * k + 3] for k in range(len(small_p))]

    landed = _finish_exchange("rs_chips_in_wait", in_flight, after=small_new[0][0])
    half_in = _chip_sum(landed["s"], landed["r"], lay["w_in"], pos, "chip_sum_w_in")
    shared["w_in"] = _exchange("share_in", [[_rs_share(lay["w_in"], "w_in")]], bufs={"w_in": half_in})["w_in"]
    new_of["w_in"] = adam("w_in", shared["w_in"])
    big_g = [new_of[n][0] for n in names]
    big_new = [new_of[n][1:] for n in names]

    order = [("s", 0), ("b", 0), ("s", 1), ("s", 2), ("s", 3), ("b", 1), ("b", 2), ("b", 3), ("s", 4), ("b", 4),
             ("s", 5), ("b", 5), ("s", 6)]
    shapes = [mix_norm.shape, w_in.shape, b_in.shape, sinks.shape, conv_w.shape, w_attn_branch.shape,
              w_conv_branch.shape, w_out.shape, ffn_norm.shape, w_up.shape, ffn_conv_w.shape, w_down.shape,
              final_norm.shape]
    small_g = [p[1] for p in small_p]
    big_g[0] = big_g[0].T
    big_new[0] = [a.T for a in big_new[0]]
    out_g = [(small_g[k] if kind == "s" else big_g[k]).reshape(shp) for (kind, k), shp in zip(order, shapes)]
    news = [[(small_new[k][j] if kind == "s" else big_new[k][j]).reshape(shp) for (kind, k), shp in zip(order, shapes)]
            for j in range(3)]
    return (loss, grad_x[None], *out_g, *news[0], *news[1], *news[2])
```

```python
import functools

import jax
import jax.numpy as jnp
from jax import lax
from jax.experimental import pallas as pl
from jax.experimental.pallas import tpu as pltpu

F32 = jnp.float32
BF16 = jnp.bfloat16

D_MODEL = 1024
HEAD_DIM = 64
N_HEADS = 8
N_KV_HEADS = 2
GROUP = N_HEADS // N_KV_HEADS
BLOCK = 128
ATTN_SCALE = HEAD_DIM ** -0.5
ATTN_W = N_HEADS * HEAD_DIM
KV_W = N_KV_HEADS * HEAD_DIM
CONV_W = 512
QKV_W = ATTN_W + 2 * KV_W
C3_W = 3 * CONV_W
GATES_W = 2 * D_MODEL
IN_W = QKV_W + C3_W + GATES_W
D_FF = 2816
FF2 = 2 * D_FF
NORM_EPS = 1e-5
N_CHIPS = 4
IN_SHARD = IN_W // N_CHIPS
NEG = -1e30

ADAM_LR = 0.001
ADAM_B1 = 0.9
ADAM_B2 = 0.999
ADAM_EPS = 1e-08
ADAM_WD = 0.01
ADAM_STEP = 10

VMEM_LIMIT = 56 * 1024 * 1024
MESH = pl.DeviceIdType.MESH

NT = (((1,), (1,)), ((), ()))
TN = (((0,), (0,)), ((), ()))


def _params(*sem):
    return pltpu.CompilerParams(dimension_semantics=sem, vmem_limit_bytes=VMEM_LIMIT)


def _resident(shape):
    return pl.BlockSpec(shape, lambda *_: (0,) * len(shape), pipeline_mode=pl.Buffered(1))


def _sigmoid(v):
    return 0.5 * jnp.tanh(0.5 * v) + 0.5


def _rstd(v):
    return lax.rsqrt(jnp.mean(v * v, axis=-1, keepdims=True) + NORM_EPS)


def _rms_bwd(dy, v, rstd, g):
    vhat = v * rstd
    t = dy * g
    return rstd * (t - vhat * jnp.mean(t * vhat, axis=-1, keepdims=True)), dy * vhat


def _taps(z, cw):
    return cw[2:3] * z + cw[1:2] * pltpu.roll(z, 1, 0) + cw[0:1] * pltpu.roll(z, 2, 0)


def _causal_conv(z, prev, cw):
    edge = _taps(jnp.concatenate([prev, z[0:8]], axis=0), cw)
    return jnp.concatenate([edge[8:16], _taps(z, cw)[8:]], axis=0)


def _rows_after(z, nxt):
    n = z.shape[0]
    edge = jnp.concatenate([z[n - 8:n], nxt], axis=0)
    return tuple(jnp.concatenate([pltpu.roll(z, n - k, 0)[:n - 8], pltpu.roll(edge, 16 - k, 0)[0:8]], axis=0)
                 for k in (1, 2))


def _inproj_fwd(x, g1, w_in, b_in, tm, comm=None):
    s = x.shape[0]

    def body(x_ref, g_ref, w_ref, b_ref, xn_ref, qkv_ref, c3_ref, gt_ref):
        xf = x_ref[...]
        xn = (xf * _rstd(xf) * g_ref[...]).astype(BF16)
        xn_ref[...] = xn

        def seg(a, b):
            return lax.dot_general(xn, w_ref[a:b, :], NT, preferred_element_type=F32) + b_ref[:, a:b]

        qkv_ref[...] = seg(0, QKV_W).astype(BF16)
        c3_ref[...] = seg(QKV_W, QKV_W + C3_W).astype(BF16)
        gt_ref[...] = seg(QKV_W + C3_W, IN_W).astype(BF16)

    row = lambda w: pl.BlockSpec((tm, w), lambda i: (i, 0))
    return _call(
        comm, body, name="inproj_fwd", grid=(s // tm,),
        in_specs=[row(D_MODEL), _resident((1, D_MODEL)), _resident((IN_W, D_MODEL)), _resident((1, IN_W))],
        out_specs=[row(D_MODEL), row(QKV_W), row(C3_W), row(GATES_W)],
        out_shape=[jax.ShapeDtypeStruct((s, D_MODEL), BF16), jax.ShapeDtypeStruct((s, QKV_W), BF16),
                   jax.ShapeDtypeStruct((s, C3_W), BF16), jax.ShapeDtypeStruct((s, GATES_W), BF16)],
        compiler_params=_params("parallel"),
    )(x, g1, w_in, b_in)


def _attn_bias():
    qi = (jnp.arange(GROUP * BLOCK) % BLOCK)[:, None]
    kj = jnp.arange(2 * BLOCK)[None, :]
    band = (kj > qi) & (kj <= qi + BLOCK)
    return jnp.stack([jnp.where(band & (kj >= BLOCK), 0.0, NEG), jnp.where(band, 0.0, NEG)]).astype(F32)


def _attn_bias_spec():
    return pl.BlockSpec((None, GROUP * BLOCK, 2 * BLOCK), lambda i: (jnp.minimum(i, 1), 0, 0))


def _sink_column(sk_ref, h):
    rows = lax.broadcasted_iota(jnp.int32, (GROUP * BLOCK, 1), 0)
    col = jnp.full((GROUP * BLOCK, 1), sk_ref[h * GROUP], F32)
    for g in range(1, GROUP):
        col = jnp.where(rows >= g * BLOCK, sk_ref[h * GROUP + g], col)
    return col


def _stack_heads(t, h):
    return jnp.concatenate(
        [t[:, (h * GROUP + g) * HEAD_DIM:(h * GROUP + g + 1) * HEAD_DIM] for g in range(GROUP)], axis=0)


def _unstack_heads(per_kv):
    return jnp.concatenate(
        [t[g * BLOCK:(g + 1) * BLOCK] for t in per_kv for g in range(GROUP)], axis=1)


def _attn_specs(nb):
    cur = lambda i: jnp.minimum(i, nb - 1)
    prev = lambda i: jnp.maximum(jnp.minimum(i, nb - 1) - 1, 0)
    q = pl.BlockSpec((BLOCK, ATTN_W), lambda i: (cur(i), 0))
    kp = pl.BlockSpec((BLOCK, KV_W), lambda i: (prev(i), ATTN_W // KV_W))
    kc = pl.BlockSpec((BLOCK, KV_W), lambda i: (cur(i), ATTN_W // KV_W))
    vp = pl.BlockSpec((BLOCK, KV_W), lambda i: (prev(i), ATTN_W // KV_W + 1))
    vc = pl.BlockSpec((BLOCK, KV_W), lambda i: (cur(i), ATTN_W // KV_W + 1))
    return q, kp, kc, vp, vc


def _attn_fwd(qkv, sinks, comm=None):
    s = qkv.shape[0]
    nb = s // BLOCK

    def body(sk_ref, bias_ref, q_ref, kp_ref, kc_ref, vp_ref, vc_ref, o_ref):
        bias = bias_ref[...]
        q, kp, kc, vp, vc = q_ref[...], kp_ref[...], kc_ref[...], vp_ref[...], vc_ref[...]
        outs = []
        for h in range(N_KV_HEADS):
            hs = slice(h * HEAD_DIM, (h + 1) * HEAD_DIM)
            k2 = jnp.concatenate([kp[:, hs], kc[:, hs]], axis=0)
            v2 = jnp.concatenate([vp[:, hs], vc[:, hs]], axis=0)
            sc = lax.dot_general(_stack_heads(q, h), k2, NT, preferred_element_type=F32) * ATTN_SCALE + bias
            sink = _sink_column(sk_ref, h)
            m = jnp.maximum(jnp.max(sc, axis=1, keepdims=True), sink)
            p = jnp.exp(sc - m)
            den = jnp.sum(p, axis=1, keepdims=True) + jnp.exp(sink - m)
            outs.append(jnp.dot(p.astype(BF16), v2, preferred_element_type=F32) / den)
        o_ref[...] = _unstack_heads(outs).astype(BF16)

    return _call(
        comm, body, name="attn_fwd", grid=(nb,),
        in_specs=[pl.BlockSpec(memory_space=pltpu.SMEM), _attn_bias_spec(), *_attn_specs(nb)],
        out_specs=pl.BlockSpec((BLOCK, ATTN_W), lambda i: (i, 0)),
        out_shape=jax.ShapeDtypeStruct((s, ATTN_W), BF16),
        compiler_params=_params("parallel"),
    )(sinks, _attn_bias(), qkv, qkv, qkv, qkv, qkv)


def _mix_fwd(x, attn, c3, gates, conv_w, w_br, w_out, g2, tm, comm=None):
    s = x.shape[0]

    def body(x_ref, at_ref, c3_ref, gt_ref, cw_ref, wbr_ref, wo_ref, g_ref,
             conv_ref, a_ref, cv_ref, mg_ref, h1_ref, hn_ref, carry_ref):
        @pl.when(pl.program_id(0) == 0)
        def _():
            carry_ref[...] = jnp.zeros_like(carry_ref)

        c3v = c3_ref[...].astype(F32)
        cb, cc, cx = c3v[:, :CONV_W], c3v[:, CONV_W:2 * CONV_W], c3v[:, 2 * CONV_W:]
        z = cc * cx
        cz = _causal_conv(z, carry_ref[...], cw_ref[...])
        carry_ref[...] = z[tm - 8:tm]
        conv = (cb * cz).astype(BF16)
        conv_ref[...] = conv
        a = jnp.dot(at_ref[...], wbr_ref[:ATTN_W, :], preferred_element_type=F32)
        cv = jnp.dot(conv, wbr_ref[ATTN_W:, :], preferred_element_type=F32)
        a_ref[...] = a.astype(BF16)
        cv_ref[...] = cv.astype(BF16)
        gt = gt_ref[...].astype(F32)
        merged = (_sigmoid(gt[:, :D_MODEL]) * a + _sigmoid(gt[:, D_MODEL:]) * cv).astype(BF16)
        mg_ref[...] = merged
        h1 = x_ref[...] + jnp.dot(merged, wo_ref[...], preferred_element_type=F32)
        h1_ref[...] = h1
        hn_ref[...] = (h1 * _rstd(h1) * g_ref[...]).astype(BF16)

    row = lambda w: pl.BlockSpec((tm, w), lambda i: (i, 0))
    return _call(
        comm, body, name="mix_fwd", grid=(s // tm,),
        in_specs=[row(D_MODEL), row(ATTN_W), row(C3_W), row(GATES_W), _resident((3, CONV_W)),
                  _resident((ATTN_W + CONV_W, D_MODEL)), _resident((D_MODEL, D_MODEL)), _resident((1, D_MODEL))],
        out_specs=[row(CONV_W), row(D_MODEL), row(D_MODEL), row(D_MODEL), row(D_MODEL), row(D_MODEL)],
        out_shape=[jax.ShapeDtypeStruct((s, CONV_W), BF16), jax.ShapeDtypeStruct((s, D_MODEL), BF16),
                   jax.ShapeDtypeStruct((s, D_MODEL), BF16), jax.ShapeDtypeStruct((s, D_MODEL), BF16),
                   jax.ShapeDtypeStruct((s, D_MODEL), F32), jax.ShapeDtypeStruct((s, D_MODEL), BF16)],
        scratch_shapes=[pltpu.VMEM((8, CONV_W), F32)],
        compiler_params=_params("arbitrary"),
    )(x, attn, c3, gates, conv_w, w_br, w_out, g2)


def _ffn_fwd_loss(hn, h1, w_up, ffn_cw, w_down, g3, target, tm):
    s = hn.shape[0]

    def body(hn_ref, h1_ref, wu_ref, cw_ref, wd_ref, g_ref, t_ref,
             u_ref, up_ref, act_ref, dh2_ref, loss_ref, gfn_ref, carry_ref):
        @pl.when(pl.program_id(0) == 0)
        def _():
            carry_ref[...] = jnp.zeros_like(carry_ref)
            loss_ref[...] = jnp.zeros_like(loss_ref)
            gfn_ref[...] = jnp.zeros_like(gfn_ref)

        u = jnp.dot(hn_ref[...], wu_ref[...], preferred_element_type=F32)
        u_ref[...] = u.astype(BF16)
        up = _causal_conv(u, carry_ref[...], cw_ref[...])
        up_ref[...] = up
        carry_ref[...] = u[tm - 8:tm]
        gate, val = up[:, :D_FF], up[:, D_FF:]
        act = (gate * _sigmoid(gate) * val).astype(BF16)
        act_ref[...] = act
        h2 = h1_ref[...] + jnp.dot(act, wd_ref[...], preferred_element_type=F32)
        rstd = _rstd(h2)
        g = g_ref[...]
        err = h2 * rstd * g - t_ref[...]
        loss_ref[...] += jnp.sum(err * err) * (0.5 / D_MODEL)
        dh2, dg = _rms_bwd(err * (1.0 / D_MODEL), h2, rstd, g)
        dh2_ref[...] = dh2
        gfn_ref[...] += jnp.sum(dg, axis=0, keepdims=True)

    row = lambda w: pl.BlockSpec((tm, w), lambda i: (i, 0))
    acc = lambda w: pl.BlockSpec((1, w), lambda i: (0, 0))
    return pl.pallas_call(
        body, name="ffn_fwd_loss", grid=(s // tm,),
        in_specs=[row(D_MODEL), row(D_MODEL), _resident((D_MODEL, FF2)), _resident((3, FF2)),
                  _resident((D_FF, D_MODEL)), _resident((1, D_MODEL)), row(D_MODEL)],
        out_specs=[row(FF2), row(FF2), row(D_FF), row(D_MODEL), acc(128), acc(D_MODEL)],
        out_shape=[jax.ShapeDtypeStruct((s, FF2), BF16), jax.ShapeDtypeStruct((s, FF2), F32),
                   jax.ShapeDtypeStruct((s, D_FF), BF16),
                   jax.ShapeDtypeStruct((s, D_MODEL), F32), jax.ShapeDtypeStruct((1, 128), F32),
                   jax.ShapeDtypeStruct((1, D_MODEL), F32)],
        scratch_shapes=[pltpu.VMEM((8, FF2), F32)],
        compiler_params=_params("arbitrary"),
    )(hn, h1, w_up, ffn_cw, w_down, g3, target)


def _ffn_bwd(dh2, u, up, h1, w_up, ffn_cw, w_down, g2, tm):
    s = dh2.shape[0]
    nt = s // tm

    def body(dh2_ref, u_ref, up_ref, h1_ref, wu_ref, cw_ref, wd_ref, g_ref,
             du_ref, dh1_ref, gcw_ref, gg_ref, carry_ref):
        @pl.when(pl.program_id(0) == 0)
        def _():
            for ref in (carry_ref, gcw_ref, gg_ref):
                ref[...] = jnp.zeros_like(ref)

        dh2v = dh2_ref[...]
        dact = lax.dot_general(dh2v.astype(BF16), wd_ref[...], NT, preferred_element_type=F32)
        upv = up_ref[...]
        gate, val = upv[:, :D_FF], upv[:, D_FF:]
        sg = _sigmoid(gate)
        dval = dact * (gate * sg)
        dgate = dact * val * (sg * (1.0 + gate * (1.0 - sg)))
        dup = jnp.concatenate([dgate, dval], axis=1)
        dup1, dup2 = _rows_after(dup, carry_ref[...])
        carry_ref[...] = dup[0:8]
        u = u_ref[...].astype(F32)
        gcw_ref[2:3, :] += jnp.sum(dup * u, axis=0, keepdims=True)
        gcw_ref[1:2, :] += jnp.sum(dup1 * u, axis=0, keepdims=True)
        gcw_ref[0:1, :] += jnp.sum(dup2 * u, axis=0, keepdims=True)
        cw = cw_ref[...]
        du = (cw[2:3] * dup + cw[1:2] * dup1 + cw[0:1] * dup2).astype(BF16)
        du_ref[...] = du
        dhn = lax.dot_general(du, wu_ref[...], NT, preferred_element_type=F32)
        h1v = h1_ref[...]
        dh1, dg = _rms_bwd(dhn, h1v, _rstd(h1v), g_ref[...])
        dh1_ref[...] = dh2v + dh1
        gg_ref[...] += jnp.sum(dg, axis=0, keepdims=True)

    row = lambda w: pl.BlockSpec((tm, w), lambda i: (nt - 1 - i, 0))
    return pl.pallas_call(
        body, name="ffn_bwd", grid=(nt,),
        in_specs=[row(D_MODEL), row(FF2), row(FF2),
                  row(D_MODEL), _resident((D_MODEL, FF2)), _resident((3, FF2)), _resident((D_FF, D_MODEL)),
                  _resident((1, D_MODEL))],
        out_specs=[row(FF2), row(D_MODEL), pl.BlockSpec((3, FF2), lambda i: (0, 0)),
                   pl.BlockSpec((1, D_MODEL), lambda i: (0, 0))],
        out_shape=[jax.ShapeDtypeStruct((s, FF2), BF16), jax.ShapeDtypeStruct((s, D_MODEL), F32),
                   jax.ShapeDtypeStruct((3, FF2), F32), jax.ShapeDtypeStruct((1, D_MODEL), F32)],
        scratch_shapes=[pltpu.VMEM((8, FF2), F32)],
        compiler_params=_params("arbitrary"),
    )(dh2, u, up, h1, w_up, ffn_cw, w_down, g2)


def _mix_bwd(dh1, gates, a, cv, c3, attn, conv, merged, conv_w, w_br, w_out, tm, comm=None):
    s = dh1.shape[0]
    nt = s // tm
    halo = 16

    def body(dh1_ref, gt_ref, a_ref, cv_ref, c3_ref, ch_ref, at_ref, cn_ref, mg_ref, cw_ref, wbr_ref,
             wo_ref, dat_ref, dc3_ref, dgt_ref, gcw_ref, gbr_ref, gout_ref, carry_ref, br_acc, out_acc):
        i = pl.program_id(0)

        @pl.when(i == 0)
        def _():
            for ref in (carry_ref, gcw_ref, br_acc, out_acc):
                ref[...] = jnp.zeros_like(ref)

        dh1v = dh1_ref[...].astype(BF16)
        out_acc[...] += lax.dot_general(mg_ref[...], dh1v, TN, preferred_element_type=F32)
        dm = lax.dot_general(dh1v, wo_ref[...], NT, preferred_element_type=F32)
        gt = gt_ref[...].astype(F32)
        sa, sc = _sigmoid(gt[:, :D_MODEL]), _sigmoid(gt[:, D_MODEL:])
        da = (dm * sa).astype(BF16)
        dcv = (dm * sc).astype(BF16)
        br_acc[:ATTN_W, :] += lax.dot_general(at_ref[...], da, TN, preferred_element_type=F32)
        br_acc[ATTN_W:, :] += lax.dot_general(cn_ref[...], dcv, TN, preferred_element_type=F32)
        dgt_ref[...] = jnp.concatenate(
            [dm * a_ref[...].astype(F32) * (sa * (1.0 - sa)), dm * cv_ref[...].astype(F32) * (sc * (1.0 - sc))],
            axis=1).astype(BF16)
        dat_ref[...] = lax.dot_general(da, wbr_ref[:ATTN_W, :], NT, preferred_element_type=F32).astype(BF16)
        dconv = lax.dot_general(dcv, wbr_ref[ATTN_W:, :], NT, preferred_element_type=F32)
        c3v = c3_ref[...].astype(F32)
        cb, cc, cx = c3v[:, :CONV_W], c3v[:, CONV_W:2 * CONV_W], c3v[:, 2 * CONV_W:]
        z = cc * cx
        chv = ch_ref[...].astype(F32)[halo - 8:halo] * (i < nt - 1).astype(F32)
        zh = chv[:, CONV_W:2 * CONV_W] * chv[:, 2 * CONV_W:]
        cw = cw_ref[...]
        cz = _causal_conv(z, zh, cw)
        dcz = dconv * cb
        dcz1, dcz2 = _rows_after(dcz, carry_ref[...])
        carry_ref[...] = dcz[0:8]
        gcw_ref[2:3, :] += jnp.sum(dcz * z, axis=0, keepdims=True)
        gcw_ref[1:2, :] += jnp.sum(dcz1 * z, axis=0, keepdims=True)
        gcw_ref[0:1, :] += jnp.sum(dcz2 * z, axis=0, keepdims=True)
        dz = cw[2:3] * dcz + cw[1:2] * dcz1 + cw[0:1] * dcz2
        dc3_ref[...] = jnp.concatenate([dconv * cz, dz * cx, dz * cc], axis=1).astype(BF16)

        @pl.when(i == nt - 1)
        def _():
            gbr_ref[...] = br_acc[...].astype(BF16)
            gout_ref[...] = out_acc[...].astype(BF16)

    row = lambda w: pl.BlockSpec((tm, w), lambda i: (nt - 1 - i, 0))
    return _call(
        comm, body, name="mix_bwd", grid=(nt,),
        in_specs=[row(D_MODEL), row(GATES_W), row(D_MODEL), row(D_MODEL), row(C3_W),
                  pl.BlockSpec((halo, C3_W), lambda i: (jnp.maximum((nt - 1 - i) * (tm // halo) - 1, 0), 0)),
                  row(ATTN_W), row(CONV_W), row(D_MODEL), _resident((3, CONV_W)),
                  _resident((ATTN_W + CONV_W, D_MODEL)), _resident((D_MODEL, D_MODEL))],
        out_specs=[row(ATTN_W), row(C3_W), row(GATES_W), pl.BlockSpec((3, CONV_W), lambda i: (0, 0)),
                   _resident((ATTN_W + CONV_W, D_MODEL)), _resident((D_MODEL, D_MODEL))],
        out_shape=[jax.ShapeDtypeStruct((s, ATTN_W), BF16), jax.ShapeDtypeStruct((s, C3_W), BF16),
                   jax.ShapeDtypeStruct((s, GATES_W), BF16), jax.ShapeDtypeStruct((3, CONV_W), F32),
                   jax.ShapeDtypeStruct((ATTN_W + CONV_W, D_MODEL), BF16),
                   jax.ShapeDtypeStruct((D_MODEL, D_MODEL), BF16)],
        scratch_shapes=[pltpu.VMEM((8, CONV_W), F32), pltpu.VMEM((ATTN_W + CONV_W, D_MODEL), F32),
                        pltpu.VMEM((D_MODEL, D_MODEL), F32)],
        compiler_params=_params("arbitrary"),
    )(dh1, gates, a, cv, c3, c3, attn, conv, merged, conv_w, w_br, w_out)


def _attn_bwd(qkv, sinks, o, do, comm=None):
    s = qkv.shape[0]
    nb = s // BLOCK

    def body(sk_ref, bias_ref, q_ref, kp_ref, kc_ref, vp_ref, vc_ref, o_ref, do_ref,
             dq_ref, dk_ref, dv_ref, dsk_ref, ck_ref, cvv_ref):
        i = pl.program_id(0)

        @pl.when(i == 0)
        def _():
            ck_ref[...] = jnp.zeros_like(ck_ref)
            cvv_ref[...] = jnp.zeros_like(cvv_ref)
            dsk_ref[...] = jnp.zeros_like(dsk_ref)

        @pl.when(i < nb)
        def _():
            bias = bias_ref[...]
            q, kp, kc, vp, vc = q_ref[...], kp_ref[...], kc_ref[...], vp_ref[...], vc_ref[...]
            ov, dov = o_ref[...], do_ref[...]
            dqs, dks, dvs = [], [], []
            for h in range(N_KV_HEADS):
                hs = slice(h * HEAD_DIM, (h + 1) * HEAD_DIM)
                k2 = jnp.concatenate([kp[:, hs], kc[:, hs]], axis=0)
                v2 = jnp.concatenate([vp[:, hs], vc[:, hs]], axis=0)
                qg, og, dog = _stack_heads(q, h), _stack_heads(ov, h), _stack_heads(dov, h)
                sc = lax.dot_general(qg, k2, NT, preferred_element_type=F32) * ATTN_SCALE + bias
                sink = _sink_column(sk_ref, h)
                m = jnp.maximum(jnp.max(sc, axis=1, keepdims=True), sink)
                p = jnp.exp(sc - m)
                psink = jnp.exp(sink - m)
                inv = 1.0 / (jnp.sum(p, axis=1, keepdims=True) + psink)
                p = p * inv
                delta = jnp.sum(dog.astype(F32) * og.astype(F32), axis=1, keepdims=True)
                dp = lax.dot_general(dog, v2, NT, preferred_element_type=F32)
                ds = (p * (dp - delta)).astype(BF16)
                dqs.append(jnp.dot(ds, k2, preferred_element_type=F32) * ATTN_SCALE)
                dks.append(lax.dot_general(ds, qg, TN, preferred_element_type=F32) * ATTN_SCALE)
                dvs.append(lax.dot_general(p.astype(BF16), dog, TN, preferred_element_type=F32))
                dsink = -(psink * inv * delta)
                for g in range(GROUP):
                    r = h * GROUP + g
                    dsk_ref[r:r + 1, :] += jnp.sum(dsink[g * BLOCK:(g + 1) * BLOCK])
            dq_ref[...] = _unstack_heads(dqs).astype(BF16)
            dk2 = jnp.concatenate(dks, axis=1)
            dv2 = jnp.concatenate(dvs, axis=1)
            dk_ref[...] = (ck_ref[...] + dk2[:BLOCK]).astype(BF16)
            dv_ref[...] = (cvv_ref[...] + dv2[:BLOCK]).astype(BF16)
            ck_ref[...] = dk2[BLOCK:]
            cvv_ref[...] = dv2[BLOCK:]

        @pl.when(i == nb)
        def _():
            dk_ref[...] = ck_ref[...].astype(BF16)
            dv_ref[...] = cvv_ref[...].astype(BF16)

    cur = lambda i: jnp.minimum(i, nb - 1)
    done = lambda i: jnp.maximum(i - 1, 0)
    return _call(
        comm, body, name="attn_bwd", grid=(nb + 1,),
        in_specs=[pl.BlockSpec(memory_space=pltpu.SMEM), _attn_bias_spec(), *_attn_specs(nb),
                  pl.BlockSpec((BLOCK, ATTN_W), lambda i: (cur(i), 0)),
                  pl.BlockSpec((BLOCK, ATTN_W), lambda i: (cur(i), 0))],
        out_specs=[pl.BlockSpec((BLOCK, ATTN_W), lambda i: (cur(i), 0)),
                   pl.BlockSpec((BLOCK, KV_W), lambda i: (done(i), 0)),
                   pl.BlockSpec((BLOCK, KV_W), lambda i: (done(i), 0)),
                   pl.BlockSpec((N_HEADS, 128), lambda i: (0, 0))],
        out_shape=[jax.ShapeDtypeStruct((s, ATTN_W), BF16), jax.ShapeDtypeStruct((s, KV_W), BF16),
                   jax.ShapeDtypeStruct((s, KV_W), BF16), jax.ShapeDtypeStruct((N_HEADS, 128), F32)],
        scratch_shapes=[pltpu.VMEM((BLOCK, KV_W), F32), pltpu.VMEM((BLOCK, KV_W), F32)],
        compiler_params=_params("arbitrary"),
    )(sinks, _attn_bias(), qkv, qkv, qkv, qkv, qkv, o, do)


def _inproj_bwd(dq, dk, dv, dc3, dgt, w_in, x, xn, dh1, g1, tm):
    s = x.shape[0]
    nt = s // tm

    def body(dq_ref, dk_ref, dv_ref, dc3_ref, dgt_ref, w_ref, x_ref, xn_ref, dh1_ref, g_ref,
             dx_ref, gw_ref, gb_ref, gg_ref, acc_ref):
        i = pl.program_id(0)

        @pl.when(i == 0)
        def _():
            for ref in (gb_ref, gg_ref, acc_ref):
                ref[...] = jnp.zeros_like(ref)

        dp = jnp.concatenate([dq_ref[...], dk_ref[...], dv_ref[...], dc3_ref[...], dgt_ref[...]], axis=1)
        acc_ref[...] += lax.dot_general(dp, xn_ref[...], TN, preferred_element_type=F32)
        gb_ref[...] += jnp.sum(dp.astype(F32), axis=0, keepdims=True)
        dxn = jnp.dot(dp, w_ref[...], preferred_element_type=F32)
        xf = x_ref[...]
        dx, dg = _rms_bwd(dxn, xf, _rstd(xf), g_ref[...])
        dx_ref[...] = dh1_ref[...] + dx
        gg_ref[...] += jnp.sum(dg, axis=0, keepdims=True)

        @pl.when(i == nt - 1)
        def _():
            gw_ref[...] = acc_ref[...].astype(BF16)

    row = lambda w: pl.BlockSpec((tm, w), lambda i: (i, 0))
    acc = lambda w: pl.BlockSpec((1, w), lambda i: (0, 0))
    return pl.pallas_call(
        body, name="inproj_bwd", grid=(nt,),
        in_specs=[row(ATTN_W), row(KV_W), row(KV_W), row(C3_W), row(GATES_W), _resident((IN_W, D_MODEL)),
                  row(D_MODEL), row(D_MODEL), row(D_MODEL), _resident((1, D_MODEL))],
        out_specs=[row(D_MODEL), _resident((IN_W, D_MODEL)), acc(IN_W), acc(D_MODEL)],
        out_shape=[jax.ShapeDtypeStruct((s, D_MODEL), F32), jax.ShapeDtypeStruct((IN_W, D_MODEL), BF16),
                   jax.ShapeDtypeStruct((1, IN_W), F32), jax.ShapeDtypeStruct((1, D_MODEL), F32)],
        scratch_shapes=[pltpu.VMEM((IN_W, D_MODEL), F32)],
        compiler_params=_params("arbitrary"),
    )(dq, dk, dv, dc3, dgt, w_in, x, xn, dh1, g1)


def _wgrad(a, b, bm, bn, bk, name, comm=None):
    s, m = a.shape
    n = b.shape[1]
    nk = s // bk

    def body(a_ref, b_ref, o_ref, acc_ref):
        k = pl.program_id(2)

        @pl.when(k == 0)
        def _():
            acc_ref[...] = jnp.zeros_like(acc_ref)

        acc_ref[...] += lax.dot_general(a_ref[...].astype(BF16), b_ref[...].astype(BF16), TN,
                                        preferred_element_type=F32)

        @pl.when(k == nk - 1)
        def _():
            o_ref[...] = acc_ref[...].astype(BF16)

    return _call(
        comm, body, name=name, grid=(m // bm, n // bn, nk),
        in_specs=[pl.BlockSpec((bk, bm), lambda i, j, k: (k, i)), pl.BlockSpec((bk, bn), lambda i, j, k: (k, j))],
        out_specs=pl.BlockSpec((bm, bn), lambda i, j, k: (i, j)),
        out_shape=jax.ShapeDtypeStruct((m, n), BF16),
        scratch_shapes=[pltpu.VMEM((bm, bn), F32)],
        compiler_params=_params("parallel", "parallel", "arbitrary"),
    )(a, b)


class _Carry:
    def __init__(self, jobs, reads=None, bufs=None, fresh=None):
        self.jobs, self.reads, self.bufs, self.fresh = jobs, reads or {}, bufs or {}, fresh or {}
        self.out = {}


class _Job:
    def __init__(self, n_sems, plan):
        self.n_sems, self.plan = n_sems, plan


def _plan_all(jobs, hbm, send, recv):
    pos = _position()
    starts, waits, base = [], [], 0
    for job in jobs:
        s, w = job.plan(hbm, pos, send, recv, base)
        starts, waits, base = starts + s, waits + w, base + job.n_sems
    return starts, waits


def _call(comm, body, **kw):
    if comm is None:
        return pl.pallas_call(body, **kw)
    grid = kw["grid"]
    single = not isinstance(kw["out_shape"], (list, tuple))
    out_shape = [kw["out_shape"]] if single else list(kw["out_shape"])
    out_specs = [kw["out_specs"]] if single else list(kw["out_specs"])
    in_specs = list(kw["in_specs"])
    scratch = list(kw.get("scratch_shapes", ()))
    r_names, b_names, f_names = list(comm.reads), list(comm.bufs), list(comm.fresh)
    n_args, n_out, n_scr = len(in_specs), len(out_shape), len(scratch)
    n_sems = sum(j.n_sems for j in comm.jobs)

    def wrapped(*refs):
        k = n_args
        hbm = dict(zip(r_names, refs[k:k + len(r_names)]))
        k += len(r_names) + len(b_names)
        outs = refs[k:k + n_out]
        k += n_out
        hbm.update(zip(b_names + f_names, refs[k:k + len(b_names) + len(f_names)]))
        k += len(b_names) + len(f_names)
        send, recv = refs[k + n_scr:]
        starts, waits = _plan_all(comm.jobs, hbm, send, recv)
        ids = [pl.program_id(a) for a in range(len(grid))]
        first = functools.reduce(jnp.logical_and, [i == 0 for i in ids])
        last = functools.reduce(jnp.logical_and, [i == g - 1 for i, g in zip(ids, grid)])

        @pl.when(first)
        def _():
            for cp in starts:
                cp.start()

        body(*refs[:n_args], *outs, *refs[k:k + n_scr])

        @pl.when(last)
        def _():
            for cp in waits:
                cp.wait_recv()
            for cp in starts:
                cp.wait_send()

    sems = pltpu.SemaphoreType.DMA((n_sems,))
    held = [jax.ShapeDtypeStruct(a.shape, a.dtype) for a in comm.bufs.values()] + list(comm.fresh.values())
    call = pl.pallas_call(
        wrapped, name=kw["name"], grid=grid,
        in_specs=in_specs + [_ANY] * (len(r_names) + len(b_names)),
        out_specs=out_specs + [_ANY] * len(held),
        out_shape=out_shape + held,
        input_output_aliases={n_args + len(r_names) + i: n_out + i for i in range(len(b_names))},
        scratch_shapes=scratch + [sems, sems],
        compiler_params=_params(*["arbitrary"] * len(grid)),
    )

    def run(*args):
        res = call(*args, *comm.reads.values(), *comm.bufs.values())
        comm.out = dict(zip(b_names + f_names, res[n_out:]))
        return res[0] if single else res[:n_out]

    return run


def _exchange(name, phases, reads=None, bufs=None, fresh=None):
    comm = _Carry([j for ph in phases for j in ph], reads, bufs, fresh)
    r_names, b_names, f_names = list(comm.reads), list(comm.bufs), list(comm.fresh)
    n_sems = sum(j.n_sems for j in comm.jobs)

    def body(*refs):
        hbm = dict(zip(r_names, refs[:len(r_names)]))
        k = len(r_names) + len(b_names)
        hbm.update(zip(b_names + f_names, refs[k:k + len(b_names) + len(f_names)]))
        send, recv = refs[-2:]
        pos = _position()
        started, base = [], 0
        for ph in phases:
            waits = []
            for job in ph:
                s, w = job.plan(hbm, pos, send, recv, base)
                base += job.n_sems
                for cp in s:
                    cp.start()
                started, waits = started + s, waits + w
            for cp in waits:
                cp.wait_recv()
        for cp in started:
            cp.wait_send()

    sems = pltpu.SemaphoreType.DMA((n_sems,))
    held = [jax.ShapeDtypeStruct(a.shape, a.dtype) for a in comm.bufs.values()] + list(comm.fresh.values())
    res = pl.pallas_call(
        body, name=name, in_specs=[_ANY] * (len(r_names) + len(b_names)), out_specs=[_ANY] * len(held),
        out_shape=held, input_output_aliases={len(r_names) + i: i for i in range(len(b_names))},
        scratch_shapes=[sems, sems],
    )(*comm.reads.values(), *comm.bufs.values())
    return dict(zip(b_names + f_names, res))


_HBM = pl.BlockSpec(memory_space=pltpu.HBM)
_SEM = pl.BlockSpec(memory_space=pltpu.SEMAPHORE)
_EFFECT = pltpu.SideEffectType.DATAFLOW_SIDE_EFFECTING


def _start_exchanges(name, groups):
    names = [list(arrays) for _, arrays in groups]
    first = [sum(len(ns) for ns in names[:g]) for g in range(len(groups))]
    n, ng = sum(len(ns) for ns in names), len(groups)

    def body(*refs):
        for g, (jobs, _) in enumerate(groups):
            hbm = dict(zip(names[g], refs[first[g]:first[g] + len(names[g])]))
            for cp in _plan_all(jobs, hbm, refs[n + 2 * g], refs[n + 2 * g + 1])[0]:
                cp.start()
        refs[-1][...] = jnp.zeros_like(refs[-1])

    given = [pltpu.with_memory_space_constraint(
        a if isinstance(a, jax.Array) else lax.empty(a.shape, a.dtype), pltpu.HBM)
        for _, arrays in groups for a in arrays.values()]
    sems = [pltpu.SemaphoreType.DMA((sum(j.n_sems for j in jobs),)) for jobs, _ in groups for _ in range(2)]
    res = pl.pallas_call(
        body, name=name,
        out_shape=(*sems, *[pltpu.HBM(a.shape, a.dtype) for a in given], jax.ShapeDtypeStruct((8, 128), F32)),
        in_specs=[_HBM] * n, out_specs=(*[_SEM] * (2 * ng), *[_HBM] * n, pl.BlockSpec(memory_space=pltpu.VMEM)),
        input_output_aliases={i: 2 * ng + i for i in range(n)},
        compiler_params=pltpu.CompilerParams(has_side_effects=_EFFECT),
    )(*given)
    held = res[2 * ng:2 * ng + n]
    states = [(names[g], groups[g][0], res[2 * g], res[2 * g + 1], held[first[g]:first[g] + len(names[g])])
              for g in range(ng)]
    return states, res[-1]


def _start_exchange(name, jobs, arrays):
    states, token = _start_exchanges(name, [(jobs, arrays)])
    return states[0], token


def _finish_exchange(name, state, after):
    names, jobs, send_sem, recv_sem, held = state
    n = len(names)

    def body(*refs):
        hbm = dict(zip(names, refs[:n]))
        send, recv = refs[n:n + 2]
        starts, waits = _plan_all(jobs, hbm, send, recv)
        for cp in waits:
            cp.wait_recv()
        for cp in starts:
            cp.wait_send()

    res = pl.pallas_call(
        body, name=name, out_shape=tuple(pltpu.HBM(a.shape, a.dtype) for a in held),
        in_specs=[_HBM] * n + [_SEM, _SEM, _ANY], out_specs=tuple([_HBM] * n),
        input_output_aliases={i: i for i in range(n)},
        compiler_params=pltpu.CompilerParams(has_side_effects=_EFFECT),
    )(*held, send_sem, recv_sem, after)
    return dict(zip(names, res))


def _row_tile(rows, bytes_per_row):
    best = 16
    for t in range(16, rows + 1, 16):
        if rows % t == 0 and t * bytes_per_row <= 9 * 1024 * 1024:
            best = t
    return best


def _rowwise(fn, ins, out_dtypes, name, after=None):
    rows, cols = ins[0].shape
    per_row = sum(cols * a.dtype.itemsize for a in ins) + sum(cols * jnp.dtype(d).itemsize for d in out_dtypes)
    tr = _row_tile(rows, per_row)
    n_in = len(ins)

    def body(*refs):
        outs = fn(*[r[...] for r in refs[:n_in]])
        for o_ref, o in zip(refs[-len(out_dtypes):], outs):
            o_ref[...] = o.astype(o_ref.dtype)

    tile = pl.BlockSpec((tr, cols), lambda i: (i, 0))
    behind = [] if after is None else [after]
    return pl.pallas_call(
        body, name=name, grid=(rows // tr,),
        in_specs=[tile] * n_in + [pl.BlockSpec((8, 128), lambda i: (0, 0))] * len(behind),
        out_specs=[tile] * len(out_dtypes),
        out_shape=[jax.ShapeDtypeStruct((rows, cols), d) for d in out_dtypes],
        compiler_params=_params("parallel"),
    )(*ins, *behind)


def _tiled(fn, name, grid, pos, ins, outs):
    n_in = len(ins)

    def body(pos_ref, *refs):
        res = fn(*[r[...] for r in refs[:n_in]])
        for o_ref, o in zip(refs[n_in:], res):
            o_ref[...] = o.astype(o_ref.dtype)

    return pl.pallas_call(
        body, name=name,
        grid_spec=pltpu.PrefetchScalarGridSpec(
            num_scalar_prefetch=1, grid=grid,
            in_specs=[pl.BlockSpec(bs, im) for _, bs, im in ins],
            out_specs=[pl.BlockSpec(bs, im) for _, _, bs, im in outs]),
        out_shape=[jax.ShapeDtypeStruct(s, d) for s, d, _, _ in outs],
        compiler_params=_params("parallel"),
    )(pos, *[a for a, _, _ in ins])


def _adamw(w, g, m, v):
    m = ADAM_B1 * m + (1.0 - ADAM_B1) * g
    v = ADAM_B2 * v + (1.0 - ADAM_B2) * (g * g)
    m_hat = m / (1.0 - ADAM_B1 ** ADAM_STEP)
    v_hat = v / (1.0 - ADAM_B2 ** ADAM_STEP)
    return -ADAM_LR * (m_hat / (jnp.sqrt(v_hat) + ADAM_EPS) + ADAM_WD * w), m, v


def _adamw_small(params):
    n = len(params)

    def body(*refs):
        for k in range(n):
            w, g, m, v = (r[...] for r in refs[4 * k:4 * k + 4])
            for o_ref, o in zip(refs[4 * n + 3 * k:4 * n + 3 * k + 3], _adamw(w, g, m, v)):
                o_ref[...] = o

    flat = [a for p in params for a in p]
    return pl.pallas_call(
        body, name="adamw_small",
        out_shape=[jax.ShapeDtypeStruct(p[0].shape, F32) for p in params for _ in range(3)],
    )(*flat)


class _Layout:
    def __init__(self, rows, cols, stacked):
        self.rows, self.cols, self.stacked = rows, cols, stacked

    def whole(self, rows=None):
        r = self.rows if rows is None else rows
        return (N_CHIPS, r, self.cols) if self.stacked else (r, N_CHIPS * self.cols)

    def part_rows(self, h, q=0, nq=1):
        n = self.rows // 2 // nq
        return pl.ds(pl.multiple_of(h * (self.rows // 2) + q * n, 16), n)

    def half_rows(self, h):
        return self.part_rows(h)

    def block(self, ref, p, rows=slice(None)):
        if self.stacked:
            return ref.at[p, rows, :]
        return ref.at[rows, pl.ds(pl.multiple_of(p * self.cols, 128), self.cols)]

    def all_chips(self, ref, rows):
        return ref.at[:, rows, :] if self.stacked else ref.at[rows, :]


BIG = (
    _Layout(IN_SHARD, D_MODEL, True),
    _Layout(ATTN_W + CONV_W, D_MODEL // N_CHIPS, False),
    _Layout(D_MODEL // N_CHIPS, D_MODEL, True),
    _Layout(D_MODEL, FF2 // N_CHIPS, False),
    _Layout(D_FF // N_CHIPS, D_MODEL, True),
)
N_BIG = len(BIG)
_ANY = pl.BlockSpec(memory_space=pl.ANY)


def _position():
    x, y, c = lax.axis_index("x"), lax.axis_index("y"), lax.axis_index("c")
    return x, y, c, 2 * x + y


def _core_of_chip(p, c):
    return (p >> 1, p & 1, c)


def _place_cast(shard, lay, pos, name, after=None):
    rows, cols = shard.shape
    tr = _row_tile(rows, cols * 6)
    if lay.stacked:
        out = (lay.whole(), BF16, (None, tr, cols), lambda i, pos: (pos[0], i, 0))
    else:
        out = (lay.whole(), BF16, (tr, cols), lambda i, pos: (i, pos[0]))
    ins = [(shard, (tr, cols), lambda i, pos: (i, 0))]
    if after is not None:
        ins.append((after, (8, 128), lambda i, pos: (0, 0)))
    return _tiled(lambda a, *_: (a,), name, (rows // tr,), pos, ins, [out])[0]


def _place_cast_pair(top, bottom, lay, pos, name, after=None):
    rows, cols = top.shape
    ins = [(top, (rows, cols), lambda i, pos: (0, 0)), (bottom, (rows, cols), lambda i, pos: (0, 0))]
    if after is not None:
        ins.append((after, (8, 128), lambda i, pos: (0, 0)))
    return _tiled(lambda a, b, *_: (jnp.concatenate([a, b], axis=0),), name, (1,), pos, ins,
                  [(lay.whole(), BF16, (2 * rows, cols), lambda i, pos: (0, pos[0]))])[0]


def _adamw_pair(top, bottom, g, after=None):
    rows = top[0].shape[0]

    def body(*refs):
        (wa, ma, va, wb, mb, vb, g_ref), outs = refs[:7], refs[-8:]
        for (w, m, v), gg, o in (((wa, ma, va), g_ref[:rows], outs[:4]), ((wb, mb, vb), g_ref[rows:], outs[4:])):
            for o_ref, val in zip(o, (gg, *_adamw(w[...], gg, m[...], v[...]))):
                o_ref[...] = val

    behind = [] if after is None else [after[0:8, 0:128]]
    res = pl.pallas_call(
        body, name="adamw_w_br", out_shape=[jax.ShapeDtypeStruct(top[0].shape, F32)] * 8,
    )(*top, *bottom, g, *behind)
    return res[:4], res[4:]


def _remote(src, dst, send, recv, k, device):
    return pltpu.make_async_remote_copy(src_ref=src, dst_ref=dst, send_sem=send.at[k], recv_sem=recv.at[k],
                                        device_id=device, device_id_type=MESH)


def _arrival(dst, send, recv, k, me):
    return _remote(dst, dst, send, recv, k, me)


def _gather_ici(lay, name, q=0, nq=1):
    def plan(hbm, pos, send, recv, base):
        x, y, c, me = pos
        rows = lay.part_rows(c, q, nq)
        mine = lay.block(hbm[name], me, rows)
        starts = [_remote(mine, mine, send, recv, base + d - 1, _core_of_chip(me ^ d, c)) for d in (1, 2, 3)]
        waits = [_arrival(lay.block(hbm[name], me ^ d, rows), send, recv, base + d - 1, (x, y, c)) for d in (1, 2, 3)]
        return starts, waits
    return _Job(3, plan)


def _gather_d2d(lay, name, q=0, nq=1):
    def plan(hbm, pos, send, recv, base):
        x, y, c, me = pos
        starts, waits = [], []
        for d in (1, 2, 3):
            got = lay.block(hbm[name], me ^ d, lay.part_rows(c, q, nq))
            starts.append(_remote(got, got, send, recv, base + d - 1, (x, y, 1 - c)))
            waits.append(_arrival(lay.block(hbm[name], me ^ d, lay.part_rows(1 - c, q, nq)), send, recv, base + d - 1,
                                  (x, y, c)))
        return starts, waits
    return _Job(3, plan)


def _rs_pair(lay, grad, theirs):
    def plan(hbm, pos, send, recv, base):
        x, y, c, _ = pos
        out = _remote(lay.all_chips(hbm[grad], lay.half_rows(1 - c)), hbm[theirs], send, recv, base, (x, y, 1 - c))
        return [out], [_arrival(hbm[theirs], send, recv, base, (x, y, c))]
    return _Job(1, plan)


def _rs_chips(lay, sums, slots):
    def plan(hbm, pos, send, recv, base):
        x, y, c, me = pos
        starts = [_remote(lay.block(hbm[sums], me ^ d), hbm[slots].at[me], send, recv, base + d - 1,
                          _core_of_chip(me ^ d, c)) for d in (1, 2, 3)]
        waits = [_arrival(hbm[slots].at[me ^ d], send, recv, base + d - 1, (x, y, c)) for d in (1, 2, 3)]
        return starts, waits
    return _Job(3, plan)


def _rs_share(lay, shard):
    def plan(hbm, pos, send, recv, base):
        x, y, c, _ = pos
        mine = hbm[shard].at[lay.half_rows(c), :]
        other = hbm[shard].at[lay.half_rows(1 - c), :]
        return [_remote(mine, mine, send, recv, base, (x, y, 1 - c))], [_arrival(other, send, recv, base, (x, y, c))]
    return _Job(1, plan)


def _slots_shape(lay):
    return jax.ShapeDtypeStruct((N_CHIPS, lay.rows // 2, lay.cols), BF16)


def _theirs_shape(lay, dtype=BF16):
    return jax.ShapeDtypeStruct(lay.whole(lay.rows // 2), dtype)


def _pair_sum(grad, theirs, lay, pos, name):
    half = lay.rows // 2
    add = lambda a, b: (a.astype(F32) + b.astype(F32),)
    if lay.stacked:
        tr = _row_tile(half, lay.cols * 6)
        nt = half // tr
        flat = lambda a: a.reshape(-1, lay.cols)
        mine = lambda t, pos: ((t // nt) * (2 * nt) + pos[1] * nt + t % nt, 0)
        grid, blk = (N_CHIPS * nt,), (tr, lay.cols)
        grad, theirs = flat(grad), flat(theirs)
    else:
        tr = _row_tile(half, N_CHIPS * lay.cols * 6)
        nt = half // tr
        mine = lambda t, pos: (pos[1] * nt + t, 0)
        grid, blk = (nt,), (tr, N_CHIPS * lay.cols)
    same = lambda t, pos: (t, 0)
    out = _tiled(add, name, grid, pos, [(grad, blk, mine), (theirs, blk, same)], [(theirs.shape, BF16, blk, same)])[0]
    return out.reshape(lay.whole(half))


def _chip_sum(sums, slots, lay, pos, name, after=None):
    half = lay.rows // 2
    tr = _row_tile(half, lay.cols * 12)
    nt = half // tr
    blk3 = (None, tr, lay.cols)
    if lay.stacked:
        own = (sums, blk3, lambda i, pos: (pos[0], i, 0))
    else:
        own = (sums, (tr, lay.cols), lambda i, pos: (i, pos[0]))
    others = [(slots, blk3, functools.partial(lambda d, i, pos: (pos[0] ^ d, i, 0), d)) for d in (1, 2, 3)]

    def add(a, b1, b2, b3, *_):
        return (((a.astype(F32) + b1.astype(F32)) + b2.astype(F32)) + b3.astype(F32),)

    if after is not None:
        others.append((after, (8, 128), lambda i, pos: (0, 0)))
    return _tiled(add, name, (nt,), pos, [own] + others,
                  [((lay.rows, lay.cols), F32, (tr, lay.cols), lambda i, pos: (pos[1] * nt + i, 0))])[0]


N_DEV = 8


def _to_all(src, slots):
    def plan(hbm, pos, send, recv, base):
        x, y, c, _ = pos
        idx = 4 * x + 2 * y + c
        starts = [_remote(hbm[src], hbm[slots].at[idx], send, recv, base + k - 1,
                          (x ^ (k >> 2), y ^ ((k >> 1) & 1), c ^ (k & 1))) for k in range(1, N_DEV)]
        waits = [_arrival(hbm[slots].at[idx ^ k], send, recv, base + k - 1, (x, y, c)) for k in range(1, N_DEV)]
        return starts, waits
    return _Job(N_DEV - 1, plan)


def _sum_slots(own, slots, pos):
    def body(pos_ref, own_ref, slots_ref, o_ref):
        idx = 2 * pos_ref[0] + pos_ref[1]
        term = lambda q: jnp.where(idx == q, own_ref[...], slots_ref[q])
        acc = term(0)
        for q in range(1, N_DEV):
            acc = acc + term(q)
        o_ref[...] = acc

    return pl.pallas_call(
        body, name="sum_small", out_shape=jax.ShapeDtypeStruct(own.shape, F32),
        in_specs=[pl.BlockSpec(memory_space=pltpu.SMEM), pl.BlockSpec(memory_space=pltpu.VMEM),
                  pl.BlockSpec(memory_space=pltpu.VMEM)],
    )(pos, own, slots)


def _pack_rows(parts):
    padded = [jnp.pad(a, ((0, -a.shape[0] % 8), (0, 0))) for a in parts]
    starts = [sum(p.shape[0] for p in padded[:k]) for k in range(len(padded))]
    return jnp.concatenate(padded, axis=0), starts


def kernel(x, mix_norm, w_in, b_in, sinks, conv_w, w_attn_branch, w_conv_branch, w_out, ffn_norm, w_up, ffn_conv_w, w_down, final_norm, loss_target, m_mix_norm, m_w_in, m_b_in, m_sinks, m_conv_w, m_w_attn_branch, m_w_conv_branch, m_w_out, m_ffn_norm, m_w_up, m_ffn_conv_w, m_w_down, m_final_norm, v_mix_norm, v_w_in, v_b_in, v_sinks, v_conv_w, v_w_attn_branch, v_w_conv_branch, v_w_out, v_ffn_norm, v_w_up, v_ffn_conv_w, v_w_down, v_final_norm):
    me = 2 * lax.axis_index("x") + lax.axis_index("y")
    names = ("w_in", "w_br", "w_out", "w_up", "w_down")
    w_of = dict(w_in=w_in[0].T, w_out=w_out[0], w_up=w_up[0], w_down=w_down[0])
    m_of = dict(w_in=m_w_in[0].T, w_out=m_w_out[0], w_up=m_w_up[0], w_down=m_w_down[0])
    v_of = dict(w_in=v_w_in[0].T, w_out=v_w_out[0], w_up=v_w_up[0], w_down=v_w_down[0])
    ab = (w_attn_branch[0], m_w_attn_branch[0], v_w_attn_branch[0])
    cb = (w_conv_branch[0], m_w_conv_branch[0], v_w_conv_branch[0])

    pos = jnp.stack([me, lax.axis_index("c")]).astype(jnp.int32)

    lay = dict(zip(names, BIG))
    xs, target, sk = x[0], loss_target[0], sinks[0]
    s = xs.shape[0]
    tm, tm2, bk, bk2 = min(256, s), min(512, s), min(1024, s), min(2048, s)

    taps, (_, t0) = _pack_rows([conv_w[0], ffn_conv_w[0].reshape(3 * (FF2 // N_CHIPS // 128), 128)])
    placed = {"w_in": _place_cast(w_of["w_in"], lay["w_in"], pos, "cast_w_in")}
    fly_in, started = _start_exchange("gather_in_start", [_gather_ici(lay["w_in"], "w_in")], {"w_in": placed["w_in"]})
    taps_flight, started = _start_exchange("taps_start", [_to_all("v", "slots")],
                                           {"v": taps + started[0:1], "slots": jnp.zeros((N_DEV, *taps.shape), F32)})
    placed["w_br"] = _place_cast_pair(ab[0], cb[0], lay["w_br"], pos, "cast_w_br", after=started)
    for n in names[2:]:
        placed[n] = _place_cast(w_of[n], lay[n], pos, "cast_" + n, after=started)
    trio = ("w_br", "w_out")
    (fly_trio, fly_up, fly_down), started = _start_exchanges("gather_rest_start", [
        ([_gather_ici(lay[n], n) for n in ws], {n: placed[n] for n in ws}) for ws in (trio, ("w_up",), ("w_down",))])

    got = _finish_exchange("gather_in_wait", fly_in, after=started)
    w_in_full = _exchange("gather_in_d2d", [[_gather_d2d(lay["w_in"], "w_in")]], bufs=got)["w_in"].reshape(IN_W, D_MODEL)
    xn, qkv, c3, gates = _inproj_fwd(xs, mix_norm, w_in_full, b_in, tm2)
    k2 = _Carry([_gather_d2d(lay[n], n) for n in trio], bufs=_finish_exchange("gather_trio_wait", fly_trio, after=qkv))
    attn = _attn_fwd(qkv, sk, comm=k2)
    w_br = k2.out["w_br"]
    w_out_full = k2.out["w_out"].reshape(D_MODEL, D_MODEL)
    k3 = _Carry([_gather_d2d(lay["w_up"], "w_up")], bufs=_finish_exchange("gather_up_wait", fly_up, after=attn))
    taps = _finish_exchange("taps_wait", taps_flight, after=attn)
    taps = lax.dynamic_update_slice(taps["slots"], taps["v"][None], (2 * me + lax.axis_index("c"), 0, 0))
    conv_full = taps[0::2, 0:3].transpose(1, 0, 2).reshape(3, CONV_W)
    ffn_cw_full = taps[0::2, t0:t0 + 33].reshape(N_CHIPS, 3, FF2 // N_CHIPS).transpose(1, 0, 2).reshape(3, FF2)
    conv, a, cv, merged, h1, hn = _mix_fwd(xs, attn, c3, gates, conv_full, w_br, w_out_full, ffn_norm, tm2, comm=k3)
    w_up_full = k3.out["w_up"]
    w_down_full = _exchange("gather_down_d2d", [[_gather_d2d(lay["w_down"], "w_down")]],
                            bufs=_finish_exchange("gather_down_wait", fly_down, after=hn))["w_down"].reshape(D_FF, D_MODEL)
    u, up, act, dh2, loss_part, g_fn = _ffn_fwd_loss(hn, h1, w_up_full, ffn_cw_full, w_down_full,
                                                     final_norm[None, :], target, tm)

    grads, sums, slots = {}, {}, {}

    def pair(*ws):
        return _Carry([_rs_pair(lay[n], "g_" + n, "t_" + n) for n in ws], reads={"g_" + n: grads[n] for n in ws},
                      fresh={"t_" + n: _theirs_shape(lay[n], grads[n].dtype) for n in ws})

    def chips(*ws, also=None):
        k = _Carry([_rs_chips(lay[n], "s_" + n, "r_" + n) for n in ws], reads={"s_" + n: sums[n] for n in ws},
                   fresh={"r_" + n: _slots_shape(lay[n]) for n in ws})
        if also is not None:
            k = _Carry(k.jobs + also.jobs, {**k.reads, **also.reads}, None, {**k.fresh, **also.fresh})
        return k

    def pair_sums(k, *ws):
        for n in ws:
            sums[n] = _pair_sum(grads[n], k.out["t_" + n], lay[n], pos, "pair_sum_" + n)

    def take_slots(k, *ws):
        for n in ws:
            slots[n] = k.out["r_" + n]

    du, dh1, g_fcw, g_g2 = _ffn_bwd(dh2, u, up, h1, w_up_full, ffn_cw_full, w_down_full, ffn_norm, tm)
    grads["w_down"] = _wgrad(act, dh2, D_FF // 2, D_MODEL, bk2, "wgrad_down").reshape(lay["w_down"].whole())
    k4 = pair("w_down")
    grads["w_up"] = _wgrad(hn, du, D_MODEL, FF2 // 4, bk2, "wgrad_up", comm=k4)
    pair_sums(k4, "w_down")
    k5 = chips("w_down", also=pair("w_up"))
    dattn, dc3, dgt, g_cw, grads["w_br"], gw_out = _mix_bwd(
        dh1, gates, a, cv, c3, attn, conv, merged, conv_full, w_br, w_out_full, tm, comm=k5)
    grads["w_out"] = gw_out.reshape(lay["w_out"].whole())
    take_slots(k5, "w_down")
    pair_sums(k5, "w_up")
    k6 = chips("w_up", also=pair(*trio))
    dq, dk, dv, g_sk = _attn_bwd(qkv, sk, attn, dattn, comm=k6)
    take_slots(k6, "w_up")
    pair_sums(k6, *trio)
    trio_flight, started = _start_exchange(
        "rs_chips_trio_start", [_rs_chips(lay[n], "s_" + n, "r_" + n) for n in trio],
        {**{"s_" + n: sums[n] for n in trio}, **{"r_" + n: _slots_shape(lay[n]) for n in trio}})
    behind = mix_norm + jnp.tile(started[0:1], (1, D_MODEL // 128))
    grad_x, gw_in, g_b, g_g1 = _inproj_bwd(dq, dk, dv, dc3, dgt, w_in_full, xs, xn, dh1, behind, tm)
    grads["w_in"] = gw_in.reshape(lay["w_in"].whole())

    parts = [loss_part, g_g1, g_b, jnp.pad(g_sk[:, 0], (0, 120))[None, :], g_cw, g_g2, g_fcw, g_fn]
    packed, at = _pack_rows([p.reshape(-1, 128) for p in parts])
    small_flight, started = _start_exchange("small_start", [_to_all("v", "slots")],
                                            {"v": packed, "slots": jnp.zeros((N_DEV, *packed.shape), F32)})
    others = names[1:]
    in_flight, started = _start_exchange("rs_pair_in_start", [_rs_pair(lay["w_in"], "g", "t")],
                                         {"g": grads["w_in"], "t": _theirs_shape(lay["w_in"]), "behind": started})
    halves = {n: _chip_sum(sums[n], slots[n], lay[n], pos, "chip_sum_" + n, after=started) for n in ("w_up", "w_down")}
    landed = _finish_exchange("rs_pair_in_wait", in_flight, after=halves["w_down"])
    sums["w_in"] = _pair_sum(landed["g"], landed["t"], lay["w_in"], pos, "pair_sum_w_in")
    in_flight, started = _start_exchange("rs_chips_in_start", [_rs_chips(lay["w_in"], "s", "r")],
                                         {"s": sums["w_in"], "r": _slots_shape(lay["w_in"])})
    landed = _finish_exchange("rs_chips_trio_wait", trio_flight, after=started)
    for n in trio:
        halves[n] = _chip_sum(landed["s_" + n], landed["r_" + n], lay[n], pos, "chip_sum_" + n)
    shared = _exchange("share_halves", [[_rs_share(lay[n], n) for n in others]], bufs=halves)

    def adam(n, g, after=None):
        return _rowwise(lambda w, g, m, v: (g, *_adamw(w, g, m, v)), [w_of[n], g, m_of[n], v_of[n]], [F32] * 4,
                        "adamw_" + n, after=after)

    new_of, last = {}, None
    for n in ("w_up", "w_down", "w_out"):
        new_of[n] = adam(n, shared[n], last)
        last = new_of[n][1]
    new_of["w_ab"], new_of["w_cb"] = _adamw_pair(ab, cb, shared["w_br"], after=last)
    last = new_of["w_cb"][1]

    arrived = _finish_exchange("small_wait", small_flight, after=last)
    total = _sum_slots(arrived["v"], arrived["slots"], pos)
    part = lambda k: total[at[k]:at[k] + parts[k].size // 128].reshape(parts[k].shape)
    loss = total[0, 0]
    g_mix, g_b, g_g2, g_fn = part(1), part(2), part(5), part(7)
    g_sk = part(3)[:, 0:N_HEADS]
    g_cw = lax.dynamic_slice(part(4), (0, me * 128), (3, 128))
    g_fcw = lax.dynamic_slice(part(6), (0, me * (FF2 // N_CHIPS)), (3, FF2 // N_CHIPS))
    small_p = [
        (mix_norm, g_mix, m_mix_norm, v_mix_norm), (b_in, g_b, m_b_in, v_b_in), (sinks, g_sk, m_sinks, v_sinks),
        (conv_w[0], g_cw, m_conv_w[0], v_conv_w[0]), (ffn_norm, g_g2, m_ffn_norm, v_ffn_norm),
        (ffn_conv_w[0], g_fcw, m_ffn_conv_w[0], v_ffn_conv_w[0]),
        (final_norm[None, :], g_fn, m_final_norm[None, :], v_final_norm[None, :])]
    small_new = _adamw_small(small_p)
    small_new = [small_new[3 * k:3 * k + 3] for k in range(len(small_p))]

    landed = _finish_exchange("rs_chips_in_wait", in_flight, after=small_new[0][0])
    half_in = _chip_sum(landed["s"], landed["r"], lay["w_in"], pos, "chip_sum_w_in")
    shared["w_in"] = _exchange("share_in", [[_rs_share(lay["w_in"], "w_in")]], bufs={"w_in": half_in})["w_in"]
    new_of["w_in"] = [a.T for a in adam("w_in", shared["w_in"])]
    big = ("w_in", "w_ab", "w_cb", "w_out", "w_up", "w_down")
    big_g = [new_of[n][0] for n in big]
    big_new = [new_of[n][1:] for n in big]

    order = [("s", 0), ("b", 0), ("s", 1), ("s", 2), ("s", 3), ("b", 1), ("b", 2), ("b", 3), ("s", 4), ("b", 4),
             ("s", 5), ("b", 5), ("s", 6)]
    shapes = [mix_norm.shape, w_in.shape, b_in.shape, sinks.shape, conv_w.shape, w_attn_branch.shape,
              w_conv_branch.shape, w_out.shape, ffn_norm.shape, w_up.shape, ffn_conv_w.shape, w_down.shape,
              final_norm.shape]
    small_g = [p[1] for p in small_p]
    out_g = [(small_g[k] if kind == "s" else big_g[k]).reshape(shp) for (kind, k), shp in zip(order, shapes)]
    news = [[(small_new[k][j] if kind == "s" else big_new[k][j]).reshape(shp) for (kind, k), shp in zip(order, shapes)]
            for j in range(3)]
    return (loss, grad_x[None], *out_g, *news[0], *news[1], *news[2])
```

```python
import functools

import jax
import jax.numpy as jnp
from jax import lax
from jax.experimental import pallas as pl
from jax.experimental.pallas import tpu as pltpu

F32 = jnp.float32
BF16 = jnp.bfloat16

D_MODEL = 1024
HEAD_DIM = 64
N_HEADS = 8
N_KV_HEADS = 2
GROUP = N_HEADS // N_KV_HEADS
BLOCK = 128
ATTN_SCALE = HEAD_DIM ** -0.5
ATTN_W = N_HEADS * HEAD_DIM
KV_W = N_KV_HEADS * HEAD_DIM
CONV_W = 512
QKV_W = ATTN_W + 2 * KV_W
C3_W = 3 * CONV_W
GATES_W = 2 * D_MODEL
IN_W = QKV_W + C3_W + GATES_W
D_FF = 2816
FF2 = 2 * D_FF
NORM_EPS = 1e-5
N_CHIPS = 4
IN_SHARD = IN_W // N_CHIPS
NEG = -1e30

ADAM_LR = 0.001
ADAM_B1 = 0.9
ADAM_B2 = 0.999
ADAM_EPS = 1e-08
ADAM_WD = 0.01
ADAM_STEP = 10

VMEM_LIMIT = 56 * 1024 * 1024
VMEM_LIMIT_BIG = 61 * 1024 * 1024
MESH = pl.DeviceIdType.MESH

NT = (((1,), (1,)), ((), ()))
TN = (((0,), (0,)), ((), ()))


def _params(*sem):
    return pltpu.CompilerParams(dimension_semantics=sem, vmem_limit_bytes=VMEM_LIMIT)


def _resident(shape):
    return pl.BlockSpec(shape, lambda *_: (0,) * len(shape), pipeline_mode=pl.Buffered(1))


def _sigmoid(v):
    return 0.5 * jnp.tanh(0.5 * v) + 0.5


def _rstd(v):
    return lax.rsqrt(jnp.mean(v * v, axis=-1, keepdims=True) + NORM_EPS)


def _rms_bwd(dy, v, rstd, g):
    vhat = v * rstd
    t = dy * g
    return rstd * (t - vhat * jnp.mean(t * vhat, axis=-1, keepdims=True)), dy * vhat


def _taps(z, cw):
    return cw[2:3] * z + cw[1:2] * pltpu.roll(z, 1, 0) + cw[0:1] * pltpu.roll(z, 2, 0)


def _causal_conv(z, prev, cw):
    edge = _taps(jnp.concatenate([prev, z[0:8]], axis=0), cw)
    return jnp.concatenate([edge[8:16], _taps(z, cw)[8:]], axis=0)


def _rows_after(z, nxt):
    n = z.shape[0]
    edge = jnp.concatenate([z[n - 8:n], nxt], axis=0)
    return tuple(jnp.concatenate([pltpu.roll(z, n - k, 0)[:n - 8], pltpu.roll(edge, 16 - k, 0)[0:8]], axis=0)
                 for k in (1, 2))


def _inproj_fwd(x, g1, w_in, b_in, tm, comm=None):
    s = x.shape[0]

    def body(x_ref, g_ref, w_ref, b_ref, xn_ref, qkv_ref, c3_ref, gt_ref):
        xf = x_ref[...]
        xn = (xf * _rstd(xf) * g_ref[...]).astype(BF16)
        xn_ref[...] = xn

        def seg(a, b):
            return lax.dot_general(xn, w_ref[a:b, :], NT, preferred_element_type=F32) + b_ref[:, a:b]

        qkv_ref[...] = seg(0, QKV_W).astype(BF16)
        c3_ref[...] = seg(QKV_W, QKV_W + C3_W).astype(BF16)
        gt_ref[...] = seg(QKV_W + C3_W, IN_W).astype(BF16)

    row = lambda w: pl.BlockSpec((tm, w), lambda i: (i, 0))
    return _call(
        comm, body, name="inproj_fwd", grid=(s // tm,),
        in_specs=[row(D_MODEL), _resident((1, D_MODEL)), _resident((IN_W, D_MODEL)), _resident((1, IN_W))],
        out_specs=[row(D_MODEL), row(QKV_W), row(C3_W), row(GATES_W)],
        out_shape=[jax.ShapeDtypeStruct((s, D_MODEL), BF16), jax.ShapeDtypeStruct((s, QKV_W), BF16),
                   jax.ShapeDtypeStruct((s, C3_W), BF16), jax.ShapeDtypeStruct((s, GATES_W), BF16)],
        compiler_params=_params("parallel"),
    )(x, g1, w_in, b_in)


def _attn_bias():
    qi = (jnp.arange(GROUP * BLOCK) % BLOCK)[:, None]
    kj = jnp.arange(2 * BLOCK)[None, :]
    band = (kj > qi) & (kj <= qi + BLOCK)
    return jnp.stack([jnp.where(band & (kj >= BLOCK), 0.0, NEG), jnp.where(band, 0.0, NEG)]).astype(F32)


def _attn_bias_spec():
    return pl.BlockSpec((None, GROUP * BLOCK, 2 * BLOCK), lambda i: (jnp.minimum(i, 1), 0, 0))


def _sink_column(sk_ref, h):
    rows = lax.broadcasted_iota(jnp.int32, (GROUP * BLOCK, 1), 0)
    col = jnp.full((GROUP * BLOCK, 1), sk_ref[h * GROUP], F32)
    for g in range(1, GROUP):
        col = jnp.where(rows >= g * BLOCK, sk_ref[h * GROUP + g], col)
    return col


def _stack_heads(t, h):
    return jnp.concatenate(
        [t[:, (h * GROUP + g) * HEAD_DIM:(h * GROUP + g + 1) * HEAD_DIM] for g in range(GROUP)], axis=0)


def _unstack_heads(per_kv):
    return jnp.concatenate(
        [t[g * BLOCK:(g + 1) * BLOCK] for t in per_kv for g in range(GROUP)], axis=1)


def _attn_specs(nb):
    cur = lambda i: jnp.minimum(i, nb - 1)
    prev = lambda i: jnp.maximum(jnp.minimum(i, nb - 1) - 1, 0)
    q = pl.BlockSpec((BLOCK, ATTN_W), lambda i: (cur(i), 0))
    kp = pl.BlockSpec((BLOCK, KV_W), lambda i: (prev(i), ATTN_W // KV_W))
    kc = pl.BlockSpec((BLOCK, KV_W), lambda i: (cur(i), ATTN_W // KV_W))
    vp = pl.BlockSpec((BLOCK, KV_W), lambda i: (prev(i), ATTN_W // KV_W + 1))
    vc = pl.BlockSpec((BLOCK, KV_W), lambda i: (cur(i), ATTN_W // KV_W + 1))
    return q, kp, kc, vp, vc


def _attn_fwd(qkv, sinks, comm=None):
    s = qkv.shape[0]
    nb = s // BLOCK

    def body(sk_ref, bias_ref, q_ref, kp_ref, kc_ref, vp_ref, vc_ref, o_ref):
        bias = bias_ref[...]
        q, kp, kc, vp, vc = q_ref[...], kp_ref[...], kc_ref[...], vp_ref[...], vc_ref[...]
        outs = []
        for h in range(N_KV_HEADS):
            hs = slice(h * HEAD_DIM, (h + 1) * HEAD_DIM)
            k2 = jnp.concatenate([kp[:, hs], kc[:, hs]], axis=0)
            v2 = jnp.concatenate([vp[:, hs], vc[:, hs]], axis=0)
            sc = lax.dot_general(_stack_heads(q, h), k2, NT, preferred_element_type=F32) * ATTN_SCALE + bias
            sink = _sink_column(sk_ref, h)
            m = jnp.maximum(jnp.max(sc, axis=1, keepdims=True), sink)
            p = jnp.exp(sc - m)
            den = jnp.sum(p, axis=1, keepdims=True) + jnp.exp(sink - m)
            outs.append(jnp.dot(p.astype(BF16), v2, preferred_element_type=F32) / den)
        o_ref[...] = _unstack_heads(outs).astype(BF16)

    return _call(
        comm, body, name="attn_fwd", grid=(nb,),
        in_specs=[pl.BlockSpec(memory_space=pltpu.SMEM), _attn_bias_spec(), *_attn_specs(nb)],
        out_specs=pl.BlockSpec((BLOCK, ATTN_W), lambda i: (i, 0)),
        out_shape=jax.ShapeDtypeStruct((s, ATTN_W), BF16),
        compiler_params=_params("parallel"),
    )(sinks, _attn_bias(), qkv, qkv, qkv, qkv, qkv)


def _mix_fwd(x, attn, c3, gates, conv_w, w_br, w_out, g2, tm, comm=None):
    s = x.shape[0]

    def body(x_ref, at_ref, c3_ref, gt_ref, cw_ref, wbr_ref, wo_ref, g_ref,
             conv_ref, a_ref, cv_ref, mg_ref, h1_ref, hn_ref, carry_ref):
        @pl.when(pl.program_id(0) == 0)
        def _():
            carry_ref[...] = jnp.zeros_like(carry_ref)

        c3v = c3_ref[...].astype(F32)
        cb, cc, cx = c3v[:, :CONV_W], c3v[:, CONV_W:2 * CONV_W], c3v[:, 2 * CONV_W:]
        z = cc * cx
        cz = _causal_conv(z, carry_ref[...], cw_ref[...])
        carry_ref[...] = z[tm - 8:tm]
        conv = (cb * cz).astype(BF16)
        conv_ref[...] = conv
        a = jnp.dot(at_ref[...], wbr_ref[:ATTN_W, :], preferred_element_type=F32)
        cv = jnp.dot(conv, wbr_ref[ATTN_W:, :], preferred_element_type=F32)
        a_ref[...] = a.astype(BF16)
        cv_ref[...] = cv.astype(BF16)
        gt = gt_ref[...].astype(F32)
        merged = (_sigmoid(gt[:, :D_MODEL]) * a + _sigmoid(gt[:, D_MODEL:]) * cv).astype(BF16)
        mg_ref[...] = merged
        h1 = x_ref[...] + jnp.dot(merged, wo_ref[...], preferred_element_type=F32)
        h1_ref[...] = h1
        hn_ref[...] = (h1 * _rstd(h1) * g_ref[...]).astype(BF16)

    row = lambda w: pl.BlockSpec((tm, w), lambda i: (i, 0))
    return _call(
        comm, body, name="mix_fwd", grid=(s // tm,),
        in_specs=[row(D_MODEL), row(ATTN_W), row(C3_W), row(GATES_W), _resident((3, CONV_W)),
                  _resident((ATTN_W + CONV_W, D_MODEL)), _resident((D_MODEL, D_MODEL)), _resident((1, D_MODEL))],
        out_specs=[row(CONV_W), row(D_MODEL), row(D_MODEL), row(D_MODEL), row(D_MODEL), row(D_MODEL)],
        out_shape=[jax.ShapeDtypeStruct((s, CONV_W), BF16), jax.ShapeDtypeStruct((s, D_MODEL), BF16),
                   jax.ShapeDtypeStruct((s, D_MODEL), BF16), jax.ShapeDtypeStruct((s, D_MODEL), BF16),
                   jax.ShapeDtypeStruct((s, D_MODEL), F32), jax.ShapeDtypeStruct((s, D_MODEL), BF16)],
        scratch_shapes=[pltpu.VMEM((8, CONV_W), F32)],
        compiler_params=_params("arbitrary"),
    )(x, attn, c3, gates, conv_w, w_br, w_out, g2)


def _ffn_fwd_loss(hn, h1, w_up, ffn_cw, w_down, g3, target, tm):
    s = hn.shape[0]

    def body(hn_ref, h1_ref, wu_ref, cw_ref, wd_ref, g_ref, t_ref,
             u_ref, up_ref, act_ref, dh2_ref, loss_ref, gfn_ref, carry_ref):
        @pl.when(pl.program_id(0) == 0)
        def _():
            carry_ref[...] = jnp.zeros_like(carry_ref)
            loss_ref[...] = jnp.zeros_like(loss_ref)
            gfn_ref[...] = jnp.zeros_like(gfn_ref)

        u = jnp.dot(hn_ref[...], wu_ref[...], preferred_element_type=F32)
        u_ref[...] = u.astype(BF16)
        up = _causal_conv(u, carry_ref[...], cw_ref[...])
        up_ref[...] = up
        carry_ref[...] = u[tm - 8:tm]
        gate, val = up[:, :D_FF], up[:, D_FF:]
        act = (gate * _sigmoid(gate) * val).astype(BF16)
        act_ref[...] = act
        h2 = h1_ref[...] + jnp.dot(act, wd_ref[...], preferred_element_type=F32)
        rstd = _rstd(h2)
        g = g_ref[...]
        err = h2 * rstd * g - t_ref[...]
        loss_ref[...] += jnp.sum(err * err) * (0.5 / D_MODEL)
        dh2, dg = _rms_bwd(err * (1.0 / D_MODEL), h2, rstd, g)
        dh2_ref[...] = dh2
        gfn_ref[...] += jnp.sum(dg, axis=0, keepdims=True)

    row = lambda w: pl.BlockSpec((tm, w), lambda i: (i, 0))
    acc = lambda w: pl.BlockSpec((1, w), lambda i: (0, 0))
    return pl.pallas_call(
        body, name="ffn_fwd_loss", grid=(s // tm,),
        in_specs=[row(D_MODEL), row(D_MODEL), _resident((D_MODEL, FF2)), _resident((3, FF2)),
                  _resident((D_FF, D_MODEL)), _resident((1, D_MODEL)), row(D_MODEL)],
        out_specs=[row(FF2), row(FF2), row(D_FF), row(D_MODEL), acc(128), acc(D_MODEL)],
        out_shape=[jax.ShapeDtypeStruct((s, FF2), BF16), jax.ShapeDtypeStruct((s, FF2), F32),
                   jax.ShapeDtypeStruct((s, D_FF), BF16),
                   jax.ShapeDtypeStruct((s, D_MODEL), F32), jax.ShapeDtypeStruct((1, 128), F32),
                   jax.ShapeDtypeStruct((1, D_MODEL), F32)],
        scratch_shapes=[pltpu.VMEM((8, FF2), F32)],
        compiler_params=_params("arbitrary"),
    )(hn, h1, w_up, ffn_cw, w_down, g3, target)


def _ffn_bwd(dh2, u, up, h1, w_up, ffn_cw, w_down, g2, tm):
    s = dh2.shape[0]
    nt = s // tm

    def body(dh2_ref, u_ref, up_ref, h1_ref, wu_ref, cw_ref, wd_ref, g_ref,
             du_ref, dh1_ref, gcw_ref, gg_ref, carry_ref):
        @pl.when(pl.program_id(0) == 0)
        def _():
            for ref in (carry_ref, gcw_ref, gg_ref):
                ref[...] = jnp.zeros_like(ref)

        dh2v = dh2_ref[...]
        dact = lax.dot_general(dh2v.astype(BF16), wd_ref[...], NT, preferred_element_type=F32)
        upv = up_ref[...]
        gate, val = upv[:, :D_FF], upv[:, D_FF:]
        sg = _sigmoid(gate)
        dval = dact * (gate * sg)
        dgate = dact * val * (sg * (1.0 + gate * (1.0 - sg)))
        dup = jnp.concatenate([dgate, dval], axis=1)
        dup1, dup2 = _rows_after(dup, carry_ref[...])
        carry_ref[...] = dup[0:8]
        u = u_ref[...].astype(F32)
        gcw_ref[2:3, :] += jnp.sum(dup * u, axis=0, keepdims=True)
        gcw_ref[1:2, :] += jnp.sum(dup1 * u, axis=0, keepdims=True)
        gcw_ref[0:1, :] += jnp.sum(dup2 * u, axis=0, keepdims=True)
        cw = cw_ref[...]
        du = (cw[2:3] * dup + cw[1:2] * dup1 + cw[0:1] * dup2).astype(BF16)
        du_ref[...] = du
        dhn = lax.dot_general(du, wu_ref[...], NT, preferred_element_type=F32)
        h1v = h1_ref[...]
        dh1, dg = _rms_bwd(dhn, h1v, _rstd(h1v), g_ref[...])
        dh1_ref[...] = dh2v + dh1
        gg_ref[...] += jnp.sum(dg, axis=0, keepdims=True)

    row = lambda w: pl.BlockSpec((tm, w), lambda i: (nt - 1 - i, 0))
    return pl.pallas_call(
        body, name="ffn_bwd", grid=(nt,),
        in_specs=[row(D_MODEL), row(FF2), row(FF2),
                  row(D_MODEL), _resident((D_MODEL, FF2)), _resident((3, FF2)), _resident((D_FF, D_MODEL)),
                  _resident((1, D_MODEL))],
        out_specs=[row(FF2), row(D_MODEL), pl.BlockSpec((3, FF2), lambda i: (0, 0)),
                   pl.BlockSpec((1, D_MODEL), lambda i: (0, 0))],
        out_shape=[jax.ShapeDtypeStruct((s, FF2), BF16), jax.ShapeDtypeStruct((s, D_MODEL), F32),
                   jax.ShapeDtypeStruct((3, FF2), F32), jax.ShapeDtypeStruct((1, D_MODEL), F32)],
        scratch_shapes=[pltpu.VMEM((8, FF2), F32)],
        compiler_params=_params("arbitrary"),
    )(dh2, u, up, h1, w_up, ffn_cw, w_down, g2)


def _mix_bwd(dh1, gates, a, cv, c3, attn, conv, merged, conv_w, w_br, w_out, tm, comm=None):
    s = dh1.shape[0]
    nt = s // tm
    halo = 16

    def body(dh1_ref, gt_ref, a_ref, cv_ref, c3_ref, ch_ref, at_ref, cn_ref, mg_ref, cw_ref, wbr_ref,
             wo_ref, dat_ref, dc3_ref, dgt_ref, gcw_ref, gbr_ref, gout_ref, carry_ref, br_acc, out_acc):
        i = pl.program_id(0)

        @pl.when(i == 0)
        def _():
            for ref in (carry_ref, gcw_ref, br_acc, out_acc):
                ref[...] = jnp.zeros_like(ref)

        dh1v = dh1_ref[...].astype(BF16)
        out_acc[...] += lax.dot_general(mg_ref[...], dh1v, TN, preferred_element_type=F32)
        dm = lax.dot_general(dh1v, wo_ref[...], NT, preferred_element_type=F32)
        gt = gt_ref[...].astype(F32)
        sa, sc = _sigmoid(gt[:, :D_MODEL]), _sigmoid(gt[:, D_MODEL:])
        da = (dm * sa).astype(BF16)
        dcv = (dm * sc).astype(BF16)
        br_acc[:ATTN_W, :] += lax.dot_general(at_ref[...], da, TN, preferred_element_type=F32)
        br_acc[ATTN_W:, :] += lax.dot_general(cn_ref[...], dcv, TN, preferred_element_type=F32)
        dgt_ref[...] = jnp.concatenate(
            [dm * a_ref[...].astype(F32) * (sa * (1.0 - sa)), dm * cv_ref[...].astype(F32) * (sc * (1.0 - sc))],
            axis=1).astype(BF16)
        dat_ref[...] = lax.dot_general(da, wbr_ref[:ATTN_W, :], NT, preferred_element_type=F32).astype(BF16)
        dconv = lax.dot_general(dcv, wbr_ref[ATTN_W:, :], NT, preferred_element_type=F32)
        c3v = c3_ref[...].astype(F32)
        cb, cc, cx = c3v[:, :CONV_W], c3v[:, CONV_W:2 * CONV_W], c3v[:, 2 * CONV_W:]
        z = cc * cx
        chv = ch_ref[...].astype(F32)[halo - 8:halo] * (i < nt - 1).astype(F32)
        zh = chv[:, CONV_W:2 * CONV_W] * chv[:, 2 * CONV_W:]
        cw = cw_ref[...]
        cz = _causal_conv(z, zh, cw)
        dcz = dconv * cb
        dcz1, dcz2 = _rows_after(dcz, carry_ref[...])
        carry_ref[...] = dcz[0:8]
        gcw_ref[2:3, :] += jnp.sum(dcz * z, axis=0, keepdims=True)
        gcw_ref[1:2, :] += jnp.sum(dcz1 * z, axis=0, keepdims=True)
        gcw_ref[0:1, :] += jnp.sum(dcz2 * z, axis=0, keepdims=True)
        dz = cw[2:3] * dcz + cw[1:2] * dcz1 + cw[0:1] * dcz2
        dc3_ref[...] = jnp.concatenate([dconv * cz, dz * cx, dz * cc], axis=1).astype(BF16)

        @pl.when(i == nt - 1)
        def _():
            gbr_ref[...] = br_acc[...].astype(BF16)
            gout_ref[...] = out_acc[...].astype(BF16)

    row = lambda w: pl.BlockSpec((tm, w), lambda i: (nt - 1 - i, 0))
    return _call(
        comm, body, name="mix_bwd", grid=(nt,),
        in_specs=[row(D_MODEL), row(GATES_W), row(D_MODEL), row(D_MODEL), row(C3_W),
                  pl.BlockSpec((halo, C3_W), lambda i: (jnp.maximum((nt - 1 - i) * (tm // halo) - 1, 0), 0)),
                  row(ATTN_W), row(CONV_W), row(D_MODEL), _resident((3, CONV_W)),
                  _resident((ATTN_W + CONV_W, D_MODEL)), _resident((D_MODEL, D_MODEL))],
        out_specs=[row(ATTN_W), row(C3_W), row(GATES_W), pl.BlockSpec((3, CONV_W), lambda i: (0, 0)),
                   _resident((ATTN_W + CONV_W, D_MODEL)), _resident((D_MODEL, D_MODEL))],
        out_shape=[jax.ShapeDtypeStruct((s, ATTN_W), BF16), jax.ShapeDtypeStruct((s, C3_W), BF16),
                   jax.ShapeDtypeStruct((s, GATES_W), BF16), jax.ShapeDtypeStruct((3, CONV_W), F32),
                   jax.ShapeDtypeStruct((ATTN_W + CONV_W, D_MODEL), BF16),
                   jax.ShapeDtypeStruct((D_MODEL, D_MODEL), BF16)],
        scratch_shapes=[pltpu.VMEM((8, CONV_W), F32), pltpu.VMEM((ATTN_W + CONV_W, D_MODEL), F32),
                        pltpu.VMEM((D_MODEL, D_MODEL), F32)],
        compiler_params=_params("arbitrary"),
    )(dh1, gates, a, cv, c3, c3, attn, conv, merged, conv_w, w_br, w_out)


def _attn_bwd(qkv, sinks, o, do, comm=None):
    s = qkv.shape[0]
    nb = s // BLOCK

    def body(sk_ref, bias_ref, q_ref, kp_ref, kc_ref, vp_ref, vc_ref, o_ref, do_ref,
             dq_ref, dk_ref, dv_ref, dsk_ref, ck_ref, cvv_ref):
        i = pl.program_id(0)

        @pl.when(i == 0)
        def _():
            ck_ref[...] = jnp.zeros_like(ck_ref)
            cvv_ref[...] = jnp.zeros_like(cvv_ref)
            dsk_ref[...] = jnp.zeros_like(dsk_ref)

        @pl.when(i < nb)
        def _():
            bias = bias_ref[...]
            q, kp, kc, vp, vc = q_ref[...], kp_ref[...], kc_ref[...], vp_ref[...], vc_ref[...]
            ov, dov = o_ref[...], do_ref[...]
            dqs, dks, dvs = [], [], []
            for h in range(N_KV_HEADS):
                hs = slice(h * HEAD_DIM, (h + 1) * HEAD_DIM)
                k2 = jnp.concatenate([kp[:, hs], kc[:, hs]], axis=0)
                v2 = jnp.concatenate([vp[:, hs], vc[:, hs]], axis=0)
                qg, og, dog = _stack_heads(q, h), _stack_heads(ov, h), _stack_heads(dov, h)
                sc = lax.dot_general(qg, k2, NT, preferred_element_type=F32) * ATTN_SCALE + bias
                sink = _sink_column(sk_ref, h)
                m = jnp.maximum(jnp.max(sc, axis=1, keepdims=True), sink)
                p = jnp.exp(sc - m)
                psink = jnp.exp(sink - m)
                inv = 1.0 / (jnp.sum(p, axis=1, keepdims=True) + psink)
                p = p * inv
                delta = jnp.sum(dog.astype(F32) * og.astype(F32), axis=1, keepdims=True)
                dp = lax.dot_general(dog, v2, NT, preferred_element_type=F32)
                ds = (p * (dp - delta)).astype(BF16)
                dqs.append(jnp.dot(ds, k2, preferred_element_type=F32) * ATTN_SCALE)
                dks.append(lax.dot_general(ds, qg, TN, preferred_element_type=F32) * ATTN_SCALE)
                dvs.append(lax.dot_general(p.astype(BF16), dog, TN, preferred_element_type=F32))
                dsink = -(psink * inv * delta)
                for g in range(GROUP):
                    r = h * GROUP + g
                    dsk_ref[r:r + 1, :] += jnp.sum(dsink[g * BLOCK:(g + 1) * BLOCK])
            dq_ref[...] = _unstack_heads(dqs).astype(BF16)
            dk2 = jnp.concatenate(dks, axis=1)
            dv2 = jnp.concatenate(dvs, axis=1)
            dk_ref[...] = (ck_ref[...] + dk2[:BLOCK]).astype(BF16)
            dv_ref[...] = (cvv_ref[...] + dv2[:BLOCK]).astype(BF16)
            ck_ref[...] = dk2[BLOCK:]
            cvv_ref[...] = dv2[BLOCK:]

        @pl.when(i == nb)
        def _():
            dk_ref[...] = ck_ref[...].astype(BF16)
            dv_ref[...] = cvv_ref[...].astype(BF16)

    cur = lambda i: jnp.minimum(i, nb - 1)
    done = lambda i: jnp.maximum(i - 1, 0)
    return _call(
        comm, body, name="attn_bwd", grid=(nb + 1,),
        in_specs=[pl.BlockSpec(memory_space=pltpu.SMEM), _attn_bias_spec(), *_attn_specs(nb),
                  pl.BlockSpec((BLOCK, ATTN_W), lambda i: (cur(i), 0)),
                  pl.BlockSpec((BLOCK, ATTN_W), lambda i: (cur(i), 0))],
        out_specs=[pl.BlockSpec((BLOCK, ATTN_W), lambda i: (cur(i), 0)),
                   pl.BlockSpec((BLOCK, KV_W), lambda i: (done(i), 0)),
                   pl.BlockSpec((BLOCK, KV_W), lambda i: (done(i), 0)),
                   pl.BlockSpec((N_HEADS, 128), lambda i: (0, 0))],
        out_shape=[jax.ShapeDtypeStruct((s, ATTN_W), BF16), jax.ShapeDtypeStruct((s, KV_W), BF16),
                   jax.ShapeDtypeStruct((s, KV_W), BF16), jax.ShapeDtypeStruct((N_HEADS, 128), F32)],
        scratch_shapes=[pltpu.VMEM((BLOCK, KV_W), F32), pltpu.VMEM((BLOCK, KV_W), F32)],
        compiler_params=_params("arbitrary"),
    )(sinks, _attn_bias(), qkv, qkv, qkv, qkv, qkv, o, do)


def _inproj_bwd(dq, dk, dv, dc3, dgt, w_in, x, xn, dh1, g1, tm):
    s = x.shape[0]
    nt = s // tm

    def body(dq_ref, dk_ref, dv_ref, dc3_ref, dgt_ref, w_ref, x_ref, xn_ref, dh1_ref, g_ref,
             dx_ref, gw_ref, gb_ref, gg_ref):
        @pl.when(pl.program_id(0) == 0)
        def _():
            for ref in (gb_ref, gg_ref, gw_ref):
                ref[...] = jnp.zeros_like(ref)

        dp = jnp.concatenate([dq_ref[...], dk_ref[...], dv_ref[...], dc3_ref[...], dgt_ref[...]], axis=1)
        gw_ref[...] += lax.dot_general(dp, xn_ref[...], TN, preferred_element_type=F32)
        gb_ref[...] += jnp.sum(dp.astype(F32), axis=0, keepdims=True)
        dxn = jnp.dot(dp, w_ref[...], preferred_element_type=F32)
        xf = x_ref[...]
        dx, dg = _rms_bwd(dxn, xf, _rstd(xf), g_ref[...])
        dx_ref[...] = dh1_ref[...] + dx
        gg_ref[...] += jnp.sum(dg, axis=0, keepdims=True)

    row = lambda w: pl.BlockSpec((tm, w), lambda i: (i, 0))
    acc = lambda w: pl.BlockSpec((1, w), lambda i: (0, 0))
    return pl.pallas_call(
        body, name="inproj_bwd", grid=(nt,),
        in_specs=[row(ATTN_W), row(KV_W), row(KV_W), row(C3_W), row(GATES_W), _resident((IN_W, D_MODEL)),
                  row(D_MODEL), row(D_MODEL), row(D_MODEL), _resident((1, D_MODEL))],
        out_specs=[row(D_MODEL), _resident((IN_W, D_MODEL)), acc(IN_W), acc(D_MODEL)],
        out_shape=[jax.ShapeDtypeStruct((s, D_MODEL), F32), jax.ShapeDtypeStruct((IN_W, D_MODEL), F32),
                   jax.ShapeDtypeStruct((1, IN_W), F32), jax.ShapeDtypeStruct((1, D_MODEL), F32)],
        compiler_params=pltpu.CompilerParams(dimension_semantics=("arbitrary",), vmem_limit_bytes=VMEM_LIMIT_BIG),
    )(dq, dk, dv, dc3, dgt, w_in, x, xn, dh1, g1)


def _wgrad(a, b, bm, bn, bk, name, comm=None):
    s, m = a.shape
    n = b.shape[1]
    nk = s // bk

    def body(a_ref, b_ref, o_ref, acc_ref):
        k = pl.program_id(2)

        @pl.when(k == 0)
        def _():
            acc_ref[...] = jnp.zeros_like(acc_ref)

        acc_ref[...] += lax.dot_general(a_ref[...].astype(BF16), b_ref[...].astype(BF16), TN,
                                        preferred_element_type=F32)

        @pl.when(k == nk - 1)
        def _():
            o_ref[...] = acc_ref[...].astype(BF16)

    return _call(
        comm, body, name=name, grid=(m // bm, n // bn, nk),
        in_specs=[pl.BlockSpec((bk, bm), lambda i, j, k: (k, i)), pl.BlockSpec((bk, bn), lambda i, j, k: (k, j))],
        out_specs=pl.BlockSpec((bm, bn), lambda i, j, k: (i, j)),
        out_shape=jax.ShapeDtypeStruct((m, n), BF16),
        scratch_shapes=[pltpu.VMEM((bm, bn), F32)],
        compiler_params=_params("parallel", "parallel", "arbitrary"),
    )(a, b)


class _Carry:
    def __init__(self, jobs, reads=None, bufs=None, fresh=None):
        self.jobs, self.reads, self.bufs, self.fresh = jobs, reads or {}, bufs or {}, fresh or {}
        self.out = {}


class _Job:
    def __init__(self, n_sems, plan):
        self.n_sems, self.plan = n_sems, plan


def _plan_all(jobs, hbm, send, recv):
    pos = _position()
    starts, waits, base = [], [], 0
    for job in jobs:
        s, w = job.plan(hbm, pos, send, recv, base)
        starts, waits, base = starts + s, waits + w, base + job.n_sems
    return starts, waits


def _call(comm, body, **kw):
    if comm is None:
        return pl.pallas_call(body, **kw)
    grid = kw["grid"]
    single = not isinstance(kw["out_shape"], (list, tuple))
    out_shape = [kw["out_shape"]] if single else list(kw["out_shape"])
    out_specs = [kw["out_specs"]] if single else list(kw["out_specs"])
    in_specs = list(kw["in_specs"])
    scratch = list(kw.get("scratch_shapes", ()))
    r_names, b_names, f_names = list(comm.reads), list(comm.bufs), list(comm.fresh)
    n_args, n_out, n_scr = len(in_specs), len(out_shape), len(scratch)
    n_sems = sum(j.n_sems for j in comm.jobs)

    def wrapped(*refs):
        k = n_args
        hbm = dict(zip(r_names, refs[k:k + len(r_names)]))
        k += len(r_names) + len(b_names)
        outs = refs[k:k + n_out]
        k += n_out
        hbm.update(zip(b_names + f_names, refs[k:k + len(b_names) + len(f_names)]))
        k += len(b_names) + len(f_names)
        send, recv = refs[k + n_scr:]
        starts, waits = _plan_all(comm.jobs, hbm, send, recv)
        ids = [pl.program_id(a) for a in range(len(grid))]
        first = functools.reduce(jnp.logical_and, [i == 0 for i in ids])
        last = functools.reduce(jnp.logical_and, [i == g - 1 for i, g in zip(ids, grid)])

        @pl.when(first)
        def _():
            for cp in starts:
                cp.start()

        body(*refs[:n_args], *outs, *refs[k:k + n_scr])

        @pl.when(last)
        def _():
            for cp in waits:
                cp.wait_recv()
            for cp in starts:
                cp.wait_send()

    sems = pltpu.SemaphoreType.DMA((n_sems,))
    held = [jax.ShapeDtypeStruct(a.shape, a.dtype) for a in comm.bufs.values()] + list(comm.fresh.values())
    call = pl.pallas_call(
        wrapped, name=kw["name"], grid=grid,
        in_specs=in_specs + [_ANY] * (len(r_names) + len(b_names)),
        out_specs=out_specs + [_ANY] * len(held),
        out_shape=out_shape + held,
        input_output_aliases={n_args + len(r_names) + i: n_out + i for i in range(len(b_names))},
        scratch_shapes=scratch + [sems, sems],
        compiler_params=_params(*["arbitrary"] * len(grid)),
    )

    def run(*args):
        res = call(*args, *comm.reads.values(), *comm.bufs.values())
        comm.out = dict(zip(b_names + f_names, res[n_out:]))
        return res[0] if single else res[:n_out]

    return run


def _exchange(name, phases, reads=None, bufs=None, fresh=None):
    comm = _Carry([j for ph in phases for j in ph], reads, bufs, fresh)
    r_names, b_names, f_names = list(comm.reads), list(comm.bufs), list(comm.fresh)
    n_sems = sum(j.n_sems for j in comm.jobs)

    def body(*refs):
        hbm = dict(zip(r_names, refs[:len(r_names)]))
        k = len(r_names) + len(b_names)
        hbm.update(zip(b_names + f_names, refs[k:k + len(b_names) + len(f_names)]))
        send, recv = refs[-2:]
        pos = _position()
        started, base = [], 0
        for ph in phases:
            waits = []
            for job in ph:
                s, w = job.plan(hbm, pos, send, recv, base)
                base += job.n_sems
                for cp in s:
                    cp.start()
                started, waits = started + s, waits + w
            for cp in waits:
                cp.wait_recv()
        for cp in started:
            cp.wait_send()

    sems = pltpu.SemaphoreType.DMA((n_sems,))
    held = [jax.ShapeDtypeStruct(a.shape, a.dtype) for a in comm.bufs.values()] + list(comm.fresh.values())
    res = pl.pallas_call(
        body, name=name, in_specs=[_ANY] * (len(r_names) + len(b_names)), out_specs=[_ANY] * len(held),
        out_shape=held, input_output_aliases={len(r_names) + i: i for i in range(len(b_names))},
        scratch_shapes=[sems, sems],
    )(*comm.reads.values(), *comm.bufs.values())
    return dict(zip(b_names + f_names, res))


_HBM = pl.BlockSpec(memory_space=pltpu.HBM)
_SEM = pl.BlockSpec(memory_space=pltpu.SEMAPHORE)
_EFFECT = pltpu.SideEffectType.DATAFLOW_SIDE_EFFECTING


def _start_exchanges(name, groups):
    names = [list(arrays) for _, arrays in groups]
    first = [sum(len(ns) for ns in names[:g]) for g in range(len(groups))]
    n, ng = sum(len(ns) for ns in names), len(groups)

    def body(*refs):
        for g, (jobs, _) in enumerate(groups):
            hbm = dict(zip(names[g], refs[first[g]:first[g] + len(names[g])]))
            for cp in _plan_all(jobs, hbm, refs[n + 2 * g], refs[n + 2 * g + 1])[0]:
                cp.start()
        refs[-1][...] = jnp.zeros_like(refs[-1])

    given = [pltpu.with_memory_space_constraint(
        a if isinstance(a, jax.Array) else lax.empty(a.shape, a.dtype), pltpu.HBM)
        for _, arrays in groups for a in arrays.values()]
    sems = [pltpu.SemaphoreType.DMA((sum(j.n_sems for j in jobs),)) for jobs, _ in groups for _ in range(2)]
    res = pl.pallas_call(
        body, name=name,
        out_shape=(*sems, *[pltpu.HBM(a.shape, a.dtype) for a in given], jax.ShapeDtypeStruct((8, 128), F32)),
        in_specs=[_HBM] * n, out_specs=(*[_SEM] * (2 * ng), *[_HBM] * n, pl.BlockSpec(memory_space=pltpu.VMEM)),
        input_output_aliases={i: 2 * ng + i for i in range(n)},
        compiler_params=pltpu.CompilerParams(has_side_effects=_EFFECT),
    )(*given)
    held = res[2 * ng:2 * ng + n]
    states = [(names[g], groups[g][0], res[2 * g], res[2 * g + 1], held[first[g]:first[g] + len(names[g])])
              for g in range(ng)]
    return states, res[-1]


def _start_exchange(name, jobs, arrays):
    states, token = _start_exchanges(name, [(jobs, arrays)])
    return states[0], token


def _finish_exchange(name, state, after):
    names, jobs, send_sem, recv_sem, held = state
    n = len(names)

    def body(*refs):
        hbm = dict(zip(names, refs[:n]))
        send, recv = refs[n:n + 2]
        starts, waits = _plan_all(jobs, hbm, send, recv)
        for cp in waits:
            cp.wait_recv()
        for cp in starts:
            cp.wait_send()

    res = pl.pallas_call(
        body, name=name, out_shape=tuple(pltpu.HBM(a.shape, a.dtype) for a in held),
        in_specs=[_HBM] * n + [_SEM, _SEM, _ANY], out_specs=tuple([_HBM] * n),
        input_output_aliases={i: i for i in range(n)},
        compiler_params=pltpu.CompilerParams(has_side_effects=_EFFECT),
    )(*held, send_sem, recv_sem, after)
    return dict(zip(names, res))


def _row_tile(rows, bytes_per_row):
    best = 16
    for t in range(16, rows + 1, 16):
        if rows % t == 0 and t * bytes_per_row <= 9 * 1024 * 1024:
            best = t
    return best


def _rowwise(fn, ins, out_dtypes, name, after=None):
    rows, cols = ins[0].shape
    per_row = sum(cols * a.dtype.itemsize for a in ins) + sum(cols * jnp.dtype(d).itemsize for d in out_dtypes)
    tr = _row_tile(rows, per_row)
    n_in = len(ins)

    def body(*refs):
        outs = fn(*[r[...] for r in refs[:n_in]])
        for o_ref, o in zip(refs[-len(out_dtypes):], outs):
            o_ref[...] = o.astype(o_ref.dtype)

    tile = pl.BlockSpec((tr, cols), lambda i: (i, 0))
    behind = [] if after is None else [after]
    return pl.pallas_call(
        body, name=name, grid=(rows // tr,),
        in_specs=[tile] * n_in + [pl.BlockSpec((8, 128), lambda i: (0, 0))] * len(behind),
        out_specs=[tile] * len(out_dtypes),
        out_shape=[jax.ShapeDtypeStruct((rows, cols), d) for d in out_dtypes],
        compiler_params=_params("parallel"),
    )(*ins, *behind)


def _tiled(fn, name, grid, pos, ins, outs):
    n_in = len(ins)

    def body(pos_ref, *refs):
        res = fn(*[r[...] for r in refs[:n_in]])
        for o_ref, o in zip(refs[n_in:], res):
            o_ref[...] = o.astype(o_ref.dtype)

    return pl.pallas_call(
        body, name=name,
        grid_spec=pltpu.PrefetchScalarGridSpec(
            num_scalar_prefetch=1, grid=grid,
            in_specs=[pl.BlockSpec(bs, im) for _, bs, im in ins],
            out_specs=[pl.BlockSpec(bs, im) for _, _, bs, im in outs]),
        out_shape=[jax.ShapeDtypeStruct(s, d) for s, d, _, _ in outs],
        compiler_params=_params("parallel"),
    )(pos, *[a for a, _, _ in ins])


def _adamw(w, g, m, v):
    m = ADAM_B1 * m + (1.0 - ADAM_B1) * g
    v = ADAM_B2 * v + (1.0 - ADAM_B2) * (g * g)
    m_hat = m / (1.0 - ADAM_B1 ** ADAM_STEP)
    v_hat = v / (1.0 - ADAM_B2 ** ADAM_STEP)
    return -ADAM_LR * (m_hat / (jnp.sqrt(v_hat) + ADAM_EPS) + ADAM_WD * w), m, v


def _adamw_small(params):
    n = len(params)

    def body(*refs):
        for k in range(n):
            w, g, m, v = (r[...] for r in refs[4 * k:4 * k + 4])
            for o_ref, o in zip(refs[4 * n + 3 * k:4 * n + 3 * k + 3], _adamw(w, g, m, v)):
                o_ref[...] = o

    flat = [a for p in params for a in p]
    return pl.pallas_call(
        body, name="adamw_small",
        out_shape=[jax.ShapeDtypeStruct(p[0].shape, F32) for p in params for _ in range(3)],
    )(*flat)


class _Layout:
    def __init__(self, rows, cols, stacked):
        self.rows, self.cols, self.stacked = rows, cols, stacked

    def whole(self, rows=None):
        r = self.rows if rows is None else rows
        return (N_CHIPS, r, self.cols) if self.stacked else (r, N_CHIPS * self.cols)

    def part_rows(self, h, q=0, nq=1):
        n = self.rows // 2 // nq
        return pl.ds(pl.multiple_of(h * (self.rows // 2) + q * n, 16), n)

    def half_rows(self, h):
        return self.part_rows(h)

    def block(self, ref, p, rows=slice(None)):
        if self.stacked:
            return ref.at[p, rows, :]
        return ref.at[rows, pl.ds(pl.multiple_of(p * self.cols, 128), self.cols)]

    def all_chips(self, ref, rows):
        return ref.at[:, rows, :] if self.stacked else ref.at[rows, :]


BIG = (
    _Layout(IN_SHARD, D_MODEL, True),
    _Layout(ATTN_W + CONV_W, D_MODEL // N_CHIPS, False),
    _Layout(D_MODEL // N_CHIPS, D_MODEL, True),
    _Layout(D_MODEL, FF2 // N_CHIPS, False),
    _Layout(D_FF // N_CHIPS, D_MODEL, True),
)
N_BIG = len(BIG)
_ANY = pl.BlockSpec(memory_space=pl.ANY)


def _position():
    x, y, c = lax.axis_index("x"), lax.axis_index("y"), lax.axis_index("c")
    return x, y, c, 2 * x + y


def _core_of_chip(p, c):
    return (p >> 1, p & 1, c)


def _place_cast(shard, lay, pos, name, after=None):
    rows, cols = shard.shape
    tr = _row_tile(rows, cols * 6)
    if lay.stacked:
        out = (lay.whole(), BF16, (None, tr, cols), lambda i, pos: (pos[0], i, 0))
    else:
        out = (lay.whole(), BF16, (tr, cols), lambda i, pos: (i, pos[0]))
    ins = [(shard, (tr, cols), lambda i, pos: (i, 0))]
    if after is not None:
        ins.append((after, (8, 128), lambda i, pos: (0, 0)))
    return _tiled(lambda a, *_: (a,), name, (rows // tr,), pos, ins, [out])[0]


def _place_cast_pair(top, bottom, lay, pos, name, after=None):
    rows, cols = top.shape
    ins = [(top, (rows, cols), lambda i, pos: (0, 0)), (bottom, (rows, cols), lambda i, pos: (0, 0))]
    if after is not None:
        ins.append((after, (8, 128), lambda i, pos: (0, 0)))
    return _tiled(lambda a, b, *_: (jnp.concatenate([a, b], axis=0),), name, (1,), pos, ins,
                  [(lay.whole(), BF16, (2 * rows, cols), lambda i, pos: (0, pos[0]))])[0]


def _adamw_pair(top, bottom, g, after=None):
    rows = top[0].shape[0]

    def body(*refs):
        (wa, ma, va, wb, mb, vb, g_ref), outs = refs[:7], refs[-8:]
        for (w, m, v), gg, o in (((wa, ma, va), g_ref[:rows], outs[:4]), ((wb, mb, vb), g_ref[rows:], outs[4:])):
            for o_ref, val in zip(o, (gg, *_adamw(w[...], gg, m[...], v[...]))):
                o_ref[...] = val

    behind = [] if after is None else [after[0:8, 0:128]]
    res = pl.pallas_call(
        body, name="adamw_w_br", out_shape=[jax.ShapeDtypeStruct(top[0].shape, F32)] * 8,
    )(*top, *bottom, g, *behind)
    return res[:4], res[4:]


def _remote(src, dst, send, recv, k, device):
    return pltpu.make_async_remote_copy(src_ref=src, dst_ref=dst, send_sem=send.at[k], recv_sem=recv.at[k],
                                        device_id=device, device_id_type=MESH)


def _arrival(dst, send, recv, k, me):
    return _remote(dst, dst, send, recv, k, me)


def _gather_ici(lay, name, q=0, nq=1):
    def plan(hbm, pos, send, recv, base):
        x, y, c, me = pos
        rows = lay.part_rows(c, q, nq)
        mine = lay.block(hbm[name], me, rows)
        starts = [_remote(mine, mine, send, recv, base + d - 1, _core_of_chip(me ^ d, c)) for d in (1, 2, 3)]
        waits = [_arrival(lay.block(hbm[name], me ^ d, rows), send, recv, base + d - 1, (x, y, c)) for d in (1, 2, 3)]
        return starts, waits
    return _Job(3, plan)


def _gather_d2d(lay, name, q=0, nq=1):
    def plan(hbm, pos, send, recv, base):
        x, y, c, me = pos
        starts, waits = [], []
        for d in (1, 2, 3):
            got = lay.block(hbm[name], me ^ d, lay.part_rows(c, q, nq))
            starts.append(_remote(got, got, send, recv, base + d - 1, (x, y, 1 - c)))
            waits.append(_arrival(lay.block(hbm[name], me ^ d, lay.part_rows(1 - c, q, nq)), send, recv, base + d - 1,
                                  (x, y, c)))
        return starts, waits
    return _Job(3, plan)


def _rs_pair(lay, grad, theirs):
    def plan(hbm, pos, send, recv, base):
        x, y, c, _ = pos
        out = _remote(lay.all_chips(hbm[grad], lay.half_rows(1 - c)), hbm[theirs], send, recv, base, (x, y, 1 - c))
        return [out], [_arrival(hbm[theirs], send, recv, base, (x, y, c))]
    return _Job(1, plan)


def _rs_chips(lay, sums, slots):
    def plan(hbm, pos, send, recv, base):
        x, y, c, me = pos
        starts = [_remote(lay.block(hbm[sums], me ^ d), hbm[slots].at[me], send, recv, base + d - 1,
                          _core_of_chip(me ^ d, c)) for d in (1, 2, 3)]
        waits = [_arrival(hbm[slots].at[me ^ d], send, recv, base + d - 1, (x, y, c)) for d in (1, 2, 3)]
        return starts, waits
    return _Job(3, plan)


def _rs_share(lay, shard):
    def plan(hbm, pos, send, recv, base):
        x, y, c, _ = pos
        mine = hbm[shard].at[lay.half_rows(c), :]
        other = hbm[shard].at[lay.half_rows(1 - c), :]
        return [_remote(mine, mine, send, recv, base, (x, y, 1 - c))], [_arrival(other, send, recv, base, (x, y, c))]
    return _Job(1, plan)


def _slots_shape(lay):
    return jax.ShapeDtypeStruct((N_CHIPS, lay.rows // 2, lay.cols), BF16)


def _theirs_shape(lay, dtype=BF16):
    return jax.ShapeDtypeStruct(lay.whole(lay.rows // 2), dtype)


def _pair_sum(grad, theirs, lay, pos, name):
    half = lay.rows // 2
    add = lambda a, b: (a.astype(F32) + b.astype(F32),)
    if lay.stacked:
        tr = _row_tile(half, lay.cols * 6)
        nt = half // tr
        flat = lambda a: a.reshape(-1, lay.cols)
        mine = lambda t, pos: ((t // nt) * (2 * nt) + pos[1] * nt + t % nt, 0)
        grid, blk = (N_CHIPS * nt,), (tr, lay.cols)
        grad, theirs = flat(grad), flat(theirs)
    else:
        tr = _row_tile(half, N_CHIPS * lay.cols * 6)
        nt = half // tr
        mine = lambda t, pos: (pos[1] * nt + t, 0)
        grid, blk = (nt,), (tr, N_CHIPS * lay.cols)
    same = lambda t, pos: (t, 0)
    out = _tiled(add, name, grid, pos, [(grad, blk, mine), (theirs, blk, same)], [(theirs.shape, BF16, blk, same)])[0]
    return out.reshape(lay.whole(half))


def _chip_sum(sums, slots, lay, pos, name, after=None):
    half = lay.rows // 2
    tr = _row_tile(half, lay.cols * 12)
    nt = half // tr
    blk3 = (None, tr, lay.cols)
    if lay.stacked:
        own = (sums, blk3, lambda i, pos: (pos[0], i, 0))
    else:
        own = (sums, (tr, lay.cols), lambda i, pos: (i, pos[0]))
    others = [(slots, blk3, functools.partial(lambda d, i, pos: (pos[0] ^ d, i, 0), d)) for d in (1, 2, 3)]

    def add(a, b1, b2, b3, *_):
        return (((a.astype(F32) + b1.astype(F32)) + b2.astype(F32)) + b3.astype(F32),)

    if after is not None:
        others.append((after, (8, 128), lambda i, pos: (0, 0)))
    return _tiled(add, name, (nt,), pos, [own] + others,
                  [((lay.rows, lay.cols), F32, (tr, lay.cols), lambda i, pos: (pos[1] * nt + i, 0))])[0]


N_DEV = 8


def _to_all(src, slots):
    def plan(hbm, pos, send, recv, base):
        x, y, c, _ = pos
        idx = 4 * x + 2 * y + c
        starts = [_remote(hbm[src], hbm[slots].at[idx], send, recv, base + k - 1,
                          (x ^ (k >> 2), y ^ ((k >> 1) & 1), c ^ (k & 1))) for k in range(1, N_DEV)]
        waits = [_arrival(hbm[slots].at[idx ^ k], send, recv, base + k - 1, (x, y, c)) for k in range(1, N_DEV)]
        return starts, waits
    return _Job(N_DEV - 1, plan)


def _sum_slots(own, slots, pos):
    def body(pos_ref, own_ref, slots_ref, o_ref):
        idx = 2 * pos_ref[0] + pos_ref[1]
        term = lambda q: jnp.where(idx == q, own_ref[...], slots_ref[q])
        acc = term(0)
        for q in range(1, N_DEV):
            acc = acc + term(q)
        o_ref[...] = acc

    return pl.pallas_call(
        body, name="sum_small", out_shape=jax.ShapeDtypeStruct(own.shape, F32),
        in_specs=[pl.BlockSpec(memory_space=pltpu.SMEM), pl.BlockSpec(memory_space=pltpu.VMEM),
                  pl.BlockSpec(memory_space=pltpu.VMEM)],
    )(pos, own, slots)


def _pack_rows(parts):
    padded = [jnp.pad(a, ((0, -a.shape[0] % 8), (0, 0))) for a in parts]
    starts = [sum(p.shape[0] for p in padded[:k]) for k in range(len(padded))]
    return jnp.concatenate(padded, axis=0), starts


def kernel(x, mix_norm, w_in, b_in, sinks, conv_w, w_attn_branch, w_conv_branch, w_out, ffn_norm, w_up, ffn_conv_w, w_down, final_norm, loss_target, m_mix_norm, m_w_in, m_b_in, m_sinks, m_conv_w, m_w_attn_branch, m_w_conv_branch, m_w_out, m_ffn_norm, m_w_up, m_ffn_conv_w, m_w_down, m_final_norm, v_mix_norm, v_w_in, v_b_in, v_sinks, v_conv_w, v_w_attn_branch, v_w_conv_branch, v_w_out, v_ffn_norm, v_w_up, v_ffn_conv_w, v_w_down, v_final_norm):
    me = 2 * lax.axis_index("x") + lax.axis_index("y")
    names = ("w_in", "w_br", "w_out", "w_up", "w_down")
    w_of = dict(w_in=w_in[0].T, w_out=w_out[0], w_up=w_up[0], w_down=w_down[0])
    m_of = dict(w_in=m_w_in[0].T, w_out=m_w_out[0], w_up=m_w_up[0], w_down=m_w_down[0])
    v_of = dict(w_in=v_w_in[0].T, w_out=v_w_out[0], w_up=v_w_up[0], w_down=v_w_down[0])
    ab = (w_attn_branch[0], m_w_attn_branch[0], v_w_attn_branch[0])
    cb = (w_conv_branch[0], m_w_conv_branch[0], v_w_conv_branch[0])

    pos = jnp.stack([me, lax.axis_index("c")]).astype(jnp.int32)

    lay = dict(zip(names, BIG))
    xs, target, sk = x[0], loss_target[0], sinks[0]
    s = xs.shape[0]
    tm, tm2, bk, bk2 = min(256, s), min(512, s), min(1024, s), min(2048, s)

    taps, (_, t0) = _pack_rows([conv_w[0], ffn_conv_w[0].reshape(3 * (FF2 // N_CHIPS // 128), 128)])
    placed = {"w_in": _place_cast(w_of["w_in"], lay["w_in"], pos, "cast_w_in")}
    fly_in, started = _start_exchange("gather_in_start", [_gather_ici(lay["w_in"], "w_in")], {"w_in": placed["w_in"]})
    taps_flight, started = _start_exchange("taps_start", [_to_all("v", "slots")],
                                           {"v": taps + started[0:1], "slots": jnp.zeros((N_DEV, *taps.shape), F32)})
    placed["w_br"] = _place_cast_pair(ab[0], cb[0], lay["w_br"], pos, "cast_w_br", after=started)
    for n in names[2:]:
        placed[n] = _place_cast(w_of[n], lay[n], pos, "cast_" + n, after=started)
    trio = ("w_br", "w_out")
    (fly_trio, fly_up, fly_down), started = _start_exchanges("gather_rest_start", [
        ([_gather_ici(lay[n], n) for n in ws], {n: placed[n] for n in ws}) for ws in (trio, ("w_up",), ("w_down",))])

    got = _finish_exchange("gather_in_wait", fly_in, after=started)
    w_in_full = _exchange("gather_in_d2d", [[_gather_d2d(lay["w_in"], "w_in")]], bufs=got)["w_in"].reshape(IN_W, D_MODEL)
    xn, qkv, c3, gates = _inproj_fwd(xs, mix_norm, w_in_full, b_in, tm2)
    k2 = _Carry([_gather_d2d(lay[n], n) for n in trio], bufs=_finish_exchange("gather_trio_wait", fly_trio, after=qkv))
    attn = _attn_fwd(qkv, sk, comm=k2)
    w_br = k2.out["w_br"]
    w_out_full = k2.out["w_out"].reshape(D_MODEL, D_MODEL)
    k3 = _Carry([_gather_d2d(lay["w_up"], "w_up")], bufs=_finish_exchange("gather_up_wait", fly_up, after=attn))
    taps = _finish_exchange("taps_wait", taps_flight, after=attn)
    taps = lax.dynamic_update_slice(taps["slots"], taps["v"][None], (2 * me + lax.axis_index("c"), 0, 0))
    conv_full = taps[0::2, 0:3].transpose(1, 0, 2).reshape(3, CONV_W)
    ffn_cw_full = taps[0::2, t0:t0 + 33].reshape(N_CHIPS, 3, FF2 // N_CHIPS).transpose(1, 0, 2).reshape(3, FF2)
    conv, a, cv, merged, h1, hn = _mix_fwd(xs, attn, c3, gates, conv_full, w_br, w_out_full, ffn_norm, tm2, comm=k3)
    w_up_full = k3.out["w_up"]
    w_down_full = _exchange("gather_down_d2d", [[_gather_d2d(lay["w_down"], "w_down")]],
                            bufs=_finish_exchange("gather_down_wait", fly_down, after=hn))["w_down"].reshape(D_FF, D_MODEL)
    u, up, act, dh2, loss_part, g_fn = _ffn_fwd_loss(hn, h1, w_up_full, ffn_cw_full, w_down_full,
                                                     final_norm[None, :], target, tm)

    grads, sums, slots = {}, {}, {}

    def pair(*ws):
        return _Carry([_rs_pair(lay[n], "g_" + n, "t_" + n) for n in ws], reads={"g_" + n: grads[n] for n in ws},
                      fresh={"t_" + n: _theirs_shape(lay[n], grads[n].dtype) for n in ws})

    def chips(*ws, also=None):
        k = _Carry([_rs_chips(lay[n], "s_" + n, "r_" + n) for n in ws], reads={"s_" + n: sums[n] for n in ws},
                   fresh={"r_" + n: _slots_shape(lay[n]) for n in ws})
        if also is not None:
            k = _Carry(k.jobs + also.jobs, {**k.reads, **also.reads}, None, {**k.fresh, **also.fresh})
        return k

    def pair_sums(k, *ws):
        for n in ws:
            sums[n] = _pair_sum(grads[n], k.out["t_" + n], lay[n], pos, "pair_sum_" + n)

    def take_slots(k, *ws):
        for n in ws:
            slots[n] = k.out["r_" + n]

    du, dh1, g_fcw, g_g2 = _ffn_bwd(dh2, u, up, h1, w_up_full, ffn_cw_full, w_down_full, ffn_norm, tm)
    grads["w_down"] = _wgrad(act, dh2, D_FF // 2, D_MODEL, bk2, "wgrad_down").reshape(lay["w_down"].whole())
    k4 = pair("w_down")
    grads["w_up"] = _wgrad(hn, du, D_MODEL, FF2 // 4, bk2, "wgrad_up", comm=k4)
    pair_sums(k4, "w_down")
    k5 = chips("w_down", also=pair("w_up"))
    dattn, dc3, dgt, g_cw, grads["w_br"], gw_out = _mix_bwd(
        dh1, gates, a, cv, c3, attn, conv, merged, conv_full, w_br, w_out_full, tm2, comm=k5)
    grads["w_out"] = gw_out.reshape(lay["w_out"].whole())
    take_slots(k5, "w_down")
    pair_sums(k5, "w_up")
    k6 = chips("w_up", also=pair(*trio))
    dq, dk, dv, g_sk = _attn_bwd(qkv, sk, attn, dattn, comm=k6)
    take_slots(k6, "w_up")
    pair_sums(k6, *trio)
    trio_flight, started = _start_exchange(
        "rs_chips_trio_start", [_rs_chips(lay[n], "s_" + n, "r_" + n) for n in trio],
        {**{"s_" + n: sums[n] for n in trio}, **{"r_" + n: _slots_shape(lay[n]) for n in trio}})
    behind = mix_norm + jnp.tile(started[0:1], (1, D_MODEL // 128))
    grad_x, gw_in, g_b, g_g1 = _inproj_bwd(dq, dk, dv, dc3, dgt, w_in_full, xs, xn, dh1, behind, tm2)
    grads["w_in"] = gw_in.reshape(lay["w_in"].whole())

    parts = [loss_part, g_g1, g_b, jnp.pad(g_sk[:, 0], (0, 120))[None, :], g_cw, g_g2, g_fcw, g_fn]
    packed, at = _pack_rows([p.reshape(-1, 128) for p in parts])
    small_flight, started = _start_exchange("small_start", [_to_all("v", "slots")],
                                            {"v": packed, "slots": jnp.zeros((N_DEV, *packed.shape), F32)})
    others = names[1:]
    in_flight, started = _start_exchange("rs_pair_in_start", [_rs_pair(lay["w_in"], "g", "t")],
                                         {"g": grads["w_in"], "t": _theirs_shape(lay["w_in"], F32), "behind": started})
    halves = {n: _chip_sum(sums[n], slots[n], lay[n], pos, "chip_sum_" + n, after=started) for n in ("w_up", "w_down")}
    landed = _finish_exchange("rs_pair_in_wait", in_flight, after=halves["w_down"])
    sums["w_in"] = _pair_sum(landed["g"], landed["t"], lay["w_in"], pos, "pair_sum_w_in")
    in_flight, started = _start_exchange("rs_chips_in_start", [_rs_chips(lay["w_in"], "s", "r")],
                                         {"s": sums["w_in"], "r": _slots_shape(lay["w_in"])})
    landed = _finish_exchange("rs_chips_trio_wait", trio_flight, after=started)
    for n in trio:
        halves[n] = _chip_sum(landed["s_" + n], landed["r_" + n], lay[n], pos, "chip_sum_" + n)
    shared = _exchange("share_halves", [[_rs_share(lay[n], n) for n in others]], bufs=halves)

    def adam(n, g, after=None):
        return _rowwise(lambda w, g, m, v: (g, *_adamw(w, g, m, v)), [w_of[n], g, m_of[n], v_of[n]], [F32] * 4,
                        "adamw_" + n, after=after)

    new_of, last = {}, None
    for n in ("w_up", "w_down", "w_out"):
        new_of[n] = adam(n, shared[n], last)
        last = new_of[n][1]
    new_of["w_ab"], new_of["w_cb"] = _adamw_pair(ab, cb, shared["w_br"], after=last)
    last = new_of["w_cb"][1]

    arrived = _finish_exchange("small_wait", small_flight, after=last)
    total = _sum_slots(arrived["v"], arrived["slots"], pos)
    part = lambda k: total[at[k]:at[k] + parts[k].size // 128].reshape(parts[k].shape)
    loss = total[0, 0]
    g_mix, g_b, g_g2, g_fn = part(1), part(2), part(5), part(7)
    g_sk = part(3)[:, 0:N_HEADS]
    g_cw = lax.dynamic_slice(part(4), (0, me * 128), (3, 128))
    g_fcw = lax.dynamic_slice(part(6), (0, me * (FF2 // N_CHIPS)), (3, FF2 // N_CHIPS))
    small_p = [
        (mix_norm, g_mix, m_mix_norm, v_mix_norm), (b_in, g_b, m_b_in, v_b_in), (sinks, g_sk, m_sinks, v_sinks),
        (conv_w[0], g_cw, m_conv_w[0], v_conv_w[0]), (ffn_norm, g_g2, m_ffn_norm, v_ffn_norm),
        (ffn_conv_w[0], g_fcw, m_ffn_conv_w[0], v_ffn_conv_w[0]),
        (final_norm[None, :], g_fn, m_final_norm[None, :], v_final_norm[None, :])]
    small_new = _adamw_small(small_p)
    small_new = [small_new[3 * k:3 * k + 3] for k in range(len(small_p))]

    landed = _finish_exchange("rs_chips_in_wait", in_flight, after=small_new[0][0])
    half_in = _chip_sum(landed["s"], landed["r"], lay["w_in"], pos, "chip_sum_w_in")
    shared["w_in"] = _exchange("share_in", [[_rs_share(lay["w_in"], "w_in")]], bufs={"w_in": half_in})["w_in"]
    new_of["w_in"] = [a.T for a in adam("w_in", shared["w_in"])]
    big = ("w_in", "w_ab", "w_cb", "w_out", "w_up", "w_down")
    big_g = [new_of[n][0] for n in big]
    big_new = [new_of[n][1:] for n in big]

    order = [("s", 0), ("b", 0), ("s", 1), ("s", 2), ("s", 3), ("b", 1), ("b", 2), ("b", 3), ("s", 4), ("b", 4),
             ("s", 5), ("b", 5), ("s", 6)]
    shapes = [mix_norm.shape, w_in.shape, b_in.shape, sinks.shape, conv_w.shape, w_attn_branch.shape,
              w_conv_branch.shape, w_out.shape, ffn_norm.shape, w_up.shape, ffn_conv_w.shape, w_down.shape,
              final_norm.shape]
    small_g = [p[1] for p in small_p]
    out_g = [(small_g[k] if kind == "s" else big_g[k]).reshape(shp) for (kind, k), shp in zip(order, shapes)]
    news = [[(small_new[k][j] if kind == "s" else big_new[k][j]).reshape(shp) for (kind, k), shp in zip(order, shapes)]
            for j in range(3)]
    return (loss, grad_x[None], *out_g, *news[0], *news[1], *news[2])
```

```python
import functools

import jax
import jax.numpy as jnp
from jax import lax
from jax.experimental import pallas as pl
from jax.experimental.pallas import tpu as pltpu

F32 = jnp.float32
BF16 = jnp.bfloat16

D_MODEL = 1024
HEAD_DIM = 64
N_HEADS = 8
N_KV_HEADS = 2
GROUP = N_HEADS // N_KV_HEADS
BLOCK = 128
ATTN_SCALE = HEAD_DIM ** -0.5
ATTN_W = N_HEADS * HEAD_DIM
KV_W = N_KV_HEADS * HEAD_DIM
CONV_W = 512
QKV_W = ATTN_W + 2 * KV_W
C3_W = 3 * CONV_W
GATES_W = 2 * D_MODEL
IN_W = QKV_W + C3_W + GATES_W
D_FF = 2816
FF2 = 2 * D_FF
NORM_EPS = 1e-5
N_CHIPS = 4
IN_SHARD = IN_W // N_CHIPS
NEG = -1e30

ADAM_LR = 0.001
ADAM_B1 = 0.9
ADAM_B2 = 0.999
ADAM_EPS = 1e-08
ADAM_WD = 0.01
ADAM_STEP = 10

VMEM_LIMIT = 56 * 1024 * 1024
MESH = pl.DeviceIdType.MESH

NT = (((1,), (1,)), ((), ()))
TN = (((0,), (0,)), ((), ()))


def _params(*sem):
    return pltpu.CompilerParams(dimension_semantics=sem, vmem_limit_bytes=VMEM_LIMIT)


def _resident(shape):
    return pl.BlockSpec(shape, lambda *_: (0,) * len(shape), pipeline_mode=pl.Buffered(1))


def _sigmoid(v):
    return 0.5 * jnp.tanh(0.5 * v) + 0.5


def _rstd(v):
    return lax.rsqrt(jnp.mean(v * v, axis=-1, keepdims=True) + NORM_EPS)


def _rms_bwd(dy, v, rstd, g):
    vhat = v * rstd
    t = dy * g
    return rstd * (t - vhat * jnp.mean(t * vhat, axis=-1, keepdims=True)), dy * vhat


def _taps(z, cw):
    return cw[2:3] * z + cw[1:2] * pltpu.roll(z, 1, 0) + cw[0:1] * pltpu.roll(z, 2, 0)


def _causal_conv(z, prev, cw):
    edge = _taps(jnp.concatenate([prev, z[0:8]], axis=0), cw)
    return jnp.concatenate([edge[8:16], _taps(z, cw)[8:]], axis=0)


def _rows_after(z, nxt):
    n = z.shape[0]
    edge = jnp.concatenate([z[n - 8:n], nxt], axis=0)
    return tuple(jnp.concatenate([pltpu.roll(z, n - k, 0)[:n - 8], pltpu.roll(edge, 16 - k, 0)[0:8]], axis=0)
                 for k in (1, 2))


def _inproj_fwd(x, g1, w_in, b_in, tm, comm=None):
    s = x.shape[0]

    def body(x_ref, g_ref, w_ref, b_ref, xn_ref, qkv_ref, c3_ref, gt_ref):
        xf = x_ref[...]
        xn = (xf * _rstd(xf) * g_ref[...]).astype(BF16)
        xn_ref[...] = xn

        def seg(a, b):
            return lax.dot_general(xn, w_ref[a:b, :], NT, preferred_element_type=F32) + b_ref[:, a:b]

        qkv_ref[...] = seg(0, QKV_W).astype(BF16)
        c3_ref[...] = seg(QKV_W, QKV_W + C3_W).astype(BF16)
        gt_ref[...] = seg(QKV_W + C3_W, IN_W).astype(BF16)

    row = lambda w: pl.BlockSpec((tm, w), lambda i: (i, 0))
    return _call(
        comm, body, name="inproj_fwd", grid=(s // tm,),
        in_specs=[row(D_MODEL), _resident((1, D_MODEL)), _resident((IN_W, D_MODEL)), _resident((1, IN_W))],
        out_specs=[row(D_MODEL), row(QKV_W), row(C3_W), row(GATES_W)],
        out_shape=[jax.ShapeDtypeStruct((s, D_MODEL), BF16), jax.ShapeDtypeStruct((s, QKV_W), BF16),
                   jax.ShapeDtypeStruct((s, C3_W), BF16), jax.ShapeDtypeStruct((s, GATES_W), BF16)],
        compiler_params=_params("parallel"),
    )(x, g1, w_in, b_in)


def _attn_bias():
    qi = (jnp.arange(GROUP * BLOCK) % BLOCK)[:, None]
    kj = jnp.arange(2 * BLOCK)[None, :]
    band = (kj > qi) & (kj <= qi + BLOCK)
    return jnp.stack([jnp.where(band & (kj >= BLOCK), 0.0, NEG), jnp.where(band, 0.0, NEG)]).astype(F32)


def _attn_bias_spec():
    return pl.BlockSpec((None, GROUP * BLOCK, 2 * BLOCK), lambda i: (jnp.minimum(i, 1), 0, 0))


def _sink_column(sk_ref, h):
    rows = lax.broadcasted_iota(jnp.int32, (GROUP * BLOCK, 1), 0)
    col = jnp.full((GROUP * BLOCK, 1), sk_ref[h * GROUP], F32)
    for g in range(1, GROUP):
        col = jnp.where(rows >= g * BLOCK, sk_ref[h * GROUP + g], col)
    return col


def _stack_heads(t, h):
    return jnp.concatenate(
        [t[:, (h * GROUP + g) * HEAD_DIM:(h * GROUP + g + 1) * HEAD_DIM] for g in range(GROUP)], axis=0)


def _unstack_heads(per_kv):
    return jnp.concatenate(
        [t[g * BLOCK:(g + 1) * BLOCK] for t in per_kv for g in range(GROUP)], axis=1)


def _attn_specs(nb):
    cur = lambda i: jnp.minimum(i, nb - 1)
    prev = lambda i: jnp.maximum(jnp.minimum(i, nb - 1) - 1, 0)
    q = pl.BlockSpec((BLOCK, ATTN_W), lambda i: (cur(i), 0))
    kp = pl.BlockSpec((BLOCK, KV_W), lambda i: (prev(i), ATTN_W // KV_W))
    kc = pl.BlockSpec((BLOCK, KV_W), lambda i: (cur(i), ATTN_W // KV_W))
    vp = pl.BlockSpec((BLOCK, KV_W), lambda i: (prev(i), ATTN_W // KV_W + 1))
    vc = pl.BlockSpec((BLOCK, KV_W), lambda i: (cur(i), ATTN_W // KV_W + 1))
    return q, kp, kc, vp, vc


def _attn_fwd(qkv, sinks, comm=None):
    s = qkv.shape[0]
    nb = s // BLOCK

    def body(sk_ref, bias_ref, q_ref, kp_ref, kc_ref, vp_ref, vc_ref, o_ref):
        bias = bias_ref[...]
        q, kp, kc, vp, vc = q_ref[...], kp_ref[...], kc_ref[...], vp_ref[...], vc_ref[...]
        outs = []
        for h in range(N_KV_HEADS):
            hs = slice(h * HEAD_DIM, (h + 1) * HEAD_DIM)
            k2 = jnp.concatenate([kp[:, hs], kc[:, hs]], axis=0)
            v2 = jnp.concatenate([vp[:, hs], vc[:, hs]], axis=0)
            sc = lax.dot_general(_stack_heads(q, h), k2, NT, preferred_element_type=F32) * ATTN_SCALE + bias
            sink = _sink_column(sk_ref, h)
            m = jnp.maximum(jnp.max(sc, axis=1, keepdims=True), sink)
            p = jnp.exp(sc - m)
            den = jnp.sum(p, axis=1, keepdims=True) + jnp.exp(sink - m)
            outs.append(jnp.dot(p.astype(BF16), v2, preferred_element_type=F32) / den)
        o_ref[...] = _unstack_heads(outs).astype(BF16)

    return _call(
        comm, body, name="attn_fwd", grid=(nb,),
        in_specs=[pl.BlockSpec(memory_space=pltpu.SMEM), _attn_bias_spec(), *_attn_specs(nb)],
        out_specs=pl.BlockSpec((BLOCK, ATTN_W), lambda i: (i, 0)),
        out_shape=jax.ShapeDtypeStruct((s, ATTN_W), BF16),
        compiler_params=_params("parallel"),
    )(sinks, _attn_bias(), qkv, qkv, qkv, qkv, qkv)


def _mix_fwd(x, attn, c3, gates, conv_w, w_br, w_out, g2, tm, comm=None):
    s = x.shape[0]

    def body(x_ref, at_ref, c3_ref, gt_ref, cw_ref, wbr_ref, wo_ref, g_ref,
             conv_ref, a_ref, cv_ref, mg_ref, h1_ref, hn_ref, carry_ref):
        @pl.when(pl.program_id(0) == 0)
        def _():
            carry_ref[...] = jnp.zeros_like(carry_ref)

        c3v = c3_ref[...].astype(F32)
        cb, cc, cx = c3v[:, :CONV_W], c3v[:, CONV_W:2 * CONV_W], c3v[:, 2 * CONV_W:]
        z = cc * cx
        cz = _causal_conv(z, carry_ref[...], cw_ref[...])
        carry_ref[...] = z[tm - 8:tm]
        conv = (cb * cz).astype(BF16)
        conv_ref[...] = conv
        a = jnp.dot(at_ref[...], wbr_ref[:ATTN_W, :], preferred_element_type=F32)
        cv = jnp.dot(conv, wbr_ref[ATTN_W:, :], preferred_element_type=F32)
        a_ref[...] = a.astype(BF16)
        cv_ref[...] = cv.astype(BF16)
        gt = gt_ref[...].astype(F32)
        merged = (_sigmoid(gt[:, :D_MODEL]) * a + _sigmoid(gt[:, D_MODEL:]) * cv).astype(BF16)
        mg_ref[...] = merged
        h1 = x_ref[...] + jnp.dot(merged, wo_ref[...], preferred_element_type=F32)
        h1_ref[...] = h1
        hn_ref[...] = (h1 * _rstd(h1) * g_ref[...]).astype(BF16)

    row = lambda w: pl.BlockSpec((tm, w), lambda i: (i, 0))
    return _call(
        comm, body, name="mix_fwd", grid=(s // tm,),
        in_specs=[row(D_MODEL), row(ATTN_W), row(C3_W), row(GATES_W), _resident((3, CONV_W)),
                  _resident((ATTN_W + CONV_W, D_MODEL)), _resident((D_MODEL, D_MODEL)), _resident((1, D_MODEL))],
        out_specs=[row(CONV_W), row(D_MODEL), row(D_MODEL), row(D_MODEL), row(D_MODEL), row(D_MODEL)],
        out_shape=[jax.ShapeDtypeStruct((s, CONV_W), BF16), jax.ShapeDtypeStruct((s, D_MODEL), BF16),
                   jax.ShapeDtypeStruct((s, D_MODEL), BF16), jax.ShapeDtypeStruct((s, D_MODEL), BF16),
                   jax.ShapeDtypeStruct((s, D_MODEL), F32), jax.ShapeDtypeStruct((s, D_MODEL), BF16)],
        scratch_shapes=[pltpu.VMEM((8, CONV_W), F32)],
        compiler_params=_params("arbitrary"),
    )(x, attn, c3, gates, conv_w, w_br, w_out, g2)


def _ffn_fwd_loss(hn, h1, w_up, ffn_cw, w_down, g3, target, tm):
    s = hn.shape[0]

    def body(hn_ref, h1_ref, wu_ref, cw_ref, wd_ref, g_ref, t_ref,
             u_ref, up_ref, act_ref, dh2_ref, loss_ref, gfn_ref, carry_ref):
        @pl.when(pl.program_id(0) == 0)
        def _():
            carry_ref[...] = jnp.zeros_like(carry_ref)
            loss_ref[...] = jnp.zeros_like(loss_ref)
            gfn_ref[...] = jnp.zeros_like(gfn_ref)

        u = jnp.dot(hn_ref[...], wu_ref[...], preferred_element_type=F32)
        u_ref[...] = u.astype(BF16)
        up = _causal_conv(u, carry_ref[...], cw_ref[...])
        up_ref[...] = up
        carry_ref[...] = u[tm - 8:tm]
        gate, val = up[:, :D_FF], up[:, D_FF:]
        act = (gate * _sigmoid(gate) * val).astype(BF16)
        act_ref[...] = act
        h2 = h1_ref[...] + jnp.dot(act, wd_ref[...], preferred_element_type=F32)
        rstd = _rstd(h2)
        g = g_ref[...]
        err = h2 * rstd * g - t_ref[...]
        loss_ref[...] += jnp.sum(err * err) * (0.5 / D_MODEL)
        dh2, dg = _rms_bwd(err * (1.0 / D_MODEL), h2, rstd, g)
        dh2_ref[...] = dh2
        gfn_ref[...] += jnp.sum(dg, axis=0, keepdims=True)

    row = lambda w: pl.BlockSpec((tm, w), lambda i: (i, 0))
    acc = lambda w: pl.BlockSpec((1, w), lambda i: (0, 0))
    return pl.pallas_call(
        body, name="ffn_fwd_loss", grid=(s // tm,),
        in_specs=[row(D_MODEL), row(D_MODEL), _resident((D_MODEL, FF2)), _resident((3, FF2)),
                  _resident((D_FF, D_MODEL)), _resident((1, D_MODEL)), row(D_MODEL)],
        out_specs=[row(FF2), row(FF2), row(D_FF), row(D_MODEL), acc(128), acc(D_MODEL)],
        out_shape=[jax.ShapeDtypeStruct((s, FF2), BF16), jax.ShapeDtypeStruct((s, FF2), F32),
                   jax.ShapeDtypeStruct((s, D_FF), BF16),
                   jax.ShapeDtypeStruct((s, D_MODEL), F32), jax.ShapeDtypeStruct((1, 128), F32),
                   jax.ShapeDtypeStruct((1, D_MODEL), F32)],
        scratch_shapes=[pltpu.VMEM((8, FF2), F32)],
        compiler_params=_params("arbitrary"),
    )(hn, h1, w_up, ffn_cw, w_down, g3, target)


def _ffn_bwd(dh2, u, up, h1, w_up, ffn_cw, w_down, g2, tm):
    s = dh2.shape[0]
    nt = s // tm

    def body(dh2_ref, u_ref, up_ref, h1_ref, wu_ref, cw_ref, wd_ref, g_ref,
             du_ref, dh1_ref, gcw_ref, gg_ref, carry_ref):
        @pl.when(pl.program_id(0) == 0)
        def _():
            for ref in (carry_ref, gcw_ref, gg_ref):
                ref[...] = jnp.zeros_like(ref)

        dh2v = dh2_ref[...]
        dact = lax.dot_general(dh2v.astype(BF16), wd_ref[...], NT, preferred_element_type=F32)
        upv = up_ref[...]
        gate, val = upv[:, :D_FF], upv[:, D_FF:]
        sg = _sigmoid(gate)
        dval = dact * (gate * sg)
        dgate = dact * val * (sg * (1.0 + gate * (1.0 - sg)))
        dup = jnp.concatenate([dgate, dval], axis=1)
        dup1, dup2 = _rows_after(dup, carry_ref[...])
        carry_ref[...] = dup[0:8]
        u = u_ref[...].astype(F32)
        gcw_ref[2:3, :] += jnp.sum(dup * u, axis=0, keepdims=True)
        gcw_ref[1:2, :] += jnp.sum(dup1 * u, axis=0, keepdims=True)
        gcw_ref[0:1, :] += jnp.sum(dup2 * u, axis=0, keepdims=True)
        cw = cw_ref[...]
        du = (cw[2:3] * dup + cw[1:2] * dup1 + cw[0:1] * dup2).astype(BF16)
        du_ref[...] = du
        dhn = lax.dot_general(du, wu_ref[...], NT, preferred_element_type=F32)
        h1v = h1_ref[...]
        dh1, dg = _rms_bwd(dhn, h1v, _rstd(h1v), g_ref[...])
        dh1_ref[...] = dh2v + dh1
        gg_ref[...] += jnp.sum(dg, axis=0, keepdims=True)

    row = lambda w: pl.BlockSpec((tm, w), lambda i: (nt - 1 - i, 0))
    return pl.pallas_call(
        body, name="ffn_bwd", grid=(nt,),
        in_specs=[row(D_MODEL), row(FF2), row(FF2),
                  row(D_MODEL), _resident((D_MODEL, FF2)), _resident((3, FF2)), _resident((D_FF, D_MODEL)),
                  _resident((1, D_MODEL))],
        out_specs=[row(FF2), row(D_MODEL), pl.BlockSpec((3, FF2), lambda i: (0, 0)),
                   pl.BlockSpec((1, D_MODEL), lambda i: (0, 0))],
        out_shape=[jax.ShapeDtypeStruct((s, FF2), BF16), jax.ShapeDtypeStruct((s, D_MODEL), F32),
                   jax.ShapeDtypeStruct((3, FF2), F32), jax.ShapeDtypeStruct((1, D_MODEL), F32)],
        scratch_shapes=[pltpu.VMEM((8, FF2), F32)],
        compiler_params=_params("arbitrary"),
    )(dh2, u, up, h1, w_up, ffn_cw, w_down, g2)


def _mix_bwd(dh1, gates, a, cv, c3, attn, conv, merged, conv_w, w_br, w_out, tm, comm=None):
    s = dh1.shape[0]
    nt = s // tm
    halo = 16

    def body(dh1_ref, gt_ref, a_ref, cv_ref, c3_ref, ch_ref, at_ref, cn_ref, mg_ref, cw_ref, wbr_ref,
             wo_ref, dat_ref, dc3_ref, dgt_ref, gcw_ref, gbr_ref, gout_ref, carry_ref, br_acc, out_acc):
        i = pl.program_id(0)

        @pl.when(i == 0)
        def _():
            for ref in (carry_ref, gcw_ref, br_acc, out_acc):
                ref[...] = jnp.zeros_like(ref)

        dh1v = dh1_ref[...].astype(BF16)
        out_acc[...] += lax.dot_general(mg_ref[...], dh1v, TN, preferred_element_type=F32)
        dm = lax.dot_general(dh1v, wo_ref[...], NT, preferred_element_type=F32)
        gt = gt_ref[...].astype(F32)
        sa, sc = _sigmoid(gt[:, :D_MODEL]), _sigmoid(gt[:, D_MODEL:])
        da = (dm * sa).astype(BF16)
        dcv = (dm * sc).astype(BF16)
        br_acc[:ATTN_W, :] += lax.dot_general(at_ref[...], da, TN, preferred_element_type=F32)
        br_acc[ATTN_W:, :] += lax.dot_general(cn_ref[...], dcv, TN, preferred_element_type=F32)
        dgt_ref[...] = jnp.concatenate(
            [dm * a_ref[...].astype(F32) * (sa * (1.0 - sa)), dm * cv_ref[...].astype(F32) * (sc * (1.0 - sc))],
            axis=1).astype(BF16)
        dat_ref[...] = lax.dot_general(da, wbr_ref[:ATTN_W, :], NT, preferred_element_type=F32).astype(BF16)
        dconv = lax.dot_general(dcv, wbr_ref[ATTN_W:, :], NT, preferred_element_type=F32)
        c3v = c3_ref[...].astype(F32)
        cb, cc, cx = c3v[:, :CONV_W], c3v[:, CONV_W:2 * CONV_W], c3v[:, 2 * CONV_W:]
        z = cc * cx
        chv = ch_ref[...].astype(F32)[halo - 8:halo] * (i < nt - 1).astype(F32)
        zh = chv[:, CONV_W:2 * CONV_W] * chv[:, 2 * CONV_W:]
        cw = cw_ref[...]
        cz = _causal_conv(z, zh, cw)
        dcz = dconv * cb
        dcz1, dcz2 = _rows_after(dcz, carry_ref[...])
        carry_ref[...] = dcz[0:8]
        gcw_ref[2:3, :] += jnp.sum(dcz * z, axis=0, keepdims=True)
        gcw_ref[1:2, :] += jnp.sum(dcz1 * z, axis=0, keepdims=True)
        gcw_ref[0:1, :] += jnp.sum(dcz2 * z, axis=0, keepdims=True)
        dz = cw[2:3] * dcz + cw[1:2] * dcz1 + cw[0:1] * dcz2
        dc3_ref[...] = jnp.concatenate([dconv * cz, dz * cx, dz * cc], axis=1).astype(BF16)

        @pl.when(i == nt - 1)
        def _():
            gbr_ref[...] = br_acc[...].astype(BF16)
            gout_ref[...] = out_acc[...].astype(BF16)

    row = lambda w: pl.BlockSpec((tm, w), lambda i: (nt - 1 - i, 0))
    return _call(
        comm, body, name="mix_bwd", grid=(nt,),
        in_specs=[row(D_MODEL), row(GATES_W), row(D_MODEL), row(D_MODEL), row(C3_W),
                  pl.BlockSpec((halo, C3_W), lambda i: (jnp.maximum((nt - 1 - i) * (tm // halo) - 1, 0), 0)),
                  row(ATTN_W), row(CONV_W), row(D_MODEL), _resident((3, CONV_W)),
                  _resident((ATTN_W + CONV_W, D_MODEL)), _resident((D_MODEL, D_MODEL))],
        out_specs=[row(ATTN_W), row(C3_W), row(GATES_W), pl.BlockSpec((3, CONV_W), lambda i: (0, 0)),
                   _resident((ATTN_W + CONV_W, D_MODEL)), _resident((D_MODEL, D_MODEL))],
        out_shape=[jax.ShapeDtypeStruct((s, ATTN_W), BF16), jax.ShapeDtypeStruct((s, C3_W), BF16),
                   jax.ShapeDtypeStruct((s, GATES_W), BF16), jax.ShapeDtypeStruct((3, CONV_W), F32),
                   jax.ShapeDtypeStruct((ATTN_W + CONV_W, D_MODEL), BF16),
                   jax.ShapeDtypeStruct((D_MODEL, D_MODEL), BF16)],
        scratch_shapes=[pltpu.VMEM((8, CONV_W), F32), pltpu.VMEM((ATTN_W + CONV_W, D_MODEL), F32),
                        pltpu.VMEM((D_MODEL, D_MODEL), F32)],
        compiler_params=_params("arbitrary"),
    )(dh1, gates, a, cv, c3, c3, attn, conv, merged, conv_w, w_br, w_out)


def _attn_bwd(qkv, sinks, o, do, comm=None):
    s = qkv.shape[0]
    nb = s // BLOCK

    def body(sk_ref, bias_ref, q_ref, kp_ref, kc_ref, vp_ref, vc_ref, o_ref, do_ref,
             dq_ref, dk_ref, dv_ref, dsk_ref, ck_ref, cvv_ref):
        i = pl.program_id(0)

        @pl.when(i == 0)
        def _():
            ck_ref[...] = jnp.zeros_like(ck_ref)
            cvv_ref[...] = jnp.zeros_like(cvv_ref)
            dsk_ref[...] = jnp.zeros_like(dsk_ref)

        @pl.when(i < nb)
        def _():
            bias = bias_ref[...]
            q, kp, kc, vp, vc = q_ref[...], kp_ref[...], kc_ref[...], vp_ref[...], vc_ref[...]
            ov, dov = o_ref[...], do_ref[...]
            dqs, dks, dvs = [], [], []
            for h in range(N_KV_HEADS):
                hs = slice(h * HEAD_DIM, (h + 1) * HEAD_DIM)
                k2 = jnp.concatenate([kp[:, hs], kc[:, hs]], axis=0)
                v2 = jnp.concatenate([vp[:, hs], vc[:, hs]], axis=0)
                qg, og, dog = _stack_heads(q, h), _stack_heads(ov, h), _stack_heads(dov, h)
                sc = lax.dot_general(qg, k2, NT, preferred_element_type=F32) * ATTN_SCALE + bias
                sink = _sink_column(sk_ref, h)
                m = jnp.maximum(jnp.max(sc, axis=1, keepdims=True), sink)
                p = jnp.exp(sc - m)
                psink = jnp.exp(sink - m)
                inv = 1.0 / (jnp.sum(p, axis=1, keepdims=True) + psink)
                p = p * inv
                delta = jnp.sum(dog.astype(F32) * og.astype(F32), axis=1, keepdims=True)
                dp = lax.dot_general(dog, v2, NT, preferred_element_type=F32)
                ds = (p * (dp - delta)).astype(BF16)
                dqs.append(jnp.dot(ds, k2, preferred_element_type=F32) * ATTN_SCALE)
                dks.append(lax.dot_general(ds, qg, TN, preferred_element_type=F32) * ATTN_SCALE)
                dvs.append(lax.dot_general(p.astype(BF16), dog, TN, preferred_element_type=F32))
                dsink = -(psink * inv * delta)
                for g in range(GROUP):
                    r = h * GROUP + g
                    dsk_ref[r:r + 1, :] += jnp.sum(dsink[g * BLOCK:(g + 1) * BLOCK])
            dq_ref[...] = _unstack_heads(dqs).astype(BF16)
            dk2 = jnp.concatenate(dks, axis=1)
            dv2 = jnp.concatenate(dvs, axis=1)
            dk_ref[...] = (ck_ref[...] + dk2[:BLOCK]).astype(BF16)
            dv_ref[...] = (cvv_ref[...] + dv2[:BLOCK]).astype(BF16)
            ck_ref[...] = dk2[BLOCK:]
            cvv_ref[...] = dv2[BLOCK:]

        @pl.when(i == nb)
        def _():
            dk_ref[...] = ck_ref[...].astype(BF16)
            dv_ref[...] = cvv_ref[...].astype(BF16)

    cur = lambda i: jnp.minimum(i, nb - 1)
    done = lambda i: jnp.maximum(i - 1, 0)
    return _call(
        comm, body, name="attn_bwd", grid=(nb + 1,),
        in_specs=[pl.BlockSpec(memory_space=pltpu.SMEM), _attn_bias_spec(), *_attn_specs(nb),
                  pl.BlockSpec((BLOCK, ATTN_W), lambda i: (cur(i), 0)),
                  pl.BlockSpec((BLOCK, ATTN_W), lambda i: (cur(i), 0))],
        out_specs=[pl.BlockSpec((BLOCK, ATTN_W), lambda i: (cur(i), 0)),
                   pl.BlockSpec((BLOCK, KV_W), lambda i: (done(i), 0)),
                   pl.BlockSpec((BLOCK, KV_W), lambda i: (done(i), 0)),
                   pl.BlockSpec((N_HEADS, 128), lambda i: (0, 0))],
        out_shape=[jax.ShapeDtypeStruct((s, ATTN_W), BF16), jax.ShapeDtypeStruct((s, KV_W), BF16),
                   jax.ShapeDtypeStruct((s, KV_W), BF16), jax.ShapeDtypeStruct((N_HEADS, 128), F32)],
        scratch_shapes=[pltpu.VMEM((BLOCK, KV_W), F32), pltpu.VMEM((BLOCK, KV_W), F32)],
        compiler_params=_params("arbitrary"),
    )(sinks, _attn_bias(), qkv, qkv, qkv, qkv, qkv, o, do)


def _inproj_bwd(dq, dk, dv, dc3, dgt, w_in, x, xn, dh1, g1, tm):
    s = x.shape[0]
    nt = s // tm

    def body(dq_ref, dk_ref, dv_ref, dc3_ref, dgt_ref, w_ref, x_ref, xn_ref, dh1_ref, g_ref,
             dx_ref, gw_ref, gb_ref, gg_ref, acc_ref):
        i = pl.program_id(0)

        @pl.when(i == 0)
        def _():
            for ref in (gb_ref, gg_ref, acc_ref):
                ref[...] = jnp.zeros_like(ref)

        dp = jnp.concatenate([dq_ref[...], dk_ref[...], dv_ref[...], dc3_ref[...], dgt_ref[...]], axis=1)
        acc_ref[...] += lax.dot_general(dp, xn_ref[...], TN, preferred_element_type=F32)
        gb_ref[...] += jnp.sum(dp.astype(F32), axis=0, keepdims=True)
        dxn = jnp.dot(dp, w_ref[...], preferred_element_type=F32)
        xf = x_ref[...]
        dx, dg = _rms_bwd(dxn, xf, _rstd(xf), g_ref[...])
        dx_ref[...] = dh1_ref[...] + dx
        gg_ref[...] += jnp.sum(dg, axis=0, keepdims=True)

        @pl.when(i == nt - 1)
        def _():
            gw_ref[...] = acc_ref[...].astype(BF16)

    row = lambda w: pl.BlockSpec((tm, w), lambda i: (i, 0))
    acc = lambda w: pl.BlockSpec((1, w), lambda i: (0, 0))
    return pl.pallas_call(
        body, name="inproj_bwd", grid=(nt,),
        in_specs=[row(ATTN_W), row(KV_W), row(KV_W), row(C3_W), row(GATES_W), _resident((IN_W, D_MODEL)),
                  row(D_MODEL), row(D_MODEL), row(D_MODEL), _resident((1, D_MODEL))],
        out_specs=[row(D_MODEL), _resident((IN_W, D_MODEL)), acc(IN_W), acc(D_MODEL)],
        out_shape=[jax.ShapeDtypeStruct((s, D_MODEL), F32), jax.ShapeDtypeStruct((IN_W, D_MODEL), BF16),
                   jax.ShapeDtypeStruct((1, IN_W), F32), jax.ShapeDtypeStruct((1, D_MODEL), F32)],
        scratch_shapes=[pltpu.VMEM((IN_W, D_MODEL), F32)],
        compiler_params=_params("arbitrary"),
    )(dq, dk, dv, dc3, dgt, w_in, x, xn, dh1, g1)


def _wgrad(a, b, bm, bn, bk, name, comm=None):
    s, m = a.shape
    n = b.shape[1]
    nk = s // bk

    def body(a_ref, b_ref, o_ref, acc_ref):
        k = pl.program_id(2)

        @pl.when(k == 0)
        def _():
            acc_ref[...] = jnp.zeros_like(acc_ref)

        acc_ref[...] += lax.dot_general(a_ref[...].astype(BF16), b_ref[...].astype(BF16), TN,
                                        preferred_element_type=F32)

        @pl.when(k == nk - 1)
        def _():
            o_ref[...] = acc_ref[...].astype(BF16)

    return _call(
        comm, body, name=name, grid=(m // bm, n // bn, nk),
        in_specs=[pl.BlockSpec((bk, bm), lambda i, j, k: (k, i)), pl.BlockSpec((bk, bn), lambda i, j, k: (k, j))],
        out_specs=pl.BlockSpec((bm, bn), lambda i, j, k: (i, j)),
        out_shape=jax.ShapeDtypeStruct((m, n), BF16),
        scratch_shapes=[pltpu.VMEM((bm, bn), F32)],
        compiler_params=_params("parallel", "parallel", "arbitrary"),
    )(a, b)


class _Carry:
    def __init__(self, jobs, reads=None, bufs=None, fresh=None):
        self.jobs, self.reads, self.bufs, self.fresh = jobs, reads or {}, bufs or {}, fresh or {}
        self.out = {}


class _Job:
    def __init__(self, n_sems, plan):
        self.n_sems, self.plan = n_sems, plan


def _plan_all(jobs, hbm, send, recv):
    pos = _position()
    starts, waits, base = [], [], 0
    for job in jobs:
        s, w = job.plan(hbm, pos, send, recv, base)
        starts, waits, base = starts + s, waits + w, base + job.n_sems
    return starts, waits


def _call(comm, body, **kw):
    if comm is None:
        return pl.pallas_call(body, **kw)
    grid = kw["grid"]
    single = not isinstance(kw["out_shape"], (list, tuple))
    out_shape = [kw["out_shape"]] if single else list(kw["out_shape"])
    out_specs = [kw["out_specs"]] if single else list(kw["out_specs"])
    in_specs = list(kw["in_specs"])
    scratch = list(kw.get("scratch_shapes", ()))
    r_names, b_names, f_names = list(comm.reads), list(comm.bufs), list(comm.fresh)
    n_args, n_out, n_scr = len(in_specs), len(out_shape), len(scratch)
    n_sems = sum(j.n_sems for j in comm.jobs)

    def wrapped(*refs):
        k = n_args
        hbm = dict(zip(r_names, refs[k:k + len(r_names)]))
        k += len(r_names) + len(b_names)
        outs = refs[k:k + n_out]
        k += n_out
        hbm.update(zip(b_names + f_names, refs[k:k + len(b_names) + len(f_names)]))
        k += len(b_names) + len(f_names)
        send, recv = refs[k + n_scr:]
        starts, waits = _plan_all(comm.jobs, hbm, send, recv)
        ids = [pl.program_id(a) for a in range(len(grid))]
        first = functools.reduce(jnp.logical_and, [i == 0 for i in ids])
        last = functools.reduce(jnp.logical_and, [i == g - 1 for i, g in zip(ids, grid)])

        @pl.when(first)
        def _():
            for cp in starts:
                cp.start()

        body(*refs[:n_args], *outs, *refs[k:k + n_scr])

        @pl.when(last)
        def _():
            for cp in waits:
                cp.wait_recv()
            for cp in starts:
                cp.wait_send()

    sems = pltpu.SemaphoreType.DMA((n_sems,))
    held = [jax.ShapeDtypeStruct(a.shape, a.dtype) for a in comm.bufs.values()] + list(comm.fresh.values())
    call = pl.pallas_call(
        wrapped, name=kw["name"], grid=grid,
        in_specs=in_specs + [_ANY] * (len(r_names) + len(b_names)),
        out_specs=out_specs + [_ANY] * len(held),
        out_shape=out_shape + held,
        input_output_aliases={n_args + len(r_names) + i: n_out + i for i in range(len(b_names))},
        scratch_shapes=scratch + [sems, sems],
        compiler_params=_params(*["arbitrary"] * len(grid)),
    )

    def run(*args):
        res = call(*args, *comm.reads.values(), *comm.bufs.values())
        comm.out = dict(zip(b_names + f_names, res[n_out:]))
        return res[0] if single else res[:n_out]

    return run


def _exchange(name, phases, reads=None, bufs=None, fresh=None):
    comm = _Carry([j for ph in phases for j in ph], reads, bufs, fresh)
    r_names, b_names, f_names = list(comm.reads), list(comm.bufs), list(comm.fresh)
    n_sems = sum(j.n_sems for j in comm.jobs)

    def body(*refs):
        hbm = dict(zip(r_names, refs[:len(r_names)]))
        k = len(r_names) + len(b_names)
        hbm.update(zip(b_names + f_names, refs[k:k + len(b_names) + len(f_names)]))
        send, recv = refs[-2:]
        pos = _position()
        started, base = [], 0
        for ph in phases:
            waits = []
            for job in ph:
                s, w = job.plan(hbm, pos, send, recv, base)
                base += job.n_sems
                for cp in s:
                    cp.start()
                started, waits = started + s, waits + w
            for cp in waits:
                cp.wait_recv()
        for cp in started:
            cp.wait_send()

    sems = pltpu.SemaphoreType.DMA((n_sems,))
    held = [jax.ShapeDtypeStruct(a.shape, a.dtype) for a in comm.bufs.values()] + list(comm.fresh.values())
    res = pl.pallas_call(
        body, name=name, in_specs=[_ANY] * (len(r_names) + len(b_names)), out_specs=[_ANY] * len(held),
        out_shape=held, input_output_aliases={len(r_names) + i: i for i in range(len(b_names))},
        scratch_shapes=[sems, sems],
    )(*comm.reads.values(), *comm.bufs.values())
    return dict(zip(b_names + f_names, res))


_HBM = pl.BlockSpec(memory_space=pltpu.HBM)
_SEM = pl.BlockSpec(memory_space=pltpu.SEMAPHORE)
_EFFECT = pltpu.SideEffectType.DATAFLOW_SIDE_EFFECTING


def _start_exchanges(name, groups):
    names = [list(arrays) for _, arrays in groups]
    first = [sum(len(ns) for ns in names[:g]) for g in range(len(groups))]
    n, ng = sum(len(ns) for ns in names), len(groups)

    def body(*refs):
        for g, (jobs, _) in enumerate(groups):
            hbm = dict(zip(names[g], refs[first[g]:first[g] + len(names[g])]))
            for cp in _plan_all(jobs, hbm, refs[n + 2 * g], refs[n + 2 * g + 1])[0]:
                cp.start()
        refs[-1][...] = jnp.zeros_like(refs[-1])

    given = [pltpu.with_memory_space_constraint(
        a if isinstance(a, jax.Array) else lax.empty(a.shape, a.dtype), pltpu.HBM)
        for _, arrays in groups for a in arrays.values()]
    sems = [pltpu.SemaphoreType.DMA((sum(j.n_sems for j in jobs),)) for jobs, _ in groups for _ in range(2)]
    res = pl.pallas_call(
        body, name=name,
        out_shape=(*sems, *[pltpu.HBM(a.shape, a.dtype) for a in given], jax.ShapeDtypeStruct((8, 128), F32)),
        in_specs=[_HBM] * n, out_specs=(*[_SEM] * (2 * ng), *[_HBM] * n, pl.BlockSpec(memory_space=pltpu.VMEM)),
        input_output_aliases={i: 2 * ng + i for i in range(n)},
        compiler_params=pltpu.CompilerParams(has_side_effects=_EFFECT),
    )(*given)
    held = res[2 * ng:2 * ng + n]
    states = [(names[g], groups[g][0], res[2 * g], res[2 * g + 1], held[first[g]:first[g] + len(names[g])])
              for g in range(ng)]
    return states, res[-1]


def _start_exchange(name, jobs, arrays):
    states, token = _start_exchanges(name, [(jobs, arrays)])
    return states[0], token


def _finish_exchange(name, state, after):
    names, jobs, send_sem, recv_sem, held = state
    n = len(names)

    def body(*refs):
        hbm = dict(zip(names, refs[:n]))
        send, recv = refs[n:n + 2]
        starts, waits = _plan_all(jobs, hbm, send, recv)
        for cp in waits:
            cp.wait_recv()
        for cp in starts:
            cp.wait_send()

    res = pl.pallas_call(
        body, name=name, out_shape=tuple(pltpu.HBM(a.shape, a.dtype) for a in held),
        in_specs=[_HBM] * n + [_SEM, _SEM, _ANY], out_specs=tuple([_HBM] * n),
        input_output_aliases={i: i for i in range(n)},
        compiler_params=pltpu.CompilerParams(has_side_effects=_EFFECT),
    )(*held, send_sem, recv_sem, after)
    return dict(zip(names, res))


def _row_tile(rows, bytes_per_row):
    best = 16
    for t in range(16, rows + 1, 16):
        if rows % t == 0 and t * bytes_per_row <= 9 * 1024 * 1024:
            best = t
    return best


def _rowwise(fn, ins, out_dtypes, name, after=None):
    rows, cols = ins[0].shape
    per_row = sum(cols * a.dtype.itemsize for a in ins) + sum(cols * jnp.dtype(d).itemsize for d in out_dtypes)
    tr = _row_tile(rows, per_row)
    n_in = len(ins)

    def body(*refs):
        outs = fn(*[r[...] for r in refs[:n_in]])
        for o_ref, o in zip(refs[-len(out_dtypes):], outs):
            o_ref[...] = o.astype(o_ref.dtype)

    tile = pl.BlockSpec((tr, cols), lambda i: (i, 0))
    behind = [] if after is None else [after]
    return pl.pallas_call(
        body, name=name, grid=(rows // tr,),
        in_specs=[tile] * n_in + [pl.BlockSpec((8, 128), lambda i: (0, 0))] * len(behind),
        out_specs=[tile] * len(out_dtypes),
        out_shape=[jax.ShapeDtypeStruct((rows, cols), d) for d in out_dtypes],
        compiler_params=_params("parallel"),
    )(*ins, *behind)


def _tiled(fn, name, grid, pos, ins, outs):
    n_in = len(ins)

    def body(pos_ref, *refs):
        res = fn(*[r[...] for r in refs[:n_in]])
        for o_ref, o in zip(refs[n_in:], res):
            o_ref[...] = o.astype(o_ref.dtype)

    return pl.pallas_call(
        body, name=name,
        grid_spec=pltpu.PrefetchScalarGridSpec(
            num_scalar_prefetch=1, grid=grid,
            in_specs=[pl.BlockSpec(bs, im) for _, bs, im in ins],
            out_specs=[pl.BlockSpec(bs, im) for _, _, bs, im in outs]),
        out_shape=[jax.ShapeDtypeStruct(s, d) for s, d, _, _ in outs],
        compiler_params=_params("parallel"),
    )(pos, *[a for a, _, _ in ins])


def _adamw(w, g, m, v):
    m = ADAM_B1 * m + (1.0 - ADAM_B1) * g
    v = ADAM_B2 * v + (1.0 - ADAM_B2) * (g * g)
    m_hat = m / (1.0 - ADAM_B1 ** ADAM_STEP)
    v_hat = v / (1.0 - ADAM_B2 ** ADAM_STEP)
    return -ADAM_LR * (m_hat / (jnp.sqrt(v_hat) + ADAM_EPS) + ADAM_WD * w), m, v


def _adamw_small(params):
    n = len(params)

    def body(*refs):
        for k in range(n):
            w, g, m, v = (r[...] for r in refs[4 * k:4 * k + 4])
            for o_ref, o in zip(refs[4 * n + 3 * k:4 * n + 3 * k + 3], _adamw(w, g, m, v)):
                o_ref[...] = o

    flat = [a for p in params for a in p]
    return pl.pallas_call(
        body, name="adamw_small",
        out_shape=[jax.ShapeDtypeStruct(p[0].shape, F32) for p in params for _ in range(3)],
    )(*flat)


class _Layout:
    def __init__(self, rows, cols, stacked):
        self.rows, self.cols, self.stacked = rows, cols, stacked

    def whole(self, rows=None):
        r = self.rows if rows is None else rows
        return (N_CHIPS, r, self.cols) if self.stacked else (r, N_CHIPS * self.cols)

    def part_rows(self, h, q=0, nq=1):
        n = self.rows // 2 // nq
        return pl.ds(pl.multiple_of(h * (self.rows // 2) + q * n, 16), n)

    def half_rows(self, h):
        return self.part_rows(h)

    def block(self, ref, p, rows=slice(None)):
        if self.stacked:
            return ref.at[p, rows, :]
        return ref.at[rows, pl.ds(pl.multiple_of(p * self.cols, 128), self.cols)]

    def all_chips(self, ref, rows):
        return ref.at[:, rows, :] if self.stacked else ref.at[rows, :]


BIG = (
    _Layout(IN_SHARD, D_MODEL, True),
    _Layout(ATTN_W + CONV_W, D_MODEL // N_CHIPS, False),
    _Layout(D_MODEL // N_CHIPS, D_MODEL, True),
    _Layout(D_MODEL, FF2 // N_CHIPS, False),
    _Layout(D_FF // N_CHIPS, D_MODEL, True),
)
N_BIG = len(BIG)
_ANY = pl.BlockSpec(memory_space=pl.ANY)


def _position():
    x, y, c = lax.axis_index("x"), lax.axis_index("y"), lax.axis_index("c")
    return x, y, c, 2 * x + y


def _core_of_chip(p, c):
    return (p >> 1, p & 1, c)


def _place_cast(shard, lay, pos, name, after=None):
    rows, cols = shard.shape
    tr = _row_tile(rows, cols * 6)
    if lay.stacked:
        out = (lay.whole(), BF16, (None, tr, cols), lambda i, pos: (pos[0], i, 0))
    else:
        out = (lay.whole(), BF16, (tr, cols), lambda i, pos: (i, pos[0]))
    ins = [(shard, (tr, cols), lambda i, pos: (i, 0))]
    if after is not None:
        ins.append((after, (8, 128), lambda i, pos: (0, 0)))
    return _tiled(lambda a, *_: (a,), name, (rows // tr,), pos, ins, [out])[0]


def _place_cast_pair(top, bottom, lay, pos, name, after=None):
    rows, cols = top.shape
    ins = [(top, (rows, cols), lambda i, pos: (0, 0)), (bottom, (rows, cols), lambda i, pos: (0, 0))]
    if after is not None:
        ins.append((after, (8, 128), lambda i, pos: (0, 0)))
    return _tiled(lambda a, b, *_: (jnp.concatenate([a, b], axis=0),), name, (1,), pos, ins,
                  [(lay.whole(), BF16, (2 * rows, cols), lambda i, pos: (0, pos[0]))])[0]


def _adamw_pair(top, bottom, g, after=None):
    rows = top[0].shape[0]

    def body(*refs):
        (wa, ma, va, wb, mb, vb, g_ref), outs = refs[:7], refs[-8:]
        for (w, m, v), gg, o in (((wa, ma, va), g_ref[:rows], outs[:4]), ((wb, mb, vb), g_ref[rows:], outs[4:])):
            for o_ref, val in zip(o, (gg, *_adamw(w[...], gg, m[...], v[...]))):
                o_ref[...] = val

    behind = [] if after is None else [after[0:8, 0:128]]
    res = pl.pallas_call(
        body, name="adamw_w_br", out_shape=[jax.ShapeDtypeStruct(top[0].shape, F32)] * 8,
    )(*top, *bottom, g, *behind)
    return res[:4], res[4:]


def _remote(src, dst, send, recv, k, device):
    return pltpu.make_async_remote_copy(src_ref=src, dst_ref=dst, send_sem=send.at[k], recv_sem=recv.at[k],
                                        device_id=device, device_id_type=MESH)


def _arrival(dst, send, recv, k, me):
    return _remote(dst, dst, send, recv, k, me)


def _gather_ici(lay, name, q=0, nq=1):
    def plan(hbm, pos, send, recv, base):
        x, y, c, me = pos
        rows = lay.part_rows(c, q, nq)
        mine = lay.block(hbm[name], me, rows)
        starts = [_remote(mine, mine, send, recv, base + d - 1, _core_of_chip(me ^ d, c)) for d in (1, 2, 3)]
        waits = [_arrival(lay.block(hbm[name], me ^ d, rows), send, recv, base + d - 1, (x, y, c)) for d in (1, 2, 3)]
        return starts, waits
    return _Job(3, plan)


def _gather_d2d(lay, name, q=0, nq=1):
    def plan(hbm, pos, send, recv, base):
        x, y, c, me = pos
        starts, waits = [], []
        for d in (1, 2, 3):
            got = lay.block(hbm[name], me ^ d, lay.part_rows(c, q, nq))
            starts.append(_remote(got, got, send, recv, base + d - 1, (x, y, 1 - c)))
            waits.append(_arrival(lay.block(hbm[name], me ^ d, lay.part_rows(1 - c, q, nq)), send, recv, base + d - 1,
                                  (x, y, c)))
        return starts, waits
    return _Job(3, plan)


def _rs_pair(lay, grad, theirs):
    def plan(hbm, pos, send, recv, base):
        x, y, c, _ = pos
        out = _remote(lay.all_chips(hbm[grad], lay.half_rows(1 - c)), hbm[theirs], send, recv, base, (x, y, 1 - c))
        return [out], [_arrival(hbm[theirs], send, recv, base, (x, y, c))]
    return _Job(1, plan)


def _rs_chips(lay, sums, slots):
    def plan(hbm, pos, send, recv, base):
        x, y, c, me = pos
        starts = [_remote(lay.block(hbm[sums], me ^ d), hbm[slots].at[me], send, recv, base + d - 1,
                          _core_of_chip(me ^ d, c)) for d in (1, 2, 3)]
        waits = [_arrival(hbm[slots].at[me ^ d], send, recv, base + d - 1, (x, y, c)) for d in (1, 2, 3)]
        return starts, waits
    return _Job(3, plan)


def _rs_share(lay, shard):
    def plan(hbm, pos, send, recv, base):
        x, y, c, _ = pos
        mine = hbm[shard].at[lay.half_rows(c), :]
        other = hbm[shard].at[lay.half_rows(1 - c), :]
        return [_remote(mine, mine, send, recv, base, (x, y, 1 - c))], [_arrival(other, send, recv, base, (x, y, c))]
    return _Job(1, plan)


def _slots_shape(lay):
    return jax.ShapeDtypeStruct((N_CHIPS, lay.rows // 2, lay.cols), BF16)


def _theirs_shape(lay, dtype=BF16):
    return jax.ShapeDtypeStruct(lay.whole(lay.rows // 2), dtype)


def _pair_sum(grad, theirs, lay, pos, name):
    half = lay.rows // 2
    add = lambda a, b: (a.astype(F32) + b.astype(F32),)
    if lay.stacked:
        tr = _row_tile(half, lay.cols * 6)
        nt = half // tr
        flat = lambda a: a.reshape(-1, lay.cols)
        mine = lambda t, pos: ((t // nt) * (2 * nt) + pos[1] * nt + t % nt, 0)
        grid, blk = (N_CHIPS * nt,), (tr, lay.cols)
        grad, theirs = flat(grad), flat(theirs)
    else:
        tr = _row_tile(half, N_CHIPS * lay.cols * 6)
        nt = half // tr
        mine = lambda t, pos: (pos[1] * nt + t, 0)
        grid, blk = (nt,), (tr, N_CHIPS * lay.cols)
    same = lambda t, pos: (t, 0)
    out = _tiled(add, name, grid, pos, [(grad, blk, mine), (theirs, blk, same)], [(theirs.shape, BF16, blk, same)])[0]
    return out.reshape(lay.whole(half))


def _chip_sum(sums, slots, lay, pos, name, after=None):
    half = lay.rows // 2
    tr = _row_tile(half, lay.cols * 12)
    nt = half // tr
    blk3 = (None, tr, lay.cols)
    if lay.stacked:
        own = (sums, blk3, lambda i, pos: (pos[0], i, 0))
    else:
        own = (sums, (tr, lay.cols), lambda i, pos: (i, pos[0]))
    others = [(slots, blk3, functools.partial(lambda d, i, pos: (pos[0] ^ d, i, 0), d)) for d in (1, 2, 3)]

    def add(a, b1, b2, b3, *_):
        return (((a.astype(F32) + b1.astype(F32)) + b2.astype(F32)) + b3.astype(F32),)

    if after is not None:
        others.append((after, (8, 128), lambda i, pos: (0, 0)))
    return _tiled(add, name, (nt,), pos, [own] + others,
                  [((lay.rows, lay.cols), F32, (tr, lay.cols), lambda i, pos: (pos[1] * nt + i, 0))])[0]


N_DEV = 8


def _to_all(src, slots):
    def plan(hbm, pos, send, recv, base):
        x, y, c, _ = pos
        idx = 4 * x + 2 * y + c
        starts = [_remote(hbm[src], hbm[slots].at[idx], send, recv, base + k - 1,
                          (x ^ (k >> 2), y ^ ((k >> 1) & 1), c ^ (k & 1))) for k in range(1, N_DEV)]
        waits = [_arrival(hbm[slots].at[idx ^ k], send, recv, base + k - 1, (x, y, c)) for k in range(1, N_DEV)]
        return starts, waits
    return _Job(N_DEV - 1, plan)


def _sum_slots(own, slots, pos):
    def body(pos_ref, own_ref, slots_ref, o_ref):
        idx = 2 * pos_ref[0] + pos_ref[1]
        term = lambda q: jnp.where(idx == q, own_ref[...], slots_ref[q])
        acc = term(0)
        for q in range(1, N_DEV):
            acc = acc + term(q)
        o_ref[...] = acc

    return pl.pallas_call(
        body, name="sum_small", out_shape=jax.ShapeDtypeStruct(own.shape, F32),
        in_specs=[pl.BlockSpec(memory_space=pltpu.SMEM), pl.BlockSpec(memory_space=pltpu.VMEM),
                  pl.BlockSpec(memory_space=pltpu.VMEM)],
    )(pos, own, slots)


def _pack_rows(parts):
    padded = [jnp.pad(a, ((0, -a.shape[0] % 8), (0, 0))) for a in parts]
    starts = [sum(p.shape[0] for p in padded[:k]) for k in range(len(padded))]
    return jnp.concatenate(padded, axis=0), starts


def kernel(x, mix_norm, w_in, b_in, sinks, conv_w, w_attn_branch, w_conv_branch, w_out, ffn_norm, w_up, ffn_conv_w, w_down, final_norm, loss_target, m_mix_norm, m_w_in, m_b_in, m_sinks, m_conv_w, m_w_attn_branch, m_w_conv_branch, m_w_out, m_ffn_norm, m_w_up, m_ffn_conv_w, m_w_down, m_final_norm, v_mix_norm, v_w_in, v_b_in, v_sinks, v_conv_w, v_w_attn_branch, v_w_conv_branch, v_w_out, v_ffn_norm, v_w_up, v_ffn_conv_w, v_w_down, v_final_norm):
    me = 2 * lax.axis_index("x") + lax.axis_index("y")
    names = ("w_in", "w_br", "w_out", "w_up", "w_down")
    w_of = dict(w_in=w_in[0].T, w_out=w_out[0], w_up=w_up[0], w_down=w_down[0])
    m_of = dict(w_in=m_w_in[0].T, w_out=m_w_out[0], w_up=m_w_up[0], w_down=m_w_down[0])
    v_of = dict(w_in=v_w_in[0].T, w_out=v_w_out[0], w_up=v_w_up[0], w_down=v_w_down[0])
    ab = (w_attn_branch[0], m_w_attn_branch[0], v_w_attn_branch[0])
    cb = (w_conv_branch[0], m_w_conv_branch[0], v_w_conv_branch[0])

    pos = jnp.stack([me, lax.axis_index("c")]).astype(jnp.int32)

    lay = dict(zip(names, BIG))
    xs, target, sk = x[0], loss_target[0], sinks[0]
    s = xs.shape[0]
    tm, tm2, bk, bk2 = min(256, s), min(512, s), min(1024, s), min(2048, s)

    taps, (_, t0) = _pack_rows([conv_w[0], ffn_conv_w[0].reshape(3 * (FF2 // N_CHIPS // 128), 128)])
    placed = {"w_in": _place_cast(w_of["w_in"], lay["w_in"], pos, "cast_w_in")}
    fly_in, started = _start_exchange("gather_in_start", [_gather_ici(lay["w_in"], "w_in")], {"w_in": placed["w_in"]})
    taps_flight, started = _start_exchange("taps_start", [_to_all("v", "slots")],
                                           {"v": taps + started[0:1], "slots": jnp.zeros((N_DEV, *taps.shape), F32)})
    placed["w_br"] = _place_cast_pair(ab[0], cb[0], lay["w_br"], pos, "cast_w_br", after=started)
    for n in names[2:]:
        placed[n] = _place_cast(w_of[n], lay[n], pos, "cast_" + n, after=started)
    trio = ("w_br", "w_out")
    (fly_trio, fly_up, fly_down), started = _start_exchanges("gather_rest_start", [
        ([_gather_ici(lay[n], n) for n in ws], {n: placed[n] for n in ws}) for ws in (trio, ("w_up",), ("w_down",))])

    got = _finish_exchange("gather_in_wait", fly_in, after=started)
    w_in_full = _exchange("gather_in_d2d", [[_gather_d2d(lay["w_in"], "w_in")]], bufs=got)["w_in"].reshape(IN_W, D_MODEL)
    xn, qkv, c3, gates = _inproj_fwd(xs, mix_norm, w_in_full, b_in, tm2)
    k2 = _Carry([_gather_d2d(lay[n], n) for n in trio], bufs=_finish_exchange("gather_trio_wait", fly_trio, after=qkv))
    attn = _attn_fwd(qkv, sk, comm=k2)
    w_br = k2.out["w_br"]
    w_out_full = k2.out["w_out"].reshape(D_MODEL, D_MODEL)
    k3 = _Carry([_gather_d2d(lay["w_up"], "w_up")], bufs=_finish_exchange("gather_up_wait", fly_up, after=attn))
    taps = _finish_exchange("taps_wait", taps_flight, after=attn)
    taps = lax.dynamic_update_slice(taps["slots"], taps["v"][None], (2 * me + lax.axis_index("c"), 0, 0))
    conv_full = taps[0::2, 0:3].transpose(1, 0, 2).reshape(3, CONV_W)
    ffn_cw_full = taps[0::2, t0:t0 + 33].reshape(N_CHIPS, 3, FF2 // N_CHIPS).transpose(1, 0, 2).reshape(3, FF2)
    conv, a, cv, merged, h1, hn = _mix_fwd(xs, attn, c3, gates, conv_full, w_br, w_out_full, ffn_norm, tm2, comm=k3)
    w_up_full = k3.out["w_up"]
    w_down_full = _exchange("gather_down_d2d", [[_gather_d2d(lay["w_down"], "w_down")]],
                            bufs=_finish_exchange("gather_down_wait", fly_down, after=hn))["w_down"].reshape(D_FF, D_MODEL)
    u, up, act, dh2, loss_part, g_fn = _ffn_fwd_loss(hn, h1, w_up_full, ffn_cw_full, w_down_full,
                                                     final_norm[None, :], target, tm)

    grads, sums, slots = {}, {}, {}

    def pair(*ws):
        return _Carry([_rs_pair(lay[n], "g_" + n, "t_" + n) for n in ws], reads={"g_" + n: grads[n] for n in ws},
                      fresh={"t_" + n: _theirs_shape(lay[n], grads[n].dtype) for n in ws})

    def chips(*ws, also=None):
        k = _Carry([_rs_chips(lay[n], "s_" + n, "r_" + n) for n in ws], reads={"s_" + n: sums[n] for n in ws},
                   fresh={"r_" + n: _slots_shape(lay[n]) for n in ws})
        if also is not None:
            k = _Carry(k.jobs + also.jobs, {**k.reads, **also.reads}, None, {**k.fresh, **also.fresh})
        return k

    def pair_sums(k, *ws):
        for n in ws:
            sums[n] = _pair_sum(grads[n], k.out["t_" + n], lay[n], pos, "pair_sum_" + n)

    def take_slots(k, *ws):
        for n in ws:
            slots[n] = k.out["r_" + n]

    du, dh1, g_fcw, g_g2 = _ffn_bwd(dh2, u, up, h1, w_up_full, ffn_cw_full, w_down_full, ffn_norm, tm)
    grads["w_down"] = _wgrad(act, dh2, D_FF // 2, D_MODEL, bk2, "wgrad_down").reshape(lay["w_down"].whole())
    k4 = pair("w_down")
    grads["w_up"] = _wgrad(hn, du, D_MODEL, FF2 // 4, bk2, "wgrad_up", comm=k4)
    pair_sums(k4, "w_down")
    k5 = chips("w_down", also=pair("w_up"))
    dattn, dc3, dgt, g_cw, grads["w_br"], gw_out = _mix_bwd(
        dh1, gates, a, cv, c3, attn, conv, merged, conv_full, w_br, w_out_full, tm2, comm=k5)
    grads["w_out"] = gw_out.reshape(lay["w_out"].whole())
    take_slots(k5, "w_down")
    pair_sums(k5, "w_up")
    k6 = chips("w_up", also=pair(*trio))
    dq, dk, dv, g_sk = _attn_bwd(qkv, sk, attn, dattn, comm=k6)
    take_slots(k6, "w_up")
    pair_sums(k6, *trio)
    trio_flight, started = _start_exchange(
        "rs_chips_trio_start", [_rs_chips(lay[n], "s_" + n, "r_" + n) for n in trio],
        {**{"s_" + n: sums[n] for n in trio}, **{"r_" + n: _slots_shape(lay[n]) for n in trio}})
    behind = mix_norm + jnp.tile(started[0:1], (1, D_MODEL // 128))
    grad_x, gw_in, g_b, g_g1 = _inproj_bwd(dq, dk, dv, dc3, dgt, w_in_full, xs, xn, dh1, behind, tm)
    grads["w_in"] = gw_in.reshape(lay["w_in"].whole())

    parts = [loss_part, g_g1, g_b, jnp.pad(g_sk[:, 0], (0, 120))[None, :], g_cw, g_g2, g_fcw, g_fn]
    packed, at = _pack_rows([p.reshape(-1, 128) for p in parts])
    small_flight, started = _start_exchange("small_start", [_to_all("v", "slots")],
                                            {"v": packed, "slots": jnp.zeros((N_DEV, *packed.shape), F32)})
    others = names[1:]
    in_flight, started = _start_exchange("rs_pair_in_start", [_rs_pair(lay["w_in"], "g", "t")],
                                         {"g": grads["w_in"], "t": _theirs_shape(lay["w_in"]), "behind": started})
    halves = {n: _chip_sum(sums[n], slots[n], lay[n], pos, "chip_sum_" + n, after=started) for n in ("w_up", "w_down")}
    landed = _finish_exchange("rs_pair_in_wait", in_flight, after=halves["w_down"])
    sums["w_in"] = _pair_sum(landed["g"], landed["t"], lay["w_in"], pos, "pair_sum_w_in")
    in_flight, started = _start_exchange("rs_chips_in_start", [_rs_chips(lay["w_in"], "s", "r")],
                                         {"s": sums["w_in"], "r": _slots_shape(lay["w_in"])})
    landed = _finish_exchange("rs_chips_trio_wait", trio_flight, after=started)
    for n in trio:
        halves[n] = _chip_sum(landed["s_" + n], landed["r_" + n], lay[n], pos, "chip_sum_" + n)
    shared = _exchange("share_halves", [[_rs_share(lay[n], n) for n in others]], bufs=halves)

    def adam(n, g, after=None):
        return _rowwise(lambda w, g, m, v: (g, *_adamw(w, g, m, v)), [w_of[n], g, m_of[n], v_of[n]], [F32] * 4,
                        "adamw_" + n, after=after)

    new_of, last = {}, None
    for n in ("w_up", "w_down", "w_out"):
        new_of[n] = adam(n, shared[n], last)
        last = new_of[n][1]
    new_of["w_ab"], new_of["w_cb"] = _adamw_pair(ab, cb, shared["w_br"], after=last)
    last = new_of["w_cb"][1]

    arrived = _finish_exchange("small_wait", small_flight, after=last)
    total = _sum_slots(arrived["v"], arrived["slots"], pos)
    part = lambda k: total[at[k]:at[k] + parts[k].size // 128].reshape(parts[k].shape)
    loss = total[0, 0]
    g_mix, g_b, g_g2, g_fn = part(1), part(2), part(5), part(7)
    g_sk = part(3)[:, 0:N_HEADS]
    g_cw = lax.dynamic_slice(part(4), (0, me * 128), (3, 128))
    g_fcw = lax.dynamic_slice(part(6), (0, me * (FF2 // N_CHIPS)), (3, FF2 // N_CHIPS))
    small_p = [
        (mix_norm, g_mix, m_mix_norm, v_mix_norm), (b_in, g_b, m_b_in, v_b_in), (sinks, g_sk, m_sinks, v_sinks),
        (conv_w[0], g_cw, m_conv_w[0], v_conv_w[0]), (ffn_norm, g_g2, m_ffn_norm, v_ffn_norm),
        (ffn_conv_w[0], g_fcw, m_ffn_conv_w[0], v_ffn_conv_w[0]),
        (final_norm[None, :], g_fn, m_final_norm[None, :], v_final_norm[None, :])]
    small_new = _adamw_small(small_p)
    small_new = [small_new[3 * k:3 * k + 3] for k in range(len(small_p))]

    landed = _finish_exchange("rs_chips_in_wait", in_flight, after=small_new[0][0])
    half_in = _chip_sum(landed["s"], landed["r"], lay["w_in"], pos, "chip_sum_w_in")
    shared["w_in"] = _exchange("share_in", [[_rs_share(lay["w_in"], "w_in")]], bufs={"w_in": half_in})["w_in"]
    new_of["w_in"] = [a.T for a in adam("w_in", shared["w_in"])]
    big = ("w_in", "w_ab", "w_cb", "w_out", "w_up", "w_down")
    big_g = [new_of[n][0] for n in big]
    big_new = [new_of[n][1:] for n in big]

    order = [("s", 0), ("b", 0), ("s", 1), ("s", 2), ("s", 3), ("b", 1), ("b", 2), ("b", 3), ("s", 4), ("b", 4),
             ("s", 5), ("b", 5), ("s", 6)]
    shapes = [mix_norm.shape, w_in.shape, b_in.shape, sinks.shape, conv_w.shape, w_attn_branch.shape,
              w_conv_branch.shape, w_out.shape, ffn_norm.shape, w_up.shape, ffn_conv_w.shape, w_down.shape,
              final_norm.shape]
    small_g = [p[1] for p in small_p]
    out_g = [(small_g[k] if kind == "s" else big_g[k]).reshape(shp) for (kind, k), shp in zip(order, shapes)]
    news = [[(small_new[k][j] if kind == "s" else big_new[k][j]).reshape(shp) for (kind, k), shp in zip(order, shapes)]
            for j in range(3)]
    return (loss, grad_x[None], *out_g, *news[0], *news[1], *news[2])
```

```python
import functools

import jax
import jax.numpy as jnp
from jax import lax
from jax.experimental import pallas as pl
from jax.experimental.pallas import tpu as pltpu

F32 = jnp.float32
BF16 = jnp.bfloat16

D_MODEL = 1024
HEAD_DIM = 64
N_HEADS = 8
N_KV_HEADS = 2
GROUP = N_HEADS // N_KV_HEADS
BLOCK = 128
ATTN_SCALE = HEAD_DIM ** -0.5
ATTN_W = N_HEADS * HEAD_DIM
KV_W = N_KV_HEADS * HEAD_DIM
CONV_W = 512
QKV_W = ATTN_W + 2 * KV_W
C3_W = 3 * CONV_W
GATES_W = 2 * D_MODEL
IN_W = QKV_W + C3_W + GATES_W
D_FF = 2816
FF2 = 2 * D_FF
NORM_EPS = 1e-5
N_CHIPS = 4
IN_SHARD = IN_W // N_CHIPS
NEG = -1e30

ADAM_LR = 0.001
ADAM_B1 = 0.9
ADAM_B2 = 0.999
ADAM_EPS = 1e-08
ADAM_WD = 0.01
ADAM_STEP = 10

VMEM_LIMIT = 56 * 1024 * 1024
MESH = pl.DeviceIdType.MESH

NT = (((1,), (1,)), ((), ()))
TN = (((0,), (0,)), ((), ()))


def _params(*sem):
    return pltpu.CompilerParams(dimension_semantics=sem, vmem_limit_bytes=VMEM_LIMIT)


def _resident(shape):
    return pl.BlockSpec(shape, lambda *_: (0,) * len(shape), pipeline_mode=pl.Buffered(1))


def _sigmoid(v):
    return 0.5 * jnp.tanh(0.5 * v) + 0.5


def _rstd(v):
    return lax.rsqrt(jnp.mean(v * v, axis=-1, keepdims=True) + NORM_EPS)


def _rms_bwd(dy, v, rstd, g):
    vhat = v * rstd
    t = dy * g
    return rstd * (t - vhat * jnp.mean(t * vhat, axis=-1, keepdims=True)), dy * vhat


def _taps(z, cw):
    return cw[2:3] * z + cw[1:2] * pltpu.roll(z, 1, 0) + cw[0:1] * pltpu.roll(z, 2, 0)


def _causal_conv(z, prev, cw):
    edge = _taps(jnp.concatenate([prev, z[0:8]], axis=0), cw)
    return jnp.concatenate([edge[8:16], _taps(z, cw)[8:]], axis=0)


def _rows_after(z, nxt):
    n = z.shape[0]
    edge = jnp.concatenate([z[n - 8:n], nxt], axis=0)
    return tuple(jnp.concatenate([pltpu.roll(z, n - k, 0)[:n - 8], pltpu.roll(edge, 16 - k, 0)[0:8]], axis=0)
                 for k in (1, 2))


def _inproj_fwd(x, g1, w_in, b_in, tm, comm=None):
    s = x.shape[0]

    def body(x_ref, g_ref, w_ref, b_ref, xn_ref, qkv_ref, c3_ref, gt_ref):
        xf = x_ref[...]
        xn = (xf * _rstd(xf) * g_ref[...]).astype(BF16)
        xn_ref[...] = xn

        def seg(a, b):
            return lax.dot_general(xn, w_ref[a:b, :], NT, preferred_element_type=F32) + b_ref[:, a:b]

        qkv_ref[...] = seg(0, QKV_W).astype(BF16)
        c3_ref[...] = seg(QKV_W, QKV_W + C3_W).astype(BF16)
        gt_ref[...] = seg(QKV_W + C3_W, IN_W).astype(BF16)

    row = lambda w: pl.BlockSpec((tm, w), lambda i: (i, 0))
    return _call(
        comm, body, name="inproj_fwd", grid=(s // tm,),
        in_specs=[row(D_MODEL), _resident((1, D_MODEL)), _resident((IN_W, D_MODEL)), _resident((1, IN_W))],
        out_specs=[row(D_MODEL), row(QKV_W), row(C3_W), row(GATES_W)],
        out_shape=[jax.ShapeDtypeStruct((s, D_MODEL), BF16), jax.ShapeDtypeStruct((s, QKV_W), BF16),
                   jax.ShapeDtypeStruct((s, C3_W), BF16), jax.ShapeDtypeStruct((s, GATES_W), BF16)],
        compiler_params=_params("parallel"),
    )(x, g1, w_in, b_in)


def _attn_bias():
    qi = (jnp.arange(GROUP * BLOCK) % BLOCK)[:, None]
    kj = jnp.arange(2 * BLOCK)[None, :]
    band = (kj > qi) & (kj <= qi + BLOCK)
    return jnp.stack([jnp.where(band & (kj >= BLOCK), 0.0, NEG), jnp.where(band, 0.0, NEG)]).astype(F32)


def _attn_bias_spec():
    return pl.BlockSpec((None, GROUP * BLOCK, 2 * BLOCK), lambda i: (jnp.minimum(i, 1), 0, 0))


def _sink_column(sk_ref, h):
    rows = lax.broadcasted_iota(jnp.int32, (GROUP * BLOCK, 1), 0)
    col = jnp.full((GROUP * BLOCK, 1), sk_ref[h * GROUP], F32)
    for g in range(1, GROUP):
        col = jnp.where(rows >= g * BLOCK, sk_ref[h * GROUP + g], col)
    return col


def _stack_heads(t, h):
    return jnp.concatenate(
        [t[:, (h * GROUP + g) * HEAD_DIM:(h * GROUP + g + 1) * HEAD_DIM] for g in range(GROUP)], axis=0)


def _unstack_heads(per_kv):
    return jnp.concatenate(
        [t[g * BLOCK:(g + 1) * BLOCK] for t in per_kv for g in range(GROUP)], axis=1)


def _pair_specs(npair):
    cur = lambda i: jnp.minimum(i, npair - 1)
    prev = lambda i: jnp.maximum(2 * jnp.minimum(i, npair - 1) - 1, 0)
    kv = ATTN_W // KV_W
    return (pl.BlockSpec((2 * BLOCK, ATTN_W), lambda i: (cur(i), 0)),
            pl.BlockSpec((BLOCK, KV_W), lambda i: (prev(i), kv)), pl.BlockSpec((2 * BLOCK, KV_W), lambda i: (cur(i), kv)),
            pl.BlockSpec((BLOCK, KV_W), lambda i: (prev(i), kv + 1)),
            pl.BlockSpec((2 * BLOCK, KV_W), lambda i: (cur(i), kv + 1)))


def _attn_fwd(qkv, sinks, comm=None):
    s = qkv.shape[0]
    npair = s // (2 * BLOCK)

    def body(sk_ref, bias0_ref, bias1_ref, q_ref, kp_ref, kc_ref, vp_ref, vc_ref, o_ref):
        kc, vc = kc_ref[...], vc_ref[...]
        for b, (bias_ref, kp, vp) in enumerate(((bias0_ref, kp_ref[...], vp_ref[...]),
                                                (bias1_ref, kc[:BLOCK], vc[:BLOCK]))):
            rows = slice(b * BLOCK, (b + 1) * BLOCK)
            q, bias = q_ref[rows, :], bias_ref[...]
            outs = []
            for h in range(N_KV_HEADS):
                hs = slice(h * HEAD_DIM, (h + 1) * HEAD_DIM)
                k2 = jnp.concatenate([kp[:, hs], kc[rows, hs]], axis=0)
                v2 = jnp.concatenate([vp[:, hs], vc[rows, hs]], axis=0)
                sc = lax.dot_general(_stack_heads(q, h), k2, NT, preferred_element_type=F32) * ATTN_SCALE + bias
                sink = _sink_column(sk_ref, h)
                m = jnp.maximum(jnp.max(sc, axis=1, keepdims=True), sink)
                p = jnp.exp(sc - m)
                den = jnp.sum(p, axis=1, keepdims=True) + jnp.exp(sink - m)
                outs.append(jnp.dot(p.astype(BF16), v2, preferred_element_type=F32) / den)
            o_ref[rows, :] = _unstack_heads(outs).astype(BF16)

    general = pl.BlockSpec((None, GROUP * BLOCK, 2 * BLOCK), lambda i: (1, 0, 0))
    return _call(
        comm, body, name="attn_fwd", grid=(npair,),
        in_specs=[pl.BlockSpec(memory_space=pltpu.SMEM), _attn_bias_spec(), general, *_pair_specs(npair)],
        out_specs=pl.BlockSpec((2 * BLOCK, ATTN_W), lambda i: (i, 0)),
        out_shape=jax.ShapeDtypeStruct((s, ATTN_W), BF16),
        compiler_params=_params("parallel"),
    )(sinks, _attn_bias(), _attn_bias(), qkv, qkv, qkv, qkv, qkv)


def _mix_fwd(x, attn, c3, gates, conv_w, w_br, w_out, g2, tm, comm=None):
    s = x.shape[0]

    def body(x_ref, at_ref, c3_ref, gt_ref, cw_ref, wbr_ref, wo_ref, g_ref,
             conv_ref, a_ref, cv_ref, mg_ref, h1_ref, hn_ref, carry_ref):
        @pl.when(pl.program_id(0) == 0)
        def _():
            carry_ref[...] = jnp.zeros_like(carry_ref)

        c3v = c3_ref[...].astype(F32)
        cb, cc, cx = c3v[:, :CONV_W], c3v[:, CONV_W:2 * CONV_W], c3v[:, 2 * CONV_W:]
        z = cc * cx
        cz = _causal_conv(z, carry_ref[...], cw_ref[...])
        carry_ref[...] = z[tm - 8:tm]
        conv = (cb * cz).astype(BF16)
        conv_ref[...] = conv
        a = jnp.dot(at_ref[...], wbr_ref[:ATTN_W, :], preferred_element_type=F32)
        cv = jnp.dot(conv, wbr_ref[ATTN_W:, :], preferred_element_type=F32)
        a_ref[...] = a.astype(BF16)
        cv_ref[...] = cv.astype(BF16)
        gt = gt_ref[...].astype(F32)
        merged = (_sigmoid(gt[:, :D_MODEL]) * a + _sigmoid(gt[:, D_MODEL:]) * cv).astype(BF16)
        mg_ref[...] = merged
        h1 = x_ref[...] + jnp.dot(merged, wo_ref[...], preferred_element_type=F32)
        h1_ref[...] = h1
        hn_ref[...] = (h1 * _rstd(h1) * g_ref[...]).astype(BF16)

    row = lambda w: pl.BlockSpec((tm, w), lambda i: (i, 0))
    return _call(
        comm, body, name="mix_fwd", grid=(s // tm,),
        in_specs=[row(D_MODEL), row(ATTN_W), row(C3_W), row(GATES_W), _resident((3, CONV_W)),
                  _resident((ATTN_W + CONV_W, D_MODEL)), _resident((D_MODEL, D_MODEL)), _resident((1, D_MODEL))],
        out_specs=[row(CONV_W), row(D_MODEL), row(D_MODEL), row(D_MODEL), row(D_MODEL), row(D_MODEL)],
        out_shape=[jax.ShapeDtypeStruct((s, CONV_W), BF16), jax.ShapeDtypeStruct((s, D_MODEL), BF16),
                   jax.ShapeDtypeStruct((s, D_MODEL), BF16), jax.ShapeDtypeStruct((s, D_MODEL), BF16),
                   jax.ShapeDtypeStruct((s, D_MODEL), F32), jax.ShapeDtypeStruct((s, D_MODEL), BF16)],
        scratch_shapes=[pltpu.VMEM((8, CONV_W), F32)],
        compiler_params=_params("arbitrary"),
    )(x, attn, c3, gates, conv_w, w_br, w_out, g2)


def _ffn_fwd_loss(hn, h1, w_up, ffn_cw, w_down, g3, target, tm):
    s = hn.shape[0]

    def body(hn_ref, h1_ref, wu_ref, cw_ref, wd_ref, g_ref, t_ref,
             u_ref, up_ref, act_ref, dh2_ref, loss_ref, gfn_ref, carry_ref):
        @pl.when(pl.program_id(0) == 0)
        def _():
            carry_ref[...] = jnp.zeros_like(carry_ref)
            loss_ref[...] = jnp.zeros_like(loss_ref)
            gfn_ref[...] = jnp.zeros_like(gfn_ref)

        u = jnp.dot(hn_ref[...], wu_ref[...], preferred_element_type=F32)
        u_ref[...] = u.astype(BF16)
        up = _causal_conv(u, carry_ref[...], cw_ref[...])
        up_ref[...] = up
        carry_ref[...] = u[tm - 8:tm]
        gate, val = up[:, :D_FF], up[:, D_FF:]
        act = (gate * _sigmoid(gate) * val).astype(BF16)
        act_ref[...] = act
        h2 = h1_ref[...] + jnp.dot(act, wd_ref[...], preferred_element_type=F32)
        rstd = _rstd(h2)
        g = g_ref[...]
        err = h2 * rstd * g - t_ref[...]
        loss_ref[...] += jnp.sum(err * err) * (0.5 / D_MODEL)
        dh2, dg = _rms_bwd(err * (1.0 / D_MODEL), h2, rstd, g)
        dh2_ref[...] = dh2
        gfn_ref[...] += jnp.sum(dg, axis=0, keepdims=True)

    row = lambda w: pl.BlockSpec((tm, w), lambda i: (i, 0))
    acc = lambda w: pl.BlockSpec((1, w), lambda i: (0, 0))
    return pl.pallas_call(
        body, name="ffn_fwd_loss", grid=(s // tm,),
        in_specs=[row(D_MODEL), row(D_MODEL), _resident((D_MODEL, FF2)), _resident((3, FF2)),
                  _resident((D_FF, D_MODEL)), _resident((1, D_MODEL)), row(D_MODEL)],
        out_specs=[row(FF2), row(FF2), row(D_FF), row(D_MODEL), acc(128), acc(D_MODEL)],
        out_shape=[jax.ShapeDtypeStruct((s, FF2), BF16), jax.ShapeDtypeStruct((s, FF2), F32),
                   jax.ShapeDtypeStruct((s, D_FF), BF16),
                   jax.ShapeDtypeStruct((s, D_MODEL), F32), jax.ShapeDtypeStruct((1, 128), F32),
                   jax.ShapeDtypeStruct((1, D_MODEL), F32)],
        scratch_shapes=[pltpu.VMEM((8, FF2), F32)],
        compiler_params=_params("arbitrary"),
    )(hn, h1, w_up, ffn_cw, w_down, g3, target)


def _ffn_bwd(dh2, u, up, h1, w_up, ffn_cw, w_down, g2, tm):
    s = dh2.shape[0]
    nt = s // tm

    def body(dh2_ref, u_ref, up_ref, h1_ref, wu_ref, cw_ref, wd_ref, g_ref,
             du_ref, dh1_ref, gcw_ref, gg_ref, carry_ref):
        @pl.when(pl.program_id(0) == 0)
        def _():
            for ref in (carry_ref, gcw_ref, gg_ref):
                ref[...] = jnp.zeros_like(ref)

        dh2v = dh2_ref[...]
        dact = lax.dot_general(dh2v.astype(BF16), wd_ref[...], NT, preferred_element_type=F32)
        upv = up_ref[...]
        gate, val = upv[:, :D_FF], upv[:, D_FF:]
        sg = _sigmoid(gate)
        dval = dact * (gate * sg)
        dgate = dact * val * (sg * (1.0 + gate * (1.0 - sg)))
        dup = jnp.concatenate([dgate, dval], axis=1)
        dup1, dup2 = _rows_after(dup, carry_ref[...])
        carry_ref[...] = dup[0:8]
        u = u_ref[...].astype(F32)
        gcw_ref[2:3, :] += jnp.sum(dup * u, axis=0, keepdims=True)
        gcw_ref[1:2, :] += jnp.sum(dup1 * u, axis=0, keepdims=True)
        gcw_ref[0:1, :] += jnp.sum(dup2 * u, axis=0, keepdims=True)
        cw = cw_ref[...]
        du = (cw[2:3] * dup + cw[1:2] * dup1 + cw[0:1] * dup2).astype(BF16)
        du_ref[...] = du
        dhn = lax.dot_general(du, wu_ref[...], NT, preferred_element_type=F32)
        h1v = h1_ref[...]
        dh1, dg = _rms_bwd(dhn, h1v, _rstd(h1v), g_ref[...])
        dh1_ref[...] = dh2v + dh1
        gg_ref[...] += jnp.sum(dg, axis=0, keepdims=True)

    row = lambda w: pl.BlockSpec((tm, w), lambda i: (nt - 1 - i, 0))
    return pl.pallas_call(
        body, name="ffn_bwd", grid=(nt,),
        in_specs=[row(D_MODEL), row(FF2), row(FF2),
                  row(D_MODEL), _resident((D_MODEL, FF2)), _resident((3, FF2)), _resident((D_FF, D_MODEL)),
                  _resident((1, D_MODEL))],
        out_specs=[row(FF2), row(D_MODEL), pl.BlockSpec((3, FF2), lambda i: (0, 0)),
                   pl.BlockSpec((1, D_MODEL), lambda i: (0, 0))],
        out_shape=[jax.ShapeDtypeStruct((s, FF2), BF16), jax.ShapeDtypeStruct((s, D_MODEL), F32),
                   jax.ShapeDtypeStruct((3, FF2), F32), jax.ShapeDtypeStruct((1, D_MODEL), F32)],
        scratch_shapes=[pltpu.VMEM((8, FF2), F32)],
        compiler_params=_params("arbitrary"),
    )(dh2, u, up, h1, w_up, ffn_cw, w_down, g2)


def _mix_bwd(dh1, gates, a, cv, c3, attn, conv, merged, conv_w, w_br, w_out, tm, comm=None):
    s = dh1.shape[0]
    nt = s // tm
    halo = 16

    def body(dh1_ref, gt_ref, a_ref, cv_ref, c3_ref, ch_ref, at_ref, cn_ref, mg_ref, cw_ref, wbr_ref,
             wo_ref, dat_ref, dc3_ref, dgt_ref, gcw_ref, gbr_ref, gout_ref, carry_ref, br_acc, out_acc):
        i = pl.program_id(0)

        @pl.when(i == 0)
        def _():
            for ref in (carry_ref, gcw_ref, br_acc, out_acc):
                ref[...] = jnp.zeros_like(ref)

        dh1v = dh1_ref[...].astype(BF16)
        out_acc[...] += lax.dot_general(mg_ref[...], dh1v, TN, preferred_element_type=F32)
        dm = lax.dot_general(dh1v, wo_ref[...], NT, preferred_element_type=F32)
        gt = gt_ref[...].astype(F32)
        sa, sc = _sigmoid(gt[:, :D_MODEL]), _sigmoid(gt[:, D_MODEL:])
        da = (dm * sa).astype(BF16)
        dcv = (dm * sc).astype(BF16)
        br_acc[:ATTN_W, :] += lax.dot_general(at_ref[...], da, TN, preferred_element_type=F32)
        br_acc[ATTN_W:, :] += lax.dot_general(cn_ref[...], dcv, TN, preferred_element_type=F32)
        dgt_ref[...] = jnp.concatenate(
            [dm * a_ref[...].astype(F32) * (sa * (1.0 - sa)), dm * cv_ref[...].astype(F32) * (sc * (1.0 - sc))],
            axis=1).astype(BF16)
        dat_ref[...] = lax.dot_general(da, wbr_ref[:ATTN_W, :], NT, preferred_element_type=F32).astype(BF16)
        dconv = lax.dot_general(dcv, wbr_ref[ATTN_W:, :], NT, preferred_element_type=F32)
        c3v = c3_ref[...].astype(F32)
        cb, cc, cx = c3v[:, :CONV_W], c3v[:, CONV_W:2 * CONV_W], c3v[:, 2 * CONV_W:]
        z = cc * cx
        chv = ch_ref[...].astype(F32)[halo - 8:halo] * (i < nt - 1).astype(F32)
        zh = chv[:, CONV_W:2 * CONV_W] * chv[:, 2 * CONV_W:]
        cw = cw_ref[...]
        cz = _causal_conv(z, zh, cw)
        dcz = dconv * cb
        dcz1, dcz2 = _rows_after(dcz, carry_ref[...])
        carry_ref[...] = dcz[0:8]
        gcw_ref[2:3, :] += jnp.sum(dcz * z, axis=0, keepdims=True)
        gcw_ref[1:2, :] += jnp.sum(dcz1 * z, axis=0, keepdims=True)
        gcw_ref[0:1, :] += jnp.sum(dcz2 * z, axis=0, keepdims=True)
        dz = cw[2:3] * dcz + cw[1:2] * dcz1 + cw[0:1] * dcz2
        dc3_ref[...] = jnp.concatenate([dconv * cz, dz * cx, dz * cc], axis=1).astype(BF16)

        @pl.when(i == nt - 1)
        def _():
            gbr_ref[...] = br_acc[...].astype(BF16)
            gout_ref[...] = out_acc[...].astype(BF16)

    row = lambda w: pl.BlockSpec((tm, w), lambda i: (nt - 1 - i, 0))
    return _call(
        comm, body, name="mix_bwd", grid=(nt,),
        in_specs=[row(D_MODEL), row(GATES_W), row(D_MODEL), row(D_MODEL), row(C3_W),
                  pl.BlockSpec((halo, C3_W), lambda i: (jnp.maximum((nt - 1 - i) * (tm // halo) - 1, 0), 0)),
                  row(ATTN_W), row(CONV_W), row(D_MODEL), _resident((3, CONV_W)),
                  _resident((ATTN_W + CONV_W, D_MODEL)), _resident((D_MODEL, D_MODEL))],
        out_specs=[row(ATTN_W), row(C3_W), row(GATES_W), pl.BlockSpec((3, CONV_W), lambda i: (0, 0)),
                   _resident((ATTN_W + CONV_W, D_MODEL)), _resident((D_MODEL, D_MODEL))],
        out_shape=[jax.ShapeDtypeStruct((s, ATTN_W), BF16), jax.ShapeDtypeStruct((s, C3_W), BF16),
                   jax.ShapeDtypeStruct((s, GATES_W), BF16), jax.ShapeDtypeStruct((3, CONV_W), F32),
                   jax.ShapeDtypeStruct((ATTN_W + CONV_W, D_MODEL), BF16),
                   jax.ShapeDtypeStruct((D_MODEL, D_MODEL), BF16)],
        scratch_shapes=[pltpu.VMEM((8, CONV_W), F32), pltpu.VMEM((ATTN_W + CONV_W, D_MODEL), F32),
                        pltpu.VMEM((D_MODEL, D_MODEL), F32)],
        compiler_params=_params("arbitrary"),
    )(dh1, gates, a, cv, c3, c3, attn, conv, merged, conv_w, w_br, w_out)


def _attn_bwd(qkv, sinks, o, do, comm=None):
    s = qkv.shape[0]
    npair = s // (2 * BLOCK)

    def one_block(sk_ref, bias, q, kp, kc, vp, vc, ov, dov, dsk_ref):
        dqs, dks, dvs = [], [], []
        for h in range(N_KV_HEADS):
            hs = slice(h * HEAD_DIM, (h + 1) * HEAD_DIM)
            k2 = jnp.concatenate([kp[:, hs], kc[:, hs]], axis=0)
            v2 = jnp.concatenate([vp[:, hs], vc[:, hs]], axis=0)
            qg, og, dog = _stack_heads(q, h), _stack_heads(ov, h), _stack_heads(dov, h)
            sc = lax.dot_general(qg, k2, NT, preferred_element_type=F32) * ATTN_SCALE + bias
            sink = _sink_column(sk_ref, h)
            m = jnp.maximum(jnp.max(sc, axis=1, keepdims=True), sink)
            p = jnp.exp(sc - m)
            psink = jnp.exp(sink - m)
            inv = 1.0 / (jnp.sum(p, axis=1, keepdims=True) + psink)
            p = p * inv
            delta = jnp.sum(dog.astype(F32) * og.astype(F32), axis=1, keepdims=True)
            dp = lax.dot_general(dog, v2, NT, preferred_element_type=F32)
            ds = (p * (dp - delta)).astype(BF16)
            dqs.append(jnp.dot(ds, k2, preferred_element_type=F32) * ATTN_SCALE)
            dks.append(lax.dot_general(ds, qg, TN, preferred_element_type=F32) * ATTN_SCALE)
            dvs.append(lax.dot_general(p.astype(BF16), dog, TN, preferred_element_type=F32))
            dsink = -(psink * inv * delta)
            for g in range(GROUP):
                r = h * GROUP + g
                dsk_ref[r:r + 1, :] += jnp.sum(dsink[g * BLOCK:(g + 1) * BLOCK])
        return _unstack_heads(dqs), jnp.concatenate(dks, axis=1), jnp.concatenate(dvs, axis=1)

    def body(sk_ref, bias0_ref, bias1_ref, q_ref, kp_ref, kc_ref, vp_ref, vc_ref, o_ref, do_ref,
             dq_ref, dke_ref, dko_ref, dve_ref, dvo_ref, dsk_ref, ck_ref, cvv_ref):
        i = pl.program_id(0)

        @pl.when(i == 0)
        def _():
            for ref in (ck_ref, cvv_ref, dsk_ref):
                ref[...] = jnp.zeros_like(ref)

        @pl.when(i < npair)
        def _():
            kc, vc = kc_ref[...], vc_ref[...]
            first, second = slice(0, BLOCK), slice(BLOCK, 2 * BLOCK)
            dq0, dk0, dv0 = one_block(sk_ref, bias0_ref[...], q_ref[first, :], kp_ref[...], kc[first], vp_ref[...],
                                      vc[first], o_ref[first, :], do_ref[first, :], dsk_ref)
            dq1, dk1, dv1 = one_block(sk_ref, bias1_ref[...], q_ref[second, :], kc[first], kc[second], vc[first],
                                      vc[second], o_ref[second, :], do_ref[second, :], dsk_ref)
            dq_ref[first, :] = dq0.astype(BF16)
            dq_ref[second, :] = dq1.astype(BF16)
            dko_ref[...] = (ck_ref[...] + dk0[:BLOCK]).astype(BF16)
            dvo_ref[...] = (cvv_ref[...] + dv0[:BLOCK]).astype(BF16)
            dke_ref[...] = (dk0[BLOCK:] + dk1[:BLOCK]).astype(BF16)
            dve_ref[...] = (dv0[BLOCK:] + dv1[:BLOCK]).astype(BF16)
            ck_ref[...] = dk1[BLOCK:]
            cvv_ref[...] = dv1[BLOCK:]

        @pl.when(i == npair)
        def _():
            dko_ref[...] = ck_ref[...].astype(BF16)
            dvo_ref[...] = cvv_ref[...].astype(BF16)

    cur = lambda i: jnp.minimum(i, npair - 1)
    done = lambda i: jnp.maximum(i - 1, 0)
    rows = pl.BlockSpec((2 * BLOCK, ATTN_W), lambda i: (cur(i), 0))
    even = pl.BlockSpec((BLOCK, KV_W), lambda i: (cur(i), 0))
    odd = pl.BlockSpec((BLOCK, KV_W), lambda i: (done(i), 0))
    general = pl.BlockSpec((None, GROUP * BLOCK, 2 * BLOCK), lambda i: (1, 0, 0))
    half = jax.ShapeDtypeStruct((s // 2, KV_W), BF16)
    return _call(
        comm, body, name="attn_bwd", grid=(npair + 1,),
        in_specs=[pl.BlockSpec(memory_space=pltpu.SMEM), _attn_bias_spec(), general, *_pair_specs(npair), rows, rows],
        out_specs=[rows, even, odd, even, odd, pl.BlockSpec((N_HEADS, 128), lambda i: (0, 0))],
        out_shape=[jax.ShapeDtypeStruct((s, ATTN_W), BF16), half, half, half, half,
                   jax.ShapeDtypeStruct((N_HEADS, 128), F32)],
        scratch_shapes=[pltpu.VMEM((BLOCK, KV_W), F32), pltpu.VMEM((BLOCK, KV_W), F32)],
        compiler_params=_params("arbitrary"),
    )(sinks, _attn_bias(), _attn_bias(), qkv, qkv, qkv, qkv, qkv, o, do)


def _inproj_bwd(dq, dk, dv, dc3, dgt, w_in, x, xn, dh1, g1):
    s = x.shape[0]
    tm = min(2 * BLOCK, s)
    nt = s // tm

    def body(dq_ref, dke_ref, dko_ref, dve_ref, dvo_ref, dc3_ref, dgt_ref, w_ref, x_ref, xn_ref, dh1_ref, g_ref,
             dx_ref, gw_ref, gb_ref, gg_ref, acc_ref):
        i = pl.program_id(0)

        @pl.when(i == 0)
        def _():
            for ref in (gb_ref, gg_ref, acc_ref):
                ref[...] = jnp.zeros_like(ref)

        dk = jnp.concatenate([dke_ref[...], dko_ref[...]], axis=0)
        dv = jnp.concatenate([dve_ref[...], dvo_ref[...]], axis=0)
        dp = jnp.concatenate([dq_ref[...], dk, dv, dc3_ref[...], dgt_ref[...]], axis=1)
        acc_ref[...] += lax.dot_general(dp, xn_ref[...], TN, preferred_element_type=F32)
        gb_ref[...] += jnp.sum(dp.astype(F32), axis=0, keepdims=True)
        dxn = jnp.dot(dp, w_ref[...], preferred_element_type=F32)
        xf = x_ref[...]
        dx, dg = _rms_bwd(dxn, xf, _rstd(xf), g_ref[...])
        dx_ref[...] = dh1_ref[...] + dx
        gg_ref[...] += jnp.sum(dg, axis=0, keepdims=True)

        @pl.when(i == nt - 1)
        def _():
            gw_ref[...] = acc_ref[...].astype(BF16)

    row = lambda w: pl.BlockSpec((tm, w), lambda i: (i, 0))
    acc = lambda w: pl.BlockSpec((1, w), lambda i: (0, 0))
    block = pl.BlockSpec((tm // 2, KV_W), lambda i: (i, 0))
    return pl.pallas_call(
        body, name="inproj_bwd", grid=(nt,),
        in_specs=[row(ATTN_W), block, block, block, block, row(C3_W), row(GATES_W), _resident((IN_W, D_MODEL)),
                  row(D_MODEL), row(D_MODEL), row(D_MODEL), _resident((1, D_MODEL))],
        out_specs=[row(D_MODEL), _resident((IN_W, D_MODEL)), acc(IN_W), acc(D_MODEL)],
        out_shape=[jax.ShapeDtypeStruct((s, D_MODEL), F32), jax.ShapeDtypeStruct((IN_W, D_MODEL), BF16),
                   jax.ShapeDtypeStruct((1, IN_W), F32), jax.ShapeDtypeStruct((1, D_MODEL), F32)],
        scratch_shapes=[pltpu.VMEM((IN_W, D_MODEL), F32)],
        compiler_params=_params("arbitrary"),
    )(dq, *dk, *dv, dc3, dgt, w_in, x, xn, dh1, g1)


def _wgrad(a, b, bm, bn, bk, name, comm=None):
    s, m = a.shape
    n = b.shape[1]
    nk = s // bk

    def body(a_ref, b_ref, o_ref, acc_ref):
        k = pl.program_id(2)

        @pl.when(k == 0)
        def _():
            acc_ref[...] = jnp.zeros_like(acc_ref)

        acc_ref[...] += lax.dot_general(a_ref[...].astype(BF16), b_ref[...].astype(BF16), TN,
                                        preferred_element_type=F32)

        @pl.when(k == nk - 1)
        def _():
            o_ref[...] = acc_ref[...].astype(BF16)

    return _call(
        comm, body, name=name, grid=(m // bm, n // bn, nk),
        in_specs=[pl.BlockSpec((bk, bm), lambda i, j, k: (k, i)), pl.BlockSpec((bk, bn), lambda i, j, k: (k, j))],
        out_specs=pl.BlockSpec((bm, bn), lambda i, j, k: (i, j)),
        out_shape=jax.ShapeDtypeStruct((m, n), BF16),
        scratch_shapes=[pltpu.VMEM((bm, bn), F32)],
        compiler_params=_params("parallel", "parallel", "arbitrary"),
    )(a, b)


class _Carry:
    def __init__(self, jobs, reads=None, bufs=None, fresh=None):
        self.jobs, self.reads, self.bufs, self.fresh = jobs, reads or {}, bufs or {}, fresh or {}
        self.out = {}


class _Job:
    def __init__(self, n_sems, plan):
        self.n_sems, self.plan = n_sems, plan


def _plan_all(jobs, hbm, send, recv):
    pos = _position()
    starts, waits, base = [], [], 0
    for job in jobs:
        s, w = job.plan(hbm, pos, send, recv, base)
        starts, waits, base = starts + s, waits + w, base + job.n_sems
    return starts, waits


def _call(comm, body, **kw):
    if comm is None:
        return pl.pallas_call(body, **kw)
    grid = kw["grid"]
    single = not isinstance(kw["out_shape"], (list, tuple))
    out_shape = [kw["out_shape"]] if single else list(kw["out_shape"])
    out_specs = [kw["out_specs"]] if single else list(kw["out_specs"])
    in_specs = list(kw["in_specs"])
    scratch = list(kw.get("scratch_shapes", ()))
    r_names, b_names, f_names = list(comm.reads), list(comm.bufs), list(comm.fresh)
    n_args, n_out, n_scr = len(in_specs), len(out_shape), len(scratch)
    n_sems = sum(j.n_sems for j in comm.jobs)

    def wrapped(*refs):
        k = n_args
        hbm = dict(zip(r_names, refs[k:k + len(r_names)]))
        k += len(r_names) + len(b_names)
        outs = refs[k:k + n_out]
        k += n_out
        hbm.update(zip(b_names + f_names, refs[k:k + len(b_names) + len(f_names)]))
        k += len(b_names) + len(f_names)
        send, recv = refs[k + n_scr:]
        starts, waits = _plan_all(comm.jobs, hbm, send, recv)
        ids = [pl.program_id(a) for a in range(len(grid))]
        first = functools.reduce(jnp.logical_and, [i == 0 for i in ids])
        last = functools.reduce(jnp.logical_and, [i == g - 1 for i, g in zip(ids, grid)])

        @pl.when(first)
        def _():
            for cp in starts:
                cp.start()

        body(*refs[:n_args], *outs, *refs[k:k + n_scr])

        @pl.when(last)
        def _():
            for cp in waits:
                cp.wait_recv()
            for cp in starts:
                cp.wait_send()

    sems = pltpu.SemaphoreType.DMA((n_sems,))
    held = [jax.ShapeDtypeStruct(a.shape, a.dtype) for a in comm.bufs.values()] + list(comm.fresh.values())
    call = pl.pallas_call(
        wrapped, name=kw["name"], grid=grid,
        in_specs=in_specs + [_ANY] * (len(r_names) + len(b_names)),
        out_specs=out_specs + [_ANY] * len(held),
        out_shape=out_shape + held,
        input_output_aliases={n_args + len(r_names) + i: n_out + i for i in range(len(b_names))},
        scratch_shapes=scratch + [sems, sems],
        compiler_params=_params(*["arbitrary"] * len(grid)),
    )

    def run(*args):
        res = call(*args, *comm.reads.values(), *comm.bufs.values())
        comm.out = dict(zip(b_names + f_names, res[n_out:]))
        return res[0] if single else res[:n_out]

    return run


def _exchange(name, phases, reads=None, bufs=None, fresh=None):
    comm = _Carry([j for ph in phases for j in ph], reads, bufs, fresh)
    r_names, b_names, f_names = list(comm.reads), list(comm.bufs), list(comm.fresh)
    n_sems = sum(j.n_sems for j in comm.jobs)

    def body(*refs):
        hbm = dict(zip(r_names, refs[:len(r_names)]))
        k = len(r_names) + len(b_names)
        hbm.update(zip(b_names + f_names, refs[k:k + len(b_names) + len(f_names)]))
        send, recv = refs[-2:]
        pos = _position()
        started, base = [], 0
        for ph in phases:
            waits = []
            for job in ph:
                s, w = job.plan(hbm, pos, send, recv, base)
                base += job.n_sems
                for cp in s:
                    cp.start()
                started, waits = started + s, waits + w
            for cp in waits:
                cp.wait_recv()
        for cp in started:
            cp.wait_send()

    sems = pltpu.SemaphoreType.DMA((n_sems,))
    held = [jax.ShapeDtypeStruct(a.shape, a.dtype) for a in comm.bufs.values()] + list(comm.fresh.values())
    res = pl.pallas_call(
        body, name=name, in_specs=[_ANY] * (len(r_names) + len(b_names)), out_specs=[_ANY] * len(held),
        out_shape=held, input_output_aliases={len(r_names) + i: i for i in range(len(b_names))},
        scratch_shapes=[sems, sems],
    )(*comm.reads.values(), *comm.bufs.values())
    return dict(zip(b_names + f_names, res))


_HBM = pl.BlockSpec(memory_space=pltpu.HBM)
_SEM = pl.BlockSpec(memory_space=pltpu.SEMAPHORE)
_EFFECT = pltpu.SideEffectType.DATAFLOW_SIDE_EFFECTING


def _start_exchanges(name, groups):
    names = [list(arrays) for _, arrays in groups]
    first = [sum(len(ns) for ns in names[:g]) for g in range(len(groups))]
    n, ng = sum(len(ns) for ns in names), len(groups)

    def body(*refs):
        for g, (jobs, _) in enumerate(groups):
            hbm = dict(zip(names[g], refs[first[g]:first[g] + len(names[g])]))
            for cp in _plan_all(jobs, hbm, refs[n + 2 * g], refs[n + 2 * g + 1])[0]:
                cp.start()
        refs[-1][...] = jnp.zeros_like(refs[-1])

    given = [pltpu.with_memory_space_constraint(
        a if isinstance(a, jax.Array) else lax.empty(a.shape, a.dtype), pltpu.HBM)
        for _, arrays in groups for a in arrays.values()]
    sems = [pltpu.SemaphoreType.DMA((sum(j.n_sems for j in jobs),)) for jobs, _ in groups for _ in range(2)]
    res = pl.pallas_call(
        body, name=name,
        out_shape=(*sems, *[pltpu.HBM(a.shape, a.dtype) for a in given], jax.ShapeDtypeStruct((8, 128), F32)),
        in_specs=[_HBM] * n, out_specs=(*[_SEM] * (2 * ng), *[_HBM] * n, pl.BlockSpec(memory_space=pltpu.VMEM)),
        input_output_aliases={i: 2 * ng + i for i in range(n)},
        compiler_params=pltpu.CompilerParams(has_side_effects=_EFFECT),
    )(*given)
    held = res[2 * ng:2 * ng + n]
    states = [(names[g], groups[g][0], res[2 * g], res[2 * g + 1], held[first[g]:first[g] + len(names[g])])
              for g in range(ng)]
    return states, res[-1]


def _start_exchange(name, jobs, arrays):
    states, token = _start_exchanges(name, [(jobs, arrays)])
    return states[0], token


def _finish_exchange(name, state, after):
    names, jobs, send_sem, recv_sem, held = state
    n = len(names)

    def body(*refs):
        hbm = dict(zip(names, refs[:n]))
        send, recv = refs[n:n + 2]
        starts, waits = _plan_all(jobs, hbm, send, recv)
        for cp in waits:
            cp.wait_recv()
        for cp in starts:
            cp.wait_send()

    res = pl.pallas_call(
        body, name=name, out_shape=tuple(pltpu.HBM(a.shape, a.dtype) for a in held),
        in_specs=[_HBM] * n + [_SEM, _SEM, _ANY], out_specs=tuple([_HBM] * n),
        input_output_aliases={i: i for i in range(n)},
        compiler_params=pltpu.CompilerParams(has_side_effects=_EFFECT),
    )(*held, send_sem, recv_sem, after)
    return dict(zip(names, res))


def _row_tile(rows, bytes_per_row):
    best = 16
    for t in range(16, rows + 1, 16):
        if rows % t == 0 and t * bytes_per_row <= 9 * 1024 * 1024:
            best = t
    return best


def _rowwise(fn, ins, out_dtypes, name, after=None):
    rows, cols = ins[0].shape
    per_row = sum(cols * a.dtype.itemsize for a in ins) + sum(cols * jnp.dtype(d).itemsize for d in out_dtypes)
    tr = _row_tile(rows, per_row)
    n_in = len(ins)

    def body(*refs):
        outs = fn(*[r[...] for r in refs[:n_in]])
        for o_ref, o in zip(refs[-len(out_dtypes):], outs):
            o_ref[...] = o.astype(o_ref.dtype)

    tile = pl.BlockSpec((tr, cols), lambda i: (i, 0))
    behind = [] if after is None else [after]
    return pl.pallas_call(
        body, name=name, grid=(rows // tr,),
        in_specs=[tile] * n_in + [pl.BlockSpec((8, 128), lambda i: (0, 0))] * len(behind),
        out_specs=[tile] * len(out_dtypes),
        out_shape=[jax.ShapeDtypeStruct((rows, cols), d) for d in out_dtypes],
        compiler_params=_params("parallel"),
    )(*ins, *behind)


def _tiled(fn, name, grid, pos, ins, outs):
    n_in = len(ins)

    def body(pos_ref, *refs):
        res = fn(*[r[...] for r in refs[:n_in]])
        for o_ref, o in zip(refs[n_in:], res):
            o_ref[...] = o.astype(o_ref.dtype)

    return pl.pallas_call(
        body, name=name,
        grid_spec=pltpu.PrefetchScalarGridSpec(
            num_scalar_prefetch=1, grid=grid,
            in_specs=[pl.BlockSpec(bs, im) for _, bs, im in ins],
            out_specs=[pl.BlockSpec(bs, im) for _, _, bs, im in outs]),
        out_shape=[jax.ShapeDtypeStruct(s, d) for s, d, _, _ in outs],
        compiler_params=_params("parallel"),
    )(pos, *[a for a, _, _ in ins])


def _adamw(w, g, m, v):
    m = ADAM_B1 * m + (1.0 - ADAM_B1) * g
    v = ADAM_B2 * v + (1.0 - ADAM_B2) * (g * g)
    m_hat = m / (1.0 - ADAM_B1 ** ADAM_STEP)
    v_hat = v / (1.0 - ADAM_B2 ** ADAM_STEP)
    return -ADAM_LR * (m_hat / (jnp.sqrt(v_hat) + ADAM_EPS) + ADAM_WD * w), m, v


def _adamw_small(params):
    n = len(params)

    def body(*refs):
        for k in range(n):
            w, g, m, v = (r[...] for r in refs[4 * k:4 * k + 4])
            for o_ref, o in zip(refs[4 * n + 3 * k:4 * n + 3 * k + 3], _adamw(w, g, m, v)):
                o_ref[...] = o

    flat = [a for p in params for a in p]
    return pl.pallas_call(
        body, name="adamw_small",
        out_shape=[jax.ShapeDtypeStruct(p[0].shape, F32) for p in params for _ in range(3)],
    )(*flat)


class _Layout:
    def __init__(self, rows, cols, stacked):
        self.rows, self.cols, self.stacked = rows, cols, stacked

    def whole(self, rows=None):
        r = self.rows if rows is None else rows
        return (N_CHIPS, r, self.cols) if self.stacked else (r, N_CHIPS * self.cols)

    def part_rows(self, h, q=0, nq=1):
        n = self.rows // 2 // nq
        return pl.ds(pl.multiple_of(h * (self.rows // 2) + q * n, 16), n)

    def half_rows(self, h):
        return self.part_rows(h)

    def block(self, ref, p, rows=slice(None)):
        if self.stacked:
            return ref.at[p, rows, :]
        return ref.at[rows, pl.ds(pl.multiple_of(p * self.cols, 128), self.cols)]

    def all_chips(self, ref, rows):
        return ref.at[:, rows, :] if self.stacked else ref.at[rows, :]


BIG = (
    _Layout(IN_SHARD, D_MODEL, True),
    _Layout(ATTN_W + CONV_W, D_MODEL // N_CHIPS, False),
    _Layout(D_MODEL // N_CHIPS, D_MODEL, True),
    _Layout(D_MODEL, FF2 // N_CHIPS, False),
    _Layout(D_FF // N_CHIPS, D_MODEL, True),
)
N_BIG = len(BIG)
_ANY = pl.BlockSpec(memory_space=pl.ANY)


def _position():
    x, y, c = lax.axis_index("x"), lax.axis_index("y"), lax.axis_index("c")
    return x, y, c, 2 * x + y


def _core_of_chip(p, c):
    return (p >> 1, p & 1, c)


def _place_cast(shard, lay, pos, name, after=None):
    rows, cols = shard.shape
    tr = _row_tile(rows, cols * 6)
    if lay.stacked:
        out = (lay.whole(), BF16, (None, tr, cols), lambda i, pos: (pos[0], i, 0))
    else:
        out = (lay.whole(), BF16, (tr, cols), lambda i, pos: (i, pos[0]))
    ins = [(shard, (tr, cols), lambda i, pos: (i, 0))]
    if after is not None:
        ins.append((after, (8, 128), lambda i, pos: (0, 0)))
    return _tiled(lambda a, *_: (a,), name, (rows // tr,), pos, ins, [out])[0]


def _place_cast_pair(top, bottom, lay, pos, name, after=None):
    rows, cols = top.shape
    ins = [(top, (rows, cols), lambda i, pos: (0, 0)), (bottom, (rows, cols), lambda i, pos: (0, 0))]
    if after is not None:
        ins.append((after, (8, 128), lambda i, pos: (0, 0)))
    return _tiled(lambda a, b, *_: (jnp.concatenate([a, b], axis=0),), name, (1,), pos, ins,
                  [(lay.whole(), BF16, (2 * rows, cols), lambda i, pos: (0, pos[0]))])[0]


def _adamw_pair(top, bottom, g, after=None):
    rows = top[0].shape[0]

    def body(*refs):
        (wa, ma, va, wb, mb, vb, g_ref), outs = refs[:7], refs[-8:]
        for (w, m, v), gg, o in (((wa, ma, va), g_ref[:rows], outs[:4]), ((wb, mb, vb), g_ref[rows:], outs[4:])):
            for o_ref, val in zip(o, (gg, *_adamw(w[...], gg, m[...], v[...]))):
                o_ref[...] = val

    behind = [] if after is None else [after[0:8, 0:128]]
    res = pl.pallas_call(
        body, name="adamw_w_br", out_shape=[jax.ShapeDtypeStruct(top[0].shape, F32)] * 8,
    )(*top, *bottom, g, *behind)
    return res[:4], res[4:]


def _remote(src, dst, send, recv, k, device):
    return pltpu.make_async_remote_copy(src_ref=src, dst_ref=dst, send_sem=send.at[k], recv_sem=recv.at[k],
                                        device_id=device, device_id_type=MESH)


def _arrival(dst, send, recv, k, me):
    return _remote(dst, dst, send, recv, k, me)


def _gather_ici(lay, name, q=0, nq=1):
    def plan(hbm, pos, send, recv, base):
        x, y, c, me = pos
        rows = lay.part_rows(c, q, nq)
        mine = lay.block(hbm[name], me, rows)
        starts = [_remote(mine, mine, send, recv, base + d - 1, _core_of_chip(me ^ d, c)) for d in (1, 2, 3)]
        waits = [_arrival(lay.block(hbm[name], me ^ d, rows), send, recv, base + d - 1, (x, y, c)) for d in (1, 2, 3)]
        return starts, waits
    return _Job(3, plan)


def _gather_d2d(lay, name, q=0, nq=1):
    def plan(hbm, pos, send, recv, base):
        x, y, c, me = pos
        starts, waits = [], []
        for d in (1, 2, 3):
            got = lay.block(hbm[name], me ^ d, lay.part_rows(c, q, nq))
            starts.append(_remote(got, got, send, recv, base + d - 1, (x, y, 1 - c)))
            waits.append(_arrival(lay.block(hbm[name], me ^ d, lay.part_rows(1 - c, q, nq)), send, recv, base + d - 1,
                                  (x, y, c)))
        return starts, waits
    return _Job(3, plan)


def _rs_pair(lay, grad, theirs):
    def plan(hbm, pos, send, recv, base):
        x, y, c, _ = pos
        out = _remote(lay.all_chips(hbm[grad], lay.half_rows(1 - c)), hbm[theirs], send, recv, base, (x, y, 1 - c))
        return [out], [_arrival(hbm[theirs], send, recv, base, (x, y, c))]
    return _Job(1, plan)


def _rs_chips(lay, sums, slots):
    def plan(hbm, pos, send, recv, base):
        x, y, c, me = pos
        starts = [_remote(lay.block(hbm[sums], me ^ d), hbm[slots].at[me], send, recv, base + d - 1,
                          _core_of_chip(me ^ d, c)) for d in (1, 2, 3)]
        waits = [_arrival(hbm[slots].at[me ^ d], send, recv, base + d - 1, (x, y, c)) for d in (1, 2, 3)]
        return starts, waits
    return _Job(3, plan)


def _rs_share(lay, shard):
    def plan(hbm, pos, send, recv, base):
        x, y, c, _ = pos
        mine = hbm[shard].at[lay.half_rows(c), :]
        other = hbm[shard].at[lay.half_rows(1 - c), :]
        return [_remote(mine, mine, send, recv, base, (x, y, 1 - c))], [_arrival(other, send, recv, base, (x, y, c))]
    return _Job(1, plan)


def _slots_shape(lay):
    return jax.ShapeDtypeStruct((N_CHIPS, lay.rows // 2, lay.cols), BF16)


def _theirs_shape(lay, dtype=BF16):
    return jax.ShapeDtypeStruct(lay.whole(lay.rows // 2), dtype)


def _pair_sum(grad, theirs, lay, pos, name):
    half = lay.rows // 2
    add = lambda a, b: (a.astype(F32) + b.astype(F32),)
    if lay.stacked:
        tr = _row_tile(half, lay.cols * 6)
        nt = half // tr
        flat = lambda a: a.reshape(-1, lay.cols)
        mine = lambda t, pos: ((t // nt) * (2 * nt) + pos[1] * nt + t % nt, 0)
        grid, blk = (N_CHIPS * nt,), (tr, lay.cols)
        grad, theirs = flat(grad), flat(theirs)
    else:
        tr = _row_tile(half, N_CHIPS * lay.cols * 6)
        nt = half // tr
        mine = lambda t, pos: (pos[1] * nt + t, 0)
        grid, blk = (nt,), (tr, N_CHIPS * lay.cols)
    same = lambda t, pos: (t, 0)
    out = _tiled(add, name, grid, pos, [(grad, blk, mine), (theirs, blk, same)], [(theirs.shape, BF16, blk, same)])[0]
    return out.reshape(lay.whole(half))


def _chip_sum(sums, slots, lay, pos, name, after=None):
    half = lay.rows // 2
    tr = _row_tile(half, lay.cols * 12)
    nt = half // tr
    blk3 = (None, tr, lay.cols)
    if lay.stacked:
        own = (sums, blk3, lambda i, pos: (pos[0], i, 0))
    else:
        own = (sums, (tr, lay.cols), lambda i, pos: (i, pos[0]))
    others = [(slots, blk3, functools.partial(lambda d, i, pos: (pos[0] ^ d, i, 0), d)) for d in (1, 2, 3)]

    def add(a, b1, b2, b3, *_):
        return (((a.astype(F32) + b1.astype(F32)) + b2.astype(F32)) + b3.astype(F32),)

    if after is not None:
        others.append((after, (8, 128), lambda i, pos: (0, 0)))
    return _tiled(add, name, (nt,), pos, [own] + others,
                  [((lay.rows, lay.cols), F32, (tr, lay.cols), lambda i, pos: (pos[1] * nt + i, 0))])[0]


N_DEV = 8


def _to_all(src, slots):
    def plan(hbm, pos, send, recv, base):
        x, y, c, _ = pos
        idx = 4 * x + 2 * y + c
        starts = [_remote(hbm[src], hbm[slots].at[idx], send, recv, base + k - 1,
                          (x ^ (k >> 2), y ^ ((k >> 1) & 1), c ^ (k & 1))) for k in range(1, N_DEV)]
        waits = [_arrival(hbm[slots].at[idx ^ k], send, recv, base + k - 1, (x, y, c)) for k in range(1, N_DEV)]
        return starts, waits
    return _Job(N_DEV - 1, plan)


def _sum_slots(own, slots, pos):
    def body(pos_ref, own_ref, slots_ref, o_ref):
        idx = 2 * pos_ref[0] + pos_ref[1]
        term = lambda q: jnp.where(idx == q, own_ref[...], slots_ref[q])
        acc = term(0)
        for q in range(1, N_DEV):
            acc = acc + term(q)
        o_ref[...] = acc

    return pl.pallas_call(
        body, name="sum_small", out_shape=jax.ShapeDtypeStruct(own.shape, F32),
        in_specs=[pl.BlockSpec(memory_space=pltpu.SMEM), pl.BlockSpec(memory_space=pltpu.VMEM),
                  pl.BlockSpec(memory_space=pltpu.VMEM)],
    )(pos, own, slots)


def _pack_rows(parts):
    padded = [jnp.pad(a, ((0, -a.shape[0] % 8), (0, 0))) for a in parts]
    starts = [sum(p.shape[0] for p in padded[:k]) for k in range(len(padded))]
    return jnp.concatenate(padded, axis=0), starts


def kernel(x, mix_norm, w_in, b_in, sinks, conv_w, w_attn_branch, w_conv_branch, w_out, ffn_norm, w_up, ffn_conv_w, w_down, final_norm, loss_target, m_mix_norm, m_w_in, m_b_in, m_sinks, m_conv_w, m_w_attn_branch, m_w_conv_branch, m_w_out, m_ffn_norm, m_w_up, m_ffn_conv_w, m_w_down, m_final_norm, v_mix_norm, v_w_in, v_b_in, v_sinks, v_conv_w, v_w_attn_branch, v_w_conv_branch, v_w_out, v_ffn_norm, v_w_up, v_ffn_conv_w, v_w_down, v_final_norm):
    me = 2 * lax.axis_index("x") + lax.axis_index("y")
    names = ("w_in", "w_br", "w_out", "w_up", "w_down")
    w_of = dict(w_in=w_in[0].T, w_out=w_out[0], w_up=w_up[0], w_down=w_down[0])
    m_of = dict(w_in=m_w_in[0].T, w_out=m_w_out[0], w_up=m_w_up[0], w_down=m_w_down[0])
    v_of = dict(w_in=v_w_in[0].T, w_out=v_w_out[0], w_up=v_w_up[0], w_down=v_w_down[0])
    ab = (w_attn_branch[0], m_w_attn_branch[0], v_w_attn_branch[0])
    cb = (w_conv_branch[0], m_w_conv_branch[0], v_w_conv_branch[0])

    pos = jnp.stack([me, lax.axis_index("c")]).astype(jnp.int32)

    lay = dict(zip(names, BIG))
    xs, target, sk = x[0], loss_target[0], sinks[0]
    s = xs.shape[0]
    tm, tm2, bk, bk2 = min(256, s), min(512, s), min(1024, s), min(2048, s)

    taps, (_, t0) = _pack_rows([conv_w[0], ffn_conv_w[0].reshape(3 * (FF2 // N_CHIPS // 128), 128)])
    placed = {"w_in": _place_cast(w_of["w_in"], lay["w_in"], pos, "cast_w_in")}
    fly_in, started = _start_exchange("gather_in_start", [_gather_ici(lay["w_in"], "w_in")], {"w_in": placed["w_in"]})
    taps_flight, started = _start_exchange("taps_start", [_to_all("v", "slots")],
                                           {"v": taps + started[0:1], "slots": jnp.zeros((N_DEV, *taps.shape), F32)})
    placed["w_br"] = _place_cast_pair(ab[0], cb[0], lay["w_br"], pos, "cast_w_br", after=started)
    for n in names[2:]:
        placed[n] = _place_cast(w_of[n], lay[n], pos, "cast_" + n, after=started)
    trio = ("w_br", "w_out")
    (fly_trio, fly_up, fly_down), started = _start_exchanges("gather_rest_start", [
        ([_gather_ici(lay[n], n) for n in ws], {n: placed[n] for n in ws}) for ws in (trio, ("w_up",), ("w_down",))])

    got = _finish_exchange("gather_in_wait", fly_in, after=started)
    w_in_full = _exchange("gather_in_d2d", [[_gather_d2d(lay["w_in"], "w_in")]], bufs=got)["w_in"].reshape(IN_W, D_MODEL)
    xn, qkv, c3, gates = _inproj_fwd(xs, mix_norm, w_in_full, b_in, tm2)
    k2 = _Carry([_gather_d2d(lay[n], n) for n in trio], bufs=_finish_exchange("gather_trio_wait", fly_trio, after=qkv))
    attn = _attn_fwd(qkv, sk, comm=k2)
    w_br = k2.out["w_br"]
    w_out_full = k2.out["w_out"].reshape(D_MODEL, D_MODEL)
    k3 = _Carry([_gather_d2d(lay["w_up"], "w_up")], bufs=_finish_exchange("gather_up_wait", fly_up, after=attn))
    taps = _finish_exchange("taps_wait", taps_flight, after=attn)
    taps = lax.dynamic_update_slice(taps["slots"], taps["v"][None], (2 * me + lax.axis_index("c"), 0, 0))
    conv_full = taps[0::2, 0:3].transpose(1, 0, 2).reshape(3, CONV_W)
    ffn_cw_full = taps[0::2, t0:t0 + 33].reshape(N_CHIPS, 3, FF2 // N_CHIPS).transpose(1, 0, 2).reshape(3, FF2)
    conv, a, cv, merged, h1, hn = _mix_fwd(xs, attn, c3, gates, conv_full, w_br, w_out_full, ffn_norm, tm2, comm=k3)
    w_up_full = k3.out["w_up"]
    w_down_full = _exchange("gather_down_d2d", [[_gather_d2d(lay["w_down"], "w_down")]],
                            bufs=_finish_exchange("gather_down_wait", fly_down, after=hn))["w_down"].reshape(D_FF, D_MODEL)
    u, up, act, dh2, loss_part, g_fn = _ffn_fwd_loss(hn, h1, w_up_full, ffn_cw_full, w_down_full,
                                                     final_norm[None, :], target, tm)

    grads, sums, slots = {}, {}, {}

    def pair(*ws):
        return _Carry([_rs_pair(lay[n], "g_" + n, "t_" + n) for n in ws], reads={"g_" + n: grads[n] for n in ws},
                      fresh={"t_" + n: _theirs_shape(lay[n], grads[n].dtype) for n in ws})

    def chips(*ws, also=None):
        k = _Carry([_rs_chips(lay[n], "s_" + n, "r_" + n) for n in ws], reads={"s_" + n: sums[n] for n in ws},
                   fresh={"r_" + n: _slots_shape(lay[n]) for n in ws})
        if also is not None:
            k = _Carry(k.jobs + also.jobs, {**k.reads, **also.reads}, None, {**k.fresh, **also.fresh})
        return k

    def pair_sums(k, *ws):
        for n in ws:
            sums[n] = _pair_sum(grads[n], k.out["t_" + n], lay[n], pos, "pair_sum_" + n)

    def take_slots(k, *ws):
        for n in ws:
            slots[n] = k.out["r_" + n]

    du, dh1, g_fcw, g_g2 = _ffn_bwd(dh2, u, up, h1, w_up_full, ffn_cw_full, w_down_full, ffn_norm, tm)
    grads["w_down"] = _wgrad(act, dh2, D_FF // 2, D_MODEL, bk2, "wgrad_down").reshape(lay["w_down"].whole())
    k4 = pair("w_down")
    grads["w_up"] = _wgrad(hn, du, D_MODEL, FF2 // 4, bk2, "wgrad_up", comm=k4)
    pair_sums(k4, "w_down")
    k5 = chips("w_down", also=pair("w_up"))
    dattn, dc3, dgt, g_cw, grads["w_br"], gw_out = _mix_bwd(
        dh1, gates, a, cv, c3, attn, conv, merged, conv_full, w_br, w_out_full, tm2, comm=k5)
    grads["w_out"] = gw_out.reshape(lay["w_out"].whole())
    take_slots(k5, "w_down")
    pair_sums(k5, "w_up")
    k6 = chips("w_up", also=pair(*trio))
    dq, dk_even, dk_odd, dv_even, dv_odd, g_sk = _attn_bwd(qkv, sk, attn, dattn, comm=k6)
    take_slots(k6, "w_up")
    pair_sums(k6, *trio)
    trio_flight, started = _start_exchange(
        "rs_chips_trio_start", [_rs_chips(lay[n], "s_" + n, "r_" + n) for n in trio],
        {**{"s_" + n: sums[n] for n in trio}, **{"r_" + n: _slots_shape(lay[n]) for n in trio}})
    behind = mix_norm + jnp.tile(started[0:1], (1, D_MODEL // 128))
    grad_x, gw_in, g_b, g_g1 = _inproj_bwd(dq, (dk_even, dk_odd), (dv_even, dv_odd), dc3, dgt, w_in_full, xs, xn,
                                           dh1, behind)
    grads["w_in"] = gw_in.reshape(lay["w_in"].whole())

    parts = [loss_part, g_g1, g_b, jnp.pad(g_sk[:, 0], (0, 120))[None, :], g_cw, g_g2, g_fcw, g_fn]
    packed, at = _pack_rows([p.reshape(-1, 128) for p in parts])
    small_flight, started = _start_exchange("small_start", [_to_all("v", "slots")],
                                            {"v": packed, "slots": jnp.zeros((N_DEV, *packed.shape), F32)})
    others = names[1:]
    in_flight, started = _start_exchange("rs_pair_in_start", [_rs_pair(lay["w_in"], "g", "t")],
                                         {"g": grads["w_in"], "t": _theirs_shape(lay["w_in"]), "behind": started})
    halves = {n: _chip_sum(sums[n], slots[n], lay[n], pos, "chip_sum_" + n, after=started) for n in ("w_up", "w_down")}
    landed = _finish_exchange("rs_pair_in_wait", in_flight, after=halves["w_down"])
    sums["w_in"] = _pair_sum(landed["g"], landed["t"], lay["w_in"], pos, "pair_sum_w_in")
    in_flight, started = _start_exchange("rs_chips_in_start", [_rs_chips(lay["w_in"], "s", "r")],
                                         {"s": sums["w_in"], "r": _slots_shape(lay["w_in"])})
    landed = _finish_exchange("rs_chips_trio_wait", trio_flight, after=started)
    for n in trio:
        halves[n] = _chip_sum(landed["s_" + n], landed["r_" + n], lay[n], pos, "chip_sum_" + n)
    shared = _exchange("share_halves", [[_rs_share(lay[n], n) for n in others]], bufs=halves)

    def adam(n, g, after=None):
        return _rowwise(lambda w, g, m, v: (g, *_adamw(w, g, m, v)), [w_of[n], g, m_of[n], v_of[n]], [F32] * 4,
                        "adamw_" + n, after=after)

    new_of, last = {}, None
    for n in ("w_up", "w_down", "w_out"):
        new_of[n] = adam(n, shared[n], last)
        last = new_of[n][1]
    new_of["w_ab"], new_of["w_cb"] = _adamw_pair(ab, cb, shared["w_br"], after=last)
    last = new_of["w_cb"][1]

    arrived = _finish_exchange("small_wait", small_flight, after=last)
    total = _sum_slots(arrived["v"], arrived["slots"], pos)
    part = lambda k: total[at[k]:at[k] + parts[k].size // 128].reshape(parts[k].shape)
    loss = total[0, 0]
    g_mix, g_b, g_g2, g_fn = part(1), part(2), part(5), part(7)
    g_sk = part(3)[:, 0:N_HEADS]
    g_cw = lax.dynamic_slice(part(4), (0, me * 128), (3, 128))
    g_fcw = lax.dynamic_slice(part(6), (0, me * (FF2 // N_CHIPS)), (3, FF2 // N_CHIPS))
    small_p = [
        (mix_norm, g_mix, m_mix_norm, v_mix_norm), (b_in, g_b, m_b_in, v_b_in), (sinks, g_sk, m_sinks, v_sinks),
        (conv_w[0], g_cw, m_conv_w[0], v_conv_w[0]), (ffn_norm, g_g2, m_ffn_norm, v_ffn_norm),
        (ffn_conv_w[0], g_fcw, m_ffn_conv_w[0], v_ffn_conv_w[0]),
        (final_norm[None, :], g_fn, m_final_norm[None, :], v_final_norm[None, :])]
    small_new = _adamw_small(small_p)
    small_new = [small_new[3 * k:3 * k + 3] for k in range(len(small_p))]

    landed = _finish_exchange("rs_chips_in_wait", in_flight, after=small_new[0][0])
    half_in = _chip_sum(landed["s"], landed["r"], lay["w_in"], pos, "chip_sum_w_in")
    shared["w_in"] = _exchange("share_in", [[_rs_share(lay["w_in"], "w_in")]], bufs={"w_in": half_in})["w_in"]
    new_of["w_in"] = [a.T for a in adam("w_in", shared["w_in"])]
    big = ("w_in", "w_ab", "w_cb", "w_out", "w_up", "w_down")
    big_g = [new_of[n][0] for n in big]
    big_new = [new_of[n][1:] for n in big]

    order = [("s", 0), ("b", 0), ("s", 1), ("s", 2), ("s", 3), ("b", 1), ("b", 2), ("b", 3), ("s", 4), ("b", 4),
             ("s", 5), ("b", 5), ("s", 6)]
    shapes = [mix_norm.shape, w_in.shape, b_in.shape, sinks.shape, conv_w.shape, w_attn_branch.shape,
              w_conv_branch.shape, w_out.shape, ffn_norm.shape, w_up.shape, ffn_conv_w.shape, w_down.shape,
              final_norm.shape]
    small_g = [p[1] for p in small_p]
    out_g = [(small_g[k] if kind == "s" else big_g[k]).reshape(shp) for (kind, k), shp in zip(order, shapes)]
    news = [[(small_new[k][j] if kind == "s" else big_new[k][j]).reshape(shp) for (kind, k), shp in zip(order, shapes)]
            for j in range(3)]
    return (loss, grad_x[None], *out_g, *news[0], *news[1], *news[2])
```

```python
import functools

import jax
import jax.numpy as jnp
from jax import lax
from jax.experimental import pallas as pl
from jax.experimental.pallas import tpu as pltpu

F32 = jnp.float32
BF16 = jnp.bfloat16

D_MODEL = 1024
HEAD_DIM = 64
N_HEADS = 8
N_KV_HEADS = 2
GROUP = N_HEADS // N_KV_HEADS
BLOCK = 128
ATTN_SCALE = HEAD_DIM ** -0.5
ATTN_W = N_HEADS * HEAD_DIM
KV_W = N_KV_HEADS * HEAD_DIM
CONV_W = 512
QKV_W = ATTN_W + 2 * KV_W
C3_W = 3 * CONV_W
GATES_W = 2 * D_MODEL
IN_W = QKV_W + C3_W + GATES_W
D_FF = 2816
FF2 = 2 * D_FF
NORM_EPS = 1e-5
N_CHIPS = 4
IN_SHARD = IN_W // N_CHIPS
NEG = -1e30

ADAM_LR = 0.001
ADAM_B1 = 0.9
ADAM_B2 = 0.999
ADAM_EPS = 1e-08
ADAM_WD = 0.01
ADAM_STEP = 10

VMEM_LIMIT = 56 * 1024 * 1024
MESH = pl.DeviceIdType.MESH

NT = (((1,), (1,)), ((), ()))
TN = (((0,), (0,)), ((), ()))


def _params(*sem):
    return pltpu.CompilerParams(dimension_semantics=sem, vmem_limit_bytes=VMEM_LIMIT)


def _resident(shape):
    return pl.BlockSpec(shape, lambda *_: (0,) * len(shape), pipeline_mode=pl.Buffered(1))


def _sigmoid(v):
    return 0.5 * jnp.tanh(0.5 * v) + 0.5


def _rstd(v):
    return lax.rsqrt(jnp.mean(v * v, axis=-1, keepdims=True) + NORM_EPS)


def _rms_bwd(dy, v, rstd, g):
    vhat = v * rstd
    t = dy * g
    return rstd * (t - vhat * jnp.mean(t * vhat, axis=-1, keepdims=True)), dy * vhat


def _taps(z, cw):
    return cw[2:3] * z + cw[1:2] * pltpu.roll(z, 1, 0) + cw[0:1] * pltpu.roll(z, 2, 0)


def _causal_conv(z, prev, cw):
    edge = _taps(jnp.concatenate([prev, z[0:8]], axis=0), cw)
    return jnp.concatenate([edge[8:16], _taps(z, cw)[8:]], axis=0)


def _rows_after(z, nxt):
    n = z.shape[0]
    edge = jnp.concatenate([z[n - 8:n], nxt], axis=0)
    return tuple(jnp.concatenate([pltpu.roll(z, n - k, 0)[:n - 8], pltpu.roll(edge, 16 - k, 0)[0:8]], axis=0)
                 for k in (1, 2))


def _inproj_fwd(x, g1, w_in, b_in, tm, comm=None):
    s = x.shape[0]

    def body(x_ref, g_ref, w_ref, b_ref, xn_ref, qkv_ref, c3_ref, gt_ref):
        xf = x_ref[...]
        xn = (xf * _rstd(xf) * g_ref[...]).astype(BF16)
        xn_ref[...] = xn

        proj = (lax.dot_general(xn, w_ref[...], NT, preferred_element_type=F32) + b_ref[...]).astype(BF16)
        qkv_ref[...] = proj[:, :QKV_W]
        c3_ref[...] = proj[:, QKV_W:QKV_W + C3_W]
        gt_ref[...] = proj[:, QKV_W + C3_W:]

    row = lambda w: pl.BlockSpec((tm, w), lambda i: (i, 0))
    return _call(
        comm, body, name="inproj_fwd", grid=(s // tm,),
        in_specs=[row(D_MODEL), _resident((1, D_MODEL)), _resident((IN_W, D_MODEL)), _resident((1, IN_W))],
        out_specs=[row(D_MODEL), row(QKV_W), row(C3_W), row(GATES_W)],
        out_shape=[jax.ShapeDtypeStruct((s, D_MODEL), BF16), jax.ShapeDtypeStruct((s, QKV_W), BF16),
                   jax.ShapeDtypeStruct((s, C3_W), BF16), jax.ShapeDtypeStruct((s, GATES_W), BF16)],
        compiler_params=_params("parallel"),
    )(x, g1, w_in, b_in)


def _attn_bias():
    qi = (jnp.arange(GROUP * BLOCK) % BLOCK)[:, None]
    kj = jnp.arange(2 * BLOCK)[None, :]
    band = (kj > qi) & (kj <= qi + BLOCK)
    return jnp.stack([jnp.where(band & (kj >= BLOCK), 0.0, NEG), jnp.where(band, 0.0, NEG)]).astype(F32)


def _attn_bias_spec():
    return pl.BlockSpec((None, GROUP * BLOCK, 2 * BLOCK), lambda i: (jnp.minimum(i, 1), 0, 0))


def _sink_column(sk_ref, h):
    rows = lax.broadcasted_iota(jnp.int32, (GROUP * BLOCK, 1), 0)
    col = jnp.full((GROUP * BLOCK, 1), sk_ref[h * GROUP], F32)
    for g in range(1, GROUP):
        col = jnp.where(rows >= g * BLOCK, sk_ref[h * GROUP + g], col)
    return col


def _stack_heads(t, h):
    return jnp.concatenate(
        [t[:, (h * GROUP + g) * HEAD_DIM:(h * GROUP + g + 1) * HEAD_DIM] for g in range(GROUP)], axis=0)


def _unstack_heads(per_kv):
    return jnp.concatenate(
        [t[g * BLOCK:(g + 1) * BLOCK] for t in per_kv for g in range(GROUP)], axis=1)


def _pair_specs(npair):
    cur = lambda i: jnp.minimum(i, npair - 1)
    prev = lambda i: jnp.maximum(2 * jnp.minimum(i, npair - 1) - 1, 0)
    kv = ATTN_W // KV_W
    return (pl.BlockSpec((2 * BLOCK, ATTN_W), lambda i: (cur(i), 0)),
            pl.BlockSpec((BLOCK, KV_W), lambda i: (prev(i), kv)), pl.BlockSpec((2 * BLOCK, KV_W), lambda i: (cur(i), kv)),
            pl.BlockSpec((BLOCK, KV_W), lambda i: (prev(i), kv + 1)),
            pl.BlockSpec((2 * BLOCK, KV_W), lambda i: (cur(i), kv + 1)))


def _attn_fwd(qkv, sinks, comm=None):
    s = qkv.shape[0]
    npair = s // (2 * BLOCK)

    def body(sk_ref, bias0_ref, bias1_ref, q_ref, kp_ref, kc_ref, vp_ref, vc_ref, o_ref):
        kc, vc = kc_ref[...], vc_ref[...]
        for b, (bias_ref, kp, vp) in enumerate(((bias0_ref, kp_ref[...], vp_ref[...]),
                                                (bias1_ref, kc[:BLOCK], vc[:BLOCK]))):
            rows = slice(b * BLOCK, (b + 1) * BLOCK)
            q, bias = q_ref[rows, :], bias_ref[...]
            outs = []
            for h in range(N_KV_HEADS):
                hs = slice(h * HEAD_DIM, (h + 1) * HEAD_DIM)
                k2 = jnp.concatenate([kp[:, hs], kc[rows, hs]], axis=0)
                v2 = jnp.concatenate([vp[:, hs], vc[rows, hs]], axis=0)
                sc = lax.dot_general(_stack_heads(q, h), k2, NT, preferred_element_type=F32) * ATTN_SCALE + bias
                sink = _sink_column(sk_ref, h)
                m = jnp.maximum(jnp.max(sc, axis=1, keepdims=True), sink)
                p = jnp.exp(sc - m)
                den = jnp.sum(p, axis=1, keepdims=True) + jnp.exp(sink - m)
                outs.append(jnp.dot(p.astype(BF16), v2, preferred_element_type=F32) / den)
            o_ref[rows, :] = _unstack_heads(outs).astype(BF16)

    general = pl.BlockSpec((None, GROUP * BLOCK, 2 * BLOCK), lambda i: (1, 0, 0))
    return _call(
        comm, body, name="attn_fwd", grid=(npair,),
        in_specs=[pl.BlockSpec(memory_space=pltpu.SMEM), _attn_bias_spec(), general, *_pair_specs(npair)],
        out_specs=pl.BlockSpec((2 * BLOCK, ATTN_W), lambda i: (i, 0)),
        out_shape=jax.ShapeDtypeStruct((s, ATTN_W), BF16),
        compiler_params=_params("parallel"),
    )(sinks, _attn_bias(), _attn_bias(), qkv, qkv, qkv, qkv, qkv)


def _mix_fwd(x, attn, c3, gates, conv_w, w_br, w_out, g2, tm, comm=None):
    s = x.shape[0]

    def body(x_ref, at_ref, c3_ref, gt_ref, cw_ref, wbr_ref, wo_ref, g_ref,
             conv_ref, a_ref, cv_ref, mg_ref, h1_ref, hn_ref, carry_ref):
        @pl.when(pl.program_id(0) == 0)
        def _():
            carry_ref[...] = jnp.zeros_like(carry_ref)

        c3v = c3_ref[...].astype(F32)
        cb, cc, cx = c3v[:, :CONV_W], c3v[:, CONV_W:2 * CONV_W], c3v[:, 2 * CONV_W:]
        z = cc * cx
        cz = _causal_conv(z, carry_ref[...], cw_ref[...])
        carry_ref[...] = z[tm - 8:tm]
        conv = (cb * cz).astype(BF16)
        conv_ref[...] = conv
        a = jnp.dot(at_ref[...], wbr_ref[:ATTN_W, :], preferred_element_type=F32)
        cv = jnp.dot(conv, wbr_ref[ATTN_W:, :], preferred_element_type=F32)
        a_ref[...] = a.astype(BF16)
        cv_ref[...] = cv.astype(BF16)
        gt = gt_ref[...].astype(F32)
        merged = (_sigmoid(gt[:, :D_MODEL]) * a + _sigmoid(gt[:, D_MODEL:]) * cv).astype(BF16)
        mg_ref[...] = merged
        h1 = x_ref[...] + jnp.dot(merged, wo_ref[...], preferred_element_type=F32)
        h1_ref[...] = h1
        hn_ref[...] = (h1 * _rstd(h1) * g_ref[...]).astype(BF16)

    row = lambda w: pl.BlockSpec((tm, w), lambda i: (i, 0))
    return _call(
        comm, body, name="mix_fwd", grid=(s // tm,),
        in_specs=[row(D_MODEL), row(ATTN_W), row(C3_W), row(GATES_W), _resident((3, CONV_W)),
                  _resident((ATTN_W + CONV_W, D_MODEL)), _resident((D_MODEL, D_MODEL)), _resident((1, D_MODEL))],
        out_specs=[row(CONV_W), row(D_MODEL), row(D_MODEL), row(D_MODEL), row(D_MODEL), row(D_MODEL)],
        out_shape=[jax.ShapeDtypeStruct((s, CONV_W), BF16), jax.ShapeDtypeStruct((s, D_MODEL), BF16),
                   jax.ShapeDtypeStruct((s, D_MODEL), BF16), jax.ShapeDtypeStruct((s, D_MODEL), BF16),
                   jax.ShapeDtypeStruct((s, D_MODEL), F32), jax.ShapeDtypeStruct((s, D_MODEL), BF16)],
        scratch_shapes=[pltpu.VMEM((8, CONV_W), F32)],
        compiler_params=_params("arbitrary"),
    )(x, attn, c3, gates, conv_w, w_br, w_out, g2)


def _ffn_fwd_loss(hn, h1, w_up, ffn_cw, w_down, g3, target, tm):
    s = hn.shape[0]

    def body(hn_ref, h1_ref, wu_ref, cw_ref, wd_ref, g_ref, t_ref,
             u_ref, up_ref, act_ref, dh2_ref, loss_ref, gfn_ref, carry_ref):
        @pl.when(pl.program_id(0) == 0)
        def _():
            carry_ref[...] = jnp.zeros_like(carry_ref)
            loss_ref[...] = jnp.zeros_like(loss_ref)
            gfn_ref[...] = jnp.zeros_like(gfn_ref)

        u = jnp.dot(hn_ref[...], wu_ref[...], preferred_element_type=F32)
        u_ref[...] = u.astype(BF16)
        up = _causal_conv(u, carry_ref[...], cw_ref[...])
        up_ref[...] = up
        carry_ref[...] = u[tm - 8:tm]
        gate, val = up[:, :D_FF], up[:, D_FF:]
        act = (gate * _sigmoid(gate) * val).astype(BF16)
        act_ref[...] = act
        h2 = h1_ref[...] + jnp.dot(act, wd_ref[...], preferred_element_type=F32)
        rstd = _rstd(h2)
        g = g_ref[...]
        err = h2 * rstd * g - t_ref[...]
        loss_ref[...] += jnp.sum(err * err) * (0.5 / D_MODEL)
        dh2, dg = _rms_bwd(err * (1.0 / D_MODEL), h2, rstd, g)
        dh2_ref[...] = dh2
        gfn_ref[...] += jnp.sum(dg, axis=0, keepdims=True)

    row = lambda w: pl.BlockSpec((tm, w), lambda i: (i, 0))
    acc = lambda w: pl.BlockSpec((1, w), lambda i: (0, 0))
    return pl.pallas_call(
        body, name="ffn_fwd_loss", grid=(s // tm,),
        in_specs=[row(D_MODEL), row(D_MODEL), _resident((D_MODEL, FF2)), _resident((3, FF2)),
                  _resident((D_FF, D_MODEL)), _resident((1, D_MODEL)), row(D_MODEL)],
        out_specs=[row(FF2), row(FF2), row(D_FF), row(D_MODEL), acc(128), acc(D_MODEL)],
        out_shape=[jax.ShapeDtypeStruct((s, FF2), BF16), jax.ShapeDtypeStruct((s, FF2), F32),
                   jax.ShapeDtypeStruct((s, D_FF), BF16),
                   jax.ShapeDtypeStruct((s, D_MODEL), F32), jax.ShapeDtypeStruct((1, 128), F32),
                   jax.ShapeDtypeStruct((1, D_MODEL), F32)],
        scratch_shapes=[pltpu.VMEM((8, FF2), F32)],
        compiler_params=_params("arbitrary"),
    )(hn, h1, w_up, ffn_cw, w_down, g3, target)


def _ffn_bwd(dh2, u, up, h1, w_up, ffn_cw, w_down, g2, tm):
    s = dh2.shape[0]
    nt = s // tm

    def body(dh2_ref, u_ref, up_ref, h1_ref, wu_ref, cw_ref, wd_ref, g_ref,
             du_ref, dh1_ref, gcw_ref, gg_ref, carry_ref):
        @pl.when(pl.program_id(0) == 0)
        def _():
            for ref in (carry_ref, gcw_ref, gg_ref):
                ref[...] = jnp.zeros_like(ref)

        dh2v = dh2_ref[...]
        dact = lax.dot_general(dh2v.astype(BF16), wd_ref[...], NT, preferred_element_type=F32)
        upv = up_ref[...]
        gate, val = upv[:, :D_FF], upv[:, D_FF:]
        sg = _sigmoid(gate)
        dval = dact * (gate * sg)
        dgate = dact * val * (sg * (1.0 + gate * (1.0 - sg)))
        dup = jnp.concatenate([dgate, dval], axis=1)
        dup1, dup2 = _rows_after(dup, carry_ref[...])
        carry_ref[...] = dup[0:8]
        u = u_ref[...].astype(F32)
        gcw_ref[2:3, :] += jnp.sum(dup * u, axis=0, keepdims=True)
        gcw_ref[1:2, :] += jnp.sum(dup1 * u, axis=0, keepdims=True)
        gcw_ref[0:1, :] += jnp.sum(dup2 * u, axis=0, keepdims=True)
        cw = cw_ref[...]
        du = (cw[2:3] * dup + cw[1:2] * dup1 + cw[0:1] * dup2).astype(BF16)
        du_ref[...] = du
        dhn = lax.dot_general(du, wu_ref[...], NT, preferred_element_type=F32)
        h1v = h1_ref[...]
        dh1, dg = _rms_bwd(dhn, h1v, _rstd(h1v), g_ref[...])
        dh1_ref[...] = dh2v + dh1
        gg_ref[...] += jnp.sum(dg, axis=0, keepdims=True)

    row = lambda w: pl.BlockSpec((tm, w), lambda i: (nt - 1 - i, 0))
    return pl.pallas_call(
        body, name="ffn_bwd", grid=(nt,),
        in_specs=[row(D_MODEL), row(FF2), row(FF2),
                  row(D_MODEL), _resident((D_MODEL, FF2)), _resident((3, FF2)), _resident((D_FF, D_MODEL)),
                  _resident((1, D_MODEL))],
        out_specs=[row(FF2), row(D_MODEL), pl.BlockSpec((3, FF2), lambda i: (0, 0)),
                   pl.BlockSpec((1, D_MODEL), lambda i: (0, 0))],
        out_shape=[jax.ShapeDtypeStruct((s, FF2), BF16), jax.ShapeDtypeStruct((s, D_MODEL), F32),
                   jax.ShapeDtypeStruct((3, FF2), F32), jax.ShapeDtypeStruct((1, D_MODEL), F32)],
        scratch_shapes=[pltpu.VMEM((8, FF2), F32)],
        compiler_params=_params("arbitrary"),
    )(dh2, u, up, h1, w_up, ffn_cw, w_down, g2)


def _mix_bwd(dh1, gates, a, cv, c3, attn, conv, merged, conv_w, w_br, w_out, tm, comm=None):
    s = dh1.shape[0]
    nt = s // tm
    halo = 16

    def body(dh1_ref, gt_ref, a_ref, cv_ref, c3_ref, ch_ref, at_ref, cn_ref, mg_ref, cw_ref, wbr_ref,
             wo_ref, dat_ref, dc3_ref, dgt_ref, gcw_ref, gbr_ref, gout_ref, carry_ref, br_acc, out_acc):
        i = pl.program_id(0)

        @pl.when(i == 0)
        def _():
            for ref in (carry_ref, gcw_ref, br_acc, out_acc):
                ref[...] = jnp.zeros_like(ref)

        dh1v = dh1_ref[...].astype(BF16)
        out_acc[...] += lax.dot_general(mg_ref[...], dh1v, TN, preferred_element_type=F32)
        dm = lax.dot_general(dh1v, wo_ref[...], NT, preferred_element_type=F32)
        gt = gt_ref[...].astype(F32)
        sa, sc = _sigmoid(gt[:, :D_MODEL]), _sigmoid(gt[:, D_MODEL:])
        da = (dm * sa).astype(BF16)
        dcv = (dm * sc).astype(BF16)
        br_acc[:ATTN_W, :] += lax.dot_general(at_ref[...], da, TN, preferred_element_type=F32)
        br_acc[ATTN_W:, :] += lax.dot_general(cn_ref[...], dcv, TN, preferred_element_type=F32)
        dgt_ref[...] = jnp.concatenate(
            [dm * a_ref[...].astype(F32) * (sa * (1.0 - sa)), dm * cv_ref[...].astype(F32) * (sc * (1.0 - sc))],
            axis=1).astype(BF16)
        dat_ref[...] = lax.dot_general(da, wbr_ref[:ATTN_W, :], NT, preferred_element_type=F32).astype(BF16)
        dconv = lax.dot_general(dcv, wbr_ref[ATTN_W:, :], NT, preferred_element_type=F32)
        c3v = c3_ref[...].astype(F32)
        cb, cc, cx = c3v[:, :CONV_W], c3v[:, CONV_W:2 * CONV_W], c3v[:, 2 * CONV_W:]
        z = cc * cx
        chv = ch_ref[...].astype(F32)[halo - 8:halo] * (i < nt - 1).astype(F32)
        zh = chv[:, CONV_W:2 * CONV_W] * chv[:, 2 * CONV_W:]
        cw = cw_ref[...]
        cz = _causal_conv(z, zh, cw)
        dcz = dconv * cb
        dcz1, dcz2 = _rows_after(dcz, carry_ref[...])
        carry_ref[...] = dcz[0:8]
        gcw_ref[2:3, :] += jnp.sum(dcz * z, axis=0, keepdims=True)
        gcw_ref[1:2, :] += jnp.sum(dcz1 * z, axis=0, keepdims=True)
        gcw_ref[0:1, :] += jnp.sum(dcz2 * z, axis=0, keepdims=True)
        dz = cw[2:3] * dcz + cw[1:2] * dcz1 + cw[0:1] * dcz2
        dc3_ref[...] = jnp.concatenate([dconv * cz, dz * cx, dz * cc], axis=1).astype(BF16)

        @pl.when(i == nt - 1)
        def _():
            gbr_ref[...] = br_acc[...].astype(BF16)
            gout_ref[...] = out_acc[...].astype(BF16)

    row = lambda w: pl.BlockSpec((tm, w), lambda i: (nt - 1 - i, 0))
    return _call(
        comm, body, name="mix_bwd", grid=(nt,),
        in_specs=[row(D_MODEL), row(GATES_W), row(D_MODEL), row(D_MODEL), row(C3_W),
                  pl.BlockSpec((halo, C3_W), lambda i: (jnp.maximum((nt - 1 - i) * (tm // halo) - 1, 0), 0)),
                  row(ATTN_W), row(CONV_W), row(D_MODEL), _resident((3, CONV_W)),
                  _resident((ATTN_W + CONV_W, D_MODEL)), _resident((D_MODEL, D_MODEL))],
        out_specs=[row(ATTN_W), row(C3_W), row(GATES_W), pl.BlockSpec((3, CONV_W), lambda i: (0, 0)),
                   _resident((ATTN_W + CONV_W, D_MODEL)), _resident((D_MODEL, D_MODEL))],
        out_shape=[jax.ShapeDtypeStruct((s, ATTN_W), BF16), jax.ShapeDtypeStruct((s, C3_W), BF16),
                   jax.ShapeDtypeStruct((s, GATES_W), BF16), jax.ShapeDtypeStruct((3, CONV_W), F32),
                   jax.ShapeDtypeStruct((ATTN_W + CONV_W, D_MODEL), BF16),
                   jax.ShapeDtypeStruct((D_MODEL, D_MODEL), BF16)],
        scratch_shapes=[pltpu.VMEM((8, CONV_W), F32), pltpu.VMEM((ATTN_W + CONV_W, D_MODEL), F32),
                        pltpu.VMEM((D_MODEL, D_MODEL), F32)],
        compiler_params=_params("arbitrary"),
    )(dh1, gates, a, cv, c3, c3, attn, conv, merged, conv_w, w_br, w_out)


def _attn_bwd(qkv, sinks, o, do, comm=None):
    s = qkv.shape[0]
    npair = s // (2 * BLOCK)

    def one_block(sk_ref, bias, q, kp, kc, vp, vc, ov, dov, dsk_ref):
        dqs, dks, dvs = [], [], []
        for h in range(N_KV_HEADS):
            hs = slice(h * HEAD_DIM, (h + 1) * HEAD_DIM)
            k2 = jnp.concatenate([kp[:, hs], kc[:, hs]], axis=0)
            v2 = jnp.concatenate([vp[:, hs], vc[:, hs]], axis=0)
            qg, og, dog = _stack_heads(q, h), _stack_heads(ov, h), _stack_heads(dov, h)
            sc = lax.dot_general(qg, k2, NT, preferred_element_type=F32) * ATTN_SCALE + bias
            sink = _sink_column(sk_ref, h)
            m = jnp.maximum(jnp.max(sc, axis=1, keepdims=True), sink)
            p = jnp.exp(sc - m)
            psink = jnp.exp(sink - m)
            inv = 1.0 / (jnp.sum(p, axis=1, keepdims=True) + psink)
            p = p * inv
            delta = jnp.sum(dog.astype(F32) * og.astype(F32), axis=1, keepdims=True)
            dp = lax.dot_general(dog, v2, NT, preferred_element_type=F32)
            ds = (p * (dp - delta)).astype(BF16)
            dqs.append(jnp.dot(ds, k2, preferred_element_type=F32) * ATTN_SCALE)
            dks.append(lax.dot_general(ds, qg, TN, preferred_element_type=F32) * ATTN_SCALE)
            dvs.append(lax.dot_general(p.astype(BF16), dog, TN, preferred_element_type=F32))
            dsink = -(psink * inv * delta)
            for g in range(GROUP):
                r = h * GROUP + g
                dsk_ref[r:r + 1, :] += jnp.sum(dsink[g * BLOCK:(g + 1) * BLOCK])
        return _unstack_heads(dqs), jnp.concatenate(dks, axis=1), jnp.concatenate(dvs, axis=1)

    def body(sk_ref, bias0_ref, bias1_ref, q_ref, kp_ref, kc_ref, vp_ref, vc_ref, o_ref, do_ref,
             dq_ref, dke_ref, dko_ref, dve_ref, dvo_ref, dsk_ref, ck_ref, cvv_ref):
        i = pl.program_id(0)

        @pl.when(i == 0)
        def _():
            for ref in (ck_ref, cvv_ref, dsk_ref):
                ref[...] = jnp.zeros_like(ref)

        @pl.when(i < npair)
        def _():
            kc, vc = kc_ref[...], vc_ref[...]
            first, second = slice(0, BLOCK), slice(BLOCK, 2 * BLOCK)
            dq0, dk0, dv0 = one_block(sk_ref, bias0_ref[...], q_ref[first, :], kp_ref[...], kc[first], vp_ref[...],
                                      vc[first], o_ref[first, :], do_ref[first, :], dsk_ref)
            dq1, dk1, dv1 = one_block(sk_ref, bias1_ref[...], q_ref[second, :], kc[first], kc[second], vc[first],
                                      vc[second], o_ref[second, :], do_ref[second, :], dsk_ref)
            dq_ref[first, :] = dq0.astype(BF16)
            dq_ref[second, :] = dq1.astype(BF16)
            dko_ref[...] = (ck_ref[...] + dk0[:BLOCK]).astype(BF16)
            dvo_ref[...] = (cvv_ref[...] + dv0[:BLOCK]).astype(BF16)
            dke_ref[...] = (dk0[BLOCK:] + dk1[:BLOCK]).astype(BF16)
            dve_ref[...] = (dv0[BLOCK:] + dv1[:BLOCK]).astype(BF16)
            ck_ref[...] = dk1[BLOCK:]
            cvv_ref[...] = dv1[BLOCK:]

        @pl.when(i == npair)
        def _():
            dko_ref[...] = ck_ref[...].astype(BF16)
            dvo_ref[...] = cvv_ref[...].astype(BF16)

    cur = lambda i: jnp.minimum(i, npair - 1)
    done = lambda i: jnp.maximum(i - 1, 0)
    rows = pl.BlockSpec((2 * BLOCK, ATTN_W), lambda i: (cur(i), 0))
    even = pl.BlockSpec((BLOCK, KV_W), lambda i: (cur(i), 0))
    odd = pl.BlockSpec((BLOCK, KV_W), lambda i: (done(i), 0))
    general = pl.BlockSpec((None, GROUP * BLOCK, 2 * BLOCK), lambda i: (1, 0, 0))
    half = jax.ShapeDtypeStruct((s // 2, KV_W), BF16)
    return _call(
        comm, body, name="attn_bwd", grid=(npair + 1,),
        in_specs=[pl.BlockSpec(memory_space=pltpu.SMEM), _attn_bias_spec(), general, *_pair_specs(npair), rows, rows],
        out_specs=[rows, even, odd, even, odd, pl.BlockSpec((N_HEADS, 128), lambda i: (0, 0))],
        out_shape=[jax.ShapeDtypeStruct((s, ATTN_W), BF16), half, half, half, half,
                   jax.ShapeDtypeStruct((N_HEADS, 128), F32)],
        scratch_shapes=[pltpu.VMEM((BLOCK, KV_W), F32), pltpu.VMEM((BLOCK, KV_W), F32)],
        compiler_params=_params("arbitrary"),
    )(sinks, _attn_bias(), _attn_bias(), qkv, qkv, qkv, qkv, qkv, o, do)


def _inproj_bwd(dq, dk, dv, dc3, dgt, w_in, x, xn, dh1, g1):
    s = x.shape[0]
    tm = min(2 * BLOCK, s)
    nt = s // tm

    def body(dq_ref, dke_ref, dko_ref, dve_ref, dvo_ref, dc3_ref, dgt_ref, w_ref, x_ref, xn_ref, dh1_ref, g_ref,
             dx_ref, gw_ref, gb_ref, gg_ref, acc_ref):
        i = pl.program_id(0)

        @pl.when(i == 0)
        def _():
            for ref in (gb_ref, gg_ref, acc_ref):
                ref[...] = jnp.zeros_like(ref)

        dk = jnp.concatenate([dke_ref[...], dko_ref[...]], axis=0)
        dv = jnp.concatenate([dve_ref[...], dvo_ref[...]], axis=0)
        dp = jnp.concatenate([dq_ref[...], dk, dv, dc3_ref[...], dgt_ref[...]], axis=1)
        acc_ref[...] += lax.dot_general(dp, xn_ref[...], TN, preferred_element_type=F32)
        gb_ref[...] += jnp.sum(dp.astype(F32), axis=0, keepdims=True)
        dxn = jnp.dot(dp, w_ref[...], preferred_element_type=F32)
        xf = x_ref[...]
        dx, dg = _rms_bwd(dxn, xf, _rstd(xf), g_ref[...])
        dx_ref[...] = dh1_ref[...] + dx
        gg_ref[...] += jnp.sum(dg, axis=0, keepdims=True)

        @pl.when(i == nt - 1)
        def _():
            gw_ref[...] = acc_ref[...].astype(BF16)

    row = lambda w: pl.BlockSpec((tm, w), lambda i: (i, 0))
    acc = lambda w: pl.BlockSpec((1, w), lambda i: (0, 0))
    block = pl.BlockSpec((tm // 2, KV_W), lambda i: (i, 0))
    return pl.pallas_call(
        body, name="inproj_bwd", grid=(nt,),
        in_specs=[row(ATTN_W), block, block, block, block, row(C3_W), row(GATES_W), _resident((IN_W, D_MODEL)),
                  row(D_MODEL), row(D_MODEL), row(D_MODEL), _resident((1, D_MODEL))],
        out_specs=[row(D_MODEL), _resident((IN_W, D_MODEL)), acc(IN_W), acc(D_MODEL)],
        out_shape=[jax.ShapeDtypeStruct((s, D_MODEL), F32), jax.ShapeDtypeStruct((IN_W, D_MODEL), BF16),
                   jax.ShapeDtypeStruct((1, IN_W), F32), jax.ShapeDtypeStruct((1, D_MODEL), F32)],
        scratch_shapes=[pltpu.VMEM((IN_W, D_MODEL), F32)],
        compiler_params=_params("arbitrary"),
    )(dq, *dk, *dv, dc3, dgt, w_in, x, xn, dh1, g1)


def _wgrad(a, b, bm, bn, bk, name, comm=None):
    s, m = a.shape
    n = b.shape[1]
    nk = s // bk

    def body(a_ref, b_ref, o_ref, acc_ref):
        k = pl.program_id(2)

        @pl.when(k == 0)
        def _():
            acc_ref[...] = jnp.zeros_like(acc_ref)

        acc_ref[...] += lax.dot_general(a_ref[...].astype(BF16), b_ref[...].astype(BF16), TN,
                                        preferred_element_type=F32)

        @pl.when(k == nk - 1)
        def _():
            o_ref[...] = acc_ref[...].astype(BF16)

    return _call(
        comm, body, name=name, grid=(m // bm, n // bn, nk),
        in_specs=[pl.BlockSpec((bk, bm), lambda i, j, k: (k, i)), pl.BlockSpec((bk, bn), lambda i, j, k: (k, j))],
        out_specs=pl.BlockSpec((bm, bn), lambda i, j, k: (i, j)),
        out_shape=jax.ShapeDtypeStruct((m, n), BF16),
        scratch_shapes=[pltpu.VMEM((bm, bn), F32)],
        compiler_params=_params("parallel", "parallel", "arbitrary"),
    )(a, b)


class _Carry:
    def __init__(self, jobs, reads=None, bufs=None, fresh=None):
        self.jobs, self.reads, self.bufs, self.fresh = jobs, reads or {}, bufs or {}, fresh or {}
        self.out = {}


class _Job:
    def __init__(self, n_sems, plan):
        self.n_sems, self.plan = n_sems, plan


def _plan_all(jobs, hbm, send, recv):
    pos = _position()
    starts, waits, base = [], [], 0
    for job in jobs:
        s, w = job.plan(hbm, pos, send, recv, base)
        starts, waits, base = starts + s, waits + w, base + job.n_sems
    return starts, waits


def _call(comm, body, **kw):
    if comm is None:
        return pl.pallas_call(body, **kw)
    grid = kw["grid"]
    single = not isinstance(kw["out_shape"], (list, tuple))
    out_shape = [kw["out_shape"]] if single else list(kw["out_shape"])
    out_specs = [kw["out_specs"]] if single else list(kw["out_specs"])
    in_specs = list(kw["in_specs"])
    scratch = list(kw.get("scratch_shapes", ()))
    r_names, b_names, f_names = list(comm.reads), list(comm.bufs), list(comm.fresh)
    n_args, n_out, n_scr = len(in_specs), len(out_shape), len(scratch)
    n_sems = sum(j.n_sems for j in comm.jobs)

    def wrapped(*refs):
        k = n_args
        hbm = dict(zip(r_names, refs[k:k + len(r_names)]))
        k += len(r_names) + len(b_names)
        outs = refs[k:k + n_out]
        k += n_out
        hbm.update(zip(b_names + f_names, refs[k:k + len(b_names) + len(f_names)]))
        k += len(b_names) + len(f_names)
        send, recv = refs[k + n_scr:]
        starts, waits = _plan_all(comm.jobs, hbm, send, recv)
        ids = [pl.program_id(a) for a in range(len(grid))]
        first = functools.reduce(jnp.logical_and, [i == 0 for i in ids])
        last = functools.reduce(jnp.logical_and, [i == g - 1 for i, g in zip(ids, grid)])

        @pl.when(first)
        def _():
            for cp in starts:
                cp.start()

        body(*refs[:n_args], *outs, *refs[k:k + n_scr])

        @pl.when(last)
        def _():
            for cp in waits:
                cp.wait_recv()
            for cp in starts:
                cp.wait_send()

    sems = pltpu.SemaphoreType.DMA((n_sems,))
    held = [jax.ShapeDtypeStruct(a.shape, a.dtype) for a in comm.bufs.values()] + list(comm.fresh.values())
    call = pl.pallas_call(
        wrapped, name=kw["name"], grid=grid,
        in_specs=in_specs + [_ANY] * (len(r_names) + len(b_names)),
        out_specs=out_specs + [_ANY] * len(held),
        out_shape=out_shape + held,
        input_output_aliases={n_args + len(r_names) + i: n_out + i for i in range(len(b_names))},
        scratch_shapes=scratch + [sems, sems],
        compiler_params=_params(*["arbitrary"] * len(grid)),
    )

    def run(*args):
        res = call(*args, *comm.reads.values(), *comm.bufs.values())
        comm.out = dict(zip(b_names + f_names, res[n_out:]))
        return res[0] if single else res[:n_out]

    return run


def _exchange(name, phases, reads=None, bufs=None, fresh=None):
    comm = _Carry([j for ph in phases for j in ph], reads, bufs, fresh)
    r_names, b_names, f_names = list(comm.reads), list(comm.bufs), list(comm.fresh)
    n_sems = sum(j.n_sems for j in comm.jobs)

    def body(*refs):
        hbm = dict(zip(r_names, refs[:len(r_names)]))
        k = len(r_names) + len(b_names)
        hbm.update(zip(b_names + f_names, refs[k:k + len(b_names) + len(f_names)]))
        send, recv = refs[-2:]
        pos = _position()
        started, base = [], 0
        for ph in phases:
            waits = []
            for job in ph:
                s, w = job.plan(hbm, pos, send, recv, base)
                base += job.n_sems
                for cp in s:
                    cp.start()
                started, waits = started + s, waits + w
            for cp in waits:
                cp.wait_recv()
        for cp in started:
            cp.wait_send()

    sems = pltpu.SemaphoreType.DMA((n_sems,))
    held = [jax.ShapeDtypeStruct(a.shape, a.dtype) for a in comm.bufs.values()] + list(comm.fresh.values())
    res = pl.pallas_call(
        body, name=name, in_specs=[_ANY] * (len(r_names) + len(b_names)), out_specs=[_ANY] * len(held),
        out_shape=held, input_output_aliases={len(r_names) + i: i for i in range(len(b_names))},
        scratch_shapes=[sems, sems],
    )(*comm.reads.values(), *comm.bufs.values())
    return dict(zip(b_names + f_names, res))


_HBM = pl.BlockSpec(memory_space=pltpu.HBM)
_SEM = pl.BlockSpec(memory_space=pltpu.SEMAPHORE)
_EFFECT = pltpu.SideEffectType.DATAFLOW_SIDE_EFFECTING


def _start_exchanges(name, groups):
    names = [list(arrays) for _, arrays in groups]
    first = [sum(len(ns) for ns in names[:g]) for g in range(len(groups))]
    n, ng = sum(len(ns) for ns in names), len(groups)

    def body(*refs):
        for g, (jobs, _) in enumerate(groups):
            hbm = dict(zip(names[g], refs[first[g]:first[g] + len(names[g])]))
            for cp in _plan_all(jobs, hbm, refs[n + 2 * g], refs[n + 2 * g + 1])[0]:
                cp.start()
        refs[-1][...] = jnp.zeros_like(refs[-1])

    given = [pltpu.with_memory_space_constraint(
        a if isinstance(a, jax.Array) else lax.empty(a.shape, a.dtype), pltpu.HBM)
        for _, arrays in groups for a in arrays.values()]
    sems = [pltpu.SemaphoreType.DMA((sum(j.n_sems for j in jobs),)) for jobs, _ in groups for _ in range(2)]
    res = pl.pallas_call(
        body, name=name,
        out_shape=(*sems, *[pltpu.HBM(a.shape, a.dtype) for a in given], jax.ShapeDtypeStruct((8, 128), F32)),
        in_specs=[_HBM] * n, out_specs=(*[_SEM] * (2 * ng), *[_HBM] * n, pl.BlockSpec(memory_space=pltpu.VMEM)),
        input_output_aliases={i: 2 * ng + i for i in range(n)},
        compiler_params=pltpu.CompilerParams(has_side_effects=_EFFECT),
    )(*given)
    held = res[2 * ng:2 * ng + n]
    states = [(names[g], groups[g][0], res[2 * g], res[2 * g + 1], held[first[g]:first[g] + len(names[g])])
              for g in range(ng)]
    return states, res[-1]


def _start_exchange(name, jobs, arrays):
    states, token = _start_exchanges(name, [(jobs, arrays)])
    return states[0], token


def _finish_exchange(name, state, after):
    names, jobs, send_sem, recv_sem, held = state
    n = len(names)

    def body(*refs):
        hbm = dict(zip(names, refs[:n]))
        send, recv = refs[n:n + 2]
        starts, waits = _plan_all(jobs, hbm, send, recv)
        for cp in waits:
            cp.wait_recv()
        for cp in starts:
            cp.wait_send()

    res = pl.pallas_call(
        body, name=name, out_shape=tuple(pltpu.HBM(a.shape, a.dtype) for a in held),
        in_specs=[_HBM] * n + [_SEM, _SEM, _ANY], out_specs=tuple([_HBM] * n),
        input_output_aliases={i: i for i in range(n)},
        compiler_params=pltpu.CompilerParams(has_side_effects=_EFFECT),
    )(*held, send_sem, recv_sem, after)
    return dict(zip(names, res))


def _row_tile(rows, bytes_per_row):
    best = 16
    for t in range(16, rows + 1, 16):
        if rows % t == 0 and t * bytes_per_row <= 9 * 1024 * 1024:
            best = t
    return best


def _rowwise(fn, ins, out_dtypes, name, after=None):
    rows, cols = ins[0].shape
    per_row = sum(cols * a.dtype.itemsize for a in ins) + sum(cols * jnp.dtype(d).itemsize for d in out_dtypes)
    tr = _row_tile(rows, per_row)
    n_in = len(ins)

    def body(*refs):
        outs = fn(*[r[...] for r in refs[:n_in]])
        for o_ref, o in zip(refs[-len(out_dtypes):], outs):
            o_ref[...] = o.astype(o_ref.dtype)

    tile = pl.BlockSpec((tr, cols), lambda i: (i, 0))
    behind = [] if after is None else [after]
    return pl.pallas_call(
        body, name=name, grid=(rows // tr,),
        in_specs=[tile] * n_in + [pl.BlockSpec((8, 128), lambda i: (0, 0))] * len(behind),
        out_specs=[tile] * len(out_dtypes),
        out_shape=[jax.ShapeDtypeStruct((rows, cols), d) for d in out_dtypes],
        compiler_params=_params("parallel"),
    )(*ins, *behind)


def _tiled(fn, name, grid, pos, ins, outs):
    n_in = len(ins)

    def body(pos_ref, *refs):
        res = fn(*[r[...] for r in refs[:n_in]])
        for o_ref, o in zip(refs[n_in:], res):
            o_ref[...] = o.astype(o_ref.dtype)

    return pl.pallas_call(
        body, name=name,
        grid_spec=pltpu.PrefetchScalarGridSpec(
            num_scalar_prefetch=1, grid=grid,
            in_specs=[pl.BlockSpec(bs, im) for _, bs, im in ins],
            out_specs=[pl.BlockSpec(bs, im) for _, _, bs, im in outs]),
        out_shape=[jax.ShapeDtypeStruct(s, d) for s, d, _, _ in outs],
        compiler_params=_params("parallel"),
    )(pos, *[a for a, _, _ in ins])


def _adamw(w, g, m, v):
    m = ADAM_B1 * m + (1.0 - ADAM_B1) * g
    v = ADAM_B2 * v + (1.0 - ADAM_B2) * (g * g)
    m_hat = m / (1.0 - ADAM_B1 ** ADAM_STEP)
    v_hat = v / (1.0 - ADAM_B2 ** ADAM_STEP)
    return -ADAM_LR * (m_hat / (jnp.sqrt(v_hat) + ADAM_EPS) + ADAM_WD * w), m, v


def _adamw_small(params):
    n = len(params)

    def body(*refs):
        for k in range(n):
            w, g, m, v = (r[...] for r in refs[4 * k:4 * k + 4])
            for o_ref, o in zip(refs[4 * n + 3 * k:4 * n + 3 * k + 3], _adamw(w, g, m, v)):
                o_ref[...] = o

    flat = [a for p in params for a in p]
    return pl.pallas_call(
        body, name="adamw_small",
        out_shape=[jax.ShapeDtypeStruct(p[0].shape, F32) for p in params for _ in range(3)],
    )(*flat)


class _Layout:
    def __init__(self, rows, cols, stacked):
        self.rows, self.cols, self.stacked = rows, cols, stacked

    def whole(self, rows=None):
        r = self.rows if rows is None else rows
        return (N_CHIPS, r, self.cols) if self.stacked else (r, N_CHIPS * self.cols)

    def part_rows(self, h, q=0, nq=1):
        n = self.rows // 2 // nq
        return pl.ds(pl.multiple_of(h * (self.rows // 2) + q * n, 16), n)

    def half_rows(self, h):
        return self.part_rows(h)

    def block(self, ref, p, rows=slice(None)):
        if self.stacked:
            return ref.at[p, rows, :]
        return ref.at[rows, pl.ds(pl.multiple_of(p * self.cols, 128), self.cols)]

    def all_chips(self, ref, rows):
        return ref.at[:, rows, :] if self.stacked else ref.at[rows, :]


BIG = (
    _Layout(IN_SHARD, D_MODEL, True),
    _Layout(ATTN_W + CONV_W, D_MODEL // N_CHIPS, False),
    _Layout(D_MODEL // N_CHIPS, D_MODEL, True),
    _Layout(D_MODEL, FF2 // N_CHIPS, False),
    _Layout(D_FF // N_CHIPS, D_MODEL, True),
)
N_BIG = len(BIG)
_ANY = pl.BlockSpec(memory_space=pl.ANY)


def _position():
    x, y, c = lax.axis_index("x"), lax.axis_index("y"), lax.axis_index("c")
    return x, y, c, 2 * x + y


def _core_of_chip(p, c):
    return (p >> 1, p & 1, c)


def _place_cast(shard, lay, pos, name, after=None):
    rows, cols = shard.shape
    tr = _row_tile(rows, cols * 6)
    if lay.stacked:
        out = (lay.whole(), BF16, (None, tr, cols), lambda i, pos: (pos[0], i, 0))
    else:
        out = (lay.whole(), BF16, (tr, cols), lambda i, pos: (i, pos[0]))
    ins = [(shard, (tr, cols), lambda i, pos: (i, 0))]
    if after is not None:
        ins.append((after, (8, 128), lambda i, pos: (0, 0)))
    return _tiled(lambda a, *_: (a,), name, (rows // tr,), pos, ins, [out])[0]


def _place_cast_pair(top, bottom, lay, pos, name, after=None):
    rows, cols = top.shape
    ins = [(top, (rows, cols), lambda i, pos: (0, 0)), (bottom, (rows, cols), lambda i, pos: (0, 0))]
    if after is not None:
        ins.append((after, (8, 128), lambda i, pos: (0, 0)))
    return _tiled(lambda a, b, *_: (jnp.concatenate([a, b], axis=0),), name, (1,), pos, ins,
                  [(lay.whole(), BF16, (2 * rows, cols), lambda i, pos: (0, pos[0]))])[0]


def _adamw_pair(top, bottom, g, after=None):
    rows = top[0].shape[0]

    def body(*refs):
        (wa, ma, va, wb, mb, vb, g_ref), outs = refs[:7], refs[-8:]
        for (w, m, v), gg, o in (((wa, ma, va), g_ref[:rows], outs[:4]), ((wb, mb, vb), g_ref[rows:], outs[4:])):
            for o_ref, val in zip(o, (gg, *_adamw(w[...], gg, m[...], v[...]))):
                o_ref[...] = val

    behind = [] if after is None else [after[0:8, 0:128]]
    res = pl.pallas_call(
        body, name="adamw_w_br", out_shape=[jax.ShapeDtypeStruct(top[0].shape, F32)] * 8,
    )(*top, *bottom, g, *behind)
    return res[:4], res[4:]


def _remote(src, dst, send, recv, k, device):
    return pltpu.make_async_remote_copy(src_ref=src, dst_ref=dst, send_sem=send.at[k], recv_sem=recv.at[k],
                                        device_id=device, device_id_type=MESH)


def _arrival(dst, send, recv, k, me):
    return _remote(dst, dst, send, recv, k, me)


def _gather_ici(lay, name, q=0, nq=1):
    def plan(hbm, pos, send, recv, base):
        x, y, c, me = pos
        rows = lay.part_rows(c, q, nq)
        mine = lay.block(hbm[name], me, rows)
        starts = [_remote(mine, mine, send, recv, base + d - 1, _core_of_chip(me ^ d, c)) for d in (1, 2, 3)]
        waits = [_arrival(lay.block(hbm[name], me ^ d, rows), send, recv, base + d - 1, (x, y, c)) for d in (1, 2, 3)]
        return starts, waits
    return _Job(3, plan)


def _gather_d2d(lay, name, q=0, nq=1):
    def plan(hbm, pos, send, recv, base):
        x, y, c, me = pos
        starts, waits = [], []
        for d in (1, 2, 3):
            got = lay.block(hbm[name], me ^ d, lay.part_rows(c, q, nq))
            starts.append(_remote(got, got, send, recv, base + d - 1, (x, y, 1 - c)))
            waits.append(_arrival(lay.block(hbm[name], me ^ d, lay.part_rows(1 - c, q, nq)), send, recv, base + d - 1,
                                  (x, y, c)))
        return starts, waits
    return _Job(3, plan)


def _rs_pair(lay, grad, theirs):
    def plan(hbm, pos, send, recv, base):
        x, y, c, _ = pos
        out = _remote(lay.all_chips(hbm[grad], lay.half_rows(1 - c)), hbm[theirs], send, recv, base, (x, y, 1 - c))
        return [out], [_arrival(hbm[theirs], send, recv, base, (x, y, c))]
    return _Job(1, plan)


def _rs_chips(lay, sums, slots):
    def plan(hbm, pos, send, recv, base):
        x, y, c, me = pos
        starts = [_remote(lay.block(hbm[sums], me ^ d), hbm[slots].at[me], send, recv, base + d - 1,
                          _core_of_chip(me ^ d, c)) for d in (1, 2, 3)]
        waits = [_arrival(hbm[slots].at[me ^ d], send, recv, base + d - 1, (x, y, c)) for d in (1, 2, 3)]
        return starts, waits
    return _Job(3, plan)


def _rs_share(lay, shard):
    def plan(hbm, pos, send, recv, base):
        x, y, c, _ = pos
        mine = hbm[shard].at[lay.half_rows(c), :]
        other = hbm[shard].at[lay.half_rows(1 - c), :]
        return [_remote(mine, mine, send, recv, base, (x, y, 1 - c))], [_arrival(other, send, recv, base, (x, y, c))]
    return _Job(1, plan)


def _slots_shape(lay):
    return jax.ShapeDtypeStruct((N_CHIPS, lay.rows // 2, lay.cols), BF16)


def _theirs_shape(lay, dtype=BF16):
    return jax.ShapeDtypeStruct(lay.whole(lay.rows // 2), dtype)


def _pair_sum(grad, theirs, lay, pos, name):
    half = lay.rows // 2
    add = lambda a, b: (a.astype(F32) + b.astype(F32),)
    if lay.stacked:
        tr = _row_tile(half, lay.cols * 6)
        nt = half // tr
        flat = lambda a: a.reshape(-1, lay.cols)
        mine = lambda t, pos: ((t // nt) * (2 * nt) + pos[1] * nt + t % nt, 0)
        grid, blk = (N_CHIPS * nt,), (tr, lay.cols)
        grad, theirs = flat(grad), flat(theirs)
    else:
        tr = _row_tile(half, N_CHIPS * lay.cols * 6)
        nt = half // tr
        mine = lambda t, pos: (pos[1] * nt + t, 0)
        grid, blk = (nt,), (tr, N_CHIPS * lay.cols)
    same = lambda t, pos: (t, 0)
    out = _tiled(add, name, grid, pos, [(grad, blk, mine), (theirs, blk, same)], [(theirs.shape, BF16, blk, same)])[0]
    return out.reshape(lay.whole(half))


def _chip_sum(sums, slots, lay, pos, name, after=None):
    half = lay.rows // 2
    tr = _row_tile(half, lay.cols * 12)
    nt = half // tr
    blk3 = (None, tr, lay.cols)
    if lay.stacked:
        own = (sums, blk3, lambda i, pos: (pos[0], i, 0))
    else:
        own = (sums, (tr, lay.cols), lambda i, pos: (i, pos[0]))
    others = [(slots, blk3, functools.partial(lambda d, i, pos: (pos[0] ^ d, i, 0), d)) for d in (1, 2, 3)]

    def add(a, b1, b2, b3, *_):
        return (((a.astype(F32) + b1.astype(F32)) + b2.astype(F32)) + b3.astype(F32),)

    if after is not None:
        others.append((after, (8, 128), lambda i, pos: (0, 0)))
    return _tiled(add, name, (nt,), pos, [own] + others,
                  [((lay.rows, lay.cols), F32, (tr, lay.cols), lambda i, pos: (pos[1] * nt + i, 0))])[0]


N_DEV = 8


def _to_all(src, slots):
    def plan(hbm, pos, send, recv, base):
        x, y, c, _ = pos
        idx = 4 * x + 2 * y + c
        starts = [_remote(hbm[src], hbm[slots].at[idx], send, recv, base + k - 1,
                          (x ^ (k >> 2), y ^ ((k >> 1) & 1), c ^ (k & 1))) for k in range(1, N_DEV)]
        waits = [_arrival(hbm[slots].at[idx ^ k], send, recv, base + k - 1, (x, y, c)) for k in range(1, N_DEV)]
        return starts, waits
    return _Job(N_DEV - 1, plan)


def _sum_slots(own, slots, pos):
    def body(pos_ref, own_ref, slots_ref, o_ref):
        idx = 2 * pos_ref[0] + pos_ref[1]
        term = lambda q: jnp.where(idx == q, own_ref[...], slots_ref[q])
        acc = term(0)
        for q in range(1, N_DEV):
            acc = acc + term(q)
        o_ref[...] = acc

    return pl.pallas_call(
        body, name="sum_small", out_shape=jax.ShapeDtypeStruct(own.shape, F32),
        in_specs=[pl.BlockSpec(memory_space=pltpu.SMEM), pl.BlockSpec(memory_space=pltpu.VMEM),
                  pl.BlockSpec(memory_space=pltpu.VMEM)],
    )(pos, own, slots)


def _pack_rows(parts):
    padded = [jnp.pad(a, ((0, -a.shape[0] % 8), (0, 0))) for a in parts]
    starts = [sum(p.shape[0] for p in padded[:k]) for k in range(len(padded))]
    return jnp.concatenate(padded, axis=0), starts


def kernel(x, mix_norm, w_in, b_in, sinks, conv_w, w_attn_branch, w_conv_branch, w_out, ffn_norm, w_up, ffn_conv_w, w_down, final_norm, loss_target, m_mix_norm, m_w_in, m_b_in, m_sinks, m_conv_w, m_w_attn_branch, m_w_conv_branch, m_w_out, m_ffn_norm, m_w_up, m_ffn_conv_w, m_w_down, m_final_norm, v_mix_norm, v_w_in, v_b_in, v_sinks, v_conv_w, v_w_attn_branch, v_w_conv_branch, v_w_out, v_ffn_norm, v_w_up, v_ffn_conv_w, v_w_down, v_final_norm):
    me = 2 * lax.axis_index("x") + lax.axis_index("y")
    names = ("w_in", "w_br", "w_out", "w_up", "w_down")
    w_of = dict(w_in=w_in[0].T, w_out=w_out[0], w_up=w_up[0], w_down=w_down[0])
    m_of = dict(w_in=m_w_in[0].T, w_out=m_w_out[0], w_up=m_w_up[0], w_down=m_w_down[0])
    v_of = dict(w_in=v_w_in[0].T, w_out=v_w_out[0], w_up=v_w_up[0], w_down=v_w_down[0])
    ab = (w_attn_branch[0], m_w_attn_branch[0], v_w_attn_branch[0])
    cb = (w_conv_branch[0], m_w_conv_branch[0], v_w_conv_branch[0])

    pos = jnp.stack([me, lax.axis_index("c")]).astype(jnp.int32)

    lay = dict(zip(names, BIG))
    xs, target, sk = x[0], loss_target[0], sinks[0]
    s = xs.shape[0]
    tm, tm2, bk, bk2 = min(256, s), min(512, s), min(1024, s), min(2048, s)

    taps, (_, t0) = _pack_rows([conv_w[0], ffn_conv_w[0].reshape(3 * (FF2 // N_CHIPS // 128), 128)])
    placed = {"w_in": _place_cast(w_of["w_in"], lay["w_in"], pos, "cast_w_in")}
    fly_in, started = _start_exchange("gather_in_start", [_gather_ici(lay["w_in"], "w_in")], {"w_in": placed["w_in"]})
    taps_flight, started = _start_exchange("taps_start", [_to_all("v", "slots")],
                                           {"v": taps + started[0:1], "slots": jnp.zeros((N_DEV, *taps.shape), F32)})
    placed["w_br"] = _place_cast_pair(ab[0], cb[0], lay["w_br"], pos, "cast_w_br", after=started)
    for n in names[2:]:
        placed[n] = _place_cast(w_of[n], lay[n], pos, "cast_" + n, after=started)
    trio = ("w_br", "w_out")
    (fly_trio, fly_up, fly_down), started = _start_exchanges("gather_rest_start", [
        ([_gather_ici(lay[n], n) for n in ws], {n: placed[n] for n in ws}) for ws in (trio, ("w_up",), ("w_down",))])

    got = _finish_exchange("gather_in_wait", fly_in, after=started)
    w_in_full = _exchange("gather_in_d2d", [[_gather_d2d(lay["w_in"], "w_in")]], bufs=got)["w_in"].reshape(IN_W, D_MODEL)
    xn, qkv, c3, gates = _inproj_fwd(xs, mix_norm, w_in_full, b_in, tm2)
    k2 = _Carry([_gather_d2d(lay[n], n) for n in trio], bufs=_finish_exchange("gather_trio_wait", fly_trio, after=qkv))
    attn = _attn_fwd(qkv, sk, comm=k2)
    w_br = k2.out["w_br"]
    w_out_full = k2.out["w_out"].reshape(D_MODEL, D_MODEL)
    k3 = _Carry([_gather_d2d(lay["w_up"], "w_up")], bufs=_finish_exchange("gather_up_wait", fly_up, after=attn))
    taps = _finish_exchange("taps_wait", taps_flight, after=attn)
    taps = lax.dynamic_update_slice(taps["slots"], taps["v"][None], (2 * me + lax.axis_index("c"), 0, 0))
    conv_full = taps[0::2, 0:3].transpose(1, 0, 2).reshape(3, CONV_W)
    ffn_cw_full = taps[0::2, t0:t0 + 33].reshape(N_CHIPS, 3, FF2 // N_CHIPS).transpose(1, 0, 2).reshape(3, FF2)
    conv, a, cv, merged, h1, hn = _mix_fwd(xs, attn, c3, gates, conv_full, w_br, w_out_full, ffn_norm, tm2, comm=k3)
    w_up_full = k3.out["w_up"]
    w_down_full = _exchange("gather_down_d2d", [[_gather_d2d(lay["w_down"], "w_down")]],
                            bufs=_finish_exchange("gather_down_wait", fly_down, after=hn))["w_down"].reshape(D_FF, D_MODEL)
    u, up, act, dh2, loss_part, g_fn = _ffn_fwd_loss(hn, h1, w_up_full, ffn_cw_full, w_down_full,
                                                     final_norm[None, :], target, tm)

    grads, sums, slots = {}, {}, {}

    def pair(*ws):
        return _Carry([_rs_pair(lay[n], "g_" + n, "t_" + n) for n in ws], reads={"g_" + n: grads[n] for n in ws},
                      fresh={"t_" + n: _theirs_shape(lay[n], grads[n].dtype) for n in ws})

    def chips(*ws, also=None):
        k = _Carry([_rs_chips(lay[n], "s_" + n, "r_" + n) for n in ws], reads={"s_" + n: sums[n] for n in ws},
                   fresh={"r_" + n: _slots_shape(lay[n]) for n in ws})
        if also is not None:
            k = _Carry(k.jobs + also.jobs, {**k.reads, **also.reads}, None, {**k.fresh, **also.fresh})
        return k

    def pair_sums(k, *ws):
        for n in ws:
            sums[n] = _pair_sum(grads[n], k.out["t_" + n], lay[n], pos, "pair_sum_" + n)

    def take_slots(k, *ws):
        for n in ws:
            slots[n] = k.out["r_" + n]

    du, dh1, g_fcw, g_g2 = _ffn_bwd(dh2, u, up, h1, w_up_full, ffn_cw_full, w_down_full, ffn_norm, tm)
    grads["w_down"] = _wgrad(act, dh2, D_FF // 2, D_MODEL, bk2, "wgrad_down").reshape(lay["w_down"].whole())
    k4 = pair("w_down")
    grads["w_up"] = _wgrad(hn, du, D_MODEL, FF2 // 4, bk2, "wgrad_up", comm=k4)
    pair_sums(k4, "w_down")
    k5 = chips("w_down", also=pair("w_up"))
    dattn, dc3, dgt, g_cw, grads["w_br"], gw_out = _mix_bwd(
        dh1, gates, a, cv, c3, attn, conv, merged, conv_full, w_br, w_out_full, tm2, comm=k5)
    grads["w_out"] = gw_out.reshape(lay["w_out"].whole())
    take_slots(k5, "w_down")
    pair_sums(k5, "w_up")
    k6 = chips("w_up", also=pair(*trio))
    dq, dk_even, dk_odd, dv_even, dv_odd, g_sk = _attn_bwd(qkv, sk, attn, dattn, comm=k6)
    take_slots(k6, "w_up")
    pair_sums(k6, *trio)
    trio_flight, started = _start_exchange(
        "rs_chips_trio_start", [_rs_chips(lay[n], "s_" + n, "r_" + n) for n in trio],
        {**{"s_" + n: sums[n] for n in trio}, **{"r_" + n: _slots_shape(lay[n]) for n in trio}})
    behind = mix_norm + jnp.tile(started[0:1], (1, D_MODEL // 128))
    grad_x, gw_in, g_b, g_g1 = _inproj_bwd(dq, (dk_even, dk_odd), (dv_even, dv_odd), dc3, dgt, w_in_full, xs, xn,
                                           dh1, behind)
    grads["w_in"] = gw_in.reshape(lay["w_in"].whole())

    parts = [loss_part, g_g1, g_b, jnp.pad(g_sk[:, 0], (0, 120))[None, :], g_cw, g_g2, g_fcw, g_fn]
    packed, at = _pack_rows([p.reshape(-1, 128) for p in parts])
    small_flight, started = _start_exchange("small_start", [_to_all("v", "slots")],
                                            {"v": packed, "slots": jnp.zeros((N_DEV, *packed.shape), F32)})
    others = names[1:]
    in_flight, started = _start_exchange("rs_pair_in_start", [_rs_pair(lay["w_in"], "g", "t")],
                                         {"g": grads["w_in"], "t": _theirs_shape(lay["w_in"]), "behind": started})
    halves = {n: _chip_sum(sums[n], slots[n], lay[n], pos, "chip_sum_" + n, after=started) for n in ("w_up", "w_down")}
    landed = _finish_exchange("rs_pair_in_wait", in_flight, after=halves["w_down"])
    sums["w_in"] = _pair_sum(landed["g"], landed["t"], lay["w_in"], pos, "pair_sum_w_in")
    in_flight, started = _start_exchange("rs_chips_in_start", [_rs_chips(lay["w_in"], "s", "r")],
                                         {"s": sums["w_in"], "r": _slots_shape(lay["w_in"])})
    landed = _finish_exchange("rs_chips_trio_wait", trio_flight, after=started)
    for n in trio:
        halves[n] = _chip_sum(landed["s_" + n], landed["r_" + n], lay[n], pos, "chip_sum_" + n)
    shared = _exchange("share_halves", [[_rs_share(lay[n], n) for n in others]], bufs=halves)

    def adam(n, g, after=None):
        return _rowwise(lambda w, g, m, v: (g, *_adamw(w, g, m, v)), [w_of[n], g, m_of[n], v_of[n]], [F32] * 4,
                        "adamw_" + n, after=after)

    new_of, last = {}, None
    for n in ("w_up", "w_down", "w_out"):
        new_of[n] = adam(n, shared[n], last)
        last = new_of[n][1]
    new_of["w_ab"], new_of["w_cb"] = _adamw_pair(ab, cb, shared["w_br"], after=last)
    last = new_of["w_cb"][1]

    arrived = _finish_exchange("small_wait", small_flight, after=last)
    total = _sum_slots(arrived["v"], arrived["slots"], pos)
    part = lambda k: total[at[k]:at[k] + parts[k].size // 128].reshape(parts[k].shape)
    loss = total[0, 0]
    g_mix, g_b, g_g2, g_fn = part(1), part(2), part(5), part(7)
    g_sk = part(3)[:, 0:N_HEADS]
    g_cw = lax.dynamic_slice(part(4), (0, me * 128), (3, 128))
    g_fcw = lax.dynamic_slice(part(6), (0, me * (FF2 // N_CHIPS)), (3, FF2 // N_CHIPS))
    small_p = [
        (mix_norm, g_mix, m_mix_norm, v_mix_norm), (b_in, g_b, m_b_in, v_b_in), (sinks, g_sk, m_sinks, v_sinks),
        (conv_w[0], g_cw, m_conv_w[0], v_conv_w[0]), (ffn_norm, g_g2, m_ffn_norm, v_ffn_norm),
        (ffn_conv_w[0], g_fcw, m_ffn_conv_w[0], v_ffn_conv_w[0]),
        (final_norm[None, :], g_fn, m_final_norm[None, :], v_final_norm[None, :])]
    small_new = _adamw_small(small_p)
    small_new = [small_new[3 * k:3 * k + 3] for k in range(len(small_p))]

    landed = _finish_exchange("rs_chips_in_wait", in_flight, after=small_new[0][0])
    half_in = _chip_sum(landed["s"], landed["r"], lay["w_in"], pos, "chip_sum_w_in")
    shared["w_in"] = _exchange("share_in", [[_rs_share(lay["w_in"], "w_in")]], bufs={"w_in": half_in})["w_in"]
    new_of["w_in"] = [a.T for a in adam("w_in", shared["w_in"])]
    big = ("w_in", "w_ab", "w_cb", "w_out", "w_up", "w_down")
    big_g = [new_of[n][0] for n in big]
    big_new = [new_of[n][1:] for n in big]

    order = [("s", 0), ("b", 0), ("s", 1), ("s", 2), ("s", 3), ("b", 1), ("b", 2), ("b", 3), ("s", 4), ("b", 4),
             ("s", 5), ("b", 5), ("s", 6)]
    shapes = [mix_norm.shape, w_in.shape, b_in.shape, sinks.shape, conv_w.shape, w_attn_branch.shape,
              w_conv_branch.shape, w_out.shape, ffn_norm.shape, w_up.shape, ffn_conv_w.shape, w_down.shape,
              final_norm.shape]
    small_g = [p[1] for p in small_p]
    out_g = [(small_g[k] if kind == "s" else big_g[k]).reshape(shp) for (kind, k), shp in zip(order, shapes)]
    news = [[(small_new[k][j] if kind == "s" else big_new[k][j]).reshape(shp) for (kind, k), shp in zip(order, shapes)]
            for j in range(3)]
    return (loss, grad_x[None], *out_g, *news[0], *news[1], *news[2])
```

```python
import functools

import jax
import jax.numpy as jnp
from jax import lax
from jax.experimental import pallas as pl
from jax.experimental.pallas import tpu as pltpu

F32 = jnp.float32
BF16 = jnp.bfloat16

D_MODEL = 1024
HEAD_DIM = 64
N_HEADS = 8
N_KV_HEADS = 2
GROUP = N_HEADS // N_KV_HEADS
BLOCK = 128
ATTN_SCALE = HEAD_DIM ** -0.5
ATTN_W = N_HEADS * HEAD_DIM
KV_W = N_KV_HEADS * HEAD_DIM
CONV_W = 512
QKV_W = ATTN_W + 2 * KV_W
C3_W = 3 * CONV_W
GATES_W = 2 * D_MODEL
IN_W = QKV_W + C3_W + GATES_W
D_FF = 2816
FF2 = 2 * D_FF
NORM_EPS = 1e-5
N_CHIPS = 4
IN_SHARD = IN_W // N_CHIPS
NEG = -1e30

ADAM_LR = 0.001
ADAM_B1 = 0.9
ADAM_B2 = 0.999
ADAM_EPS = 1e-08
ADAM_WD = 0.01
ADAM_STEP = 10

VMEM_LIMIT = 56 * 1024 * 1024
MESH = pl.DeviceIdType.MESH

NT = (((1,), (1,)), ((), ()))
TN = (((0,), (0,)), ((), ()))


def _params(*sem):
    return pltpu.CompilerParams(dimension_semantics=sem, vmem_limit_bytes=VMEM_LIMIT)


def _resident(shape):
    return pl.BlockSpec(shape, lambda *_: (0,) * len(shape), pipeline_mode=pl.Buffered(1))


def _sigmoid(v):
    return 0.5 * jnp.tanh(0.5 * v) + 0.5


def _rstd(v):
    return lax.rsqrt(jnp.mean(v * v, axis=-1, keepdims=True) + NORM_EPS)


def _rms_bwd(dy, v, rstd, g):
    vhat = v * rstd
    t = dy * g
    return rstd * (t - vhat * jnp.mean(t * vhat, axis=-1, keepdims=True)), dy * vhat


def _taps(z, cw):
    return cw[2:3] * z + cw[1:2] * pltpu.roll(z, 1, 0) + cw[0:1] * pltpu.roll(z, 2, 0)


def _causal_conv(z, prev, cw):
    edge = _taps(jnp.concatenate([prev, z[0:8]], axis=0), cw)
    return jnp.concatenate([edge[8:16], _taps(z, cw)[8:]], axis=0)


def _rows_after(z, nxt):
    n = z.shape[0]
    edge = jnp.concatenate([z[n - 8:n], nxt], axis=0)
    return tuple(jnp.concatenate([pltpu.roll(z, n - k, 0)[:n - 8], pltpu.roll(edge, 16 - k, 0)[0:8]], axis=0)
                 for k in (1, 2))


def _inproj_fwd(x, g1, w_in, b_in, tm, comm=None):
    s = x.shape[0]

    def body(x_ref, g_ref, w_ref, b_ref, xn_ref, qkv_ref, c3_ref, gt_ref):
        xf = x_ref[...]
        xn = (xf * _rstd(xf) * g_ref[...]).astype(BF16)
        xn_ref[...] = xn

        proj = (lax.dot_general(xn, w_ref[...], NT, preferred_element_type=F32) + b_ref[...]).astype(BF16)
        qkv_ref[...] = proj[:, :QKV_W]
        c3_ref[...] = proj[:, QKV_W:QKV_W + C3_W]
        gt_ref[...] = proj[:, QKV_W + C3_W:]

    row = lambda w: pl.BlockSpec((tm, w), lambda i: (i, 0))
    return _call(
        comm, body, name="inproj_fwd", grid=(s // tm,),
        in_specs=[row(D_MODEL), _resident((1, D_MODEL)), _resident((IN_W, D_MODEL)), _resident((1, IN_W))],
        out_specs=[row(D_MODEL), row(QKV_W), row(C3_W), row(GATES_W)],
        out_shape=[jax.ShapeDtypeStruct((s, D_MODEL), BF16), jax.ShapeDtypeStruct((s, QKV_W), BF16),
                   jax.ShapeDtypeStruct((s, C3_W), BF16), jax.ShapeDtypeStruct((s, GATES_W), BF16)],
        compiler_params=_params("parallel"),
    )(x, g1, w_in, b_in)


def _attn_bias():
    qi = (jnp.arange(GROUP * BLOCK) % BLOCK)[:, None]
    kj = jnp.arange(2 * BLOCK)[None, :]
    band = (kj > qi) & (kj <= qi + BLOCK)
    return jnp.stack([jnp.where(band & (kj >= BLOCK), 0.0, NEG), jnp.where(band, 0.0, NEG)]).astype(F32)


def _attn_bias_spec():
    return pl.BlockSpec((None, GROUP * BLOCK, 2 * BLOCK), lambda i: (jnp.minimum(i, 1), 0, 0))


def _sink_column(sk_ref, h):
    rows = lax.broadcasted_iota(jnp.int32, (GROUP * BLOCK, 1), 0)
    col = jnp.full((GROUP * BLOCK, 1), sk_ref[h * GROUP], F32)
    for g in range(1, GROUP):
        col = jnp.where(rows >= g * BLOCK, sk_ref[h * GROUP + g], col)
    return col


def _stack_heads(t, h):
    return jnp.concatenate(
        [t[:, (h * GROUP + g) * HEAD_DIM:(h * GROUP + g + 1) * HEAD_DIM] for g in range(GROUP)], axis=0)


def _unstack_heads(per_kv):
    return jnp.concatenate(
        [t[g * BLOCK:(g + 1) * BLOCK] for t in per_kv for g in range(GROUP)], axis=1)


def _pair_specs(npair):
    cur = lambda i: jnp.minimum(i, npair - 1)
    prev = lambda i: jnp.maximum(2 * jnp.minimum(i, npair - 1) - 1, 0)
    kv = ATTN_W // KV_W
    return (pl.BlockSpec((2 * BLOCK, ATTN_W), lambda i: (cur(i), 0)),
            pl.BlockSpec((BLOCK, KV_W), lambda i: (prev(i), kv)), pl.BlockSpec((2 * BLOCK, KV_W), lambda i: (cur(i), kv)),
            pl.BlockSpec((BLOCK, KV_W), lambda i: (prev(i), kv + 1)),
            pl.BlockSpec((2 * BLOCK, KV_W), lambda i: (cur(i), kv + 1)))


def _attn_fwd(qkv, sinks, comm=None):
    s = qkv.shape[0]
    npair = s // (2 * BLOCK)

    def body(sk_ref, bias0_ref, bias1_ref, q_ref, kp_ref, kc_ref, vp_ref, vc_ref, o_ref):
        kc, vc = kc_ref[...], vc_ref[...]
        for b, (bias_ref, kp, vp) in enumerate(((bias0_ref, kp_ref[...], vp_ref[...]),
                                                (bias1_ref, kc[:BLOCK], vc[:BLOCK]))):
            rows = slice(b * BLOCK, (b + 1) * BLOCK)
            q, bias = q_ref[rows, :], bias_ref[...]
            outs = []
            for h in range(N_KV_HEADS):
                hs = slice(h * HEAD_DIM, (h + 1) * HEAD_DIM)
                k2 = jnp.concatenate([kp[:, hs], kc[rows, hs]], axis=0)
                v2 = jnp.concatenate([vp[:, hs], vc[rows, hs]], axis=0)
                sc = lax.dot_general(_stack_heads(q, h), k2, NT, preferred_element_type=F32) * ATTN_SCALE + bias
                sink = _sink_column(sk_ref, h)
                m = jnp.maximum(jnp.max(sc, axis=1, keepdims=True), sink)
                p = jnp.exp(sc - m)
                den = jnp.sum(p, axis=1, keepdims=True) + jnp.exp(sink - m)
                outs.append(jnp.dot(p.astype(BF16), v2, preferred_element_type=F32) / den)
            o_ref[rows, :] = _unstack_heads(outs).astype(BF16)

    general = pl.BlockSpec((None, GROUP * BLOCK, 2 * BLOCK), lambda i: (1, 0, 0))
    return _call(
        comm, body, name="attn_fwd", grid=(npair,),
        in_specs=[pl.BlockSpec(memory_space=pltpu.SMEM), _attn_bias_spec(), general, *_pair_specs(npair)],
        out_specs=pl.BlockSpec((2 * BLOCK, ATTN_W), lambda i: (i, 0)),
        out_shape=jax.ShapeDtypeStruct((s, ATTN_W), BF16),
        compiler_params=_params("parallel"),
    )(sinks, _attn_bias(), _attn_bias(), qkv, qkv, qkv, qkv, qkv)


def _mix_fwd(x, attn, c3, gates, conv_w, w_br, w_out, g2, tm, comm=None):
    s = x.shape[0]

    def body(x_ref, at_ref, c3_ref, gt_ref, cw_ref, wbr_ref, wo_ref, g_ref,
             conv_ref, a_ref, cv_ref, mg_ref, h1_ref, hn_ref, carry_ref):
        @pl.when(pl.program_id(0) == 0)
        def _():
            carry_ref[...] = jnp.zeros_like(carry_ref)

        c3v = c3_ref[...].astype(F32)
        cb, cc, cx = c3v[:, :CONV_W], c3v[:, CONV_W:2 * CONV_W], c3v[:, 2 * CONV_W:]
        z = cc * cx
        cz = _causal_conv(z, carry_ref[...], cw_ref[...])
        carry_ref[...] = z[tm - 8:tm]
        conv = (cb * cz).astype(BF16)
        conv_ref[...] = conv
        a = jnp.dot(at_ref[...], wbr_ref[:ATTN_W, :], preferred_element_type=F32)
        cv = jnp.dot(conv, wbr_ref[ATTN_W:, :], preferred_element_type=F32)
        a_ref[...] = a.astype(BF16)
        cv_ref[...] = cv.astype(BF16)
        gt = gt_ref[...].astype(F32)
        merged = (_sigmoid(gt[:, :D_MODEL]) * a + _sigmoid(gt[:, D_MODEL:]) * cv).astype(BF16)
        mg_ref[...] = merged
        h1 = x_ref[...] + jnp.dot(merged, wo_ref[...], preferred_element_type=F32)
        h1_ref[...] = h1
        hn_ref[...] = (h1 * _rstd(h1) * g_ref[...]).astype(BF16)

    row = lambda w: pl.BlockSpec((tm, w), lambda i: (i, 0))
    return _call(
        comm, body, name="mix_fwd", grid=(s // tm,),
        in_specs=[row(D_MODEL), row(ATTN_W), row(C3_W), row(GATES_W), _resident((3, CONV_W)),
                  _resident((ATTN_W + CONV_W, D_MODEL)), _resident((D_MODEL, D_MODEL)), _resident((1, D_MODEL))],
        out_specs=[row(CONV_W), row(D_MODEL), row(D_MODEL), row(D_MODEL), row(D_MODEL), row(D_MODEL)],
        out_shape=[jax.ShapeDtypeStruct((s, CONV_W), BF16), jax.ShapeDtypeStruct((s, D_MODEL), BF16),
                   jax.ShapeDtypeStruct((s, D_MODEL), BF16), jax.ShapeDtypeStruct((s, D_MODEL), BF16),
                   jax.ShapeDtypeStruct((s, D_MODEL), F32), jax.ShapeDtypeStruct((s, D_MODEL), BF16)],
        scratch_shapes=[pltpu.VMEM((8, CONV_W), F32)],
        compiler_params=_params("arbitrary"),
    )(x, attn, c3, gates, conv_w, w_br, w_out, g2)


def _ffn_fwd_loss(hn, h1, w_up, ffn_cw, w_down, g3, target, tm):
    s = hn.shape[0]

    def body(hn_ref, h1_ref, wu_ref, cw_ref, wd_ref, g_ref, t_ref,
             u_ref, up_ref, act_ref, dh2_ref, loss_ref, gfn_ref, carry_ref):
        @pl.when(pl.program_id(0) == 0)
        def _():
            carry_ref[...] = jnp.zeros_like(carry_ref)
            loss_ref[...] = jnp.zeros_like(loss_ref)
            gfn_ref[...] = jnp.zeros_like(gfn_ref)

        u = jnp.dot(hn_ref[...], wu_ref[...], preferred_element_type=F32)
        u_ref[...] = u.astype(BF16)
        up = _causal_conv(u, carry_ref[...], cw_ref[...])
        up_ref[...] = up
        carry_ref[...] = u[tm - 8:tm]
        gate, val = up[:, :D_FF], up[:, D_FF:]
        act = (gate * _sigmoid(gate) * val).astype(BF16)
        act_ref[...] = act
        h2 = h1_ref[...] + jnp.dot(act, wd_ref[...], preferred_element_type=F32)
        rstd = _rstd(h2)
        g = g_ref[...]
        err = h2 * rstd * g - t_ref[...]
        loss_ref[...] += jnp.sum(err * err) * (0.5 / D_MODEL)
        dh2, dg = _rms_bwd(err * (1.0 / D_MODEL), h2, rstd, g)
        dh2_ref[...] = dh2
        gfn_ref[...] += jnp.sum(dg, axis=0, keepdims=True)

    row = lambda w: pl.BlockSpec((tm, w), lambda i: (i, 0))
    acc = lambda w: pl.BlockSpec((1, w), lambda i: (0, 0))
    return pl.pallas_call(
        body, name="ffn_fwd_loss", grid=(s // tm,),
        in_specs=[row(D_MODEL), row(D_MODEL), _resident((D_MODEL, FF2)), _resident((3, FF2)),
                  _resident((D_FF, D_MODEL)), _resident((1, D_MODEL)), row(D_MODEL)],
        out_specs=[row(FF2), row(FF2), row(D_FF), row(D_MODEL), acc(128), acc(D_MODEL)],
        out_shape=[jax.ShapeDtypeStruct((s, FF2), BF16), jax.ShapeDtypeStruct((s, FF2), F32),
                   jax.ShapeDtypeStruct((s, D_FF), BF16),
                   jax.ShapeDtypeStruct((s, D_MODEL), F32), jax.ShapeDtypeStruct((1, 128), F32),
                   jax.ShapeDtypeStruct((1, D_MODEL), F32)],
        scratch_shapes=[pltpu.VMEM((8, FF2), F32)],
        compiler_params=_params("arbitrary"),
    )(hn, h1, w_up, ffn_cw, w_down, g3, target)


def _ffn_bwd(dh2, u, up, h1, w_up, ffn_cw, w_down, g2, tm):
    s = dh2.shape[0]
    nt = s // tm

    def body(dh2_ref, u_ref, up_ref, h1_ref, wu_ref, cw_ref, wd_ref, g_ref,
             du_ref, dh1_ref, gcw_ref, gg_ref, carry_ref):
        @pl.when(pl.program_id(0) == 0)
        def _():
            for ref in (carry_ref, gcw_ref, gg_ref):
                ref[...] = jnp.zeros_like(ref)

        dh2v = dh2_ref[...]
        dact = lax.dot_general(dh2v.astype(BF16), wd_ref[...], NT, preferred_element_type=F32)
        upv = up_ref[...]
        gate, val = upv[:, :D_FF], upv[:, D_FF:]
        sg = _sigmoid(gate)
        dval = dact * (gate * sg)
        dgate = dact * val * (sg * (1.0 + gate * (1.0 - sg)))
        dup = jnp.concatenate([dgate, dval], axis=1)
        dup1, dup2 = _rows_after(dup, carry_ref[...])
        carry_ref[...] = dup[0:8]
        u = u_ref[...].astype(F32)
        gcw_ref[2:3, :] += jnp.sum(dup * u, axis=0, keepdims=True)
        gcw_ref[1:2, :] += jnp.sum(dup1 * u, axis=0, keepdims=True)
        gcw_ref[0:1, :] += jnp.sum(dup2 * u, axis=0, keepdims=True)
        cw = cw_ref[...]
        du = (cw[2:3] * dup + cw[1:2] * dup1 + cw[0:1] * dup2).astype(BF16)
        du_ref[...] = du
        dhn = lax.dot_general(du, wu_ref[...], NT, preferred_element_type=F32)
        h1v = h1_ref[...]
        dh1, dg = _rms_bwd(dhn, h1v, _rstd(h1v), g_ref[...])
        dh1_ref[...] = dh2v + dh1
        gg_ref[...] += jnp.sum(dg, axis=0, keepdims=True)

    row = lambda w: pl.BlockSpec((tm, w), lambda i: (nt - 1 - i, 0))
    return pl.pallas_call(
        body, name="ffn_bwd", grid=(nt,),
        in_specs=[row(D_MODEL), row(FF2), row(FF2),
                  row(D_MODEL), _resident((D_MODEL, FF2)), _resident((3, FF2)), _resident((D_FF, D_MODEL)),
                  _resident((1, D_MODEL))],
        out_specs=[row(FF2), row(D_MODEL), pl.BlockSpec((3, FF2), lambda i: (0, 0)),
                   pl.BlockSpec((1, D_MODEL), lambda i: (0, 0))],
        out_shape=[jax.ShapeDtypeStruct((s, FF2), BF16), jax.ShapeDtypeStruct((s, D_MODEL), F32),
                   jax.ShapeDtypeStruct((3, FF2), F32), jax.ShapeDtypeStruct((1, D_MODEL), F32)],
        scratch_shapes=[pltpu.VMEM((8, FF2), F32)],
        compiler_params=_params("arbitrary"),
    )(dh2, u, up, h1, w_up, ffn_cw, w_down, g2)


def _mix_bwd(dh1, gates, a, cv, c3, attn, conv, merged, conv_w, w_br, w_out, tm, comm=None):
    s = dh1.shape[0]
    nt = s // tm
    halo = 16

    def body(dh1_ref, gt_ref, a_ref, cv_ref, c3_ref, ch_ref, at_ref, cn_ref, mg_ref, cw_ref, wbr_ref,
             wo_ref, dat_ref, dc3_ref, dgt_ref, gcw_ref, gbr_ref, gout_ref, carry_ref, br_acc, out_acc):
        i = pl.program_id(0)

        @pl.when(i == 0)
        def _():
            for ref in (carry_ref, gcw_ref, br_acc, out_acc):
                ref[...] = jnp.zeros_like(ref)

        dh1v = dh1_ref[...].astype(BF16)
        out_acc[...] += lax.dot_general(mg_ref[...], dh1v, TN, preferred_element_type=F32)
        dm = lax.dot_general(dh1v, wo_ref[...], NT, preferred_element_type=F32)
        gt = gt_ref[...].astype(F32)
        sa, sc = _sigmoid(gt[:, :D_MODEL]), _sigmoid(gt[:, D_MODEL:])
        da = (dm * sa).astype(BF16)
        dcv = (dm * sc).astype(BF16)
        br_acc[:ATTN_W, :] += lax.dot_general(at_ref[...], da, TN, preferred_element_type=F32)
        br_acc[ATTN_W:, :] += lax.dot_general(cn_ref[...], dcv, TN, preferred_element_type=F32)
        dgt_ref[...] = jnp.concatenate(
            [dm * a_ref[...].astype(F32) * (sa * (1.0 - sa)), dm * cv_ref[...].astype(F32) * (sc * (1.0 - sc))],
            axis=1).astype(BF16)
        dat_ref[...] = lax.dot_general(da, wbr_ref[:ATTN_W, :], NT, preferred_element_type=F32).astype(BF16)
        dconv = lax.dot_general(dcv, wbr_ref[ATTN_W:, :], NT, preferred_element_type=F32)
        c3v = c3_ref[...].astype(F32)
        cb, cc, cx = c3v[:, :CONV_W], c3v[:, CONV_W:2 * CONV_W], c3v[:, 2 * CONV_W:]
        z = cc * cx
        chv = ch_ref[...].astype(F32)[halo - 8:halo] * (i < nt - 1).astype(F32)
        zh = chv[:, CONV_W:2 * CONV_W] * chv[:, 2 * CONV_W:]
        cw = cw_ref[...]
        cz = _causal_conv(z, zh, cw)
        dcz = dconv * cb
        dcz1, dcz2 = _rows_after(dcz, carry_ref[...])
        carry_ref[...] = dcz[0:8]
        gcw_ref[2:3, :] += jnp.sum(dcz * z, axis=0, keepdims=True)
        gcw_ref[1:2, :] += jnp.sum(dcz1 * z, axis=0, keepdims=True)
        gcw_ref[0:1, :] += jnp.sum(dcz2 * z, axis=0, keepdims=True)
        dz = cw[2:3] * dcz + cw[1:2] * dcz1 + cw[0:1] * dcz2
        dc3_ref[...] = jnp.concatenate([dconv * cz, dz * cx, dz * cc], axis=1).astype(BF16)

        @pl.when(i == nt - 1)
        def _():
            gbr_ref[...] = br_acc[...].astype(BF16)
            gout_ref[...] = out_acc[...].astype(BF16)

    row = lambda w: pl.BlockSpec((tm, w), lambda i: (nt - 1 - i, 0))
    return _call(
        comm, body, name="mix_bwd", grid=(nt,),
        in_specs=[row(D_MODEL), row(GATES_W), row(D_MODEL), row(D_MODEL), row(C3_W),
                  pl.BlockSpec((halo, C3_W), lambda i: (jnp.maximum((nt - 1 - i) * (tm // halo) - 1, 0), 0)),
                  row(ATTN_W), row(CONV_W), row(D_MODEL), _resident((3, CONV_W)),
                  _resident((ATTN_W + CONV_W, D_MODEL)), _resident((D_MODEL, D_MODEL))],
        out_specs=[row(ATTN_W), row(C3_W), row(GATES_W), pl.BlockSpec((3, CONV_W), lambda i: (0, 0)),
                   _resident((ATTN_W + CONV_W, D_MODEL)), _resident((D_MODEL, D_MODEL))],
        out_shape=[jax.ShapeDtypeStruct((s, ATTN_W), BF16), jax.ShapeDtypeStruct((s, C3_W), BF16),
                   jax.ShapeDtypeStruct((s, GATES_W), BF16), jax.ShapeDtypeStruct((3, CONV_W), F32),
                   jax.ShapeDtypeStruct((ATTN_W + CONV_W, D_MODEL), BF16),
                   jax.ShapeDtypeStruct((D_MODEL, D_MODEL), BF16)],
        scratch_shapes=[pltpu.VMEM((8, CONV_W), F32), pltpu.VMEM((ATTN_W + CONV_W, D_MODEL), F32),
                        pltpu.VMEM((D_MODEL, D_MODEL), F32)],
        compiler_params=_params("arbitrary"),
    )(dh1, gates, a, cv, c3, c3, attn, conv, merged, conv_w, w_br, w_out)


def _attn_bwd(qkv, sinks, o, do, comm=None):
    s = qkv.shape[0]
    npair = s // (2 * BLOCK)

    def one_block(sk_ref, bias, q, kp, kc, vp, vc, ov, dov, dsk_ref):
        dqs, dks, dvs = [], [], []
        for h in range(N_KV_HEADS):
            hs = slice(h * HEAD_DIM, (h + 1) * HEAD_DIM)
            k2 = jnp.concatenate([kp[:, hs], kc[:, hs]], axis=0)
            v2 = jnp.concatenate([vp[:, hs], vc[:, hs]], axis=0)
            qg, og, dog = _stack_heads(q, h), _stack_heads(ov, h), _stack_heads(dov, h)
            sc = lax.dot_general(qg, k2, NT, preferred_element_type=F32) * ATTN_SCALE + bias
            sink = _sink_column(sk_ref, h)
            m = jnp.maximum(jnp.max(sc, axis=1, keepdims=True), sink)
            p = jnp.exp(sc - m)
            psink = jnp.exp(sink - m)
            inv = 1.0 / (jnp.sum(p, axis=1, keepdims=True) + psink)
            p = p * inv
            delta = jnp.sum(dog.astype(F32) * og.astype(F32), axis=1, keepdims=True)
            dp = lax.dot_general(dog, v2, NT, preferred_element_type=F32)
            ds = (p * (dp - delta)).astype(BF16)
            dqs.append(jnp.dot(ds, k2, preferred_element_type=F32) * ATTN_SCALE)
            dks.append(lax.dot_general(ds, qg, TN, preferred_element_type=F32) * ATTN_SCALE)
            dvs.append(lax.dot_general(p.astype(BF16), dog, TN, preferred_element_type=F32))
            dsink = -(psink * inv * delta)
            for g in range(GROUP):
                r = h * GROUP + g
                dsk_ref[r:r + 1, :] += jnp.sum(dsink[g * BLOCK:(g + 1) * BLOCK])
        return _unstack_heads(dqs), jnp.concatenate(dks, axis=1), jnp.concatenate(dvs, axis=1)

    def body(sk_ref, bias0_ref, bias1_ref, q_ref, kp_ref, kc_ref, vp_ref, vc_ref, o_ref, do_ref,
             dq_ref, dke_ref, dko_ref, dve_ref, dvo_ref, dsk_ref, ck_ref, cvv_ref):
        i = pl.program_id(0)

        @pl.when(i == 0)
        def _():
            for ref in (ck_ref, cvv_ref, dsk_ref):
                ref[...] = jnp.zeros_like(ref)

        @pl.when(i < npair)
        def _():
            kc, vc = kc_ref[...], vc_ref[...]
            first, second = slice(0, BLOCK), slice(BLOCK, 2 * BLOCK)
            dq0, dk0, dv0 = one_block(sk_ref, bias0_ref[...], q_ref[first, :], kp_ref[...], kc[first], vp_ref[...],
                                      vc[first], o_ref[first, :], do_ref[first, :], dsk_ref)
            dq1, dk1, dv1 = one_block(sk_ref, bias1_ref[...], q_ref[second, :], kc[first], kc[second], vc[first],
                                      vc[second], o_ref[second, :], do_ref[second, :], dsk_ref)
            dq_ref[first, :] = dq0.astype(BF16)
            dq_ref[second, :] = dq1.astype(BF16)
            dko_ref[...] = (ck_ref[...] + dk0[:BLOCK]).astype(BF16)
            dvo_ref[...] = (cvv_ref[...] + dv0[:BLOCK]).astype(BF16)
            dke_ref[...] = (dk0[BLOCK:] + dk1[:BLOCK]).astype(BF16)
            dve_ref[...] = (dv0[BLOCK:] + dv1[:BLOCK]).astype(BF16)
            ck_ref[...] = dk1[BLOCK:]
            cvv_ref[...] = dv1[BLOCK:]

        @pl.when(i == npair)
        def _():
            dko_ref[...] = ck_ref[...].astype(BF16)
            dvo_ref[...] = cvv_ref[...].astype(BF16)

    cur = lambda i: jnp.minimum(i, npair - 1)
    done = lambda i: jnp.maximum(i - 1, 0)
    rows = pl.BlockSpec((2 * BLOCK, ATTN_W), lambda i: (cur(i), 0))
    even = pl.BlockSpec((BLOCK, KV_W), lambda i: (cur(i), 0))
    odd = pl.BlockSpec((BLOCK, KV_W), lambda i: (done(i), 0))
    general = pl.BlockSpec((None, GROUP * BLOCK, 2 * BLOCK), lambda i: (1, 0, 0))
    half = jax.ShapeDtypeStruct((s // 2, KV_W), BF16)
    return _call(
        comm, body, name="attn_bwd", grid=(npair + 1,),
        in_specs=[pl.BlockSpec(memory_space=pltpu.SMEM), _attn_bias_spec(), general, *_pair_specs(npair), rows, rows],
        out_specs=[rows, even, odd, even, odd, pl.BlockSpec((N_HEADS, 128), lambda i: (0, 0))],
        out_shape=[jax.ShapeDtypeStruct((s, ATTN_W), BF16), half, half, half, half,
                   jax.ShapeDtypeStruct((N_HEADS, 128), F32)],
        scratch_shapes=[pltpu.VMEM((BLOCK, KV_W), F32), pltpu.VMEM((BLOCK, KV_W), F32)],
        compiler_params=_params("arbitrary"),
    )(sinks, _attn_bias(), _attn_bias(), qkv, qkv, qkv, qkv, qkv, o, do)


def _inproj_bwd(dq, dk, dv, dc3, dgt, w_in, x, xn, dh1, g1):
    s = x.shape[0]
    tm = min(2 * BLOCK, s)
    nt = s // tm

    def body(dq_ref, dke_ref, dko_ref, dve_ref, dvo_ref, dc3_ref, dgt_ref, w_ref, x_ref, xn_ref, dh1_ref, g_ref,
             dx_ref, gw_ref, gb_ref, gg_ref, acc_ref):
        i = pl.program_id(0)

        @pl.when(i == 0)
        def _():
            for ref in (gb_ref, gg_ref, acc_ref):
                ref[...] = jnp.zeros_like(ref)

        dk = jnp.concatenate([dke_ref[...], dko_ref[...]], axis=0)
        dv = jnp.concatenate([dve_ref[...], dvo_ref[...]], axis=0)
        dp = jnp.concatenate([dq_ref[...], dk, dv, dc3_ref[...], dgt_ref[...]], axis=1)
        acc_ref[...] += lax.dot_general(dp, xn_ref[...], TN, preferred_element_type=F32)
        gb_ref[...] += jnp.sum(dp.astype(F32), axis=0, keepdims=True)
        dxn = jnp.dot(dp, w_ref[...], preferred_element_type=F32)
        xf = x_ref[...]
        dx, dg = _rms_bwd(dxn, xf, _rstd(xf), g_ref[...])
        dx_ref[...] = dh1_ref[...] + dx
        gg_ref[...] += jnp.sum(dg, axis=0, keepdims=True)

        @pl.when(i == nt - 1)
        def _():
            gw_ref[...] = acc_ref[...].astype(BF16)

    row = lambda w: pl.BlockSpec((tm, w), lambda i: (i, 0))
    acc = lambda w: pl.BlockSpec((1, w), lambda i: (0, 0))
    block = pl.BlockSpec((tm // 2, KV_W), lambda i: (i, 0))
    return pl.pallas_call(
        body, name="inproj_bwd", grid=(nt,),
        in_specs=[row(ATTN_W), block, block, block, block, row(C3_W), row(GATES_W), _resident((IN_W, D_MODEL)),
                  row(D_MODEL), row(D_MODEL), row(D_MODEL), _resident((1, D_MODEL))],
        out_specs=[row(D_MODEL), _resident((IN_W, D_MODEL)), acc(IN_W), acc(D_MODEL)],
        out_shape=[jax.ShapeDtypeStruct((s, D_MODEL), F32), jax.ShapeDtypeStruct((IN_W, D_MODEL), BF16),
                   jax.ShapeDtypeStruct((1, IN_W), F32), jax.ShapeDtypeStruct((1, D_MODEL), F32)],
        scratch_shapes=[pltpu.VMEM((IN_W, D_MODEL), F32)],
        compiler_params=_params("arbitrary"),
    )(dq, *dk, *dv, dc3, dgt, w_in, x, xn, dh1, g1)


def _wgrad(a, b, bm, bn, bk, name, comm=None):
    s, m = a.shape
    n = b.shape[1]
    nk = s // bk

    def body(a_ref, b_ref, o_ref, acc_ref):
        k = pl.program_id(2)

        @pl.when(k == 0)
        def _():
            acc_ref[...] = jnp.zeros_like(acc_ref)

        acc_ref[...] += lax.dot_general(a_ref[...].astype(BF16), b_ref[...].astype(BF16), TN,
                                        preferred_element_type=F32)

        @pl.when(k == nk - 1)
        def _():
            o_ref[...] = acc_ref[...].astype(BF16)

    return _call(
        comm, body, name=name, grid=(m // bm, n // bn, nk),
        in_specs=[pl.BlockSpec((bk, bm), lambda i, j, k: (k, i)), pl.BlockSpec((bk, bn), lambda i, j, k: (k, j))],
        out_specs=pl.BlockSpec((bm, bn), lambda i, j, k: (i, j)),
        out_shape=jax.ShapeDtypeStruct((m, n), BF16),
        scratch_shapes=[pltpu.VMEM((bm, bn), F32)],
        compiler_params=_params("parallel", "parallel", "arbitrary"),
    )(a, b)


class _Carry:
    def __init__(self, jobs, reads=None, bufs=None, fresh=None):
        self.jobs, self.reads, self.bufs, self.fresh = jobs, reads or {}, bufs or {}, fresh or {}
        self.out = {}


class _Job:
    def __init__(self, n_sems, plan):
        self.n_sems, self.plan = n_sems, plan


def _plan_all(jobs, hbm, send, recv):
    pos = _position()
    starts, waits, base = [], [], 0
    for job in jobs:
        s, w = job.plan(hbm, pos, send, recv, base)
        starts, waits, base = starts + s, waits + w, base + job.n_sems
    return starts, waits


def _call(comm, body, **kw):
    if comm is None:
        return pl.pallas_call(body, **kw)
    grid = kw["grid"]
    single = not isinstance(kw["out_shape"], (list, tuple))
    out_shape = [kw["out_shape"]] if single else list(kw["out_shape"])
    out_specs = [kw["out_specs"]] if single else list(kw["out_specs"])
    in_specs = list(kw["in_specs"])
    scratch = list(kw.get("scratch_shapes", ()))
    r_names, b_names, f_names = list(comm.reads), list(comm.bufs), list(comm.fresh)
    n_args, n_out, n_scr = len(in_specs), len(out_shape), len(scratch)
    n_sems = sum(j.n_sems for j in comm.jobs)

    def wrapped(*refs):
        k = n_args
        hbm = dict(zip(r_names, refs[k:k + len(r_names)]))
        k += len(r_names) + len(b_names)
        outs = refs[k:k + n_out]
        k += n_out
        hbm.update(zip(b_names + f_names, refs[k:k + len(b_names) + len(f_names)]))
        k += len(b_names) + len(f_names)
        send, recv = refs[k + n_scr:]
        starts, waits = _plan_all(comm.jobs, hbm, send, recv)
        ids = [pl.program_id(a) for a in range(len(grid))]
        first = functools.reduce(jnp.logical_and, [i == 0 for i in ids])
        last = functools.reduce(jnp.logical_and, [i == g - 1 for i, g in zip(ids, grid)])

        @pl.when(first)
        def _():
            for cp in starts:
                cp.start()

        body(*refs[:n_args], *outs, *refs[k:k + n_scr])

        @pl.when(last)
        def _():
            for cp in waits:
                cp.wait_recv()
            for cp in starts:
                cp.wait_send()

    sems = pltpu.SemaphoreType.DMA((n_sems,))
    held = [jax.ShapeDtypeStruct(a.shape, a.dtype) for a in comm.bufs.values()] + list(comm.fresh.values())
    call = pl.pallas_call(
        wrapped, name=kw["name"], grid=grid,
        in_specs=in_specs + [_ANY] * (len(r_names) + len(b_names)),
        out_specs=out_specs + [_ANY] * len(held),
        out_shape=out_shape + held,
        input_output_aliases={n_args + len(r_names) + i: n_out + i for i in range(len(b_names))},
        scratch_shapes=scratch + [sems, sems],
        compiler_params=_params(*["arbitrary"] * len(grid)),
    )

    def run(*args):
        res = call(*args, *comm.reads.values(), *comm.bufs.values())
        comm.out = dict(zip(b_names + f_names, res[n_out:]))
        return res[0] if single else res[:n_out]

    return run


def _exchange(name, phases, reads=None, bufs=None, fresh=None):
    comm = _Carry([j for ph in phases for j in ph], reads, bufs, fresh)
    r_names, b_names, f_names = list(comm.reads), list(comm.bufs), list(comm.fresh)
    n_sems = sum(j.n_sems for j in comm.jobs)

    def body(*refs):
        hbm = dict(zip(r_names, refs[:len(r_names)]))
        k = len(r_names) + len(b_names)
        hbm.update(zip(b_names + f_names, refs[k:k + len(b_names) + len(f_names)]))
        send, recv = refs[-2:]
        pos = _position()
        started, base = [], 0
        for ph in phases:
            waits = []
            for job in ph:
                s, w = job.plan(hbm, pos, send, recv, base)
                base += job.n_sems
                for cp in s:
                    cp.start()
                started, waits = started + s, waits + w
            for cp in waits:
                cp.wait_recv()
        for cp in started:
            cp.wait_send()

    sems = pltpu.SemaphoreType.DMA((n_sems,))
    held = [jax.ShapeDtypeStruct(a.shape, a.dtype) for a in comm.bufs.values()] + list(comm.fresh.values())
    res = pl.pallas_call(
        body, name=name, in_specs=[_ANY] * (len(r_names) + len(b_names)), out_specs=[_ANY] * len(held),
        out_shape=held, input_output_aliases={len(r_names) + i: i for i in range(len(b_names))},
        scratch_shapes=[sems, sems],
    )(*comm.reads.values(), *comm.bufs.values())
    return dict(zip(b_names + f_names, res))


_HBM = pl.BlockSpec(memory_space=pltpu.HBM)
_SEM = pl.BlockSpec(memory_space=pltpu.SEMAPHORE)
_EFFECT = pltpu.SideEffectType.DATAFLOW_SIDE_EFFECTING


def _start_exchanges(name, groups):
    names = [list(arrays) for _, arrays in groups]
    first = [sum(len(ns) for ns in names[:g]) for g in range(len(groups))]
    n, ng = sum(len(ns) for ns in names), len(groups)

    def body(*refs):
        for g, (jobs, _) in enumerate(groups):
            hbm = dict(zip(names[g], refs[first[g]:first[g] + len(names[g])]))
            for cp in _plan_all(jobs, hbm, refs[n + 2 * g], refs[n + 2 * g + 1])[0]:
                cp.start()
        refs[-1][...] = jnp.zeros_like(refs[-1])

    given = [pltpu.with_memory_space_constraint(
        a if isinstance(a, jax.Array) else lax.empty(a.shape, a.dtype), pltpu.HBM)
        for _, arrays in groups for a in arrays.values()]
    sems = [pltpu.SemaphoreType.DMA((sum(j.n_sems for j in jobs),)) for jobs, _ in groups for _ in range(2)]
    res = pl.pallas_call(
        body, name=name,
        out_shape=(*sems, *[pltpu.HBM(a.shape, a.dtype) for a in given], jax.ShapeDtypeStruct((8, 128), F32)),
        in_specs=[_HBM] * n, out_specs=(*[_SEM] * (2 * ng), *[_HBM] * n, pl.BlockSpec(memory_space=pltpu.VMEM)),
        input_output_aliases={i: 2 * ng + i for i in range(n)},
        compiler_params=pltpu.CompilerParams(has_side_effects=_EFFECT),
    )(*given)
    held = res[2 * ng:2 * ng + n]
    states = [(names[g], groups[g][0], res[2 * g], res[2 * g + 1], held[first[g]:first[g] + len(names[g])])
              for g in range(ng)]
    return states, res[-1]


def _start_exchange(name, jobs, arrays):
    states, token = _start_exchanges(name, [(jobs, arrays)])
    return states[0], token


def _finish_exchange(name, state, after):
    names, jobs, send_sem, recv_sem, held = state
    n = len(names)

    def body(*refs):
        hbm = dict(zip(names, refs[:n]))
        send, recv = refs[n:n + 2]
        starts, waits = _plan_all(jobs, hbm, send, recv)
        for cp in waits:
            cp.wait_recv()
        for cp in starts:
            cp.wait_send()

    res = pl.pallas_call(
        body, name=name, out_shape=tuple(pltpu.HBM(a.shape, a.dtype) for a in held),
        in_specs=[_HBM] * n + [_SEM, _SEM, _ANY], out_specs=tuple([_HBM] * n),
        input_output_aliases={i: i for i in range(n)},
        compiler_params=pltpu.CompilerParams(has_side_effects=_EFFECT),
    )(*held, send_sem, recv_sem, after)
    return dict(zip(names, res))


def _row_tile(rows, bytes_per_row):
    best = 16
    for t in range(16, rows + 1, 16):
        if rows % t == 0 and t * bytes_per_row <= 9 * 1024 * 1024:
            best = t
    return best


def _rowwise(fn, ins, out_dtypes, name, after=None):
    rows, cols = ins[0].shape
    per_row = sum(cols * a.dtype.itemsize for a in ins) + sum(cols * jnp.dtype(d).itemsize for d in out_dtypes)
    tr = _row_tile(rows, per_row)
    n_in = len(ins)

    def body(*refs):
        outs = fn(*[r[...] for r in refs[:n_in]])
        for o_ref, o in zip(refs[-len(out_dtypes):], outs):
            o_ref[...] = o.astype(o_ref.dtype)

    tile = pl.BlockSpec((tr, cols), lambda i: (i, 0))
    behind = [] if after is None else [after]
    return pl.pallas_call(
        body, name=name, grid=(rows // tr,),
        in_specs=[tile] * n_in + [pl.BlockSpec((8, 128), lambda i: (0, 0))] * len(behind),
        out_specs=[tile] * len(out_dtypes),
        out_shape=[jax.ShapeDtypeStruct((rows, cols), d) for d in out_dtypes],
        compiler_params=_params("parallel"),
    )(*ins, *behind)


def _tiled(fn, name, grid, pos, ins, outs):
    n_in = len(ins)

    def body(pos_ref, *refs):
        res = fn(*[r[...] for r in refs[:n_in]])
        for o_ref, o in zip(refs[n_in:], res):
            o_ref[...] = o.astype(o_ref.dtype)

    return pl.pallas_call(
        body, name=name,
        grid_spec=pltpu.PrefetchScalarGridSpec(
            num_scalar_prefetch=1, grid=grid,
            in_specs=[pl.BlockSpec(bs, im) for _, bs, im in ins],
            out_specs=[pl.BlockSpec(bs, im) for _, _, bs, im in outs]),
        out_shape=[jax.ShapeDtypeStruct(s, d) for s, d, _, _ in outs],
        compiler_params=_params("parallel"),
    )(pos, *[a for a, _, _ in ins])


def _adamw(w, g, m, v):
    m = ADAM_B1 * m + (1.0 - ADAM_B1) * g
    v = ADAM_B2 * v + (1.0 - ADAM_B2) * (g * g)
    m_hat = m / (1.0 - ADAM_B1 ** ADAM_STEP)
    v_hat = v / (1.0 - ADAM_B2 ** ADAM_STEP)
    return -ADAM_LR * (m_hat / (jnp.sqrt(v_hat) + ADAM_EPS) + ADAM_WD * w), m, v


def _adamw_small(params):
    n = len(params)

    def body(*refs):
        for k in range(n):
            w, g, m, v = (r[...] for r in refs[4 * k:4 * k + 4])
            for o_ref, o in zip(refs[4 * n + 3 * k:4 * n + 3 * k + 3], _adamw(w, g, m, v)):
                o_ref[...] = o

    flat = [a for p in params for a in p]
    return pl.pallas_call(
        body, name="adamw_small",
        out_shape=[jax.ShapeDtypeStruct(p[0].shape, F32) for p in params for _ in range(3)],
    )(*flat)


class _Layout:
    def __init__(self, rows, cols, stacked):
        self.rows, self.cols, self.stacked = rows, cols, stacked

    def whole(self, rows=None):
        r = self.rows if rows is None else rows
        return (N_CHIPS, r, self.cols) if self.stacked else (r, N_CHIPS * self.cols)

    def part_rows(self, h, q=0, nq=1):
        n = self.rows // 2 // nq
        return pl.ds(pl.multiple_of(h * (self.rows // 2) + q * n, 16), n)

    def half_rows(self, h):
        return self.part_rows(h)

    def block(self, ref, p, rows=slice(None)):
        if self.stacked:
            return ref.at[p, rows, :]
        return ref.at[rows, pl.ds(pl.multiple_of(p * self.cols, 128), self.cols)]

    def all_chips(self, ref, rows):
        return ref.at[:, rows, :] if self.stacked else ref.at[rows, :]


BIG = (
    _Layout(IN_SHARD, D_MODEL, True),
    _Layout(ATTN_W + CONV_W, D_MODEL // N_CHIPS, False),
    _Layout(D_MODEL // N_CHIPS, D_MODEL, True),
    _Layout(D_MODEL, FF2 // N_CHIPS, False),
    _Layout(D_FF // N_CHIPS, D_MODEL, True),
)
N_BIG = len(BIG)
_ANY = pl.BlockSpec(memory_space=pl.ANY)


def _position():
    x, y, c = lax.axis_index("x"), lax.axis_index("y"), lax.axis_index("c")
    return x, y, c, 2 * x + y


def _core_of_chip(p, c):
    return (p >> 1, p & 1, c)


def _place_cast(shard, lay, pos, name, after=None):
    rows, cols = shard.shape
    tr = _row_tile(rows, cols * 6)
    if lay.stacked:
        out = (lay.whole(), BF16, (None, tr, cols), lambda i, pos: (pos[0], i, 0))
    else:
        out = (lay.whole(), BF16, (tr, cols), lambda i, pos: (i, pos[0]))
    ins = [(shard, (tr, cols), lambda i, pos: (i, 0))]
    if after is not None:
        ins.append((after, (8, 128), lambda i, pos: (0, 0)))
    return _tiled(lambda a, *_: (a,), name, (rows // tr,), pos, ins, [out])[0]


def _place_cast_pair(top, bottom, lay, pos, name, after=None):
    rows, cols = top.shape
    ins = [(top, (rows, cols), lambda i, pos: (0, 0)), (bottom, (rows, cols), lambda i, pos: (0, 0))]
    if after is not None:
        ins.append((after, (8, 128), lambda i, pos: (0, 0)))
    return _tiled(lambda a, b, *_: (jnp.concatenate([a, b], axis=0),), name, (1,), pos, ins,
                  [(lay.whole(), BF16, (2 * rows, cols), lambda i, pos: (0, pos[0]))])[0]


def _adamw_pair(top, bottom, g, after=None):
    rows = top[0].shape[0]

    def body(*refs):
        (wa, ma, va, wb, mb, vb, g_ref), outs = refs[:7], refs[-8:]
        for (w, m, v), gg, o in (((wa, ma, va), g_ref[:rows], outs[:4]), ((wb, mb, vb), g_ref[rows:], outs[4:])):
            for o_ref, val in zip(o, (gg, *_adamw(w[...], gg, m[...], v[...]))):
                o_ref[...] = val

    behind = [] if after is None else [after[0:8, 0:128]]
    res = pl.pallas_call(
        body, name="adamw_w_br", out_shape=[jax.ShapeDtypeStruct(top[0].shape, F32)] * 8,
    )(*top, *bottom, g, *behind)
    return res[:4], res[4:]


def _remote(src, dst, send, recv, k, device):
    return pltpu.make_async_remote_copy(src_ref=src, dst_ref=dst, send_sem=send.at[k], recv_sem=recv.at[k],
                                        device_id=device, device_id_type=MESH)


def _arrival(dst, send, recv, k, me):
    return _remote(dst, dst, send, recv, k, me)


def _gather_ici(lay, name, q=0, nq=1):
    def plan(hbm, pos, send, recv, base):
        x, y, c, me = pos
        rows = lay.part_rows(c, q, nq)
        mine = lay.block(hbm[name], me, rows)
        starts = [_remote(mine, mine, send, recv, base + d - 1, _core_of_chip(me ^ d, c)) for d in (1, 2, 3)]
        waits = [_arrival(lay.block(hbm[name], me ^ d, rows), send, recv, base + d - 1, (x, y, c)) for d in (1, 2, 3)]
        return starts, waits
    return _Job(3, plan)


def _gather_d2d(lay, name, q=0, nq=1):
    def plan(hbm, pos, send, recv, base):
        x, y, c, me = pos
        starts, waits = [], []
        for d in (1, 2, 3):
            got = lay.block(hbm[name], me ^ d, lay.part_rows(c, q, nq))
            starts.append(_remote(got, got, send, recv, base + d - 1, (x, y, 1 - c)))
            waits.append(_arrival(lay.block(hbm[name], me ^ d, lay.part_rows(1 - c, q, nq)), send, recv, base + d - 1,
                                  (x, y, c)))
        return starts, waits
    return _Job(3, plan)


def _rs_pair(lay, grad, theirs):
    def plan(hbm, pos, send, recv, base):
        x, y, c, _ = pos
        out = _remote(lay.all_chips(hbm[grad], lay.half_rows(1 - c)), hbm[theirs], send, recv, base, (x, y, 1 - c))
        return [out], [_arrival(hbm[theirs], send, recv, base, (x, y, c))]
    return _Job(1, plan)


def _rs_chips(lay, sums, slots):
    def plan(hbm, pos, send, recv, base):
        x, y, c, me = pos
        starts = [_remote(lay.block(hbm[sums], me ^ d), hbm[slots].at[me], send, recv, base + d - 1,
                          _core_of_chip(me ^ d, c)) for d in (1, 2, 3)]
        waits = [_arrival(hbm[slots].at[me ^ d], send, recv, base + d - 1, (x, y, c)) for d in (1, 2, 3)]
        return starts, waits
    return _Job(3, plan)


def _rs_share(lay, *shards):
    def plan(hbm, pos, send, recv, base):
        x, y, c, _ = pos
        starts, waits = [], []
        for k, shard in enumerate(shards):
            mine = hbm[shard].at[lay.half_rows(c), :]
            other = hbm[shard].at[lay.half_rows(1 - c), :]
            starts.append(_remote(mine, mine, send, recv, base + k, (x, y, 1 - c)))
            waits.append(_arrival(other, send, recv, base + k, (x, y, c)))
        return starts, waits
    return _Job(len(shards), plan)


def _adamw_half(w, g, m, v, lay, pos, name, after=None):
    half = lay.rows // 2
    tr = _row_tile(half, lay.cols * 4 * 7)
    nt = half // tr
    blk = (tr, lay.cols)
    mine = lambda i, pos: (pos[1] * nt + i, 0)
    ins = [(a, blk, mine) for a in (w, g, m, v)]
    if after is not None:
        ins.append((after, (8, 128), lambda i, pos: (0, 0)))
    return _tiled(lambda w, g, m, v, *_: _adamw(w, g, m, v), name, (nt,), pos, ins,
                  [((lay.rows, lay.cols), F32, blk, mine)] * 3)


def _slots_shape(lay):
    return jax.ShapeDtypeStruct((N_CHIPS, lay.rows // 2, lay.cols), BF16)


def _theirs_shape(lay, dtype=BF16):
    return jax.ShapeDtypeStruct(lay.whole(lay.rows // 2), dtype)


def _pair_sum(grad, theirs, lay, pos, name):
    half = lay.rows // 2
    add = lambda a, b: (a.astype(F32) + b.astype(F32),)
    if lay.stacked:
        tr = _row_tile(half, lay.cols * 6)
        nt = half // tr
        flat = lambda a: a.reshape(-1, lay.cols)
        mine = lambda t, pos: ((t // nt) * (2 * nt) + pos[1] * nt + t % nt, 0)
        grid, blk = (N_CHIPS * nt,), (tr, lay.cols)
        grad, theirs = flat(grad), flat(theirs)
    else:
        tr = _row_tile(half, N_CHIPS * lay.cols * 6)
        nt = half // tr
        mine = lambda t, pos: (pos[1] * nt + t, 0)
        grid, blk = (nt,), (tr, N_CHIPS * lay.cols)
    same = lambda t, pos: (t, 0)
    out = _tiled(add, name, grid, pos, [(grad, blk, mine), (theirs, blk, same)], [(theirs.shape, BF16, blk, same)])[0]
    return out.reshape(lay.whole(half))


def _chip_sum(sums, slots, lay, pos, name, after=None):
    half = lay.rows // 2
    tr = _row_tile(half, lay.cols * 12)
    nt = half // tr
    blk3 = (None, tr, lay.cols)
    if lay.stacked:
        own = (sums, blk3, lambda i, pos: (pos[0], i, 0))
    else:
        own = (sums, (tr, lay.cols), lambda i, pos: (i, pos[0]))
    others = [(slots, blk3, functools.partial(lambda d, i, pos: (pos[0] ^ d, i, 0), d)) for d in (1, 2, 3)]

    def add(a, b1, b2, b3, *_):
        return (((a.astype(F32) + b1.astype(F32)) + b2.astype(F32)) + b3.astype(F32),)

    if after is not None:
        others.append((after, (8, 128), lambda i, pos: (0, 0)))
    return _tiled(add, name, (nt,), pos, [own] + others,
                  [((lay.rows, lay.cols), F32, (tr, lay.cols), lambda i, pos: (pos[1] * nt + i, 0))])[0]


N_DEV = 8


def _to_all(src, slots):
    def plan(hbm, pos, send, recv, base):
        x, y, c, _ = pos
        idx = 4 * x + 2 * y + c
        starts = [_remote(hbm[src], hbm[slots].at[idx], send, recv, base + k - 1,
                          (x ^ (k >> 2), y ^ ((k >> 1) & 1), c ^ (k & 1))) for k in range(1, N_DEV)]
        waits = [_arrival(hbm[slots].at[idx ^ k], send, recv, base + k - 1, (x, y, c)) for k in range(1, N_DEV)]
        return starts, waits
    return _Job(N_DEV - 1, plan)


def _sum_slots(own, slots, pos):
    def body(pos_ref, own_ref, slots_ref, o_ref):
        idx = 2 * pos_ref[0] + pos_ref[1]
        term = lambda q: jnp.where(idx == q, own_ref[...], slots_ref[q])
        acc = term(0)
        for q in range(1, N_DEV):
            acc = acc + term(q)
        o_ref[...] = acc

    return pl.pallas_call(
        body, name="sum_small", out_shape=jax.ShapeDtypeStruct(own.shape, F32),
        in_specs=[pl.BlockSpec(memory_space=pltpu.SMEM), pl.BlockSpec(memory_space=pltpu.VMEM),
                  pl.BlockSpec(memory_space=pltpu.VMEM)],
    )(pos, own, slots)


def _pack_rows(parts):
    padded = [jnp.pad(a, ((0, -a.shape[0] % 8), (0, 0))) for a in parts]
    starts = [sum(p.shape[0] for p in padded[:k]) for k in range(len(padded))]
    return jnp.concatenate(padded, axis=0), starts


def kernel(x, mix_norm, w_in, b_in, sinks, conv_w, w_attn_branch, w_conv_branch, w_out, ffn_norm, w_up, ffn_conv_w, w_down, final_norm, loss_target, m_mix_norm, m_w_in, m_b_in, m_sinks, m_conv_w, m_w_attn_branch, m_w_conv_branch, m_w_out, m_ffn_norm, m_w_up, m_ffn_conv_w, m_w_down, m_final_norm, v_mix_norm, v_w_in, v_b_in, v_sinks, v_conv_w, v_w_attn_branch, v_w_conv_branch, v_w_out, v_ffn_norm, v_w_up, v_ffn_conv_w, v_w_down, v_final_norm):
    me = 2 * lax.axis_index("x") + lax.axis_index("y")
    names = ("w_in", "w_br", "w_out", "w_up", "w_down")
    w_of = dict(w_in=w_in[0].T, w_out=w_out[0], w_up=w_up[0], w_down=w_down[0])
    m_of = dict(w_in=m_w_in[0].T, w_out=m_w_out[0], w_up=m_w_up[0], w_down=m_w_down[0])
    v_of = dict(w_in=v_w_in[0].T, w_out=v_w_out[0], w_up=v_w_up[0], w_down=v_w_down[0])
    ab = (w_attn_branch[0], m_w_attn_branch[0], v_w_attn_branch[0])
    cb = (w_conv_branch[0], m_w_conv_branch[0], v_w_conv_branch[0])

    pos = jnp.stack([me, lax.axis_index("c")]).astype(jnp.int32)

    lay = dict(zip(names, BIG))
    xs, target, sk = x[0], loss_target[0], sinks[0]
    s = xs.shape[0]
    tm, tm2, bk, bk2 = min(256, s), min(512, s), min(1024, s), min(2048, s)

    taps, (_, t0) = _pack_rows([conv_w[0], ffn_conv_w[0].reshape(3 * (FF2 // N_CHIPS // 128), 128)])
    placed = {"w_in": _place_cast(w_of["w_in"], lay["w_in"], pos, "cast_w_in")}
    fly_in, started = _start_exchange("gather_in_start", [_gather_ici(lay["w_in"], "w_in")], {"w_in": placed["w_in"]})
    taps_flight, started = _start_exchange("taps_start", [_to_all("v", "slots")],
                                           {"v": taps + started[0:1], "slots": jnp.zeros((N_DEV, *taps.shape), F32)})
    placed["w_br"] = _place_cast_pair(ab[0], cb[0], lay["w_br"], pos, "cast_w_br", after=started)
    for n in names[2:]:
        placed[n] = _place_cast(w_of[n], lay[n], pos, "cast_" + n, after=started)
    trio = ("w_br", "w_out")
    (fly_trio, fly_up, fly_down), started = _start_exchanges("gather_rest_start", [
        ([_gather_ici(lay[n], n) for n in ws], {n: placed[n] for n in ws}) for ws in (trio, ("w_up",), ("w_down",))])

    got = _finish_exchange("gather_in_wait", fly_in, after=started)
    w_in_full = _exchange("gather_in_d2d", [[_gather_d2d(lay["w_in"], "w_in")]], bufs=got)["w_in"].reshape(IN_W, D_MODEL)
    xn, qkv, c3, gates = _inproj_fwd(xs, mix_norm, w_in_full, b_in, tm2)
    k2 = _Carry([_gather_d2d(lay[n], n) for n in trio], bufs=_finish_exchange("gather_trio_wait", fly_trio, after=qkv))
    attn = _attn_fwd(qkv, sk, comm=k2)
    w_br = k2.out["w_br"]
    w_out_full = k2.out["w_out"].reshape(D_MODEL, D_MODEL)
    k3 = _Carry([_gather_d2d(lay["w_up"], "w_up")], bufs=_finish_exchange("gather_up_wait", fly_up, after=attn))
    taps = _finish_exchange("taps_wait", taps_flight, after=attn)
    taps = lax.dynamic_update_slice(taps["slots"], taps["v"][None], (2 * me + lax.axis_index("c"), 0, 0))
    conv_full = taps[0::2, 0:3].transpose(1, 0, 2).reshape(3, CONV_W)
    ffn_cw_full = taps[0::2, t0:t0 + 33].reshape(N_CHIPS, 3, FF2 // N_CHIPS).transpose(1, 0, 2).reshape(3, FF2)
    conv, a, cv, merged, h1, hn = _mix_fwd(xs, attn, c3, gates, conv_full, w_br, w_out_full, ffn_norm, tm2, comm=k3)
    w_up_full = k3.out["w_up"]
    w_down_full = _exchange("gather_down_d2d", [[_gather_d2d(lay["w_down"], "w_down")]],
                            bufs=_finish_exchange("gather_down_wait", fly_down, after=hn))["w_down"].reshape(D_FF, D_MODEL)
    u, up, act, dh2, loss_part, g_fn = _ffn_fwd_loss(hn, h1, w_up_full, ffn_cw_full, w_down_full,
                                                     final_norm[None, :], target, tm)

    grads, sums, slots = {}, {}, {}

    def pair(*ws):
        return _Carry([_rs_pair(lay[n], "g_" + n, "t_" + n) for n in ws], reads={"g_" + n: grads[n] for n in ws},
                      fresh={"t_" + n: _theirs_shape(lay[n], grads[n].dtype) for n in ws})

    def chips(*ws, also=None):
        k = _Carry([_rs_chips(lay[n], "s_" + n, "r_" + n) for n in ws], reads={"s_" + n: sums[n] for n in ws},
                   fresh={"r_" + n: _slots_shape(lay[n]) for n in ws})
        if also is not None:
            k = _Carry(k.jobs + also.jobs, {**k.reads, **also.reads}, None, {**k.fresh, **also.fresh})
        return k

    def pair_sums(k, *ws):
        for n in ws:
            sums[n] = _pair_sum(grads[n], k.out["t_" + n], lay[n], pos, "pair_sum_" + n)

    def take_slots(k, *ws):
        for n in ws:
            slots[n] = k.out["r_" + n]

    du, dh1, g_fcw, g_g2 = _ffn_bwd(dh2, u, up, h1, w_up_full, ffn_cw_full, w_down_full, ffn_norm, tm)
    grads["w_down"] = _wgrad(act, dh2, D_FF // 2, D_MODEL, bk2, "wgrad_down").reshape(lay["w_down"].whole())
    k4 = pair("w_down")
    grads["w_up"] = _wgrad(hn, du, D_MODEL, FF2 // 4, bk2, "wgrad_up", comm=k4)
    pair_sums(k4, "w_down")
    k5 = chips("w_down", also=pair("w_up"))
    dattn, dc3, dgt, g_cw, grads["w_br"], gw_out = _mix_bwd(
        dh1, gates, a, cv, c3, attn, conv, merged, conv_full, w_br, w_out_full, tm2, comm=k5)
    grads["w_out"] = gw_out.reshape(lay["w_out"].whole())
    take_slots(k5, "w_down")
    pair_sums(k5, "w_up")
    k6 = chips("w_up", also=pair(*trio))
    dq, dk_even, dk_odd, dv_even, dv_odd, g_sk = _attn_bwd(qkv, sk, attn, dattn, comm=k6)
    take_slots(k6, "w_up")
    pair_sums(k6, *trio)
    trio_flight, started = _start_exchange(
        "rs_chips_trio_start", [_rs_chips(lay[n], "s_" + n, "r_" + n) for n in trio],
        {**{"s_" + n: sums[n] for n in trio}, **{"r_" + n: _slots_shape(lay[n]) for n in trio}})
    behind = mix_norm + jnp.tile(started[0:1], (1, D_MODEL // 128))
    grad_x, gw_in, g_b, g_g1 = _inproj_bwd(dq, (dk_even, dk_odd), (dv_even, dv_odd), dc3, dgt, w_in_full, xs, xn,
                                           dh1, behind)
    grads["w_in"] = gw_in.reshape(lay["w_in"].whole())

    parts = [loss_part, g_g1, g_b, jnp.pad(g_sk[:, 0], (0, 120))[None, :], g_cw, g_g2, g_fcw, g_fn]
    packed, at = _pack_rows([p.reshape(-1, 128) for p in parts])
    small_flight, started = _start_exchange("small_start", [_to_all("v", "slots")],
                                            {"v": packed, "slots": jnp.zeros((N_DEV, *packed.shape), F32)})
    in_flight, started = _start_exchange("rs_pair_in_start", [_rs_pair(lay["w_in"], "g", "t")],
                                         {"g": grads["w_in"], "t": _theirs_shape(lay["w_in"]), "behind": started})
    halves = {n: _chip_sum(sums[n], slots[n], lay[n], pos, "chip_sum_" + n, after=started) for n in ("w_up", "w_down")}
    landed = _finish_exchange("rs_pair_in_wait", in_flight, after=halves["w_down"])
    sums["w_in"] = _pair_sum(landed["g"], landed["t"], lay["w_in"], pos, "pair_sum_w_in")
    in_flight, started = _start_exchange("rs_chips_in_start", [_rs_chips(lay["w_in"], "s", "r")],
                                         {"s": sums["w_in"], "r": _slots_shape(lay["w_in"])})
    landed = _finish_exchange("rs_chips_trio_wait", trio_flight, after=started)
    for n in trio:
        halves[n] = _chip_sum(landed["s_" + n], landed["r_" + n], lay[n], pos, "chip_sum_" + n)
    g_br = _exchange("share_br", [[_rs_share(lay["w_br"], "w_br")]], bufs={"w_br": halves["w_br"]})["w_br"]

    sharing, last = {}, g_br

    def update_half(n, after):
        keys = [n + k for k in ("_g", "_d", "_m", "_v")]
        new = _adamw_half(w_of[n], halves[n], m_of[n], v_of[n], lay[n], pos, "adamw_" + n, after=after)
        sharing[n], started = _start_exchange("share_" + n + "_start", [_rs_share(lay[n], *keys)],
                                              dict(zip(keys, (halves[n], *new))))
        return started

    for n in ("w_up", "w_down", "w_out"):
        last = update_half(n, last)
    new_of = {}
    new_of["w_ab"], new_of["w_cb"] = _adamw_pair(ab, cb, g_br, after=last)
    last = new_of["w_cb"][1]

    arrived = _finish_exchange("small_wait", small_flight, after=last)
    total = _sum_slots(arrived["v"], arrived["slots"], pos)
    part = lambda k: total[at[k]:at[k] + parts[k].size // 128].reshape(parts[k].shape)
    loss = total[0, 0]
    g_mix, g_b, g_g2, g_fn = part(1), part(2), part(5), part(7)
    g_sk = part(3)[:, 0:N_HEADS]
    g_cw = lax.dynamic_slice(part(4), (0, me * 128), (3, 128))
    g_fcw = lax.dynamic_slice(part(6), (0, me * (FF2 // N_CHIPS)), (3, FF2 // N_CHIPS))
    small_p = [
        (mix_norm, g_mix, m_mix_norm, v_mix_norm), (b_in, g_b, m_b_in, v_b_in), (sinks, g_sk, m_sinks, v_sinks),
        (conv_w[0], g_cw, m_conv_w[0], v_conv_w[0]), (ffn_norm, g_g2, m_ffn_norm, v_ffn_norm),
        (ffn_conv_w[0], g_fcw, m_ffn_conv_w[0], v_ffn_conv_w[0]),
        (final_norm[None, :], g_fn, m_final_norm[None, :], v_final_norm[None, :])]
    small_new = _adamw_small(small_p)
    small_new = [small_new[3 * k:3 * k + 3] for k in range(len(small_p))]

    landed = _finish_exchange("rs_chips_in_wait", in_flight, after=small_new[0][0])
    halves["w_in"] = _chip_sum(landed["s"], landed["r"], lay["w_in"], pos, "chip_sum_w_in")
    last = update_half("w_in", None)
    for n in ("w_up", "w_down", "w_out", "w_in"):
        got = _finish_exchange("share_" + n + "_wait", sharing[n], after=last)
        new_of[n] = [got[n + k] for k in ("_g", "_d", "_m", "_v")]
    new_of["w_in"] = [a.T for a in new_of["w_in"]]
    big = ("w_in", "w_ab", "w_cb", "w_out", "w_up", "w_down")
    big_g = [new_of[n][0] for n in big]
    big_new = [new_of[n][1:] for n in big]

    order = [("s", 0), ("b", 0), ("s", 1), ("s", 2), ("s", 3), ("b", 1), ("b", 2), ("b", 3), ("s", 4), ("b", 4),
             ("s", 5), ("b", 5), ("s", 6)]
    shapes = [mix_norm.shape, w_in.shape, b_in.shape, sinks.shape, conv_w.shape, w_attn_branch.shape,
              w_conv_branch.shape, w_out.shape, ffn_norm.shape, w_up.shape, ffn_conv_w.shape, w_down.shape,
              final_norm.shape]
    small_g = [p[1] for p in small_p]
    out_g = [(small_g[k] if kind == "s" else big_g[k]).reshape(shp) for (kind, k), shp in zip(order, shapes)]
    news = [[(small_new[k][j] if kind == "s" else big_new[k][j]).reshape(shp) for (kind, k), shp in zip(order, shapes)]
            for j in range(3)]
    return (loss, grad_x[None], *out_g, *news[0], *news[1], *news[2])
```

```python
import functools

import jax
import jax.numpy as jnp
from jax import lax
from jax.experimental import pallas as pl
from jax.experimental.pallas import tpu as pltpu

F32 = jnp.float32
BF16 = jnp.bfloat16

D_MODEL = 1024
HEAD_DIM = 64
N_HEADS = 8
N_KV_HEADS = 2
GROUP = N_HEADS // N_KV_HEADS
BLOCK = 128
ATTN_SCALE = HEAD_DIM ** -0.5
ATTN_W = N_HEADS * HEAD_DIM
KV_W = N_KV_HEADS * HEAD_DIM
CONV_W = 512
QKV_W = ATTN_W + 2 * KV_W
C3_W = 3 * CONV_W
GATES_W = 2 * D_MODEL
IN_W = QKV_W + C3_W + GATES_W
D_FF = 2816
FF2 = 2 * D_FF
NORM_EPS = 1e-5
N_CHIPS = 4
IN_SHARD = IN_W // N_CHIPS
NEG = -1e30

ADAM_LR = 0.001
ADAM_B1 = 0.9
ADAM_B2 = 0.999
ADAM_EPS = 1e-08
ADAM_WD = 0.01
ADAM_STEP = 10

VMEM_LIMIT = 56 * 1024 * 1024
MESH = pl.DeviceIdType.MESH

NT = (((1,), (1,)), ((), ()))
TN = (((0,), (0,)), ((), ()))


def _params(*sem):
    return pltpu.CompilerParams(dimension_semantics=sem, vmem_limit_bytes=VMEM_LIMIT)


def _resident(shape):
    return pl.BlockSpec(shape, lambda *_: (0,) * len(shape), pipeline_mode=pl.Buffered(1))


def _sigmoid(v):
    return 0.5 * jnp.tanh(0.5 * v) + 0.5


def _rstd(v):
    return lax.rsqrt(jnp.mean(v * v, axis=-1, keepdims=True) + NORM_EPS)


def _rms_bwd(dy, v, rstd, g):
    vhat = v * rstd
    t = dy * g
    return rstd * (t - vhat * jnp.mean(t * vhat, axis=-1, keepdims=True)), dy * vhat


def _taps(z, cw):
    return cw[2:3] * z + cw[1:2] * pltpu.roll(z, 1, 0) + cw[0:1] * pltpu.roll(z, 2, 0)


def _causal_conv(z, prev, cw):
    edge = _taps(jnp.concatenate([prev, z[0:8]], axis=0), cw)
    return jnp.concatenate([edge[8:16], _taps(z, cw)[8:]], axis=0)


def _rows_after(z, nxt):
    n = z.shape[0]
    edge = jnp.concatenate([z[n - 8:n], nxt], axis=0)
    return tuple(jnp.concatenate([pltpu.roll(z, n - k, 0)[:n - 8], pltpu.roll(edge, 16 - k, 0)[0:8]], axis=0)
                 for k in (1, 2))


def _inproj_fwd(x, g1, w_in, b_in, tm, comm=None):
    s = x.shape[0]

    def body(x_ref, g_ref, w_ref, b_ref, xn_ref, qkv_ref, c3_ref, gt_ref):
        xf = x_ref[...]
        xn = (xf * _rstd(xf) * g_ref[...]).astype(BF16)
        xn_ref[...] = xn

        proj = (lax.dot_general(xn, w_ref[...], NT, preferred_element_type=F32) + b_ref[...]).astype(BF16)
        qkv_ref[...] = proj[:, :QKV_W]
        c3_ref[...] = proj[:, QKV_W:QKV_W + C3_W]
        gt_ref[...] = proj[:, QKV_W + C3_W:]

    row = lambda w: pl.BlockSpec((tm, w), lambda i: (i, 0))
    return _call(
        comm, body, name="inproj_fwd", grid=(s // tm,),
        in_specs=[row(D_MODEL), _resident((1, D_MODEL)), _resident((IN_W, D_MODEL)), _resident((1, IN_W))],
        out_specs=[row(D_MODEL), row(QKV_W), row(C3_W), row(GATES_W)],
        out_shape=[jax.ShapeDtypeStruct((s, D_MODEL), BF16), jax.ShapeDtypeStruct((s, QKV_W), BF16),
                   jax.ShapeDtypeStruct((s, C3_W), BF16), jax.ShapeDtypeStruct((s, GATES_W), BF16)],
        compiler_params=_params("parallel"),
    )(x, g1, w_in, b_in)


def _attn_bias():
    kj = jnp.arange(2 * BLOCK)[:, None]
    qi = (jnp.arange(GROUP * BLOCK) % BLOCK)[None, :]
    band = (kj > qi) & (kj <= qi + BLOCK)
    return jnp.stack([jnp.where(band & (kj >= BLOCK), 0.0, NEG), jnp.where(band, 0.0, NEG)]).astype(F32)


def _attn_bias_specs():
    shape = (None, 2 * BLOCK, GROUP * BLOCK)
    return pl.BlockSpec(shape, lambda i: (jnp.minimum(i, 1), 0, 0)), pl.BlockSpec(shape, lambda i: (1, 0, 0))


def _sink_row(sk_ref, h):
    lane = lax.broadcasted_iota(jnp.int32, (1, GROUP * BLOCK), 1)
    row = jnp.full((1, GROUP * BLOCK), sk_ref[h * GROUP], F32)
    for g in range(1, GROUP):
        row = jnp.where(lane >= g * BLOCK, sk_ref[h * GROUP + g], row)
    return row


def _stack_heads(t, h):
    return jnp.concatenate(
        [t[:, (h * GROUP + g) * HEAD_DIM:(h * GROUP + g + 1) * HEAD_DIM] for g in range(GROUP)], axis=0)


def _unstack_heads(per_kv):
    return jnp.concatenate(
        [t[g * BLOCK:(g + 1) * BLOCK] for t in per_kv for g in range(GROUP)], axis=1)


def _pair_specs(npair):
    cur = lambda i: jnp.minimum(i, npair - 1)
    prev = lambda i: jnp.maximum(2 * jnp.minimum(i, npair - 1) - 1, 0)
    kv = ATTN_W // KV_W
    return (pl.BlockSpec((2 * BLOCK, ATTN_W), lambda i: (cur(i), 0)),
            pl.BlockSpec((BLOCK, KV_W), lambda i: (prev(i), kv)), pl.BlockSpec((2 * BLOCK, KV_W), lambda i: (cur(i), kv)),
            pl.BlockSpec((BLOCK, KV_W), lambda i: (prev(i), kv + 1)),
            pl.BlockSpec((2 * BLOCK, KV_W), lambda i: (cur(i), kv + 1)))


def _attn_fwd(qkv, sinks, comm=None):
    s = qkv.shape[0]
    npair = s // (2 * BLOCK)

    def body(sk_ref, bias0_ref, bias1_ref, q_ref, kp_ref, kc_ref, vp_ref, vc_ref, o_ref):
        kc, vc = kc_ref[...], vc_ref[...]
        for b, (bias_ref, kp, vp) in enumerate(((bias0_ref, kp_ref[...], vp_ref[...]),
                                                (bias1_ref, kc[:BLOCK], vc[:BLOCK]))):
            rows = slice(b * BLOCK, (b + 1) * BLOCK)
            q, bias = q_ref[rows, :], bias_ref[...]
            outs = []
            for h in range(N_KV_HEADS):
                hs = slice(h * HEAD_DIM, (h + 1) * HEAD_DIM)
                k2 = jnp.concatenate([kp[:, hs], kc[rows, hs]], axis=0)
                v2 = jnp.concatenate([vp[:, hs], vc[rows, hs]], axis=0)
                sc = lax.dot_general(k2, _stack_heads(q, h), NT, preferred_element_type=F32) * ATTN_SCALE + bias
                sink = _sink_row(sk_ref, h)
                m = jnp.maximum(jnp.max(sc, axis=0, keepdims=True), sink)
                p = jnp.exp(sc - m)
                den = jnp.sum(p, axis=0, keepdims=True) + jnp.exp(sink - m)
                out = lax.dot_general(v2, p.astype(BF16), TN, preferred_element_type=F32) / den
                outs.append(out.T)
            o_ref[rows, :] = _unstack_heads(outs).astype(BF16)

    return _call(
        comm, body, name="attn_fwd", grid=(npair,),
        in_specs=[pl.BlockSpec(memory_space=pltpu.SMEM), *_attn_bias_specs(), *_pair_specs(npair)],
        out_specs=pl.BlockSpec((2 * BLOCK, ATTN_W), lambda i: (i, 0)),
        out_shape=jax.ShapeDtypeStruct((s, ATTN_W), BF16),
        compiler_params=_params("parallel"),
    )(sinks, _attn_bias(), _attn_bias(), qkv, qkv, qkv, qkv, qkv)


def _mix_fwd(x, attn, c3, gates, conv_w, w_br, w_out, g2, tm, comm=None):
    s = x.shape[0]

    def body(x_ref, at_ref, c3_ref, gt_ref, cw_ref, wbr_ref, wo_ref, g_ref,
             conv_ref, a_ref, cv_ref, mg_ref, h1_ref, hn_ref, carry_ref):
        @pl.when(pl.program_id(0) == 0)
        def _():
            carry_ref[...] = jnp.zeros_like(carry_ref)

        c3v = c3_ref[...].astype(F32)
        cb, cc, cx = c3v[:, :CONV_W], c3v[:, CONV_W:2 * CONV_W], c3v[:, 2 * CONV_W:]
        z = cc * cx
        cz = _causal_conv(z, carry_ref[...], cw_ref[...])
        carry_ref[...] = z[tm - 8:tm]
        conv = (cb * cz).astype(BF16)
        conv_ref[...] = conv
        a = jnp.dot(at_ref[...], wbr_ref[:ATTN_W, :], preferred_element_type=F32)
        cv = jnp.dot(conv, wbr_ref[ATTN_W:, :], preferred_element_type=F32)
        a_ref[...] = a.astype(BF16)
        cv_ref[...] = cv.astype(BF16)
        gt = gt_ref[...].astype(F32)
        merged = (_sigmoid(gt[:, :D_MODEL]) * a + _sigmoid(gt[:, D_MODEL:]) * cv).astype(BF16)
        mg_ref[...] = merged
        h1 = x_ref[...] + jnp.dot(merged, wo_ref[...], preferred_element_type=F32)
        h1_ref[...] = h1
        hn_ref[...] = (h1 * _rstd(h1) * g_ref[...]).astype(BF16)

    row = lambda w: pl.BlockSpec((tm, w), lambda i: (i, 0))
    return _call(
        comm, body, name="mix_fwd", grid=(s // tm,),
        in_specs=[row(D_MODEL), row(ATTN_W), row(C3_W), row(GATES_W), _resident((3, CONV_W)),
                  _resident((ATTN_W + CONV_W, D_MODEL)), _resident((D_MODEL, D_MODEL)), _resident((1, D_MODEL))],
        out_specs=[row(CONV_W), row(D_MODEL), row(D_MODEL), row(D_MODEL), row(D_MODEL), row(D_MODEL)],
        out_shape=[jax.ShapeDtypeStruct((s, CONV_W), BF16), jax.ShapeDtypeStruct((s, D_MODEL), BF16),
                   jax.ShapeDtypeStruct((s, D_MODEL), BF16), jax.ShapeDtypeStruct((s, D_MODEL), BF16),
                   jax.ShapeDtypeStruct((s, D_MODEL), F32), jax.ShapeDtypeStruct((s, D_MODEL), BF16)],
        scratch_shapes=[pltpu.VMEM((8, CONV_W), F32)],
        compiler_params=_params("arbitrary"),
    )(x, attn, c3, gates, conv_w, w_br, w_out, g2)


def _ffn_fwd_loss(hn, h1, w_up, ffn_cw, w_down, g3, target, tm):
    s = hn.shape[0]

    def body(hn_ref, h1_ref, wu_ref, cw_ref, wd_ref, g_ref, t_ref,
             u_ref, up_ref, act_ref, dh2_ref, loss_ref, gfn_ref, carry_ref):
        @pl.when(pl.program_id(0) == 0)
        def _():
            carry_ref[...] = jnp.zeros_like(carry_ref)
            loss_ref[...] = jnp.zeros_like(loss_ref)
            gfn_ref[...] = jnp.zeros_like(gfn_ref)

        u = jnp.dot(hn_ref[...], wu_ref[...], preferred_element_type=F32)
        u_ref[...] = u.astype(BF16)
        up = _causal_conv(u, carry_ref[...], cw_ref[...])
        up_ref[...] = up
        carry_ref[...] = u[tm - 8:tm]
        gate, val = up[:, :D_FF], up[:, D_FF:]
        act = (gate * _sigmoid(gate) * val).astype(BF16)
        act_ref[...] = act
        h2 = h1_ref[...] + jnp.dot(act, wd_ref[...], preferred_element_type=F32)
        rstd = _rstd(h2)
        g = g_ref[...]
        err = h2 * rstd * g - t_ref[...]
        loss_ref[...] += jnp.sum(err * err) * (0.5 / D_MODEL)
        dh2, dg = _rms_bwd(err * (1.0 / D_MODEL), h2, rstd, g)
        dh2_ref[...] = dh2
        gfn_ref[...] += jnp.sum(dg, axis=0, keepdims=True)

    row = lambda w: pl.BlockSpec((tm, w), lambda i: (i, 0))
    acc = lambda w: pl.BlockSpec((1, w), lambda i: (0, 0))
    return pl.pallas_call(
        body, name="ffn_fwd_loss", grid=(s // tm,),
        in_specs=[row(D_MODEL), row(D_MODEL), _resident((D_MODEL, FF2)), _resident((3, FF2)),
                  _resident((D_FF, D_MODEL)), _resident((1, D_MODEL)), row(D_MODEL)],
        out_specs=[row(FF2), row(FF2), row(D_FF), row(D_MODEL), acc(128), acc(D_MODEL)],
        out_shape=[jax.ShapeDtypeStruct((s, FF2), BF16), jax.ShapeDtypeStruct((s, FF2), F32),
                   jax.ShapeDtypeStruct((s, D_FF), BF16),
                   jax.ShapeDtypeStruct((s, D_MODEL), F32), jax.ShapeDtypeStruct((1, 128), F32),
                   jax.ShapeDtypeStruct((1, D_MODEL), F32)],
        scratch_shapes=[pltpu.VMEM((8, FF2), F32)],
        compiler_params=_params("arbitrary"),
    )(hn, h1, w_up, ffn_cw, w_down, g3, target)


def _ffn_bwd(dh2, u, up, h1, w_up, ffn_cw, w_down, g2, tm):
    s = dh2.shape[0]
    nt = s // tm

    def body(dh2_ref, u_ref, up_ref, h1_ref, wu_ref, cw_ref, wd_ref, g_ref,
             du_ref, dh1_ref, gcw_ref, gg_ref, carry_ref):
        @pl.when(pl.program_id(0) == 0)
        def _():
            for ref in (carry_ref, gcw_ref, gg_ref):
                ref[...] = jnp.zeros_like(ref)

        dh2v = dh2_ref[...]
        dact = lax.dot_general(dh2v.astype(BF16), wd_ref[...], NT, preferred_element_type=F32)
        upv = up_ref[...]
        gate, val = upv[:, :D_FF], upv[:, D_FF:]
        sg = _sigmoid(gate)
        dval = dact * (gate * sg)
        dgate = dact * val * (sg * (1.0 + gate * (1.0 - sg)))
        dup = jnp.concatenate([dgate, dval], axis=1)
        dup1, dup2 = _rows_after(dup, carry_ref[...])
        carry_ref[...] = dup[0:8]
        u = u_ref[...].astype(F32)
        gcw_ref[2:3, :] += jnp.sum(dup * u, axis=0, keepdims=True)
        gcw_ref[1:2, :] += jnp.sum(dup1 * u, axis=0, keepdims=True)
        gcw_ref[0:1, :] += jnp.sum(dup2 * u, axis=0, keepdims=True)
        cw = cw_ref[...]
        du = (cw[2:3] * dup + cw[1:2] * dup1 + cw[0:1] * dup2).astype(BF16)
        du_ref[...] = du
        dhn = lax.dot_general(du, wu_ref[...], NT, preferred_element_type=F32)
        h1v = h1_ref[...]
        dh1, dg = _rms_bwd(dhn, h1v, _rstd(h1v), g_ref[...])
        dh1_ref[...] = dh2v + dh1
        gg_ref[...] += jnp.sum(dg, axis=0, keepdims=True)

    row = lambda w: pl.BlockSpec((tm, w), lambda i: (nt - 1 - i, 0))
    return pl.pallas_call(
        body, name="ffn_bwd", grid=(nt,),
        in_specs=[row(D_MODEL), row(FF2), row(FF2),
                  row(D_MODEL), _resident((D_MODEL, FF2)), _resident((3, FF2)), _resident((D_FF, D_MODEL)),
                  _resident((1, D_MODEL))],
        out_specs=[row(FF2), row(D_MODEL), pl.BlockSpec((3, FF2), lambda i: (0, 0)),
                   pl.BlockSpec((1, D_MODEL), lambda i: (0, 0))],
        out_shape=[jax.ShapeDtypeStruct((s, FF2), BF16), jax.ShapeDtypeStruct((s, D_MODEL), F32),
                   jax.ShapeDtypeStruct((3, FF2), F32), jax.ShapeDtypeStruct((1, D_MODEL), F32)],
        scratch_shapes=[pltpu.VMEM((8, FF2), F32)],
        compiler_params=_params("arbitrary"),
    )(dh2, u, up, h1, w_up, ffn_cw, w_down, g2)


def _mix_bwd(dh1, gates, a, cv, c3, attn, conv, merged, conv_w, w_br, w_out, tm, comm=None):
    s = dh1.shape[0]
    nt = s // tm
    halo = 16

    def body(dh1_ref, gt_ref, a_ref, cv_ref, c3_ref, ch_ref, at_ref, cn_ref, mg_ref, cw_ref, wbr_ref,
             wo_ref, dat_ref, dc3_ref, dgt_ref, gcw_ref, gbr_ref, gout_ref, carry_ref, br_acc, out_acc):
        i = pl.program_id(0)

        @pl.when(i == 0)
        def _():
            for ref in (carry_ref, gcw_ref, br_acc, out_acc):
                ref[...] = jnp.zeros_like(ref)

        dh1v = dh1_ref[...].astype(BF16)
        out_acc[...] += lax.dot_general(mg_ref[...], dh1v, TN, preferred_element_type=F32)
        dm = lax.dot_general(dh1v, wo_ref[...], NT, preferred_element_type=F32)
        gt = gt_ref[...].astype(F32)
        sa, sc = _sigmoid(gt[:, :D_MODEL]), _sigmoid(gt[:, D_MODEL:])
        da = (dm * sa).astype(BF16)
        dcv = (dm * sc).astype(BF16)
        br_acc[:ATTN_W, :] += lax.dot_general(at_ref[...], da, TN, preferred_element_type=F32)
        br_acc[ATTN_W:, :] += lax.dot_general(cn_ref[...], dcv, TN, preferred_element_type=F32)
        dgt_ref[...] = jnp.concatenate(
            [dm * a_ref[...].astype(F32) * (sa * (1.0 - sa)), dm * cv_ref[...].astype(F32) * (sc * (1.0 - sc))],
            axis=1).astype(BF16)
        dat_ref[...] = lax.dot_general(da, wbr_ref[:ATTN_W, :], NT, preferred_element_type=F32).astype(BF16)
        dconv = lax.dot_general(dcv, wbr_ref[ATTN_W:, :], NT, preferred_element_type=F32)
        c3v = c3_ref[...].astype(F32)
        cb, cc, cx = c3v[:, :CONV_W], c3v[:, CONV_W:2 * CONV_W], c3v[:, 2 * CONV_W:]
        z = cc * cx
        chv = ch_ref[...].astype(F32)[halo - 8:halo] * (i < nt - 1).astype(F32)
        zh = chv[:, CONV_W:2 * CONV_W] * chv[:, 2 * CONV_W:]
        cw = cw_ref[...]
        cz = _causal_conv(z, zh, cw)
        dcz = dconv * cb
        dcz1, dcz2 = _rows_after(dcz, carry_ref[...])
        carry_ref[...] = dcz[0:8]
        gcw_ref[2:3, :] += jnp.sum(dcz * z, axis=0, keepdims=True)
        gcw_ref[1:2, :] += jnp.sum(dcz1 * z, axis=0, keepdims=True)
        gcw_ref[0:1, :] += jnp.sum(dcz2 * z, axis=0, keepdims=True)
        dz = cw[2:3] * dcz + cw[1:2] * dcz1 + cw[0:1] * dcz2
        dc3_ref[...] = jnp.concatenate([dconv * cz, dz * cx, dz * cc], axis=1).astype(BF16)

        @pl.when(i == nt - 1)
        def _():
            gbr_ref[...] = br_acc[...].astype(BF16)
            gout_ref[...] = out_acc[...].astype(BF16)

    row = lambda w: pl.BlockSpec((tm, w), lambda i: (nt - 1 - i, 0))
    return _call(
        comm, body, name="mix_bwd", grid=(nt,),
        in_specs=[row(D_MODEL), row(GATES_W), row(D_MODEL), row(D_MODEL), row(C3_W),
                  pl.BlockSpec((halo, C3_W), lambda i: (jnp.maximum((nt - 1 - i) * (tm // halo) - 1, 0), 0)),
                  row(ATTN_W), row(CONV_W), row(D_MODEL), _resident((3, CONV_W)),
                  _resident((ATTN_W + CONV_W, D_MODEL)), _resident((D_MODEL, D_MODEL))],
        out_specs=[row(ATTN_W), row(C3_W), row(GATES_W), pl.BlockSpec((3, CONV_W), lambda i: (0, 0)),
                   _resident((ATTN_W + CONV_W, D_MODEL)), _resident((D_MODEL, D_MODEL))],
        out_shape=[jax.ShapeDtypeStruct((s, ATTN_W), BF16), jax.ShapeDtypeStruct((s, C3_W), BF16),
                   jax.ShapeDtypeStruct((s, GATES_W), BF16), jax.ShapeDtypeStruct((3, CONV_W), F32),
                   jax.ShapeDtypeStruct((ATTN_W + CONV_W, D_MODEL), BF16),
                   jax.ShapeDtypeStruct((D_MODEL, D_MODEL), BF16)],
        scratch_shapes=[pltpu.VMEM((8, CONV_W), F32), pltpu.VMEM((ATTN_W + CONV_W, D_MODEL), F32),
                        pltpu.VMEM((D_MODEL, D_MODEL), F32)],
        compiler_params=_params("arbitrary"),
    )(dh1, gates, a, cv, c3, c3, attn, conv, merged, conv_w, w_br, w_out)


def _attn_bwd(qkv, sinks, o, do, comm=None):
    s = qkv.shape[0]
    npair = s // (2 * BLOCK)

    def one_block(sk_ref, bias, q, kp, kc, vp, vc, ov, dov, dsk_ref):
        dqs, dks, dvs = [], [], []
        for h in range(N_KV_HEADS):
            hs = slice(h * HEAD_DIM, (h + 1) * HEAD_DIM)
            k2 = jnp.concatenate([kp[:, hs], kc[:, hs]], axis=0)
            v2 = jnp.concatenate([vp[:, hs], vc[:, hs]], axis=0)
            qg, og, dog = _stack_heads(q, h), _stack_heads(ov, h), _stack_heads(dov, h)
            sc = lax.dot_general(k2, qg, NT, preferred_element_type=F32) * ATTN_SCALE + bias
            sink = _sink_row(sk_ref, h)
            m = jnp.maximum(jnp.max(sc, axis=0, keepdims=True), sink)
            p = jnp.exp(sc - m)
            psink = jnp.exp(sink - m)
            inv = 1.0 / (jnp.sum(p, axis=0, keepdims=True) + psink)
            p = p * inv
            delta = jnp.sum(dog.astype(F32) * og.astype(F32), axis=1, keepdims=True).T
            dp = lax.dot_general(v2, dog, NT, preferred_element_type=F32)
            ds = (p * (dp - delta)).astype(BF16)
            dqs.append((lax.dot_general(k2, ds, TN, preferred_element_type=F32) * ATTN_SCALE).T)
            dks.append(jnp.dot(ds, qg, preferred_element_type=F32) * ATTN_SCALE)
            dvs.append(jnp.dot(p.astype(BF16), dog, preferred_element_type=F32))
            dsink = -(psink * inv * delta)
            for g in range(GROUP):
                r = h * GROUP + g
                dsk_ref[r:r + 1, :] += jnp.sum(dsink[:, g * BLOCK:(g + 1) * BLOCK])
        return _unstack_heads(dqs), jnp.concatenate(dks, axis=1), jnp.concatenate(dvs, axis=1)

    def body(sk_ref, bias0_ref, bias1_ref, q_ref, kp_ref, kc_ref, vp_ref, vc_ref, o_ref, do_ref,
             dq_ref, dke_ref, dko_ref, dve_ref, dvo_ref, dsk_ref, ck_ref, cvv_ref):
        i = pl.program_id(0)

        @pl.when(i == 0)
        def _():
            for ref in (ck_ref, cvv_ref, dsk_ref):
                ref[...] = jnp.zeros_like(ref)

        @pl.when(i < npair)
        def _():
            kc, vc = kc_ref[...], vc_ref[...]
            first, second = slice(0, BLOCK), slice(BLOCK, 2 * BLOCK)
            dq0, dk0, dv0 = one_block(sk_ref, bias0_ref[...], q_ref[first, :], kp_ref[...], kc[first], vp_ref[...],
                                      vc[first], o_ref[first, :], do_ref[first, :], dsk_ref)
            dq1, dk1, dv1 = one_block(sk_ref, bias1_ref[...], q_ref[second, :], kc[first], kc[second], vc[first],
                                      vc[second], o_ref[second, :], do_ref[second, :], dsk_ref)
            dq_ref[first, :] = dq0.astype(BF16)
            dq_ref[second, :] = dq1.astype(BF16)
            dko_ref[...] = (ck_ref[...] + dk0[:BLOCK]).astype(BF16)
            dvo_ref[...] = (cvv_ref[...] + dv0[:BLOCK]).astype(BF16)
            dke_ref[...] = (dk0[BLOCK:] + dk1[:BLOCK]).astype(BF16)
            dve_ref[...] = (dv0[BLOCK:] + dv1[:BLOCK]).astype(BF16)
            ck_ref[...] = dk1[BLOCK:]
            cvv_ref[...] = dv1[BLOCK:]

        @pl.when(i == npair)
        def _():
            dko_ref[...] = ck_ref[...].astype(BF16)
            dvo_ref[...] = cvv_ref[...].astype(BF16)

    cur = lambda i: jnp.minimum(i, npair - 1)
    done = lambda i: jnp.maximum(i - 1, 0)
    rows = pl.BlockSpec((2 * BLOCK, ATTN_W), lambda i: (cur(i), 0))
    even = pl.BlockSpec((BLOCK, KV_W), lambda i: (cur(i), 0))
    odd = pl.BlockSpec((BLOCK, KV_W), lambda i: (done(i), 0))
    half = jax.ShapeDtypeStruct((s // 2, KV_W), BF16)
    return _call(
        comm, body, name="attn_bwd", grid=(npair + 1,),
        in_specs=[pl.BlockSpec(memory_space=pltpu.SMEM), *_attn_bias_specs(), *_pair_specs(npair), rows, rows],
        out_specs=[rows, even, odd, even, odd, pl.BlockSpec((N_HEADS, 128), lambda i: (0, 0))],
        out_shape=[jax.ShapeDtypeStruct((s, ATTN_W), BF16), half, half, half, half,
                   jax.ShapeDtypeStruct((N_HEADS, 128), F32)],
        scratch_shapes=[pltpu.VMEM((BLOCK, KV_W), F32), pltpu.VMEM((BLOCK, KV_W), F32)],
        compiler_params=_params("arbitrary"),
    )(sinks, _attn_bias(), _attn_bias(), qkv, qkv, qkv, qkv, qkv, o, do)


def _inproj_bwd(dq, dk, dv, dc3, dgt, w_in, x, xn, dh1, g1):
    s = x.shape[0]
    tm = min(2 * BLOCK, s)
    nt = s // tm

    def body(dq_ref, dke_ref, dko_ref, dve_ref, dvo_ref, dc3_ref, dgt_ref, w_ref, x_ref, xn_ref, dh1_ref, g_ref,
             dx_ref, gw_ref, gb_ref, gg_ref, acc_ref):
        i = pl.program_id(0)

        @pl.when(i == 0)
        def _():
            for ref in (gb_ref, gg_ref, acc_ref):
                ref[...] = jnp.zeros_like(ref)

        dk = jnp.concatenate([dke_ref[...], dko_ref[...]], axis=0)
        dv = jnp.concatenate([dve_ref[...], dvo_ref[...]], axis=0)
        dp = jnp.concatenate([dq_ref[...], dk, dv, dc3_ref[...], dgt_ref[...]], axis=1)
        acc_ref[...] += lax.dot_general(dp, xn_ref[...], TN, preferred_element_type=F32)
        gb_ref[...] += jnp.sum(dp.astype(F32), axis=0, keepdims=True)
        dxn = jnp.dot(dp, w_ref[...], preferred_element_type=F32)
        xf = x_ref[...]
        dx, dg = _rms_bwd(dxn, xf, _rstd(xf), g_ref[...])
        dx_ref[...] = dh1_ref[...] + dx
        gg_ref[...] += jnp.sum(dg, axis=0, keepdims=True)

        @pl.when(i == nt - 1)
        def _():
            gw_ref[...] = acc_ref[...].astype(BF16)

    row = lambda w: pl.BlockSpec((tm, w), lambda i: (i, 0))
    acc = lambda w: pl.BlockSpec((1, w), lambda i: (0, 0))
    block = pl.BlockSpec((tm // 2, KV_W), lambda i: (i, 0))
    return pl.pallas_call(
        body, name="inproj_bwd", grid=(nt,),
        in_specs=[row(ATTN_W), block, block, block, block, row(C3_W), row(GATES_W), _resident((IN_W, D_MODEL)),
                  row(D_MODEL), row(D_MODEL), row(D_MODEL), _resident((1, D_MODEL))],
        out_specs=[row(D_MODEL), _resident((IN_W, D_MODEL)), acc(IN_W), acc(D_MODEL)],
        out_shape=[jax.ShapeDtypeStruct((s, D_MODEL), F32), jax.ShapeDtypeStruct((IN_W, D_MODEL), BF16),
                   jax.ShapeDtypeStruct((1, IN_W), F32), jax.ShapeDtypeStruct((1, D_MODEL), F32)],
        scratch_shapes=[pltpu.VMEM((IN_W, D_MODEL), F32)],
        compiler_params=_params("arbitrary"),
    )(dq, *dk, *dv, dc3, dgt, w_in, x, xn, dh1, g1)


def _wgrad(a, b, bm, bn, bk, name, comm=None):
    s, m = a.shape
    n = b.shape[1]
    nk = s // bk

    def body(a_ref, b_ref, o_ref, acc_ref):
        k = pl.program_id(2)

        @pl.when(k == 0)
        def _():
            acc_ref[...] = jnp.zeros_like(acc_ref)

        acc_ref[...] += lax.dot_general(a_ref[...].astype(BF16), b_ref[...].astype(BF16), TN,
                                        preferred_element_type=F32)

        @pl.when(k == nk - 1)
        def _():
            o_ref[...] = acc_ref[...].astype(BF16)

    return _call(
        comm, body, name=name, grid=(m // bm, n // bn, nk),
        in_specs=[pl.BlockSpec((bk, bm), lambda i, j, k: (k, i)), pl.BlockSpec((bk, bn), lambda i, j, k: (k, j))],
        out_specs=pl.BlockSpec((bm, bn), lambda i, j, k: (i, j)),
        out_shape=jax.ShapeDtypeStruct((m, n), BF16),
        scratch_shapes=[pltpu.VMEM((bm, bn), F32)],
        compiler_params=_params("parallel", "parallel", "arbitrary"),
    )(a, b)


class _Carry:
    def __init__(self, jobs, reads=None, bufs=None, fresh=None):
        self.jobs, self.reads, self.bufs, self.fresh = jobs, reads or {}, bufs or {}, fresh or {}
        self.out = {}


class _Job:
    def __init__(self, n_sems, plan):
        self.n_sems, self.plan = n_sems, plan


def _plan_all(jobs, hbm, send, recv):
    pos = _position()
    starts, waits, base = [], [], 0
    for job in jobs:
        s, w = job.plan(hbm, pos, send, recv, base)
        starts, waits, base = starts + s, waits + w, base + job.n_sems
    return starts, waits


def _call(comm, body, **kw):
    if comm is None:
        return pl.pallas_call(body, **kw)
    grid = kw["grid"]
    single = not isinstance(kw["out_shape"], (list, tuple))
    out_shape = [kw["out_shape"]] if single else list(kw["out_shape"])
    out_specs = [kw["out_specs"]] if single else list(kw["out_specs"])
    in_specs = list(kw["in_specs"])
    scratch = list(kw.get("scratch_shapes", ()))
    r_names, b_names, f_names = list(comm.reads), list(comm.bufs), list(comm.fresh)
    n_args, n_out, n_scr = len(in_specs), len(out_shape), len(scratch)
    n_sems = sum(j.n_sems for j in comm.jobs)

    def wrapped(*refs):
        k = n_args
        hbm = dict(zip(r_names, refs[k:k + len(r_names)]))
        k += len(r_names) + len(b_names)
        outs = refs[k:k + n_out]
        k += n_out
        hbm.update(zip(b_names + f_names, refs[k:k + len(b_names) + len(f_names)]))
        k += len(b_names) + len(f_names)
        send, recv = refs[k + n_scr:]
        starts, waits = _plan_all(comm.jobs, hbm, send, recv)
        ids = [pl.program_id(a) for a in range(len(grid))]
        first = functools.reduce(jnp.logical_and, [i == 0 for i in ids])
        last = functools.reduce(jnp.logical_and, [i == g - 1 for i, g in zip(ids, grid)])

        @pl.when(first)
        def _():
            for cp in starts:
                cp.start()

        body(*refs[:n_args], *outs, *refs[k:k + n_scr])

        @pl.when(last)
        def _():
            for cp in waits:
                cp.wait_recv()
            for cp in starts:
                cp.wait_send()

    sems = pltpu.SemaphoreType.DMA((n_sems,))
    held = [jax.ShapeDtypeStruct(a.shape, a.dtype) for a in comm.bufs.values()] + list(comm.fresh.values())
    call = pl.pallas_call(
        wrapped, name=kw["name"], grid=grid,
        in_specs=in_specs + [_ANY] * (len(r_names) + len(b_names)),
        out_specs=out_specs + [_ANY] * len(held),
        out_shape=out_shape + held,
        input_output_aliases={n_args + len(r_names) + i: n_out + i for i in range(len(b_names))},
        scratch_shapes=scratch + [sems, sems],
        compiler_params=_params(*["arbitrary"] * len(grid)),
    )

    def run(*args):
        res = call(*args, *comm.reads.values(), *comm.bufs.values())
        comm.out = dict(zip(b_names + f_names, res[n_out:]))
        return res[0] if single else res[:n_out]

    return run


def _exchange(name, phases, reads=None, bufs=None, fresh=None):
    comm = _Carry([j for ph in phases for j in ph], reads, bufs, fresh)
    r_names, b_names, f_names = list(comm.reads), list(comm.bufs), list(comm.fresh)
    n_sems = sum(j.n_sems for j in comm.jobs)

    def body(*refs):
        hbm = dict(zip(r_names, refs[:len(r_names)]))
        k = len(r_names) + len(b_names)
        hbm.update(zip(b_names + f_names, refs[k:k + len(b_names) + len(f_names)]))
        send, recv = refs[-2:]
        pos = _position()
        started, base = [], 0
        for ph in phases:
            waits = []
            for job in ph:
                s, w = job.plan(hbm, pos, send, recv, base)
                base += job.n_sems
                for cp in s:
                    cp.start()
                started, waits = started + s, waits + w
            for cp in waits:
                cp.wait_recv()
        for cp in started:
            cp.wait_send()

    sems = pltpu.SemaphoreType.DMA((n_sems,))
    held = [jax.ShapeDtypeStruct(a.shape, a.dtype) for a in comm.bufs.values()] + list(comm.fresh.values())
    res = pl.pallas_call(
        body, name=name, in_specs=[_ANY] * (len(r_names) + len(b_names)), out_specs=[_ANY] * len(held),
        out_shape=held, input_output_aliases={len(r_names) + i: i for i in range(len(b_names))},
        scratch_shapes=[sems, sems],
    )(*comm.reads.values(), *comm.bufs.values())
    return dict(zip(b_names + f_names, res))


_HBM = pl.BlockSpec(memory_space=pltpu.HBM)
_SEM = pl.BlockSpec(memory_space=pltpu.SEMAPHORE)
_EFFECT = pltpu.SideEffectType.DATAFLOW_SIDE_EFFECTING


def _start_exchanges(name, groups):
    names = [list(arrays) for _, arrays in groups]
    first = [sum(len(ns) for ns in names[:g]) for g in range(len(groups))]
    n, ng = sum(len(ns) for ns in names), len(groups)

    def body(*refs):
        for g, (jobs, _) in enumerate(groups):
            hbm = dict(zip(names[g], refs[first[g]:first[g] + len(names[g])]))
            for cp in _plan_all(jobs, hbm, refs[n + 2 * g], refs[n + 2 * g + 1])[0]:
                cp.start()
        refs[-1][...] = jnp.zeros_like(refs[-1])

    given = [pltpu.with_memory_space_constraint(
        a if isinstance(a, jax.Array) else lax.empty(a.shape, a.dtype), pltpu.HBM)
        for _, arrays in groups for a in arrays.values()]
    sems = [pltpu.SemaphoreType.DMA((sum(j.n_sems for j in jobs),)) for jobs, _ in groups for _ in range(2)]
    res = pl.pallas_call(
        body, name=name,
        out_shape=(*sems, *[pltpu.HBM(a.shape, a.dtype) for a in given], jax.ShapeDtypeStruct((8, 128), F32)),
        in_specs=[_HBM] * n, out_specs=(*[_SEM] * (2 * ng), *[_HBM] * n, pl.BlockSpec(memory_space=pltpu.VMEM)),
        input_output_aliases={i: 2 * ng + i for i in range(n)},
        compiler_params=pltpu.CompilerParams(has_side_effects=_EFFECT),
    )(*given)
    held = res[2 * ng:2 * ng + n]
    states = [(names[g], groups[g][0], res[2 * g], res[2 * g + 1], held[first[g]:first[g] + len(names[g])])
              for g in range(ng)]
    return states, res[-1]


def _start_exchange(name, jobs, arrays):
    states, token = _start_exchanges(name, [(jobs, arrays)])
    return states[0], token


def _finish_exchange(name, state, after):
    names, jobs, send_sem, recv_sem, held = state
    n = len(names)

    def body(*refs):
        hbm = dict(zip(names, refs[:n]))
        send, recv = refs[n:n + 2]
        starts, waits = _plan_all(jobs, hbm, send, recv)
        for cp in waits:
            cp.wait_recv()
        for cp in starts:
            cp.wait_send()

    res = pl.pallas_call(
        body, name=name, out_shape=tuple(pltpu.HBM(a.shape, a.dtype) for a in held),
        in_specs=[_HBM] * n + [_SEM, _SEM, _ANY], out_specs=tuple([_HBM] * n),
        input_output_aliases={i: i for i in range(n)},
        compiler_params=pltpu.CompilerParams(has_side_effects=_EFFECT),
    )(*held, send_sem, recv_sem, after)
    return dict(zip(names, res))


def _row_tile(rows, bytes_per_row):
    best = 16
    for t in range(16, rows + 1, 16):
        if rows % t == 0 and t * bytes_per_row <= 9 * 1024 * 1024:
            best = t
    return best


def _rowwise(fn, ins, out_dtypes, name, after=None):
    rows, cols = ins[0].shape
    per_row = sum(cols * a.dtype.itemsize for a in ins) + sum(cols * jnp.dtype(d).itemsize for d in out_dtypes)
    tr = _row_tile(rows, per_row)
    n_in = len(ins)

    def body(*refs):
        outs = fn(*[r[...] for r in refs[:n_in]])
        for o_ref, o in zip(refs[-len(out_dtypes):], outs):
            o_ref[...] = o.astype(o_ref.dtype)

    tile = pl.BlockSpec((tr, cols), lambda i: (i, 0))
    behind = [] if after is None else [after]
    return pl.pallas_call(
        body, name=name, grid=(rows // tr,),
        in_specs=[tile] * n_in + [pl.BlockSpec((8, 128), lambda i: (0, 0))] * len(behind),
        out_specs=[tile] * len(out_dtypes),
        out_shape=[jax.ShapeDtypeStruct((rows, cols), d) for d in out_dtypes],
        compiler_params=_params("parallel"),
    )(*ins, *behind)


def _tiled(fn, name, grid, pos, ins, outs):
    n_in = len(ins)

    def body(pos_ref, *refs):
        res = fn(*[r[...] for r in refs[:n_in]])
        for o_ref, o in zip(refs[n_in:], res):
            o_ref[...] = o.astype(o_ref.dtype)

    return pl.pallas_call(
        body, name=name,
        grid_spec=pltpu.PrefetchScalarGridSpec(
            num_scalar_prefetch=1, grid=grid,
            in_specs=[pl.BlockSpec(bs, im) for _, bs, im in ins],
            out_specs=[pl.BlockSpec(bs, im) for _, _, bs, im in outs]),
        out_shape=[jax.ShapeDtypeStruct(s, d) for s, d, _, _ in outs],
        compiler_params=_params("parallel"),
    )(pos, *[a for a, _, _ in ins])


def _adamw(w, g, m, v):
    m = ADAM_B1 * m + (1.0 - ADAM_B1) * g
    v = ADAM_B2 * v + (1.0 - ADAM_B2) * (g * g)
    m_hat = m / (1.0 - ADAM_B1 ** ADAM_STEP)
    v_hat = v / (1.0 - ADAM_B2 ** ADAM_STEP)
    return -ADAM_LR * (m_hat / (jnp.sqrt(v_hat) + ADAM_EPS) + ADAM_WD * w), m, v


def _adamw_small(params):
    n = len(params)

    def body(*refs):
        for k in range(n):
            w, g, m, v = (r[...] for r in refs[4 * k:4 * k + 4])
            for o_ref, o in zip(refs[4 * n + 3 * k:4 * n + 3 * k + 3], _adamw(w, g, m, v)):
                o_ref[...] = o

    flat = [a for p in params for a in p]
    return pl.pallas_call(
        body, name="adamw_small",
        out_shape=[jax.ShapeDtypeStruct(p[0].shape, F32) for p in params for _ in range(3)],
    )(*flat)


class _Layout:
    def __init__(self, rows, cols, stacked):
        self.rows, self.cols, self.stacked = rows, cols, stacked

    def whole(self, rows=None):
        r = self.rows if rows is None else rows
        return (N_CHIPS, r, self.cols) if self.stacked else (r, N_CHIPS * self.cols)

    def part_rows(self, h, q=0, nq=1):
        n = self.rows // 2 // nq
        return pl.ds(pl.multiple_of(h * (self.rows // 2) + q * n, 16), n)

    def half_rows(self, h):
        return self.part_rows(h)

    def block(self, ref, p, rows=slice(None)):
        if self.stacked:
            return ref.at[p, rows, :]
        return ref.at[rows, pl.ds(pl.multiple_of(p * self.cols, 128), self.cols)]

    def all_chips(self, ref, rows):
        return ref.at[:, rows, :] if self.stacked else ref.at[rows, :]


BIG = (
    _Layout(IN_SHARD, D_MODEL, True),
    _Layout(ATTN_W + CONV_W, D_MODEL // N_CHIPS, False),
    _Layout(D_MODEL // N_CHIPS, D_MODEL, True),
    _Layout(D_MODEL, FF2 // N_CHIPS, False),
    _Layout(D_FF // N_CHIPS, D_MODEL, True),
)
N_BIG = len(BIG)
_ANY = pl.BlockSpec(memory_space=pl.ANY)


def _position():
    x, y, c = lax.axis_index("x"), lax.axis_index("y"), lax.axis_index("c")
    return x, y, c, 2 * x + y


def _core_of_chip(p, c):
    return (p >> 1, p & 1, c)


def _place_cast(shard, lay, pos, name, after=None):
    rows, cols = shard.shape
    tr = _row_tile(rows, cols * 6)
    if lay.stacked:
        out = (lay.whole(), BF16, (None, tr, cols), lambda i, pos: (pos[0], i, 0))
    else:
        out = (lay.whole(), BF16, (tr, cols), lambda i, pos: (i, pos[0]))
    ins = [(shard, (tr, cols), lambda i, pos: (i, 0))]
    if after is not None:
        ins.append((after, (8, 128), lambda i, pos: (0, 0)))
    return _tiled(lambda a, *_: (a,), name, (rows // tr,), pos, ins, [out])[0]


def _place_cast_pair(top, bottom, lay, pos, name, after=None):
    rows, cols = top.shape
    ins = [(top, (rows, cols), lambda i, pos: (0, 0)), (bottom, (rows, cols), lambda i, pos: (0, 0))]
    if after is not None:
        ins.append((after, (8, 128), lambda i, pos: (0, 0)))
    return _tiled(lambda a, b, *_: (jnp.concatenate([a, b], axis=0),), name, (1,), pos, ins,
                  [(lay.whole(), BF16, (2 * rows, cols), lambda i, pos: (0, pos[0]))])[0]


def _adamw_pair(top, bottom, g, after=None):
    rows = top[0].shape[0]

    def body(*refs):
        (wa, ma, va, wb, mb, vb, g_ref), outs = refs[:7], refs[-8:]
        for (w, m, v), gg, o in (((wa, ma, va), g_ref[:rows], outs[:4]), ((wb, mb, vb), g_ref[rows:], outs[4:])):
            for o_ref, val in zip(o, (gg, *_adamw(w[...], gg, m[...], v[...]))):
                o_ref[...] = val

    behind = [] if after is None else [after[0:8, 0:128]]
    res = pl.pallas_call(
        body, name="adamw_w_br", out_shape=[jax.ShapeDtypeStruct(top[0].shape, F32)] * 8,
    )(*top, *bottom, g, *behind)
    return res[:4], res[4:]


def _remote(src, dst, send, recv, k, device):
    return pltpu.make_async_remote_copy(src_ref=src, dst_ref=dst, send_sem=send.at[k], recv_sem=recv.at[k],
                                        device_id=device, device_id_type=MESH)


def _arrival(dst, send, recv, k, me):
    return _remote(dst, dst, send, recv, k, me)


def _gather_ici(lay, name, q=0, nq=1):
    def plan(hbm, pos, send, recv, base):
        x, y, c, me = pos
        rows = lay.part_rows(c, q, nq)
        mine = lay.block(hbm[name], me, rows)
        starts = [_remote(mine, mine, send, recv, base + d - 1, _core_of_chip(me ^ d, c)) for d in (1, 2, 3)]
        waits = [_arrival(lay.block(hbm[name], me ^ d, rows), send, recv, base + d - 1, (x, y, c)) for d in (1, 2, 3)]
        return starts, waits
    return _Job(3, plan)


def _gather_d2d(lay, name, q=0, nq=1):
    def plan(hbm, pos, send, recv, base):
        x, y, c, me = pos
        starts, waits = [], []
        for d in (1, 2, 3):
            got = lay.block(hbm[name], me ^ d, lay.part_rows(c, q, nq))
            starts.append(_remote(got, got, send, recv, base + d - 1, (x, y, 1 - c)))
            waits.append(_arrival(lay.block(hbm[name], me ^ d, lay.part_rows(1 - c, q, nq)), send, recv, base + d - 1,
                                  (x, y, c)))
        return starts, waits
    return _Job(3, plan)


def _rs_pair(lay, grad, theirs):
    def plan(hbm, pos, send, recv, base):
        x, y, c, _ = pos
        out = _remote(lay.all_chips(hbm[grad], lay.half_rows(1 - c)), hbm[theirs], send, recv, base, (x, y, 1 - c))
        return [out], [_arrival(hbm[theirs], send, recv, base, (x, y, c))]
    return _Job(1, plan)


def _rs_chips(lay, sums, slots):
    def plan(hbm, pos, send, recv, base):
        x, y, c, me = pos
        starts = [_remote(lay.block(hbm[sums], me ^ d), hbm[slots].at[me], send, recv, base + d - 1,
                          _core_of_chip(me ^ d, c)) for d in (1, 2, 3)]
        waits = [_arrival(hbm[slots].at[me ^ d], send, recv, base + d - 1, (x, y, c)) for d in (1, 2, 3)]
        return starts, waits
    return _Job(3, plan)


def _rs_share(lay, shard):
    def plan(hbm, pos, send, recv, base):
        x, y, c, _ = pos
        mine = hbm[shard].at[lay.half_rows(c), :]
        other = hbm[shard].at[lay.half_rows(1 - c), :]
        return [_remote(mine, mine, send, recv, base, (x, y, 1 - c))], [_arrival(other, send, recv, base, (x, y, c))]
    return _Job(1, plan)


def _slots_shape(lay):
    return jax.ShapeDtypeStruct((N_CHIPS, lay.rows // 2, lay.cols), BF16)


def _theirs_shape(lay, dtype=BF16):
    return jax.ShapeDtypeStruct(lay.whole(lay.rows // 2), dtype)


def _pair_sum(grad, theirs, lay, pos, name):
    half = lay.rows // 2
    add = lambda a, b: (a.astype(F32) + b.astype(F32),)
    if lay.stacked:
        tr = _row_tile(half, lay.cols * 6)
        nt = half // tr
        flat = lambda a: a.reshape(-1, lay.cols)
        mine = lambda t, pos: ((t // nt) * (2 * nt) + pos[1] * nt + t % nt, 0)
        grid, blk = (N_CHIPS * nt,), (tr, lay.cols)
        grad, theirs = flat(grad), flat(theirs)
    else:
        tr = _row_tile(half, N_CHIPS * lay.cols * 6)
        nt = half // tr
        mine = lambda t, pos: (pos[1] * nt + t, 0)
        grid, blk = (nt,), (tr, N_CHIPS * lay.cols)
    same = lambda t, pos: (t, 0)
    out = _tiled(add, name, grid, pos, [(grad, blk, mine), (theirs, blk, same)], [(theirs.shape, BF16, blk, same)])[0]
    return out.reshape(lay.whole(half))


def _chip_sum(sums, slots, lay, pos, name, after=None):
    half = lay.rows // 2
    tr = _row_tile(half, lay.cols * 12)
    nt = half // tr
    blk3 = (None, tr, lay.cols)
    if lay.stacked:
        own = (sums, blk3, lambda i, pos: (pos[0], i, 0))
    else:
        own = (sums, (tr, lay.cols), lambda i, pos: (i, pos[0]))
    others = [(slots, blk3, functools.partial(lambda d, i, pos: (pos[0] ^ d, i, 0), d)) for d in (1, 2, 3)]

    def add(a, b1, b2, b3, *_):
        return (((a.astype(F32) + b1.astype(F32)) + b2.astype(F32)) + b3.astype(F32),)

    if after is not None:
        others.append((after, (8, 128), lambda i, pos: (0, 0)))
    return _tiled(add, name, (nt,), pos, [own] + others,
                  [((lay.rows, lay.cols), F32, (tr, lay.cols), lambda i, pos: (pos[1] * nt + i, 0))])[0]


N_DEV = 8


def _to_all(src, slots):
    def plan(hbm, pos, send, recv, base):
        x, y, c, _ = pos
        idx = 4 * x + 2 * y + c
        starts = [_remote(hbm[src], hbm[slots].at[idx], send, recv, base + k - 1,
                          (x ^ (k >> 2), y ^ ((k >> 1) & 1), c ^ (k & 1))) for k in range(1, N_DEV)]
        waits = [_arrival(hbm[slots].at[idx ^ k], send, recv, base + k - 1, (x, y, c)) for k in range(1, N_DEV)]
        return starts, waits
    return _Job(N_DEV - 1, plan)


def _sum_slots(own, slots, pos):
    def body(pos_ref, own_ref, slots_ref, o_ref):
        idx = 2 * pos_ref[0] + pos_ref[1]
        term = lambda q: jnp.where(idx == q, own_ref[...], slots_ref[q])
        acc = term(0)
        for q in range(1, N_DEV):
            acc = acc + term(q)
        o_ref[...] = acc

    return pl.pallas_call(
        body, name="sum_small", out_shape=jax.ShapeDtypeStruct(own.shape, F32),
        in_specs=[pl.BlockSpec(memory_space=pltpu.SMEM), pl.BlockSpec(memory_space=pltpu.VMEM),
                  pl.BlockSpec(memory_space=pltpu.VMEM)],
    )(pos, own, slots)


def _pack_rows(parts):
    padded = [jnp.pad(a, ((0, -a.shape[0] % 8), (0, 0))) for a in parts]
    starts = [sum(p.shape[0] for p in padded[:k]) for k in range(len(padded))]
    return jnp.concatenate(padded, axis=0), starts


def kernel(x, mix_norm, w_in, b_in, sinks, conv_w, w_attn_branch, w_conv_branch, w_out, ffn_norm, w_up, ffn_conv_w, w_down, final_norm, loss_target, m_mix_norm, m_w_in, m_b_in, m_sinks, m_conv_w, m_w_attn_branch, m_w_conv_branch, m_w_out, m_ffn_norm, m_w_up, m_ffn_conv_w, m_w_down, m_final_norm, v_mix_norm, v_w_in, v_b_in, v_sinks, v_conv_w, v_w_attn_branch, v_w_conv_branch, v_w_out, v_ffn_norm, v_w_up, v_ffn_conv_w, v_w_down, v_final_norm):
    me = 2 * lax.axis_index("x") + lax.axis_index("y")
    names = ("w_in", "w_br", "w_out", "w_up", "w_down")
    w_of = dict(w_in=w_in[0].T, w_out=w_out[0], w_up=w_up[0], w_down=w_down[0])
    m_of = dict(w_in=m_w_in[0].T, w_out=m_w_out[0], w_up=m_w_up[0], w_down=m_w_down[0])
    v_of = dict(w_in=v_w_in[0].T, w_out=v_w_out[0], w_up=v_w_up[0], w_down=v_w_down[0])
    ab = (w_attn_branch[0], m_w_attn_branch[0], v_w_attn_branch[0])
    cb = (w_conv_branch[0], m_w_conv_branch[0], v_w_conv_branch[0])

    pos = jnp.stack([me, lax.axis_index("c")]).astype(jnp.int32)

    lay = dict(zip(names, BIG))
    xs, target, sk = x[0], loss_target[0], sinks[0]
    s = xs.shape[0]
    tm, tm2, bk, bk2 = min(256, s), min(512, s), min(1024, s), min(2048, s)

    taps, (_, t0) = _pack_rows([conv_w[0], ffn_conv_w[0].reshape(3 * (FF2 // N_CHIPS // 128), 128)])
    placed = {"w_in": _place_cast(w_of["w_in"], lay["w_in"], pos, "cast_w_in")}
    fly_in, started = _start_exchange("gather_in_start", [_gather_ici(lay["w_in"], "w_in")], {"w_in": placed["w_in"]})
    taps_flight, started = _start_exchange("taps_start", [_to_all("v", "slots")],
                                           {"v": taps + started[0:1], "slots": jnp.zeros((N_DEV, *taps.shape), F32)})
    placed["w_br"] = _place_cast_pair(ab[0], cb[0], lay["w_br"], pos, "cast_w_br", after=started)
    for n in names[2:]:
        placed[n] = _place_cast(w_of[n], lay[n], pos, "cast_" + n, after=started)
    trio = ("w_br", "w_out")
    (fly_trio, fly_up, fly_down), started = _start_exchanges("gather_rest_start", [
        ([_gather_ici(lay[n], n) for n in ws], {n: placed[n] for n in ws}) for ws in (trio, ("w_up",), ("w_down",))])

    got = _finish_exchange("gather_in_wait", fly_in, after=started)
    w_in_full = _exchange("gather_in_d2d", [[_gather_d2d(lay["w_in"], "w_in")]], bufs=got)["w_in"].reshape(IN_W, D_MODEL)
    xn, qkv, c3, gates = _inproj_fwd(xs, mix_norm, w_in_full, b_in, tm2)
    k2 = _Carry([_gather_d2d(lay[n], n) for n in trio], bufs=_finish_exchange("gather_trio_wait", fly_trio, after=qkv))
    attn = _attn_fwd(qkv, sk, comm=k2)
    w_br = k2.out["w_br"]
    w_out_full = k2.out["w_out"].reshape(D_MODEL, D_MODEL)
    k3 = _Carry([_gather_d2d(lay["w_up"], "w_up")], bufs=_finish_exchange("gather_up_wait", fly_up, after=attn))
    taps = _finish_exchange("taps_wait", taps_flight, after=attn)
    taps = lax.dynamic_update_slice(taps["slots"], taps["v"][None], (2 * me + lax.axis_index("c"), 0, 0))
    conv_full = taps[0::2, 0:3].transpose(1, 0, 2).reshape(3, CONV_W)
    ffn_cw_full = taps[0::2, t0:t0 + 33].reshape(N_CHIPS, 3, FF2 // N_CHIPS).transpose(1, 0, 2).reshape(3, FF2)
    conv, a, cv, merged, h1, hn = _mix_fwd(xs, attn, c3, gates, conv_full, w_br, w_out_full, ffn_norm, tm2, comm=k3)
    w_up_full = k3.out["w_up"]
    w_down_full = _exchange("gather_down_d2d", [[_gather_d2d(lay["w_down"], "w_down")]],
                            bufs=_finish_exchange("gather_down_wait", fly_down, after=hn))["w_down"].reshape(D_FF, D_MODEL)
    u, up, act, dh2, loss_part, g_fn = _ffn_fwd_loss(hn, h1, w_up_full, ffn_cw_full, w_down_full,
                                                     final_norm[None, :], target, tm)

    grads, sums, slots = {}, {}, {}

    def pair(*ws):
        return _Carry([_rs_pair(lay[n], "g_" + n, "t_" + n) for n in ws], reads={"g_" + n: grads[n] for n in ws},
                      fresh={"t_" + n: _theirs_shape(lay[n], grads[n].dtype) for n in ws})

    def chips(*ws, also=None):
        k = _Carry([_rs_chips(lay[n], "s_" + n, "r_" + n) for n in ws], reads={"s_" + n: sums[n] for n in ws},
                   fresh={"r_" + n: _slots_shape(lay[n]) for n in ws})
        if also is not None:
            k = _Carry(k.jobs + also.jobs, {**k.reads, **also.reads}, None, {**k.fresh, **also.fresh})
        return k

    def pair_sums(k, *ws):
        for n in ws:
            sums[n] = _pair_sum(grads[n], k.out["t_" + n], lay[n], pos, "pair_sum_" + n)

    def take_slots(k, *ws):
        for n in ws:
            slots[n] = k.out["r_" + n]

    du, dh1, g_fcw, g_g2 = _ffn_bwd(dh2, u, up, h1, w_up_full, ffn_cw_full, w_down_full, ffn_norm, tm)
    grads["w_down"] = _wgrad(act, dh2, D_FF // 2, D_MODEL, bk2, "wgrad_down").reshape(lay["w_down"].whole())
    k4 = pair("w_down")
    grads["w_up"] = _wgrad(hn, du, D_MODEL, FF2 // 4, bk2, "wgrad_up", comm=k4)
    pair_sums(k4, "w_down")
    k5 = chips("w_down", also=pair("w_up"))
    dattn, dc3, dgt, g_cw, grads["w_br"], gw_out = _mix_bwd(
        dh1, gates, a, cv, c3, attn, conv, merged, conv_full, w_br, w_out_full, tm2, comm=k5)
    grads["w_out"] = gw_out.reshape(lay["w_out"].whole())
    take_slots(k5, "w_down")
    pair_sums(k5, "w_up")
    k6 = chips("w_up", also=pair(*trio))
    dq, dk_even, dk_odd, dv_even, dv_odd, g_sk = _attn_bwd(qkv, sk, attn, dattn, comm=k6)
    take_slots(k6, "w_up")
    pair_sums(k6, *trio)
    trio_flight, started = _start_exchange(
        "rs_chips_trio_start", [_rs_chips(lay[n], "s_" + n, "r_" + n) for n in trio],
        {**{"s_" + n: sums[n] for n in trio}, **{"r_" + n: _slots_shape(lay[n]) for n in trio}})
    behind = mix_norm + jnp.tile(started[0:1], (1, D_MODEL // 128))
    grad_x, gw_in, g_b, g_g1 = _inproj_bwd(dq, (dk_even, dk_odd), (dv_even, dv_odd), dc3, dgt, w_in_full, xs, xn,
                                           dh1, behind)
    grads["w_in"] = gw_in.reshape(lay["w_in"].whole())

    parts = [loss_part, g_g1, g_b, jnp.pad(g_sk[:, 0], (0, 120))[None, :], g_cw, g_g2, g_fcw, g_fn]
    packed, at = _pack_rows([p.reshape(-1, 128) for p in parts])
    small_flight, started = _start_exchange("small_start", [_to_all("v", "slots")],
                                            {"v": packed, "slots": jnp.zeros((N_DEV, *packed.shape), F32)})
    in_flight, started = _start_exchange("rs_pair_in_start", [_rs_pair(lay["w_in"], "g", "t")],
                                         {"g": grads["w_in"], "t": _theirs_shape(lay["w_in"]), "behind": started})
    halves = {n: _chip_sum(sums[n], slots[n], lay[n], pos, "chip_sum_" + n, after=started) for n in ("w_up", "w_down")}
    landed = _finish_exchange("rs_pair_in_wait", in_flight, after=halves["w_down"])
    sums["w_in"] = _pair_sum(landed["g"], landed["t"], lay["w_in"], pos, "pair_sum_w_in")
    in_flight, started = _start_exchange("rs_chips_in_start", [_rs_chips(lay["w_in"], "s", "r")],
                                         {"s": sums["w_in"], "r": _slots_shape(lay["w_in"])})
    landed = _finish_exchange("rs_chips_trio_wait", trio_flight, after=started)
    for n in trio:
        halves[n] = _chip_sum(landed["s_" + n], landed["r_" + n], lay[n], pos, "chip_sum_" + n)
    shared = _exchange("share_halves", [[_rs_share(lay[n], n) for n in names[1:]]], bufs=halves)

    def adam(n, g, after=None):
        return _rowwise(lambda w, g, m, v: (g, *_adamw(w, g, m, v)), [w_of[n], g, m_of[n], v_of[n]], [F32] * 4,
                        "adamw_" + n, after=after)

    new_of, last = {}, None
    for n in ("w_up", "w_down", "w_out"):
        new_of[n] = adam(n, shared[n], last)
        last = new_of[n][1]
    new_of["w_ab"], new_of["w_cb"] = _adamw_pair(ab, cb, shared["w_br"], after=last)
    last = new_of["w_cb"][1]

    arrived = _finish_exchange("small_wait", small_flight, after=last)
    total = _sum_slots(arrived["v"], arrived["slots"], pos)
    part = lambda k: total[at[k]:at[k] + parts[k].size // 128].reshape(parts[k].shape)
    loss = total[0, 0]
    g_mix, g_b, g_g2, g_fn = part(1), part(2), part(5), part(7)
    g_sk = part(3)[:, 0:N_HEADS]
    g_cw = lax.dynamic_slice(part(4), (0, me * 128), (3, 128))
    g_fcw = lax.dynamic_slice(part(6), (0, me * (FF2 // N_CHIPS)), (3, FF2 // N_CHIPS))
    small_p = [
        (mix_norm, g_mix, m_mix_norm, v_mix_norm), (b_in, g_b, m_b_in, v_b_in), (sinks, g_sk, m_sinks, v_sinks),
        (conv_w[0], g_cw, m_conv_w[0], v_conv_w[0]), (ffn_norm, g_g2, m_ffn_norm, v_ffn_norm),
        (ffn_conv_w[0], g_fcw, m_ffn_conv_w[0], v_ffn_conv_w[0]),
        (final_norm[None, :], g_fn, m_final_norm[None, :], v_final_norm[None, :])]
    small_new = _adamw_small(small_p)
    small_new = [small_new[3 * k:3 * k + 3] for k in range(len(small_p))]

    landed = _finish_exchange("rs_chips_in_wait", in_flight, after=small_new[0][0])
    half_in = _chip_sum(landed["s"], landed["r"], lay["w_in"], pos, "chip_sum_w_in")
    shared["w_in"] = _exchange("share_in", [[_rs_share(lay["w_in"], "w_in")]], bufs={"w_in": half_in})["w_in"]
    new_of["w_in"] = [a.T for a in adam("w_in", shared["w_in"])]
    big = ("w_in", "w_ab", "w_cb", "w_out", "w_up", "w_down")
    big_g = [new_of[n][0] for n in big]
    big_new = [new_of[n][1:] for n in big]

    order = [("s", 0), ("b", 0), ("s", 1), ("s", 2), ("s", 3), ("b", 1), ("b", 2), ("b", 3), ("s", 4), ("b", 4),
             ("s", 5), ("b", 5), ("s", 6)]
    shapes = [mix_norm.shape, w_in.shape, b_in.shape, sinks.shape, conv_w.shape, w_attn_branch.shape,
              w_conv_branch.shape, w_out.shape, ffn_norm.shape, w_up.shape, ffn_conv_w.shape, w_down.shape,
              final_norm.shape]
    small_g = [p[1] for p in small_p]
    out_g = [(small_g[k] if kind == "s" else big_g[k]).reshape(shp) for (kind, k), shp in zip(order, shapes)]
    news = [[(small_new[k][j] if kind == "s" else big_new[k][j]).reshape(shp) for (kind, k), shp in zip(order, shapes)]
            for j in range(3)]
    return (loss, grad_x[None], *out_g, *news[0], *news[1], *news[2])
```

```python
import functools

import jax
import jax.numpy as jnp
from jax import lax
from jax.experimental import pallas as pl
from jax.experimental.pallas import tpu as pltpu

F32 = jnp.float32
BF16 = jnp.bfloat16

D_MODEL = 1024
HEAD_DIM = 64
N_HEADS = 8
N_KV_HEADS = 2
GROUP = N_HEADS // N_KV_HEADS
BLOCK = 128
ATTN_SCALE = HEAD_DIM ** -0.5
ATTN_W = N_HEADS * HEAD_DIM
KV_W = N_KV_HEADS * HEAD_DIM
CONV_W = 512
QKV_W = ATTN_W + 2 * KV_W
C3_W = 3 * CONV_W
GATES_W = 2 * D_MODEL
IN_W = QKV_W + C3_W + GATES_W
D_FF = 2816
FF2 = 2 * D_FF
NORM_EPS = 1e-5
N_CHIPS = 4
IN_SHARD = IN_W // N_CHIPS
NEG = -1e30

ADAM_LR = 0.001
ADAM_B1 = 0.9
ADAM_B2 = 0.999
ADAM_EPS = 1e-08
ADAM_WD = 0.01
ADAM_STEP = 10

VMEM_LIMIT = 56 * 1024 * 1024
MESH = pl.DeviceIdType.MESH

NT = (((1,), (1,)), ((), ()))
TN = (((0,), (0,)), ((), ()))


def _params(*sem):
    return pltpu.CompilerParams(dimension_semantics=sem, vmem_limit_bytes=VMEM_LIMIT)


def _resident(shape):
    return pl.BlockSpec(shape, lambda *_: (0,) * len(shape), pipeline_mode=pl.Buffered(1))


def _sigmoid(v):
    return 0.5 * jnp.tanh(0.5 * v) + 0.5


def _rstd(v):
    return lax.rsqrt(jnp.mean(v * v, axis=-1, keepdims=True) + NORM_EPS)


def _rms_bwd(dy, v, rstd, g):
    vhat = v * rstd
    t = dy * g
    return rstd * (t - vhat * jnp.mean(t * vhat, axis=-1, keepdims=True)), dy * vhat


def _taps(z, cw):
    return cw[2:3] * z + cw[1:2] * pltpu.roll(z, 1, 0) + cw[0:1] * pltpu.roll(z, 2, 0)


def _causal_conv(z, prev, cw):
    edge = _taps(jnp.concatenate([prev, z[0:8]], axis=0), cw)
    return jnp.concatenate([edge[8:16], _taps(z, cw)[8:]], axis=0)


def _rows_after(z, nxt):
    n = z.shape[0]
    edge = jnp.concatenate([z[n - 8:n], nxt], axis=0)
    return tuple(jnp.concatenate([pltpu.roll(z, n - k, 0)[:n - 8], pltpu.roll(edge, 16 - k, 0)[0:8]], axis=0)
                 for k in (1, 2))


def _inproj_fwd(x, g1, w_in, b_in, tm, comm=None):
    s = x.shape[0]

    def body(x_ref, g_ref, w_ref, b_ref, xn_ref, qkv_ref, c3_ref, gt_ref):
        xf = x_ref[...]
        xn = (xf * _rstd(xf) * g_ref[...]).astype(BF16)
        xn_ref[...] = xn

        proj = (lax.dot_general(xn, w_ref[...], NT, preferred_element_type=F32) + b_ref[...]).astype(BF16)
        qkv_ref[...] = proj[:, :QKV_W]
        c3_ref[...] = proj[:, QKV_W:QKV_W + C3_W]
        gt_ref[...] = proj[:, QKV_W + C3_W:]

    row = lambda w: pl.BlockSpec((tm, w), lambda i: (i, 0))
    return _call(
        comm, body, name="inproj_fwd", grid=(s // tm,),
        in_specs=[row(D_MODEL), _resident((1, D_MODEL)), _resident((IN_W, D_MODEL)), _resident((1, IN_W))],
        out_specs=[row(D_MODEL), row(QKV_W), row(C3_W), row(GATES_W)],
        out_shape=[jax.ShapeDtypeStruct((s, D_MODEL), BF16), jax.ShapeDtypeStruct((s, QKV_W), BF16),
                   jax.ShapeDtypeStruct((s, C3_W), BF16), jax.ShapeDtypeStruct((s, GATES_W), BF16)],
        compiler_params=_params("parallel"),
    )(x, g1, w_in, b_in)


def _attn_bias():
    kj = jnp.arange(2 * BLOCK)[:, None]
    qi = (jnp.arange(GROUP * BLOCK) % BLOCK)[None, :]
    band = (kj > qi) & (kj <= qi + BLOCK)
    return jnp.stack([jnp.where(band & (kj >= BLOCK), 0.0, NEG), jnp.where(band, 0.0, NEG)]).astype(F32)


def _attn_bias_specs():
    shape = (None, 2 * BLOCK, GROUP * BLOCK)
    return pl.BlockSpec(shape, lambda i: (jnp.minimum(i, 1), 0, 0)), pl.BlockSpec(shape, lambda i: (1, 0, 0))


def _sink_row(sk_ref, h):
    lane = lax.broadcasted_iota(jnp.int32, (1, GROUP * BLOCK), 1)
    row = jnp.full((1, GROUP * BLOCK), sk_ref[h * GROUP], F32)
    for g in range(1, GROUP):
        row = jnp.where(lane >= g * BLOCK, sk_ref[h * GROUP + g], row)
    return row


def _stack_heads(t, h):
    return jnp.concatenate(
        [t[:, (h * GROUP + g) * HEAD_DIM:(h * GROUP + g + 1) * HEAD_DIM] for g in range(GROUP)], axis=0)


def _unstack_heads(per_kv):
    return jnp.concatenate(
        [t[g * BLOCK:(g + 1) * BLOCK] for t in per_kv for g in range(GROUP)], axis=1)


def _pair_specs(npair):
    cur = lambda i: jnp.minimum(i, npair - 1)
    prev = lambda i: jnp.maximum(2 * jnp.minimum(i, npair - 1) - 1, 0)
    kv = ATTN_W // KV_W
    return (pl.BlockSpec((2 * BLOCK, ATTN_W), lambda i: (cur(i), 0)),
            pl.BlockSpec((BLOCK, KV_W), lambda i: (prev(i), kv)), pl.BlockSpec((2 * BLOCK, KV_W), lambda i: (cur(i), kv)),
            pl.BlockSpec((BLOCK, KV_W), lambda i: (prev(i), kv + 1)),
            pl.BlockSpec((2 * BLOCK, KV_W), lambda i: (cur(i), kv + 1)))


def _attn_fwd(qkv, sinks, comm=None):
    s = qkv.shape[0]
    npair = s // (2 * BLOCK)

    def body(sk_ref, bias0_ref, bias1_ref, q_ref, kp_ref, kc_ref, vp_ref, vc_ref, o_ref):
        kc, vc = kc_ref[...], vc_ref[...]
        for b, (bias_ref, kp, vp) in enumerate(((bias0_ref, kp_ref[...], vp_ref[...]),
                                                (bias1_ref, kc[:BLOCK], vc[:BLOCK]))):
            rows = slice(b * BLOCK, (b + 1) * BLOCK)
            q, bias = q_ref[rows, :], bias_ref[...]
            outs = []
            for h in range(N_KV_HEADS):
                hs = slice(h * HEAD_DIM, (h + 1) * HEAD_DIM)
                k2 = jnp.concatenate([kp[:, hs], kc[rows, hs]], axis=0)
                v2 = jnp.concatenate([vp[:, hs], vc[rows, hs]], axis=0)
                sc = lax.dot_general(k2, _stack_heads(q, h), NT, preferred_element_type=F32) * ATTN_SCALE + bias
                sink = _sink_row(sk_ref, h)
                m = jnp.maximum(jnp.max(sc, axis=0, keepdims=True), sink)
                p = jnp.exp(sc - m)
                den = jnp.sum(p, axis=0, keepdims=True) + jnp.exp(sink - m)
                out = lax.dot_general(v2, p.astype(BF16), TN, preferred_element_type=F32) / den
                outs.append(out.T)
            o_ref[rows, :] = _unstack_heads(outs).astype(BF16)

    return _call(
        comm, body, name="attn_fwd", grid=(npair,),
        in_specs=[pl.BlockSpec(memory_space=pltpu.SMEM), *_attn_bias_specs(), *_pair_specs(npair)],
        out_specs=pl.BlockSpec((2 * BLOCK, ATTN_W), lambda i: (i, 0)),
        out_shape=jax.ShapeDtypeStruct((s, ATTN_W), BF16),
        compiler_params=_params("parallel"),
    )(sinks, _attn_bias(), _attn_bias(), qkv, qkv, qkv, qkv, qkv)


def _mix_fwd(x, attn, c3, gates, conv_w, w_br, w_out, g2, tm, comm=None):
    s = x.shape[0]

    def body(x_ref, at_ref, c3_ref, gt_ref, cw_ref, wbr_ref, wo_ref, g_ref,
             conv_ref, a_ref, cv_ref, mg_ref, h1_ref, hn_ref, carry_ref):
        @pl.when(pl.program_id(0) == 0)
        def _():
            carry_ref[...] = jnp.zeros_like(carry_ref)

        c3v = c3_ref[...].astype(F32)
        cb, cc, cx = c3v[:, :CONV_W], c3v[:, CONV_W:2 * CONV_W], c3v[:, 2 * CONV_W:]
        z = cc * cx
        cz = _causal_conv(z, carry_ref[...], cw_ref[...])
        carry_ref[...] = z[tm - 8:tm]
        conv = (cb * cz).astype(BF16)
        conv_ref[...] = conv
        a = jnp.dot(at_ref[...], wbr_ref[:ATTN_W, :], preferred_element_type=F32)
        cv = jnp.dot(conv, wbr_ref[ATTN_W:, :], preferred_element_type=F32)
        a_ref[...] = a.astype(BF16)
        cv_ref[...] = cv.astype(BF16)
        gt = gt_ref[...].astype(F32)
        merged = (_sigmoid(gt[:, :D_MODEL]) * a + _sigmoid(gt[:, D_MODEL:]) * cv).astype(BF16)
        mg_ref[...] = merged
        h1 = x_ref[...] + jnp.dot(merged, wo_ref[...], preferred_element_type=F32)
        h1_ref[...] = h1
        hn_ref[...] = (h1 * _rstd(h1) * g_ref[...]).astype(BF16)

    row = lambda w: pl.BlockSpec((tm, w), lambda i: (i, 0))
    return _call(
        comm, body, name="mix_fwd", grid=(s // tm,),
        in_specs=[row(D_MODEL), row(ATTN_W), row(C3_W), row(GATES_W), _resident((3, CONV_W)),
                  _resident((ATTN_W + CONV_W, D_MODEL)), _resident((D_MODEL, D_MODEL)), _resident((1, D_MODEL))],
        out_specs=[row(CONV_W), row(D_MODEL), row(D_MODEL), row(D_MODEL), row(D_MODEL), row(D_MODEL)],
        out_shape=[jax.ShapeDtypeStruct((s, CONV_W), BF16), jax.ShapeDtypeStruct((s, D_MODEL), BF16),
                   jax.ShapeDtypeStruct((s, D_MODEL), BF16), jax.ShapeDtypeStruct((s, D_MODEL), BF16),
                   jax.ShapeDtypeStruct((s, D_MODEL), F32), jax.ShapeDtypeStruct((s, D_MODEL), BF16)],
        scratch_shapes=[pltpu.VMEM((8, CONV_W), F32)],
        compiler_params=_params("arbitrary"),
    )(x, attn, c3, gates, conv_w, w_br, w_out, g2)


def _ffn_fwd_loss(hn, h1, w_up, ffn_cw, w_down, g3, target, tm):
    s = hn.shape[0]

    def body(hn_ref, h1_ref, wu_ref, cw_ref, wd_ref, g_ref, t_ref,
             u_ref, up_ref, act_ref, dh2_ref, loss_ref, gfn_ref, carry_ref):
        @pl.when(pl.program_id(0) == 0)
        def _():
            carry_ref[...] = jnp.zeros_like(carry_ref)
            loss_ref[...] = jnp.zeros_like(loss_ref)
            gfn_ref[...] = jnp.zeros_like(gfn_ref)

        u = jnp.dot(hn_ref[...], wu_ref[...], preferred_element_type=F32)
        u_ref[...] = u.astype(BF16)
        up = _causal_conv(u, carry_ref[...], cw_ref[...])
        up_ref[...] = up
        carry_ref[...] = u[tm - 8:tm]
        gate, val = up[:, :D_FF], up[:, D_FF:]
        act = (gate * _sigmoid(gate) * val).astype(BF16)
        act_ref[...] = act
        h2 = h1_ref[...] + jnp.dot(act, wd_ref[...], preferred_element_type=F32)
        rstd = _rstd(h2)
        g = g_ref[...]
        err = h2 * rstd * g - t_ref[...]
        loss_ref[...] += jnp.sum(err * err) * (0.5 / D_MODEL)
        dh2, dg = _rms_bwd(err * (1.0 / D_MODEL), h2, rstd, g)
        dh2_ref[...] = dh2
        gfn_ref[...] += jnp.sum(dg, axis=0, keepdims=True)

    row = lambda w: pl.BlockSpec((tm, w), lambda i: (i, 0))
    acc = lambda w: pl.BlockSpec((1, w), lambda i: (0, 0))
    return pl.pallas_call(
        body, name="ffn_fwd_loss", grid=(s // tm,),
        in_specs=[row(D_MODEL), row(D_MODEL), _resident((D_MODEL, FF2)), _resident((3, FF2)),
                  _resident((D_FF, D_MODEL)), _resident((1, D_MODEL)), row(D_MODEL)],
        out_specs=[row(FF2), row(FF2), row(D_FF), row(D_MODEL), acc(128), acc(D_MODEL)],
        out_shape=[jax.ShapeDtypeStruct((s, FF2), BF16), jax.ShapeDtypeStruct((s, FF2), F32),
                   jax.ShapeDtypeStruct((s, D_FF), BF16),
                   jax.ShapeDtypeStruct((s, D_MODEL), F32), jax.ShapeDtypeStruct((1, 128), F32),
                   jax.ShapeDtypeStruct((1, D_MODEL), F32)],
        scratch_shapes=[pltpu.VMEM((8, FF2), F32)],
        compiler_params=_params("arbitrary"),
    )(hn, h1, w_up, ffn_cw, w_down, g3, target)


def _ffn_bwd(dh2, u, up, h1, w_up, ffn_cw, w_down, g2, tm):
    s = dh2.shape[0]
    nt = s // tm

    def body(dh2_ref, u_ref, up_ref, h1_ref, wu_ref, cw_ref, wd_ref, g_ref,
             du_ref, dh1_ref, gcw_ref, gg_ref, carry_ref):
        @pl.when(pl.program_id(0) == 0)
        def _():
            for ref in (carry_ref, gcw_ref, gg_ref):
                ref[...] = jnp.zeros_like(ref)

        dh2v = dh2_ref[...]
        dact = lax.dot_general(dh2v.astype(BF16), wd_ref[...], NT, preferred_element_type=F32)
        upv = up_ref[...]
        gate, val = upv[:, :D_FF], upv[:, D_FF:]
        sg = _sigmoid(gate)
        dval = dact * (gate * sg)
        dgate = dact * val * (sg * (1.0 + gate * (1.0 - sg)))
        dup = jnp.concatenate([dgate, dval], axis=1)
        dup1, dup2 = _rows_after(dup, carry_ref[...])
        carry_ref[...] = dup[0:8]
        u = u_ref[...].astype(F32)
        gcw_ref[2:3, :] += jnp.sum(dup * u, axis=0, keepdims=True)
        gcw_ref[1:2, :] += jnp.sum(dup1 * u, axis=0, keepdims=True)
        gcw_ref[0:1, :] += jnp.sum(dup2 * u, axis=0, keepdims=True)
        cw = cw_ref[...]
        du = (cw[2:3] * dup + cw[1:2] * dup1 + cw[0:1] * dup2).astype(BF16)
        du_ref[...] = du
        dhn = lax.dot_general(du, wu_ref[...], NT, preferred_element_type=F32)
        h1v = h1_ref[...]
        dh1, dg = _rms_bwd(dhn, h1v, _rstd(h1v), g_ref[...])
        dh1_ref[...] = dh2v + dh1
        gg_ref[...] += jnp.sum(dg, axis=0, keepdims=True)

    row = lambda w: pl.BlockSpec((tm, w), lambda i: (nt - 1 - i, 0))
    return pl.pallas_call(
        body, name="ffn_bwd", grid=(nt,),
        in_specs=[row(D_MODEL), row(FF2), row(FF2),
                  row(D_MODEL), _resident((D_MODEL, FF2)), _resident((3, FF2)), _resident((D_FF, D_MODEL)),
                  _resident((1, D_MODEL))],
        out_specs=[row(FF2), row(D_MODEL), pl.BlockSpec((3, FF2), lambda i: (0, 0)),
                   pl.BlockSpec((1, D_MODEL), lambda i: (0, 0))],
        out_shape=[jax.ShapeDtypeStruct((s, FF2), BF16), jax.ShapeDtypeStruct((s, D_MODEL), F32),
                   jax.ShapeDtypeStruct((3, FF2), F32), jax.ShapeDtypeStruct((1, D_MODEL), F32)],
        scratch_shapes=[pltpu.VMEM((8, FF2), F32)],
        compiler_params=_params("arbitrary"),
    )(dh2, u, up, h1, w_up, ffn_cw, w_down, g2)


def _mix_bwd(dh1, gates, a, cv, c3, attn, conv, merged, conv_w, w_br, w_out, tm, comm=None):
    s = dh1.shape[0]
    nt = s // tm
    halo = 16

    def body(dh1_ref, gt_ref, a_ref, cv_ref, c3_ref, ch_ref, at_ref, cn_ref, mg_ref, cw_ref, wbr_ref,
             wo_ref, dat_ref, dc3_ref, dgt_ref, gcw_ref, gbr_ref, gout_ref, carry_ref, br_acc, out_acc):
        i = pl.program_id(0)

        @pl.when(i == 0)
        def _():
            for ref in (carry_ref, gcw_ref, br_acc, out_acc):
                ref[...] = jnp.zeros_like(ref)

        dh1v = dh1_ref[...].astype(BF16)
        out_acc[...] += lax.dot_general(mg_ref[...], dh1v, TN, preferred_element_type=F32)
        dm = lax.dot_general(dh1v, wo_ref[...], NT, preferred_element_type=F32)
        gt = gt_ref[...].astype(F32)
        sa, sc = _sigmoid(gt[:, :D_MODEL]), _sigmoid(gt[:, D_MODEL:])
        da = (dm * sa).astype(BF16)
        dcv = (dm * sc).astype(BF16)
        br_acc[:ATTN_W, :] += lax.dot_general(at_ref[...], da, TN, preferred_element_type=F32)
        br_acc[ATTN_W:, :] += lax.dot_general(cn_ref[...], dcv, TN, preferred_element_type=F32)
        dgt_ref[...] = jnp.concatenate(
            [dm * a_ref[...].astype(F32) * (sa * (1.0 - sa)), dm * cv_ref[...].astype(F32) * (sc * (1.0 - sc))],
            axis=1).astype(BF16)
        dat_ref[...] = lax.dot_general(da, wbr_ref[:ATTN_W, :], NT, preferred_element_type=F32).astype(BF16)
        dconv = lax.dot_general(dcv, wbr_ref[ATTN_W:, :], NT, preferred_element_type=F32)
        c3v = c3_ref[...].astype(F32)
        cb, cc, cx = c3v[:, :CONV_W], c3v[:, CONV_W:2 * CONV_W], c3v[:, 2 * CONV_W:]
        z = cc * cx
        chv = ch_ref[...].astype(F32)[halo - 8:halo] * (i < nt - 1).astype(F32)
        zh = chv[:, CONV_W:2 * CONV_W] * chv[:, 2 * CONV_W:]
        cw = cw_ref[...]
        cz = _causal_conv(z, zh, cw)
        dcz = dconv * cb
        dcz1, dcz2 = _rows_after(dcz, carry_ref[...])
        carry_ref[...] = dcz[0:8]
        gcw_ref[2:3, :] += jnp.sum(dcz * z, axis=0, keepdims=True)
        gcw_ref[1:2, :] += jnp.sum(dcz1 * z, axis=0, keepdims=True)
        gcw_ref[0:1, :] += jnp.sum(dcz2 * z, axis=0, keepdims=True)
        dz = cw[2:3] * dcz + cw[1:2] * dcz1 + cw[0:1] * dcz2
        dc3_ref[...] = jnp.concatenate([dconv * cz, dz * cx, dz * cc], axis=1).astype(BF16)

        @pl.when(i == nt - 1)
        def _():
            gbr_ref[...] = br_acc[...].astype(BF16)
            gout_ref[...] = out_acc[...].astype(BF16)

    row = lambda w: pl.BlockSpec((tm, w), lambda i: (nt - 1 - i, 0))
    return _call(
        comm, body, name="mix_bwd", grid=(nt,),
        in_specs=[row(D_MODEL), row(GATES_W), row(D_MODEL), row(D_MODEL), row(C3_W),
                  pl.BlockSpec((halo, C3_W), lambda i: (jnp.maximum((nt - 1 - i) * (tm // halo) - 1, 0), 0)),
                  row(ATTN_W), row(CONV_W), row(D_MODEL), _resident((3, CONV_W)),
                  _resident((ATTN_W + CONV_W, D_MODEL)), _resident((D_MODEL, D_MODEL))],
        out_specs=[row(ATTN_W), row(C3_W), row(GATES_W), pl.BlockSpec((3, CONV_W), lambda i: (0, 0)),
                   _resident((ATTN_W + CONV_W, D_MODEL)), _resident((D_MODEL, D_MODEL))],
        out_shape=[jax.ShapeDtypeStruct((s, ATTN_W), BF16), jax.ShapeDtypeStruct((s, C3_W), BF16),
                   jax.ShapeDtypeStruct((s, GATES_W), BF16), jax.ShapeDtypeStruct((3, CONV_W), F32),
                   jax.ShapeDtypeStruct((ATTN_W + CONV_W, D_MODEL), BF16),
                   jax.ShapeDtypeStruct((D_MODEL, D_MODEL), BF16)],
        scratch_shapes=[pltpu.VMEM((8, CONV_W), F32), pltpu.VMEM((ATTN_W + CONV_W, D_MODEL), F32),
                        pltpu.VMEM((D_MODEL, D_MODEL), F32)],
        compiler_params=_params("arbitrary"),
    )(dh1, gates, a, cv, c3, c3, attn, conv, merged, conv_w, w_br, w_out)


def _attn_bwd(qkv, sinks, o, do, comm=None):
    s = qkv.shape[0]
    npair = s // (2 * BLOCK)

    def one_block(sk_ref, bias, q, kp, kc, vp, vc, ov, dov, dsk_ref):
        dqs, dks, dvs = [], [], []
        for h in range(N_KV_HEADS):
            hs = slice(h * HEAD_DIM, (h + 1) * HEAD_DIM)
            k2 = jnp.concatenate([kp[:, hs], kc[:, hs]], axis=0)
            v2 = jnp.concatenate([vp[:, hs], vc[:, hs]], axis=0)
            qg, og, dog = _stack_heads(q, h), _stack_heads(ov, h), _stack_heads(dov, h)
            sc = lax.dot_general(k2, qg, NT, preferred_element_type=F32) * ATTN_SCALE + bias
            sink = _sink_row(sk_ref, h)
            m = jnp.maximum(jnp.max(sc, axis=0, keepdims=True), sink)
            p = jnp.exp(sc - m)
            psink = jnp.exp(sink - m)
            inv = 1.0 / (jnp.sum(p, axis=0, keepdims=True) + psink)
            p = p * inv
            delta = jnp.sum(dog.astype(F32) * og.astype(F32), axis=1, keepdims=True).T
            dp = lax.dot_general(v2, dog, NT, preferred_element_type=F32)
            ds = (p * (dp - delta)).astype(BF16)
            dqs.append((lax.dot_general(k2, ds, TN, preferred_element_type=F32) * ATTN_SCALE).T)
            dks.append(jnp.dot(ds, qg, preferred_element_type=F32) * ATTN_SCALE)
            dvs.append(jnp.dot(p.astype(BF16), dog, preferred_element_type=F32))
            dsink = -(psink * inv * delta)
            for g in range(GROUP):
                r = h * GROUP + g
                dsk_ref[r:r + 1, :] += jnp.sum(dsink[:, g * BLOCK:(g + 1) * BLOCK])
        return _unstack_heads(dqs), jnp.concatenate(dks, axis=1), jnp.concatenate(dvs, axis=1)

    def body(sk_ref, bias0_ref, bias1_ref, q_ref, kp_ref, kc_ref, vp_ref, vc_ref, o_ref, do_ref,
             dq_ref, dke_ref, dko_ref, dve_ref, dvo_ref, dsk_ref, ck_ref, cvv_ref):
        i = pl.program_id(0)

        @pl.when(i == 0)
        def _():
            for ref in (ck_ref, cvv_ref, dsk_ref):
                ref[...] = jnp.zeros_like(ref)

        @pl.when(i < npair)
        def _():
            kc, vc = kc_ref[...], vc_ref[...]
            first, second = slice(0, BLOCK), slice(BLOCK, 2 * BLOCK)
            dq0, dk0, dv0 = one_block(sk_ref, bias0_ref[...], q_ref[first, :], kp_ref[...], kc[first], vp_ref[...],
                                      vc[first], o_ref[first, :], do_ref[first, :], dsk_ref)
            dq1, dk1, dv1 = one_block(sk_ref, bias1_ref[...], q_ref[second, :], kc[first], kc[second], vc[first],
                                      vc[second], o_ref[second, :], do_ref[second, :], dsk_ref)
            dq_ref[first, :] = dq0.astype(BF16)
            dq_ref[second, :] = dq1.astype(BF16)
            dko_ref[...] = (ck_ref[...] + dk0[:BLOCK]).astype(BF16)
            dvo_ref[...] = (cvv_ref[...] + dv0[:BLOCK]).astype(BF16)
            dke_ref[...] = (dk0[BLOCK:] + dk1[:BLOCK]).astype(BF16)
            dve_ref[...] = (dv0[BLOCK:] + dv1[:BLOCK]).astype(BF16)
            ck_ref[...] = dk1[BLOCK:]
            cvv_ref[...] = dv1[BLOCK:]

        @pl.when(i == npair)
        def _():
            dko_ref[...] = ck_ref[...].astype(BF16)
            dvo_ref[...] = cvv_ref[...].astype(BF16)

    cur = lambda i: jnp.minimum(i, npair - 1)
    done = lambda i: jnp.maximum(i - 1, 0)
    rows = pl.BlockSpec((2 * BLOCK, ATTN_W), lambda i: (cur(i), 0))
    even = pl.BlockSpec((BLOCK, KV_W), lambda i: (cur(i), 0))
    odd = pl.BlockSpec((BLOCK, KV_W), lambda i: (done(i), 0))
    half = jax.ShapeDtypeStruct((s // 2, KV_W), BF16)
    return _call(
        comm, body, name="attn_bwd", grid=(npair + 1,),
        in_specs=[pl.BlockSpec(memory_space=pltpu.SMEM), *_attn_bias_specs(), *_pair_specs(npair), rows, rows],
        out_specs=[rows, even, odd, even, odd, pl.BlockSpec((N_HEADS, 128), lambda i: (0, 0))],
        out_shape=[jax.ShapeDtypeStruct((s, ATTN_W), BF16), half, half, half, half,
                   jax.ShapeDtypeStruct((N_HEADS, 128), F32)],
        scratch_shapes=[pltpu.VMEM((BLOCK, KV_W), F32), pltpu.VMEM((BLOCK, KV_W), F32)],
        compiler_params=_params("arbitrary"),
    )(sinks, _attn_bias(), _attn_bias(), qkv, qkv, qkv, qkv, qkv, o, do)


def _inproj_bwd(dq, dk, dv, dc3, dgt, w_in, x, xn, dh1, g1):
    s = x.shape[0]
    tm = min(2 * BLOCK, s)
    nt = s // tm

    def body(dq_ref, dke_ref, dko_ref, dve_ref, dvo_ref, dc3_ref, dgt_ref, w_ref, x_ref, xn_ref, dh1_ref, g_ref,
             dx_ref, gw_ref, gb_ref, gg_ref, acc_ref):
        i = pl.program_id(0)

        @pl.when(i == 0)
        def _():
            for ref in (gb_ref, gg_ref, acc_ref):
                ref[...] = jnp.zeros_like(ref)

        dk = jnp.concatenate([dke_ref[...], dko_ref[...]], axis=0)
        dv = jnp.concatenate([dve_ref[...], dvo_ref[...]], axis=0)
        dp = jnp.concatenate([dq_ref[...], dk, dv, dc3_ref[...], dgt_ref[...]], axis=1)
        acc_ref[...] += lax.dot_general(dp, xn_ref[...], TN, preferred_element_type=F32)
        gb_ref[...] += jnp.sum(dp.astype(F32), axis=0, keepdims=True)
        dxn = jnp.dot(dp, w_ref[...], preferred_element_type=F32)
        xf = x_ref[...]
        dx, dg = _rms_bwd(dxn, xf, _rstd(xf), g_ref[...])
        dx_ref[...] = dh1_ref[...] + dx
        gg_ref[...] += jnp.sum(dg, axis=0, keepdims=True)

        @pl.when(i == nt - 1)
        def _():
            gw_ref[...] = acc_ref[...].astype(BF16)

    row = lambda w: pl.BlockSpec((tm, w), lambda i: (i, 0))
    acc = lambda w: pl.BlockSpec((1, w), lambda i: (0, 0))
    block = pl.BlockSpec((tm // 2, KV_W), lambda i: (i, 0))
    return pl.pallas_call(
        body, name="inproj_bwd", grid=(nt,),
        in_specs=[row(ATTN_W), block, block, block, block, row(C3_W), row(GATES_W), _resident((IN_W, D_MODEL)),
                  row(D_MODEL), row(D_MODEL), row(D_MODEL), _resident((1, D_MODEL))],
        out_specs=[row(D_MODEL), _resident((IN_W, D_MODEL)), acc(IN_W), acc(D_MODEL)],
        out_shape=[jax.ShapeDtypeStruct((s, D_MODEL), F32), jax.ShapeDtypeStruct((IN_W, D_MODEL), BF16),
                   jax.ShapeDtypeStruct((1, IN_W), F32), jax.ShapeDtypeStruct((1, D_MODEL), F32)],
        scratch_shapes=[pltpu.VMEM((IN_W, D_MODEL), F32)],
        compiler_params=_params("arbitrary"),
    )(dq, *dk, *dv, dc3, dgt, w_in, x, xn, dh1, g1)


def _wgrad(a, b, bm, bn, bk, name, comm=None):
    s, m = a.shape
    n = b.shape[1]
    nk = s // bk

    def body(a_ref, b_ref, o_ref, acc_ref):
        k = pl.program_id(2)

        @pl.when(k == 0)
        def _():
            acc_ref[...] = jnp.zeros_like(acc_ref)

        acc_ref[...] += lax.dot_general(a_ref[...].astype(BF16), b_ref[...].astype(BF16), TN,
                                        preferred_element_type=F32)

        @pl.when(k == nk - 1)
        def _():
            o_ref[...] = acc_ref[...].astype(BF16)

    return _call(
        comm, body, name=name, grid=(m // bm, n // bn, nk),
        in_specs=[pl.BlockSpec((bk, bm), lambda i, j, k: (k, i)), pl.BlockSpec((bk, bn), lambda i, j, k: (k, j))],
        out_specs=pl.BlockSpec((bm, bn), lambda i, j, k: (i, j)),
        out_shape=jax.ShapeDtypeStruct((m, n), BF16),
        scratch_shapes=[pltpu.VMEM((bm, bn), F32)],
        compiler_params=_params("parallel", "parallel", "arbitrary"),
    )(a, b)


class _Carry:
    def __init__(self, jobs, reads=None, bufs=None, fresh=None):
        self.jobs, self.reads, self.bufs, self.fresh = jobs, reads or {}, bufs or {}, fresh or {}
        self.out = {}


class _Job:
    def __init__(self, n_sems, plan):
        self.n_sems, self.plan = n_sems, plan


def _plan_all(jobs, hbm, send, recv):
    pos = _position()
    starts, waits, base = [], [], 0
    for job in jobs:
        s, w = job.plan(hbm, pos, send, recv, base)
        starts, waits, base = starts + s, waits + w, base + job.n_sems
    return starts, waits


def _call(comm, body, **kw):
    if comm is None:
        return pl.pallas_call(body, **kw)
    grid = kw["grid"]
    single = not isinstance(kw["out_shape"], (list, tuple))
    out_shape = [kw["out_shape"]] if single else list(kw["out_shape"])
    out_specs = [kw["out_specs"]] if single else list(kw["out_specs"])
    in_specs = list(kw["in_specs"])
    scratch = list(kw.get("scratch_shapes", ()))
    r_names, b_names, f_names = list(comm.reads), list(comm.bufs), list(comm.fresh)
    n_args, n_out, n_scr = len(in_specs), len(out_shape), len(scratch)
    n_sems = sum(j.n_sems for j in comm.jobs)

    def wrapped(*refs):
        k = n_args
        hbm = dict(zip(r_names, refs[k:k + len(r_names)]))
        k += len(r_names) + len(b_names)
        outs = refs[k:k + n_out]
        k += n_out
        hbm.update(zip(b_names + f_names, refs[k:k + len(b_names) + len(f_names)]))
        k += len(b_names) + len(f_names)
        send, recv = refs[k + n_scr:]
        starts, waits = _plan_all(comm.jobs, hbm, send, recv)
        ids = [pl.program_id(a) for a in range(len(grid))]
        first = functools.reduce(jnp.logical_and, [i == 0 for i in ids])
        last = functools.reduce(jnp.logical_and, [i == g - 1 for i, g in zip(ids, grid)])

        @pl.when(first)
        def _():
            for cp in starts:
                cp.start()

        body(*refs[:n_args], *outs, *refs[k:k + n_scr])

        @pl.when(last)
        def _():
            for cp in waits:
                cp.wait_recv()
            for cp in starts:
                cp.wait_send()

    sems = pltpu.SemaphoreType.DMA((n_sems,))
    held = [jax.ShapeDtypeStruct(a.shape, a.dtype) for a in comm.bufs.values()] + list(comm.fresh.values())
    call = pl.pallas_call(
        wrapped, name=kw["name"], grid=grid,
        in_specs=in_specs + [_ANY] * (len(r_names) + len(b_names)),
        out_specs=out_specs + [_ANY] * len(held),
        out_shape=out_shape + held,
        input_output_aliases={n_args + len(r_names) + i: n_out + i for i in range(len(b_names))},
        scratch_shapes=scratch + [sems, sems],
        compiler_params=_params(*["arbitrary"] * len(grid)),
    )

    def run(*args):
        res = call(*args, *comm.reads.values(), *comm.bufs.values())
        comm.out = dict(zip(b_names + f_names, res[n_out:]))
        return res[0] if single else res[:n_out]

    return run


def _exchange(name, phases, reads=None, bufs=None, fresh=None):
    comm = _Carry([j for ph in phases for j in ph], reads, bufs, fresh)
    r_names, b_names, f_names = list(comm.reads), list(comm.bufs), list(comm.fresh)
    n_sems = sum(j.n_sems for j in comm.jobs)

    def body(*refs):
        hbm = dict(zip(r_names, refs[:len(r_names)]))
        k = len(r_names) + len(b_names)
        hbm.update(zip(b_names + f_names, refs[k:k + len(b_names) + len(f_names)]))
        send, recv = refs[-2:]
        pos = _position()
        started, base = [], 0
        for ph in phases:
            waits = []
            for job in ph:
                s, w = job.plan(hbm, pos, send, recv, base)
                base += job.n_sems
                for cp in s:
                    cp.start()
                started, waits = started + s, waits + w
            for cp in waits:
                cp.wait_recv()
        for cp in started:
            cp.wait_send()

    sems = pltpu.SemaphoreType.DMA((n_sems,))
    held = [jax.ShapeDtypeStruct(a.shape, a.dtype) for a in comm.bufs.values()] + list(comm.fresh.values())
    res = pl.pallas_call(
        body, name=name, in_specs=[_ANY] * (len(r_names) + len(b_names)), out_specs=[_ANY] * len(held),
        out_shape=held, input_output_aliases={len(r_names) + i: i for i in range(len(b_names))},
        scratch_shapes=[sems, sems],
    )(*comm.reads.values(), *comm.bufs.values())
    return dict(zip(b_names + f_names, res))


_HBM = pl.BlockSpec(memory_space=pltpu.HBM)
_SEM = pl.BlockSpec(memory_space=pltpu.SEMAPHORE)
_EFFECT = pltpu.SideEffectType.DATAFLOW_SIDE_EFFECTING


def _start_exchanges(name, groups):
    names = [list(arrays) for _, arrays in groups]
    first = [sum(len(ns) for ns in names[:g]) for g in range(len(groups))]
    n, ng = sum(len(ns) for ns in names), len(groups)

    def body(*refs):
        for g, (jobs, _) in enumerate(groups):
            hbm = dict(zip(names[g], refs[first[g]:first[g] + len(names[g])]))
            for cp in _plan_all(jobs, hbm, refs[n + 2 * g], refs[n + 2 * g + 1])[0]:
                cp.start()
        refs[-1][...] = jnp.zeros_like(refs[-1])

    given = [pltpu.with_memory_space_constraint(
        a if isinstance(a, jax.Array) else lax.empty(a.shape, a.dtype), pltpu.HBM)
        for _, arrays in groups for a in arrays.values()]
    sems = [pltpu.SemaphoreType.DMA((sum(j.n_sems for j in jobs),)) for jobs, _ in groups for _ in range(2)]
    res = pl.pallas_call(
        body, name=name,
        out_shape=(*sems, *[pltpu.HBM(a.shape, a.dtype) for a in given], jax.ShapeDtypeStruct((8, 128), F32)),
        in_specs=[_HBM] * n, out_specs=(*[_SEM] * (2 * ng), *[_HBM] * n, pl.BlockSpec(memory_space=pltpu.VMEM)),
        input_output_aliases={i: 2 * ng + i for i in range(n)},
        compiler_params=pltpu.CompilerParams(has_side_effects=_EFFECT),
    )(*given)
    held = res[2 * ng:2 * ng + n]
    states = [(names[g], groups[g][0], res[2 * g], res[2 * g + 1], held[first[g]:first[g] + len(names[g])])
              for g in range(ng)]
    return states, res[-1]


def _start_exchange(name, jobs, arrays):
    states, token = _start_exchanges(name, [(jobs, arrays)])
    return states[0], token


def _finish_exchange(name, state, after):
    names, jobs, send_sem, recv_sem, held = state
    n = len(names)

    def body(*refs):
        hbm = dict(zip(names, refs[:n]))
        send, recv = refs[n:n + 2]
        starts, waits = _plan_all(jobs, hbm, send, recv)
        for cp in waits:
            cp.wait_recv()
        for cp in starts:
            cp.wait_send()

    res = pl.pallas_call(
        body, name=name, out_shape=tuple(pltpu.HBM(a.shape, a.dtype) for a in held),
        in_specs=[_HBM] * n + [_SEM, _SEM, _ANY], out_specs=tuple([_HBM] * n),
        input_output_aliases={i: i for i in range(n)},
        compiler_params=pltpu.CompilerParams(has_side_effects=_EFFECT),
    )(*held, send_sem, recv_sem, after)
    return dict(zip(names, res))


def _row_tile(rows, bytes_per_row):
    best = 16
    for t in range(16, rows + 1, 16):
        if rows % t == 0 and t * bytes_per_row <= 9 * 1024 * 1024:
            best = t
    return best


def _rowwise(fn, ins, out_dtypes, name, after=None):
    rows, cols = ins[0].shape
    per_row = sum(cols * a.dtype.itemsize for a in ins) + sum(cols * jnp.dtype(d).itemsize for d in out_dtypes)
    tr = _row_tile(rows, per_row)
    n_in = len(ins)

    def body(*refs):
        outs = fn(*[r[...] for r in refs[:n_in]])
        for o_ref, o in zip(refs[-len(out_dtypes):], outs):
            o_ref[...] = o.astype(o_ref.dtype)

    tile = pl.BlockSpec((tr, cols), lambda i: (i, 0))
    behind = [] if after is None else [after]
    return pl.pallas_call(
        body, name=name, grid=(rows // tr,),
        in_specs=[tile] * n_in + [pl.BlockSpec((8, 128), lambda i: (0, 0))] * len(behind),
        out_specs=[tile] * len(out_dtypes),
        out_shape=[jax.ShapeDtypeStruct((rows, cols), d) for d in out_dtypes],
        compiler_params=_params("parallel"),
    )(*ins, *behind)


def _tiled(fn, name, grid, pos, ins, outs):
    n_in = len(ins)

    def body(pos_ref, *refs):
        res = fn(*[r[...] for r in refs[:n_in]])
        for o_ref, o in zip(refs[n_in:], res):
            o_ref[...] = o.astype(o_ref.dtype)

    return pl.pallas_call(
        body, name=name,
        grid_spec=pltpu.PrefetchScalarGridSpec(
            num_scalar_prefetch=1, grid=grid,
            in_specs=[pl.BlockSpec(bs, im) for _, bs, im in ins],
            out_specs=[pl.BlockSpec(bs, im) for _, _, bs, im in outs]),
        out_shape=[jax.ShapeDtypeStruct(s, d) for s, d, _, _ in outs],
        compiler_params=_params("parallel"),
    )(pos, *[a for a, _, _ in ins])


def _adamw(w, g, m, v):
    m = ADAM_B1 * m + (1.0 - ADAM_B1) * g
    v = ADAM_B2 * v + (1.0 - ADAM_B2) * (g * g)
    m_hat = m / (1.0 - ADAM_B1 ** ADAM_STEP)
    v_hat = v / (1.0 - ADAM_B2 ** ADAM_STEP)
    return -ADAM_LR * (m_hat / (jnp.sqrt(v_hat) + ADAM_EPS) + ADAM_WD * w), m, v


def _adamw_small(params):
    n = len(params)

    def body(*refs):
        for k in range(n):
            w, g, m, v = (r[...] for r in refs[4 * k:4 * k + 4])
            for o_ref, o in zip(refs[4 * n + 3 * k:4 * n + 3 * k + 3], _adamw(w, g, m, v)):
                o_ref[...] = o

    flat = [a for p in params for a in p]
    return pl.pallas_call(
        body, name="adamw_small",
        out_shape=[jax.ShapeDtypeStruct(p[0].shape, F32) for p in params for _ in range(3)],
    )(*flat)


class _Layout:
    def __init__(self, rows, cols, stacked):
        self.rows, self.cols, self.stacked = rows, cols, stacked

    def whole(self, rows=None):
        r = self.rows if rows is None else rows
        return (N_CHIPS, r, self.cols) if self.stacked else (r, N_CHIPS * self.cols)

    def part_rows(self, h, q=0, nq=1):
        n = self.rows // 2 // nq
        return pl.ds(pl.multiple_of(h * (self.rows // 2) + q * n, 16), n)

    def half_rows(self, h):
        return self.part_rows(h)

    def block(self, ref, p, rows=slice(None)):
        if self.stacked:
            return ref.at[p, rows, :]
        return ref.at[rows, pl.ds(pl.multiple_of(p * self.cols, 128), self.cols)]

    def all_chips(self, ref, rows):
        return ref.at[:, rows, :] if self.stacked else ref.at[rows, :]


BIG = (
    _Layout(IN_SHARD, D_MODEL, True),
    _Layout(ATTN_W + CONV_W, D_MODEL // N_CHIPS, False),
    _Layout(D_MODEL // N_CHIPS, D_MODEL, True),
    _Layout(D_MODEL, FF2 // N_CHIPS, False),
    _Layout(D_FF // N_CHIPS, D_MODEL, True),
)
N_BIG = len(BIG)
_ANY = pl.BlockSpec(memory_space=pl.ANY)


def _position():
    x, y, c = lax.axis_index("x"), lax.axis_index("y"), lax.axis_index("c")
    return x, y, c, 2 * x + y


def _core_of_chip(p, c):
    return (p >> 1, p & 1, c)


def _place_cast(shard, lay, pos, name, after=None):
    rows, cols = shard.shape
    tr = _row_tile(rows, cols * 6)
    if lay.stacked:
        out = (lay.whole(), BF16, (None, tr, cols), lambda i, pos: (pos[0], i, 0))
    else:
        out = (lay.whole(), BF16, (tr, cols), lambda i, pos: (i, pos[0]))
    ins = [(shard, (tr, cols), lambda i, pos: (i, 0))]
    if after is not None:
        ins.append((after, (8, 128), lambda i, pos: (0, 0)))
    return _tiled(lambda a, *_: (a,), name, (rows // tr,), pos, ins, [out])[0]


def _place_cast_pair(top, bottom, lay, pos, name, after=None):
    rows, cols = top.shape
    ins = [(top, (rows, cols), lambda i, pos: (0, 0)), (bottom, (rows, cols), lambda i, pos: (0, 0))]
    if after is not None:
        ins.append((after, (8, 128), lambda i, pos: (0, 0)))
    return _tiled(lambda a, b, *_: (jnp.concatenate([a, b], axis=0),), name, (1,), pos, ins,
                  [(lay.whole(), BF16, (2 * rows, cols), lambda i, pos: (0, pos[0]))])[0]


def _adamw_pair(top, bottom, g, after=None):
    rows = top[0].shape[0]

    def body(*refs):
        (wa, ma, va, wb, mb, vb, g_ref), outs = refs[:7], refs[-8:]
        for (w, m, v), gg, o in (((wa, ma, va), g_ref[:rows], outs[:4]), ((wb, mb, vb), g_ref[rows:], outs[4:])):
            for o_ref, val in zip(o, (gg, *_adamw(w[...], gg, m[...], v[...]))):
                o_ref[...] = val

    behind = [] if after is None else [after[0:8, 0:128]]
    res = pl.pallas_call(
        body, name="adamw_w_br", out_shape=[jax.ShapeDtypeStruct(top[0].shape, F32)] * 8,
    )(*top, *bottom, g, *behind)
    return res[:4], res[4:]


def _remote(src, dst, send, recv, k, device):
    return pltpu.make_async_remote_copy(src_ref=src, dst_ref=dst, send_sem=send.at[k], recv_sem=recv.at[k],
                                        device_id=device, device_id_type=MESH)


def _arrival(dst, send, recv, k, me):
    return _remote(dst, dst, send, recv, k, me)


def _gather_ici(lay, name, q=0, nq=1):
    def plan(hbm, pos, send, recv, base):
        x, y, c, me = pos
        rows = lay.part_rows(c, q, nq)
        mine = lay.block(hbm[name], me, rows)
        starts = [_remote(mine, mine, send, recv, base + d - 1, _core_of_chip(me ^ d, c)) for d in (1, 2, 3)]
        waits = [_arrival(lay.block(hbm[name], me ^ d, rows), send, recv, base + d - 1, (x, y, c)) for d in (1, 2, 3)]
        return starts, waits
    return _Job(3, plan)


def _gather_d2d(lay, name, q=0, nq=1):
    def plan(hbm, pos, send, recv, base):
        x, y, c, me = pos
        starts, waits = [], []
        for d in (1, 2, 3):
            got = lay.block(hbm[name], me ^ d, lay.part_rows(c, q, nq))
            starts.append(_remote(got, got, send, recv, base + d - 1, (x, y, 1 - c)))
            waits.append(_arrival(lay.block(hbm[name], me ^ d, lay.part_rows(1 - c, q, nq)), send, recv, base + d - 1,
                                  (x, y, c)))
        return starts, waits
    return _Job(3, plan)


def _rs_pair(lay, grad, theirs):
    def plan(hbm, pos, send, recv, base):
        x, y, c, _ = pos
        out = _remote(lay.all_chips(hbm[grad], lay.half_rows(1 - c)), hbm[theirs], send, recv, base, (x, y, 1 - c))
        return [out], [_arrival(hbm[theirs], send, recv, base, (x, y, c))]
    return _Job(1, plan)


def _rs_chips(lay, sums, slots):
    def plan(hbm, pos, send, recv, base):
        x, y, c, me = pos
        starts = [_remote(lay.block(hbm[sums], me ^ d), hbm[slots].at[me], send, recv, base + d - 1,
                          _core_of_chip(me ^ d, c)) for d in (1, 2, 3)]
        waits = [_arrival(hbm[slots].at[me ^ d], send, recv, base + d - 1, (x, y, c)) for d in (1, 2, 3)]
        return starts, waits
    return _Job(3, plan)


def _rs_share(lay, shard):
    def plan(hbm, pos, send, recv, base):
        x, y, c, _ = pos
        mine = hbm[shard].at[lay.half_rows(c), :]
        other = hbm[shard].at[lay.half_rows(1 - c), :]
        return [_remote(mine, mine, send, recv, base, (x, y, 1 - c))], [_arrival(other, send, recv, base, (x, y, c))]
    return _Job(1, plan)


def _slots_shape(lay):
    return jax.ShapeDtypeStruct((N_CHIPS, lay.rows // 2, lay.cols), BF16)


def _theirs_shape(lay, dtype=BF16):
    return jax.ShapeDtypeStruct(lay.whole(lay.rows // 2), dtype)


def _pair_sum(grad, theirs, lay, pos, name):
    half = lay.rows // 2
    add = lambda a, b: (a.astype(F32) + b.astype(F32),)
    if lay.stacked:
        tr = _row_tile(half, lay.cols * 6)
        nt = half // tr
        flat = lambda a: a.reshape(-1, lay.cols)
        mine = lambda t, pos: ((t // nt) * (2 * nt) + pos[1] * nt + t % nt, 0)
        grid, blk = (N_CHIPS * nt,), (tr, lay.cols)
        grad, theirs = flat(grad), flat(theirs)
    else:
        tr = _row_tile(half, N_CHIPS * lay.cols * 6)
        nt = half // tr
        mine = lambda t, pos: (pos[1] * nt + t, 0)
        grid, blk = (nt,), (tr, N_CHIPS * lay.cols)
    same = lambda t, pos: (t, 0)
    out = _tiled(add, name, grid, pos, [(grad, blk, mine), (theirs, blk, same)], [(theirs.shape, BF16, blk, same)])[0]
    return out.reshape(lay.whole(half))


def _chip_sum(sums, slots, lay, pos, name, after=None):
    half = lay.rows // 2
    tr = _row_tile(half, lay.cols * 12)
    nt = half // tr
    blk3 = (None, tr, lay.cols)
    if lay.stacked:
        own = (sums, blk3, lambda i, pos: (pos[0], i, 0))
    else:
        own = (sums, (tr, lay.cols), lambda i, pos: (i, pos[0]))
    others = [(slots, blk3, functools.partial(lambda d, i, pos: (pos[0] ^ d, i, 0), d)) for d in (1, 2, 3)]

    def add(a, b1, b2, b3, *_):
        return (((a.astype(F32) + b1.astype(F32)) + b2.astype(F32)) + b3.astype(F32),)

    if after is not None:
        others.append((after, (8, 128), lambda i, pos: (0, 0)))
    return _tiled(add, name, (nt,), pos, [own] + others,
                  [((lay.rows, lay.cols), F32, (tr, lay.cols), lambda i, pos: (pos[1] * nt + i, 0))])[0]


N_DEV = 8


def _to_all(src, slots):
    def plan(hbm, pos, send, recv, base):
        x, y, c, _ = pos
        idx = 4 * x + 2 * y + c
        starts = [_remote(hbm[src], hbm[slots].at[idx], send, recv, base + k - 1,
                          (x ^ (k >> 2), y ^ ((k >> 1) & 1), c ^ (k & 1))) for k in range(1, N_DEV)]
        waits = [_arrival(hbm[slots].at[idx ^ k], send, recv, base + k - 1, (x, y, c)) for k in range(1, N_DEV)]
        return starts, waits
    return _Job(N_DEV - 1, plan)


def _sum_slots(own, slots, pos):
    def body(pos_ref, own_ref, slots_ref, o_ref):
        idx = 2 * pos_ref[0] + pos_ref[1]
        term = lambda q: jnp.where(idx == q, own_ref[...], slots_ref[q])
        acc = term(0)
        for q in range(1, N_DEV):
            acc = acc + term(q)
        o_ref[...] = acc

    return pl.pallas_call(
        body, name="sum_small", out_shape=jax.ShapeDtypeStruct(own.shape, F32),
        in_specs=[pl.BlockSpec(memory_space=pltpu.SMEM), pl.BlockSpec(memory_space=pltpu.VMEM),
                  pl.BlockSpec(memory_space=pltpu.VMEM)],
    )(pos, own, slots)


def _pack_rows(parts):
    padded = [jnp.pad(a, ((0, -a.shape[0] % 8), (0, 0))) for a in parts]
    starts = [sum(p.shape[0] for p in padded[:k]) for k in range(len(padded))]
    return jnp.concatenate(padded, axis=0), starts


def kernel(x, mix_norm, w_in, b_in, sinks, conv_w, w_attn_branch, w_conv_branch, w_out, ffn_norm, w_up, ffn_conv_w, w_down, final_norm, loss_target, m_mix_norm, m_w_in, m_b_in, m_sinks, m_conv_w, m_w_attn_branch, m_w_conv_branch, m_w_out, m_ffn_norm, m_w_up, m_ffn_conv_w, m_w_down, m_final_norm, v_mix_norm, v_w_in, v_b_in, v_sinks, v_conv_w, v_w_attn_branch, v_w_conv_branch, v_w_out, v_ffn_norm, v_w_up, v_ffn_conv_w, v_w_down, v_final_norm):
    me = 2 * lax.axis_index("x") + lax.axis_index("y")
    names = ("w_in", "w_br", "w_out", "w_up", "w_down")
    w_of = dict(w_in=w_in[0].T, w_out=w_out[0], w_up=w_up[0], w_down=w_down[0])
    m_of = dict(w_in=m_w_in[0].T, w_out=m_w_out[0], w_up=m_w_up[0], w_down=m_w_down[0])
    v_of = dict(w_in=v_w_in[0].T, w_out=v_w_out[0], w_up=v_w_up[0], w_down=v_w_down[0])
    ab = (w_attn_branch[0], m_w_attn_branch[0], v_w_attn_branch[0])
    cb = (w_conv_branch[0], m_w_conv_branch[0], v_w_conv_branch[0])

    pos = jnp.stack([me, lax.axis_index("c")]).astype(jnp.int32)

    lay = dict(zip(names, BIG))
    xs, target, sk = x[0], loss_target[0], sinks[0]
    s = xs.shape[0]
    tm, tm2, bk, bk2 = min(256, s), min(512, s), min(1024, s), min(2048, s)

    taps, (_, t0) = _pack_rows([conv_w[0], ffn_conv_w[0].reshape(3 * (FF2 // N_CHIPS // 128), 128)])
    placed = {"w_in": _place_cast(w_of["w_in"], lay["w_in"], pos, "cast_w_in")}
    fly_in, started = _start_exchange("gather_in_start", [_gather_ici(lay["w_in"], "w_in")], {"w_in": placed["w_in"]})
    taps_flight, started = _start_exchange("taps_start", [_to_all("v", "slots")],
                                           {"v": taps + started[0:1], "slots": jnp.zeros((N_DEV, *taps.shape), F32)})
    placed["w_br"] = _place_cast_pair(ab[0], cb[0], lay["w_br"], pos, "cast_w_br", after=started)
    for n in names[2:]:
        placed[n] = _place_cast(w_of[n], lay[n], pos, "cast_" + n, after=started)
    trio = ("w_br", "w_out")
    (fly_trio, fly_up, fly_down), started = _start_exchanges("gather_rest_start", [
        ([_gather_ici(lay[n], n) for n in ws], {n: placed[n] for n in ws}) for ws in (trio, ("w_up",), ("w_down",))])

    got = _finish_exchange("gather_in_wait", fly_in, after=started)
    w_in_full = _exchange("gather_in_d2d", [[_gather_d2d(lay["w_in"], "w_in")]], bufs=got)["w_in"].reshape(IN_W, D_MODEL)
    xn, qkv, c3, gates = _inproj_fwd(xs, mix_norm, w_in_full, b_in, tm2)
    k2 = _Carry([_gather_d2d(lay[n], n) for n in trio], bufs=_finish_exchange("gather_trio_wait", fly_trio, after=qkv))
    attn = _attn_fwd(qkv, sk, comm=k2)
    w_br = k2.out["w_br"]
    w_out_full = k2.out["w_out"].reshape(D_MODEL, D_MODEL)
    k3 = _Carry([_gather_d2d(lay["w_up"], "w_up")], bufs=_finish_exchange("gather_up_wait", fly_up, after=attn))
    taps = _finish_exchange("taps_wait", taps_flight, after=attn)
    taps = lax.dynamic_update_slice(taps["slots"], taps["v"][None], (2 * me + lax.axis_index("c"), 0, 0))
    conv_full = taps[0::2, 0:3].transpose(1, 0, 2).reshape(3, CONV_W)
    ffn_cw_full = taps[0::2, t0:t0 + 33].reshape(N_CHIPS, 3, FF2 // N_CHIPS).transpose(1, 0, 2).reshape(3, FF2)
    conv, a, cv, merged, h1, hn = _mix_fwd(xs, attn, c3, gates, conv_full, w_br, w_out_full, ffn_norm, tm2, comm=k3)
    w_up_full = k3.out["w_up"]
    w_down_full = _exchange("gather_down_d2d", [[_gather_d2d(lay["w_down"], "w_down")]],
                            bufs=_finish_exchange("gather_down_wait", fly_down, after=hn))["w_down"].reshape(D_FF, D_MODEL)
    u, up, act, dh2, loss_part, g_fn = _ffn_fwd_loss(hn, h1, w_up_full, ffn_cw_full, w_down_full,
                                                     final_norm[None, :], target, tm)

    grads, sums, slots = {}, {}, {}

    def pair(*ws):
        return _Carry([_rs_pair(lay[n], "g_" + n, "t_" + n) for n in ws], reads={"g_" + n: grads[n] for n in ws},
                      fresh={"t_" + n: _theirs_shape(lay[n], grads[n].dtype) for n in ws})

    def chips(*ws, also=None):
        k = _Carry([_rs_chips(lay[n], "s_" + n, "r_" + n) for n in ws], reads={"s_" + n: sums[n] for n in ws},
                   fresh={"r_" + n: _slots_shape(lay[n]) for n in ws})
        if also is not None:
            k = _Carry(k.jobs + also.jobs, {**k.reads, **also.reads}, None, {**k.fresh, **also.fresh})
        return k

    def pair_sums(k, *ws):
        for n in ws:
            sums[n] = _pair_sum(grads[n], k.out["t_" + n], lay[n], pos, "pair_sum_" + n)

    def take_slots(k, *ws):
        for n in ws:
            slots[n] = k.out["r_" + n]

    du, dh1, g_fcw, g_g2 = _ffn_bwd(dh2, u, up, h1, w_up_full, ffn_cw_full, w_down_full, ffn_norm, tm)
    grads["w_down"] = _wgrad(act, dh2, D_FF // 2, D_MODEL, bk2, "wgrad_down").reshape(lay["w_down"].whole())
    k4 = pair("w_down")
    grads["w_up"] = _wgrad(hn, du, D_MODEL, FF2 // 4, bk2, "wgrad_up", comm=k4)
    pair_sums(k4, "w_down")
    k5 = chips("w_down", also=pair("w_up"))
    dattn, dc3, dgt, g_cw, grads["w_br"], gw_out = _mix_bwd(
        dh1, gates, a, cv, c3, attn, conv, merged, conv_full, w_br, w_out_full, tm2, comm=k5)
    grads["w_out"] = gw_out.reshape(lay["w_out"].whole())
    take_slots(k5, "w_down")
    pair_sums(k5, "w_up")
    up_flight, started = _start_exchange("rs_chips_up_start", [_rs_chips(lay["w_up"], "s", "r")],
                                         {"s": sums["w_up"], "r": _slots_shape(lay["w_up"])})
    k6 = pair(*trio)
    k6.reads["after"] = started
    dq, dk_even, dk_odd, dv_even, dv_odd, g_sk = _attn_bwd(qkv, sk, attn, dattn, comm=k6)
    pair_sums(k6, *trio)
    trio_flight, started = _start_exchange(
        "rs_chips_trio_start", [_rs_chips(lay[n], "s_" + n, "r_" + n) for n in trio],
        {**{"s_" + n: sums[n] for n in trio}, **{"r_" + n: _slots_shape(lay[n]) for n in trio}})
    behind = mix_norm + jnp.tile(started[0:1], (1, D_MODEL // 128))
    grad_x, gw_in, g_b, g_g1 = _inproj_bwd(dq, (dk_even, dk_odd), (dv_even, dv_odd), dc3, dgt, w_in_full, xs, xn,
                                           dh1, behind)
    grads["w_in"] = gw_in.reshape(lay["w_in"].whole())

    parts = [loss_part, g_g1, g_b, jnp.pad(g_sk[:, 0], (0, 120))[None, :], g_cw, g_g2, g_fcw, g_fn]
    packed, at = _pack_rows([p.reshape(-1, 128) for p in parts])
    small_flight, started = _start_exchange("small_start", [_to_all("v", "slots")],
                                            {"v": packed, "slots": jnp.zeros((N_DEV, *packed.shape), F32)})
    in_flight, started = _start_exchange("rs_pair_in_start", [_rs_pair(lay["w_in"], "g", "t")],
                                         {"g": grads["w_in"], "t": _theirs_shape(lay["w_in"]), "behind": started})
    landed = _finish_exchange("rs_chips_up_wait", up_flight, after=started)
    halves = {"w_up": _chip_sum(landed["s"], landed["r"], lay["w_up"], pos, "chip_sum_w_up"),
              "w_down": _chip_sum(sums["w_down"], slots["w_down"], lay["w_down"], pos, "chip_sum_w_down", after=started)}
    landed = _finish_exchange("rs_pair_in_wait", in_flight, after=halves["w_down"])
    sums["w_in"] = _pair_sum(landed["g"], landed["t"], lay["w_in"], pos, "pair_sum_w_in")
    in_flight, started = _start_exchange("rs_chips_in_start", [_rs_chips(lay["w_in"], "s", "r")],
                                         {"s": sums["w_in"], "r": _slots_shape(lay["w_in"])})
    landed = _finish_exchange("rs_chips_trio_wait", trio_flight, after=started)
    for n in trio:
        halves[n] = _chip_sum(landed["s_" + n], landed["r_" + n], lay[n], pos, "chip_sum_" + n)
    shared = _exchange("share_halves", [[_rs_share(lay[n], n) for n in names[1:]]], bufs=halves)

    def adam(n, g, after=None):
        return _rowwise(lambda w, g, m, v: (g, *_adamw(w, g, m, v)), [w_of[n], g, m_of[n], v_of[n]], [F32] * 4,
                        "adamw_" + n, after=after)

    new_of, last = {}, None
    for n in ("w_up", "w_down", "w_out"):
        new_of[n] = adam(n, shared[n], last)
        last = new_of[n][1]
    new_of["w_ab"], new_of["w_cb"] = _adamw_pair(ab, cb, shared["w_br"], after=last)
    last = new_of["w_cb"][1]

    arrived = _finish_exchange("small_wait", small_flight, after=last)
    total = _sum_slots(arrived["v"], arrived["slots"], pos)
    part = lambda k: total[at[k]:at[k] + parts[k].size // 128].reshape(parts[k].shape)
    loss = total[0, 0]
    g_mix, g_b, g_g2, g_fn = part(1), part(2), part(5), part(7)
    g_sk = part(3)[:, 0:N_HEADS]
    g_cw = lax.dynamic_slice(part(4), (0, me * 128), (3, 128))
    g_fcw = lax.dynamic_slice(part(6), (0, me * (FF2 // N_CHIPS)), (3, FF2 // N_CHIPS))
    small_p = [
        (mix_norm, g_mix, m_mix_norm, v_mix_norm), (b_in, g_b, m_b_in, v_b_in), (sinks, g_sk, m_sinks, v_sinks),
        (conv_w[0], g_cw, m_conv_w[0], v_conv_w[0]), (ffn_norm, g_g2, m_ffn_norm, v_ffn_norm),
        (ffn_conv_w[0], g_fcw, m_ffn_conv_w[0], v_ffn_conv_w[0]),
        (final_norm[None, :], g_fn, m_final_norm[None, :], v_final_norm[None, :])]
    small_new = _adamw_small(small_p)
    small_new = [small_new[3 * k:3 * k + 3] for k in range(len(small_p))]

    landed = _finish_exchange("rs_chips_in_wait", in_flight, after=small_new[0][0])
    half_in = _chip_sum(landed["s"], landed["r"], lay["w_in"], pos, "chip_sum_w_in")
    shared["w_in"] = _exchange("share_in", [[_rs_share(lay["w_in"], "w_in")]], bufs={"w_in": half_in})["w_in"]
    new_of["w_in"] = [a.T for a in adam("w_in", shared["w_in"])]
    big = ("w_in", "w_ab", "w_cb", "w_out", "w_up", "w_down")
    big_g = [new_of[n][0] for n in big]
    big_new = [new_of[n][1:] for n in big]

    order = [("s", 0), ("b", 0), ("s", 1), ("s", 2), ("s", 3), ("b", 1), ("b", 2), ("b", 3), ("s", 4), ("b", 4),
             ("s", 5), ("b", 5), ("s", 6)]
    shapes = [mix_norm.shape, w_in.shape, b_in.shape, sinks.shape, conv_w.shape, w_attn_branch.shape,
              w_conv_branch.shape, w_out.shape, ffn_norm.shape, w_up.shape, ffn_conv_w.shape, w_down.shape,
              final_norm.shape]
    small_g = [p[1] for p in small_p]
    out_g = [(small_g[k] if kind == "s" else big_g[k]).reshape(shp) for (kind, k), shp in zip(order, shapes)]
    news = [[(small_new[k][j] if kind == "s" else big_new[k][j]).reshape(shp) for (kind, k), shp in zip(order, shapes)]
            for j in range(3)]
    return (loss, grad_x[None], *out_g, *news[0], *news[1], *news[2])
```

```python
import functools

import jax
import jax.numpy as jnp
from jax import lax
from jax.experimental import pallas as pl
from jax.experimental.pallas import tpu as pltpu

F32 = jnp.float32
BF16 = jnp.bfloat16

D_MODEL = 1024
HEAD_DIM = 64
N_HEADS = 8
N_KV_HEADS = 2
GROUP = N_HEADS // N_KV_HEADS
BLOCK = 128
ATTN_SCALE = HEAD_DIM ** -0.5
ATTN_W = N_HEADS * HEAD_DIM
KV_W = N_KV_HEADS * HEAD_DIM
CONV_W = 512
QKV_W = ATTN_W + 2 * KV_W
C3_W = 3 * CONV_W
GATES_W = 2 * D_MODEL
IN_W = QKV_W + C3_W + GATES_W
D_FF = 2816
FF2 = 2 * D_FF
NORM_EPS = 1e-5
N_CHIPS = 4
IN_SHARD = IN_W // N_CHIPS
NEG = -1e30

ADAM_LR = 0.001
ADAM_B1 = 0.9
ADAM_B2 = 0.999
ADAM_EPS = 1e-08
ADAM_WD = 0.01
ADAM_STEP = 10

VMEM_LIMIT = 56 * 1024 * 1024
MESH = pl.DeviceIdType.MESH

NT = (((1,), (1,)), ((), ()))
TN = (((0,), (0,)), ((), ()))


def _params(*sem):
    return pltpu.CompilerParams(dimension_semantics=sem, vmem_limit_bytes=VMEM_LIMIT)


def _resident(shape):
    return pl.BlockSpec(shape, lambda *_: (0,) * len(shape), pipeline_mode=pl.Buffered(1))


def _sigmoid(v):
    return 0.5 * jnp.tanh(0.5 * v) + 0.5


def _rstd(v):
    return lax.rsqrt(jnp.mean(v * v, axis=-1, keepdims=True) + NORM_EPS)


def _rms_bwd(dy, v, rstd, g):
    vhat = v * rstd
    t = dy * g
    return rstd * (t - vhat * jnp.mean(t * vhat, axis=-1, keepdims=True)), dy * vhat


def _taps(z, cw):
    return cw[2:3] * z + cw[1:2] * pltpu.roll(z, 1, 0) + cw[0:1] * pltpu.roll(z, 2, 0)


def _causal_conv(z, prev, cw):
    edge = _taps(jnp.concatenate([prev, z[0:8]], axis=0), cw)
    return jnp.concatenate([edge[8:16], _taps(z, cw)[8:]], axis=0)


def _rows_after(z, nxt):
    n = z.shape[0]
    edge = jnp.concatenate([z[n - 8:n], nxt], axis=0)
    return tuple(jnp.concatenate([pltpu.roll(z, n - k, 0)[:n - 8], pltpu.roll(edge, 16 - k, 0)[0:8]], axis=0)
                 for k in (1, 2))


def _inproj_fwd(x, g1, w_in, b_in, tm, comm=None):
    s = x.shape[0]

    def body(x_ref, g_ref, w_ref, b_ref, xn_ref, qkv_ref, c3_ref, gt_ref):
        xf = x_ref[...]
        xn = (xf * _rstd(xf) * g_ref[...]).astype(BF16)
        xn_ref[...] = xn

        proj = (lax.dot_general(xn, w_ref[...], NT, preferred_element_type=F32) + b_ref[...]).astype(BF16)
        qkv_ref[...] = proj[:, :QKV_W]
        c3_ref[...] = proj[:, QKV_W:QKV_W + C3_W]
        gt_ref[...] = proj[:, QKV_W + C3_W:]

    row = lambda w: pl.BlockSpec((tm, w), lambda i: (i, 0))
    return _call(
        comm, body, name="inproj_fwd", grid=(s // tm,),
        in_specs=[row(D_MODEL), _resident((1, D_MODEL)), _resident((IN_W, D_MODEL)), _resident((1, IN_W))],
        out_specs=[row(D_MODEL), row(QKV_W), row(C3_W), row(GATES_W)],
        out_shape=[jax.ShapeDtypeStruct((s, D_MODEL), BF16), jax.ShapeDtypeStruct((s, QKV_W), BF16),
                   jax.ShapeDtypeStruct((s, C3_W), BF16), jax.ShapeDtypeStruct((s, GATES_W), BF16)],
        compiler_params=_params("parallel"),
    )(x, g1, w_in, b_in)


def _attn_bias():
    kj = jnp.arange(2 * BLOCK)[:, None]
    qi = (jnp.arange(GROUP * BLOCK) % BLOCK)[None, :]
    band = (kj > qi) & (kj <= qi + BLOCK)
    return jnp.stack([jnp.where(band & (kj >= BLOCK), 0.0, NEG), jnp.where(band, 0.0, NEG)]).astype(F32)


def _attn_bias_specs():
    shape = (None, 2 * BLOCK, GROUP * BLOCK)
    return pl.BlockSpec(shape, lambda i: (jnp.minimum(i, 1), 0, 0)), pl.BlockSpec(shape, lambda i: (1, 0, 0))


def _sink_row(sk_ref, h):
    lane = lax.broadcasted_iota(jnp.int32, (1, GROUP * BLOCK), 1)
    row = jnp.full((1, GROUP * BLOCK), sk_ref[h * GROUP], F32)
    for g in range(1, GROUP):
        row = jnp.where(lane >= g * BLOCK, sk_ref[h * GROUP + g], row)
    return row


def _stack_heads(t, h):
    return jnp.concatenate(
        [t[:, (h * GROUP + g) * HEAD_DIM:(h * GROUP + g + 1) * HEAD_DIM] for g in range(GROUP)], axis=0)


def _unstack_heads(per_kv):
    return jnp.concatenate(
        [t[g * BLOCK:(g + 1) * BLOCK] for t in per_kv for g in range(GROUP)], axis=1)


def _block_specs(n, steps):
    cur = lambda i: jnp.minimum(i, steps - 1)
    prev = lambda i: jnp.maximum(n * jnp.minimum(i, steps - 1) - 1, 0)
    kv = ATTN_W // KV_W
    return (pl.BlockSpec((n * BLOCK, ATTN_W), lambda i: (cur(i), 0)),
            pl.BlockSpec((BLOCK, KV_W), lambda i: (prev(i), kv)), pl.BlockSpec((n * BLOCK, KV_W), lambda i: (cur(i), kv)),
            pl.BlockSpec((BLOCK, KV_W), lambda i: (prev(i), kv + 1)),
            pl.BlockSpec((n * BLOCK, KV_W), lambda i: (cur(i), kv + 1)))


def _attn_fwd(qkv, sinks, comm=None):
    s = qkv.shape[0]
    n = min(4, s // BLOCK)
    steps = s // (n * BLOCK)

    def body(sk_ref, bias0_ref, bias1_ref, q_ref, kp_ref, kc_ref, vp_ref, vc_ref, o_ref):
        kc, vc = kc_ref[...], vc_ref[...]
        for b in range(n):
            rows, before = slice(b * BLOCK, (b + 1) * BLOCK), slice((b - 1) * BLOCK, b * BLOCK)
            kp, vp = (kp_ref[...], vp_ref[...]) if b == 0 else (kc[before], vc[before])
            q, bias = q_ref[rows, :], (bias0_ref if b == 0 else bias1_ref)[...]
            outs = []
            for h in range(N_KV_HEADS):
                hs = slice(h * HEAD_DIM, (h + 1) * HEAD_DIM)
                k2 = jnp.concatenate([kp[:, hs], kc[rows, hs]], axis=0)
                v2 = jnp.concatenate([vp[:, hs], vc[rows, hs]], axis=0)
                sc = lax.dot_general(k2, _stack_heads(q, h), NT, preferred_element_type=F32) * ATTN_SCALE + bias
                sink = _sink_row(sk_ref, h)
                m = jnp.maximum(jnp.max(sc, axis=0, keepdims=True), sink)
                p = jnp.exp(sc - m)
                den = jnp.sum(p, axis=0, keepdims=True) + jnp.exp(sink - m)
                out = lax.dot_general(v2, p.astype(BF16), TN, preferred_element_type=F32) / den
                outs.append(out.T)
            o_ref[rows, :] = _unstack_heads(outs).astype(BF16)

    return _call(
        comm, body, name="attn_fwd", grid=(steps,),
        in_specs=[pl.BlockSpec(memory_space=pltpu.SMEM), *_attn_bias_specs(), *_block_specs(n, steps)],
        out_specs=pl.BlockSpec((n * BLOCK, ATTN_W), lambda i: (i, 0)),
        out_shape=jax.ShapeDtypeStruct((s, ATTN_W), BF16),
        compiler_params=_params("parallel"),
    )(sinks, _attn_bias(), _attn_bias(), qkv, qkv, qkv, qkv, qkv)


def _mix_fwd(x, attn, c3, gates, conv_w, w_br, w_out, g2, tm, comm=None):
    s = x.shape[0]

    def body(x_ref, at_ref, c3_ref, gt_ref, cw_ref, wbr_ref, wo_ref, g_ref,
             conv_ref, a_ref, cv_ref, mg_ref, h1_ref, hn_ref, carry_ref):
        @pl.when(pl.program_id(0) == 0)
        def _():
            carry_ref[...] = jnp.zeros_like(carry_ref)

        c3v = c3_ref[...].astype(F32)
        cb, cc, cx = c3v[:, :CONV_W], c3v[:, CONV_W:2 * CONV_W], c3v[:, 2 * CONV_W:]
        z = cc * cx
        cz = _causal_conv(z, carry_ref[...], cw_ref[...])
        carry_ref[...] = z[tm - 8:tm]
        conv = (cb * cz).astype(BF16)
        conv_ref[...] = conv
        a = jnp.dot(at_ref[...], wbr_ref[:ATTN_W, :], preferred_element_type=F32)
        cv = jnp.dot(conv, wbr_ref[ATTN_W:, :], preferred_element_type=F32)
        a_ref[...] = a.astype(BF16)
        cv_ref[...] = cv.astype(BF16)
        gt = gt_ref[...].astype(F32)
        merged = (_sigmoid(gt[:, :D_MODEL]) * a + _sigmoid(gt[:, D_MODEL:]) * cv).astype(BF16)
        mg_ref[...] = merged
        h1 = x_ref[...] + jnp.dot(merged, wo_ref[...], preferred_element_type=F32)
        h1_ref[...] = h1
        hn_ref[...] = (h1 * _rstd(h1) * g_ref[...]).astype(BF16)

    row = lambda w: pl.BlockSpec((tm, w), lambda i: (i, 0))
    return _call(
        comm, body, name="mix_fwd", grid=(s // tm,),
        in_specs=[row(D_MODEL), row(ATTN_W), row(C3_W), row(GATES_W), _resident((3, CONV_W)),
                  _resident((ATTN_W + CONV_W, D_MODEL)), _resident((D_MODEL, D_MODEL)), _resident((1, D_MODEL))],
        out_specs=[row(CONV_W), row(D_MODEL), row(D_MODEL), row(D_MODEL), row(D_MODEL), row(D_MODEL)],
        out_shape=[jax.ShapeDtypeStruct((s, CONV_W), BF16), jax.ShapeDtypeStruct((s, D_MODEL), BF16),
                   jax.ShapeDtypeStruct((s, D_MODEL), BF16), jax.ShapeDtypeStruct((s, D_MODEL), BF16),
                   jax.ShapeDtypeStruct((s, D_MODEL), F32), jax.ShapeDtypeStruct((s, D_MODEL), BF16)],
        scratch_shapes=[pltpu.VMEM((8, CONV_W), F32)],
        compiler_params=_params("arbitrary"),
    )(x, attn, c3, gates, conv_w, w_br, w_out, g2)


def _ffn_fwd_loss(hn, h1, w_up, ffn_cw, w_down, g3, target, tm):
    s = hn.shape[0]

    def body(hn_ref, h1_ref, wu_ref, cw_ref, wd_ref, g_ref, t_ref,
             u_ref, up_ref, act_ref, dh2_ref, loss_ref, gfn_ref, carry_ref):
        @pl.when(pl.program_id(0) == 0)
        def _():
            carry_ref[...] = jnp.zeros_like(carry_ref)
            loss_ref[...] = jnp.zeros_like(loss_ref)
            gfn_ref[...] = jnp.zeros_like(gfn_ref)

        u = jnp.dot(hn_ref[...], wu_ref[...], preferred_element_type=F32)
        u_ref[...] = u.astype(BF16)
        up = _causal_conv(u, carry_ref[...], cw_ref[...])
        up_ref[...] = up
        carry_ref[...] = u[tm - 8:tm]
        gate, val = up[:, :D_FF], up[:, D_FF:]
        act = (gate * _sigmoid(gate) * val).astype(BF16)
        act_ref[...] = act
        h2 = h1_ref[...] + jnp.dot(act, wd_ref[...], preferred_element_type=F32)
        rstd = _rstd(h2)
        g = g_ref[...]
        err = h2 * rstd * g - t_ref[...]
        loss_ref[...] += jnp.sum(err * err) * (0.5 / D_MODEL)
        dh2, dg = _rms_bwd(err * (1.0 / D_MODEL), h2, rstd, g)
        dh2_ref[...] = dh2
        gfn_ref[...] += jnp.sum(dg, axis=0, keepdims=True)

    row = lambda w: pl.BlockSpec((tm, w), lambda i: (i, 0))
    acc = lambda w: pl.BlockSpec((1, w), lambda i: (0, 0))
    return pl.pallas_call(
        body, name="ffn_fwd_loss", grid=(s // tm,),
        in_specs=[row(D_MODEL), row(D_MODEL), _resident((D_MODEL, FF2)), _resident((3, FF2)),
                  _resident((D_FF, D_MODEL)), _resident((1, D_MODEL)), row(D_MODEL)],
        out_specs=[row(FF2), row(FF2), row(D_FF), row(D_MODEL), acc(128), acc(D_MODEL)],
        out_shape=[jax.ShapeDtypeStruct((s, FF2), BF16), jax.ShapeDtypeStruct((s, FF2), F32),
                   jax.ShapeDtypeStruct((s, D_FF), BF16),
                   jax.ShapeDtypeStruct((s, D_MODEL), F32), jax.ShapeDtypeStruct((1, 128), F32),
                   jax.ShapeDtypeStruct((1, D_MODEL), F32)],
        scratch_shapes=[pltpu.VMEM((8, FF2), F32)],
        compiler_params=_params("arbitrary"),
    )(hn, h1, w_up, ffn_cw, w_down, g3, target)


def _ffn_bwd(dh2, u, up, h1, w_up, ffn_cw, w_down, g2, tm):
    s = dh2.shape[0]
    nt = s // tm

    def body(dh2_ref, u_ref, up_ref, h1_ref, wu_ref, cw_ref, wd_ref, g_ref,
             du_ref, dh1_ref, gcw_ref, gg_ref, carry_ref):
        @pl.when(pl.program_id(0) == 0)
        def _():
            for ref in (carry_ref, gcw_ref, gg_ref):
                ref[...] = jnp.zeros_like(ref)

        dh2v = dh2_ref[...]
        dact = lax.dot_general(dh2v.astype(BF16), wd_ref[...], NT, preferred_element_type=F32)
        upv = up_ref[...]
        gate, val = upv[:, :D_FF], upv[:, D_FF:]
        sg = _sigmoid(gate)
        dval = dact * (gate * sg)
        dgate = dact * val * (sg * (1.0 + gate * (1.0 - sg)))
        dup = jnp.concatenate([dgate, dval], axis=1)
        dup1, dup2 = _rows_after(dup, carry_ref[...])
        carry_ref[...] = dup[0:8]
        u = u_ref[...].astype(F32)
        gcw_ref[2:3, :] += jnp.sum(dup * u, axis=0, keepdims=True)
        gcw_ref[1:2, :] += jnp.sum(dup1 * u, axis=0, keepdims=True)
        gcw_ref[0:1, :] += jnp.sum(dup2 * u, axis=0, keepdims=True)
        cw = cw_ref[...]
        du = (cw[2:3] * dup + cw[1:2] * dup1 + cw[0:1] * dup2).astype(BF16)
        du_ref[...] = du
        dhn = lax.dot_general(du, wu_ref[...], NT, preferred_element_type=F32)
        h1v = h1_ref[...]
        dh1, dg = _rms_bwd(dhn, h1v, _rstd(h1v), g_ref[...])
        dh1_ref[...] = dh2v + dh1
        gg_ref[...] += jnp.sum(dg, axis=0, keepdims=True)

    row = lambda w: pl.BlockSpec((tm, w), lambda i: (nt - 1 - i, 0))
    return pl.pallas_call(
        body, name="ffn_bwd", grid=(nt,),
        in_specs=[row(D_MODEL), row(FF2), row(FF2),
                  row(D_MODEL), _resident((D_MODEL, FF2)), _resident((3, FF2)), _resident((D_FF, D_MODEL)),
                  _resident((1, D_MODEL))],
        out_specs=[row(FF2), row(D_MODEL), pl.BlockSpec((3, FF2), lambda i: (0, 0)),
                   pl.BlockSpec((1, D_MODEL), lambda i: (0, 0))],
        out_shape=[jax.ShapeDtypeStruct((s, FF2), BF16), jax.ShapeDtypeStruct((s, D_MODEL), F32),
                   jax.ShapeDtypeStruct((3, FF2), F32), jax.ShapeDtypeStruct((1, D_MODEL), F32)],
        scratch_shapes=[pltpu.VMEM((8, FF2), F32)],
        compiler_params=_params("arbitrary"),
    )(dh2, u, up, h1, w_up, ffn_cw, w_down, g2)


def _mix_bwd(dh1, gates, a, cv, c3, attn, conv, merged, conv_w, w_br, w_out, tm, comm=None):
    s = dh1.shape[0]
    nt = s // tm
    halo = 16

    def body(dh1_ref, gt_ref, a_ref, cv_ref, c3_ref, ch_ref, at_ref, cn_ref, mg_ref, cw_ref, wbr_ref,
             wo_ref, dat_ref, dc3_ref, dgt_ref, gcw_ref, gbr_ref, gout_ref, carry_ref, br_acc, out_acc):
        i = pl.program_id(0)

        @pl.when(i == 0)
        def _():
            for ref in (carry_ref, gcw_ref, br_acc, out_acc):
                ref[...] = jnp.zeros_like(ref)

        dh1v = dh1_ref[...].astype(BF16)
        out_acc[...] += lax.dot_general(mg_ref[...], dh1v, TN, preferred_element_type=F32)
        dm = lax.dot_general(dh1v, wo_ref[...], NT, preferred_element_type=F32)
        gt = gt_ref[...].astype(F32)
        sa, sc = _sigmoid(gt[:, :D_MODEL]), _sigmoid(gt[:, D_MODEL:])
        da = (dm * sa).astype(BF16)
        dcv = (dm * sc).astype(BF16)
        br_acc[:ATTN_W, :] += lax.dot_general(at_ref[...], da, TN, preferred_element_type=F32)
        br_acc[ATTN_W:, :] += lax.dot_general(cn_ref[...], dcv, TN, preferred_element_type=F32)
        dgt_ref[...] = jnp.concatenate(
            [dm * a_ref[...].astype(F32) * (sa * (1.0 - sa)), dm * cv_ref[...].astype(F32) * (sc * (1.0 - sc))],
            axis=1).astype(BF16)
        dat_ref[...] = lax.dot_general(da, wbr_ref[:ATTN_W, :], NT, preferred_element_type=F32).astype(BF16)
        dconv = lax.dot_general(dcv, wbr_ref[ATTN_W:, :], NT, preferred_element_type=F32)
        c3v = c3_ref[...].astype(F32)
        cb, cc, cx = c3v[:, :CONV_W], c3v[:, CONV_W:2 * CONV_W], c3v[:, 2 * CONV_W:]
        z = cc * cx
        chv = ch_ref[...].astype(F32)[halo - 8:halo] * (i < nt - 1).astype(F32)
        zh = chv[:, CONV_W:2 * CONV_W] * chv[:, 2 * CONV_W:]
        cw = cw_ref[...]
        cz = _causal_conv(z, zh, cw)
        dcz = dconv * cb
        dcz1, dcz2 = _rows_after(dcz, carry_ref[...])
        carry_ref[...] = dcz[0:8]
        gcw_ref[2:3, :] += jnp.sum(dcz * z, axis=0, keepdims=True)
        gcw_ref[1:2, :] += jnp.sum(dcz1 * z, axis=0, keepdims=True)
        gcw_ref[0:1, :] += jnp.sum(dcz2 * z, axis=0, keepdims=True)
        dz = cw[2:3] * dcz + cw[1:2] * dcz1 + cw[0:1] * dcz2
        dc3_ref[...] = jnp.concatenate([dconv * cz, dz * cx, dz * cc], axis=1).astype(BF16)

        @pl.when(i == nt - 1)
        def _():
            gbr_ref[...] = br_acc[...].astype(BF16)
            gout_ref[...] = out_acc[...].astype(BF16)

    row = lambda w: pl.BlockSpec((tm, w), lambda i: (nt - 1 - i, 0))
    return _call(
        comm, body, name="mix_bwd", grid=(nt,),
        in_specs=[row(D_MODEL), row(GATES_W), row(D_MODEL), row(D_MODEL), row(C3_W),
                  pl.BlockSpec((halo, C3_W), lambda i: (jnp.maximum((nt - 1 - i) * (tm // halo) - 1, 0), 0)),
                  row(ATTN_W), row(CONV_W), row(D_MODEL), _resident((3, CONV_W)),
                  _resident((ATTN_W + CONV_W, D_MODEL)), _resident((D_MODEL, D_MODEL))],
        out_specs=[row(ATTN_W), row(C3_W), row(GATES_W), pl.BlockSpec((3, CONV_W), lambda i: (0, 0)),
                   _resident((ATTN_W + CONV_W, D_MODEL)), _resident((D_MODEL, D_MODEL))],
        out_shape=[jax.ShapeDtypeStruct((s, ATTN_W), BF16), jax.ShapeDtypeStruct((s, C3_W), BF16),
                   jax.ShapeDtypeStruct((s, GATES_W), BF16), jax.ShapeDtypeStruct((3, CONV_W), F32),
                   jax.ShapeDtypeStruct((ATTN_W + CONV_W, D_MODEL), BF16),
                   jax.ShapeDtypeStruct((D_MODEL, D_MODEL), BF16)],
        scratch_shapes=[pltpu.VMEM((8, CONV_W), F32), pltpu.VMEM((ATTN_W + CONV_W, D_MODEL), F32),
                        pltpu.VMEM((D_MODEL, D_MODEL), F32)],
        compiler_params=_params("arbitrary"),
    )(dh1, gates, a, cv, c3, c3, attn, conv, merged, conv_w, w_br, w_out)


def _attn_bwd(qkv, sinks, o, do, comm=None):
    s = qkv.shape[0]
    npair = s // (2 * BLOCK)

    def one_block(sk_ref, bias, q, kp, kc, vp, vc, ov, dov, dsk_ref):
        dqs, dks, dvs = [], [], []
        for h in range(N_KV_HEADS):
            hs = slice(h * HEAD_DIM, (h + 1) * HEAD_DIM)
            k2 = jnp.concatenate([kp[:, hs], kc[:, hs]], axis=0)
            v2 = jnp.concatenate([vp[:, hs], vc[:, hs]], axis=0)
            qg, og, dog = _stack_heads(q, h), _stack_heads(ov, h), _stack_heads(dov, h)
            sc = lax.dot_general(k2, qg, NT, preferred_element_type=F32) * ATTN_SCALE + bias
            sink = _sink_row(sk_ref, h)
            m = jnp.maximum(jnp.max(sc, axis=0, keepdims=True), sink)
            p = jnp.exp(sc - m)
            psink = jnp.exp(sink - m)
            inv = 1.0 / (jnp.sum(p, axis=0, keepdims=True) + psink)
            p = p * inv
            delta = jnp.sum(dog.astype(F32) * og.astype(F32), axis=1, keepdims=True).T
            dp = lax.dot_general(v2, dog, NT, preferred_element_type=F32)
            ds = (p * (dp - delta)).astype(BF16)
            dqs.append((lax.dot_general(k2, ds, TN, preferred_element_type=F32) * ATTN_SCALE).T)
            dks.append(jnp.dot(ds, qg, preferred_element_type=F32) * ATTN_SCALE)
            dvs.append(jnp.dot(p.astype(BF16), dog, preferred_element_type=F32))
            dsink = -(psink * inv * delta)
            for g in range(GROUP):
                r = h * GROUP + g
                dsk_ref[r:r + 1, :] += jnp.sum(dsink[:, g * BLOCK:(g + 1) * BLOCK])
        return _unstack_heads(dqs), jnp.concatenate(dks, axis=1), jnp.concatenate(dvs, axis=1)

    def body(sk_ref, bias0_ref, bias1_ref, q_ref, kp_ref, kc_ref, vp_ref, vc_ref, o_ref, do_ref,
             dq_ref, dke_ref, dko_ref, dve_ref, dvo_ref, dsk_ref, ck_ref, cvv_ref):
        i = pl.program_id(0)

        @pl.when(i == 0)
        def _():
            for ref in (ck_ref, cvv_ref, dsk_ref):
                ref[...] = jnp.zeros_like(ref)

        @pl.when(i < npair)
        def _():
            kc, vc = kc_ref[...], vc_ref[...]
            first, second = slice(0, BLOCK), slice(BLOCK, 2 * BLOCK)
            dq0, dk0, dv0 = one_block(sk_ref, bias0_ref[...], q_ref[first, :], kp_ref[...], kc[first], vp_ref[...],
                                      vc[first], o_ref[first, :], do_ref[first, :], dsk_ref)
            dq1, dk1, dv1 = one_block(sk_ref, bias1_ref[...], q_ref[second, :], kc[first], kc[second], vc[first],
                                      vc[second], o_ref[second, :], do_ref[second, :], dsk_ref)
            dq_ref[first, :] = dq0.astype(BF16)
            dq_ref[second, :] = dq1.astype(BF16)
            dko_ref[...] = (ck_ref[...] + dk0[:BLOCK]).astype(BF16)
            dvo_ref[...] = (cvv_ref[...] + dv0[:BLOCK]).astype(BF16)
            dke_ref[...] = (dk0[BLOCK:] + dk1[:BLOCK]).astype(BF16)
            dve_ref[...] = (dv0[BLOCK:] + dv1[:BLOCK]).astype(BF16)
            ck_ref[...] = dk1[BLOCK:]
            cvv_ref[...] = dv1[BLOCK:]

        @pl.when(i == npair)
        def _():
            dko_ref[...] = ck_ref[...].astype(BF16)
            dvo_ref[...] = cvv_ref[...].astype(BF16)

    cur = lambda i: jnp.minimum(i, npair - 1)
    done = lambda i: jnp.maximum(i - 1, 0)
    rows = pl.BlockSpec((2 * BLOCK, ATTN_W), lambda i: (cur(i), 0))
    even = pl.BlockSpec((BLOCK, KV_W), lambda i: (cur(i), 0))
    odd = pl.BlockSpec((BLOCK, KV_W), lambda i: (done(i), 0))
    half = jax.ShapeDtypeStruct((s // 2, KV_W), BF16)
    return _call(
        comm, body, name="attn_bwd", grid=(npair + 1,),
        in_specs=[pl.BlockSpec(memory_space=pltpu.SMEM), *_attn_bias_specs(), *_block_specs(2, npair), rows, rows],
        out_specs=[rows, even, odd, even, odd, pl.BlockSpec((N_HEADS, 128), lambda i: (0, 0))],
        out_shape=[jax.ShapeDtypeStruct((s, ATTN_W), BF16), half, half, half, half,
                   jax.ShapeDtypeStruct((N_HEADS, 128), F32)],
        scratch_shapes=[pltpu.VMEM((BLOCK, KV_W), F32), pltpu.VMEM((BLOCK, KV_W), F32)],
        compiler_params=_params("arbitrary"),
    )(sinks, _attn_bias(), _attn_bias(), qkv, qkv, qkv, qkv, qkv, o, do)


def _inproj_bwd(dq, dk, dv, dc3, dgt, w_in, x, xn, dh1, g1):
    s = x.shape[0]
    tm = min(2 * BLOCK, s)
    nt = s // tm

    def body(dq_ref, dke_ref, dko_ref, dve_ref, dvo_ref, dc3_ref, dgt_ref, w_ref, x_ref, xn_ref, dh1_ref, g_ref,
             dx_ref, gw_ref, gb_ref, gg_ref, acc_ref):
        i = pl.program_id(0)

        @pl.when(i == 0)
        def _():
            for ref in (gb_ref, gg_ref, acc_ref):
                ref[...] = jnp.zeros_like(ref)

        dk = jnp.concatenate([dke_ref[...], dko_ref[...]], axis=0)
        dv = jnp.concatenate([dve_ref[...], dvo_ref[...]], axis=0)
        dp = jnp.concatenate([dq_ref[...], dk, dv, dc3_ref[...], dgt_ref[...]], axis=1)
        acc_ref[...] += lax.dot_general(dp, xn_ref[...], TN, preferred_element_type=F32)
        gb_ref[...] += jnp.sum(dp.astype(F32), axis=0, keepdims=True)
        dxn = jnp.dot(dp, w_ref[...], preferred_element_type=F32)
        xf = x_ref[...]
        dx, dg = _rms_bwd(dxn, xf, _rstd(xf), g_ref[...])
        dx_ref[...] = dh1_ref[...] + dx
        gg_ref[...] += jnp.sum(dg, axis=0, keepdims=True)

        @pl.when(i == nt - 1)
        def _():
            gw_ref[...] = acc_ref[...].astype(BF16)

    row = lambda w: pl.BlockSpec((tm, w), lambda i: (i, 0))
    acc = lambda w: pl.BlockSpec((1, w), lambda i: (0, 0))
    block = pl.BlockSpec((tm // 2, KV_W), lambda i: (i, 0))
    return pl.pallas_call(
        body, name="inproj_bwd", grid=(nt,),
        in_specs=[row(ATTN_W), block, block, block, block, row(C3_W), row(GATES_W), _resident((IN_W, D_MODEL)),
                  row(D_MODEL), row(D_MODEL), row(D_MODEL), _resident((1, D_MODEL))],
        out_specs=[row(D_MODEL), _resident((IN_W, D_MODEL)), acc(IN_W), acc(D_MODEL)],
        out_shape=[jax.ShapeDtypeStruct((s, D_MODEL), F32), jax.ShapeDtypeStruct((IN_W, D_MODEL), BF16),
                   jax.ShapeDtypeStruct((1, IN_W), F32), jax.ShapeDtypeStruct((1, D_MODEL), F32)],
        scratch_shapes=[pltpu.VMEM((IN_W, D_MODEL), F32)],
        compiler_params=_params("arbitrary"),
    )(dq, *dk, *dv, dc3, dgt, w_in, x, xn, dh1, g1)


def _wgrad(a, b, bm, bn, bk, name, comm=None):
    s, m = a.shape
    n = b.shape[1]
    nk = s // bk

    def body(a_ref, b_ref, o_ref, acc_ref):
        k = pl.program_id(2)

        @pl.when(k == 0)
        def _():
            acc_ref[...] = jnp.zeros_like(acc_ref)

        acc_ref[...] += lax.dot_general(a_ref[...].astype(BF16), b_ref[...].astype(BF16), TN,
                                        preferred_element_type=F32)

        @pl.when(k == nk - 1)
        def _():
            o_ref[...] = acc_ref[...].astype(BF16)

    return _call(
        comm, body, name=name, grid=(m // bm, n // bn, nk),
        in_specs=[pl.BlockSpec((bk, bm), lambda i, j, k: (k, i)), pl.BlockSpec((bk, bn), lambda i, j, k: (k, j))],
        out_specs=pl.BlockSpec((bm, bn), lambda i, j, k: (i, j)),
        out_shape=jax.ShapeDtypeStruct((m, n), BF16),
        scratch_shapes=[pltpu.VMEM((bm, bn), F32)],
        compiler_params=_params("parallel", "parallel", "arbitrary"),
    )(a, b)


class _Carry:
    def __init__(self, jobs, reads=None, bufs=None, fresh=None):
        self.jobs, self.reads, self.bufs, self.fresh = jobs, reads or {}, bufs or {}, fresh or {}
        self.out = {}


class _Job:
    def __init__(self, n_sems, plan):
        self.n_sems, self.plan = n_sems, plan


def _plan_all(jobs, hbm, send, recv):
    pos = _position()
    starts, waits, base = [], [], 0
    for job in jobs:
        s, w = job.plan(hbm, pos, send, recv, base)
        starts, waits, base = starts + s, waits + w, base + job.n_sems
    return starts, waits


def _call(comm, body, **kw):
    if comm is None:
        return pl.pallas_call(body, **kw)
    grid = kw["grid"]
    single = not isinstance(kw["out_shape"], (list, tuple))
    out_shape = [kw["out_shape"]] if single else list(kw["out_shape"])
    out_specs = [kw["out_specs"]] if single else list(kw["out_specs"])
    in_specs = list(kw["in_specs"])
    scratch = list(kw.get("scratch_shapes", ()))
    r_names, b_names, f_names = list(comm.reads), list(comm.bufs), list(comm.fresh)
    n_args, n_out, n_scr = len(in_specs), len(out_shape), len(scratch)
    n_sems = sum(j.n_sems for j in comm.jobs)

    def wrapped(*refs):
        k = n_args
        hbm = dict(zip(r_names, refs[k:k + len(r_names)]))
        k += len(r_names) + len(b_names)
        outs = refs[k:k + n_out]
        k += n_out
        hbm.update(zip(b_names + f_names, refs[k:k + len(b_names) + len(f_names)]))
        k += len(b_names) + len(f_names)
        send, recv = refs[k + n_scr:]
        starts, waits = _plan_all(comm.jobs, hbm, send, recv)
        ids = [pl.program_id(a) for a in range(len(grid))]
        first = functools.reduce(jnp.logical_and, [i == 0 for i in ids])
        last = functools.reduce(jnp.logical_and, [i == g - 1 for i, g in zip(ids, grid)])

        @pl.when(first)
        def _():
            for cp in starts:
                cp.start()

        body(*refs[:n_args], *outs, *refs[k:k + n_scr])

        @pl.when(last)
        def _():
            for cp in waits:
                cp.wait_recv()
            for cp in starts:
                cp.wait_send()

    sems = pltpu.SemaphoreType.DMA((n_sems,))
    held = [jax.ShapeDtypeStruct(a.shape, a.dtype) for a in comm.bufs.values()] + list(comm.fresh.values())
    call = pl.pallas_call(
        wrapped, name=kw["name"], grid=grid,
        in_specs=in_specs + [_ANY] * (len(r_names) + len(b_names)),
        out_specs=out_specs + [_ANY] * len(held),
        out_shape=out_shape + held,
        input_output_aliases={n_args + len(r_names) + i: n_out + i for i in range(len(b_names))},
        scratch_shapes=scratch + [sems, sems],
        compiler_params=_params(*["arbitrary"] * len(grid)),
    )

    def run(*args):
        res = call(*args, *comm.reads.values(), *comm.bufs.values())
        comm.out = dict(zip(b_names + f_names, res[n_out:]))
        return res[0] if single else res[:n_out]

    return run


def _exchange(name, phases, reads=None, bufs=None, fresh=None):
    comm = _Carry([j for ph in phases for j in ph], reads, bufs, fresh)
    r_names, b_names, f_names = list(comm.reads), list(comm.bufs), list(comm.fresh)
    n_sems = sum(j.n_sems for j in comm.jobs)

    def body(*refs):
        hbm = dict(zip(r_names, refs[:len(r_names)]))
        k = len(r_names) + len(b_names)
        hbm.update(zip(b_names + f_names, refs[k:k + len(b_names) + len(f_names)]))
        send, recv = refs[-2:]
        pos = _position()
        started, base = [], 0
        for ph in phases:
            waits = []
            for job in ph:
                s, w = job.plan(hbm, pos, send, recv, base)
                base += job.n_sems
                for cp in s:
                    cp.start()
                started, waits = started + s, waits + w
            for cp in waits:
                cp.wait_recv()
        for cp in started:
            cp.wait_send()

    sems = pltpu.SemaphoreType.DMA((n_sems,))
    held = [jax.ShapeDtypeStruct(a.shape, a.dtype) for a in comm.bufs.values()] + list(comm.fresh.values())
    res = pl.pallas_call(
        body, name=name, in_specs=[_ANY] * (len(r_names) + len(b_names)), out_specs=[_ANY] * len(held),
        out_shape=held, input_output_aliases={len(r_names) + i: i for i in range(len(b_names))},
        scratch_shapes=[sems, sems],
    )(*comm.reads.values(), *comm.bufs.values())
    return dict(zip(b_names + f_names, res))


_HBM = pl.BlockSpec(memory_space=pltpu.HBM)
_SEM = pl.BlockSpec(memory_space=pltpu.SEMAPHORE)
_EFFECT = pltpu.SideEffectType.DATAFLOW_SIDE_EFFECTING


def _start_exchanges(name, groups):
    names = [list(arrays) for _, arrays in groups]
    first = [sum(len(ns) for ns in names[:g]) for g in range(len(groups))]
    n, ng = sum(len(ns) for ns in names), len(groups)

    def body(*refs):
        for g, (jobs, _) in enumerate(groups):
            hbm = dict(zip(names[g], refs[first[g]:first[g] + len(names[g])]))
            for cp in _plan_all(jobs, hbm, refs[n + 2 * g], refs[n + 2 * g + 1])[0]:
                cp.start()
        refs[-1][...] = jnp.zeros_like(refs[-1])

    given = [pltpu.with_memory_space_constraint(
        a if isinstance(a, jax.Array) else lax.empty(a.shape, a.dtype), pltpu.HBM)
        for _, arrays in groups for a in arrays.values()]
    sems = [pltpu.SemaphoreType.DMA((sum(j.n_sems for j in jobs),)) for jobs, _ in groups for _ in range(2)]
    res = pl.pallas_call(
        body, name=name,
        out_shape=(*sems, *[pltpu.HBM(a.shape, a.dtype) for a in given], jax.ShapeDtypeStruct((8, 128), F32)),
        in_specs=[_HBM] * n, out_specs=(*[_SEM] * (2 * ng), *[_HBM] * n, pl.BlockSpec(memory_space=pltpu.VMEM)),
        input_output_aliases={i: 2 * ng + i for i in range(n)},
        compiler_params=pltpu.CompilerParams(has_side_effects=_EFFECT),
    )(*given)
    held = res[2 * ng:2 * ng + n]
    states = [(names[g], groups[g][0], res[2 * g], res[2 * g + 1], held[first[g]:first[g] + len(names[g])])
              for g in range(ng)]
    return states, res[-1]


def _start_exchange(name, jobs, arrays):
    states, token = _start_exchanges(name, [(jobs, arrays)])
    return states[0], token


def _finish_exchange(name, state, after):
    names, jobs, send_sem, recv_sem, held = state
    n = len(names)

    def body(*refs):
        hbm = dict(zip(names, refs[:n]))
        send, recv = refs[n:n + 2]
        starts, waits = _plan_all(jobs, hbm, send, recv)
        for cp in waits:
            cp.wait_recv()
        for cp in starts:
            cp.wait_send()

    res = pl.pallas_call(
        body, name=name, out_shape=tuple(pltpu.HBM(a.shape, a.dtype) for a in held),
        in_specs=[_HBM] * n + [_SEM, _SEM, _ANY], out_specs=tuple([_HBM] * n),
        input_output_aliases={i: i for i in range(n)},
        compiler_params=pltpu.CompilerParams(has_side_effects=_EFFECT),
    )(*held, send_sem, recv_sem, after)
    return dict(zip(names, res))


def _row_tile(rows, bytes_per_row):
    best = 16
    for t in range(16, rows + 1, 16):
        if rows % t == 0 and t * bytes_per_row <= 9 * 1024 * 1024:
            best = t
    return best


def _rowwise(fn, ins, out_dtypes, name, after=None):
    rows, cols = ins[0].shape
    per_row = sum(cols * a.dtype.itemsize for a in ins) + sum(cols * jnp.dtype(d).itemsize for d in out_dtypes)
    tr = _row_tile(rows, per_row)
    n_in = len(ins)

    def body(*refs):
        outs = fn(*[r[...] for r in refs[:n_in]])
        for o_ref, o in zip(refs[-len(out_dtypes):], outs):
            o_ref[...] = o.astype(o_ref.dtype)

    tile = pl.BlockSpec((tr, cols), lambda i: (i, 0))
    behind = [] if after is None else [after]
    return pl.pallas_call(
        body, name=name, grid=(rows // tr,),
        in_specs=[tile] * n_in + [pl.BlockSpec((8, 128), lambda i: (0, 0))] * len(behind),
        out_specs=[tile] * len(out_dtypes),
        out_shape=[jax.ShapeDtypeStruct((rows, cols), d) for d in out_dtypes],
        compiler_params=_params("parallel"),
    )(*ins, *behind)


def _tiled(fn, name, grid, pos, ins, outs):
    n_in = len(ins)

    def body(pos_ref, *refs):
        res = fn(*[r[...] for r in refs[:n_in]])
        for o_ref, o in zip(refs[n_in:], res):
            o_ref[...] = o.astype(o_ref.dtype)

    return pl.pallas_call(
        body, name=name,
        grid_spec=pltpu.PrefetchScalarGridSpec(
            num_scalar_prefetch=1, grid=grid,
            in_specs=[pl.BlockSpec(bs, im) for _, bs, im in ins],
            out_specs=[pl.BlockSpec(bs, im) for _, _, bs, im in outs]),
        out_shape=[jax.ShapeDtypeStruct(s, d) for s, d, _, _ in outs],
        compiler_params=_params("parallel"),
    )(pos, *[a for a, _, _ in ins])


def _adamw(w, g, m, v):
    m = ADAM_B1 * m + (1.0 - ADAM_B1) * g
    v = ADAM_B2 * v + (1.0 - ADAM_B2) * (g * g)
    m_hat = m / (1.0 - ADAM_B1 ** ADAM_STEP)
    v_hat = v / (1.0 - ADAM_B2 ** ADAM_STEP)
    return -ADAM_LR * (m_hat / (jnp.sqrt(v_hat) + ADAM_EPS) + ADAM_WD * w), m, v


def _adamw_small(params):
    n = len(params)

    def body(*refs):
        for k in range(n):
            w, g, m, v = (r[...] for r in refs[4 * k:4 * k + 4])
            for o_ref, o in zip(refs[4 * n + 3 * k:4 * n + 3 * k + 3], _adamw(w, g, m, v)):
                o_ref[...] = o

    flat = [a for p in params for a in p]
    return pl.pallas_call(
        body, name="adamw_small",
        out_shape=[jax.ShapeDtypeStruct(p[0].shape, F32) for p in params for _ in range(3)],
    )(*flat)


class _Layout:
    def __init__(self, rows, cols, stacked):
        self.rows, self.cols, self.stacked = rows, cols, stacked

    def whole(self, rows=None):
        r = self.rows if rows is None else rows
        return (N_CHIPS, r, self.cols) if self.stacked else (r, N_CHIPS * self.cols)

    def part_rows(self, h, q=0, nq=1):
        n = self.rows // 2 // nq
        return pl.ds(pl.multiple_of(h * (self.rows // 2) + q * n, 16), n)

    def half_rows(self, h):
        return self.part_rows(h)

    def block(self, ref, p, rows=slice(None)):
        if self.stacked:
            return ref.at[p, rows, :]
        return ref.at[rows, pl.ds(pl.multiple_of(p * self.cols, 128), self.cols)]

    def all_chips(self, ref, rows):
        return ref.at[:, rows, :] if self.stacked else ref.at[rows, :]


BIG = (
    _Layout(IN_SHARD, D_MODEL, True),
    _Layout(ATTN_W + CONV_W, D_MODEL // N_CHIPS, False),
    _Layout(D_MODEL // N_CHIPS, D_MODEL, True),
    _Layout(D_MODEL, FF2 // N_CHIPS, False),
    _Layout(D_FF // N_CHIPS, D_MODEL, True),
)
N_BIG = len(BIG)
_ANY = pl.BlockSpec(memory_space=pl.ANY)


def _position():
    x, y, c = lax.axis_index("x"), lax.axis_index("y"), lax.axis_index("c")
    return x, y, c, 2 * x + y


def _core_of_chip(p, c):
    return (p >> 1, p & 1, c)


def _place_cast(shard, lay, pos, name, after=None):
    rows, cols = shard.shape
    tr = _row_tile(rows, cols * 6)
    if lay.stacked:
        out = (lay.whole(), BF16, (None, tr, cols), lambda i, pos: (pos[0], i, 0))
    else:
        out = (lay.whole(), BF16, (tr, cols), lambda i, pos: (i, pos[0]))
    ins = [(shard, (tr, cols), lambda i, pos: (i, 0))]
    if after is not None:
        ins.append((after, (8, 128), lambda i, pos: (0, 0)))
    return _tiled(lambda a, *_: (a,), name, (rows // tr,), pos, ins, [out])[0]


def _place_cast_pair(top, bottom, lay, pos, name, after=None):
    rows, cols = top.shape
    ins = [(top, (rows, cols), lambda i, pos: (0, 0)), (bottom, (rows, cols), lambda i, pos: (0, 0))]
    if after is not None:
        ins.append((after, (8, 128), lambda i, pos: (0, 0)))
    return _tiled(lambda a, b, *_: (jnp.concatenate([a, b], axis=0),), name, (1,), pos, ins,
                  [(lay.whole(), BF16, (2 * rows, cols), lambda i, pos: (0, pos[0]))])[0]


def _adamw_pair(top, bottom, g, after=None):
    rows = top[0].shape[0]

    def body(*refs):
        (wa, ma, va, wb, mb, vb, g_ref), outs = refs[:7], refs[-8:]
        for (w, m, v), gg, o in (((wa, ma, va), g_ref[:rows], outs[:4]), ((wb, mb, vb), g_ref[rows:], outs[4:])):
            for o_ref, val in zip(o, (gg, *_adamw(w[...], gg, m[...], v[...]))):
                o_ref[...] = val

    behind = [] if after is None else [after[0:8, 0:128]]
    res = pl.pallas_call(
        body, name="adamw_w_br", out_shape=[jax.ShapeDtypeStruct(top[0].shape, F32)] * 8,
    )(*top, *bottom, g, *behind)
    return res[:4], res[4:]


def _remote(src, dst, send, recv, k, device):
    return pltpu.make_async_remote_copy(src_ref=src, dst_ref=dst, send_sem=send.at[k], recv_sem=recv.at[k],
                                        device_id=device, device_id_type=MESH)


def _arrival(dst, send, recv, k, me):
    return _remote(dst, dst, send, recv, k, me)


def _gather_ici(lay, name, q=0, nq=1):
    def plan(hbm, pos, send, recv, base):
        x, y, c, me = pos
        rows = lay.part_rows(c, q, nq)
        mine = lay.block(hbm[name], me, rows)
        starts = [_remote(mine, mine, send, recv, base + d - 1, _core_of_chip(me ^ d, c)) for d in (1, 2, 3)]
        waits = [_arrival(lay.block(hbm[name], me ^ d, rows), send, recv, base + d - 1, (x, y, c)) for d in (1, 2, 3)]
        return starts, waits
    return _Job(3, plan)


def _gather_d2d(lay, name, q=0, nq=1):
    def plan(hbm, pos, send, recv, base):
        x, y, c, me = pos
        starts, waits = [], []
        for d in (1, 2, 3):
            got = lay.block(hbm[name], me ^ d, lay.part_rows(c, q, nq))
            starts.append(_remote(got, got, send, recv, base + d - 1, (x, y, 1 - c)))
            waits.append(_arrival(lay.block(hbm[name], me ^ d, lay.part_rows(1 - c, q, nq)), send, recv, base + d - 1,
                                  (x, y, c)))
        return starts, waits
    return _Job(3, plan)


def _rs_pair(lay, grad, theirs):
    def plan(hbm, pos, send, recv, base):
        x, y, c, _ = pos
        out = _remote(lay.all_chips(hbm[grad], lay.half_rows(1 - c)), hbm[theirs], send, recv, base, (x, y, 1 - c))
        return [out], [_arrival(hbm[theirs], send, recv, base, (x, y, c))]
    return _Job(1, plan)


def _rs_chips(lay, sums, slots):
    def plan(hbm, pos, send, recv, base):
        x, y, c, me = pos
        starts = [_remote(lay.block(hbm[sums], me ^ d), hbm[slots].at[me], send, recv, base + d - 1,
                          _core_of_chip(me ^ d, c)) for d in (1, 2, 3)]
        waits = [_arrival(hbm[slots].at[me ^ d], send, recv, base + d - 1, (x, y, c)) for d in (1, 2, 3)]
        return starts, waits
    return _Job(3, plan)


def _rs_share(lay, shard):
    def plan(hbm, pos, send, recv, base):
        x, y, c, _ = pos
        mine = hbm[shard].at[lay.half_rows(c), :]
        other = hbm[shard].at[lay.half_rows(1 - c), :]
        return [_remote(mine, mine, send, recv, base, (x, y, 1 - c))], [_arrival(other, send, recv, base, (x, y, c))]
    return _Job(1, plan)


def _slots_shape(lay):
    return jax.ShapeDtypeStruct((N_CHIPS, lay.rows // 2, lay.cols), BF16)


def _theirs_shape(lay, dtype=BF16):
    return jax.ShapeDtypeStruct(lay.whole(lay.rows // 2), dtype)


def _pair_sum(grad, theirs, lay, pos, name):
    half = lay.rows // 2
    add = lambda a, b: (a.astype(F32) + b.astype(F32),)
    if lay.stacked:
        tr = _row_tile(half, lay.cols * 6)
        nt = half // tr
        flat = lambda a: a.reshape(-1, lay.cols)
        mine = lambda t, pos: ((t // nt) * (2 * nt) + pos[1] * nt + t % nt, 0)
        grid, blk = (N_CHIPS * nt,), (tr, lay.cols)
        grad, theirs = flat(grad), flat(theirs)
    else:
        tr = _row_tile(half, N_CHIPS * lay.cols * 6)
        nt = half // tr
        mine = lambda t, pos: (pos[1] * nt + t, 0)
        grid, blk = (nt,), (tr, N_CHIPS * lay.cols)
    same = lambda t, pos: (t, 0)
    out = _tiled(add, name, grid, pos, [(grad, blk, mine), (theirs, blk, same)], [(theirs.shape, BF16, blk, same)])[0]
    return out.reshape(lay.whole(half))


def _chip_sum(sums, slots, lay, pos, name, after=None):
    half = lay.rows // 2
    tr = _row_tile(half, lay.cols * 12)
    nt = half // tr
    blk3 = (None, tr, lay.cols)
    if lay.stacked:
        own = (sums, blk3, lambda i, pos: (pos[0], i, 0))
    else:
        own = (sums, (tr, lay.cols), lambda i, pos: (i, pos[0]))
    others = [(slots, blk3, functools.partial(lambda d, i, pos: (pos[0] ^ d, i, 0), d)) for d in (1, 2, 3)]

    def add(a, b1, b2, b3, *_):
        return (((a.astype(F32) + b1.astype(F32)) + b2.astype(F32)) + b3.astype(F32),)

    if after is not None:
        others.append((after, (8, 128), lambda i, pos: (0, 0)))
    return _tiled(add, name, (nt,), pos, [own] + others,
                  [((lay.rows, lay.cols), F32, (tr, lay.cols), lambda i, pos: (pos[1] * nt + i, 0))])[0]


N_DEV = 8


def _to_all(src, slots):
    def plan(hbm, pos, send, recv, base):
        x, y, c, _ = pos
        idx = 4 * x + 2 * y + c
        starts = [_remote(hbm[src], hbm[slots].at[idx], send, recv, base + k - 1,
                          (x ^ (k >> 2), y ^ ((k >> 1) & 1), c ^ (k & 1))) for k in range(1, N_DEV)]
        waits = [_arrival(hbm[slots].at[idx ^ k], send, recv, base + k - 1, (x, y, c)) for k in range(1, N_DEV)]
        return starts, waits
    return _Job(N_DEV - 1, plan)


def _sum_slots(own, slots, pos):
    def body(pos_ref, own_ref, slots_ref, o_ref):
        idx = 2 * pos_ref[0] + pos_ref[1]
        term = lambda q: jnp.where(idx == q, own_ref[...], slots_ref[q])
        acc = term(0)
        for q in range(1, N_DEV):
            acc = acc + term(q)
        o_ref[...] = acc

    return pl.pallas_call(
        body, name="sum_small", out_shape=jax.ShapeDtypeStruct(own.shape, F32),
        in_specs=[pl.BlockSpec(memory_space=pltpu.SMEM), pl.BlockSpec(memory_space=pltpu.VMEM),
                  pl.BlockSpec(memory_space=pltpu.VMEM)],
    )(pos, own, slots)


def _pack_rows(parts):
    padded = [jnp.pad(a, ((0, -a.shape[0] % 8), (0, 0))) for a in parts]
    starts = [sum(p.shape[0] for p in padded[:k]) for k in range(len(padded))]
    return jnp.concatenate(padded, axis=0), starts


def kernel(x, mix_norm, w_in, b_in, sinks, conv_w, w_attn_branch, w_conv_branch, w_out, ffn_norm, w_up, ffn_conv_w, w_down, final_norm, loss_target, m_mix_norm, m_w_in, m_b_in, m_sinks, m_conv_w, m_w_attn_branch, m_w_conv_branch, m_w_out, m_ffn_norm, m_w_up, m_ffn_conv_w, m_w_down, m_final_norm, v_mix_norm, v_w_in, v_b_in, v_sinks, v_conv_w, v_w_attn_branch, v_w_conv_branch, v_w_out, v_ffn_norm, v_w_up, v_ffn_conv_w, v_w_down, v_final_norm):
    me = 2 * lax.axis_index("x") + lax.axis_index("y")
    names = ("w_in", "w_br", "w_out", "w_up", "w_down")
    w_of = dict(w_in=w_in[0].T, w_out=w_out[0], w_up=w_up[0], w_down=w_down[0])
    m_of = dict(w_in=m_w_in[0].T, w_out=m_w_out[0], w_up=m_w_up[0], w_down=m_w_down[0])
    v_of = dict(w_in=v_w_in[0].T, w_out=v_w_out[0], w_up=v_w_up[0], w_down=v_w_down[0])
    ab = (w_attn_branch[0], m_w_attn_branch[0], v_w_attn_branch[0])
    cb = (w_conv_branch[0], m_w_conv_branch[0], v_w_conv_branch[0])

    pos = jnp.stack([me, lax.axis_index("c")]).astype(jnp.int32)

    lay = dict(zip(names, BIG))
    xs, target, sk = x[0], loss_target[0], sinks[0]
    s = xs.shape[0]
    tm, tm2, bk, bk2 = min(256, s), min(512, s), min(1024, s), min(2048, s)

    taps, (_, t0) = _pack_rows([conv_w[0], ffn_conv_w[0].reshape(3 * (FF2 // N_CHIPS // 128), 128)])
    placed = {"w_in": _place_cast(w_of["w_in"], lay["w_in"], pos, "cast_w_in")}
    fly_in, started = _start_exchange("gather_in_start", [_gather_ici(lay["w_in"], "w_in")], {"w_in": placed["w_in"]})
    taps_flight, started = _start_exchange("taps_start", [_to_all("v", "slots")],
                                           {"v": taps + started[0:1], "slots": jnp.zeros((N_DEV, *taps.shape), F32)})
    placed["w_br"] = _place_cast_pair(ab[0], cb[0], lay["w_br"], pos, "cast_w_br", after=started)
    for n in names[2:]:
        placed[n] = _place_cast(w_of[n], lay[n], pos, "cast_" + n, after=started)
    trio = ("w_br", "w_out")
    (fly_trio, fly_up, fly_down), started = _start_exchanges("gather_rest_start", [
        ([_gather_ici(lay[n], n) for n in ws], {n: placed[n] for n in ws}) for ws in (trio, ("w_up",), ("w_down",))])

    got = _finish_exchange("gather_in_wait", fly_in, after=started)
    w_in_full = _exchange("gather_in_d2d", [[_gather_d2d(lay["w_in"], "w_in")]], bufs=got)["w_in"].reshape(IN_W, D_MODEL)
    xn, qkv, c3, gates = _inproj_fwd(xs, mix_norm, w_in_full, b_in, tm2)
    k2 = _Carry([_gather_d2d(lay[n], n) for n in trio], bufs=_finish_exchange("gather_trio_wait", fly_trio, after=qkv))
    attn = _attn_fwd(qkv, sk, comm=k2)
    w_br = k2.out["w_br"]
    w_out_full = k2.out["w_out"].reshape(D_MODEL, D_MODEL)
    k3 = _Carry([_gather_d2d(lay["w_up"], "w_up")], bufs=_finish_exchange("gather_up_wait", fly_up, after=attn))
    taps = _finish_exchange("taps_wait", taps_flight, after=attn)
    taps = lax.dynamic_update_slice(taps["slots"], taps["v"][None], (2 * me + lax.axis_index("c"), 0, 0))
    conv_full = taps[0::2, 0:3].transpose(1, 0, 2).reshape(3, CONV_W)
    ffn_cw_full = taps[0::2, t0:t0 + 33].reshape(N_CHIPS, 3, FF2 // N_CHIPS).transpose(1, 0, 2).reshape(3, FF2)
    conv, a, cv, merged, h1, hn = _mix_fwd(xs, attn, c3, gates, conv_full, w_br, w_out_full, ffn_norm, tm2, comm=k3)
    w_up_full = k3.out["w_up"]
    w_down_full = _exchange("gather_down_d2d", [[_gather_d2d(lay["w_down"], "w_down")]],
                            bufs=_finish_exchange("gather_down_wait", fly_down, after=hn))["w_down"].reshape(D_FF, D_MODEL)
    u, up, act, dh2, loss_part, g_fn = _ffn_fwd_loss(hn, h1, w_up_full, ffn_cw_full, w_down_full,
                                                     final_norm[None, :], target, tm)

    grads, sums, slots = {}, {}, {}

    def pair(*ws):
        return _Carry([_rs_pair(lay[n], "g_" + n, "t_" + n) for n in ws], reads={"g_" + n: grads[n] for n in ws},
                      fresh={"t_" + n: _theirs_shape(lay[n], grads[n].dtype) for n in ws})

    def chips(*ws, also=None):
        k = _Carry([_rs_chips(lay[n], "s_" + n, "r_" + n) for n in ws], reads={"s_" + n: sums[n] for n in ws},
                   fresh={"r_" + n: _slots_shape(lay[n]) for n in ws})
        if also is not None:
            k = _Carry(k.jobs + also.jobs, {**k.reads, **also.reads}, None, {**k.fresh, **also.fresh})
        return k

    def pair_sums(k, *ws):
        for n in ws:
            sums[n] = _pair_sum(grads[n], k.out["t_" + n], lay[n], pos, "pair_sum_" + n)

    def take_slots(k, *ws):
        for n in ws:
            slots[n] = k.out["r_" + n]

    du, dh1, g_fcw, g_g2 = _ffn_bwd(dh2, u, up, h1, w_up_full, ffn_cw_full, w_down_full, ffn_norm, tm)
    grads["w_down"] = _wgrad(act, dh2, D_FF // 2, D_MODEL, bk2, "wgrad_down").reshape(lay["w_down"].whole())
    k4 = pair("w_down")
    grads["w_up"] = _wgrad(hn, du, D_MODEL, FF2 // 4, bk2, "wgrad_up", comm=k4)
    pair_sums(k4, "w_down")
    k5 = chips("w_down", also=pair("w_up"))
    dattn, dc3, dgt, g_cw, grads["w_br"], gw_out = _mix_bwd(
        dh1, gates, a, cv, c3, attn, conv, merged, conv_full, w_br, w_out_full, tm2, comm=k5)
    grads["w_out"] = gw_out.reshape(lay["w_out"].whole())
    take_slots(k5, "w_down")
    pair_sums(k5, "w_up")
    up_flight, started = _start_exchange("rs_chips_up_start", [_rs_chips(lay["w_up"], "s", "r")],
                                         {"s": sums["w_up"], "r": _slots_shape(lay["w_up"])})
    k6 = pair(*trio)
    k6.reads["after"] = started
    dq, dk_even, dk_odd, dv_even, dv_odd, g_sk = _attn_bwd(qkv, sk, attn, dattn, comm=k6)
    pair_sums(k6, *trio)
    trio_flight, started = _start_exchange(
        "rs_chips_trio_start", [_rs_chips(lay[n], "s_" + n, "r_" + n) for n in trio],
        {**{"s_" + n: sums[n] for n in trio}, **{"r_" + n: _slots_shape(lay[n]) for n in trio}})
    behind = mix_norm + jnp.tile(started[0:1], (1, D_MODEL // 128))
    grad_x, gw_in, g_b, g_g1 = _inproj_bwd(dq, (dk_even, dk_odd), (dv_even, dv_odd), dc3, dgt, w_in_full, xs, xn,
                                           dh1, behind)
    grads["w_in"] = gw_in.reshape(lay["w_in"].whole())

    parts = [loss_part, g_g1, g_b, jnp.pad(g_sk[:, 0], (0, 120))[None, :], g_cw, g_g2, g_fcw, g_fn]
    packed, at = _pack_rows([p.reshape(-1, 128) for p in parts])
    small_flight, started = _start_exchange("small_start", [_to_all("v", "slots")],
                                            {"v": packed, "slots": jnp.zeros((N_DEV, *packed.shape), F32)})
    in_flight, started = _start_exchange("rs_pair_in_start", [_rs_pair(lay["w_in"], "g", "t")],
                                         {"g": grads["w_in"], "t": _theirs_shape(lay["w_in"]), "behind": started})
    landed = _finish_exchange("rs_chips_up_wait", up_flight, after=started)
    halves = {"w_up": _chip_sum(landed["s"], landed["r"], lay["w_up"], pos, "chip_sum_w_up"),
              "w_down": _chip_sum(sums["w_down"], slots["w_down"], lay["w_down"], pos, "chip_sum_w_down", after=started)}
    landed = _finish_exchange("rs_pair_in_wait", in_flight, after=halves["w_down"])
    sums["w_in"] = _pair_sum(landed["g"], landed["t"], lay["w_in"], pos, "pair_sum_w_in")
    in_flight, started = _start_exchange("rs_chips_in_start", [_rs_chips(lay["w_in"], "s", "r")],
                                         {"s": sums["w_in"], "r": _slots_shape(lay["w_in"])})
    landed = _finish_exchange("rs_chips_trio_wait", trio_flight, after=started)
    for n in trio:
        halves[n] = _chip_sum(landed["s_" + n], landed["r_" + n], lay[n], pos, "chip_sum_" + n)
    shared = _exchange("share_halves", [[_rs_share(lay[n], n) for n in names[1:]]], bufs=halves)

    def adam(n, g, after=None):
        return _rowwise(lambda w, g, m, v: (g, *_adamw(w, g, m, v)), [w_of[n], g, m_of[n], v_of[n]], [F32] * 4,
                        "adamw_" + n, after=after)

    new_of, last = {}, None
    for n in ("w_up", "w_down", "w_out"):
        new_of[n] = adam(n, shared[n], last)
        last = new_of[n][1]
    new_of["w_ab"], new_of["w_cb"] = _adamw_pair(ab, cb, shared["w_br"], after=last)
    last = new_of["w_cb"][1]

    arrived = _finish_exchange("small_wait", small_flight, after=last)
    total = _sum_slots(arrived["v"], arrived["slots"], pos)
    part = lambda k: total[at[k]:at[k] + parts[k].size // 128].reshape(parts[k].shape)
    loss = total[0, 0]
    g_mix, g_b, g_g2, g_fn = part(1), part(2), part(5), part(7)
    g_sk = part(3)[:, 0:N_HEADS]
    g_cw = lax.dynamic_slice(part(4), (0, me * 128), (3, 128))
    g_fcw = lax.dynamic_slice(part(6), (0, me * (FF2 // N_CHIPS)), (3, FF2 // N_CHIPS))
    small_p = [
        (mix_norm, g_mix, m_mix_norm, v_mix_norm), (b_in, g_b, m_b_in, v_b_in), (sinks, g_sk, m_sinks, v_sinks),
        (conv_w[0], g_cw, m_conv_w[0], v_conv_w[0]), (ffn_norm, g_g2, m_ffn_norm, v_ffn_norm),
        (ffn_conv_w[0], g_fcw, m_ffn_conv_w[0], v_ffn_conv_w[0]),
        (final_norm[None, :], g_fn, m_final_norm[None, :], v_final_norm[None, :])]
    small_new = _adamw_small(small_p)
    small_new = [small_new[3 * k:3 * k + 3] for k in range(len(small_p))]

    landed = _finish_exchange("rs_chips_in_wait", in_flight, after=small_new[0][0])
    half_in = _chip_sum(landed["s"], landed["r"], lay["w_in"], pos, "chip_sum_w_in")
    shared["w_in"] = _exchange("share_in", [[_rs_share(lay["w_in"], "w_in")]], bufs={"w_in": half_in})["w_in"]
    new_of["w_in"] = [a.T for a in adam("w_in", shared["w_in"])]
    big = ("w_in", "w_ab", "w_cb", "w_out", "w_up", "w_down")
    big_g = [new_of[n][0] for n in big]
    big_new = [new_of[n][1:] for n in big]

    order = [("s", 0), ("b", 0), ("s", 1), ("s", 2), ("s", 3), ("b", 1), ("b", 2), ("b", 3), ("s", 4), ("b", 4),
             ("s", 5), ("b", 5), ("s", 6)]
    shapes = [mix_norm.shape, w_in.shape, b_in.shape, sinks.shape, conv_w.shape, w_attn_branch.shape,
              w_conv_branch.shape, w_out.shape, ffn_norm.shape, w_up.shape, ffn_conv_w.shape, w_down.shape,
              final_norm.shape]
    small_g = [p[1] for p in small_p]
    out_g = [(small_g[k] if kind == "s" else big_g[k]).reshape(shp) for (kind, k), shp in zip(order, shapes)]
    news = [[(small_new[k][j] if kind == "s" else big_new[k][j]).reshape(shp) for (kind, k), shp in zip(order, shapes)]
            for j in range(3)]
    return (loss, grad_x[None], *out_g, *news[0], *news[1], *news[2])
```

```python
import functools

import jax
import jax.numpy as jnp
from jax import lax
from jax.experimental import pallas as pl
from jax.experimental.pallas import tpu as pltpu

F32 = jnp.float32
BF16 = jnp.bfloat16

D_MODEL = 1024
HEAD_DIM = 64
N_HEADS = 8
N_KV_HEADS = 2
GROUP = N_HEADS // N_KV_HEADS
BLOCK = 128
ATTN_SCALE = HEAD_DIM ** -0.5
ATTN_W = N_HEADS * HEAD_DIM
KV_W = N_KV_HEADS * HEAD_DIM
CONV_W = 512
QKV_W = ATTN_W + 2 * KV_W
C3_W = 3 * CONV_W
GATES_W = 2 * D_MODEL
IN_W = QKV_W + C3_W + GATES_W
D_FF = 2816
FF2 = 2 * D_FF
NORM_EPS = 1e-5
N_CHIPS = 4
IN_SHARD = IN_W // N_CHIPS
NEG = -1e30

ADAM_LR = 0.001
ADAM_B1 = 0.9
ADAM_B2 = 0.999
ADAM_EPS = 1e-08
ADAM_WD = 0.01
ADAM_STEP = 10

VMEM_LIMIT = 56 * 1024 * 1024
MESH = pl.DeviceIdType.MESH

NT = (((1,), (1,)), ((), ()))
TN = (((0,), (0,)), ((), ()))


def _params(*sem):
    return pltpu.CompilerParams(dimension_semantics=sem, vmem_limit_bytes=VMEM_LIMIT)


def _resident(shape):
    return pl.BlockSpec(shape, lambda *_: (0,) * len(shape), pipeline_mode=pl.Buffered(1))


def _sigmoid(v):
    return 0.5 * jnp.tanh(0.5 * v) + 0.5


def _rstd(v):
    return lax.rsqrt(jnp.mean(v * v, axis=-1, keepdims=True) + NORM_EPS)


def _rms_bwd(dy, v, rstd, g):
    vhat = v * rstd
    t = dy * g
    return rstd * (t - vhat * jnp.mean(t * vhat, axis=-1, keepdims=True)), dy * vhat


def _taps(z, cw):
    return cw[2:3] * z + cw[1:2] * pltpu.roll(z, 1, 0) + cw[0:1] * pltpu.roll(z, 2, 0)


def _causal_conv(z, prev, cw):
    edge = _taps(jnp.concatenate([prev, z[0:8]], axis=0), cw)
    return jnp.concatenate([edge[8:16], _taps(z, cw)[8:]], axis=0)


def _rows_after(z, nxt):
    n = z.shape[0]
    edge = jnp.concatenate([z[n - 8:n], nxt], axis=0)
    return tuple(jnp.concatenate([pltpu.roll(z, n - k, 0)[:n - 8], pltpu.roll(edge, 16 - k, 0)[0:8]], axis=0)
                 for k in (1, 2))


def _inproj_fwd(x, g1, w_in, b_in, tm, comm=None):
    s = x.shape[0]

    def body(x_ref, g_ref, w_ref, b_ref, xn_ref, qkv_ref, c3_ref, gt_ref):
        xf = x_ref[...]
        xn = (xf * _rstd(xf) * g_ref[...]).astype(BF16)
        xn_ref[...] = xn

        proj = (lax.dot_general(xn, w_ref[...], NT, preferred_element_type=F32) + b_ref[...]).astype(BF16)
        qkv_ref[...] = proj[:, :QKV_W]
        c3_ref[...] = proj[:, QKV_W:QKV_W + C3_W]
        gt_ref[...] = proj[:, QKV_W + C3_W:]

    row = lambda w: pl.BlockSpec((tm, w), lambda i: (i, 0))
    return _call(
        comm, body, name="inproj_fwd", grid=(s // tm,),
        in_specs=[row(D_MODEL), _resident((1, D_MODEL)), _resident((IN_W, D_MODEL)), _resident((1, IN_W))],
        out_specs=[row(D_MODEL), row(QKV_W), row(C3_W), row(GATES_W)],
        out_shape=[jax.ShapeDtypeStruct((s, D_MODEL), BF16), jax.ShapeDtypeStruct((s, QKV_W), BF16),
                   jax.ShapeDtypeStruct((s, C3_W), BF16), jax.ShapeDtypeStruct((s, GATES_W), BF16)],
        compiler_params=_params("parallel"),
    )(x, g1, w_in, b_in)


def _attn_bias():
    kj = jnp.arange(2 * BLOCK)[:, None]
    qi = (jnp.arange(GROUP * BLOCK) % BLOCK)[None, :]
    band = (kj > qi) & (kj <= qi + BLOCK)
    return jnp.stack([jnp.where(band & (kj >= BLOCK), 0.0, NEG), jnp.where(band, 0.0, NEG)]).astype(F32)


def _attn_bias_specs():
    shape = (None, 2 * BLOCK, GROUP * BLOCK)
    return pl.BlockSpec(shape, lambda i: (jnp.minimum(i, 1), 0, 0)), pl.BlockSpec(shape, lambda i: (1, 0, 0))


def _sink_row(sk_ref, h):
    lane = lax.broadcasted_iota(jnp.int32, (1, GROUP * BLOCK), 1)
    row = jnp.full((1, GROUP * BLOCK), sk_ref[h * GROUP], F32)
    for g in range(1, GROUP):
        row = jnp.where(lane >= g * BLOCK, sk_ref[h * GROUP + g], row)
    return row


def _stack_heads(t, h):
    return jnp.concatenate(
        [t[:, (h * GROUP + g) * HEAD_DIM:(h * GROUP + g + 1) * HEAD_DIM] for g in range(GROUP)], axis=0)


def _unstack_heads(per_kv):
    return jnp.concatenate(
        [t[g * BLOCK:(g + 1) * BLOCK] for t in per_kv for g in range(GROUP)], axis=1)


def _block_specs(n, steps):
    cur = lambda i: jnp.minimum(i, steps - 1)
    prev = lambda i: jnp.maximum(n * jnp.minimum(i, steps - 1) - 1, 0)
    kv = ATTN_W // KV_W
    return (pl.BlockSpec((n * BLOCK, ATTN_W), lambda i: (cur(i), 0)),
            pl.BlockSpec((BLOCK, KV_W), lambda i: (prev(i), kv)), pl.BlockSpec((n * BLOCK, KV_W), lambda i: (cur(i), kv)),
            pl.BlockSpec((BLOCK, KV_W), lambda i: (prev(i), kv + 1)),
            pl.BlockSpec((n * BLOCK, KV_W), lambda i: (cur(i), kv + 1)))


def _attn_fwd(qkv, sinks, comm=None):
    s = qkv.shape[0]
    n = min(4, s // BLOCK)
    steps = s // (n * BLOCK)

    def body(sk_ref, bias0_ref, bias1_ref, q_ref, kp_ref, kc_ref, vp_ref, vc_ref, o_ref):
        kc, vc = kc_ref[...], vc_ref[...]
        for b in range(n):
            rows, before = slice(b * BLOCK, (b + 1) * BLOCK), slice((b - 1) * BLOCK, b * BLOCK)
            kp, vp = (kp_ref[...], vp_ref[...]) if b == 0 else (kc[before], vc[before])
            q, bias = q_ref[rows, :], (bias0_ref if b == 0 else bias1_ref)[...]
            outs = []
            for h in range(N_KV_HEADS):
                hs = slice(h * HEAD_DIM, (h + 1) * HEAD_DIM)
                k2 = jnp.concatenate([kp[:, hs], kc[rows, hs]], axis=0)
                v2 = jnp.concatenate([vp[:, hs], vc[rows, hs]], axis=0)
                sc = lax.dot_general(k2, _stack_heads(q, h), NT, preferred_element_type=F32) * ATTN_SCALE + bias
                sink = _sink_row(sk_ref, h)
                m = jnp.maximum(jnp.max(sc, axis=0, keepdims=True), sink)
                p = jnp.exp(sc - m)
                den = jnp.sum(p, axis=0, keepdims=True) + jnp.exp(sink - m)
                out = lax.dot_general(v2, p.astype(BF16), TN, preferred_element_type=F32) / den
                outs.append(out.T)
            o_ref[rows, :] = _unstack_heads(outs).astype(BF16)

    return _call(
        comm, body, name="attn_fwd", grid=(steps,),
        in_specs=[pl.BlockSpec(memory_space=pltpu.SMEM), *_attn_bias_specs(), *_block_specs(n, steps)],
        out_specs=pl.BlockSpec((n * BLOCK, ATTN_W), lambda i: (i, 0)),
        out_shape=jax.ShapeDtypeStruct((s, ATTN_W), BF16),
        compiler_params=_params("parallel"),
    )(sinks, _attn_bias(), _attn_bias(), qkv, qkv, qkv, qkv, qkv)


def _mix_fwd(x, attn, c3, gates, conv_w, w_br, w_out, g2, tm, comm=None):
    s = x.shape[0]

    def body(x_ref, at_ref, c3_ref, gt_ref, cw_ref, wbr_ref, wo_ref, g_ref,
             conv_ref, a_ref, cv_ref, mg_ref, h1_ref, hn_ref, carry_ref):
        @pl.when(pl.program_id(0) == 0)
        def _():
            carry_ref[...] = jnp.zeros_like(carry_ref)

        c3v = c3_ref[...].astype(F32)
        cb, cc, cx = c3v[:, :CONV_W], c3v[:, CONV_W:2 * CONV_W], c3v[:, 2 * CONV_W:]
        z = cc * cx
        cz = _causal_conv(z, carry_ref[...], cw_ref[...])
        carry_ref[...] = z[tm - 8:tm]
        conv = (cb * cz).astype(BF16)
        conv_ref[...] = conv
        a = jnp.dot(at_ref[...], wbr_ref[:ATTN_W, :], preferred_element_type=F32)
        cv = jnp.dot(conv, wbr_ref[ATTN_W:, :], preferred_element_type=F32)
        a_ref[...] = a.astype(BF16)
        cv_ref[...] = cv.astype(BF16)
        gt = gt_ref[...].astype(F32)
        merged = (_sigmoid(gt[:, :D_MODEL]) * a + _sigmoid(gt[:, D_MODEL:]) * cv).astype(BF16)
        mg_ref[...] = merged
        h1 = x_ref[...] + jnp.dot(merged, wo_ref[...], preferred_element_type=F32)
        h1_ref[...] = h1
        hn_ref[...] = (h1 * _rstd(h1) * g_ref[...]).astype(BF16)

    row = lambda w: pl.BlockSpec((tm, w), lambda i: (i, 0))
    return _call(
        comm, body, name="mix_fwd", grid=(s // tm,),
        in_specs=[row(D_MODEL), row(ATTN_W), row(C3_W), row(GATES_W), _resident((3, CONV_W)),
                  _resident((ATTN_W + CONV_W, D_MODEL)), _resident((D_MODEL, D_MODEL)), _resident((1, D_MODEL))],
        out_specs=[row(CONV_W), row(D_MODEL), row(D_MODEL), row(D_MODEL), row(D_MODEL), row(D_MODEL)],
        out_shape=[jax.ShapeDtypeStruct((s, CONV_W), BF16), jax.ShapeDtypeStruct((s, D_MODEL), BF16),
                   jax.ShapeDtypeStruct((s, D_MODEL), BF16), jax.ShapeDtypeStruct((s, D_MODEL), BF16),
                   jax.ShapeDtypeStruct((s, D_MODEL), F32), jax.ShapeDtypeStruct((s, D_MODEL), BF16)],
        scratch_shapes=[pltpu.VMEM((8, CONV_W), F32)],
        compiler_params=_params("arbitrary"),
    )(x, attn, c3, gates, conv_w, w_br, w_out, g2)


def _ffn_fwd_loss(hn, h1, w_up, ffn_cw, w_down, g3, target, tm):
    s = hn.shape[0]

    def body(hn_ref, h1_ref, wu_ref, cw_ref, wd_ref, g_ref, t_ref,
             u_ref, up_ref, act_ref, dh2_ref, loss_ref, gfn_ref, carry_ref):
        @pl.when(pl.program_id(0) == 0)
        def _():
            carry_ref[...] = jnp.zeros_like(carry_ref)
            loss_ref[...] = jnp.zeros_like(loss_ref)
            gfn_ref[...] = jnp.zeros_like(gfn_ref)

        u = jnp.dot(hn_ref[...], wu_ref[...], preferred_element_type=F32)
        u_ref[...] = u.astype(BF16)
        up = _causal_conv(u, carry_ref[...], cw_ref[...])
        up_ref[...] = up
        carry_ref[...] = u[tm - 8:tm]
        gate, val = up[:, :D_FF], up[:, D_FF:]
        act = (gate * _sigmoid(gate) * val).astype(BF16)
        act_ref[...] = act
        h2 = h1_ref[...] + jnp.dot(act, wd_ref[...], preferred_element_type=F32)
        rstd = _rstd(h2)
        g = g_ref[...]
        err = h2 * rstd * g - t_ref[...]
        loss_ref[...] += jnp.sum(err * err) * (0.5 / D_MODEL)
        dh2, dg = _rms_bwd(err * (1.0 / D_MODEL), h2, rstd, g)
        dh2_ref[...] = dh2
        gfn_ref[...] += jnp.sum(dg, axis=0, keepdims=True)

    row = lambda w: pl.BlockSpec((tm, w), lambda i: (i, 0))
    acc = lambda w: pl.BlockSpec((1, w), lambda i: (0, 0))
    return pl.pallas_call(
        body, name="ffn_fwd_loss", grid=(s // tm,),
        in_specs=[row(D_MODEL), row(D_MODEL), _resident((D_MODEL, FF2)), _resident((3, FF2)),
                  _resident((D_FF, D_MODEL)), _resident((1, D_MODEL)), row(D_MODEL)],
        out_specs=[row(FF2), row(FF2), row(D_FF), row(D_MODEL), acc(128), acc(D_MODEL)],
        out_shape=[jax.ShapeDtypeStruct((s, FF2), BF16), jax.ShapeDtypeStruct((s, FF2), F32),
                   jax.ShapeDtypeStruct((s, D_FF), BF16),
                   jax.ShapeDtypeStruct((s, D_MODEL), F32), jax.ShapeDtypeStruct((1, 128), F32),
                   jax.ShapeDtypeStruct((1, D_MODEL), F32)],
        scratch_shapes=[pltpu.VMEM((8, FF2), F32)],
        compiler_params=_params("arbitrary"),
    )(hn, h1, w_up, ffn_cw, w_down, g3, target)


def _ffn_bwd(dh2, u, up, h1, w_up, ffn_cw, w_down, g2, tm):
    s = dh2.shape[0]
    nt = s // tm

    def body(dh2_ref, u_ref, up_ref, h1_ref, wu_ref, cw_ref, wd_ref, g_ref,
             du_ref, dh1_ref, gcw_ref, gg_ref, carry_ref):
        @pl.when(pl.program_id(0) == 0)
        def _():
            for ref in (carry_ref, gcw_ref, gg_ref):
                ref[...] = jnp.zeros_like(ref)

        dh2v = dh2_ref[...]
        dact = lax.dot_general(dh2v.astype(BF16), wd_ref[...], NT, preferred_element_type=F32)
        upv = up_ref[...]
        gate, val = upv[:, :D_FF], upv[:, D_FF:]
        sg = _sigmoid(gate)
        dval = dact * (gate * sg)
        dgate = dact * val * (sg * (1.0 + gate * (1.0 - sg)))
        dup = jnp.concatenate([dgate, dval], axis=1)
        dup1, dup2 = _rows_after(dup, carry_ref[...])
        carry_ref[...] = dup[0:8]
        u = u_ref[...].astype(F32)
        gcw_ref[2:3, :] += jnp.sum(dup * u, axis=0, keepdims=True)
        gcw_ref[1:2, :] += jnp.sum(dup1 * u, axis=0, keepdims=True)
        gcw_ref[0:1, :] += jnp.sum(dup2 * u, axis=0, keepdims=True)
        cw = cw_ref[...]
        du = (cw[2:3] * dup + cw[1:2] * dup1 + cw[0:1] * dup2).astype(BF16)
        du_ref[...] = du
        dhn = lax.dot_general(du, wu_ref[...], NT, preferred_element_type=F32)
        h1v = h1_ref[...]
        dh1, dg = _rms_bwd(dhn, h1v, _rstd(h1v), g_ref[...])
        dh1_ref[...] = dh2v + dh1
        gg_ref[...] += jnp.sum(dg, axis=0, keepdims=True)

    row = lambda w: pl.BlockSpec((tm, w), lambda i: (nt - 1 - i, 0))
    return pl.pallas_call(
        body, name="ffn_bwd", grid=(nt,),
        in_specs=[row(D_MODEL), row(FF2), row(FF2),
                  row(D_MODEL), _resident((D_MODEL, FF2)), _resident((3, FF2)), _resident((D_FF, D_MODEL)),
                  _resident((1, D_MODEL))],
        out_specs=[row(FF2), row(D_MODEL), pl.BlockSpec((3, FF2), lambda i: (0, 0)),
                   pl.BlockSpec((1, D_MODEL), lambda i: (0, 0))],
        out_shape=[jax.ShapeDtypeStruct((s, FF2), BF16), jax.ShapeDtypeStruct((s, D_MODEL), F32),
                   jax.ShapeDtypeStruct((3, FF2), F32), jax.ShapeDtypeStruct((1, D_MODEL), F32)],
        scratch_shapes=[pltpu.VMEM((8, FF2), F32)],
        compiler_params=_params("arbitrary"),
    )(dh2, u, up, h1, w_up, ffn_cw, w_down, g2)


def _mix_bwd(dh1, gates, a, cv, c3, attn, conv, merged, conv_w, w_br, w_out, tm, comm=None):
    s = dh1.shape[0]
    nt = s // tm
    halo = 16

    def body(dh1_ref, gt_ref, a_ref, cv_ref, c3_ref, ch_ref, at_ref, cn_ref, mg_ref, cw_ref, wbr_ref,
             wo_ref, dat_ref, dc3_ref, dgt_ref, gcw_ref, gbr_ref, gout_ref, carry_ref, br_acc, out_acc):
        i = pl.program_id(0)

        @pl.when(i == 0)
        def _():
            for ref in (carry_ref, gcw_ref, br_acc, out_acc):
                ref[...] = jnp.zeros_like(ref)

        dh1v = dh1_ref[...].astype(BF16)
        out_acc[...] += lax.dot_general(mg_ref[...], dh1v, TN, preferred_element_type=F32)
        dm = lax.dot_general(dh1v, wo_ref[...], NT, preferred_element_type=F32)
        gt = gt_ref[...].astype(F32)
        sa, sc = _sigmoid(gt[:, :D_MODEL]), _sigmoid(gt[:, D_MODEL:])
        da = (dm * sa).astype(BF16)
        dcv = (dm * sc).astype(BF16)
        br_acc[:ATTN_W, :] += lax.dot_general(at_ref[...], da, TN, preferred_element_type=F32)
        br_acc[ATTN_W:, :] += lax.dot_general(cn_ref[...], dcv, TN, preferred_element_type=F32)
        dgt_ref[...] = jnp.concatenate(
            [dm * a_ref[...].astype(F32) * (sa * (1.0 - sa)), dm * cv_ref[...].astype(F32) * (sc * (1.0 - sc))],
            axis=1).astype(BF16)
        dat_ref[...] = lax.dot_general(da, wbr_ref[:ATTN_W, :], NT, preferred_element_type=F32).astype(BF16)
        dconv = lax.dot_general(dcv, wbr_ref[ATTN_W:, :], NT, preferred_element_type=F32)
        c3v = c3_ref[...].astype(F32)
        cb, cc, cx = c3v[:, :CONV_W], c3v[:, CONV_W:2 * CONV_W], c3v[:, 2 * CONV_W:]
        z = cc * cx
        chv = ch_ref[...].astype(F32)[halo - 8:halo] * (i < nt - 1).astype(F32)
        zh = chv[:, CONV_W:2 * CONV_W] * chv[:, 2 * CONV_W:]
        cw = cw_ref[...]
        cz = _causal_conv(z, zh, cw)
        dcz = dconv * cb
        dcz1, dcz2 = _rows_after(dcz, carry_ref[...])
        carry_ref[...] = dcz[0:8]
        gcw_ref[2:3, :] += jnp.sum(dcz * z, axis=0, keepdims=True)
        gcw_ref[1:2, :] += jnp.sum(dcz1 * z, axis=0, keepdims=True)
        gcw_ref[0:1, :] += jnp.sum(dcz2 * z, axis=0, keepdims=True)
        dz = cw[2:3] * dcz + cw[1:2] * dcz1 + cw[0:1] * dcz2
        dc3_ref[...] = jnp.concatenate([dconv * cz, dz * cx, dz * cc], axis=1).astype(BF16)

        @pl.when(i == nt - 1)
        def _():
            gbr_ref[...] = br_acc[...].astype(BF16)
            gout_ref[...] = out_acc[...].astype(BF16)

    row = lambda w: pl.BlockSpec((tm, w), lambda i: (nt - 1 - i, 0))
    return _call(
        comm, body, name="mix_bwd", grid=(nt,),
        in_specs=[row(D_MODEL), row(GATES_W), row(D_MODEL), row(D_MODEL), row(C3_W),
                  pl.BlockSpec((halo, C3_W), lambda i: (jnp.maximum((nt - 1 - i) * (tm // halo) - 1, 0), 0)),
                  row(ATTN_W), row(CONV_W), row(D_MODEL), _resident((3, CONV_W)),
                  _resident((ATTN_W + CONV_W, D_MODEL)), _resident((D_MODEL, D_MODEL))],
        out_specs=[row(ATTN_W), row(C3_W), row(GATES_W), pl.BlockSpec((3, CONV_W), lambda i: (0, 0)),
                   _resident((ATTN_W + CONV_W, D_MODEL)), _resident((D_MODEL, D_MODEL))],
        out_shape=[jax.ShapeDtypeStruct((s, ATTN_W), BF16), jax.ShapeDtypeStruct((s, C3_W), BF16),
                   jax.ShapeDtypeStruct((s, GATES_W), BF16), jax.ShapeDtypeStruct((3, CONV_W), F32),
                   jax.ShapeDtypeStruct((ATTN_W + CONV_W, D_MODEL), BF16),
                   jax.ShapeDtypeStruct((D_MODEL, D_MODEL), BF16)],
        scratch_shapes=[pltpu.VMEM((8, CONV_W), F32), pltpu.VMEM((ATTN_W + CONV_W, D_MODEL), F32),
                        pltpu.VMEM((D_MODEL, D_MODEL), F32)],
        compiler_params=_params("arbitrary"),
    )(dh1, gates, a, cv, c3, c3, attn, conv, merged, conv_w, w_br, w_out)


def _attn_bwd(qkv, sinks, o, do, comm=None):
    s = qkv.shape[0]
    npair = s // (2 * BLOCK)

    def one_block(sk_ref, bias, q, kp, kc, vp, vc, ov, dov, dsk_ref):
        dqs, dks, dvs = [], [], []
        for h in range(N_KV_HEADS):
            hs = slice(h * HEAD_DIM, (h + 1) * HEAD_DIM)
            k2 = jnp.concatenate([kp[:, hs], kc[:, hs]], axis=0)
            v2 = jnp.concatenate([vp[:, hs], vc[:, hs]], axis=0)
            qg, og, dog = _stack_heads(q, h), _stack_heads(ov, h), _stack_heads(dov, h)
            sc = lax.dot_general(k2, qg, NT, preferred_element_type=F32) * ATTN_SCALE + bias
            sink = _sink_row(sk_ref, h)
            m = jnp.maximum(jnp.max(sc, axis=0, keepdims=True), sink)
            p = jnp.exp(sc - m)
            psink = jnp.exp(sink - m)
            inv = 1.0 / (jnp.sum(p, axis=0, keepdims=True) + psink)
            p = p * inv
            delta = jnp.sum(dog.astype(F32) * og.astype(F32), axis=1, keepdims=True).T
            dp = lax.dot_general(v2, dog, NT, preferred_element_type=F32)
            ds = (p * (dp - delta)).astype(BF16)
            dqs.append((lax.dot_general(k2, ds, TN, preferred_element_type=F32) * ATTN_SCALE).T)
            dks.append(jnp.dot(ds, qg, preferred_element_type=F32) * ATTN_SCALE)
            dvs.append(jnp.dot(p.astype(BF16), dog, preferred_element_type=F32))
            dsink = -(psink * inv * delta)
            for g in range(GROUP):
                r = h * GROUP + g
                dsk_ref[r:r + 1, :] += jnp.sum(dsink[:, g * BLOCK:(g + 1) * BLOCK])
        return _unstack_heads(dqs), jnp.concatenate(dks, axis=1), jnp.concatenate(dvs, axis=1)

    def body(sk_ref, bias0_ref, bias1_ref, q_ref, kp_ref, kc_ref, vp_ref, vc_ref, o_ref, do_ref,
             dq_ref, dke_ref, dko_ref, dve_ref, dvo_ref, dsk_ref, ck_ref, cvv_ref):
        i = pl.program_id(0)

        @pl.when(i == 0)
        def _():
            for ref in (ck_ref, cvv_ref, dsk_ref):
                ref[...] = jnp.zeros_like(ref)

        @pl.when(i < npair)
        def _():
            kc, vc = kc_ref[...], vc_ref[...]
            first, second = slice(0, BLOCK), slice(BLOCK, 2 * BLOCK)
            dq0, dk0, dv0 = one_block(sk_ref, bias0_ref[...], q_ref[first, :], kp_ref[...], kc[first], vp_ref[...],
                                      vc[first], o_ref[first, :], do_ref[first, :], dsk_ref)
            dq1, dk1, dv1 = one_block(sk_ref, bias1_ref[...], q_ref[second, :], kc[first], kc[second], vc[first],
                                      vc[second], o_ref[second, :], do_ref[second, :], dsk_ref)
            dq_ref[first, :] = dq0.astype(BF16)
            dq_ref[second, :] = dq1.astype(BF16)
            dko_ref[...] = (ck_ref[...] + dk0[:BLOCK]).astype(BF16)
            dvo_ref[...] = (cvv_ref[...] + dv0[:BLOCK]).astype(BF16)
            dke_ref[...] = (dk0[BLOCK:] + dk1[:BLOCK]).astype(BF16)
            dve_ref[...] = (dv0[BLOCK:] + dv1[:BLOCK]).astype(BF16)
            ck_ref[...] = dk1[BLOCK:]
            cvv_ref[...] = dv1[BLOCK:]

        @pl.when(i == npair)
        def _():
            dko_ref[...] = ck_ref[...].astype(BF16)
            dvo_ref[...] = cvv_ref[...].astype(BF16)

    cur = lambda i: jnp.minimum(i, npair - 1)
    done = lambda i: jnp.maximum(i - 1, 0)
    rows = pl.BlockSpec((2 * BLOCK, ATTN_W), lambda i: (cur(i), 0))
    even = pl.BlockSpec((BLOCK, KV_W), lambda i: (cur(i), 0))
    odd = pl.BlockSpec((BLOCK, KV_W), lambda i: (done(i), 0))
    half = jax.ShapeDtypeStruct((s // 2, KV_W), BF16)
    return _call(
        comm, body, name="attn_bwd", grid=(npair + 1,),
        in_specs=[pl.BlockSpec(memory_space=pltpu.SMEM), *_attn_bias_specs(), *_block_specs(2, npair), rows, rows],
        out_specs=[rows, even, odd, even, odd, pl.BlockSpec((N_HEADS, 128), lambda i: (0, 0))],
        out_shape=[jax.ShapeDtypeStruct((s, ATTN_W), BF16), half, half, half, half,
                   jax.ShapeDtypeStruct((N_HEADS, 128), F32)],
        scratch_shapes=[pltpu.VMEM((BLOCK, KV_W), F32), pltpu.VMEM((BLOCK, KV_W), F32)],
        compiler_params=_params("arbitrary"),
    )(sinks, _attn_bias(), _attn_bias(), qkv, qkv, qkv, qkv, qkv, o, do)


def _inproj_bwd(dq, dk, dv, dc3, dgt, w_in, x, xn, dh1, g1):
    s = x.shape[0]
    tm = min(2 * BLOCK, s)
    nt = s // tm

    def body(dq_ref, dke_ref, dko_ref, dve_ref, dvo_ref, dc3_ref, dgt_ref, w_ref, x_ref, xn_ref, dh1_ref, g_ref,
             dx_ref, gw_ref, gb_ref, gg_ref, acc_ref):
        i = pl.program_id(0)

        @pl.when(i == 0)
        def _():
            for ref in (gb_ref, gg_ref, acc_ref):
                ref[...] = jnp.zeros_like(ref)

        dk = jnp.concatenate([dke_ref[...], dko_ref[...]], axis=0)
        dv = jnp.concatenate([dve_ref[...], dvo_ref[...]], axis=0)
        dp = jnp.concatenate([dq_ref[...], dk, dv, dc3_ref[...], dgt_ref[...]], axis=1)
        acc_ref[...] += lax.dot_general(dp, xn_ref[...], TN, preferred_element_type=F32)
        gb_ref[...] += jnp.sum(dp.astype(F32), axis=0, keepdims=True)
        dxn = jnp.dot(dp, w_ref[...], preferred_element_type=F32)
        xf = x_ref[...]
        dx, dg = _rms_bwd(dxn, xf, _rstd(xf), g_ref[...])
        dx_ref[...] = dh1_ref[...] + dx
        gg_ref[...] += jnp.sum(dg, axis=0, keepdims=True)

        @pl.when(i == nt - 1)
        def _():
            gw_ref[...] = acc_ref[...].astype(BF16)

    row = lambda w: pl.BlockSpec((tm, w), lambda i: (i, 0))
    acc = lambda w: pl.BlockSpec((1, w), lambda i: (0, 0))
    block = pl.BlockSpec((tm // 2, KV_W), lambda i: (i, 0))
    return pl.pallas_call(
        body, name="inproj_bwd", grid=(nt,),
        in_specs=[row(ATTN_W), block, block, block, block, row(C3_W), row(GATES_W), _resident((IN_W, D_MODEL)),
                  row(D_MODEL), row(D_MODEL), row(D_MODEL), _resident((1, D_MODEL))],
        out_specs=[row(D_MODEL), _resident((IN_W, D_MODEL)), acc(IN_W), acc(D_MODEL)],
        out_shape=[jax.ShapeDtypeStruct((s, D_MODEL), F32), jax.ShapeDtypeStruct((IN_W, D_MODEL), BF16),
                   jax.ShapeDtypeStruct((1, IN_W), F32), jax.ShapeDtypeStruct((1, D_MODEL), F32)],
        scratch_shapes=[pltpu.VMEM((IN_W, D_MODEL), F32)],
        compiler_params=_params("arbitrary"),
    )(dq, *dk, *dv, dc3, dgt, w_in, x, xn, dh1, g1)


def _wgrad(a, b, bm, bn, bk, name, comm=None):
    s, m = a.shape
    n = b.shape[1]
    nk = s // bk

    def body(a_ref, b_ref, o_ref, acc_ref):
        k = pl.program_id(2)

        @pl.when(k == 0)
        def _():
            acc_ref[...] = jnp.zeros_like(acc_ref)

        acc_ref[...] += lax.dot_general(a_ref[...].astype(BF16), b_ref[...].astype(BF16), TN,
                                        preferred_element_type=F32)

        @pl.when(k == nk - 1)
        def _():
            o_ref[...] = acc_ref[...].astype(BF16)

    return _call(
        comm, body, name=name, grid=(m // bm, n // bn, nk),
        in_specs=[pl.BlockSpec((bk, bm), lambda i, j, k: (k, i)), pl.BlockSpec((bk, bn), lambda i, j, k: (k, j))],
        out_specs=pl.BlockSpec((bm, bn), lambda i, j, k: (i, j)),
        out_shape=jax.ShapeDtypeStruct((m, n), BF16),
        scratch_shapes=[pltpu.VMEM((bm, bn), F32)],
        compiler_params=_params("parallel", "parallel", "arbitrary"),
    )(a, b)


class _Carry:
    def __init__(self, jobs, reads=None, bufs=None, fresh=None):
        self.jobs, self.reads, self.bufs, self.fresh = jobs, reads or {}, bufs or {}, fresh or {}
        self.out = {}


class _Job:
    def __init__(self, n_sems, plan):
        self.n_sems, self.plan = n_sems, plan


def _plan_all(jobs, hbm, send, recv):
    pos = _position()
    starts, waits, base = [], [], 0
    for job in jobs:
        s, w = job.plan(hbm, pos, send, recv, base)
        starts, waits, base = starts + s, waits + w, base + job.n_sems
    return starts, waits


def _call(comm, body, **kw):
    if comm is None:
        return pl.pallas_call(body, **kw)
    grid = kw["grid"]
    single = not isinstance(kw["out_shape"], (list, tuple))
    out_shape = [kw["out_shape"]] if single else list(kw["out_shape"])
    out_specs = [kw["out_specs"]] if single else list(kw["out_specs"])
    in_specs = list(kw["in_specs"])
    scratch = list(kw.get("scratch_shapes", ()))
    r_names, b_names, f_names = list(comm.reads), list(comm.bufs), list(comm.fresh)
    n_args, n_out, n_scr = len(in_specs), len(out_shape), len(scratch)
    n_sems = sum(j.n_sems for j in comm.jobs)

    def wrapped(*refs):
        k = n_args
        hbm = dict(zip(r_names, refs[k:k + len(r_names)]))
        k += len(r_names) + len(b_names)
        outs = refs[k:k + n_out]
        k += n_out
        hbm.update(zip(b_names + f_names, refs[k:k + len(b_names) + len(f_names)]))
        k += len(b_names) + len(f_names)
        send, recv = refs[k + n_scr:]
        starts, waits = _plan_all(comm.jobs, hbm, send, recv)
        ids = [pl.program_id(a) for a in range(len(grid))]
        first = functools.reduce(jnp.logical_and, [i == 0 for i in ids])
        last = functools.reduce(jnp.logical_and, [i == g - 1 for i, g in zip(ids, grid)])

        @pl.when(first)
        def _():
            for cp in starts:
                cp.start()

        body(*refs[:n_args], *outs, *refs[k:k + n_scr])

        @pl.when(last)
        def _():
            for cp in waits:
                cp.wait_recv()
            for cp in starts:
                cp.wait_send()

    sems = pltpu.SemaphoreType.DMA((n_sems,))
    held = [jax.ShapeDtypeStruct(a.shape, a.dtype) for a in comm.bufs.values()] + list(comm.fresh.values())
    call = pl.pallas_call(
        wrapped, name=kw["name"], grid=grid,
        in_specs=in_specs + [_ANY] * (len(r_names) + len(b_names)),
        out_specs=out_specs + [_ANY] * len(held),
        out_shape=out_shape + held,
        input_output_aliases={n_args + len(r_names) + i: n_out + i for i in range(len(b_names))},
        scratch_shapes=scratch + [sems, sems],
        compiler_params=_params(*["arbitrary"] * len(grid)),
    )

    def run(*args):
        res = call(*args, *comm.reads.values(), *comm.bufs.values())
        comm.out = dict(zip(b_names + f_names, res[n_out:]))
        return res[0] if single else res[:n_out]

    return run


def _exchange(name, phases, reads=None, bufs=None, fresh=None):
    comm = _Carry([j for ph in phases for j in ph], reads, bufs, fresh)
    r_names, b_names, f_names = list(comm.reads), list(comm.bufs), list(comm.fresh)
    n_sems = sum(j.n_sems for j in comm.jobs)

    def body(*refs):
        hbm = dict(zip(r_names, refs[:len(r_names)]))
        k = len(r_names) + len(b_names)
        hbm.update(zip(b_names + f_names, refs[k:k + len(b_names) + len(f_names)]))
        send, recv = refs[-2:]
        pos = _position()
        started, base = [], 0
        for ph in phases:
            waits = []
            for job in ph:
                s, w = job.plan(hbm, pos, send, recv, base)
                base += job.n_sems
                for cp in s:
                    cp.start()
                started, waits = started + s, waits + w
            for cp in waits:
                cp.wait_recv()
        for cp in started:
            cp.wait_send()

    sems = pltpu.SemaphoreType.DMA((n_sems,))
    held = [jax.ShapeDtypeStruct(a.shape, a.dtype) for a in comm.bufs.values()] + list(comm.fresh.values())
    res = pl.pallas_call(
        body, name=name, in_specs=[_ANY] * (len(r_names) + len(b_names)), out_specs=[_ANY] * len(held),
        out_shape=held, input_output_aliases={len(r_names) + i: i for i in range(len(b_names))},
        scratch_shapes=[sems, sems],
    )(*comm.reads.values(), *comm.bufs.values())
    return dict(zip(b_names + f_names, res))


_HBM = pl.BlockSpec(memory_space=pltpu.HBM)
_SEM = pl.BlockSpec(memory_space=pltpu.SEMAPHORE)
_EFFECT = pltpu.SideEffectType.DATAFLOW_SIDE_EFFECTING


def _start_exchanges(name, groups):
    names = [list(arrays) for _, arrays in groups]
    first = [sum(len(ns) for ns in names[:g]) for g in range(len(groups))]
    n, ng = sum(len(ns) for ns in names), len(groups)

    def body(*refs):
        for g, (jobs, _) in enumerate(groups):
            hbm = dict(zip(names[g], refs[first[g]:first[g] + len(names[g])]))
            for cp in _plan_all(jobs, hbm, refs[n + 2 * g], refs[n + 2 * g + 1])[0]:
                cp.start()
        refs[-1][...] = jnp.zeros_like(refs[-1])

    given = [pltpu.with_memory_space_constraint(
        a if isinstance(a, jax.Array) else lax.empty(a.shape, a.dtype), pltpu.HBM)
        for _, arrays in groups for a in arrays.values()]
    sems = [pltpu.SemaphoreType.DMA((sum(j.n_sems for j in jobs),)) for jobs, _ in groups for _ in range(2)]
    res = pl.pallas_call(
        body, name=name,
        out_shape=(*sems, *[pltpu.HBM(a.shape, a.dtype) for a in given], jax.ShapeDtypeStruct((8, 128), F32)),
        in_specs=[_HBM] * n, out_specs=(*[_SEM] * (2 * ng), *[_HBM] * n, pl.BlockSpec(memory_space=pltpu.VMEM)),
        input_output_aliases={i: 2 * ng + i for i in range(n)},
        compiler_params=pltpu.CompilerParams(has_side_effects=_EFFECT),
    )(*given)
    held = res[2 * ng:2 * ng + n]
    states = [(names[g], groups[g][0], res[2 * g], res[2 * g + 1], held[first[g]:first[g] + len(names[g])])
              for g in range(ng)]
    return states, res[-1]


def _start_exchange(name, jobs, arrays):
    states, token = _start_exchanges(name, [(jobs, arrays)])
    return states[0], token


def _finish_exchange(name, state, after):
    names, jobs, send_sem, recv_sem, held = state
    n = len(names)

    def body(*refs):
        hbm = dict(zip(names, refs[:n]))
        send, recv = refs[n:n + 2]
        starts, waits = _plan_all(jobs, hbm, send, recv)
        for cp in waits:
            cp.wait_recv()
        for cp in starts:
            cp.wait_send()

    res = pl.pallas_call(
        body, name=name, out_shape=tuple(pltpu.HBM(a.shape, a.dtype) for a in held),
        in_specs=[_HBM] * n + [_SEM, _SEM, _ANY], out_specs=tuple([_HBM] * n),
        input_output_aliases={i: i for i in range(n)},
        compiler_params=pltpu.CompilerParams(has_side_effects=_EFFECT),
    )(*held, send_sem, recv_sem, after)
    return dict(zip(names, res))


def _row_tile(rows, bytes_per_row):
    best = 16
    for t in range(16, rows + 1, 16):
        if rows % t == 0 and t * bytes_per_row <= 9 * 1024 * 1024:
            best = t
    return best


def _rowwise(fn, ins, out_dtypes, name, after=None):
    rows, cols = ins[0].shape
    per_row = sum(cols * a.dtype.itemsize for a in ins) + sum(cols * jnp.dtype(d).itemsize for d in out_dtypes)
    tr = _row_tile(rows, per_row)
    n_in = len(ins)

    def body(*refs):
        outs = fn(*[r[...] for r in refs[:n_in]])
        for o_ref, o in zip(refs[-len(out_dtypes):], outs):
            o_ref[...] = o.astype(o_ref.dtype)

    tile = pl.BlockSpec((tr, cols), lambda i: (i, 0))
    behind = [] if after is None else [after]
    return pl.pallas_call(
        body, name=name, grid=(rows // tr,),
        in_specs=[tile] * n_in + [pl.BlockSpec((8, 128), lambda i: (0, 0))] * len(behind),
        out_specs=[tile] * len(out_dtypes),
        out_shape=[jax.ShapeDtypeStruct((rows, cols), d) for d in out_dtypes],
        compiler_params=_params("parallel"),
    )(*ins, *behind)


def _tiled(fn, name, grid, pos, ins, outs):
    n_in = len(ins)

    def body(pos_ref, *refs):
        res = fn(*[r[...] for r in refs[:n_in]])
        for o_ref, o in zip(refs[n_in:], res):
            o_ref[...] = o.astype(o_ref.dtype)

    return pl.pallas_call(
        body, name=name,
        grid_spec=pltpu.PrefetchScalarGridSpec(
            num_scalar_prefetch=1, grid=grid,
            in_specs=[pl.BlockSpec(bs, im) for _, bs, im in ins],
            out_specs=[pl.BlockSpec(bs, im) for _, _, bs, im in outs]),
        out_shape=[jax.ShapeDtypeStruct(s, d) for s, d, _, _ in outs],
        compiler_params=_params("parallel"),
    )(pos, *[a for a, _, _ in ins])


def _adamw(w, g, m, v):
    m = ADAM_B1 * m + (1.0 - ADAM_B1) * g
    v = ADAM_B2 * v + (1.0 - ADAM_B2) * (g * g)
    m_hat = m / (1.0 - ADAM_B1 ** ADAM_STEP)
    v_hat = v / (1.0 - ADAM_B2 ** ADAM_STEP)
    return -ADAM_LR * (m_hat / (jnp.sqrt(v_hat) + ADAM_EPS) + ADAM_WD * w), m, v


def _adamw_small(pos, own, slots, params):
    n = len(params)

    def body(pos_ref, own_ref, slots_ref, *refs):
        ins, outs, total_ref = refs[:3 * n], refs[3 * n:-1], refs[-1]
        chip = pos_ref[0]
        idx = 2 * chip + pos_ref[1]
        term = lambda q: jnp.where(idx == q, own_ref[...], slots_ref[q])
        acc = term(0)
        for q in range(1, N_DEV):
            acc = acc + term(q)
        total_ref[...] = acc
        outs[0][...] = total_ref[0:1, :]
        for k, (w, _, _, row) in enumerate(params):
            width = min(w.shape[1], 128)
            for t in range(w.shape[0]):
                for j in range(w.shape[1] // width):
                    at = (slice(t, t + 1), slice(j * width, (j + 1) * width))
                    g = total_ref[pl.ds(row(t, j, chip), 1), :][:, :width]
                    new = _adamw(ins[3 * k][at], g, ins[3 * k + 1][at], ins[3 * k + 2][at])
                    for o_ref, o in zip(outs[1 + 4 * k:5 + 4 * k], (g, *new)):
                        o_ref[at] = o

    vmem = pl.BlockSpec(memory_space=pltpu.VMEM)
    return pl.pallas_call(
        body, name="adamw_small",
        in_specs=[pl.BlockSpec(memory_space=pltpu.SMEM)] + [vmem] * (2 + 3 * n),
        out_shape=[jax.ShapeDtypeStruct((1, 128), F32)]
        + [jax.ShapeDtypeStruct(p[0].shape, F32) for p in params for _ in range(4)],
        scratch_shapes=[pltpu.VMEM(own.shape, F32)],
    )(pos, own, slots, *[a for p in params for a in p[:3]])


class _Layout:
    def __init__(self, rows, cols, stacked):
        self.rows, self.cols, self.stacked = rows, cols, stacked

    def whole(self, rows=None):
        r = self.rows if rows is None else rows
        return (N_CHIPS, r, self.cols) if self.stacked else (r, N_CHIPS * self.cols)

    def part_rows(self, h, q=0, nq=1):
        n = self.rows // 2 // nq
        return pl.ds(pl.multiple_of(h * (self.rows // 2) + q * n, 16), n)

    def half_rows(self, h):
        return self.part_rows(h)

    def block(self, ref, p, rows=slice(None)):
        if self.stacked:
            return ref.at[p, rows, :]
        return ref.at[rows, pl.ds(pl.multiple_of(p * self.cols, 128), self.cols)]

    def all_chips(self, ref, rows):
        return ref.at[:, rows, :] if self.stacked else ref.at[rows, :]


BIG = (
    _Layout(IN_SHARD, D_MODEL, True),
    _Layout(ATTN_W + CONV_W, D_MODEL // N_CHIPS, False),
    _Layout(D_MODEL // N_CHIPS, D_MODEL, True),
    _Layout(D_MODEL, FF2 // N_CHIPS, False),
    _Layout(D_FF // N_CHIPS, D_MODEL, True),
)
N_BIG = len(BIG)
_ANY = pl.BlockSpec(memory_space=pl.ANY)


def _position():
    x, y, c = lax.axis_index("x"), lax.axis_index("y"), lax.axis_index("c")
    return x, y, c, 2 * x + y


def _core_of_chip(p, c):
    return (p >> 1, p & 1, c)


def _place_cast(shard, lay, pos, name, after=None):
    rows, cols = shard.shape
    tr = _row_tile(rows, cols * 6)
    if lay.stacked:
        out = (lay.whole(), BF16, (None, tr, cols), lambda i, pos: (pos[0], i, 0))
    else:
        out = (lay.whole(), BF16, (tr, cols), lambda i, pos: (i, pos[0]))
    ins = [(shard, (tr, cols), lambda i, pos: (i, 0))]
    if after is not None:
        ins.append((after, (8, 128), lambda i, pos: (0, 0)))
    return _tiled(lambda a, *_: (a,), name, (rows // tr,), pos, ins, [out])[0]


def _place_cast_pair(top, bottom, lay, pos, name, after=None):
    rows, cols = top.shape
    ins = [(top, (rows, cols), lambda i, pos: (0, 0)), (bottom, (rows, cols), lambda i, pos: (0, 0))]
    if after is not None:
        ins.append((after, (8, 128), lambda i, pos: (0, 0)))
    return _tiled(lambda a, b, *_: (jnp.concatenate([a, b], axis=0),), name, (1,), pos, ins,
                  [(lay.whole(), BF16, (2 * rows, cols), lambda i, pos: (0, pos[0]))])[0]


def _adamw_pair(top, bottom, g, after=None):
    rows = top[0].shape[0]

    def body(*refs):
        (wa, ma, va, wb, mb, vb, g_ref), outs = refs[:7], refs[-8:]
        for (w, m, v), gg, o in (((wa, ma, va), g_ref[:rows], outs[:4]), ((wb, mb, vb), g_ref[rows:], outs[4:])):
            for o_ref, val in zip(o, (gg, *_adamw(w[...], gg, m[...], v[...]))):
                o_ref[...] = val

    behind = [] if after is None else [after[0:8, 0:128]]
    res = pl.pallas_call(
        body, name="adamw_w_br", out_shape=[jax.ShapeDtypeStruct(top[0].shape, F32)] * 8,
    )(*top, *bottom, g, *behind)
    return res[:4], res[4:]


def _remote(src, dst, send, recv, k, device):
    return pltpu.make_async_remote_copy(src_ref=src, dst_ref=dst, send_sem=send.at[k], recv_sem=recv.at[k],
                                        device_id=device, device_id_type=MESH)


def _arrival(dst, send, recv, k, me):
    return _remote(dst, dst, send, recv, k, me)


def _gather_ici(lay, name, q=0, nq=1):
    def plan(hbm, pos, send, recv, base):
        x, y, c, me = pos
        rows = lay.part_rows(c, q, nq)
        mine = lay.block(hbm[name], me, rows)
        starts = [_remote(mine, mine, send, recv, base + d - 1, _core_of_chip(me ^ d, c)) for d in (1, 2, 3)]
        waits = [_arrival(lay.block(hbm[name], me ^ d, rows), send, recv, base + d - 1, (x, y, c)) for d in (1, 2, 3)]
        return starts, waits
    return _Job(3, plan)


def _gather_d2d(lay, name, q=0, nq=1):
    def plan(hbm, pos, send, recv, base):
        x, y, c, me = pos
        starts, waits = [], []
        for d in (1, 2, 3):
            got = lay.block(hbm[name], me ^ d, lay.part_rows(c, q, nq))
            starts.append(_remote(got, got, send, recv, base + d - 1, (x, y, 1 - c)))
            waits.append(_arrival(lay.block(hbm[name], me ^ d, lay.part_rows(1 - c, q, nq)), send, recv, base + d - 1,
                                  (x, y, c)))
        return starts, waits
    return _Job(3, plan)


def _rs_pair(lay, grad, theirs):
    def plan(hbm, pos, send, recv, base):
        x, y, c, _ = pos
        out = _remote(lay.all_chips(hbm[grad], lay.half_rows(1 - c)), hbm[theirs], send, recv, base, (x, y, 1 - c))
        return [out], [_arrival(hbm[theirs], send, recv, base, (x, y, c))]
    return _Job(1, plan)


def _rs_chips(lay, sums, slots):
    def plan(hbm, pos, send, recv, base):
        x, y, c, me = pos
        starts = [_remote(lay.block(hbm[sums], me ^ d), hbm[slots].at[me], send, recv, base + d - 1,
                          _core_of_chip(me ^ d, c)) for d in (1, 2, 3)]
        waits = [_arrival(hbm[slots].at[me ^ d], send, recv, base + d - 1, (x, y, c)) for d in (1, 2, 3)]
        return starts, waits
    return _Job(3, plan)


def _rs_share(lay, shard):
    def plan(hbm, pos, send, recv, base):
        x, y, c, _ = pos
        mine = hbm[shard].at[lay.half_rows(c), :]
        other = hbm[shard].at[lay.half_rows(1 - c), :]
        return [_remote(mine, mine, send, recv, base, (x, y, 1 - c))], [_arrival(other, send, recv, base, (x, y, c))]
    return _Job(1, plan)


def _slots_shape(lay):
    return jax.ShapeDtypeStruct((N_CHIPS, lay.rows // 2, lay.cols), BF16)


def _theirs_shape(lay, dtype=BF16):
    return jax.ShapeDtypeStruct(lay.whole(lay.rows // 2), dtype)


def _pair_sum(grad, theirs, lay, pos, name):
    half = lay.rows // 2
    add = lambda a, b: (a.astype(F32) + b.astype(F32),)
    if lay.stacked:
        tr = _row_tile(half, lay.cols * 6)
        nt = half // tr
        flat = lambda a: a.reshape(-1, lay.cols)
        mine = lambda t, pos: ((t // nt) * (2 * nt) + pos[1] * nt + t % nt, 0)
        grid, blk = (N_CHIPS * nt,), (tr, lay.cols)
        grad, theirs = flat(grad), flat(theirs)
    else:
        tr = _row_tile(half, N_CHIPS * lay.cols * 6)
        nt = half // tr
        mine = lambda t, pos: (pos[1] * nt + t, 0)
        grid, blk = (nt,), (tr, N_CHIPS * lay.cols)
    same = lambda t, pos: (t, 0)
    out = _tiled(add, name, grid, pos, [(grad, blk, mine), (theirs, blk, same)], [(theirs.shape, BF16, blk, same)])[0]
    return out.reshape(lay.whole(half))


def _chip_sum(sums, slots, lay, pos, name, after=None):
    half = lay.rows // 2
    tr = _row_tile(half, lay.cols * 12)
    nt = half // tr
    blk3 = (None, tr, lay.cols)
    if lay.stacked:
        own = (sums, blk3, lambda i, pos: (pos[0], i, 0))
    else:
        own = (sums, (tr, lay.cols), lambda i, pos: (i, pos[0]))
    others = [(slots, blk3, functools.partial(lambda d, i, pos: (pos[0] ^ d, i, 0), d)) for d in (1, 2, 3)]

    def add(a, b1, b2, b3, *_):
        return (((a.astype(F32) + b1.astype(F32)) + b2.astype(F32)) + b3.astype(F32),)

    if after is not None:
        others.append((after, (8, 128), lambda i, pos: (0, 0)))
    return _tiled(add, name, (nt,), pos, [own] + others,
                  [((lay.rows, lay.cols), F32, (tr, lay.cols), lambda i, pos: (pos[1] * nt + i, 0))])[0]


N_DEV = 8


def _to_all(src, slots):
    def plan(hbm, pos, send, recv, base):
        x, y, c, _ = pos
        idx = 4 * x + 2 * y + c
        starts = [_remote(hbm[src], hbm[slots].at[idx], send, recv, base + k - 1,
                          (x ^ (k >> 2), y ^ ((k >> 1) & 1), c ^ (k & 1))) for k in range(1, N_DEV)]
        waits = [_arrival(hbm[slots].at[idx ^ k], send, recv, base + k - 1, (x, y, c)) for k in range(1, N_DEV)]
        return starts, waits
    return _Job(N_DEV - 1, plan)


def _pack_rows(parts):
    padded = [jnp.pad(a, ((0, -a.shape[0] % 8), (0, 0))) for a in parts]
    starts = [sum(p.shape[0] for p in padded[:k]) for k in range(len(padded))]
    return jnp.concatenate(padded, axis=0), starts


def kernel(x, mix_norm, w_in, b_in, sinks, conv_w, w_attn_branch, w_conv_branch, w_out, ffn_norm, w_up, ffn_conv_w, w_down, final_norm, loss_target, m_mix_norm, m_w_in, m_b_in, m_sinks, m_conv_w, m_w_attn_branch, m_w_conv_branch, m_w_out, m_ffn_norm, m_w_up, m_ffn_conv_w, m_w_down, m_final_norm, v_mix_norm, v_w_in, v_b_in, v_sinks, v_conv_w, v_w_attn_branch, v_w_conv_branch, v_w_out, v_ffn_norm, v_w_up, v_ffn_conv_w, v_w_down, v_final_norm):
    me = 2 * lax.axis_index("x") + lax.axis_index("y")
    names = ("w_in", "w_br", "w_out", "w_up", "w_down")
    w_of = dict(w_in=w_in[0].T, w_out=w_out[0], w_up=w_up[0], w_down=w_down[0])
    m_of = dict(w_in=m_w_in[0].T, w_out=m_w_out[0], w_up=m_w_up[0], w_down=m_w_down[0])
    v_of = dict(w_in=v_w_in[0].T, w_out=v_w_out[0], w_up=v_w_up[0], w_down=v_w_down[0])
    ab = (w_attn_branch[0], m_w_attn_branch[0], v_w_attn_branch[0])
    cb = (w_conv_branch[0], m_w_conv_branch[0], v_w_conv_branch[0])

    pos = jnp.stack([me, lax.axis_index("c")]).astype(jnp.int32)

    lay = dict(zip(names, BIG))
    xs, target, sk = x[0], loss_target[0], sinks[0]
    s = xs.shape[0]
    tm, tm2, bk, bk2 = min(256, s), min(512, s), min(1024, s), min(2048, s)

    taps, (_, t0) = _pack_rows([conv_w[0], ffn_conv_w[0].reshape(3 * (FF2 // N_CHIPS // 128), 128)])
    placed = {"w_in": _place_cast(w_of["w_in"], lay["w_in"], pos, "cast_w_in")}
    fly_in, started = _start_exchange("gather_in_start", [_gather_ici(lay["w_in"], "w_in")], {"w_in": placed["w_in"]})
    taps_flight, started = _start_exchange("taps_start", [_to_all("v", "slots")],
                                           {"v": taps + started[0:1], "slots": jnp.zeros((N_DEV, *taps.shape), F32)})
    placed["w_br"] = _place_cast_pair(ab[0], cb[0], lay["w_br"], pos, "cast_w_br", after=started)
    for n in names[2:]:
        placed[n] = _place_cast(w_of[n], lay[n], pos, "cast_" + n, after=started)
    trio = ("w_br", "w_out")
    (fly_trio, fly_up, fly_down), started = _start_exchanges("gather_rest_start", [
        ([_gather_ici(lay[n], n) for n in ws], {n: placed[n] for n in ws}) for ws in (trio, ("w_up",), ("w_down",))])

    got = _finish_exchange("gather_in_wait", fly_in, after=started)
    w_in_full = _exchange("gather_in_d2d", [[_gather_d2d(lay["w_in"], "w_in")]], bufs=got)["w_in"].reshape(IN_W, D_MODEL)
    xn, qkv, c3, gates = _inproj_fwd(xs, mix_norm, w_in_full, b_in, tm2)
    k2 = _Carry([_gather_d2d(lay[n], n) for n in trio], bufs=_finish_exchange("gather_trio_wait", fly_trio, after=qkv))
    attn = _attn_fwd(qkv, sk, comm=k2)
    w_br = k2.out["w_br"]
    w_out_full = k2.out["w_out"].reshape(D_MODEL, D_MODEL)
    k3 = _Carry([_gather_d2d(lay["w_up"], "w_up")], bufs=_finish_exchange("gather_up_wait", fly_up, after=attn))
    taps = _finish_exchange("taps_wait", taps_flight, after=attn)
    taps = lax.dynamic_update_slice(taps["slots"], taps["v"][None], (2 * me + lax.axis_index("c"), 0, 0))
    conv_full = taps[0::2, 0:3].transpose(1, 0, 2).reshape(3, CONV_W)
    ffn_cw_full = taps[0::2, t0:t0 + 33].reshape(N_CHIPS, 3, FF2 // N_CHIPS).transpose(1, 0, 2).reshape(3, FF2)
    conv, a, cv, merged, h1, hn = _mix_fwd(xs, attn, c3, gates, conv_full, w_br, w_out_full, ffn_norm, tm2, comm=k3)
    w_up_full = k3.out["w_up"]
    w_down_full = _exchange("gather_down_d2d", [[_gather_d2d(lay["w_down"], "w_down")]],
                            bufs=_finish_exchange("gather_down_wait", fly_down, after=hn))["w_down"].reshape(D_FF, D_MODEL)
    u, up, act, dh2, loss_part, g_fn = _ffn_fwd_loss(hn, h1, w_up_full, ffn_cw_full, w_down_full,
                                                     final_norm[None, :], target, tm)

    grads, sums, slots = {}, {}, {}

    def pair(*ws):
        return _Carry([_rs_pair(lay[n], "g_" + n, "t_" + n) for n in ws], reads={"g_" + n: grads[n] for n in ws},
                      fresh={"t_" + n: _theirs_shape(lay[n], grads[n].dtype) for n in ws})

    def chips(*ws, also=None):
        k = _Carry([_rs_chips(lay[n], "s_" + n, "r_" + n) for n in ws], reads={"s_" + n: sums[n] for n in ws},
                   fresh={"r_" + n: _slots_shape(lay[n]) for n in ws})
        if also is not None:
            k = _Carry(k.jobs + also.jobs, {**k.reads, **also.reads}, None, {**k.fresh, **also.fresh})
        return k

    def pair_sums(k, *ws):
        for n in ws:
            sums[n] = _pair_sum(grads[n], k.out["t_" + n], lay[n], pos, "pair_sum_" + n)

    def take_slots(k, *ws):
        for n in ws:
            slots[n] = k.out["r_" + n]

    du, dh1, g_fcw, g_g2 = _ffn_bwd(dh2, u, up, h1, w_up_full, ffn_cw_full, w_down_full, ffn_norm, tm)
    grads["w_down"] = _wgrad(act, dh2, D_FF // 2, D_MODEL, bk2, "wgrad_down").reshape(lay["w_down"].whole())
    k4 = pair("w_down")
    grads["w_up"] = _wgrad(hn, du, D_MODEL, FF2 // 4, bk2, "wgrad_up", comm=k4)
    pair_sums(k4, "w_down")
    k5 = chips("w_down", also=pair("w_up"))
    dattn, dc3, dgt, g_cw, grads["w_br"], gw_out = _mix_bwd(
        dh1, gates, a, cv, c3, attn, conv, merged, conv_full, w_br, w_out_full, tm2, comm=k5)
    grads["w_out"] = gw_out.reshape(lay["w_out"].whole())
    take_slots(k5, "w_down")
    pair_sums(k5, "w_up")
    up_flight, started = _start_exchange("rs_chips_up_start", [_rs_chips(lay["w_up"], "s", "r")],
                                         {"s": sums["w_up"], "r": _slots_shape(lay["w_up"])})
    k6 = pair(*trio)
    k6.reads["after"] = started
    dq, dk_even, dk_odd, dv_even, dv_odd, g_sk = _attn_bwd(qkv, sk, attn, dattn, comm=k6)
    pair_sums(k6, *trio)
    trio_flight, started = _start_exchange(
        "rs_chips_trio_start", [_rs_chips(lay[n], "s_" + n, "r_" + n) for n in trio],
        {**{"s_" + n: sums[n] for n in trio}, **{"r_" + n: _slots_shape(lay[n]) for n in trio}})
    behind = mix_norm + jnp.tile(started[0:1], (1, D_MODEL // 128))
    grad_x, gw_in, g_b, g_g1 = _inproj_bwd(dq, (dk_even, dk_odd), (dv_even, dv_odd), dc3, dgt, w_in_full, xs, xn,
                                           dh1, behind)
    grads["w_in"] = gw_in.reshape(lay["w_in"].whole())

    parts = [loss_part, g_g1, g_b, jnp.pad(g_sk[:, 0], (0, 120))[None, :], g_cw, g_g2, g_fcw, g_fn]
    packed, at = _pack_rows([p.reshape(-1, 128) for p in parts])
    small_flight, started = _start_exchange("small_start", [_to_all("v", "slots")],
                                            {"v": packed, "slots": jnp.zeros((N_DEV, *packed.shape), F32)})
    in_flight, started = _start_exchange("rs_pair_in_start", [_rs_pair(lay["w_in"], "g", "t")],
                                         {"g": grads["w_in"], "t": _theirs_shape(lay["w_in"]), "behind": started})
    landed = _finish_exchange("rs_chips_up_wait", up_flight, after=started)
    halves = {"w_up": _chip_sum(landed["s"], landed["r"], lay["w_up"], pos, "chip_sum_w_up"),
              "w_down": _chip_sum(sums["w_down"], slots["w_down"], lay["w_down"], pos, "chip_sum_w_down", after=started)}
    landed = _finish_exchange("rs_pair_in_wait", in_flight, after=halves["w_down"])
    sums["w_in"] = _pair_sum(landed["g"], landed["t"], lay["w_in"], pos, "pair_sum_w_in")
    in_flight, started = _start_exchange("rs_chips_in_start", [_rs_chips(lay["w_in"], "s", "r")],
                                         {"s": sums["w_in"], "r": _slots_shape(lay["w_in"])})
    landed = _finish_exchange("rs_chips_trio_wait", trio_flight, after=started)
    for n in trio:
        halves[n] = _chip_sum(landed["s_" + n], landed["r_" + n], lay[n], pos, "chip_sum_" + n)
    shared = _exchange("share_halves", [[_rs_share(lay[n], n) for n in names[1:]]], bufs=halves)

    def adam(n, g, after=None):
        return _rowwise(lambda w, g, m, v: (g, *_adamw(w, g, m, v)), [w_of[n], g, m_of[n], v_of[n]], [F32] * 4,
                        "adamw_" + n, after=after)

    new_of, last = {}, None
    for n in ("w_up", "w_down", "w_out"):
        new_of[n] = adam(n, shared[n], last)
        last = new_of[n][1]
    new_of["w_ab"], new_of["w_cb"] = _adamw_pair(ab, cb, shared["w_br"], after=last)
    last = new_of["w_cb"][1]

    arrived = _finish_exchange("small_wait", small_flight, after=last)
    flat = lambda k: lambda t, j, chip: at[k] + j
    mine = lambda k, per_tap: lambda t, j, chip: at[k] + per_tap * t + (per_tap // N_CHIPS) * chip + j
    small_p = [
        (mix_norm, m_mix_norm, v_mix_norm, flat(1)), (b_in, m_b_in, v_b_in, flat(2)), (sinks, m_sinks, v_sinks, flat(3)),
        (conv_w[0], m_conv_w[0], v_conv_w[0], mine(4, CONV_W // 128)), (ffn_norm, m_ffn_norm, v_ffn_norm, flat(5)),
        (ffn_conv_w[0], m_ffn_conv_w[0], v_ffn_conv_w[0], mine(6, FF2 // 128)),
        (final_norm[None, :], m_final_norm[None, :], v_final_norm[None, :], flat(7))]
    small_new = _adamw_small(pos, arrived["v"], arrived["slots"], small_p)
    loss = small_new[0][0, 0]
    small_g = small_new[1::4]
    small_new = [small_new[4 * k + 2:4 * k + 5] for k in range(len(small_p))]

    landed = _finish_exchange("rs_chips_in_wait", in_flight, after=small_new[0][0])
    half_in = _chip_sum(landed["s"], landed["r"], lay["w_in"], pos, "chip_sum_w_in")
    shared["w_in"] = _exchange("share_in", [[_rs_share(lay["w_in"], "w_in")]], bufs={"w_in": half_in})["w_in"]
    new_of["w_in"] = [a.T for a in adam("w_in", shared["w_in"])]
    big = ("w_in", "w_ab", "w_cb", "w_out", "w_up", "w_down")
    big_g = [new_of[n][0] for n in big]
    big_new = [new_of[n][1:] for n in big]

    order = [("s", 0), ("b", 0), ("s", 1), ("s", 2), ("s", 3), ("b", 1), ("b", 2), ("b", 3), ("s", 4), ("b", 4),
             ("s", 5), ("b", 5), ("s", 6)]
    shapes = [mix_norm.shape, w_in.shape, b_in.shape, sinks.shape, conv_w.shape, w_attn_branch.shape,
              w_conv_branch.shape, w_out.shape, ffn_norm.shape, w_up.shape, ffn_conv_w.shape, w_down.shape,
              final_norm.shape]
    out_g = [(small_g[k] if kind == "s" else big_g[k]).reshape(shp) for (kind, k), shp in zip(order, shapes)]
    news = [[(small_new[k][j] if kind == "s" else big_new[k][j]).reshape(shp) for (kind, k), shp in zip(order, shapes)]
            for j in range(3)]
    return (loss, grad_x[None], *out_g, *news[0], *news[1], *news[2])
```

```python
import functools

import jax
import jax.numpy as jnp
from jax import lax
from jax.experimental import pallas as pl
from jax.experimental.pallas import tpu as pltpu

F32 = jnp.float32
BF16 = jnp.bfloat16

D_MODEL = 1024
HEAD_DIM = 64
N_HEADS = 8
N_KV_HEADS = 2
GROUP = N_HEADS // N_KV_HEADS
BLOCK = 128
ATTN_SCALE = HEAD_DIM ** -0.5
ATTN_W = N_HEADS * HEAD_DIM
KV_W = N_KV_HEADS * HEAD_DIM
CONV_W = 512
QKV_W = ATTN_W + 2 * KV_W
C3_W = 3 * CONV_W
GATES_W = 2 * D_MODEL
IN_W = QKV_W + C3_W + GATES_W
D_FF = 2816
FF2 = 2 * D_FF
NORM_EPS = 1e-5
N_CHIPS = 4
IN_SHARD = IN_W // N_CHIPS
NEG = -1e30

ADAM_LR = 0.001
ADAM_B1 = 0.9
ADAM_B2 = 0.999
ADAM_EPS = 1e-08
ADAM_WD = 0.01
ADAM_STEP = 10

VMEM_LIMIT = 56 * 1024 * 1024
MESH = pl.DeviceIdType.MESH

NT = (((1,), (1,)), ((), ()))
TN = (((0,), (0,)), ((), ()))


def _params(*sem):
    return pltpu.CompilerParams(dimension_semantics=sem, vmem_limit_bytes=VMEM_LIMIT)


def _resident(shape):
    return pl.BlockSpec(shape, lambda *_: (0,) * len(shape), pipeline_mode=pl.Buffered(1))


def _sigmoid(v):
    return 0.5 * jnp.tanh(0.5 * v) + 0.5


def _rstd(v):
    return lax.rsqrt(jnp.mean(v * v, axis=-1, keepdims=True) + NORM_EPS)


def _rms_bwd(dy, v, rstd, g):
    vhat = v * rstd
    t = dy * g
    return rstd * (t - vhat * jnp.mean(t * vhat, axis=-1, keepdims=True)), dy * vhat


def _taps(z, cw):
    return cw[2:3] * z + cw[1:2] * pltpu.roll(z, 1, 0) + cw[0:1] * pltpu.roll(z, 2, 0)


def _causal_conv(z, prev, cw):
    edge = _taps(jnp.concatenate([prev, z[0:8]], axis=0), cw)
    return jnp.concatenate([edge[8:16], _taps(z, cw)[8:]], axis=0)


def _rows_after(z, nxt):
    n = z.shape[0]
    edge = jnp.concatenate([z[n - 8:n], nxt], axis=0)
    return tuple(jnp.concatenate([pltpu.roll(z, n - k, 0)[:n - 8], pltpu.roll(edge, 16 - k, 0)[0:8]], axis=0)
                 for k in (1, 2))


def _inproj_fwd(x, g1, w_in, b_in, tm, comm=None):
    s = x.shape[0]

    def body(x_ref, g_ref, w_ref, b_ref, xn_ref, qkv_ref, c3_ref, gt_ref):
        xf = x_ref[...]
        xn = (xf * _rstd(xf) * g_ref[...]).astype(BF16)
        xn_ref[...] = xn

        proj = (lax.dot_general(xn, w_ref[...], NT, preferred_element_type=F32) + b_ref[...]).astype(BF16)
        qkv_ref[...] = proj[:, :QKV_W]
        c3_ref[...] = proj[:, QKV_W:QKV_W + C3_W]
        gt_ref[...] = proj[:, QKV_W + C3_W:]

    row = lambda w: pl.BlockSpec((tm, w), lambda i: (i, 0))
    return _call(
        comm, body, name="inproj_fwd", grid=(s // tm,),
        in_specs=[row(D_MODEL), _resident((1, D_MODEL)), _resident((IN_W, D_MODEL)), _resident((1, IN_W))],
        out_specs=[row(D_MODEL), row(QKV_W), row(C3_W), row(GATES_W)],
        out_shape=[jax.ShapeDtypeStruct((s, D_MODEL), BF16), jax.ShapeDtypeStruct((s, QKV_W), BF16),
                   jax.ShapeDtypeStruct((s, C3_W), BF16), jax.ShapeDtypeStruct((s, GATES_W), BF16)],
        compiler_params=_params("parallel"),
    )(x, g1, w_in, b_in)


def _attn_bias():
    kj = jnp.arange(2 * BLOCK)[:, None]
    qi = (jnp.arange(GROUP * BLOCK) % BLOCK)[None, :]
    band = (kj > qi) & (kj <= qi + BLOCK)
    return jnp.stack([jnp.where(band & (kj >= BLOCK), 0.0, NEG), jnp.where(band, 0.0, NEG)]).astype(F32)


def _attn_bias_specs():
    shape = (None, 2 * BLOCK, GROUP * BLOCK)
    return pl.BlockSpec(shape, lambda i: (jnp.minimum(i, 1), 0, 0)), pl.BlockSpec(shape, lambda i: (1, 0, 0))


def _sink_row(sk_ref, h):
    lane = lax.broadcasted_iota(jnp.int32, (1, GROUP * BLOCK), 1)
    row = jnp.full((1, GROUP * BLOCK), sk_ref[h * GROUP], F32)
    for g in range(1, GROUP):
        row = jnp.where(lane >= g * BLOCK, sk_ref[h * GROUP + g], row)
    return row


def _stack_heads(t, h):
    return jnp.concatenate(
        [t[:, (h * GROUP + g) * HEAD_DIM:(h * GROUP + g + 1) * HEAD_DIM] for g in range(GROUP)], axis=0)


def _unstack_heads(per_kv):
    return jnp.concatenate(
        [t[g * BLOCK:(g + 1) * BLOCK] for t in per_kv for g in range(GROUP)], axis=1)


def _block_specs(n, steps):
    cur = lambda i: jnp.minimum(i, steps - 1)
    prev = lambda i: jnp.maximum(n * jnp.minimum(i, steps - 1) - 1, 0)
    kv = ATTN_W // KV_W
    return (pl.BlockSpec((n * BLOCK, ATTN_W), lambda i: (cur(i), 0)),
            pl.BlockSpec((BLOCK, KV_W), lambda i: (prev(i), kv)), pl.BlockSpec((n * BLOCK, KV_W), lambda i: (cur(i), kv)),
            pl.BlockSpec((BLOCK, KV_W), lambda i: (prev(i), kv + 1)),
            pl.BlockSpec((n * BLOCK, KV_W), lambda i: (cur(i), kv + 1)))


def _attn_fwd(qkv, sinks, comm=None):
    s = qkv.shape[0]
    n = min(4, s // BLOCK)
    steps = s // (n * BLOCK)

    def body(sk_ref, bias0_ref, bias1_ref, q_ref, kp_ref, kc_ref, vp_ref, vc_ref, o_ref):
        kc, vc = kc_ref[...], vc_ref[...]
        for b in range(n):
            rows, before = slice(b * BLOCK, (b + 1) * BLOCK), slice((b - 1) * BLOCK, b * BLOCK)
            kp, vp = (kp_ref[...], vp_ref[...]) if b == 0 else (kc[before], vc[before])
            q, bias = q_ref[rows, :], (bias0_ref if b == 0 else bias1_ref)[...]
            outs = []
            for h in range(N_KV_HEADS):
                hs = slice(h * HEAD_DIM, (h + 1) * HEAD_DIM)
                k2 = jnp.concatenate([kp[:, hs], kc[rows, hs]], axis=0)
                v2 = jnp.concatenate([vp[:, hs], vc[rows, hs]], axis=0)
                sc = lax.dot_general(k2, _stack_heads(q, h), NT, preferred_element_type=F32) * ATTN_SCALE + bias
                sink = _sink_row(sk_ref, h)
                m = jnp.maximum(jnp.max(sc, axis=0, keepdims=True), sink)
                p = jnp.exp(sc - m)
                den = jnp.sum(p, axis=0, keepdims=True) + jnp.exp(sink - m)
                out = lax.dot_general(v2, p.astype(BF16), TN, preferred_element_type=F32) / den
                outs.append(out.T)
            o_ref[rows, :] = _unstack_heads(outs).astype(BF16)

    return _call(
        comm, body, name="attn_fwd", grid=(steps,),
        in_specs=[pl.BlockSpec(memory_space=pltpu.SMEM), *_attn_bias_specs(), *_block_specs(n, steps)],
        out_specs=pl.BlockSpec((n * BLOCK, ATTN_W), lambda i: (i, 0)),
        out_shape=jax.ShapeDtypeStruct((s, ATTN_W), BF16),
        compiler_params=_params("parallel"),
    )(sinks, _attn_bias(), _attn_bias(), qkv, qkv, qkv, qkv, qkv)


def _mix_fwd(x, attn, c3, gates, conv_w, w_br, w_out, g2, tm, comm=None):
    s = x.shape[0]

    def body(x_ref, at_ref, c3_ref, gt_ref, cw_ref, wbr_ref, wo_ref, g_ref,
             conv_ref, a_ref, cv_ref, mg_ref, h1_ref, hn_ref, carry_ref):
        @pl.when(pl.program_id(0) == 0)
        def _():
            carry_ref[...] = jnp.zeros_like(carry_ref)

        c3v = c3_ref[...].astype(F32)
        cb, cc, cx = c3v[:, :CONV_W], c3v[:, CONV_W:2 * CONV_W], c3v[:, 2 * CONV_W:]
        z = cc * cx
        cz = _causal_conv(z, carry_ref[...], cw_ref[...])
        carry_ref[...] = z[tm - 8:tm]
        conv = (cb * cz).astype(BF16)
        conv_ref[...] = conv
        a = jnp.dot(at_ref[...], wbr_ref[:ATTN_W, :], preferred_element_type=F32)
        cv = jnp.dot(conv, wbr_ref[ATTN_W:, :], preferred_element_type=F32)
        a_ref[...] = a.astype(BF16)
        cv_ref[...] = cv.astype(BF16)
        gt = gt_ref[...].astype(F32)
        merged = (_sigmoid(gt[:, :D_MODEL]) * a + _sigmoid(gt[:, D_MODEL:]) * cv).astype(BF16)
        mg_ref[...] = merged
        h1 = x_ref[...] + jnp.dot(merged, wo_ref[...], preferred_element_type=F32)
        h1_ref[...] = h1
        hn_ref[...] = (h1 * _rstd(h1) * g_ref[...]).astype(BF16)

    row = lambda w: pl.BlockSpec((tm, w), lambda i: (i, 0))
    return _call(
        comm, body, name="mix_fwd", grid=(s // tm,),
        in_specs=[row(D_MODEL), row(ATTN_W), row(C3_W), row(GATES_W), _resident((3, CONV_W)),
                  _resident((ATTN_W + CONV_W, D_MODEL)), _resident((D_MODEL, D_MODEL)), _resident((1, D_MODEL))],
        out_specs=[row(CONV_W), row(D_MODEL), row(D_MODEL), row(D_MODEL), row(D_MODEL), row(D_MODEL)],
        out_shape=[jax.ShapeDtypeStruct((s, CONV_W), BF16), jax.ShapeDtypeStruct((s, D_MODEL), BF16),
                   jax.ShapeDtypeStruct((s, D_MODEL), BF16), jax.ShapeDtypeStruct((s, D_MODEL), BF16),
                   jax.ShapeDtypeStruct((s, D_MODEL), F32), jax.ShapeDtypeStruct((s, D_MODEL), BF16)],
        scratch_shapes=[pltpu.VMEM((8, CONV_W), F32)],
        compiler_params=_params("arbitrary"),
    )(x, attn, c3, gates, conv_w, w_br, w_out, g2)


def _ffn_fwd_loss(hn, h1, w_up, ffn_cw, w_down, g3, target, tm):
    s = hn.shape[0]

    def body(hn_ref, h1_ref, wu_ref, cw_ref, wd_ref, g_ref, t_ref,
             u_ref, up_ref, act_ref, dh2_ref, loss_ref, gfn_ref, carry_ref):
        @pl.when(pl.program_id(0) == 0)
        def _():
            carry_ref[...] = jnp.zeros_like(carry_ref)
            loss_ref[...] = jnp.zeros_like(loss_ref)
            gfn_ref[...] = jnp.zeros_like(gfn_ref)

        u = jnp.dot(hn_ref[...], wu_ref[...], preferred_element_type=F32)
        u_ref[...] = u.astype(BF16)
        up = _causal_conv(u, carry_ref[...], cw_ref[...])
        up_ref[...] = up
        carry_ref[...] = u[tm - 8:tm]
        gate, val = up[:, :D_FF], up[:, D_FF:]
        act = (gate * _sigmoid(gate) * val).astype(BF16)
        act_ref[...] = act
        h2 = h1_ref[...] + jnp.dot(act, wd_ref[...], preferred_element_type=F32)
        rstd = _rstd(h2)
        g = g_ref[...]
        err = h2 * rstd * g - t_ref[...]
        loss_ref[...] += jnp.sum(err * err) * (0.5 / D_MODEL)
        dh2, dg = _rms_bwd(err * (1.0 / D_MODEL), h2, rstd, g)
        dh2_ref[...] = dh2
        gfn_ref[...] += jnp.sum(dg, axis=0, keepdims=True)

    row = lambda w: pl.BlockSpec((tm, w), lambda i: (i, 0))
    acc = lambda w: pl.BlockSpec((1, w), lambda i: (0, 0))
    return pl.pallas_call(
        body, name="ffn_fwd_loss", grid=(s // tm,),
        in_specs=[row(D_MODEL), row(D_MODEL), _resident((D_MODEL, FF2)), _resident((3, FF2)),
                  _resident((D_FF, D_MODEL)), _resident((1, D_MODEL)), row(D_MODEL)],
        out_specs=[row(FF2), row(FF2), row(D_FF), row(D_MODEL), acc(128), acc(D_MODEL)],
        out_shape=[jax.ShapeDtypeStruct((s, FF2), BF16), jax.ShapeDtypeStruct((s, FF2), F32),
                   jax.ShapeDtypeStruct((s, D_FF), BF16),
                   jax.ShapeDtypeStruct((s, D_MODEL), F32), jax.ShapeDtypeStruct((1, 128), F32),
                   jax.ShapeDtypeStruct((1, D_MODEL), F32)],
        scratch_shapes=[pltpu.VMEM((8, FF2), F32)],
        compiler_params=_params("arbitrary"),
    )(hn, h1, w_up, ffn_cw, w_down, g3, target)


def _ffn_bwd(dh2, u, up, h1, w_up, ffn_cw, w_down, g2, tm):
    s = dh2.shape[0]
    nt = s // tm

    def body(dh2_ref, u_ref, up_ref, h1_ref, wu_ref, cw_ref, wd_ref, g_ref,
             du_ref, dh1_ref, gcw_ref, gg_ref, carry_ref):
        @pl.when(pl.program_id(0) == 0)
        def _():
            for ref in (carry_ref, gcw_ref, gg_ref):
                ref[...] = jnp.zeros_like(ref)

        dh2v = dh2_ref[...]
        dact = lax.dot_general(dh2v.astype(BF16), wd_ref[...], NT, preferred_element_type=F32)
        upv = up_ref[...]
        gate, val = upv[:, :D_FF], upv[:, D_FF:]
        sg = _sigmoid(gate)
        dval = dact * (gate * sg)
        dgate = dact * val * (sg * (1.0 + gate * (1.0 - sg)))
        dup = jnp.concatenate([dgate, dval], axis=1)
        dup1, dup2 = _rows_after(dup, carry_ref[...])
        carry_ref[...] = dup[0:8]
        u = u_ref[...].astype(F32)
        gcw_ref[2:3, :] += jnp.sum(dup * u, axis=0, keepdims=True)
        gcw_ref[1:2, :] += jnp.sum(dup1 * u, axis=0, keepdims=True)
        gcw_ref[0:1, :] += jnp.sum(dup2 * u, axis=0, keepdims=True)
        cw = cw_ref[...]
        du = (cw[2:3] * dup + cw[1:2] * dup1 + cw[0:1] * dup2).astype(BF16)
        du_ref[...] = du
        dhn = lax.dot_general(du, wu_ref[...], NT, preferred_element_type=F32)
        h1v = h1_ref[...]
        dh1, dg = _rms_bwd(dhn, h1v, _rstd(h1v), g_ref[...])
        dh1_ref[...] = dh2v + dh1
        gg_ref[...] += jnp.sum(dg, axis=0, keepdims=True)

    row = lambda w: pl.BlockSpec((tm, w), lambda i: (nt - 1 - i, 0))
    return pl.pallas_call(
        body, name="ffn_bwd", grid=(nt,),
        in_specs=[row(D_MODEL), row(FF2), row(FF2),
                  row(D_MODEL), _resident((D_MODEL, FF2)), _resident((3, FF2)), _resident((D_FF, D_MODEL)),
                  _resident((1, D_MODEL))],
        out_specs=[row(FF2), row(D_MODEL), pl.BlockSpec((3, FF2), lambda i: (0, 0)),
                   pl.BlockSpec((1, D_MODEL), lambda i: (0, 0))],
        out_shape=[jax.ShapeDtypeStruct((s, FF2), BF16), jax.ShapeDtypeStruct((s, D_MODEL), F32),
                   jax.ShapeDtypeStruct((3, FF2), F32), jax.ShapeDtypeStruct((1, D_MODEL), F32)],
        scratch_shapes=[pltpu.VMEM((8, FF2), F32)],
        compiler_params=_params("arbitrary"),
    )(dh2, u, up, h1, w_up, ffn_cw, w_down, g2)


def _mix_bwd(dh1, gates, a, cv, c3, attn, conv, merged, conv_w, w_br, w_out, tm, comm=None):
    s = dh1.shape[0]
    nt = s // tm
    halo = 16

    def body(dh1_ref, gt_ref, a_ref, cv_ref, c3_ref, ch_ref, at_ref, cn_ref, mg_ref, cw_ref, wbr_ref,
             wo_ref, dat_ref, dc3_ref, dgt_ref, gcw_ref, gbr_ref, gout_ref, carry_ref, br_acc, out_acc):
        i = pl.program_id(0)

        @pl.when(i == 0)
        def _():
            for ref in (carry_ref, gcw_ref, br_acc, out_acc):
                ref[...] = jnp.zeros_like(ref)

        dh1v = dh1_ref[...].astype(BF16)
        out_acc[...] += lax.dot_general(mg_ref[...], dh1v, TN, preferred_element_type=F32)
        dm = lax.dot_general(dh1v, wo_ref[...], NT, preferred_element_type=F32)
        gt = gt_ref[...].astype(F32)
        sa, sc = _sigmoid(gt[:, :D_MODEL]), _sigmoid(gt[:, D_MODEL:])
        da = (dm * sa).astype(BF16)
        dcv = (dm * sc).astype(BF16)
        br_acc[:ATTN_W, :] += lax.dot_general(at_ref[...], da, TN, preferred_element_type=F32)
        br_acc[ATTN_W:, :] += lax.dot_general(cn_ref[...], dcv, TN, preferred_element_type=F32)
        dgt_ref[...] = jnp.concatenate(
            [dm * a_ref[...].astype(F32) * (sa * (1.0 - sa)), dm * cv_ref[...].astype(F32) * (sc * (1.0 - sc))],
            axis=1).astype(BF16)
        dat_ref[...] = lax.dot_general(da, wbr_ref[:ATTN_W, :], NT, preferred_element_type=F32).astype(BF16)
        dconv = lax.dot_general(dcv, wbr_ref[ATTN_W:, :], NT, preferred_element_type=F32)
        c3v = c3_ref[...].astype(F32)
        cb, cc, cx = c3v[:, :CONV_W], c3v[:, CONV_W:2 * CONV_W], c3v[:, 2 * CONV_W:]
        z = cc * cx
        chv = ch_ref[...].astype(F32)[halo - 8:halo] * (i < nt - 1).astype(F32)
        zh = chv[:, CONV_W:2 * CONV_W] * chv[:, 2 * CONV_W:]
        cw = cw_ref[...]
        cz = _causal_conv(z, zh, cw)
        dcz = dconv * cb
        dcz1, dcz2 = _rows_after(dcz, carry_ref[...])
        carry_ref[...] = dcz[0:8]
        gcw_ref[2:3, :] += jnp.sum(dcz * z, axis=0, keepdims=True)
        gcw_ref[1:2, :] += jnp.sum(dcz1 * z, axis=0, keepdims=True)
        gcw_ref[0:1, :] += jnp.sum(dcz2 * z, axis=0, keepdims=True)
        dz = cw[2:3] * dcz + cw[1:2] * dcz1 + cw[0:1] * dcz2
        dc3_ref[...] = jnp.concatenate([dconv * cz, dz * cx, dz * cc], axis=1).astype(BF16)

        @pl.when(i == nt - 1)
        def _():
            gbr_ref[...] = br_acc[...].astype(BF16)
            gout_ref[...] = out_acc[...].astype(BF16)

    row = lambda w: pl.BlockSpec((tm, w), lambda i: (nt - 1 - i, 0))
    return _call(
        comm, body, name="mix_bwd", grid=(nt,),
        in_specs=[row(D_MODEL), row(GATES_W), row(D_MODEL), row(D_MODEL), row(C3_W),
                  pl.BlockSpec((halo, C3_W), lambda i: (jnp.maximum((nt - 1 - i) * (tm // halo) - 1, 0), 0)),
                  row(ATTN_W), row(CONV_W), row(D_MODEL), _resident((3, CONV_W)),
                  _resident((ATTN_W + CONV_W, D_MODEL)), _resident((D_MODEL, D_MODEL))],
        out_specs=[row(ATTN_W), row(C3_W), row(GATES_W), pl.BlockSpec((3, CONV_W), lambda i: (0, 0)),
                   _resident((ATTN_W + CONV_W, D_MODEL)), _resident((D_MODEL, D_MODEL))],
        out_shape=[jax.ShapeDtypeStruct((s, ATTN_W), BF16), jax.ShapeDtypeStruct((s, C3_W), BF16),
                   jax.ShapeDtypeStruct((s, GATES_W), BF16), jax.ShapeDtypeStruct((3, CONV_W), F32),
                   jax.ShapeDtypeStruct((ATTN_W + CONV_W, D_MODEL), BF16),
                   jax.ShapeDtypeStruct((D_MODEL, D_MODEL), BF16)],
        scratch_shapes=[pltpu.VMEM((8, CONV_W), F32), pltpu.VMEM((ATTN_W + CONV_W, D_MODEL), F32),
                        pltpu.VMEM((D_MODEL, D_MODEL), F32)],
        compiler_params=_params("arbitrary"),
    )(dh1, gates, a, cv, c3, c3, attn, conv, merged, conv_w, w_br, w_out)


def _attn_bwd(qkv, sinks, o, do, comm=None):
    s = qkv.shape[0]
    npair = s // (2 * BLOCK)

    def one_block(sk_ref, bias, q, kp, kc, vp, vc, ov, dov, dsk_ref):
        dqs, dks, dvs = [], [], []
        for h in range(N_KV_HEADS):
            hs = slice(h * HEAD_DIM, (h + 1) * HEAD_DIM)
            k2 = jnp.concatenate([kp[:, hs], kc[:, hs]], axis=0)
            v2 = jnp.concatenate([vp[:, hs], vc[:, hs]], axis=0)
            qg, og, dog = _stack_heads(q, h), _stack_heads(ov, h), _stack_heads(dov, h)
            sc = lax.dot_general(k2, qg, NT, preferred_element_type=F32) * ATTN_SCALE + bias
            sink = _sink_row(sk_ref, h)
            m = jnp.maximum(jnp.max(sc, axis=0, keepdims=True), sink)
            p = jnp.exp(sc - m)
            psink = jnp.exp(sink - m)
            inv = 1.0 / (jnp.sum(p, axis=0, keepdims=True) + psink)
            p = p * inv
            delta = jnp.sum(dog.astype(F32) * og.astype(F32), axis=1, keepdims=True).T
            dp = lax.dot_general(v2, dog, NT, preferred_element_type=F32)
            ds = (p * (dp - delta)).astype(BF16)
            dqs.append((lax.dot_general(k2, ds, TN, preferred_element_type=F32) * ATTN_SCALE).T)
            dks.append(jnp.dot(ds, qg, preferred_element_type=F32) * ATTN_SCALE)
            dvs.append(jnp.dot(p.astype(BF16), dog, preferred_element_type=F32))
            dsink = -(psink * inv * delta)
            for g in range(GROUP):
                r = h * GROUP + g
                dsk_ref[r:r + 1, :] += jnp.sum(dsink[:, g * BLOCK:(g + 1) * BLOCK])
        return _unstack_heads(dqs), jnp.concatenate(dks, axis=1), jnp.concatenate(dvs, axis=1)

    def body(sk_ref, bias0_ref, bias1_ref, q_ref, kp_ref, kc_ref, vp_ref, vc_ref, o_ref, do_ref,
             dq_ref, dke_ref, dko_ref, dve_ref, dvo_ref, dsk_ref, ck_ref, cvv_ref):
        i = pl.program_id(0)

        @pl.when(i == 0)
        def _():
            for ref in (ck_ref, cvv_ref, dsk_ref):
                ref[...] = jnp.zeros_like(ref)

        @pl.when(i < npair)
        def _():
            kc, vc = kc_ref[...], vc_ref[...]
            first, second = slice(0, BLOCK), slice(BLOCK, 2 * BLOCK)
            dq0, dk0, dv0 = one_block(sk_ref, bias0_ref[...], q_ref[first, :], kp_ref[...], kc[first], vp_ref[...],
                                      vc[first], o_ref[first, :], do_ref[first, :], dsk_ref)
            dq1, dk1, dv1 = one_block(sk_ref, bias1_ref[...], q_ref[second, :], kc[first], kc[second], vc[first],
                                      vc[second], o_ref[second, :], do_ref[second, :], dsk_ref)
            dq_ref[first, :] = dq0.astype(BF16)
            dq_ref[second, :] = dq1.astype(BF16)
            dko_ref[...] = (ck_ref[...] + dk0[:BLOCK]).astype(BF16)
            dvo_ref[...] = (cvv_ref[...] + dv0[:BLOCK]).astype(BF16)
            dke_ref[...] = (dk0[BLOCK:] + dk1[:BLOCK]).astype(BF16)
            dve_ref[...] = (dv0[BLOCK:] + dv1[:BLOCK]).astype(BF16)
            ck_ref[...] = dk1[BLOCK:]
            cvv_ref[...] = dv1[BLOCK:]

        @pl.when(i == npair)
        def _():
            dko_ref[...] = ck_ref[...].astype(BF16)
            dvo_ref[...] = cvv_ref[...].astype(BF16)

    cur = lambda i: jnp.minimum(i, npair - 1)
    done = lambda i: jnp.maximum(i - 1, 0)
    rows = pl.BlockSpec((2 * BLOCK, ATTN_W), lambda i: (cur(i), 0))
    even = pl.BlockSpec((BLOCK, KV_W), lambda i: (cur(i), 0))
    odd = pl.BlockSpec((BLOCK, KV_W), lambda i: (done(i), 0))
    half = jax.ShapeDtypeStruct((s // 2, KV_W), BF16)
    return _call(
        comm, body, name="attn_bwd", grid=(npair + 1,),
        in_specs=[pl.BlockSpec(memory_space=pltpu.SMEM), *_attn_bias_specs(), *_block_specs(2, npair), rows, rows],
        out_specs=[rows, even, odd, even, odd, pl.BlockSpec((N_HEADS, 128), lambda i: (0, 0))],
        out_shape=[jax.ShapeDtypeStruct((s, ATTN_W), BF16), half, half, half, half,
                   jax.ShapeDtypeStruct((N_HEADS, 128), F32)],
        scratch_shapes=[pltpu.VMEM((BLOCK, KV_W), F32), pltpu.VMEM((BLOCK, KV_W), F32)],
        compiler_params=_params("arbitrary"),
    )(sinks, _attn_bias(), _attn_bias(), qkv, qkv, qkv, qkv, qkv, o, do)


def _inproj_bwd(dq, dk, dv, dc3, dgt, w_in, x, xn, dh1, g1):
    s = x.shape[0]
    tm = min(2 * BLOCK, s)
    nt = s // tm

    def body(dq_ref, dke_ref, dko_ref, dve_ref, dvo_ref, dc3_ref, dgt_ref, w_ref, x_ref, xn_ref, dh1_ref, g_ref,
             dx_ref, gw_ref, gb_ref, gg_ref, acc_ref):
        i = pl.program_id(0)

        @pl.when(i == 0)
        def _():
            for ref in (gb_ref, gg_ref, acc_ref):
                ref[...] = jnp.zeros_like(ref)

        dk = jnp.concatenate([dke_ref[...], dko_ref[...]], axis=0)
        dv = jnp.concatenate([dve_ref[...], dvo_ref[...]], axis=0)
        dp = jnp.concatenate([dq_ref[...], dk, dv, dc3_ref[...], dgt_ref[...]], axis=1)
        acc_ref[...] += lax.dot_general(dp, xn_ref[...], TN, preferred_element_type=F32)
        gb_ref[...] += jnp.sum(dp.astype(F32), axis=0, keepdims=True)
        dxn = jnp.dot(dp, w_ref[...], preferred_element_type=F32)
        xf = x_ref[...]
        dx, dg = _rms_bwd(dxn, xf, _rstd(xf), g_ref[...])
        dx_ref[...] = dh1_ref[...] + dx
        gg_ref[...] += jnp.sum(dg, axis=0, keepdims=True)

        @pl.when(i == nt - 1)
        def _():
            gw_ref[...] = acc_ref[...].astype(BF16)

    row = lambda w: pl.BlockSpec((tm, w), lambda i: (i, 0))
    acc = lambda w: pl.BlockSpec((1, w), lambda i: (0, 0))
    block = pl.BlockSpec((tm // 2, KV_W), lambda i: (i, 0))
    return pl.pallas_call(
        body, name="inproj_bwd", grid=(nt,),
        in_specs=[row(ATTN_W), block, block, block, block, row(C3_W), row(GATES_W), _resident((IN_W, D_MODEL)),
                  row(D_MODEL), row(D_MODEL), row(D_MODEL), _resident((1, D_MODEL))],
        out_specs=[row(D_MODEL), _resident((IN_W, D_MODEL)), acc(IN_W), acc(D_MODEL)],
        out_shape=[jax.ShapeDtypeStruct((s, D_MODEL), F32), jax.ShapeDtypeStruct((IN_W, D_MODEL), BF16),
                   jax.ShapeDtypeStruct((1, IN_W), F32), jax.ShapeDtypeStruct((1, D_MODEL), F32)],
        scratch_shapes=[pltpu.VMEM((IN_W, D_MODEL), F32)],
        compiler_params=_params("arbitrary"),
    )(dq, *dk, *dv, dc3, dgt, w_in, x, xn, dh1, g1)


def _wgrad(a, b, bm, bn, bk, name, comm=None):
    s, m = a.shape
    n = b.shape[1]
    nk = s // bk

    def body(a_ref, b_ref, o_ref, acc_ref):
        k = pl.program_id(2)

        @pl.when(k == 0)
        def _():
            acc_ref[...] = jnp.zeros_like(acc_ref)

        acc_ref[...] += lax.dot_general(a_ref[...].astype(BF16), b_ref[...].astype(BF16), TN,
                                        preferred_element_type=F32)

        @pl.when(k == nk - 1)
        def _():
            o_ref[...] = acc_ref[...].astype(BF16)

    return _call(
        comm, body, name=name, grid=(m // bm, n // bn, nk),
        in_specs=[pl.BlockSpec((bk, bm), lambda i, j, k: (k, i)), pl.BlockSpec((bk, bn), lambda i, j, k: (k, j))],
        out_specs=pl.BlockSpec((bm, bn), lambda i, j, k: (i, j)),
        out_shape=jax.ShapeDtypeStruct((m, n), BF16),
        scratch_shapes=[pltpu.VMEM((bm, bn), F32)],
        compiler_params=_params("parallel", "parallel", "arbitrary"),
    )(a, b)


class _Carry:
    def __init__(self, jobs, reads=None, bufs=None, fresh=None):
        self.jobs, self.reads, self.bufs, self.fresh = jobs, reads or {}, bufs or {}, fresh or {}
        self.out = {}


class _Job:
    def __init__(self, n_sems, plan):
        self.n_sems, self.plan = n_sems, plan


def _plan_all(jobs, hbm, send, recv):
    pos = _position()
    starts, waits, base = [], [], 0
    for job in jobs:
        s, w = job.plan(hbm, pos, send, recv, base)
        starts, waits, base = starts + s, waits + w, base + job.n_sems
    return starts, waits


def _call(comm, body, **kw):
    if comm is None:
        return pl.pallas_call(body, **kw)
    grid = kw["grid"]
    single = not isinstance(kw["out_shape"], (list, tuple))
    out_shape = [kw["out_shape"]] if single else list(kw["out_shape"])
    out_specs = [kw["out_specs"]] if single else list(kw["out_specs"])
    in_specs = list(kw["in_specs"])
    scratch = list(kw.get("scratch_shapes", ()))
    r_names, b_names, f_names = list(comm.reads), list(comm.bufs), list(comm.fresh)
    n_args, n_out, n_scr = len(in_specs), len(out_shape), len(scratch)
    n_sems = sum(j.n_sems for j in comm.jobs)

    def wrapped(*refs):
        k = n_args
        hbm = dict(zip(r_names, refs[k:k + len(r_names)]))
        k += len(r_names) + len(b_names)
        outs = refs[k:k + n_out]
        k += n_out
        hbm.update(zip(b_names + f_names, refs[k:k + len(b_names) + len(f_names)]))
        k += len(b_names) + len(f_names)
        send, recv = refs[k + n_scr:]
        starts, waits = _plan_all(comm.jobs, hbm, send, recv)
        ids = [pl.program_id(a) for a in range(len(grid))]
        first = functools.reduce(jnp.logical_and, [i == 0 for i in ids])
        last = functools.reduce(jnp.logical_and, [i == g - 1 for i, g in zip(ids, grid)])

        @pl.when(first)
        def _():
            for cp in starts:
                cp.start()

        body(*refs[:n_args], *outs, *refs[k:k + n_scr])

        @pl.when(last)
        def _():
            for cp in waits:
                cp.wait_recv()
            for cp in starts:
                cp.wait_send()

    sems = pltpu.SemaphoreType.DMA((n_sems,))
    held = [jax.ShapeDtypeStruct(a.shape, a.dtype) for a in comm.bufs.values()] + list(comm.fresh.values())
    call = pl.pallas_call(
        wrapped, name=kw["name"], grid=grid,
        in_specs=in_specs + [_ANY] * (len(r_names) + len(b_names)),
        out_specs=out_specs + [_ANY] * len(held),
        out_shape=out_shape + held,
        input_output_aliases={n_args + len(r_names) + i: n_out + i for i in range(len(b_names))},
        scratch_shapes=scratch + [sems, sems],
        compiler_params=_params(*["arbitrary"] * len(grid)),
    )

    def run(*args):
        res = call(*args, *comm.reads.values(), *comm.bufs.values())
        comm.out = dict(zip(b_names + f_names, res[n_out:]))
        return res[0] if single else res[:n_out]

    return run


def _exchange(name, phases, reads=None, bufs=None, fresh=None):
    comm = _Carry([j for ph in phases for j in ph], reads, bufs, fresh)
    r_names, b_names, f_names = list(comm.reads), list(comm.bufs), list(comm.fresh)
    n_sems = sum(j.n_sems for j in comm.jobs)

    def body(*refs):
        hbm = dict(zip(r_names, refs[:len(r_names)]))
        k = len(r_names) + len(b_names)
        hbm.update(zip(b_names + f_names, refs[k:k + len(b_names) + len(f_names)]))
        send, recv = refs[-2:]
        pos = _position()
        started, base = [], 0
        for ph in phases:
            waits = []
            for job in ph:
                s, w = job.plan(hbm, pos, send, recv, base)
                base += job.n_sems
                for cp in s:
                    cp.start()
                started, waits = started + s, waits + w
            for cp in waits:
                cp.wait_recv()
        for cp in started:
            cp.wait_send()

    sems = pltpu.SemaphoreType.DMA((n_sems,))
    held = [jax.ShapeDtypeStruct(a.shape, a.dtype) for a in comm.bufs.values()] + list(comm.fresh.values())
    res = pl.pallas_call(
        body, name=name, in_specs=[_ANY] * (len(r_names) + len(b_names)), out_specs=[_ANY] * len(held),
        out_shape=held, input_output_aliases={len(r_names) + i: i for i in range(len(b_names))},
        scratch_shapes=[sems, sems],
    )(*comm.reads.values(), *comm.bufs.values())
    return dict(zip(b_names + f_names, res))


_HBM = pl.BlockSpec(memory_space=pltpu.HBM)
_SEM = pl.BlockSpec(memory_space=pltpu.SEMAPHORE)
_EFFECT = pltpu.SideEffectType.DATAFLOW_SIDE_EFFECTING


def _start_exchanges(name, groups):
    names = [list(arrays) for _, arrays in groups]
    first = [sum(len(ns) for ns in names[:g]) for g in range(len(groups))]
    n, ng = sum(len(ns) for ns in names), len(groups)

    def body(*refs):
        for g, (jobs, _) in enumerate(groups):
            hbm = dict(zip(names[g], refs[first[g]:first[g] + len(names[g])]))
            for cp in _plan_all(jobs, hbm, refs[n + 2 * g], refs[n + 2 * g + 1])[0]:
                cp.start()
        refs[-1][...] = jnp.zeros_like(refs[-1])

    given = [pltpu.with_memory_space_constraint(
        a if isinstance(a, jax.Array) else lax.empty(a.shape, a.dtype), pltpu.HBM)
        for _, arrays in groups for a in arrays.values()]
    sems = [pltpu.SemaphoreType.DMA((sum(j.n_sems for j in jobs),)) for jobs, _ in groups for _ in range(2)]
    res = pl.pallas_call(
        body, name=name,
        out_shape=(*sems, *[pltpu.HBM(a.shape, a.dtype) for a in given], jax.ShapeDtypeStruct((8, 128), F32)),
        in_specs=[_HBM] * n, out_specs=(*[_SEM] * (2 * ng), *[_HBM] * n, pl.BlockSpec(memory_space=pltpu.VMEM)),
        input_output_aliases={i: 2 * ng + i for i in range(n)},
        compiler_params=pltpu.CompilerParams(has_side_effects=_EFFECT),
    )(*given)
    held = res[2 * ng:2 * ng + n]
    states = [(names[g], groups[g][0], res[2 * g], res[2 * g + 1], held[first[g]:first[g] + len(names[g])])
              for g in range(ng)]
    return states, res[-1]


def _start_exchange(name, jobs, arrays):
    states, token = _start_exchanges(name, [(jobs, arrays)])
    return states[0], token


def _finish_exchange(name, state, after):
    names, jobs, send_sem, recv_sem, held = state
    n = len(names)

    def body(*refs):
        hbm = dict(zip(names, refs[:n]))
        send, recv = refs[n:n + 2]
        starts, waits = _plan_all(jobs, hbm, send, recv)
        for cp in waits:
            cp.wait_recv()
        for cp in starts:
            cp.wait_send()

    res = pl.pallas_call(
        body, name=name, out_shape=tuple(pltpu.HBM(a.shape, a.dtype) for a in held),
        in_specs=[_HBM] * n + [_SEM, _SEM, _ANY], out_specs=tuple([_HBM] * n),
        input_output_aliases={i: i for i in range(n)},
        compiler_params=pltpu.CompilerParams(has_side_effects=_EFFECT),
    )(*held, send_sem, recv_sem, after)
    return dict(zip(names, res))


def _row_tile(rows, bytes_per_row):
    best = 16
    for t in range(16, rows + 1, 16):
        if rows % t == 0 and t * bytes_per_row <= 9 * 1024 * 1024:
            best = t
    return best


def _rowwise(fn, ins, out_dtypes, name, after=None):
    rows, cols = ins[0].shape
    per_row = sum(cols * a.dtype.itemsize for a in ins) + sum(cols * jnp.dtype(d).itemsize for d in out_dtypes)
    tr = _row_tile(rows, per_row)
    n_in = len(ins)

    def body(*refs):
        outs = fn(*[r[...] for r in refs[:n_in]])
        for o_ref, o in zip(refs[-len(out_dtypes):], outs):
            o_ref[...] = o.astype(o_ref.dtype)

    tile = pl.BlockSpec((tr, cols), lambda i: (i, 0))
    behind = [] if after is None else [after]
    return pl.pallas_call(
        body, name=name, grid=(rows // tr,),
        in_specs=[tile] * n_in + [pl.BlockSpec((8, 128), lambda i: (0, 0))] * len(behind),
        out_specs=[tile] * len(out_dtypes),
        out_shape=[jax.ShapeDtypeStruct((rows, cols), d) for d in out_dtypes],
        compiler_params=_params("parallel"),
    )(*ins, *behind)


def _tiled(fn, name, grid, pos, ins, outs):
    n_in = len(ins)

    def body(pos_ref, *refs):
        res = fn(*[r[...] for r in refs[:n_in]])
        for o_ref, o in zip(refs[n_in:], res):
            o_ref[...] = o.astype(o_ref.dtype)

    return pl.pallas_call(
        body, name=name,
        grid_spec=pltpu.PrefetchScalarGridSpec(
            num_scalar_prefetch=1, grid=grid,
            in_specs=[pl.BlockSpec(bs, im) for _, bs, im in ins],
            out_specs=[pl.BlockSpec(bs, im) for _, _, bs, im in outs]),
        out_shape=[jax.ShapeDtypeStruct(s, d) for s, d, _, _ in outs],
        compiler_params=_params("parallel"),
    )(pos, *[a for a, _, _ in ins])


def _adamw(w, g, m, v):
    m = ADAM_B1 * m + (1.0 - ADAM_B1) * g
    v = ADAM_B2 * v + (1.0 - ADAM_B2) * (g * g)
    m_hat = m / (1.0 - ADAM_B1 ** ADAM_STEP)
    v_hat = v / (1.0 - ADAM_B2 ** ADAM_STEP)
    return -ADAM_LR * (m_hat / (jnp.sqrt(v_hat) + ADAM_EPS) + ADAM_WD * w), m, v


def _adamw_small(pos, own, slots, params):
    n = len(params)

    def body(pos_ref, own_ref, slots_ref, *refs):
        ins, outs, total_ref = refs[:3 * n], refs[3 * n:-1], refs[-1]
        chip = pos_ref[0]
        idx = 2 * chip + pos_ref[1]
        term = lambda q: jnp.where(idx == q, own_ref[...], slots_ref[q])
        acc = term(0)
        for q in range(1, N_DEV):
            acc = acc + term(q)
        total_ref[...] = acc
        outs[0][...] = total_ref[0:1, :]
        for k, (w, _, _, row) in enumerate(params):
            width = min(w.shape[-1], 128)
            for t in range(w.shape[0]):
                for j in range(w.shape[-1] // width):
                    lanes = slice(j * width, (j + 1) * width)
                    at = (slice(t, t + 1), lanes) if w.ndim == 2 else (t, slice(None), lanes)
                    g = total_ref[pl.ds(row(t, j, chip), 1), :][:, :width]
                    new = _adamw(ins[3 * k][at], g, ins[3 * k + 1][at], ins[3 * k + 2][at])
                    for o_ref, o in zip(outs[1 + 4 * k:5 + 4 * k], (g, *new)):
                        o_ref[at] = o

    vmem = pl.BlockSpec(memory_space=pltpu.VMEM)
    return pl.pallas_call(
        body, name="adamw_small",
        in_specs=[pl.BlockSpec(memory_space=pltpu.SMEM)] + [vmem] * (2 + 3 * n),
        out_shape=[jax.ShapeDtypeStruct((1, 128), F32)]
        + [jax.ShapeDtypeStruct(p[0].shape, F32) for p in params for _ in range(4)],
        scratch_shapes=[pltpu.VMEM(own.shape, F32)],
    )(pos, own, slots, *[a for p in params for a in p[:3]])


class _Layout:
    def __init__(self, rows, cols, stacked):
        self.rows, self.cols, self.stacked = rows, cols, stacked

    def whole(self, rows=None):
        r = self.rows if rows is None else rows
        return (N_CHIPS, r, self.cols) if self.stacked else (r, N_CHIPS * self.cols)

    def part_rows(self, h, q=0, nq=1):
        n = self.rows // 2 // nq
        return pl.ds(pl.multiple_of(h * (self.rows // 2) + q * n, 16), n)

    def half_rows(self, h):
        return self.part_rows(h)

    def block(self, ref, p, rows=slice(None)):
        if self.stacked:
            return ref.at[p, rows, :]
        return ref.at[rows, pl.ds(pl.multiple_of(p * self.cols, 128), self.cols)]

    def all_chips(self, ref, rows):
        return ref.at[:, rows, :] if self.stacked else ref.at[rows, :]


BIG = (
    _Layout(IN_SHARD, D_MODEL, True),
    _Layout(ATTN_W + CONV_W, D_MODEL // N_CHIPS, False),
    _Layout(D_MODEL // N_CHIPS, D_MODEL, True),
    _Layout(D_MODEL, FF2 // N_CHIPS, False),
    _Layout(D_FF // N_CHIPS, D_MODEL, True),
)
N_BIG = len(BIG)
_ANY = pl.BlockSpec(memory_space=pl.ANY)


def _position():
    x, y, c = lax.axis_index("x"), lax.axis_index("y"), lax.axis_index("c")
    return x, y, c, 2 * x + y


def _core_of_chip(p, c):
    return (p >> 1, p & 1, c)


def _place_cast(shard, lay, pos, name, after=None):
    rows, cols = shard.shape
    tr = _row_tile(rows, cols * 6)
    if lay.stacked:
        out = (lay.whole(), BF16, (None, tr, cols), lambda i, pos: (pos[0], i, 0))
    else:
        out = (lay.whole(), BF16, (tr, cols), lambda i, pos: (i, pos[0]))
    ins = [(shard, (tr, cols), lambda i, pos: (i, 0))]
    if after is not None:
        ins.append((after, (8, 128), lambda i, pos: (0, 0)))
    return _tiled(lambda a, *_: (a,), name, (rows // tr,), pos, ins, [out])[0]


def _place_cast_pair(top, bottom, lay, pos, name, after=None):
    rows, cols = top.shape
    ins = [(top, (rows, cols), lambda i, pos: (0, 0)), (bottom, (rows, cols), lambda i, pos: (0, 0))]
    if after is not None:
        ins.append((after, (8, 128), lambda i, pos: (0, 0)))
    return _tiled(lambda a, b, *_: (jnp.concatenate([a, b], axis=0),), name, (1,), pos, ins,
                  [(lay.whole(), BF16, (2 * rows, cols), lambda i, pos: (0, pos[0]))])[0]


def _adamw_pair(top, bottom, g, after=None):
    rows = top[0].shape[0]

    def body(*refs):
        (wa, ma, va, wb, mb, vb, g_ref), outs = refs[:7], refs[-8:]
        for (w, m, v), gg, o in (((wa, ma, va), g_ref[:rows], outs[:4]), ((wb, mb, vb), g_ref[rows:], outs[4:])):
            for o_ref, val in zip(o, (gg, *_adamw(w[...], gg, m[...], v[...]))):
                o_ref[...] = val

    behind = [] if after is None else [after[0:8, 0:128]]
    res = pl.pallas_call(
        body, name="adamw_w_br", out_shape=[jax.ShapeDtypeStruct(top[0].shape, F32)] * 8,
    )(*top, *bottom, g, *behind)
    return res[:4], res[4:]


def _remote(src, dst, send, recv, k, device):
    return pltpu.make_async_remote_copy(src_ref=src, dst_ref=dst, send_sem=send.at[k], recv_sem=recv.at[k],
                                        device_id=device, device_id_type=MESH)


def _arrival(dst, send, recv, k, me):
    return _remote(dst, dst, send, recv, k, me)


def _gather_ici(lay, name, q=0, nq=1):
    def plan(hbm, pos, send, recv, base):
        x, y, c, me = pos
        rows = lay.part_rows(c, q, nq)
        mine = lay.block(hbm[name], me, rows)
        starts = [_remote(mine, mine, send, recv, base + d - 1, _core_of_chip(me ^ d, c)) for d in (1, 2, 3)]
        waits = [_arrival(lay.block(hbm[name], me ^ d, rows), send, recv, base + d - 1, (x, y, c)) for d in (1, 2, 3)]
        return starts, waits
    return _Job(3, plan)


def _gather_d2d(lay, name, q=0, nq=1):
    def plan(hbm, pos, send, recv, base):
        x, y, c, me = pos
        starts, waits = [], []
        for d in (1, 2, 3):
            got = lay.block(hbm[name], me ^ d, lay.part_rows(c, q, nq))
            starts.append(_remote(got, got, send, recv, base + d - 1, (x, y, 1 - c)))
            waits.append(_arrival(lay.block(hbm[name], me ^ d, lay.part_rows(1 - c, q, nq)), send, recv, base + d - 1,
                                  (x, y, c)))
        return starts, waits
    return _Job(3, plan)


def _rs_pair(lay, grad, theirs):
    def plan(hbm, pos, send, recv, base):
        x, y, c, _ = pos
        out = _remote(lay.all_chips(hbm[grad], lay.half_rows(1 - c)), hbm[theirs], send, recv, base, (x, y, 1 - c))
        return [out], [_arrival(hbm[theirs], send, recv, base, (x, y, c))]
    return _Job(1, plan)


def _rs_chips(lay, sums, slots):
    def plan(hbm, pos, send, recv, base):
        x, y, c, me = pos
        starts = [_remote(lay.block(hbm[sums], me ^ d), hbm[slots].at[me], send, recv, base + d - 1,
                          _core_of_chip(me ^ d, c)) for d in (1, 2, 3)]
        waits = [_arrival(hbm[slots].at[me ^ d], send, recv, base + d - 1, (x, y, c)) for d in (1, 2, 3)]
        return starts, waits
    return _Job(3, plan)


def _rs_share(lay, shard):
    def plan(hbm, pos, send, recv, base):
        x, y, c, _ = pos
        mine = hbm[shard].at[lay.half_rows(c), :]
        other = hbm[shard].at[lay.half_rows(1 - c), :]
        return [_remote(mine, mine, send, recv, base, (x, y, 1 - c))], [_arrival(other, send, recv, base, (x, y, c))]
    return _Job(1, plan)


def _slots_shape(lay):
    return jax.ShapeDtypeStruct((N_CHIPS, lay.rows // 2, lay.cols), BF16)


def _theirs_shape(lay, dtype=BF16):
    return jax.ShapeDtypeStruct(lay.whole(lay.rows // 2), dtype)


def _pair_sum(grad, theirs, lay, pos, name):
    half = lay.rows // 2
    add = lambda a, b: (a.astype(F32) + b.astype(F32),)
    if lay.stacked:
        tr = _row_tile(half, lay.cols * 6)
        nt = half // tr
        flat = lambda a: a.reshape(-1, lay.cols)
        mine = lambda t, pos: ((t // nt) * (2 * nt) + pos[1] * nt + t % nt, 0)
        grid, blk = (N_CHIPS * nt,), (tr, lay.cols)
        grad, theirs = flat(grad), flat(theirs)
    else:
        tr = _row_tile(half, N_CHIPS * lay.cols * 6)
        nt = half // tr
        mine = lambda t, pos: (pos[1] * nt + t, 0)
        grid, blk = (nt,), (tr, N_CHIPS * lay.cols)
    same = lambda t, pos: (t, 0)
    out = _tiled(add, name, grid, pos, [(grad, blk, mine), (theirs, blk, same)], [(theirs.shape, BF16, blk, same)])[0]
    return out.reshape(lay.whole(half))


def _chip_sum(sums, slots, lay, pos, name, after=None):
    half = lay.rows // 2
    tr = _row_tile(half, lay.cols * 12)
    nt = half // tr
    blk3 = (None, tr, lay.cols)
    if lay.stacked:
        own = (sums, blk3, lambda i, pos: (pos[0], i, 0))
    else:
        own = (sums, (tr, lay.cols), lambda i, pos: (i, pos[0]))
    others = [(slots, blk3, functools.partial(lambda d, i, pos: (pos[0] ^ d, i, 0), d)) for d in (1, 2, 3)]

    def add(a, b1, b2, b3, *_):
        return (((a.astype(F32) + b1.astype(F32)) + b2.astype(F32)) + b3.astype(F32),)

    if after is not None:
        others.append((after, (8, 128), lambda i, pos: (0, 0)))
    return _tiled(add, name, (nt,), pos, [own] + others,
                  [((lay.rows, lay.cols), F32, (tr, lay.cols), lambda i, pos: (pos[1] * nt + i, 0))])[0]


N_DEV = 8


def _to_all(src, slots):
    def plan(hbm, pos, send, recv, base):
        x, y, c, _ = pos
        idx = 4 * x + 2 * y + c
        starts = [_remote(hbm[src], hbm[slots].at[idx], send, recv, base + k - 1,
                          (x ^ (k >> 2), y ^ ((k >> 1) & 1), c ^ (k & 1))) for k in range(1, N_DEV)]
        waits = [_arrival(hbm[slots].at[idx ^ k], send, recv, base + k - 1, (x, y, c)) for k in range(1, N_DEV)]
        return starts, waits
    return _Job(N_DEV - 1, plan)


def _pack_rows(parts):
    padded = [jnp.pad(a, ((0, -a.shape[0] % 8), (0, 0))) for a in parts]
    starts = [sum(p.shape[0] for p in padded[:k]) for k in range(len(padded))]
    return jnp.concatenate(padded, axis=0), starts


def kernel(x, mix_norm, w_in, b_in, sinks, conv_w, w_attn_branch, w_conv_branch, w_out, ffn_norm, w_up, ffn_conv_w, w_down, final_norm, loss_target, m_mix_norm, m_w_in, m_b_in, m_sinks, m_conv_w, m_w_attn_branch, m_w_conv_branch, m_w_out, m_ffn_norm, m_w_up, m_ffn_conv_w, m_w_down, m_final_norm, v_mix_norm, v_w_in, v_b_in, v_sinks, v_conv_w, v_w_attn_branch, v_w_conv_branch, v_w_out, v_ffn_norm, v_w_up, v_ffn_conv_w, v_w_down, v_final_norm):
    me = 2 * lax.axis_index("x") + lax.axis_index("y")
    names = ("w_in", "w_br", "w_out", "w_up", "w_down")
    w_of = dict(w_in=w_in[0].T, w_out=w_out[0], w_up=w_up[0], w_down=w_down[0])
    m_of = dict(w_in=m_w_in[0].T, w_out=m_w_out[0], w_up=m_w_up[0], w_down=m_w_down[0])
    v_of = dict(w_in=v_w_in[0].T, w_out=v_w_out[0], w_up=v_w_up[0], w_down=v_w_down[0])
    ab = (w_attn_branch[0], m_w_attn_branch[0], v_w_attn_branch[0])
    cb = (w_conv_branch[0], m_w_conv_branch[0], v_w_conv_branch[0])

    pos = jnp.stack([me, lax.axis_index("c")]).astype(jnp.int32)

    lay = dict(zip(names, BIG))
    xs, target, sk = x[0], loss_target[0], sinks[0]
    s = xs.shape[0]
    tm, tm2, bk, bk2 = min(256, s), min(512, s), min(1024, s), min(2048, s)

    taps, (_, t0) = _pack_rows([conv_w[0], ffn_conv_w[0].reshape(3 * (FF2 // N_CHIPS // 128), 128)])
    placed = {"w_in": _place_cast(w_of["w_in"], lay["w_in"], pos, "cast_w_in")}
    fly_in, started = _start_exchange("gather_in_start", [_gather_ici(lay["w_in"], "w_in")], {"w_in": placed["w_in"]})
    taps_flight, started = _start_exchange("taps_start", [_to_all("v", "slots")],
                                           {"v": taps + started[0:1], "slots": jnp.zeros((N_DEV, *taps.shape), F32)})
    placed["w_br"] = _place_cast_pair(ab[0], cb[0], lay["w_br"], pos, "cast_w_br", after=started)
    for n in names[2:]:
        placed[n] = _place_cast(w_of[n], lay[n], pos, "cast_" + n, after=started)
    trio = ("w_br", "w_out")
    (fly_trio, fly_up, fly_down), started = _start_exchanges("gather_rest_start", [
        ([_gather_ici(lay[n], n) for n in ws], {n: placed[n] for n in ws}) for ws in (trio, ("w_up",), ("w_down",))])

    got = _finish_exchange("gather_in_wait", fly_in, after=started)
    w_in_full = _exchange("gather_in_d2d", [[_gather_d2d(lay["w_in"], "w_in")]], bufs=got)["w_in"].reshape(IN_W, D_MODEL)
    xn, qkv, c3, gates = _inproj_fwd(xs, mix_norm, w_in_full, b_in, tm2)
    k2 = _Carry([_gather_d2d(lay[n], n) for n in trio], bufs=_finish_exchange("gather_trio_wait", fly_trio, after=qkv))
    attn = _attn_fwd(qkv, sk, comm=k2)
    w_br = k2.out["w_br"]
    w_out_full = k2.out["w_out"].reshape(D_MODEL, D_MODEL)
    k3 = _Carry([_gather_d2d(lay["w_up"], "w_up")], bufs=_finish_exchange("gather_up_wait", fly_up, after=attn))
    taps = _finish_exchange("taps_wait", taps_flight, after=attn)
    taps = lax.dynamic_update_slice(taps["slots"], taps["v"][None], (2 * me + lax.axis_index("c"), 0, 0))
    conv_full = taps[0::2, 0:3].transpose(1, 0, 2).reshape(3, CONV_W)
    ffn_cw_full = taps[0::2, t0:t0 + 33].reshape(N_CHIPS, 3, FF2 // N_CHIPS).transpose(1, 0, 2).reshape(3, FF2)
    conv, a, cv, merged, h1, hn = _mix_fwd(xs, attn, c3, gates, conv_full, w_br, w_out_full, ffn_norm, tm2, comm=k3)
    w_up_full = k3.out["w_up"]
    w_down_full = _exchange("gather_down_d2d", [[_gather_d2d(lay["w_down"], "w_down")]],
                            bufs=_finish_exchange("gather_down_wait", fly_down, after=hn))["w_down"].reshape(D_FF, D_MODEL)
    u, up, act, dh2, loss_part, g_fn = _ffn_fwd_loss(hn, h1, w_up_full, ffn_cw_full, w_down_full,
                                                     final_norm[None, :], target, tm)

    grads, sums, slots = {}, {}, {}

    def pair(*ws):
        return _Carry([_rs_pair(lay[n], "g_" + n, "t_" + n) for n in ws], reads={"g_" + n: grads[n] for n in ws},
                      fresh={"t_" + n: _theirs_shape(lay[n], grads[n].dtype) for n in ws})

    def chips(*ws, also=None):
        k = _Carry([_rs_chips(lay[n], "s_" + n, "r_" + n) for n in ws], reads={"s_" + n: sums[n] for n in ws},
                   fresh={"r_" + n: _slots_shape(lay[n]) for n in ws})
        if also is not None:
            k = _Carry(k.jobs + also.jobs, {**k.reads, **also.reads}, None, {**k.fresh, **also.fresh})
        return k

    def pair_sums(k, *ws):
        for n in ws:
            sums[n] = _pair_sum(grads[n], k.out["t_" + n], lay[n], pos, "pair_sum_" + n)

    def take_slots(k, *ws):
        for n in ws:
            slots[n] = k.out["r_" + n]

    du, dh1, g_fcw, g_g2 = _ffn_bwd(dh2, u, up, h1, w_up_full, ffn_cw_full, w_down_full, ffn_norm, tm)
    grads["w_down"] = _wgrad(act, dh2, D_FF // 2, D_MODEL, bk2, "wgrad_down").reshape(lay["w_down"].whole())
    k4 = pair("w_down")
    grads["w_up"] = _wgrad(hn, du, D_MODEL, FF2 // 4, bk2, "wgrad_up", comm=k4)
    pair_sums(k4, "w_down")
    k5 = chips("w_down", also=pair("w_up"))
    dattn, dc3, dgt, g_cw, grads["w_br"], gw_out = _mix_bwd(
        dh1, gates, a, cv, c3, attn, conv, merged, conv_full, w_br, w_out_full, tm2, comm=k5)
    grads["w_out"] = gw_out.reshape(lay["w_out"].whole())
    take_slots(k5, "w_down")
    pair_sums(k5, "w_up")
    up_flight, started = _start_exchange("rs_chips_up_start", [_rs_chips(lay["w_up"], "s", "r")],
                                         {"s": sums["w_up"], "r": _slots_shape(lay["w_up"])})
    k6 = pair(*trio)
    k6.reads["after"] = started
    dq, dk_even, dk_odd, dv_even, dv_odd, g_sk = _attn_bwd(qkv, sk, attn, dattn, comm=k6)
    pair_sums(k6, *trio)
    trio_flight, started = _start_exchange(
        "rs_chips_trio_start", [_rs_chips(lay[n], "s_" + n, "r_" + n) for n in trio],
        {**{"s_" + n: sums[n] for n in trio}, **{"r_" + n: _slots_shape(lay[n]) for n in trio}})
    behind = mix_norm + jnp.tile(started[0:1], (1, D_MODEL // 128))
    grad_x, gw_in, g_b, g_g1 = _inproj_bwd(dq, (dk_even, dk_odd), (dv_even, dv_odd), dc3, dgt, w_in_full, xs, xn,
                                           dh1, behind)
    grads["w_in"] = gw_in.reshape(lay["w_in"].whole())

    parts = [loss_part, g_g1, g_b, jnp.pad(g_sk[:, 0], (0, 120))[None, :], g_cw, g_g2, g_fcw, g_fn]
    packed, at = _pack_rows([p.reshape(-1, 128) for p in parts])
    small_flight, started = _start_exchange("small_start", [_to_all("v", "slots")],
                                            {"v": packed, "slots": jnp.zeros((N_DEV, *packed.shape), F32)})
    in_flight, started = _start_exchange("rs_pair_in_start", [_rs_pair(lay["w_in"], "g", "t")],
                                         {"g": grads["w_in"], "t": _theirs_shape(lay["w_in"]), "behind": started})
    landed = _finish_exchange("rs_chips_up_wait", up_flight, after=started)
    halves = {"w_up": _chip_sum(landed["s"], landed["r"], lay["w_up"], pos, "chip_sum_w_up"),
              "w_down": _chip_sum(sums["w_down"], slots["w_down"], lay["w_down"], pos, "chip_sum_w_down", after=started)}
    landed = _finish_exchange("rs_pair_in_wait", in_flight, after=halves["w_down"])
    sums["w_in"] = _pair_sum(landed["g"], landed["t"], lay["w_in"], pos, "pair_sum_w_in")
    in_flight, started = _start_exchange("rs_chips_in_start", [_rs_chips(lay["w_in"], "s", "r")],
                                         {"s": sums["w_in"], "r": _slots_shape(lay["w_in"])})
    landed = _finish_exchange("rs_chips_trio_wait", trio_flight, after=started)
    for n in trio:
        halves[n] = _chip_sum(landed["s_" + n], landed["r_" + n], lay[n], pos, "chip_sum_" + n)
    shared = _exchange("share_halves", [[_rs_share(lay[n], n) for n in names[1:]]], bufs=halves)

    def adam(n, g, after=None):
        return _rowwise(lambda w, g, m, v: (g, *_adamw(w, g, m, v)), [w_of[n], g, m_of[n], v_of[n]], [F32] * 4,
                        "adamw_" + n, after=after)

    new_of, last = {}, None
    for n in ("w_up", "w_down", "w_out"):
        new_of[n] = adam(n, shared[n], last)
        last = new_of[n][1]
    new_of["w_ab"], new_of["w_cb"] = _adamw_pair(ab, cb, shared["w_br"], after=last)
    last = new_of["w_cb"][1]

    arrived = _finish_exchange("small_wait", small_flight, after=last)
    flat = lambda k: lambda t, j, chip: at[k] + j
    mine = lambda k, per_tap: lambda t, j, chip: at[k] + per_tap * t + (per_tap // N_CHIPS) * chip + j
    rows = lambda a: a.reshape(a.shape[1], 1, a.shape[2])
    small_p = [
        (mix_norm, m_mix_norm, v_mix_norm, flat(1)), (b_in, m_b_in, v_b_in, flat(2)), (sinks, m_sinks, v_sinks, flat(3)),
        (rows(conv_w), rows(m_conv_w), rows(v_conv_w), mine(4, CONV_W // 128)),
        (ffn_norm, m_ffn_norm, v_ffn_norm, flat(5)),
        (rows(ffn_conv_w), rows(m_ffn_conv_w), rows(v_ffn_conv_w), mine(6, FF2 // 128)),
        (final_norm[None, :], m_final_norm[None, :], v_final_norm[None, :], flat(7))]
    small_new = _adamw_small(pos, arrived["v"], arrived["slots"], small_p)
    loss = small_new[0][0, 0]
    small_g = small_new[1::4]
    small_new = [small_new[4 * k + 2:4 * k + 5] for k in range(len(small_p))]

    landed = _finish_exchange("rs_chips_in_wait", in_flight, after=small_new[0][0])
    half_in = _chip_sum(landed["s"], landed["r"], lay["w_in"], pos, "chip_sum_w_in")
    shared["w_in"] = _exchange("share_in", [[_rs_share(lay["w_in"], "w_in")]], bufs={"w_in": half_in})["w_in"]
    new_of["w_in"] = [a.T for a in adam("w_in", shared["w_in"])]
    big = ("w_in", "w_ab", "w_cb", "w_out", "w_up", "w_down")
    big_g = [new_of[n][0] for n in big]
    big_new = [new_of[n][1:] for n in big]

    order = [("s", 0), ("b", 0), ("s", 1), ("s", 2), ("s", 3), ("b", 1), ("b", 2), ("b", 3), ("s", 4), ("b", 4),
             ("s", 5), ("b", 5), ("s", 6)]
    shapes = [mix_norm.shape, w_in.shape, b_in.shape, sinks.shape, conv_w.shape, w_attn_branch.shape,
              w_conv_branch.shape, w_out.shape, ffn_norm.shape, w_up.shape, ffn_conv_w.shape, w_down.shape,
              final_norm.shape]
    out_g = [(small_g[k] if kind == "s" else big_g[k]).reshape(shp) for (kind, k), shp in zip(order, shapes)]
    news = [[(small_new[k][j] if kind == "s" else big_new[k][j]).reshape(shp) for (kind, k), shp in zip(order, shapes)]
            for j in range(3)]
    return (loss, grad_x[None], *out_g, *news[0], *news[1], *news[2])
```

```python
import functools

import jax
import jax.numpy as jnp
from jax import lax
from jax.experimental import pallas as pl
from jax.experimental.pallas import tpu as pltpu

F32 = jnp.float32
BF16 = jnp.bfloat16

D_MODEL = 1024
HEAD_DIM = 64
N_HEADS = 8
N_KV_HEADS = 2
GROUP = N_HEADS // N_KV_HEADS
BLOCK = 128
ATTN_SCALE = HEAD_DIM ** -0.5
ATTN_W = N_HEADS * HEAD_DIM
KV_W = N_KV_HEADS * HEAD_DIM
CONV_W = 512
QKV_W = ATTN_W + 2 * KV_W
C3_W = 3 * CONV_W
GATES_W = 2 * D_MODEL
IN_W = QKV_W + C3_W + GATES_W
D_FF = 2816
FF2 = 2 * D_FF
NORM_EPS = 1e-5
N_CHIPS = 4
IN_SHARD = IN_W // N_CHIPS
NEG = -1e30

ADAM_LR = 0.001
ADAM_B1 = 0.9
ADAM_B2 = 0.999
ADAM_EPS = 1e-08
ADAM_WD = 0.01
ADAM_STEP = 10

VMEM_LIMIT = 56 * 1024 * 1024
MESH = pl.DeviceIdType.MESH

NT = (((1,), (1,)), ((), ()))
TN = (((0,), (0,)), ((), ()))


def _params(*sem):
    return pltpu.CompilerParams(dimension_semantics=sem, vmem_limit_bytes=VMEM_LIMIT)


def _resident(shape):
    return pl.BlockSpec(shape, lambda *_: (0,) * len(shape), pipeline_mode=pl.Buffered(1))


def _sigmoid(v):
    return 0.5 * jnp.tanh(0.5 * v) + 0.5


def _rstd(v):
    return lax.rsqrt(jnp.mean(v * v, axis=-1, keepdims=True) + NORM_EPS)


def _rms_bwd(dy, v, rstd, g):
    vhat = v * rstd
    t = dy * g
    return rstd * (t - vhat * jnp.mean(t * vhat, axis=-1, keepdims=True)), dy * vhat


def _taps(z, cw):
    return cw[2:3] * z + cw[1:2] * pltpu.roll(z, 1, 0) + cw[0:1] * pltpu.roll(z, 2, 0)


def _causal_conv(z, prev, cw):
    edge = _taps(jnp.concatenate([prev, z[0:8]], axis=0), cw)
    return jnp.concatenate([edge[8:16], _taps(z, cw)[8:]], axis=0)


def _rows_after(z, nxt):
    n = z.shape[0]
    edge = jnp.concatenate([z[n - 8:n], nxt], axis=0)
    return tuple(jnp.concatenate([pltpu.roll(z, n - k, 0)[:n - 8], pltpu.roll(edge, 16 - k, 0)[0:8]], axis=0)
                 for k in (1, 2))


def _inproj_fwd(x, g1, w_in, b_in, tm, comm=None):
    s = x.shape[0]

    def body(x_ref, g_ref, w_ref, b_ref, xn_ref, qkv_ref, c3_ref, gt_ref):
        xf = x_ref[...]
        xn = (xf * _rstd(xf) * g_ref[...]).astype(BF16)
        xn_ref[...] = xn

        proj = (lax.dot_general(xn, w_ref[...], NT, preferred_element_type=F32) + b_ref[...]).astype(BF16)
        qkv_ref[...] = proj[:, :QKV_W]
        c3_ref[...] = proj[:, QKV_W:QKV_W + C3_W]
        gt_ref[...] = proj[:, QKV_W + C3_W:]

    row = lambda w: pl.BlockSpec((tm, w), lambda i: (i, 0))
    return _call(
        comm, body, name="inproj_fwd", grid=(s // tm,),
        in_specs=[row(D_MODEL), _resident((1, D_MODEL)), _resident((IN_W, D_MODEL)), _resident((1, IN_W))],
        out_specs=[row(D_MODEL), row(QKV_W), row(C3_W), row(GATES_W)],
        out_shape=[jax.ShapeDtypeStruct((s, D_MODEL), BF16), jax.ShapeDtypeStruct((s, QKV_W), BF16),
                   jax.ShapeDtypeStruct((s, C3_W), BF16), jax.ShapeDtypeStruct((s, GATES_W), BF16)],
        compiler_params=_params("parallel"),
    )(x, g1, w_in, b_in)


def _attn_bias():
    kj = jnp.arange(2 * BLOCK)[:, None]
    qi = (jnp.arange(GROUP * BLOCK) % BLOCK)[None, :]
    band = (kj > qi) & (kj <= qi + BLOCK)
    return jnp.stack([jnp.where(band & (kj >= BLOCK), 0.0, NEG), jnp.where(band, 0.0, NEG)]).astype(F32)


def _attn_bias_specs():
    shape = (None, 2 * BLOCK, GROUP * BLOCK)
    return pl.BlockSpec(shape, lambda i: (jnp.minimum(i, 1), 0, 0)), pl.BlockSpec(shape, lambda i: (1, 0, 0))


def _sink_row(sk_ref, h):
    lane = lax.broadcasted_iota(jnp.int32, (1, GROUP * BLOCK), 1)
    row = jnp.full((1, GROUP * BLOCK), sk_ref[h * GROUP], F32)
    for g in range(1, GROUP):
        row = jnp.where(lane >= g * BLOCK, sk_ref[h * GROUP + g], row)
    return row


def _stack_heads(t, h):
    return jnp.concatenate(
        [t[:, (h * GROUP + g) * HEAD_DIM:(h * GROUP + g + 1) * HEAD_DIM] for g in range(GROUP)], axis=0)


def _unstack_heads(per_kv):
    return jnp.concatenate(
        [t[g * BLOCK:(g + 1) * BLOCK] for t in per_kv for g in range(GROUP)], axis=1)


def _block_specs(n, steps):
    cur = lambda i: jnp.minimum(i, steps - 1)
    prev = lambda i: jnp.maximum(n * jnp.minimum(i, steps - 1) - 1, 0)
    kv = ATTN_W // KV_W
    return (pl.BlockSpec((n * BLOCK, ATTN_W), lambda i: (cur(i), 0)),
            pl.BlockSpec((BLOCK, KV_W), lambda i: (prev(i), kv)), pl.BlockSpec((n * BLOCK, KV_W), lambda i: (cur(i), kv)),
            pl.BlockSpec((BLOCK, KV_W), lambda i: (prev(i), kv + 1)),
            pl.BlockSpec((n * BLOCK, KV_W), lambda i: (cur(i), kv + 1)))


def _attn_fwd(qkv, sinks, comm=None):
    s = qkv.shape[0]
    n = min(4, s // BLOCK)
    steps = s // (n * BLOCK)

    def body(sk_ref, bias0_ref, bias1_ref, q_ref, kp_ref, kc_ref, vp_ref, vc_ref, o_ref):
        kc, vc = kc_ref[...], vc_ref[...]
        for b in range(n):
            rows, before = slice(b * BLOCK, (b + 1) * BLOCK), slice((b - 1) * BLOCK, b * BLOCK)
            kp, vp = (kp_ref[...], vp_ref[...]) if b == 0 else (kc[before], vc[before])
            q, bias = q_ref[rows, :], (bias0_ref if b == 0 else bias1_ref)[...]
            outs = []
            for h in range(N_KV_HEADS):
                hs = slice(h * HEAD_DIM, (h + 1) * HEAD_DIM)
                k2 = jnp.concatenate([kp[:, hs], kc[rows, hs]], axis=0)
                v2 = jnp.concatenate([vp[:, hs], vc[rows, hs]], axis=0)
                sc = lax.dot_general(k2, _stack_heads(q, h), NT, preferred_element_type=F32) * ATTN_SCALE + bias
                sink = _sink_row(sk_ref, h)
                m = jnp.maximum(jnp.max(sc, axis=0, keepdims=True), sink)
                p = jnp.exp(sc - m)
                den = jnp.sum(p, axis=0, keepdims=True) + jnp.exp(sink - m)
                out = lax.dot_general(v2, p.astype(BF16), TN, preferred_element_type=F32) / den
                outs.append(out.T)
            o_ref[rows, :] = _unstack_heads(outs).astype(BF16)

    return _call(
        comm, body, name="attn_fwd", grid=(steps,),
        in_specs=[pl.BlockSpec(memory_space=pltpu.SMEM), *_attn_bias_specs(), *_block_specs(n, steps)],
        out_specs=pl.BlockSpec((n * BLOCK, ATTN_W), lambda i: (i, 0)),
        out_shape=jax.ShapeDtypeStruct((s, ATTN_W), BF16),
        compiler_params=_params("parallel"),
    )(sinks, _attn_bias(), _attn_bias(), qkv, qkv, qkv, qkv, qkv)


def _mix_fwd(x, attn, c3, gates, conv_w, w_br, w_out, g2, tm, comm=None):
    s = x.shape[0]

    def body(x_ref, at_ref, c3_ref, gt_ref, cw_ref, wbr_ref, wo_ref, g_ref,
             conv_ref, a_ref, cv_ref, mg_ref, h1_ref, hn_ref, carry_ref):
        @pl.when(pl.program_id(0) == 0)
        def _():
            carry_ref[...] = jnp.zeros_like(carry_ref)

        c3v = c3_ref[...].astype(F32)
        cb, cc, cx = c3v[:, :CONV_W], c3v[:, CONV_W:2 * CONV_W], c3v[:, 2 * CONV_W:]
        z = cc * cx
        cz = _causal_conv(z, carry_ref[...], cw_ref[...])
        carry_ref[...] = z[tm - 8:tm]
        conv = (cb * cz).astype(BF16)
        conv_ref[...] = conv
        a = jnp.dot(at_ref[...], wbr_ref[:ATTN_W, :], preferred_element_type=F32)
        cv = jnp.dot(conv, wbr_ref[ATTN_W:, :], preferred_element_type=F32)
        a_ref[...] = a.astype(BF16)
        cv_ref[...] = cv.astype(BF16)
        gt = gt_ref[...].astype(F32)
        merged = (_sigmoid(gt[:, :D_MODEL]) * a + _sigmoid(gt[:, D_MODEL:]) * cv).astype(BF16)
        mg_ref[...] = merged
        h1 = x_ref[...] + jnp.dot(merged, wo_ref[...], preferred_element_type=F32)
        h1_ref[...] = h1
        hn_ref[...] = (h1 * _rstd(h1) * g_ref[...]).astype(BF16)

    row = lambda w: pl.BlockSpec((tm, w), lambda i: (i, 0))
    return _call(
        comm, body, name="mix_fwd", grid=(s // tm,),
        in_specs=[row(D_MODEL), row(ATTN_W), row(C3_W), row(GATES_W), _resident((3, CONV_W)),
                  _resident((ATTN_W + CONV_W, D_MODEL)), _resident((D_MODEL, D_MODEL)), _resident((1, D_MODEL))],
        out_specs=[row(CONV_W), row(D_MODEL), row(D_MODEL), row(D_MODEL), row(D_MODEL), row(D_MODEL)],
        out_shape=[jax.ShapeDtypeStruct((s, CONV_W), BF16), jax.ShapeDtypeStruct((s, D_MODEL), BF16),
                   jax.ShapeDtypeStruct((s, D_MODEL), BF16), jax.ShapeDtypeStruct((s, D_MODEL), BF16),
                   jax.ShapeDtypeStruct((s, D_MODEL), F32), jax.ShapeDtypeStruct((s, D_MODEL), BF16)],
        scratch_shapes=[pltpu.VMEM((8, CONV_W), F32)],
        compiler_params=_params("arbitrary"),
    )(x, attn, c3, gates, conv_w, w_br, w_out, g2)


def _ffn_fwd_loss(hn, h1, w_up, ffn_cw, w_down, g3, target, tm):
    s = hn.shape[0]

    def body(hn_ref, h1_ref, wu_ref, cw_ref, wd_ref, g_ref, t_ref,
             u_ref, up_ref, act_ref, dh2_ref, loss_ref, gfn_ref, carry_ref):
        @pl.when(pl.program_id(0) == 0)
        def _():
            carry_ref[...] = jnp.zeros_like(carry_ref)
            loss_ref[...] = jnp.zeros_like(loss_ref)
            gfn_ref[...] = jnp.zeros_like(gfn_ref)

        u = jnp.dot(hn_ref[...], wu_ref[...], preferred_element_type=F32)
        u_ref[...] = u.astype(BF16)
        up = _causal_conv(u, carry_ref[...], cw_ref[...])
        up_ref[...] = up
        carry_ref[...] = u[tm - 8:tm]
        gate, val = up[:, :D_FF], up[:, D_FF:]
        act = (gate * _sigmoid(gate) * val).astype(BF16)
        act_ref[...] = act
        h2 = h1_ref[...] + jnp.dot(act, wd_ref[...], preferred_element_type=F32)
        rstd = _rstd(h2)
        g = g_ref[...]
        err = h2 * rstd * g - t_ref[...]
        loss_ref[...] += jnp.sum(err * err) * (0.5 / D_MODEL)
        dh2, dg = _rms_bwd(err * (1.0 / D_MODEL), h2, rstd, g)
        dh2_ref[...] = dh2
        gfn_ref[...] += jnp.sum(dg, axis=0, keepdims=True)

    row = lambda w: pl.BlockSpec((tm, w), lambda i: (i, 0))
    acc = lambda w: pl.BlockSpec((1, w), lambda i: (0, 0))
    return pl.pallas_call(
        body, name="ffn_fwd_loss", grid=(s // tm,),
        in_specs=[row(D_MODEL), row(D_MODEL), _resident((D_MODEL, FF2)), _resident((3, FF2)),
                  _resident((D_FF, D_MODEL)), _resident((1, D_MODEL)), row(D_MODEL)],
        out_specs=[row(FF2), row(FF2), row(D_FF), row(D_MODEL), acc(128), acc(D_MODEL)],
        out_shape=[jax.ShapeDtypeStruct((s, FF2), BF16), jax.ShapeDtypeStruct((s, FF2), F32),
                   jax.ShapeDtypeStruct((s, D_FF), BF16),
                   jax.ShapeDtypeStruct((s, D_MODEL), F32), jax.ShapeDtypeStruct((1, 128), F32),
                   jax.ShapeDtypeStruct((1, D_MODEL), F32)],
        scratch_shapes=[pltpu.VMEM((8, FF2), F32)],
        compiler_params=_params("arbitrary"),
    )(hn, h1, w_up, ffn_cw, w_down, g3, target)


def _ffn_bwd(dh2, u, up, h1, w_up, ffn_cw, w_down, g2, tm):
    s = dh2.shape[0]
    nt = s // tm

    def body(dh2_ref, u_ref, up_ref, h1_ref, wu_ref, cw_ref, wd_ref, g_ref,
             du_ref, dh1_ref, gcw_ref, gg_ref, carry_ref):
        @pl.when(pl.program_id(0) == 0)
        def _():
            for ref in (carry_ref, gcw_ref, gg_ref):
                ref[...] = jnp.zeros_like(ref)

        dh2v = dh2_ref[...]
        dact = lax.dot_general(dh2v.astype(BF16), wd_ref[...], NT, preferred_element_type=F32)
        upv = up_ref[...]
        gate, val = upv[:, :D_FF], upv[:, D_FF:]
        sg = _sigmoid(gate)
        dval = dact * (gate * sg)
        dgate = dact * val * (sg * (1.0 + gate * (1.0 - sg)))
        dup = jnp.concatenate([dgate, dval], axis=1)
        dup1, dup2 = _rows_after(dup, carry_ref[...])
        carry_ref[...] = dup[0:8]
        u = u_ref[...].astype(F32)
        gcw_ref[2:3, :] += jnp.sum(dup * u, axis=0, keepdims=True)
        gcw_ref[1:2, :] += jnp.sum(dup1 * u, axis=0, keepdims=True)
        gcw_ref[0:1, :] += jnp.sum(dup2 * u, axis=0, keepdims=True)
        cw = cw_ref[...]
        du = (cw[2:3] * dup + cw[1:2] * dup1 + cw[0:1] * dup2).astype(BF16)
        du_ref[...] = du
        dhn = lax.dot_general(du, wu_ref[...], NT, preferred_element_type=F32)
        h1v = h1_ref[...]
        dh1, dg = _rms_bwd(dhn, h1v, _rstd(h1v), g_ref[...])
        dh1_ref[...] = dh2v + dh1
        gg_ref[...] += jnp.sum(dg, axis=0, keepdims=True)

    row = lambda w: pl.BlockSpec((tm, w), lambda i: (nt - 1 - i, 0))
    return pl.pallas_call(
        body, name="ffn_bwd", grid=(nt,),
        in_specs=[row(D_MODEL), row(FF2), row(FF2),
                  row(D_MODEL), _resident((D_MODEL, FF2)), _resident((3, FF2)), _resident((D_FF, D_MODEL)),
                  _resident((1, D_MODEL))],
        out_specs=[row(FF2), row(D_MODEL), pl.BlockSpec((3, FF2), lambda i: (0, 0)),
                   pl.BlockSpec((1, D_MODEL), lambda i: (0, 0))],
        out_shape=[jax.ShapeDtypeStruct((s, FF2), BF16), jax.ShapeDtypeStruct((s, D_MODEL), F32),
                   jax.ShapeDtypeStruct((3, FF2), F32), jax.ShapeDtypeStruct((1, D_MODEL), F32)],
        scratch_shapes=[pltpu.VMEM((8, FF2), F32)],
        compiler_params=_params("arbitrary"),
    )(dh2, u, up, h1, w_up, ffn_cw, w_down, g2)


def _mix_bwd(dh1, gates, a, cv, c3, attn, conv, merged, conv_w, w_br, w_out, tm, comm=None):
    s = dh1.shape[0]
    nt = s // tm
    halo = 16

    def body(dh1_ref, gt_ref, a_ref, cv_ref, c3_ref, ch_ref, at_ref, cn_ref, mg_ref, cw_ref, wbr_ref,
             wo_ref, dat_ref, dc3_ref, dgt_ref, gcw_ref, gbr_ref, gout_ref, carry_ref, br_acc, out_acc):
        i = pl.program_id(0)

        @pl.when(i == 0)
        def _():
            for ref in (carry_ref, gcw_ref, br_acc, out_acc):
                ref[...] = jnp.zeros_like(ref)

        dh1v = dh1_ref[...].astype(BF16)
        out_acc[...] += lax.dot_general(mg_ref[...], dh1v, TN, preferred_element_type=F32)
        dm = lax.dot_general(dh1v, wo_ref[...], NT, preferred_element_type=F32)
        gt = gt_ref[...].astype(F32)
        sa, sc = _sigmoid(gt[:, :D_MODEL]), _sigmoid(gt[:, D_MODEL:])
        da = (dm * sa).astype(BF16)
        dcv = (dm * sc).astype(BF16)
        br_acc[:ATTN_W, :] += lax.dot_general(at_ref[...], da, TN, preferred_element_type=F32)
        br_acc[ATTN_W:, :] += lax.dot_general(cn_ref[...], dcv, TN, preferred_element_type=F32)
        dgt_ref[...] = jnp.concatenate(
            [dm * a_ref[...].astype(F32) * (sa * (1.0 - sa)), dm * cv_ref[...].astype(F32) * (sc * (1.0 - sc))],
            axis=1).astype(BF16)
        dat_ref[...] = lax.dot_general(da, wbr_ref[:ATTN_W, :], NT, preferred_element_type=F32).astype(BF16)
        dconv = lax.dot_general(dcv, wbr_ref[ATTN_W:, :], NT, preferred_element_type=F32)
        c3v = c3_ref[...].astype(F32)
        cb, cc, cx = c3v[:, :CONV_W], c3v[:, CONV_W:2 * CONV_W], c3v[:, 2 * CONV_W:]
        z = cc * cx
        chv = ch_ref[...].astype(F32)[halo - 8:halo] * (i < nt - 1).astype(F32)
        zh = chv[:, CONV_W:2 * CONV_W] * chv[:, 2 * CONV_W:]
        cw = cw_ref[...]
        cz = _causal_conv(z, zh, cw)
        dcz = dconv * cb
        dcz1, dcz2 = _rows_after(dcz, carry_ref[...])
        carry_ref[...] = dcz[0:8]
        gcw_ref[2:3, :] += jnp.sum(dcz * z, axis=0, keepdims=True)
        gcw_ref[1:2, :] += jnp.sum(dcz1 * z, axis=0, keepdims=True)
        gcw_ref[0:1, :] += jnp.sum(dcz2 * z, axis=0, keepdims=True)
        dz = cw[2:3] * dcz + cw[1:2] * dcz1 + cw[0:1] * dcz2
        dc3_ref[...] = jnp.concatenate([dconv * cz, dz * cx, dz * cc], axis=1).astype(BF16)

        @pl.when(i == nt - 1)
        def _():
            gbr_ref[...] = br_acc[...].astype(BF16)
            gout_ref[...] = out_acc[...].astype(BF16)

    row = lambda w: pl.BlockSpec((tm, w), lambda i: (nt - 1 - i, 0))
    return _call(
        comm, body, name="mix_bwd", grid=(nt,),
        in_specs=[row(D_MODEL), row(GATES_W), row(D_MODEL), row(D_MODEL), row(C3_W),
                  pl.BlockSpec((halo, C3_W), lambda i: (jnp.maximum((nt - 1 - i) * (tm // halo) - 1, 0), 0)),
                  row(ATTN_W), row(CONV_W), row(D_MODEL), _resident((3, CONV_W)),
                  _resident((ATTN_W + CONV_W, D_MODEL)), _resident((D_MODEL, D_MODEL))],
        out_specs=[row(ATTN_W), row(C3_W), row(GATES_W), pl.BlockSpec((3, CONV_W), lambda i: (0, 0)),
                   _resident((ATTN_W + CONV_W, D_MODEL)), _resident((D_MODEL, D_MODEL))],
        out_shape=[jax.ShapeDtypeStruct((s, ATTN_W), BF16), jax.ShapeDtypeStruct((s, C3_W), BF16),
                   jax.ShapeDtypeStruct((s, GATES_W), BF16), jax.ShapeDtypeStruct((3, CONV_W), F32),
                   jax.ShapeDtypeStruct((ATTN_W + CONV_W, D_MODEL), BF16),
                   jax.ShapeDtypeStruct((D_MODEL, D_MODEL), BF16)],
        scratch_shapes=[pltpu.VMEM((8, CONV_W), F32), pltpu.VMEM((ATTN_W + CONV_W, D_MODEL), F32),
                        pltpu.VMEM((D_MODEL, D_MODEL), F32)],
        compiler_params=_params("arbitrary"),
    )(dh1, gates, a, cv, c3, c3, attn, conv, merged, conv_w, w_br, w_out)


def _attn_bwd(qkv, sinks, o, do, comm=None):
    s = qkv.shape[0]
    npair = s // (2 * BLOCK)

    def one_block(sk_ref, bias, q, kp, kc, vp, vc, ov, dov, dsk_ref):
        dqs, dks, dvs = [], [], []
        for h in range(N_KV_HEADS):
            hs = slice(h * HEAD_DIM, (h + 1) * HEAD_DIM)
            k2 = jnp.concatenate([kp[:, hs], kc[:, hs]], axis=0)
            v2 = jnp.concatenate([vp[:, hs], vc[:, hs]], axis=0)
            qg, og, dog = _stack_heads(q, h), _stack_heads(ov, h), _stack_heads(dov, h)
            sc = lax.dot_general(k2, qg, NT, preferred_element_type=F32) * ATTN_SCALE + bias
            sink = _sink_row(sk_ref, h)
            m = jnp.maximum(jnp.max(sc, axis=0, keepdims=True), sink)
            p = jnp.exp(sc - m)
            psink = jnp.exp(sink - m)
            inv = 1.0 / (jnp.sum(p, axis=0, keepdims=True) + psink)
            p = p * inv
            delta = jnp.sum(dog.astype(F32) * og.astype(F32), axis=1, keepdims=True).T
            dp = lax.dot_general(v2, dog, NT, preferred_element_type=F32)
            ds = (p * (dp - delta)).astype(BF16)
            dqs.append((lax.dot_general(k2, ds, TN, preferred_element_type=F32) * ATTN_SCALE).T)
            dks.append(jnp.dot(ds, qg, preferred_element_type=F32) * ATTN_SCALE)
            dvs.append(jnp.dot(p.astype(BF16), dog, preferred_element_type=F32))
            dsink = -(psink * inv * delta)
            for g in range(GROUP):
                r = h * GROUP + g
                dsk_ref[r:r + 1, :] += jnp.sum(dsink[:, g * BLOCK:(g + 1) * BLOCK])
        return _unstack_heads(dqs), jnp.concatenate(dks, axis=1), jnp.concatenate(dvs, axis=1)

    def body(sk_ref, bias0_ref, bias1_ref, q_ref, kp_ref, kc_ref, vp_ref, vc_ref, o_ref, do_ref,
             dq_ref, dke_ref, dko_ref, dve_ref, dvo_ref, dsk_ref, ck_ref, cvv_ref):
        i = pl.program_id(0)

        @pl.when(i == 0)
        def _():
            for ref in (ck_ref, cvv_ref, dsk_ref):
                ref[...] = jnp.zeros_like(ref)

        @pl.when(i < npair)
        def _():
            kc, vc = kc_ref[...], vc_ref[...]
            first, second = slice(0, BLOCK), slice(BLOCK, 2 * BLOCK)
            dq0, dk0, dv0 = one_block(sk_ref, bias0_ref[...], q_ref[first, :], kp_ref[...], kc[first], vp_ref[...],
                                      vc[first], o_ref[first, :], do_ref[first, :], dsk_ref)
            dq1, dk1, dv1 = one_block(sk_ref, bias1_ref[...], q_ref[second, :], kc[first], kc[second], vc[first],
                                      vc[second], o_ref[second, :], do_ref[second, :], dsk_ref)
            dq_ref[first, :] = dq0.astype(BF16)
            dq_ref[second, :] = dq1.astype(BF16)
            dko_ref[...] = (ck_ref[...] + dk0[:BLOCK]).astype(BF16)
            dvo_ref[...] = (cvv_ref[...] + dv0[:BLOCK]).astype(BF16)
            dke_ref[...] = (dk0[BLOCK:] + dk1[:BLOCK]).astype(BF16)
            dve_ref[...] = (dv0[BLOCK:] + dv1[:BLOCK]).astype(BF16)
            ck_ref[...] = dk1[BLOCK:]
            cvv_ref[...] = dv1[BLOCK:]

        @pl.when(i == npair)
        def _():
            dko_ref[...] = ck_ref[...].astype(BF16)
            dvo_ref[...] = cvv_ref[...].astype(BF16)

    cur = lambda i: jnp.minimum(i, npair - 1)
    done = lambda i: jnp.maximum(i - 1, 0)
    rows = pl.BlockSpec((2 * BLOCK, ATTN_W), lambda i: (cur(i), 0))
    even = pl.BlockSpec((BLOCK, KV_W), lambda i: (cur(i), 0))
    odd = pl.BlockSpec((BLOCK, KV_W), lambda i: (done(i), 0))
    half = jax.ShapeDtypeStruct((s // 2, KV_W), BF16)
    return _call(
        comm, body, name="attn_bwd", grid=(npair + 1,),
        in_specs=[pl.BlockSpec(memory_space=pltpu.SMEM), *_attn_bias_specs(), *_block_specs(2, npair), rows, rows],
        out_specs=[rows, even, odd, even, odd, pl.BlockSpec((N_HEADS, 128), lambda i: (0, 0))],
        out_shape=[jax.ShapeDtypeStruct((s, ATTN_W), BF16), half, half, half, half,
                   jax.ShapeDtypeStruct((N_HEADS, 128), F32)],
        scratch_shapes=[pltpu.VMEM((BLOCK, KV_W), F32), pltpu.VMEM((BLOCK, KV_W), F32)],
        compiler_params=_params("arbitrary"),
    )(sinks, _attn_bias(), _attn_bias(), qkv, qkv, qkv, qkv, qkv, o, do)


def _inproj_bwd(dq, dk, dv, dc3, dgt, w_in, x, xn, dh1, g1):
    s = x.shape[0]
    tm = min(2 * BLOCK, s)
    nt = s // tm

    def body(dq_ref, dke_ref, dko_ref, dve_ref, dvo_ref, dc3_ref, dgt_ref, w_ref, x_ref, xn_ref, dh1_ref, g_ref,
             dx_ref, gw_ref, gb_ref, gg_ref, acc_ref):
        i = pl.program_id(0)

        @pl.when(i == 0)
        def _():
            for ref in (gb_ref, gg_ref, acc_ref):
                ref[...] = jnp.zeros_like(ref)

        dk = jnp.concatenate([dke_ref[...], dko_ref[...]], axis=0)
        dv = jnp.concatenate([dve_ref[...], dvo_ref[...]], axis=0)
        dp = jnp.concatenate([dq_ref[...], dk, dv, dc3_ref[...], dgt_ref[...]], axis=1)
        acc_ref[...] += lax.dot_general(dp, xn_ref[...], TN, preferred_element_type=F32)
        gb_ref[...] += jnp.sum(dp.astype(F32), axis=0, keepdims=True)
        dxn = jnp.dot(dp, w_ref[...], preferred_element_type=F32)
        xf = x_ref[...]
        dx, dg = _rms_bwd(dxn, xf, _rstd(xf), g_ref[...])
        dx_ref[...] = dh1_ref[...] + dx
        gg_ref[...] += jnp.sum(dg, axis=0, keepdims=True)

        @pl.when(i == nt - 1)
        def _():
            gw_ref[...] = acc_ref[...].astype(BF16)

    row = lambda w: pl.BlockSpec((tm, w), lambda i: (i, 0))
    acc = lambda w: pl.BlockSpec((1, w), lambda i: (0, 0))
    block = pl.BlockSpec((tm // 2, KV_W), lambda i: (i, 0))
    return pl.pallas_call(
        body, name="inproj_bwd", grid=(nt,),
        in_specs=[row(ATTN_W), block, block, block, block, row(C3_W), row(GATES_W), _resident((IN_W, D_MODEL)),
                  row(D_MODEL), row(D_MODEL), row(D_MODEL), _resident((1, D_MODEL))],
        out_specs=[row(D_MODEL), _resident((IN_W, D_MODEL)), acc(IN_W), acc(D_MODEL)],
        out_shape=[jax.ShapeDtypeStruct((s, D_MODEL), F32), jax.ShapeDtypeStruct((IN_W, D_MODEL), BF16),
                   jax.ShapeDtypeStruct((1, IN_W), F32), jax.ShapeDtypeStruct((1, D_MODEL), F32)],
        scratch_shapes=[pltpu.VMEM((IN_W, D_MODEL), F32)],
        compiler_params=_params("arbitrary"),
    )(dq, *dk, *dv, dc3, dgt, w_in, x, xn, dh1, g1)


def _wgrad(a, b, bm, bn, bk, name, comm=None):
    s, m = a.shape
    n = b.shape[1]
    nk = s // bk

    def body(a_ref, b_ref, o_ref, acc_ref):
        k = pl.program_id(2)

        @pl.when(k == 0)
        def _():
            acc_ref[...] = jnp.zeros_like(acc_ref)

        acc_ref[...] += lax.dot_general(a_ref[...].astype(BF16), b_ref[...].astype(BF16), TN,
                                        preferred_element_type=F32)

        @pl.when(k == nk - 1)
        def _():
            o_ref[...] = acc_ref[...].astype(BF16)

    return _call(
        comm, body, name=name, grid=(m // bm, n // bn, nk),
        in_specs=[pl.BlockSpec((bk, bm), lambda i, j, k: (k, i)), pl.BlockSpec((bk, bn), lambda i, j, k: (k, j))],
        out_specs=pl.BlockSpec((bm, bn), lambda i, j, k: (i, j)),
        out_shape=jax.ShapeDtypeStruct((m, n), BF16),
        scratch_shapes=[pltpu.VMEM((bm, bn), F32)],
        compiler_params=_params("parallel", "parallel", "arbitrary"),
    )(a, b)


class _Carry:
    def __init__(self, jobs, reads=None, bufs=None, fresh=None):
        self.jobs, self.reads, self.bufs, self.fresh = jobs, reads or {}, bufs or {}, fresh or {}
        self.out = {}


class _Job:
    def __init__(self, n_sems, plan):
        self.n_sems, self.plan = n_sems, plan


def _plan_all(jobs, hbm, send, recv):
    pos = _position()
    starts, waits, base = [], [], 0
    for job in jobs:
        s, w = job.plan(hbm, pos, send, recv, base)
        starts, waits, base = starts + s, waits + w, base + job.n_sems
    return starts, waits


def _call(comm, body, **kw):
    if comm is None:
        return pl.pallas_call(body, **kw)
    grid = kw["grid"]
    single = not isinstance(kw["out_shape"], (list, tuple))
    out_shape = [kw["out_shape"]] if single else list(kw["out_shape"])
    out_specs = [kw["out_specs"]] if single else list(kw["out_specs"])
    in_specs = list(kw["in_specs"])
    scratch = list(kw.get("scratch_shapes", ()))
    r_names, b_names, f_names = list(comm.reads), list(comm.bufs), list(comm.fresh)
    n_args, n_out, n_scr = len(in_specs), len(out_shape), len(scratch)
    n_sems = sum(j.n_sems for j in comm.jobs)

    def wrapped(*refs):
        k = n_args
        hbm = dict(zip(r_names, refs[k:k + len(r_names)]))
        k += len(r_names) + len(b_names)
        outs = refs[k:k + n_out]
        k += n_out
        hbm.update(zip(b_names + f_names, refs[k:k + len(b_names) + len(f_names)]))
        k += len(b_names) + len(f_names)
        send, recv = refs[k + n_scr:]
        starts, waits = _plan_all(comm.jobs, hbm, send, recv)
        ids = [pl.program_id(a) for a in range(len(grid))]
        first = functools.reduce(jnp.logical_and, [i == 0 for i in ids])
        last = functools.reduce(jnp.logical_and, [i == g - 1 for i, g in zip(ids, grid)])

        @pl.when(first)
        def _():
            for cp in starts:
                cp.start()

        body(*refs[:n_args], *outs, *refs[k:k + n_scr])

        @pl.when(last)
        def _():
            for cp in waits:
                cp.wait_recv()
            for cp in starts:
                cp.wait_send()

    sems = pltpu.SemaphoreType.DMA((n_sems,))
    held = [jax.ShapeDtypeStruct(a.shape, a.dtype) for a in comm.bufs.values()] + list(comm.fresh.values())
    call = pl.pallas_call(
        wrapped, name=kw["name"], grid=grid,
        in_specs=in_specs + [_ANY] * (len(r_names) + len(b_names)),
        out_specs=out_specs + [_ANY] * len(held),
        out_shape=out_shape + held,
        input_output_aliases={n_args + len(r_names) + i: n_out + i for i in range(len(b_names))},
        scratch_shapes=scratch + [sems, sems],
        compiler_params=_params(*["arbitrary"] * len(grid)),
    )

    def run(*args):
        res = call(*args, *comm.reads.values(), *comm.bufs.values())
        comm.out = dict(zip(b_names + f_names, res[n_out:]))
        return res[0] if single else res[:n_out]

    return run


def _exchange(name, phases, reads=None, bufs=None, fresh=None):
    comm = _Carry([j for ph in phases for j in ph], reads, bufs, fresh)
    r_names, b_names, f_names = list(comm.reads), list(comm.bufs), list(comm.fresh)
    n_sems = sum(j.n_sems for j in comm.jobs)

    def body(*refs):
        hbm = dict(zip(r_names, refs[:len(r_names)]))
        k = len(r_names) + len(b_names)
        hbm.update(zip(b_names + f_names, refs[k:k + len(b_names) + len(f_names)]))
        send, recv = refs[-2:]
        pos = _position()
        started, base = [], 0
        for ph in phases:
            waits = []
            for job in ph:
                s, w = job.plan(hbm, pos, send, recv, base)
                base += job.n_sems
                for cp in s:
                    cp.start()
                started, waits = started + s, waits + w
            for cp in waits:
                cp.wait_recv()
        for cp in started:
            cp.wait_send()

    sems = pltpu.SemaphoreType.DMA((n_sems,))
    held = [jax.ShapeDtypeStruct(a.shape, a.dtype) for a in comm.bufs.values()] + list(comm.fresh.values())
    res = pl.pallas_call(
        body, name=name, in_specs=[_ANY] * (len(r_names) + len(b_names)), out_specs=[_ANY] * len(held),
        out_shape=held, input_output_aliases={len(r_names) + i: i for i in range(len(b_names))},
        scratch_shapes=[sems, sems],
    )(*comm.reads.values(), *comm.bufs.values())
    return dict(zip(b_names + f_names, res))


_HBM = pl.BlockSpec(memory_space=pltpu.HBM)
_SEM = pl.BlockSpec(memory_space=pltpu.SEMAPHORE)
_EFFECT = pltpu.SideEffectType.DATAFLOW_SIDE_EFFECTING


def _start_exchanges(name, groups):
    names = [list(arrays) for _, arrays in groups]
    first = [sum(len(ns) for ns in names[:g]) for g in range(len(groups))]
    n, ng = sum(len(ns) for ns in names), len(groups)

    def body(*refs):
        for g, (jobs, _) in enumerate(groups):
            hbm = dict(zip(names[g], refs[first[g]:first[g] + len(names[g])]))
            for cp in _plan_all(jobs, hbm, refs[n + 2 * g], refs[n + 2 * g + 1])[0]:
                cp.start()
        refs[-1][...] = jnp.zeros_like(refs[-1])

    given = [pltpu.with_memory_space_constraint(
        a if isinstance(a, jax.Array) else lax.empty(a.shape, a.dtype), pltpu.HBM)
        for _, arrays in groups for a in arrays.values()]
    sems = [pltpu.SemaphoreType.DMA((sum(j.n_sems for j in jobs),)) for jobs, _ in groups for _ in range(2)]
    res = pl.pallas_call(
        body, name=name,
        out_shape=(*sems, *[pltpu.HBM(a.shape, a.dtype) for a in given], jax.ShapeDtypeStruct((8, 128), F32)),
        in_specs=[_HBM] * n, out_specs=(*[_SEM] * (2 * ng), *[_HBM] * n, pl.BlockSpec(memory_space=pltpu.VMEM)),
        input_output_aliases={i: 2 * ng + i for i in range(n)},
        compiler_params=pltpu.CompilerParams(has_side_effects=_EFFECT),
    )(*given)
    held = res[2 * ng:2 * ng + n]
    states = [(names[g], groups[g][0], res[2 * g], res[2 * g + 1], held[first[g]:first[g] + len(names[g])])
              for g in range(ng)]
    return states, res[-1]


def _start_exchange(name, jobs, arrays):
    states, token = _start_exchanges(name, [(jobs, arrays)])
    return states[0], token


def _finish_exchange(name, state, after):
    names, jobs, send_sem, recv_sem, held = state
    n = len(names)

    def body(*refs):
        hbm = dict(zip(names, refs[:n]))
        send, recv = refs[n:n + 2]
        starts, waits = _plan_all(jobs, hbm, send, recv)
        for cp in waits:
            cp.wait_recv()
        for cp in starts:
            cp.wait_send()

    res = pl.pallas_call(
        body, name=name, out_shape=tuple(pltpu.HBM(a.shape, a.dtype) for a in held),
        in_specs=[_HBM] * n + [_SEM, _SEM, _ANY], out_specs=tuple([_HBM] * n),
        input_output_aliases={i: i for i in range(n)},
        compiler_params=pltpu.CompilerParams(has_side_effects=_EFFECT),
    )(*held, send_sem, recv_sem, after)
    return dict(zip(names, res))


def _row_tile(rows, bytes_per_row):
    best = 16
    for t in range(16, rows + 1, 16):
        if rows % t == 0 and t * bytes_per_row <= 9 * 1024 * 1024:
            best = t
    return best


def _rowwise(fn, ins, out_dtypes, name, after=None):
    rows, cols = ins[0].shape
    per_row = sum(cols * a.dtype.itemsize for a in ins) + sum(cols * jnp.dtype(d).itemsize for d in out_dtypes)
    tr = _row_tile(rows, per_row)
    n_in = len(ins)

    def body(*refs):
        outs = fn(*[r[...] for r in refs[:n_in]])
        for o_ref, o in zip(refs[-len(out_dtypes):], outs):
            o_ref[...] = o.astype(o_ref.dtype)

    tile = pl.BlockSpec((tr, cols), lambda i: (i, 0))
    behind = [] if after is None else [after]
    return pl.pallas_call(
        body, name=name, grid=(rows // tr,),
        in_specs=[tile] * n_in + [pl.BlockSpec((8, 128), lambda i: (0, 0))] * len(behind),
        out_specs=[tile] * len(out_dtypes),
        out_shape=[jax.ShapeDtypeStruct((rows, cols), d) for d in out_dtypes],
        compiler_params=_params("parallel"),
    )(*ins, *behind)


def _tiled(fn, name, grid, pos, ins, outs):
    n_in = len(ins)

    def body(pos_ref, *refs):
        res = fn(*[r[...] for r in refs[:n_in]])
        for o_ref, o in zip(refs[n_in:], res):
            o_ref[...] = o.astype(o_ref.dtype)

    return pl.pallas_call(
        body, name=name,
        grid_spec=pltpu.PrefetchScalarGridSpec(
            num_scalar_prefetch=1, grid=grid,
            in_specs=[pl.BlockSpec(bs, im) for _, bs, im in ins],
            out_specs=[pl.BlockSpec(bs, im) for _, _, bs, im in outs]),
        out_shape=[jax.ShapeDtypeStruct(s, d) for s, d, _, _ in outs],
        compiler_params=_params("parallel"),
    )(pos, *[a for a, _, _ in ins])


def _adamw(w, g, m, v):
    m = ADAM_B1 * m + (1.0 - ADAM_B1) * g
    v = ADAM_B2 * v + (1.0 - ADAM_B2) * (g * g)
    m_hat = m / (1.0 - ADAM_B1 ** ADAM_STEP)
    v_hat = v / (1.0 - ADAM_B2 ** ADAM_STEP)
    return -ADAM_LR * (m_hat / (jnp.sqrt(v_hat) + ADAM_EPS) + ADAM_WD * w), m, v


def _adamw_small(pos, own, slots, params):
    n = len(params)

    def body(pos_ref, own_ref, slots_ref, *refs):
        ins, outs, total_ref = refs[:3 * n], refs[3 * n:-1], refs[-1]
        chip = pos_ref[0]
        idx = 2 * chip + pos_ref[1]
        term = lambda q: jnp.where(idx == q, own_ref[...], slots_ref[q])
        acc = term(0)
        for q in range(1, N_DEV):
            acc = acc + term(q)
        total_ref[...] = acc
        outs[0][...] = total_ref[0:1, :]
        for k, (w, _, _, row) in enumerate(params):
            width = min(w.shape[-1], 128)
            for t in range(w.shape[0]):
                for j in range(w.shape[-1] // width):
                    lanes = slice(j * width, (j + 1) * width)
                    at = (slice(t, t + 1), lanes) if w.ndim == 2 else (t, slice(None), lanes)
                    g = total_ref[pl.ds(row(t, j, chip), 1), :][:, :width]
                    new = _adamw(ins[3 * k][at], g, ins[3 * k + 1][at], ins[3 * k + 2][at])
                    for o_ref, o in zip(outs[1 + 4 * k:5 + 4 * k], (g, *new)):
                        o_ref[at] = o

    vmem = pl.BlockSpec(memory_space=pltpu.VMEM)
    return pl.pallas_call(
        body, name="adamw_small",
        in_specs=[pl.BlockSpec(memory_space=pltpu.SMEM)] + [vmem] * (2 + 3 * n),
        out_shape=[jax.ShapeDtypeStruct((1, 128), F32)]
        + [jax.ShapeDtypeStruct(p[0].shape, F32) for p in params for _ in range(4)],
        scratch_shapes=[pltpu.VMEM(own.shape, F32)],
    )(pos, own, slots, *[a for p in params for a in p[:3]])


class _Layout:
    def __init__(self, rows, cols, stacked):
        self.rows, self.cols, self.stacked = rows, cols, stacked

    def whole(self, rows=None):
        r = self.rows if rows is None else rows
        return (N_CHIPS, r, self.cols) if self.stacked else (r, N_CHIPS * self.cols)

    def part_rows(self, h, q=0, nq=1):
        n = self.rows // 2 // nq
        return pl.ds(pl.multiple_of(h * (self.rows // 2) + q * n, 16), n)

    def half_rows(self, h):
        return self.part_rows(h)

    def block(self, ref, p, rows=slice(None)):
        if self.stacked:
            return ref.at[p, rows, :]
        return ref.at[rows, pl.ds(pl.multiple_of(p * self.cols, 128), self.cols)]

    def all_chips(self, ref, rows):
        return ref.at[:, rows, :] if self.stacked else ref.at[rows, :]


BIG = (
    _Layout(IN_SHARD, D_MODEL, True),
    _Layout(ATTN_W + CONV_W, D_MODEL // N_CHIPS, False),
    _Layout(D_MODEL // N_CHIPS, D_MODEL, True),
    _Layout(D_MODEL, FF2 // N_CHIPS, False),
    _Layout(D_FF // N_CHIPS, D_MODEL, True),
)
N_BIG = len(BIG)
_ANY = pl.BlockSpec(memory_space=pl.ANY)


def _position():
    x, y, c = lax.axis_index("x"), lax.axis_index("y"), lax.axis_index("c")
    return x, y, c, 2 * x + y


def _core_of_chip(p, c):
    return (p >> 1, p & 1, c)


def _place_cast(shard, lay, pos, name, after=None):
    rows, cols = shard.shape
    tr = _row_tile(rows, cols * 6)
    if lay.stacked:
        out = (lay.whole(), BF16, (None, tr, cols), lambda i, pos: (pos[0], i, 0))
    else:
        out = (lay.whole(), BF16, (tr, cols), lambda i, pos: (i, pos[0]))
    ins = [(shard, (tr, cols), lambda i, pos: (i, 0))]
    if after is not None:
        ins.append((after, (8, 128), lambda i, pos: (0, 0)))
    return _tiled(lambda a, *_: (a,), name, (rows // tr,), pos, ins, [out])[0]


def _place_cast_pair(top, bottom, lay, pos, name, after=None):
    rows, cols = top.shape
    ins = [(top, (rows, cols), lambda i, pos: (0, 0)), (bottom, (rows, cols), lambda i, pos: (0, 0))]
    if after is not None:
        ins.append((after, (8, 128), lambda i, pos: (0, 0)))
    return _tiled(lambda a, b, *_: (jnp.concatenate([a, b], axis=0),), name, (1,), pos, ins,
                  [(lay.whole(), BF16, (2 * rows, cols), lambda i, pos: (0, pos[0]))])[0]


def _adamw_pair(top, bottom, g, after=None):
    rows = top[0].shape[0]

    def body(*refs):
        (wa, ma, va, wb, mb, vb, g_ref), outs = refs[:7], refs[-8:]
        for (w, m, v), gg, o in (((wa, ma, va), g_ref[:rows], outs[:4]), ((wb, mb, vb), g_ref[rows:], outs[4:])):
            for o_ref, val in zip(o, (gg, *_adamw(w[...], gg, m[...], v[...]))):
                o_ref[...] = val

    behind = [] if after is None else [after[0:8, 0:128]]
    res = pl.pallas_call(
        body, name="adamw_w_br", out_shape=[jax.ShapeDtypeStruct(top[0].shape, F32)] * 8,
    )(*top, *bottom, g, *behind)
    return res[:4], res[4:]


def _remote(src, dst, send, recv, k, device):
    return pltpu.make_async_remote_copy(src_ref=src, dst_ref=dst, send_sem=send.at[k], recv_sem=recv.at[k],
                                        device_id=device, device_id_type=MESH)


def _arrival(dst, send, recv, k, me):
    return _remote(dst, dst, send, recv, k, me)


def _gather_ici(lay, name, q=0, nq=1):
    def plan(hbm, pos, send, recv, base):
        x, y, c, me = pos
        rows = lay.part_rows(c, q, nq)
        mine = lay.block(hbm[name], me, rows)
        starts = [_remote(mine, mine, send, recv, base + d - 1, _core_of_chip(me ^ d, c)) for d in (1, 2, 3)]
        waits = [_arrival(lay.block(hbm[name], me ^ d, rows), send, recv, base + d - 1, (x, y, c)) for d in (1, 2, 3)]
        return starts, waits
    return _Job(3, plan)


def _gather_d2d(lay, name, q=0, nq=1):
    def plan(hbm, pos, send, recv, base):
        x, y, c, me = pos
        starts, waits = [], []
        for d in (1, 2, 3):
            got = lay.block(hbm[name], me ^ d, lay.part_rows(c, q, nq))
            starts.append(_remote(got, got, send, recv, base + d - 1, (x, y, 1 - c)))
            waits.append(_arrival(lay.block(hbm[name], me ^ d, lay.part_rows(1 - c, q, nq)), send, recv, base + d - 1,
                                  (x, y, c)))
        return starts, waits
    return _Job(3, plan)


def _rs_pair(lay, grad, theirs):
    def plan(hbm, pos, send, recv, base):
        x, y, c, _ = pos
        out = _remote(lay.all_chips(hbm[grad], lay.half_rows(1 - c)), hbm[theirs], send, recv, base, (x, y, 1 - c))
        return [out], [_arrival(hbm[theirs], send, recv, base, (x, y, c))]
    return _Job(1, plan)


def _rs_chips(lay, sums, slots):
    def plan(hbm, pos, send, recv, base):
        x, y, c, me = pos
        starts = [_remote(lay.block(hbm[sums], me ^ d), hbm[slots].at[me], send, recv, base + d - 1,
                          _core_of_chip(me ^ d, c)) for d in (1, 2, 3)]
        waits = [_arrival(hbm[slots].at[me ^ d], send, recv, base + d - 1, (x, y, c)) for d in (1, 2, 3)]
        return starts, waits
    return _Job(3, plan)


def _rs_share(lay, shard):
    def plan(hbm, pos, send, recv, base):
        x, y, c, _ = pos
        mine = hbm[shard].at[lay.half_rows(c), :]
        other = hbm[shard].at[lay.half_rows(1 - c), :]
        return [_remote(mine, mine, send, recv, base, (x, y, 1 - c))], [_arrival(other, send, recv, base, (x, y, c))]
    return _Job(1, plan)


def _slots_shape(lay):
    return jax.ShapeDtypeStruct((N_CHIPS, lay.rows // 2, lay.cols), BF16)


def _theirs_shape(lay, dtype=BF16):
    return jax.ShapeDtypeStruct(lay.whole(lay.rows // 2), dtype)


def _pair_sum(grad, theirs, lay, pos, name):
    half = lay.rows // 2
    add = lambda a, b: (a.astype(F32) + b.astype(F32),)
    if lay.stacked:
        tr = _row_tile(half, lay.cols * 6)
        nt = half // tr
        flat = lambda a: a.reshape(-1, lay.cols)
        mine = lambda t, pos: ((t // nt) * (2 * nt) + pos[1] * nt + t % nt, 0)
        grid, blk = (N_CHIPS * nt,), (tr, lay.cols)
        grad, theirs = flat(grad), flat(theirs)
    else:
        tr = _row_tile(half, N_CHIPS * lay.cols * 6)
        nt = half // tr
        mine = lambda t, pos: (pos[1] * nt + t, 0)
        grid, blk = (nt,), (tr, N_CHIPS * lay.cols)
    same = lambda t, pos: (t, 0)
    out = _tiled(add, name, grid, pos, [(grad, blk, mine), (theirs, blk, same)], [(theirs.shape, BF16, blk, same)])[0]
    return out.reshape(lay.whole(half))


def _chip_sum(sums, slots, lay, pos, name, after=None):
    half = lay.rows // 2
    tr = _row_tile(half, lay.cols * 12)
    nt = half // tr
    blk3 = (None, tr, lay.cols)
    if lay.stacked:
        own = (sums, blk3, lambda i, pos: (pos[0], i, 0))
    else:
        own = (sums, (tr, lay.cols), lambda i, pos: (i, pos[0]))
    others = [(slots, blk3, functools.partial(lambda d, i, pos: (pos[0] ^ d, i, 0), d)) for d in (1, 2, 3)]

    def add(a, b1, b2, b3, *_):
        return (((a.astype(F32) + b1.astype(F32)) + b2.astype(F32)) + b3.astype(F32),)

    if after is not None:
        others.append((after, (8, 128), lambda i, pos: (0, 0)))
    return _tiled(add, name, (nt,), pos, [own] + others,
                  [((lay.rows, lay.cols), F32, (tr, lay.cols), lambda i, pos: (pos[1] * nt + i, 0))])[0]


N_DEV = 8


def _to_all(src, slots):
    def plan(hbm, pos, send, recv, base):
        x, y, c, _ = pos
        idx = 4 * x + 2 * y + c
        starts = [_remote(hbm[src], hbm[slots].at[idx], send, recv, base + k - 1,
                          (x ^ (k >> 2), y ^ ((k >> 1) & 1), c ^ (k & 1))) for k in range(1, N_DEV)]
        waits = [_arrival(hbm[slots].at[idx ^ k], send, recv, base + k - 1, (x, y, c)) for k in range(1, N_DEV)]
        return starts, waits
    return _Job(N_DEV - 1, plan)


def _pack_rows(parts):
    padded = [jnp.pad(a, ((0, -a.shape[0] % 8), (0, 0))) for a in parts]
    starts = [sum(p.shape[0] for p in padded[:k]) for k in range(len(padded))]
    return jnp.concatenate(padded, axis=0), starts


def kernel(x, mix_norm, w_in, b_in, sinks, conv_w, w_attn_branch, w_conv_branch, w_out, ffn_norm, w_up, ffn_conv_w, w_down, final_norm, loss_target, m_mix_norm, m_w_in, m_b_in, m_sinks, m_conv_w, m_w_attn_branch, m_w_conv_branch, m_w_out, m_ffn_norm, m_w_up, m_ffn_conv_w, m_w_down, m_final_norm, v_mix_norm, v_w_in, v_b_in, v_sinks, v_conv_w, v_w_attn_branch, v_w_conv_branch, v_w_out, v_ffn_norm, v_w_up, v_ffn_conv_w, v_w_down, v_final_norm):
    me = 2 * lax.axis_index("x") + lax.axis_index("y")
    names = ("w_in", "w_br", "w_out", "w_up", "w_down")
    w_of = dict(w_in=w_in[0].T, w_out=w_out[0], w_up=w_up[0], w_down=w_down[0])
    m_of = dict(w_in=m_w_in[0].T, w_out=m_w_out[0], w_up=m_w_up[0], w_down=m_w_down[0])
    v_of = dict(w_in=v_w_in[0].T, w_out=v_w_out[0], w_up=v_w_up[0], w_down=v_w_down[0])
    ab = (w_attn_branch[0], m_w_attn_branch[0], v_w_attn_branch[0])
    cb = (w_conv_branch[0], m_w_conv_branch[0], v_w_conv_branch[0])

    pos = jnp.stack([me, lax.axis_index("c")]).astype(jnp.int32)

    lay = dict(zip(names, BIG))
    xs, target, sk = x[0], loss_target[0], sinks[0]
    s = xs.shape[0]
    tm, tm2, bk, bk2 = min(256, s), min(512, s), min(1024, s), min(2048, s)

    taps, (_, t0) = _pack_rows([conv_w[0], ffn_conv_w[0].reshape(3 * (FF2 // N_CHIPS // 128), 128)])
    placed = {"w_in": _place_cast(w_of["w_in"], lay["w_in"], pos, "cast_w_in")}
    fly_in, started = _start_exchange("gather_in_start", [_gather_ici(lay["w_in"], "w_in")], {"w_in": placed["w_in"]})
    taps_flight, started = _start_exchange("taps_start", [_to_all("v", "slots")],
                                           {"v": taps + started[0:1], "slots": jnp.zeros((N_DEV, *taps.shape), F32)})
    placed["w_br"] = _place_cast_pair(ab[0], cb[0], lay["w_br"], pos, "cast_w_br", after=started)
    for n in names[2:]:
        placed[n] = _place_cast(w_of[n], lay[n], pos, "cast_" + n, after=started)
    trio = ("w_br", "w_out")
    (fly_trio, fly_up, fly_down), started = _start_exchanges("gather_rest_start", [
        ([_gather_ici(lay[n], n) for n in ws], {n: placed[n] for n in ws}) for ws in (trio, ("w_up",), ("w_down",))])

    got = _finish_exchange("gather_in_wait", fly_in, after=started)
    w_in_full = _exchange("gather_in_d2d", [[_gather_d2d(lay["w_in"], "w_in")]], bufs=got)["w_in"].reshape(IN_W, D_MODEL)
    xn, qkv, c3, gates = _inproj_fwd(xs, mix_norm, w_in_full, b_in, tm2)
    k2 = _Carry([_gather_d2d(lay[n], n) for n in trio], bufs=_finish_exchange("gather_trio_wait", fly_trio, after=qkv))
    attn = _attn_fwd(qkv, sk, comm=k2)
    w_br = k2.out["w_br"]
    w_out_full = k2.out["w_out"].reshape(D_MODEL, D_MODEL)
    k3 = _Carry([_gather_d2d(lay["w_up"], "w_up")], bufs=_finish_exchange("gather_up_wait", fly_up, after=attn))
    taps = _finish_exchange("taps_wait", taps_flight, after=attn)
    taps = lax.dynamic_update_slice(taps["slots"], taps["v"][None], (2 * me + lax.axis_index("c"), 0, 0))
    conv_full = taps[0::2, 0:3].transpose(1, 0, 2).reshape(3, CONV_W)
    ffn_cw_full = taps[0::2, t0:t0 + 33].reshape(N_CHIPS, 3, FF2 // N_CHIPS).transpose(1, 0, 2).reshape(3, FF2)
    conv, a, cv, merged, h1, hn = _mix_fwd(xs, attn, c3, gates, conv_full, w_br, w_out_full, ffn_norm, tm2, comm=k3)
    w_up_full = k3.out["w_up"]
    w_down_full = _exchange("gather_down_d2d", [[_gather_d2d(lay["w_down"], "w_down")]],
                            bufs=_finish_exchange("gather_down_wait", fly_down, after=hn))["w_down"].reshape(D_FF, D_MODEL)
    u, up, act, dh2, loss_part, g_fn = _ffn_fwd_loss(hn, h1, w_up_full, ffn_cw_full, w_down_full,
                                                     final_norm[None, :], target, tm)

    grads, sums, slots = {}, {}, {}

    def pair(*ws):
        return _Carry([_rs_pair(lay[n], "g_" + n, "t_" + n) for n in ws], reads={"g_" + n: grads[n] for n in ws},
                      fresh={"t_" + n: _theirs_shape(lay[n], grads[n].dtype) for n in ws})

    def chips(*ws, also=None):
        k = _Carry([_rs_chips(lay[n], "s_" + n, "r_" + n) for n in ws], reads={"s_" + n: sums[n] for n in ws},
                   fresh={"r_" + n: _slots_shape(lay[n]) for n in ws})
        if also is not None:
            k = _Carry(k.jobs + also.jobs, {**k.reads, **also.reads}, None, {**k.fresh, **also.fresh})
        return k

    def pair_sums(k, *ws):
        for n in ws:
            sums[n] = _pair_sum(grads[n], k.out["t_" + n], lay[n], pos, "pair_sum_" + n)

    def take_slots(k, *ws):
        for n in ws:
            slots[n] = k.out["r_" + n]

    du, dh1, g_fcw, g_g2 = _ffn_bwd(dh2, u, up, h1, w_up_full, ffn_cw_full, w_down_full, ffn_norm, tm)
    grads["w_down"] = _wgrad(act, dh2, D_FF // 2, D_MODEL, bk2, "wgrad_down").reshape(lay["w_down"].whole())
    k4 = pair("w_down")
    grads["w_up"] = _wgrad(hn, du, D_MODEL, FF2 // 4, bk2, "wgrad_up", comm=k4)
    pair_sums(k4, "w_down")
    k5 = chips("w_down", also=pair("w_up"))
    dattn, dc3, dgt, g_cw, grads["w_br"], gw_out = _mix_bwd(
        dh1, gates, a, cv, c3, attn, conv, merged, conv_full, w_br, w_out_full, tm2, comm=k5)
    grads["w_out"] = gw_out.reshape(lay["w_out"].whole())
    take_slots(k5, "w_down")
    pair_sums(k5, "w_up")
    up_flight, started = _start_exchange("rs_chips_up_start", [_rs_chips(lay["w_up"], "s", "r")],
                                         {"s": sums["w_up"], "r": _slots_shape(lay["w_up"])})
    k6 = pair(*trio)
    k6.reads["after"] = started
    dq, dk_even, dk_odd, dv_even, dv_odd, g_sk = _attn_bwd(qkv, sk, attn, dattn, comm=k6)
    pair_sums(k6, *trio)
    trio_flight, started = _start_exchange(
        "rs_chips_trio_start", [_rs_chips(lay[n], "s_" + n, "r_" + n) for n in trio],
        {**{"s_" + n: sums[n] for n in trio}, **{"r_" + n: _slots_shape(lay[n]) for n in trio}})
    behind = mix_norm + jnp.tile(started[0:1], (1, D_MODEL // 128))
    grad_x, gw_in, g_b, g_g1 = _inproj_bwd(dq, (dk_even, dk_odd), (dv_even, dv_odd), dc3, dgt, w_in_full, xs, xn,
                                           dh1, behind)
    grads["w_in"] = gw_in.reshape(lay["w_in"].whole())

    in_flight, started = _start_exchange("rs_pair_in_start", [_rs_pair(lay["w_in"], "g", "t")],
                                         {"g": grads["w_in"], "t": _theirs_shape(lay["w_in"])})
    parts = [loss_part, g_g1, g_b, jnp.pad(g_sk[:, 0], (0, 120))[None, :], g_cw, g_g2, g_fcw, g_fn]
    packed, at = _pack_rows([p.reshape(-1, 128) for p in parts])
    small_flight, started = _start_exchange("small_start", [_to_all("v", "slots")],
                                            {"v": packed + started[0:1], "slots": jnp.zeros((N_DEV, *packed.shape), F32)})
    halves = {"w_down": _chip_sum(sums["w_down"], slots["w_down"], lay["w_down"], pos, "chip_sum_w_down", after=started)}
    landed = _finish_exchange("rs_chips_up_wait", up_flight, after=halves["w_down"])
    halves["w_up"] = _chip_sum(landed["s"], landed["r"], lay["w_up"], pos, "chip_sum_w_up")
    landed = _finish_exchange("rs_pair_in_wait", in_flight, after=halves["w_up"])
    sums["w_in"] = _pair_sum(landed["g"], landed["t"], lay["w_in"], pos, "pair_sum_w_in")
    (in_flight, down_flight, up_flight), started = _start_exchanges("rs_chips_in_start", [
        ([_rs_chips(lay["w_in"], "s", "r")], {"s": sums["w_in"], "r": _slots_shape(lay["w_in"])}),
        ([_rs_share(lay["w_down"], "w_down")], {"w_down": halves["w_down"]}),
        ([_rs_share(lay["w_up"], "w_up")], {"w_up": halves["w_up"]})])
    landed = _finish_exchange("rs_chips_trio_wait", trio_flight, after=started)
    for n in trio:
        halves[n] = _chip_sum(landed["s_" + n], landed["r_" + n], lay[n], pos, "chip_sum_" + n)
    shared = _exchange("share_halves", [[_rs_share(lay[n], n) for n in trio]], bufs={n: halves[n] for n in trio})
    shared["w_down"] = _finish_exchange("share_down_wait", down_flight, after=shared[trio[-1]])["w_down"]
    shared["w_up"] = _finish_exchange("share_up_wait", up_flight, after=shared["w_down"])["w_up"]

    def adam(n, g, after=None):
        return _rowwise(lambda w, g, m, v: (g, *_adamw(w, g, m, v)), [w_of[n], g, m_of[n], v_of[n]], [F32] * 4,
                        "adamw_" + n, after=after)

    new_of, last = {}, None
    for n in ("w_down", "w_up", "w_out"):
        new_of[n] = adam(n, shared[n], last)
        last = new_of[n][1]
    new_of["w_ab"], new_of["w_cb"] = _adamw_pair(ab, cb, shared["w_br"], after=last)
    last = new_of["w_cb"][1]

    arrived = _finish_exchange("small_wait", small_flight, after=last)
    flat = lambda k: lambda t, j, chip: at[k] + j
    mine = lambda k, per_tap: lambda t, j, chip: at[k] + per_tap * t + (per_tap // N_CHIPS) * chip + j
    rows = lambda a: a.reshape(a.shape[1], 1, a.shape[2])
    small_p = [
        (mix_norm, m_mix_norm, v_mix_norm, flat(1)), (b_in, m_b_in, v_b_in, flat(2)), (sinks, m_sinks, v_sinks, flat(3)),
        (rows(conv_w), rows(m_conv_w), rows(v_conv_w), mine(4, CONV_W // 128)),
        (ffn_norm, m_ffn_norm, v_ffn_norm, flat(5)),
        (rows(ffn_conv_w), rows(m_ffn_conv_w), rows(v_ffn_conv_w), mine(6, FF2 // 128)),
        (final_norm[None, :], m_final_norm[None, :], v_final_norm[None, :], flat(7))]
    small_new = _adamw_small(pos, arrived["v"], arrived["slots"], small_p)
    loss = small_new[0][0, 0]
    small_g = small_new[1::4]
    small_new = [small_new[4 * k + 2:4 * k + 5] for k in range(len(small_p))]

    landed = _finish_exchange("rs_chips_in_wait", in_flight, after=small_new[0][0])
    half_in = _chip_sum(landed["s"], landed["r"], lay["w_in"], pos, "chip_sum_w_in")
    shared["w_in"] = _exchange("share_in", [[_rs_share(lay["w_in"], "w_in")]], bufs={"w_in": half_in})["w_in"]
    new_of["w_in"] = [a.T for a in adam("w_in", shared["w_in"])]
    big = ("w_in", "w_ab", "w_cb", "w_out", "w_up", "w_down")
    big_g = [new_of[n][0] for n in big]
    big_new = [new_of[n][1:] for n in big]

    order = [("s", 0), ("b", 0), ("s", 1), ("s", 2), ("s", 3), ("b", 1), ("b", 2), ("b", 3), ("s", 4), ("b", 4),
             ("s", 5), ("b", 5), ("s", 6)]
    shapes = [mix_norm.shape, w_in.shape, b_in.shape, sinks.shape, conv_w.shape, w_attn_branch.shape,
              w_conv_branch.shape, w_out.shape, ffn_norm.shape, w_up.shape, ffn_conv_w.shape, w_down.shape,
              final_norm.shape]
    out_g = [(small_g[k] if kind == "s" else big_g[k]).reshape(shp) for (kind, k), shp in zip(order, shapes)]
    news = [[(small_new[k][j] if kind == "s" else big_new[k][j]).reshape(shp) for (kind, k), shp in zip(order, shapes)]
            for j in range(3)]
    return (loss, grad_x[None], *out_g, *news[0], *news[1], *news[2])
```

```python
import functools

import jax
import jax.numpy as jnp
from jax import lax
from jax.experimental import pallas as pl
from jax.experimental.pallas import tpu as pltpu

F32 = jnp.float32
BF16 = jnp.bfloat16

D_MODEL = 1024
HEAD_DIM = 64
N_HEADS = 8
N_KV_HEADS = 2
GROUP = N_HEADS // N_KV_HEADS
BLOCK = 128
ATTN_SCALE = HEAD_DIM ** -0.5
ATTN_W = N_HEADS * HEAD_DIM
KV_W = N_KV_HEADS * HEAD_DIM
CONV_W = 512
QKV_W = ATTN_W + 2 * KV_W
C3_W = 3 * CONV_W
GATES_W = 2 * D_MODEL
IN_W = QKV_W + C3_W + GATES_W
D_FF = 2816
FF2 = 2 * D_FF
NORM_EPS = 1e-5
N_CHIPS = 4
IN_SHARD = IN_W // N_CHIPS
NEG = -1e30

ADAM_LR = 0.001
ADAM_B1 = 0.9
ADAM_B2 = 0.999
ADAM_EPS = 1e-08
ADAM_WD = 0.01
ADAM_STEP = 10

VMEM_LIMIT = 56 * 1024 * 1024
MESH = pl.DeviceIdType.MESH

NT = (((1,), (1,)), ((), ()))
TN = (((0,), (0,)), ((), ()))


def _params(*sem):
    return pltpu.CompilerParams(dimension_semantics=sem, vmem_limit_bytes=VMEM_LIMIT)


def _resident(shape):
    return pl.BlockSpec(shape, lambda *_: (0,) * len(shape), pipeline_mode=pl.Buffered(1))


def _sigmoid(v):
    return 0.5 * jnp.tanh(0.5 * v) + 0.5


def _rstd(v):
    return lax.rsqrt(jnp.mean(v * v, axis=-1, keepdims=True) + NORM_EPS)


def _rms_bwd(dy, v, rstd, g):
    vhat = v * rstd
    t = dy * g
    return rstd * (t - vhat * jnp.mean(t * vhat, axis=-1, keepdims=True)), dy * vhat


def _taps(z, cw):
    return cw[2:3] * z + cw[1:2] * pltpu.roll(z, 1, 0) + cw[0:1] * pltpu.roll(z, 2, 0)


def _causal_conv(z, prev, cw):
    edge = _taps(jnp.concatenate([prev, z[0:8]], axis=0), cw)
    return jnp.concatenate([edge[8:16], _taps(z, cw)[8:]], axis=0)


def _rows_after(z, nxt):
    n = z.shape[0]
    edge = jnp.concatenate([z[n - 8:n], nxt], axis=0)
    return tuple(jnp.concatenate([pltpu.roll(z, n - k, 0)[:n - 8], pltpu.roll(edge, 16 - k, 0)[0:8]], axis=0)
                 for k in (1, 2))


def _inproj_fwd(x, g1, w_in, b_in, tm, comm=None):
    s = x.shape[0]

    def body(x_ref, g_ref, w_ref, b_ref, xn_ref, qkv_ref, c3_ref, gt_ref):
        xf = x_ref[...]
        xn = (xf * _rstd(xf) * g_ref[...]).astype(BF16)
        xn_ref[...] = xn

        proj = (lax.dot_general(xn, w_ref[...], NT, preferred_element_type=F32) + b_ref[...]).astype(BF16)
        qkv_ref[...] = proj[:, :QKV_W]
        c3_ref[...] = proj[:, QKV_W:QKV_W + C3_W]
        gt_ref[...] = proj[:, QKV_W + C3_W:]

    row = lambda w: pl.BlockSpec((tm, w), lambda i: (i, 0))
    return _call(
        comm, body, name="inproj_fwd", grid=(s // tm,),
        in_specs=[row(D_MODEL), _resident((1, D_MODEL)), _resident((IN_W, D_MODEL)), _resident((1, IN_W))],
        out_specs=[row(D_MODEL), row(QKV_W), row(C3_W), row(GATES_W)],
        out_shape=[jax.ShapeDtypeStruct((s, D_MODEL), BF16), jax.ShapeDtypeStruct((s, QKV_W), BF16),
                   jax.ShapeDtypeStruct((s, C3_W), BF16), jax.ShapeDtypeStruct((s, GATES_W), BF16)],
        compiler_params=_params("parallel"),
    )(x, g1, w_in, b_in)


def _attn_bias():
    kj = jnp.arange(2 * BLOCK)[:, None]
    qi = (jnp.arange(GROUP * BLOCK) % BLOCK)[None, :]
    band = (kj > qi) & (kj <= qi + BLOCK)
    return jnp.stack([jnp.where(band & (kj >= BLOCK), 0.0, NEG), jnp.where(band, 0.0, NEG)]).astype(F32)


def _attn_bias_specs():
    shape = (None, 2 * BLOCK, GROUP * BLOCK)
    return pl.BlockSpec(shape, lambda i: (jnp.minimum(i, 1), 0, 0)), pl.BlockSpec(shape, lambda i: (1, 0, 0))


def _sink_row(sk_ref, h):
    lane = lax.broadcasted_iota(jnp.int32, (1, GROUP * BLOCK), 1)
    row = jnp.full((1, GROUP * BLOCK), sk_ref[h * GROUP], F32)
    for g in range(1, GROUP):
        row = jnp.where(lane >= g * BLOCK, sk_ref[h * GROUP + g], row)
    return row


def _stack_heads(t, h):
    return jnp.concatenate(
        [t[:, (h * GROUP + g) * HEAD_DIM:(h * GROUP + g + 1) * HEAD_DIM] for g in range(GROUP)], axis=0)


def _unstack_heads(per_kv):
    return jnp.concatenate(
        [t[g * BLOCK:(g + 1) * BLOCK] for t in per_kv for g in range(GROUP)], axis=1)


def _block_specs(n, steps):
    cur = lambda i: jnp.minimum(i, steps - 1)
    prev = lambda i: jnp.maximum(n * jnp.minimum(i, steps - 1) - 1, 0)
    kv = ATTN_W // KV_W
    return (pl.BlockSpec((n * BLOCK, ATTN_W), lambda i: (cur(i), 0)),
            pl.BlockSpec((BLOCK, KV_W), lambda i: (prev(i), kv)), pl.BlockSpec((n * BLOCK, KV_W), lambda i: (cur(i), kv)),
            pl.BlockSpec((BLOCK, KV_W), lambda i: (prev(i), kv + 1)),
            pl.BlockSpec((n * BLOCK, KV_W), lambda i: (cur(i), kv + 1)))


def _attn_fwd(qkv, sinks, comm=None):
    s = qkv.shape[0]
    n = min(4, s // BLOCK)
    steps = s // (n * BLOCK)

    def body(sk_ref, bias0_ref, bias1_ref, q_ref, kp_ref, kc_ref, vp_ref, vc_ref, o_ref):
        kc, vc = kc_ref[...], vc_ref[...]
        for b in range(n):
            rows, before = slice(b * BLOCK, (b + 1) * BLOCK), slice((b - 1) * BLOCK, b * BLOCK)
            kp, vp = (kp_ref[...], vp_ref[...]) if b == 0 else (kc[before], vc[before])
            q, bias = q_ref[rows, :], (bias0_ref if b == 0 else bias1_ref)[...]
            outs = []
            for h in range(N_KV_HEADS):
                hs = slice(h * HEAD_DIM, (h + 1) * HEAD_DIM)
                k2 = jnp.concatenate([kp[:, hs], kc[rows, hs]], axis=0)
                v2 = jnp.concatenate([vp[:, hs], vc[rows, hs]], axis=0)
                sc = lax.dot_general(k2, _stack_heads(q, h), NT, preferred_element_type=F32) * ATTN_SCALE + bias
                sink = _sink_row(sk_ref, h)
                m = jnp.maximum(jnp.max(sc, axis=0, keepdims=True), sink)
                p = jnp.exp(sc - m)
                den = jnp.sum(p, axis=0, keepdims=True) + jnp.exp(sink - m)
                out = lax.dot_general(v2, p.astype(BF16), TN, preferred_element_type=F32) / den
                outs.append(out.T)
            o_ref[rows, :] = _unstack_heads(outs).astype(BF16)

    return _call(
        comm, body, name="attn_fwd", grid=(steps,),
        in_specs=[pl.BlockSpec(memory_space=pltpu.SMEM), *_attn_bias_specs(), *_block_specs(n, steps)],
        out_specs=pl.BlockSpec((n * BLOCK, ATTN_W), lambda i: (i, 0)),
        out_shape=jax.ShapeDtypeStruct((s, ATTN_W), BF16),
        compiler_params=_params("parallel"),
    )(sinks, _attn_bias(), _attn_bias(), qkv, qkv, qkv, qkv, qkv)


def _mix_fwd(x, attn, c3, gates, conv_w, w_br, w_out, g2, tm, comm=None):
    s = x.shape[0]

    def body(x_ref, at_ref, c3_ref, gt_ref, cw_ref, wbr_ref, wo_ref, g_ref,
             conv_ref, a_ref, cv_ref, mg_ref, h1_ref, hn_ref, carry_ref):
        @pl.when(pl.program_id(0) == 0)
        def _():
            carry_ref[...] = jnp.zeros_like(carry_ref)

        c3v = c3_ref[...].astype(F32)
        cb, cc, cx = c3v[:, :CONV_W], c3v[:, CONV_W:2 * CONV_W], c3v[:, 2 * CONV_W:]
        z = cc * cx
        cz = _causal_conv(z, carry_ref[...], cw_ref[...])
        carry_ref[...] = z[tm - 8:tm]
        conv = (cb * cz).astype(BF16)
        conv_ref[...] = conv
        a = jnp.dot(at_ref[...], wbr_ref[:ATTN_W, :], preferred_element_type=F32)
        cv = jnp.dot(conv, wbr_ref[ATTN_W:, :], preferred_element_type=F32)
        a_ref[...] = a.astype(BF16)
        cv_ref[...] = cv.astype(BF16)
        gt = gt_ref[...].astype(F32)
        merged = (_sigmoid(gt[:, :D_MODEL]) * a + _sigmoid(gt[:, D_MODEL:]) * cv).astype(BF16)
        mg_ref[...] = merged
        h1 = x_ref[...] + jnp.dot(merged, wo_ref[...], preferred_element_type=F32)
        h1_ref[...] = h1
        hn_ref[...] = (h1 * _rstd(h1) * g_ref[...]).astype(BF16)

    row = lambda w: pl.BlockSpec((tm, w), lambda i: (i, 0))
    return _call(
        comm, body, name="mix_fwd", grid=(s // tm,),
        in_specs=[row(D_MODEL), row(ATTN_W), row(C3_W), row(GATES_W), _resident((3, CONV_W)),
                  _resident((ATTN_W + CONV_W, D_MODEL)), _resident((D_MODEL, D_MODEL)), _resident((1, D_MODEL))],
        out_specs=[row(CONV_W), row(D_MODEL), row(D_MODEL), row(D_MODEL), row(D_MODEL), row(D_MODEL)],
        out_shape=[jax.ShapeDtypeStruct((s, CONV_W), BF16), jax.ShapeDtypeStruct((s, D_MODEL), BF16),
                   jax.ShapeDtypeStruct((s, D_MODEL), BF16), jax.ShapeDtypeStruct((s, D_MODEL), BF16),
                   jax.ShapeDtypeStruct((s, D_MODEL), F32), jax.ShapeDtypeStruct((s, D_MODEL), BF16)],
        scratch_shapes=[pltpu.VMEM((8, CONV_W), F32)],
        compiler_params=_params("arbitrary"),
    )(x, attn, c3, gates, conv_w, w_br, w_out, g2)


def _ffn_fwd_loss(hn, h1, w_up, ffn_cw, w_down, g3, target, tm):
    s = hn.shape[0]

    def body(hn_ref, h1_ref, wu_ref, cw_ref, wd_ref, g_ref, t_ref,
             u_ref, up_ref, act_ref, dh2_ref, loss_ref, gfn_ref, carry_ref):
        @pl.when(pl.program_id(0) == 0)
        def _():
            carry_ref[...] = jnp.zeros_like(carry_ref)
            loss_ref[...] = jnp.zeros_like(loss_ref)
            gfn_ref[...] = jnp.zeros_like(gfn_ref)

        u = jnp.dot(hn_ref[...], wu_ref[...], preferred_element_type=F32)
        u_ref[...] = u.astype(BF16)
        up = _causal_conv(u, carry_ref[...], cw_ref[...])
        up_ref[...] = up.astype(BF16)
        carry_ref[...] = u[tm - 8:tm]
        gate, val = up[:, :D_FF], up[:, D_FF:]
        act = (gate * _sigmoid(gate) * val).astype(BF16)
        act_ref[...] = act
        h2 = h1_ref[...] + jnp.dot(act, wd_ref[...], preferred_element_type=F32)
        rstd = _rstd(h2)
        g = g_ref[...]
        err = h2 * rstd * g - t_ref[...]
        loss_ref[...] += jnp.sum(err * err) * (0.5 / D_MODEL)
        dh2, dg = _rms_bwd(err * (1.0 / D_MODEL), h2, rstd, g)
        dh2_ref[...] = dh2
        gfn_ref[...] += jnp.sum(dg, axis=0, keepdims=True)

    row = lambda w: pl.BlockSpec((tm, w), lambda i: (i, 0))
    acc = lambda w: pl.BlockSpec((1, w), lambda i: (0, 0))
    return pl.pallas_call(
        body, name="ffn_fwd_loss", grid=(s // tm,),
        in_specs=[row(D_MODEL), row(D_MODEL), _resident((D_MODEL, FF2)), _resident((3, FF2)),
                  _resident((D_FF, D_MODEL)), _resident((1, D_MODEL)), row(D_MODEL)],
        out_specs=[row(FF2), row(FF2), row(D_FF), row(D_MODEL), acc(128), acc(D_MODEL)],
        out_shape=[jax.ShapeDtypeStruct((s, FF2), BF16), jax.ShapeDtypeStruct((s, FF2), BF16),
                   jax.ShapeDtypeStruct((s, D_FF), BF16),
                   jax.ShapeDtypeStruct((s, D_MODEL), F32), jax.ShapeDtypeStruct((1, 128), F32),
                   jax.ShapeDtypeStruct((1, D_MODEL), F32)],
        scratch_shapes=[pltpu.VMEM((8, FF2), F32)],
        compiler_params=_params("arbitrary"),
    )(hn, h1, w_up, ffn_cw, w_down, g3, target)


def _ffn_bwd(dh2, u, up, h1, w_up, ffn_cw, w_down, g2, tm):
    s = dh2.shape[0]
    nt = s // tm

    def body(dh2_ref, u_ref, up_ref, h1_ref, wu_ref, cw_ref, wd_ref, g_ref,
             du_ref, dh1_ref, gcw_ref, gg_ref, carry_ref):
        @pl.when(pl.program_id(0) == 0)
        def _():
            for ref in (carry_ref, gcw_ref, gg_ref):
                ref[...] = jnp.zeros_like(ref)

        dh2v = dh2_ref[...]
        dact = lax.dot_general(dh2v.astype(BF16), wd_ref[...], NT, preferred_element_type=F32)
        upv = up_ref[...].astype(F32)
        gate, val = upv[:, :D_FF], upv[:, D_FF:]
        sg = _sigmoid(gate)
        dval = dact * (gate * sg)
        dgate = dact * val * (sg * (1.0 + gate * (1.0 - sg)))
        dup = jnp.concatenate([dgate, dval], axis=1)
        dup1, dup2 = _rows_after(dup, carry_ref[...])
        carry_ref[...] = dup[0:8]
        u = u_ref[...].astype(F32)
        gcw_ref[2:3, :] += jnp.sum(dup * u, axis=0, keepdims=True)
        gcw_ref[1:2, :] += jnp.sum(dup1 * u, axis=0, keepdims=True)
        gcw_ref[0:1, :] += jnp.sum(dup2 * u, axis=0, keepdims=True)
        cw = cw_ref[...]
        du = (cw[2:3] * dup + cw[1:2] * dup1 + cw[0:1] * dup2).astype(BF16)
        du_ref[...] = du
        dhn = lax.dot_general(du, wu_ref[...], NT, preferred_element_type=F32)
        h1v = h1_ref[...]
        dh1, dg = _rms_bwd(dhn, h1v, _rstd(h1v), g_ref[...])
        dh1_ref[...] = dh2v + dh1
        gg_ref[...] += jnp.sum(dg, axis=0, keepdims=True)

    row = lambda w: pl.BlockSpec((tm, w), lambda i: (nt - 1 - i, 0))
    return pl.pallas_call(
        body, name="ffn_bwd", grid=(nt,),
        in_specs=[row(D_MODEL), row(FF2), row(FF2),
                  row(D_MODEL), _resident((D_MODEL, FF2)), _resident((3, FF2)), _resident((D_FF, D_MODEL)),
                  _resident((1, D_MODEL))],
        out_specs=[row(FF2), row(D_MODEL), pl.BlockSpec((3, FF2), lambda i: (0, 0)),
                   pl.BlockSpec((1, D_MODEL), lambda i: (0, 0))],
        out_shape=[jax.ShapeDtypeStruct((s, FF2), BF16), jax.ShapeDtypeStruct((s, D_MODEL), F32),
                   jax.ShapeDtypeStruct((3, FF2), F32), jax.ShapeDtypeStruct((1, D_MODEL), F32)],
        scratch_shapes=[pltpu.VMEM((8, FF2), F32)],
        compiler_params=_params("arbitrary"),
    )(dh2, u, up, h1, w_up, ffn_cw, w_down, g2)


def _mix_bwd(dh1, gates, a, cv, c3, attn, conv, merged, conv_w, w_br, w_out, tm, comm=None):
    s = dh1.shape[0]
    nt = s // tm
    halo = 16

    def body(dh1_ref, gt_ref, a_ref, cv_ref, c3_ref, ch_ref, at_ref, cn_ref, mg_ref, cw_ref, wbr_ref,
             wo_ref, dat_ref, dc3_ref, dgt_ref, gcw_ref, gbr_ref, gout_ref, carry_ref, br_acc, out_acc):
        i = pl.program_id(0)

        @pl.when(i == 0)
        def _():
            for ref in (carry_ref, gcw_ref, br_acc, out_acc):
                ref[...] = jnp.zeros_like(ref)

        dh1v = dh1_ref[...].astype(BF16)
        out_acc[...] += lax.dot_general(mg_ref[...], dh1v, TN, preferred_element_type=F32)
        dm = lax.dot_general(dh1v, wo_ref[...], NT, preferred_element_type=F32)
        gt = gt_ref[...].astype(F32)
        sa, sc = _sigmoid(gt[:, :D_MODEL]), _sigmoid(gt[:, D_MODEL:])
        da = (dm * sa).astype(BF16)
        dcv = (dm * sc).astype(BF16)
        br_acc[:ATTN_W, :] += lax.dot_general(at_ref[...], da, TN, preferred_element_type=F32)
        br_acc[ATTN_W:, :] += lax.dot_general(cn_ref[...], dcv, TN, preferred_element_type=F32)
        dgt_ref[...] = jnp.concatenate(
            [dm * a_ref[...].astype(F32) * (sa * (1.0 - sa)), dm * cv_ref[...].astype(F32) * (sc * (1.0 - sc))],
            axis=1).astype(BF16)
        dat_ref[...] = lax.dot_general(da, wbr_ref[:ATTN_W, :], NT, preferred_element_type=F32).astype(BF16)
        dconv = lax.dot_general(dcv, wbr_ref[ATTN_W:, :], NT, preferred_element_type=F32)
        c3v = c3_ref[...].astype(F32)
        cb, cc, cx = c3v[:, :CONV_W], c3v[:, CONV_W:2 * CONV_W], c3v[:, 2 * CONV_W:]
        z = cc * cx
        chv = ch_ref[...].astype(F32)[halo - 8:halo] * (i < nt - 1).astype(F32)
        zh = chv[:, CONV_W:2 * CONV_W] * chv[:, 2 * CONV_W:]
        cw = cw_ref[...]
        cz = _causal_conv(z, zh, cw)
        dcz = dconv * cb
        dcz1, dcz2 = _rows_after(dcz, carry_ref[...])
        carry_ref[...] = dcz[0:8]
        gcw_ref[2:3, :] += jnp.sum(dcz * z, axis=0, keepdims=True)
        gcw_ref[1:2, :] += jnp.sum(dcz1 * z, axis=0, keepdims=True)
        gcw_ref[0:1, :] += jnp.sum(dcz2 * z, axis=0, keepdims=True)
        dz = cw[2:3] * dcz + cw[1:2] * dcz1 + cw[0:1] * dcz2
        dc3_ref[...] = jnp.concatenate([dconv * cz, dz * cx, dz * cc], axis=1).astype(BF16)

        @pl.when(i == nt - 1)
        def _():
            gbr_ref[...] = br_acc[...].astype(BF16)
            gout_ref[...] = out_acc[...].astype(BF16)

    row = lambda w: pl.BlockSpec((tm, w), lambda i: (nt - 1 - i, 0))
    return _call(
        comm, body, name="mix_bwd", grid=(nt,),
        in_specs=[row(D_MODEL), row(GATES_W), row(D_MODEL), row(D_MODEL), row(C3_W),
                  pl.BlockSpec((halo, C3_W), lambda i: (jnp.maximum((nt - 1 - i) * (tm // halo) - 1, 0), 0)),
                  row(ATTN_W), row(CONV_W), row(D_MODEL), _resident((3, CONV_W)),
                  _resident((ATTN_W + CONV_W, D_MODEL)), _resident((D_MODEL, D_MODEL))],
        out_specs=[row(ATTN_W), row(C3_W), row(GATES_W), pl.BlockSpec((3, CONV_W), lambda i: (0, 0)),
                   _resident((ATTN_W + CONV_W, D_MODEL)), _resident((D_MODEL, D_MODEL))],
        out_shape=[jax.ShapeDtypeStruct((s, ATTN_W), BF16), jax.ShapeDtypeStruct((s, C3_W), BF16),
                   jax.ShapeDtypeStruct((s, GATES_W), BF16), jax.ShapeDtypeStruct((3, CONV_W), F32),
                   jax.ShapeDtypeStruct((ATTN_W + CONV_W, D_MODEL), BF16),
                   jax.ShapeDtypeStruct((D_MODEL, D_MODEL), BF16)],
        scratch_shapes=[pltpu.VMEM((8, CONV_W), F32), pltpu.VMEM((ATTN_W + CONV_W, D_MODEL), F32),
                        pltpu.VMEM((D_MODEL, D_MODEL), F32)],
        compiler_params=_params("arbitrary"),
    )(dh1, gates, a, cv, c3, c3, attn, conv, merged, conv_w, w_br, w_out)


def _attn_bwd(qkv, sinks, o, do, comm=None):
    s = qkv.shape[0]
    npair = s // (2 * BLOCK)

    def one_block(sk_ref, bias, q, kp, kc, vp, vc, ov, dov, dsk_ref):
        dqs, dks, dvs = [], [], []
        for h in range(N_KV_HEADS):
            hs = slice(h * HEAD_DIM, (h + 1) * HEAD_DIM)
            k2 = jnp.concatenate([kp[:, hs], kc[:, hs]], axis=0)
            v2 = jnp.concatenate([vp[:, hs], vc[:, hs]], axis=0)
            qg, og, dog = _stack_heads(q, h), _stack_heads(ov, h), _stack_heads(dov, h)
            sc = lax.dot_general(k2, qg, NT, preferred_element_type=F32) * ATTN_SCALE + bias
            sink = _sink_row(sk_ref, h)
            m = jnp.maximum(jnp.max(sc, axis=0, keepdims=True), sink)
            p = jnp.exp(sc - m)
            psink = jnp.exp(sink - m)
            inv = 1.0 / (jnp.sum(p, axis=0, keepdims=True) + psink)
            p = p * inv
            delta = jnp.sum(dog.astype(F32) * og.astype(F32), axis=1, keepdims=True).T
            dp = lax.dot_general(v2, dog, NT, preferred_element_type=F32)
            ds = (p * (dp - delta)).astype(BF16)
            dqs.append((lax.dot_general(k2, ds, TN, preferred_element_type=F32) * ATTN_SCALE).T)
            dks.append(jnp.dot(ds, qg, preferred_element_type=F32) * ATTN_SCALE)
            dvs.append(jnp.dot(p.astype(BF16), dog, preferred_element_type=F32))
            dsink = -(psink * inv * delta)
            for g in range(GROUP):
                r = h * GROUP + g
                dsk_ref[r:r + 1, :] += jnp.sum(dsink[:, g * BLOCK:(g + 1) * BLOCK])
        return _unstack_heads(dqs), jnp.concatenate(dks, axis=1), jnp.concatenate(dvs, axis=1)

    def body(sk_ref, bias0_ref, bias1_ref, q_ref, kp_ref, kc_ref, vp_ref, vc_ref, o_ref, do_ref,
             dq_ref, dke_ref, dko_ref, dve_ref, dvo_ref, dsk_ref, ck_ref, cvv_ref):
        i = pl.program_id(0)

        @pl.when(i == 0)
        def _():
            for ref in (ck_ref, cvv_ref, dsk_ref):
                ref[...] = jnp.zeros_like(ref)

        @pl.when(i < npair)
        def _():
            kc, vc = kc_ref[...], vc_ref[...]
            first, second = slice(0, BLOCK), slice(BLOCK, 2 * BLOCK)
            dq0, dk0, dv0 = one_block(sk_ref, bias0_ref[...], q_ref[first, :], kp_ref[...], kc[first], vp_ref[...],
                                      vc[first], o_ref[first, :], do_ref[first, :], dsk_ref)
            dq1, dk1, dv1 = one_block(sk_ref, bias1_ref[...], q_ref[second, :], kc[first], kc[second], vc[first],
                                      vc[second], o_ref[second, :], do_ref[second, :], dsk_ref)
            dq_ref[first, :] = dq0.astype(BF16)
            dq_ref[second, :] = dq1.astype(BF16)
            dko_ref[...] = (ck_ref[...] + dk0[:BLOCK]).astype(BF16)
            dvo_ref[...] = (cvv_ref[...] + dv0[:BLOCK]).astype(BF16)
            dke_ref[...] = (dk0[BLOCK:] + dk1[:BLOCK]).astype(BF16)
            dve_ref[...] = (dv0[BLOCK:] + dv1[:BLOCK]).astype(BF16)
            ck_ref[...] = dk1[BLOCK:]
            cvv_ref[...] = dv1[BLOCK:]

        @pl.when(i == npair)
        def _():
            dko_ref[...] = ck_ref[...].astype(BF16)
            dvo_ref[...] = cvv_ref[...].astype(BF16)

    cur = lambda i: jnp.minimum(i, npair - 1)
    done = lambda i: jnp.maximum(i - 1, 0)
    rows = pl.BlockSpec((2 * BLOCK, ATTN_W), lambda i: (cur(i), 0))
    even = pl.BlockSpec((BLOCK, KV_W), lambda i: (cur(i), 0))
    odd = pl.BlockSpec((BLOCK, KV_W), lambda i: (done(i), 0))
    half = jax.ShapeDtypeStruct((s // 2, KV_W), BF16)
    return _call(
        comm, body, name="attn_bwd", grid=(npair + 1,),
        in_specs=[pl.BlockSpec(memory_space=pltpu.SMEM), *_attn_bias_specs(), *_block_specs(2, npair), rows, rows],
        out_specs=[rows, even, odd, even, odd, pl.BlockSpec((N_HEADS, 128), lambda i: (0, 0))],
        out_shape=[jax.ShapeDtypeStruct((s, ATTN_W), BF16), half, half, half, half,
                   jax.ShapeDtypeStruct((N_HEADS, 128), F32)],
        scratch_shapes=[pltpu.VMEM((BLOCK, KV_W), F32), pltpu.VMEM((BLOCK, KV_W), F32)],
        compiler_params=_params("arbitrary"),
    )(sinks, _attn_bias(), _attn_bias(), qkv, qkv, qkv, qkv, qkv, o, do)


def _inproj_bwd(dq, dk, dv, dc3, dgt, w_in, x, xn, dh1, g1):
    s = x.shape[0]
    tm = min(2 * BLOCK, s)
    nt = s // tm

    def body(dq_ref, dke_ref, dko_ref, dve_ref, dvo_ref, dc3_ref, dgt_ref, w_ref, x_ref, xn_ref, dh1_ref, g_ref,
             dx_ref, gw_ref, gb_ref, gg_ref, acc_ref):
        i = pl.program_id(0)

        @pl.when(i == 0)
        def _():
            for ref in (gb_ref, gg_ref, acc_ref):
                ref[...] = jnp.zeros_like(ref)

        dk = jnp.concatenate([dke_ref[...], dko_ref[...]], axis=0)
        dv = jnp.concatenate([dve_ref[...], dvo_ref[...]], axis=0)
        dp = jnp.concatenate([dq_ref[...], dk, dv, dc3_ref[...], dgt_ref[...]], axis=1)
        acc_ref[...] += lax.dot_general(dp, xn_ref[...], TN, preferred_element_type=F32)
        gb_ref[...] += jnp.sum(dp.astype(F32), axis=0, keepdims=True)
        dxn = jnp.dot(dp, w_ref[...], preferred_element_type=F32)
        xf = x_ref[...]
        dx, dg = _rms_bwd(dxn, xf, _rstd(xf), g_ref[...])
        dx_ref[...] = dh1_ref[...] + dx
        gg_ref[...] += jnp.sum(dg, axis=0, keepdims=True)

        @pl.when(i == nt - 1)
        def _():
            gw_ref[...] = acc_ref[...].astype(BF16)

    row = lambda w: pl.BlockSpec((tm, w), lambda i: (i, 0))
    acc = lambda w: pl.BlockSpec((1, w), lambda i: (0, 0))
    block = pl.BlockSpec((tm // 2, KV_W), lambda i: (i, 0))
    return pl.pallas_call(
        body, name="inproj_bwd", grid=(nt,),
        in_specs=[row(ATTN_W), block, block, block, block, row(C3_W), row(GATES_W), _resident((IN_W, D_MODEL)),
                  row(D_MODEL), row(D_MODEL), row(D_MODEL), _resident((1, D_MODEL))],
        out_specs=[row(D_MODEL), _resident((IN_W, D_MODEL)), acc(IN_W), acc(D_MODEL)],
        out_shape=[jax.ShapeDtypeStruct((s, D_MODEL), F32), jax.ShapeDtypeStruct((IN_W, D_MODEL), BF16),
                   jax.ShapeDtypeStruct((1, IN_W), F32), jax.ShapeDtypeStruct((1, D_MODEL), F32)],
        scratch_shapes=[pltpu.VMEM((IN_W, D_MODEL), F32)],
        compiler_params=_params("arbitrary"),
    )(dq, *dk, *dv, dc3, dgt, w_in, x, xn, dh1, g1)


def _wgrad(a, b, bm, bn, bk, name, comm=None):
    s, m = a.shape
    n = b.shape[1]
    nk = s // bk

    def body(a_ref, b_ref, o_ref, acc_ref):
        k = pl.program_id(2)

        @pl.when(k == 0)
        def _():
            acc_ref[...] = jnp.zeros_like(acc_ref)

        acc_ref[...] += lax.dot_general(a_ref[...].astype(BF16), b_ref[...].astype(BF16), TN,
                                        preferred_element_type=F32)

        @pl.when(k == nk - 1)
        def _():
            o_ref[...] = acc_ref[...].astype(BF16)

    return _call(
        comm, body, name=name, grid=(m // bm, n // bn, nk),
        in_specs=[pl.BlockSpec((bk, bm), lambda i, j, k: (k, i)), pl.BlockSpec((bk, bn), lambda i, j, k: (k, j))],
        out_specs=pl.BlockSpec((bm, bn), lambda i, j, k: (i, j)),
        out_shape=jax.ShapeDtypeStruct((m, n), BF16),
        scratch_shapes=[pltpu.VMEM((bm, bn), F32)],
        compiler_params=_params("parallel", "parallel", "arbitrary"),
    )(a, b)


class _Carry:
    def __init__(self, jobs, reads=None, bufs=None, fresh=None):
        self.jobs, self.reads, self.bufs, self.fresh = jobs, reads or {}, bufs or {}, fresh or {}
        self.out = {}


class _Job:
    def __init__(self, n_sems, plan):
        self.n_sems, self.plan = n_sems, plan


def _plan_all(jobs, hbm, send, recv):
    pos = _position()
    starts, waits, base = [], [], 0
    for job in jobs:
        s, w = job.plan(hbm, pos, send, recv, base)
        starts, waits, base = starts + s, waits + w, base + job.n_sems
    return starts, waits


def _call(comm, body, **kw):
    if comm is None:
        return pl.pallas_call(body, **kw)
    grid = kw["grid"]
    single = not isinstance(kw["out_shape"], (list, tuple))
    out_shape = [kw["out_shape"]] if single else list(kw["out_shape"])
    out_specs = [kw["out_specs"]] if single else list(kw["out_specs"])
    in_specs = list(kw["in_specs"])
    scratch = list(kw.get("scratch_shapes", ()))
    r_names, b_names, f_names = list(comm.reads), list(comm.bufs), list(comm.fresh)
    n_args, n_out, n_scr = len(in_specs), len(out_shape), len(scratch)
    n_sems = sum(j.n_sems for j in comm.jobs)

    def wrapped(*refs):
        k = n_args
        hbm = dict(zip(r_names, refs[k:k + len(r_names)]))
        k += len(r_names) + len(b_names)
        outs = refs[k:k + n_out]
        k += n_out
        hbm.update(zip(b_names + f_names, refs[k:k + len(b_names) + len(f_names)]))
        k += len(b_names) + len(f_names)
        send, recv = refs[k + n_scr:]
        starts, waits = _plan_all(comm.jobs, hbm, send, recv)
        ids = [pl.program_id(a) for a in range(len(grid))]
        first = functools.reduce(jnp.logical_and, [i == 0 for i in ids])
        last = functools.reduce(jnp.logical_and, [i == g - 1 for i, g in zip(ids, grid)])

        @pl.when(first)
        def _():
            for cp in starts:
                cp.start()

        body(*refs[:n_args], *outs, *refs[k:k + n_scr])

        @pl.when(last)
        def _():
            for cp in waits:
                cp.wait_recv()
            for cp in starts:
                cp.wait_send()

    sems = pltpu.SemaphoreType.DMA((n_sems,))
    held = [jax.ShapeDtypeStruct(a.shape, a.dtype) for a in comm.bufs.values()] + list(comm.fresh.values())
    call = pl.pallas_call(
        wrapped, name=kw["name"], grid=grid,
        in_specs=in_specs + [_ANY] * (len(r_names) + len(b_names)),
        out_specs=out_specs + [_ANY] * len(held),
        out_shape=out_shape + held,
        input_output_aliases={n_args + len(r_names) + i: n_out + i for i in range(len(b_names))},
        scratch_shapes=scratch + [sems, sems],
        compiler_params=_params(*["arbitrary"] * len(grid)),
    )

    def run(*args):
        res = call(*args, *comm.reads.values(), *comm.bufs.values())
        comm.out = dict(zip(b_names + f_names, res[n_out:]))
        return res[0] if single else res[:n_out]

    return run


def _exchange(name, phases, reads=None, bufs=None, fresh=None):
    comm = _Carry([j for ph in phases for j in ph], reads, bufs, fresh)
    r_names, b_names, f_names = list(comm.reads), list(comm.bufs), list(comm.fresh)
    n_sems = sum(j.n_sems for j in comm.jobs)

    def body(*refs):
        hbm = dict(zip(r_names, refs[:len(r_names)]))
        k = len(r_names) + len(b_names)
        hbm.update(zip(b_names + f_names, refs[k:k + len(b_names) + len(f_names)]))
        send, recv = refs[-2:]
        pos = _position()
        started, base = [], 0
        for ph in phases:
            waits = []
            for job in ph:
                s, w = job.plan(hbm, pos, send, recv, base)
                base += job.n_sems
                for cp in s:
                    cp.start()
                started, waits = started + s, waits + w
            for cp in waits:
                cp.wait_recv()
        for cp in started:
            cp.wait_send()

    sems = pltpu.SemaphoreType.DMA((n_sems,))
    held = [jax.ShapeDtypeStruct(a.shape, a.dtype) for a in comm.bufs.values()] + list(comm.fresh.values())
    res = pl.pallas_call(
        body, name=name, in_specs=[_ANY] * (len(r_names) + len(b_names)), out_specs=[_ANY] * len(held),
        out_shape=held, input_output_aliases={len(r_names) + i: i for i in range(len(b_names))},
        scratch_shapes=[sems, sems],
    )(*comm.reads.values(), *comm.bufs.values())
    return dict(zip(b_names + f_names, res))


_HBM = pl.BlockSpec(memory_space=pltpu.HBM)
_SEM = pl.BlockSpec(memory_space=pltpu.SEMAPHORE)
_EFFECT = pltpu.SideEffectType.DATAFLOW_SIDE_EFFECTING


def _start_exchanges(name, groups):
    names = [list(arrays) for _, arrays in groups]
    first = [sum(len(ns) for ns in names[:g]) for g in range(len(groups))]
    n, ng = sum(len(ns) for ns in names), len(groups)

    def body(*refs):
        for g, (jobs, _) in enumerate(groups):
            hbm = dict(zip(names[g], refs[first[g]:first[g] + len(names[g])]))
            for cp in _plan_all(jobs, hbm, refs[n + 2 * g], refs[n + 2 * g + 1])[0]:
                cp.start()
        refs[-1][...] = jnp.zeros_like(refs[-1])

    given = [pltpu.with_memory_space_constraint(
        a if isinstance(a, jax.Array) else lax.empty(a.shape, a.dtype), pltpu.HBM)
        for _, arrays in groups for a in arrays.values()]
    sems = [pltpu.SemaphoreType.DMA((sum(j.n_sems for j in jobs),)) for jobs, _ in groups for _ in range(2)]
    res = pl.pallas_call(
        body, name=name,
        out_shape=(*sems, *[pltpu.HBM(a.shape, a.dtype) for a in given], jax.ShapeDtypeStruct((8, 128), F32)),
        in_specs=[_HBM] * n, out_specs=(*[_SEM] * (2 * ng), *[_HBM] * n, pl.BlockSpec(memory_space=pltpu.VMEM)),
        input_output_aliases={i: 2 * ng + i for i in range(n)},
        compiler_params=pltpu.CompilerParams(has_side_effects=_EFFECT),
    )(*given)
    held = res[2 * ng:2 * ng + n]
    states = [(names[g], groups[g][0], res[2 * g], res[2 * g + 1], held[first[g]:first[g] + len(names[g])])
              for g in range(ng)]
    return states, res[-1]


def _start_exchange(name, jobs, arrays):
    states, token = _start_exchanges(name, [(jobs, arrays)])
    return states[0], token


def _finish_exchange(name, state, after):
    names, jobs, send_sem, recv_sem, held = state
    n = len(names)

    def body(*refs):
        hbm = dict(zip(names, refs[:n]))
        send, recv = refs[n:n + 2]
        starts, waits = _plan_all(jobs, hbm, send, recv)
        for cp in waits:
            cp.wait_recv()
        for cp in starts:
            cp.wait_send()

    res = pl.pallas_call(
        body, name=name, out_shape=tuple(pltpu.HBM(a.shape, a.dtype) for a in held),
        in_specs=[_HBM] * n + [_SEM, _SEM, _ANY], out_specs=tuple([_HBM] * n),
        input_output_aliases={i: i for i in range(n)},
        compiler_params=pltpu.CompilerParams(has_side_effects=_EFFECT),
    )(*held, send_sem, recv_sem, after)
    return dict(zip(names, res))


def _row_tile(rows, bytes_per_row):
    best = 16
    for t in range(16, rows + 1, 16):
        if rows % t == 0 and t * bytes_per_row <= 9 * 1024 * 1024:
            best = t
    return best


def _rowwise(fn, ins, out_dtypes, name, after=None):
    rows, cols = ins[0].shape
    per_row = sum(cols * a.dtype.itemsize for a in ins) + sum(cols * jnp.dtype(d).itemsize for d in out_dtypes)
    tr = _row_tile(rows, per_row)
    n_in = len(ins)

    def body(*refs):
        outs = fn(*[r[...] for r in refs[:n_in]])
        for o_ref, o in zip(refs[-len(out_dtypes):], outs):
            o_ref[...] = o.astype(o_ref.dtype)

    tile = pl.BlockSpec((tr, cols), lambda i: (i, 0))
    behind = [] if after is None else [after]
    return pl.pallas_call(
        body, name=name, grid=(rows // tr,),
        in_specs=[tile] * n_in + [pl.BlockSpec((8, 128), lambda i: (0, 0))] * len(behind),
        out_specs=[tile] * len(out_dtypes),
        out_shape=[jax.ShapeDtypeStruct((rows, cols), d) for d in out_dtypes],
        compiler_params=_params("parallel"),
    )(*ins, *behind)


def _tiled(fn, name, grid, pos, ins, outs):
    n_in = len(ins)

    def body(pos_ref, *refs):
        res = fn(*[r[...] for r in refs[:n_in]])
        for o_ref, o in zip(refs[n_in:], res):
            o_ref[...] = o.astype(o_ref.dtype)

    return pl.pallas_call(
        body, name=name,
        grid_spec=pltpu.PrefetchScalarGridSpec(
            num_scalar_prefetch=1, grid=grid,
            in_specs=[pl.BlockSpec(bs, im) for _, bs, im in ins],
            out_specs=[pl.BlockSpec(bs, im) for _, _, bs, im in outs]),
        out_shape=[jax.ShapeDtypeStruct(s, d) for s, d, _, _ in outs],
        compiler_params=_params("parallel"),
    )(pos, *[a for a, _, _ in ins])


def _adamw(w, g, m, v):
    m = ADAM_B1 * m + (1.0 - ADAM_B1) * g
    v = ADAM_B2 * v + (1.0 - ADAM_B2) * (g * g)
    m_hat = m / (1.0 - ADAM_B1 ** ADAM_STEP)
    v_hat = v / (1.0 - ADAM_B2 ** ADAM_STEP)
    return -ADAM_LR * (m_hat / (jnp.sqrt(v_hat) + ADAM_EPS) + ADAM_WD * w), m, v


def _adamw_small(pos, own, slots, params):
    n = len(params)

    def body(pos_ref, own_ref, slots_ref, *refs):
        ins, outs, total_ref = refs[:3 * n], refs[3 * n:-1], refs[-1]
        chip = pos_ref[0]
        idx = 2 * chip + pos_ref[1]
        term = lambda q: jnp.where(idx == q, own_ref[...], slots_ref[q])
        acc = term(0)
        for q in range(1, N_DEV):
            acc = acc + term(q)
        total_ref[...] = acc
        outs[0][...] = total_ref[0:1, :]
        for k, (w, _, _, row) in enumerate(params):
            width = min(w.shape[-1], 128)
            for t in range(w.shape[0]):
                for j in range(w.shape[-1] // width):
                    lanes = slice(j * width, (j + 1) * width)
                    at = (slice(t, t + 1), lanes) if w.ndim == 2 else (t, slice(None), lanes)
                    g = total_ref[pl.ds(row(t, j, chip), 1), :][:, :width]
                    new = _adamw(ins[3 * k][at], g, ins[3 * k + 1][at], ins[3 * k + 2][at])
                    for o_ref, o in zip(outs[1 + 4 * k:5 + 4 * k], (g, *new)):
                        o_ref[at] = o

    vmem = pl.BlockSpec(memory_space=pltpu.VMEM)
    return pl.pallas_call(
        body, name="adamw_small",
        in_specs=[pl.BlockSpec(memory_space=pltpu.SMEM)] + [vmem] * (2 + 3 * n),
        out_shape=[jax.ShapeDtypeStruct((1, 128), F32)]
        + [jax.ShapeDtypeStruct(p[0].shape, F32) for p in params for _ in range(4)],
        scratch_shapes=[pltpu.VMEM(own.shape, F32)],
    )(pos, own, slots, *[a for p in params for a in p[:3]])


class _Layout:
    def __init__(self, rows, cols, stacked):
        self.rows, self.cols, self.stacked = rows, cols, stacked

    def whole(self, rows=None):
        r = self.rows if rows is None else rows
        return (N_CHIPS, r, self.cols) if self.stacked else (r, N_CHIPS * self.cols)

    def part_rows(self, h, q=0, nq=1):
        n = self.rows // 2 // nq
        return pl.ds(pl.multiple_of(h * (self.rows // 2) + q * n, 16), n)

    def half_rows(self, h):
        return self.part_rows(h)

    def block(self, ref, p, rows=slice(None)):
        if self.stacked:
            return ref.at[p, rows, :]
        return ref.at[rows, pl.ds(pl.multiple_of(p * self.cols, 128), self.cols)]

    def all_chips(self, ref, rows):
        return ref.at[:, rows, :] if self.stacked else ref.at[rows, :]


BIG = (
    _Layout(IN_SHARD, D_MODEL, True),
    _Layout(ATTN_W + CONV_W, D_MODEL // N_CHIPS, False),
    _Layout(D_MODEL // N_CHIPS, D_MODEL, True),
    _Layout(D_MODEL, FF2 // N_CHIPS, False),
    _Layout(D_FF // N_CHIPS, D_MODEL, True),
)
N_BIG = len(BIG)
_ANY = pl.BlockSpec(memory_space=pl.ANY)


def _position():
    x, y, c = lax.axis_index("x"), lax.axis_index("y"), lax.axis_index("c")
    return x, y, c, 2 * x + y


def _core_of_chip(p, c):
    return (p >> 1, p & 1, c)


def _place_cast(shard, lay, pos, name, after=None):
    rows, cols = shard.shape
    tr = _row_tile(rows, cols * 6)
    if lay.stacked:
        out = (lay.whole(), BF16, (None, tr, cols), lambda i, pos: (pos[0], i, 0))
    else:
        out = (lay.whole(), BF16, (tr, cols), lambda i, pos: (i, pos[0]))
    ins = [(shard, (tr, cols), lambda i, pos: (i, 0))]
    if after is not None:
        ins.append((after, (8, 128), lambda i, pos: (0, 0)))
    return _tiled(lambda a, *_: (a,), name, (rows // tr,), pos, ins, [out])[0]


def _place_cast_pair(top, bottom, lay, pos, name, after=None):
    rows, cols = top.shape
    ins = [(top, (rows, cols), lambda i, pos: (0, 0)), (bottom, (rows, cols), lambda i, pos: (0, 0))]
    if after is not None:
        ins.append((after, (8, 128), lambda i, pos: (0, 0)))
    return _tiled(lambda a, b, *_: (jnp.concatenate([a, b], axis=0),), name, (1,), pos, ins,
                  [(lay.whole(), BF16, (2 * rows, cols), lambda i, pos: (0, pos[0]))])[0]


def _adamw_pair(top, bottom, g, after=None):
    rows = top[0].shape[0]

    def body(*refs):
        (wa, ma, va, wb, mb, vb, g_ref), outs = refs[:7], refs[-8:]
        for (w, m, v), gg, o in (((wa, ma, va), g_ref[:rows], outs[:4]), ((wb, mb, vb), g_ref[rows:], outs[4:])):
            for o_ref, val in zip(o, (gg, *_adamw(w[...], gg, m[...], v[...]))):
                o_ref[...] = val

    behind = [] if after is None else [after[0:8, 0:128]]
    res = pl.pallas_call(
        body, name="adamw_w_br", out_shape=[jax.ShapeDtypeStruct(top[0].shape, F32)] * 8,
    )(*top, *bottom, g, *behind)
    return res[:4], res[4:]


def _remote(src, dst, send, recv, k, device):
    return pltpu.make_async_remote_copy(src_ref=src, dst_ref=dst, send_sem=send.at[k], recv_sem=recv.at[k],
                                        device_id=device, device_id_type=MESH)


def _arrival(dst, send, recv, k, me):
    return _remote(dst, dst, send, recv, k, me)


def _gather_ici(lay, name, q=0, nq=1):
    def plan(hbm, pos, send, recv, base):
        x, y, c, me = pos
        rows = lay.part_rows(c, q, nq)
        mine = lay.block(hbm[name], me, rows)
        starts = [_remote(mine, mine, send, recv, base + d - 1, _core_of_chip(me ^ d, c)) for d in (1, 2, 3)]
        waits = [_arrival(lay.block(hbm[name], me ^ d, rows), send, recv, base + d - 1, (x, y, c)) for d in (1, 2, 3)]
        return starts, waits
    return _Job(3, plan)


def _gather_d2d(lay, name, q=0, nq=1):
    def plan(hbm, pos, send, recv, base):
        x, y, c, me = pos
        starts, waits = [], []
        for d in (1, 2, 3):
            got = lay.block(hbm[name], me ^ d, lay.part_rows(c, q, nq))
            starts.append(_remote(got, got, send, recv, base + d - 1, (x, y, 1 - c)))
            waits.append(_arrival(lay.block(hbm[name], me ^ d, lay.part_rows(1 - c, q, nq)), send, recv, base + d - 1,
                                  (x, y, c)))
        return starts, waits
    return _Job(3, plan)


def _rs_pair(lay, grad, theirs):
    def plan(hbm, pos, send, recv, base):
        x, y, c, _ = pos
        out = _remote(lay.all_chips(hbm[grad], lay.half_rows(1 - c)), hbm[theirs], send, recv, base, (x, y, 1 - c))
        return [out], [_arrival(hbm[theirs], send, recv, base, (x, y, c))]
    return _Job(1, plan)


def _rs_chips(lay, sums, slots):
    def plan(hbm, pos, send, recv, base):
        x, y, c, me = pos
        starts = [_remote(lay.block(hbm[sums], me ^ d), hbm[slots].at[me], send, recv, base + d - 1,
                          _core_of_chip(me ^ d, c)) for d in (1, 2, 3)]
        waits = [_arrival(hbm[slots].at[me ^ d], send, recv, base + d - 1, (x, y, c)) for d in (1, 2, 3)]
        return starts, waits
    return _Job(3, plan)


def _rs_share(lay, shard):
    def plan(hbm, pos, send, recv, base):
        x, y, c, _ = pos
        mine = hbm[shard].at[lay.half_rows(c), :]
        other = hbm[shard].at[lay.half_rows(1 - c), :]
        return [_remote(mine, mine, send, recv, base, (x, y, 1 - c))], [_arrival(other, send, recv, base, (x, y, c))]
    return _Job(1, plan)


def _slots_shape(lay):
    return jax.ShapeDtypeStruct((N_CHIPS, lay.rows // 2, lay.cols), BF16)


def _theirs_shape(lay, dtype=BF16):
    return jax.ShapeDtypeStruct(lay.whole(lay.rows // 2), dtype)


def _pair_sum(grad, theirs, lay, pos, name):
    half = lay.rows // 2
    add = lambda a, b: (a.astype(F32) + b.astype(F32),)
    if lay.stacked:
        tr = _row_tile(half, lay.cols * 6)
        nt = half // tr
        flat = lambda a: a.reshape(-1, lay.cols)
        mine = lambda t, pos: ((t // nt) * (2 * nt) + pos[1] * nt + t % nt, 0)
        grid, blk = (N_CHIPS * nt,), (tr, lay.cols)
        grad, theirs = flat(grad), flat(theirs)
    else:
        tr = _row_tile(half, N_CHIPS * lay.cols * 6)
        nt = half // tr
        mine = lambda t, pos: (pos[1] * nt + t, 0)
        grid, blk = (nt,), (tr, N_CHIPS * lay.cols)
    same = lambda t, pos: (t, 0)
    out = _tiled(add, name, grid, pos, [(grad, blk, mine), (theirs, blk, same)], [(theirs.shape, BF16, blk, same)])[0]
    return out.reshape(lay.whole(half))


def _chip_sum(sums, slots, lay, pos, name, after=None):
    half = lay.rows // 2
    tr = _row_tile(half, lay.cols * 12)
    nt = half // tr
    blk3 = (None, tr, lay.cols)
    if lay.stacked:
        own = (sums, blk3, lambda i, pos: (pos[0], i, 0))
    else:
        own = (sums, (tr, lay.cols), lambda i, pos: (i, pos[0]))
    others = [(slots, blk3, functools.partial(lambda d, i, pos: (pos[0] ^ d, i, 0), d)) for d in (1, 2, 3)]

    def add(a, b1, b2, b3, *_):
        return (((a.astype(F32) + b1.astype(F32)) + b2.astype(F32)) + b3.astype(F32),)

    if after is not None:
        others.append((after, (8, 128), lambda i, pos: (0, 0)))
    return _tiled(add, name, (nt,), pos, [own] + others,
                  [((lay.rows, lay.cols), F32, (tr, lay.cols), lambda i, pos: (pos[1] * nt + i, 0))])[0]


N_DEV = 8


def _to_all(src, slots):
    def plan(hbm, pos, send, recv, base):
        x, y, c, _ = pos
        idx = 4 * x + 2 * y + c
        starts = [_remote(hbm[src], hbm[slots].at[idx], send, recv, base + k - 1,
                          (x ^ (k >> 2), y ^ ((k >> 1) & 1), c ^ (k & 1))) for k in range(1, N_DEV)]
        waits = [_arrival(hbm[slots].at[idx ^ k], send, recv, base + k - 1, (x, y, c)) for k in range(1, N_DEV)]
        return starts, waits
    return _Job(N_DEV - 1, plan)


def _pack_rows(parts):
    padded = [jnp.pad(a, ((0, -a.shape[0] % 8), (0, 0))) for a in parts]
    starts = [sum(p.shape[0] for p in padded[:k]) for k in range(len(padded))]
    return jnp.concatenate(padded, axis=0), starts


def kernel(x, mix_norm, w_in, b_in, sinks, conv_w, w_attn_branch, w_conv_branch, w_out, ffn_norm, w_up, ffn_conv_w, w_down, final_norm, loss_target, m_mix_norm, m_w_in, m_b_in, m_sinks, m_conv_w, m_w_attn_branch, m_w_conv_branch, m_w_out, m_ffn_norm, m_w_up, m_ffn_conv_w, m_w_down, m_final_norm, v_mix_norm, v_w_in, v_b_in, v_sinks, v_conv_w, v_w_attn_branch, v_w_conv_branch, v_w_out, v_ffn_norm, v_w_up, v_ffn_conv_w, v_w_down, v_final_norm):
    me = 2 * lax.axis_index("x") + lax.axis_index("y")
    names = ("w_in", "w_br", "w_out", "w_up", "w_down")
    w_of = dict(w_in=w_in[0].T, w_out=w_out[0], w_up=w_up[0], w_down=w_down[0])
    m_of = dict(w_in=m_w_in[0].T, w_out=m_w_out[0], w_up=m_w_up[0], w_down=m_w_down[0])
    v_of = dict(w_in=v_w_in[0].T, w_out=v_w_out[0], w_up=v_w_up[0], w_down=v_w_down[0])
    ab = (w_attn_branch[0], m_w_attn_branch[0], v_w_attn_branch[0])
    cb = (w_conv_branch[0], m_w_conv_branch[0], v_w_conv_branch[0])

    pos = jnp.stack([me, lax.axis_index("c")]).astype(jnp.int32)

    lay = dict(zip(names, BIG))
    xs, target, sk = x[0], loss_target[0], sinks[0]
    s = xs.shape[0]
    tm, tm2, bk, bk2 = min(256, s), min(512, s), min(1024, s), min(2048, s)

    taps, (_, t0) = _pack_rows([conv_w[0], ffn_conv_w[0].reshape(3 * (FF2 // N_CHIPS // 128), 128)])
    placed = {"w_in": _place_cast(w_of["w_in"], lay["w_in"], pos, "cast_w_in")}
    fly_in, started = _start_exchange("gather_in_start", [_gather_ici(lay["w_in"], "w_in")], {"w_in": placed["w_in"]})
    taps_flight, started = _start_exchange("taps_start", [_to_all("v", "slots")],
                                           {"v": taps + started[0:1], "slots": jnp.zeros((N_DEV, *taps.shape), F32)})
    placed["w_br"] = _place_cast_pair(ab[0], cb[0], lay["w_br"], pos, "cast_w_br", after=started)
    for n in names[2:]:
        placed[n] = _place_cast(w_of[n], lay[n], pos, "cast_" + n, after=started)
    trio = ("w_br", "w_out")
    (fly_trio, fly_up, fly_down), started = _start_exchanges("gather_rest_start", [
        ([_gather_ici(lay[n], n) for n in ws], {n: placed[n] for n in ws}) for ws in (trio, ("w_up",), ("w_down",))])

    got = _finish_exchange("gather_in_wait", fly_in, after=started)
    w_in_full = _exchange("gather_in_d2d", [[_gather_d2d(lay["w_in"], "w_in")]], bufs=got)["w_in"].reshape(IN_W, D_MODEL)
    xn, qkv, c3, gates = _inproj_fwd(xs, mix_norm, w_in_full, b_in, tm2)
    k2 = _Carry([_gather_d2d(lay[n], n) for n in trio], bufs=_finish_exchange("gather_trio_wait", fly_trio, after=qkv))
    attn = _attn_fwd(qkv, sk, comm=k2)
    w_br = k2.out["w_br"]
    w_out_full = k2.out["w_out"].reshape(D_MODEL, D_MODEL)
    k3 = _Carry([_gather_d2d(lay["w_up"], "w_up")], bufs=_finish_exchange("gather_up_wait", fly_up, after=attn))
    taps = _finish_exchange("taps_wait", taps_flight, after=attn)
    taps = lax.dynamic_update_slice(taps["slots"], taps["v"][None], (2 * me + lax.axis_index("c"), 0, 0))
    conv_full = taps[0::2, 0:3].transpose(1, 0, 2).reshape(3, CONV_W)
    ffn_cw_full = taps[0::2, t0:t0 + 33].reshape(N_CHIPS, 3, FF2 // N_CHIPS).transpose(1, 0, 2).reshape(3, FF2)
    conv, a, cv, merged, h1, hn = _mix_fwd(xs, attn, c3, gates, conv_full, w_br, w_out_full, ffn_norm, tm2, comm=k3)
    w_up_full = k3.out["w_up"]
    w_down_full = _exchange("gather_down_d2d", [[_gather_d2d(lay["w_down"], "w_down")]],
                            bufs=_finish_exchange("gather_down_wait", fly_down, after=hn))["w_down"].reshape(D_FF, D_MODEL)
    u, up, act, dh2, loss_part, g_fn = _ffn_fwd_loss(hn, h1, w_up_full, ffn_cw_full, w_down_full,
                                                     final_norm[None, :], target, tm)

    grads, sums, slots = {}, {}, {}

    def pair(*ws):
        return _Carry([_rs_pair(lay[n], "g_" + n, "t_" + n) for n in ws], reads={"g_" + n: grads[n] for n in ws},
                      fresh={"t_" + n: _theirs_shape(lay[n], grads[n].dtype) for n in ws})

    def chips(*ws, also=None):
        k = _Carry([_rs_chips(lay[n], "s_" + n, "r_" + n) for n in ws], reads={"s_" + n: sums[n] for n in ws},
                   fresh={"r_" + n: _slots_shape(lay[n]) for n in ws})
        if also is not None:
            k = _Carry(k.jobs + also.jobs, {**k.reads, **also.reads}, None, {**k.fresh, **also.fresh})
        return k

    def pair_sums(k, *ws):
        for n in ws:
            sums[n] = _pair_sum(grads[n], k.out["t_" + n], lay[n], pos, "pair_sum_" + n)

    def take_slots(k, *ws):
        for n in ws:
            slots[n] = k.out["r_" + n]

    du, dh1, g_fcw, g_g2 = _ffn_bwd(dh2, u, up, h1, w_up_full, ffn_cw_full, w_down_full, ffn_norm, tm)
    grads["w_down"] = _wgrad(act, dh2, D_FF // 2, D_MODEL, bk2, "wgrad_down").reshape(lay["w_down"].whole())
    k4 = pair("w_down")
    grads["w_up"] = _wgrad(hn, du, D_MODEL, FF2 // 4, bk2, "wgrad_up", comm=k4)
    pair_sums(k4, "w_down")
    k5 = chips("w_down", also=pair("w_up"))
    dattn, dc3, dgt, g_cw, grads["w_br"], gw_out = _mix_bwd(
        dh1, gates, a, cv, c3, attn, conv, merged, conv_full, w_br, w_out_full, tm2, comm=k5)
    grads["w_out"] = gw_out.reshape(lay["w_out"].whole())
    take_slots(k5, "w_down")
    pair_sums(k5, "w_up")
    up_flight, started = _start_exchange("rs_chips_up_start", [_rs_chips(lay["w_up"], "s", "r")],
                                         {"s": sums["w_up"], "r": _slots_shape(lay["w_up"])})
    k6 = pair(*trio)
    k6.reads["after"] = started
    dq, dk_even, dk_odd, dv_even, dv_odd, g_sk = _attn_bwd(qkv, sk, attn, dattn, comm=k6)
    pair_sums(k6, *trio)
    trio_flight, started = _start_exchange(
        "rs_chips_trio_start", [_rs_chips(lay[n], "s_" + n, "r_" + n) for n in trio],
        {**{"s_" + n: sums[n] for n in trio}, **{"r_" + n: _slots_shape(lay[n]) for n in trio}})
    behind = mix_norm + jnp.tile(started[0:1], (1, D_MODEL // 128))
    grad_x, gw_in, g_b, g_g1 = _inproj_bwd(dq, (dk_even, dk_odd), (dv_even, dv_odd), dc3, dgt, w_in_full, xs, xn,
                                           dh1, behind)
    grads["w_in"] = gw_in.reshape(lay["w_in"].whole())

    in_flight, started = _start_exchange("rs_pair_in_start", [_rs_pair(lay["w_in"], "g", "t")],
                                         {"g": grads["w_in"], "t": _theirs_shape(lay["w_in"])})
    parts = [loss_part, g_g1, g_b, jnp.pad(g_sk[:, 0], (0, 120))[None, :], g_cw, g_g2, g_fcw, g_fn]
    packed, at = _pack_rows([p.reshape(-1, 128) for p in parts])
    small_flight, started = _start_exchange("small_start", [_to_all("v", "slots")],
                                            {"v": packed + started[0:1], "slots": jnp.zeros((N_DEV, *packed.shape), F32)})
    halves = {"w_down": _chip_sum(sums["w_down"], slots["w_down"], lay["w_down"], pos, "chip_sum_w_down", after=started)}
    landed = _finish_exchange("rs_chips_up_wait", up_flight, after=halves["w_down"])
    halves["w_up"] = _chip_sum(landed["s"], landed["r"], lay["w_up"], pos, "chip_sum_w_up")
    landed = _finish_exchange("rs_pair_in_wait", in_flight, after=halves["w_up"])
    sums["w_in"] = _pair_sum(landed["g"], landed["t"], lay["w_in"], pos, "pair_sum_w_in")
    (in_flight, down_flight, up_flight), started = _start_exchanges("rs_chips_in_start", [
        ([_rs_chips(lay["w_in"], "s", "r")], {"s": sums["w_in"], "r": _slots_shape(lay["w_in"])}),
        ([_rs_share(lay["w_down"], "w_down")], {"w_down": halves["w_down"]}),
        ([_rs_share(lay["w_up"], "w_up")], {"w_up": halves["w_up"]})])
    landed = _finish_exchange("rs_chips_trio_wait", trio_flight, after=started)
    for n in trio:
        halves[n] = _chip_sum(landed["s_" + n], landed["r_" + n], lay[n], pos, "chip_sum_" + n)
    shared = _exchange("share_halves", [[_rs_share(lay[n], n) for n in trio]], bufs={n: halves[n] for n in trio})
    shared["w_down"] = _finish_exchange("share_down_wait", down_flight, after=shared[trio[-1]])["w_down"]
    shared["w_up"] = _finish_exchange("share_up_wait", up_flight, after=shared["w_down"])["w_up"]

    def adam(n, g, after=None):
        return _rowwise(lambda w, g, m, v: (g, *_adamw(w, g, m, v)), [w_of[n], g, m_of[n], v_of[n]], [F32] * 4,
                        "adamw_" + n, after=after)

    new_of, last = {}, None
    for n in ("w_down", "w_up", "w_out"):
        new_of[n] = adam(n, shared[n], last)
        last = new_of[n][1]
    new_of["w_ab"], new_of["w_cb"] = _adamw_pair(ab, cb, shared["w_br"], after=last)
    last = new_of["w_cb"][1]

    arrived = _finish_exchange("small_wait", small_flight, after=last)
    flat = lambda k: lambda t, j, chip: at[k] + j
    mine = lambda k, per_tap: lambda t, j, chip: at[k] + per_tap * t + (per_tap // N_CHIPS) * chip + j
    rows = lambda a: a.reshape(a.shape[1], 1, a.shape[2])
    small_p = [
        (mix_norm, m_mix_norm, v_mix_norm, flat(1)), (b_in, m_b_in, v_b_in, flat(2)), (sinks, m_sinks, v_sinks, flat(3)),
        (rows(conv_w), rows(m_conv_w), rows(v_conv_w), mine(4, CONV_W // 128)),
        (ffn_norm, m_ffn_norm, v_ffn_norm, flat(5)),
        (rows(ffn_conv_w), rows(m_ffn_conv_w), rows(v_ffn_conv_w), mine(6, FF2 // 128)),
        (final_norm[None, :], m_final_norm[None, :], v_final_norm[None, :], flat(7))]
    small_new = _adamw_small(pos, arrived["v"], arrived["slots"], small_p)
    loss = small_new[0][0, 0]
    small_g = small_new[1::4]
    small_new = [small_new[4 * k + 2:4 * k + 5] for k in range(len(small_p))]

    landed = _finish_exchange("rs_chips_in_wait", in_flight, after=small_new[0][0])
    half_in = _chip_sum(landed["s"], landed["r"], lay["w_in"], pos, "chip_sum_w_in")
    shared["w_in"] = _exchange("share_in", [[_rs_share(lay["w_in"], "w_in")]], bufs={"w_in": half_in})["w_in"]
    new_of["w_in"] = [a.T for a in adam("w_in", shared["w_in"])]
    big = ("w_in", "w_ab", "w_cb", "w_out", "w_up", "w_down")
    big_g = [new_of[n][0] for n in big]
    big_new = [new_of[n][1:] for n in big]

    order = [("s", 0), ("b", 0), ("s", 1), ("s", 2), ("s", 3), ("b", 1), ("b", 2), ("b", 3), ("s", 4), ("b", 4),
             ("s", 5), ("b", 5), ("s", 6)]
    shapes = [mix_norm.shape, w_in.shape, b_in.shape, sinks.shape, conv_w.shape, w_attn_branch.shape,
              w_conv_branch.shape, w_out.shape, ffn_norm.shape, w_up.shape, ffn_conv_w.shape, w_down.shape,
              final_norm.shape]
    out_g = [(small_g[k] if kind == "s" else big_g[k]).reshape(shp) for (kind, k), shp in zip(order, shapes)]
    news = [[(small_new[k][j] if kind == "s" else big_new[k][j]).reshape(shp) for (kind, k), shp in zip(order, shapes)]
            for j in range(3)]
    return (loss, grad_x[None], *out_g, *news[0], *news[1], *news[2])
```

```python
import functools

import jax
import jax.numpy as jnp
from jax import lax
from jax.experimental import pallas as pl
from jax.experimental.pallas import tpu as pltpu

F32 = jnp.float32
BF16 = jnp.bfloat16

D_MODEL = 1024
HEAD_DIM = 64
N_HEADS = 8
N_KV_HEADS = 2
GROUP = N_HEADS // N_KV_HEADS
BLOCK = 128
ATTN_SCALE = HEAD_DIM ** -0.5
ATTN_W = N_HEADS * HEAD_DIM
KV_W = N_KV_HEADS * HEAD_DIM
CONV_W = 512
QKV_W = ATTN_W + 2 * KV_W
C3_W = 3 * CONV_W
GATES_W = 2 * D_MODEL
IN_W = QKV_W + C3_W + GATES_W
D_FF = 2816
FF2 = 2 * D_FF
NORM_EPS = 1e-5
N_CHIPS = 4
IN_SHARD = IN_W // N_CHIPS
NEG = -1e30

ADAM_LR = 0.001
ADAM_B1 = 0.9
ADAM_B2 = 0.999
ADAM_EPS = 1e-08
ADAM_WD = 0.01
ADAM_STEP = 10

VMEM_LIMIT = 56 * 1024 * 1024
MESH = pl.DeviceIdType.MESH

NT = (((1,), (1,)), ((), ()))
TN = (((0,), (0,)), ((), ()))


def _params(*sem):
    return pltpu.CompilerParams(dimension_semantics=sem, vmem_limit_bytes=VMEM_LIMIT)


def _resident(shape):
    return pl.BlockSpec(shape, lambda *_: (0,) * len(shape), pipeline_mode=pl.Buffered(1))


def _sigmoid(v):
    return 0.5 * jnp.tanh(0.5 * v) + 0.5


def _rstd(v):
    return lax.rsqrt(jnp.mean(v * v, axis=-1, keepdims=True) + NORM_EPS)


def _rms_bwd(dy, v, rstd, g):
    vhat = v * rstd
    t = dy * g
    return rstd * (t - vhat * jnp.mean(t * vhat, axis=-1, keepdims=True)), dy * vhat


def _taps(z, cw):
    return cw[2:3] * z + cw[1:2] * pltpu.roll(z, 1, 0) + cw[0:1] * pltpu.roll(z, 2, 0)


def _causal_conv(z, prev, cw):
    edge = _taps(jnp.concatenate([prev, z[0:8]], axis=0), cw)
    return jnp.concatenate([edge[8:16], _taps(z, cw)[8:]], axis=0)


def _rows_after(z, nxt):
    n = z.shape[0]
    edge = jnp.concatenate([z[n - 8:n], nxt], axis=0)
    return tuple(jnp.concatenate([pltpu.roll(z, n - k, 0)[:n - 8], pltpu.roll(edge, 16 - k, 0)[0:8]], axis=0)
                 for k in (1, 2))


def _inproj_fwd(x, g1, w_in, b_in, tm, comm=None):
    s = x.shape[0]

    def body(x_ref, g_ref, w_ref, b_ref, xn_ref, qkv_ref, c3_ref, gt_ref):
        xf = x_ref[...]
        xn = (xf * _rstd(xf) * g_ref[...]).astype(BF16)
        xn_ref[...] = xn

        proj = (lax.dot_general(xn, w_ref[...], NT, preferred_element_type=F32) + b_ref[...]).astype(BF16)
        qkv_ref[...] = proj[:, :QKV_W]
        c3_ref[...] = proj[:, QKV_W:QKV_W + C3_W]
        gt_ref[...] = proj[:, QKV_W + C3_W:]

    row = lambda w: pl.BlockSpec((tm, w), lambda i: (i, 0))
    return _call(
        comm, body, name="inproj_fwd", grid=(s // tm,),
        in_specs=[row(D_MODEL), _resident((1, D_MODEL)), _resident((IN_W, D_MODEL)), _resident((1, IN_W))],
        out_specs=[row(D_MODEL), row(QKV_W), row(C3_W), row(GATES_W)],
        out_shape=[jax.ShapeDtypeStruct((s, D_MODEL), BF16), jax.ShapeDtypeStruct((s, QKV_W), BF16),
                   jax.ShapeDtypeStruct((s, C3_W), BF16), jax.ShapeDtypeStruct((s, GATES_W), BF16)],
        compiler_params=_params("parallel"),
    )(x, g1, w_in, b_in)


def _attn_bias():
    kj = jnp.arange(2 * BLOCK)[:, None]
    qi = (jnp.arange(GROUP * BLOCK) % BLOCK)[None, :]
    band = (kj > qi) & (kj <= qi + BLOCK)
    return jnp.stack([jnp.where(band & (kj >= BLOCK), 0.0, NEG), jnp.where(band, 0.0, NEG)]).astype(F32)


def _attn_bias_specs():
    shape = (None, 2 * BLOCK, GROUP * BLOCK)
    return pl.BlockSpec(shape, lambda i: (jnp.minimum(i, 1), 0, 0)), pl.BlockSpec(shape, lambda i: (1, 0, 0))


def _sink_row(sk_ref, h):
    lane = lax.broadcasted_iota(jnp.int32, (1, GROUP * BLOCK), 1)
    row = jnp.full((1, GROUP * BLOCK), sk_ref[h * GROUP], F32)
    for g in range(1, GROUP):
        row = jnp.where(lane >= g * BLOCK, sk_ref[h * GROUP + g], row)
    return row


def _stack_heads(t, h):
    return jnp.concatenate(
        [t[:, (h * GROUP + g) * HEAD_DIM:(h * GROUP + g + 1) * HEAD_DIM] for g in range(GROUP)], axis=0)


def _unstack_heads(per_kv):
    return jnp.concatenate(
        [t[g * BLOCK:(g + 1) * BLOCK] for t in per_kv for g in range(GROUP)], axis=1)


def _block_specs(n, steps):
    cur = lambda i: jnp.minimum(i, steps - 1)
    prev = lambda i: jnp.maximum(n * jnp.minimum(i, steps - 1) - 1, 0)
    kv = ATTN_W // KV_W
    return (pl.BlockSpec((n * BLOCK, ATTN_W), lambda i: (cur(i), 0)),
            pl.BlockSpec((BLOCK, KV_W), lambda i: (prev(i), kv)), pl.BlockSpec((n * BLOCK, KV_W), lambda i: (cur(i), kv)),
            pl.BlockSpec((BLOCK, KV_W), lambda i: (prev(i), kv + 1)),
            pl.BlockSpec((n * BLOCK, KV_W), lambda i: (cur(i), kv + 1)))


def _attn_fwd(qkv, sinks, comm=None):
    s = qkv.shape[0]
    n = min(4, s // BLOCK)
    steps = s // (n * BLOCK)

    def body(sk_ref, bias0_ref, bias1_ref, q_ref, kp_ref, kc_ref, vp_ref, vc_ref, o_ref):
        kc, vc = kc_ref[...], vc_ref[...]
        for b in range(n):
            rows, before = slice(b * BLOCK, (b + 1) * BLOCK), slice((b - 1) * BLOCK, b * BLOCK)
            kp, vp = (kp_ref[...], vp_ref[...]) if b == 0 else (kc[before], vc[before])
            q, bias = q_ref[rows, :], (bias0_ref if b == 0 else bias1_ref)[...]
            outs = []
            for h in range(N_KV_HEADS):
                hs = slice(h * HEAD_DIM, (h + 1) * HEAD_DIM)
                k2 = jnp.concatenate([kp[:, hs], kc[rows, hs]], axis=0)
                v2 = jnp.concatenate([vp[:, hs], vc[rows, hs]], axis=0)
                sc = lax.dot_general(k2, _stack_heads(q, h), NT, preferred_element_type=F32) * ATTN_SCALE + bias
                sink = _sink_row(sk_ref, h)
                m = jnp.maximum(jnp.max(sc, axis=0, keepdims=True), sink)
                p = jnp.exp(sc - m)
                den = jnp.sum(p, axis=0, keepdims=True) + jnp.exp(sink - m)
                out = lax.dot_general(v2, p.astype(BF16), TN, preferred_element_type=F32) / den
                outs.append(out.T)
            o_ref[rows, :] = _unstack_heads(outs).astype(BF16)

    return _call(
        comm, body, name="attn_fwd", grid=(steps,),
        in_specs=[pl.BlockSpec(memory_space=pltpu.SMEM), *_attn_bias_specs(), *_block_specs(n, steps)],
        out_specs=pl.BlockSpec((n * BLOCK, ATTN_W), lambda i: (i, 0)),
        out_shape=jax.ShapeDtypeStruct((s, ATTN_W), BF16),
        compiler_params=_params("parallel"),
    )(sinks, _attn_bias(), _attn_bias(), qkv, qkv, qkv, qkv, qkv)


def _mix_fwd(x, attn, c3, gates, conv_w, w_br, w_out, g2, tm, comm=None):
    s = x.shape[0]

    def body(x_ref, at_ref, c3_ref, gt_ref, cw_ref, wbr_ref, wo_ref, g_ref,
             conv_ref, a_ref, cv_ref, mg_ref, h1_ref, hn_ref, carry_ref):
        @pl.when(pl.program_id(0) == 0)
        def _():
            carry_ref[...] = jnp.zeros_like(carry_ref)

        c3v = c3_ref[...].astype(F32)
        cb, cc, cx = c3v[:, :CONV_W], c3v[:, CONV_W:2 * CONV_W], c3v[:, 2 * CONV_W:]
        z = cc * cx
        cz = _causal_conv(z, carry_ref[...], cw_ref[...])
        carry_ref[...] = z[tm - 8:tm]
        conv = (cb * cz).astype(BF16)
        conv_ref[...] = conv
        a = jnp.dot(at_ref[...], wbr_ref[:ATTN_W, :], preferred_element_type=F32)
        cv = jnp.dot(conv, wbr_ref[ATTN_W:, :], preferred_element_type=F32)
        a_ref[...] = a.astype(BF16)
        cv_ref[...] = cv.astype(BF16)
        gt = gt_ref[...].astype(F32)
        merged = (_sigmoid(gt[:, :D_MODEL]) * a + _sigmoid(gt[:, D_MODEL:]) * cv).astype(BF16)
        mg_ref[...] = merged
        h1 = x_ref[...] + jnp.dot(merged, wo_ref[...], preferred_element_type=F32)
        h1_ref[...] = h1
        hn_ref[...] = (h1 * _rstd(h1) * g_ref[...]).astype(BF16)

    row = lambda w: pl.BlockSpec((tm, w), lambda i: (i, 0))
    return _call(
        comm, body, name="mix_fwd", grid=(s // tm,),
        in_specs=[row(D_MODEL), row(ATTN_W), row(C3_W), row(GATES_W), _resident((3, CONV_W)),
                  _resident((ATTN_W + CONV_W, D_MODEL)), _resident((D_MODEL, D_MODEL)), _resident((1, D_MODEL))],
        out_specs=[row(CONV_W), row(D_MODEL), row(D_MODEL), row(D_MODEL), row(D_MODEL), row(D_MODEL)],
        out_shape=[jax.ShapeDtypeStruct((s, CONV_W), BF16), jax.ShapeDtypeStruct((s, D_MODEL), BF16),
                   jax.ShapeDtypeStruct((s, D_MODEL), BF16), jax.ShapeDtypeStruct((s, D_MODEL), BF16),
                   jax.ShapeDtypeStruct((s, D_MODEL), F32), jax.ShapeDtypeStruct((s, D_MODEL), BF16)],
        scratch_shapes=[pltpu.VMEM((8, CONV_W), F32)],
        compiler_params=_params("arbitrary"),
    )(x, attn, c3, gates, conv_w, w_br, w_out, g2)


def _ffn_fwd_loss(hn, h1, w_up, ffn_cw, w_down, g3, target, tm):
    s = hn.shape[0]

    def body(hn_ref, h1_ref, wu_ref, cw_ref, wd_ref, g_ref, t_ref,
             u_ref, up_ref, act_ref, dh2_ref, loss_ref, gfn_ref, carry_ref):
        @pl.when(pl.program_id(0) == 0)
        def _():
            carry_ref[...] = jnp.zeros_like(carry_ref)
            loss_ref[...] = jnp.zeros_like(loss_ref)
            gfn_ref[...] = jnp.zeros_like(gfn_ref)

        u = jnp.dot(hn_ref[...], wu_ref[...], preferred_element_type=F32)
        u_ref[...] = u.astype(BF16)
        up = _causal_conv(u, carry_ref[...], cw_ref[...])
        up_ref[...] = up
        carry_ref[...] = u[tm - 8:tm]
        gate, val = up[:, :D_FF], up[:, D_FF:]
        act = (gate * _sigmoid(gate) * val).astype(BF16)
        act_ref[...] = act
        h2 = h1_ref[...] + jnp.dot(act, wd_ref[...], preferred_element_type=F32)
        rstd = _rstd(h2)
        g = g_ref[...]
        err = h2 * rstd * g - t_ref[...]
        loss_ref[...] += jnp.sum(err * err) * (0.5 / D_MODEL)
        dh2, dg = _rms_bwd(err * (1.0 / D_MODEL), h2, rstd, g)
        dh2_ref[...] = dh2
        gfn_ref[...] += jnp.sum(dg, axis=0, keepdims=True)

    row = lambda w: pl.BlockSpec((tm, w), lambda i: (i, 0))
    acc = lambda w: pl.BlockSpec((1, w), lambda i: (0, 0))
    return pl.pallas_call(
        body, name="ffn_fwd_loss", grid=(s // tm,),
        in_specs=[row(D_MODEL), row(D_MODEL), _resident((D_MODEL, FF2)), _resident((3, FF2)),
                  _resident((D_FF, D_MODEL)), _resident((1, D_MODEL)), row(D_MODEL)],
        out_specs=[row(FF2), row(FF2), row(D_FF), row(D_MODEL), acc(128), acc(D_MODEL)],
        out_shape=[jax.ShapeDtypeStruct((s, FF2), BF16), jax.ShapeDtypeStruct((s, FF2), F32),
                   jax.ShapeDtypeStruct((s, D_FF), BF16),
                   jax.ShapeDtypeStruct((s, D_MODEL), F32), jax.ShapeDtypeStruct((1, 128), F32),
                   jax.ShapeDtypeStruct((1, D_MODEL), F32)],
        scratch_shapes=[pltpu.VMEM((8, FF2), F32)],
        compiler_params=_params("arbitrary"),
    )(hn, h1, w_up, ffn_cw, w_down, g3, target)


def _ffn_bwd(dh2, u, up, h1, w_up, ffn_cw, w_down, g2, tm):
    s = dh2.shape[0]
    nt = s // tm

    def body(dh2_ref, u_ref, up_ref, h1_ref, wu_ref, cw_ref, wd_ref, g_ref,
             du_ref, dh1_ref, gcw_ref, gg_ref, carry_ref):
        @pl.when(pl.program_id(0) == 0)
        def _():
            for ref in (carry_ref, gcw_ref, gg_ref):
                ref[...] = jnp.zeros_like(ref)

        dh2v = dh2_ref[...]
        dact = lax.dot_general(dh2v.astype(BF16), wd_ref[...], NT, preferred_element_type=F32)
        upv = up_ref[...]
        gate, val = upv[:, :D_FF], upv[:, D_FF:]
        sg = _sigmoid(gate)
        dval = dact * (gate * sg)
        dgate = dact * val * (sg * (1.0 + gate * (1.0 - sg)))
        dup = jnp.concatenate([dgate, dval], axis=1)
        dup1, dup2 = _rows_after(dup, carry_ref[...])
        carry_ref[...] = dup[0:8]
        u = u_ref[...].astype(F32)
        gcw_ref[2:3, :] += jnp.sum(dup * u, axis=0, keepdims=True)
        gcw_ref[1:2, :] += jnp.sum(dup1 * u, axis=0, keepdims=True)
        gcw_ref[0:1, :] += jnp.sum(dup2 * u, axis=0, keepdims=True)
        cw = cw_ref[...]
        du = (cw[2:3] * dup + cw[1:2] * dup1 + cw[0:1] * dup2).astype(BF16)
        du_ref[...] = du
        dhn = lax.dot_general(du, wu_ref[...], NT, preferred_element_type=F32)
        h1v = h1_ref[...]
        dh1, dg = _rms_bwd(dhn, h1v, _rstd(h1v), g_ref[...])
        dh1_ref[...] = dh2v + dh1
        gg_ref[...] += jnp.sum(dg, axis=0, keepdims=True)

    row = lambda w: pl.BlockSpec((tm, w), lambda i: (nt - 1 - i, 0))
    return pl.pallas_call(
        body, name="ffn_bwd", grid=(nt,),
        in_specs=[row(D_MODEL), row(FF2), row(FF2),
                  row(D_MODEL), _resident((D_MODEL, FF2)), _resident((3, FF2)), _resident((D_FF, D_MODEL)),
                  _resident((1, D_MODEL))],
        out_specs=[row(FF2), row(D_MODEL), pl.BlockSpec((3, FF2), lambda i: (0, 0)),
                   pl.BlockSpec((1, D_MODEL), lambda i: (0, 0))],
        out_shape=[jax.ShapeDtypeStruct((s, FF2), BF16), jax.ShapeDtypeStruct((s, D_MODEL), F32),
                   jax.ShapeDtypeStruct((3, FF2), F32), jax.ShapeDtypeStruct((1, D_MODEL), F32)],
        scratch_shapes=[pltpu.VMEM((8, FF2), F32)],
        compiler_params=_params("arbitrary"),
    )(dh2, u, up, h1, w_up, ffn_cw, w_down, g2)


def _mix_bwd(dh1, gates, a, cv, c3, attn, conv, merged, conv_w, w_br, w_out, tm, comm=None):
    s = dh1.shape[0]
    nt = s // tm
    halo = 16

    def body(dh1_ref, gt_ref, a_ref, cv_ref, c3_ref, ch_ref, at_ref, cn_ref, mg_ref, cw_ref, wbr_ref,
             wo_ref, dat_ref, dc3_ref, dgt_ref, gcw_ref, gbr_ref, gout_ref, carry_ref, br_acc, out_acc):
        i = pl.program_id(0)

        @pl.when(i == 0)
        def _():
            for ref in (carry_ref, gcw_ref, br_acc, out_acc):
                ref[...] = jnp.zeros_like(ref)

        dh1v = dh1_ref[...].astype(BF16)
        out_acc[...] += lax.dot_general(mg_ref[...], dh1v, TN, preferred_element_type=F32)
        dm = lax.dot_general(dh1v, wo_ref[...], NT, preferred_element_type=F32)
        gt = gt_ref[...].astype(F32)
        sa, sc = _sigmoid(gt[:, :D_MODEL]), _sigmoid(gt[:, D_MODEL:])
        da = (dm * sa).astype(BF16)
        dcv = (dm * sc).astype(BF16)
        br_acc[:ATTN_W, :] += lax.dot_general(at_ref[...], da, TN, preferred_element_type=F32)
        br_acc[ATTN_W:, :] += lax.dot_general(cn_ref[...], dcv, TN, preferred_element_type=F32)
        dgt_ref[...] = jnp.concatenate(
            [dm * a_ref[...].astype(F32) * (sa * (1.0 - sa)), dm * cv_ref[...].astype(F32) * (sc * (1.0 - sc))],
            axis=1).astype(BF16)
        dat_ref[...] = lax.dot_general(da, wbr_ref[:ATTN_W, :], NT, preferred_element_type=F32).astype(BF16)
        dconv = lax.dot_general(dcv, wbr_ref[ATTN_W:, :], NT, preferred_element_type=F32)
        c3v = c3_ref[...].astype(F32)
        cb, cc, cx = c3v[:, :CONV_W], c3v[:, CONV_W:2 * CONV_W], c3v[:, 2 * CONV_W:]
        z = cc * cx
        chv = ch_ref[...].astype(F32)[halo - 8:halo] * (i < nt - 1).astype(F32)
        zh = chv[:, CONV_W:2 * CONV_W] * chv[:, 2 * CONV_W:]
        cw = cw_ref[...]
        cz = _causal_conv(z, zh, cw)
        dcz = dconv * cb
        dcz1, dcz2 = _rows_after(dcz, carry_ref[...])
        carry_ref[...] = dcz[0:8]
        gcw_ref[2:3, :] += jnp.sum(dcz * z, axis=0, keepdims=True)
        gcw_ref[1:2, :] += jnp.sum(dcz1 * z, axis=0, keepdims=True)
        gcw_ref[0:1, :] += jnp.sum(dcz2 * z, axis=0, keepdims=True)
        dz = cw[2:3] * dcz + cw[1:2] * dcz1 + cw[0:1] * dcz2
        dc3_ref[...] = jnp.concatenate([dconv * cz, dz * cx, dz * cc], axis=1).astype(BF16)

        @pl.when(i == nt - 1)
        def _():
            gbr_ref[...] = br_acc[...].astype(BF16)
            gout_ref[...] = out_acc[...].astype(BF16)

    row = lambda w: pl.BlockSpec((tm, w), lambda i: (nt - 1 - i, 0))
    return _call(
        comm, body, name="mix_bwd", grid=(nt,),
        in_specs=[row(D_MODEL), row(GATES_W), row(D_MODEL), row(D_MODEL), row(C3_W),
                  pl.BlockSpec((halo, C3_W), lambda i: (jnp.maximum((nt - 1 - i) * (tm // halo) - 1, 0), 0)),
                  row(ATTN_W), row(CONV_W), row(D_MODEL), _resident((3, CONV_W)),
                  _resident((ATTN_W + CONV_W, D_MODEL)), _resident((D_MODEL, D_MODEL))],
        out_specs=[row(ATTN_W), row(C3_W), row(GATES_W), pl.BlockSpec((3, CONV_W), lambda i: (0, 0)),
                   _resident((ATTN_W + CONV_W, D_MODEL)), _resident((D_MODEL, D_MODEL))],
        out_shape=[jax.ShapeDtypeStruct((s, ATTN_W), BF16), jax.ShapeDtypeStruct((s, C3_W), BF16),
                   jax.ShapeDtypeStruct((s, GATES_W), BF16), jax.ShapeDtypeStruct((3, CONV_W), F32),
                   jax.ShapeDtypeStruct((ATTN_W + CONV_W, D_MODEL), BF16),
                   jax.ShapeDtypeStruct((D_MODEL, D_MODEL), BF16)],
        scratch_shapes=[pltpu.VMEM((8, CONV_W), F32), pltpu.VMEM((ATTN_W + CONV_W, D_MODEL), F32),
                        pltpu.VMEM((D_MODEL, D_MODEL), F32)],
        compiler_params=_params("arbitrary"),
    )(dh1, gates, a, cv, c3, c3, attn, conv, merged, conv_w, w_br, w_out)


def _attn_bwd(qkv, sinks, o, do, comm=None):
    s = qkv.shape[0]
    npair = s // (2 * BLOCK)

    def one_block(sk_ref, bias, q, kp, kc, vp, vc, ov, dov, dsk_ref):
        dqs, dks, dvs = [], [], []
        for h in range(N_KV_HEADS):
            hs = slice(h * HEAD_DIM, (h + 1) * HEAD_DIM)
            k2 = jnp.concatenate([kp[:, hs], kc[:, hs]], axis=0)
            v2 = jnp.concatenate([vp[:, hs], vc[:, hs]], axis=0)
            qg, og, dog = _stack_heads(q, h), _stack_heads(ov, h), _stack_heads(dov, h)
            sc = lax.dot_general(k2, qg, NT, preferred_element_type=F32) * ATTN_SCALE + bias
            sink = _sink_row(sk_ref, h)
            m = jnp.maximum(jnp.max(sc, axis=0, keepdims=True), sink)
            p = jnp.exp(sc - m)
            psink = jnp.exp(sink - m)
            inv = 1.0 / (jnp.sum(p, axis=0, keepdims=True) + psink)
            p = p * inv
            delta = jnp.sum(dog.astype(F32) * og.astype(F32), axis=1, keepdims=True).T
            dp = lax.dot_general(v2, dog, NT, preferred_element_type=F32)
            ds = (p * (dp - delta)).astype(BF16)
            dqs.append((lax.dot_general(k2, ds, TN, preferred_element_type=F32) * ATTN_SCALE).T)
            dks.append(jnp.dot(ds, qg, preferred_element_type=F32) * ATTN_SCALE)
            dvs.append(jnp.dot(p.astype(BF16), dog, preferred_element_type=F32))
            dsink = -(psink * inv * delta)
            for g in range(GROUP):
                r = h * GROUP + g
                dsk_ref[r:r + 1, :] += jnp.sum(dsink[:, g * BLOCK:(g + 1) * BLOCK])
        return _unstack_heads(dqs), jnp.concatenate(dks, axis=1), jnp.concatenate(dvs, axis=1)

    def body(sk_ref, bias0_ref, bias1_ref, q_ref, kp_ref, kc_ref, vp_ref, vc_ref, o_ref, do_ref,
             dq_ref, dke_ref, dko_ref, dve_ref, dvo_ref, dsk_ref, ck_ref, cvv_ref):
        i = pl.program_id(0)

        @pl.when(i == 0)
        def _():
            for ref in (ck_ref, cvv_ref, dsk_ref):
                ref[...] = jnp.zeros_like(ref)

        @pl.when(i < npair)
        def _():
            kc, vc = kc_ref[...], vc_ref[...]
            first, second = slice(0, BLOCK), slice(BLOCK, 2 * BLOCK)
            dq0, dk0, dv0 = one_block(sk_ref, bias0_ref[...], q_ref[first, :], kp_ref[...], kc[first], vp_ref[...],
                                      vc[first], o_ref[first, :], do_ref[first, :], dsk_ref)
            dq1, dk1, dv1 = one_block(sk_ref, bias1_ref[...], q_ref[second, :], kc[first], kc[second], vc[first],
                                      vc[second], o_ref[second, :], do_ref[second, :], dsk_ref)
            dq_ref[first, :] = dq0.astype(BF16)
            dq_ref[second, :] = dq1.astype(BF16)
            dko_ref[...] = (ck_ref[...] + dk0[:BLOCK]).astype(BF16)
            dvo_ref[...] = (cvv_ref[...] + dv0[:BLOCK]).astype(BF16)
            dke_ref[...] = (dk0[BLOCK:] + dk1[:BLOCK]).astype(BF16)
            dve_ref[...] = (dv0[BLOCK:] + dv1[:BLOCK]).astype(BF16)
            ck_ref[...] = dk1[BLOCK:]
            cvv_ref[...] = dv1[BLOCK:]

        @pl.when(i == npair)
        def _():
            dko_ref[...] = ck_ref[...].astype(BF16)
            dvo_ref[...] = cvv_ref[...].astype(BF16)

    cur = lambda i: jnp.minimum(i, npair - 1)
    done = lambda i: jnp.maximum(i - 1, 0)
    rows = pl.BlockSpec((2 * BLOCK, ATTN_W), lambda i: (cur(i), 0))
    even = pl.BlockSpec((BLOCK, KV_W), lambda i: (cur(i), 0))
    odd = pl.BlockSpec((BLOCK, KV_W), lambda i: (done(i), 0))
    half = jax.ShapeDtypeStruct((s // 2, KV_W), BF16)
    return _call(
        comm, body, name="attn_bwd", grid=(npair + 1,),
        in_specs=[pl.BlockSpec(memory_space=pltpu.SMEM), *_attn_bias_specs(), *_block_specs(2, npair), rows, rows],
        out_specs=[rows, even, odd, even, odd, pl.BlockSpec((N_HEADS, 128), lambda i: (0, 0))],
        out_shape=[jax.ShapeDtypeStruct((s, ATTN_W), BF16), half, half, half, half,
                   jax.ShapeDtypeStruct((N_HEADS, 128), F32)],
        scratch_shapes=[pltpu.VMEM((BLOCK, KV_W), F32), pltpu.VMEM((BLOCK, KV_W), F32)],
        compiler_params=_params("arbitrary"),
    )(sinks, _attn_bias(), _attn_bias(), qkv, qkv, qkv, qkv, qkv, o, do)


def _inproj_bwd(dq, dk, dv, dc3, dgt, w_in, x, xn, dh1, g1):
    s = x.shape[0]
    tm = min(2 * BLOCK, s)
    nt = s // tm

    def body(dq_ref, dke_ref, dko_ref, dve_ref, dvo_ref, dc3_ref, dgt_ref, w_ref, x_ref, xn_ref, dh1_ref, g_ref,
             dx_ref, gw_ref, gb_ref, gg_ref, acc_ref):
        i = pl.program_id(0)

        @pl.when(i == 0)
        def _():
            for ref in (gb_ref, gg_ref, acc_ref):
                ref[...] = jnp.zeros_like(ref)

        dk = jnp.concatenate([dke_ref[...], dko_ref[...]], axis=0)
        dv = jnp.concatenate([dve_ref[...], dvo_ref[...]], axis=0)
        dp = jnp.concatenate([dq_ref[...], dk, dv, dc3_ref[...], dgt_ref[...]], axis=1)
        acc_ref[...] += lax.dot_general(dp, xn_ref[...], TN, preferred_element_type=F32)
        gb_ref[...] += jnp.sum(dp.astype(F32), axis=0, keepdims=True)
        dxn = jnp.dot(dp, w_ref[...], preferred_element_type=F32)
        xf = x_ref[...]
        dx, dg = _rms_bwd(dxn, xf, _rstd(xf), g_ref[...])
        dx_ref[...] = dh1_ref[...] + dx
        gg_ref[...] += jnp.sum(dg, axis=0, keepdims=True)

        @pl.when(i == nt - 1)
        def _():
            gw_ref[...] = acc_ref[...].astype(BF16)

    row = lambda w: pl.BlockSpec((tm, w), lambda i: (i, 0))
    acc = lambda w: pl.BlockSpec((1, w), lambda i: (0, 0))
    block = pl.BlockSpec((tm // 2, KV_W), lambda i: (i, 0))
    return pl.pallas_call(
        body, name="inproj_bwd", grid=(nt,),
        in_specs=[row(ATTN_W), block, block, block, block, row(C3_W), row(GATES_W), _resident((IN_W, D_MODEL)),
                  row(D_MODEL), row(D_MODEL), row(D_MODEL), _resident((1, D_MODEL))],
        out_specs=[row(D_MODEL), _resident((IN_W, D_MODEL)), acc(IN_W), acc(D_MODEL)],
        out_shape=[jax.ShapeDtypeStruct((s, D_MODEL), F32), jax.ShapeDtypeStruct((IN_W, D_MODEL), BF16),
                   jax.ShapeDtypeStruct((1, IN_W), F32), jax.ShapeDtypeStruct((1, D_MODEL), F32)],
        scratch_shapes=[pltpu.VMEM((IN_W, D_MODEL), F32)],
        compiler_params=_params("arbitrary"),
    )(dq, *dk, *dv, dc3, dgt, w_in, x, xn, dh1, g1)


def _wgrad(a, b, bm, bn, bk, name, comm=None):
    s, m = a.shape
    n = b.shape[1]
    nk = s // bk

    def body(a_ref, b_ref, o_ref, acc_ref):
        k = pl.program_id(2)

        @pl.when(k == 0)
        def _():
            acc_ref[...] = jnp.zeros_like(acc_ref)

        acc_ref[...] += lax.dot_general(a_ref[...].astype(BF16), b_ref[...].astype(BF16), TN,
                                        preferred_element_type=F32)

        @pl.when(k == nk - 1)
        def _():
            o_ref[...] = acc_ref[...].astype(BF16)

    return _call(
        comm, body, name=name, grid=(m // bm, n // bn, nk),
        in_specs=[pl.BlockSpec((bk, bm), lambda i, j, k: (k, i)), pl.BlockSpec((bk, bn), lambda i, j, k: (k, j))],
        out_specs=pl.BlockSpec((bm, bn), lambda i, j, k: (i, j)),
        out_shape=jax.ShapeDtypeStruct((m, n), BF16),
        scratch_shapes=[pltpu.VMEM((bm, bn), F32)],
        compiler_params=_params("parallel", "parallel", "arbitrary"),
    )(a, b)


class _Carry:
    def __init__(self, jobs, reads=None, bufs=None, fresh=None):
        self.jobs, self.reads, self.bufs, self.fresh = jobs, reads or {}, bufs or {}, fresh or {}
        self.out = {}


class _Job:
    def __init__(self, n_sems, plan):
        self.n_sems, self.plan = n_sems, plan


def _plan_all(jobs, hbm, send, recv):
    pos = _position()
    starts, waits, base = [], [], 0
    for job in jobs:
        s, w = job.plan(hbm, pos, send, recv, base)
        starts, waits, base = starts + s, waits + w, base + job.n_sems
    return starts, waits


def _call(comm, body, **kw):
    if comm is None:
        return pl.pallas_call(body, **kw)
    grid = kw["grid"]
    single = not isinstance(kw["out_shape"], (list, tuple))
    out_shape = [kw["out_shape"]] if single else list(kw["out_shape"])
    out_specs = [kw["out_specs"]] if single else list(kw["out_specs"])
    in_specs = list(kw["in_specs"])
    scratch = list(kw.get("scratch_shapes", ()))
    r_names, b_names, f_names = list(comm.reads), list(comm.bufs), list(comm.fresh)
    n_args, n_out, n_scr = len(in_specs), len(out_shape), len(scratch)
    n_sems = sum(j.n_sems for j in comm.jobs)

    def wrapped(*refs):
        k = n_args
        hbm = dict(zip(r_names, refs[k:k + len(r_names)]))
        k += len(r_names) + len(b_names)
        outs = refs[k:k + n_out]
        k += n_out
        hbm.update(zip(b_names + f_names, refs[k:k + len(b_names) + len(f_names)]))
        k += len(b_names) + len(f_names)
        send, recv = refs[k + n_scr:]
        starts, waits = _plan_all(comm.jobs, hbm, send, recv)
        ids = [pl.program_id(a) for a in range(len(grid))]
        first = functools.reduce(jnp.logical_and, [i == 0 for i in ids])
        last = functools.reduce(jnp.logical_and, [i == g - 1 for i, g in zip(ids, grid)])

        @pl.when(first)
        def _():
            for cp in starts:
                cp.start()

        body(*refs[:n_args], *outs, *refs[k:k + n_scr])

        @pl.when(last)
        def _():
            for cp in waits:
                cp.wait_recv()
            for cp in starts:
                cp.wait_send()

    sems = pltpu.SemaphoreType.DMA((n_sems,))
    held = [jax.ShapeDtypeStruct(a.shape, a.dtype) for a in comm.bufs.values()] + list(comm.fresh.values())
    call = pl.pallas_call(
        wrapped, name=kw["name"], grid=grid,
        in_specs=in_specs + [_ANY] * (len(r_names) + len(b_names)),
        out_specs=out_specs + [_ANY] * len(held),
        out_shape=out_shape + held,
        input_output_aliases={n_args + len(r_names) + i: n_out + i for i in range(len(b_names))},
        scratch_shapes=scratch + [sems, sems],
        compiler_params=_params(*["arbitrary"] * len(grid)),
    )

    def run(*args):
        res = call(*args, *comm.reads.values(), *comm.bufs.values())
        comm.out = dict(zip(b_names + f_names, res[n_out:]))
        return res[0] if single else res[:n_out]

    return run


def _exchange(name, phases, reads=None, bufs=None, fresh=None):
    comm = _Carry([j for ph in phases for j in ph], reads, bufs, fresh)
    r_names, b_names, f_names = list(comm.reads), list(comm.bufs), list(comm.fresh)
    n_sems = sum(j.n_sems for j in comm.jobs)

    def body(*refs):
        hbm = dict(zip(r_names, refs[:len(r_names)]))
        k = len(r_names) + len(b_names)
        hbm.update(zip(b_names + f_names, refs[k:k + len(b_names) + len(f_names)]))
        send, recv = refs[-2:]
        pos = _position()
        started, base = [], 0
        for ph in phases:
            waits = []
            for job in ph:
                s, w = job.plan(hbm, pos, send, recv, base)
                base += job.n_sems
                for cp in s:
                    cp.start()
                started, waits = started + s, waits + w
            for cp in waits:
                cp.wait_recv()
        for cp in started:
            cp.wait_send()

    sems = pltpu.SemaphoreType.DMA((n_sems,))
    held = [jax.ShapeDtypeStruct(a.shape, a.dtype) for a in comm.bufs.values()] + list(comm.fresh.values())
    res = pl.pallas_call(
        body, name=name, in_specs=[_ANY] * (len(r_names) + len(b_names)), out_specs=[_ANY] * len(held),
        out_shape=held, input_output_aliases={len(r_names) + i: i for i in range(len(b_names))},
        scratch_shapes=[sems, sems],
    )(*comm.reads.values(), *comm.bufs.values())
    return dict(zip(b_names + f_names, res))


_HBM = pl.BlockSpec(memory_space=pltpu.HBM)
_SEM = pl.BlockSpec(memory_space=pltpu.SEMAPHORE)
_EFFECT = pltpu.SideEffectType.DATAFLOW_SIDE_EFFECTING


def _start_exchanges(name, groups):
    names = [list(arrays) for _, arrays in groups]
    first = [sum(len(ns) for ns in names[:g]) for g in range(len(groups))]
    n, ng = sum(len(ns) for ns in names), len(groups)

    def body(*refs):
        for g, (jobs, _) in enumerate(groups):
            hbm = dict(zip(names[g], refs[first[g]:first[g] + len(names[g])]))
            for cp in _plan_all(jobs, hbm, refs[n + 2 * g], refs[n + 2 * g + 1])[0]:
                cp.start()
        refs[-1][...] = jnp.zeros_like(refs[-1])

    given = [pltpu.with_memory_space_constraint(
        a if isinstance(a, jax.Array) else lax.empty(a.shape, a.dtype), pltpu.HBM)
        for _, arrays in groups for a in arrays.values()]
    sems = [pltpu.SemaphoreType.DMA((sum(j.n_sems for j in jobs),)) for jobs, _ in groups for _ in range(2)]
    res = pl.pallas_call(
        body, name=name,
        out_shape=(*sems, *[pltpu.HBM(a.shape, a.dtype) for a in given], jax.ShapeDtypeStruct((8, 128), F32)),
        in_specs=[_HBM] * n, out_specs=(*[_SEM] * (2 * ng), *[_HBM] * n, pl.BlockSpec(memory_space=pltpu.VMEM)),
        input_output_aliases={i: 2 * ng + i for i in range(n)},
        compiler_params=pltpu.CompilerParams(has_side_effects=_EFFECT),
    )(*given)
    held = res[2 * ng:2 * ng + n]
    states = [(names[g], groups[g][0], res[2 * g], res[2 * g + 1], held[first[g]:first[g] + len(names[g])])
              for g in range(ng)]
    return states, res[-1]


def _start_exchange(name, jobs, arrays):
    states, token = _start_exchanges(name, [(jobs, arrays)])
    return states[0], token


def _finish_exchange(name, state, after):
    names, jobs, send_sem, recv_sem, held = state
    n = len(names)

    def body(*refs):
        hbm = dict(zip(names, refs[:n]))
        send, recv = refs[n:n + 2]
        starts, waits = _plan_all(jobs, hbm, send, recv)
        for cp in waits:
            cp.wait_recv()
        for cp in starts:
            cp.wait_send()

    res = pl.pallas_call(
        body, name=name, out_shape=tuple(pltpu.HBM(a.shape, a.dtype) for a in held),
        in_specs=[_HBM] * n + [_SEM, _SEM, _ANY], out_specs=tuple([_HBM] * n),
        input_output_aliases={i: i for i in range(n)},
        compiler_params=pltpu.CompilerParams(has_side_effects=_EFFECT),
    )(*held, send_sem, recv_sem, after)
    return dict(zip(names, res))


def _relay_exchange(name, state, jobs, after):
    names, arrived, send_sem, recv_sem, held = state
    n = len(names)

    def body(*refs):
        hbm = dict(zip(names, refs[:n]))
        starts, waits = _plan_all(arrived, hbm, refs[n], refs[n + 1])
        for cp in waits:
            cp.wait_recv()
        for cp in starts:
            cp.wait_send()
        for cp in _plan_all(jobs, hbm, refs[n + 3], refs[n + 4])[0]:
            cp.start()
        refs[-1][...] = jnp.zeros_like(refs[-1])

    sems = pltpu.SemaphoreType.DMA((sum(j.n_sems for j in jobs),))
    res = pl.pallas_call(
        body, name=name,
        out_shape=(sems, sems, *[pltpu.HBM(a.shape, a.dtype) for a in held], jax.ShapeDtypeStruct((8, 128), F32)),
        in_specs=[_HBM] * n + [_SEM, _SEM, _ANY],
        out_specs=(_SEM, _SEM, *[_HBM] * n, pl.BlockSpec(memory_space=pltpu.VMEM)),
        input_output_aliases={i: 2 + i for i in range(n)},
        compiler_params=pltpu.CompilerParams(has_side_effects=_EFFECT),
    )(*held, send_sem, recv_sem, after)
    return (names, jobs, res[0], res[1], res[2:2 + n]), res[-1]


def _row_tile(rows, bytes_per_row):
    best = 16
    for t in range(16, rows + 1, 16):
        if rows % t == 0 and t * bytes_per_row <= 9 * 1024 * 1024:
            best = t
    return best


def _rowwise(fn, ins, out_dtypes, name, after=None):
    rows, cols = ins[0].shape
    per_row = sum(cols * a.dtype.itemsize for a in ins) + sum(cols * jnp.dtype(d).itemsize for d in out_dtypes)
    tr = _row_tile(rows, per_row)
    n_in = len(ins)

    def body(*refs):
        outs = fn(*[r[...] for r in refs[:n_in]])
        for o_ref, o in zip(refs[-len(out_dtypes):], outs):
            o_ref[...] = o.astype(o_ref.dtype)

    tile = pl.BlockSpec((tr, cols), lambda i: (i, 0))
    behind = [] if after is None else [after]
    return pl.pallas_call(
        body, name=name, grid=(rows // tr,),
        in_specs=[tile] * n_in + [pl.BlockSpec((8, 128), lambda i: (0, 0))] * len(behind),
        out_specs=[tile] * len(out_dtypes),
        out_shape=[jax.ShapeDtypeStruct((rows, cols), d) for d in out_dtypes],
        compiler_params=_params("parallel"),
    )(*ins, *behind)


def _tiled(fn, name, grid, pos, ins, outs):
    n_in = len(ins)

    def body(pos_ref, *refs):
        res = fn(*[r[...] for r in refs[:n_in]])
        for o_ref, o in zip(refs[n_in:], res):
            o_ref[...] = o.astype(o_ref.dtype)

    return pl.pallas_call(
        body, name=name,
        grid_spec=pltpu.PrefetchScalarGridSpec(
            num_scalar_prefetch=1, grid=grid,
            in_specs=[pl.BlockSpec(bs, im) for _, bs, im in ins],
            out_specs=[pl.BlockSpec(bs, im) for _, _, bs, im in outs]),
        out_shape=[jax.ShapeDtypeStruct(s, d) for s, d, _, _ in outs],
        compiler_params=_params("parallel"),
    )(pos, *[a for a, _, _ in ins])


def _adamw(w, g, m, v):
    m = ADAM_B1 * m + (1.0 - ADAM_B1) * g
    v = ADAM_B2 * v + (1.0 - ADAM_B2) * (g * g)
    m_hat = m / (1.0 - ADAM_B1 ** ADAM_STEP)
    v_hat = v / (1.0 - ADAM_B2 ** ADAM_STEP)
    return -ADAM_LR * (m_hat / (jnp.sqrt(v_hat) + ADAM_EPS) + ADAM_WD * w), m, v


def _adamw_small(pos, own, slots, params):
    n = len(params)

    def body(pos_ref, own_ref, slots_ref, *refs):
        ins, outs, total_ref = refs[:3 * n], refs[3 * n:-1], refs[-1]
        chip = pos_ref[0]
        idx = 2 * chip + pos_ref[1]
        term = lambda q: jnp.where(idx == q, own_ref[...], slots_ref[q])
        acc = term(0)
        for q in range(1, N_DEV):
            acc = acc + term(q)
        total_ref[...] = acc
        outs[0][...] = total_ref[0:1, :]
        for k, (w, _, _, row) in enumerate(params):
            width = min(w.shape[-1], 128)
            for t in range(w.shape[0]):
                for j in range(w.shape[-1] // width):
                    lanes = slice(j * width, (j + 1) * width)
                    at = (slice(t, t + 1), lanes) if w.ndim == 2 else (t, slice(None), lanes)
                    g = total_ref[pl.ds(row(t, j, chip), 1), :][:, :width]
                    new = _adamw(ins[3 * k][at], g, ins[3 * k + 1][at], ins[3 * k + 2][at])
                    for o_ref, o in zip(outs[1 + 4 * k:5 + 4 * k], (g, *new)):
                        o_ref[at] = o

    vmem = pl.BlockSpec(memory_space=pltpu.VMEM)
    return pl.pallas_call(
        body, name="adamw_small",
        in_specs=[pl.BlockSpec(memory_space=pltpu.SMEM)] + [vmem] * (2 + 3 * n),
        out_shape=[jax.ShapeDtypeStruct((1, 128), F32)]
        + [jax.ShapeDtypeStruct(p[0].shape, F32) for p in params for _ in range(4)],
        scratch_shapes=[pltpu.VMEM(own.shape, F32)],
    )(pos, own, slots, *[a for p in params for a in p[:3]])


class _Layout:
    def __init__(self, rows, cols, stacked):
        self.rows, self.cols, self.stacked = rows, cols, stacked

    def whole(self, rows=None):
        r = self.rows if rows is None else rows
        return (N_CHIPS, r, self.cols) if self.stacked else (r, N_CHIPS * self.cols)

    def part_rows(self, h, q=0, nq=1):
        n = self.rows // 2 // nq
        return pl.ds(pl.multiple_of(h * (self.rows // 2) + q * n, 16), n)

    def half_rows(self, h):
        return self.part_rows(h)

    def block(self, ref, p, rows=slice(None)):
        if self.stacked:
            return ref.at[p, rows, :]
        return ref.at[rows, pl.ds(pl.multiple_of(p * self.cols, 128), self.cols)]

    def all_chips(self, ref, rows):
        return ref.at[:, rows, :] if self.stacked else ref.at[rows, :]


BIG = (
    _Layout(IN_SHARD, D_MODEL, True),
    _Layout(ATTN_W + CONV_W, D_MODEL // N_CHIPS, False),
    _Layout(D_MODEL // N_CHIPS, D_MODEL, True),
    _Layout(D_MODEL, FF2 // N_CHIPS, False),
    _Layout(D_FF // N_CHIPS, D_MODEL, True),
)
N_BIG = len(BIG)
_ANY = pl.BlockSpec(memory_space=pl.ANY)


def _position():
    x, y, c = lax.axis_index("x"), lax.axis_index("y"), lax.axis_index("c")
    return x, y, c, 2 * x + y


def _core_of_chip(p, c):
    return (p >> 1, p & 1, c)


def _place_cast(shard, lay, pos, name, after=None):
    rows, cols = shard.shape
    tr = _row_tile(rows, cols * 6)
    if lay.stacked:
        out = (lay.whole(), BF16, (None, tr, cols), lambda i, pos: (pos[0], i, 0))
    else:
        out = (lay.whole(), BF16, (tr, cols), lambda i, pos: (i, pos[0]))
    ins = [(shard, (tr, cols), lambda i, pos: (i, 0))]
    if after is not None:
        ins.append((after, (8, 128), lambda i, pos: (0, 0)))
    return _tiled(lambda a, *_: (a,), name, (rows // tr,), pos, ins, [out])[0]


def _place_cast_pair(top, bottom, lay, pos, name, after=None):
    rows, cols = top.shape
    ins = [(top, (rows, cols), lambda i, pos: (0, 0)), (bottom, (rows, cols), lambda i, pos: (0, 0))]
    if after is not None:
        ins.append((after, (8, 128), lambda i, pos: (0, 0)))
    return _tiled(lambda a, b, *_: (jnp.concatenate([a, b], axis=0),), name, (1,), pos, ins,
                  [(lay.whole(), BF16, (2 * rows, cols), lambda i, pos: (0, pos[0]))])[0]


def _adamw_pair(top, bottom, g, after=None):
    rows = top[0].shape[0]

    def body(*refs):
        (wa, ma, va, wb, mb, vb, g_ref), outs = refs[:7], refs[-8:]
        for (w, m, v), gg, o in (((wa, ma, va), g_ref[:rows], outs[:4]), ((wb, mb, vb), g_ref[rows:], outs[4:])):
            for o_ref, val in zip(o, (gg, *_adamw(w[...], gg, m[...], v[...]))):
                o_ref[...] = val

    behind = [] if after is None else [after]
    res = pl.pallas_call(
        body, name="adamw_w_br", out_shape=[jax.ShapeDtypeStruct(top[0].shape, F32)] * 8,
        in_specs=[pl.BlockSpec(memory_space=pltpu.VMEM)] * 7 + [_ANY] * len(behind),
    )(*top, *bottom, g, *behind)
    return res[:4], res[4:]


def _remote(src, dst, send, recv, k, device):
    return pltpu.make_async_remote_copy(src_ref=src, dst_ref=dst, send_sem=send.at[k], recv_sem=recv.at[k],
                                        device_id=device, device_id_type=MESH)


def _arrival(dst, send, recv, k, me):
    return _remote(dst, dst, send, recv, k, me)


def _gather_ici(lay, name, q=0, nq=1):
    def plan(hbm, pos, send, recv, base):
        x, y, c, me = pos
        rows = lay.part_rows(c, q, nq)
        mine = lay.block(hbm[name], me, rows)
        starts = [_remote(mine, mine, send, recv, base + d - 1, _core_of_chip(me ^ d, c)) for d in (1, 2, 3)]
        waits = [_arrival(lay.block(hbm[name], me ^ d, rows), send, recv, base + d - 1, (x, y, c)) for d in (1, 2, 3)]
        return starts, waits
    return _Job(3, plan)


def _gather_near(lay, name):
    def plan(hbm, pos, send, recv, base):
        x, y, c, me = pos
        rows = lay.part_rows(c)
        mine = lay.block(hbm[name], me, rows)
        starts = [_remote(mine, mine, send, recv, base + d - 1, _core_of_chip(me ^ d, c)) for d in (1, 2)]
        waits = [_arrival(lay.block(hbm[name], me ^ d, rows), send, recv, base + d - 1, (x, y, c)) for d in (1, 2)]
        return starts, waits
    return _Job(2, plan)


def _gather_far(lay, name):
    def plan(hbm, pos, send, recv, base):
        x, y, c, me = pos
        starts, waits = [], []
        for q, d in ((0, 1), (1, 2)):
            got = lay.block(hbm[name], me ^ (3 - d), lay.part_rows(c, q, 2))
            starts.append(_remote(got, got, send, recv, base + q, _core_of_chip(me ^ d, c)))
            waits.append(_arrival(lay.block(hbm[name], me ^ 3, lay.part_rows(c, q, 2)), send, recv, base + q, (x, y, c)))
        return starts, waits
    return _Job(2, plan)


def _gather_d2d(lay, name, q=0, nq=1):
    def plan(hbm, pos, send, recv, base):
        x, y, c, me = pos
        starts, waits = [], []
        for d in (1, 2, 3):
            got = lay.block(hbm[name], me ^ d, lay.part_rows(c, q, nq))
            starts.append(_remote(got, got, send, recv, base + d - 1, (x, y, 1 - c)))
            waits.append(_arrival(lay.block(hbm[name], me ^ d, lay.part_rows(1 - c, q, nq)), send, recv, base + d - 1,
                                  (x, y, c)))
        return starts, waits
    return _Job(3, plan)


def _rs_pair(lay, grad, theirs):
    def plan(hbm, pos, send, recv, base):
        x, y, c, _ = pos
        out = _remote(lay.all_chips(hbm[grad], lay.half_rows(1 - c)), hbm[theirs], send, recv, base, (x, y, 1 - c))
        return [out], [_arrival(hbm[theirs], send, recv, base, (x, y, c))]
    return _Job(1, plan)


def _rs_chips(lay, sums, slots):
    def plan(hbm, pos, send, recv, base):
        x, y, c, me = pos
        starts = [_remote(lay.block(hbm[sums], me ^ d), hbm[slots].at[me], send, recv, base + d - 1,
                          _core_of_chip(me ^ d, c)) for d in (1, 2, 3)]
        waits = [_arrival(hbm[slots].at[me ^ d], send, recv, base + d - 1, (x, y, c)) for d in (1, 2, 3)]
        return starts, waits
    return _Job(3, plan)


def _rs_share(lay, shard):
    def plan(hbm, pos, send, recv, base):
        x, y, c, _ = pos
        mine = hbm[shard].at[lay.half_rows(c), :]
        other = hbm[shard].at[lay.half_rows(1 - c), :]
        return [_remote(mine, mine, send, recv, base, (x, y, 1 - c))], [_arrival(other, send, recv, base, (x, y, c))]
    return _Job(1, plan)


def _slots_shape(lay):
    return jax.ShapeDtypeStruct((N_CHIPS, lay.rows // 2, lay.cols), BF16)


def _theirs_shape(lay, dtype=BF16):
    return jax.ShapeDtypeStruct(lay.whole(lay.rows // 2), dtype)


def _pair_sum(grad, theirs, lay, pos, name):
    half = lay.rows // 2
    add = lambda a, b: (a.astype(F32) + b.astype(F32),)
    if lay.stacked:
        tr = _row_tile(half, lay.cols * 6)
        nt = half // tr
        flat = lambda a: a.reshape(-1, lay.cols)
        mine = lambda t, pos: ((t // nt) * (2 * nt) + pos[1] * nt + t % nt, 0)
        grid, blk = (N_CHIPS * nt,), (tr, lay.cols)
        grad, theirs = flat(grad), flat(theirs)
    else:
        tr = _row_tile(half, N_CHIPS * lay.cols * 6)
        nt = half // tr
        mine = lambda t, pos: (pos[1] * nt + t, 0)
        grid, blk = (nt,), (tr, N_CHIPS * lay.cols)
    same = lambda t, pos: (t, 0)
    out = _tiled(add, name, grid, pos, [(grad, blk, mine), (theirs, blk, same)], [(theirs.shape, BF16, blk, same)])[0]
    return out.reshape(lay.whole(half))


def _chip_sum(sums, slots, lay, pos, name, after=None):
    half = lay.rows // 2
    tr = _row_tile(half, lay.cols * 12)
    nt = half // tr
    blk3 = (None, tr, lay.cols)
    if lay.stacked:
        own = (sums, blk3, lambda i, pos: (pos[0], i, 0))
    else:
        own = (sums, (tr, lay.cols), lambda i, pos: (i, pos[0]))
    others = [(slots, blk3, functools.partial(lambda d, i, pos: (pos[0] ^ d, i, 0), d)) for d in (1, 2, 3)]

    def add(a, b1, b2, b3, *_):
        return (((a.astype(F32) + b1.astype(F32)) + b2.astype(F32)) + b3.astype(F32),)

    if after is not None:
        others.append((after, (8, 128), lambda i, pos: (0, 0)))
    return _tiled(add, name, (nt,), pos, [own] + others,
                  [((lay.rows, lay.cols), F32, (tr, lay.cols), lambda i, pos: (pos[1] * nt + i, 0))])[0]


N_DEV = 8


def _to_all(src, slots):
    def plan(hbm, pos, send, recv, base):
        x, y, c, _ = pos
        idx = 4 * x + 2 * y + c
        starts = [_remote(hbm[src], hbm[slots].at[idx], send, recv, base + k - 1,
                          (x ^ (k >> 2), y ^ ((k >> 1) & 1), c ^ (k & 1))) for k in range(1, N_DEV)]
        waits = [_arrival(hbm[slots].at[idx ^ k], send, recv, base + k - 1, (x, y, c)) for k in range(1, N_DEV)]
        return starts, waits
    return _Job(N_DEV - 1, plan)


def _pack_rows(parts):
    padded = [jnp.pad(a, ((0, -a.shape[0] % 8), (0, 0))) for a in parts]
    starts = [sum(p.shape[0] for p in padded[:k]) for k in range(len(padded))]
    return jnp.concatenate(padded, axis=0), starts


def kernel(x, mix_norm, w_in, b_in, sinks, conv_w, w_attn_branch, w_conv_branch, w_out, ffn_norm, w_up, ffn_conv_w, w_down, final_norm, loss_target, m_mix_norm, m_w_in, m_b_in, m_sinks, m_conv_w, m_w_attn_branch, m_w_conv_branch, m_w_out, m_ffn_norm, m_w_up, m_ffn_conv_w, m_w_down, m_final_norm, v_mix_norm, v_w_in, v_b_in, v_sinks, v_conv_w, v_w_attn_branch, v_w_conv_branch, v_w_out, v_ffn_norm, v_w_up, v_ffn_conv_w, v_w_down, v_final_norm):
    me = 2 * lax.axis_index("x") + lax.axis_index("y")
    names = ("w_in", "w_br", "w_out", "w_up", "w_down")
    w_of = dict(w_in=w_in[0].T, w_out=w_out[0], w_up=w_up[0], w_down=w_down[0])
    m_of = dict(w_in=m_w_in[0].T, w_out=m_w_out[0], w_up=m_w_up[0], w_down=m_w_down[0])
    v_of = dict(w_in=v_w_in[0].T, w_out=v_w_out[0], w_up=v_w_up[0], w_down=v_w_down[0])
    ab = (w_attn_branch[0], m_w_attn_branch[0], v_w_attn_branch[0])
    cb = (w_conv_branch[0], m_w_conv_branch[0], v_w_conv_branch[0])

    pos = jnp.stack([me, lax.axis_index("c")]).astype(jnp.int32)

    lay = dict(zip(names, BIG))
    xs, target, sk = x[0], loss_target[0], sinks[0]
    s = xs.shape[0]
    tm, tm2, bk, bk2 = min(256, s), min(512, s), min(1024, s), min(2048, s)

    taps, (_, t0) = _pack_rows([conv_w[0], ffn_conv_w[0].reshape(3 * (FF2 // N_CHIPS // 128), 128)])
    placed = {"w_in": _place_cast(w_of["w_in"], lay["w_in"], pos, "cast_w_in")}
    fly_in, started = _start_exchange("gather_in_start", [_gather_near(lay["w_in"], "w_in")], {"w_in": placed["w_in"]})
    taps_flight, started = _start_exchange("taps_start", [_to_all("v", "slots")],
                                           {"v": taps + started[0:1], "slots": jnp.zeros((N_DEV, *taps.shape), F32)})
    placed["w_br"] = _place_cast_pair(ab[0], cb[0], lay["w_br"], pos, "cast_w_br", after=started)
    for n in names[2:]:
        placed[n] = _place_cast(w_of[n], lay[n], pos, "cast_" + n, after=started)
    trio = ("w_br", "w_out")
    fly_in, started = _relay_exchange("gather_in_relay", fly_in, [_gather_far(lay["w_in"], "w_in")], after=placed["w_down"])
    (fly_trio, fly_up, fly_down), started = _start_exchanges("gather_rest_start", [
        ([_gather_ici(lay[n], n) for n in ws], {**{n: placed[n] for n in ws}, **behind})
        for ws, behind in ((trio, {"behind": started}), (("w_up",), {}), (("w_down",), {}))])

    got = _finish_exchange("gather_in_wait", fly_in, after=started)
    w_in_full = _exchange("gather_in_d2d", [[_gather_d2d(lay["w_in"], "w_in")]], bufs=got)["w_in"].reshape(IN_W, D_MODEL)
    xn, qkv, c3, gates = _inproj_fwd(xs, mix_norm, w_in_full, b_in, tm2)
    got = _finish_exchange("gather_trio_wait", fly_trio, after=qkv)
    k2 = _Carry([_gather_d2d(lay[n], n) for n in trio], bufs={n: got[n] for n in trio})
    attn = _attn_fwd(qkv, sk, comm=k2)
    w_br = k2.out["w_br"]
    w_out_full = k2.out["w_out"].reshape(D_MODEL, D_MODEL)
    k3 = _Carry([_gather_d2d(lay["w_up"], "w_up")], bufs=_finish_exchange("gather_up_wait", fly_up, after=attn))
    taps = _finish_exchange("taps_wait", taps_flight, after=attn)
    taps = lax.dynamic_update_slice(taps["slots"], taps["v"][None], (2 * me + lax.axis_index("c"), 0, 0))
    conv_full = taps[0::2, 0:3].transpose(1, 0, 2).reshape(3, CONV_W)
    ffn_cw_full = taps[0::2, t0:t0 + 33].reshape(N_CHIPS, 3, FF2 // N_CHIPS).transpose(1, 0, 2).reshape(3, FF2)
    conv, a, cv, merged, h1, hn = _mix_fwd(xs, attn, c3, gates, conv_full, w_br, w_out_full, ffn_norm, tm2, comm=k3)
    w_up_full = k3.out["w_up"]
    w_down_full = _exchange("gather_down_d2d", [[_gather_d2d(lay["w_down"], "w_down")]],
                            bufs=_finish_exchange("gather_down_wait", fly_down, after=hn))["w_down"].reshape(D_FF, D_MODEL)
    u, up, act, dh2, loss_part, g_fn = _ffn_fwd_loss(hn, h1, w_up_full, ffn_cw_full, w_down_full,
                                                     final_norm[None, :], target, tm)

    grads, sums, slots = {}, {}, {}

    def pair(*ws):
        return _Carry([_rs_pair(lay[n], "g_" + n, "t_" + n) for n in ws], reads={"g_" + n: grads[n] for n in ws},
                      fresh={"t_" + n: _theirs_shape(lay[n], grads[n].dtype) for n in ws})

    def chips(*ws, also=None):
        k = _Carry([_rs_chips(lay[n], "s_" + n, "r_" + n) for n in ws], reads={"s_" + n: sums[n] for n in ws},
                   fresh={"r_" + n: _slots_shape(lay[n]) for n in ws})
        if also is not None:
            k = _Carry(k.jobs + also.jobs, {**k.reads, **also.reads}, None, {**k.fresh, **also.fresh})
        return k

    def pair_sums(k, *ws):
        for n in ws:
            sums[n] = _pair_sum(grads[n], k.out["t_" + n], lay[n], pos, "pair_sum_" + n)

    def take_slots(k, *ws):
        for n in ws:
            slots[n] = k.out["r_" + n]

    du, dh1, g_fcw, g_g2 = _ffn_bwd(dh2, u, up, h1, w_up_full, ffn_cw_full, w_down_full, ffn_norm, tm)
    grads["w_down"] = _wgrad(act, dh2, D_FF // 2, D_MODEL, bk2, "wgrad_down").reshape(lay["w_down"].whole())
    k4 = pair("w_down")
    grads["w_up"] = _wgrad(hn, du, D_MODEL, FF2 // 4, bk2, "wgrad_up", comm=k4)
    pair_sums(k4, "w_down")
    k5 = chips("w_down", also=pair("w_up"))
    dattn, dc3, dgt, g_cw, grads["w_br"], gw_out = _mix_bwd(
        dh1, gates, a, cv, c3, attn, conv, merged, conv_full, w_br, w_out_full, tm2, comm=k5)
    grads["w_out"] = gw_out.reshape(lay["w_out"].whole())
    take_slots(k5, "w_down")
    pair_sums(k5, "w_up")
    up_flight, started = _start_exchange("rs_chips_up_start", [_rs_chips(lay["w_up"], "s", "r")],
                                         {"s": sums["w_up"], "r": _slots_shape(lay["w_up"])})
    k6 = pair(*trio)
    k6.reads["after"] = started
    dq, dk_even, dk_odd, dv_even, dv_odd, g_sk = _attn_bwd(qkv, sk, attn, dattn, comm=k6)
    pair_sums(k6, *trio)
    trio_flight, started = _start_exchange(
        "rs_chips_trio_start", [_rs_chips(lay[n], "s_" + n, "r_" + n) for n in trio],
        {**{"s_" + n: sums[n] for n in trio}, **{"r_" + n: _slots_shape(lay[n]) for n in trio}})
    behind = mix_norm + jnp.tile(started[0:1], (1, D_MODEL // 128))
    grad_x, gw_in, g_b, g_g1 = _inproj_bwd(dq, (dk_even, dk_odd), (dv_even, dv_odd), dc3, dgt, w_in_full, xs, xn,
                                           dh1, behind)
    grads["w_in"] = gw_in.reshape(lay["w_in"].whole())

    in_flight, started = _start_exchange("rs_pair_in_start", [_rs_pair(lay["w_in"], "g", "t")],
                                         {"g": grads["w_in"], "t": _theirs_shape(lay["w_in"])})
    parts = [loss_part, g_g1, g_b, jnp.pad(g_sk[:, 0], (0, 120))[None, :], g_cw, g_g2, g_fcw, g_fn]
    packed, at = _pack_rows([p.reshape(-1, 128) for p in parts])
    small_flight, started = _start_exchange("small_start", [_to_all("v", "slots")],
                                            {"v": packed + started[0:1], "slots": jnp.zeros((N_DEV, *packed.shape), F32)})
    halves = {"w_down": _chip_sum(sums["w_down"], slots["w_down"], lay["w_down"], pos, "chip_sum_w_down", after=started)}
    landed = _finish_exchange("rs_chips_up_wait", up_flight, after=halves["w_down"])
    halves["w_up"] = _chip_sum(landed["s"], landed["r"], lay["w_up"], pos, "chip_sum_w_up")
    landed = _finish_exchange("rs_pair_in_wait", in_flight, after=halves["w_up"])
    sums["w_in"] = _pair_sum(landed["g"], landed["t"], lay["w_in"], pos, "pair_sum_w_in")
    (in_flight, down_flight, up_flight), started = _start_exchanges("rs_chips_in_start", [
        ([_rs_chips(lay["w_in"], "s", "r")], {"s": sums["w_in"], "r": _slots_shape(lay["w_in"])}),
        ([_rs_share(lay["w_down"], "w_down")], {"w_down": halves["w_down"]}),
        ([_rs_share(lay["w_up"], "w_up")], {"w_up": halves["w_up"]})])
    landed = _finish_exchange("rs_chips_trio_wait", trio_flight, after=started)
    for n in trio:
        halves[n] = _chip_sum(landed["s_" + n], landed["r_" + n], lay[n], pos, "chip_sum_" + n)
    shared = _exchange("share_halves", [[_rs_share(lay[n], n) for n in trio]], bufs={n: halves[n] for n in trio})
    shared["w_down"] = _finish_exchange("share_down_wait", down_flight, after=shared[trio[-1]])["w_down"]
    shared["w_up"] = _finish_exchange("share_up_wait", up_flight, after=shared["w_down"])["w_up"]

    def adam(n, g, after=None):
        return _rowwise(lambda w, g, m, v: (g, *_adamw(w, g, m, v)), [w_of[n], g, m_of[n], v_of[n]], [F32] * 4,
                        "adamw_" + n, after=after)

    new_of, last = {}, None
    for n in ("w_down", "w_up", "w_out"):
        new_of[n] = adam(n, shared[n], last)
        last = new_of[n][1]
    new_of["w_ab"], new_of["w_cb"] = _adamw_pair(ab, cb, shared["w_br"], after=last)
    last = new_of["w_cb"][1]

    arrived = _finish_exchange("small_wait", small_flight, after=last)
    flat = lambda k: lambda t, j, chip: at[k] + j
    mine = lambda k, per_tap: lambda t, j, chip: at[k] + per_tap * t + (per_tap // N_CHIPS) * chip + j
    rows = lambda a: a.reshape(a.shape[1], 1, a.shape[2])
    small_p = [
        (mix_norm, m_mix_norm, v_mix_norm, flat(1)), (b_in, m_b_in, v_b_in, flat(2)), (sinks, m_sinks, v_sinks, flat(3)),
        (rows(conv_w), rows(m_conv_w), rows(v_conv_w), mine(4, CONV_W // 128)),
        (ffn_norm, m_ffn_norm, v_ffn_norm, flat(5)),
        (rows(ffn_conv_w), rows(m_ffn_conv_w), rows(v_ffn_conv_w), mine(6, FF2 // 128)),
        (final_norm[None, :], m_final_norm[None, :], v_final_norm[None, :], flat(7))]
    small_new = _adamw_small(pos, arrived["v"], arrived["slots"], small_p)
    loss = small_new[0][0, 0]
    small_g = small_new[1::4]
    small_new = [small_new[4 * k + 2:4 * k + 5] for k in range(len(small_p))]

    landed = _finish_exchange("rs_chips_in_wait", in_flight, after=small_new[0][0])
    half_in = _chip_sum(landed["s"], landed["r"], lay["w_in"], pos, "chip_sum_w_in")
    shared["w_in"] = _exchange("share_in", [[_rs_share(lay["w_in"], "w_in")]], bufs={"w_in": half_in})["w_in"]
    new_of["w_in"] = [a.T for a in adam("w_in", shared["w_in"])]
    big = ("w_in", "w_ab", "w_cb", "w_out", "w_up", "w_down")
    big_g = [new_of[n][0] for n in big]
    big_new = [new_of[n][1:] for n in big]

    order = [("s", 0), ("b", 0), ("s", 1), ("s", 2), ("s", 3), ("b", 1), ("b", 2), ("b", 3), ("s", 4), ("b", 4),
             ("s", 5), ("b", 5), ("s", 6)]
    shapes = [mix_norm.shape, w_in.shape, b_in.shape, sinks.shape, conv_w.shape, w_attn_branch.shape,
              w_conv_branch.shape, w_out.shape, ffn_norm.shape, w_up.shape, ffn_conv_w.shape, w_down.shape,
              final_norm.shape]
    out_g = [(small_g[k] if kind == "s" else big_g[k]).reshape(shp) for (kind, k), shp in zip(order, shapes)]
    news = [[(small_new[k][j] if kind == "s" else big_new[k][j]).reshape(shp) for (kind, k), shp in zip(order, shapes)]
            for j in range(3)]
    return (loss, grad_x[None], *out_g, *news[0], *news[1], *news[2])
```

```python
import functools

import jax
import jax.numpy as jnp
from jax import lax
from jax.experimental import pallas as pl
from jax.experimental.pallas import tpu as pltpu

F32 = jnp.float32
BF16 = jnp.bfloat16

D_MODEL = 1024
HEAD_DIM = 64
N_HEADS = 8
N_KV_HEADS = 2
GROUP = N_HEADS // N_KV_HEADS
BLOCK = 128
ATTN_SCALE = HEAD_DIM ** -0.5
ATTN_W = N_HEADS * HEAD_DIM
KV_W = N_KV_HEADS * HEAD_DIM
CONV_W = 512
QKV_W = ATTN_W + 2 * KV_W
C3_W = 3 * CONV_W
GATES_W = 2 * D_MODEL
IN_W = QKV_W + C3_W + GATES_W
D_FF = 2816
FF2 = 2 * D_FF
NORM_EPS = 1e-5
N_CHIPS = 4
IN_SHARD = IN_W // N_CHIPS
NEG = -1e30

ADAM_LR = 0.001
ADAM_B1 = 0.9
ADAM_B2 = 0.999
ADAM_EPS = 1e-08
ADAM_WD = 0.01
ADAM_STEP = 10

VMEM_LIMIT = 56 * 1024 * 1024
MESH = pl.DeviceIdType.MESH

NT = (((1,), (1,)), ((), ()))
TN = (((0,), (0,)), ((), ()))


def _params(*sem):
    return pltpu.CompilerParams(dimension_semantics=sem, vmem_limit_bytes=VMEM_LIMIT)


def _resident(shape):
    return pl.BlockSpec(shape, lambda *_: (0,) * len(shape), pipeline_mode=pl.Buffered(1))


def _sigmoid(v):
    return 0.5 * jnp.tanh(0.5 * v) + 0.5


def _rstd(v):
    return lax.rsqrt(jnp.mean(v * v, axis=-1, keepdims=True) + NORM_EPS)


def _rms_bwd(dy, v, rstd, g):
    vhat = v * rstd
    t = dy * g
    return rstd * (t - vhat * jnp.mean(t * vhat, axis=-1, keepdims=True)), dy * vhat


def _taps(z, cw):
    return cw[2:3] * z + cw[1:2] * pltpu.roll(z, 1, 0) + cw[0:1] * pltpu.roll(z, 2, 0)


def _causal_conv(z, prev, cw):
    edge = _taps(jnp.concatenate([prev, z[0:8]], axis=0), cw)
    return jnp.concatenate([edge[8:16], _taps(z, cw)[8:]], axis=0)


def _rows_after(z, nxt):
    n = z.shape[0]
    edge = jnp.concatenate([z[n - 8:n], nxt], axis=0)
    return tuple(jnp.concatenate([pltpu.roll(z, n - k, 0)[:n - 8], pltpu.roll(edge, 16 - k, 0)[0:8]], axis=0)
                 for k in (1, 2))


def _inproj_fwd(x, g1, w_in, b_in, tm, comm=None):
    s = x.shape[0]

    def body(x_ref, g_ref, w_ref, b_ref, xn_ref, qkv_ref, c3_ref, gt_ref):
        xf = x_ref[...]
        xn = (xf * _rstd(xf) * g_ref[...]).astype(BF16)
        xn_ref[...] = xn

        proj = (lax.dot_general(xn, w_ref[...], NT, preferred_element_type=F32) + b_ref[...]).astype(BF16)
        qkv_ref[...] = proj[:, :QKV_W]
        c3_ref[...] = proj[:, QKV_W:QKV_W + C3_W]
        gt_ref[...] = proj[:, QKV_W + C3_W:]

    row = lambda w: pl.BlockSpec((tm, w), lambda i: (i, 0))
    return _call(
        comm, body, name="inproj_fwd", grid=(s // tm,),
        in_specs=[row(D_MODEL), _resident((1, D_MODEL)), _resident((IN_W, D_MODEL)), _resident((1, IN_W))],
        out_specs=[row(D_MODEL), row(QKV_W), row(C3_W), row(GATES_W)],
        out_shape=[jax.ShapeDtypeStruct((s, D_MODEL), BF16), jax.ShapeDtypeStruct((s, QKV_W), BF16),
                   jax.ShapeDtypeStruct((s, C3_W), BF16), jax.ShapeDtypeStruct((s, GATES_W), BF16)],
        compiler_params=_params("parallel"),
    )(x, g1, w_in, b_in)


def _attn_bias():
    kj = jnp.arange(2 * BLOCK)[:, None]
    qi = (jnp.arange(GROUP * BLOCK) % BLOCK)[None, :]
    band = (kj > qi) & (kj <= qi + BLOCK)
    return jnp.stack([jnp.where(band & (kj >= BLOCK), 0.0, NEG), jnp.where(band, 0.0, NEG)]).astype(F32)


def _attn_bias_specs():
    shape = (None, 2 * BLOCK, GROUP * BLOCK)
    return pl.BlockSpec(shape, lambda i: (jnp.minimum(i, 1), 0, 0)), pl.BlockSpec(shape, lambda i: (1, 0, 0))


def _sink_row(sk_ref, h):
    lane = lax.broadcasted_iota(jnp.int32, (1, GROUP * BLOCK), 1)
    row = jnp.full((1, GROUP * BLOCK), sk_ref[h * GROUP], F32)
    for g in range(1, GROUP):
        row = jnp.where(lane >= g * BLOCK, sk_ref[h * GROUP + g], row)
    return row


def _stack_heads(t, h):
    return jnp.concatenate(
        [t[:, (h * GROUP + g) * HEAD_DIM:(h * GROUP + g + 1) * HEAD_DIM] for g in range(GROUP)], axis=0)


def _unstack_heads(per_kv):
    return jnp.concatenate(
        [t[g * BLOCK:(g + 1) * BLOCK] for t in per_kv for g in range(GROUP)], axis=1)


def _block_specs(n, steps):
    cur = lambda i: jnp.minimum(i, steps - 1)
    prev = lambda i: jnp.maximum(n * jnp.minimum(i, steps - 1) - 1, 0)
    kv = ATTN_W // KV_W
    return (pl.BlockSpec((n * BLOCK, ATTN_W), lambda i: (cur(i), 0)),
            pl.BlockSpec((BLOCK, KV_W), lambda i: (prev(i), kv)), pl.BlockSpec((n * BLOCK, KV_W), lambda i: (cur(i), kv)),
            pl.BlockSpec((BLOCK, KV_W), lambda i: (prev(i), kv + 1)),
            pl.BlockSpec((n * BLOCK, KV_W), lambda i: (cur(i), kv + 1)))


def _attn_fwd(qkv, sinks, comm=None):
    s = qkv.shape[0]
    n = min(4, s // BLOCK)
    steps = s // (n * BLOCK)

    def body(sk_ref, bias0_ref, bias1_ref, q_ref, kp_ref, kc_ref, vp_ref, vc_ref, o_ref):
        kc, vc = kc_ref[...], vc_ref[...]
        for b in range(n):
            rows, before = slice(b * BLOCK, (b + 1) * BLOCK), slice((b - 1) * BLOCK, b * BLOCK)
            kp, vp = (kp_ref[...], vp_ref[...]) if b == 0 else (kc[before], vc[before])
            q, bias = q_ref[rows, :], (bias0_ref if b == 0 else bias1_ref)[...]
            outs = []
            for h in range(N_KV_HEADS):
                hs = slice(h * HEAD_DIM, (h + 1) * HEAD_DIM)
                k2 = jnp.concatenate([kp[:, hs], kc[rows, hs]], axis=0)
                v2 = jnp.concatenate([vp[:, hs], vc[rows, hs]], axis=0)
                sc = lax.dot_general(k2, _stack_heads(q, h), NT, preferred_element_type=F32) * ATTN_SCALE + bias
                sink = _sink_row(sk_ref, h)
                m = jnp.maximum(jnp.max(sc, axis=0, keepdims=True), sink)
                p = jnp.exp(sc - m)
                den = jnp.sum(p, axis=0, keepdims=True) + jnp.exp(sink - m)
                out = lax.dot_general(v2, p.astype(BF16), TN, preferred_element_type=F32) / den
                outs.append(out.T)
            o_ref[rows, :] = _unstack_heads(outs).astype(BF16)

    return _call(
        comm, body, name="attn_fwd", grid=(steps,),
        in_specs=[pl.BlockSpec(memory_space=pltpu.SMEM), *_attn_bias_specs(), *_block_specs(n, steps)],
        out_specs=pl.BlockSpec((n * BLOCK, ATTN_W), lambda i: (i, 0)),
        out_shape=jax.ShapeDtypeStruct((s, ATTN_W), BF16),
        compiler_params=_params("parallel"),
    )(sinks, _attn_bias(), _attn_bias(), qkv, qkv, qkv, qkv, qkv)


def _mix_fwd(x, attn, c3, gates, conv_w, w_br, w_out, g2, tm, comm=None):
    s = x.shape[0]

    def body(x_ref, at_ref, c3_ref, gt_ref, cw_ref, wbr_ref, wo_ref, g_ref,
             conv_ref, a_ref, cv_ref, mg_ref, h1_ref, hn_ref, carry_ref):
        @pl.when(pl.program_id(0) == 0)
        def _():
            carry_ref[...] = jnp.zeros_like(carry_ref)

        c3v = c3_ref[...].astype(F32)
        cb, cc, cx = c3v[:, :CONV_W], c3v[:, CONV_W:2 * CONV_W], c3v[:, 2 * CONV_W:]
        z = cc * cx
        cz = _causal_conv(z, carry_ref[...], cw_ref[...])
        carry_ref[...] = z[tm - 8:tm]
        conv = (cb * cz).astype(BF16)
        conv_ref[...] = conv
        a = jnp.dot(at_ref[...], wbr_ref[:ATTN_W, :], preferred_element_type=F32)
        cv = jnp.dot(conv, wbr_ref[ATTN_W:, :], preferred_element_type=F32)
        a_ref[...] = a.astype(BF16)
        cv_ref[...] = cv.astype(BF16)
        gt = gt_ref[...].astype(F32)
        merged = (_sigmoid(gt[:, :D_MODEL]) * a + _sigmoid(gt[:, D_MODEL:]) * cv).astype(BF16)
        mg_ref[...] = merged
        h1 = x_ref[...] + jnp.dot(merged, wo_ref[...], preferred_element_type=F32)
        h1_ref[...] = h1
        hn_ref[...] = (h1 * _rstd(h1) * g_ref[...]).astype(BF16)

    row = lambda w: pl.BlockSpec((tm, w), lambda i: (i, 0))
    return _call(
        comm, body, name="mix_fwd", grid=(s // tm,),
        in_specs=[row(D_MODEL), row(ATTN_W), row(C3_W), row(GATES_W), _resident((3, CONV_W)),
                  _resident((ATTN_W + CONV_W, D_MODEL)), _resident((D_MODEL, D_MODEL)), _resident((1, D_MODEL))],
        out_specs=[row(CONV_W), row(D_MODEL), row(D_MODEL), row(D_MODEL), row(D_MODEL), row(D_MODEL)],
        out_shape=[jax.ShapeDtypeStruct((s, CONV_W), BF16), jax.ShapeDtypeStruct((s, D_MODEL), BF16),
                   jax.ShapeDtypeStruct((s, D_MODEL), BF16), jax.ShapeDtypeStruct((s, D_MODEL), BF16),
                   jax.ShapeDtypeStruct((s, D_MODEL), F32), jax.ShapeDtypeStruct((s, D_MODEL), BF16)],
        scratch_shapes=[pltpu.VMEM((8, CONV_W), F32)],
        compiler_params=_params("arbitrary"),
    )(x, attn, c3, gates, conv_w, w_br, w_out, g2)


def _ffn_fwd_loss(hn, h1, w_up, ffn_cw, w_down, g3, target, tm):
    s = hn.shape[0]

    def body(hn_ref, h1_ref, wu_ref, cw_ref, wd_ref, g_ref, t_ref,
             u_ref, up_ref, act_ref, dh2_ref, loss_ref, gfn_ref, carry_ref):
        @pl.when(pl.program_id(0) == 0)
        def _():
            carry_ref[...] = jnp.zeros_like(carry_ref)
            loss_ref[...] = jnp.zeros_like(loss_ref)
            gfn_ref[...] = jnp.zeros_like(gfn_ref)

        u = jnp.dot(hn_ref[...], wu_ref[...], preferred_element_type=F32)
        u_ref[...] = u.astype(BF16)
        up = _causal_conv(u, carry_ref[...], cw_ref[...])
        up_ref[...] = up
        carry_ref[...] = u[tm - 8:tm]
        gate, val = up[:, :D_FF], up[:, D_FF:]
        act = (gate * _sigmoid(gate) * val).astype(BF16)
        act_ref[...] = act
        h2 = h1_ref[...] + jnp.dot(act, wd_ref[...], preferred_element_type=F32)
        rstd = _rstd(h2)
        g = g_ref[...]
        err = h2 * rstd * g - t_ref[...]
        loss_ref[...] += jnp.sum(err * err) * (0.5 / D_MODEL)
        dh2, dg = _rms_bwd(err * (1.0 / D_MODEL), h2, rstd, g)
        dh2_ref[...] = dh2
        gfn_ref[...] += jnp.sum(dg, axis=0, keepdims=True)

    row = lambda w: pl.BlockSpec((tm, w), lambda i: (i, 0))
    acc = lambda w: pl.BlockSpec((1, w), lambda i: (0, 0))
    return pl.pallas_call(
        body, name="ffn_fwd_loss", grid=(s // tm,),
        in_specs=[row(D_MODEL), row(D_MODEL), _resident((D_MODEL, FF2)), _resident((3, FF2)),
                  _resident((D_FF, D_MODEL)), _resident((1, D_MODEL)), row(D_MODEL)],
        out_specs=[row(FF2), row(FF2), row(D_FF), row(D_MODEL), acc(128), acc(D_MODEL)],
        out_shape=[jax.ShapeDtypeStruct((s, FF2), BF16), jax.ShapeDtypeStruct((s, FF2), F32),
                   jax.ShapeDtypeStruct((s, D_FF), BF16),
                   jax.ShapeDtypeStruct((s, D_MODEL), F32), jax.ShapeDtypeStruct((1, 128), F32),
                   jax.ShapeDtypeStruct((1, D_MODEL), F32)],
        scratch_shapes=[pltpu.VMEM((8, FF2), F32)],
        compiler_params=_params("arbitrary"),
    )(hn, h1, w_up, ffn_cw, w_down, g3, target)


def _ffn_bwd(dh2, u, up, h1, w_up, ffn_cw, w_down, g2, tm):
    s = dh2.shape[0]
    nt = s // tm

    def body(dh2_ref, u_ref, up_ref, h1_ref, wu_ref, cw_ref, wd_ref, g_ref,
             du_ref, dh1_ref, gcw_ref, gg_ref, carry_ref):
        @pl.when(pl.program_id(0) == 0)
        def _():
            for ref in (carry_ref, gcw_ref, gg_ref):
                ref[...] = jnp.zeros_like(ref)

        dh2v = dh2_ref[...]
        dact = lax.dot_general(dh2v.astype(BF16), wd_ref[...], NT, preferred_element_type=F32)
        upv = up_ref[...]
        gate, val = upv[:, :D_FF], upv[:, D_FF:]
        sg = _sigmoid(gate)
        dval = dact * (gate * sg)
        dgate = dact * val * (sg * (1.0 + gate * (1.0 - sg)))
        dup = jnp.concatenate([dgate, dval], axis=1)
        dup1, dup2 = _rows_after(dup, carry_ref[...])
        carry_ref[...] = dup[0:8]
        u = u_ref[...].astype(F32)
        gcw_ref[2:3, :] += jnp.sum(dup * u, axis=0, keepdims=True)
        gcw_ref[1:2, :] += jnp.sum(dup1 * u, axis=0, keepdims=True)
        gcw_ref[0:1, :] += jnp.sum(dup2 * u, axis=0, keepdims=True)
        cw = cw_ref[...]
        du = (cw[2:3] * dup + cw[1:2] * dup1 + cw[0:1] * dup2).astype(BF16)
        du_ref[...] = du
        dhn = lax.dot_general(du, wu_ref[...], NT, preferred_element_type=F32)
        h1v = h1_ref[...]
        dh1, dg = _rms_bwd(dhn, h1v, _rstd(h1v), g_ref[...])
        dh1_ref[...] = dh2v + dh1
        gg_ref[...] += jnp.sum(dg, axis=0, keepdims=True)

    row = lambda w: pl.BlockSpec((tm, w), lambda i: (nt - 1 - i, 0))
    return pl.pallas_call(
        body, name="ffn_bwd", grid=(nt,),
        in_specs=[row(D_MODEL), row(FF2), row(FF2),
                  row(D_MODEL), _resident((D_MODEL, FF2)), _resident((3, FF2)), _resident((D_FF, D_MODEL)),
                  _resident((1, D_MODEL))],
        out_specs=[row(FF2), row(D_MODEL), pl.BlockSpec((3, FF2), lambda i: (0, 0)),
                   pl.BlockSpec((1, D_MODEL), lambda i: (0, 0))],
        out_shape=[jax.ShapeDtypeStruct((s, FF2), BF16), jax.ShapeDtypeStruct((s, D_MODEL), F32),
                   jax.ShapeDtypeStruct((3, FF2), F32), jax.ShapeDtypeStruct((1, D_MODEL), F32)],
        scratch_shapes=[pltpu.VMEM((8, FF2), F32)],
        compiler_params=_params("arbitrary"),
    )(dh2, u, up, h1, w_up, ffn_cw, w_down, g2)


def _mix_bwd(dh1, gates, a, cv, c3, attn, conv, merged, conv_w, w_br, w_out, tm, comm=None):
    s = dh1.shape[0]
    nt = s // tm
    halo = 16

    def body(dh1_ref, gt_ref, a_ref, cv_ref, c3_ref, ch_ref, at_ref, cn_ref, mg_ref, cw_ref, wbr_ref,
             wo_ref, dat_ref, dc3_ref, dgt_ref, gcw_ref, gbr_ref, gout_ref, carry_ref, br_acc, out_acc):
        i = pl.program_id(0)

        @pl.when(i == 0)
        def _():
            for ref in (carry_ref, gcw_ref, br_acc, out_acc):
                ref[...] = jnp.zeros_like(ref)

        dh1v = dh1_ref[...].astype(BF16)
        out_acc[...] += lax.dot_general(mg_ref[...], dh1v, TN, preferred_element_type=F32)
        dm = lax.dot_general(dh1v, wo_ref[...], NT, preferred_element_type=F32)
        gt = gt_ref[...].astype(F32)
        sa, sc = _sigmoid(gt[:, :D_MODEL]), _sigmoid(gt[:, D_MODEL:])
        da = (dm * sa).astype(BF16)
        dcv = (dm * sc).astype(BF16)
        br_acc[:ATTN_W, :] += lax.dot_general(at_ref[...], da, TN, preferred_element_type=F32)
        br_acc[ATTN_W:, :] += lax.dot_general(cn_ref[...], dcv, TN, preferred_element_type=F32)
        dgt_ref[...] = jnp.concatenate(
            [dm * a_ref[...].astype(F32) * (sa * (1.0 - sa)), dm * cv_ref[...].astype(F32) * (sc * (1.0 - sc))],
            axis=1).astype(BF16)
        dat_ref[...] = lax.dot_general(da, wbr_ref[:ATTN_W, :], NT, preferred_element_type=F32).astype(BF16)
        dconv = lax.dot_general(dcv, wbr_ref[ATTN_W:, :], NT, preferred_element_type=F32)
        c3v = c3_ref[...].astype(F32)
        cb, cc, cx = c3v[:, :CONV_W], c3v[:, CONV_W:2 * CONV_W], c3v[:, 2 * CONV_W:]
        z = cc * cx
        chv = ch_ref[...].astype(F32)[halo - 8:halo] * (i < nt - 1).astype(F32)
        zh = chv[:, CONV_W:2 * CONV_W] * chv[:, 2 * CONV_W:]
        cw = cw_ref[...]
        cz = _causal_conv(z, zh, cw)
        dcz = dconv * cb
        dcz1, dcz2 = _rows_after(dcz, carry_ref[...])
        carry_ref[...] = dcz[0:8]
        gcw_ref[2:3, :] += jnp.sum(dcz * z, axis=0, keepdims=True)
        gcw_ref[1:2, :] += jnp.sum(dcz1 * z, axis=0, keepdims=True)
        gcw_ref[0:1, :] += jnp.sum(dcz2 * z, axis=0, keepdims=True)
        dz = cw[2:3] * dcz + cw[1:2] * dcz1 + cw[0:1] * dcz2
        dc3_ref[...] = jnp.concatenate([dconv * cz, dz * cx, dz * cc], axis=1).astype(BF16)

        @pl.when(i == nt - 1)
        def _():
            gbr_ref[...] = br_acc[...].astype(BF16)
            gout_ref[...] = out_acc[...].astype(BF16)

    row = lambda w: pl.BlockSpec((tm, w), lambda i: (nt - 1 - i, 0))
    return _call(
        comm, body, name="mix_bwd", grid=(nt,),
        in_specs=[row(D_MODEL), row(GATES_W), row(D_MODEL), row(D_MODEL), row(C3_W),
                  pl.BlockSpec((halo, C3_W), lambda i: (jnp.maximum((nt - 1 - i) * (tm // halo) - 1, 0), 0)),
                  row(ATTN_W), row(CONV_W), row(D_MODEL), _resident((3, CONV_W)),
                  _resident((ATTN_W + CONV_W, D_MODEL)), _resident((D_MODEL, D_MODEL))],
        out_specs=[row(ATTN_W), row(C3_W), row(GATES_W), pl.BlockSpec((3, CONV_W), lambda i: (0, 0)),
                   _resident((ATTN_W + CONV_W, D_MODEL)), _resident((D_MODEL, D_MODEL))],
        out_shape=[jax.ShapeDtypeStruct((s, ATTN_W), BF16), jax.ShapeDtypeStruct((s, C3_W), BF16),
                   jax.ShapeDtypeStruct((s, GATES_W), BF16), jax.ShapeDtypeStruct((3, CONV_W), F32),
                   jax.ShapeDtypeStruct((ATTN_W + CONV_W, D_MODEL), BF16),
                   jax.ShapeDtypeStruct((D_MODEL, D_MODEL), BF16)],
        scratch_shapes=[pltpu.VMEM((8, CONV_W), F32), pltpu.VMEM((ATTN_W + CONV_W, D_MODEL), F32),
                        pltpu.VMEM((D_MODEL, D_MODEL), F32)],
        compiler_params=_params("arbitrary"),
    )(dh1, gates, a, cv, c3, c3, attn, conv, merged, conv_w, w_br, w_out)


def _attn_bwd(qkv, sinks, o, do, comm=None):
    s = qkv.shape[0]
    npair = s // (2 * BLOCK)

    def one_block(sk_ref, bias, q, kp, kc, vp, vc, ov, dov, dsk_ref):
        dqs, dks, dvs = [], [], []
        for h in range(N_KV_HEADS):
            hs = slice(h * HEAD_DIM, (h + 1) * HEAD_DIM)
            k2 = jnp.concatenate([kp[:, hs], kc[:, hs]], axis=0)
            v2 = jnp.concatenate([vp[:, hs], vc[:, hs]], axis=0)
            qg, og, dog = _stack_heads(q, h), _stack_heads(ov, h), _stack_heads(dov, h)
            sc = lax.dot_general(k2, qg, NT, preferred_element_type=F32) * ATTN_SCALE + bias
            sink = _sink_row(sk_ref, h)
            m = jnp.maximum(jnp.max(sc, axis=0, keepdims=True), sink)
            p = jnp.exp(sc - m)
            psink = jnp.exp(sink - m)
            inv = 1.0 / (jnp.sum(p, axis=0, keepdims=True) + psink)
            p = p * inv
            delta = jnp.sum(dog.astype(F32) * og.astype(F32), axis=1, keepdims=True).T
            dp = lax.dot_general(v2, dog, NT, preferred_element_type=F32)
            ds = (p * (dp - delta)).astype(BF16)
            dqs.append((lax.dot_general(k2, ds, TN, preferred_element_type=F32) * ATTN_SCALE).T)
            dks.append(jnp.dot(ds, qg, preferred_element_type=F32) * ATTN_SCALE)
            dvs.append(jnp.dot(p.astype(BF16), dog, preferred_element_type=F32))
            dsink = -(psink * inv * delta)
            for g in range(GROUP):
                r = h * GROUP + g
                dsk_ref[r:r + 1, :] += jnp.sum(dsink[:, g * BLOCK:(g + 1) * BLOCK])
        return _unstack_heads(dqs), jnp.concatenate(dks, axis=1), jnp.concatenate(dvs, axis=1)

    def body(sk_ref, bias0_ref, bias1_ref, q_ref, kp_ref, kc_ref, vp_ref, vc_ref, o_ref, do_ref,
             dq_ref, dke_ref, dko_ref, dve_ref, dvo_ref, dsk_ref, ck_ref, cvv_ref):
        i = pl.program_id(0)

        @pl.when(i == 0)
        def _():
            for ref in (ck_ref, cvv_ref, dsk_ref):
                ref[...] = jnp.zeros_like(ref)

        @pl.when(i < npair)
        def _():
            kc, vc = kc_ref[...], vc_ref[...]
            first, second = slice(0, BLOCK), slice(BLOCK, 2 * BLOCK)
            dq0, dk0, dv0 = one_block(sk_ref, bias0_ref[...], q_ref[first, :], kp_ref[...], kc[first], vp_ref[...],
                                      vc[first], o_ref[first, :], do_ref[first, :], dsk_ref)
            dq1, dk1, dv1 = one_block(sk_ref, bias1_ref[...], q_ref[second, :], kc[first], kc[second], vc[first],
                                      vc[second], o_ref[second, :], do_ref[second, :], dsk_ref)
            dq_ref[first, :] = dq0.astype(BF16)
            dq_ref[second, :] = dq1.astype(BF16)
            dko_ref[...] = (ck_ref[...] + dk0[:BLOCK]).astype(BF16)
            dvo_ref[...] = (cvv_ref[...] + dv0[:BLOCK]).astype(BF16)
            dke_ref[...] = (dk0[BLOCK:] + dk1[:BLOCK]).astype(BF16)
            dve_ref[...] = (dv0[BLOCK:] + dv1[:BLOCK]).astype(BF16)
            ck_ref[...] = dk1[BLOCK:]
            cvv_ref[...] = dv1[BLOCK:]

        @pl.when(i == npair)
        def _():
            dko_ref[...] = ck_ref[...].astype(BF16)
            dvo_ref[...] = cvv_ref[...].astype(BF16)

    cur = lambda i: jnp.minimum(i, npair - 1)
    done = lambda i: jnp.maximum(i - 1, 0)
    rows = pl.BlockSpec((2 * BLOCK, ATTN_W), lambda i: (cur(i), 0))
    even = pl.BlockSpec((BLOCK, KV_W), lambda i: (cur(i), 0))
    odd = pl.BlockSpec((BLOCK, KV_W), lambda i: (done(i), 0))
    half = jax.ShapeDtypeStruct((s // 2, KV_W), BF16)
    return _call(
        comm, body, name="attn_bwd", grid=(npair + 1,),
        in_specs=[pl.BlockSpec(memory_space=pltpu.SMEM), *_attn_bias_specs(), *_block_specs(2, npair), rows, rows],
        out_specs=[rows, even, odd, even, odd, pl.BlockSpec((N_HEADS, 128), lambda i: (0, 0))],
        out_shape=[jax.ShapeDtypeStruct((s, ATTN_W), BF16), half, half, half, half,
                   jax.ShapeDtypeStruct((N_HEADS, 128), F32)],
        scratch_shapes=[pltpu.VMEM((BLOCK, KV_W), F32), pltpu.VMEM((BLOCK, KV_W), F32)],
        compiler_params=_params("arbitrary"),
    )(sinks, _attn_bias(), _attn_bias(), qkv, qkv, qkv, qkv, qkv, o, do)


def _inproj_bwd(dq, dk, dv, dc3, dgt, w_in, x, xn, dh1, g1):
    s = x.shape[0]
    tm = min(2 * BLOCK, s)
    nt = s // tm

    def body(dq_ref, dke_ref, dko_ref, dve_ref, dvo_ref, dc3_ref, dgt_ref, w_ref, x_ref, xn_ref, dh1_ref, g_ref,
             dx_ref, gw_ref, gb_ref, gg_ref, acc_ref):
        i = pl.program_id(0)

        @pl.when(i == 0)
        def _():
            for ref in (gb_ref, gg_ref, acc_ref):
                ref[...] = jnp.zeros_like(ref)

        dk = jnp.concatenate([dke_ref[...], dko_ref[...]], axis=0)
        dv = jnp.concatenate([dve_ref[...], dvo_ref[...]], axis=0)
        dp = jnp.concatenate([dq_ref[...], dk, dv, dc3_ref[...], dgt_ref[...]], axis=1)
        acc_ref[...] += lax.dot_general(dp, xn_ref[...], TN, preferred_element_type=F32)
        gb_ref[...] += jnp.sum(dp.astype(F32), axis=0, keepdims=True)
        dxn = jnp.dot(dp, w_ref[...], preferred_element_type=F32)
        xf = x_ref[...]
        dx, dg = _rms_bwd(dxn, xf, _rstd(xf), g_ref[...])
        dx_ref[...] = dh1_ref[...] + dx
        gg_ref[...] += jnp.sum(dg, axis=0, keepdims=True)

        @pl.when(i == nt - 1)
        def _():
            gw_ref[...] = acc_ref[...].astype(BF16)

    row = lambda w: pl.BlockSpec((tm, w), lambda i: (i, 0))
    acc = lambda w: pl.BlockSpec((1, w), lambda i: (0, 0))
    block = pl.BlockSpec((tm // 2, KV_W), lambda i: (i, 0))
    return pl.pallas_call(
        body, name="inproj_bwd", grid=(nt,),
        in_specs=[row(ATTN_W), block, block, block, block, row(C3_W), row(GATES_W), _resident((IN_W, D_MODEL)),
                  row(D_MODEL), row(D_MODEL), row(D_MODEL), _resident((1, D_MODEL))],
        out_specs=[row(D_MODEL), _resident((IN_W, D_MODEL)), acc(IN_W), acc(D_MODEL)],
        out_shape=[jax.ShapeDtypeStruct((s, D_MODEL), F32), jax.ShapeDtypeStruct((IN_W, D_MODEL), BF16),
                   jax.ShapeDtypeStruct((1, IN_W), F32), jax.ShapeDtypeStruct((1, D_MODEL), F32)],
        scratch_shapes=[pltpu.VMEM((IN_W, D_MODEL), F32)],
        compiler_params=_params("arbitrary"),
    )(dq, *dk, *dv, dc3, dgt, w_in, x, xn, dh1, g1)


def _wgrad(a, b, bm, bn, bk, name, comm=None):
    s, m = a.shape
    n = b.shape[1]
    nk = s // bk

    def body(a_ref, b_ref, o_ref, acc_ref):
        k = pl.program_id(2)

        @pl.when(k == 0)
        def _():
            acc_ref[...] = jnp.zeros_like(acc_ref)

        acc_ref[...] += lax.dot_general(a_ref[...].astype(BF16), b_ref[...].astype(BF16), TN,
                                        preferred_element_type=F32)

        @pl.when(k == nk - 1)
        def _():
            o_ref[...] = acc_ref[...].astype(BF16)

    return _call(
        comm, body, name=name, grid=(m // bm, n // bn, nk),
        in_specs=[pl.BlockSpec((bk, bm), lambda i, j, k: (k, i)), pl.BlockSpec((bk, bn), lambda i, j, k: (k, j))],
        out_specs=pl.BlockSpec((bm, bn), lambda i, j, k: (i, j)),
        out_shape=jax.ShapeDtypeStruct((m, n), BF16),
        scratch_shapes=[pltpu.VMEM((bm, bn), F32)],
        compiler_params=_params("parallel", "parallel", "arbitrary"),
    )(a, b)


class _Carry:
    def __init__(self, jobs, reads=None, bufs=None, fresh=None):
        self.jobs, self.reads, self.bufs, self.fresh = jobs, reads or {}, bufs or {}, fresh or {}
        self.out = {}


class _Job:
    def __init__(self, n_sems, plan):
        self.n_sems, self.plan = n_sems, plan


def _plan_all(jobs, hbm, send, recv):
    pos = _position()
    starts, waits, base = [], [], 0
    for job in jobs:
        s, w = job.plan(hbm, pos, send, recv, base)
        starts, waits, base = starts + s, waits + w, base + job.n_sems
    return starts, waits


def _call(comm, body, **kw):
    if comm is None:
        return pl.pallas_call(body, **kw)
    grid = kw["grid"]
    single = not isinstance(kw["out_shape"], (list, tuple))
    out_shape = [kw["out_shape"]] if single else list(kw["out_shape"])
    out_specs = [kw["out_specs"]] if single else list(kw["out_specs"])
    in_specs = list(kw["in_specs"])
    scratch = list(kw.get("scratch_shapes", ()))
    r_names, b_names, f_names = list(comm.reads), list(comm.bufs), list(comm.fresh)
    n_args, n_out, n_scr = len(in_specs), len(out_shape), len(scratch)
    n_sems = sum(j.n_sems for j in comm.jobs)

    def wrapped(*refs):
        k = n_args
        hbm = dict(zip(r_names, refs[k:k + len(r_names)]))
        k += len(r_names) + len(b_names)
        outs = refs[k:k + n_out]
        k += n_out
        hbm.update(zip(b_names + f_names, refs[k:k + len(b_names) + len(f_names)]))
        k += len(b_names) + len(f_names)
        send, recv = refs[k + n_scr:]
        starts, waits = _plan_all(comm.jobs, hbm, send, recv)
        ids = [pl.program_id(a) for a in range(len(grid))]
        first = functools.reduce(jnp.logical_and, [i == 0 for i in ids])
        last = functools.reduce(jnp.logical_and, [i == g - 1 for i, g in zip(ids, grid)])

        @pl.when(first)
        def _():
            for cp in starts:
                cp.start()

        body(*refs[:n_args], *outs, *refs[k:k + n_scr])

        @pl.when(last)
        def _():
            for cp in waits:
                cp.wait_recv()
            for cp in starts:
                cp.wait_send()

    sems = pltpu.SemaphoreType.DMA((n_sems,))
    held = [jax.ShapeDtypeStruct(a.shape, a.dtype) for a in comm.bufs.values()] + list(comm.fresh.values())
    call = pl.pallas_call(
        wrapped, name=kw["name"], grid=grid,
        in_specs=in_specs + [_ANY] * (len(r_names) + len(b_names)),
        out_specs=out_specs + [_ANY] * len(held),
        out_shape=out_shape + held,
        input_output_aliases={n_args + len(r_names) + i: n_out + i for i in range(len(b_names))},
        scratch_shapes=scratch + [sems, sems],
        compiler_params=_params(*["arbitrary"] * len(grid)),
    )

    def run(*args):
        res = call(*args, *comm.reads.values(), *comm.bufs.values())
        comm.out = dict(zip(b_names + f_names, res[n_out:]))
        return res[0] if single else res[:n_out]

    return run


def _exchange(name, phases, reads=None, bufs=None, fresh=None):
    comm = _Carry([j for ph in phases for j in ph], reads, bufs, fresh)
    r_names, b_names, f_names = list(comm.reads), list(comm.bufs), list(comm.fresh)
    n_sems = sum(j.n_sems for j in comm.jobs)

    def body(*refs):
        hbm = dict(zip(r_names, refs[:len(r_names)]))
        k = len(r_names) + len(b_names)
        hbm.update(zip(b_names + f_names, refs[k:k + len(b_names) + len(f_names)]))
        send, recv = refs[-2:]
        pos = _position()
        started, base = [], 0
        for ph in phases:
            waits = []
            for job in ph:
                s, w = job.plan(hbm, pos, send, recv, base)
                base += job.n_sems
                for cp in s:
                    cp.start()
                started, waits = started + s, waits + w
            for cp in waits:
                cp.wait_recv()
        for cp in started:
            cp.wait_send()

    sems = pltpu.SemaphoreType.DMA((n_sems,))
    held = [jax.ShapeDtypeStruct(a.shape, a.dtype) for a in comm.bufs.values()] + list(comm.fresh.values())
    res = pl.pallas_call(
        body, name=name, in_specs=[_ANY] * (len(r_names) + len(b_names)), out_specs=[_ANY] * len(held),
        out_shape=held, input_output_aliases={len(r_names) + i: i for i in range(len(b_names))},
        scratch_shapes=[sems, sems],
    )(*comm.reads.values(), *comm.bufs.values())
    return dict(zip(b_names + f_names, res))


_HBM = pl.BlockSpec(memory_space=pltpu.HBM)
_SEM = pl.BlockSpec(memory_space=pltpu.SEMAPHORE)
_EFFECT = pltpu.SideEffectType.DATAFLOW_SIDE_EFFECTING


def _start_exchanges(name, groups):
    names = [list(arrays) for _, arrays in groups]
    first = [sum(len(ns) for ns in names[:g]) for g in range(len(groups))]
    n, ng = sum(len(ns) for ns in names), len(groups)

    def body(*refs):
        for g, (jobs, _) in enumerate(groups):
            hbm = dict(zip(names[g], refs[first[g]:first[g] + len(names[g])]))
            for cp in _plan_all(jobs, hbm, refs[n + 2 * g], refs[n + 2 * g + 1])[0]:
                cp.start()
        refs[-1][...] = jnp.zeros_like(refs[-1])

    given = [pltpu.with_memory_space_constraint(
        a if isinstance(a, jax.Array) else lax.empty(a.shape, a.dtype), pltpu.HBM)
        for _, arrays in groups for a in arrays.values()]
    sems = [pltpu.SemaphoreType.DMA((sum(j.n_sems for j in jobs),)) for jobs, _ in groups for _ in range(2)]
    res = pl.pallas_call(
        body, name=name,
        out_shape=(*sems, *[pltpu.HBM(a.shape, a.dtype) for a in given], jax.ShapeDtypeStruct((8, 128), F32)),
        in_specs=[_HBM] * n, out_specs=(*[_SEM] * (2 * ng), *[_HBM] * n, pl.BlockSpec(memory_space=pltpu.VMEM)),
        input_output_aliases={i: 2 * ng + i for i in range(n)},
        compiler_params=pltpu.CompilerParams(has_side_effects=_EFFECT),
    )(*given)
    held = res[2 * ng:2 * ng + n]
    states = [(names[g], groups[g][0], res[2 * g], res[2 * g + 1], held[first[g]:first[g] + len(names[g])])
              for g in range(ng)]
    return states, res[-1]


def _start_exchange(name, jobs, arrays):
    states, token = _start_exchanges(name, [(jobs, arrays)])
    return states[0], token


def _finish_exchange(name, state, after):
    names, jobs, send_sem, recv_sem, held = state
    n = len(names)

    def body(*refs):
        hbm = dict(zip(names, refs[:n]))
        send, recv = refs[n:n + 2]
        starts, waits = _plan_all(jobs, hbm, send, recv)
        for cp in waits:
            cp.wait_recv()
        for cp in starts:
            cp.wait_send()

    res = pl.pallas_call(
        body, name=name, out_shape=tuple(pltpu.HBM(a.shape, a.dtype) for a in held),
        in_specs=[_HBM] * n + [_SEM, _SEM, _ANY], out_specs=tuple([_HBM] * n),
        input_output_aliases={i: i for i in range(n)},
        compiler_params=pltpu.CompilerParams(has_side_effects=_EFFECT),
    )(*held, send_sem, recv_sem, after)
    return dict(zip(names, res))


def _relay_exchange(name, state, jobs, after):
    names, arrived, send_sem, recv_sem, held = state
    n = len(names)

    def body(*refs):
        hbm = dict(zip(names, refs[:n]))
        starts, waits = _plan_all(arrived, hbm, refs[n], refs[n + 1])
        for cp in waits:
            cp.wait_recv()
        for cp in starts:
            cp.wait_send()
        for cp in _plan_all(jobs, hbm, refs[n + 3], refs[n + 4])[0]:
            cp.start()
        refs[-1][...] = jnp.zeros_like(refs[-1])

    sems = pltpu.SemaphoreType.DMA((sum(j.n_sems for j in jobs),))
    res = pl.pallas_call(
        body, name=name,
        out_shape=(sems, sems, *[pltpu.HBM(a.shape, a.dtype) for a in held], jax.ShapeDtypeStruct((8, 128), F32)),
        in_specs=[_HBM] * n + [_SEM, _SEM, _ANY],
        out_specs=(_SEM, _SEM, *[_HBM] * n, pl.BlockSpec(memory_space=pltpu.VMEM)),
        input_output_aliases={i: 2 + i for i in range(n)},
        compiler_params=pltpu.CompilerParams(has_side_effects=_EFFECT),
    )(*held, send_sem, recv_sem, after)
    return (names, jobs, res[0], res[1], res[2:2 + n]), res[-1]


def _row_tile(rows, bytes_per_row):
    best = 16
    for t in range(16, rows + 1, 16):
        if rows % t == 0 and t * bytes_per_row <= 9 * 1024 * 1024:
            best = t
    return best


def _rowwise(fn, ins, out_dtypes, name, after=None):
    rows, cols = ins[0].shape
    per_row = sum(cols * a.dtype.itemsize for a in ins) + sum(cols * jnp.dtype(d).itemsize for d in out_dtypes)
    tr = _row_tile(rows, per_row)
    n_in = len(ins)

    def body(*refs):
        outs = fn(*[r[...] for r in refs[:n_in]])
        for o_ref, o in zip(refs[-len(out_dtypes):], outs):
            o_ref[...] = o.astype(o_ref.dtype)

    tile = pl.BlockSpec((tr, cols), lambda i: (i, 0))
    behind = [] if after is None else [after]
    return pl.pallas_call(
        body, name=name, grid=(rows // tr,),
        in_specs=[tile] * n_in + [pl.BlockSpec((8, 128), lambda i: (0, 0))] * len(behind),
        out_specs=[tile] * len(out_dtypes),
        out_shape=[jax.ShapeDtypeStruct((rows, cols), d) for d in out_dtypes],
        compiler_params=_params("parallel"),
    )(*ins, *behind)


def _tiled(fn, name, grid, pos, ins, outs):
    n_in = len(ins)

    def body(pos_ref, *refs):
        res = fn(*[r[...] for r in refs[:n_in]])
        for o_ref, o in zip(refs[n_in:], res):
            o_ref[...] = o.astype(o_ref.dtype)

    return pl.pallas_call(
        body, name=name,
        grid_spec=pltpu.PrefetchScalarGridSpec(
            num_scalar_prefetch=1, grid=grid,
            in_specs=[pl.BlockSpec(bs, im) for _, bs, im in ins],
            out_specs=[pl.BlockSpec(bs, im) for _, _, bs, im in outs]),
        out_shape=[jax.ShapeDtypeStruct(s, d) for s, d, _, _ in outs],
        compiler_params=_params("parallel"),
    )(pos, *[a for a, _, _ in ins])


def _adamw(w, g, m, v):
    m = ADAM_B1 * m + (1.0 - ADAM_B1) * g
    v = ADAM_B2 * v + (1.0 - ADAM_B2) * (g * g)
    m_hat = m / (1.0 - ADAM_B1 ** ADAM_STEP)
    v_hat = v / (1.0 - ADAM_B2 ** ADAM_STEP)
    return -ADAM_LR * (m_hat / (jnp.sqrt(v_hat) + ADAM_EPS) + ADAM_WD * w), m, v


def _adamw_small(pos, own, slots, params):
    n = len(params)

    def body(pos_ref, own_ref, slots_ref, *refs):
        ins, outs, total_ref = refs[:3 * n], refs[3 * n:-1], refs[-1]
        chip = pos_ref[0]
        idx = 2 * chip + pos_ref[1]
        term = lambda q: jnp.where(idx == q, own_ref[...], slots_ref[q])
        acc = term(0)
        for q in range(1, N_DEV):
            acc = acc + term(q)
        total_ref[...] = acc
        outs[0][...] = total_ref[0:1, :]
        for k, (w, _, _, row) in enumerate(params):
            width = min(w.shape[-1], 128)
            for t in range(w.shape[0]):
                for j in range(w.shape[-1] // width):
                    lanes = slice(j * width, (j + 1) * width)
                    at = (slice(t, t + 1), lanes) if w.ndim == 2 else (t, slice(None), lanes)
                    g = total_ref[pl.ds(row(t, j, chip), 1), :][:, :width]
                    new = _adamw(ins[3 * k][at], g, ins[3 * k + 1][at], ins[3 * k + 2][at])
                    for o_ref, o in zip(outs[1 + 4 * k:5 + 4 * k], (g, *new)):
                        o_ref[at] = o

    vmem = pl.BlockSpec(memory_space=pltpu.VMEM)
    return pl.pallas_call(
        body, name="adamw_small",
        in_specs=[pl.BlockSpec(memory_space=pltpu.SMEM)] + [vmem] * (2 + 3 * n),
        out_shape=[jax.ShapeDtypeStruct((1, 128), F32)]
        + [jax.ShapeDtypeStruct(p[0].shape, F32) for p in params for _ in range(4)],
        scratch_shapes=[pltpu.VMEM(own.shape, F32)],
    )(pos, own, slots, *[a for p in params for a in p[:3]])


class _Layout:
    def __init__(self, rows, cols, stacked):
        self.rows, self.cols, self.stacked = rows, cols, stacked

    def whole(self, rows=None):
        r = self.rows if rows is None else rows
        return (N_CHIPS, r, self.cols) if self.stacked else (r, N_CHIPS * self.cols)

    def part_rows(self, h, q=0, nq=1):
        n = self.rows // 2 // nq
        return pl.ds(pl.multiple_of(h * (self.rows // 2) + q * n, 16), n)

    def half_rows(self, h):
        return self.part_rows(h)

    def block(self, ref, p, rows=slice(None)):
        if self.stacked:
            return ref.at[p, rows, :]
        return ref.at[rows, pl.ds(pl.multiple_of(p * self.cols, 128), self.cols)]

    def all_chips(self, ref, rows):
        return ref.at[:, rows, :] if self.stacked else ref.at[rows, :]


BIG = (
    _Layout(IN_SHARD, D_MODEL, True),
    _Layout(ATTN_W + CONV_W, D_MODEL // N_CHIPS, False),
    _Layout(D_MODEL // N_CHIPS, D_MODEL, True),
    _Layout(D_MODEL, FF2 // N_CHIPS, False),
    _Layout(D_FF // N_CHIPS, D_MODEL, True),
)
N_BIG = len(BIG)
_ANY = pl.BlockSpec(memory_space=pl.ANY)


def _position():
    x, y, c = lax.axis_index("x"), lax.axis_index("y"), lax.axis_index("c")
    return x, y, c, 2 * x + y


def _core_of_chip(p, c):
    return (p >> 1, p & 1, c)


def _place_cast(shard, lay, pos, name, after=None):
    rows, cols = shard.shape
    tr = _row_tile(rows, cols * 6)
    if lay.stacked:
        out = (lay.whole(), BF16, (None, tr, cols), lambda i, pos: (pos[0], i, 0))
    else:
        out = (lay.whole(), BF16, (tr, cols), lambda i, pos: (i, pos[0]))
    ins = [(shard, (tr, cols), lambda i, pos: (i, 0))]
    if after is not None:
        ins.append((after, (8, 128), lambda i, pos: (0, 0)))
    return _tiled(lambda a, *_: (a,), name, (rows // tr,), pos, ins, [out])[0]


def _place_cast_pair(top, bottom, lay, pos, name, after=None):
    rows, cols = top.shape
    ins = [(top, (rows, cols), lambda i, pos: (0, 0)), (bottom, (rows, cols), lambda i, pos: (0, 0))]
    if after is not None:
        ins.append((after, (8, 128), lambda i, pos: (0, 0)))
    return _tiled(lambda a, b, *_: (jnp.concatenate([a, b], axis=0),), name, (1,), pos, ins,
                  [(lay.whole(), BF16, (2 * rows, cols), lambda i, pos: (0, pos[0]))])[0]


def _adamw_pair(top, bottom, g, after=None):
    rows = top[0].shape[0]

    def body(*refs):
        (wa, ma, va, wb, mb, vb, g_ref), outs = refs[:7], refs[-8:]
        for (w, m, v), gg, o in (((wa, ma, va), g_ref[:rows], outs[:4]), ((wb, mb, vb), g_ref[rows:], outs[4:])):
            for o_ref, val in zip(o, (gg, *_adamw(w[...], gg, m[...], v[...]))):
                o_ref[...] = val

    behind = [] if after is None else [after]
    res = pl.pallas_call(
        body, name="adamw_w_br", out_shape=[jax.ShapeDtypeStruct(top[0].shape, F32)] * 8,
        in_specs=[pl.BlockSpec(memory_space=pltpu.VMEM)] * 7 + [_ANY] * len(behind),
    )(*top, *bottom, g, *behind)
    return res[:4], res[4:]


def _remote(src, dst, send, recv, k, device):
    return pltpu.make_async_remote_copy(src_ref=src, dst_ref=dst, send_sem=send.at[k], recv_sem=recv.at[k],
                                        device_id=device, device_id_type=MESH)


def _arrival(dst, send, recv, k, me):
    return _remote(dst, dst, send, recv, k, me)


def _gather_ici(lay, name, q=0, nq=1):
    def plan(hbm, pos, send, recv, base):
        x, y, c, me = pos
        rows = lay.part_rows(c, q, nq)
        mine = lay.block(hbm[name], me, rows)
        starts = [_remote(mine, mine, send, recv, base + d - 1, _core_of_chip(me ^ d, c)) for d in (1, 2, 3)]
        waits = [_arrival(lay.block(hbm[name], me ^ d, rows), send, recv, base + d - 1, (x, y, c)) for d in (1, 2, 3)]
        return starts, waits
    return _Job(3, plan)


def _gather_near(lay, name):
    def plan(hbm, pos, send, recv, base):
        x, y, c, me = pos
        rows = lay.part_rows(c)
        mine = lay.block(hbm[name], me, rows)
        starts = [_remote(mine, mine, send, recv, base + d - 1, _core_of_chip(me ^ d, c)) for d in (1, 2)]
        waits = [_arrival(lay.block(hbm[name], me ^ d, rows), send, recv, base + d - 1, (x, y, c)) for d in (1, 2)]
        return starts, waits
    return _Job(2, plan)


def _gather_far(lay, name):
    def plan(hbm, pos, send, recv, base):
        x, y, c, me = pos
        starts, waits = [], []
        for q, d in ((0, 1), (1, 2)):
            got = lay.block(hbm[name], me ^ (3 - d), lay.part_rows(c, q, 2))
            starts.append(_remote(got, got, send, recv, base + q, _core_of_chip(me ^ d, c)))
            waits.append(_arrival(lay.block(hbm[name], me ^ 3, lay.part_rows(c, q, 2)), send, recv, base + q, (x, y, c)))
        return starts, waits
    return _Job(2, plan)


def _gather_d2d(lay, name, q=0, nq=1, chips=(1, 2, 3)):
    def plan(hbm, pos, send, recv, base):
        x, y, c, me = pos
        starts, waits = [], []
        for k, d in enumerate(chips):
            got = lay.block(hbm[name], me ^ d, lay.part_rows(c, q, nq))
            starts.append(_remote(got, got, send, recv, base + k, (x, y, 1 - c)))
            waits.append(_arrival(lay.block(hbm[name], me ^ d, lay.part_rows(1 - c, q, nq)), send, recv, base + k,
                                  (x, y, c)))
        return starts, waits
    return _Job(len(chips), plan)


def _rs_pair(lay, grad, theirs):
    def plan(hbm, pos, send, recv, base):
        x, y, c, _ = pos
        out = _remote(lay.all_chips(hbm[grad], lay.half_rows(1 - c)), hbm[theirs], send, recv, base, (x, y, 1 - c))
        return [out], [_arrival(hbm[theirs], send, recv, base, (x, y, c))]
    return _Job(1, plan)


def _rs_chips(lay, sums, slots):
    def plan(hbm, pos, send, recv, base):
        x, y, c, me = pos
        starts = [_remote(lay.block(hbm[sums], me ^ d), hbm[slots].at[me], send, recv, base + d - 1,
                          _core_of_chip(me ^ d, c)) for d in (1, 2, 3)]
        waits = [_arrival(hbm[slots].at[me ^ d], send, recv, base + d - 1, (x, y, c)) for d in (1, 2, 3)]
        return starts, waits
    return _Job(3, plan)


def _rs_share(lay, shard):
    def plan(hbm, pos, send, recv, base):
        x, y, c, _ = pos
        mine = hbm[shard].at[lay.half_rows(c), :]
        other = hbm[shard].at[lay.half_rows(1 - c), :]
        return [_remote(mine, mine, send, recv, base, (x, y, 1 - c))], [_arrival(other, send, recv, base, (x, y, c))]
    return _Job(1, plan)


def _slots_shape(lay):
    return jax.ShapeDtypeStruct((N_CHIPS, lay.rows // 2, lay.cols), BF16)


def _theirs_shape(lay, dtype=BF16):
    return jax.ShapeDtypeStruct(lay.whole(lay.rows // 2), dtype)


def _pair_sum(grad, theirs, lay, pos, name):
    half = lay.rows // 2
    add = lambda a, b: (a.astype(F32) + b.astype(F32),)
    if lay.stacked:
        tr = _row_tile(half, lay.cols * 6)
        nt = half // tr
        flat = lambda a: a.reshape(-1, lay.cols)
        mine = lambda t, pos: ((t // nt) * (2 * nt) + pos[1] * nt + t % nt, 0)
        grid, blk = (N_CHIPS * nt,), (tr, lay.cols)
        grad, theirs = flat(grad), flat(theirs)
    else:
        tr = _row_tile(half, N_CHIPS * lay.cols * 6)
        nt = half // tr
        mine = lambda t, pos: (pos[1] * nt + t, 0)
        grid, blk = (nt,), (tr, N_CHIPS * lay.cols)
    same = lambda t, pos: (t, 0)
    out = _tiled(add, name, grid, pos, [(grad, blk, mine), (theirs, blk, same)], [(theirs.shape, BF16, blk, same)])[0]
    return out.reshape(lay.whole(half))


def _chip_sum(sums, slots, lay, pos, name, after=None):
    half = lay.rows // 2
    tr = _row_tile(half, lay.cols * 12)
    nt = half // tr
    blk3 = (None, tr, lay.cols)
    if lay.stacked:
        own = (sums, blk3, lambda i, pos: (pos[0], i, 0))
    else:
        own = (sums, (tr, lay.cols), lambda i, pos: (i, pos[0]))
    others = [(slots, blk3, functools.partial(lambda d, i, pos: (pos[0] ^ d, i, 0), d)) for d in (1, 2, 3)]

    def add(a, b1, b2, b3, *_):
        return (((a.astype(F32) + b1.astype(F32)) + b2.astype(F32)) + b3.astype(F32),)

    if after is not None:
        others.append((after, (8, 128), lambda i, pos: (0, 0)))
    return _tiled(add, name, (nt,), pos, [own] + others,
                  [((lay.rows, lay.cols), F32, (tr, lay.cols), lambda i, pos: (pos[1] * nt + i, 0))])[0]


N_DEV = 8


def _to_all(src, slots):
    def plan(hbm, pos, send, recv, base):
        x, y, c, _ = pos
        idx = 4 * x + 2 * y + c
        starts = [_remote(hbm[src], hbm[slots].at[idx], send, recv, base + k - 1,
                          (x ^ (k >> 2), y ^ ((k >> 1) & 1), c ^ (k & 1))) for k in range(1, N_DEV)]
        waits = [_arrival(hbm[slots].at[idx ^ k], send, recv, base + k - 1, (x, y, c)) for k in range(1, N_DEV)]
        return starts, waits
    return _Job(N_DEV - 1, plan)


def _pack_rows(parts):
    padded = [jnp.pad(a, ((0, -a.shape[0] % 8), (0, 0))) for a in parts]
    starts = [sum(p.shape[0] for p in padded[:k]) for k in range(len(padded))]
    return jnp.concatenate(padded, axis=0), starts


def kernel(x, mix_norm, w_in, b_in, sinks, conv_w, w_attn_branch, w_conv_branch, w_out, ffn_norm, w_up, ffn_conv_w, w_down, final_norm, loss_target, m_mix_norm, m_w_in, m_b_in, m_sinks, m_conv_w, m_w_attn_branch, m_w_conv_branch, m_w_out, m_ffn_norm, m_w_up, m_ffn_conv_w, m_w_down, m_final_norm, v_mix_norm, v_w_in, v_b_in, v_sinks, v_conv_w, v_w_attn_branch, v_w_conv_branch, v_w_out, v_ffn_norm, v_w_up, v_ffn_conv_w, v_w_down, v_final_norm):
    me = 2 * lax.axis_index("x") + lax.axis_index("y")
    names = ("w_in", "w_br", "w_out", "w_up", "w_down")
    w_of = dict(w_in=w_in[0].T, w_out=w_out[0], w_up=w_up[0], w_down=w_down[0])
    m_of = dict(w_in=m_w_in[0].T, w_out=m_w_out[0], w_up=m_w_up[0], w_down=m_w_down[0])
    v_of = dict(w_in=v_w_in[0].T, w_out=v_w_out[0], w_up=v_w_up[0], w_down=v_w_down[0])
    ab = (w_attn_branch[0], m_w_attn_branch[0], v_w_attn_branch[0])
    cb = (w_conv_branch[0], m_w_conv_branch[0], v_w_conv_branch[0])

    pos = jnp.stack([me, lax.axis_index("c")]).astype(jnp.int32)

    lay = dict(zip(names, BIG))
    xs, target, sk = x[0], loss_target[0], sinks[0]
    s = xs.shape[0]
    tm, tm2, bk, bk2 = min(256, s), min(512, s), min(1024, s), min(2048, s)

    taps, (_, t0) = _pack_rows([conv_w[0], ffn_conv_w[0].reshape(3 * (FF2 // N_CHIPS // 128), 128)])
    placed = {"w_in": _place_cast(w_of["w_in"], lay["w_in"], pos, "cast_w_in")}
    fly_in, started = _start_exchange("gather_in_start", [_gather_near(lay["w_in"], "w_in")], {"w_in": placed["w_in"]})
    taps_flight, started = _start_exchange("taps_start", [_to_all("v", "slots")],
                                           {"v": taps + started[0:1], "slots": jnp.zeros((N_DEV, *taps.shape), F32)})
    placed["w_br"] = _place_cast_pair(ab[0], cb[0], lay["w_br"], pos, "cast_w_br", after=started)
    for n in names[2:]:
        placed[n] = _place_cast(w_of[n], lay[n], pos, "cast_" + n, after=started)
    trio = ("w_br", "w_out")
    fly_in, started = _relay_exchange(
        "gather_in_relay", fly_in, [_gather_far(lay["w_in"], "w_in"), _gather_d2d(lay["w_in"], "w_in", chips=(1, 2))],
        after=placed["w_down"])
    (fly_trio, fly_up, fly_down), started = _start_exchanges("gather_rest_start", [
        ([_gather_ici(lay[n], n) for n in ws], {**{n: placed[n] for n in ws}, **behind})
        for ws, behind in ((trio, {"behind": started}), (("w_up",), {}), (("w_down",), {}))])

    got = _finish_exchange("gather_in_wait", fly_in, after=started)
    w_in_full = _exchange("gather_in_d2d", [[_gather_d2d(lay["w_in"], "w_in", chips=(3,))]],
                          bufs=got)["w_in"].reshape(IN_W, D_MODEL)
    xn, qkv, c3, gates = _inproj_fwd(xs, mix_norm, w_in_full, b_in, tm2)
    got = _finish_exchange("gather_trio_wait", fly_trio, after=qkv)
    k2 = _Carry([_gather_d2d(lay[n], n) for n in trio], bufs={n: got[n] for n in trio})
    attn = _attn_fwd(qkv, sk, comm=k2)
    w_br = k2.out["w_br"]
    w_out_full = k2.out["w_out"].reshape(D_MODEL, D_MODEL)
    k3 = _Carry([_gather_d2d(lay["w_up"], "w_up")], bufs=_finish_exchange("gather_up_wait", fly_up, after=attn))
    taps = _finish_exchange("taps_wait", taps_flight, after=attn)
    taps = lax.dynamic_update_slice(taps["slots"], taps["v"][None], (2 * me + lax.axis_index("c"), 0, 0))
    conv_full = taps[0::2, 0:3].transpose(1, 0, 2).reshape(3, CONV_W)
    ffn_cw_full = taps[0::2, t0:t0 + 33].reshape(N_CHIPS, 3, FF2 // N_CHIPS).transpose(1, 0, 2).reshape(3, FF2)
    conv, a, cv, merged, h1, hn = _mix_fwd(xs, attn, c3, gates, conv_full, w_br, w_out_full, ffn_norm, tm2, comm=k3)
    w_up_full = k3.out["w_up"]
    w_down_full = _exchange("gather_down_d2d", [[_gather_d2d(lay["w_down"], "w_down")]],
                            bufs=_finish_exchange("gather_down_wait", fly_down, after=hn))["w_down"].reshape(D_FF, D_MODEL)
    u, up, act, dh2, loss_part, g_fn = _ffn_fwd_loss(hn, h1, w_up_full, ffn_cw_full, w_down_full,
                                                     final_norm[None, :], target, tm)

    grads, sums, slots = {}, {}, {}

    def pair(*ws):
        return _Carry([_rs_pair(lay[n], "g_" + n, "t_" + n) for n in ws], reads={"g_" + n: grads[n] for n in ws},
                      fresh={"t_" + n: _theirs_shape(lay[n], grads[n].dtype) for n in ws})

    def chips(*ws, also=None):
        k = _Carry([_rs_chips(lay[n], "s_" + n, "r_" + n) for n in ws], reads={"s_" + n: sums[n] for n in ws},
                   fresh={"r_" + n: _slots_shape(lay[n]) for n in ws})
        if also is not None:
            k = _Carry(k.jobs + also.jobs, {**k.reads, **also.reads}, None, {**k.fresh, **also.fresh})
        return k

    def pair_sums(k, *ws):
        for n in ws:
            sums[n] = _pair_sum(grads[n], k.out["t_" + n], lay[n], pos, "pair_sum_" + n)

    def take_slots(k, *ws):
        for n in ws:
            slots[n] = k.out["r_" + n]

    du, dh1, g_fcw, g_g2 = _ffn_bwd(dh2, u, up, h1, w_up_full, ffn_cw_full, w_down_full, ffn_norm, tm)
    grads["w_down"] = _wgrad(act, dh2, D_FF // 2, D_MODEL, bk2, "wgrad_down").reshape(lay["w_down"].whole())
    k4 = pair("w_down")
    grads["w_up"] = _wgrad(hn, du, D_MODEL, FF2 // 4, bk2, "wgrad_up", comm=k4)
    pair_sums(k4, "w_down")
    k5 = chips("w_down", also=pair("w_up"))
    dattn, dc3, dgt, g_cw, grads["w_br"], gw_out = _mix_bwd(
        dh1, gates, a, cv, c3, attn, conv, merged, conv_full, w_br, w_out_full, tm2, comm=k5)
    grads["w_out"] = gw_out.reshape(lay["w_out"].whole())
    take_slots(k5, "w_down")
    pair_sums(k5, "w_up")
    up_flight, started = _start_exchange("rs_chips_up_start", [_rs_chips(lay["w_up"], "s", "r")],
                                         {"s": sums["w_up"], "r": _slots_shape(lay["w_up"])})
    k6 = pair(*trio)
    k6.reads["after"] = started
    dq, dk_even, dk_odd, dv_even, dv_odd, g_sk = _attn_bwd(qkv, sk, attn, dattn, comm=k6)
    pair_sums(k6, *trio)
    trio_flight, started = _start_exchange(
        "rs_chips_trio_start", [_rs_chips(lay[n], "s_" + n, "r_" + n) for n in trio],
        {**{"s_" + n: sums[n] for n in trio}, **{"r_" + n: _slots_shape(lay[n]) for n in trio}})
    behind = mix_norm + jnp.tile(started[0:1], (1, D_MODEL // 128))
    grad_x, gw_in, g_b, g_g1 = _inproj_bwd(dq, (dk_even, dk_odd), (dv_even, dv_odd), dc3, dgt, w_in_full, xs, xn,
                                           dh1, behind)
    grads["w_in"] = gw_in.reshape(lay["w_in"].whole())

    in_flight, started = _start_exchange("rs_pair_in_start", [_rs_pair(lay["w_in"], "g", "t")],
                                         {"g": grads["w_in"], "t": _theirs_shape(lay["w_in"])})
    parts = [loss_part, g_g1, g_b, jnp.pad(g_sk[:, 0], (0, 120))[None, :], g_cw, g_g2, g_fcw, g_fn]
    packed, at = _pack_rows([p.reshape(-1, 128) for p in parts])
    small_flight, started = _start_exchange("small_start", [_to_all("v", "slots")],
                                            {"v": packed + started[0:1], "slots": jnp.zeros((N_DEV, *packed.shape), F32)})
    halves = {"w_down": _chip_sum(sums["w_down"], slots["w_down"], lay["w_down"], pos, "chip_sum_w_down", after=started)}
    landed = _finish_exchange("rs_chips_up_wait", up_flight, after=halves["w_down"])
    halves["w_up"] = _chip_sum(landed["s"], landed["r"], lay["w_up"], pos, "chip_sum_w_up")
    landed = _finish_exchange("rs_pair_in_wait", in_flight, after=halves["w_up"])
    sums["w_in"] = _pair_sum(landed["g"], landed["t"], lay["w_in"], pos, "pair_sum_w_in")
    (in_flight, down_flight, up_flight), started = _start_exchanges("rs_chips_in_start", [
        ([_rs_chips(lay["w_in"], "s", "r")], {"s": sums["w_in"], "r": _slots_shape(lay["w_in"])}),
        ([_rs_share(lay["w_down"], "w_down")], {"w_down": halves["w_down"]}),
        ([_rs_share(lay["w_up"], "w_up")], {"w_up": halves["w_up"]})])
    landed = _finish_exchange("rs_chips_trio_wait", trio_flight, after=started)
    for n in trio:
        halves[n] = _chip_sum(landed["s_" + n], landed["r_" + n], lay[n], pos, "chip_sum_" + n)
    shared = _exchange("share_halves", [[_rs_share(lay[n], n) for n in trio]], bufs={n: halves[n] for n in trio})
    shared["w_down"] = _finish_exchange("share_down_wait", down_flight, after=shared[trio[-1]])["w_down"]
    shared["w_up"] = _finish_exchange("share_up_wait", up_flight, after=shared["w_down"])["w_up"]

    def adam(n, g, after=None):
        return _rowwise(lambda w, g, m, v: (g, *_adamw(w, g, m, v)), [w_of[n], g, m_of[n], v_of[n]], [F32] * 4,
                        "adamw_" + n, after=after)

    new_of, last = {}, None
    for n in ("w_down", "w_up", "w_out"):
        new_of[n] = adam(n, shared[n], last)
        last = new_of[n][1]
    new_of["w_ab"], new_of["w_cb"] = _adamw_pair(ab, cb, shared["w_br"], after=last)
    last = new_of["w_cb"][1]

    arrived = _finish_exchange("small_wait", small_flight, after=last)
    flat = lambda k: lambda t, j, chip: at[k] + j
    mine = lambda k, per_tap: lambda t, j, chip: at[k] + per_tap * t + (per_tap // N_CHIPS) * chip + j
    rows = lambda a: a.reshape(a.shape[1], 1, a.shape[2])
    small_p = [
        (mix_norm, m_mix_norm, v_mix_norm, flat(1)), (b_in, m_b_in, v_b_in, flat(2)), (sinks, m_sinks, v_sinks, flat(3)),
        (rows(conv_w), rows(m_conv_w), rows(v_conv_w), mine(4, CONV_W // 128)),
        (ffn_norm, m_ffn_norm, v_ffn_norm, flat(5)),
        (rows(ffn_conv_w), rows(m_ffn_conv_w), rows(v_ffn_conv_w), mine(6, FF2 // 128)),
        (final_norm[None, :], m_final_norm[None, :], v_final_norm[None, :], flat(7))]
    small_new = _adamw_small(pos, arrived["v"], arrived["slots"], small_p)
    loss = small_new[0][0, 0]
    small_g = small_new[1::4]
    small_new = [small_new[4 * k + 2:4 * k + 5] for k in range(len(small_p))]

    landed = _finish_exchange("rs_chips_in_wait", in_flight, after=small_new[0][0])
    half_in = _chip_sum(landed["s"], landed["r"], lay["w_in"], pos, "chip_sum_w_in")
    shared["w_in"] = _exchange("share_in", [[_rs_share(lay["w_in"], "w_in")]], bufs={"w_in": half_in})["w_in"]
    new_of["w_in"] = [a.T for a in adam("w_in", shared["w_in"])]
    big = ("w_in", "w_ab", "w_cb", "w_out", "w_up", "w_down")
    big_g = [new_of[n][0] for n in big]
    big_new = [new_of[n][1:] for n in big]

    order = [("s", 0), ("b", 0), ("s", 1), ("s", 2), ("s", 3), ("b", 1), ("b", 2), ("b", 3), ("s", 4), ("b", 4),
             ("s", 5), ("b", 5), ("s", 6)]
    shapes = [mix_norm.shape, w_in.shape, b_in.shape, sinks.shape, conv_w.shape, w_attn_branch.shape,
              w_conv_branch.shape, w_out.shape, ffn_norm.shape, w_up.shape, ffn_conv_w.shape, w_down.shape,
              final_norm.shape]
    out_g = [(small_g[k] if kind == "s" else big_g[k]).reshape(shp) for (kind, k), shp in zip(order, shapes)]
    news = [[(small_new[k][j] if kind == "s" else big_new[k][j]).reshape(shp) for (kind, k), shp in zip(order, shapes)]
            for j in range(3)]
    return (loss, grad_x[None], *out_g, *news[0], *news[1], *news[2])
```

```python
import functools

import jax
import jax.numpy as jnp
from jax import lax
from jax.experimental import pallas as pl
from jax.experimental.pallas import tpu as pltpu

F32 = jnp.float32
BF16 = jnp.bfloat16

D_MODEL = 1024
HEAD_DIM = 64
N_HEADS = 8
N_KV_HEADS = 2
GROUP = N_HEADS // N_KV_HEADS
BLOCK = 128
ATTN_SCALE = HEAD_DIM ** -0.5
ATTN_W = N_HEADS * HEAD_DIM
KV_W = N_KV_HEADS * HEAD_DIM
CONV_W = 512
QKV_W = ATTN_W + 2 * KV_W
C3_W = 3 * CONV_W
GATES_W = 2 * D_MODEL
IN_W = QKV_W + C3_W + GATES_W
D_FF = 2816
FF2 = 2 * D_FF
NORM_EPS = 1e-5
N_CHIPS = 4
IN_SHARD = IN_W // N_CHIPS
NEG = -1e30

ADAM_LR = 0.001
ADAM_B1 = 0.9
ADAM_B2 = 0.999
ADAM_EPS = 1e-08
ADAM_WD = 0.01
ADAM_STEP = 10

VMEM_LIMIT = 56 * 1024 * 1024
MESH = pl.DeviceIdType.MESH

NT = (((1,), (1,)), ((), ()))
TN = (((0,), (0,)), ((), ()))


def _params(*sem):
    return pltpu.CompilerParams(dimension_semantics=sem, vmem_limit_bytes=VMEM_LIMIT)


def _resident(shape):
    return pl.BlockSpec(shape, lambda *_: (0,) * len(shape), pipeline_mode=pl.Buffered(1))


def _sigmoid(v):
    return 0.5 * jnp.tanh(0.5 * v) + 0.5


def _rstd(v):
    return lax.rsqrt(jnp.mean(v * v, axis=-1, keepdims=True) + NORM_EPS)


def _rms_bwd(dy, v, rstd, g):
    vhat = v * rstd
    t = dy * g
    return rstd * (t - vhat * jnp.mean(t * vhat, axis=-1, keepdims=True)), dy * vhat


def _taps(z, cw):
    return cw[2:3] * z + cw[1:2] * pltpu.roll(z, 1, 0) + cw[0:1] * pltpu.roll(z, 2, 0)


def _causal_conv(z, prev, cw):
    edge = _taps(jnp.concatenate([prev, z[0:8]], axis=0), cw)
    return jnp.concatenate([edge[8:16], _taps(z, cw)[8:]], axis=0)


def _rows_after(z, nxt):
    n = z.shape[0]
    edge = jnp.concatenate([z[n - 8:n], nxt], axis=0)
    return tuple(jnp.concatenate([pltpu.roll(z, n - k, 0)[:n - 8], pltpu.roll(edge, 16 - k, 0)[0:8]], axis=0)
                 for k in (1, 2))


def _inproj_fwd(x, g1, w_in, b_in, tm, comm=None):
    s = x.shape[0]

    def body(x_ref, g_ref, w_ref, b_ref, xn_ref, qkv_ref, c3_ref, gt_ref):
        xf = x_ref[...]
        xn = (xf * _rstd(xf) * g_ref[...]).astype(BF16)
        xn_ref[...] = xn

        proj = (lax.dot_general(xn, w_ref[...], NT, preferred_element_type=F32) + b_ref[...]).astype(BF16)
        qkv_ref[...] = proj[:, :QKV_W]
        c3_ref[...] = proj[:, QKV_W:QKV_W + C3_W]
        gt_ref[...] = proj[:, QKV_W + C3_W:]

    row = lambda w: pl.BlockSpec((tm, w), lambda i: (i, 0))
    return _call(
        comm, body, name="inproj_fwd", grid=(s // tm,),
        in_specs=[row(D_MODEL), _resident((1, D_MODEL)), _resident((IN_W, D_MODEL)), _resident((1, IN_W))],
        out_specs=[row(D_MODEL), row(QKV_W), row(C3_W), row(GATES_W)],
        out_shape=[jax.ShapeDtypeStruct((s, D_MODEL), BF16), jax.ShapeDtypeStruct((s, QKV_W), BF16),
                   jax.ShapeDtypeStruct((s, C3_W), BF16), jax.ShapeDtypeStruct((s, GATES_W), BF16)],
        compiler_params=_params("parallel"),
    )(x, g1, w_in, b_in)


def _attn_bias():
    kj = jnp.arange(2 * BLOCK)[:, None]
    qi = (jnp.arange(GROUP * BLOCK) % BLOCK)[None, :]
    band = (kj > qi) & (kj <= qi + BLOCK)
    return jnp.stack([jnp.where(band & (kj >= BLOCK), 0.0, NEG), jnp.where(band, 0.0, NEG)]).astype(F32)


def _attn_bias_specs():
    shape = (None, 2 * BLOCK, GROUP * BLOCK)
    return pl.BlockSpec(shape, lambda i: (jnp.minimum(i, 1), 0, 0)), pl.BlockSpec(shape, lambda i: (1, 0, 0))


def _sink_row(sk_ref, h):
    lane = lax.broadcasted_iota(jnp.int32, (1, GROUP * BLOCK), 1)
    row = jnp.full((1, GROUP * BLOCK), sk_ref[h * GROUP], F32)
    for g in range(1, GROUP):
        row = jnp.where(lane >= g * BLOCK, sk_ref[h * GROUP + g], row)
    return row


def _stack_heads(t, h):
    return jnp.concatenate(
        [t[:, (h * GROUP + g) * HEAD_DIM:(h * GROUP + g + 1) * HEAD_DIM] for g in range(GROUP)], axis=0)


def _unstack_heads(per_kv):
    return jnp.concatenate(
        [t[g * BLOCK:(g + 1) * BLOCK] for t in per_kv for g in range(GROUP)], axis=1)


def _block_specs(n, steps):
    cur = lambda i: jnp.minimum(i, steps - 1)
    prev = lambda i: jnp.maximum(n * jnp.minimum(i, steps - 1) - 1, 0)
    kv = ATTN_W // KV_W
    return (pl.BlockSpec((n * BLOCK, ATTN_W), lambda i: (cur(i), 0)),
            pl.BlockSpec((BLOCK, KV_W), lambda i: (prev(i), kv)), pl.BlockSpec((n * BLOCK, KV_W), lambda i: (cur(i), kv)),
            pl.BlockSpec((BLOCK, KV_W), lambda i: (prev(i), kv + 1)),
            pl.BlockSpec((n * BLOCK, KV_W), lambda i: (cur(i), kv + 1)))


def _attn_fwd(qkv, sinks, comm=None):
    s = qkv.shape[0]
    n = min(4, s // BLOCK)
    steps = s // (n * BLOCK)

    def body(sk_ref, bias0_ref, bias1_ref, q_ref, kp_ref, kc_ref, vp_ref, vc_ref, o_ref):
        kc, vc = kc_ref[...], vc_ref[...]
        for b in range(n):
            rows, before = slice(b * BLOCK, (b + 1) * BLOCK), slice((b - 1) * BLOCK, b * BLOCK)
            kp, vp = (kp_ref[...], vp_ref[...]) if b == 0 else (kc[before], vc[before])
            q, bias = q_ref[rows, :], (bias0_ref if b == 0 else bias1_ref)[...]
            outs = []
            for h in range(N_KV_HEADS):
                hs = slice(h * HEAD_DIM, (h + 1) * HEAD_DIM)
                k2 = jnp.concatenate([kp[:, hs], kc[rows, hs]], axis=0)
                v2 = jnp.concatenate([vp[:, hs], vc[rows, hs]], axis=0)
                sc = lax.dot_general(k2, _stack_heads(q, h), NT, preferred_element_type=F32) * ATTN_SCALE + bias
                sink = _sink_row(sk_ref, h)
                m = jnp.maximum(jnp.max(sc, axis=0, keepdims=True), sink)
                p = jnp.exp(sc - m)
                den = jnp.sum(p, axis=0, keepdims=True) + jnp.exp(sink - m)
                out = lax.dot_general(v2, p.astype(BF16), TN, preferred_element_type=F32) / den
                outs.append(out.T)
            o_ref[rows, :] = _unstack_heads(outs).astype(BF16)

    return _call(
        comm, body, name="attn_fwd", grid=(steps,),
        in_specs=[pl.BlockSpec(memory_space=pltpu.SMEM), *_attn_bias_specs(), *_block_specs(n, steps)],
        out_specs=pl.BlockSpec((n * BLOCK, ATTN_W), lambda i: (i, 0)),
        out_shape=jax.ShapeDtypeStruct((s, ATTN_W), BF16),
        compiler_params=_params("parallel"),
    )(sinks, _attn_bias(), _attn_bias(), qkv, qkv, qkv, qkv, qkv)


def _mix_fwd(x, attn, c3, gates, conv_w, w_br, w_out, g2, tm, comm=None):
    s = x.shape[0]

    def body(x_ref, at_ref, c3_ref, gt_ref, cw_ref, wbr_ref, wo_ref, g_ref,
             conv_ref, a_ref, cv_ref, mg_ref, h1_ref, hn_ref, carry_ref):
        @pl.when(pl.program_id(0) == 0)
        def _():
            carry_ref[...] = jnp.zeros_like(carry_ref)

        c3v = c3_ref[...].astype(F32)
        cb, cc, cx = c3v[:, :CONV_W], c3v[:, CONV_W:2 * CONV_W], c3v[:, 2 * CONV_W:]
        z = cc * cx
        cz = _causal_conv(z, carry_ref[...], cw_ref[...])
        carry_ref[...] = z[tm - 8:tm]
        conv = (cb * cz).astype(BF16)
        conv_ref[...] = conv
        a = jnp.dot(at_ref[...], wbr_ref[:ATTN_W, :], preferred_element_type=F32)
        cv = jnp.dot(conv, wbr_ref[ATTN_W:, :], preferred_element_type=F32)
        a_ref[...] = a.astype(BF16)
        cv_ref[...] = cv.astype(BF16)
        gt = gt_ref[...].astype(F32)
        merged = (_sigmoid(gt[:, :D_MODEL]) * a + _sigmoid(gt[:, D_MODEL:]) * cv).astype(BF16)
        mg_ref[...] = merged
        h1 = x_ref[...] + jnp.dot(merged, wo_ref[...], preferred_element_type=F32)
        h1_ref[...] = h1
        hn_ref[...] = (h1 * _rstd(h1) * g_ref[...]).astype(BF16)

    row = lambda w: pl.BlockSpec((tm, w), lambda i: (i, 0))
    return _call(
        comm, body, name="mix_fwd", grid=(s // tm,),
        in_specs=[row(D_MODEL), row(ATTN_W), row(C3_W), row(GATES_W), _resident((3, CONV_W)),
                  _resident((ATTN_W + CONV_W, D_MODEL)), _resident((D_MODEL, D_MODEL)), _resident((1, D_MODEL))],
        out_specs=[row(CONV_W), row(D_MODEL), row(D_MODEL), row(D_MODEL), row(D_MODEL), row(D_MODEL)],
        out_shape=[jax.ShapeDtypeStruct((s, CONV_W), BF16), jax.ShapeDtypeStruct((s, D_MODEL), BF16),
                   jax.ShapeDtypeStruct((s, D_MODEL), BF16), jax.ShapeDtypeStruct((s, D_MODEL), BF16),
                   jax.ShapeDtypeStruct((s, D_MODEL), F32), jax.ShapeDtypeStruct((s, D_MODEL), BF16)],
        scratch_shapes=[pltpu.VMEM((8, CONV_W), F32)],
        compiler_params=_params("arbitrary"),
    )(x, attn, c3, gates, conv_w, w_br, w_out, g2)


def _ffn_fwd_loss(hn, h1, w_up, ffn_cw, w_down, g3, target, tm):
    s = hn.shape[0]

    def body(hn_ref, h1_ref, wu_ref, cw_ref, wd_ref, g_ref, t_ref,
             u_ref, up_ref, act_ref, dh2_ref, loss_ref, gfn_ref, carry_ref):
        @pl.when(pl.program_id(0) == 0)
        def _():
            carry_ref[...] = jnp.zeros_like(carry_ref)
            loss_ref[...] = jnp.zeros_like(loss_ref)
            gfn_ref[...] = jnp.zeros_like(gfn_ref)

        u = jnp.dot(hn_ref[...], wu_ref[...], preferred_element_type=F32)
        u_ref[...] = u.astype(BF16)
        up = _causal_conv(u, carry_ref[...], cw_ref[...])
        up_ref[...] = up
        carry_ref[...] = u[tm - 8:tm]
        gate, val = up[:, :D_FF], up[:, D_FF:]
        act = (gate * _sigmoid(gate) * val).astype(BF16)
        act_ref[...] = act
        h2 = h1_ref[...] + jnp.dot(act, wd_ref[...], preferred_element_type=F32)
        rstd = _rstd(h2)
        g = g_ref[...]
        err = h2 * rstd * g - t_ref[...]
        loss_ref[...] += jnp.sum(err * err) * (0.5 / D_MODEL)
        dh2, dg = _rms_bwd(err * (1.0 / D_MODEL), h2, rstd, g)
        dh2_ref[...] = dh2
        gfn_ref[...] += jnp.sum(dg, axis=0, keepdims=True)

    row = lambda w: pl.BlockSpec((tm, w), lambda i: (i, 0))
    acc = lambda w: pl.BlockSpec((1, w), lambda i: (0, 0))
    return pl.pallas_call(
        body, name="ffn_fwd_loss", grid=(s // tm,),
        in_specs=[row(D_MODEL), row(D_MODEL), _resident((D_MODEL, FF2)), _resident((3, FF2)),
                  _resident((D_FF, D_MODEL)), _resident((1, D_MODEL)), row(D_MODEL)],
        out_specs=[row(FF2), row(FF2), row(D_FF), row(D_MODEL), acc(128), acc(D_MODEL)],
        out_shape=[jax.ShapeDtypeStruct((s, FF2), BF16), jax.ShapeDtypeStruct((s, FF2), F32),
                   jax.ShapeDtypeStruct((s, D_FF), BF16),
                   jax.ShapeDtypeStruct((s, D_MODEL), F32), jax.ShapeDtypeStruct((1, 128), F32),
                   jax.ShapeDtypeStruct((1, D_MODEL), F32)],
        scratch_shapes=[pltpu.VMEM((8, FF2), F32)],
        compiler_params=_params("arbitrary"),
    )(hn, h1, w_up, ffn_cw, w_down, g3, target)


def _ffn_bwd(dh2, u, up, h1, w_up, ffn_cw, w_down, g2, tm):
    s = dh2.shape[0]
    nt = s // tm

    def body(dh2_ref, u_ref, up_ref, h1_ref, wu_ref, cw_ref, wd_ref, g_ref,
             du_ref, dh1_ref, gcw_ref, gg_ref, carry_ref):
        @pl.when(pl.program_id(0) == 0)
        def _():
            for ref in (carry_ref, gcw_ref, gg_ref):
                ref[...] = jnp.zeros_like(ref)

        dh2v = dh2_ref[...]
        dact = lax.dot_general(dh2v.astype(BF16), wd_ref[...], NT, preferred_element_type=F32)
        upv = up_ref[...]
        gate, val = upv[:, :D_FF], upv[:, D_FF:]
        sg = _sigmoid(gate)
        dval = dact * (gate * sg)
        dgate = dact * val * (sg * (1.0 + gate * (1.0 - sg)))
        dup = jnp.concatenate([dgate, dval], axis=1)
        dup1, dup2 = _rows_after(dup, carry_ref[...])
        carry_ref[...] = dup[0:8]
        u = u_ref[...].astype(F32)
        gcw_ref[2:3, :] += jnp.sum(dup * u, axis=0, keepdims=True)
        gcw_ref[1:2, :] += jnp.sum(dup1 * u, axis=0, keepdims=True)
        gcw_ref[0:1, :] += jnp.sum(dup2 * u, axis=0, keepdims=True)
        cw = cw_ref[...]
        du = (cw[2:3] * dup + cw[1:2] * dup1 + cw[0:1] * dup2).astype(BF16)
        du_ref[...] = du
        dhn = lax.dot_general(du, wu_ref[...], NT, preferred_element_type=F32)
        h1v = h1_ref[...]
        dh1, dg = _rms_bwd(dhn, h1v, _rstd(h1v), g_ref[...])
        dh1_ref[...] = dh2v + dh1
        gg_ref[...] += jnp.sum(dg, axis=0, keepdims=True)

    row = lambda w: pl.BlockSpec((tm, w), lambda i: (nt - 1 - i, 0))
    return pl.pallas_call(
        body, name="ffn_bwd", grid=(nt,),
        in_specs=[row(D_MODEL), row(FF2), row(FF2),
                  row(D_MODEL), _resident((D_MODEL, FF2)), _resident((3, FF2)), _resident((D_FF, D_MODEL)),
                  _resident((1, D_MODEL))],
        out_specs=[row(FF2), row(D_MODEL), pl.BlockSpec((3, FF2), lambda i: (0, 0)),
                   pl.BlockSpec((1, D_MODEL), lambda i: (0, 0))],
        out_shape=[jax.ShapeDtypeStruct((s, FF2), BF16), jax.ShapeDtypeStruct((s, D_MODEL), F32),
                   jax.ShapeDtypeStruct((3, FF2), F32), jax.ShapeDtypeStruct((1, D_MODEL), F32)],
        scratch_shapes=[pltpu.VMEM((8, FF2), F32)],
        compiler_params=_params("arbitrary"),
    )(dh2, u, up, h1, w_up, ffn_cw, w_down, g2)


def _mix_bwd(dh1, gates, a, cv, c3, attn, conv, merged, conv_w, w_br, w_out, tm, comm=None):
    s = dh1.shape[0]
    nt = s // tm
    halo = 16

    def body(dh1_ref, gt_ref, a_ref, cv_ref, c3_ref, ch_ref, at_ref, cn_ref, mg_ref, cw_ref, wbr_ref,
             wo_ref, dat_ref, dc3_ref, dgt_ref, gcw_ref, gbr_ref, gout_ref, carry_ref, br_acc, out_acc):
        i = pl.program_id(0)

        @pl.when(i == 0)
        def _():
            for ref in (carry_ref, gcw_ref, br_acc, out_acc):
                ref[...] = jnp.zeros_like(ref)

        dh1v = dh1_ref[...].astype(BF16)
        out_acc[...] += lax.dot_general(mg_ref[...], dh1v, TN, preferred_element_type=F32)
        dm = lax.dot_general(dh1v, wo_ref[...], NT, preferred_element_type=F32)
        gt = gt_ref[...].astype(F32)
        sa, sc = _sigmoid(gt[:, :D_MODEL]), _sigmoid(gt[:, D_MODEL:])
        da = (dm * sa).astype(BF16)
        dcv = (dm * sc).astype(BF16)
        br_acc[:ATTN_W, :] += lax.dot_general(at_ref[...], da, TN, preferred_element_type=F32)
        br_acc[ATTN_W:, :] += lax.dot_general(cn_ref[...], dcv, TN, preferred_element_type=F32)
        dgt_ref[...] = jnp.concatenate(
            [dm * a_ref[...].astype(F32) * (sa * (1.0 - sa)), dm * cv_ref[...].astype(F32) * (sc * (1.0 - sc))],
            axis=1).astype(BF16)
        dat_ref[...] = lax.dot_general(da, wbr_ref[:ATTN_W, :], NT, preferred_element_type=F32).astype(BF16)
        dconv = lax.dot_general(dcv, wbr_ref[ATTN_W:, :], NT, preferred_element_type=F32)
        c3v = c3_ref[...].astype(F32)
        cb, cc, cx = c3v[:, :CONV_W], c3v[:, CONV_W:2 * CONV_W], c3v[:, 2 * CONV_W:]
        z = cc * cx
        chv = ch_ref[...].astype(F32)[halo - 8:halo] * (i < nt - 1).astype(F32)
        zh = chv[:, CONV_W:2 * CONV_W] * chv[:, 2 * CONV_W:]
        cw = cw_ref[...]
        cz = _causal_conv(z, zh, cw)
        dcz = dconv * cb
        dcz1, dcz2 = _rows_after(dcz, carry_ref[...])
        carry_ref[...] = dcz[0:8]
        gcw_ref[2:3, :] += jnp.sum(dcz * z, axis=0, keepdims=True)
        gcw_ref[1:2, :] += jnp.sum(dcz1 * z, axis=0, keepdims=True)
        gcw_ref[0:1, :] += jnp.sum(dcz2 * z, axis=0, keepdims=True)
        dz = cw[2:3] * dcz + cw[1:2] * dcz1 + cw[0:1] * dcz2
        dc3_ref[...] = jnp.concatenate([dconv * cz, dz * cx, dz * cc], axis=1).astype(BF16)

        @pl.when(i == nt - 1)
        def _():
            gbr_ref[...] = br_acc[...].astype(BF16)
            gout_ref[...] = out_acc[...].astype(BF16)

    row = lambda w: pl.BlockSpec((tm, w), lambda i: (nt - 1 - i, 0))
    return _call(
        comm, body, name="mix_bwd", grid=(nt,),
        in_specs=[row(D_MODEL), row(GATES_W), row(D_MODEL), row(D_MODEL), row(C3_W),
                  pl.BlockSpec((halo, C3_W), lambda i: (jnp.maximum((nt - 1 - i) * (tm // halo) - 1, 0), 0)),
                  row(ATTN_W), row(CONV_W), row(D_MODEL), _resident((3, CONV_W)),
                  _resident((ATTN_W + CONV_W, D_MODEL)), _resident((D_MODEL, D_MODEL))],
        out_specs=[row(ATTN_W), row(C3_W), row(GATES_W), pl.BlockSpec((3, CONV_W), lambda i: (0, 0)),
                   _resident((ATTN_W + CONV_W, D_MODEL)), _resident((D_MODEL, D_MODEL))],
        out_shape=[jax.ShapeDtypeStruct((s, ATTN_W), BF16), jax.ShapeDtypeStruct((s, C3_W), BF16),
                   jax.ShapeDtypeStruct((s, GATES_W), BF16), jax.ShapeDtypeStruct((3, CONV_W), F32),
                   jax.ShapeDtypeStruct((ATTN_W + CONV_W, D_MODEL), BF16),
                   jax.ShapeDtypeStruct((D_MODEL, D_MODEL), BF16)],
        scratch_shapes=[pltpu.VMEM((8, CONV_W), F32), pltpu.VMEM((ATTN_W + CONV_W, D_MODEL), F32),
                        pltpu.VMEM((D_MODEL, D_MODEL), F32)],
        compiler_params=_params("arbitrary"),
    )(dh1, gates, a, cv, c3, c3, attn, conv, merged, conv_w, w_br, w_out)


def _attn_bwd(qkv, sinks, o, do, comm=None):
    s = qkv.shape[0]
    npair = s // (2 * BLOCK)

    def one_block(sk_ref, bias, q, kp, kc, vp, vc, ov, dov, dsk_ref):
        dqs, dks, dvs = [], [], []
        for h in range(N_KV_HEADS):
            hs = slice(h * HEAD_DIM, (h + 1) * HEAD_DIM)
            k2 = jnp.concatenate([kp[:, hs], kc[:, hs]], axis=0)
            v2 = jnp.concatenate([vp[:, hs], vc[:, hs]], axis=0)
            qg, og, dog = _stack_heads(q, h), _stack_heads(ov, h), _stack_heads(dov, h)
            sc = lax.dot_general(k2, qg, NT, preferred_element_type=F32) * ATTN_SCALE + bias
            sink = _sink_row(sk_ref, h)
            m = jnp.maximum(jnp.max(sc, axis=0, keepdims=True), sink)
            p = jnp.exp(sc - m)
            psink = jnp.exp(sink - m)
            inv = 1.0 / (jnp.sum(p, axis=0, keepdims=True) + psink)
            p = p * inv
            delta = jnp.sum(dog.astype(F32) * og.astype(F32), axis=1, keepdims=True).T
            dp = lax.dot_general(v2, dog, NT, preferred_element_type=F32)
            ds = (p * (dp - delta)).astype(BF16)
            dqs.append((lax.dot_general(k2, ds, TN, preferred_element_type=F32) * ATTN_SCALE).T)
            dks.append(jnp.dot(ds, qg, preferred_element_type=F32) * ATTN_SCALE)
            dvs.append(jnp.dot(p.astype(BF16), dog, preferred_element_type=F32))
            dsink = -(psink * inv * delta)
            for g in range(GROUP):
                r = h * GROUP + g
                dsk_ref[r:r + 1, :] += jnp.sum(dsink[:, g * BLOCK:(g + 1) * BLOCK])
        return _unstack_heads(dqs), jnp.concatenate(dks, axis=1), jnp.concatenate(dvs, axis=1)

    def body(sk_ref, bias0_ref, bias1_ref, q_ref, kp_ref, kc_ref, vp_ref, vc_ref, o_ref, do_ref,
             dq_ref, dke_ref, dko_ref, dve_ref, dvo_ref, dsk_ref, ck_ref, cvv_ref):
        i = pl.program_id(0)

        @pl.when(i == 0)
        def _():
            for ref in (ck_ref, cvv_ref, dsk_ref):
                ref[...] = jnp.zeros_like(ref)

        @pl.when(i < npair)
        def _():
            kc, vc = kc_ref[...], vc_ref[...]
            first, second = slice(0, BLOCK), slice(BLOCK, 2 * BLOCK)
            dq0, dk0, dv0 = one_block(sk_ref, bias0_ref[...], q_ref[first, :], kp_ref[...], kc[first], vp_ref[...],
                                      vc[first], o_ref[first, :], do_ref[first, :], dsk_ref)
            dq1, dk1, dv1 = one_block(sk_ref, bias1_ref[...], q_ref[second, :], kc[first], kc[second], vc[first],
                                      vc[second], o_ref[second, :], do_ref[second, :], dsk_ref)
            dq_ref[first, :] = dq0.astype(BF16)
            dq_ref[second, :] = dq1.astype(BF16)
            dko_ref[...] = (ck_ref[...] + dk0[:BLOCK]).astype(BF16)
            dvo_ref[...] = (cvv_ref[...] + dv0[:BLOCK]).astype(BF16)
            dke_ref[...] = (dk0[BLOCK:] + dk1[:BLOCK]).astype(BF16)
            dve_ref[...] = (dv0[BLOCK:] + dv1[:BLOCK]).astype(BF16)
            ck_ref[...] = dk1[BLOCK:]
            cvv_ref[...] = dv1[BLOCK:]

        @pl.when(i == npair)
        def _():
            dko_ref[...] = ck_ref[...].astype(BF16)
            dvo_ref[...] = cvv_ref[...].astype(BF16)

    cur = lambda i: jnp.minimum(i, npair - 1)
    done = lambda i: jnp.maximum(i - 1, 0)
    rows = pl.BlockSpec((2 * BLOCK, ATTN_W), lambda i: (cur(i), 0))
    even = pl.BlockSpec((BLOCK, KV_W), lambda i: (cur(i), 0))
    odd = pl.BlockSpec((BLOCK, KV_W), lambda i: (done(i), 0))
    half = jax.ShapeDtypeStruct((s // 2, KV_W), BF16)
    return _call(
        comm, body, name="attn_bwd", grid=(npair + 1,),
        in_specs=[pl.BlockSpec(memory_space=pltpu.SMEM), *_attn_bias_specs(), *_block_specs(2, npair), rows, rows],
        out_specs=[rows, even, odd, even, odd, pl.BlockSpec((N_HEADS, 128), lambda i: (0, 0))],
        out_shape=[jax.ShapeDtypeStruct((s, ATTN_W), BF16), half, half, half, half,
                   jax.ShapeDtypeStruct((N_HEADS, 128), F32)],
        scratch_shapes=[pltpu.VMEM((BLOCK, KV_W), F32), pltpu.VMEM((BLOCK, KV_W), F32)],
        compiler_params=_params("arbitrary"),
    )(sinks, _attn_bias(), _attn_bias(), qkv, qkv, qkv, qkv, qkv, o, do)


def _inproj_bwd(dq, dk, dv, dc3, dgt, w_in, x, xn, dh1, g1):
    s = x.shape[0]
    tm = min(2 * BLOCK, s)
    nt = s // tm

    def body(dq_ref, dke_ref, dko_ref, dve_ref, dvo_ref, dc3_ref, dgt_ref, w_ref, x_ref, xn_ref, dh1_ref, g_ref,
             dx_ref, gw_ref, gb_ref, gg_ref, acc_ref):
        i = pl.program_id(0)

        @pl.when(i == 0)
        def _():
            for ref in (gb_ref, gg_ref, acc_ref):
                ref[...] = jnp.zeros_like(ref)

        dk = jnp.concatenate([dke_ref[...], dko_ref[...]], axis=0)
        dv = jnp.concatenate([dve_ref[...], dvo_ref[...]], axis=0)
        dp = jnp.concatenate([dq_ref[...], dk, dv, dc3_ref[...], dgt_ref[...]], axis=1)
        acc_ref[...] += lax.dot_general(dp, xn_ref[...], TN, preferred_element_type=F32)
        gb_ref[...] += jnp.sum(dp.astype(F32), axis=0, keepdims=True)
        dxn = jnp.dot(dp, w_ref[...], preferred_element_type=F32)
        xf = x_ref[...]
        dx, dg = _rms_bwd(dxn, xf, _rstd(xf), g_ref[...])
        dx_ref[...] = dh1_ref[...] + dx
        gg_ref[...] += jnp.sum(dg, axis=0, keepdims=True)

        @pl.when(i == nt - 1)
        def _():
            gw_ref[...] = acc_ref[...].astype(BF16)

    row = lambda w: pl.BlockSpec((tm, w), lambda i: (i, 0))
    acc = lambda w: pl.BlockSpec((1, w), lambda i: (0, 0))
    block = pl.BlockSpec((tm // 2, KV_W), lambda i: (i, 0))
    return pl.pallas_call(
        body, name="inproj_bwd", grid=(nt,),
        in_specs=[row(ATTN_W), block, block, block, block, row(C3_W), row(GATES_W), _resident((IN_W, D_MODEL)),
                  row(D_MODEL), row(D_MODEL), row(D_MODEL), _resident((1, D_MODEL))],
        out_specs=[row(D_MODEL), _resident((IN_W, D_MODEL)), acc(IN_W), acc(D_MODEL)],
        out_shape=[jax.ShapeDtypeStruct((s, D_MODEL), F32), jax.ShapeDtypeStruct((IN_W, D_MODEL), BF16),
                   jax.ShapeDtypeStruct((1, IN_W), F32), jax.ShapeDtypeStruct((1, D_MODEL), F32)],
        scratch_shapes=[pltpu.VMEM((IN_W, D_MODEL), F32)],
        compiler_params=_params("arbitrary"),
    )(dq, *dk, *dv, dc3, dgt, w_in, x, xn, dh1, g1)


def _wgrad(a, b, bm, bn, bk, name, comm=None):
    s, m = a.shape
    n = b.shape[1]
    nk = s // bk

    def body(a_ref, b_ref, o_ref, acc_ref):
        k = pl.program_id(2)

        @pl.when(k == 0)
        def _():
            acc_ref[...] = jnp.zeros_like(acc_ref)

        acc_ref[...] += lax.dot_general(a_ref[...].astype(BF16), b_ref[...].astype(BF16), TN,
                                        preferred_element_type=F32)

        @pl.when(k == nk - 1)
        def _():
            o_ref[...] = acc_ref[...].astype(BF16)

    return _call(
        comm, body, name=name, grid=(m // bm, n // bn, nk),
        in_specs=[pl.BlockSpec((bk, bm), lambda i, j, k: (k, i)), pl.BlockSpec((bk, bn), lambda i, j, k: (k, j))],
        out_specs=pl.BlockSpec((bm, bn), lambda i, j, k: (i, j)),
        out_shape=jax.ShapeDtypeStruct((m, n), BF16),
        scratch_shapes=[pltpu.VMEM((bm, bn), F32)],
        compiler_params=_params("parallel", "parallel", "arbitrary"),
    )(a, b)


class _Carry:
    def __init__(self, jobs, reads=None, bufs=None, fresh=None):
        self.jobs, self.reads, self.bufs, self.fresh = jobs, reads or {}, bufs or {}, fresh or {}
        self.out = {}


class _Job:
    def __init__(self, n_sems, plan):
        self.n_sems, self.plan = n_sems, plan


def _plan_all(jobs, hbm, send, recv):
    pos = _position()
    starts, waits, base = [], [], 0
    for job in jobs:
        s, w = job.plan(hbm, pos, send, recv, base)
        starts, waits, base = starts + s, waits + w, base + job.n_sems
    return starts, waits


def _call(comm, body, **kw):
    if comm is None:
        return pl.pallas_call(body, **kw)
    grid = kw["grid"]
    single = not isinstance(kw["out_shape"], (list, tuple))
    out_shape = [kw["out_shape"]] if single else list(kw["out_shape"])
    out_specs = [kw["out_specs"]] if single else list(kw["out_specs"])
    in_specs = list(kw["in_specs"])
    scratch = list(kw.get("scratch_shapes", ()))
    r_names, b_names, f_names = list(comm.reads), list(comm.bufs), list(comm.fresh)
    n_args, n_out, n_scr = len(in_specs), len(out_shape), len(scratch)
    n_sems = sum(j.n_sems for j in comm.jobs)

    def wrapped(*refs):
        k = n_args
        hbm = dict(zip(r_names, refs[k:k + len(r_names)]))
        k += len(r_names) + len(b_names)
        outs = refs[k:k + n_out]
        k += n_out
        hbm.update(zip(b_names + f_names, refs[k:k + len(b_names) + len(f_names)]))
        k += len(b_names) + len(f_names)
        send, recv = refs[k + n_scr:]
        starts, waits = _plan_all(comm.jobs, hbm, send, recv)
        ids = [pl.program_id(a) for a in range(len(grid))]
        first = functools.reduce(jnp.logical_and, [i == 0 for i in ids])
        last = functools.reduce(jnp.logical_and, [i == g - 1 for i, g in zip(ids, grid)])

        @pl.when(first)
        def _():
            for cp in starts:
                cp.start()

        body(*refs[:n_args], *outs, *refs[k:k + n_scr])

        @pl.when(last)
        def _():
            for cp in waits:
                cp.wait_recv()
            for cp in starts:
                cp.wait_send()

    sems = pltpu.SemaphoreType.DMA((n_sems,))
    held = [jax.ShapeDtypeStruct(a.shape, a.dtype) for a in comm.bufs.values()] + list(comm.fresh.values())
    call = pl.pallas_call(
        wrapped, name=kw["name"], grid=grid,
        in_specs=in_specs + [_ANY] * (len(r_names) + len(b_names)),
        out_specs=out_specs + [_ANY] * len(held),
        out_shape=out_shape + held,
        input_output_aliases={n_args + len(r_names) + i: n_out + i for i in range(len(b_names))},
        scratch_shapes=scratch + [sems, sems],
        compiler_params=_params(*["arbitrary"] * len(grid)),
    )

    def run(*args):
        res = call(*args, *comm.reads.values(), *comm.bufs.values())
        comm.out = dict(zip(b_names + f_names, res[n_out:]))
        return res[0] if single else res[:n_out]

    return run


def _exchange(name, phases, reads=None, bufs=None, fresh=None):
    comm = _Carry([j for ph in phases for j in ph], reads, bufs, fresh)
    r_names, b_names, f_names = list(comm.reads), list(comm.bufs), list(comm.fresh)
    n_sems = sum(j.n_sems for j in comm.jobs)

    def body(*refs):
        hbm = dict(zip(r_names, refs[:len(r_names)]))
        k = len(r_names) + len(b_names)
        hbm.update(zip(b_names + f_names, refs[k:k + len(b_names) + len(f_names)]))
        send, recv = refs[-2:]
        pos = _position()
        started, base = [], 0
        for ph in phases:
            waits = []
            for job in ph:
                s, w = job.plan(hbm, pos, send, recv, base)
                base += job.n_sems
                for cp in s:
                    cp.start()
                started, waits = started + s, waits + w
            for cp in waits:
                cp.wait_recv()
        for cp in started:
            cp.wait_send()

    sems = pltpu.SemaphoreType.DMA((n_sems,))
    held = [jax.ShapeDtypeStruct(a.shape, a.dtype) for a in comm.bufs.values()] + list(comm.fresh.values())
    res = pl.pallas_call(
        body, name=name, in_specs=[_ANY] * (len(r_names) + len(b_names)), out_specs=[_ANY] * len(held),
        out_shape=held, input_output_aliases={len(r_names) + i: i for i in range(len(b_names))},
        scratch_shapes=[sems, sems],
    )(*comm.reads.values(), *comm.bufs.values())
    return dict(zip(b_names + f_names, res))


_HBM = pl.BlockSpec(memory_space=pltpu.HBM)
_SEM = pl.BlockSpec(memory_space=pltpu.SEMAPHORE)
_EFFECT = pltpu.SideEffectType.DATAFLOW_SIDE_EFFECTING


def _start_exchanges(name, groups):
    names = [list(arrays) for _, arrays in groups]
    first = [sum(len(ns) for ns in names[:g]) for g in range(len(groups))]
    n, ng = sum(len(ns) for ns in names), len(groups)

    def body(*refs):
        for g, (jobs, _) in enumerate(groups):
            hbm = dict(zip(names[g], refs[first[g]:first[g] + len(names[g])]))
            for cp in _plan_all(jobs, hbm, refs[n + 2 * g], refs[n + 2 * g + 1])[0]:
                cp.start()
        refs[-1][...] = jnp.zeros_like(refs[-1])

    given = [pltpu.with_memory_space_constraint(
        a if isinstance(a, jax.Array) else lax.empty(a.shape, a.dtype), pltpu.HBM)
        for _, arrays in groups for a in arrays.values()]
    sems = [pltpu.SemaphoreType.DMA((sum(j.n_sems for j in jobs),)) for jobs, _ in groups for _ in range(2)]
    res = pl.pallas_call(
        body, name=name,
        out_shape=(*sems, *[pltpu.HBM(a.shape, a.dtype) for a in given], jax.ShapeDtypeStruct((8, 128), F32)),
        in_specs=[_HBM] * n, out_specs=(*[_SEM] * (2 * ng), *[_HBM] * n, pl.BlockSpec(memory_space=pltpu.VMEM)),
        input_output_aliases={i: 2 * ng + i for i in range(n)},
        compiler_params=pltpu.CompilerParams(has_side_effects=_EFFECT),
    )(*given)
    held = res[2 * ng:2 * ng + n]
    states = [(names[g], groups[g][0], res[2 * g], res[2 * g + 1], held[first[g]:first[g] + len(names[g])])
              for g in range(ng)]
    return states, res[-1]


def _start_exchange(name, jobs, arrays):
    states, token = _start_exchanges(name, [(jobs, arrays)])
    return states[0], token


def _finish_exchange(name, state, after):
    names, jobs, send_sem, recv_sem, held = state
    n = len(names)

    def body(*refs):
        hbm = dict(zip(names, refs[:n]))
        send, recv = refs[n:n + 2]
        starts, waits = _plan_all(jobs, hbm, send, recv)
        for cp in waits:
            cp.wait_recv()
        for cp in starts:
            cp.wait_send()

    res = pl.pallas_call(
        body, name=name, out_shape=tuple(pltpu.HBM(a.shape, a.dtype) for a in held),
        in_specs=[_HBM] * n + [_SEM, _SEM, _ANY], out_specs=tuple([_HBM] * n),
        input_output_aliases={i: i for i in range(n)},
        compiler_params=pltpu.CompilerParams(has_side_effects=_EFFECT),
    )(*held, send_sem, recv_sem, after)
    return dict(zip(names, res))


def _relay_exchange(name, state, jobs, after):
    names, arrived, send_sem, recv_sem, held = state
    n = len(names)

    def body(*refs):
        hbm = dict(zip(names, refs[:n]))
        starts, waits = _plan_all(arrived, hbm, refs[n], refs[n + 1])
        for cp in waits:
            cp.wait_recv()
        for cp in starts:
            cp.wait_send()
        for cp in _plan_all(jobs, hbm, refs[n + 3], refs[n + 4])[0]:
            cp.start()
        refs[-1][...] = jnp.zeros_like(refs[-1])

    sems = pltpu.SemaphoreType.DMA((sum(j.n_sems for j in jobs),))
    res = pl.pallas_call(
        body, name=name,
        out_shape=(sems, sems, *[pltpu.HBM(a.shape, a.dtype) for a in held], jax.ShapeDtypeStruct((8, 128), F32)),
        in_specs=[_HBM] * n + [_SEM, _SEM, _ANY],
        out_specs=(_SEM, _SEM, *[_HBM] * n, pl.BlockSpec(memory_space=pltpu.VMEM)),
        input_output_aliases={i: 2 + i for i in range(n)},
        compiler_params=pltpu.CompilerParams(has_side_effects=_EFFECT),
    )(*held, send_sem, recv_sem, after)
    return (names, jobs, res[0], res[1], res[2:2 + n]), res[-1]


def _row_tile(rows, bytes_per_row):
    best = 16
    for t in range(16, rows + 1, 16):
        if rows % t == 0 and t * bytes_per_row <= 9 * 1024 * 1024:
            best = t
    return best


def _rowwise(fn, ins, out_dtypes, name, after=None):
    rows, cols = ins[0].shape
    per_row = sum(cols * a.dtype.itemsize for a in ins) + sum(cols * jnp.dtype(d).itemsize for d in out_dtypes)
    tr = _row_tile(rows, per_row)
    n_in = len(ins)

    def body(*refs):
        outs = fn(*[r[...] for r in refs[:n_in]])
        for o_ref, o in zip(refs[-len(out_dtypes):], outs):
            o_ref[...] = o.astype(o_ref.dtype)

    tile = pl.BlockSpec((tr, cols), lambda i: (i, 0))
    behind = [] if after is None else [after]
    return pl.pallas_call(
        body, name=name, grid=(rows // tr,),
        in_specs=[tile] * n_in + [pl.BlockSpec((8, 128), lambda i: (0, 0))] * len(behind),
        out_specs=[tile] * len(out_dtypes),
        out_shape=[jax.ShapeDtypeStruct((rows, cols), d) for d in out_dtypes],
        compiler_params=_params("parallel"),
    )(*ins, *behind)


def _tiled(fn, name, grid, pos, ins, outs):
    n_in = len(ins)

    def body(pos_ref, *refs):
        res = fn(*[r[...] for r in refs[:n_in]])
        for o_ref, o in zip(refs[n_in:], res):
            o_ref[...] = o.astype(o_ref.dtype)

    return pl.pallas_call(
        body, name=name,
        grid_spec=pltpu.PrefetchScalarGridSpec(
            num_scalar_prefetch=1, grid=grid,
            in_specs=[pl.BlockSpec(bs, im) for _, bs, im in ins],
            out_specs=[pl.BlockSpec(bs, im) for _, _, bs, im in outs]),
        out_shape=[jax.ShapeDtypeStruct(s, d) for s, d, _, _ in outs],
        compiler_params=_params("parallel"),
    )(pos, *[a for a, _, _ in ins])


def _adamw(w, g, m, v):
    m = ADAM_B1 * m + (1.0 - ADAM_B1) * g
    v = ADAM_B2 * v + (1.0 - ADAM_B2) * (g * g)
    m_hat = m / (1.0 - ADAM_B1 ** ADAM_STEP)
    v_hat = v / (1.0 - ADAM_B2 ** ADAM_STEP)
    return -ADAM_LR * (m_hat / (jnp.sqrt(v_hat) + ADAM_EPS) + ADAM_WD * w), m, v


def _adamw_small(pos, own, slots, params):
    n = len(params)

    def body(pos_ref, own_ref, slots_ref, *refs):
        ins, outs, total_ref = refs[:3 * n], refs[3 * n:-1], refs[-1]
        chip = pos_ref[0]
        idx = 2 * chip + pos_ref[1]
        term = lambda q: jnp.where(idx == q, own_ref[...], slots_ref[q])
        acc = term(0)
        for q in range(1, N_DEV):
            acc = acc + term(q)
        total_ref[...] = acc
        outs[0][...] = total_ref[0:1, :]
        for k, (w, _, _, row) in enumerate(params):
            width = min(w.shape[-1], 128)
            for t in range(w.shape[0]):
                for j in range(w.shape[-1] // width):
                    lanes = slice(j * width, (j + 1) * width)
                    at = (slice(t, t + 1), lanes) if w.ndim == 2 else (t, slice(None), lanes)
                    g = total_ref[pl.ds(row(t, j, chip), 1), :][:, :width]
                    new = _adamw(ins[3 * k][at], g, ins[3 * k + 1][at], ins[3 * k + 2][at])
                    for o_ref, o in zip(outs[1 + 4 * k:5 + 4 * k], (g, *new)):
                        o_ref[at] = o

    vmem = pl.BlockSpec(memory_space=pltpu.VMEM)
    return pl.pallas_call(
        body, name="adamw_small",
        in_specs=[pl.BlockSpec(memory_space=pltpu.SMEM)] + [vmem] * (2 + 3 * n),
        out_shape=[jax.ShapeDtypeStruct((1, 128), F32)]
        + [jax.ShapeDtypeStruct(p[0].shape, F32) for p in params for _ in range(4)],
        scratch_shapes=[pltpu.VMEM(own.shape, F32)],
    )(pos, own, slots, *[a for p in params for a in p[:3]])


class _Layout:
    def __init__(self, rows, cols, stacked):
        self.rows, self.cols, self.stacked = rows, cols, stacked

    def whole(self, rows=None):
        r = self.rows if rows is None else rows
        return (N_CHIPS, r, self.cols) if self.stacked else (r, N_CHIPS * self.cols)

    def part_rows(self, h, q=0, nq=1):
        n = self.rows // 2 // nq
        return pl.ds(pl.multiple_of(h * (self.rows // 2) + q * n, 16), n)

    def half_rows(self, h):
        return self.part_rows(h)

    def block(self, ref, p, rows=slice(None)):
        if self.stacked:
            return ref.at[p, rows, :]
        return ref.at[rows, pl.ds(pl.multiple_of(p * self.cols, 128), self.cols)]

    def all_chips(self, ref, rows):
        return ref.at[:, rows, :] if self.stacked else ref.at[rows, :]


BIG = (
    _Layout(IN_SHARD, D_MODEL, True),
    _Layout(ATTN_W + CONV_W, D_MODEL // N_CHIPS, False),
    _Layout(D_MODEL // N_CHIPS, D_MODEL, True),
    _Layout(D_MODEL, FF2 // N_CHIPS, False),
    _Layout(D_FF // N_CHIPS, D_MODEL, True),
)
N_BIG = len(BIG)
_ANY = pl.BlockSpec(memory_space=pl.ANY)


def _position():
    x, y, c = lax.axis_index("x"), lax.axis_index("y"), lax.axis_index("c")
    return x, y, c, 2 * x + y


def _core_of_chip(p, c):
    return (p >> 1, p & 1, c)


def _place_cast(shard, lay, pos, name, after=None):
    rows, cols = shard.shape
    tr = _row_tile(rows, cols * 6)
    if lay.stacked:
        out = (lay.whole(), BF16, (None, tr, cols), lambda i, pos: (pos[0], i, 0))
    else:
        out = (lay.whole(), BF16, (tr, cols), lambda i, pos: (i, pos[0]))
    ins = [(shard, (tr, cols), lambda i, pos: (i, 0))]
    if after is not None:
        ins.append((after, (8, 128), lambda i, pos: (0, 0)))
    return _tiled(lambda a, *_: (a,), name, (rows // tr,), pos, ins, [out])[0]


def _place_cast_pair(top, bottom, lay, pos, name, after=None):
    rows, cols = top.shape
    ins = [(top, (rows, cols), lambda i, pos: (0, 0)), (bottom, (rows, cols), lambda i, pos: (0, 0))]
    if after is not None:
        ins.append((after, (8, 128), lambda i, pos: (0, 0)))
    return _tiled(lambda a, b, *_: (jnp.concatenate([a, b], axis=0),), name, (1,), pos, ins,
                  [(lay.whole(), BF16, (2 * rows, cols), lambda i, pos: (0, pos[0]))])[0]


def _adamw_pair(top, bottom, g, after=None):
    rows = top[0].shape[0]

    def body(*refs):
        (wa, ma, va, wb, mb, vb, g_ref), outs = refs[:7], refs[-8:]
        for (w, m, v), gg, o in (((wa, ma, va), g_ref[:rows], outs[:4]), ((wb, mb, vb), g_ref[rows:], outs[4:])):
            for o_ref, val in zip(o, (gg, *_adamw(w[...], gg, m[...], v[...]))):
                o_ref[...] = val

    behind = [] if after is None else [after]
    res = pl.pallas_call(
        body, name="adamw_w_br", out_shape=[jax.ShapeDtypeStruct(top[0].shape, F32)] * 8,
        in_specs=[pl.BlockSpec(memory_space=pltpu.VMEM)] * 7 + [_ANY] * len(behind),
    )(*top, *bottom, g, *behind)
    return res[:4], res[4:]


def _remote(src, dst, send, recv, k, device):
    return pltpu.make_async_remote_copy(src_ref=src, dst_ref=dst, send_sem=send.at[k], recv_sem=recv.at[k],
                                        device_id=device, device_id_type=MESH)


def _arrival(dst, send, recv, k, me):
    return _remote(dst, dst, send, recv, k, me)


def _gather_ici(lay, name, q=0, nq=1):
    def plan(hbm, pos, send, recv, base):
        x, y, c, me = pos
        rows = lay.part_rows(c, q, nq)
        mine = lay.block(hbm[name], me, rows)
        starts = [_remote(mine, mine, send, recv, base + d - 1, _core_of_chip(me ^ d, c)) for d in (1, 2, 3)]
        waits = [_arrival(lay.block(hbm[name], me ^ d, rows), send, recv, base + d - 1, (x, y, c)) for d in (1, 2, 3)]
        return starts, waits
    return _Job(3, plan)


def _gather_near(lay, name):
    def plan(hbm, pos, send, recv, base):
        x, y, c, me = pos
        rows = lay.part_rows(c)
        mine = lay.block(hbm[name], me, rows)
        starts = [_remote(mine, mine, send, recv, base + d - 1, _core_of_chip(me ^ d, c)) for d in (1, 2)]
        waits = [_arrival(lay.block(hbm[name], me ^ d, rows), send, recv, base + d - 1, (x, y, c)) for d in (1, 2)]
        return starts, waits
    return _Job(2, plan)


def _gather_far(lay, name):
    def plan(hbm, pos, send, recv, base):
        x, y, c, me = pos
        starts, waits = [], []
        for q, d in ((0, 1), (1, 2)):
            got = lay.block(hbm[name], me ^ (3 - d), lay.part_rows(c, q, 2))
            starts.append(_remote(got, got, send, recv, base + q, _core_of_chip(me ^ d, c)))
            waits.append(_arrival(lay.block(hbm[name], me ^ 3, lay.part_rows(c, q, 2)), send, recv, base + q, (x, y, c)))
        return starts, waits
    return _Job(2, plan)


def _gather_d2d(lay, name, q=0, nq=1, chips=(1, 2, 3)):
    def plan(hbm, pos, send, recv, base):
        x, y, c, me = pos
        starts, waits = [], []
        for k, d in enumerate(chips):
            got = lay.block(hbm[name], me ^ d, lay.part_rows(c, q, nq))
            starts.append(_remote(got, got, send, recv, base + k, (x, y, 1 - c)))
            waits.append(_arrival(lay.block(hbm[name], me ^ d, lay.part_rows(1 - c, q, nq)), send, recv, base + k,
                                  (x, y, c)))
        return starts, waits
    return _Job(len(chips), plan)


def _rs_pair(lay, grad, theirs):
    def plan(hbm, pos, send, recv, base):
        x, y, c, _ = pos
        out = _remote(lay.all_chips(hbm[grad], lay.half_rows(1 - c)), hbm[theirs], send, recv, base, (x, y, 1 - c))
        return [out], [_arrival(hbm[theirs], send, recv, base, (x, y, c))]
    return _Job(1, plan)


def _rs_chips(lay, sums, slots):
    def plan(hbm, pos, send, recv, base):
        x, y, c, me = pos
        starts = [_remote(lay.block(hbm[sums], me ^ d), hbm[slots].at[me], send, recv, base + d - 1,
                          _core_of_chip(me ^ d, c)) for d in (1, 2, 3)]
        waits = [_arrival(hbm[slots].at[me ^ d], send, recv, base + d - 1, (x, y, c)) for d in (1, 2, 3)]
        return starts, waits
    return _Job(3, plan)


def _rs_share(lay, shard):
    def plan(hbm, pos, send, recv, base):
        x, y, c, _ = pos
        mine = hbm[shard].at[lay.half_rows(c), :]
        other = hbm[shard].at[lay.half_rows(1 - c), :]
        return [_remote(mine, mine, send, recv, base, (x, y, 1 - c))], [_arrival(other, send, recv, base, (x, y, c))]
    return _Job(1, plan)


def _slots_shape(lay):
    return jax.ShapeDtypeStruct((N_CHIPS, lay.rows // 2, lay.cols), BF16)


def _theirs_shape(lay, dtype=BF16):
    return jax.ShapeDtypeStruct(lay.whole(lay.rows // 2), dtype)


def _pair_sum(grad, theirs, lay, pos, name):
    half = lay.rows // 2
    add = lambda a, b: (a.astype(F32) + b.astype(F32),)
    if lay.stacked:
        tr = _row_tile(half, lay.cols * 6)
        nt = half // tr
        flat = lambda a: a.reshape(-1, lay.cols)
        mine = lambda t, pos: ((t // nt) * (2 * nt) + pos[1] * nt + t % nt, 0)
        grid, blk = (N_CHIPS * nt,), (tr, lay.cols)
        grad, theirs = flat(grad), flat(theirs)
    else:
        tr = _row_tile(half, N_CHIPS * lay.cols * 6)
        nt = half // tr
        mine = lambda t, pos: (pos[1] * nt + t, 0)
        grid, blk = (nt,), (tr, N_CHIPS * lay.cols)
    same = lambda t, pos: (t, 0)
    out = _tiled(add, name, grid, pos, [(grad, blk, mine), (theirs, blk, same)], [(theirs.shape, BF16, blk, same)])[0]
    return out.reshape(lay.whole(half))


def _chip_sums(items, pos, name, after=None):
    ins, outs = [], []
    for sums, slots, lay in items:
        half = lay.rows // 2
        blk3 = (None, half, lay.cols)
        if lay.stacked:
            own = (sums, blk3, lambda i, pos: (pos[0], 0, 0))
        else:
            own = (sums, (half, lay.cols), lambda i, pos: (0, pos[0]))
        ins += [own] + [(slots, blk3, functools.partial(lambda d, i, pos: (pos[0] ^ d, 0, 0), d)) for d in (1, 2, 3)]
        outs.append(((lay.rows, lay.cols), F32, (half, lay.cols), lambda i, pos: (pos[1], 0)))

    def add(*vals):
        v = [a.astype(F32) for a in vals[:4 * len(items)]]
        return tuple(((v[4 * k] + v[4 * k + 1]) + v[4 * k + 2]) + v[4 * k + 3] for k in range(len(items)))

    if after is not None:
        ins.append((after, (8, 128), lambda i, pos: (0, 0)))
    return _tiled(add, name, (1,), pos, ins, outs)


N_DEV = 8


def _to_all(src, slots):
    def plan(hbm, pos, send, recv, base):
        x, y, c, _ = pos
        idx = 4 * x + 2 * y + c
        starts = [_remote(hbm[src], hbm[slots].at[idx], send, recv, base + k - 1,
                          (x ^ (k >> 2), y ^ ((k >> 1) & 1), c ^ (k & 1))) for k in range(1, N_DEV)]
        waits = [_arrival(hbm[slots].at[idx ^ k], send, recv, base + k - 1, (x, y, c)) for k in range(1, N_DEV)]
        return starts, waits
    return _Job(N_DEV - 1, plan)


def _pack_rows(parts):
    padded = [jnp.pad(a, ((0, -a.shape[0] % 8), (0, 0))) for a in parts]
    starts = [sum(p.shape[0] for p in padded[:k]) for k in range(len(padded))]
    return jnp.concatenate(padded, axis=0), starts


def kernel(x, mix_norm, w_in, b_in, sinks, conv_w, w_attn_branch, w_conv_branch, w_out, ffn_norm, w_up, ffn_conv_w, w_down, final_norm, loss_target, m_mix_norm, m_w_in, m_b_in, m_sinks, m_conv_w, m_w_attn_branch, m_w_conv_branch, m_w_out, m_ffn_norm, m_w_up, m_ffn_conv_w, m_w_down, m_final_norm, v_mix_norm, v_w_in, v_b_in, v_sinks, v_conv_w, v_w_attn_branch, v_w_conv_branch, v_w_out, v_ffn_norm, v_w_up, v_ffn_conv_w, v_w_down, v_final_norm):
    me = 2 * lax.axis_index("x") + lax.axis_index("y")
    names = ("w_in", "w_br", "w_out", "w_up", "w_down")
    w_of = dict(w_in=w_in[0].T, w_out=w_out[0], w_up=w_up[0], w_down=w_down[0])
    m_of = dict(w_in=m_w_in[0].T, w_out=m_w_out[0], w_up=m_w_up[0], w_down=m_w_down[0])
    v_of = dict(w_in=v_w_in[0].T, w_out=v_w_out[0], w_up=v_w_up[0], w_down=v_w_down[0])
    ab = (w_attn_branch[0], m_w_attn_branch[0], v_w_attn_branch[0])
    cb = (w_conv_branch[0], m_w_conv_branch[0], v_w_conv_branch[0])

    pos = jnp.stack([me, lax.axis_index("c")]).astype(jnp.int32)

    lay = dict(zip(names, BIG))
    xs, target, sk = x[0], loss_target[0], sinks[0]
    s = xs.shape[0]
    tm, tm2, bk, bk2 = min(256, s), min(512, s), min(1024, s), min(2048, s)

    taps, (_, t0) = _pack_rows([conv_w[0], ffn_conv_w[0].reshape(3 * (FF2 // N_CHIPS // 128), 128)])
    placed = {"w_in": _place_cast(w_of["w_in"], lay["w_in"], pos, "cast_w_in")}
    fly_in, started = _start_exchange("gather_in_start", [_gather_near(lay["w_in"], "w_in")], {"w_in": placed["w_in"]})
    taps_flight, started = _start_exchange("taps_start", [_to_all("v", "slots")],
                                           {"v": taps + started[0:1], "slots": jnp.zeros((N_DEV, *taps.shape), F32)})
    placed["w_br"] = _place_cast_pair(ab[0], cb[0], lay["w_br"], pos, "cast_w_br", after=started)
    for n in names[2:]:
        placed[n] = _place_cast(w_of[n], lay[n], pos, "cast_" + n, after=started)
    trio = ("w_br", "w_out")
    fly_in, started = _relay_exchange(
        "gather_in_relay", fly_in, [_gather_far(lay["w_in"], "w_in"), _gather_d2d(lay["w_in"], "w_in", chips=(1, 2))],
        after=placed["w_down"])
    (fly_trio, fly_up, fly_down), started = _start_exchanges("gather_rest_start", [
        ([_gather_ici(lay[n], n) for n in ws], {**{n: placed[n] for n in ws}, **behind})
        for ws, behind in ((trio, {"behind": started}), (("w_up",), {}), (("w_down",), {}))])

    got = _finish_exchange("gather_in_wait", fly_in, after=started)
    w_in_full = _exchange("gather_in_d2d", [[_gather_d2d(lay["w_in"], "w_in", chips=(3,))]],
                          bufs=got)["w_in"].reshape(IN_W, D_MODEL)
    xn, qkv, c3, gates = _inproj_fwd(xs, mix_norm, w_in_full, b_in, tm2)
    got = _finish_exchange("gather_trio_wait", fly_trio, after=qkv)
    k2 = _Carry([_gather_d2d(lay[n], n) for n in trio], bufs={n: got[n] for n in trio})
    attn = _attn_fwd(qkv, sk, comm=k2)
    w_br = k2.out["w_br"]
    w_out_full = k2.out["w_out"].reshape(D_MODEL, D_MODEL)
    k3 = _Carry([_gather_d2d(lay["w_up"], "w_up")], bufs=_finish_exchange("gather_up_wait", fly_up, after=attn))
    taps = _finish_exchange("taps_wait", taps_flight, after=attn)
    taps = lax.dynamic_update_slice(taps["slots"], taps["v"][None], (2 * me + lax.axis_index("c"), 0, 0))
    conv_full = taps[0::2, 0:3].transpose(1, 0, 2).reshape(3, CONV_W)
    ffn_cw_full = taps[0::2, t0:t0 + 33].reshape(N_CHIPS, 3, FF2 // N_CHIPS).transpose(1, 0, 2).reshape(3, FF2)
    conv, a, cv, merged, h1, hn = _mix_fwd(xs, attn, c3, gates, conv_full, w_br, w_out_full, ffn_norm, tm2, comm=k3)
    w_up_full = k3.out["w_up"]
    w_down_full = _exchange("gather_down_d2d", [[_gather_d2d(lay["w_down"], "w_down")]],
                            bufs=_finish_exchange("gather_down_wait", fly_down, after=hn))["w_down"].reshape(D_FF, D_MODEL)
    u, up, act, dh2, loss_part, g_fn = _ffn_fwd_loss(hn, h1, w_up_full, ffn_cw_full, w_down_full,
                                                     final_norm[None, :], target, tm)

    grads, sums, slots = {}, {}, {}

    def pair(*ws):
        return _Carry([_rs_pair(lay[n], "g_" + n, "t_" + n) for n in ws], reads={"g_" + n: grads[n] for n in ws},
                      fresh={"t_" + n: _theirs_shape(lay[n], grads[n].dtype) for n in ws})

    def chips(*ws, also=None):
        k = _Carry([_rs_chips(lay[n], "s_" + n, "r_" + n) for n in ws], reads={"s_" + n: sums[n] for n in ws},
                   fresh={"r_" + n: _slots_shape(lay[n]) for n in ws})
        if also is not None:
            k = _Carry(k.jobs + also.jobs, {**k.reads, **also.reads}, None, {**k.fresh, **also.fresh})
        return k

    def pair_sums(k, *ws):
        for n in ws:
            sums[n] = _pair_sum(grads[n], k.out["t_" + n], lay[n], pos, "pair_sum_" + n)

    def take_slots(k, *ws):
        for n in ws:
            slots[n] = k.out["r_" + n]

    du, dh1, g_fcw, g_g2 = _ffn_bwd(dh2, u, up, h1, w_up_full, ffn_cw_full, w_down_full, ffn_norm, tm)
    grads["w_down"] = _wgrad(act, dh2, D_FF // 2, D_MODEL, bk2, "wgrad_down").reshape(lay["w_down"].whole())
    k4 = pair("w_down")
    grads["w_up"] = _wgrad(hn, du, D_MODEL, FF2 // 4, bk2, "wgrad_up", comm=k4)
    pair_sums(k4, "w_down")
    k5 = chips("w_down", also=pair("w_up"))
    dattn, dc3, dgt, g_cw, grads["w_br"], gw_out = _mix_bwd(
        dh1, gates, a, cv, c3, attn, conv, merged, conv_full, w_br, w_out_full, tm2, comm=k5)
    grads["w_out"] = gw_out.reshape(lay["w_out"].whole())
    take_slots(k5, "w_down")
    pair_sums(k5, "w_up")
    up_flight, started = _start_exchange("rs_chips_up_start", [_rs_chips(lay["w_up"], "s", "r")],
                                         {"s": sums["w_up"], "r": _slots_shape(lay["w_up"])})
    k6 = pair(*trio)
    k6.reads["after"] = started
    dq, dk_even, dk_odd, dv_even, dv_odd, g_sk = _attn_bwd(qkv, sk, attn, dattn, comm=k6)
    pair_sums(k6, *trio)
    trio_flight, started = _start_exchange(
        "rs_chips_trio_start", [_rs_chips(lay[n], "s_" + n, "r_" + n) for n in trio],
        {**{"s_" + n: sums[n] for n in trio}, **{"r_" + n: _slots_shape(lay[n]) for n in trio}})
    behind = mix_norm + jnp.tile(started[0:1], (1, D_MODEL // 128))
    grad_x, gw_in, g_b, g_g1 = _inproj_bwd(dq, (dk_even, dk_odd), (dv_even, dv_odd), dc3, dgt, w_in_full, xs, xn,
                                           dh1, behind)
    grads["w_in"] = gw_in.reshape(lay["w_in"].whole())

    in_flight, started = _start_exchange("rs_pair_in_start", [_rs_pair(lay["w_in"], "g", "t")],
                                         {"g": grads["w_in"], "t": _theirs_shape(lay["w_in"])})
    parts = [loss_part, g_g1, g_b, jnp.pad(g_sk[:, 0], (0, 120))[None, :], g_cw, g_g2, g_fcw, g_fn]
    packed, at = _pack_rows([p.reshape(-1, 128) for p in parts])
    small_flight, started = _start_exchange("small_start", [_to_all("v", "slots")],
                                            {"v": packed + started[0:1], "slots": jnp.zeros((N_DEV, *packed.shape), F32)})
    landed = _finish_exchange("rs_chips_up_wait", up_flight, after=started)
    halves = dict(zip(("w_down", "w_up"), _chip_sums(
        [(sums["w_down"], slots["w_down"], lay["w_down"]), (landed["s"], landed["r"], lay["w_up"])], pos,
        "chip_sum_w_down_up")))
    landed = _finish_exchange("rs_pair_in_wait", in_flight, after=halves["w_up"])
    sums["w_in"] = _pair_sum(landed["g"], landed["t"], lay["w_in"], pos, "pair_sum_w_in")
    (in_flight, down_flight, up_flight), started = _start_exchanges("rs_chips_in_start", [
        ([_rs_chips(lay["w_in"], "s", "r")], {"s": sums["w_in"], "r": _slots_shape(lay["w_in"])}),
        ([_rs_share(lay["w_down"], "w_down")], {"w_down": halves["w_down"]}),
        ([_rs_share(lay["w_up"], "w_up")], {"w_up": halves["w_up"]})])
    landed = _finish_exchange("rs_chips_trio_wait", trio_flight, after=started)
    halves.update(zip(trio, _chip_sums([(landed["s_" + n], landed["r_" + n], lay[n]) for n in trio], pos,
                                       "chip_sum_w_br_out")))
    shared = _exchange("share_halves", [[_rs_share(lay[n], n) for n in trio]], bufs={n: halves[n] for n in trio})
    shared["w_down"] = _finish_exchange("share_down_wait", down_flight, after=shared[trio[-1]])["w_down"]
    shared["w_up"] = _finish_exchange("share_up_wait", up_flight, after=shared["w_down"])["w_up"]

    def adam(n, g, after=None):
        return _rowwise(lambda w, g, m, v: (g, *_adamw(w, g, m, v)), [w_of[n], g, m_of[n], v_of[n]], [F32] * 4,
                        "adamw_" + n, after=after)

    new_of, last = {}, None
    for n in ("w_down", "w_up", "w_out"):
        new_of[n] = adam(n, shared[n], last)
        last = new_of[n][1]
    new_of["w_ab"], new_of["w_cb"] = _adamw_pair(ab, cb, shared["w_br"], after=last)
    last = new_of["w_cb"][1]

    arrived = _finish_exchange("small_wait", small_flight, after=last)
    flat = lambda k: lambda t, j, chip: at[k] + j
    mine = lambda k, per_tap: lambda t, j, chip: at[k] + per_tap * t + (per_tap // N_CHIPS) * chip + j
    rows = lambda a: a.reshape(a.shape[1], 1, a.shape[2])
    small_p = [
        (mix_norm, m_mix_norm, v_mix_norm, flat(1)), (b_in, m_b_in, v_b_in, flat(2)), (sinks, m_sinks, v_sinks, flat(3)),
        (rows(conv_w), rows(m_conv_w), rows(v_conv_w), mine(4, CONV_W // 128)),
        (ffn_norm, m_ffn_norm, v_ffn_norm, flat(5)),
        (rows(ffn_conv_w), rows(m_ffn_conv_w), rows(v_ffn_conv_w), mine(6, FF2 // 128)),
        (final_norm[None, :], m_final_norm[None, :], v_final_norm[None, :], flat(7))]
    small_new = _adamw_small(pos, arrived["v"], arrived["slots"], small_p)
    loss = small_new[0][0, 0]
    small_g = small_new[1::4]
    small_new = [small_new[4 * k + 2:4 * k + 5] for k in range(len(small_p))]

    landed = _finish_exchange("rs_chips_in_wait", in_flight, after=small_new[0][0])
    half_in = _chip_sums([(landed["s"], landed["r"], lay["w_in"])], pos, "chip_sum_w_in")[0]
    shared["w_in"] = _exchange("share_in", [[_rs_share(lay["w_in"], "w_in")]], bufs={"w_in": half_in})["w_in"]
    new_of["w_in"] = [a.T for a in adam("w_in", shared["w_in"])]
    big = ("w_in", "w_ab", "w_cb", "w_out", "w_up", "w_down")
    big_g = [new_of[n][0] for n in big]
    big_new = [new_of[n][1:] for n in big]

    order = [("s", 0), ("b", 0), ("s", 1), ("s", 2), ("s", 3), ("b", 1), ("b", 2), ("b", 3), ("s", 4), ("b", 4),
             ("s", 5), ("b", 5), ("s", 6)]
    shapes = [mix_norm.shape, w_in.shape, b_in.shape, sinks.shape, conv_w.shape, w_attn_branch.shape,
              w_conv_branch.shape, w_out.shape, ffn_norm.shape, w_up.shape, ffn_conv_w.shape, w_down.shape,
              final_norm.shape]
    out_g = [(small_g[k] if kind == "s" else big_g[k]).reshape(shp) for (kind, k), shp in zip(order, shapes)]
    news = [[(small_new[k][j] if kind == "s" else big_new[k][j]).reshape(shp) for (kind, k), shp in zip(order, shapes)]
            for j in range(3)]
    return (loss, grad_x[None], *out_g, *news[0], *news[1], *news[2])
```

```python
import functools

import jax
import jax.numpy as jnp
from jax import lax
from jax.experimental import pallas as pl
from jax.experimental.pallas import tpu as pltpu

F32 = jnp.float32
BF16 = jnp.bfloat16

D_MODEL = 1024
HEAD_DIM = 64
N_HEADS = 8
N_KV_HEADS = 2
GROUP = N_HEADS // N_KV_HEADS
BLOCK = 128
ATTN_SCALE = HEAD_DIM ** -0.5
ATTN_W = N_HEADS * HEAD_DIM
KV_W = N_KV_HEADS * HEAD_DIM
CONV_W = 512
QKV_W = ATTN_W + 2 * KV_W
C3_W = 3 * CONV_W
GATES_W = 2 * D_MODEL
IN_W = QKV_W + C3_W + GATES_W
D_FF = 2816
FF2 = 2 * D_FF
NORM_EPS = 1e-5
N_CHIPS = 4
IN_SHARD = IN_W // N_CHIPS
NEG = -1e30

ADAM_LR = 0.001
ADAM_B1 = 0.9
ADAM_B2 = 0.999
ADAM_EPS = 1e-08
ADAM_WD = 0.01
ADAM_STEP = 10

VMEM_LIMIT = 56 * 1024 * 1024
MESH = pl.DeviceIdType.MESH

NT = (((1,), (1,)), ((), ()))
TN = (((0,), (0,)), ((), ()))


def _params(*sem):
    return pltpu.CompilerParams(dimension_semantics=sem, vmem_limit_bytes=VMEM_LIMIT)


def _resident(shape):
    return pl.BlockSpec(shape, lambda *_: (0,) * len(shape), pipeline_mode=pl.Buffered(1))


def _sigmoid(v):
    return 0.5 * jnp.tanh(0.5 * v) + 0.5


def _rstd(v):
    return lax.rsqrt(jnp.mean(v * v, axis=-1, keepdims=True) + NORM_EPS)


def _rms_bwd(dy, v, rstd, g):
    vhat = v * rstd
    t = dy * g
    return rstd * (t - vhat * jnp.mean(t * vhat, axis=-1, keepdims=True)), dy * vhat


def _taps(z, cw):
    return cw[2:3] * z + cw[1:2] * pltpu.roll(z, 1, 0) + cw[0:1] * pltpu.roll(z, 2, 0)


def _causal_conv(z, prev, cw):
    edge = _taps(jnp.concatenate([prev, z[0:8]], axis=0), cw)
    return jnp.concatenate([edge[8:16], _taps(z, cw)[8:]], axis=0)


def _rows_after(z, nxt):
    n = z.shape[0]
    edge = jnp.concatenate([z[n - 8:n], nxt], axis=0)
    return tuple(jnp.concatenate([pltpu.roll(z, n - k, 0)[:n - 8], pltpu.roll(edge, 16 - k, 0)[0:8]], axis=0)
                 for k in (1, 2))


def _inproj_fwd(x, g1, w_in, b_in, tm, comm=None):
    s = x.shape[0]

    def body(x_ref, g_ref, w_ref, b_ref, xn_ref, qkv_ref, c3_ref, gt_ref):
        xf = x_ref[...]
        xn = (xf * _rstd(xf) * g_ref[...]).astype(BF16)
        xn_ref[...] = xn

        proj = (lax.dot_general(xn, w_ref[...], NT, preferred_element_type=F32) + b_ref[...]).astype(BF16)
        qkv_ref[...] = proj[:, :QKV_W]
        c3_ref[...] = proj[:, QKV_W:QKV_W + C3_W]
        gt_ref[...] = proj[:, QKV_W + C3_W:]

    row = lambda w: pl.BlockSpec((tm, w), lambda i: (i, 0))
    return _call(
        comm, body, name="inproj_fwd", grid=(s // tm,),
        in_specs=[row(D_MODEL), _resident((1, D_MODEL)), _resident((IN_W, D_MODEL)), _resident((1, IN_W))],
        out_specs=[row(D_MODEL), row(QKV_W), row(C3_W), row(GATES_W)],
        out_shape=[jax.ShapeDtypeStruct((s, D_MODEL), BF16), jax.ShapeDtypeStruct((s, QKV_W), BF16),
                   jax.ShapeDtypeStruct((s, C3_W), BF16), jax.ShapeDtypeStruct((s, GATES_W), BF16)],
        compiler_params=_params("parallel"),
    )(x, g1, w_in, b_in)


def _attn_bias():
    kj = jnp.arange(2 * BLOCK)[:, None]
    qi = (jnp.arange(GROUP * BLOCK) % BLOCK)[None, :]
    band = (kj > qi) & (kj <= qi + BLOCK)
    return jnp.stack([jnp.where(band & (kj >= BLOCK), 0.0, NEG), jnp.where(band, 0.0, NEG)]).astype(F32)


def _attn_bias_specs():
    shape = (None, 2 * BLOCK, GROUP * BLOCK)
    return pl.BlockSpec(shape, lambda i: (jnp.minimum(i, 1), 0, 0)), pl.BlockSpec(shape, lambda i: (1, 0, 0))


def _sink_row(sk_ref, h):
    lane = lax.broadcasted_iota(jnp.int32, (1, GROUP * BLOCK), 1)
    row = jnp.full((1, GROUP * BLOCK), sk_ref[h * GROUP], F32)
    for g in range(1, GROUP):
        row = jnp.where(lane >= g * BLOCK, sk_ref[h * GROUP + g], row)
    return row


def _stack_heads(t, h):
    return jnp.concatenate(
        [t[:, (h * GROUP + g) * HEAD_DIM:(h * GROUP + g + 1) * HEAD_DIM] for g in range(GROUP)], axis=0)


def _unstack_heads(per_kv):
    return jnp.concatenate(
        [t[g * BLOCK:(g + 1) * BLOCK] for t in per_kv for g in range(GROUP)], axis=1)


def _block_specs(n, steps):
    cur = lambda i: jnp.minimum(i, steps - 1)
    prev = lambda i: jnp.maximum(n * jnp.minimum(i, steps - 1) - 1, 0)
    kv = ATTN_W // KV_W
    return (pl.BlockSpec((n * BLOCK, ATTN_W), lambda i: (cur(i), 0)),
            pl.BlockSpec((BLOCK, KV_W), lambda i: (prev(i), kv)), pl.BlockSpec((n * BLOCK, KV_W), lambda i: (cur(i), kv)),
            pl.BlockSpec((BLOCK, KV_W), lambda i: (prev(i), kv + 1)),
            pl.BlockSpec((n * BLOCK, KV_W), lambda i: (cur(i), kv + 1)))


def _attn_fwd(qkv, sinks, comm=None):
    s = qkv.shape[0]
    n = min(4, s // BLOCK)
    steps = s // (n * BLOCK)

    def body(sk_ref, bias0_ref, bias1_ref, q_ref, kp_ref, kc_ref, vp_ref, vc_ref, o_ref):
        kc, vc = kc_ref[...], vc_ref[...]
        for b in range(n):
            rows, before = slice(b * BLOCK, (b + 1) * BLOCK), slice((b - 1) * BLOCK, b * BLOCK)
            kp, vp = (kp_ref[...], vp_ref[...]) if b == 0 else (kc[before], vc[before])
            q, bias = q_ref[rows, :], (bias0_ref if b == 0 else bias1_ref)[...]
            outs = []
            for h in range(N_KV_HEADS):
                hs = slice(h * HEAD_DIM, (h + 1) * HEAD_DIM)
                k2 = jnp.concatenate([kp[:, hs], kc[rows, hs]], axis=0)
                v2 = jnp.concatenate([vp[:, hs], vc[rows, hs]], axis=0)
                sc = lax.dot_general(k2, _stack_heads(q, h), NT, preferred_element_type=F32) * ATTN_SCALE + bias
                sink = _sink_row(sk_ref, h)
                m = jnp.maximum(jnp.max(sc, axis=0, keepdims=True), sink)
                p = jnp.exp(sc - m)
                den = jnp.sum(p, axis=0, keepdims=True) + jnp.exp(sink - m)
                out = lax.dot_general(v2, p.astype(BF16), TN, preferred_element_type=F32) / den
                outs.append(out.T)
            o_ref[rows, :] = _unstack_heads(outs).astype(BF16)

    return _call(
        comm, body, name="attn_fwd", grid=(steps,),
        in_specs=[pl.BlockSpec(memory_space=pltpu.SMEM), *_attn_bias_specs(), *_block_specs(n, steps)],
        out_specs=pl.BlockSpec((n * BLOCK, ATTN_W), lambda i: (i, 0)),
        out_shape=jax.ShapeDtypeStruct((s, ATTN_W), BF16),
        compiler_params=_params("parallel"),
    )(sinks, _attn_bias(), _attn_bias(), qkv, qkv, qkv, qkv, qkv)


def _mix_fwd(x, attn, c3, gates, conv_w, w_br, w_out, g2, tm, comm=None):
    s = x.shape[0]

    def body(x_ref, at_ref, c3_ref, gt_ref, cw_ref, wbr_ref, wo_ref, g_ref,
             conv_ref, a_ref, cv_ref, mg_ref, h1_ref, hn_ref, carry_ref):
        @pl.when(pl.program_id(0) == 0)
        def _():
            carry_ref[...] = jnp.zeros_like(carry_ref)

        c3v = c3_ref[...].astype(F32)
        cb, cc, cx = c3v[:, :CONV_W], c3v[:, CONV_W:2 * CONV_W], c3v[:, 2 * CONV_W:]
        z = cc * cx
        cz = _causal_conv(z, carry_ref[...], cw_ref[...])
        carry_ref[...] = z[tm - 8:tm]
        conv = (cb * cz).astype(BF16)
        conv_ref[...] = conv
        a = jnp.dot(at_ref[...], wbr_ref[:ATTN_W, :], preferred_element_type=F32)
        cv = jnp.dot(conv, wbr_ref[ATTN_W:, :], preferred_element_type=F32)
        a_ref[...] = a.astype(BF16)
        cv_ref[...] = cv.astype(BF16)
        gt = gt_ref[...].astype(F32)
        merged = (_sigmoid(gt[:, :D_MODEL]) * a + _sigmoid(gt[:, D_MODEL:]) * cv).astype(BF16)
        mg_ref[...] = merged
        h1 = x_ref[...] + jnp.dot(merged, wo_ref[...], preferred_element_type=F32)
        h1_ref[...] = h1
        hn_ref[...] = (h1 * _rstd(h1) * g_ref[...]).astype(BF16)

    row = lambda w: pl.BlockSpec((tm, w), lambda i: (i, 0))
    return _call(
        comm, body, name="mix_fwd", grid=(s // tm,),
        in_specs=[row(D_MODEL), row(ATTN_W), row(C3_W), row(GATES_W), _resident(conv_w.shape),
                  _resident((ATTN_W + CONV_W, D_MODEL)), _resident((D_MODEL, D_MODEL)), _resident((1, D_MODEL))],
        out_specs=[row(CONV_W), row(D_MODEL), row(D_MODEL), row(D_MODEL), row(D_MODEL), row(D_MODEL)],
        out_shape=[jax.ShapeDtypeStruct((s, CONV_W), BF16), jax.ShapeDtypeStruct((s, D_MODEL), BF16),
                   jax.ShapeDtypeStruct((s, D_MODEL), BF16), jax.ShapeDtypeStruct((s, D_MODEL), BF16),
                   jax.ShapeDtypeStruct((s, D_MODEL), F32), jax.ShapeDtypeStruct((s, D_MODEL), BF16)],
        scratch_shapes=[pltpu.VMEM((8, CONV_W), F32)],
        compiler_params=_params("arbitrary"),
    )(x, attn, c3, gates, conv_w, w_br, w_out, g2)


def _ffn_fwd_loss(hn, h1, w_up, ffn_cw, w_down, g3, target, tm):
    s = hn.shape[0]

    def body(hn_ref, h1_ref, wu_ref, cw_ref, wd_ref, g_ref, t_ref,
             u_ref, up_ref, act_ref, dh2_ref, loss_ref, gfn_ref, carry_ref):
        @pl.when(pl.program_id(0) == 0)
        def _():
            carry_ref[...] = jnp.zeros_like(carry_ref)
            loss_ref[...] = jnp.zeros_like(loss_ref)
            gfn_ref[...] = jnp.zeros_like(gfn_ref)

        u = jnp.dot(hn_ref[...], wu_ref[...], preferred_element_type=F32)
        u_ref[...] = u.astype(BF16)
        up = _causal_conv(u, carry_ref[...], cw_ref[...])
        up_ref[...] = up
        carry_ref[...] = u[tm - 8:tm]
        gate, val = up[:, :D_FF], up[:, D_FF:]
        act = (gate * _sigmoid(gate) * val).astype(BF16)
        act_ref[...] = act
        h2 = h1_ref[...] + jnp.dot(act, wd_ref[...], preferred_element_type=F32)
        rstd = _rstd(h2)
        g = g_ref[...]
        err = h2 * rstd * g - t_ref[...]
        loss_ref[...] += jnp.sum(err * err) * (0.5 / D_MODEL)
        dh2, dg = _rms_bwd(err * (1.0 / D_MODEL), h2, rstd, g)
        dh2_ref[...] = dh2
        gfn_ref[...] += jnp.sum(dg, axis=0, keepdims=True)

    row = lambda w: pl.BlockSpec((tm, w), lambda i: (i, 0))
    acc = lambda w: pl.BlockSpec((1, w), lambda i: (0, 0))
    return pl.pallas_call(
        body, name="ffn_fwd_loss", grid=(s // tm,),
        in_specs=[row(D_MODEL), row(D_MODEL), _resident((D_MODEL, FF2)), _resident(ffn_cw.shape),
                  _resident((D_FF, D_MODEL)), _resident((1, D_MODEL)), row(D_MODEL)],
        out_specs=[row(FF2), row(FF2), row(D_FF), row(D_MODEL), acc(128), acc(D_MODEL)],
        out_shape=[jax.ShapeDtypeStruct((s, FF2), BF16), jax.ShapeDtypeStruct((s, FF2), F32),
                   jax.ShapeDtypeStruct((s, D_FF), BF16),
                   jax.ShapeDtypeStruct((s, D_MODEL), F32), jax.ShapeDtypeStruct((1, 128), F32),
                   jax.ShapeDtypeStruct((1, D_MODEL), F32)],
        scratch_shapes=[pltpu.VMEM((8, FF2), F32)],
        compiler_params=_params("arbitrary"),
    )(hn, h1, w_up, ffn_cw, w_down, g3, target)


def _ffn_bwd(dh2, u, up, h1, w_up, ffn_cw, w_down, g2, tm):
    s = dh2.shape[0]
    nt = s // tm

    def body(dh2_ref, u_ref, up_ref, h1_ref, wu_ref, cw_ref, wd_ref, g_ref,
             du_ref, dh1_ref, gcw_ref, gg_ref, carry_ref):
        @pl.when(pl.program_id(0) == 0)
        def _():
            for ref in (carry_ref, gcw_ref, gg_ref):
                ref[...] = jnp.zeros_like(ref)

        dh2v = dh2_ref[...]
        dact = lax.dot_general(dh2v.astype(BF16), wd_ref[...], NT, preferred_element_type=F32)
        upv = up_ref[...]
        gate, val = upv[:, :D_FF], upv[:, D_FF:]
        sg = _sigmoid(gate)
        dval = dact * (gate * sg)
        dgate = dact * val * (sg * (1.0 + gate * (1.0 - sg)))
        dup = jnp.concatenate([dgate, dval], axis=1)
        dup1, dup2 = _rows_after(dup, carry_ref[...])
        carry_ref[...] = dup[0:8]
        u = u_ref[...].astype(F32)
        gcw_ref[2:3, :] += jnp.sum(dup * u, axis=0, keepdims=True)
        gcw_ref[1:2, :] += jnp.sum(dup1 * u, axis=0, keepdims=True)
        gcw_ref[0:1, :] += jnp.sum(dup2 * u, axis=0, keepdims=True)
        cw = cw_ref[...]
        du = (cw[2:3] * dup + cw[1:2] * dup1 + cw[0:1] * dup2).astype(BF16)
        du_ref[...] = du
        dhn = lax.dot_general(du, wu_ref[...], NT, preferred_element_type=F32)
        h1v = h1_ref[...]
        dh1, dg = _rms_bwd(dhn, h1v, _rstd(h1v), g_ref[...])
        dh1_ref[...] = dh2v + dh1
        gg_ref[...] += jnp.sum(dg, axis=0, keepdims=True)

    row = lambda w: pl.BlockSpec((tm, w), lambda i: (nt - 1 - i, 0))
    return pl.pallas_call(
        body, name="ffn_bwd", grid=(nt,),
        in_specs=[row(D_MODEL), row(FF2), row(FF2),
                  row(D_MODEL), _resident((D_MODEL, FF2)), _resident(ffn_cw.shape), _resident((D_FF, D_MODEL)),
                  _resident((1, D_MODEL))],
        out_specs=[row(FF2), row(D_MODEL), pl.BlockSpec((3, FF2), lambda i: (0, 0)),
                   pl.BlockSpec((1, D_MODEL), lambda i: (0, 0))],
        out_shape=[jax.ShapeDtypeStruct((s, FF2), BF16), jax.ShapeDtypeStruct((s, D_MODEL), F32),
                   jax.ShapeDtypeStruct((3, FF2), F32), jax.ShapeDtypeStruct((1, D_MODEL), F32)],
        scratch_shapes=[pltpu.VMEM((8, FF2), F32)],
        compiler_params=_params("arbitrary"),
    )(dh2, u, up, h1, w_up, ffn_cw, w_down, g2)


def _mix_bwd(dh1, gates, a, cv, c3, attn, conv, merged, conv_w, w_br, w_out, tm, comm=None):
    s = dh1.shape[0]
    nt = s // tm
    halo = 16

    def body(dh1_ref, gt_ref, a_ref, cv_ref, c3_ref, ch_ref, at_ref, cn_ref, mg_ref, cw_ref, wbr_ref,
             wo_ref, dat_ref, dc3_ref, dgt_ref, gcw_ref, gbr_ref, gout_ref, carry_ref, br_acc, out_acc):
        i = pl.program_id(0)

        @pl.when(i == 0)
        def _():
            for ref in (carry_ref, gcw_ref, br_acc, out_acc):
                ref[...] = jnp.zeros_like(ref)

        dh1v = dh1_ref[...].astype(BF16)
        out_acc[...] += lax.dot_general(mg_ref[...], dh1v, TN, preferred_element_type=F32)
        dm = lax.dot_general(dh1v, wo_ref[...], NT, preferred_element_type=F32)
        gt = gt_ref[...].astype(F32)
        sa, sc = _sigmoid(gt[:, :D_MODEL]), _sigmoid(gt[:, D_MODEL:])
        da = (dm * sa).astype(BF16)
        dcv = (dm * sc).astype(BF16)
        br_acc[:ATTN_W, :] += lax.dot_general(at_ref[...], da, TN, preferred_element_type=F32)
        br_acc[ATTN_W:, :] += lax.dot_general(cn_ref[...], dcv, TN, preferred_element_type=F32)
        dgt_ref[...] = jnp.concatenate(
            [dm * a_ref[...].astype(F32) * (sa * (1.0 - sa)), dm * cv_ref[...].astype(F32) * (sc * (1.0 - sc))],
            axis=1).astype(BF16)
        dat_ref[...] = lax.dot_general(da, wbr_ref[:ATTN_W, :], NT, preferred_element_type=F32).astype(BF16)
        dconv = lax.dot_general(dcv, wbr_ref[ATTN_W:, :], NT, preferred_element_type=F32)
        c3v = c3_ref[...].astype(F32)
        cb, cc, cx = c3v[:, :CONV_W], c3v[:, CONV_W:2 * CONV_W], c3v[:, 2 * CONV_W:]
        z = cc * cx
        chv = ch_ref[...].astype(F32)[halo - 8:halo] * (i < nt - 1).astype(F32)
        zh = chv[:, CONV_W:2 * CONV_W] * chv[:, 2 * CONV_W:]
        cw = cw_ref[...]
        cz = _causal_conv(z, zh, cw)
        dcz = dconv * cb
        dcz1, dcz2 = _rows_after(dcz, carry_ref[...])
        carry_ref[...] = dcz[0:8]
        gcw_ref[2:3, :] += jnp.sum(dcz * z, axis=0, keepdims=True)
        gcw_ref[1:2, :] += jnp.sum(dcz1 * z, axis=0, keepdims=True)
        gcw_ref[0:1, :] += jnp.sum(dcz2 * z, axis=0, keepdims=True)
        dz = cw[2:3] * dcz + cw[1:2] * dcz1 + cw[0:1] * dcz2
        dc3_ref[...] = jnp.concatenate([dconv * cz, dz * cx, dz * cc], axis=1).astype(BF16)

        @pl.when(i == nt - 1)
        def _():
            gbr_ref[...] = br_acc[...].astype(BF16)
            gout_ref[...] = out_acc[...].astype(BF16)

    row = lambda w: pl.BlockSpec((tm, w), lambda i: (nt - 1 - i, 0))
    return _call(
        comm, body, name="mix_bwd", grid=(nt,),
        in_specs=[row(D_MODEL), row(GATES_W), row(D_MODEL), row(D_MODEL), row(C3_W),
                  pl.BlockSpec((halo, C3_W), lambda i: (jnp.maximum((nt - 1 - i) * (tm // halo) - 1, 0), 0)),
                  row(ATTN_W), row(CONV_W), row(D_MODEL), _resident(conv_w.shape),
                  _resident((ATTN_W + CONV_W, D_MODEL)), _resident((D_MODEL, D_MODEL))],
        out_specs=[row(ATTN_W), row(C3_W), row(GATES_W), pl.BlockSpec((3, CONV_W), lambda i: (0, 0)),
                   _resident((ATTN_W + CONV_W, D_MODEL)), _resident((D_MODEL, D_MODEL))],
        out_shape=[jax.ShapeDtypeStruct((s, ATTN_W), BF16), jax.ShapeDtypeStruct((s, C3_W), BF16),
                   jax.ShapeDtypeStruct((s, GATES_W), BF16), jax.ShapeDtypeStruct((3, CONV_W), F32),
                   jax.ShapeDtypeStruct((ATTN_W + CONV_W, D_MODEL), BF16),
                   jax.ShapeDtypeStruct((D_MODEL, D_MODEL), BF16)],
        scratch_shapes=[pltpu.VMEM((8, CONV_W), F32), pltpu.VMEM((ATTN_W + CONV_W, D_MODEL), F32),
                        pltpu.VMEM((D_MODEL, D_MODEL), F32)],
        compiler_params=_params("arbitrary"),
    )(dh1, gates, a, cv, c3, c3, attn, conv, merged, conv_w, w_br, w_out)


def _attn_bwd(qkv, sinks, o, do, comm=None):
    s = qkv.shape[0]
    npair = s // (2 * BLOCK)

    def one_block(sk_ref, bias, q, kp, kc, vp, vc, ov, dov, dsk_ref):
        dqs, dks, dvs = [], [], []
        for h in range(N_KV_HEADS):
            hs = slice(h * HEAD_DIM, (h + 1) * HEAD_DIM)
            k2 = jnp.concatenate([kp[:, hs], kc[:, hs]], axis=0)
            v2 = jnp.concatenate([vp[:, hs], vc[:, hs]], axis=0)
            qg, og, dog = _stack_heads(q, h), _stack_heads(ov, h), _stack_heads(dov, h)
            sc = lax.dot_general(k2, qg, NT, preferred_element_type=F32) * ATTN_SCALE + bias
            sink = _sink_row(sk_ref, h)
            m = jnp.maximum(jnp.max(sc, axis=0, keepdims=True), sink)
            p = jnp.exp(sc - m)
            psink = jnp.exp(sink - m)
            inv = 1.0 / (jnp.sum(p, axis=0, keepdims=True) + psink)
            p = p * inv
            delta = jnp.sum(dog.astype(F32) * og.astype(F32), axis=1, keepdims=True).T
            dp = lax.dot_general(v2, dog, NT, preferred_element_type=F32)
            ds = (p * (dp - delta)).astype(BF16)
            dqs.append((lax.dot_general(k2, ds, TN, preferred_element_type=F32) * ATTN_SCALE).T)
            dks.append(jnp.dot(ds, qg, preferred_element_type=F32) * ATTN_SCALE)
            dvs.append(jnp.dot(p.astype(BF16), dog, preferred_element_type=F32))
            dsink = -(psink * inv * delta)
            for g in range(GROUP):
                r = h * GROUP + g
                dsk_ref[r:r + 1, :] += jnp.sum(dsink[:, g * BLOCK:(g + 1) * BLOCK])
        return _unstack_heads(dqs), jnp.concatenate(dks, axis=1), jnp.concatenate(dvs, axis=1)

    def body(sk_ref, bias0_ref, bias1_ref, q_ref, kp_ref, kc_ref, vp_ref, vc_ref, o_ref, do_ref,
             dq_ref, dke_ref, dko_ref, dve_ref, dvo_ref, dsk_ref, ck_ref, cvv_ref):
        i = pl.program_id(0)

        @pl.when(i == 0)
        def _():
            for ref in (ck_ref, cvv_ref, dsk_ref):
                ref[...] = jnp.zeros_like(ref)

        @pl.when(i < npair)
        def _():
            kc, vc = kc_ref[...], vc_ref[...]
            first, second = slice(0, BLOCK), slice(BLOCK, 2 * BLOCK)
            dq0, dk0, dv0 = one_block(sk_ref, bias0_ref[...], q_ref[first, :], kp_ref[...], kc[first], vp_ref[...],
                                      vc[first], o_ref[first, :], do_ref[first, :], dsk_ref)
            dq1, dk1, dv1 = one_block(sk_ref, bias1_ref[...], q_ref[second, :], kc[first], kc[second], vc[first],
                                      vc[second], o_ref[second, :], do_ref[second, :], dsk_ref)
            dq_ref[first, :] = dq0.astype(BF16)
            dq_ref[second, :] = dq1.astype(BF16)
            dko_ref[...] = (ck_ref[...] + dk0[:BLOCK]).astype(BF16)
            dvo_ref[...] = (cvv_ref[...] + dv0[:BLOCK]).astype(BF16)
            dke_ref[...] = (dk0[BLOCK:] + dk1[:BLOCK]).astype(BF16)
            dve_ref[...] = (dv0[BLOCK:] + dv1[:BLOCK]).astype(BF16)
            ck_ref[...] = dk1[BLOCK:]
            cvv_ref[...] = dv1[BLOCK:]

        @pl.when(i == npair)
        def _():
            dko_ref[...] = ck_ref[...].astype(BF16)
            dvo_ref[...] = cvv_ref[...].astype(BF16)

    cur = lambda i: jnp.minimum(i, npair - 1)
    done = lambda i: jnp.maximum(i - 1, 0)
    rows = pl.BlockSpec((2 * BLOCK, ATTN_W), lambda i: (cur(i), 0))
    even = pl.BlockSpec((BLOCK, KV_W), lambda i: (cur(i), 0))
    odd = pl.BlockSpec((BLOCK, KV_W), lambda i: (done(i), 0))
    half = jax.ShapeDtypeStruct((s // 2, KV_W), BF16)
    return _call(
        comm, body, name="attn_bwd", grid=(npair + 1,),
        in_specs=[pl.BlockSpec(memory_space=pltpu.SMEM), *_attn_bias_specs(), *_block_specs(2, npair), rows, rows],
        out_specs=[rows, even, odd, even, odd, pl.BlockSpec((N_HEADS, 128), lambda i: (0, 0))],
        out_shape=[jax.ShapeDtypeStruct((s, ATTN_W), BF16), half, half, half, half,
                   jax.ShapeDtypeStruct((N_HEADS, 128), F32)],
        scratch_shapes=[pltpu.VMEM((BLOCK, KV_W), F32), pltpu.VMEM((BLOCK, KV_W), F32)],
        compiler_params=_params("arbitrary"),
    )(sinks, _attn_bias(), _attn_bias(), qkv, qkv, qkv, qkv, qkv, o, do)


def _inproj_bwd(dq, dk, dv, dc3, dgt, w_in, x, xn, dh1, g1):
    s = x.shape[0]
    tm = min(2 * BLOCK, s)
    nt = s // tm

    def body(dq_ref, dke_ref, dko_ref, dve_ref, dvo_ref, dc3_ref, dgt_ref, w_ref, x_ref, xn_ref, dh1_ref, g_ref,
             dx_ref, gw_ref, gb_ref, gg_ref, acc_ref):
        i = pl.program_id(0)

        @pl.when(i == 0)
        def _():
            for ref in (gb_ref, gg_ref, acc_ref):
                ref[...] = jnp.zeros_like(ref)

        dk = jnp.concatenate([dke_ref[...], dko_ref[...]], axis=0)
        dv = jnp.concatenate([dve_ref[...], dvo_ref[...]], axis=0)
        dp = jnp.concatenate([dq_ref[...], dk, dv, dc3_ref[...], dgt_ref[...]], axis=1)
        acc_ref[...] += lax.dot_general(dp, xn_ref[...], TN, preferred_element_type=F32)
        gb_ref[...] += jnp.sum(dp.astype(F32), axis=0, keepdims=True)
        dxn = jnp.dot(dp, w_ref[...], preferred_element_type=F32)
        xf = x_ref[...]
        dx, dg = _rms_bwd(dxn, xf, _rstd(xf), g_ref[...])
        dx_ref[...] = dh1_ref[...] + dx
        gg_ref[...] += jnp.sum(dg, axis=0, keepdims=True)

        @pl.when(i == nt - 1)
        def _():
            gw_ref[...] = acc_ref[...].astype(BF16)

    row = lambda w: pl.BlockSpec((tm, w), lambda i: (i, 0))
    acc = lambda w: pl.BlockSpec((1, w), lambda i: (0, 0))
    block = pl.BlockSpec((tm // 2, KV_W), lambda i: (i, 0))
    return pl.pallas_call(
        body, name="inproj_bwd", grid=(nt,),
        in_specs=[row(ATTN_W), block, block, block, block, row(C3_W), row(GATES_W), _resident((IN_W, D_MODEL)),
                  row(D_MODEL), row(D_MODEL), row(D_MODEL), _resident((1, D_MODEL))],
        out_specs=[row(D_MODEL), _resident((IN_W, D_MODEL)), acc(IN_W), acc(D_MODEL)],
        out_shape=[jax.ShapeDtypeStruct((s, D_MODEL), F32), jax.ShapeDtypeStruct((IN_W, D_MODEL), BF16),
                   jax.ShapeDtypeStruct((1, IN_W), F32), jax.ShapeDtypeStruct((1, D_MODEL), F32)],
        scratch_shapes=[pltpu.VMEM((IN_W, D_MODEL), F32)],
        compiler_params=_params("arbitrary"),
    )(dq, *dk, *dv, dc3, dgt, w_in, x, xn, dh1, g1)


def _wgrad(a, b, bm, bn, bk, name, comm=None):
    s, m = a.shape
    n = b.shape[1]
    nk = s // bk

    def body(a_ref, b_ref, o_ref, acc_ref):
        k = pl.program_id(2)

        @pl.when(k == 0)
        def _():
            acc_ref[...] = jnp.zeros_like(acc_ref)

        acc_ref[...] += lax.dot_general(a_ref[...].astype(BF16), b_ref[...].astype(BF16), TN,
                                        preferred_element_type=F32)

        @pl.when(k == nk - 1)
        def _():
            o_ref[...] = acc_ref[...].astype(BF16)

    return _call(
        comm, body, name=name, grid=(m // bm, n // bn, nk),
        in_specs=[pl.BlockSpec((bk, bm), lambda i, j, k: (k, i)), pl.BlockSpec((bk, bn), lambda i, j, k: (k, j))],
        out_specs=pl.BlockSpec((bm, bn), lambda i, j, k: (i, j)),
        out_shape=jax.ShapeDtypeStruct((m, n), BF16),
        scratch_shapes=[pltpu.VMEM((bm, bn), F32)],
        compiler_params=_params("parallel", "parallel", "arbitrary"),
    )(a, b)


class _Carry:
    def __init__(self, jobs, reads=None, bufs=None, fresh=None):
        self.jobs, self.reads, self.bufs, self.fresh = jobs, reads or {}, bufs or {}, fresh or {}
        self.out = {}


class _Job:
    def __init__(self, n_sems, plan):
        self.n_sems, self.plan = n_sems, plan


def _plan_all(jobs, hbm, send, recv):
    pos = _position()
    starts, waits, base = [], [], 0
    for job in jobs:
        s, w = job.plan(hbm, pos, send, recv, base)
        starts, waits, base = starts + s, waits + w, base + job.n_sems
    return starts, waits


def _call(comm, body, **kw):
    if comm is None:
        return pl.pallas_call(body, **kw)
    grid = kw["grid"]
    single = not isinstance(kw["out_shape"], (list, tuple))
    out_shape = [kw["out_shape"]] if single else list(kw["out_shape"])
    out_specs = [kw["out_specs"]] if single else list(kw["out_specs"])
    in_specs = list(kw["in_specs"])
    scratch = list(kw.get("scratch_shapes", ()))
    r_names, b_names, f_names = list(comm.reads), list(comm.bufs), list(comm.fresh)
    n_args, n_out, n_scr = len(in_specs), len(out_shape), len(scratch)
    n_sems = sum(j.n_sems for j in comm.jobs)

    def wrapped(*refs):
        k = n_args
        hbm = dict(zip(r_names, refs[k:k + len(r_names)]))
        k += len(r_names) + len(b_names)
        outs = refs[k:k + n_out]
        k += n_out
        hbm.update(zip(b_names + f_names, refs[k:k + len(b_names) + len(f_names)]))
        k += len(b_names) + len(f_names)
        send, recv = refs[k + n_scr:]
        starts, waits = _plan_all(comm.jobs, hbm, send, recv)
        ids = [pl.program_id(a) for a in range(len(grid))]
        first = functools.reduce(jnp.logical_and, [i == 0 for i in ids])
        last = functools.reduce(jnp.logical_and, [i == g - 1 for i, g in zip(ids, grid)])

        @pl.when(first)
        def _():
            for cp in starts:
                cp.start()

        body(*refs[:n_args], *outs, *refs[k:k + n_scr])

        @pl.when(last)
        def _():
            for cp in waits:
                cp.wait_recv()
            for cp in starts:
                cp.wait_send()

    sems = pltpu.SemaphoreType.DMA((n_sems,))
    held = [jax.ShapeDtypeStruct(a.shape, a.dtype) for a in comm.bufs.values()] + list(comm.fresh.values())
    call = pl.pallas_call(
        wrapped, name=kw["name"], grid=grid,
        in_specs=in_specs + [_ANY] * (len(r_names) + len(b_names)),
        out_specs=out_specs + [_ANY] * len(held),
        out_shape=out_shape + held,
        input_output_aliases={n_args + len(r_names) + i: n_out + i for i in range(len(b_names))},
        scratch_shapes=scratch + [sems, sems],
        compiler_params=_params(*["arbitrary"] * len(grid)),
    )

    def run(*args):
        res = call(*args, *comm.reads.values(), *comm.bufs.values())
        comm.out = dict(zip(b_names + f_names, res[n_out:]))
        return res[0] if single else res[:n_out]

    return run


def _exchange(name, phases, reads=None, bufs=None, fresh=None):
    comm = _Carry([j for ph in phases for j in ph], reads, bufs, fresh)
    r_names, b_names, f_names = list(comm.reads), list(comm.bufs), list(comm.fresh)
    n_sems = sum(j.n_sems for j in comm.jobs)

    def body(*refs):
        hbm = dict(zip(r_names, refs[:len(r_names)]))
        k = len(r_names) + len(b_names)
        hbm.update(zip(b_names + f_names, refs[k:k + len(b_names) + len(f_names)]))
        send, recv = refs[-2:]
        pos = _position()
        started, base = [], 0
        for ph in phases:
            waits = []
            for job in ph:
                s, w = job.plan(hbm, pos, send, recv, base)
                base += job.n_sems
                for cp in s:
                    cp.start()
                started, waits = started + s, waits + w
            for cp in waits:
                cp.wait_recv()
        for cp in started:
            cp.wait_send()

    sems = pltpu.SemaphoreType.DMA((n_sems,))
    held = [jax.ShapeDtypeStruct(a.shape, a.dtype) for a in comm.bufs.values()] + list(comm.fresh.values())
    res = pl.pallas_call(
        body, name=name, in_specs=[_ANY] * (len(r_names) + len(b_names)), out_specs=[_ANY] * len(held),
        out_shape=held, input_output_aliases={len(r_names) + i: i for i in range(len(b_names))},
        scratch_shapes=[sems, sems],
    )(*comm.reads.values(), *comm.bufs.values())
    return dict(zip(b_names + f_names, res))


_HBM = pl.BlockSpec(memory_space=pltpu.HBM)
_SEM = pl.BlockSpec(memory_space=pltpu.SEMAPHORE)
_EFFECT = pltpu.SideEffectType.DATAFLOW_SIDE_EFFECTING


def _start_exchanges(name, groups):
    names = [list(arrays) for _, arrays in groups]
    first = [sum(len(ns) for ns in names[:g]) for g in range(len(groups))]
    n, ng = sum(len(ns) for ns in names), len(groups)

    def body(*refs):
        for g, (jobs, _) in enumerate(groups):
            hbm = dict(zip(names[g], refs[first[g]:first[g] + len(names[g])]))
            for cp in _plan_all(jobs, hbm, refs[n + 2 * g], refs[n + 2 * g + 1])[0]:
                cp.start()
        refs[-1][...] = jnp.zeros_like(refs[-1])

    given = [pltpu.with_memory_space_constraint(
        a if isinstance(a, jax.Array) else lax.empty(a.shape, a.dtype), pltpu.HBM)
        for _, arrays in groups for a in arrays.values()]
    sems = [pltpu.SemaphoreType.DMA((sum(j.n_sems for j in jobs),)) for jobs, _ in groups for _ in range(2)]
    res = pl.pallas_call(
        body, name=name,
        out_shape=(*sems, *[pltpu.HBM(a.shape, a.dtype) for a in given], jax.ShapeDtypeStruct((8, 128), F32)),
        in_specs=[_HBM] * n, out_specs=(*[_SEM] * (2 * ng), *[_HBM] * n, pl.BlockSpec(memory_space=pltpu.VMEM)),
        input_output_aliases={i: 2 * ng + i for i in range(n)},
        compiler_params=pltpu.CompilerParams(has_side_effects=_EFFECT),
    )(*given)
    held = res[2 * ng:2 * ng + n]
    states = [(names[g], groups[g][0], res[2 * g], res[2 * g + 1], held[first[g]:first[g] + len(names[g])])
              for g in range(ng)]
    return states, res[-1]


def _start_exchange(name, jobs, arrays):
    states, token = _start_exchanges(name, [(jobs, arrays)])
    return states[0], token


def _finish_exchange(name, state, after):
    names, jobs, send_sem, recv_sem, held = state
    n = len(names)

    def body(*refs):
        hbm = dict(zip(names, refs[:n]))
        send, recv = refs[n:n + 2]
        starts, waits = _plan_all(jobs, hbm, send, recv)
        for cp in waits:
            cp.wait_recv()
        for cp in starts:
            cp.wait_send()

    res = pl.pallas_call(
        body, name=name, out_shape=tuple(pltpu.HBM(a.shape, a.dtype) for a in held),
        in_specs=[_HBM] * n + [_SEM, _SEM, _ANY], out_specs=tuple([_HBM] * n),
        input_output_aliases={i: i for i in range(n)},
        compiler_params=pltpu.CompilerParams(has_side_effects=_EFFECT),
    )(*held, send_sem, recv_sem, after)
    return dict(zip(names, res))


def _relay_exchange(name, state, jobs, after):
    names, arrived, send_sem, recv_sem, held = state
    n = len(names)

    def body(*refs):
        hbm = dict(zip(names, refs[:n]))
        starts, waits = _plan_all(arrived, hbm, refs[n], refs[n + 1])
        for cp in waits:
            cp.wait_recv()
        for cp in starts:
            cp.wait_send()
        for cp in _plan_all(jobs, hbm, refs[n + 3], refs[n + 4])[0]:
            cp.start()
        refs[-1][...] = jnp.zeros_like(refs[-1])

    sems = pltpu.SemaphoreType.DMA((sum(j.n_sems for j in jobs),))
    res = pl.pallas_call(
        body, name=name,
        out_shape=(sems, sems, *[pltpu.HBM(a.shape, a.dtype) for a in held], jax.ShapeDtypeStruct((8, 128), F32)),
        in_specs=[_HBM] * n + [_SEM, _SEM, _ANY],
        out_specs=(_SEM, _SEM, *[_HBM] * n, pl.BlockSpec(memory_space=pltpu.VMEM)),
        input_output_aliases={i: 2 + i for i in range(n)},
        compiler_params=pltpu.CompilerParams(has_side_effects=_EFFECT),
    )(*held, send_sem, recv_sem, after)
    return (names, jobs, res[0], res[1], res[2:2 + n]), res[-1]


def _row_tile(rows, bytes_per_row):
    best = 16
    for t in range(16, rows + 1, 16):
        if rows % t == 0 and t * bytes_per_row <= 9 * 1024 * 1024:
            best = t
    return best


def _rowwise(fn, ins, out_dtypes, name, after=None):
    rows, cols = ins[0].shape
    per_row = sum(cols * a.dtype.itemsize for a in ins) + sum(cols * jnp.dtype(d).itemsize for d in out_dtypes)
    tr = _row_tile(rows, per_row)
    n_in = len(ins)

    def body(*refs):
        outs = fn(*[r[...] for r in refs[:n_in]])
        for o_ref, o in zip(refs[-len(out_dtypes):], outs):
            o_ref[...] = o.astype(o_ref.dtype)

    tile = pl.BlockSpec((tr, cols), lambda i: (i, 0))
    behind = [] if after is None else [after]
    return pl.pallas_call(
        body, name=name, grid=(rows // tr,),
        in_specs=[tile] * n_in + [pl.BlockSpec((8, 128), lambda i: (0, 0))] * len(behind),
        out_specs=[tile] * len(out_dtypes),
        out_shape=[jax.ShapeDtypeStruct((rows, cols), d) for d in out_dtypes],
        compiler_params=_params("parallel"),
    )(*ins, *behind)


def _tiled(fn, name, grid, pos, ins, outs):
    n_in = len(ins)

    def body(pos_ref, *refs):
        res = fn(*[r[...] for r in refs[:n_in]])
        for o_ref, o in zip(refs[n_in:], res):
            o_ref[...] = o.astype(o_ref.dtype)

    return pl.pallas_call(
        body, name=name,
        grid_spec=pltpu.PrefetchScalarGridSpec(
            num_scalar_prefetch=1, grid=grid,
            in_specs=[pl.BlockSpec(bs, im) for _, bs, im in ins],
            out_specs=[pl.BlockSpec(bs, im) for _, _, bs, im in outs]),
        out_shape=[jax.ShapeDtypeStruct(s, d) for s, d, _, _ in outs],
        compiler_params=_params("parallel"),
    )(pos, *[a for a, _, _ in ins])


def _adamw(w, g, m, v):
    m = ADAM_B1 * m + (1.0 - ADAM_B1) * g
    v = ADAM_B2 * v + (1.0 - ADAM_B2) * (g * g)
    m_hat = m / (1.0 - ADAM_B1 ** ADAM_STEP)
    v_hat = v / (1.0 - ADAM_B2 ** ADAM_STEP)
    return -ADAM_LR * (m_hat / (jnp.sqrt(v_hat) + ADAM_EPS) + ADAM_WD * w), m, v


def _adamw_small(pos, own, slots, params):
    n = len(params)

    def body(pos_ref, own_ref, slots_ref, *refs):
        ins, outs, total_ref = refs[:3 * n], refs[3 * n:-1], refs[-1]
        chip = pos_ref[0]
        idx = 2 * chip + pos_ref[1]
        term = lambda q: jnp.where(idx == q, own_ref[...], slots_ref[q])
        acc = term(0)
        for q in range(1, N_DEV):
            acc = acc + term(q)
        total_ref[...] = acc
        outs[0][...] = total_ref[0:1, :]
        for k, (w, _, _, row) in enumerate(params):
            width = min(w.shape[-1], 128)
            for t in range(w.shape[0]):
                for j in range(w.shape[-1] // width):
                    lanes = slice(j * width, (j + 1) * width)
                    at = (slice(t, t + 1), lanes) if w.ndim == 2 else (t, slice(None), lanes)
                    g = total_ref[pl.ds(row(t, j, chip), 1), :][:, :width]
                    new = _adamw(ins[3 * k][at], g, ins[3 * k + 1][at], ins[3 * k + 2][at])
                    for o_ref, o in zip(outs[1 + 4 * k:5 + 4 * k], (g, *new)):
                        o_ref[at] = o

    vmem = pl.BlockSpec(memory_space=pltpu.VMEM)
    return pl.pallas_call(
        body, name="adamw_small",
        in_specs=[pl.BlockSpec(memory_space=pltpu.SMEM)] + [vmem] * (2 + 3 * n),
        out_shape=[jax.ShapeDtypeStruct((1, 128), F32)]
        + [jax.ShapeDtypeStruct(p[0].shape, F32) for p in params for _ in range(4)],
        scratch_shapes=[pltpu.VMEM(own.shape, F32)],
    )(pos, own, slots, *[a for p in params for a in p[:3]])


class _Layout:
    def __init__(self, rows, cols, stacked):
        self.rows, self.cols, self.stacked = rows, cols, stacked

    def whole(self, rows=None):
        r = self.rows if rows is None else rows
        return (N_CHIPS, r, self.cols) if self.stacked else (r, N_CHIPS * self.cols)

    def part_rows(self, h, q=0, nq=1):
        n = self.rows // 2 // nq
        return pl.ds(pl.multiple_of(h * (self.rows // 2) + q * n, 16), n)

    def half_rows(self, h):
        return self.part_rows(h)

    def block(self, ref, p, rows=slice(None)):
        if self.stacked:
            return ref.at[p, rows, :]
        return ref.at[rows, pl.ds(pl.multiple_of(p * self.cols, 128), self.cols)]

    def all_chips(self, ref, rows):
        return ref.at[:, rows, :] if self.stacked else ref.at[rows, :]


BIG = (
    _Layout(IN_SHARD, D_MODEL, True),
    _Layout(ATTN_W + CONV_W, D_MODEL // N_CHIPS, False),
    _Layout(D_MODEL // N_CHIPS, D_MODEL, True),
    _Layout(D_MODEL, FF2 // N_CHIPS, False),
    _Layout(D_FF // N_CHIPS, D_MODEL, True),
)
N_BIG = len(BIG)
_ANY = pl.BlockSpec(memory_space=pl.ANY)


def _position():
    x, y, c = lax.axis_index("x"), lax.axis_index("y"), lax.axis_index("c")
    return x, y, c, 2 * x + y


def _core_of_chip(p, c):
    return (p >> 1, p & 1, c)


def _place_cast(shard, lay, pos, name, after=None):
    rows, cols = shard.shape
    tr = _row_tile(rows, cols * 6)
    if lay.stacked:
        out = (lay.whole(), BF16, (None, tr, cols), lambda i, pos: (pos[0], i, 0))
    else:
        out = (lay.whole(), BF16, (tr, cols), lambda i, pos: (i, pos[0]))
    ins = [(shard, (tr, cols), lambda i, pos: (i, 0))]
    if after is not None:
        ins.append((after, (8, 128), lambda i, pos: (0, 0)))
    return _tiled(lambda a, *_: (a,), name, (rows // tr,), pos, ins, [out])[0]


def _place_cast_pair(top, bottom, lay, pos, name, after=None):
    rows, cols = top.shape
    ins = [(top, (rows, cols), lambda i, pos: (0, 0)), (bottom, (rows, cols), lambda i, pos: (0, 0))]
    if after is not None:
        ins.append((after, (8, 128), lambda i, pos: (0, 0)))
    return _tiled(lambda a, b, *_: (jnp.concatenate([a, b], axis=0),), name, (1,), pos, ins,
                  [(lay.whole(), BF16, (2 * rows, cols), lambda i, pos: (0, pos[0]))])[0]


def _adamw_pair(top, bottom, g, after=None):
    rows = top[0].shape[0]

    def body(*refs):
        (wa, ma, va, wb, mb, vb, g_ref), outs = refs[:7], refs[-8:]
        for (w, m, v), gg, o in (((wa, ma, va), g_ref[:rows], outs[:4]), ((wb, mb, vb), g_ref[rows:], outs[4:])):
            for o_ref, val in zip(o, (gg, *_adamw(w[...], gg, m[...], v[...]))):
                o_ref[...] = val

    behind = [] if after is None else [after]
    res = pl.pallas_call(
        body, name="adamw_w_br", out_shape=[jax.ShapeDtypeStruct(top[0].shape, F32)] * 8,
        in_specs=[pl.BlockSpec(memory_space=pltpu.VMEM)] * 7 + [_ANY] * len(behind),
    )(*top, *bottom, g, *behind)
    return res[:4], res[4:]


def _remote(src, dst, send, recv, k, device):
    return pltpu.make_async_remote_copy(src_ref=src, dst_ref=dst, send_sem=send.at[k], recv_sem=recv.at[k],
                                        device_id=device, device_id_type=MESH)


def _arrival(dst, send, recv, k, me):
    return _remote(dst, dst, send, recv, k, me)


def _gather_ici(lay, name, q=0, nq=1):
    def plan(hbm, pos, send, recv, base):
        x, y, c, me = pos
        rows = lay.part_rows(c, q, nq)
        mine = lay.block(hbm[name], me, rows)
        starts = [_remote(mine, mine, send, recv, base + d - 1, _core_of_chip(me ^ d, c)) for d in (1, 2, 3)]
        waits = [_arrival(lay.block(hbm[name], me ^ d, rows), send, recv, base + d - 1, (x, y, c)) for d in (1, 2, 3)]
        return starts, waits
    return _Job(3, plan)


def _gather_near(lay, name):
    def plan(hbm, pos, send, recv, base):
        x, y, c, me = pos
        rows = lay.part_rows(c)
        mine = lay.block(hbm[name], me, rows)
        starts = [_remote(mine, mine, send, recv, base + d - 1, _core_of_chip(me ^ d, c)) for d in (1, 2)]
        waits = [_arrival(lay.block(hbm[name], me ^ d, rows), send, recv, base + d - 1, (x, y, c)) for d in (1, 2)]
        return starts, waits
    return _Job(2, plan)


def _gather_far(lay, name):
    def plan(hbm, pos, send, recv, base):
        x, y, c, me = pos
        starts, waits = [], []
        for q, d in ((0, 1), (1, 2)):
            got = lay.block(hbm[name], me ^ (3 - d), lay.part_rows(c, q, 2))
            starts.append(_remote(got, got, send, recv, base + q, _core_of_chip(me ^ d, c)))
            waits.append(_arrival(lay.block(hbm[name], me ^ 3, lay.part_rows(c, q, 2)), send, recv, base + q, (x, y, c)))
        return starts, waits
    return _Job(2, plan)


def _gather_d2d(lay, name, q=0, nq=1, chips=(1, 2, 3)):
    def plan(hbm, pos, send, recv, base):
        x, y, c, me = pos
        starts, waits = [], []
        for k, d in enumerate(chips):
            got = lay.block(hbm[name], me ^ d, lay.part_rows(c, q, nq))
            starts.append(_remote(got, got, send, recv, base + k, (x, y, 1 - c)))
            waits.append(_arrival(lay.block(hbm[name], me ^ d, lay.part_rows(1 - c, q, nq)), send, recv, base + k,
                                  (x, y, c)))
        return starts, waits
    return _Job(len(chips), plan)


def _rs_pair(lay, grad, theirs):
    def plan(hbm, pos, send, recv, base):
        x, y, c, _ = pos
        out = _remote(lay.all_chips(hbm[grad], lay.half_rows(1 - c)), hbm[theirs], send, recv, base, (x, y, 1 - c))
        return [out], [_arrival(hbm[theirs], send, recv, base, (x, y, c))]
    return _Job(1, plan)


def _rs_chips(lay, sums, slots):
    def plan(hbm, pos, send, recv, base):
        x, y, c, me = pos
        starts = [_remote(lay.block(hbm[sums], me ^ d), hbm[slots].at[me], send, recv, base + d - 1,
                          _core_of_chip(me ^ d, c)) for d in (1, 2, 3)]
        waits = [_arrival(hbm[slots].at[me ^ d], send, recv, base + d - 1, (x, y, c)) for d in (1, 2, 3)]
        return starts, waits
    return _Job(3, plan)


def _rs_share(lay, shard):
    def plan(hbm, pos, send, recv, base):
        x, y, c, _ = pos
        mine = hbm[shard].at[lay.half_rows(c), :]
        other = hbm[shard].at[lay.half_rows(1 - c), :]
        return [_remote(mine, mine, send, recv, base, (x, y, 1 - c))], [_arrival(other, send, recv, base, (x, y, c))]
    return _Job(1, plan)


def _slots_shape(lay):
    return jax.ShapeDtypeStruct((N_CHIPS, lay.rows // 2, lay.cols), BF16)


def _theirs_shape(lay, dtype=BF16):
    return jax.ShapeDtypeStruct(lay.whole(lay.rows // 2), dtype)


def _pair_sum(grad, theirs, lay, pos, name):
    half = lay.rows // 2
    add = lambda a, b: (a.astype(F32) + b.astype(F32),)
    if lay.stacked:
        tr = _row_tile(half, lay.cols * 6)
        nt = half // tr
        flat = lambda a: a.reshape(-1, lay.cols)
        mine = lambda t, pos: ((t // nt) * (2 * nt) + pos[1] * nt + t % nt, 0)
        grid, blk = (N_CHIPS * nt,), (tr, lay.cols)
        grad, theirs = flat(grad), flat(theirs)
    else:
        tr = _row_tile(half, N_CHIPS * lay.cols * 6)
        nt = half // tr
        mine = lambda t, pos: (pos[1] * nt + t, 0)
        grid, blk = (nt,), (tr, N_CHIPS * lay.cols)
    same = lambda t, pos: (t, 0)
    out = _tiled(add, name, grid, pos, [(grad, blk, mine), (theirs, blk, same)], [(theirs.shape, BF16, blk, same)])[0]
    return out.reshape(lay.whole(half))


def _chip_sums(items, pos, name, after=None):
    ins, outs = [], []
    for sums, slots, lay in items:
        half = lay.rows // 2
        blk3 = (None, half, lay.cols)
        if lay.stacked:
            own = (sums, blk3, lambda i, pos: (pos[0], 0, 0))
        else:
            own = (sums, (half, lay.cols), lambda i, pos: (0, pos[0]))
        ins += [own] + [(slots, blk3, functools.partial(lambda d, i, pos: (pos[0] ^ d, 0, 0), d)) for d in (1, 2, 3)]
        outs.append(((lay.rows, lay.cols), F32, (half, lay.cols), lambda i, pos: (pos[1], 0)))

    def add(*vals):
        v = [a.astype(F32) for a in vals[:4 * len(items)]]
        return tuple(((v[4 * k] + v[4 * k + 1]) + v[4 * k + 2]) + v[4 * k + 3] for k in range(len(items)))

    if after is not None:
        ins.append((after, (8, 128), lambda i, pos: (0, 0)))
    return _tiled(add, name, (1,), pos, ins, outs)


N_DEV = 8


def _to_all(src, slots):
    def plan(hbm, pos, send, recv, base):
        x, y, c, _ = pos
        idx = 4 * x + 2 * y + c
        starts = [_remote(hbm[src], hbm[slots].at[idx], send, recv, base + k - 1,
                          (x ^ (k >> 2), y ^ ((k >> 1) & 1), c ^ (k & 1))) for k in range(1, N_DEV)]
        waits = [_arrival(hbm[slots].at[idx ^ k], send, recv, base + k - 1, (x, y, c)) for k in range(1, N_DEV)]
        return starts, waits
    return _Job(N_DEV - 1, plan)


def _taps_gather(name, cols):
    def plan(hbm, pos, send, recv, base):
        x, y, c, me = pos
        block = lambda p: hbm[name].at[:, pl.ds(pl.multiple_of(p * cols, 128), cols)]
        starts = [_remote(block(me), block(me), send, recv, base + d - 1, _core_of_chip(me ^ d, c)) for d in (1, 2, 3)]
        waits = [_arrival(block(me ^ d), send, recv, base + d - 1, (x, y, c)) for d in (1, 2, 3)]
        return starts, waits
    return _Job(3, plan)


def _pack_rows(parts):
    padded = [jnp.pad(a, ((0, -a.shape[0] % 8), (0, 0))) for a in parts]
    starts = [sum(p.shape[0] for p in padded[:k]) for k in range(len(padded))]
    return jnp.concatenate(padded, axis=0), starts


def kernel(x, mix_norm, w_in, b_in, sinks, conv_w, w_attn_branch, w_conv_branch, w_out, ffn_norm, w_up, ffn_conv_w, w_down, final_norm, loss_target, m_mix_norm, m_w_in, m_b_in, m_sinks, m_conv_w, m_w_attn_branch, m_w_conv_branch, m_w_out, m_ffn_norm, m_w_up, m_ffn_conv_w, m_w_down, m_final_norm, v_mix_norm, v_w_in, v_b_in, v_sinks, v_conv_w, v_w_attn_branch, v_w_conv_branch, v_w_out, v_ffn_norm, v_w_up, v_ffn_conv_w, v_w_down, v_final_norm):
    me = 2 * lax.axis_index("x") + lax.axis_index("y")
    names = ("w_in", "w_br", "w_out", "w_up", "w_down")
    w_of = dict(w_in=w_in[0].T, w_out=w_out[0], w_up=w_up[0], w_down=w_down[0])
    m_of = dict(w_in=m_w_in[0].T, w_out=m_w_out[0], w_up=m_w_up[0], w_down=m_w_down[0])
    v_of = dict(w_in=v_w_in[0].T, w_out=v_w_out[0], w_up=v_w_up[0], w_down=v_w_down[0])
    ab = (w_attn_branch[0], m_w_attn_branch[0], v_w_attn_branch[0])
    cb = (w_conv_branch[0], m_w_conv_branch[0], v_w_conv_branch[0])

    pos = jnp.stack([me, lax.axis_index("c")]).astype(jnp.int32)

    lay = dict(zip(names, BIG))
    xs, target, sk = x[0], loss_target[0], sinks[0]
    s = xs.shape[0]
    tm, tm2, bk, bk2 = min(256, s), min(512, s), min(1024, s), min(2048, s)

    placed = {"w_in": _place_cast(w_of["w_in"], lay["w_in"], pos, "cast_w_in")}
    fly_in, started = _start_exchange("gather_in_start", [_gather_near(lay["w_in"], "w_in")], {"w_in": placed["w_in"]})
    whole = lambda a: jnp.tile(jnp.pad(a[0], ((0, 5), (0, 0))), (1, N_CHIPS)) + started[0:1, 0:1]
    taps_flight, started = _start_exchange(
        "taps_start", [_taps_gather("conv", CONV_W // N_CHIPS), _taps_gather("ffn", FF2 // N_CHIPS)],
        {"conv": whole(conv_w), "ffn": whole(ffn_conv_w)})
    placed["w_br"] = _place_cast_pair(ab[0], cb[0], lay["w_br"], pos, "cast_w_br", after=started)
    for n in names[2:]:
        placed[n] = _place_cast(w_of[n], lay[n], pos, "cast_" + n, after=started)
    trio = ("w_br", "w_out")
    fly_in, started = _relay_exchange(
        "gather_in_relay", fly_in, [_gather_far(lay["w_in"], "w_in"), _gather_d2d(lay["w_in"], "w_in", chips=(1, 2))],
        after=placed["w_down"])
    (fly_trio, fly_up, fly_down), started = _start_exchanges("gather_rest_start", [
        ([_gather_ici(lay[n], n) for n in ws], {**{n: placed[n] for n in ws}, **behind})
        for ws, behind in ((trio, {"behind": started}), (("w_up",), {}), (("w_down",), {}))])

    got = _finish_exchange("gather_in_wait", fly_in, after=started)
    w_in_full = _exchange("gather_in_d2d", [[_gather_d2d(lay["w_in"], "w_in", chips=(3,))]],
                          bufs=got)["w_in"].reshape(IN_W, D_MODEL)
    xn, qkv, c3, gates = _inproj_fwd(xs, mix_norm, w_in_full, b_in, tm2)
    got = _finish_exchange("gather_trio_wait", fly_trio, after=qkv)
    k2 = _Carry([_gather_d2d(lay[n], n) for n in trio], bufs={n: got[n] for n in trio})
    attn = _attn_fwd(qkv, sk, comm=k2)
    w_br = k2.out["w_br"]
    w_out_full = k2.out["w_out"].reshape(D_MODEL, D_MODEL)
    k3 = _Carry([_gather_d2d(lay["w_up"], "w_up")], bufs=_finish_exchange("gather_up_wait", fly_up, after=attn))
    taps = _finish_exchange("taps_wait", taps_flight, after=attn)
    conv_full, ffn_cw_full = taps["conv"], taps["ffn"]
    conv, a, cv, merged, h1, hn = _mix_fwd(xs, attn, c3, gates, conv_full, w_br, w_out_full, ffn_norm, tm2, comm=k3)
    w_up_full = k3.out["w_up"]
    w_down_full = _exchange("gather_down_d2d", [[_gather_d2d(lay["w_down"], "w_down")]],
                            bufs=_finish_exchange("gather_down_wait", fly_down, after=hn))["w_down"].reshape(D_FF, D_MODEL)
    u, up, act, dh2, loss_part, g_fn = _ffn_fwd_loss(hn, h1, w_up_full, ffn_cw_full, w_down_full,
                                                     final_norm[None, :], target, tm)

    grads, sums, slots = {}, {}, {}

    def pair(*ws):
        return _Carry([_rs_pair(lay[n], "g_" + n, "t_" + n) for n in ws], reads={"g_" + n: grads[n] for n in ws},
                      fresh={"t_" + n: _theirs_shape(lay[n], grads[n].dtype) for n in ws})

    def chips(*ws, also=None):
        k = _Carry([_rs_chips(lay[n], "s_" + n, "r_" + n) for n in ws], reads={"s_" + n: sums[n] for n in ws},
                   fresh={"r_" + n: _slots_shape(lay[n]) for n in ws})
        if also is not None:
            k = _Carry(k.jobs + also.jobs, {**k.reads, **also.reads}, None, {**k.fresh, **also.fresh})
        return k

    def pair_sums(k, *ws):
        for n in ws:
            sums[n] = _pair_sum(grads[n], k.out["t_" + n], lay[n], pos, "pair_sum_" + n)

    def take_slots(k, *ws):
        for n in ws:
            slots[n] = k.out["r_" + n]

    du, dh1, g_fcw, g_g2 = _ffn_bwd(dh2, u, up, h1, w_up_full, ffn_cw_full, w_down_full, ffn_norm, tm)
    grads["w_down"] = _wgrad(act, dh2, D_FF // 2, D_MODEL, bk2, "wgrad_down").reshape(lay["w_down"].whole())
    k4 = pair("w_down")
    grads["w_up"] = _wgrad(hn, du, D_MODEL, FF2 // 4, bk2, "wgrad_up", comm=k4)
    pair_sums(k4, "w_down")
    k5 = chips("w_down", also=pair("w_up"))
    dattn, dc3, dgt, g_cw, grads["w_br"], gw_out = _mix_bwd(
        dh1, gates, a, cv, c3, attn, conv, merged, conv_full, w_br, w_out_full, tm2, comm=k5)
    grads["w_out"] = gw_out.reshape(lay["w_out"].whole())
    take_slots(k5, "w_down")
    pair_sums(k5, "w_up")
    up_flight, started = _start_exchange("rs_chips_up_start", [_rs_chips(lay["w_up"], "s", "r")],
                                         {"s": sums["w_up"], "r": _slots_shape(lay["w_up"])})
    k6 = pair(*trio)
    k6.reads["after"] = started
    dq, dk_even, dk_odd, dv_even, dv_odd, g_sk = _attn_bwd(qkv, sk, attn, dattn, comm=k6)
    pair_sums(k6, *trio)
    trio_flight, started = _start_exchange(
        "rs_chips_trio_start", [_rs_chips(lay[n], "s_" + n, "r_" + n) for n in trio],
        {**{"s_" + n: sums[n] for n in trio}, **{"r_" + n: _slots_shape(lay[n]) for n in trio}})
    behind = mix_norm + jnp.tile(started[0:1], (1, D_MODEL // 128))
    grad_x, gw_in, g_b, g_g1 = _inproj_bwd(dq, (dk_even, dk_odd), (dv_even, dv_odd), dc3, dgt, w_in_full, xs, xn,
                                           dh1, behind)
    grads["w_in"] = gw_in.reshape(lay["w_in"].whole())

    in_flight, started = _start_exchange("rs_pair_in_start", [_rs_pair(lay["w_in"], "g", "t")],
                                         {"g": grads["w_in"], "t": _theirs_shape(lay["w_in"])})
    parts = [loss_part, g_g1, g_b, jnp.pad(g_sk[:, 0], (0, 120))[None, :], g_cw, g_g2, g_fcw, g_fn]
    packed, at = _pack_rows([p.reshape(-1, 128) for p in parts])
    small_flight, started = _start_exchange("small_start", [_to_all("v", "slots")],
                                            {"v": packed + started[0:1], "slots": jnp.zeros((N_DEV, *packed.shape), F32)})
    landed = _finish_exchange("rs_chips_up_wait", up_flight, after=started)
    halves = dict(zip(("w_down", "w_up"), _chip_sums(
        [(sums["w_down"], slots["w_down"], lay["w_down"]), (landed["s"], landed["r"], lay["w_up"])], pos,
        "chip_sum_w_down_up")))
    landed = _finish_exchange("rs_pair_in_wait", in_flight, after=halves["w_up"])
    sums["w_in"] = _pair_sum(landed["g"], landed["t"], lay["w_in"], pos, "pair_sum_w_in")
    (in_flight, down_flight, up_flight), started = _start_exchanges("rs_chips_in_start", [
        ([_rs_chips(lay["w_in"], "s", "r")], {"s": sums["w_in"], "r": _slots_shape(lay["w_in"])}),
        ([_rs_share(lay["w_down"], "w_down")], {"w_down": halves["w_down"]}),
        ([_rs_share(lay["w_up"], "w_up")], {"w_up": halves["w_up"]})])
    landed = _finish_exchange("rs_chips_trio_wait", trio_flight, after=started)
    halves.update(zip(trio, _chip_sums([(landed["s_" + n], landed["r_" + n], lay[n]) for n in trio], pos,
                                       "chip_sum_w_br_out")))
    shared = _exchange("share_halves", [[_rs_share(lay[n], n) for n in trio]], bufs={n: halves[n] for n in trio})
    shared["w_down"] = _finish_exchange("share_down_wait", down_flight, after=shared[trio[-1]])["w_down"]
    shared["w_up"] = _finish_exchange("share_up_wait", up_flight, after=shared["w_down"])["w_up"]

    def adam(n, g, after=None):
        return _rowwise(lambda w, g, m, v: (g, *_adamw(w, g, m, v)), [w_of[n], g, m_of[n], v_of[n]], [F32] * 4,
                        "adamw_" + n, after=after)

    new_of, last = {}, None
    for n in ("w_down", "w_up", "w_out"):
        new_of[n] = adam(n, shared[n], last)
        last = new_of[n][1]
    new_of["w_ab"], new_of["w_cb"] = _adamw_pair(ab, cb, shared["w_br"], after=last)
    last = new_of["w_cb"][1]

    arrived = _finish_exchange("small_wait", small_flight, after=last)
    flat = lambda k: lambda t, j, chip: at[k] + j
    mine = lambda k, per_tap: lambda t, j, chip: at[k] + per_tap * t + (per_tap // N_CHIPS) * chip + j
    rows = lambda a: a.reshape(a.shape[1], 1, a.shape[2])
    small_p = [
        (mix_norm, m_mix_norm, v_mix_norm, flat(1)), (b_in, m_b_in, v_b_in, flat(2)), (sinks, m_sinks, v_sinks, flat(3)),
        (rows(conv_w), rows(m_conv_w), rows(v_conv_w), mine(4, CONV_W // 128)),
        (ffn_norm, m_ffn_norm, v_ffn_norm, flat(5)),
        (rows(ffn_conv_w), rows(m_ffn_conv_w), rows(v_ffn_conv_w), mine(6, FF2 // 128)),
        (final_norm[None, :], m_final_norm[None, :], v_final_norm[None, :], flat(7))]
    small_new = _adamw_small(pos, arrived["v"], arrived["slots"], small_p)
    loss = small_new[0][0, 0]
    small_g = small_new[1::4]
    small_new = [small_new[4 * k + 2:4 * k + 5] for k in range(len(small_p))]

    landed = _finish_exchange("rs_chips_in_wait", in_flight, after=small_new[0][0])
    half_in = _chip_sums([(landed["s"], landed["r"], lay["w_in"])], pos, "chip_sum_w_in")[0]
    shared["w_in"] = _exchange("share_in", [[_rs_share(lay["w_in"], "w_in")]], bufs={"w_in": half_in})["w_in"]
    new_of["w_in"] = [a.T for a in adam("w_in", shared["w_in"])]
    big = ("w_in", "w_ab", "w_cb", "w_out", "w_up", "w_down")
    big_g = [new_of[n][0] for n in big]
    big_new = [new_of[n][1:] for n in big]

    order = [("s", 0), ("b", 0), ("s", 1), ("s", 2), ("s", 3), ("b", 1), ("b", 2), ("b", 3), ("s", 4), ("b", 4),
             ("s", 5), ("b", 5), ("s", 6)]
    shapes = [mix_norm.shape, w_in.shape, b_in.shape, sinks.shape, conv_w.shape, w_attn_branch.shape,
              w_conv_branch.shape, w_out.shape, ffn_norm.shape, w_up.shape, ffn_conv_w.shape, w_down.shape,
              final_norm.shape]
    out_g = [(small_g[k] if kind == "s" else big_g[k]).reshape(shp) for (kind, k), shp in zip(order, shapes)]
    news = [[(small_new[k][j] if kind == "s" else big_new[k][j]).reshape(shp) for (kind, k), shp in zip(order, shapes)]
            for j in range(3)]
    return (loss, grad_x[None], *out_g, *news[0], *news[1], *news[2])
```

```python
import functools

import jax
import jax.numpy as jnp
from jax import lax
from jax.experimental import pallas as pl
from jax.experimental.pallas import tpu as pltpu

F32 = jnp.float32
BF16 = jnp.bfloat16

D_MODEL = 1024
HEAD_DIM = 64
N_HEADS = 8
N_KV_HEADS = 2
GROUP = N_HEADS // N_KV_HEADS
BLOCK = 128
ATTN_SCALE = HEAD_DIM ** -0.5
ATTN_W = N_HEADS * HEAD_DIM
KV_W = N_KV_HEADS * HEAD_DIM
CONV_W = 512
QKV_W = ATTN_W + 2 * KV_W
C3_W = 3 * CONV_W
GATES_W = 2 * D_MODEL
IN_W = QKV_W + C3_W + GATES_W
D_FF = 2816
FF2 = 2 * D_FF
NORM_EPS = 1e-5
N_CHIPS = 4
IN_SHARD = IN_W // N_CHIPS
NEG = -1e30

ADAM_LR = 0.001
ADAM_B1 = 0.9
ADAM_B2 = 0.999
ADAM_EPS = 1e-08
ADAM_WD = 0.01
ADAM_STEP = 10

VMEM_LIMIT = 56 * 1024 * 1024
MESH = pl.DeviceIdType.MESH

NT = (((1,), (1,)), ((), ()))
TN = (((0,), (0,)), ((), ()))


def _params(*sem):
    return pltpu.CompilerParams(dimension_semantics=sem, vmem_limit_bytes=VMEM_LIMIT)


def _resident(shape):
    return pl.BlockSpec(shape, lambda *_: (0,) * len(shape), pipeline_mode=pl.Buffered(1))


def _sigmoid(v):
    return 0.5 * jnp.tanh(0.5 * v) + 0.5


def _rstd(v):
    return lax.rsqrt(jnp.mean(v * v, axis=-1, keepdims=True) + NORM_EPS)


def _rms_bwd(dy, v, rstd, g):
    vhat = v * rstd
    t = dy * g
    return rstd * (t - vhat * jnp.mean(t * vhat, axis=-1, keepdims=True)), dy * vhat


def _taps(z, cw):
    return cw[2:3] * z + cw[1:2] * pltpu.roll(z, 1, 0) + cw[0:1] * pltpu.roll(z, 2, 0)


def _causal_conv(z, prev, cw):
    edge = _taps(jnp.concatenate([prev, z[0:8]], axis=0), cw)
    return jnp.concatenate([edge[8:16], _taps(z, cw)[8:]], axis=0)


def _rows_after(z, nxt):
    n = z.shape[0]
    edge = jnp.concatenate([z[n - 8:n], nxt], axis=0)
    return tuple(jnp.concatenate([pltpu.roll(z, n - k, 0)[:n - 8], pltpu.roll(edge, 16 - k, 0)[0:8]], axis=0)
                 for k in (1, 2))


def _inproj_fwd(x, g1, w_in, b_in, tm, comm=None):
    s = x.shape[0]

    def body(x_ref, g_ref, w_ref, b_ref, xn_ref, qkv_ref, c3_ref, gt_ref):
        xf = x_ref[...]
        xn = (xf * _rstd(xf) * g_ref[...]).astype(BF16)
        xn_ref[...] = xn

        proj = (lax.dot_general(xn, w_ref[...], NT, preferred_element_type=F32) + b_ref[...]).astype(BF16)
        qkv_ref[...] = proj[:, :QKV_W]
        c3_ref[...] = proj[:, QKV_W:QKV_W + C3_W]
        gt_ref[...] = proj[:, QKV_W + C3_W:]

    row = lambda w: pl.BlockSpec((tm, w), lambda i: (i, 0))
    return _call(
        comm, body, name="inproj_fwd", grid=(s // tm,),
        in_specs=[row(D_MODEL), _resident((1, D_MODEL)), _resident((IN_W, D_MODEL)), _resident((1, IN_W))],
        out_specs=[row(D_MODEL), row(QKV_W), row(C3_W), row(GATES_W)],
        out_shape=[jax.ShapeDtypeStruct((s, D_MODEL), BF16), jax.ShapeDtypeStruct((s, QKV_W), BF16),
                   jax.ShapeDtypeStruct((s, C3_W), BF16), jax.ShapeDtypeStruct((s, GATES_W), BF16)],
        compiler_params=_params("parallel"),
    )(x, g1, w_in, b_in)


def _attn_bias():
    kj = jnp.arange(2 * BLOCK)[:, None]
    qi = (jnp.arange(GROUP * BLOCK) % BLOCK)[None, :]
    band = (kj > qi) & (kj <= qi + BLOCK)
    return jnp.stack([jnp.where(band & (kj >= BLOCK), 0.0, NEG), jnp.where(band, 0.0, NEG)]).astype(F32)


def _attn_bias_specs():
    shape = (None, 2 * BLOCK, GROUP * BLOCK)
    return pl.BlockSpec(shape, lambda i: (jnp.minimum(i, 1), 0, 0)), pl.BlockSpec(shape, lambda i: (1, 0, 0))


def _sink_row(sk_ref, h):
    lane = lax.broadcasted_iota(jnp.int32, (1, GROUP * BLOCK), 1)
    row = jnp.full((1, GROUP * BLOCK), sk_ref[h * GROUP], F32)
    for g in range(1, GROUP):
        row = jnp.where(lane >= g * BLOCK, sk_ref[h * GROUP + g], row)
    return row


def _stack_heads(t, h):
    return jnp.concatenate(
        [t[:, (h * GROUP + g) * HEAD_DIM:(h * GROUP + g + 1) * HEAD_DIM] for g in range(GROUP)], axis=0)


def _unstack_heads(per_kv):
    return jnp.concatenate(
        [t[g * BLOCK:(g + 1) * BLOCK] for t in per_kv for g in range(GROUP)], axis=1)


def _block_specs(n, steps):
    cur = lambda i: jnp.minimum(i, steps - 1)
    prev = lambda i: jnp.maximum(n * jnp.minimum(i, steps - 1) - 1, 0)
    kv = ATTN_W // KV_W
    return (pl.BlockSpec((n * BLOCK, ATTN_W), lambda i: (cur(i), 0)),
            pl.BlockSpec((BLOCK, KV_W), lambda i: (prev(i), kv)), pl.BlockSpec((n * BLOCK, KV_W), lambda i: (cur(i), kv)),
            pl.BlockSpec((BLOCK, KV_W), lambda i: (prev(i), kv + 1)),
            pl.BlockSpec((n * BLOCK, KV_W), lambda i: (cur(i), kv + 1)))


def _attn_fwd(qkv, sinks, comm=None):
    s = qkv.shape[0]
    n = min(8, s // BLOCK)
    steps = s // (n * BLOCK)

    def body(sk_ref, bias0_ref, bias1_ref, q_ref, kp_ref, kc_ref, vp_ref, vc_ref, o_ref):
        kc, vc = kc_ref[...], vc_ref[...]
        for b in range(n):
            rows, before = slice(b * BLOCK, (b + 1) * BLOCK), slice((b - 1) * BLOCK, b * BLOCK)
            kp, vp = (kp_ref[...], vp_ref[...]) if b == 0 else (kc[before], vc[before])
            q, bias = q_ref[rows, :], (bias0_ref if b == 0 else bias1_ref)[...]
            outs = []
            for h in range(N_KV_HEADS):
                hs = slice(h * HEAD_DIM, (h + 1) * HEAD_DIM)
                k2 = jnp.concatenate([kp[:, hs], kc[rows, hs]], axis=0)
                v2 = jnp.concatenate([vp[:, hs], vc[rows, hs]], axis=0)
                sc = lax.dot_general(k2, _stack_heads(q, h), NT, preferred_element_type=F32) * ATTN_SCALE + bias
                sink = _sink_row(sk_ref, h)
                m = jnp.maximum(jnp.max(sc, axis=0, keepdims=True), sink)
                p = jnp.exp(sc - m)
                den = jnp.sum(p, axis=0, keepdims=True) + jnp.exp(sink - m)
                out = lax.dot_general(v2, p.astype(BF16), TN, preferred_element_type=F32) / den
                outs.append(out.T)
            o_ref[rows, :] = _unstack_heads(outs).astype(BF16)

    return _call(
        comm, body, name="attn_fwd", grid=(steps,),
        in_specs=[pl.BlockSpec(memory_space=pltpu.SMEM), *_attn_bias_specs(), *_block_specs(n, steps)],
        out_specs=pl.BlockSpec((n * BLOCK, ATTN_W), lambda i: (i, 0)),
        out_shape=jax.ShapeDtypeStruct((s, ATTN_W), BF16),
        compiler_params=_params("parallel"),
    )(sinks, _attn_bias(), _attn_bias(), qkv, qkv, qkv, qkv, qkv)


def _mix_fwd(x, attn, c3, gates, conv_w, w_br, w_out, g2, tm, comm=None):
    s = x.shape[0]

    def body(x_ref, at_ref, c3_ref, gt_ref, cw_ref, wbr_ref, wo_ref, g_ref,
             conv_ref, a_ref, cv_ref, mg_ref, h1_ref, hn_ref, carry_ref):
        @pl.when(pl.program_id(0) == 0)
        def _():
            carry_ref[...] = jnp.zeros_like(carry_ref)

        c3v = c3_ref[...].astype(F32)
        cb, cc, cx = c3v[:, :CONV_W], c3v[:, CONV_W:2 * CONV_W], c3v[:, 2 * CONV_W:]
        z = cc * cx
        cz = _causal_conv(z, carry_ref[...], cw_ref[...])
        carry_ref[...] = z[tm - 8:tm]
        conv = (cb * cz).astype(BF16)
        conv_ref[...] = conv
        a = jnp.dot(at_ref[...], wbr_ref[:ATTN_W, :], preferred_element_type=F32)
        cv = jnp.dot(conv, wbr_ref[ATTN_W:, :], preferred_element_type=F32)
        a_ref[...] = a.astype(BF16)
        cv_ref[...] = cv.astype(BF16)
        gt = gt_ref[...].astype(F32)
        merged = (_sigmoid(gt[:, :D_MODEL]) * a + _sigmoid(gt[:, D_MODEL:]) * cv).astype(BF16)
        mg_ref[...] = merged
        h1 = x_ref[...] + jnp.dot(merged, wo_ref[...], preferred_element_type=F32)
        h1_ref[...] = h1
        hn_ref[...] = (h1 * _rstd(h1) * g_ref[...]).astype(BF16)

    row = lambda w: pl.BlockSpec((tm, w), lambda i: (i, 0))
    return _call(
        comm, body, name="mix_fwd", grid=(s // tm,),
        in_specs=[row(D_MODEL), row(ATTN_W), row(C3_W), row(GATES_W), _resident(conv_w.shape),
                  _resident((ATTN_W + CONV_W, D_MODEL)), _resident((D_MODEL, D_MODEL)), _resident((1, D_MODEL))],
        out_specs=[row(CONV_W), row(D_MODEL), row(D_MODEL), row(D_MODEL), row(D_MODEL), row(D_MODEL)],
        out_shape=[jax.ShapeDtypeStruct((s, CONV_W), BF16), jax.ShapeDtypeStruct((s, D_MODEL), BF16),
                   jax.ShapeDtypeStruct((s, D_MODEL), BF16), jax.ShapeDtypeStruct((s, D_MODEL), BF16),
                   jax.ShapeDtypeStruct((s, D_MODEL), F32), jax.ShapeDtypeStruct((s, D_MODEL), BF16)],
        scratch_shapes=[pltpu.VMEM((8, CONV_W), F32)],
        compiler_params=_params("arbitrary"),
    )(x, attn, c3, gates, conv_w, w_br, w_out, g2)


def _ffn_fwd_loss(hn, h1, w_up, ffn_cw, w_down, g3, target, tm):
    s = hn.shape[0]

    def body(hn_ref, h1_ref, wu_ref, cw_ref, wd_ref, g_ref, t_ref,
             u_ref, up_ref, act_ref, dh2_ref, loss_ref, gfn_ref, carry_ref):
        @pl.when(pl.program_id(0) == 0)
        def _():
            carry_ref[...] = jnp.zeros_like(carry_ref)
            loss_ref[...] = jnp.zeros_like(loss_ref)
            gfn_ref[...] = jnp.zeros_like(gfn_ref)

        u = jnp.dot(hn_ref[...], wu_ref[...], preferred_element_type=F32)
        u_ref[...] = u.astype(BF16)
        up = _causal_conv(u, carry_ref[...], cw_ref[...])
        up_ref[...] = up
        carry_ref[...] = u[tm - 8:tm]
        gate, val = up[:, :D_FF], up[:, D_FF:]
        act = (gate * _sigmoid(gate) * val).astype(BF16)
        act_ref[...] = act
        h2 = h1_ref[...] + jnp.dot(act, wd_ref[...], preferred_element_type=F32)
        rstd = _rstd(h2)
        g = g_ref[...]
        err = h2 * rstd * g - t_ref[...]
        loss_ref[...] += jnp.sum(err * err) * (0.5 / D_MODEL)
        dh2, dg = _rms_bwd(err * (1.0 / D_MODEL), h2, rstd, g)
        dh2_ref[...] = dh2
        gfn_ref[...] += jnp.sum(dg, axis=0, keepdims=True)

    row = lambda w: pl.BlockSpec((tm, w), lambda i: (i, 0))
    acc = lambda w: pl.BlockSpec((1, w), lambda i: (0, 0))
    return pl.pallas_call(
        body, name="ffn_fwd_loss", grid=(s // tm,),
        in_specs=[row(D_MODEL), row(D_MODEL), _resident((D_MODEL, FF2)), _resident(ffn_cw.shape),
                  _resident((D_FF, D_MODEL)), _resident((1, D_MODEL)), row(D_MODEL)],
        out_specs=[row(FF2), row(FF2), row(D_FF), row(D_MODEL), acc(128), acc(D_MODEL)],
        out_shape=[jax.ShapeDtypeStruct((s, FF2), BF16), jax.ShapeDtypeStruct((s, FF2), F32),
                   jax.ShapeDtypeStruct((s, D_FF), BF16),
                   jax.ShapeDtypeStruct((s, D_MODEL), F32), jax.ShapeDtypeStruct((1, 128), F32),
                   jax.ShapeDtypeStruct((1, D_MODEL), F32)],
        scratch_shapes=[pltpu.VMEM((8, FF2), F32)],
        compiler_params=_params("arbitrary"),
    )(hn, h1, w_up, ffn_cw, w_down, g3, target)


def _ffn_bwd(dh2, u, up, h1, w_up, ffn_cw, w_down, g2, tm):
    s = dh2.shape[0]
    nt = s // tm

    def body(dh2_ref, u_ref, up_ref, h1_ref, wu_ref, cw_ref, wd_ref, g_ref,
             du_ref, dh1_ref, gcw_ref, gg_ref, carry_ref):
        @pl.when(pl.program_id(0) == 0)
        def _():
            for ref in (carry_ref, gcw_ref, gg_ref):
                ref[...] = jnp.zeros_like(ref)

        dh2v = dh2_ref[...]
        dact = lax.dot_general(dh2v.astype(BF16), wd_ref[...], NT, preferred_element_type=F32)
        upv = up_ref[...]
        gate, val = upv[:, :D_FF], upv[:, D_FF:]
        sg = _sigmoid(gate)
        dval = dact * (gate * sg)
        dgate = dact * val * (sg * (1.0 + gate * (1.0 - sg)))
        dup = jnp.concatenate([dgate, dval], axis=1)
        dup1, dup2 = _rows_after(dup, carry_ref[...])
        carry_ref[...] = dup[0:8]
        u = u_ref[...].astype(F32)
        gcw_ref[2:3, :] += jnp.sum(dup * u, axis=0, keepdims=True)
        gcw_ref[1:2, :] += jnp.sum(dup1 * u, axis=0, keepdims=True)
        gcw_ref[0:1, :] += jnp.sum(dup2 * u, axis=0, keepdims=True)
        cw = cw_ref[...]
        du = (cw[2:3] * dup + cw[1:2] * dup1 + cw[0:1] * dup2).astype(BF16)
        du_ref[...] = du
        dhn = lax.dot_general(du, wu_ref[...], NT, preferred_element_type=F32)
        h1v = h1_ref[...]
        dh1, dg = _rms_bwd(dhn, h1v, _rstd(h1v), g_ref[...])
        dh1_ref[...] = dh2v + dh1
        gg_ref[...] += jnp.sum(dg, axis=0, keepdims=True)

    row = lambda w: pl.BlockSpec((tm, w), lambda i: (nt - 1 - i, 0))
    return pl.pallas_call(
        body, name="ffn_bwd", grid=(nt,),
        in_specs=[row(D_MODEL), row(FF2), row(FF2),
                  row(D_MODEL), _resident((D_MODEL, FF2)), _resident(ffn_cw.shape), _resident((D_FF, D_MODEL)),
                  _resident((1, D_MODEL))],
        out_specs=[row(FF2), row(D_MODEL), pl.BlockSpec((3, FF2), lambda i: (0, 0)),
                   pl.BlockSpec((1, D_MODEL), lambda i: (0, 0))],
        out_shape=[jax.ShapeDtypeStruct((s, FF2), BF16), jax.ShapeDtypeStruct((s, D_MODEL), F32),
                   jax.ShapeDtypeStruct((3, FF2), F32), jax.ShapeDtypeStruct((1, D_MODEL), F32)],
        scratch_shapes=[pltpu.VMEM((8, FF2), F32)],
        compiler_params=_params("arbitrary"),
    )(dh2, u, up, h1, w_up, ffn_cw, w_down, g2)


def _mix_bwd(dh1, gates, a, cv, c3, attn, conv, merged, conv_w, w_br, w_out, tm, comm=None):
    s = dh1.shape[0]
    nt = s // tm
    halo = 16

    def body(dh1_ref, gt_ref, a_ref, cv_ref, c3_ref, ch_ref, at_ref, cn_ref, mg_ref, cw_ref, wbr_ref,
             wo_ref, dat_ref, dc3_ref, dgt_ref, gcw_ref, gbr_ref, gout_ref, carry_ref, br_acc, out_acc):
        i = pl.program_id(0)

        @pl.when(i == 0)
        def _():
            for ref in (carry_ref, gcw_ref, br_acc, out_acc):
                ref[...] = jnp.zeros_like(ref)

        dh1v = dh1_ref[...].astype(BF16)
        out_acc[...] += lax.dot_general(mg_ref[...], dh1v, TN, preferred_element_type=F32)
        dm = lax.dot_general(dh1v, wo_ref[...], NT, preferred_element_type=F32)
        gt = gt_ref[...].astype(F32)
        sa, sc = _sigmoid(gt[:, :D_MODEL]), _sigmoid(gt[:, D_MODEL:])
        da = (dm * sa).astype(BF16)
        dcv = (dm * sc).astype(BF16)
        br_acc[:ATTN_W, :] += lax.dot_general(at_ref[...], da, TN, preferred_element_type=F32)
        br_acc[ATTN_W:, :] += lax.dot_general(cn_ref[...], dcv, TN, preferred_element_type=F32)
        dgt_ref[...] = jnp.concatenate(
            [dm * a_ref[...].astype(F32) * (sa * (1.0 - sa)), dm * cv_ref[...].astype(F32) * (sc * (1.0 - sc))],
            axis=1).astype(BF16)
        dat_ref[...] = lax.dot_general(da, wbr_ref[:ATTN_W, :], NT, preferred_element_type=F32).astype(BF16)
        dconv = lax.dot_general(dcv, wbr_ref[ATTN_W:, :], NT, preferred_element_type=F32)
        c3v = c3_ref[...].astype(F32)
        cb, cc, cx = c3v[:, :CONV_W], c3v[:, CONV_W:2 * CONV_W], c3v[:, 2 * CONV_W:]
        z = cc * cx
        chv = ch_ref[...].astype(F32)[halo - 8:halo] * (i < nt - 1).astype(F32)
        zh = chv[:, CONV_W:2 * CONV_W] * chv[:, 2 * CONV_W:]
        cw = cw_ref[...]
        cz = _causal_conv(z, zh, cw)
        dcz = dconv * cb
        dcz1, dcz2 = _rows_after(dcz, carry_ref[...])
        carry_ref[...] = dcz[0:8]
        gcw_ref[2:3, :] += jnp.sum(dcz * z, axis=0, keepdims=True)
        gcw_ref[1:2, :] += jnp.sum(dcz1 * z, axis=0, keepdims=True)
        gcw_ref[0:1, :] += jnp.sum(dcz2 * z, axis=0, keepdims=True)
        dz = cw[2:3] * dcz + cw[1:2] * dcz1 + cw[0:1] * dcz2
        dc3_ref[...] = jnp.concatenate([dconv * cz, dz * cx, dz * cc], axis=1).astype(BF16)

        @pl.when(i == nt - 1)
        def _():
            gbr_ref[...] = br_acc[...].astype(BF16)
            gout_ref[...] = out_acc[...].astype(BF16)

    row = lambda w: pl.BlockSpec((tm, w), lambda i: (nt - 1 - i, 0))
    return _call(
        comm, body, name="mix_bwd", grid=(nt,),
        in_specs=[row(D_MODEL), row(GATES_W), row(D_MODEL), row(D_MODEL), row(C3_W),
                  pl.BlockSpec((halo, C3_W), lambda i: (jnp.maximum((nt - 1 - i) * (tm // halo) - 1, 0), 0)),
                  row(ATTN_W), row(CONV_W), row(D_MODEL), _resident(conv_w.shape),
                  _resident((ATTN_W + CONV_W, D_MODEL)), _resident((D_MODEL, D_MODEL))],
        out_specs=[row(ATTN_W), row(C3_W), row(GATES_W), pl.BlockSpec((3, CONV_W), lambda i: (0, 0)),
                   _resident((ATTN_W + CONV_W, D_MODEL)), _resident((D_MODEL, D_MODEL))],
        out_shape=[jax.ShapeDtypeStruct((s, ATTN_W), BF16), jax.ShapeDtypeStruct((s, C3_W), BF16),
                   jax.ShapeDtypeStruct((s, GATES_W), BF16), jax.ShapeDtypeStruct((3, CONV_W), F32),
                   jax.ShapeDtypeStruct((ATTN_W + CONV_W, D_MODEL), BF16),
                   jax.ShapeDtypeStruct((D_MODEL, D_MODEL), BF16)],
        scratch_shapes=[pltpu.VMEM((8, CONV_W), F32), pltpu.VMEM((ATTN_W + CONV_W, D_MODEL), F32),
                        pltpu.VMEM((D_MODEL, D_MODEL), F32)],
        compiler_params=_params("arbitrary"),
    )(dh1, gates, a, cv, c3, c3, attn, conv, merged, conv_w, w_br, w_out)


def _attn_bwd(qkv, sinks, o, do, comm=None):
    s = qkv.shape[0]
    npair = s // (2 * BLOCK)

    def one_block(sk_ref, bias, q, kp, kc, vp, vc, ov, dov, dsk_ref):
        dqs, dks, dvs = [], [], []
        for h in range(N_KV_HEADS):
            hs = slice(h * HEAD_DIM, (h + 1) * HEAD_DIM)
            k2 = jnp.concatenate([kp[:, hs], kc[:, hs]], axis=0)
            v2 = jnp.concatenate([vp[:, hs], vc[:, hs]], axis=0)
            qg, og, dog = _stack_heads(q, h), _stack_heads(ov, h), _stack_heads(dov, h)
            sc = lax.dot_general(k2, qg, NT, preferred_element_type=F32) * ATTN_SCALE + bias
            sink = _sink_row(sk_ref, h)
            m = jnp.maximum(jnp.max(sc, axis=0, keepdims=True), sink)
            p = jnp.exp(sc - m)
            psink = jnp.exp(sink - m)
            inv = 1.0 / (jnp.sum(p, axis=0, keepdims=True) + psink)
            p = p * inv
            delta = jnp.sum(dog.astype(F32) * og.astype(F32), axis=1, keepdims=True).T
            dp = lax.dot_general(v2, dog, NT, preferred_element_type=F32)
            ds = (p * (dp - delta)).astype(BF16)
            dqs.append((lax.dot_general(k2, ds, TN, preferred_element_type=F32) * ATTN_SCALE).T)
            dks.append(jnp.dot(ds, qg, preferred_element_type=F32) * ATTN_SCALE)
            dvs.append(jnp.dot(p.astype(BF16), dog, preferred_element_type=F32))
            dsink = -(psink * inv * delta)
            for g in range(GROUP):
                r = h * GROUP + g
                dsk_ref[r:r + 1, :] += jnp.sum(dsink[:, g * BLOCK:(g + 1) * BLOCK])
        return _unstack_heads(dqs), jnp.concatenate(dks, axis=1), jnp.concatenate(dvs, axis=1)

    def body(sk_ref, bias0_ref, bias1_ref, q_ref, kp_ref, kc_ref, vp_ref, vc_ref, o_ref, do_ref,
             dq_ref, dke_ref, dko_ref, dve_ref, dvo_ref, dsk_ref, ck_ref, cvv_ref):
        i = pl.program_id(0)

        @pl.when(i == 0)
        def _():
            for ref in (ck_ref, cvv_ref, dsk_ref):
                ref[...] = jnp.zeros_like(ref)

        @pl.when(i < npair)
        def _():
            kc, vc = kc_ref[...], vc_ref[...]
            first, second = slice(0, BLOCK), slice(BLOCK, 2 * BLOCK)
            dq0, dk0, dv0 = one_block(sk_ref, bias0_ref[...], q_ref[first, :], kp_ref[...], kc[first], vp_ref[...],
                                      vc[first], o_ref[first, :], do_ref[first, :], dsk_ref)
            dq1, dk1, dv1 = one_block(sk_ref, bias1_ref[...], q_ref[second, :], kc[first], kc[second], vc[first],
                                      vc[second], o_ref[second, :], do_ref[second, :], dsk_ref)
            dq_ref[first, :] = dq0.astype(BF16)
            dq_ref[second, :] = dq1.astype(BF16)
            dko_ref[...] = (ck_ref[...] + dk0[:BLOCK]).astype(BF16)
            dvo_ref[...] = (cvv_ref[...] + dv0[:BLOCK]).astype(BF16)
            dke_ref[...] = (dk0[BLOCK:] + dk1[:BLOCK]).astype(BF16)
            dve_ref[...] = (dv0[BLOCK:] + dv1[:BLOCK]).astype(BF16)
            ck_ref[...] = dk1[BLOCK:]
            cvv_ref[...] = dv1[BLOCK:]

        @pl.when(i == npair)
        def _():
            dko_ref[...] = ck_ref[...].astype(BF16)
            dvo_ref[...] = cvv_ref[...].astype(BF16)

    cur = lambda i: jnp.minimum(i, npair - 1)
    done = lambda i: jnp.maximum(i - 1, 0)
    rows = pl.BlockSpec((2 * BLOCK, ATTN_W), lambda i: (cur(i), 0))
    even = pl.BlockSpec((BLOCK, KV_W), lambda i: (cur(i), 0))
    odd = pl.BlockSpec((BLOCK, KV_W), lambda i: (done(i), 0))
    half = jax.ShapeDtypeStruct((s // 2, KV_W), BF16)
    return _call(
        comm, body, name="attn_bwd", grid=(npair + 1,),
        in_specs=[pl.BlockSpec(memory_space=pltpu.SMEM), *_attn_bias_specs(), *_block_specs(2, npair), rows, rows],
        out_specs=[rows, even, odd, even, odd, pl.BlockSpec((N_HEADS, 128), lambda i: (0, 0))],
        out_shape=[jax.ShapeDtypeStruct((s, ATTN_W), BF16), half, half, half, half,
                   jax.ShapeDtypeStruct((N_HEADS, 128), F32)],
        scratch_shapes=[pltpu.VMEM((BLOCK, KV_W), F32), pltpu.VMEM((BLOCK, KV_W), F32)],
        compiler_params=_params("arbitrary"),
    )(sinks, _attn_bias(), _attn_bias(), qkv, qkv, qkv, qkv, qkv, o, do)


def _inproj_bwd(dq, dk, dv, dc3, dgt, w_in, x, xn, dh1, g1):
    s = x.shape[0]
    tm = min(2 * BLOCK, s)
    nt = s // tm

    def body(dq_ref, dke_ref, dko_ref, dve_ref, dvo_ref, dc3_ref, dgt_ref, w_ref, x_ref, xn_ref, dh1_ref, g_ref,
             dx_ref, gw_ref, gb_ref, gg_ref, acc_ref):
        i = pl.program_id(0)

        @pl.when(i == 0)
        def _():
            for ref in (gb_ref, gg_ref, acc_ref):
                ref[...] = jnp.zeros_like(ref)

        dk = jnp.concatenate([dke_ref[...], dko_ref[...]], axis=0)
        dv = jnp.concatenate([dve_ref[...], dvo_ref[...]], axis=0)
        dp = jnp.concatenate([dq_ref[...], dk, dv, dc3_ref[...], dgt_ref[...]], axis=1)
        acc_ref[...] += lax.dot_general(dp, xn_ref[...], TN, preferred_element_type=F32)
        gb_ref[...] += jnp.sum(dp.astype(F32), axis=0, keepdims=True)
        dxn = jnp.dot(dp, w_ref[...], preferred_element_type=F32)
        xf = x_ref[...]
        dx, dg = _rms_bwd(dxn, xf, _rstd(xf), g_ref[...])
        dx_ref[...] = dh1_ref[...] + dx
        gg_ref[...] += jnp.sum(dg, axis=0, keepdims=True)

        @pl.when(i == nt - 1)
        def _():
            gw_ref[...] = acc_ref[...].astype(BF16)

    row = lambda w: pl.BlockSpec((tm, w), lambda i: (i, 0))
    acc = lambda w: pl.BlockSpec((1, w), lambda i: (0, 0))
    block = pl.BlockSpec((tm // 2, KV_W), lambda i: (i, 0))
    return pl.pallas_call(
        body, name="inproj_bwd", grid=(nt,),
        in_specs=[row(ATTN_W), block, block, block, block, row(C3_W), row(GATES_W), _resident((IN_W, D_MODEL)),
                  row(D_MODEL), row(D_MODEL), row(D_MODEL), _resident((1, D_MODEL))],
        out_specs=[row(D_MODEL), _resident((IN_W, D_MODEL)), acc(IN_W), acc(D_MODEL)],
        out_shape=[jax.ShapeDtypeStruct((s, D_MODEL), F32), jax.ShapeDtypeStruct((IN_W, D_MODEL), BF16),
                   jax.ShapeDtypeStruct((1, IN_W), F32), jax.ShapeDtypeStruct((1, D_MODEL), F32)],
        scratch_shapes=[pltpu.VMEM((IN_W, D_MODEL), F32)],
        compiler_params=_params("arbitrary"),
    )(dq, *dk, *dv, dc3, dgt, w_in, x, xn, dh1, g1)


def _wgrad(a, b, bm, bn, bk, name, comm=None):
    s, m = a.shape
    n = b.shape[1]
    nk = s // bk

    def body(a_ref, b_ref, o_ref, acc_ref):
        k = pl.program_id(2)

        @pl.when(k == 0)
        def _():
            acc_ref[...] = jnp.zeros_like(acc_ref)

        acc_ref[...] += lax.dot_general(a_ref[...].astype(BF16), b_ref[...].astype(BF16), TN,
                                        preferred_element_type=F32)

        @pl.when(k == nk - 1)
        def _():
            o_ref[...] = acc_ref[...].astype(BF16)

    return _call(
        comm, body, name=name, grid=(m // bm, n // bn, nk),
        in_specs=[pl.BlockSpec((bk, bm), lambda i, j, k: (k, i)), pl.BlockSpec((bk, bn), lambda i, j, k: (k, j))],
        out_specs=pl.BlockSpec((bm, bn), lambda i, j, k: (i, j)),
        out_shape=jax.ShapeDtypeStruct((m, n), BF16),
        scratch_shapes=[pltpu.VMEM((bm, bn), F32)],
        compiler_params=_params("parallel", "parallel", "arbitrary"),
    )(a, b)


class _Carry:
    def __init__(self, jobs, reads=None, bufs=None, fresh=None):
        self.jobs, self.reads, self.bufs, self.fresh = jobs, reads or {}, bufs or {}, fresh or {}
        self.out = {}


class _Job:
    def __init__(self, n_sems, plan):
        self.n_sems, self.plan = n_sems, plan


def _plan_all(jobs, hbm, send, recv):
    pos = _position()
    starts, waits, base = [], [], 0
    for job in jobs:
        s, w = job.plan(hbm, pos, send, recv, base)
        starts, waits, base = starts + s, waits + w, base + job.n_sems
    return starts, waits


def _call(comm, body, **kw):
    if comm is None:
        return pl.pallas_call(body, **kw)
    grid = kw["grid"]
    single = not isinstance(kw["out_shape"], (list, tuple))
    out_shape = [kw["out_shape"]] if single else list(kw["out_shape"])
    out_specs = [kw["out_specs"]] if single else list(kw["out_specs"])
    in_specs = list(kw["in_specs"])
    scratch = list(kw.get("scratch_shapes", ()))
    r_names, b_names, f_names = list(comm.reads), list(comm.bufs), list(comm.fresh)
    n_args, n_out, n_scr = len(in_specs), len(out_shape), len(scratch)
    n_sems = sum(j.n_sems for j in comm.jobs)

    def wrapped(*refs):
        k = n_args
        hbm = dict(zip(r_names, refs[k:k + len(r_names)]))
        k += len(r_names) + len(b_names)
        outs = refs[k:k + n_out]
        k += n_out
        hbm.update(zip(b_names + f_names, refs[k:k + len(b_names) + len(f_names)]))
        k += len(b_names) + len(f_names)
        send, recv = refs[k + n_scr:]
        starts, waits = _plan_all(comm.jobs, hbm, send, recv)
        ids = [pl.program_id(a) for a in range(len(grid))]
        first = functools.reduce(jnp.logical_and, [i == 0 for i in ids])
        last = functools.reduce(jnp.logical_and, [i == g - 1 for i, g in zip(ids, grid)])

        @pl.when(first)
        def _():
            for cp in starts:
                cp.start()

        body(*refs[:n_args], *outs, *refs[k:k + n_scr])

        @pl.when(last)
        def _():
            for cp in waits:
                cp.wait_recv()
            for cp in starts:
                cp.wait_send()

    sems = pltpu.SemaphoreType.DMA((n_sems,))
    held = [jax.ShapeDtypeStruct(a.shape, a.dtype) for a in comm.bufs.values()] + list(comm.fresh.values())
    call = pl.pallas_call(
        wrapped, name=kw["name"], grid=grid,
        in_specs=in_specs + [_ANY] * (len(r_names) + len(b_names)),
        out_specs=out_specs + [_ANY] * len(held),
        out_shape=out_shape + held,
        input_output_aliases={n_args + len(r_names) + i: n_out + i for i in range(len(b_names))},
        scratch_shapes=scratch + [sems, sems],
        compiler_params=_params(*["arbitrary"] * len(grid)),
    )

    def run(*args):
        res = call(*args, *comm.reads.values(), *comm.bufs.values())
        comm.out = dict(zip(b_names + f_names, res[n_out:]))
        return res[0] if single else res[:n_out]

    return run


def _exchange(name, phases, reads=None, bufs=None, fresh=None):
    comm = _Carry([j for ph in phases for j in ph], reads, bufs, fresh)
    r_names, b_names, f_names = list(comm.reads), list(comm.bufs), list(comm.fresh)
    n_sems = sum(j.n_sems for j in comm.jobs)

    def body(*refs):
        hbm = dict(zip(r_names, refs[:len(r_names)]))
        k = len(r_names) + len(b_names)
        hbm.update(zip(b_names + f_names, refs[k:k + len(b_names) + len(f_names)]))
        send, recv = refs[-2:]
        pos = _position()
        started, base = [], 0
        for ph in phases:
            waits = []
            for job in ph:
                s, w = job.plan(hbm, pos, send, recv, base)
                base += job.n_sems
                for cp in s:
                    cp.start()
                started, waits = started + s, waits + w
            for cp in waits:
                cp.wait_recv()
        for cp in started:
            cp.wait_send()

    sems = pltpu.SemaphoreType.DMA((n_sems,))
    held = [jax.ShapeDtypeStruct(a.shape, a.dtype) for a in comm.bufs.values()] + list(comm.fresh.values())
    res = pl.pallas_call(
        body, name=name, in_specs=[_ANY] * (len(r_names) + len(b_names)), out_specs=[_ANY] * len(held),
        out_shape=held, input_output_aliases={len(r_names) + i: i for i in range(len(b_names))},
        scratch_shapes=[sems, sems],
    )(*comm.reads.values(), *comm.bufs.values())
    return dict(zip(b_names + f_names, res))


_HBM = pl.BlockSpec(memory_space=pltpu.HBM)
_SEM = pl.BlockSpec(memory_space=pltpu.SEMAPHORE)
_EFFECT = pltpu.SideEffectType.DATAFLOW_SIDE_EFFECTING


def _start_exchanges(name, groups):
    names = [list(arrays) for _, arrays in groups]
    first = [sum(len(ns) for ns in names[:g]) for g in range(len(groups))]
    n, ng = sum(len(ns) for ns in names), len(groups)

    def body(*refs):
        for g, (jobs, _) in enumerate(groups):
            hbm = dict(zip(names[g], refs[first[g]:first[g] + len(names[g])]))
            for cp in _plan_all(jobs, hbm, refs[n + 2 * g], refs[n + 2 * g + 1])[0]:
                cp.start()
        refs[-1][...] = jnp.zeros_like(refs[-1])

    given = [pltpu.with_memory_space_constraint(
        a if isinstance(a, jax.Array) else lax.empty(a.shape, a.dtype), pltpu.HBM)
        for _, arrays in groups for a in arrays.values()]
    sems = [pltpu.SemaphoreType.DMA((sum(j.n_sems for j in jobs),)) for jobs, _ in groups for _ in range(2)]
    res = pl.pallas_call(
        body, name=name,
        out_shape=(*sems, *[pltpu.HBM(a.shape, a.dtype) for a in given], jax.ShapeDtypeStruct((8, 128), F32)),
        in_specs=[_HBM] * n, out_specs=(*[_SEM] * (2 * ng), *[_HBM] * n, pl.BlockSpec(memory_space=pltpu.VMEM)),
        input_output_aliases={i: 2 * ng + i for i in range(n)},
        compiler_params=pltpu.CompilerParams(has_side_effects=_EFFECT),
    )(*given)
    held = res[2 * ng:2 * ng + n]
    states = [(names[g], groups[g][0], res[2 * g], res[2 * g + 1], held[first[g]:first[g] + len(names[g])])
              for g in range(ng)]
    return states, res[-1]


def _start_exchange(name, jobs, arrays):
    states, token = _start_exchanges(name, [(jobs, arrays)])
    return states[0], token


def _finish_exchange(name, state, after):
    names, jobs, send_sem, recv_sem, held = state
    n = len(names)

    def body(*refs):
        hbm = dict(zip(names, refs[:n]))
        send, recv = refs[n:n + 2]
        starts, waits = _plan_all(jobs, hbm, send, recv)
        for cp in waits:
            cp.wait_recv()
        for cp in starts:
            cp.wait_send()

    res = pl.pallas_call(
        body, name=name, out_shape=tuple(pltpu.HBM(a.shape, a.dtype) for a in held),
        in_specs=[_HBM] * n + [_SEM, _SEM, _ANY], out_specs=tuple([_HBM] * n),
        input_output_aliases={i: i for i in range(n)},
        compiler_params=pltpu.CompilerParams(has_side_effects=_EFFECT),
    )(*held, send_sem, recv_sem, after)
    return dict(zip(names, res))


def _relay_exchange(name, state, jobs, after):
    names, arrived, send_sem, recv_sem, held = state
    n = len(names)

    def body(*refs):
        hbm = dict(zip(names, refs[:n]))
        starts, waits = _plan_all(arrived, hbm, refs[n], refs[n + 1])
        for cp in waits:
            cp.wait_recv()
        for cp in starts:
            cp.wait_send()
        for cp in _plan_all(jobs, hbm, refs[n + 3], refs[n + 4])[0]:
            cp.start()
        refs[-1][...] = jnp.zeros_like(refs[-1])

    sems = pltpu.SemaphoreType.DMA((sum(j.n_sems for j in jobs),))
    res = pl.pallas_call(
        body, name=name,
        out_shape=(sems, sems, *[pltpu.HBM(a.shape, a.dtype) for a in held], jax.ShapeDtypeStruct((8, 128), F32)),
        in_specs=[_HBM] * n + [_SEM, _SEM, _ANY],
        out_specs=(_SEM, _SEM, *[_HBM] * n, pl.BlockSpec(memory_space=pltpu.VMEM)),
        input_output_aliases={i: 2 + i for i in range(n)},
        compiler_params=pltpu.CompilerParams(has_side_effects=_EFFECT),
    )(*held, send_sem, recv_sem, after)
    return (names, jobs, res[0], res[1], res[2:2 + n]), res[-1]


def _row_tile(rows, bytes_per_row):
    best = 16
    for t in range(16, rows + 1, 16):
        if rows % t == 0 and t * bytes_per_row <= 9 * 1024 * 1024:
            best = t
    return best


def _rowwise(fn, ins, out_dtypes, name, after=None):
    rows, cols = ins[0].shape
    per_row = sum(cols * a.dtype.itemsize for a in ins) + sum(cols * jnp.dtype(d).itemsize for d in out_dtypes)
    tr = _row_tile(rows, per_row)
    n_in = len(ins)

    def body(*refs):
        outs = fn(*[r[...] for r in refs[:n_in]])
        for o_ref, o in zip(refs[-len(out_dtypes):], outs):
            o_ref[...] = o.astype(o_ref.dtype)

    tile = pl.BlockSpec((tr, cols), lambda i: (i, 0))
    behind = [] if after is None else [after]
    return pl.pallas_call(
        body, name=name, grid=(rows // tr,),
        in_specs=[tile] * n_in + [pl.BlockSpec((8, 128), lambda i: (0, 0))] * len(behind),
        out_specs=[tile] * len(out_dtypes),
        out_shape=[jax.ShapeDtypeStruct((rows, cols), d) for d in out_dtypes],
        compiler_params=_params("parallel"),
    )(*ins, *behind)


def _tiled(fn, name, grid, pos, ins, outs):
    n_in = len(ins)

    def body(pos_ref, *refs):
        res = fn(*[r[...] for r in refs[:n_in]])
        for o_ref, o in zip(refs[n_in:], res):
            o_ref[...] = o.astype(o_ref.dtype)

    return pl.pallas_call(
        body, name=name,
        grid_spec=pltpu.PrefetchScalarGridSpec(
            num_scalar_prefetch=1, grid=grid,
            in_specs=[pl.BlockSpec(bs, im) for _, bs, im in ins],
            out_specs=[pl.BlockSpec(bs, im) for _, _, bs, im in outs]),
        out_shape=[jax.ShapeDtypeStruct(s, d) for s, d, _, _ in outs],
        compiler_params=_params("parallel"),
    )(pos, *[a for a, _, _ in ins])


def _adamw(w, g, m, v):
    m = ADAM_B1 * m + (1.0 - ADAM_B1) * g
    v = ADAM_B2 * v + (1.0 - ADAM_B2) * (g * g)
    m_hat = m / (1.0 - ADAM_B1 ** ADAM_STEP)
    v_hat = v / (1.0 - ADAM_B2 ** ADAM_STEP)
    return -ADAM_LR * (m_hat / (jnp.sqrt(v_hat) + ADAM_EPS) + ADAM_WD * w), m, v


def _adamw_small(pos, own, slots, params):
    n = len(params)

    def body(pos_ref, own_ref, slots_ref, *refs):
        ins, outs, total_ref = refs[:3 * n], refs[3 * n:-1], refs[-1]
        chip = pos_ref[0]
        idx = 2 * chip + pos_ref[1]
        term = lambda q: jnp.where(idx == q, own_ref[...], slots_ref[q])
        acc = term(0)
        for q in range(1, N_DEV):
            acc = acc + term(q)
        total_ref[...] = acc
        outs[0][...] = total_ref[0:1, :]
        for k, (w, _, _, row) in enumerate(params):
            width = min(w.shape[-1], 128)
            for t in range(w.shape[0]):
                for j in range(w.shape[-1] // width):
                    lanes = slice(j * width, (j + 1) * width)
                    at = (slice(t, t + 1), lanes) if w.ndim == 2 else (t, slice(None), lanes)
                    g = total_ref[pl.ds(row(t, j, chip), 1), :][:, :width]
                    new = _adamw(ins[3 * k][at], g, ins[3 * k + 1][at], ins[3 * k + 2][at])
                    for o_ref, o in zip(outs[1 + 4 * k:5 + 4 * k], (g, *new)):
                        o_ref[at] = o

    vmem = pl.BlockSpec(memory_space=pltpu.VMEM)
    return pl.pallas_call(
        body, name="adamw_small",
        in_specs=[pl.BlockSpec(memory_space=pltpu.SMEM)] + [vmem] * (2 + 3 * n),
        out_shape=[jax.ShapeDtypeStruct((1, 128), F32)]
        + [jax.ShapeDtypeStruct(p[0].shape, F32) for p in params for _ in range(4)],
        scratch_shapes=[pltpu.VMEM(own.shape, F32)],
    )(pos, own, slots, *[a for p in params for a in p[:3]])


class _Layout:
    def __init__(self, rows, cols, stacked):
        self.rows, self.cols, self.stacked = rows, cols, stacked

    def whole(self, rows=None):
        r = self.rows if rows is None else rows
        return (N_CHIPS, r, self.cols) if self.stacked else (r, N_CHIPS * self.cols)

    def part_rows(self, h, q=0, nq=1):
        n = self.rows // 2 // nq
        return pl.ds(pl.multiple_of(h * (self.rows // 2) + q * n, 16), n)

    def half_rows(self, h):
        return self.part_rows(h)

    def block(self, ref, p, rows=slice(None)):
        if self.stacked:
            return ref.at[p, rows, :]
        return ref.at[rows, pl.ds(pl.multiple_of(p * self.cols, 128), self.cols)]

    def all_chips(self, ref, rows):
        return ref.at[:, rows, :] if self.stacked else ref.at[rows, :]


BIG = (
    _Layout(IN_SHARD, D_MODEL, True),
    _Layout(ATTN_W + CONV_W, D_MODEL // N_CHIPS, False),
    _Layout(D_MODEL // N_CHIPS, D_MODEL, True),
    _Layout(D_MODEL, FF2 // N_CHIPS, False),
    _Layout(D_FF // N_CHIPS, D_MODEL, True),
)
N_BIG = len(BIG)
_ANY = pl.BlockSpec(memory_space=pl.ANY)


def _position():
    x, y, c = lax.axis_index("x"), lax.axis_index("y"), lax.axis_index("c")
    return x, y, c, 2 * x + y


def _core_of_chip(p, c):
    return (p >> 1, p & 1, c)


def _place_cast(shard, lay, pos, name, after=None):
    rows, cols = shard.shape
    tr = _row_tile(rows, cols * 6)
    if lay.stacked:
        out = (lay.whole(), BF16, (None, tr, cols), lambda i, pos: (pos[0], i, 0))
    else:
        out = (lay.whole(), BF16, (tr, cols), lambda i, pos: (i, pos[0]))
    ins = [(shard, (tr, cols), lambda i, pos: (i, 0))]
    if after is not None:
        ins.append((after, (8, 128), lambda i, pos: (0, 0)))
    return _tiled(lambda a, *_: (a,), name, (rows // tr,), pos, ins, [out])[0]


def _place_cast_pair(top, bottom, lay, pos, name, after=None):
    rows, cols = top.shape
    ins = [(top, (rows, cols), lambda i, pos: (0, 0)), (bottom, (rows, cols), lambda i, pos: (0, 0))]
    if after is not None:
        ins.append((after, (8, 128), lambda i, pos: (0, 0)))
    return _tiled(lambda a, b, *_: (jnp.concatenate([a, b], axis=0),), name, (1,), pos, ins,
                  [(lay.whole(), BF16, (2 * rows, cols), lambda i, pos: (0, pos[0]))])[0]


def _adamw_pair(top, bottom, g, after=None):
    rows = top[0].shape[0]

    def body(*refs):
        (wa, ma, va, wb, mb, vb, g_ref), outs = refs[:7], refs[-8:]
        for (w, m, v), gg, o in (((wa, ma, va), g_ref[:rows], outs[:4]), ((wb, mb, vb), g_ref[rows:], outs[4:])):
            for o_ref, val in zip(o, (gg, *_adamw(w[...], gg, m[...], v[...]))):
                o_ref[...] = val

    behind = [] if after is None else [after]
    res = pl.pallas_call(
        body, name="adamw_w_br", out_shape=[jax.ShapeDtypeStruct(top[0].shape, F32)] * 8,
        in_specs=[pl.BlockSpec(memory_space=pltpu.VMEM)] * 7 + [_ANY] * len(behind),
    )(*top, *bottom, g, *behind)
    return res[:4], res[4:]


def _remote(src, dst, send, recv, k, device):
    return pltpu.make_async_remote_copy(src_ref=src, dst_ref=dst, send_sem=send.at[k], recv_sem=recv.at[k],
                                        device_id=device, device_id_type=MESH)


def _arrival(dst, send, recv, k, me):
    return _remote(dst, dst, send, recv, k, me)


def _gather_ici(lay, name, q=0, nq=1):
    def plan(hbm, pos, send, recv, base):
        x, y, c, me = pos
        rows = lay.part_rows(c, q, nq)
        mine = lay.block(hbm[name], me, rows)
        starts = [_remote(mine, mine, send, recv, base + d - 1, _core_of_chip(me ^ d, c)) for d in (1, 2, 3)]
        waits = [_arrival(lay.block(hbm[name], me ^ d, rows), send, recv, base + d - 1, (x, y, c)) for d in (1, 2, 3)]
        return starts, waits
    return _Job(3, plan)


def _gather_near(lay, name):
    def plan(hbm, pos, send, recv, base):
        x, y, c, me = pos
        rows = lay.part_rows(c)
        mine = lay.block(hbm[name], me, rows)
        starts = [_remote(mine, mine, send, recv, base + d - 1, _core_of_chip(me ^ d, c)) for d in (1, 2)]
        waits = [_arrival(lay.block(hbm[name], me ^ d, rows), send, recv, base + d - 1, (x, y, c)) for d in (1, 2)]
        return starts, waits
    return _Job(2, plan)


def _gather_far(lay, name):
    def plan(hbm, pos, send, recv, base):
        x, y, c, me = pos
        starts, waits = [], []
        for q, d in ((0, 1), (1, 2)):
            got = lay.block(hbm[name], me ^ (3 - d), lay.part_rows(c, q, 2))
            starts.append(_remote(got, got, send, recv, base + q, _core_of_chip(me ^ d, c)))
            waits.append(_arrival(lay.block(hbm[name], me ^ 3, lay.part_rows(c, q, 2)), send, recv, base + q, (x, y, c)))
        return starts, waits
    return _Job(2, plan)


def _gather_d2d(lay, name, q=0, nq=1, chips=(1, 2, 3)):
    def plan(hbm, pos, send, recv, base):
        x, y, c, me = pos
        starts, waits = [], []
        for k, d in enumerate(chips):
            got = lay.block(hbm[name], me ^ d, lay.part_rows(c, q, nq))
            starts.append(_remote(got, got, send, recv, base + k, (x, y, 1 - c)))
            waits.append(_arrival(lay.block(hbm[name], me ^ d, lay.part_rows(1 - c, q, nq)), send, recv, base + k,
                                  (x, y, c)))
        return starts, waits
    return _Job(len(chips), plan)


def _rs_pair(lay, grad, theirs):
    def plan(hbm, pos, send, recv, base):
        x, y, c, _ = pos
        out = _remote(lay.all_chips(hbm[grad], lay.half_rows(1 - c)), hbm[theirs], send, recv, base, (x, y, 1 - c))
        return [out], [_arrival(hbm[theirs], send, recv, base, (x, y, c))]
    return _Job(1, plan)


def _rs_chips(lay, sums, slots):
    def plan(hbm, pos, send, recv, base):
        x, y, c, me = pos
        starts = [_remote(lay.block(hbm[sums], me ^ d), hbm[slots].at[me], send, recv, base + d - 1,
                          _core_of_chip(me ^ d, c)) for d in (1, 2, 3)]
        waits = [_arrival(hbm[slots].at[me ^ d], send, recv, base + d - 1, (x, y, c)) for d in (1, 2, 3)]
        return starts, waits
    return _Job(3, plan)


def _rs_share(lay, shard):
    def plan(hbm, pos, send, recv, base):
        x, y, c, _ = pos
        mine = hbm[shard].at[lay.half_rows(c), :]
        other = hbm[shard].at[lay.half_rows(1 - c), :]
        return [_remote(mine, mine, send, recv, base, (x, y, 1 - c))], [_arrival(other, send, recv, base, (x, y, c))]
    return _Job(1, plan)


def _slots_shape(lay):
    return jax.ShapeDtypeStruct((N_CHIPS, lay.rows // 2, lay.cols), BF16)


def _theirs_shape(lay, dtype=BF16):
    return jax.ShapeDtypeStruct(lay.whole(lay.rows // 2), dtype)


def _pair_sum(grad, theirs, lay, pos, name):
    half = lay.rows // 2
    add = lambda a, b: (a.astype(F32) + b.astype(F32),)
    if lay.stacked:
        tr = _row_tile(half, lay.cols * 6)
        nt = half // tr
        flat = lambda a: a.reshape(-1, lay.cols)
        mine = lambda t, pos: ((t // nt) * (2 * nt) + pos[1] * nt + t % nt, 0)
        grid, blk = (N_CHIPS * nt,), (tr, lay.cols)
        grad, theirs = flat(grad), flat(theirs)
    else:
        tr = _row_tile(half, N_CHIPS * lay.cols * 6)
        nt = half // tr
        mine = lambda t, pos: (pos[1] * nt + t, 0)
        grid, blk = (nt,), (tr, N_CHIPS * lay.cols)
    same = lambda t, pos: (t, 0)
    out = _tiled(add, name, grid, pos, [(grad, blk, mine), (theirs, blk, same)], [(theirs.shape, BF16, blk, same)])[0]
    return out.reshape(lay.whole(half))


def _chip_sums(items, pos, name, after=None):
    ins, outs = [], []
    for sums, slots, lay in items:
        half = lay.rows // 2
        blk3 = (None, half, lay.cols)
        if lay.stacked:
            own = (sums, blk3, lambda i, pos: (pos[0], 0, 0))
        else:
            own = (sums, (half, lay.cols), lambda i, pos: (0, pos[0]))
        ins += [own] + [(slots, blk3, functools.partial(lambda d, i, pos: (pos[0] ^ d, 0, 0), d)) for d in (1, 2, 3)]
        outs.append(((lay.rows, lay.cols), F32, (half, lay.cols), lambda i, pos: (pos[1], 0)))

    def add(*vals):
        v = [a.astype(F32) for a in vals[:4 * len(items)]]
        return tuple(((v[4 * k] + v[4 * k + 1]) + v[4 * k + 2]) + v[4 * k + 3] for k in range(len(items)))

    if after is not None:
        ins.append((after, (8, 128), lambda i, pos: (0, 0)))
    return _tiled(add, name, (1,), pos, ins, outs)


N_DEV = 8


def _to_all(src, slots):
    def plan(hbm, pos, send, recv, base):
        x, y, c, _ = pos
        idx = 4 * x + 2 * y + c
        starts = [_remote(hbm[src], hbm[slots].at[idx], send, recv, base + k - 1,
                          (x ^ (k >> 2), y ^ ((k >> 1) & 1), c ^ (k & 1))) for k in range(1, N_DEV)]
        waits = [_arrival(hbm[slots].at[idx ^ k], send, recv, base + k - 1, (x, y, c)) for k in range(1, N_DEV)]
        return starts, waits
    return _Job(N_DEV - 1, plan)


def _taps_gather(name, cols):
    def plan(hbm, pos, send, recv, base):
        x, y, c, me = pos
        block = lambda p: hbm[name].at[:, pl.ds(pl.multiple_of(p * cols, 128), cols)]
        starts = [_remote(block(me), block(me), send, recv, base + d - 1, _core_of_chip(me ^ d, c)) for d in (1, 2, 3)]
        waits = [_arrival(block(me ^ d), send, recv, base + d - 1, (x, y, c)) for d in (1, 2, 3)]
        return starts, waits
    return _Job(3, plan)


def _pack_rows(parts):
    padded = [jnp.pad(a, ((0, -a.shape[0] % 8), (0, 0))) for a in parts]
    starts = [sum(p.shape[0] for p in padded[:k]) for k in range(len(padded))]
    return jnp.concatenate(padded, axis=0), starts


def kernel(x, mix_norm, w_in, b_in, sinks, conv_w, w_attn_branch, w_conv_branch, w_out, ffn_norm, w_up, ffn_conv_w, w_down, final_norm, loss_target, m_mix_norm, m_w_in, m_b_in, m_sinks, m_conv_w, m_w_attn_branch, m_w_conv_branch, m_w_out, m_ffn_norm, m_w_up, m_ffn_conv_w, m_w_down, m_final_norm, v_mix_norm, v_w_in, v_b_in, v_sinks, v_conv_w, v_w_attn_branch, v_w_conv_branch, v_w_out, v_ffn_norm, v_w_up, v_ffn_conv_w, v_w_down, v_final_norm):
    me = 2 * lax.axis_index("x") + lax.axis_index("y")
    names = ("w_in", "w_br", "w_out", "w_up", "w_down")
    w_of = dict(w_in=w_in[0].T, w_out=w_out[0], w_up=w_up[0], w_down=w_down[0])
    m_of = dict(w_in=m_w_in[0].T, w_out=m_w_out[0], w_up=m_w_up[0], w_down=m_w_down[0])
    v_of = dict(w_in=v_w_in[0].T, w_out=v_w_out[0], w_up=v_w_up[0], w_down=v_w_down[0])
    ab = (w_attn_branch[0], m_w_attn_branch[0], v_w_attn_branch[0])
    cb = (w_conv_branch[0], m_w_conv_branch[0], v_w_conv_branch[0])

    pos = jnp.stack([me, lax.axis_index("c")]).astype(jnp.int32)

    lay = dict(zip(names, BIG))
    xs, target, sk = x[0], loss_target[0], sinks[0]
    s = xs.shape[0]
    tm, tm2, bk, bk2 = min(256, s), min(512, s), min(1024, s), min(2048, s)

    placed = {"w_in": _place_cast(w_of["w_in"], lay["w_in"], pos, "cast_w_in")}
    fly_in, started = _start_exchange("gather_in_start", [_gather_near(lay["w_in"], "w_in")], {"w_in": placed["w_in"]})
    whole = lambda a: jnp.tile(jnp.pad(a[0], ((0, 5), (0, 0))), (1, N_CHIPS)) + started[0:1, 0:1]
    taps_flight, started = _start_exchange(
        "taps_start", [_taps_gather("conv", CONV_W // N_CHIPS), _taps_gather("ffn", FF2 // N_CHIPS)],
        {"conv": whole(conv_w), "ffn": whole(ffn_conv_w)})
    placed["w_br"] = _place_cast_pair(ab[0], cb[0], lay["w_br"], pos, "cast_w_br", after=started)
    for n in names[2:]:
        placed[n] = _place_cast(w_of[n], lay[n], pos, "cast_" + n, after=started)
    trio = ("w_br", "w_out")
    fly_in, started = _relay_exchange(
        "gather_in_relay", fly_in, [_gather_far(lay["w_in"], "w_in"), _gather_d2d(lay["w_in"], "w_in", chips=(1, 2))],
        after=placed["w_down"])
    (fly_trio, fly_up, fly_down), started = _start_exchanges("gather_rest_start", [
        ([_gather_ici(lay[n], n) for n in ws], {**{n: placed[n] for n in ws}, **behind})
        for ws, behind in ((trio, {"behind": started}), (("w_up",), {}), (("w_down",), {}))])

    got = _finish_exchange("gather_in_wait", fly_in, after=started)
    w_in_full = _exchange("gather_in_d2d", [[_gather_d2d(lay["w_in"], "w_in", chips=(3,))]],
                          bufs=got)["w_in"].reshape(IN_W, D_MODEL)
    xn, qkv, c3, gates = _inproj_fwd(xs, mix_norm, w_in_full, b_in, tm2)
    got = _finish_exchange("gather_trio_wait", fly_trio, after=qkv)
    k2 = _Carry([_gather_d2d(lay[n], n) for n in trio], bufs={n: got[n] for n in trio})
    attn = _attn_fwd(qkv, sk, comm=k2)
    w_br = k2.out["w_br"]
    w_out_full = k2.out["w_out"].reshape(D_MODEL, D_MODEL)
    k3 = _Carry([_gather_d2d(lay["w_up"], "w_up")], bufs=_finish_exchange("gather_up_wait", fly_up, after=attn))
    taps = _finish_exchange("taps_wait", taps_flight, after=attn)
    conv_full, ffn_cw_full = taps["conv"], taps["ffn"]
    conv, a, cv, merged, h1, hn = _mix_fwd(xs, attn, c3, gates, conv_full, w_br, w_out_full, ffn_norm, tm2, comm=k3)
    w_up_full = k3.out["w_up"]
    w_down_full = _exchange("gather_down_d2d", [[_gather_d2d(lay["w_down"], "w_down")]],
                            bufs=_finish_exchange("gather_down_wait", fly_down, after=hn))["w_down"].reshape(D_FF, D_MODEL)
    u, up, act, dh2, loss_part, g_fn = _ffn_fwd_loss(hn, h1, w_up_full, ffn_cw_full, w_down_full,
                                                     final_norm[None, :], target, tm)

    grads, sums, slots = {}, {}, {}

    def pair(*ws):
        return _Carry([_rs_pair(lay[n], "g_" + n, "t_" + n) for n in ws], reads={"g_" + n: grads[n] for n in ws},
                      fresh={"t_" + n: _theirs_shape(lay[n], grads[n].dtype) for n in ws})

    def chips(*ws, also=None):
        k = _Carry([_rs_chips(lay[n], "s_" + n, "r_" + n) for n in ws], reads={"s_" + n: sums[n] for n in ws},
                   fresh={"r_" + n: _slots_shape(lay[n]) for n in ws})
        if also is not None:
            k = _Carry(k.jobs + also.jobs, {**k.reads, **also.reads}, None, {**k.fresh, **also.fresh})
        return k

    def pair_sums(k, *ws):
        for n in ws:
            sums[n] = _pair_sum(grads[n], k.out["t_" + n], lay[n], pos, "pair_sum_" + n)

    def take_slots(k, *ws):
        for n in ws:
            slots[n] = k.out["r_" + n]

    du, dh1, g_fcw, g_g2 = _ffn_bwd(dh2, u, up, h1, w_up_full, ffn_cw_full, w_down_full, ffn_norm, tm)
    grads["w_down"] = _wgrad(act, dh2, D_FF // 2, D_MODEL, bk2, "wgrad_down").reshape(lay["w_down"].whole())
    k4 = pair("w_down")
    grads["w_up"] = _wgrad(hn, du, D_MODEL, FF2 // 4, bk2, "wgrad_up", comm=k4)
    pair_sums(k4, "w_down")
    k5 = chips("w_down", also=pair("w_up"))
    dattn, dc3, dgt, g_cw, grads["w_br"], gw_out = _mix_bwd(
        dh1, gates, a, cv, c3, attn, conv, merged, conv_full, w_br, w_out_full, tm2, comm=k5)
    grads["w_out"] = gw_out.reshape(lay["w_out"].whole())
    take_slots(k5, "w_down")
    pair_sums(k5, "w_up")
    up_flight, started = _start_exchange("rs_chips_up_start", [_rs_chips(lay["w_up"], "s", "r")],
                                         {"s": sums["w_up"], "r": _slots_shape(lay["w_up"])})
    k6 = pair(*trio)
    k6.reads["after"] = started
    dq, dk_even, dk_odd, dv_even, dv_odd, g_sk = _attn_bwd(qkv, sk, attn, dattn, comm=k6)
    pair_sums(k6, *trio)
    trio_flight, started = _start_exchange(
        "rs_chips_trio_start", [_rs_chips(lay[n], "s_" + n, "r_" + n) for n in trio],
        {**{"s_" + n: sums[n] for n in trio}, **{"r_" + n: _slots_shape(lay[n]) for n in trio}})
    behind = mix_norm + jnp.tile(started[0:1], (1, D_MODEL // 128))
    grad_x, gw_in, g_b, g_g1 = _inproj_bwd(dq, (dk_even, dk_odd), (dv_even, dv_odd), dc3, dgt, w_in_full, xs, xn,
                                           dh1, behind)
    grads["w_in"] = gw_in.reshape(lay["w_in"].whole())

    in_flight, started = _start_exchange("rs_pair_in_start", [_rs_pair(lay["w_in"], "g", "t")],
                                         {"g": grads["w_in"], "t": _theirs_shape(lay["w_in"])})
    parts = [loss_part, g_g1, g_b, jnp.pad(g_sk[:, 0], (0, 120))[None, :], g_cw, g_g2, g_fcw, g_fn]
    packed, at = _pack_rows([p.reshape(-1, 128) for p in parts])
    small_flight, started = _start_exchange("small_start", [_to_all("v", "slots")],
                                            {"v": packed + started[0:1], "slots": jnp.zeros((N_DEV, *packed.shape), F32)})
    landed = _finish_exchange("rs_chips_up_wait", up_flight, after=started)
    halves = dict(zip(("w_down", "w_up"), _chip_sums(
        [(sums["w_down"], slots["w_down"], lay["w_down"]), (landed["s"], landed["r"], lay["w_up"])], pos,
        "chip_sum_w_down_up")))
    landed = _finish_exchange("rs_pair_in_wait", in_flight, after=halves["w_up"])
    sums["w_in"] = _pair_sum(landed["g"], landed["t"], lay["w_in"], pos, "pair_sum_w_in")
    (in_flight, down_flight, up_flight), started = _start_exchanges("rs_chips_in_start", [
        ([_rs_chips(lay["w_in"], "s", "r")], {"s": sums["w_in"], "r": _slots_shape(lay["w_in"])}),
        ([_rs_share(lay["w_down"], "w_down")], {"w_down": halves["w_down"]}),
        ([_rs_share(lay["w_up"], "w_up")], {"w_up": halves["w_up"]})])
    landed = _finish_exchange("rs_chips_trio_wait", trio_flight, after=started)
    halves.update(zip(trio, _chip_sums([(landed["s_" + n], landed["r_" + n], lay[n]) for n in trio], pos,
                                       "chip_sum_w_br_out")))
    shared = _exchange("share_halves", [[_rs_share(lay[n], n) for n in trio]], bufs={n: halves[n] for n in trio})
    shared["w_down"] = _finish_exchange("share_down_wait", down_flight, after=shared[trio[-1]])["w_down"]
    shared["w_up"] = _finish_exchange("share_up_wait", up_flight, after=shared["w_down"])["w_up"]

    def adam(n, g, after=None):
        return _rowwise(lambda w, g, m, v: (g, *_adamw(w, g, m, v)), [w_of[n], g, m_of[n], v_of[n]], [F32] * 4,
                        "adamw_" + n, after=after)

    new_of, last = {}, None
    for n in ("w_down", "w_up", "w_out"):
        new_of[n] = adam(n, shared[n], last)
        last = new_of[n][1]
    new_of["w_ab"], new_of["w_cb"] = _adamw_pair(ab, cb, shared["w_br"], after=last)
    last = new_of["w_cb"][1]

    arrived = _finish_exchange("small_wait", small_flight, after=last)
    flat = lambda k: lambda t, j, chip: at[k] + j
    mine = lambda k, per_tap: lambda t, j, chip: at[k] + per_tap * t + (per_tap // N_CHIPS) * chip + j
    rows = lambda a: a.reshape(a.shape[1], 1, a.shape[2])
    small_p = [
        (mix_norm, m_mix_norm, v_mix_norm, flat(1)), (b_in, m_b_in, v_b_in, flat(2)), (sinks, m_sinks, v_sinks, flat(3)),
        (rows(conv_w), rows(m_conv_w), rows(v_conv_w), mine(4, CONV_W // 128)),
        (ffn_norm, m_ffn_norm, v_ffn_norm, flat(5)),
        (rows(ffn_conv_w), rows(m_ffn_conv_w), rows(v_ffn_conv_w), mine(6, FF2 // 128)),
        (final_norm[None, :], m_final_norm[None, :], v_final_norm[None, :], flat(7))]
    small_new = _adamw_small(pos, arrived["v"], arrived["slots"], small_p)
    loss = small_new[0][0, 0]
    small_g = small_new[1::4]
    small_new = [small_new[4 * k + 2:4 * k + 5] for k in range(len(small_p))]

    landed = _finish_exchange("rs_chips_in_wait", in_flight, after=small_new[0][0])
    half_in = _chip_sums([(landed["s"], landed["r"], lay["w_in"])], pos, "chip_sum_w_in")[0]
    shared["w_in"] = _exchange("share_in", [[_rs_share(lay["w_in"], "w_in")]], bufs={"w_in": half_in})["w_in"]
    new_of["w_in"] = [a.T for a in adam("w_in", shared["w_in"])]
    big = ("w_in", "w_ab", "w_cb", "w_out", "w_up", "w_down")
    big_g = [new_of[n][0] for n in big]
    big_new = [new_of[n][1:] for n in big]

    order = [("s", 0), ("b", 0), ("s", 1), ("s", 2), ("s", 3), ("b", 1), ("b", 2), ("b", 3), ("s", 4), ("b", 4),
             ("s", 5), ("b", 5), ("s", 6)]
    shapes = [mix_norm.shape, w_in.shape, b_in.shape, sinks.shape, conv_w.shape, w_attn_branch.shape,
              w_conv_branch.shape, w_out.shape, ffn_norm.shape, w_up.shape, ffn_conv_w.shape, w_down.shape,
              final_norm.shape]
    out_g = [(small_g[k] if kind == "s" else big_g[k]).reshape(shp) for (kind, k), shp in zip(order, shapes)]
    news = [[(small_new[k][j] if kind == "s" else big_new[k][j]).reshape(shp) for (kind, k), shp in zip(order, shapes)]
            for j in range(3)]
    return (loss, grad_x[None], *out_g, *news[0], *news[1], *news[2])
```

```python
import functools

import jax
import jax.numpy as jnp
from jax import lax
from jax.experimental import pallas as pl
from jax.experimental.pallas import tpu as pltpu

F32 = jnp.float32
BF16 = jnp.bfloat16

D_MODEL = 1024
HEAD_DIM = 64
N_HEADS = 8
N_KV_HEADS = 2
GROUP = N_HEADS // N_KV_HEADS
BLOCK = 128
ATTN_SCALE = HEAD_DIM ** -0.5
ATTN_W = N_HEADS * HEAD_DIM
KV_W = N_KV_HEADS * HEAD_DIM
CONV_W = 512
QKV_W = ATTN_W + 2 * KV_W
C3_W = 3 * CONV_W
GATES_W = 2 * D_MODEL
IN_W = QKV_W + C3_W + GATES_W
D_FF = 2816
FF2 = 2 * D_FF
NORM_EPS = 1e-5
N_CHIPS = 4
IN_SHARD = IN_W // N_CHIPS
NEG = -1e30

ADAM_LR = 0.001
ADAM_B1 = 0.9
ADAM_B2 = 0.999
ADAM_EPS = 1e-08
ADAM_WD = 0.01
ADAM_STEP = 10

VMEM_LIMIT = 56 * 1024 * 1024
MESH = pl.DeviceIdType.MESH

NT = (((1,), (1,)), ((), ()))
TN = (((0,), (0,)), ((), ()))


STREAM_VMEM_LIMIT = 32 * 1024 * 1024


def _params(*sem, limit=VMEM_LIMIT):
    return pltpu.CompilerParams(dimension_semantics=sem, vmem_limit_bytes=limit)


def _resident(shape):
    return pl.BlockSpec(shape, lambda *_: (0,) * len(shape), pipeline_mode=pl.Buffered(1))


def _sigmoid(v):
    return 0.5 * jnp.tanh(0.5 * v) + 0.5


def _rstd(v):
    return lax.rsqrt(jnp.mean(v * v, axis=-1, keepdims=True) + NORM_EPS)


def _rms_bwd(dy, v, rstd, g):
    vhat = v * rstd
    t = dy * g
    return rstd * (t - vhat * jnp.mean(t * vhat, axis=-1, keepdims=True)), dy * vhat


def _taps(z, cw):
    return cw[2:3] * z + cw[1:2] * pltpu.roll(z, 1, 0) + cw[0:1] * pltpu.roll(z, 2, 0)


def _causal_conv(z, prev, cw):
    edge = _taps(jnp.concatenate([prev, z[0:8]], axis=0), cw)
    return jnp.concatenate([edge[8:16], _taps(z, cw)[8:]], axis=0)


def _rows_after(z, nxt):
    n = z.shape[0]
    edge = jnp.concatenate([z[n - 8:n], nxt], axis=0)
    return tuple(jnp.concatenate([pltpu.roll(z, n - k, 0)[:n - 8], pltpu.roll(edge, 16 - k, 0)[0:8]], axis=0)
                 for k in (1, 2))


def _inproj_fwd(x, g1, w_in, b_in, tm, comm=None):
    s = x.shape[0]

    def body(x_ref, g_ref, w_ref, b_ref, xn_ref, qkv_ref, c3_ref, gt_ref):
        xf = x_ref[...]
        xn = (xf * _rstd(xf) * g_ref[...]).astype(BF16)
        xn_ref[...] = xn

        proj = (lax.dot_general(xn, w_ref[...], NT, preferred_element_type=F32) + b_ref[...]).astype(BF16)
        qkv_ref[...] = proj[:, :QKV_W]
        c3_ref[...] = proj[:, QKV_W:QKV_W + C3_W]
        gt_ref[...] = proj[:, QKV_W + C3_W:]

    row = lambda w: pl.BlockSpec((tm, w), lambda i: (i, 0))
    return _call(
        comm, body, name="inproj_fwd", grid=(s // tm,),
        in_specs=[row(D_MODEL), _resident((1, D_MODEL)), _resident((IN_W, D_MODEL)), _resident((1, IN_W))],
        out_specs=[row(D_MODEL), row(QKV_W), row(C3_W), row(GATES_W)],
        out_shape=[jax.ShapeDtypeStruct((s, D_MODEL), BF16), jax.ShapeDtypeStruct((s, QKV_W), BF16),
                   jax.ShapeDtypeStruct((s, C3_W), BF16), jax.ShapeDtypeStruct((s, GATES_W), BF16)],
        compiler_params=_params("parallel"),
    )(x, g1, w_in, b_in)


def _attn_bias():
    kj = jnp.arange(2 * BLOCK)[:, None]
    qi = (jnp.arange(GROUP * BLOCK) % BLOCK)[None, :]
    band = (kj > qi) & (kj <= qi + BLOCK)
    return jnp.stack([jnp.where(band & (kj >= BLOCK), 0.0, NEG), jnp.where(band, 0.0, NEG)]).astype(F32)


def _attn_bias_specs():
    shape = (None, 2 * BLOCK, GROUP * BLOCK)
    return pl.BlockSpec(shape, lambda i: (jnp.minimum(i, 1), 0, 0)), pl.BlockSpec(shape, lambda i: (1, 0, 0))


def _sink_row(sk_ref, h):
    lane = lax.broadcasted_iota(jnp.int32, (1, GROUP * BLOCK), 1)
    row = jnp.full((1, GROUP * BLOCK), sk_ref[h * GROUP], F32)
    for g in range(1, GROUP):
        row = jnp.where(lane >= g * BLOCK, sk_ref[h * GROUP + g], row)
    return row


def _stack_heads(t, h):
    return jnp.concatenate(
        [t[:, (h * GROUP + g) * HEAD_DIM:(h * GROUP + g + 1) * HEAD_DIM] for g in range(GROUP)], axis=0)


def _unstack_heads(per_kv):
    return jnp.concatenate(
        [t[g * BLOCK:(g + 1) * BLOCK] for t in per_kv for g in range(GROUP)], axis=1)


def _block_specs(n, steps):
    cur = lambda i: jnp.minimum(i, steps - 1)
    prev = lambda i: jnp.maximum(n * jnp.minimum(i, steps - 1) - 1, 0)
    kv = ATTN_W // KV_W
    return (pl.BlockSpec((n * BLOCK, ATTN_W), lambda i: (cur(i), 0)),
            pl.BlockSpec((BLOCK, KV_W), lambda i: (prev(i), kv)), pl.BlockSpec((n * BLOCK, KV_W), lambda i: (cur(i), kv)),
            pl.BlockSpec((BLOCK, KV_W), lambda i: (prev(i), kv + 1)),
            pl.BlockSpec((n * BLOCK, KV_W), lambda i: (cur(i), kv + 1)))


def _attn_fwd(qkv, sinks, comm=None):
    s = qkv.shape[0]
    n = min(4, s // BLOCK)
    steps = s // (n * BLOCK)

    def body(sk_ref, bias0_ref, bias1_ref, q_ref, kp_ref, kc_ref, vp_ref, vc_ref, o_ref):
        kc, vc = kc_ref[...], vc_ref[...]
        for b in range(n):
            rows, before = slice(b * BLOCK, (b + 1) * BLOCK), slice((b - 1) * BLOCK, b * BLOCK)
            kp, vp = (kp_ref[...], vp_ref[...]) if b == 0 else (kc[before], vc[before])
            q, bias = q_ref[rows, :], (bias0_ref if b == 0 else bias1_ref)[...]
            outs = []
            for h in range(N_KV_HEADS):
                hs = slice(h * HEAD_DIM, (h + 1) * HEAD_DIM)
                k2 = jnp.concatenate([kp[:, hs], kc[rows, hs]], axis=0)
                v2 = jnp.concatenate([vp[:, hs], vc[rows, hs]], axis=0)
                sc = lax.dot_general(k2, _stack_heads(q, h), NT, preferred_element_type=F32) * ATTN_SCALE + bias
                sink = _sink_row(sk_ref, h)
                m = jnp.maximum(jnp.max(sc, axis=0, keepdims=True), sink)
                p = jnp.exp(sc - m)
                den = jnp.sum(p, axis=0, keepdims=True) + jnp.exp(sink - m)
                out = lax.dot_general(v2, p.astype(BF16), TN, preferred_element_type=F32) / den
                outs.append(out.T)
            o_ref[rows, :] = _unstack_heads(outs).astype(BF16)

    return _call(
        comm, body, name="attn_fwd", grid=(steps,),
        in_specs=[pl.BlockSpec(memory_space=pltpu.SMEM), *_attn_bias_specs(), *_block_specs(n, steps)],
        out_specs=pl.BlockSpec((n * BLOCK, ATTN_W), lambda i: (i, 0)),
        out_shape=jax.ShapeDtypeStruct((s, ATTN_W), BF16),
        compiler_params=_params("parallel"),
    )(sinks, _attn_bias(), _attn_bias(), qkv, qkv, qkv, qkv, qkv)


def _mix_fwd(x, attn, c3, gates, conv_w, w_br, w_out, g2, tm, comm=None):
    s = x.shape[0]

    def body(x_ref, at_ref, c3_ref, gt_ref, cw_ref, wbr_ref, wo_ref, g_ref,
             conv_ref, a_ref, cv_ref, mg_ref, h1_ref, hn_ref, carry_ref):
        @pl.when(pl.program_id(0) == 0)
        def _():
            carry_ref[...] = jnp.zeros_like(carry_ref)

        c3v = c3_ref[...].astype(F32)
        cb, cc, cx = c3v[:, :CONV_W], c3v[:, CONV_W:2 * CONV_W], c3v[:, 2 * CONV_W:]
        z = cc * cx
        cz = _causal_conv(z, carry_ref[...], cw_ref[...])
        carry_ref[...] = z[tm - 8:tm]
        conv = (cb * cz).astype(BF16)
        conv_ref[...] = conv
        a = jnp.dot(at_ref[...], wbr_ref[:ATTN_W, :], preferred_element_type=F32)
        cv = jnp.dot(conv, wbr_ref[ATTN_W:, :], preferred_element_type=F32)
        a_ref[...] = a.astype(BF16)
        cv_ref[...] = cv.astype(BF16)
        gt = gt_ref[...].astype(F32)
        merged = (_sigmoid(gt[:, :D_MODEL]) * a + _sigmoid(gt[:, D_MODEL:]) * cv).astype(BF16)
        mg_ref[...] = merged
        h1 = x_ref[...] + jnp.dot(merged, wo_ref[...], preferred_element_type=F32)
        h1_ref[...] = h1
        hn_ref[...] = (h1 * _rstd(h1) * g_ref[...]).astype(BF16)

    row = lambda w: pl.BlockSpec((tm, w), lambda i: (i, 0))
    return _call(
        comm, body, name="mix_fwd", grid=(s // tm,),
        in_specs=[row(D_MODEL), row(ATTN_W), row(C3_W), row(GATES_W), _resident(conv_w.shape),
                  _resident((ATTN_W + CONV_W, D_MODEL)), _resident((D_MODEL, D_MODEL)), _resident((1, D_MODEL))],
        out_specs=[row(CONV_W), row(D_MODEL), row(D_MODEL), row(D_MODEL), row(D_MODEL), row(D_MODEL)],
        out_shape=[jax.ShapeDtypeStruct((s, CONV_W), BF16), jax.ShapeDtypeStruct((s, D_MODEL), BF16),
                   jax.ShapeDtypeStruct((s, D_MODEL), BF16), jax.ShapeDtypeStruct((s, D_MODEL), BF16),
                   jax.ShapeDtypeStruct((s, D_MODEL), F32), jax.ShapeDtypeStruct((s, D_MODEL), BF16)],
        scratch_shapes=[pltpu.VMEM((8, CONV_W), F32)],
        compiler_params=_params("arbitrary"),
    )(x, attn, c3, gates, conv_w, w_br, w_out, g2)


def _ffn_fwd_loss(hn, h1, w_up, ffn_cw, w_down, g3, target, tm):
    s = hn.shape[0]

    def body(hn_ref, h1_ref, wu_ref, cw_ref, wd_ref, g_ref, t_ref,
             u_ref, up_ref, act_ref, dh2_ref, loss_ref, gfn_ref, carry_ref):
        @pl.when(pl.program_id(0) == 0)
        def _():
            carry_ref[...] = jnp.zeros_like(carry_ref)
            loss_ref[...] = jnp.zeros_like(loss_ref)
            gfn_ref[...] = jnp.zeros_like(gfn_ref)

        u = jnp.dot(hn_ref[...], wu_ref[...], preferred_element_type=F32)
        u_ref[...] = u.astype(BF16)
        up = _causal_conv(u, carry_ref[...], cw_ref[...])
        up_ref[...] = up
        carry_ref[...] = u[tm - 8:tm]
        gate, val = up[:, :D_FF], up[:, D_FF:]
        act = (gate * _sigmoid(gate) * val).astype(BF16)
        act_ref[...] = act
        h2 = h1_ref[...] + jnp.dot(act, wd_ref[...], preferred_element_type=F32)
        rstd = _rstd(h2)
        g = g_ref[...]
        err = h2 * rstd * g - t_ref[...]
        loss_ref[...] += jnp.sum(err * err) * (0.5 / D_MODEL)
        dh2, dg = _rms_bwd(err * (1.0 / D_MODEL), h2, rstd, g)
        dh2_ref[...] = dh2
        gfn_ref[...] += jnp.sum(dg, axis=0, keepdims=True)

    row = lambda w: pl.BlockSpec((tm, w), lambda i: (i, 0))
    acc = lambda w: pl.BlockSpec((1, w), lambda i: (0, 0))
    return pl.pallas_call(
        body, name="ffn_fwd_loss", grid=(s // tm,),
        in_specs=[row(D_MODEL), row(D_MODEL), _resident((D_MODEL, FF2)), _resident(ffn_cw.shape),
                  _resident((D_FF, D_MODEL)), _resident((1, D_MODEL)), row(D_MODEL)],
        out_specs=[row(FF2), row(FF2), row(D_FF), row(D_MODEL), acc(128), acc(D_MODEL)],
        out_shape=[jax.ShapeDtypeStruct((s, FF2), BF16), jax.ShapeDtypeStruct((s, FF2), F32),
                   jax.ShapeDtypeStruct((s, D_FF), BF16),
                   jax.ShapeDtypeStruct((s, D_MODEL), F32), jax.ShapeDtypeStruct((1, 128), F32),
                   jax.ShapeDtypeStruct((1, D_MODEL), F32)],
        scratch_shapes=[pltpu.VMEM((8, FF2), F32)],
        compiler_params=_params("arbitrary"),
    )(hn, h1, w_up, ffn_cw, w_down, g3, target)


def _ffn_bwd(dh2, u, up, h1, w_up, ffn_cw, w_down, g2, tm):
    s = dh2.shape[0]
    nt = s // tm

    def body(dh2_ref, u_ref, up_ref, h1_ref, wu_ref, cw_ref, wd_ref, g_ref,
             du_ref, dh1_ref, gcw_ref, gg_ref, carry_ref):
        @pl.when(pl.program_id(0) == 0)
        def _():
            for ref in (carry_ref, gcw_ref, gg_ref):
                ref[...] = jnp.zeros_like(ref)

        dh2v = dh2_ref[...]
        dact = lax.dot_general(dh2v.astype(BF16), wd_ref[...], NT, preferred_element_type=F32)
        upv = up_ref[...]
        gate, val = upv[:, :D_FF], upv[:, D_FF:]
        sg = _sigmoid(gate)
        dval = dact * (gate * sg)
        dgate = dact * val * (sg * (1.0 + gate * (1.0 - sg)))
        dup = jnp.concatenate([dgate, dval], axis=1)
        dup1, dup2 = _rows_after(dup, carry_ref[...])
        carry_ref[...] = dup[0:8]
        u = u_ref[...].astype(F32)
        gcw_ref[2:3, :] += jnp.sum(dup * u, axis=0, keepdims=True)
        gcw_ref[1:2, :] += jnp.sum(dup1 * u, axis=0, keepdims=True)
        gcw_ref[0:1, :] += jnp.sum(dup2 * u, axis=0, keepdims=True)
        cw = cw_ref[...]
        du = (cw[2:3] * dup + cw[1:2] * dup1 + cw[0:1] * dup2).astype(BF16)
        du_ref[...] = du
        dhn = lax.dot_general(du, wu_ref[...], NT, preferred_element_type=F32)
        h1v = h1_ref[...]
        dh1, dg = _rms_bwd(dhn, h1v, _rstd(h1v), g_ref[...])
        dh1_ref[...] = dh2v + dh1
        gg_ref[...] += jnp.sum(dg, axis=0, keepdims=True)

    row = lambda w: pl.BlockSpec((tm, w), lambda i: (nt - 1 - i, 0))
    return pl.pallas_call(
        body, name="ffn_bwd", grid=(nt,),
        in_specs=[row(D_MODEL), row(FF2), row(FF2),
                  row(D_MODEL), _resident((D_MODEL, FF2)), _resident(ffn_cw.shape), _resident((D_FF, D_MODEL)),
                  _resident((1, D_MODEL))],
        out_specs=[row(FF2), row(D_MODEL), pl.BlockSpec((3, FF2), lambda i: (0, 0)),
                   pl.BlockSpec((1, D_MODEL), lambda i: (0, 0))],
        out_shape=[jax.ShapeDtypeStruct((s, FF2), BF16), jax.ShapeDtypeStruct((s, D_MODEL), F32),
                   jax.ShapeDtypeStruct((3, FF2), F32), jax.ShapeDtypeStruct((1, D_MODEL), F32)],
        scratch_shapes=[pltpu.VMEM((8, FF2), F32)],
        compiler_params=_params("arbitrary"),
    )(dh2, u, up, h1, w_up, ffn_cw, w_down, g2)


def _mix_bwd(dh1, gates, a, cv, c3, attn, conv, merged, conv_w, w_br, w_out, tm, comm=None):
    s = dh1.shape[0]
    nt = s // tm
    halo = 16

    def body(dh1_ref, gt_ref, a_ref, cv_ref, c3_ref, ch_ref, at_ref, cn_ref, mg_ref, cw_ref, wbr_ref,
             wo_ref, dat_ref, dc3_ref, dgt_ref, gcw_ref, gbr_ref, gout_ref, carry_ref, br_acc, out_acc):
        i = pl.program_id(0)

        @pl.when(i == 0)
        def _():
            for ref in (carry_ref, gcw_ref, br_acc, out_acc):
                ref[...] = jnp.zeros_like(ref)

        dh1v = dh1_ref[...].astype(BF16)
        out_acc[...] += lax.dot_general(mg_ref[...], dh1v, TN, preferred_element_type=F32)
        dm = lax.dot_general(dh1v, wo_ref[...], NT, preferred_element_type=F32)
        gt = gt_ref[...].astype(F32)
        sa, sc = _sigmoid(gt[:, :D_MODEL]), _sigmoid(gt[:, D_MODEL:])
        da = (dm * sa).astype(BF16)
        dcv = (dm * sc).astype(BF16)
        br_acc[:ATTN_W, :] += lax.dot_general(at_ref[...], da, TN, preferred_element_type=F32)
        br_acc[ATTN_W:, :] += lax.dot_general(cn_ref[...], dcv, TN, preferred_element_type=F32)
        dgt_ref[...] = jnp.concatenate(
            [dm * a_ref[...].astype(F32) * (sa * (1.0 - sa)), dm * cv_ref[...].astype(F32) * (sc * (1.0 - sc))],
            axis=1).astype(BF16)
        dat_ref[...] = lax.dot_general(da, wbr_ref[:ATTN_W, :], NT, preferred_element_type=F32).astype(BF16)
        dconv = lax.dot_general(dcv, wbr_ref[ATTN_W:, :], NT, preferred_element_type=F32)
        c3v = c3_ref[...].astype(F32)
        cb, cc, cx = c3v[:, :CONV_W], c3v[:, CONV_W:2 * CONV_W], c3v[:, 2 * CONV_W:]
        z = cc * cx
        chv = ch_ref[...].astype(F32)[halo - 8:halo] * (i < nt - 1).astype(F32)
        zh = chv[:, CONV_W:2 * CONV_W] * chv[:, 2 * CONV_W:]
        cw = cw_ref[...]
        cz = _causal_conv(z, zh, cw)
        dcz = dconv * cb
        dcz1, dcz2 = _rows_after(dcz, carry_ref[...])
        carry_ref[...] = dcz[0:8]
        gcw_ref[2:3, :] += jnp.sum(dcz * z, axis=0, keepdims=True)
        gcw_ref[1:2, :] += jnp.sum(dcz1 * z, axis=0, keepdims=True)
        gcw_ref[0:1, :] += jnp.sum(dcz2 * z, axis=0, keepdims=True)
        dz = cw[2:3] * dcz + cw[1:2] * dcz1 + cw[0:1] * dcz2
        dc3_ref[...] = jnp.concatenate([dconv * cz, dz * cx, dz * cc], axis=1).astype(BF16)

        @pl.when(i == nt - 1)
        def _():
            gbr_ref[...] = br_acc[...].astype(BF16)
            gout_ref[...] = out_acc[...].astype(BF16)

    row = lambda w: pl.BlockSpec((tm, w), lambda i: (nt - 1 - i, 0))
    return _call(
        comm, body, name="mix_bwd", grid=(nt,),
        in_specs=[row(D_MODEL), row(GATES_W), row(D_MODEL), row(D_MODEL), row(C3_W),
                  pl.BlockSpec((halo, C3_W), lambda i: (jnp.maximum((nt - 1 - i) * (tm // halo) - 1, 0), 0)),
                  row(ATTN_W), row(CONV_W), row(D_MODEL), _resident(conv_w.shape),
                  _resident((ATTN_W + CONV_W, D_MODEL)), _resident((D_MODEL, D_MODEL))],
        out_specs=[row(ATTN_W), row(C3_W), row(GATES_W), pl.BlockSpec((3, CONV_W), lambda i: (0, 0)),
                   _resident((ATTN_W + CONV_W, D_MODEL)), _resident((D_MODEL, D_MODEL))],
        out_shape=[jax.ShapeDtypeStruct((s, ATTN_W), BF16), jax.ShapeDtypeStruct((s, C3_W), BF16),
                   jax.ShapeDtypeStruct((s, GATES_W), BF16), jax.ShapeDtypeStruct((3, CONV_W), F32),
                   jax.ShapeDtypeStruct((ATTN_W + CONV_W, D_MODEL), BF16),
                   jax.ShapeDtypeStruct((D_MODEL, D_MODEL), BF16)],
        scratch_shapes=[pltpu.VMEM((8, CONV_W), F32), pltpu.VMEM((ATTN_W + CONV_W, D_MODEL), F32),
                        pltpu.VMEM((D_MODEL, D_MODEL), F32)],
        compiler_params=_params("arbitrary"),
    )(dh1, gates, a, cv, c3, c3, attn, conv, merged, conv_w, w_br, w_out)


def _attn_bwd(qkv, sinks, o, do, comm=None):
    s = qkv.shape[0]
    npair = s // (2 * BLOCK)

    def one_block(sk_ref, bias, q, kp, kc, vp, vc, ov, dov, dsk_ref):
        dqs, dks, dvs = [], [], []
        for h in range(N_KV_HEADS):
            hs = slice(h * HEAD_DIM, (h + 1) * HEAD_DIM)
            k2 = jnp.concatenate([kp[:, hs], kc[:, hs]], axis=0)
            v2 = jnp.concatenate([vp[:, hs], vc[:, hs]], axis=0)
            qg, og, dog = _stack_heads(q, h), _stack_heads(ov, h), _stack_heads(dov, h)
            sc = lax.dot_general(k2, qg, NT, preferred_element_type=F32) * ATTN_SCALE + bias
            sink = _sink_row(sk_ref, h)
            m = jnp.maximum(jnp.max(sc, axis=0, keepdims=True), sink)
            p = jnp.exp(sc - m)
            psink = jnp.exp(sink - m)
            inv = 1.0 / (jnp.sum(p, axis=0, keepdims=True) + psink)
            p = p * inv
            delta = jnp.sum(dog.astype(F32) * og.astype(F32), axis=1, keepdims=True).T
            dp = lax.dot_general(v2, dog, NT, preferred_element_type=F32)
            ds = (p * (dp - delta)).astype(BF16)
            dqs.append((lax.dot_general(k2, ds, TN, preferred_element_type=F32) * ATTN_SCALE).T)
            dks.append(jnp.dot(ds, qg, preferred_element_type=F32) * ATTN_SCALE)
            dvs.append(jnp.dot(p.astype(BF16), dog, preferred_element_type=F32))
            dsink = -(psink * inv * delta)
            for g in range(GROUP):
                r = h * GROUP + g
                dsk_ref[r:r + 1, :] += jnp.sum(dsink[:, g * BLOCK:(g + 1) * BLOCK])
        return _unstack_heads(dqs), jnp.concatenate(dks, axis=1), jnp.concatenate(dvs, axis=1)

    def body(sk_ref, bias0_ref, bias1_ref, q_ref, kp_ref, kc_ref, vp_ref, vc_ref, o_ref, do_ref,
             dq_ref, dke_ref, dko_ref, dve_ref, dvo_ref, dsk_ref, ck_ref, cvv_ref):
        i = pl.program_id(0)

        @pl.when(i == 0)
        def _():
            for ref in (ck_ref, cvv_ref, dsk_ref):
                ref[...] = jnp.zeros_like(ref)

        @pl.when(i < npair)
        def _():
            kc, vc = kc_ref[...], vc_ref[...]
            first, second = slice(0, BLOCK), slice(BLOCK, 2 * BLOCK)
            dq0, dk0, dv0 = one_block(sk_ref, bias0_ref[...], q_ref[first, :], kp_ref[...], kc[first], vp_ref[...],
                                      vc[first], o_ref[first, :], do_ref[first, :], dsk_ref)
            dq1, dk1, dv1 = one_block(sk_ref, bias1_ref[...], q_ref[second, :], kc[first], kc[second], vc[first],
                                      vc[second], o_ref[second, :], do_ref[second, :], dsk_ref)
            dq_ref[first, :] = dq0.astype(BF16)
            dq_ref[second, :] = dq1.astype(BF16)
            dko_ref[...] = (ck_ref[...] + dk0[:BLOCK]).astype(BF16)
            dvo_ref[...] = (cvv_ref[...] + dv0[:BLOCK]).astype(BF16)
            dke_ref[...] = (dk0[BLOCK:] + dk1[:BLOCK]).astype(BF16)
            dve_ref[...] = (dv0[BLOCK:] + dv1[:BLOCK]).astype(BF16)
            ck_ref[...] = dk1[BLOCK:]
            cvv_ref[...] = dv1[BLOCK:]

        @pl.when(i == npair)
        def _():
            dko_ref[...] = ck_ref[...].astype(BF16)
            dvo_ref[...] = cvv_ref[...].astype(BF16)

    cur = lambda i: jnp.minimum(i, npair - 1)
    done = lambda i: jnp.maximum(i - 1, 0)
    rows = pl.BlockSpec((2 * BLOCK, ATTN_W), lambda i: (cur(i), 0))
    even = pl.BlockSpec((BLOCK, KV_W), lambda i: (cur(i), 0))
    odd = pl.BlockSpec((BLOCK, KV_W), lambda i: (done(i), 0))
    half = jax.ShapeDtypeStruct((s // 2, KV_W), BF16)
    return _call(
        comm, body, name="attn_bwd", grid=(npair + 1,),
        in_specs=[pl.BlockSpec(memory_space=pltpu.SMEM), *_attn_bias_specs(), *_block_specs(2, npair), rows, rows],
        out_specs=[rows, even, odd, even, odd, pl.BlockSpec((N_HEADS, 128), lambda i: (0, 0))],
        out_shape=[jax.ShapeDtypeStruct((s, ATTN_W), BF16), half, half, half, half,
                   jax.ShapeDtypeStruct((N_HEADS, 128), F32)],
        scratch_shapes=[pltpu.VMEM((BLOCK, KV_W), F32), pltpu.VMEM((BLOCK, KV_W), F32)],
        compiler_params=_params("arbitrary"),
    )(sinks, _attn_bias(), _attn_bias(), qkv, qkv, qkv, qkv, qkv, o, do)


def _inproj_bwd(dq, dk, dv, dc3, dgt, w_in, x, xn, dh1, g1):
    s = x.shape[0]
    tm = min(2 * BLOCK, s)
    nt = s // tm

    def body(dq_ref, dke_ref, dko_ref, dve_ref, dvo_ref, dc3_ref, dgt_ref, w_ref, x_ref, xn_ref, dh1_ref, g_ref,
             dx_ref, gw_ref, gb_ref, gg_ref, acc_ref):
        i = pl.program_id(0)

        @pl.when(i == 0)
        def _():
            for ref in (gb_ref, gg_ref, acc_ref):
                ref[...] = jnp.zeros_like(ref)

        dk = jnp.concatenate([dke_ref[...], dko_ref[...]], axis=0)
        dv = jnp.concatenate([dve_ref[...], dvo_ref[...]], axis=0)
        dp = jnp.concatenate([dq_ref[...], dk, dv, dc3_ref[...], dgt_ref[...]], axis=1)
        acc_ref[...] += lax.dot_general(dp, xn_ref[...], TN, preferred_element_type=F32)
        gb_ref[...] += jnp.sum(dp.astype(F32), axis=0, keepdims=True)
        dxn = jnp.dot(dp, w_ref[...], preferred_element_type=F32)
        xf = x_ref[...]
        dx, dg = _rms_bwd(dxn, xf, _rstd(xf), g_ref[...])
        dx_ref[...] = dh1_ref[...] + dx
        gg_ref[...] += jnp.sum(dg, axis=0, keepdims=True)

        @pl.when(i == nt - 1)
        def _():
            gw_ref[...] = acc_ref[...].astype(BF16)

    row = lambda w: pl.BlockSpec((tm, w), lambda i: (i, 0))
    acc = lambda w: pl.BlockSpec((1, w), lambda i: (0, 0))
    block = pl.BlockSpec((tm // 2, KV_W), lambda i: (i, 0))
    return pl.pallas_call(
        body, name="inproj_bwd", grid=(nt,),
        in_specs=[row(ATTN_W), block, block, block, block, row(C3_W), row(GATES_W), _resident((IN_W, D_MODEL)),
                  row(D_MODEL), row(D_MODEL), row(D_MODEL), _resident((1, D_MODEL))],
        out_specs=[row(D_MODEL), _resident((IN_W, D_MODEL)), acc(IN_W), acc(D_MODEL)],
        out_shape=[jax.ShapeDtypeStruct((s, D_MODEL), F32), jax.ShapeDtypeStruct((IN_W, D_MODEL), BF16),
                   jax.ShapeDtypeStruct((1, IN_W), F32), jax.ShapeDtypeStruct((1, D_MODEL), F32)],
        scratch_shapes=[pltpu.VMEM((IN_W, D_MODEL), F32)],
        compiler_params=_params("arbitrary"),
    )(dq, *dk, *dv, dc3, dgt, w_in, x, xn, dh1, g1)


def _wgrad(a, b, bm, bn, bk, name, comm=None):
    s, m = a.shape
    n = b.shape[1]
    nk = s // bk

    def body(a_ref, b_ref, o_ref, acc_ref):
        k = pl.program_id(2)

        @pl.when(k == 0)
        def _():
            acc_ref[...] = jnp.zeros_like(acc_ref)

        acc_ref[...] += lax.dot_general(a_ref[...].astype(BF16), b_ref[...].astype(BF16), TN,
                                        preferred_element_type=F32)

        @pl.when(k == nk - 1)
        def _():
            o_ref[...] = acc_ref[...].astype(BF16)

    return _call(
        comm, body, name=name, grid=(m // bm, n // bn, nk),
        in_specs=[pl.BlockSpec((bk, bm), lambda i, j, k: (k, i)), pl.BlockSpec((bk, bn), lambda i, j, k: (k, j))],
        out_specs=pl.BlockSpec((bm, bn), lambda i, j, k: (i, j)),
        out_shape=jax.ShapeDtypeStruct((m, n), BF16),
        scratch_shapes=[pltpu.VMEM((bm, bn), F32)],
        compiler_params=_params("parallel", "parallel", "arbitrary"),
    )(a, b)


class _Carry:
    def __init__(self, jobs, reads=None, bufs=None, fresh=None):
        self.jobs, self.reads, self.bufs, self.fresh = jobs, reads or {}, bufs or {}, fresh or {}
        self.out = {}


class _Job:
    def __init__(self, n_sems, plan):
        self.n_sems, self.plan = n_sems, plan


def _plan_all(jobs, hbm, send, recv):
    pos = _position()
    starts, waits, base = [], [], 0
    for job in jobs:
        s, w = job.plan(hbm, pos, send, recv, base)
        starts, waits, base = starts + s, waits + w, base + job.n_sems
    return starts, waits


def _call(comm, body, **kw):
    if comm is None:
        return pl.pallas_call(body, **kw)
    grid = kw["grid"]
    single = not isinstance(kw["out_shape"], (list, tuple))
    out_shape = [kw["out_shape"]] if single else list(kw["out_shape"])
    out_specs = [kw["out_specs"]] if single else list(kw["out_specs"])
    in_specs = list(kw["in_specs"])
    scratch = list(kw.get("scratch_shapes", ()))
    r_names, b_names, f_names = list(comm.reads), list(comm.bufs), list(comm.fresh)
    n_args, n_out, n_scr = len(in_specs), len(out_shape), len(scratch)
    n_sems = sum(j.n_sems for j in comm.jobs)

    def wrapped(*refs):
        k = n_args
        hbm = dict(zip(r_names, refs[k:k + len(r_names)]))
        k += len(r_names) + len(b_names)
        outs = refs[k:k + n_out]
        k += n_out
        hbm.update(zip(b_names + f_names, refs[k:k + len(b_names) + len(f_names)]))
        k += len(b_names) + len(f_names)
        send, recv = refs[k + n_scr:]
        starts, waits = _plan_all(comm.jobs, hbm, send, recv)
        ids = [pl.program_id(a) for a in range(len(grid))]
        first = functools.reduce(jnp.logical_and, [i == 0 for i in ids])
        last = functools.reduce(jnp.logical_and, [i == g - 1 for i, g in zip(ids, grid)])

        @pl.when(first)
        def _():
            for cp in starts:
                cp.start()

        body(*refs[:n_args], *outs, *refs[k:k + n_scr])

        @pl.when(last)
        def _():
            for cp in waits:
                cp.wait_recv()
            for cp in starts:
                cp.wait_send()

    sems = pltpu.SemaphoreType.DMA((n_sems,))
    held = [jax.ShapeDtypeStruct(a.shape, a.dtype) for a in comm.bufs.values()] + list(comm.fresh.values())
    call = pl.pallas_call(
        wrapped, name=kw["name"], grid=grid,
        in_specs=in_specs + [_ANY] * (len(r_names) + len(b_names)),
        out_specs=out_specs + [_ANY] * len(held),
        out_shape=out_shape + held,
        input_output_aliases={n_args + len(r_names) + i: n_out + i for i in range(len(b_names))},
        scratch_shapes=scratch + [sems, sems],
        compiler_params=_params(*["arbitrary"] * len(grid)),
    )

    def run(*args):
        res = call(*args, *comm.reads.values(), *comm.bufs.values())
        comm.out = dict(zip(b_names + f_names, res[n_out:]))
        return res[0] if single else res[:n_out]

    return run


def _exchange(name, phases, reads=None, bufs=None, fresh=None):
    comm = _Carry([j for ph in phases for j in ph], reads, bufs, fresh)
    r_names, b_names, f_names = list(comm.reads), list(comm.bufs), list(comm.fresh)
    n_sems = sum(j.n_sems for j in comm.jobs)

    def body(*refs):
        hbm = dict(zip(r_names, refs[:len(r_names)]))
        k = len(r_names) + len(b_names)
        hbm.update(zip(b_names + f_names, refs[k:k + len(b_names) + len(f_names)]))
        send, recv = refs[-2:]
        pos = _position()
        started, base = [], 0
        for ph in phases:
            waits = []
            for job in ph:
                s, w = job.plan(hbm, pos, send, recv, base)
                base += job.n_sems
                for cp in s:
                    cp.start()
                started, waits = started + s, waits + w
            for cp in waits:
                cp.wait_recv()
        for cp in started:
            cp.wait_send()

    sems = pltpu.SemaphoreType.DMA((n_sems,))
    held = [jax.ShapeDtypeStruct(a.shape, a.dtype) for a in comm.bufs.values()] + list(comm.fresh.values())
    res = pl.pallas_call(
        body, name=name, in_specs=[_ANY] * (len(r_names) + len(b_names)), out_specs=[_ANY] * len(held),
        out_shape=held, input_output_aliases={len(r_names) + i: i for i in range(len(b_names))},
        scratch_shapes=[sems, sems],
    )(*comm.reads.values(), *comm.bufs.values())
    return dict(zip(b_names + f_names, res))


_HBM = pl.BlockSpec(memory_space=pltpu.HBM)
_SEM = pl.BlockSpec(memory_space=pltpu.SEMAPHORE)
_EFFECT = pltpu.SideEffectType.DATAFLOW_SIDE_EFFECTING


def _start_exchanges(name, groups):
    names = [list(arrays) for _, arrays in groups]
    first = [sum(len(ns) for ns in names[:g]) for g in range(len(groups))]
    n, ng = sum(len(ns) for ns in names), len(groups)

    def body(*refs):
        for g, (jobs, _) in enumerate(groups):
            hbm = dict(zip(names[g], refs[first[g]:first[g] + len(names[g])]))
            for cp in _plan_all(jobs, hbm, refs[n + 2 * g], refs[n + 2 * g + 1])[0]:
                cp.start()
        refs[-1][...] = jnp.zeros_like(refs[-1])

    given = [pltpu.with_memory_space_constraint(
        a if isinstance(a, jax.Array) else lax.empty(a.shape, a.dtype), pltpu.HBM)
        for _, arrays in groups for a in arrays.values()]
    sems = [pltpu.SemaphoreType.DMA((sum(j.n_sems for j in jobs),)) for jobs, _ in groups for _ in range(2)]
    res = pl.pallas_call(
        body, name=name,
        out_shape=(*sems, *[pltpu.HBM(a.shape, a.dtype) for a in given], jax.ShapeDtypeStruct((8, 128), F32)),
        in_specs=[_HBM] * n, out_specs=(*[_SEM] * (2 * ng), *[_HBM] * n, pl.BlockSpec(memory_space=pltpu.VMEM)),
        input_output_aliases={i: 2 * ng + i for i in range(n)},
        compiler_params=pltpu.CompilerParams(has_side_effects=_EFFECT),
    )(*given)
    held = res[2 * ng:2 * ng + n]
    states = [(names[g], groups[g][0], res[2 * g], res[2 * g + 1], held[first[g]:first[g] + len(names[g])])
              for g in range(ng)]
    return states, res[-1]


def _start_exchange(name, jobs, arrays):
    states, token = _start_exchanges(name, [(jobs, arrays)])
    return states[0], token


def _finish_exchange(name, state, after):
    names, jobs, send_sem, recv_sem, held = state
    n = len(names)

    def body(*refs):
        hbm = dict(zip(names, refs[:n]))
        send, recv = refs[n:n + 2]
        starts, waits = _plan_all(jobs, hbm, send, recv)
        for cp in waits:
            cp.wait_recv()
        for cp in starts:
            cp.wait_send()

    res = pl.pallas_call(
        body, name=name, out_shape=tuple(pltpu.HBM(a.shape, a.dtype) for a in held),
        in_specs=[_HBM] * n + [_SEM, _SEM, _ANY], out_specs=tuple([_HBM] * n),
        input_output_aliases={i: i for i in range(n)},
        compiler_params=pltpu.CompilerParams(has_side_effects=_EFFECT),
    )(*held, send_sem, recv_sem, after)
    return dict(zip(names, res))


def _relay_exchange(name, state, jobs, after):
    names, arrived, send_sem, recv_sem, held = state
    n = len(names)

    def body(*refs):
        hbm = dict(zip(names, refs[:n]))
        starts, waits = _plan_all(arrived, hbm, refs[n], refs[n + 1])
        for cp in waits:
            cp.wait_recv()
        for cp in starts:
            cp.wait_send()
        for cp in _plan_all(jobs, hbm, refs[n + 3], refs[n + 4])[0]:
            cp.start()
        refs[-1][...] = jnp.zeros_like(refs[-1])

    sems = pltpu.SemaphoreType.DMA((sum(j.n_sems for j in jobs),))
    res = pl.pallas_call(
        body, name=name,
        out_shape=(sems, sems, *[pltpu.HBM(a.shape, a.dtype) for a in held], jax.ShapeDtypeStruct((8, 128), F32)),
        in_specs=[_HBM] * n + [_SEM, _SEM, _ANY],
        out_specs=(_SEM, _SEM, *[_HBM] * n, pl.BlockSpec(memory_space=pltpu.VMEM)),
        input_output_aliases={i: 2 + i for i in range(n)},
        compiler_params=pltpu.CompilerParams(has_side_effects=_EFFECT),
    )(*held, send_sem, recv_sem, after)
    return (names, jobs, res[0], res[1], res[2:2 + n]), res[-1]


def _row_tile(rows, bytes_per_row):
    best = 16
    for t in range(16, rows + 1, 16):
        if rows % t == 0 and t * bytes_per_row <= 9 * 1024 * 1024:
            best = t
    return best


def _rowwise(fn, ins, out_dtypes, name, after=None):
    rows, cols = ins[0].shape
    per_row = sum(cols * a.dtype.itemsize for a in ins) + sum(cols * jnp.dtype(d).itemsize for d in out_dtypes)
    tr = _row_tile(rows, per_row)
    n_in = len(ins)

    def body(*refs):
        outs = fn(*[r[...] for r in refs[:n_in]])
        for o_ref, o in zip(refs[-len(out_dtypes):], outs):
            o_ref[...] = o.astype(o_ref.dtype)

    tile = pl.BlockSpec((tr, cols), lambda i: (i, 0))
    behind = [] if after is None else [after]
    return pl.pallas_call(
        body, name=name, grid=(rows // tr,),
        in_specs=[tile] * n_in + [pl.BlockSpec((8, 128), lambda i: (0, 0))] * len(behind),
        out_specs=[tile] * len(out_dtypes),
        out_shape=[jax.ShapeDtypeStruct((rows, cols), d) for d in out_dtypes],
        compiler_params=_params("parallel", limit=STREAM_VMEM_LIMIT),
    )(*ins, *behind)


def _tiled(fn, name, grid, pos, ins, outs):
    n_in = len(ins)

    def body(pos_ref, *refs):
        res = fn(*[r[...] for r in refs[:n_in]])
        for o_ref, o in zip(refs[n_in:], res):
            o_ref[...] = o.astype(o_ref.dtype)

    return pl.pallas_call(
        body, name=name,
        grid_spec=pltpu.PrefetchScalarGridSpec(
            num_scalar_prefetch=1, grid=grid,
            in_specs=[pl.BlockSpec(bs, im) for _, bs, im in ins],
            out_specs=[pl.BlockSpec(bs, im) for _, _, bs, im in outs]),
        out_shape=[jax.ShapeDtypeStruct(s, d) for s, d, _, _ in outs],
        compiler_params=_params("parallel", limit=STREAM_VMEM_LIMIT),
    )(pos, *[a for a, _, _ in ins])


def _adamw(w, g, m, v):
    m = ADAM_B1 * m + (1.0 - ADAM_B1) * g
    v = ADAM_B2 * v + (1.0 - ADAM_B2) * (g * g)
    m_hat = m / (1.0 - ADAM_B1 ** ADAM_STEP)
    v_hat = v / (1.0 - ADAM_B2 ** ADAM_STEP)
    return -ADAM_LR * (m_hat / (jnp.sqrt(v_hat) + ADAM_EPS) + ADAM_WD * w), m, v


def _adamw_small(pos, own, slots, params):
    n = len(params)

    def body(pos_ref, own_ref, slots_ref, *refs):
        ins, outs, total_ref = refs[:3 * n], refs[3 * n:-1], refs[-1]
        chip = pos_ref[0]
        idx = 2 * chip + pos_ref[1]
        term = lambda q: jnp.where(idx == q, own_ref[...], slots_ref[q])
        acc = term(0)
        for q in range(1, N_DEV):
            acc = acc + term(q)
        total_ref[...] = acc
        outs[0][...] = total_ref[0:1, :]
        for k, (w, _, _, row) in enumerate(params):
            width = min(w.shape[-1], 128)
            for t in range(w.shape[0]):
                for j in range(w.shape[-1] // width):
                    lanes = slice(j * width, (j + 1) * width)
                    at = (slice(t, t + 1), lanes) if w.ndim == 2 else (t, slice(None), lanes)
                    g = total_ref[pl.ds(row(t, j, chip), 1), :][:, :width]
                    new = _adamw(ins[3 * k][at], g, ins[3 * k + 1][at], ins[3 * k + 2][at])
                    for o_ref, o in zip(outs[1 + 4 * k:5 + 4 * k], (g, *new)):
                        o_ref[at] = o

    vmem = pl.BlockSpec(memory_space=pltpu.VMEM)
    return pl.pallas_call(
        body, name="adamw_small",
        in_specs=[pl.BlockSpec(memory_space=pltpu.SMEM)] + [vmem] * (2 + 3 * n),
        out_shape=[jax.ShapeDtypeStruct((1, 128), F32)]
        + [jax.ShapeDtypeStruct(p[0].shape, F32) for p in params for _ in range(4)],
        scratch_shapes=[pltpu.VMEM(own.shape, F32)],
    )(pos, own, slots, *[a for p in params for a in p[:3]])


class _Layout:
    def __init__(self, rows, cols, stacked):
        self.rows, self.cols, self.stacked = rows, cols, stacked

    def whole(self, rows=None):
        r = self.rows if rows is None else rows
        return (N_CHIPS, r, self.cols) if self.stacked else (r, N_CHIPS * self.cols)

    def part_rows(self, h, q=0, nq=1):
        n = self.rows // 2 // nq
        return pl.ds(pl.multiple_of(h * (self.rows // 2) + q * n, 16), n)

    def half_rows(self, h):
        return self.part_rows(h)

    def block(self, ref, p, rows=slice(None)):
        if self.stacked:
            return ref.at[p, rows, :]
        return ref.at[rows, pl.ds(pl.multiple_of(p * self.cols, 128), self.cols)]

    def all_chips(self, ref, rows):
        return ref.at[:, rows, :] if self.stacked else ref.at[rows, :]


BIG = (
    _Layout(IN_SHARD, D_MODEL, True),
    _Layout(ATTN_W + CONV_W, D_MODEL // N_CHIPS, False),
    _Layout(D_MODEL // N_CHIPS, D_MODEL, True),
    _Layout(D_MODEL, FF2 // N_CHIPS, False),
    _Layout(D_FF // N_CHIPS, D_MODEL, True),
)
N_BIG = len(BIG)
_ANY = pl.BlockSpec(memory_space=pl.ANY)


def _position():
    x, y, c = lax.axis_index("x"), lax.axis_index("y"), lax.axis_index("c")
    return x, y, c, 2 * x + y


def _core_of_chip(p, c):
    return (p >> 1, p & 1, c)


def _place_cast(shard, lay, pos, name, after=None):
    rows, cols = shard.shape
    tr = _row_tile(rows, cols * 6)
    if lay.stacked:
        out = (lay.whole(), BF16, (None, tr, cols), lambda i, pos: (pos[0], i, 0))
    else:
        out = (lay.whole(), BF16, (tr, cols), lambda i, pos: (i, pos[0]))
    ins = [(shard, (tr, cols), lambda i, pos: (i, 0))]
    if after is not None:
        ins.append((after, (8, 128), lambda i, pos: (0, 0)))
    return _tiled(lambda a, *_: (a,), name, (rows // tr,), pos, ins, [out])[0]


def _place_cast_pair(top, bottom, lay, pos, name, after=None):
    rows, cols = top.shape
    ins = [(top, (rows, cols), lambda i, pos: (0, 0)), (bottom, (rows, cols), lambda i, pos: (0, 0))]
    if after is not None:
        ins.append((after, (8, 128), lambda i, pos: (0, 0)))
    return _tiled(lambda a, b, *_: (jnp.concatenate([a, b], axis=0),), name, (1,), pos, ins,
                  [(lay.whole(), BF16, (2 * rows, cols), lambda i, pos: (0, pos[0]))])[0]


def _adamw_pair(top, bottom, g, after=None):
    rows = top[0].shape[0]

    def body(*refs):
        (wa, ma, va, wb, mb, vb, g_ref), outs = refs[:7], refs[-8:]
        for (w, m, v), gg, o in (((wa, ma, va), g_ref[:rows], outs[:4]), ((wb, mb, vb), g_ref[rows:], outs[4:])):
            for o_ref, val in zip(o, (gg, *_adamw(w[...], gg, m[...], v[...]))):
                o_ref[...] = val

    behind = [] if after is None else [after]
    res = pl.pallas_call(
        body, name="adamw_w_br", out_shape=[jax.ShapeDtypeStruct(top[0].shape, F32)] * 8,
        in_specs=[pl.BlockSpec(memory_space=pltpu.VMEM)] * 7 + [_ANY] * len(behind),
    )(*top, *bottom, g, *behind)
    return res[:4], res[4:]


def _remote(src, dst, send, recv, k, device):
    return pltpu.make_async_remote_copy(src_ref=src, dst_ref=dst, send_sem=send.at[k], recv_sem=recv.at[k],
                                        device_id=device, device_id_type=MESH)


def _arrival(dst, send, recv, k, me):
    return _remote(dst, dst, send, recv, k, me)


def _gather_ici(lay, name, q=0, nq=1):
    def plan(hbm, pos, send, recv, base):
        x, y, c, me = pos
        rows = lay.part_rows(c, q, nq)
        mine = lay.block(hbm[name], me, rows)
        starts = [_remote(mine, mine, send, recv, base + d - 1, _core_of_chip(me ^ d, c)) for d in (1, 2, 3)]
        waits = [_arrival(lay.block(hbm[name], me ^ d, rows), send, recv, base + d - 1, (x, y, c)) for d in (1, 2, 3)]
        return starts, waits
    return _Job(3, plan)


def _gather_near(lay, name):
    def plan(hbm, pos, send, recv, base):
        x, y, c, me = pos
        rows = lay.part_rows(c)
        mine = lay.block(hbm[name], me, rows)
        starts = [_remote(mine, mine, send, recv, base + d - 1, _core_of_chip(me ^ d, c)) for d in (1, 2)]
        waits = [_arrival(lay.block(hbm[name], me ^ d, rows), send, recv, base + d - 1, (x, y, c)) for d in (1, 2)]
        return starts, waits
    return _Job(2, plan)


def _gather_far(lay, name):
    def plan(hbm, pos, send, recv, base):
        x, y, c, me = pos
        starts, waits = [], []
        for q, d in ((0, 1), (1, 2)):
            got = lay.block(hbm[name], me ^ (3 - d), lay.part_rows(c, q, 2))
            starts.append(_remote(got, got, send, recv, base + q, _core_of_chip(me ^ d, c)))
            waits.append(_arrival(lay.block(hbm[name], me ^ 3, lay.part_rows(c, q, 2)), send, recv, base + q, (x, y, c)))
        return starts, waits
    return _Job(2, plan)


def _gather_d2d(lay, name, q=0, nq=1, chips=(1, 2, 3)):
    def plan(hbm, pos, send, recv, base):
        x, y, c, me = pos
        starts, waits = [], []
        for k, d in enumerate(chips):
            got = lay.block(hbm[name], me ^ d, lay.part_rows(c, q, nq))
            starts.append(_remote(got, got, send, recv, base + k, (x, y, 1 - c)))
            waits.append(_arrival(lay.block(hbm[name], me ^ d, lay.part_rows(1 - c, q, nq)), send, recv, base + k,
                                  (x, y, c)))
        return starts, waits
    return _Job(len(chips), plan)


def _rs_pair(lay, grad, theirs):
    def plan(hbm, pos, send, recv, base):
        x, y, c, _ = pos
        out = _remote(lay.all_chips(hbm[grad], lay.half_rows(1 - c)), hbm[theirs], send, recv, base, (x, y, 1 - c))
        return [out], [_arrival(hbm[theirs], send, recv, base, (x, y, c))]
    return _Job(1, plan)


def _rs_chips(lay, sums, slots):
    def plan(hbm, pos, send, recv, base):
        x, y, c, me = pos
        starts = [_remote(lay.block(hbm[sums], me ^ d), hbm[slots].at[me], send, recv, base + d - 1,
                          _core_of_chip(me ^ d, c)) for d in (1, 2, 3)]
        waits = [_arrival(hbm[slots].at[me ^ d], send, recv, base + d - 1, (x, y, c)) for d in (1, 2, 3)]
        return starts, waits
    return _Job(3, plan)


def _rs_share(lay, shard):
    def plan(hbm, pos, send, recv, base):
        x, y, c, _ = pos
        mine = hbm[shard].at[lay.half_rows(c), :]
        other = hbm[shard].at[lay.half_rows(1 - c), :]
        return [_remote(mine, mine, send, recv, base, (x, y, 1 - c))], [_arrival(other, send, recv, base, (x, y, c))]
    return _Job(1, plan)


def _slots_shape(lay):
    return jax.ShapeDtypeStruct((N_CHIPS, lay.rows // 2, lay.cols), BF16)


def _theirs_shape(lay, dtype=BF16):
    return jax.ShapeDtypeStruct(lay.whole(lay.rows // 2), dtype)


def _pair_sum(grad, theirs, lay, pos, name):
    half = lay.rows // 2
    add = lambda a, b: (a.astype(F32) + b.astype(F32),)
    if lay.stacked:
        tr = _row_tile(half, lay.cols * 6)
        nt = half // tr
        flat = lambda a: a.reshape(-1, lay.cols)
        mine = lambda t, pos: ((t // nt) * (2 * nt) + pos[1] * nt + t % nt, 0)
        grid, blk = (N_CHIPS * nt,), (tr, lay.cols)
        grad, theirs = flat(grad), flat(theirs)
    else:
        tr = _row_tile(half, N_CHIPS * lay.cols * 6)
        nt = half // tr
        mine = lambda t, pos: (pos[1] * nt + t, 0)
        grid, blk = (nt,), (tr, N_CHIPS * lay.cols)
    same = lambda t, pos: (t, 0)
    out = _tiled(add, name, grid, pos, [(grad, blk, mine), (theirs, blk, same)], [(theirs.shape, BF16, blk, same)])[0]
    return out.reshape(lay.whole(half))


def _chip_sums(items, pos, name, after=None):
    ins, outs = [], []
    for sums, slots, lay in items:
        half = lay.rows // 2
        blk3 = (None, half, lay.cols)
        if lay.stacked:
            own = (sums, blk3, lambda i, pos: (pos[0], 0, 0))
        else:
            own = (sums, (half, lay.cols), lambda i, pos: (0, pos[0]))
        ins += [own] + [(slots, blk3, functools.partial(lambda d, i, pos: (pos[0] ^ d, 0, 0), d)) for d in (1, 2, 3)]
        outs.append(((lay.rows, lay.cols), F32, (half, lay.cols), lambda i, pos: (pos[1], 0)))

    def add(*vals):
        v = [a.astype(F32) for a in vals[:4 * len(items)]]
        return tuple(((v[4 * k] + v[4 * k + 1]) + v[4 * k + 2]) + v[4 * k + 3] for k in range(len(items)))

    if after is not None:
        ins.append((after, (8, 128), lambda i, pos: (0, 0)))
    return _tiled(add, name, (1,), pos, ins, outs)


N_DEV = 8


def _to_all(src, slots):
    def plan(hbm, pos, send, recv, base):
        x, y, c, _ = pos
        idx = 4 * x + 2 * y + c
        starts = [_remote(hbm[src], hbm[slots].at[idx], send, recv, base + k - 1,
                          (x ^ (k >> 2), y ^ ((k >> 1) & 1), c ^ (k & 1))) for k in range(1, N_DEV)]
        waits = [_arrival(hbm[slots].at[idx ^ k], send, recv, base + k - 1, (x, y, c)) for k in range(1, N_DEV)]
        return starts, waits
    return _Job(N_DEV - 1, plan)


def _taps_gather(name, cols):
    def plan(hbm, pos, send, recv, base):
        x, y, c, me = pos
        block = lambda p: hbm[name].at[:, pl.ds(pl.multiple_of(p * cols, 128), cols)]
        starts = [_remote(block(me), block(me), send, recv, base + d - 1, _core_of_chip(me ^ d, c)) for d in (1, 2, 3)]
        waits = [_arrival(block(me ^ d), send, recv, base + d - 1, (x, y, c)) for d in (1, 2, 3)]
        return starts, waits
    return _Job(3, plan)


def _pack_rows(parts):
    padded = [jnp.pad(a, ((0, -a.shape[0] % 8), (0, 0))) for a in parts]
    starts = [sum(p.shape[0] for p in padded[:k]) for k in range(len(padded))]
    return jnp.concatenate(padded, axis=0), starts


def kernel(x, mix_norm, w_in, b_in, sinks, conv_w, w_attn_branch, w_conv_branch, w_out, ffn_norm, w_up, ffn_conv_w, w_down, final_norm, loss_target, m_mix_norm, m_w_in, m_b_in, m_sinks, m_conv_w, m_w_attn_branch, m_w_conv_branch, m_w_out, m_ffn_norm, m_w_up, m_ffn_conv_w, m_w_down, m_final_norm, v_mix_norm, v_w_in, v_b_in, v_sinks, v_conv_w, v_w_attn_branch, v_w_conv_branch, v_w_out, v_ffn_norm, v_w_up, v_ffn_conv_w, v_w_down, v_final_norm):
    me = 2 * lax.axis_index("x") + lax.axis_index("y")
    names = ("w_in", "w_br", "w_out", "w_up", "w_down")
    w_of = dict(w_in=w_in[0].T, w_out=w_out[0], w_up=w_up[0], w_down=w_down[0])
    m_of = dict(w_in=m_w_in[0].T, w_out=m_w_out[0], w_up=m_w_up[0], w_down=m_w_down[0])
    v_of = dict(w_in=v_w_in[0].T, w_out=v_w_out[0], w_up=v_w_up[0], w_down=v_w_down[0])
    ab = (w_attn_branch[0], m_w_attn_branch[0], v_w_attn_branch[0])
    cb = (w_conv_branch[0], m_w_conv_branch[0], v_w_conv_branch[0])

    pos = jnp.stack([me, lax.axis_index("c")]).astype(jnp.int32)

    lay = dict(zip(names, BIG))
    xs, target, sk = x[0], loss_target[0], sinks[0]
    s = xs.shape[0]
    tm, tm2, bk, bk2 = min(256, s), min(512, s), min(1024, s), min(2048, s)

    placed = {"w_in": _place_cast(w_of["w_in"], lay["w_in"], pos, "cast_w_in")}
    fly_in, started = _start_exchange("gather_in_start", [_gather_near(lay["w_in"], "w_in")], {"w_in": placed["w_in"]})
    whole = lambda a: jnp.tile(jnp.pad(a[0], ((0, 5), (0, 0))), (1, N_CHIPS)) + started[0:1, 0:1]
    taps_flight, started = _start_exchange(
        "taps_start", [_taps_gather("conv", CONV_W // N_CHIPS), _taps_gather("ffn", FF2 // N_CHIPS)],
        {"conv": whole(conv_w), "ffn": whole(ffn_conv_w)})
    placed["w_br"] = _place_cast_pair(ab[0], cb[0], lay["w_br"], pos, "cast_w_br", after=started)
    for n in names[2:]:
        placed[n] = _place_cast(w_of[n], lay[n], pos, "cast_" + n, after=started)
    trio = ("w_br", "w_out")
    fly_in, started = _relay_exchange(
        "gather_in_relay", fly_in, [_gather_far(lay["w_in"], "w_in"), _gather_d2d(lay["w_in"], "w_in", chips=(1, 2))],
        after=placed["w_down"])
    (fly_trio, fly_up, fly_down), started = _start_exchanges("gather_rest_start", [
        ([_gather_ici(lay[n], n) for n in ws], {**{n: placed[n] for n in ws}, **behind})
        for ws, behind in ((trio, {"behind": started}), (("w_up",), {}), (("w_down",), {}))])

    got = _finish_exchange("gather_in_wait", fly_in, after=started)
    w_in_full = _exchange("gather_in_d2d", [[_gather_d2d(lay["w_in"], "w_in", chips=(3,))]],
                          bufs=got)["w_in"].reshape(IN_W, D_MODEL)
    xn, qkv, c3, gates = _inproj_fwd(xs, mix_norm, w_in_full, b_in, tm2)
    got = _finish_exchange("gather_trio_wait", fly_trio, after=qkv)
    k2 = _Carry([_gather_d2d(lay[n], n) for n in trio], bufs={n: got[n] for n in trio})
    attn = _attn_fwd(qkv, sk, comm=k2)
    w_br = k2.out["w_br"]
    w_out_full = k2.out["w_out"].reshape(D_MODEL, D_MODEL)
    k3 = _Carry([_gather_d2d(lay["w_up"], "w_up")], bufs=_finish_exchange("gather_up_wait", fly_up, after=attn))
    taps = _finish_exchange("taps_wait", taps_flight, after=attn)
    conv_full, ffn_cw_full = taps["conv"], taps["ffn"]
    conv, a, cv, merged, h1, hn = _mix_fwd(xs, attn, c3, gates, conv_full, w_br, w_out_full, ffn_norm, tm2, comm=k3)
    w_up_full = k3.out["w_up"]
    w_down_full = _exchange("gather_down_d2d", [[_gather_d2d(lay["w_down"], "w_down")]],
                            bufs=_finish_exchange("gather_down_wait", fly_down, after=hn))["w_down"].reshape(D_FF, D_MODEL)
    u, up, act, dh2, loss_part, g_fn = _ffn_fwd_loss(hn, h1, w_up_full, ffn_cw_full, w_down_full,
                                                     final_norm[None, :], target, tm)

    grads, sums, slots = {}, {}, {}

    def pair(*ws):
        return _Carry([_rs_pair(lay[n], "g_" + n, "t_" + n) for n in ws], reads={"g_" + n: grads[n] for n in ws},
                      fresh={"t_" + n: _theirs_shape(lay[n], grads[n].dtype) for n in ws})

    def chips(*ws, also=None):
        k = _Carry([_rs_chips(lay[n], "s_" + n, "r_" + n) for n in ws], reads={"s_" + n: sums[n] for n in ws},
                   fresh={"r_" + n: _slots_shape(lay[n]) for n in ws})
        if also is not None:
            k = _Carry(k.jobs + also.jobs, {**k.reads, **also.reads}, None, {**k.fresh, **also.fresh})
        return k

    def pair_sums(k, *ws):
        for n in ws:
            sums[n] = _pair_sum(grads[n], k.out["t_" + n], lay[n], pos, "pair_sum_" + n)

    def take_slots(k, *ws):
        for n in ws:
            slots[n] = k.out["r_" + n]

    du, dh1, g_fcw, g_g2 = _ffn_bwd(dh2, u, up, h1, w_up_full, ffn_cw_full, w_down_full, ffn_norm, tm)
    grads["w_down"] = _wgrad(act, dh2, D_FF // 2, D_MODEL, bk2, "wgrad_down").reshape(lay["w_down"].whole())
    k4 = pair("w_down")
    grads["w_up"] = _wgrad(hn, du, D_MODEL, FF2 // 4, bk2, "wgrad_up", comm=k4)
    pair_sums(k4, "w_down")
    k5 = chips("w_down", also=pair("w_up"))
    dattn, dc3, dgt, g_cw, grads["w_br"], gw_out = _mix_bwd(
        dh1, gates, a, cv, c3, attn, conv, merged, conv_full, w_br, w_out_full, tm2, comm=k5)
    grads["w_out"] = gw_out.reshape(lay["w_out"].whole())
    take_slots(k5, "w_down")
    pair_sums(k5, "w_up")
    up_flight, started = _start_exchange("rs_chips_up_start", [_rs_chips(lay["w_up"], "s", "r")],
                                         {"s": sums["w_up"], "r": _slots_shape(lay["w_up"])})
    k6 = pair(*trio)
    k6.reads["after"] = started
    dq, dk_even, dk_odd, dv_even, dv_odd, g_sk = _attn_bwd(qkv, sk, attn, dattn, comm=k6)
    pair_sums(k6, *trio)
    trio_flight, started = _start_exchange(
        "rs_chips_trio_start", [_rs_chips(lay[n], "s_" + n, "r_" + n) for n in trio],
        {**{"s_" + n: sums[n] for n in trio}, **{"r_" + n: _slots_shape(lay[n]) for n in trio}})
    behind = mix_norm + jnp.tile(started[0:1], (1, D_MODEL // 128))
    grad_x, gw_in, g_b, g_g1 = _inproj_bwd(dq, (dk_even, dk_odd), (dv_even, dv_odd), dc3, dgt, w_in_full, xs, xn,
                                           dh1, behind)
    grads["w_in"] = gw_in.reshape(lay["w_in"].whole())

    in_flight, started = _start_exchange("rs_pair_in_start", [_rs_pair(lay["w_in"], "g", "t")],
                                         {"g": grads["w_in"], "t": _theirs_shape(lay["w_in"])})
    parts = [loss_part, g_g1, g_b, jnp.pad(g_sk[:, 0], (0, 120))[None, :], g_cw, g_g2, g_fcw, g_fn]
    packed, at = _pack_rows([p.reshape(-1, 128) for p in parts])
    small_flight, started = _start_exchange("small_start", [_to_all("v", "slots")],
                                            {"v": packed + started[0:1], "slots": jnp.zeros((N_DEV, *packed.shape), F32)})
    landed = _finish_exchange("rs_chips_up_wait", up_flight, after=started)
    halves = dict(zip(("w_down", "w_up"), _chip_sums(
        [(sums["w_down"], slots["w_down"], lay["w_down"]), (landed["s"], landed["r"], lay["w_up"])], pos,
        "chip_sum_w_down_up")))
    landed = _finish_exchange("rs_pair_in_wait", in_flight, after=halves["w_up"])
    sums["w_in"] = _pair_sum(landed["g"], landed["t"], lay["w_in"], pos, "pair_sum_w_in")
    (in_flight, down_flight, up_flight), started = _start_exchanges("rs_chips_in_start", [
        ([_rs_chips(lay["w_in"], "s", "r")], {"s": sums["w_in"], "r": _slots_shape(lay["w_in"])}),
        ([_rs_share(lay["w_down"], "w_down")], {"w_down": halves["w_down"]}),
        ([_rs_share(lay["w_up"], "w_up")], {"w_up": halves["w_up"]})])
    landed = _finish_exchange("rs_chips_trio_wait", trio_flight, after=started)
    halves.update(zip(trio, _chip_sums([(landed["s_" + n], landed["r_" + n], lay[n]) for n in trio], pos,
                                       "chip_sum_w_br_out")))
    shared = _exchange("share_halves", [[_rs_share(lay[n], n) for n in trio]], bufs={n: halves[n] for n in trio})
    shared["w_down"] = _finish_exchange("share_down_wait", down_flight, after=shared[trio[-1]])["w_down"]
    shared["w_up"] = _finish_exchange("share_up_wait", up_flight, after=shared["w_down"])["w_up"]

    def adam(n, g, after=None):
        return _rowwise(lambda w, g, m, v: (g, *_adamw(w, g, m, v)), [w_of[n], g, m_of[n], v_of[n]], [F32] * 4,
                        "adamw_" + n, after=after)

    new_of, last = {}, None
    for n in ("w_down", "w_up", "w_out"):
        new_of[n] = adam(n, shared[n], last)
        last = new_of[n][1]
    new_of["w_ab"], new_of["w_cb"] = _adamw_pair(ab, cb, shared["w_br"], after=last)
    last = new_of["w_cb"][1]

    arrived = _finish_exchange("small_wait", small_flight, after=last)
    flat = lambda k: lambda t, j, chip: at[k] + j
    mine = lambda k, per_tap: lambda t, j, chip: at[k] + per_tap * t + (per_tap // N_CHIPS) * chip + j
    rows = lambda a: a.reshape(a.shape[1], 1, a.shape[2])
    small_p = [
        (mix_norm, m_mix_norm, v_mix_norm, flat(1)), (b_in, m_b_in, v_b_in, flat(2)), (sinks, m_sinks, v_sinks, flat(3)),
        (rows(conv_w), rows(m_conv_w), rows(v_conv_w), mine(4, CONV_W // 128)),
        (ffn_norm, m_ffn_norm, v_ffn_norm, flat(5)),
        (rows(ffn_conv_w), rows(m_ffn_conv_w), rows(v_ffn_conv_w), mine(6, FF2 // 128)),
        (final_norm[None, :], m_final_norm[None, :], v_final_norm[None, :], flat(7))]
    small_new = _adamw_small(pos, arrived["v"], arrived["slots"], small_p)
    loss = small_new[0][0, 0]
    small_g = small_new[1::4]
    small_new = [small_new[4 * k + 2:4 * k + 5] for k in range(len(small_p))]

    landed = _finish_exchange("rs_chips_in_wait", in_flight, after=small_new[0][0])
    half_in = _chip_sums([(landed["s"], landed["r"], lay["w_in"])], pos, "chip_sum_w_in")[0]
    shared["w_in"] = _exchange("share_in", [[_rs_share(lay["w_in"], "w_in")]], bufs={"w_in": half_in})["w_in"]
    new_of["w_in"] = [a.T for a in adam("w_in", shared["w_in"])]
    big = ("w_in", "w_ab", "w_cb", "w_out", "w_up", "w_down")
    big_g = [new_of[n][0] for n in big]
    big_new = [new_of[n][1:] for n in big]

    order = [("s", 0), ("b", 0), ("s", 1), ("s", 2), ("s", 3), ("b", 1), ("b", 2), ("b", 3), ("s", 4), ("b", 4),
             ("s", 5), ("b", 5), ("s", 6)]
    shapes = [mix_norm.shape, w_in.shape, b_in.shape, sinks.shape, conv_w.shape, w_attn_branch.shape,
              w_conv_branch.shape, w_out.shape, ffn_norm.shape, w_up.shape, ffn_conv_w.shape, w_down.shape,
              final_norm.shape]
    out_g = [(small_g[k] if kind == "s" else big_g[k]).reshape(shp) for (kind, k), shp in zip(order, shapes)]
    news = [[(small_new[k][j] if kind == "s" else big_new[k][j]).reshape(shp) for (kind, k), shp in zip(order, shapes)]
            for j in range(3)]
    return (loss, grad_x[None], *out_g, *news[0], *news[1], *news[2])
```
